```python
import jax, jax.numpy as jnp
from jax import lax
import numpy as np

D_MODEL = 1024
BATCH = 8
SEQ = 2048
DEPTH = 1

DSW_GROUPS = ((128, 1), (512, 4), (2048, 16))
N_DSW_GROUPS = 3
DSW_HEADS_PER_GROUP = 4
DSW_HEAD_DIM = 64
DSW_BLOCK = 128
DSW_QKV_WIDTH = N_DSW_GROUPS * DSW_HEADS_PER_GROUP * DSW_HEAD_DIM
DSW_OUT_WIDTH = DSW_HEADS_PER_GROUP * DSW_HEAD_DIM
ROPE_THETA = 10000.0

GDN_HEADS = 8
GDN_HEAD_DIM = 128
GDN_WIDTH = GDN_HEADS * GDN_HEAD_DIM
GDN_CONV = 4
GDN_CHUNK = 64

D_FF = 2816
EPS = 1e-6

IN_WIDTHS = (DSW_QKV_WIDTH, DSW_QKV_WIDTH, DSW_QKV_WIDTH,
             3 * GDN_WIDTH,
             GDN_HEADS, GDN_HEADS,
             GDN_WIDTH,
             D_MODEL, D_MODEL)
D_IN = sum(IN_WIDTHS)

kernel_name = "hybrid_dilated_swa_gated_deltanet_macaron"


def rmsnorm(x, g):
    xf = x.astype(jnp.float32)
    y = xf * lax.rsqrt(jnp.mean(xf * xf, axis=-1, keepdims=True) + EPS)
    return (y * g.astype(jnp.float32)).astype(x.dtype)


def swiglu(x, w_gate, w_up, w_down):
    return (jax.nn.silu(x @ w_gate) * (x @ w_up)) @ w_down


def rope(x, pos):
    half = x.shape[-1] // 2
    inv_freq = ROPE_THETA ** (-jnp.arange(half, dtype=jnp.float32) / half)
    ang = pos.astype(jnp.float32)[:, None] * inv_freq[None, :]
    cos = jnp.cos(ang)[None, :, None, :]
    sin = jnp.sin(ang)[None, :, None, :]
    xf = x.astype(jnp.float32)
    x1, x2 = xf[..., :half], xf[..., half:]
    return jnp.concatenate([x1 * cos - x2 * sin, x2 * cos + x1 * sin], axis=-1).astype(x.dtype)


def dilated_window_attention(q, k, v, window, dilation):
    B, S, H, Dh = q.shape
    L = S // dilation
    span = window // dilation
    nb = -(-L // DSW_BLOCK)
    Lp = nb * DSW_BLOCK

    def to_blocks(t):
        t = t.reshape(B, L, dilation, H, Dh).transpose(0, 2, 1, 3, 4)
        t = jnp.pad(t, ((0, 0), (0, 0), (0, Lp - L), (0, 0), (0, 0)))
        return t.reshape(B, dilation, nb, DSW_BLOCK, H, Dh)

    def with_prev(t):
        prev = jnp.pad(t, ((0, 0), (0, 0), (1, 0), (0, 0), (0, 0), (0, 0)))[:, :, :-1]
        return jnp.concatenate([prev, t], axis=3)

    qb = to_blocks(q)
    kw = with_prev(to_blocks(k))
    vw = with_prev(to_blocks(v))
    s = jnp.einsum('brnqhd,brnkhd->brnhqk', qb, kw).astype(jnp.float32) * (Dh ** -0.5)
    qi = jnp.arange(nb)[:, None] * DSW_BLOCK + jnp.arange(DSW_BLOCK)[None, :]
    ki = (jnp.arange(nb)[:, None] - 1) * DSW_BLOCK + jnp.arange(2 * DSW_BLOCK)[None, :]
    dist = qi[:, :, None] - ki[:, None, :]
    valid = (dist >= 0) & (dist <= span) & (ki[:, None, :] >= 0)
    s = jnp.where(valid[None, None, :, None], s, -jnp.inf)
    m = jnp.max(s, axis=-1, keepdims=True)
    p = jnp.exp(s - m)
    l = jnp.sum(p, axis=-1, keepdims=True)
    o = jnp.einsum('brnhqk,brnkhd->brnqhd', (p / l).astype(v.dtype), vw)
    lse = (m + jnp.log(l))[..., 0]
    o = o.reshape(B, dilation, Lp, H, Dh)[:, :, :L].transpose(0, 2, 1, 3, 4).reshape(B, S, H, Dh)
    lse = lse.transpose(0, 1, 2, 4, 3).reshape(B, dilation, Lp, H)[:, :, :L]
    lse = lse.transpose(0, 2, 1, 3).reshape(B, S, H)
    return o, lse


def causal_depthwise_conv(x, w):
    K = w.shape[0]
    S = x.shape[1]
    xp = jnp.pad(x, ((0, 0), (K - 1, 0), (0, 0)))
    y = xp[:, 0:S] * w[0]
    for i in range(1, K):
        y = y + xp[:, i:i + S] * w[i]
    return y


def l2norm(x):
    return x * lax.rsqrt(jnp.sum(x * x, axis=-1, keepdims=True) + EPS)


def gated_delta_rule(q, k, v, beta, g):
    B, S, H, Dk = q.shape
    Dv = v.shape[-1]
    C = GDN_CHUNK
    n = S // C
    q = q * (Dk ** -0.5)
    qc, kc, vc = [t.transpose(0, 2, 1, 3).reshape(B, H, n, C, t.shape[-1]) for t in (q, k, v)]
    bc, gc = [t.transpose(0, 2, 1).reshape(B, H, n, C) for t in (beta, g)]
    gcum = jnp.cumsum(gc, axis=-1)
    tri = jnp.tril(jnp.ones((C, C), dtype=bool))
    tri_strict = jnp.tril(jnp.ones((C, C), dtype=bool), -1)
    decay = jnp.exp(jnp.where(tri, gcum[..., :, None] - gcum[..., None, :], -jnp.inf))
    kbeta = kc * bc[..., None]
    vbeta = vc * bc[..., None]
    M = jnp.where(tri_strict, jnp.einsum('bhnid,bhnjd->bhnij', kbeta, kc) * decay, 0.0)
    A = M + jnp.eye(C, dtype=M.dtype)
    rhs = jnp.concatenate([vbeta, kbeta * jnp.exp(gcum)[..., None]], axis=-1)
    sol = lax.linalg.triangular_solve(A, rhs, left_side=True, lower=True, unit_diagonal=True)
    u, w = sol[..., :Dv], sol[..., Dv:]
    a_qk = jnp.einsum('bhnid,bhnjd->bhnij', qc, kc) * decay
    q_dec = qc * jnp.exp(gcum)[..., None]
    g_last = gcum[..., -1]
    k_dec = kc * jnp.exp(g_last[..., None] - gcum)[..., None]
    state_dec = jnp.exp(g_last)

    def step(state, inp):
        q_i, k_i, u_i, w_i, a_i, sd_i = inp
        v_new = u_i - jnp.einsum('bhck,bhkv->bhcv', w_i, state)
        o_i = jnp.einsum('bhck,bhkv->bhcv', q_i, state) + jnp.einsum('bhij,bhjv->bhiv', a_i, v_new)
        state = state * sd_i[..., None, None] + jnp.einsum('bhck,bhcv->bhkv', k_i, v_new)
        return state, o_i

    xs = tuple(jnp.moveaxis(t, 2, 0) for t in (q_dec, k_dec, u, w, a_qk, state_dec))
    state0 = jnp.zeros((B, H, Dk, Dv), dtype=jnp.float32)
    _, o = lax.scan(step, state0, xs)
    return jnp.moveaxis(o, 0, 2).reshape(B, H, S, Dv).transpose(0, 2, 1, 3)


def hybrid_mixer(h, pos, w_in, gdn_conv_w, gdn_a_log, gdn_dt_bias, gdn_out_norm,
                 w_branch_a, w_branch_b, w_out):
    B, S, _ = h.shape
    offsets = []
    acc = 0
    for wdt in IN_WIDTHS[:-1]:
        acc += wdt
        offsets.append(acc)
    qa, ka, va, qkv_b, beta_raw, decay_raw, gdn_gate, gate_a, gate_b = jnp.split(h @ w_in, offsets, axis=-1)

    n_a = N_DSW_GROUPS * DSW_HEADS_PER_GROUP
    qa = rope(qa.reshape(B, S, n_a, DSW_HEAD_DIM), pos)
    ka = rope(ka.reshape(B, S, n_a, DSW_HEAD_DIM), pos)
    va = va.reshape(B, S, n_a, DSW_HEAD_DIM)
    outs, lses = [], []
    for gi, (window, dilation) in enumerate(DSW_GROUPS):
        sl = slice(gi * DSW_HEADS_PER_GROUP, (gi + 1) * DSW_HEADS_PER_GROUP)
        o, lse = dilated_window_attention(qa[:, :, sl], ka[:, :, sl], va[:, :, sl], window, dilation)
        outs.append(o)
        lses.append(lse)
    wts = jax.nn.softmax(jnp.stack(lses, axis=0), axis=0)
    ya = jnp.einsum('gbsh,gbshd->bshd', wts.astype(h.dtype), jnp.stack(outs, axis=0))
    ya = ya.reshape(B, S, DSW_OUT_WIDTH)

    qkv = jax.nn.silu(causal_depthwise_conv(qkv_b, gdn_conv_w)).astype(jnp.float32)
    qb, kb, vb = jnp.split(qkv, 3, axis=-1)
    qb = l2norm(qb.reshape(B, S, GDN_HEADS, GDN_HEAD_DIM))
    kb = l2norm(kb.reshape(B, S, GDN_HEADS, GDN_HEAD_DIM))
    vb = vb.reshape(B, S, GDN_HEADS, GDN_HEAD_DIM)
    beta = jax.nn.sigmoid(beta_raw.astype(jnp.float32))
    g = -jnp.exp(gdn_a_log.astype(jnp.float32)) * jax.nn.softplus(
        decay_raw.astype(jnp.float32) + gdn_dt_bias.astype(jnp.float32))
    ob = gated_delta_rule(qb, kb, vb, beta, g)
    ob = rmsnorm(ob, gdn_out_norm) * jax.nn.silu(
        gdn_gate.astype(jnp.float32).reshape(B, S, GDN_HEADS, GDN_HEAD_DIM))
    yb = ob.reshape(B, S, GDN_WIDTH).astype(h.dtype)

    merged = jax.nn.sigmoid(gate_a) * (ya @ w_branch_a) + jax.nn.sigmoid(gate_b) * (yb @ w_branch_b)
    return merged @ w_out


def _fwd_setup_inputs(seed: int = 0) -> dict:
    key = jax.random.key(seed)
    ks = jax.random.split(key, 20)
    f32 = jnp.float32

    def nrm(k, shape, fan_in):
        return jax.random.normal(k, shape, f32) * (fan_in ** -0.5)

    def gain(k, shape):
        return 1.0 + 0.01 * jax.random.normal(k, shape, f32)

    dt = jnp.exp(jax.random.uniform(ks[9], (DEPTH, GDN_HEADS), f32, np.log(1e-3), np.log(1e-1)))
    return {
        "x": jax.random.normal(ks[0], (BATCH, SEQ, D_MODEL), f32),
        "ffn1_norm": gain(ks[1], (DEPTH, D_MODEL)),
        "ffn1_w_gate": nrm(ks[2], (DEPTH, D_MODEL, D_FF), D_MODEL),
        "ffn1_w_up": nrm(ks[3], (DEPTH, D_MODEL, D_FF), D_MODEL),
        "ffn1_w_down": nrm(ks[4], (DEPTH, D_FF, D_MODEL), D_FF),
        "mix_norm": gain(ks[5], (DEPTH, D_MODEL)),
        "w_in": nrm(ks[6], (DEPTH, D_MODEL, D_IN), D_MODEL),
        "gdn_conv_w": nrm(ks[7], (DEPTH, GDN_CONV, 3 * GDN_WIDTH), GDN_CONV),
        "gdn_a_log": jnp.log(jax.random.uniform(ks[8], (DEPTH, GDN_HEADS), f32, 1.0, 16.0)),
        "gdn_dt_bias": dt + jnp.log(-jnp.expm1(-dt)),
        "gdn_out_norm": gain(ks[10], (DEPTH, GDN_HEAD_DIM)),
        "w_branch_a": nrm(ks[11], (DEPTH, DSW_OUT_WIDTH, D_MODEL), DSW_OUT_WIDTH),
        "w_branch_b": nrm(ks[12], (DEPTH, GDN_WIDTH, D_MODEL), GDN_WIDTH),
        "w_out": nrm(ks[13], (DEPTH, D_MODEL, D_MODEL), D_MODEL),
        "ffn2_norm": gain(ks[14], (DEPTH, D_MODEL)),
        "ffn2_w_gate": nrm(ks[15], (DEPTH, D_MODEL, D_FF), D_MODEL),
        "ffn2_w_up": nrm(ks[16], (DEPTH, D_MODEL, D_FF), D_MODEL),
        "ffn2_w_down": nrm(ks[17], (DEPTH, D_FF, D_MODEL), D_FF),
        "final_norm": gain(ks[18], (D_MODEL,)),
    }


def _fwd_reference(x, ffn1_norm, ffn1_w_gate, ffn1_w_up, ffn1_w_down, mix_norm, w_in, gdn_conv_w,
              gdn_a_log, gdn_dt_bias, gdn_out_norm, w_branch_a, w_branch_b, w_out,
              ffn2_norm, ffn2_w_gate, ffn2_w_up, ffn2_w_down, final_norm):
    pos = jnp.arange(x.shape[1])
    for layer in range(DEPTH):
        x = x + 0.5 * swiglu(rmsnorm(x, ffn1_norm[layer]),
                             ffn1_w_gate[layer], ffn1_w_up[layer], ffn1_w_down[layer])
        h = rmsnorm(x, mix_norm[layer])
        x = x + hybrid_mixer(h, pos, w_in[layer], gdn_conv_w[layer], gdn_a_log[layer],
                             gdn_dt_bias[layer], gdn_out_norm[layer], w_branch_a[layer],
                             w_branch_b[layer], w_out[layer])
        x = x + 0.5 * swiglu(rmsnorm(x, ffn2_norm[layer]),
                             ffn2_w_gate[layer], ffn2_w_up[layer], ffn2_w_down[layer])
    return rmsnorm(x, final_norm)


import jax as _jax
import jax.numpy as _jnp

TWIN_FORMAT = 'train_step'
FWD_PARAMS = ['x', 'ffn1_norm', 'ffn1_w_gate', 'ffn1_w_up', 'ffn1_w_down', 'mix_norm', 'w_in', 'gdn_conv_w', 'gdn_a_log', 'gdn_dt_bias', 'gdn_out_norm', 'w_branch_a', 'w_branch_b', 'w_out', 'ffn2_norm', 'ffn2_w_gate', 'ffn2_w_up', 'ffn2_w_down', 'final_norm']
TWIN_WEIGHTS = ['ffn1_norm', 'ffn1_w_gate', 'ffn1_w_up', 'ffn1_w_down', 'mix_norm', 'w_in', 'gdn_conv_w', 'gdn_a_log', 'gdn_dt_bias', 'gdn_out_norm', 'w_branch_a', 'w_branch_b', 'w_out', 'ffn2_norm', 'ffn2_w_gate', 'ffn2_w_up', 'ffn2_w_down', 'final_norm']
TWIN_DIFF_INPUT = 'x'
TWIN_INPUTS = ['x', 'ffn1_norm', 'ffn1_w_gate', 'ffn1_w_up', 'ffn1_w_down', 'mix_norm', 'w_in', 'gdn_conv_w', 'gdn_a_log', 'gdn_dt_bias', 'gdn_out_norm', 'w_branch_a', 'w_branch_b', 'w_out', 'ffn2_norm', 'ffn2_w_gate', 'ffn2_w_up', 'ffn2_w_down', 'final_norm', 'loss_target', 'm_ffn1_norm', 'm_ffn1_w_gate', 'm_ffn1_w_up', 'm_ffn1_w_down', 'm_mix_norm', 'm_w_in', 'm_gdn_conv_w', 'm_gdn_a_log', 'm_gdn_dt_bias', 'm_gdn_out_norm', 'm_w_branch_a', 'm_w_branch_b', 'm_w_out', 'm_ffn2_norm', 'm_ffn2_w_gate', 'm_ffn2_w_up', 'm_ffn2_w_down', 'm_final_norm', 'v_ffn1_norm', 'v_ffn1_w_gate', 'v_ffn1_w_up', 'v_ffn1_w_down', 'v_mix_norm', 'v_w_in', 'v_gdn_conv_w', 'v_gdn_a_log', 'v_gdn_dt_bias', 'v_gdn_out_norm', 'v_w_branch_a', 'v_w_branch_b', 'v_w_out', 'v_ffn2_norm', 'v_ffn2_w_gate', 'v_ffn2_w_up', 'v_ffn2_w_down', 'v_final_norm']
TWIN_OUTPUTS = ['loss', 'grad_x', 'grad_ffn1_norm', 'grad_ffn1_w_gate', 'grad_ffn1_w_up', 'grad_ffn1_w_down', 'grad_mix_norm', 'grad_w_in', 'grad_gdn_conv_w', 'grad_gdn_a_log', 'grad_gdn_dt_bias', 'grad_gdn_out_norm', 'grad_w_branch_a', 'grad_w_branch_b', 'grad_w_out', 'grad_ffn2_norm', 'grad_ffn2_w_gate', 'grad_ffn2_w_up', 'grad_ffn2_w_down', 'grad_final_norm', 'delta_ffn1_norm', 'delta_ffn1_w_gate', 'delta_ffn1_w_up', 'delta_ffn1_w_down', 'delta_mix_norm', 'delta_w_in', 'delta_gdn_conv_w', 'delta_gdn_a_log', 'delta_gdn_dt_bias', 'delta_gdn_out_norm', 'delta_w_branch_a', 'delta_w_branch_b', 'delta_w_out', 'delta_ffn2_norm', 'delta_ffn2_w_gate', 'delta_ffn2_w_up', 'delta_ffn2_w_down', 'delta_final_norm', 'new_m_ffn1_norm', 'new_m_ffn1_w_gate', 'new_m_ffn1_w_up', 'new_m_ffn1_w_down', 'new_m_mix_norm', 'new_m_w_in', 'new_m_gdn_conv_w', 'new_m_gdn_a_log', 'new_m_gdn_dt_bias', 'new_m_gdn_out_norm', 'new_m_w_branch_a', 'new_m_w_branch_b', 'new_m_w_out', 'new_m_ffn2_norm', 'new_m_ffn2_w_gate', 'new_m_ffn2_w_up', 'new_m_ffn2_w_down', 'new_m_final_norm', 'new_v_ffn1_norm', 'new_v_ffn1_w_gate', 'new_v_ffn1_w_up', 'new_v_ffn1_w_down', 'new_v_mix_norm', 'new_v_w_in', 'new_v_gdn_conv_w', 'new_v_gdn_a_log', 'new_v_gdn_dt_bias', 'new_v_gdn_out_norm', 'new_v_w_branch_a', 'new_v_w_branch_b', 'new_v_w_out', 'new_v_ffn2_norm', 'new_v_ffn2_w_gate', 'new_v_ffn2_w_up', 'new_v_ffn2_w_down', 'new_v_final_norm']
TWIN_LEAF_KINDS = {'loss': 'loss', 'grad_x': 'grad_x', 'grad_ffn1_norm': 'grad_w', 'grad_ffn1_w_gate': 'grad_w', 'grad_ffn1_w_up': 'grad_w', 'grad_ffn1_w_down': 'grad_w', 'grad_mix_norm': 'grad_w', 'grad_w_in': 'grad_w', 'grad_gdn_conv_w': 'grad_w', 'grad_gdn_a_log': 'grad_w', 'grad_gdn_dt_bias': 'grad_w', 'grad_gdn_out_norm': 'grad_w', 'grad_w_branch_a': 'grad_w', 'grad_w_branch_b': 'grad_w', 'grad_w_out': 'grad_w', 'grad_ffn2_norm': 'grad_w', 'grad_ffn2_w_gate': 'grad_w', 'grad_ffn2_w_up': 'grad_w', 'grad_ffn2_w_down': 'grad_w', 'grad_final_norm': 'grad_w', 'delta_ffn1_norm': 'delta_w', 'delta_ffn1_w_gate': 'delta_w', 'delta_ffn1_w_up': 'delta_w', 'delta_ffn1_w_down': 'delta_w', 'delta_mix_norm': 'delta_w', 'delta_w_in': 'delta_w', 'delta_gdn_conv_w': 'delta_w', 'delta_gdn_a_log': 'delta_w', 'delta_gdn_dt_bias': 'delta_w', 'delta_gdn_out_norm': 'delta_w', 'delta_w_branch_a': 'delta_w', 'delta_w_branch_b': 'delta_w', 'delta_w_out': 'delta_w', 'delta_ffn2_norm': 'delta_w', 'delta_ffn2_w_gate': 'delta_w', 'delta_ffn2_w_up': 'delta_w', 'delta_ffn2_w_down': 'delta_w', 'delta_final_norm': 'delta_w', 'new_m_ffn1_norm': 'new_m', 'new_m_ffn1_w_gate': 'new_m', 'new_m_ffn1_w_up': 'new_m', 'new_m_ffn1_w_down': 'new_m', 'new_m_mix_norm': 'new_m', 'new_m_w_in': 'new_m', 'new_m_gdn_conv_w': 'new_m', 'new_m_gdn_a_log': 'new_m', 'new_m_gdn_dt_bias': 'new_m', 'new_m_gdn_out_norm': 'new_m', 'new_m_w_branch_a': 'new_m', 'new_m_w_branch_b': 'new_m', 'new_m_w_out': 'new_m', 'new_m_ffn2_norm': 'new_m', 'new_m_ffn2_w_gate': 'new_m', 'new_m_ffn2_w_up': 'new_m', 'new_m_ffn2_w_down': 'new_m', 'new_m_final_norm': 'new_m', 'new_v_ffn1_norm': 'new_v', 'new_v_ffn1_w_gate': 'new_v', 'new_v_ffn1_w_up': 'new_v', 'new_v_ffn1_w_down': 'new_v', 'new_v_mix_norm': 'new_v', 'new_v_w_in': 'new_v', 'new_v_gdn_conv_w': 'new_v', 'new_v_gdn_a_log': 'new_v', 'new_v_gdn_dt_bias': 'new_v', 'new_v_gdn_out_norm': 'new_v', 'new_v_w_branch_a': 'new_v', 'new_v_w_branch_b': 'new_v', 'new_v_w_out': 'new_v', 'new_v_ffn2_norm': 'new_v', 'new_v_ffn2_w_gate': 'new_v', 'new_v_ffn2_w_up': 'new_v', 'new_v_ffn2_w_down': 'new_v', 'new_v_final_norm': 'new_v'}


def _forward(args):
    return _fwd_reference(*[args[k] for k in FWD_PARAMS])


def _output_shape():
    out = _jax.eval_shape(lambda: _forward(_fwd_setup_inputs(0)))
    return out.shape, out.dtype

N_MICROBATCH = 1
ADAM_LR = 0.001
ADAM_B1 = 0.9
ADAM_B2 = 0.999
ADAM_EPS = 1e-08
ADAM_WD = 0.01
ADAM_STEP = 10
PER_EXAMPLE_BATCH_AXIS = {'x': 0, 'loss_target': 0}
SHARED_INPUTS = []
_WEIGHT_DTYPES = {'ffn1_norm': _jnp.float32, 'ffn1_w_gate': _jnp.float32, 'ffn1_w_up': _jnp.float32, 'ffn1_w_down': _jnp.float32, 'mix_norm': _jnp.float32, 'w_in': _jnp.float32, 'gdn_conv_w': _jnp.float32, 'gdn_a_log': _jnp.float32, 'gdn_dt_bias': _jnp.float32, 'gdn_out_norm': _jnp.float32, 'w_branch_a': _jnp.float32, 'w_branch_b': _jnp.float32, 'w_out': _jnp.float32, 'ffn2_norm': _jnp.float32, 'ffn2_w_gate': _jnp.float32, 'ffn2_w_up': _jnp.float32, 'ffn2_w_down': _jnp.float32, 'final_norm': _jnp.float32}
MOMENT_SCALE = {'ffn1_norm': 6.051080e-02, 'ffn1_w_gate': 2.509210e-02, 'ffn1_w_up': 2.426550e-02, 'ffn1_w_down': 4.019894e-02, 'mix_norm': 6.868324e-02, 'w_in': 2.341340e-02, 'gdn_conv_w': 3.032350e-02, 'gdn_a_log': 1.063049e-01, 'gdn_dt_bias': 1.040019e-01, 'gdn_out_norm': 1.145321e-01, 'w_branch_a': 1.198200e-02, 'w_branch_b': 3.774388e-02, 'w_out': 3.908952e-02, 'ffn2_norm': 4.781920e-02, 'ffn2_w_gate': 2.052319e-02, 'ffn2_w_up': 1.991141e-02, 'ffn2_w_down': 3.298912e-02, 'final_norm': 1.599818e+01}


def _to_microbatches(a, axis):
    t = _jnp.moveaxis(a, axis, 0)
    t = t.reshape((N_MICROBATCH, t.shape[0] // N_MICROBATCH) + t.shape[1:])
    return _jnp.moveaxis(t, 1, axis + 1)


def setup_inputs(seed: int = 0) -> dict:
    inp = _fwd_setup_inputs(seed)
    key = _jax.random.fold_in(_jax.random.key(seed), 7919)
    shape, _ = _output_shape()
    out = dict(inp)
    out["loss_target"] = _jax.random.normal(_jax.random.fold_in(key, 0), shape, _jnp.float32)
    for i, name in enumerate(TWIN_WEIGHTS):
        w = inp[name].astype(_jnp.float32)
        if MOMENT_SCALE is None:
            s = _jnp.sqrt(_jnp.mean(_jnp.square(w)) + 1e-30)
        else:
            s = MOMENT_SCALE[name]
        km, kv = _jax.random.split(_jax.random.fold_in(key, i + 1))
        out[name] = w
        out["m_" + name] = s * _jax.random.normal(km, w.shape, _jnp.float32)
        out["v_" + name] = (s * s) * _jax.random.uniform(kv, w.shape, _jnp.float32, 0.5, 1.5)
    if N_MICROBATCH > 1:
        for name, axis in PER_EXAMPLE_BATCH_AXIS.items():
            out[name] = _to_microbatches(out[name], axis)
    return {'x': out['x'], 'ffn1_norm': out['ffn1_norm'], 'ffn1_w_gate': out['ffn1_w_gate'], 'ffn1_w_up': out['ffn1_w_up'], 'ffn1_w_down': out['ffn1_w_down'], 'mix_norm': out['mix_norm'], 'w_in': out['w_in'], 'gdn_conv_w': out['gdn_conv_w'], 'gdn_a_log': out['gdn_a_log'], 'gdn_dt_bias': out['gdn_dt_bias'], 'gdn_out_norm': out['gdn_out_norm'], 'w_branch_a': out['w_branch_a'], 'w_branch_b': out['w_branch_b'], 'w_out': out['w_out'], 'ffn2_norm': out['ffn2_norm'], 'ffn2_w_gate': out['ffn2_w_gate'], 'ffn2_w_up': out['ffn2_w_up'], 'ffn2_w_down': out['ffn2_w_down'], 'final_norm': out['final_norm'], 'loss_target': out['loss_target'], 'm_ffn1_norm': out['m_ffn1_norm'], 'm_ffn1_w_gate': out['m_ffn1_w_gate'], 'm_ffn1_w_up': out['m_ffn1_w_up'], 'm_ffn1_w_down': out['m_ffn1_w_down'], 'm_mix_norm': out['m_mix_norm'], 'm_w_in': out['m_w_in'], 'm_gdn_conv_w': out['m_gdn_conv_w'], 'm_gdn_a_log': out['m_gdn_a_log'], 'm_gdn_dt_bias': out['m_gdn_dt_bias'], 'm_gdn_out_norm': out['m_gdn_out_norm'], 'm_w_branch_a': out['m_w_branch_a'], 'm_w_branch_b': out['m_w_branch_b'], 'm_w_out': out['m_w_out'], 'm_ffn2_norm': out['m_ffn2_norm'], 'm_ffn2_w_gate': out['m_ffn2_w_gate'], 'm_ffn2_w_up': out['m_ffn2_w_up'], 'm_ffn2_w_down': out['m_ffn2_w_down'], 'm_final_norm': out['m_final_norm'], 'v_ffn1_norm': out['v_ffn1_norm'], 'v_ffn1_w_gate': out['v_ffn1_w_gate'], 'v_ffn1_w_up': out['v_ffn1_w_up'], 'v_ffn1_w_down': out['v_ffn1_w_down'], 'v_mix_norm': out['v_mix_norm'], 'v_w_in': out['v_w_in'], 'v_gdn_conv_w': out['v_gdn_conv_w'], 'v_gdn_a_log': out['v_gdn_a_log'], 'v_gdn_dt_bias': out['v_gdn_dt_bias'], 'v_gdn_out_norm': out['v_gdn_out_norm'], 'v_w_branch_a': out['v_w_branch_a'], 'v_w_branch_b': out['v_w_branch_b'], 'v_w_out': out['v_w_out'], 'v_ffn2_norm': out['v_ffn2_norm'], 'v_ffn2_w_gate': out['v_ffn2_w_gate'], 'v_ffn2_w_up': out['v_ffn2_w_up'], 'v_ffn2_w_down': out['v_ffn2_w_down'], 'v_final_norm': out['v_final_norm']}


def _loss(weights, diff, rest, loss_target):
    with _jax.named_scope("forward"):
        args = {**rest, TWIN_DIFF_INPUT: diff, **{k: w.astype(_WEIGHT_DTYPES[k]) for k, w in weights.items()}}
        y = _forward(args)
    with _jax.named_scope("loss_head"):
        err = _jnp.square(y.astype(_jnp.float32) - loss_target)
        return 0.5 * _jnp.sum(_jnp.mean(err, axis=-1)) if err.ndim else 0.5 * err


def _adamw(w, g, m, v):
    m = ADAM_B1 * m + (1.0 - ADAM_B1) * g
    v = ADAM_B2 * v + (1.0 - ADAM_B2) * _jnp.square(g)
    m_hat = m / (1.0 - ADAM_B1 ** ADAM_STEP)
    v_hat = v / (1.0 - ADAM_B2 ** ADAM_STEP)
    delta = -ADAM_LR * (m_hat / (_jnp.sqrt(v_hat) + ADAM_EPS) + ADAM_WD * w)
    return delta, m, v


def reference(x, ffn1_norm, ffn1_w_gate, ffn1_w_up, ffn1_w_down, mix_norm, w_in, gdn_conv_w, gdn_a_log, gdn_dt_bias, gdn_out_norm, w_branch_a, w_branch_b, w_out, ffn2_norm, ffn2_w_gate, ffn2_w_up, ffn2_w_down, final_norm, loss_target, m_ffn1_norm, m_ffn1_w_gate, m_ffn1_w_up, m_ffn1_w_down, m_mix_norm, m_w_in, m_gdn_conv_w, m_gdn_a_log, m_gdn_dt_bias, m_gdn_out_norm, m_w_branch_a, m_w_branch_b, m_w_out, m_ffn2_norm, m_ffn2_w_gate, m_ffn2_w_up, m_ffn2_w_down, m_final_norm, v_ffn1_norm, v_ffn1_w_gate, v_ffn1_w_up, v_ffn1_w_down, v_mix_norm, v_w_in, v_gdn_conv_w, v_gdn_a_log, v_gdn_dt_bias, v_gdn_out_norm, v_w_branch_a, v_w_branch_b, v_w_out, v_ffn2_norm, v_ffn2_w_gate, v_ffn2_w_up, v_ffn2_w_down, v_final_norm):
    given = dict(x=x, ffn1_norm=ffn1_norm, ffn1_w_gate=ffn1_w_gate, ffn1_w_up=ffn1_w_up, ffn1_w_down=ffn1_w_down, mix_norm=mix_norm, w_in=w_in, gdn_conv_w=gdn_conv_w, gdn_a_log=gdn_a_log, gdn_dt_bias=gdn_dt_bias, gdn_out_norm=gdn_out_norm, w_branch_a=w_branch_a, w_branch_b=w_branch_b, w_out=w_out, ffn2_norm=ffn2_norm, ffn2_w_gate=ffn2_w_gate, ffn2_w_up=ffn2_w_up, ffn2_w_down=ffn2_w_down, final_norm=final_norm, loss_target=loss_target, m_ffn1_norm=m_ffn1_norm, m_ffn1_w_gate=m_ffn1_w_gate, m_ffn1_w_up=m_ffn1_w_up, m_ffn1_w_down=m_ffn1_w_down, m_mix_norm=m_mix_norm, m_w_in=m_w_in, m_gdn_conv_w=m_gdn_conv_w, m_gdn_a_log=m_gdn_a_log, m_gdn_dt_bias=m_gdn_dt_bias, m_gdn_out_norm=m_gdn_out_norm, m_w_branch_a=m_w_branch_a, m_w_branch_b=m_w_branch_b, m_w_out=m_w_out, m_ffn2_norm=m_ffn2_norm, m_ffn2_w_gate=m_ffn2_w_gate, m_ffn2_w_up=m_ffn2_w_up, m_ffn2_w_down=m_ffn2_w_down, m_final_norm=m_final_norm, v_ffn1_norm=v_ffn1_norm, v_ffn1_w_gate=v_ffn1_w_gate, v_ffn1_w_up=v_ffn1_w_up, v_ffn1_w_down=v_ffn1_w_down, v_mix_norm=v_mix_norm, v_w_in=v_w_in, v_gdn_conv_w=v_gdn_conv_w, v_gdn_a_log=v_gdn_a_log, v_gdn_dt_bias=v_gdn_dt_bias, v_gdn_out_norm=v_gdn_out_norm, v_w_branch_a=v_w_branch_a, v_w_branch_b=v_w_branch_b, v_w_out=v_w_out, v_ffn2_norm=v_ffn2_norm, v_ffn2_w_gate=v_ffn2_w_gate, v_ffn2_w_up=v_ffn2_w_up, v_ffn2_w_down=v_ffn2_w_down, v_final_norm=v_final_norm)
    weights = {n: given[n] for n in TWIN_WEIGHTS}
    shared = {n: given[n] for n in SHARED_INPUTS}
    per_example = {n: given[n] for n in ['x']}
    grad_fn = _jax.value_and_grad(_loss, argnums=(0, 1))

    def one_microbatch(ex, loss_target):
        ex = dict(ex)
        diff = ex.pop(TWIN_DIFF_INPUT)
        return grad_fn(weights, diff, {**shared, **ex}, loss_target)

    if N_MICROBATCH == 1:
        loss, (grad_w, grad_x) = one_microbatch(per_example, given["loss_target"])
    else:
        def body(carry, xs):
            loss_sum, grad_sum = carry
            l_k, (gw_k, gx_k) = one_microbatch(xs[0], xs[1])
            with _jax.named_scope("update"):
                return (loss_sum + l_k, _jax.tree.map(_jnp.add, grad_sum, gw_k)), gx_k

        init = (_jnp.zeros((), _jnp.float32), _jax.tree.map(_jnp.zeros_like, weights))
        (loss, grad_w), grad_x = _jax.lax.scan(body, init, (per_example, given["loss_target"]))
    with _jax.named_scope("update"):
        delta_w, new_m, new_v = {}, {}, {}
        for n in TWIN_WEIGHTS:
            delta_w[n], new_m[n], new_v[n] = _adamw(weights[n], grad_w[n], given["m_" + n], given["v_" + n])
    return (loss, grad_x, *[grad_w[n] for n in TWIN_WEIGHTS], *[delta_w[n] for n in TWIN_WEIGHTS],
            *[new_m[n] for n in TWIN_WEIGHTS], *[new_v[n] for n in TWIN_WEIGHTS])
```

```python
import functools

import jax
import jax.numpy as jnp
import numpy as np
from jax import lax
from jax.experimental import pallas as pl
from jax.experimental.pallas import tpu as pltpu

F32 = jnp.float32
BF16 = jnp.bfloat16
HI = lax.Precision.HIGHEST
MESH = pl.DeviceIdType.MESH

N_DEV = 8
D_MODEL = 1024
D_FF = 2816
EPS = 1e-6
ROPE_THETA = 10000.0
DSW_DILATIONS = (1, 4, 16)
DSW_HEADS_PER_GROUP = 4
DSW_HEAD_DIM = 64
DSW_BLOCK = 128
DSW_WIDTH = 768
N_DSW_HEADS = 12
GDN_HEADS = 8
GDN_HEAD_DIM = 128
GDN_WIDTH = 1024
GDN_CONV = 4
GDN_CHUNK = 64
IN_OFFSETS = dict(qa=0, ka=768, va=1536, qkvb=2304, small=5376, ggate=5392, gatea=6416, gateb=7440)
D_IN = 8464

ADAM_LR = 0.001
ADAM_B1 = 0.9
ADAM_B2 = 0.999
ADAM_EPS = 1e-08
ADAM_WD = 0.01
ADAM_STEP = 10

VMEM_LIMIT_BYTES = 56 * 1024 * 1024
LANES = 128
PACK_COLS = 1024

NN = (((1,), (0,)), ((), ()))
NT = (((1,), (1,)), ((), ()))
TN = (((0,), (0,)), ((), ()))


def _params(n_grid):
    return pltpu.CompilerParams(dimension_semantics=("arbitrary",) * n_grid, vmem_limit_bytes=VMEM_LIMIT_BYTES)


def _tile(n, pref):
    best = None
    t = LANES
    while t <= min(n, pref):
        if n % t == 0:
            best = t
        t += LANES
    return n if best is None else best


def _matmul(a, b, *, name, ta=False, tb=False, res=None, scale=1.0):
    K, M = a.shape if ta else a.shape[::-1]
    N = b.shape[0] if tb else b.shape[1]
    assert (b.shape[1] if tb else b.shape[0]) == K, (a.shape, b.shape, ta, tb)
    tm = _tile(M, 512)
    tn = _tile(N, 512)
    dn = (((0 if ta else 1,), (1 if tb else 0,)), ((), ()))

    def body(*refs):
        a_ref, b_ref = refs[:2]
        o_ref = refs[-1]
        acc = lax.dot_general(a_ref[...].astype(BF16), b_ref[...].astype(BF16), dn, preferred_element_type=F32)
        if scale != 1.0:
            acc = acc * scale
        if res is not None:
            acc = refs[2][...] + acc
        o_ref[...] = acc

    a_spec = pl.BlockSpec((K, tm), lambda i, j: (0, i)) if ta else pl.BlockSpec((tm, K), lambda i, j: (i, 0))
    b_spec = pl.BlockSpec((tn, K), lambda i, j: (j, 0)) if tb else pl.BlockSpec((K, tn), lambda i, j: (0, j))
    o_spec = pl.BlockSpec((tm, tn), lambda i, j: (i, j))
    ins, specs = [a, b], [a_spec, b_spec]
    if res is not None:
        ins.append(res)
        specs.append(o_spec)
    return pl.pallas_call(
        body, grid=(M // tm, N // tn), in_specs=specs, out_specs=o_spec,
        out_shape=jax.ShapeDtypeStruct((M, N), F32), name=name, compiler_params=_params(2),
    )(*ins)


def _make_mm(name, scale=1.0, with_res=False):
    @jax.custom_vjp
    def op(a, w, slot, res):
        return _matmul(a, w, name=name, res=res if with_res else None, scale=scale)

    def fwd(a, w, slot, res):
        return op(a, w, slot, res), (a, w)

    def bwd(saved, g):
        a, w = saved
        da = _matmul(g, w, name=name + "_da", tb=True, scale=scale)
        dw = _matmul(a, g, name=name + "_dw", ta=True, scale=scale)
        return da, None, dw, (g if with_res else None)

    op.defvjp(fwd, bwd)
    return op


def mm(a, w, slot, name):
    return _make_mm(name)(a, w, slot, None)


def mm_res(a, w, slot, res, scale, name):
    return _make_mm(name, scale=scale, with_res=True)(a, w, slot, res)


def _rw_specs(arrs, tm, nblk):
    return [pl.BlockSpec((tm, a.shape[1] // nblk), lambda i, j: (i, j)) for a in arrs]


def _rowwise_fwd(fn, name, rows, consts, params, tm, nblk):
    n_rows = rows[0].shape[0]
    tm = min(tm, n_rows)
    ins = list(rows) + list(consts)
    avals = [jax.ShapeDtypeStruct((tm, a.shape[1] // nblk), a.dtype) for a in ins]
    avals += [jax.ShapeDtypeStruct(p.shape, p.dtype) for p in params]
    out_avals = jax.eval_shape(fn, *avals)
    n_in = len(ins) + len(params)

    def body(*refs):
        outs = fn(*[r[...] for r in refs[:n_in]])
        for r, o in zip(refs[n_in:], outs):
            r[...] = o.astype(r.dtype)

    return pl.pallas_call(
        body, grid=(n_rows // tm, nblk),
        in_specs=_rw_specs(ins, tm, nblk) + [pl.BlockSpec(p.shape, lambda i, j: (0, 0)) for p in params],
        out_specs=tuple(pl.BlockSpec((tm, o.shape[1]), lambda i, j: (i, j)) for o in out_avals),
        out_shape=tuple(jax.ShapeDtypeStruct((n_rows, o.shape[1] * nblk), o.dtype) for o in out_avals),
        name=name, compiler_params=_params(2),
    )(*ins, *params)


def _rowwise_bwd(fn, name, rows, consts, params, cts, tm, nblk):
    n_rows = rows[0].shape[0]
    tm = min(tm, n_rows)
    nr, nc, npar, nct = len(rows), len(consts), len(params), len(cts)

    def body(*refs):
        rv = [r[...] for r in refs[:nr]]
        cv = [r[...] for r in refs[nr:nr + nc]]
        pv = [r[...] for r in refs[nr + nc:nr + nc + npar]]
        ctv = [r[...] for r in refs[nr + nc + npar:nr + nc + npar + nct]]
        outs = refs[nr + nc + npar + nct:]
        _, vjp = jax.vjp(lambda *d: fn(*d[:nr], *cv, *d[nr:]), *rv, *pv)
        grads = vjp(tuple(ctv))
        for k in range(nr):
            outs[k][...] = grads[k]
        first = jnp.logical_and(pl.program_id(0) == 0, pl.program_id(1) == 0)
        for k in range(npar):
            ref = outs[nr + k]

            @pl.when(first)
            def _(ref=ref):
                ref[...] = jnp.zeros_like(ref)

            ref[...] += grads[nr + k]

    ins = list(rows) + list(consts)
    return pl.pallas_call(
        body, grid=(n_rows // tm, nblk),
        in_specs=(_rw_specs(ins, tm, nblk) + [pl.BlockSpec(p.shape, lambda i, j: (0, 0)) for p in params]
                  + _rw_specs(cts, tm, nblk)),
        out_specs=tuple(_rw_specs(rows, tm, nblk) + [pl.BlockSpec(p.shape, lambda i, j: (0, 0)) for p in params]),
        out_shape=tuple([jax.ShapeDtypeStruct(a.shape, F32) for a in rows]
                        + [jax.ShapeDtypeStruct(p.shape, F32) for p in params]),
        name=name, compiler_params=_params(2),
    )(*ins, *params, *cts)


def rowwise(fn, name, rows, consts=(), params=(), tm=256, nblk=1):
    rows, consts, params = tuple(rows), tuple(consts), tuple(params)

    @jax.custom_vjp
    def op(rows, consts, params):
        return _rowwise_fwd(fn, name, rows, consts, params, tm, nblk)

    def fwd(rows, consts, params):
        return op(rows, consts, params), (rows, consts, params)

    def bwd(saved, cts):
        rows, consts, params = saved
        grads = _rowwise_bwd(fn, name + "_bwd", rows, consts, params, tuple(cts), tm, nblk)
        return tuple(grads[:len(rows)]), None, tuple(grads[len(rows):])

    op.defvjp(fwd, bwd)
    return op(rows, consts, params)


def _rms_fn(x, gain):
    return (x * lax.rsqrt(jnp.mean(x * x, axis=-1, keepdims=True) + EPS) * gain,)


def _swiglu_fn(g, u):
    return (g * jax.nn.sigmoid(g) * u,)


def _merge_fn(ga, gb, pa, pb):
    return (jax.nn.sigmoid(ga) * pa + jax.nn.sigmoid(gb) * pb,)


def _outnorm_gate_fn(o, gate, gain):
    y = o * lax.rsqrt(jnp.mean(o * o, axis=-1, keepdims=True) + EPS) * gain
    return (y * (gate * jax.nn.sigmoid(gate)),)


def _beta_decay_fn(beta_raw, decay_raw, a_log, dt_bias):
    z = decay_raw + dt_bias
    softplus = jnp.maximum(z, 0.0) + jnp.log(1.0 + jnp.exp(-jnp.abs(z)))
    return jax.nn.sigmoid(beta_raw), -jnp.exp(a_log) * softplus


def _combine_fn(o0, o1, o2, l0, l1, l2):
    m = lax.stop_gradient(jnp.maximum(jnp.maximum(l0, l1), l2))
    e0, e1, e2 = jnp.exp(l0 - m), jnp.exp(l1 - m), jnp.exp(l2 - m)
    return ((e0 * o0 + e1 * o1 + e2 * o2) / (e0 + e1 + e2),)


def _loss_fn(x, target, gain):
    y = x * lax.rsqrt(jnp.mean(x * x, axis=-1, keepdims=True) + EPS) * gain
    err = y - target
    return (0.5 * jnp.mean(err * err, axis=-1, keepdims=True),)


def _rope_call(x, cos, sin, name):
    n_rows, width = x.shape
    tm = 512

    def body(x_ref, c_ref, s_ref, o_ref):
        v = x_ref[...]
        lane = lax.broadcasted_iota(jnp.int32, v.shape, 1)
        low = (lane % DSW_HEAD_DIM) < DSW_HEAD_DIM // 2
        half = DSW_HEAD_DIM // 2
        swapped = jnp.where(low, pltpu.roll(v, LANES - half, 1), pltpu.roll(v, half, 1))
        o_ref[...] = v * c_ref[...] + swapped * s_ref[...]

    tab = pl.BlockSpec((tm, LANES), lambda i, j: (i, 0))
    blk = pl.BlockSpec((tm, LANES), lambda i, j: (i, j))
    return pl.pallas_call(
        body, grid=(n_rows // tm, width // LANES), in_specs=[blk, tab, tab], out_specs=blk,
        out_shape=jax.ShapeDtypeStruct(x.shape, F32), name=name, compiler_params=_params(2),
    )(x, cos, sin)


def rope(x, cos, sin, name):
    @jax.custom_vjp
    def op(x):
        return _rope_call(x, cos, sin, name)

    def fwd(x):
        return op(x), None

    def bwd(_, g):
        return (_rope_call(g, cos, -sin, name + "_bwd"),)

    op.defvjp(fwd, bwd)
    return op(x)


def _rope_tables(n_tokens):
    half = DSW_HEAD_DIM // 2
    inv_freq = ROPE_THETA ** (-jnp.arange(half, dtype=F32) / half)
    ang = jnp.arange(n_tokens, dtype=F32)[:, None] * inv_freq[None, :]
    cos, sin = jnp.cos(ang), jnp.sin(ang)
    return jnp.tile(jnp.concatenate([cos, cos], 1), (1, 2)), jnp.tile(jnp.concatenate([-sin, sin], 1), (1, 2))


def _attn_probs(q, kp, kc, h, n):
    blk = DSW_BLOCK
    k = jnp.concatenate([kp, kc], axis=0).astype(BF16)
    s = lax.dot_general(q.astype(BF16), k, NT, preferred_element_type=F32) * (DSW_HEAD_DIM ** -0.5)
    blocks_per_seq = jnp.where(h < 4, 16, jnp.where(h < 8, 4, 1))
    first = (n % blocks_per_seq) == 0
    qi = lax.broadcasted_iota(jnp.int32, (blk, 2 * blk), 0)
    kj = lax.broadcasted_iota(jnp.int32, (blk, 2 * blk), 1)
    dist = qi + blk - kj
    valid = (dist >= 0) & (dist <= blk) & jnp.logical_or(kj >= blk, jnp.logical_not(first))
    s = jnp.where(valid, s, -1e30)
    m = jnp.max(s, axis=-1, keepdims=True)
    p = jnp.exp(s - m)
    l = jnp.sum(p, axis=-1, keepdims=True)
    return p / l, m + jnp.log(l), k


def _attn_specs(n_tokens):
    blk = DSW_BLOCK
    cur = pl.BlockSpec((1, blk, DSW_HEAD_DIM), lambda h, n: (h, n, 0))
    prev = pl.BlockSpec((1, blk, DSW_HEAD_DIM), lambda h, n: (h, jnp.maximum(n - 1, 0), 0))
    return cur, prev


def _attn_fwd(q, k, v):
    nh, n_tokens, hd = q.shape
    cur, prev = _attn_specs(n_tokens)

    def body(q_ref, kp_ref, kc_ref, vp_ref, vc_ref, o_ref, l_ref):
        h, n = pl.program_id(0), pl.program_id(1)
        p, lse, _ = _attn_probs(q_ref[0], kp_ref[0], kc_ref[0], h, n)
        vv = jnp.concatenate([vp_ref[0], vc_ref[0]], axis=0).astype(BF16)
        o_ref[0] = lax.dot_general(p.astype(BF16), vv, NN, preferred_element_type=F32)
        l_ref[0] = jnp.broadcast_to(lse, (DSW_BLOCK, hd))

    return pl.pallas_call(
        body, grid=(nh, n_tokens // DSW_BLOCK), in_specs=[cur, prev, cur, prev, cur], out_specs=(cur, cur),
        out_shape=(jax.ShapeDtypeStruct(q.shape, F32), jax.ShapeDtypeStruct(q.shape, F32)),
        name="attn_fwd", compiler_params=_params(2),
    )(q, k, k, v, v)


def _attn_bwd(q, k, v, do, dlse):
    nh, n_tokens, hd = q.shape
    nblk = n_tokens // DSW_BLOCK
    cur, prev = _attn_specs(n_tokens)
    part = pl.BlockSpec((1, 1, 2 * DSW_BLOCK, hd), lambda h, n: (h, n, 0, 0))

    def body(q_ref, kp_ref, kc_ref, vp_ref, vc_ref, do_ref, dl_ref, dq_ref, dk_ref, dv_ref):
        h, n = pl.program_id(0), pl.program_id(1)
        qb = q_ref[0].astype(BF16)
        p, _, kb = _attn_probs(q_ref[0], kp_ref[0], kc_ref[0], h, n)
        vv = jnp.concatenate([vp_ref[0], vc_ref[0]], axis=0).astype(BF16)
        dob = do_ref[0].astype(BF16)
        dp = lax.dot_general(dob, vv, NT, preferred_element_type=F32)
        dv_ref[0, 0] = lax.dot_general(p.astype(BF16), dob, TN, preferred_element_type=F32)
        dlse = jnp.sum(dl_ref[0], axis=-1, keepdims=True)
        ds = p * (dp - jnp.sum(dp * p, axis=-1, keepdims=True) + dlse) * (DSW_HEAD_DIM ** -0.5)
        dsb = ds.astype(BF16)
        dq_ref[0] = lax.dot_general(dsb, kb, NN, preferred_element_type=F32)
        dk_ref[0, 0] = lax.dot_general(dsb, qb, TN, preferred_element_type=F32)

    dq, dkp, dvp = pl.pallas_call(
        body, grid=(nh, nblk), in_specs=[cur, prev, cur, prev, cur, cur, cur], out_specs=(cur, part, part),
        out_shape=(jax.ShapeDtypeStruct(q.shape, F32),
                   jax.ShapeDtypeStruct((nh, nblk, 2 * DSW_BLOCK, hd), F32),
                   jax.ShapeDtypeStruct((nh, nblk, 2 * DSW_BLOCK, hd), F32)),
        name="attn_bwd", compiler_params=_params(2),
    )(q, k, k, v, v, do, dlse)

    def fold(partial):
        own = partial[:, :, DSW_BLOCK:]
        from_next = jnp.pad(partial[:, 1:, :DSW_BLOCK], ((0, 0), (0, 1), (0, 0), (0, 0)))
        return (own + from_next).reshape(nh, n_tokens, hd)

    return dq, fold(dkp), fold(dvp)


@jax.custom_vjp
def attention(q, k, v):
    return _attn_fwd(q, k, v)


def _attention_fwd(q, k, v):
    return _attn_fwd(q, k, v), (q, k, v)


def _attention_bwd(saved, cts):
    q, k, v = saved
    return _attn_bwd(q, k, v, cts[0], cts[1])


attention.defvjp(_attention_fwd, _attention_bwd)


def _to_heads(a):
    n_tokens = a.shape[0]
    outs = []
    for gi, d in enumerate(DSW_DILATIONS):
        blk = a[:, gi * 256:(gi + 1) * 256].reshape(n_tokens // d, d, DSW_HEADS_PER_GROUP, DSW_HEAD_DIM)
        outs.append(blk.transpose(2, 1, 0, 3).reshape(DSW_HEADS_PER_GROUP, n_tokens, DSW_HEAD_DIM))
    return jnp.concatenate(outs, 0)


def _from_heads(a):
    n_tokens = a.shape[1]
    outs = []
    for gi, d in enumerate(DSW_DILATIONS):
        blk = a[gi * 4:(gi + 1) * 4].reshape(DSW_HEADS_PER_GROUP, d, n_tokens // d, DSW_HEAD_DIM)
        outs.append(blk.transpose(2, 1, 0, 3).reshape(n_tokens, DSW_HEADS_PER_GROUP * DSW_HEAD_DIM))
    return outs


CONV_TILE = 512


def _shift_down(x, k, rows):
    return x if k == 0 else jnp.where(rows >= k, pltpu.roll(x, k, 0), 0.0)


def _shift_up(x, k, rows):
    n = x.shape[0]
    return x if k == 0 else jnp.where(rows < n - k, pltpu.roll(x, n - k, 0), 0.0)


def _conv_pre(x, w):
    rows = lax.broadcasted_iota(jnp.int32, x.shape, 0)
    acc = x * w[GDN_CONV - 1:GDN_CONV]
    for k in range(1, GDN_CONV):
        acc = acc + _shift_down(x, k, rows) * w[GDN_CONV - 1 - k:GDN_CONV - k]
    return acc, rows


def _conv_fwd(x, w):
    n_tokens, width = x.shape
    big = pl.BlockSpec((n_tokens, CONV_TILE), lambda j: (0, j))
    wsp = pl.BlockSpec((GDN_CONV, CONV_TILE), lambda j: (0, j))

    def body(x_ref, w_ref, o_ref):
        acc, _ = _conv_pre(x_ref[...], w_ref[...])
        o_ref[...] = acc * jax.nn.sigmoid(acc)

    return pl.pallas_call(
        body, grid=(width // CONV_TILE,), in_specs=[big, wsp], out_specs=big,
        out_shape=jax.ShapeDtypeStruct(x.shape, F32), name="conv_fwd", compiler_params=_params(1),
    )(x, w)


def _conv_bwd(x, w, dy):
    n_tokens, width = x.shape
    big = pl.BlockSpec((n_tokens, CONV_TILE), lambda j: (0, j))
    wsp = pl.BlockSpec((GDN_CONV, CONV_TILE), lambda j: (0, j))

    def body(x_ref, w_ref, dy_ref, dx_ref, dw_ref):
        xv, wv = x_ref[...], w_ref[...]
        acc, rows = _conv_pre(xv, wv)
        sg = jax.nn.sigmoid(acc)
        dacc = dy_ref[...] * (sg + acc * sg * (1.0 - sg))
        dx = dacc * wv[GDN_CONV - 1:GDN_CONV]
        for k in range(1, GDN_CONV):
            dx = dx + _shift_up(dacc, k, rows) * wv[GDN_CONV - 1 - k:GDN_CONV - k]
        dx_ref[...] = dx
        for k in range(GDN_CONV):
            dw_ref[GDN_CONV - 1 - k:GDN_CONV - k, :] = jnp.sum(dacc * _shift_down(xv, k, rows), axis=0, keepdims=True)

    return pl.pallas_call(
        body, grid=(width // CONV_TILE,), in_specs=[big, wsp, big], out_specs=(big, wsp),
        out_shape=(jax.ShapeDtypeStruct(x.shape, F32), jax.ShapeDtypeStruct(w.shape, F32)),
        name="conv_bwd", compiler_params=_params(1),
    )(x, w, dy)


@jax.custom_vjp
def conv_silu(x, w):
    return _conv_fwd(x, w)


def _conv_silu_fwd(x, w):
    return _conv_fwd(x, w), (x, w)


def _conv_silu_bwd(saved, g):
    return _conv_bwd(saved[0], saved[1], g)


conv_silu.defvjp(_conv_silu_fwd, _conv_silu_bwd)


def _dot(a, b, dn=NN):
    return lax.dot_general(a, b, dn, precision=HI, preferred_element_type=F32)


def _gdn_chunk(q, k, v, b, g, state):
    c = GDN_CHUNK
    qn = q * lax.rsqrt(jnp.sum(q * q, axis=-1, keepdims=True) + EPS) * (GDN_HEAD_DIM ** -0.5)
    kn = k * lax.rsqrt(jnp.sum(k * k, axis=-1, keepdims=True) + EPS)
    ii = lax.broadcasted_iota(jnp.int32, (c, c), 0)
    jj = lax.broadcasted_iota(jnp.int32, (c, c), 1)
    gb = jnp.broadcast_to(g, (c, c))
    gcum_i = _dot((jj <= ii).astype(F32), gb)
    gcum_j = _dot(jnp.ones((c, c), F32), gb * (ii <= jj).astype(F32))
    decay = jnp.exp(jnp.where(jj <= ii, gcum_i - gcum_j, -1e30))
    gcum = gcum_i[:, :1]
    g_last = gcum_i[c - 1:c, :1]
    e_gcum = jnp.exp(gcum)
    kbeta, vbeta = kn * b, v * b
    m = jnp.where(jj < ii, _dot(kbeta, kn, NT) * decay, 0.0)
    inv = (ii == jj).astype(F32) - m
    power = _dot(m, m)
    for step in range(5):
        inv = inv + _dot(inv, power)
        if step < 4:
            power = _dot(power, power)
    u = _dot(inv, vbeta)
    w = _dot(inv, kbeta * e_gcum)
    a_qk = _dot(qn, kn, NT) * decay
    v_new = u - _dot(w, state)
    o = _dot(qn * e_gcum, state) + _dot(a_qk, v_new)
    new_state = state * jnp.exp(g_last) + _dot(kn * jnp.exp(g_last - gcum), v_new, TN)
    return o, new_state


def _gdn_specs(n_tokens):
    nh = GDN_HEADS
    q = pl.BlockSpec((n_tokens, GDN_HEAD_DIM), lambda h: (0, h))
    k = pl.BlockSpec((n_tokens, GDN_HEAD_DIM), lambda h: (0, nh + h))
    v = pl.BlockSpec((n_tokens, GDN_HEAD_DIM), lambda h: (0, 2 * nh + h))
    vec = pl.BlockSpec((1, n_tokens, 1), lambda h: (h, 0, 0))
    states = pl.BlockSpec((1, n_tokens // GDN_CHUNK, GDN_HEAD_DIM, GDN_HEAD_DIM), lambda h: (h, 0, 0, 0))
    return q, k, v, vec, states


def _gdn_fwd(qkv, beta, g):
    n_tokens = qkv.shape[0]
    n_chunks = n_tokens // GDN_CHUNK
    q_s, k_s, v_s, vec, st = _gdn_specs(n_tokens)

    def body(q_ref, k_ref, v_ref, b_ref, g_ref, o_ref, st_ref, state):
        state[...] = jnp.zeros_like(state)

        def step(c, carry):
            r = pl.ds(pl.multiple_of(c * GDN_CHUNK, GDN_CHUNK), GDN_CHUNK)
            st_ref[0, c] = state[...]
            o, new = _gdn_chunk(q_ref[r, :], k_ref[r, :], v_ref[r, :], b_ref[0, r, :], g_ref[0, r, :], state[...])
            o_ref[r, :] = o
            state[...] = new
            return carry

        lax.fori_loop(0, n_chunks, step, 0)

    return pl.pallas_call(
        body, grid=(GDN_HEADS,), in_specs=[q_s, k_s, v_s, vec, vec], out_specs=(q_s, st),
        out_shape=(jax.ShapeDtypeStruct((n_tokens, GDN_WIDTH), F32),
                   jax.ShapeDtypeStruct((GDN_HEADS, n_chunks, GDN_HEAD_DIM, GDN_HEAD_DIM), F32)),
        scratch_shapes=[pltpu.VMEM((GDN_HEAD_DIM, GDN_HEAD_DIM), F32)],
        name="gdn_fwd", compiler_params=_params(1),
    )(qkv, qkv, qkv, beta, g)


def _gdn_bwd(qkv, beta, g, states, do):
    n_tokens = qkv.shape[0]
    n_chunks = n_tokens // GDN_CHUNK
    q_s, k_s, v_s, vec, st = _gdn_specs(n_tokens)

    def body(q_ref, k_ref, v_ref, b_ref, g_ref, st_ref, do_ref, dq_ref, dk_ref, dv_ref, db_ref, dg_ref, dstate):
        dstate[...] = jnp.zeros_like(dstate)

        def step(i, carry):
            c = n_chunks - 1 - i
            r = pl.ds(pl.multiple_of(c * GDN_CHUNK, GDN_CHUNK), GDN_CHUNK)
            args = (q_ref[r, :], k_ref[r, :], v_ref[r, :], b_ref[0, r, :], g_ref[0, r, :], st_ref[0, c])
            _, vjp = jax.vjp(_gdn_chunk, *args)
            dq, dk, dv, db, dg, dst = vjp((do_ref[r, :], dstate[...]))
            dq_ref[r, :] = dq
            dk_ref[r, :] = dk
            dv_ref[r, :] = dv
            db_ref[0, r, :] = db
            dg_ref[0, r, :] = dg
            dstate[...] = dst
            return carry

        lax.fori_loop(0, n_chunks, step, 0)

    wide = jax.ShapeDtypeStruct((n_tokens, GDN_WIDTH), F32)
    thin = jax.ShapeDtypeStruct(beta.shape, F32)
    dq, dk, dv, db, dg = pl.pallas_call(
        body, grid=(GDN_HEADS,), in_specs=[q_s, k_s, v_s, vec, vec, st, q_s], out_specs=(q_s, q_s, q_s, vec, vec),
        out_shape=(wide, wide, wide, thin, thin),
        scratch_shapes=[pltpu.VMEM((GDN_HEAD_DIM, GDN_HEAD_DIM), F32)],
        name="gdn_bwd", compiler_params=_params(1),
    )(qkv, qkv, qkv, beta, g, states, do)
    return jnp.concatenate([dq, dk, dv], axis=1), db, dg


@jax.custom_vjp
def gated_delta(qkv, beta, g):
    return _gdn_fwd(qkv, beta, g)[0]


def _gated_delta_fwd(qkv, beta, g):
    o, states = _gdn_fwd(qkv, beta, g)
    return o, (qkv, beta, g, states)


def _gated_delta_bwd(saved, do):
    return _gdn_bwd(*saved, do)


gated_delta.defvjp(_gated_delta_fwd, _gated_delta_bwd)


BIG = ("ffn1_w_gate", "ffn1_w_up", "ffn1_w_down", "wq_a", "wk_a", "wv_a", "w_qkvb", "w_small", "w_ggate",
       "w_gatea", "w_gateb", "w_branch_a", "w_branch_b", "w_out", "ffn2_w_gate", "ffn2_w_up", "ffn2_w_down")
SMALL = ("ffn1_norm", "mix_norm", "ffn2_norm", "final_norm", "gdn_a_log", "gdn_dt_bias", "gdn_out_norm", "gdn_conv_w")


def _ffn(x, gain, w, slots, tag):
    h = rowwise(_rms_fn, tag + "_norm", (x,), params=(gain,))[0]
    g = mm(h, w[tag + "_w_gate"], slots[tag + "_w_gate"], tag + "_gate")
    u = mm(h, w[tag + "_w_up"], slots[tag + "_w_up"], tag + "_up")
    a = rowwise(_swiglu_fn, tag + "_act", (g, u), tm=256, nblk=2)[0]
    return mm_res(a, w[tag + "_w_down"], slots[tag + "_w_down"], x, 0.5, tag + "_down")


def _local_loss(diff, x_target, w):
    x, slots, small = diff
    target = x_target
    n_tokens = x.shape[0]
    x1 = _ffn(x, small["ffn1_norm"], w, slots, "ffn1")

    h = rowwise(_rms_fn, "mix_norm", (x1,), params=(small["mix_norm"],))[0]
    proj = {n: mm(h, w[n], slots[n], "in_" + n)
            for n in ("wq_a", "wk_a", "wv_a", "w_qkvb", "w_small", "w_ggate", "w_gatea", "w_gateb")}

    cos, sin = _rope_tables(n_tokens)
    q = _to_heads(rope(proj["wq_a"], cos, sin, "rope_q"))
    k = _to_heads(rope(proj["wk_a"], cos, sin, "rope_k"))
    v = _to_heads(proj["wv_a"])
    o, lse = attention(q, k, v)
    ya = rowwise(_combine_fn, "combine", tuple(_from_heads(o)) + tuple(_from_heads(lse)), tm=512)[0]
    pa = mm(ya, w["w_branch_a"], slots["w_branch_a"], "branch_a")

    qkv = conv_silu(proj["w_qkvb"], small["gdn_conv_w"])
    beta, g = rowwise(_beta_decay_fn, "beta_decay",
                      (proj["w_small"][:, :GDN_HEADS], proj["w_small"][:, GDN_HEADS:2 * GDN_HEADS]),
                      params=(small["gdn_a_log"], small["gdn_dt_bias"]), tm=512)
    ob = gated_delta(qkv, beta.T[:, :, None], g.T[:, :, None])
    yb = rowwise(_outnorm_gate_fn, "outnorm_gate", (ob, proj["w_ggate"]), params=(small["gdn_out_norm"],),
                 tm=512, nblk=GDN_HEADS)[0]
    pb = mm(yb, w["w_branch_b"], slots["w_branch_b"], "branch_b")

    merged = rowwise(_merge_fn, "merge", (proj["w_gatea"], proj["w_gateb"], pa, pb))[0]
    x2 = mm_res(merged, w["w_out"], slots["w_out"], x1, 1.0, "out")

    x3 = _ffn(x2, small["ffn2_norm"], w, slots, "ffn2")
    row_loss = rowwise(_loss_fn, "loss", (x3,), consts=(target,), params=(small["final_norm"],))[0]
    return jnp.sum(row_loss)


SHARDS = (
    ("ffn1_w_gate", "col", 1024, 352), ("ffn1_w_up", "col", 1024, 352), ("ffn1_w_down", "row", 352, 1024),
    ("w_in", "col", 1024, 1058), ("w_branch_a", "col", 256, 128), ("w_branch_b", "row", 128, 1024),
    ("w_out", "row", 128, 1024),
    ("ffn2_w_gate", "col", 1024, 352), ("ffn2_w_up", "col", 1024, 352), ("ffn2_w_down", "row", 352, 1024),
)
CONV_SHARD = (GDN_CONV, 3 * GDN_WIDTH // N_DEV)
PACK_ROWS, PACK_WIDTH, PACK_TILE = 1760, 2048, 160
N_PACKED = sum(r * c for _, _, r, c in SHARDS)
assert N_PACKED + 2 * CONV_SHARD[0] * CONV_SHARD[1] <= PACK_ROWS * PACK_WIDTH
SMALL_ROWS = 24
ANY = pl.BlockSpec(memory_space=pl.ANY)


def _position():
    return lax.axis_index("x"), lax.axis_index("y"), lax.axis_index("c")


def all_gather_slabs(shard, name):
    def body(x_ref, out_ref, send_sems, recv_sems, local_sem):
        x, y, c = _position()
        me, sibling = (x, y, c), (x, y, 1 - c)
        chips = [(1 - x, y), (x, 1 - y), (1 - x, 1 - y)]

        def slab(px, py, pc):
            return out_ref.at[4 * px + 2 * py + pc]

        def copy(k, block, to, src=None):
            return pltpu.make_async_remote_copy(
                src_ref=slab(*block) if src is None else src, dst_ref=slab(*block),
                send_sem=send_sems.at[k], recv_sem=recv_sems.at[k], device_id=to, device_id_type=MESH)

        mine = pltpu.make_async_copy(x_ref, slab(*me), local_sem)
        mine.start()
        first = [copy(0, me, sibling, src=x_ref)]
        first += [copy(1 + j, me, (*chip, c), src=x_ref) for j, chip in enumerate(chips)]
        for cp in first:
            cp.start()
        passed = [copy(4 + j, (*chip, c), sibling) for j, chip in enumerate(chips)]
        for j, chip in enumerate(chips):
            copy(1 + j, (*chip, c), me).wait_recv()
            passed[j].start()
        copy(0, sibling, me).wait_recv()
        for j, chip in enumerate(chips):
            copy(4 + j, (*chip, 1 - c), me).wait_recv()
        for cp in first + passed:
            cp.wait_send()
        mine.wait()

    return pl.pallas_call(
        body, out_shape=jax.ShapeDtypeStruct((N_DEV,) + shard.shape, shard.dtype), in_specs=[ANY], out_specs=ANY,
        scratch_shapes=[pltpu.SemaphoreType.DMA((7,)), pltpu.SemaphoreType.DMA((7,)), pltpu.SemaphoreType.DMA],
        name=name,
    )(shard)


def exchange_with_sibling(grads):
    def body(g_ref, recv_ref, send_sems, recv_sems):
        x, y, c = _position()
        copies = [pltpu.make_async_remote_copy(
            src_ref=g_ref.at[2 * k + 1 - c], dst_ref=recv_ref.at[k], send_sem=send_sems.at[k],
            recv_sem=recv_sems.at[k], device_id=(x, y, 1 - c), device_id_type=MESH) for k in range(4)]
        for cp in copies:
            cp.start()
        for cp in copies:
            cp.wait()

    return pl.pallas_call(
        body, out_shape=jax.ShapeDtypeStruct((4,) + grads.shape[1:], grads.dtype), in_specs=[ANY], out_specs=ANY,
        scratch_shapes=[pltpu.SemaphoreType.DMA((4,)), pltpu.SemaphoreType.DMA((4,))], name="rs_sibling",
    )(grads)


def add_sibling(grads, received, core):
    _, rows, width = grads.shape

    def body(c_ref, g_ref, r_ref, o_ref):
        o_ref[...] = g_ref[...] + r_ref[...]

    blk = (1, PACK_TILE, width)
    return pl.pallas_call(
        body,
        grid_spec=pltpu.PrefetchScalarGridSpec(
            num_scalar_prefetch=1, grid=(4, rows // PACK_TILE),
            in_specs=[pl.BlockSpec(blk, lambda k, i, c_ref: (2 * k + c_ref[0], i, 0)),
                      pl.BlockSpec(blk, lambda k, i, c_ref: (k, i, 0))],
            out_specs=pl.BlockSpec(blk, lambda k, i, c_ref: (k, i, 0))),
        out_shape=jax.ShapeDtypeStruct((4, rows, width), F32), name="rs_add", compiler_params=_params(2),
    )(core, grads, received)


def exchange_with_chips(partial):
    def body(p_ref, recv_ref, send_sems, recv_sems):
        x, y, c = _position()
        chips = [(1 - x, y), (x, 1 - y), (1 - x, 1 - y)]
        copies = [pltpu.make_async_remote_copy(
            src_ref=p_ref.at[2 * cx + cy], dst_ref=recv_ref.at[j], send_sem=send_sems.at[j],
            recv_sem=recv_sems.at[j], device_id=(cx, cy, c), device_id_type=MESH) for j, (cx, cy) in enumerate(chips)]
        for cp in copies:
            cp.start()
        for cp in copies:
            cp.wait()

    return pl.pallas_call(
        body, out_shape=jax.ShapeDtypeStruct((3,) + partial.shape[1:], partial.dtype), in_specs=[ANY], out_specs=ANY,
        scratch_shapes=[pltpu.SemaphoreType.DMA((3,)), pltpu.SemaphoreType.DMA((3,))], name="rs_chips",
    )(partial)


def add_chips(partial, received, chip):
    _, rows, width = partial.shape

    def body(c_ref, p_ref, r_ref, o_ref):
        o_ref[...] = ((p_ref[0] + r_ref[0]) + r_ref[1]) + r_ref[2]

    return pl.pallas_call(
        body,
        grid_spec=pltpu.PrefetchScalarGridSpec(
            num_scalar_prefetch=1, grid=(rows // PACK_TILE,),
            in_specs=[pl.BlockSpec((1, PACK_TILE, width), lambda i, c_ref: (c_ref[0], i, 0)),
                      pl.BlockSpec((3, PACK_TILE, width), lambda i, c_ref: (0, i, 0))],
            out_specs=pl.BlockSpec((PACK_TILE, width), lambda i, c_ref: (i, 0))),
        out_shape=jax.ShapeDtypeStruct((rows, width), F32), name="rs_final", compiler_params=_params(1),
    )(chip, partial, received)


def all_reduce_small(vals):
    rows, width = vals.shape

    def body(x_ref, out_ref, all_ref, send_sems, recv_sems):
        x, y, c = _position()
        me, sibling = (x, y, c), (x, y, 1 - c)
        chips = [(1 - x, y), (x, 1 - y), (1 - x, 1 - y)]

        def slab(px, py, pc):
            return all_ref.at[4 * px + 2 * py + pc]

        def copy(k, block, to, src=None):
            return pltpu.make_async_remote_copy(
                src_ref=slab(*block) if src is None else src, dst_ref=slab(*block),
                send_sem=send_sems.at[k], recv_sem=recv_sems.at[k], device_id=to, device_id_type=MESH)

        first = [copy(0, me, sibling, src=x_ref)]
        first += [copy(1 + j, me, (*chip, c), src=x_ref) for j, chip in enumerate(chips)]
        for cp in first:
            cp.start()
        all_ref[4 * x + 2 * y + c] = x_ref[...]
        passed = [copy(4 + j, (*chip, c), sibling) for j, chip in enumerate(chips)]
        for j, chip in enumerate(chips):
            copy(1 + j, (*chip, c), me).wait_recv()
            passed[j].start()
        copy(0, sibling, me).wait_recv()
        for j, chip in enumerate(chips):
            copy(4 + j, (*chip, 1 - c), me).wait_recv()
        for cp in first + passed:
            cp.wait_send()
        total = all_ref[0]
        for d in range(1, N_DEV):
            total = total + all_ref[d]
        out_ref[...] = total

    vmem = pl.BlockSpec(memory_space=pltpu.VMEM)
    return pl.pallas_call(
        body, out_shape=(jax.ShapeDtypeStruct(vals.shape, F32), jax.ShapeDtypeStruct((N_DEV, rows, width), F32)),
        in_specs=[vmem], out_specs=(vmem, vmem),
        scratch_shapes=[pltpu.SemaphoreType.DMA((7,)), pltpu.SemaphoreType.DMA((7,))], name="small_allreduce",
    )(vals)[0]


def adamw(w, g, m, v, name):
    shape = w.shape
    w2, g2, m2, v2 = [a.reshape((-1, shape[-1])) for a in (w, g, m, v)]
    rows, cols = w2.shape
    tr = 256 if rows % 256 == 0 else rows

    def body(w_ref, g_ref, m_ref, v_ref, d_ref, nm_ref, nv_ref):
        gv = g_ref[...]
        nm = ADAM_B1 * m_ref[...] + (1.0 - ADAM_B1) * gv
        nv = ADAM_B2 * v_ref[...] + (1.0 - ADAM_B2) * (gv * gv)
        m_hat = nm / (1.0 - ADAM_B1 ** ADAM_STEP)
        v_hat = nv / (1.0 - ADAM_B2 ** ADAM_STEP)
        d_ref[...] = -ADAM_LR * (m_hat / (jnp.sqrt(v_hat) + ADAM_EPS) + ADAM_WD * w_ref[...])
        nm_ref[...] = nm
        nv_ref[...] = nv

    blk = pl.BlockSpec((tr, cols), lambda i: (i, 0))
    out = jax.ShapeDtypeStruct((rows, cols), F32)
    outs = pl.pallas_call(
        body, grid=(rows // tr,), in_specs=[blk] * 4, out_specs=(blk,) * 3, out_shape=(out,) * 3,
        name=name, compiler_params=_params(1),
    )(w2, g2, m2, v2)
    return tuple(o.reshape(shape) for o in outs)


def _pack_weight_shards(local):
    parts = [local[n].reshape(-1).astype(BF16) for n, _, _, _ in SHARDS]
    parts.append(lax.bitcast_convert_type(local["gdn_conv_w"].reshape(CONV_SHARD), BF16).reshape(-1))
    flat = jnp.concatenate(parts)
    return jnp.pad(flat, (0, PACK_ROWS * PACK_WIDTH - flat.shape[0])).reshape(PACK_ROWS, PACK_WIDTH)


def _unpack_gathered(gathered):
    flat = gathered.reshape(N_DEV, -1)
    full, off = {}, 0
    for n, kind, r, c in SHARDS:
        seg = flat[:, off:off + r * c].reshape(N_DEV, r, c)
        full[n] = seg.transpose(1, 0, 2).reshape(r, N_DEV * c) if kind == "col" else seg.reshape(N_DEV * r, c)
        off += r * c
    n_conv = CONV_SHARD[0] * CONV_SHARD[1]
    conv = lax.bitcast_convert_type(flat[:, off:off + 2 * n_conv].reshape(N_DEV, *CONV_SHARD, 2), F32)
    conv = conv.transpose(1, 0, 2).reshape(GDN_CONV, 3 * GDN_WIDTH)
    w_in = full.pop("w_in")
    o = IN_OFFSETS
    full.update(
        wq_a=w_in[:, o["qa"]:o["ka"]], wk_a=w_in[:, o["ka"]:o["va"]], wv_a=w_in[:, o["va"]:o["qkvb"]],
        w_qkvb=w_in[:, o["qkvb"]:o["small"]],
        w_small=jnp.pad(w_in[:, o["small"]:o["ggate"]], ((0, 0), (0, LANES - 2 * GDN_HEADS))),
        w_ggate=w_in[:, o["ggate"]:o["gatea"]], w_gatea=w_in[:, o["gatea"]:o["gateb"]], w_gateb=w_in[:, o["gateb"]:])
    return full, conv


def _pack_grads(gw):
    gw = dict(gw)
    gw["w_in"] = jnp.concatenate(
        [gw["wq_a"], gw["wk_a"], gw["wv_a"], gw["w_qkvb"], gw["w_small"][:, :2 * GDN_HEADS], gw["w_ggate"],
         gw["w_gatea"], gw["w_gateb"]], axis=1)
    parts = []
    for n, kind, r, c in SHARDS:
        g = gw[n]
        g = g.reshape(r, N_DEV, c).transpose(1, 0, 2) if kind == "col" else g.reshape(N_DEV, r, c)
        parts.append(g.reshape(N_DEV, r * c))
    flat = jnp.concatenate(parts, axis=1)
    flat = jnp.pad(flat, ((0, 0), (0, PACK_ROWS * PACK_WIDTH - flat.shape[1])))
    return flat.reshape(N_DEV, PACK_ROWS, PACK_WIDTH)


def _unpack_shard_grads(packed):
    flat, out, off = packed.reshape(-1), {}, 0
    for n, _, r, c in SHARDS:
        out[n] = flat[off:off + r * c].reshape(r, c)
        off += r * c
    return out


SMALL_VECTORS = ("ffn1_norm", "mix_norm", "ffn2_norm", "final_norm")


def _pack_small(gs):
    row = jnp.concatenate([gs["gdn_a_log"].reshape(-1), gs["gdn_dt_bias"].reshape(-1), gs["gdn_out_norm"].reshape(-1)])
    rows = [gs[n].reshape(1, D_MODEL) for n in SMALL_VECTORS]
    rows.append(jnp.pad(row, (0, D_MODEL - row.shape[0])).reshape(1, D_MODEL))
    rows.append(gs["gdn_conv_w"].reshape(-1, D_MODEL))
    packed = jnp.concatenate(rows, axis=0)
    return jnp.pad(packed, ((0, SMALL_ROWS - packed.shape[0]), (0, 0)))


def _unpack_small(packed):
    out = {n: packed[i].reshape(1, D_MODEL) for i, n in enumerate(SMALL_VECTORS)}
    row = packed[len(SMALL_VECTORS)]
    out["gdn_a_log"] = row[:GDN_HEADS].reshape(1, GDN_HEADS)
    out["gdn_dt_bias"] = row[GDN_HEADS:2 * GDN_HEADS].reshape(1, GDN_HEADS)
    out["gdn_out_norm"] = row[2 * GDN_HEADS:2 * GDN_HEADS + GDN_HEAD_DIM].reshape(1, GDN_HEAD_DIM)
    first = len(SMALL_VECTORS) + 1
    out["gdn_conv_w"] = packed[first:first + GDN_CONV * 3].reshape(GDN_CONV, 3 * GDN_WIDTH)
    return out


WEIGHTS = ("ffn1_norm", "ffn1_w_gate", "ffn1_w_up", "ffn1_w_down", "mix_norm", "w_in", "gdn_conv_w", "gdn_a_log",
           "gdn_dt_bias", "gdn_out_norm", "w_branch_a", "w_branch_b", "w_out", "ffn2_norm", "ffn2_w_gate",
           "ffn2_w_up", "ffn2_w_down", "final_norm")


def kernel(x, ffn1_norm, ffn1_w_gate, ffn1_w_up, ffn1_w_down, mix_norm, w_in, gdn_conv_w, gdn_a_log, gdn_dt_bias, gdn_out_norm, w_branch_a, w_branch_b, w_out, ffn2_norm, ffn2_w_gate, ffn2_w_up, ffn2_w_down, final_norm, loss_target, m_ffn1_norm, m_ffn1_w_gate, m_ffn1_w_up, m_ffn1_w_down, m_mix_norm, m_w_in, m_gdn_conv_w, m_gdn_a_log, m_gdn_dt_bias, m_gdn_out_norm, m_w_branch_a, m_w_branch_b, m_w_out, m_ffn2_norm, m_ffn2_w_gate, m_ffn2_w_up, m_ffn2_w_down, m_final_norm, v_ffn1_norm, v_ffn1_w_gate, v_ffn1_w_up, v_ffn1_w_down, v_mix_norm, v_w_in, v_gdn_conv_w, v_gdn_a_log, v_gdn_dt_bias, v_gdn_out_norm, v_w_branch_a, v_w_branch_b, v_w_out, v_ffn2_norm, v_ffn2_w_gate, v_ffn2_w_up, v_ffn2_w_down, v_final_norm):
    given = dict(locals())
    local = {n: given[n] for n in WEIGHTS}
    px, py, pc = _position()

    gathered = all_gather_slabs(_pack_weight_shards(local), "gather_weights")
    w, conv_full = _unpack_gathered(gathered)
    slots = {n: jnp.zeros(w[n].shape, F32) for n in BIG}
    small = dict(ffn1_norm=ffn1_norm, mix_norm=mix_norm, ffn2_norm=ffn2_norm, final_norm=final_norm.reshape(1, D_MODEL),
                 gdn_a_log=gdn_a_log, gdn_dt_bias=gdn_dt_bias, gdn_out_norm=gdn_out_norm, gdn_conv_w=conv_full)

    loss_local, (grad_x, g_big, g_small) = jax.value_and_grad(_local_loss)((x[0], slots, small), loss_target[0], w)
    loss = lax.psum(loss_local, ("x", "y", "c"))

    packed = _pack_grads(g_big)
    from_sibling = exchange_with_sibling(packed)
    partial = add_sibling(packed, from_sibling, pc.astype(jnp.int32).reshape(1))
    from_chips = exchange_with_chips(partial)
    summed = add_chips(partial, from_chips, (2 * px + py).astype(jnp.int32).reshape(1))
    grads = _unpack_shard_grads(summed)

    small_sum = _unpack_small(all_reduce_small(_pack_small(g_small)))
    conv_cols = CONV_SHARD[1]
    me = 4 * px + 2 * py + pc
    grads["gdn_conv_w"] = lax.dynamic_slice(small_sum.pop("gdn_conv_w"), (0, me * conv_cols), (GDN_CONV, conv_cols))
    grads.update(small_sum)

    g_out, d_out, m_out, v_out = [], [], [], []
    for n in WEIGHTS:
        g = grads[n].reshape(given[n].shape)
        d, nm, nv = adamw(given[n], g, given["m_" + n], given["v_" + n], "adamw_" + n)
        g_out.append(g)
        d_out.append(d)
        m_out.append(nm)
        v_out.append(nv)
    return (loss, grad_x[None], *g_out, *d_out, *m_out, *v_out)
```

```python
import jax
import jax.numpy as jnp
from jax import lax
from jax.experimental import pallas as pl
from jax.experimental.pallas import tpu as pltpu

F32 = jnp.float32
BF16 = jnp.bfloat16
HI = lax.Precision.HIGHEST
MESH = pl.DeviceIdType.MESH

N_DEV = 8
D_MODEL = 1024
D_FF = 2816
EPS = 1e-6
ROPE_THETA = 10000.0
DSW_DILATIONS = (1, 4, 16)
DSW_HEADS_PER_GROUP = 4
DSW_HEAD_DIM = 64
DSW_BLOCK = 128
DSW_WIDTH = 768
N_DSW_HEADS = 12
GDN_HEADS = 8
GDN_HEAD_DIM = 128
GDN_WIDTH = 1024
GDN_CONV = 4
GDN_CHUNK = 64
IN_OFFSETS = dict(qa=0, ka=768, va=1536, qkvb=2304, small=5376, ggate=5392, gatea=6416, gateb=7440)
D_IN = 8464

ADAM_LR = 0.001
ADAM_B1 = 0.9
ADAM_B2 = 0.999
ADAM_EPS = 1e-08
ADAM_WD = 0.01
ADAM_STEP = 10

VMEM_LIMIT_BYTES = 56 * 1024 * 1024
LANES = 128

NN = (((1,), (0,)), ((), ()))
NT = (((1,), (1,)), ((), ()))
TN = (((0,), (0,)), ((), ()))


def _params(n_grid):
    return pltpu.CompilerParams(dimension_semantics=("arbitrary",) * n_grid, vmem_limit_bytes=VMEM_LIMIT_BYTES)


def _tile(n, pref):
    best = None
    t = LANES
    while t <= min(n, pref):
        if n % t == 0:
            best = t
        t += LANES
    return n if best is None else best


def _matmul(a, b, *, name, ta=False, tb=False, res=None, scale=1.0):
    K, M = a.shape if ta else a.shape[::-1]
    N = b.shape[0] if tb else b.shape[1]
    assert (b.shape[1] if tb else b.shape[0]) == K, (a.shape, b.shape, ta, tb)
    tm = _tile(M, 512)
    tn = _tile(N, 512)
    dn = (((0 if ta else 1,), (1 if tb else 0,)), ((), ()))

    def body(*refs):
        a_ref, b_ref = refs[:2]
        o_ref = refs[-1]
        acc = lax.dot_general(a_ref[...].astype(BF16), b_ref[...].astype(BF16), dn, preferred_element_type=F32)
        if scale != 1.0:
            acc = acc * scale
        if res is not None:
            acc = refs[2][...] + acc
        o_ref[...] = acc

    a_spec = pl.BlockSpec((K, tm), lambda i, j: (0, i)) if ta else pl.BlockSpec((tm, K), lambda i, j: (i, 0))
    b_spec = pl.BlockSpec((tn, K), lambda i, j: (j, 0)) if tb else pl.BlockSpec((K, tn), lambda i, j: (0, j))
    o_spec = pl.BlockSpec((tm, tn), lambda i, j: (i, j))
    ins, specs = [a, b], [a_spec, b_spec]
    if res is not None:
        ins.append(res)
        specs.append(o_spec)
    return pl.pallas_call(
        body, grid=(M // tm, N // tn), in_specs=specs, out_specs=o_spec,
        out_shape=jax.ShapeDtypeStruct((M, N), F32), name=name, compiler_params=_params(2),
    )(*ins)


def _make_mm(name, scale=1.0, with_res=False):
    @jax.custom_vjp
    def op(a, w, slot, res):
        return _matmul(a, w, name=name, res=res if with_res else None, scale=scale)

    def fwd(a, w, slot, res):
        return op(a, w, slot, res), (a, w)

    def bwd(saved, g):
        a, w = saved
        da = _matmul(g, w, name=name + "_da", tb=True, scale=scale)
        dw = _matmul(a, g, name=name + "_dw", ta=True, scale=scale)
        return da, None, dw, (g if with_res else None)

    op.defvjp(fwd, bwd)
    return op


def mm(a, w, slot, name):
    return _make_mm(name)(a, w, slot, None)


def mm_res(a, w, slot, res, scale, name):
    return _make_mm(name, scale=scale, with_res=True)(a, w, slot, res)


def _mm_shards_out(a, w, *, name, tb=False, scale=1.0):
    n_shards = w.shape[0]
    m, k = a.shape
    n = w.shape[1] if tb else w.shape[2]
    tm = _tile(m, 512)
    dn = NT if tb else NN

    def body(a_ref, w_ref, o_ref):
        acc = lax.dot_general(a_ref[...].astype(BF16), w_ref[0].astype(BF16), dn, preferred_element_type=F32)
        o_ref[0] = acc * scale if scale != 1.0 else acc

    return pl.pallas_call(
        body, grid=(m // tm, n_shards),
        in_specs=[pl.BlockSpec((tm, k), lambda i, j: (i, 0)), pl.BlockSpec((1,) + w.shape[1:], lambda i, j: (j, 0, 0))],
        out_specs=pl.BlockSpec((1, tm, n), lambda i, j: (j, i, 0)),
        out_shape=jax.ShapeDtypeStruct((n_shards, m, n), F32), name=name, compiler_params=_params(2),
    )(a, w)


def _mm_shards_sum(a, w, *, name, tb=False, res=None, scale=1.0):
    n_shards, m, n = a.shape
    n_out = w.shape[1] if tb else w.shape[2]
    tm, tn = _tile(m, 512), _tile(n_out, 512)
    dn = NT if tb else NN

    def body(*refs):
        a_ref, w_ref = refs[:2]
        o_ref = refs[-1]
        acc = None
        for j in range(n_shards):
            part = lax.dot_general(a_ref[j].astype(BF16), w_ref[j].astype(BF16), dn, preferred_element_type=F32)
            acc = part if acc is None else acc + part
        if scale != 1.0:
            acc = acc * scale
        if res is not None:
            acc = refs[2][...] + acc
        o_ref[...] = acc

    w_spec = (pl.BlockSpec((n_shards, tn, n), lambda i, j: (0, j, 0)) if tb
              else pl.BlockSpec((n_shards, n, tn), lambda i, j: (0, 0, j)))
    o_spec = pl.BlockSpec((tm, tn), lambda i, j: (i, j))
    ins, specs = [a, w], [pl.BlockSpec((n_shards, tm, n), lambda i, j: (0, i, 0)), w_spec]
    if res is not None:
        ins.append(res)
        specs.append(o_spec)
    return pl.pallas_call(
        body, grid=(m // tm, n_out // tn), in_specs=specs, out_specs=o_spec,
        out_shape=jax.ShapeDtypeStruct((m, n_out), F32), name=name, compiler_params=_params(2),
    )(*ins)


def _mm_dw_shards_out(a, g, *, name):
    n_shards, m, n = g.shape
    k = a.shape[1]

    def body(a_ref, g_ref, o_ref):
        o_ref[0] = lax.dot_general(a_ref[...].astype(BF16), g_ref[0].astype(BF16), TN, preferred_element_type=F32)

    return pl.pallas_call(
        body, grid=(n_shards,),
        in_specs=[pl.BlockSpec((m, k), lambda j: (0, 0)), pl.BlockSpec((1, m, n), lambda j: (j, 0, 0))],
        out_specs=pl.BlockSpec((1, k, n), lambda j: (j, 0, 0)),
        out_shape=jax.ShapeDtypeStruct((n_shards, k, n), F32), name=name, compiler_params=_params(1),
    )(a, g)


def _mm_dw_shards_sum(a, g, *, name, scale=1.0):
    n_shards, m, n = a.shape
    n_out = g.shape[1]

    def body(a_ref, g_ref, o_ref):
        acc = lax.dot_general(a_ref[0].astype(BF16), g_ref[...].astype(BF16), TN, preferred_element_type=F32)
        o_ref[0] = acc * scale if scale != 1.0 else acc

    return pl.pallas_call(
        body, grid=(n_shards,),
        in_specs=[pl.BlockSpec((1, m, n), lambda j: (j, 0, 0)), pl.BlockSpec((m, n_out), lambda j: (0, 0))],
        out_specs=pl.BlockSpec((1, n, n_out), lambda j: (j, 0, 0)),
        out_shape=jax.ShapeDtypeStruct((n_shards, n, n_out), F32), name=name, compiler_params=_params(1),
    )(a, g)


def mm_shards_out(a, w, slot, name):
    @jax.custom_vjp
    def op(a, w, slot):
        return _mm_shards_out(a, w, name=name)

    def fwd(a, w, slot):
        return op(a, w, slot), (a, w)

    def bwd(saved, g):
        a, w = saved
        return _mm_shards_sum(g, w, name=name + "_da", tb=True), None, _mm_dw_shards_out(a, g, name=name + "_dw")

    op.defvjp(fwd, bwd)
    return op(a, w, slot)


def mm_shards_sum(a, w, slot, res, scale, name):
    @jax.custom_vjp
    def op(a, w, slot, res):
        return _mm_shards_sum(a, w, name=name, res=res, scale=scale)

    def fwd(a, w, slot, res):
        return op(a, w, slot, res), (a, w)

    def bwd(saved, g):
        a, w = saved
        da = _mm_shards_out(g, w, name=name + "_da", tb=True, scale=scale)
        return da, None, _mm_dw_shards_sum(a, g, name=name + "_dw", scale=scale), g

    op.defvjp(fwd, bwd)
    return op(a, w, slot, res)


def _rw_specs(arrs, tm, nblk):
    return [pl.BlockSpec((tm, a.shape[1] // nblk), lambda i, j: (i, j)) for a in arrs]


def _rowwise_fwd(fn, name, rows, consts, params, tm, nblk):
    n_rows = rows[0].shape[0]
    tm = min(tm, n_rows)
    ins = list(rows) + list(consts)
    avals = [jax.ShapeDtypeStruct((tm, a.shape[1] // nblk), a.dtype) for a in ins]
    avals += [jax.ShapeDtypeStruct(p.shape, p.dtype) for p in params]
    out_avals = jax.eval_shape(fn, *avals)
    n_in = len(ins) + len(params)

    def body(*refs):
        outs = fn(*[r[...] for r in refs[:n_in]])
        for r, o in zip(refs[n_in:], outs):
            r[...] = o.astype(r.dtype)

    return pl.pallas_call(
        body, grid=(n_rows // tm, nblk),
        in_specs=_rw_specs(ins, tm, nblk) + [pl.BlockSpec(p.shape, lambda i, j: (0, 0)) for p in params],
        out_specs=tuple(pl.BlockSpec((tm, o.shape[1]), lambda i, j: (i, j)) for o in out_avals),
        out_shape=tuple(jax.ShapeDtypeStruct((n_rows, o.shape[1] * nblk), o.dtype) for o in out_avals),
        name=name, compiler_params=_params(2),
    )(*ins, *params)


def _rowwise_bwd(fn, name, rows, consts, params, cts, tm, nblk):
    n_rows = rows[0].shape[0]
    tm = min(tm, n_rows)
    nr, nc, npar, nct = len(rows), len(consts), len(params), len(cts)

    def body(*refs):
        rv = [r[...] for r in refs[:nr]]
        cv = [r[...] for r in refs[nr:nr + nc]]
        pv = [r[...] for r in refs[nr + nc:nr + nc + npar]]
        ctv = [r[...] for r in refs[nr + nc + npar:nr + nc + npar + nct]]
        outs = refs[nr + nc + npar + nct:]
        _, vjp = jax.vjp(lambda *d: fn(*d[:nr], *cv, *d[nr:]), *rv, *pv)
        grads = vjp(tuple(ctv))
        for k in range(nr):
            outs[k][...] = grads[k]
        first = jnp.logical_and(pl.program_id(0) == 0, pl.program_id(1) == 0)
        for k in range(npar):
            ref = outs[nr + k]

            @pl.when(first)
            def _(ref=ref):
                ref[...] = jnp.zeros_like(ref)

            ref[...] += grads[nr + k]

    ins = list(rows) + list(consts)
    return pl.pallas_call(
        body, grid=(n_rows // tm, nblk),
        in_specs=(_rw_specs(ins, tm, nblk) + [pl.BlockSpec(p.shape, lambda i, j: (0, 0)) for p in params]
                  + _rw_specs(cts, tm, nblk)),
        out_specs=tuple(_rw_specs(rows, tm, nblk) + [pl.BlockSpec(p.shape, lambda i, j: (0, 0)) for p in params]),
        out_shape=tuple([jax.ShapeDtypeStruct(a.shape, F32) for a in rows]
                        + [jax.ShapeDtypeStruct(p.shape, F32) for p in params]),
        name=name, compiler_params=_params(2),
    )(*ins, *params, *cts)


def rowwise(fn, name, rows, consts=(), params=(), tm=256, nblk=1):
    rows, consts, params = tuple(rows), tuple(consts), tuple(params)

    @jax.custom_vjp
    def op(rows, consts, params):
        return _rowwise_fwd(fn, name, rows, consts, params, tm, nblk)

    def fwd(rows, consts, params):
        return op(rows, consts, params), (rows, consts, params)

    def bwd(saved, cts):
        rows, consts, params = saved
        grads = _rowwise_bwd(fn, name + "_bwd", rows, consts, params, tuple(cts), tm, nblk)
        return tuple(grads[:len(rows)]), None, tuple(grads[len(rows):])

    op.defvjp(fwd, bwd)
    return op(rows, consts, params)


def _rms_fn(x, gain):
    return (x * lax.rsqrt(jnp.mean(x * x, axis=-1, keepdims=True) + EPS) * gain,)


def _swiglu_fn(g, u):
    return (g * jax.nn.sigmoid(g) * u,)


def _merge_fn(ga, gb, pa, pb):
    return (jax.nn.sigmoid(ga) * pa + jax.nn.sigmoid(gb) * pb,)


def _outnorm_gate_fn(o, gate, gain):
    y = o * lax.rsqrt(jnp.mean(o * o, axis=-1, keepdims=True) + EPS) * gain
    return (y * (gate * jax.nn.sigmoid(gate)),)


def _beta_decay_fn(beta_raw, decay_raw, a_log, dt_bias):
    z = decay_raw + dt_bias
    softplus = jnp.maximum(z, 0.0) + jnp.log(1.0 + jnp.exp(-jnp.abs(z)))
    return jax.nn.sigmoid(beta_raw), -jnp.exp(a_log) * softplus


def _combine_fn(o0, o1, o2, l0, l1, l2):
    m = lax.stop_gradient(jnp.maximum(jnp.maximum(l0, l1), l2))
    e0, e1, e2 = jnp.exp(l0 - m), jnp.exp(l1 - m), jnp.exp(l2 - m)
    return ((e0 * o0 + e1 * o1 + e2 * o2) / (e0 + e1 + e2),)


def _loss_fn(x, target, gain):
    y = x * lax.rsqrt(jnp.mean(x * x, axis=-1, keepdims=True) + EPS) * gain
    err = y - target
    return (0.5 * jnp.mean(err * err, axis=-1, keepdims=True),)


def _rope_call(x, cos, sin, name):
    n_rows, width = x.shape
    tm = 512

    def body(x_ref, c_ref, s_ref, o_ref):
        v = x_ref[...]
        lane = lax.broadcasted_iota(jnp.int32, v.shape, 1)
        low = (lane % DSW_HEAD_DIM) < DSW_HEAD_DIM // 2
        half = DSW_HEAD_DIM // 2
        swapped = jnp.where(low, pltpu.roll(v, LANES - half, 1), pltpu.roll(v, half, 1))
        o_ref[...] = v * c_ref[...] + swapped * s_ref[...]

    tab = pl.BlockSpec((tm, LANES), lambda i, j: (i, 0))
    blk = pl.BlockSpec((tm, LANES), lambda i, j: (i, j))
    return pl.pallas_call(
        body, grid=(n_rows // tm, width // LANES), in_specs=[blk, tab, tab], out_specs=blk,
        out_shape=jax.ShapeDtypeStruct(x.shape, F32), name=name, compiler_params=_params(2),
    )(x, cos, sin)


def rope(x, cos, sin, name):
    @jax.custom_vjp
    def op(x):
        return _rope_call(x, cos, sin, name)

    def fwd(x):
        return op(x), None

    def bwd(_, g):
        return (_rope_call(g, cos, -sin, name + "_bwd"),)

    op.defvjp(fwd, bwd)
    return op(x)


def _rope_tables(n_tokens):
    half = DSW_HEAD_DIM // 2
    inv_freq = ROPE_THETA ** (-jnp.arange(half, dtype=F32) / half)
    ang = jnp.arange(n_tokens, dtype=F32)[:, None] * inv_freq[None, :]
    cos, sin = jnp.cos(ang), jnp.sin(ang)
    return jnp.tile(jnp.concatenate([cos, cos], 1), (1, 2)), jnp.tile(jnp.concatenate([-sin, sin], 1), (1, 2))


def _attn_probs(q, kp, kc, h, n):
    blk = DSW_BLOCK
    k = jnp.concatenate([kp, kc], axis=0).astype(BF16)
    s = lax.dot_general(q.astype(BF16), k, NT, preferred_element_type=F32) * (DSW_HEAD_DIM ** -0.5)
    blocks_per_seq = jnp.where(h < 4, 16, jnp.where(h < 8, 4, 1))
    first = (n % blocks_per_seq) == 0
    qi = lax.broadcasted_iota(jnp.int32, (blk, 2 * blk), 0)
    kj = lax.broadcasted_iota(jnp.int32, (blk, 2 * blk), 1)
    dist = qi + blk - kj
    valid = (dist >= 0) & (dist <= blk) & jnp.logical_or(kj >= blk, jnp.logical_not(first))
    s = jnp.where(valid, s, -1e30)
    m = jnp.max(s, axis=-1, keepdims=True)
    p = jnp.exp(s - m)
    l = jnp.sum(p, axis=-1, keepdims=True)
    return p / l, m + jnp.log(l), k


def _attn_specs(n_tokens):
    blk = DSW_BLOCK
    cur = pl.BlockSpec((1, blk, DSW_HEAD_DIM), lambda h, n: (h, n, 0))
    prev = pl.BlockSpec((1, blk, DSW_HEAD_DIM), lambda h, n: (h, jnp.maximum(n - 1, 0), 0))
    return cur, prev


def _attn_fwd(q, k, v):
    nh, n_tokens, hd = q.shape
    cur, prev = _attn_specs(n_tokens)

    def body(q_ref, kp_ref, kc_ref, vp_ref, vc_ref, o_ref, l_ref):
        h, n = pl.program_id(0), pl.program_id(1)
        p, lse, _ = _attn_probs(q_ref[0], kp_ref[0], kc_ref[0], h, n)
        vv = jnp.concatenate([vp_ref[0], vc_ref[0]], axis=0).astype(BF16)
        o_ref[0] = lax.dot_general(p.astype(BF16), vv, NN, preferred_element_type=F32)
        l_ref[0] = jnp.broadcast_to(lse, (DSW_BLOCK, hd))

    return pl.pallas_call(
        body, grid=(nh, n_tokens // DSW_BLOCK), in_specs=[cur, prev, cur, prev, cur], out_specs=(cur, cur),
        out_shape=(jax.ShapeDtypeStruct(q.shape, F32), jax.ShapeDtypeStruct(q.shape, F32)),
        name="attn_fwd", compiler_params=_params(2),
    )(q, k, k, v, v)


def _attn_bwd(q, k, v, do, dlse):
    nh, n_tokens, hd = q.shape
    nblk = n_tokens // DSW_BLOCK
    cur, prev = _attn_specs(n_tokens)
    part = pl.BlockSpec((1, 1, 2 * DSW_BLOCK, hd), lambda h, n: (h, n, 0, 0))

    def body(q_ref, kp_ref, kc_ref, vp_ref, vc_ref, do_ref, dl_ref, dq_ref, dk_ref, dv_ref):
        h, n = pl.program_id(0), pl.program_id(1)
        qb = q_ref[0].astype(BF16)
        p, _, kb = _attn_probs(q_ref[0], kp_ref[0], kc_ref[0], h, n)
        vv = jnp.concatenate([vp_ref[0], vc_ref[0]], axis=0).astype(BF16)
        dob = do_ref[0].astype(BF16)
        dp = lax.dot_general(dob, vv, NT, preferred_element_type=F32)
        dv_ref[0, 0] = lax.dot_general(p.astype(BF16), dob, TN, preferred_element_type=F32)
        dlse = jnp.sum(dl_ref[0], axis=-1, keepdims=True)
        ds = p * (dp - jnp.sum(dp * p, axis=-1, keepdims=True) + dlse) * (DSW_HEAD_DIM ** -0.5)
        dsb = ds.astype(BF16)
        dq_ref[0] = lax.dot_general(dsb, kb, NN, preferred_element_type=F32)
        dk_ref[0, 0] = lax.dot_general(dsb, qb, TN, preferred_element_type=F32)

    dq, dkp, dvp = pl.pallas_call(
        body, grid=(nh, nblk), in_specs=[cur, prev, cur, prev, cur, cur, cur], out_specs=(cur, part, part),
        out_shape=(jax.ShapeDtypeStruct(q.shape, F32),
                   jax.ShapeDtypeStruct((nh, nblk, 2 * DSW_BLOCK, hd), F32),
                   jax.ShapeDtypeStruct((nh, nblk, 2 * DSW_BLOCK, hd), F32)),
        name="attn_bwd", compiler_params=_params(2),
    )(q, k, k, v, v, do, dlse)

    def fold(partial):
        own = partial[:, :, DSW_BLOCK:]
        from_next = jnp.pad(partial[:, 1:, :DSW_BLOCK], ((0, 0), (0, 1), (0, 0), (0, 0)))
        return (own + from_next).reshape(nh, n_tokens, hd)

    return dq, fold(dkp), fold(dvp)


@jax.custom_vjp
def attention(q, k, v):
    return _attn_fwd(q, k, v)


def _attention_fwd(q, k, v):
    return _attn_fwd(q, k, v), (q, k, v)


def _attention_bwd(saved, cts):
    q, k, v = saved
    return _attn_bwd(q, k, v, cts[0], cts[1])


attention.defvjp(_attention_fwd, _attention_bwd)


def _to_heads(a):
    n_tokens = a.shape[0]
    outs = []
    for gi, d in enumerate(DSW_DILATIONS):
        blk = a[:, gi * 256:(gi + 1) * 256].reshape(n_tokens // d, d, DSW_HEADS_PER_GROUP, DSW_HEAD_DIM)
        outs.append(blk.transpose(2, 1, 0, 3).reshape(DSW_HEADS_PER_GROUP, n_tokens, DSW_HEAD_DIM))
    return jnp.concatenate(outs, 0)


def _from_heads(a):
    n_tokens = a.shape[1]
    outs = []
    for gi, d in enumerate(DSW_DILATIONS):
        blk = a[gi * 4:(gi + 1) * 4].reshape(DSW_HEADS_PER_GROUP, d, n_tokens // d, DSW_HEAD_DIM)
        outs.append(blk.transpose(2, 1, 0, 3).reshape(n_tokens, DSW_HEADS_PER_GROUP * DSW_HEAD_DIM))
    return outs


CONV_TILE = 512


def _shift_down(x, k, rows):
    return x if k == 0 else jnp.where(rows >= k, pltpu.roll(x, k, 0), 0.0)


def _shift_up(x, k, rows):
    n = x.shape[0]
    return x if k == 0 else jnp.where(rows < n - k, pltpu.roll(x, n - k, 0), 0.0)


def _conv_pre(x, w):
    rows = lax.broadcasted_iota(jnp.int32, x.shape, 0)
    acc = x * w[GDN_CONV - 1:GDN_CONV]
    for k in range(1, GDN_CONV):
        acc = acc + _shift_down(x, k, rows) * w[GDN_CONV - 1 - k:GDN_CONV - k]
    return acc, rows


def _conv_fwd(x, w):
    n_tokens, width = x.shape
    big = pl.BlockSpec((n_tokens, CONV_TILE), lambda j: (0, j))
    wsp = pl.BlockSpec((GDN_CONV, CONV_TILE), lambda j: (0, j))

    def body(x_ref, w_ref, o_ref):
        acc, _ = _conv_pre(x_ref[...], w_ref[...])
        o_ref[...] = acc * jax.nn.sigmoid(acc)

    return pl.pallas_call(
        body, grid=(width // CONV_TILE,), in_specs=[big, wsp], out_specs=big,
        out_shape=jax.ShapeDtypeStruct(x.shape, F32), name="conv_fwd", compiler_params=_params(1),
    )(x, w)


def _conv_bwd(x, w, dy):
    n_tokens, width = x.shape
    big = pl.BlockSpec((n_tokens, CONV_TILE), lambda j: (0, j))
    wsp = pl.BlockSpec((GDN_CONV, CONV_TILE), lambda j: (0, j))

    def body(x_ref, w_ref, dy_ref, dx_ref, dw_ref):
        xv, wv = x_ref[...], w_ref[...]
        acc, rows = _conv_pre(xv, wv)
        sg = jax.nn.sigmoid(acc)
        dacc = dy_ref[...] * (sg + acc * sg * (1.0 - sg))
        dx = dacc * wv[GDN_CONV - 1:GDN_CONV]
        for k in range(1, GDN_CONV):
            dx = dx + _shift_up(dacc, k, rows) * wv[GDN_CONV - 1 - k:GDN_CONV - k]
        dx_ref[...] = dx
        for k in range(GDN_CONV):
            dw_ref[GDN_CONV - 1 - k:GDN_CONV - k, :] = jnp.sum(dacc * _shift_down(xv, k, rows), axis=0, keepdims=True)

    return pl.pallas_call(
        body, grid=(width // CONV_TILE,), in_specs=[big, wsp, big], out_specs=(big, wsp),
        out_shape=(jax.ShapeDtypeStruct(x.shape, F32), jax.ShapeDtypeStruct(w.shape, F32)),
        name="conv_bwd", compiler_params=_params(1),
    )(x, w, dy)


@jax.custom_vjp
def conv_silu(x, w):
    return _conv_fwd(x, w)


def _conv_silu_fwd(x, w):
    return _conv_fwd(x, w), (x, w)


def _conv_silu_bwd(saved, g):
    return _conv_bwd(saved[0], saved[1], g)


conv_silu.defvjp(_conv_silu_fwd, _conv_silu_bwd)


def _dot(a, b, dn=NN):
    return lax.dot_general(a, b, dn, precision=HI, preferred_element_type=F32)


def _gdn_chunk(q, k, v, b, g, state):
    c = GDN_CHUNK
    qn = q * lax.rsqrt(jnp.sum(q * q, axis=-1, keepdims=True) + EPS) * (GDN_HEAD_DIM ** -0.5)
    kn = k * lax.rsqrt(jnp.sum(k * k, axis=-1, keepdims=True) + EPS)
    ii = lax.broadcasted_iota(jnp.int32, (c, c), 0)
    jj = lax.broadcasted_iota(jnp.int32, (c, c), 1)
    gb = jnp.broadcast_to(g, (c, c))
    gcum_i = _dot((jj <= ii).astype(F32), gb)
    gcum_j = _dot(jnp.ones((c, c), F32), gb * (ii <= jj).astype(F32))
    decay = jnp.exp(jnp.where(jj <= ii, gcum_i - gcum_j, -1e30))
    gcum = gcum_i[:, :1]
    g_last = gcum_i[c - 1:c, :1]
    e_gcum = jnp.exp(gcum)
    kbeta, vbeta = kn * b, v * b
    m = jnp.where(jj < ii, _dot(kbeta, kn, NT) * decay, 0.0)
    inv = (ii == jj).astype(F32) - m
    power = _dot(m, m)
    for step in range(5):
        inv = inv + _dot(inv, power)
        if step < 4:
            power = _dot(power, power)
    u = _dot(inv, vbeta)
    w = _dot(inv, kbeta * e_gcum)
    a_qk = _dot(qn, kn, NT) * decay
    v_new = u - _dot(w, state)
    o = _dot(qn * e_gcum, state) + _dot(a_qk, v_new)
    new_state = state * jnp.exp(g_last) + _dot(kn * jnp.exp(g_last - gcum), v_new, TN)
    return o, new_state


def _gdn_specs(n_tokens):
    nh = GDN_HEADS
    q = pl.BlockSpec((n_tokens, GDN_HEAD_DIM), lambda h: (0, h))
    k = pl.BlockSpec((n_tokens, GDN_HEAD_DIM), lambda h: (0, nh + h))
    v = pl.BlockSpec((n_tokens, GDN_HEAD_DIM), lambda h: (0, 2 * nh + h))
    vec = pl.BlockSpec((1, n_tokens, 1), lambda h: (h, 0, 0))
    states = pl.BlockSpec((1, n_tokens // GDN_CHUNK, GDN_HEAD_DIM, GDN_HEAD_DIM), lambda h: (h, 0, 0, 0))
    return q, k, v, vec, states


def _gdn_fwd(qkv, beta, g):
    n_tokens = qkv.shape[0]
    n_chunks = n_tokens // GDN_CHUNK
    q_s, k_s, v_s, vec, st = _gdn_specs(n_tokens)

    def body(q_ref, k_ref, v_ref, b_ref, g_ref, o_ref, st_ref, state):
        state[...] = jnp.zeros_like(state)

        def step(c, carry):
            r = pl.ds(pl.multiple_of(c * GDN_CHUNK, GDN_CHUNK), GDN_CHUNK)
            st_ref[0, c] = state[...]
            o, new = _gdn_chunk(q_ref[r, :], k_ref[r, :], v_ref[r, :], b_ref[0, r, :], g_ref[0, r, :], state[...])
            o_ref[r, :] = o
            state[...] = new
            return carry

        lax.fori_loop(0, n_chunks, step, 0)

    return pl.pallas_call(
        body, grid=(GDN_HEADS,), in_specs=[q_s, k_s, v_s, vec, vec], out_specs=(q_s, st),
        out_shape=(jax.ShapeDtypeStruct((n_tokens, GDN_WIDTH), F32),
                   jax.ShapeDtypeStruct((GDN_HEADS, n_chunks, GDN_HEAD_DIM, GDN_HEAD_DIM), F32)),
        scratch_shapes=[pltpu.VMEM((GDN_HEAD_DIM, GDN_HEAD_DIM), F32)],
        name="gdn_fwd", compiler_params=_params(1),
    )(qkv, qkv, qkv, beta, g)


def _gdn_bwd(qkv, beta, g, states, do):
    n_tokens = qkv.shape[0]
    n_chunks = n_tokens // GDN_CHUNK
    q_s, k_s, v_s, vec, st = _gdn_specs(n_tokens)

    def body(q_ref, k_ref, v_ref, b_ref, g_ref, st_ref, do_ref, dq_ref, dk_ref, dv_ref, db_ref, dg_ref, dstate):
        dstate[...] = jnp.zeros_like(dstate)

        def step(i, carry):
            c = n_chunks - 1 - i
            r = pl.ds(pl.multiple_of(c * GDN_CHUNK, GDN_CHUNK), GDN_CHUNK)
            args = (q_ref[r, :], k_ref[r, :], v_ref[r, :], b_ref[0, r, :], g_ref[0, r, :], st_ref[0, c])
            _, vjp = jax.vjp(_gdn_chunk, *args)
            dq, dk, dv, db, dg, dst = vjp((do_ref[r, :], dstate[...]))
            dq_ref[r, :] = dq
            dk_ref[r, :] = dk
            dv_ref[r, :] = dv
            db_ref[0, r, :] = db
            dg_ref[0, r, :] = dg
            dstate[...] = dst
            return carry

        lax.fori_loop(0, n_chunks, step, 0)

    wide = jax.ShapeDtypeStruct((n_tokens, GDN_WIDTH), F32)
    thin = jax.ShapeDtypeStruct(beta.shape, F32)
    dq, dk, dv, db, dg = pl.pallas_call(
        body, grid=(GDN_HEADS,), in_specs=[q_s, k_s, v_s, vec, vec, st, q_s], out_specs=(q_s, q_s, q_s, vec, vec),
        out_shape=(wide, wide, wide, thin, thin),
        scratch_shapes=[pltpu.VMEM((GDN_HEAD_DIM, GDN_HEAD_DIM), F32)],
        name="gdn_bwd", compiler_params=_params(1),
    )(qkv, qkv, qkv, beta, g, states, do)
    return jnp.concatenate([dq, dk, dv], axis=1), db, dg


@jax.custom_vjp
def gated_delta(qkv, beta, g):
    return _gdn_fwd(qkv, beta, g)[0]


def _gated_delta_fwd(qkv, beta, g):
    o, states = _gdn_fwd(qkv, beta, g)
    return o, (qkv, beta, g, states)


def _gated_delta_bwd(saved, do):
    return _gdn_bwd(*saved, do)


gated_delta.defvjp(_gated_delta_fwd, _gated_delta_bwd)


def _ffn(x, gain, w, slots, tag):
    h = rowwise(_rms_fn, tag + "_norm", (x,), params=(gain,))[0]
    g = mm_shards_out(h, w[tag + "_w_gate"], slots[tag + "_w_gate"], tag + "_gate")
    u = mm_shards_out(h, w[tag + "_w_up"], slots[tag + "_w_up"], tag + "_up")
    flat = (g.shape[0] * g.shape[1], g.shape[2])
    a = rowwise(_swiglu_fn, tag + "_act", (g.reshape(flat), u.reshape(flat)), tm=512)[0].reshape(g.shape)
    return mm_shards_sum(a, w[tag + "_w_down"], slots[tag + "_w_down"], x, 0.5, tag + "_down")


def _local_loss(diff, x_target, w):
    x, slots, small = diff
    target = x_target
    n_tokens = x.shape[0]
    x1 = _ffn(x, small["ffn1_norm"], w, slots, "ffn1")

    h = rowwise(_rms_fn, "mix_norm", (x1,), params=(small["mix_norm"],))[0]
    proj = {n: mm(h, w[n], slots[n], "in_" + n)
            for n in ("wq_a", "wk_a", "wv_a", "w_qkvb", "w_small", "w_ggate", "w_gatea", "w_gateb")}

    cos, sin = _rope_tables(n_tokens)
    q = _to_heads(rope(proj["wq_a"], cos, sin, "rope_q"))
    k = _to_heads(rope(proj["wk_a"], cos, sin, "rope_k"))
    v = _to_heads(proj["wv_a"])
    o, lse = attention(q, k, v)
    ya = rowwise(_combine_fn, "combine", tuple(_from_heads(o)) + tuple(_from_heads(lse)), tm=512)[0]
    pa = mm(ya, w["w_branch_a"], slots["w_branch_a"], "branch_a")

    qkv = conv_silu(proj["w_qkvb"], small["gdn_conv_w"])
    beta, g = rowwise(_beta_decay_fn, "beta_decay",
                      (proj["w_small"][:, :GDN_HEADS], proj["w_small"][:, GDN_HEADS:2 * GDN_HEADS]),
                      params=(small["gdn_a_log"], small["gdn_dt_bias"]), tm=512)
    ob = gated_delta(qkv, beta.T[:, :, None], g.T[:, :, None])
    yb = rowwise(_outnorm_gate_fn, "outnorm_gate", (ob, proj["w_ggate"]), params=(small["gdn_out_norm"],),
                 tm=512, nblk=GDN_HEADS)[0]
    pb = mm(yb, w["w_branch_b"], slots["w_branch_b"], "branch_b")

    merged = rowwise(_merge_fn, "merge", (proj["w_gatea"], proj["w_gateb"], pa, pb))[0]
    x2 = mm_res(merged, w["w_out"], slots["w_out"], x1, 1.0, "out")

    x3 = _ffn(x2, small["ffn2_norm"], w, slots, "ffn2")
    row_loss = rowwise(_loss_fn, "loss", (x3,), consts=(target,), params=(small["final_norm"],))[0]
    return jnp.sum(row_loss)


SHARDS = (
    ("ffn1_w_gate", 1024, 352), ("ffn1_w_up", 1024, 352), ("ffn1_w_down", 352, 1024), ("w_in", 1024, 1058),
    ("w_branch_a", 256, 128), ("w_branch_b", 128, 1024), ("w_out", 128, 1024),
    ("ffn2_w_gate", 1024, 352), ("ffn2_w_up", 1024, 352), ("ffn2_w_down", 352, 1024),
)
IN_SHARD = D_IN // N_DEV
CONV_SHARD = (GDN_CONV, 3 * GDN_WIDTH // N_DEV)
SMALL_ROWS = 24
ANY = pl.BlockSpec(memory_space=pl.ANY)


def _position():
    return lax.axis_index("x"), lax.axis_index("y"), lax.axis_index("c")


def all_gather_shards(shards, name):
    n = len(shards)

    def body(*refs):
        x_refs, out_refs = refs[:n], refs[n:2 * n]
        send_sems, recv_sems, local_sems = refs[2 * n:]
        x, y, c = _position()
        me, sibling = (x, y, c), (x, y, 1 - c)
        chips = [(1 - x, y), (x, 1 - y), (1 - x, 1 - y)]

        def slab(a, px, py, pc):
            return out_refs[a].at[4 * px + 2 * py + pc]

        def copy(a, k, block, to, src=None):
            return pltpu.make_async_remote_copy(
                src_ref=slab(a, *block) if src is None else src, dst_ref=slab(a, *block),
                send_sem=send_sems.at[7 * a + k], recv_sem=recv_sems.at[7 * a + k], device_id=to, device_id_type=MESH)

        mine = [pltpu.make_async_copy(x_refs[a], slab(a, *me), local_sems.at[a]) for a in range(n)]
        for cp in mine:
            cp.start()
        first = []
        for j, chip in enumerate(chips):
            first += [copy(a, 1 + j, me, (*chip, c), src=x_refs[a]) for a in range(n)]
        first += [copy(a, 0, me, sibling, src=x_refs[a]) for a in range(n)]
        for cp in first:
            cp.start()
        passed = []
        for j, chip in enumerate(chips):
            for a in range(n):
                copy(a, 1 + j, (*chip, c), me).wait_recv()
                cp = copy(a, 4 + j, (*chip, c), sibling)
                cp.start()
                passed.append(cp)
        for a in range(n):
            copy(a, 0, sibling, me).wait_recv()
        for j, chip in enumerate(chips):
            for a in range(n):
                copy(a, 4 + j, (*chip, 1 - c), me).wait_recv()
        for cp in first + passed:
            cp.wait_send()
        for cp in mine:
            cp.wait()

    return pl.pallas_call(
        body, out_shape=tuple(jax.ShapeDtypeStruct((N_DEV,) + s.shape, s.dtype) for s in shards),
        in_specs=[ANY] * n, out_specs=(ANY,) * n,
        scratch_shapes=[pltpu.SemaphoreType.DMA((7 * n,)), pltpu.SemaphoreType.DMA((7 * n,)),
                        pltpu.SemaphoreType.DMA((n,))],
        name=name,
    )(*shards)


def exchange_with_sibling(grads):
    n = len(grads)

    def body(*refs):
        g_refs, recv_refs = refs[:n], refs[n:2 * n]
        send_sems, recv_sems = refs[2 * n:]
        x, y, c = _position()
        copies = [pltpu.make_async_remote_copy(
            src_ref=g_refs[a].at[2 * k + 1 - c], dst_ref=recv_refs[a].at[k], send_sem=send_sems.at[4 * a + k],
            recv_sem=recv_sems.at[4 * a + k], device_id=(x, y, 1 - c), device_id_type=MESH)
            for k in range(4) for a in range(n)]
        for cp in copies:
            cp.start()
        for cp in copies:
            cp.wait()

    return pl.pallas_call(
        body, out_shape=tuple(jax.ShapeDtypeStruct((4,) + g.shape[1:], g.dtype) for g in grads),
        in_specs=[ANY] * n, out_specs=(ANY,) * n,
        scratch_shapes=[pltpu.SemaphoreType.DMA((4 * n,)), pltpu.SemaphoreType.DMA((4 * n,))], name="rs_sibling",
    )(*grads)


def _row_tile(rows):
    return 256 if rows % 256 == 0 else rows


def add_sibling(grads, received, core, name):
    _, rows, width = grads.shape
    tr = _row_tile(rows)

    def body(c_ref, g_ref, r_ref, o_ref):
        o_ref[...] = g_ref[...] + r_ref[...]

    blk = (1, tr, width)
    return pl.pallas_call(
        body,
        grid_spec=pltpu.PrefetchScalarGridSpec(
            num_scalar_prefetch=1, grid=(4, rows // tr),
            in_specs=[pl.BlockSpec(blk, lambda k, i, c_ref: (2 * k + c_ref[0], i, 0)),
                      pl.BlockSpec(blk, lambda k, i, c_ref: (k, i, 0))],
            out_specs=pl.BlockSpec(blk, lambda k, i, c_ref: (k, i, 0))),
        out_shape=jax.ShapeDtypeStruct((4, rows, width), F32), name=name, compiler_params=_params(2),
    )(core, grads, received)


def exchange_with_chips(partials):
    n = len(partials)

    def body(*refs):
        p_refs, recv_refs = refs[:n], refs[n:2 * n]
        send_sems, recv_sems = refs[2 * n:]
        x, y, c = _position()
        chips = [(1 - x, y), (x, 1 - y), (1 - x, 1 - y)]
        copies = [pltpu.make_async_remote_copy(
            src_ref=p_refs[a].at[2 * cx + cy], dst_ref=recv_refs[a].at[j], send_sem=send_sems.at[3 * a + j],
            recv_sem=recv_sems.at[3 * a + j], device_id=(cx, cy, c), device_id_type=MESH)
            for a in range(n) for j, (cx, cy) in enumerate(chips)]
        for cp in copies:
            cp.start()
        for cp in copies:
            cp.wait()

    return pl.pallas_call(
        body, out_shape=tuple(jax.ShapeDtypeStruct((3,) + p.shape[1:], p.dtype) for p in partials),
        in_specs=[ANY] * n, out_specs=(ANY,) * n,
        scratch_shapes=[pltpu.SemaphoreType.DMA((3 * n,)), pltpu.SemaphoreType.DMA((3 * n,))], name="rs_chips",
    )(*partials)


def all_reduce_small(vals):
    rows, width = vals.shape

    def body(x_ref, out_ref, all_ref, send_sems, recv_sems):
        x, y, c = _position()
        me, sibling = (x, y, c), (x, y, 1 - c)
        chips = [(1 - x, y), (x, 1 - y), (1 - x, 1 - y)]

        def slab(px, py, pc):
            return all_ref.at[4 * px + 2 * py + pc]

        def copy(k, block, to, src=None):
            return pltpu.make_async_remote_copy(
                src_ref=slab(*block) if src is None else src, dst_ref=slab(*block),
                send_sem=send_sems.at[k], recv_sem=recv_sems.at[k], device_id=to, device_id_type=MESH)

        first = [copy(0, me, sibling, src=x_ref)]
        first += [copy(1 + j, me, (*chip, c), src=x_ref) for j, chip in enumerate(chips)]
        for cp in first:
            cp.start()
        all_ref[4 * x + 2 * y + c] = x_ref[...]
        passed = [copy(4 + j, (*chip, c), sibling) for j, chip in enumerate(chips)]
        for j, chip in enumerate(chips):
            copy(1 + j, (*chip, c), me).wait_recv()
            passed[j].start()
        copy(0, sibling, me).wait_recv()
        for j, chip in enumerate(chips):
            copy(4 + j, (*chip, 1 - c), me).wait_recv()
        for cp in first + passed:
            cp.wait_send()
        total = all_ref[0]
        for d in range(1, N_DEV):
            total = total + all_ref[d]
        out_ref[...] = total

    vmem = pl.BlockSpec(memory_space=pltpu.VMEM)
    return pl.pallas_call(
        body, out_shape=(jax.ShapeDtypeStruct(vals.shape, F32), jax.ShapeDtypeStruct((N_DEV, rows, width), F32)),
        in_specs=[vmem], out_specs=(vmem, vmem),
        scratch_shapes=[pltpu.SemaphoreType.DMA((7,)), pltpu.SemaphoreType.DMA((7,))], name="small_allreduce",
    )(vals)[0]


def adamw(w, g, m, v, name):
    shape = w.shape
    w2, g2, m2, v2 = [a.reshape((-1, shape[-1])) for a in (w, g, m, v)]
    rows, cols = w2.shape
    tr = 256 if rows % 256 == 0 else rows

    def body(w_ref, g_ref, m_ref, v_ref, d_ref, nm_ref, nv_ref):
        gv = g_ref[...]
        nm = ADAM_B1 * m_ref[...] + (1.0 - ADAM_B1) * gv
        nv = ADAM_B2 * v_ref[...] + (1.0 - ADAM_B2) * (gv * gv)
        m_hat = nm / (1.0 - ADAM_B1 ** ADAM_STEP)
        v_hat = nv / (1.0 - ADAM_B2 ** ADAM_STEP)
        d_ref[...] = -ADAM_LR * (m_hat / (jnp.sqrt(v_hat) + ADAM_EPS) + ADAM_WD * w_ref[...])
        nm_ref[...] = nm
        nv_ref[...] = nv

    blk = pl.BlockSpec((tr, cols), lambda i: (i, 0))
    out = jax.ShapeDtypeStruct((rows, cols), F32)
    outs = pl.pallas_call(
        body, grid=(rows // tr,), in_specs=[blk] * 4, out_specs=(blk,) * 3, out_shape=(out,) * 3,
        name=name, compiler_params=_params(1),
    )(w2, g2, m2, v2)
    return tuple(o.reshape(shape) for o in outs)


def adamw_summed(w, m, v, partial, received, chip, name):
    shape = w.shape
    rows, cols = shape[-2:]
    w3, m3, v3 = [a.reshape((1, rows, cols)) for a in (w, m, v)]
    tr = _row_tile(rows)

    def body(c_ref, w_ref, m_ref, v_ref, p_ref, r_ref, g_ref, d_ref, nm_ref, nv_ref):
        gv = ((p_ref[0] + r_ref[0]) + r_ref[1]) + r_ref[2]
        nm = ADAM_B1 * m_ref[0] + (1.0 - ADAM_B1) * gv
        nv = ADAM_B2 * v_ref[0] + (1.0 - ADAM_B2) * (gv * gv)
        m_hat = nm / (1.0 - ADAM_B1 ** ADAM_STEP)
        v_hat = nv / (1.0 - ADAM_B2 ** ADAM_STEP)
        g_ref[0] = gv
        d_ref[0] = -ADAM_LR * (m_hat / (jnp.sqrt(v_hat) + ADAM_EPS) + ADAM_WD * w_ref[0])
        nm_ref[0] = nm
        nv_ref[0] = nv

    one = pl.BlockSpec((1, tr, cols), lambda i, c_ref: (0, i, 0))
    out = jax.ShapeDtypeStruct((1, rows, cols), F32)
    outs = pl.pallas_call(
        body,
        grid_spec=pltpu.PrefetchScalarGridSpec(
            num_scalar_prefetch=1, grid=(rows // tr,),
            in_specs=[one, one, one, pl.BlockSpec((1, tr, cols), lambda i, c_ref: (c_ref[0], i, 0)),
                      pl.BlockSpec((3, tr, cols), lambda i, c_ref: (0, i, 0))],
            out_specs=(one,) * 4),
        out_shape=(out,) * 4, name=name, compiler_params=_params(1),
    )(chip, w3, m3, v3, partial, received)
    return tuple(o.reshape(shape) for o in outs)


IN_PIECES = (("wq_a", 0, 768), ("wk_a", 768, 1536), ("wv_a", 1536, 2304), ("w_qkvb", 2304, 5376),
             ("w_small", 5376, 5392), ("w_ggate", 5392, 6416), ("w_gatea", 6416, 7440), ("w_gateb", 7440, 8464))
IN_ROW_TILE = 128


def _piece_width(lo, hi):
    return max(hi - lo, LANES)


def _piece_segments(lo, hi):
    out = []
    for j in range(N_DEV):
        a, b = max(lo, IN_SHARD * j), min(hi, IN_SHARD * (j + 1))
        if a < b:
            out.append((j, a - IN_SHARD * j, b - IN_SHARD * j, a - lo, b - lo))
    return out


def split_in_shards(gathered):
    rows = gathered.shape[1]

    def body(g_ref, *outs):
        for (name, lo, hi), o_ref in zip(IN_PIECES, outs):
            if hi - lo < LANES:
                o_ref[...] = jnp.zeros_like(o_ref)
            for j, s0, s1, d0, d1 in _piece_segments(lo, hi):
                o_ref[:, d0:d1] = g_ref[j, :, s0:s1]

    widths = [_piece_width(lo, hi) for _, lo, hi in IN_PIECES]
    outs = pl.pallas_call(
        body, grid=(rows // IN_ROW_TILE,),
        in_specs=[pl.BlockSpec((N_DEV, IN_ROW_TILE, IN_SHARD), lambda i: (0, i, 0))],
        out_specs=tuple(pl.BlockSpec((IN_ROW_TILE, wd), lambda i: (i, 0)) for wd in widths),
        out_shape=tuple(jax.ShapeDtypeStruct((rows, wd), gathered.dtype) for wd in widths),
        name="split_in_shards", compiler_params=_params(1),
    )(gathered)
    return {name: o for (name, _, _), o in zip(IN_PIECES, outs)}


def join_in_grads(grads):
    pieces = [grads[name] for name, _, _ in IN_PIECES]
    rows = pieces[0].shape[0]

    def body(*refs):
        o_ref = refs[-1]
        for (name, lo, hi), p_ref in zip(IN_PIECES, refs[:-1]):
            for j, s0, s1, d0, d1 in _piece_segments(lo, hi):
                o_ref[j, :, s0:s1] = p_ref[:, d0:d1]

    return pl.pallas_call(
        body, grid=(rows // IN_ROW_TILE,),
        in_specs=[pl.BlockSpec((IN_ROW_TILE, p.shape[1]), lambda i: (i, 0)) for p in pieces],
        out_specs=pl.BlockSpec((N_DEV, IN_ROW_TILE, IN_SHARD), lambda i: (0, i, 0)),
        out_shape=jax.ShapeDtypeStruct((N_DEV, rows, IN_SHARD), F32), name="join_in_grads", compiler_params=_params(1),
    )(*pieces)


SMALL_VECTORS = ("ffn1_norm", "mix_norm", "ffn2_norm", "final_norm")


def _pack_small(gs):
    row = jnp.concatenate([gs["gdn_a_log"].reshape(-1), gs["gdn_dt_bias"].reshape(-1), gs["gdn_out_norm"].reshape(-1)])
    rows = [gs[n].reshape(1, D_MODEL) for n in SMALL_VECTORS]
    rows.append(jnp.pad(row, (0, D_MODEL - row.shape[0])).reshape(1, D_MODEL))
    rows.append(gs["gdn_conv_w"].reshape(-1, D_MODEL))
    packed = jnp.concatenate(rows, axis=0)
    return jnp.pad(packed, ((0, SMALL_ROWS - packed.shape[0]), (0, 0)))


def _unpack_small(packed):
    out = {n: packed[i].reshape(1, D_MODEL) for i, n in enumerate(SMALL_VECTORS)}
    row = packed[len(SMALL_VECTORS)]
    out["gdn_a_log"] = row[:GDN_HEADS].reshape(1, GDN_HEADS)
    out["gdn_dt_bias"] = row[GDN_HEADS:2 * GDN_HEADS].reshape(1, GDN_HEADS)
    out["gdn_out_norm"] = row[2 * GDN_HEADS:2 * GDN_HEADS + GDN_HEAD_DIM].reshape(1, GDN_HEAD_DIM)
    first = len(SMALL_VECTORS) + 1
    out["gdn_conv_w"] = packed[first:first + GDN_CONV * 3].reshape(GDN_CONV, 3 * GDN_WIDTH)
    return out


WEIGHTS = ("ffn1_norm", "ffn1_w_gate", "ffn1_w_up", "ffn1_w_down", "mix_norm", "w_in", "gdn_conv_w", "gdn_a_log",
           "gdn_dt_bias", "gdn_out_norm", "w_branch_a", "w_branch_b", "w_out", "ffn2_norm", "ffn2_w_gate",
           "ffn2_w_up", "ffn2_w_down", "final_norm")


def kernel(x, ffn1_norm, ffn1_w_gate, ffn1_w_up, ffn1_w_down, mix_norm, w_in, gdn_conv_w, gdn_a_log, gdn_dt_bias, gdn_out_norm, w_branch_a, w_branch_b, w_out, ffn2_norm, ffn2_w_gate, ffn2_w_up, ffn2_w_down, final_norm, loss_target, m_ffn1_norm, m_ffn1_w_gate, m_ffn1_w_up, m_ffn1_w_down, m_mix_norm, m_w_in, m_gdn_conv_w, m_gdn_a_log, m_gdn_dt_bias, m_gdn_out_norm, m_w_branch_a, m_w_branch_b, m_w_out, m_ffn2_norm, m_ffn2_w_gate, m_ffn2_w_up, m_ffn2_w_down, m_final_norm, v_ffn1_norm, v_ffn1_w_gate, v_ffn1_w_up, v_ffn1_w_down, v_mix_norm, v_w_in, v_gdn_conv_w, v_gdn_a_log, v_gdn_dt_bias, v_gdn_out_norm, v_w_branch_a, v_w_branch_b, v_w_out, v_ffn2_norm, v_ffn2_w_gate, v_ffn2_w_up, v_ffn2_w_down, v_final_norm):
    given = dict(locals())
    px, py, pc = _position()
    big_names = [n for n, _, _ in SHARDS]

    shards = [given[n][0].astype(BF16) for n in big_names] + [gdn_conv_w[0]]
    gathered = dict(zip(big_names + ["gdn_conv_w"], all_gather_shards(shards, "gather_weights")))
    w = {n: gathered[n] for n in big_names if n.startswith("ffn")}
    w.update(split_in_shards(gathered["w_in"]))
    w["w_branch_a"] = gathered["w_branch_a"].transpose(1, 0, 2).reshape(256, D_MODEL)
    w["w_branch_b"] = gathered["w_branch_b"].reshape(D_MODEL, D_MODEL)
    w["w_out"] = gathered["w_out"].reshape(D_MODEL, D_MODEL)
    conv_full = gathered["gdn_conv_w"].transpose(1, 0, 2).reshape(GDN_CONV, 3 * GDN_WIDTH)
    slots = {n: jnp.zeros(a.shape, F32) for n, a in w.items()}
    small = dict(ffn1_norm=ffn1_norm, mix_norm=mix_norm, ffn2_norm=ffn2_norm, final_norm=final_norm.reshape(1, D_MODEL),
                 gdn_a_log=gdn_a_log, gdn_dt_bias=gdn_dt_bias, gdn_out_norm=gdn_out_norm, gdn_conv_w=conv_full)

    loss_local, (grad_x, g_w, g_small) = jax.value_and_grad(_local_loss)((x[0], slots, small), loss_target[0], w)
    loss = lax.psum(loss_local, ("x", "y", "c"))

    g_big = {n: g_w[n] for n in big_names if n.startswith("ffn")}
    g_big["w_in"] = join_in_grads(g_w)
    g_big["w_branch_a"] = g_w["w_branch_a"].reshape(256, N_DEV, 128).transpose(1, 0, 2)
    g_big["w_branch_b"] = g_w["w_branch_b"].reshape(N_DEV, 128, D_MODEL)
    g_big["w_out"] = g_w["w_out"].reshape(N_DEV, 128, D_MODEL)
    g_list = [g_big[n] for n in big_names]
    core = pc.astype(jnp.int32).reshape(1)
    chip = (2 * px + py).astype(jnp.int32).reshape(1)
    from_sibling = exchange_with_sibling(g_list)
    partials = [add_sibling(g, r, core, "rs_add_" + n) for n, g, r in zip(big_names, g_list, from_sibling)]
    from_chips = exchange_with_chips(partials)

    results = {}
    for n, part, recv in zip(big_names, partials, from_chips):
        results[n] = adamw_summed(given[n], given["m_" + n], given["v_" + n], part, recv, chip, "adamw_" + n)

    small_sum = _unpack_small(all_reduce_small(_pack_small(g_small)))
    conv_cols = CONV_SHARD[1]
    me = 4 * px + 2 * py + pc
    small_sum["gdn_conv_w"] = lax.dynamic_slice(small_sum["gdn_conv_w"], (0, me * conv_cols), (GDN_CONV, conv_cols))
    for n in WEIGHTS:
        if n not in results:
            g = small_sum[n].reshape(given[n].shape)
            results[n] = (g,) + adamw(given[n], g, given["m_" + n], given["v_" + n], "adamw_" + n)

    outs = [[results[n][i] for n in WEIGHTS] for i in range(4)]
    return (loss, grad_x[None], *outs[0], *outs[1], *outs[2], *outs[3])
```

```python
import jax
import jax.numpy as jnp
from jax import lax
from jax.experimental import pallas as pl
from jax.experimental.pallas import tpu as pltpu

F32 = jnp.float32
BF16 = jnp.bfloat16
HI = lax.Precision.HIGHEST
MESH = pl.DeviceIdType.MESH

N_DEV = 8
D_MODEL = 1024
D_FF = 2816
EPS = 1e-6
ROPE_THETA = 10000.0
DSW_DILATIONS = (1, 4, 16)
DSW_HEADS_PER_GROUP = 4
DSW_HEAD_DIM = 64
DSW_BLOCK = 128
DSW_WIDTH = 768
N_DSW_HEADS = 12
GDN_HEADS = 8
GDN_HEAD_DIM = 128
GDN_WIDTH = 1024
GDN_CONV = 4
GDN_CHUNK = 64
IN_OFFSETS = dict(qa=0, ka=768, va=1536, qkvb=2304, small=5376, ggate=5392, gatea=6416, gateb=7440)
D_IN = 8464

ADAM_LR = 0.001
ADAM_B1 = 0.9
ADAM_B2 = 0.999
ADAM_EPS = 1e-08
ADAM_WD = 0.01
ADAM_STEP = 10

VMEM_LIMIT_BYTES = 56 * 1024 * 1024
LANES = 128

NN = (((1,), (0,)), ((), ()))
NT = (((1,), (1,)), ((), ()))
TN = (((0,), (0,)), ((), ()))


def _params(n_grid):
    return pltpu.CompilerParams(dimension_semantics=("arbitrary",) * n_grid, vmem_limit_bytes=VMEM_LIMIT_BYTES)


def _tile(n, pref):
    best = None
    t = LANES
    while t <= min(n, pref):
        if n % t == 0:
            best = t
        t += LANES
    return n if best is None else best


def _matmul(a, b, *, name, ta=False, tb=False, res=None, scale=1.0):
    K, M = a.shape if ta else a.shape[::-1]
    N = b.shape[0] if tb else b.shape[1]
    assert (b.shape[1] if tb else b.shape[0]) == K, (a.shape, b.shape, ta, tb)
    tm = _tile(M, 512)
    tn = _tile(N, 512)
    dn = (((0 if ta else 1,), (1 if tb else 0,)), ((), ()))

    def body(*refs):
        a_ref, b_ref = refs[:2]
        o_ref = refs[-1]
        acc = lax.dot_general(a_ref[...].astype(BF16), b_ref[...].astype(BF16), dn, preferred_element_type=F32)
        if scale != 1.0:
            acc = acc * scale
        if res is not None:
            acc = refs[2][...] + acc
        o_ref[...] = acc

    a_spec = pl.BlockSpec((K, tm), lambda i, j: (0, i)) if ta else pl.BlockSpec((tm, K), lambda i, j: (i, 0))
    b_spec = pl.BlockSpec((tn, K), lambda i, j: (j, 0)) if tb else pl.BlockSpec((K, tn), lambda i, j: (0, j))
    o_spec = pl.BlockSpec((tm, tn), lambda i, j: (i, j))
    ins, specs = [a, b], [a_spec, b_spec]
    if res is not None:
        ins.append(res)
        specs.append(o_spec)
    return pl.pallas_call(
        body, grid=(M // tm, N // tn), in_specs=specs, out_specs=o_spec,
        out_shape=jax.ShapeDtypeStruct((M, N), F32), name=name, compiler_params=_params(2),
    )(*ins)


def _make_mm(name, scale=1.0, with_res=False):
    @jax.custom_vjp
    def op(a, w, slot, res):
        return _matmul(a, w, name=name, res=res if with_res else None, scale=scale)

    def fwd(a, w, slot, res):
        return op(a, w, slot, res), (a, w)

    def bwd(saved, g):
        a, w = saved
        da = _matmul(g, w, name=name + "_da", tb=True, scale=scale)
        dw = _matmul(a, g, name=name + "_dw", ta=True, scale=scale)
        return da, None, dw, (g if with_res else None)

    op.defvjp(fwd, bwd)
    return op


def mm(a, w, slot, name):
    return _make_mm(name)(a, w, slot, None)


def mm_res(a, w, slot, res, scale, name):
    return _make_mm(name, scale=scale, with_res=True)(a, w, slot, res)


def _mm_shards_out(a, w, *, name, tb=False, scale=1.0):
    n_shards = w.shape[0]
    m, k = a.shape
    n = w.shape[1] if tb else w.shape[2]
    tm = _tile(m, 512)
    dn = NT if tb else NN

    def body(a_ref, w_ref, o_ref):
        acc = lax.dot_general(a_ref[...].astype(BF16), w_ref[0].astype(BF16), dn, preferred_element_type=F32)
        o_ref[0] = acc * scale if scale != 1.0 else acc

    return pl.pallas_call(
        body, grid=(m // tm, n_shards),
        in_specs=[pl.BlockSpec((tm, k), lambda i, j: (i, 0)), pl.BlockSpec((1,) + w.shape[1:], lambda i, j: (j, 0, 0))],
        out_specs=pl.BlockSpec((1, tm, n), lambda i, j: (j, i, 0)),
        out_shape=jax.ShapeDtypeStruct((n_shards, m, n), F32), name=name, compiler_params=_params(2),
    )(a, w)


def _mm_shards_sum(a, w, *, name, tb=False, res=None, scale=1.0):
    n_shards, m, n = a.shape
    n_out = w.shape[1] if tb else w.shape[2]
    tm, tn = _tile(m, 512), _tile(n_out, 512)
    dn = NT if tb else NN

    def body(*refs):
        a_ref, w_ref = refs[:2]
        o_ref = refs[-1]
        acc = None
        for j in range(n_shards):
            part = lax.dot_general(a_ref[j].astype(BF16), w_ref[j].astype(BF16), dn, preferred_element_type=F32)
            acc = part if acc is None else acc + part
        if scale != 1.0:
            acc = acc * scale
        if res is not None:
            acc = refs[2][...] + acc
        o_ref[...] = acc

    w_spec = (pl.BlockSpec((n_shards, tn, n), lambda i, j: (0, j, 0)) if tb
              else pl.BlockSpec((n_shards, n, tn), lambda i, j: (0, 0, j)))
    o_spec = pl.BlockSpec((tm, tn), lambda i, j: (i, j))
    ins, specs = [a, w], [pl.BlockSpec((n_shards, tm, n), lambda i, j: (0, i, 0)), w_spec]
    if res is not None:
        ins.append(res)
        specs.append(o_spec)
    return pl.pallas_call(
        body, grid=(m // tm, n_out // tn), in_specs=specs, out_specs=o_spec,
        out_shape=jax.ShapeDtypeStruct((m, n_out), F32), name=name, compiler_params=_params(2),
    )(*ins)


def _mm_dw_shards_out(a, g, *, name):
    n_shards, m, n = g.shape
    k = a.shape[1]

    def body(a_ref, g_ref, o_ref):
        o_ref[0] = lax.dot_general(a_ref[...].astype(BF16), g_ref[0].astype(BF16), TN, preferred_element_type=F32)

    return pl.pallas_call(
        body, grid=(n_shards,),
        in_specs=[pl.BlockSpec((m, k), lambda j: (0, 0)), pl.BlockSpec((1, m, n), lambda j: (j, 0, 0))],
        out_specs=pl.BlockSpec((1, k, n), lambda j: (j, 0, 0)),
        out_shape=jax.ShapeDtypeStruct((n_shards, k, n), F32), name=name, compiler_params=_params(1),
    )(a, g)


def _mm_dw_shards_sum(a, g, *, name, scale=1.0):
    n_shards, m, n = a.shape
    n_out = g.shape[1]

    def body(a_ref, g_ref, o_ref):
        acc = lax.dot_general(a_ref[0].astype(BF16), g_ref[...].astype(BF16), TN, preferred_element_type=F32)
        o_ref[0] = acc * scale if scale != 1.0 else acc

    return pl.pallas_call(
        body, grid=(n_shards,),
        in_specs=[pl.BlockSpec((1, m, n), lambda j: (j, 0, 0)), pl.BlockSpec((m, n_out), lambda j: (0, 0))],
        out_specs=pl.BlockSpec((1, n, n_out), lambda j: (j, 0, 0)),
        out_shape=jax.ShapeDtypeStruct((n_shards, n, n_out), F32), name=name, compiler_params=_params(1),
    )(a, g)


def mm_shards_out(a, w, slot, name):
    @jax.custom_vjp
    def op(a, w, slot):
        return _mm_shards_out(a, w, name=name)

    def fwd(a, w, slot):
        return op(a, w, slot), (a, w)

    def bwd(saved, g):
        a, w = saved
        return _mm_shards_sum(g, w, name=name + "_da", tb=True), None, _mm_dw_shards_out(a, g, name=name + "_dw")

    op.defvjp(fwd, bwd)
    return op(a, w, slot)


def mm_shards_sum(a, w, slot, res, scale, name):
    @jax.custom_vjp
    def op(a, w, slot, res):
        return _mm_shards_sum(a, w, name=name, res=res, scale=scale)

    def fwd(a, w, slot, res):
        return op(a, w, slot, res), (a, w)

    def bwd(saved, g):
        a, w = saved
        da = _mm_shards_out(g, w, name=name + "_da", tb=True, scale=scale)
        return da, None, _mm_dw_shards_sum(a, g, name=name + "_dw", scale=scale), g

    op.defvjp(fwd, bwd)
    return op(a, w, slot, res)


def _rw_specs(arrs, tm, nblk):
    return [pl.BlockSpec((tm, a.shape[1] // nblk), lambda i, j: (i, j)) for a in arrs]


def _rowwise_fwd(fn, name, rows, consts, params, tm, nblk):
    n_rows = rows[0].shape[0]
    tm = min(tm, n_rows)
    ins = list(rows) + list(consts)
    avals = [jax.ShapeDtypeStruct((tm, a.shape[1] // nblk), a.dtype) for a in ins]
    avals += [jax.ShapeDtypeStruct(p.shape, p.dtype) for p in params]
    out_avals = jax.eval_shape(fn, *avals)
    n_in = len(ins) + len(params)

    def body(*refs):
        outs = fn(*[r[...] for r in refs[:n_in]])
        for r, o in zip(refs[n_in:], outs):
            r[...] = o.astype(r.dtype)

    return pl.pallas_call(
        body, grid=(n_rows // tm, nblk),
        in_specs=_rw_specs(ins, tm, nblk) + [pl.BlockSpec(p.shape, lambda i, j: (0, 0)) for p in params],
        out_specs=tuple(pl.BlockSpec((tm, o.shape[1]), lambda i, j: (i, j)) for o in out_avals),
        out_shape=tuple(jax.ShapeDtypeStruct((n_rows, o.shape[1] * nblk), o.dtype) for o in out_avals),
        name=name, compiler_params=_params(2),
    )(*ins, *params)


def _rowwise_bwd(fn, name, rows, consts, params, cts, tm, nblk):
    n_rows = rows[0].shape[0]
    tm = min(tm, n_rows)
    nr, nc, npar, nct = len(rows), len(consts), len(params), len(cts)

    def body(*refs):
        rv = [r[...] for r in refs[:nr]]
        cv = [r[...] for r in refs[nr:nr + nc]]
        pv = [r[...] for r in refs[nr + nc:nr + nc + npar]]
        ctv = [r[...] for r in refs[nr + nc + npar:nr + nc + npar + nct]]
        outs = refs[nr + nc + npar + nct:]
        _, vjp = jax.vjp(lambda *d: fn(*d[:nr], *cv, *d[nr:]), *rv, *pv)
        grads = vjp(tuple(ctv))
        for k in range(nr):
            outs[k][...] = grads[k]
        first = jnp.logical_and(pl.program_id(0) == 0, pl.program_id(1) == 0)
        for k in range(npar):
            ref = outs[nr + k]

            @pl.when(first)
            def _(ref=ref):
                ref[...] = jnp.zeros_like(ref)

            ref[...] += grads[nr + k]

    ins = list(rows) + list(consts)
    return pl.pallas_call(
        body, grid=(n_rows // tm, nblk),
        in_specs=(_rw_specs(ins, tm, nblk) + [pl.BlockSpec(p.shape, lambda i, j: (0, 0)) for p in params]
                  + _rw_specs(cts, tm, nblk)),
        out_specs=tuple(_rw_specs(rows, tm, nblk) + [pl.BlockSpec(p.shape, lambda i, j: (0, 0)) for p in params]),
        out_shape=tuple([jax.ShapeDtypeStruct(a.shape, F32) for a in rows]
                        + [jax.ShapeDtypeStruct(p.shape, F32) for p in params]),
        name=name, compiler_params=_params(2),
    )(*ins, *params, *cts)


def rowwise(fn, name, rows, consts=(), params=(), tm=256, nblk=1):
    rows, consts, params = tuple(rows), tuple(consts), tuple(params)

    @jax.custom_vjp
    def op(rows, consts, params):
        return _rowwise_fwd(fn, name, rows, consts, params, tm, nblk)

    def fwd(rows, consts, params):
        return op(rows, consts, params), (rows, consts, params)

    def bwd(saved, cts):
        rows, consts, params = saved
        grads = _rowwise_bwd(fn, name + "_bwd", rows, consts, params, tuple(cts), tm, nblk)
        return tuple(grads[:len(rows)]), None, tuple(grads[len(rows):])

    op.defvjp(fwd, bwd)
    return op(rows, consts, params)


def _rms_fn(x, gain):
    return (x * lax.rsqrt(jnp.mean(x * x, axis=-1, keepdims=True) + EPS) * gain,)


def _swiglu_fn(g, u):
    return (g * jax.nn.sigmoid(g) * u,)


def _merge_fn(ga, gb, pa, pb):
    return (jax.nn.sigmoid(ga) * pa + jax.nn.sigmoid(gb) * pb,)


def _outnorm_gate_fn(o, gate, gain):
    y = o * lax.rsqrt(jnp.mean(o * o, axis=-1, keepdims=True) + EPS) * gain
    return (y * (gate * jax.nn.sigmoid(gate)),)


def _beta_decay_fn(beta_raw, decay_raw, a_log, dt_bias):
    z = decay_raw + dt_bias
    softplus = jnp.maximum(z, 0.0) + jnp.log(1.0 + jnp.exp(-jnp.abs(z)))
    g = -jnp.exp(a_log) * softplus
    rows = g.shape[0]
    ii = lax.broadcasted_iota(jnp.int32, (rows, rows), 0)
    jj = lax.broadcasted_iota(jnp.int32, (rows, rows), 1)
    same_chunk_before = jnp.logical_and(jj <= ii, jj // GDN_CHUNK == ii // GDN_CHUNK).astype(F32)
    gcum = lax.dot_general(same_chunk_before, g, NN, precision=HI, preferred_element_type=F32)
    return jax.nn.sigmoid(beta_raw), gcum


def _combine_fn(o0, o1, o2, l0, l1, l2):
    m = lax.stop_gradient(jnp.maximum(jnp.maximum(l0, l1), l2))
    e0, e1, e2 = jnp.exp(l0 - m), jnp.exp(l1 - m), jnp.exp(l2 - m)
    return ((e0 * o0 + e1 * o1 + e2 * o2) / (e0 + e1 + e2),)


def _loss_fn(x, target, gain):
    y = x * lax.rsqrt(jnp.mean(x * x, axis=-1, keepdims=True) + EPS) * gain
    err = y - target
    return (0.5 * jnp.mean(err * err, axis=-1, keepdims=True),)


def _rope_call(x, cos, sin, name):
    n_rows, width = x.shape
    tm = 512

    def body(x_ref, c_ref, s_ref, o_ref):
        v = x_ref[...]
        lane = lax.broadcasted_iota(jnp.int32, v.shape, 1)
        low = (lane % DSW_HEAD_DIM) < DSW_HEAD_DIM // 2
        half = DSW_HEAD_DIM // 2
        swapped = jnp.where(low, pltpu.roll(v, LANES - half, 1), pltpu.roll(v, half, 1))
        o_ref[...] = v * c_ref[...] + swapped * s_ref[...]

    tab = pl.BlockSpec((tm, LANES), lambda i, j: (i, 0))
    blk = pl.BlockSpec((tm, LANES), lambda i, j: (i, j))
    return pl.pallas_call(
        body, grid=(n_rows // tm, width // LANES), in_specs=[blk, tab, tab], out_specs=blk,
        out_shape=jax.ShapeDtypeStruct(x.shape, F32), name=name, compiler_params=_params(2),
    )(x, cos, sin)


def rope(x, cos, sin, name):
    @jax.custom_vjp
    def op(x):
        return _rope_call(x, cos, sin, name)

    def fwd(x):
        return op(x), None

    def bwd(_, g):
        return (_rope_call(g, cos, -sin, name + "_bwd"),)

    op.defvjp(fwd, bwd)
    return op(x)


def _rope_tables(n_tokens):
    half = DSW_HEAD_DIM // 2
    inv_freq = ROPE_THETA ** (-jnp.arange(half, dtype=F32) / half)
    ang = jnp.arange(n_tokens, dtype=F32)[:, None] * inv_freq[None, :]
    cos, sin = jnp.cos(ang), jnp.sin(ang)
    return jnp.tile(jnp.concatenate([cos, cos], 1), (1, 2)), jnp.tile(jnp.concatenate([-sin, sin], 1), (1, 2))


def _attn_probs(q, kp, kc, h, n):
    blk = DSW_BLOCK
    k = jnp.concatenate([kp, kc], axis=0).astype(BF16)
    s = lax.dot_general(q.astype(BF16), k, NT, preferred_element_type=F32) * (DSW_HEAD_DIM ** -0.5)
    blocks_per_seq = jnp.where(h < 4, 16, jnp.where(h < 8, 4, 1))
    first = (n % blocks_per_seq) == 0
    qi = lax.broadcasted_iota(jnp.int32, (blk, 2 * blk), 0)
    kj = lax.broadcasted_iota(jnp.int32, (blk, 2 * blk), 1)
    dist = qi + blk - kj
    valid = (dist >= 0) & (dist <= blk) & jnp.logical_or(kj >= blk, jnp.logical_not(first))
    s = jnp.where(valid, s, -1e30)
    m = jnp.max(s, axis=-1, keepdims=True)
    p = jnp.exp(s - m)
    l = jnp.sum(p, axis=-1, keepdims=True)
    return p / l, m + jnp.log(l), k


def _attn_specs(n_tokens):
    blk = DSW_BLOCK
    cur = pl.BlockSpec((1, blk, DSW_HEAD_DIM), lambda h, n: (h, n, 0))
    prev = pl.BlockSpec((1, blk, DSW_HEAD_DIM), lambda h, n: (h, jnp.maximum(n - 1, 0), 0))
    return cur, prev


def _attn_fwd(q, k, v):
    nh, n_tokens, hd = q.shape
    cur, prev = _attn_specs(n_tokens)

    def body(q_ref, kp_ref, kc_ref, vp_ref, vc_ref, o_ref, l_ref):
        h, n = pl.program_id(0), pl.program_id(1)
        p, lse, _ = _attn_probs(q_ref[0], kp_ref[0], kc_ref[0], h, n)
        vv = jnp.concatenate([vp_ref[0], vc_ref[0]], axis=0).astype(BF16)
        o_ref[0] = lax.dot_general(p.astype(BF16), vv, NN, preferred_element_type=F32)
        l_ref[0] = jnp.broadcast_to(lse, (DSW_BLOCK, hd))

    return pl.pallas_call(
        body, grid=(nh, n_tokens // DSW_BLOCK), in_specs=[cur, prev, cur, prev, cur], out_specs=(cur, cur),
        out_shape=(jax.ShapeDtypeStruct(q.shape, F32), jax.ShapeDtypeStruct(q.shape, F32)),
        name="attn_fwd", compiler_params=_params(2),
    )(q, k, k, v, v)


def _attn_bwd(q, k, v, do, dlse):
    nh, n_tokens, hd = q.shape
    nblk = n_tokens // DSW_BLOCK
    cur, prev = _attn_specs(n_tokens)
    part = pl.BlockSpec((1, 1, 2 * DSW_BLOCK, hd), lambda h, n: (h, n, 0, 0))

    def body(q_ref, kp_ref, kc_ref, vp_ref, vc_ref, do_ref, dl_ref, dq_ref, dk_ref, dv_ref):
        h, n = pl.program_id(0), pl.program_id(1)
        qb = q_ref[0].astype(BF16)
        p, _, kb = _attn_probs(q_ref[0], kp_ref[0], kc_ref[0], h, n)
        vv = jnp.concatenate([vp_ref[0], vc_ref[0]], axis=0).astype(BF16)
        dob = do_ref[0].astype(BF16)
        dp = lax.dot_general(dob, vv, NT, preferred_element_type=F32)
        dv_ref[0, 0] = lax.dot_general(p.astype(BF16), dob, TN, preferred_element_type=F32)
        dlse = jnp.sum(dl_ref[0], axis=-1, keepdims=True)
        ds = p * (dp - jnp.sum(dp * p, axis=-1, keepdims=True) + dlse) * (DSW_HEAD_DIM ** -0.5)
        dsb = ds.astype(BF16)
        dq_ref[0] = lax.dot_general(dsb, kb, NN, preferred_element_type=F32)
        dk_ref[0, 0] = lax.dot_general(dsb, qb, TN, preferred_element_type=F32)

    dq, dkp, dvp = pl.pallas_call(
        body, grid=(nh, nblk), in_specs=[cur, prev, cur, prev, cur, cur, cur], out_specs=(cur, part, part),
        out_shape=(jax.ShapeDtypeStruct(q.shape, F32),
                   jax.ShapeDtypeStruct((nh, nblk, 2 * DSW_BLOCK, hd), F32),
                   jax.ShapeDtypeStruct((nh, nblk, 2 * DSW_BLOCK, hd), F32)),
        name="attn_bwd", compiler_params=_params(2),
    )(q, k, k, v, v, do, dlse)

    def fold(partial):
        own = partial[:, :, DSW_BLOCK:]
        from_next = jnp.pad(partial[:, 1:, :DSW_BLOCK], ((0, 0), (0, 1), (0, 0), (0, 0)))
        return (own + from_next).reshape(nh, n_tokens, hd)

    return dq, fold(dkp), fold(dvp)


@jax.custom_vjp
def attention(q, k, v):
    return _attn_fwd(q, k, v)


def _attention_fwd(q, k, v):
    return _attn_fwd(q, k, v), (q, k, v)


def _attention_bwd(saved, cts):
    q, k, v = saved
    return _attn_bwd(q, k, v, cts[0], cts[1])


attention.defvjp(_attention_fwd, _attention_bwd)


def _to_heads(a):
    n_tokens = a.shape[0]
    outs = []
    for gi, d in enumerate(DSW_DILATIONS):
        blk = a[:, gi * 256:(gi + 1) * 256].reshape(n_tokens // d, d, DSW_HEADS_PER_GROUP, DSW_HEAD_DIM)
        outs.append(blk.transpose(2, 1, 0, 3).reshape(DSW_HEADS_PER_GROUP, n_tokens, DSW_HEAD_DIM))
    return jnp.concatenate(outs, 0)


def _from_heads(a):
    n_tokens = a.shape[1]
    outs = []
    for gi, d in enumerate(DSW_DILATIONS):
        blk = a[gi * 4:(gi + 1) * 4].reshape(DSW_HEADS_PER_GROUP, d, n_tokens // d, DSW_HEAD_DIM)
        outs.append(blk.transpose(2, 1, 0, 3).reshape(n_tokens, DSW_HEADS_PER_GROUP * DSW_HEAD_DIM))
    return outs


CONV_TILE = 512


def _shift_down(x, k, rows):
    return x if k == 0 else jnp.where(rows >= k, pltpu.roll(x, k, 0), 0.0)


def _shift_up(x, k, rows):
    n = x.shape[0]
    return x if k == 0 else jnp.where(rows < n - k, pltpu.roll(x, n - k, 0), 0.0)


def _conv_pre(x, w):
    rows = lax.broadcasted_iota(jnp.int32, x.shape, 0)
    acc = x * w[GDN_CONV - 1:GDN_CONV]
    for k in range(1, GDN_CONV):
        acc = acc + _shift_down(x, k, rows) * w[GDN_CONV - 1 - k:GDN_CONV - k]
    return acc, rows


def _conv_fwd(x, w):
    n_tokens, width = x.shape
    big = pl.BlockSpec((n_tokens, CONV_TILE), lambda j: (0, j))
    wsp = pl.BlockSpec((GDN_CONV, CONV_TILE), lambda j: (0, j))

    def body(x_ref, w_ref, o_ref):
        acc, _ = _conv_pre(x_ref[...], w_ref[...])
        o_ref[...] = acc * jax.nn.sigmoid(acc)

    return pl.pallas_call(
        body, grid=(width // CONV_TILE,), in_specs=[big, wsp], out_specs=big,
        out_shape=jax.ShapeDtypeStruct(x.shape, F32), name="conv_fwd", compiler_params=_params(1),
    )(x, w)


def _conv_bwd(x, w, dy):
    n_tokens, width = x.shape
    big = pl.BlockSpec((n_tokens, CONV_TILE), lambda j: (0, j))
    wsp = pl.BlockSpec((GDN_CONV, CONV_TILE), lambda j: (0, j))

    def body(x_ref, w_ref, dy_ref, dx_ref, dw_ref):
        xv, wv = x_ref[...], w_ref[...]
        acc, rows = _conv_pre(xv, wv)
        sg = jax.nn.sigmoid(acc)
        dacc = dy_ref[...] * (sg + acc * sg * (1.0 - sg))
        dx = dacc * wv[GDN_CONV - 1:GDN_CONV]
        for k in range(1, GDN_CONV):
            dx = dx + _shift_up(dacc, k, rows) * wv[GDN_CONV - 1 - k:GDN_CONV - k]
        dx_ref[...] = dx
        for k in range(GDN_CONV):
            dw_ref[GDN_CONV - 1 - k:GDN_CONV - k, :] = jnp.sum(dacc * _shift_down(xv, k, rows), axis=0, keepdims=True)

    return pl.pallas_call(
        body, grid=(width // CONV_TILE,), in_specs=[big, wsp, big], out_specs=(big, wsp),
        out_shape=(jax.ShapeDtypeStruct(x.shape, F32), jax.ShapeDtypeStruct(w.shape, F32)),
        name="conv_bwd", compiler_params=_params(1),
    )(x, w, dy)


@jax.custom_vjp
def conv_silu(x, w):
    return _conv_fwd(x, w)


def _conv_silu_fwd(x, w):
    return _conv_fwd(x, w), (x, w)


def _conv_silu_bwd(saved, g):
    return _conv_bwd(saved[0], saved[1], g)


conv_silu.defvjp(_conv_silu_fwd, _conv_silu_bwd)


def _dot(a, b, dn=NN):
    return lax.dot_general(a, b, dn, precision=HI, preferred_element_type=F32)


def _dot3(a, b, dn=NN):
    return lax.dot_general(a, b, dn, precision=lax.Precision.HIGH, preferred_element_type=F32)


def _bf16_dot(a, b, dn):
    return lax.dot_general(a.astype(BF16), b.astype(BF16), dn, preferred_element_type=F32)


_DOT_GRADS = {NN: (("g", "b", NT), ("a", "g", TN)), NT: (("g", "b", NN), ("g", "a", TN)),
              TN: (("b", "g", NT), ("a", "g", NN))}


def _make_bdot(dn):
    @jax.custom_vjp
    def op(a, b):
        return _bf16_dot(a, b, dn)

    def fwd(a, b):
        return op(a, b), (a, b)

    def bwd(saved, g):
        vals = dict(a=saved[0], b=saved[1], g=g)
        return tuple(_bf16_dot(vals[x], vals[y], form) for x, y, form in _DOT_GRADS[dn])

    op.defvjp(fwd, bwd)
    return op


_BDOTS = {dn: _make_bdot(dn) for dn in (NN, NT, TN)}


def _bdot(a, b, dn=NN):
    return _BDOTS[dn](a, b)


def _each(fn, *lists):
    return [fn(*items) for items in zip(*lists)]


def _gdn_chunks(q, k, v, b, gcum, state):
    c = GDN_CHUNK
    ii = lax.broadcasted_iota(jnp.int32, (c, c), 0)
    jj = lax.broadcasted_iota(jnp.int32, (c, c), 1)
    eye = (ii == jj).astype(F32)
    qn = _each(lambda x: x * lax.rsqrt(jnp.sum(x * x, axis=-1, keepdims=True) + EPS) * (GDN_HEAD_DIM ** -0.5), q)
    kn = _each(lambda x: x * lax.rsqrt(jnp.sum(x * x, axis=-1, keepdims=True) + EPS), k)
    gcum_i = _each(lambda x: jnp.broadcast_to(x, (c, c)), gcum)
    gcum_j = _each(jnp.transpose, gcum_i)
    decay = _each(lambda x, y: jnp.exp(jnp.where(jj <= ii, x - y, -1e30)), gcum_i, gcum_j)
    g_last = _each(lambda x: x[c - 1:c, :], gcum)
    e_gcum = _each(jnp.exp, gcum)
    kbeta = _each(lambda x, y: x * y, kn, b)
    vbeta = _each(lambda x, y: x * y, v, b)
    m = _each(lambda x, y, d: jnp.where(jj < ii, _bdot(x, y, NT) * d, 0.0), kbeta, kn, decay)
    inv = _each(lambda x: eye - x, m)
    power = _each(lambda x: _dot3(x, x), m)
    for step in range(5):
        inv = _each(lambda x, p: x + _dot3(x, p), inv, power)
        if step < 4:
            power = _each(lambda p: _dot3(p, p), power)
    u = _each(_dot3, inv, vbeta)
    w = _each(lambda x, y, e: _dot3(x, y * e), inv, kbeta, e_gcum)
    a_qk = _each(lambda x, y, d: _bdot(x, y, NT) * d, qn, kn, decay)
    v_new = _each(lambda x, y, s: x - _bdot(y, s), u, w, state)
    o = _each(lambda x, e, s, a, vn: _bdot(x * e, s) + _bdot(a, vn), qn, e_gcum, state, a_qk, v_new)
    new_state = _each(lambda s, gl, x, gc, vn: s * jnp.exp(gl) + _bdot(x * jnp.exp(gl - gc), vn, TN),
                      state, g_last, kn, gcum, v_new)
    return o, new_state


GDN_HEADS_PER_STEP = 4


GDN_TIME_TILE = 512


def _gdn_specs(n_tokens, reverse):
    hb, hd, tt = GDN_HEADS_PER_STEP, GDN_HEAD_DIM, GDN_TIME_TILE
    nb, nt = GDN_HEADS // hb, n_tokens // tt

    def when(t):
        return nt - 1 - t if reverse else t

    q = pl.BlockSpec((tt, hb * hd), lambda h, t: (when(t), h))
    k = pl.BlockSpec((tt, hb * hd), lambda h, t: (when(t), nb + h))
    v = pl.BlockSpec((tt, hb * hd), lambda h, t: (when(t), 2 * nb + h))
    vec = pl.BlockSpec((hb, tt, 1), lambda h, t: (h, when(t), 0))
    states = pl.BlockSpec((hb, tt // GDN_CHUNK, hd, hd), lambda h, t: (h, when(t), 0, 0))
    return q, k, v, vec, states


def _gdn_fwd(qkv, beta, g):
    n_tokens = qkv.shape[0]
    hb, hd, tt = GDN_HEADS_PER_STEP, GDN_HEAD_DIM, GDN_TIME_TILE
    n_chunks = tt // GDN_CHUNK
    q_s, k_s, v_s, vec, st = _gdn_specs(n_tokens, False)

    def body(q_ref, k_ref, v_ref, b_ref, g_ref, o_ref, st_ref, state):
        @pl.when(pl.program_id(1) == 0)
        def _():
            state[...] = jnp.zeros_like(state)

        def step(c, carry):
            r = pl.ds(pl.multiple_of(c * GDN_CHUNK, GDN_CHUNK), GDN_CHUNK)
            cols = [slice(h * hd, (h + 1) * hd) for h in range(hb)]
            old = [state[h] for h in range(hb)]
            o, new = _gdn_chunks(
                [q_ref[r, cs] for cs in cols], [k_ref[r, cs] for cs in cols], [v_ref[r, cs] for cs in cols],
                [b_ref[h, r, :] for h in range(hb)], [g_ref[h, r, :] for h in range(hb)], old)
            for h in range(hb):
                st_ref[h, c] = old[h]
                o_ref[r, cols[h]] = o[h]
                state[h] = new[h]
            return carry

        lax.fori_loop(0, n_chunks, step, 0)

    return pl.pallas_call(
        body, grid=(GDN_HEADS // hb, n_tokens // tt), in_specs=[q_s, k_s, v_s, vec, vec], out_specs=(q_s, st),
        out_shape=(jax.ShapeDtypeStruct((n_tokens, GDN_WIDTH), F32),
                   jax.ShapeDtypeStruct((GDN_HEADS, n_tokens // GDN_CHUNK, hd, hd), F32)),
        scratch_shapes=[pltpu.VMEM((hb, hd, hd), F32)],
        name="gdn_fwd", compiler_params=_params(2),
    )(qkv, qkv, qkv, beta, g)


def _gdn_bwd(qkv, beta, g, states, do):
    n_tokens = qkv.shape[0]
    hb, hd, tt = GDN_HEADS_PER_STEP, GDN_HEAD_DIM, GDN_TIME_TILE
    n_chunks = tt // GDN_CHUNK
    q_s, k_s, v_s, vec, st = _gdn_specs(n_tokens, True)

    def body(q_ref, k_ref, v_ref, b_ref, g_ref, st_ref, do_ref, dq_ref, dk_ref, dv_ref, db_ref, dg_ref, dstate):
        @pl.when(pl.program_id(1) == 0)
        def _():
            dstate[...] = jnp.zeros_like(dstate)

        def step(i, carry):
            c = n_chunks - 1 - i
            r = pl.ds(pl.multiple_of(c * GDN_CHUNK, GDN_CHUNK), GDN_CHUNK)
            cols = [slice(h * hd, (h + 1) * hd) for h in range(hb)]
            args = ([q_ref[r, cs] for cs in cols], [k_ref[r, cs] for cs in cols], [v_ref[r, cs] for cs in cols],
                    [b_ref[h, r, :] for h in range(hb)], [g_ref[h, r, :] for h in range(hb)],
                    [st_ref[h, c] for h in range(hb)])
            cts = ([do_ref[r, cs] for cs in cols], [dstate[h] for h in range(hb)])
            dq, dk, dv, db, dg, dst = jax.vjp(_gdn_chunks, *args)[1](cts)
            for h in range(hb):
                dq_ref[r, cols[h]] = dq[h]
                dk_ref[r, cols[h]] = dk[h]
                dv_ref[r, cols[h]] = dv[h]
                db_ref[h, r, :] = db[h]
                dg_ref[h, r, :] = dg[h]
                dstate[h] = dst[h]
            return carry

        lax.fori_loop(0, n_chunks, step, 0)

    wide = jax.ShapeDtypeStruct((n_tokens, GDN_WIDTH), F32)
    thin = jax.ShapeDtypeStruct(beta.shape, F32)
    dq, dk, dv, db, dg = pl.pallas_call(
        body, grid=(GDN_HEADS // hb, n_tokens // tt), in_specs=[q_s, k_s, v_s, vec, vec, st, q_s],
        out_specs=(q_s, q_s, q_s, vec, vec), out_shape=(wide, wide, wide, thin, thin),
        scratch_shapes=[pltpu.VMEM((hb, hd, hd), F32)],
        name="gdn_bwd", compiler_params=_params(2),
    )(qkv, qkv, qkv, beta, g, states, do)
    return jnp.concatenate([dq, dk, dv], axis=1), db, dg


@jax.custom_vjp
def gated_delta(qkv, beta, g):
    return _gdn_fwd(qkv, beta, g)[0]


def _gated_delta_fwd(qkv, beta, g):
    o, states = _gdn_fwd(qkv, beta, g)
    return o, (qkv, beta, g, states)


def _gated_delta_bwd(saved, do):
    return _gdn_bwd(*saved, do)


gated_delta.defvjp(_gated_delta_fwd, _gated_delta_bwd)


def _ffn(x, gain, w, slots, tag):
    h = rowwise(_rms_fn, tag + "_norm", (x,), params=(gain,))[0]
    g = mm_shards_out(h, w[tag + "_w_gate"], slots[tag + "_w_gate"], tag + "_gate")
    u = mm_shards_out(h, w[tag + "_w_up"], slots[tag + "_w_up"], tag + "_up")
    flat = (g.shape[0] * g.shape[1], g.shape[2])
    a = rowwise(_swiglu_fn, tag + "_act", (g.reshape(flat), u.reshape(flat)), tm=512)[0].reshape(g.shape)
    return mm_shards_sum(a, w[tag + "_w_down"], slots[tag + "_w_down"], x, 0.5, tag + "_down")


def _local_loss(diff, x_target, w):
    x, slots, small = diff
    target = x_target
    n_tokens = x.shape[0]
    x1 = _ffn(x, small["ffn1_norm"], w, slots, "ffn1")

    h = rowwise(_rms_fn, "mix_norm", (x1,), params=(small["mix_norm"],))[0]
    proj = {n: mm(h, w[n], slots[n], "in_" + n)
            for n in ("wq_a", "wk_a", "wv_a", "w_qkvb", "w_small", "w_ggate", "w_gatea", "w_gateb")}

    cos, sin = _rope_tables(n_tokens)
    q = _to_heads(rope(proj["wq_a"], cos, sin, "rope_q"))
    k = _to_heads(rope(proj["wk_a"], cos, sin, "rope_k"))
    v = _to_heads(proj["wv_a"])
    o, lse = attention(q, k, v)
    ya = rowwise(_combine_fn, "combine", tuple(_from_heads(o)) + tuple(_from_heads(lse)), tm=512)[0]
    pa = mm(ya, w["w_branch_a"], slots["w_branch_a"], "branch_a")

    qkv = conv_silu(proj["w_qkvb"], small["gdn_conv_w"])
    beta, g = rowwise(_beta_decay_fn, "beta_decay",
                      (proj["w_small"][:, :GDN_HEADS], proj["w_small"][:, GDN_HEADS:2 * GDN_HEADS]),
                      params=(small["gdn_a_log"], small["gdn_dt_bias"]), tm=512)
    ob = gated_delta(qkv, beta.T[:, :, None], g.T[:, :, None])
    yb = rowwise(_outnorm_gate_fn, "outnorm_gate", (ob, proj["w_ggate"]), params=(small["gdn_out_norm"],),
                 tm=512, nblk=GDN_HEADS)[0]
    pb = mm(yb, w["w_branch_b"], slots["w_branch_b"], "branch_b")

    merged = rowwise(_merge_fn, "merge", (proj["w_gatea"], proj["w_gateb"], pa, pb))[0]
    x2 = mm_res(merged, w["w_out"], slots["w_out"], x1, 1.0, "out")

    x3 = _ffn(x2, small["ffn2_norm"], w, slots, "ffn2")
    row_loss = rowwise(_loss_fn, "loss", (x3,), consts=(target,), params=(small["final_norm"],))[0]
    return jnp.sum(row_loss)


SHARDS = (
    ("ffn1_w_gate", 1024, 352), ("ffn1_w_up", 1024, 352), ("ffn1_w_down", 352, 1024), ("w_in", 1024, 1058),
    ("w_branch_a", 256, 128), ("w_branch_b", 128, 1024), ("w_out", 128, 1024),
    ("ffn2_w_gate", 1024, 352), ("ffn2_w_up", 1024, 352), ("ffn2_w_down", 352, 1024),
)
IN_SHARD = D_IN // N_DEV
CONV_SHARD = (GDN_CONV, 3 * GDN_WIDTH // N_DEV)
SMALL_ROWS = 24
ANY = pl.BlockSpec(memory_space=pl.ANY)


def _position():
    return lax.axis_index("x"), lax.axis_index("y"), lax.axis_index("c")


def all_gather_shards(shards, name):
    n = len(shards)

    def body(*refs):
        x_refs, out_refs = refs[:n], refs[n:2 * n]
        send_sems, recv_sems, local_sems = refs[2 * n:]
        x, y, c = _position()
        me, sibling = (x, y, c), (x, y, 1 - c)
        chips = [(1 - x, y), (x, 1 - y), (1 - x, 1 - y)]

        def slab(a, px, py, pc):
            return out_refs[a].at[4 * px + 2 * py + pc]

        def copy(a, k, block, to, src=None):
            return pltpu.make_async_remote_copy(
                src_ref=slab(a, *block) if src is None else src, dst_ref=slab(a, *block),
                send_sem=send_sems.at[7 * a + k], recv_sem=recv_sems.at[7 * a + k], device_id=to, device_id_type=MESH)

        mine = [pltpu.make_async_copy(x_refs[a], slab(a, *me), local_sems.at[a]) for a in range(n)]
        for cp in mine:
            cp.start()
        first = []
        for j, chip in enumerate(chips):
            first += [copy(a, 1 + j, me, (*chip, c), src=x_refs[a]) for a in range(n)]
        first += [copy(a, 0, me, sibling, src=x_refs[a]) for a in range(n)]
        for cp in first:
            cp.start()
        passed = []
        for j, chip in enumerate(chips):
            for a in range(n):
                copy(a, 1 + j, (*chip, c), me).wait_recv()
                cp = copy(a, 4 + j, (*chip, c), sibling)
                cp.start()
                passed.append(cp)
        for a in range(n):
            copy(a, 0, sibling, me).wait_recv()
        for j, chip in enumerate(chips):
            for a in range(n):
                copy(a, 4 + j, (*chip, 1 - c), me).wait_recv()
        for cp in first + passed:
            cp.wait_send()
        for cp in mine:
            cp.wait()

    return pl.pallas_call(
        body, out_shape=tuple(jax.ShapeDtypeStruct((N_DEV,) + s.shape, s.dtype) for s in shards),
        in_specs=[ANY] * n, out_specs=(ANY,) * n,
        scratch_shapes=[pltpu.SemaphoreType.DMA((7 * n,)), pltpu.SemaphoreType.DMA((7 * n,)),
                        pltpu.SemaphoreType.DMA((n,))],
        name=name,
    )(*shards)


def exchange_with_sibling(grads):
    n = len(grads)

    def body(*refs):
        g_refs, recv_refs = refs[:n], refs[n:2 * n]
        send_sems, recv_sems = refs[2 * n:]
        x, y, c = _position()
        copies = [pltpu.make_async_remote_copy(
            src_ref=g_refs[a].at[2 * k + 1 - c], dst_ref=recv_refs[a].at[k], send_sem=send_sems.at[4 * a + k],
            recv_sem=recv_sems.at[4 * a + k], device_id=(x, y, 1 - c), device_id_type=MESH)
            for k in range(4) for a in range(n)]
        for cp in copies:
            cp.start()
        for cp in copies:
            cp.wait()

    return pl.pallas_call(
        body, out_shape=tuple(jax.ShapeDtypeStruct((4,) + g.shape[1:], g.dtype) for g in grads),
        in_specs=[ANY] * n, out_specs=(ANY,) * n,
        scratch_shapes=[pltpu.SemaphoreType.DMA((4 * n,)), pltpu.SemaphoreType.DMA((4 * n,))], name="rs_sibling",
    )(*grads)


def _row_tile(rows):
    return 256 if rows % 256 == 0 else rows


def add_sibling(grads, received, core, name):
    _, rows, width = grads.shape
    tr = _row_tile(rows)

    def body(c_ref, g_ref, r_ref, o_ref):
        o_ref[...] = g_ref[...] + r_ref[...]

    blk = (1, tr, width)
    return pl.pallas_call(
        body,
        grid_spec=pltpu.PrefetchScalarGridSpec(
            num_scalar_prefetch=1, grid=(4, rows // tr),
            in_specs=[pl.BlockSpec(blk, lambda k, i, c_ref: (2 * k + c_ref[0], i, 0)),
                      pl.BlockSpec(blk, lambda k, i, c_ref: (k, i, 0))],
            out_specs=pl.BlockSpec(blk, lambda k, i, c_ref: (k, i, 0))),
        out_shape=jax.ShapeDtypeStruct((4, rows, width), F32), name=name, compiler_params=_params(2),
    )(core, grads, received)


def exchange_with_chips(partials):
    n = len(partials)

    def body(*refs):
        p_refs, recv_refs = refs[:n], refs[n:2 * n]
        send_sems, recv_sems = refs[2 * n:]
        x, y, c = _position()
        chips = [(1 - x, y), (x, 1 - y), (1 - x, 1 - y)]
        copies = [pltpu.make_async_remote_copy(
            src_ref=p_refs[a].at[2 * cx + cy], dst_ref=recv_refs[a].at[j], send_sem=send_sems.at[3 * a + j],
            recv_sem=recv_sems.at[3 * a + j], device_id=(cx, cy, c), device_id_type=MESH)
            for a in range(n) for j, (cx, cy) in enumerate(chips)]
        for cp in copies:
            cp.start()
        for cp in copies:
            cp.wait()

    return pl.pallas_call(
        body, out_shape=tuple(jax.ShapeDtypeStruct((3,) + p.shape[1:], p.dtype) for p in partials),
        in_specs=[ANY] * n, out_specs=(ANY,) * n,
        scratch_shapes=[pltpu.SemaphoreType.DMA((3 * n,)), pltpu.SemaphoreType.DMA((3 * n,))], name="rs_chips",
    )(*partials)


def all_reduce_small(vals):
    rows, width = vals.shape

    def body(x_ref, out_ref, all_ref, send_sems, recv_sems):
        x, y, c = _position()
        me, sibling = (x, y, c), (x, y, 1 - c)
        chips = [(1 - x, y), (x, 1 - y), (1 - x, 1 - y)]

        def slab(px, py, pc):
            return all_ref.at[4 * px + 2 * py + pc]

        def copy(k, block, to, src=None):
            return pltpu.make_async_remote_copy(
                src_ref=slab(*block) if src is None else src, dst_ref=slab(*block),
                send_sem=send_sems.at[k], recv_sem=recv_sems.at[k], device_id=to, device_id_type=MESH)

        first = [copy(0, me, sibling, src=x_ref)]
        first += [copy(1 + j, me, (*chip, c), src=x_ref) for j, chip in enumerate(chips)]
        for cp in first:
            cp.start()
        all_ref[4 * x + 2 * y + c] = x_ref[...]
        passed = [copy(4 + j, (*chip, c), sibling) for j, chip in enumerate(chips)]
        for j, chip in enumerate(chips):
            copy(1 + j, (*chip, c), me).wait_recv()
            passed[j].start()
        copy(0, sibling, me).wait_recv()
        for j, chip in enumerate(chips):
            copy(4 + j, (*chip, 1 - c), me).wait_recv()
        for cp in first + passed:
            cp.wait_send()
        total = all_ref[0]
        for d in range(1, N_DEV):
            total = total + all_ref[d]
        out_ref[...] = total

    vmem = pl.BlockSpec(memory_space=pltpu.VMEM)
    return pl.pallas_call(
        body, out_shape=(jax.ShapeDtypeStruct(vals.shape, F32), jax.ShapeDtypeStruct((N_DEV, rows, width), F32)),
        in_specs=[vmem], out_specs=(vmem, vmem),
        scratch_shapes=[pltpu.SemaphoreType.DMA((7,)), pltpu.SemaphoreType.DMA((7,))], name="small_allreduce",
    )(vals)[0]


def adamw(w, g, m, v, name):
    shape = w.shape
    w2, g2, m2, v2 = [a.reshape((-1, shape[-1])) for a in (w, g, m, v)]
    rows, cols = w2.shape
    tr = 256 if rows % 256 == 0 else rows

    def body(w_ref, g_ref, m_ref, v_ref, d_ref, nm_ref, nv_ref):
        gv = g_ref[...]
        nm = ADAM_B1 * m_ref[...] + (1.0 - ADAM_B1) * gv
        nv = ADAM_B2 * v_ref[...] + (1.0 - ADAM_B2) * (gv * gv)
        m_hat = nm / (1.0 - ADAM_B1 ** ADAM_STEP)
        v_hat = nv / (1.0 - ADAM_B2 ** ADAM_STEP)
        d_ref[...] = -ADAM_LR * (m_hat / (jnp.sqrt(v_hat) + ADAM_EPS) + ADAM_WD * w_ref[...])
        nm_ref[...] = nm
        nv_ref[...] = nv

    blk = pl.BlockSpec((tr, cols), lambda i: (i, 0))
    out = jax.ShapeDtypeStruct((rows, cols), F32)
    outs = pl.pallas_call(
        body, grid=(rows // tr,), in_specs=[blk] * 4, out_specs=(blk,) * 3, out_shape=(out,) * 3,
        name=name, compiler_params=_params(1),
    )(w2, g2, m2, v2)
    return tuple(o.reshape(shape) for o in outs)


def adamw_summed(w, m, v, partial, received, chip, name):
    shape = w.shape
    rows, cols = shape[-2:]
    w3, m3, v3 = [a.reshape((1, rows, cols)) for a in (w, m, v)]
    tr = _row_tile(rows)

    def body(c_ref, w_ref, m_ref, v_ref, p_ref, r_ref, g_ref, d_ref, nm_ref, nv_ref):
        gv = ((p_ref[0] + r_ref[0]) + r_ref[1]) + r_ref[2]
        nm = ADAM_B1 * m_ref[0] + (1.0 - ADAM_B1) * gv
        nv = ADAM_B2 * v_ref[0] + (1.0 - ADAM_B2) * (gv * gv)
        m_hat = nm / (1.0 - ADAM_B1 ** ADAM_STEP)
        v_hat = nv / (1.0 - ADAM_B2 ** ADAM_STEP)
        g_ref[0] = gv
        d_ref[0] = -ADAM_LR * (m_hat / (jnp.sqrt(v_hat) + ADAM_EPS) + ADAM_WD * w_ref[0])
        nm_ref[0] = nm
        nv_ref[0] = nv

    one = pl.BlockSpec((1, tr, cols), lambda i, c_ref: (0, i, 0))
    out = jax.ShapeDtypeStruct((1, rows, cols), F32)
    outs = pl.pallas_call(
        body,
        grid_spec=pltpu.PrefetchScalarGridSpec(
            num_scalar_prefetch=1, grid=(rows // tr,),
            in_specs=[one, one, one, pl.BlockSpec((1, tr, cols), lambda i, c_ref: (c_ref[0], i, 0)),
                      pl.BlockSpec((3, tr, cols), lambda i, c_ref: (0, i, 0))],
            out_specs=(one,) * 4),
        out_shape=(out,) * 4, name=name, compiler_params=_params(1),
    )(chip, w3, m3, v3, partial, received)
    return tuple(o.reshape(shape) for o in outs)


IN_PIECES = (("wq_a", 0, 768), ("wk_a", 768, 1536), ("wv_a", 1536, 2304), ("w_qkvb", 2304, 5376),
             ("w_small", 5376, 5392), ("w_ggate", 5392, 6416), ("w_gatea", 6416, 7440), ("w_gateb", 7440, 8464))
IN_ROW_TILE = 128


def _piece_width(lo, hi):
    return max(hi - lo, LANES)


def _piece_segments(lo, hi):
    out = []
    for j in range(N_DEV):
        a, b = max(lo, IN_SHARD * j), min(hi, IN_SHARD * (j + 1))
        if a < b:
            out.append((j, a - IN_SHARD * j, b - IN_SHARD * j, a - lo, b - lo))
    return out


def split_in_shards(gathered):
    rows = gathered.shape[1]

    def body(g_ref, *outs):
        for (name, lo, hi), o_ref in zip(IN_PIECES, outs):
            if hi - lo < LANES:
                o_ref[...] = jnp.zeros_like(o_ref)
            for j, s0, s1, d0, d1 in _piece_segments(lo, hi):
                o_ref[:, d0:d1] = g_ref[j, :, s0:s1]

    widths = [_piece_width(lo, hi) for _, lo, hi in IN_PIECES]
    outs = pl.pallas_call(
        body, grid=(rows // IN_ROW_TILE,),
        in_specs=[pl.BlockSpec((N_DEV, IN_ROW_TILE, IN_SHARD), lambda i: (0, i, 0))],
        out_specs=tuple(pl.BlockSpec((IN_ROW_TILE, wd), lambda i: (i, 0)) for wd in widths),
        out_shape=tuple(jax.ShapeDtypeStruct((rows, wd), gathered.dtype) for wd in widths),
        name="split_in_shards", compiler_params=_params(1),
    )(gathered)
    return {name: o for (name, _, _), o in zip(IN_PIECES, outs)}


def join_in_grads(grads):
    pieces = [grads[name] for name, _, _ in IN_PIECES]
    rows = pieces[0].shape[0]

    def body(*refs):
        o_ref = refs[-1]
        for (name, lo, hi), p_ref in zip(IN_PIECES, refs[:-1]):
            for j, s0, s1, d0, d1 in _piece_segments(lo, hi):
                o_ref[j, :, s0:s1] = p_ref[:, d0:d1]

    return pl.pallas_call(
        body, grid=(rows // IN_ROW_TILE,),
        in_specs=[pl.BlockSpec((IN_ROW_TILE, p.shape[1]), lambda i: (i, 0)) for p in pieces],
        out_specs=pl.BlockSpec((N_DEV, IN_ROW_TILE, IN_SHARD), lambda i: (0, i, 0)),
        out_shape=jax.ShapeDtypeStruct((N_DEV, rows, IN_SHARD), F32), name="join_in_grads", compiler_params=_params(1),
    )(*pieces)


SMALL_VECTORS = ("ffn1_norm", "mix_norm", "ffn2_norm", "final_norm")


def _pack_small(gs):
    row = jnp.concatenate([gs["gdn_a_log"].reshape(-1), gs["gdn_dt_bias"].reshape(-1), gs["gdn_out_norm"].reshape(-1)])
    rows = [gs[n].reshape(1, D_MODEL) for n in SMALL_VECTORS]
    rows.append(jnp.pad(row, (0, D_MODEL - row.shape[0])).reshape(1, D_MODEL))
    rows.append(gs["gdn_conv_w"].reshape(-1, D_MODEL))
    packed = jnp.concatenate(rows, axis=0)
    return jnp.pad(packed, ((0, SMALL_ROWS - packed.shape[0]), (0, 0)))


def _unpack_small(packed):
    out = {n: packed[i].reshape(1, D_MODEL) for i, n in enumerate(SMALL_VECTORS)}
    row = packed[len(SMALL_VECTORS)]
    out["gdn_a_log"] = row[:GDN_HEADS].reshape(1, GDN_HEADS)
    out["gdn_dt_bias"] = row[GDN_HEADS:2 * GDN_HEADS].reshape(1, GDN_HEADS)
    out["gdn_out_norm"] = row[2 * GDN_HEADS:2 * GDN_HEADS + GDN_HEAD_DIM].reshape(1, GDN_HEAD_DIM)
    first = len(SMALL_VECTORS) + 1
    out["gdn_conv_w"] = packed[first:first + GDN_CONV * 3].reshape(GDN_CONV, 3 * GDN_WIDTH)
    return out


WEIGHTS = ("ffn1_norm", "ffn1_w_gate", "ffn1_w_up", "ffn1_w_down", "mix_norm", "w_in", "gdn_conv_w", "gdn_a_log",
           "gdn_dt_bias", "gdn_out_norm", "w_branch_a", "w_branch_b", "w_out", "ffn2_norm", "ffn2_w_gate",
           "ffn2_w_up", "ffn2_w_down", "final_norm")


def kernel(x, ffn1_norm, ffn1_w_gate, ffn1_w_up, ffn1_w_down, mix_norm, w_in, gdn_conv_w, gdn_a_log, gdn_dt_bias, gdn_out_norm, w_branch_a, w_branch_b, w_out, ffn2_norm, ffn2_w_gate, ffn2_w_up, ffn2_w_down, final_norm, loss_target, m_ffn1_norm, m_ffn1_w_gate, m_ffn1_w_up, m_ffn1_w_down, m_mix_norm, m_w_in, m_gdn_conv_w, m_gdn_a_log, m_gdn_dt_bias, m_gdn_out_norm, m_w_branch_a, m_w_branch_b, m_w_out, m_ffn2_norm, m_ffn2_w_gate, m_ffn2_w_up, m_ffn2_w_down, m_final_norm, v_ffn1_norm, v_ffn1_w_gate, v_ffn1_w_up, v_ffn1_w_down, v_mix_norm, v_w_in, v_gdn_conv_w, v_gdn_a_log, v_gdn_dt_bias, v_gdn_out_norm, v_w_branch_a, v_w_branch_b, v_w_out, v_ffn2_norm, v_ffn2_w_gate, v_ffn2_w_up, v_ffn2_w_down, v_final_norm):
    given = dict(locals())
    px, py, pc = _position()
    big_names = [n for n, _, _ in SHARDS]

    shards = [given[n][0].astype(BF16) for n in big_names] + [gdn_conv_w[0]]
    gathered = dict(zip(big_names + ["gdn_conv_w"], all_gather_shards(shards, "gather_weights")))
    w = {n: gathered[n] for n in big_names if n.startswith("ffn")}
    w.update(split_in_shards(gathered["w_in"]))
    w["w_branch_a"] = gathered["w_branch_a"].transpose(1, 0, 2).reshape(256, D_MODEL)
    w["w_branch_b"] = gathered["w_branch_b"].reshape(D_MODEL, D_MODEL)
    w["w_out"] = gathered["w_out"].reshape(D_MODEL, D_MODEL)
    conv_full = gathered["gdn_conv_w"].transpose(1, 0, 2).reshape(GDN_CONV, 3 * GDN_WIDTH)
    slots = {n: jnp.zeros(a.shape, F32) for n, a in w.items()}
    small = dict(ffn1_norm=ffn1_norm, mix_norm=mix_norm, ffn2_norm=ffn2_norm, final_norm=final_norm.reshape(1, D_MODEL),
                 gdn_a_log=gdn_a_log, gdn_dt_bias=gdn_dt_bias, gdn_out_norm=gdn_out_norm, gdn_conv_w=conv_full)

    loss_local, (grad_x, g_w, g_small) = jax.value_and_grad(_local_loss)((x[0], slots, small), loss_target[0], w)
    loss = lax.psum(loss_local, ("x", "y", "c"))

    g_big = {n: g_w[n] for n in big_names if n.startswith("ffn")}
    g_big["w_in"] = join_in_grads(g_w)
    g_big["w_branch_a"] = g_w["w_branch_a"].reshape(256, N_DEV, 128).transpose(1, 0, 2)
    g_big["w_branch_b"] = g_w["w_branch_b"].reshape(N_DEV, 128, D_MODEL)
    g_big["w_out"] = g_w["w_out"].reshape(N_DEV, 128, D_MODEL)
    g_list = [g_big[n] for n in big_names]
    core = pc.astype(jnp.int32).reshape(1)
    chip = (2 * px + py).astype(jnp.int32).reshape(1)
    from_sibling = exchange_with_sibling(g_list)
    partials = [add_sibling(g, r, core, "rs_add_" + n) for n, g, r in zip(big_names, g_list, from_sibling)]
    from_chips = exchange_with_chips(partials)

    results = {}
    for n, part, recv in zip(big_names, partials, from_chips):
        results[n] = adamw_summed(given[n], given["m_" + n], given["v_" + n], part, recv, chip, "adamw_" + n)

    small_sum = _unpack_small(all_reduce_small(_pack_small(g_small)))
    conv_cols = CONV_SHARD[1]
    me = 4 * px + 2 * py + pc
    small_sum["gdn_conv_w"] = lax.dynamic_slice(small_sum["gdn_conv_w"], (0, me * conv_cols), (GDN_CONV, conv_cols))
    for n in WEIGHTS:
        if n not in results:
            g = small_sum[n].reshape(given[n].shape)
            results[n] = (g,) + adamw(given[n], g, given["m_" + n], given["v_" + n], "adamw_" + n)

    outs = [[results[n][i] for n in WEIGHTS] for i in range(4)]
    return (loss, grad_x[None], *outs[0], *outs[1], *outs[2], *outs[3])
```

```python
import jax
import jax.numpy as jnp
from jax import lax
from jax.experimental import pallas as pl
from jax.experimental.pallas import tpu as pltpu

F32 = jnp.float32
BF16 = jnp.bfloat16
HI = lax.Precision.HIGHEST
MESH = pl.DeviceIdType.MESH

N_DEV = 8
D_MODEL = 1024
D_FF = 2816
EPS = 1e-6
ROPE_THETA = 10000.0
DSW_DILATIONS = (1, 4, 16)
DSW_HEADS_PER_GROUP = 4
DSW_HEAD_DIM = 64
DSW_BLOCK = 128
DSW_WIDTH = 768
N_DSW_HEADS = 12
GDN_HEADS = 8
GDN_HEAD_DIM = 128
GDN_WIDTH = 1024
GDN_CONV = 4
GDN_CHUNK = 64
IN_OFFSETS = dict(qa=0, ka=768, va=1536, qkvb=2304, small=5376, ggate=5392, gatea=6416, gateb=7440)
D_IN = 8464

ADAM_LR = 0.001
ADAM_B1 = 0.9
ADAM_B2 = 0.999
ADAM_EPS = 1e-08
ADAM_WD = 0.01
ADAM_STEP = 10

VMEM_LIMIT_BYTES = 56 * 1024 * 1024
LANES = 128

NN = (((1,), (0,)), ((), ()))
NT = (((1,), (1,)), ((), ()))
TN = (((0,), (0,)), ((), ()))


def _params(n_grid):
    return pltpu.CompilerParams(dimension_semantics=("arbitrary",) * n_grid, vmem_limit_bytes=VMEM_LIMIT_BYTES)


def _tile(n, pref):
    best = None
    t = LANES
    while t <= min(n, pref):
        if n % t == 0:
            best = t
        t += LANES
    return n if best is None else best


def _matmul(a, b, *, name, ta=False, tb=False, res=None, scale=1.0):
    K, M = a.shape if ta else a.shape[::-1]
    N = b.shape[0] if tb else b.shape[1]
    assert (b.shape[1] if tb else b.shape[0]) == K, (a.shape, b.shape, ta, tb)
    tm = _tile(M, 512)
    tn = _tile(N, 512)
    dn = (((0 if ta else 1,), (1 if tb else 0,)), ((), ()))

    def body(*refs):
        a_ref, b_ref = refs[:2]
        o_ref = refs[-1]
        acc = lax.dot_general(a_ref[...].astype(BF16), b_ref[...].astype(BF16), dn, preferred_element_type=F32)
        if scale != 1.0:
            acc = acc * scale
        if res is not None:
            acc = refs[2][...] + acc
        o_ref[...] = acc

    a_spec = pl.BlockSpec((K, tm), lambda i, j: (0, i)) if ta else pl.BlockSpec((tm, K), lambda i, j: (i, 0))
    b_spec = pl.BlockSpec((tn, K), lambda i, j: (j, 0)) if tb else pl.BlockSpec((K, tn), lambda i, j: (0, j))
    o_spec = pl.BlockSpec((tm, tn), lambda i, j: (i, j))
    ins, specs = [a, b], [a_spec, b_spec]
    if res is not None:
        ins.append(res)
        specs.append(o_spec)
    return pl.pallas_call(
        body, grid=(M // tm, N // tn), in_specs=specs, out_specs=o_spec,
        out_shape=jax.ShapeDtypeStruct((M, N), F32), name=name, compiler_params=_params(2),
    )(*ins)


def _make_mm(name, scale=1.0, with_res=False):
    @jax.custom_vjp
    def op(a, w, slot, res):
        return _matmul(a, w, name=name, res=res if with_res else None, scale=scale)

    def fwd(a, w, slot, res):
        return op(a, w, slot, res), (a, w)

    def bwd(saved, g):
        a, w = saved
        da = _matmul(g, w, name=name + "_da", tb=True, scale=scale)
        dw = _matmul(a, g, name=name + "_dw", ta=True, scale=scale)
        return da, None, dw, (g if with_res else None)

    op.defvjp(fwd, bwd)
    return op


def mm(a, w, slot, name):
    return _make_mm(name)(a, w, slot, None)


def mm_res(a, w, slot, res, scale, name):
    return _make_mm(name, scale=scale, with_res=True)(a, w, slot, res)


def _mm_shards_out(a, w, *, name, tb=False, scale=1.0):
    n_shards = w.shape[0]
    m, k = a.shape
    n = w.shape[1] if tb else w.shape[2]
    tm = _tile(m, 512)
    dn = NT if tb else NN

    def body(a_ref, w_ref, o_ref):
        acc = lax.dot_general(a_ref[...].astype(BF16), w_ref[0].astype(BF16), dn, preferred_element_type=F32)
        o_ref[0] = acc * scale if scale != 1.0 else acc

    return pl.pallas_call(
        body, grid=(m // tm, n_shards),
        in_specs=[pl.BlockSpec((tm, k), lambda i, j: (i, 0)), pl.BlockSpec((1,) + w.shape[1:], lambda i, j: (j, 0, 0))],
        out_specs=pl.BlockSpec((1, tm, n), lambda i, j: (j, i, 0)),
        out_shape=jax.ShapeDtypeStruct((n_shards, m, n), F32), name=name, compiler_params=_params(2),
    )(a, w)


def _mm_shards_sum(a, w, *, name, tb=False, res=None, scale=1.0):
    n_shards, m, n = a.shape
    n_out = w.shape[1] if tb else w.shape[2]
    tm, tn = _tile(m, 512), _tile(n_out, 512)
    dn = NT if tb else NN

    def body(*refs):
        a_ref, w_ref = refs[:2]
        o_ref = refs[-1]
        acc = None
        for j in range(n_shards):
            part = lax.dot_general(a_ref[j].astype(BF16), w_ref[j].astype(BF16), dn, preferred_element_type=F32)
            acc = part if acc is None else acc + part
        if scale != 1.0:
            acc = acc * scale
        if res is not None:
            acc = refs[2][...] + acc
        o_ref[...] = acc

    w_spec = (pl.BlockSpec((n_shards, tn, n), lambda i, j: (0, j, 0)) if tb
              else pl.BlockSpec((n_shards, n, tn), lambda i, j: (0, 0, j)))
    o_spec = pl.BlockSpec((tm, tn), lambda i, j: (i, j))
    ins, specs = [a, w], [pl.BlockSpec((n_shards, tm, n), lambda i, j: (0, i, 0)), w_spec]
    if res is not None:
        ins.append(res)
        specs.append(o_spec)
    return pl.pallas_call(
        body, grid=(m // tm, n_out // tn), in_specs=specs, out_specs=o_spec,
        out_shape=jax.ShapeDtypeStruct((m, n_out), F32), name=name, compiler_params=_params(2),
    )(*ins)


def _mm_dw_shards_out(a, g, *, name):
    n_shards, m, n = g.shape
    k = a.shape[1]

    def body(a_ref, g_ref, o_ref):
        o_ref[0] = lax.dot_general(a_ref[...].astype(BF16), g_ref[0].astype(BF16), TN, preferred_element_type=F32)

    return pl.pallas_call(
        body, grid=(n_shards,),
        in_specs=[pl.BlockSpec((m, k), lambda j: (0, 0)), pl.BlockSpec((1, m, n), lambda j: (j, 0, 0))],
        out_specs=pl.BlockSpec((1, k, n), lambda j: (j, 0, 0)),
        out_shape=jax.ShapeDtypeStruct((n_shards, k, n), F32), name=name, compiler_params=_params(1),
    )(a, g)


def _mm_dw_shards_sum(a, g, *, name, scale=1.0):
    n_shards, m, n = a.shape
    n_out = g.shape[1]

    def body(a_ref, g_ref, o_ref):
        acc = lax.dot_general(a_ref[0].astype(BF16), g_ref[...].astype(BF16), TN, preferred_element_type=F32)
        o_ref[0] = acc * scale if scale != 1.0 else acc

    return pl.pallas_call(
        body, grid=(n_shards,),
        in_specs=[pl.BlockSpec((1, m, n), lambda j: (j, 0, 0)), pl.BlockSpec((m, n_out), lambda j: (0, 0))],
        out_specs=pl.BlockSpec((1, n, n_out), lambda j: (j, 0, 0)),
        out_shape=jax.ShapeDtypeStruct((n_shards, n, n_out), F32), name=name, compiler_params=_params(1),
    )(a, g)


def mm_shards_out(a, w, slot, name):
    @jax.custom_vjp
    def op(a, w, slot):
        return _mm_shards_out(a, w, name=name)

    def fwd(a, w, slot):
        return op(a, w, slot), (a, w)

    def bwd(saved, g):
        a, w = saved
        return _mm_shards_sum(g, w, name=name + "_da", tb=True), None, _mm_dw_shards_out(a, g, name=name + "_dw")

    op.defvjp(fwd, bwd)
    return op(a, w, slot)


def mm_shards_sum(a, w, slot, res, scale, name):
    @jax.custom_vjp
    def op(a, w, slot, res):
        return _mm_shards_sum(a, w, name=name, res=res, scale=scale)

    def fwd(a, w, slot, res):
        return op(a, w, slot, res), (a, w)

    def bwd(saved, g):
        a, w = saved
        da = _mm_shards_out(g, w, name=name + "_da", tb=True, scale=scale)
        return da, None, _mm_dw_shards_sum(a, g, name=name + "_dw", scale=scale), g

    op.defvjp(fwd, bwd)
    return op(a, w, slot, res)


def _rw_specs(arrs, tm, nblk):
    return [pl.BlockSpec((tm, a.shape[1] // nblk), lambda i, j: (i, j)) for a in arrs]


def _rowwise_fwd(fn, name, rows, consts, params, tm, nblk):
    n_rows = rows[0].shape[0]
    tm = min(tm, n_rows)
    ins = list(rows) + list(consts)
    avals = [jax.ShapeDtypeStruct((tm, a.shape[1] // nblk), a.dtype) for a in ins]
    avals += [jax.ShapeDtypeStruct(p.shape, p.dtype) for p in params]
    out_avals = jax.eval_shape(fn, *avals)
    n_in = len(ins) + len(params)

    def body(*refs):
        outs = fn(*[r[...] for r in refs[:n_in]])
        for r, o in zip(refs[n_in:], outs):
            r[...] = o.astype(r.dtype)

    return pl.pallas_call(
        body, grid=(n_rows // tm, nblk),
        in_specs=_rw_specs(ins, tm, nblk) + [pl.BlockSpec(p.shape, lambda i, j: (0, 0)) for p in params],
        out_specs=tuple(pl.BlockSpec((tm, o.shape[1]), lambda i, j: (i, j)) for o in out_avals),
        out_shape=tuple(jax.ShapeDtypeStruct((n_rows, o.shape[1] * nblk), o.dtype) for o in out_avals),
        name=name, compiler_params=_params(2),
    )(*ins, *params)


def _rowwise_bwd(fn, name, rows, consts, params, cts, tm, nblk):
    n_rows = rows[0].shape[0]
    tm = min(tm, n_rows)
    nr, nc, npar, nct = len(rows), len(consts), len(params), len(cts)

    def body(*refs):
        rv = [r[...] for r in refs[:nr]]
        cv = [r[...] for r in refs[nr:nr + nc]]
        pv = [r[...] for r in refs[nr + nc:nr + nc + npar]]
        ctv = [r[...] for r in refs[nr + nc + npar:nr + nc + npar + nct]]
        outs = refs[nr + nc + npar + nct:]
        _, vjp = jax.vjp(lambda *d: fn(*d[:nr], *cv, *d[nr:]), *rv, *pv)
        grads = vjp(tuple(ctv))
        for k in range(nr):
            outs[k][...] = grads[k]
        first = jnp.logical_and(pl.program_id(0) == 0, pl.program_id(1) == 0)
        for k in range(npar):
            ref = outs[nr + k]

            @pl.when(first)
            def _(ref=ref):
                ref[...] = jnp.zeros_like(ref)

            ref[...] += grads[nr + k]

    ins = list(rows) + list(consts)
    return pl.pallas_call(
        body, grid=(n_rows // tm, nblk),
        in_specs=(_rw_specs(ins, tm, nblk) + [pl.BlockSpec(p.shape, lambda i, j: (0, 0)) for p in params]
                  + _rw_specs(cts, tm, nblk)),
        out_specs=tuple(_rw_specs(rows, tm, nblk) + [pl.BlockSpec(p.shape, lambda i, j: (0, 0)) for p in params]),
        out_shape=tuple([jax.ShapeDtypeStruct(a.shape, F32) for a in rows]
                        + [jax.ShapeDtypeStruct(p.shape, F32) for p in params]),
        name=name, compiler_params=_params(2),
    )(*ins, *params, *cts)


def rowwise(fn, name, rows, consts=(), params=(), tm=256, nblk=1):
    rows, consts, params = tuple(rows), tuple(consts), tuple(params)

    @jax.custom_vjp
    def op(rows, consts, params):
        return _rowwise_fwd(fn, name, rows, consts, params, tm, nblk)

    def fwd(rows, consts, params):
        return op(rows, consts, params), (rows, consts, params)

    def bwd(saved, cts):
        rows, consts, params = saved
        grads = _rowwise_bwd(fn, name + "_bwd", rows, consts, params, tuple(cts), tm, nblk)
        return tuple(grads[:len(rows)]), None, tuple(grads[len(rows):])

    op.defvjp(fwd, bwd)
    return op(rows, consts, params)


def _rms_fn(x, gain):
    return (x * lax.rsqrt(jnp.mean(x * x, axis=-1, keepdims=True) + EPS) * gain,)


def _swiglu_fn(g, u):
    return (g * jax.nn.sigmoid(g) * u,)


def _merge_fn(ga, gb, pa, pb):
    return (jax.nn.sigmoid(ga) * pa + jax.nn.sigmoid(gb) * pb,)


def _outnorm_gate_fn(o, gate, gain):
    y = o * lax.rsqrt(jnp.mean(o * o, axis=-1, keepdims=True) + EPS) * gain
    return (y * (gate * jax.nn.sigmoid(gate)),)


def _beta_decay_fn(beta_raw, decay_raw, a_log, dt_bias):
    z = decay_raw + dt_bias
    softplus = jnp.maximum(z, 0.0) + jnp.log(1.0 + jnp.exp(-jnp.abs(z)))
    g = -jnp.exp(a_log) * softplus
    rows = g.shape[0]
    ii = lax.broadcasted_iota(jnp.int32, (rows, rows), 0)
    jj = lax.broadcasted_iota(jnp.int32, (rows, rows), 1)
    same_chunk_before = jnp.logical_and(jj <= ii, jj // GDN_CHUNK == ii // GDN_CHUNK).astype(F32)
    gcum = lax.dot_general(same_chunk_before, g, NN, precision=HI, preferred_element_type=F32)
    return jax.nn.sigmoid(beta_raw), gcum


def _combine_fn(o0, o1, o2, l0, l1, l2):
    m = lax.stop_gradient(jnp.maximum(jnp.maximum(l0, l1), l2))
    e0, e1, e2 = jnp.exp(l0 - m), jnp.exp(l1 - m), jnp.exp(l2 - m)
    return ((e0 * o0 + e1 * o1 + e2 * o2) / (e0 + e1 + e2),)


def _loss_fn(x, target, gain):
    y = x * lax.rsqrt(jnp.mean(x * x, axis=-1, keepdims=True) + EPS) * gain
    err = y - target
    return (0.5 * jnp.mean(err * err, axis=-1, keepdims=True),)


def _rope_call(x, cos, sin, name):
    n_rows, width = x.shape
    tm = 512

    def body(x_ref, c_ref, s_ref, o_ref):
        v = x_ref[...]
        lane = lax.broadcasted_iota(jnp.int32, v.shape, 1)
        low = (lane % DSW_HEAD_DIM) < DSW_HEAD_DIM // 2
        half = DSW_HEAD_DIM // 2
        swapped = jnp.where(low, pltpu.roll(v, LANES - half, 1), pltpu.roll(v, half, 1))
        o_ref[...] = v * c_ref[...] + swapped * s_ref[...]

    tab = pl.BlockSpec((tm, LANES), lambda i, j: (i, 0))
    blk = pl.BlockSpec((tm, LANES), lambda i, j: (i, j))
    return pl.pallas_call(
        body, grid=(n_rows // tm, width // LANES), in_specs=[blk, tab, tab], out_specs=blk,
        out_shape=jax.ShapeDtypeStruct(x.shape, F32), name=name, compiler_params=_params(2),
    )(x, cos, sin)


def rope(x, cos, sin, name):
    @jax.custom_vjp
    def op(x):
        return _rope_call(x, cos, sin, name)

    def fwd(x):
        return op(x), None

    def bwd(_, g):
        return (_rope_call(g, cos, -sin, name + "_bwd"),)

    op.defvjp(fwd, bwd)
    return op(x)


def _rope_tables(n_tokens):
    half = DSW_HEAD_DIM // 2
    inv_freq = ROPE_THETA ** (-jnp.arange(half, dtype=F32) / half)
    ang = jnp.arange(n_tokens, dtype=F32)[:, None] * inv_freq[None, :]
    cos, sin = jnp.cos(ang), jnp.sin(ang)
    return jnp.tile(jnp.concatenate([cos, cos], 1), (1, 2)), jnp.tile(jnp.concatenate([-sin, sin], 1), (1, 2))


def _attn_probs(q, kp, kc, h, n):
    blk = DSW_BLOCK
    k = jnp.concatenate([kp, kc], axis=0).astype(BF16)
    s = lax.dot_general(q.astype(BF16), k, NT, preferred_element_type=F32) * (DSW_HEAD_DIM ** -0.5)
    blocks_per_seq = jnp.where(h < 4, 16, jnp.where(h < 8, 4, 1))
    first = (n % blocks_per_seq) == 0
    qi = lax.broadcasted_iota(jnp.int32, (blk, 2 * blk), 0)
    kj = lax.broadcasted_iota(jnp.int32, (blk, 2 * blk), 1)
    dist = qi + blk - kj
    valid = (dist >= 0) & (dist <= blk) & jnp.logical_or(kj >= blk, jnp.logical_not(first))
    s = jnp.where(valid, s, -1e30)
    m = jnp.max(s, axis=-1, keepdims=True)
    p = jnp.exp(s - m)
    l = jnp.sum(p, axis=-1, keepdims=True)
    return p / l, m + jnp.log(l), k


def _attn_specs(n_tokens):
    blk = DSW_BLOCK
    cur = pl.BlockSpec((1, blk, DSW_HEAD_DIM), lambda h, n: (h, n, 0))
    prev = pl.BlockSpec((1, blk, DSW_HEAD_DIM), lambda h, n: (h, jnp.maximum(n - 1, 0), 0))
    return cur, prev


def _attn_fwd(q, k, v):
    nh, n_tokens, hd = q.shape
    cur, prev = _attn_specs(n_tokens)

    def body(q_ref, kp_ref, kc_ref, vp_ref, vc_ref, o_ref, l_ref):
        h, n = pl.program_id(0), pl.program_id(1)
        p, lse, _ = _attn_probs(q_ref[0], kp_ref[0], kc_ref[0], h, n)
        vv = jnp.concatenate([vp_ref[0], vc_ref[0]], axis=0).astype(BF16)
        o_ref[0] = lax.dot_general(p.astype(BF16), vv, NN, preferred_element_type=F32)
        l_ref[0] = jnp.broadcast_to(lse, (DSW_BLOCK, hd))

    return pl.pallas_call(
        body, grid=(nh, n_tokens // DSW_BLOCK), in_specs=[cur, prev, cur, prev, cur], out_specs=(cur, cur),
        out_shape=(jax.ShapeDtypeStruct(q.shape, F32), jax.ShapeDtypeStruct(q.shape, F32)),
        name="attn_fwd", compiler_params=_params(2),
    )(q, k, k, v, v)


def _attn_bwd(q, k, v, do, dlse):
    nh, n_tokens, hd = q.shape
    nblk = n_tokens // DSW_BLOCK
    cur, prev = _attn_specs(n_tokens)
    part = pl.BlockSpec((1, 1, 2 * DSW_BLOCK, hd), lambda h, n: (h, n, 0, 0))

    def body(q_ref, kp_ref, kc_ref, vp_ref, vc_ref, do_ref, dl_ref, dq_ref, dk_ref, dv_ref):
        h, n = pl.program_id(0), pl.program_id(1)
        qb = q_ref[0].astype(BF16)
        p, _, kb = _attn_probs(q_ref[0], kp_ref[0], kc_ref[0], h, n)
        vv = jnp.concatenate([vp_ref[0], vc_ref[0]], axis=0).astype(BF16)
        dob = do_ref[0].astype(BF16)
        dp = lax.dot_general(dob, vv, NT, preferred_element_type=F32)
        dv_ref[0, 0] = lax.dot_general(p.astype(BF16), dob, TN, preferred_element_type=F32)
        dlse = jnp.sum(dl_ref[0], axis=-1, keepdims=True)
        ds = p * (dp - jnp.sum(dp * p, axis=-1, keepdims=True) + dlse) * (DSW_HEAD_DIM ** -0.5)
        dsb = ds.astype(BF16)
        dq_ref[0] = lax.dot_general(dsb, kb, NN, preferred_element_type=F32)
        dk_ref[0, 0] = lax.dot_general(dsb, qb, TN, preferred_element_type=F32)

    dq, dkp, dvp = pl.pallas_call(
        body, grid=(nh, nblk), in_specs=[cur, prev, cur, prev, cur, cur, cur], out_specs=(cur, part, part),
        out_shape=(jax.ShapeDtypeStruct(q.shape, F32),
                   jax.ShapeDtypeStruct((nh, nblk, 2 * DSW_BLOCK, hd), F32),
                   jax.ShapeDtypeStruct((nh, nblk, 2 * DSW_BLOCK, hd), F32)),
        name="attn_bwd", compiler_params=_params(2),
    )(q, k, k, v, v, do, dlse)

    def fold(partial):
        own = partial[:, :, DSW_BLOCK:]
        from_next = jnp.pad(partial[:, 1:, :DSW_BLOCK], ((0, 0), (0, 1), (0, 0), (0, 0)))
        return (own + from_next).reshape(nh, n_tokens, hd)

    return dq, fold(dkp), fold(dvp)


@jax.custom_vjp
def attention(q, k, v):
    return _attn_fwd(q, k, v)


def _attention_fwd(q, k, v):
    return _attn_fwd(q, k, v), (q, k, v)


def _attention_bwd(saved, cts):
    q, k, v = saved
    return _attn_bwd(q, k, v, cts[0], cts[1])


attention.defvjp(_attention_fwd, _attention_bwd)


def _to_heads(a):
    n_tokens = a.shape[0]
    outs = []
    for gi, d in enumerate(DSW_DILATIONS):
        blk = a[:, gi * 256:(gi + 1) * 256].reshape(n_tokens // d, d, DSW_HEADS_PER_GROUP, DSW_HEAD_DIM)
        outs.append(blk.transpose(2, 1, 0, 3).reshape(DSW_HEADS_PER_GROUP, n_tokens, DSW_HEAD_DIM))
    return jnp.concatenate(outs, 0)


def _from_heads(a):
    n_tokens = a.shape[1]
    outs = []
    for gi, d in enumerate(DSW_DILATIONS):
        blk = a[gi * 4:(gi + 1) * 4].reshape(DSW_HEADS_PER_GROUP, d, n_tokens // d, DSW_HEAD_DIM)
        outs.append(blk.transpose(2, 1, 0, 3).reshape(n_tokens, DSW_HEADS_PER_GROUP * DSW_HEAD_DIM))
    return outs


CONV_TILE = 512


def _shift_down(x, k, rows):
    return x if k == 0 else jnp.where(rows >= k, pltpu.roll(x, k, 0), 0.0)


def _shift_up(x, k, rows):
    n = x.shape[0]
    return x if k == 0 else jnp.where(rows < n - k, pltpu.roll(x, n - k, 0), 0.0)


def _conv_pre(x, w):
    rows = lax.broadcasted_iota(jnp.int32, x.shape, 0)
    acc = x * w[GDN_CONV - 1:GDN_CONV]
    for k in range(1, GDN_CONV):
        acc = acc + _shift_down(x, k, rows) * w[GDN_CONV - 1 - k:GDN_CONV - k]
    return acc, rows


def _conv_fwd(x, w):
    n_tokens, width = x.shape
    big = pl.BlockSpec((n_tokens, CONV_TILE), lambda j: (0, j))
    wsp = pl.BlockSpec((GDN_CONV, CONV_TILE), lambda j: (0, j))

    def body(x_ref, w_ref, o_ref):
        acc, _ = _conv_pre(x_ref[...], w_ref[...])
        o_ref[...] = acc * jax.nn.sigmoid(acc)

    return pl.pallas_call(
        body, grid=(width // CONV_TILE,), in_specs=[big, wsp], out_specs=big,
        out_shape=jax.ShapeDtypeStruct(x.shape, F32), name="conv_fwd", compiler_params=_params(1),
    )(x, w)


def _conv_bwd(x, w, dy):
    n_tokens, width = x.shape
    big = pl.BlockSpec((n_tokens, CONV_TILE), lambda j: (0, j))
    wsp = pl.BlockSpec((GDN_CONV, CONV_TILE), lambda j: (0, j))

    def body(x_ref, w_ref, dy_ref, dx_ref, dw_ref):
        xv, wv = x_ref[...], w_ref[...]
        acc, rows = _conv_pre(xv, wv)
        sg = jax.nn.sigmoid(acc)
        dacc = dy_ref[...] * (sg + acc * sg * (1.0 - sg))
        dx = dacc * wv[GDN_CONV - 1:GDN_CONV]
        for k in range(1, GDN_CONV):
            dx = dx + _shift_up(dacc, k, rows) * wv[GDN_CONV - 1 - k:GDN_CONV - k]
        dx_ref[...] = dx
        for k in range(GDN_CONV):
            dw_ref[GDN_CONV - 1 - k:GDN_CONV - k, :] = jnp.sum(dacc * _shift_down(xv, k, rows), axis=0, keepdims=True)

    return pl.pallas_call(
        body, grid=(width // CONV_TILE,), in_specs=[big, wsp, big], out_specs=(big, wsp),
        out_shape=(jax.ShapeDtypeStruct(x.shape, F32), jax.ShapeDtypeStruct(w.shape, F32)),
        name="conv_bwd", compiler_params=_params(1),
    )(x, w, dy)


@jax.custom_vjp
def conv_silu(x, w):
    return _conv_fwd(x, w)


def _conv_silu_fwd(x, w):
    return _conv_fwd(x, w), (x, w)


def _conv_silu_bwd(saved, g):
    return _conv_bwd(saved[0], saved[1], g)


conv_silu.defvjp(_conv_silu_fwd, _conv_silu_bwd)


def _dot(a, b, dn=NN):
    return lax.dot_general(a, b, dn, precision=HI, preferred_element_type=F32)


def _dot3(a, b, dn=NN):
    return lax.dot_general(a, b, dn, precision=lax.Precision.HIGH, preferred_element_type=F32)


def _bf16_dot(a, b, dn):
    return lax.dot_general(a.astype(BF16), b.astype(BF16), dn, preferred_element_type=F32)


_DOT_GRADS = {NN: (("g", "b", NT), ("a", "g", TN)), NT: (("g", "b", NN), ("g", "a", TN)),
              TN: (("b", "g", NT), ("a", "g", NN))}


def _make_bdot(dn):
    @jax.custom_vjp
    def op(a, b):
        return _bf16_dot(a, b, dn)

    def fwd(a, b):
        return op(a, b), (a, b)

    def bwd(saved, g):
        vals = dict(a=saved[0], b=saved[1], g=g)
        return tuple(_bf16_dot(vals[x], vals[y], form) for x, y, form in _DOT_GRADS[dn])

    op.defvjp(fwd, bwd)
    return op


_BDOTS = {dn: _make_bdot(dn) for dn in (NN, NT, TN)}


def _bdot(a, b, dn=NN):
    return _BDOTS[dn](a, b)


def _each(fn, *lists):
    return [fn(*items) for items in zip(*lists)]


def _gdn_chunks(q, k, v, b, gcum, state):
    c = GDN_CHUNK
    ii = lax.broadcasted_iota(jnp.int32, (c, c), 0)
    jj = lax.broadcasted_iota(jnp.int32, (c, c), 1)
    eye = (ii == jj).astype(F32)
    qn = _each(lambda x: x * lax.rsqrt(jnp.sum(x * x, axis=-1, keepdims=True) + EPS) * (GDN_HEAD_DIM ** -0.5), q)
    kn = _each(lambda x: x * lax.rsqrt(jnp.sum(x * x, axis=-1, keepdims=True) + EPS), k)
    gcum_i = _each(lambda x: jnp.broadcast_to(x, (c, c)), gcum)
    gcum_j = _each(jnp.transpose, gcum_i)
    decay = _each(lambda x, y: jnp.exp(jnp.where(jj <= ii, x - y, -1e30)), gcum_i, gcum_j)
    g_last = _each(lambda x: x[c - 1:c, :], gcum)
    e_gcum = _each(jnp.exp, gcum)
    kbeta = _each(lambda x, y: x * y, kn, b)
    vbeta = _each(lambda x, y: x * y, v, b)
    m = _each(lambda x, y, d: jnp.where(jj < ii, _bdot(x, y, NT) * d, 0.0), kbeta, kn, decay)
    inv = _each(lambda x: eye - x, m)
    power = _each(lambda x: _dot3(x, x), m)
    for step in range(5):
        inv = _each(lambda x, p: x + _dot3(x, p), inv, power)
        if step < 4:
            power = _each(lambda p: _dot3(p, p), power)
    u = _each(_dot3, inv, vbeta)
    w = _each(lambda x, y, e: _dot3(x, y * e), inv, kbeta, e_gcum)
    a_qk = _each(lambda x, y, d: _bdot(x, y, NT) * d, qn, kn, decay)
    v_new = _each(lambda x, y, s: x - _bdot(y, s), u, w, state)
    o = _each(lambda x, e, s, a, vn: _bdot(x * e, s) + _bdot(a, vn), qn, e_gcum, state, a_qk, v_new)
    new_state = _each(lambda s, gl, x, gc, vn: s * jnp.exp(gl) + _bdot(x * jnp.exp(gl - gc), vn, TN),
                      state, g_last, kn, gcum, v_new)
    return o, new_state


GDN_HEADS_PER_STEP = 4


GDN_TIME_TILE = 512


def _gdn_specs(n_tokens, reverse):
    hb, hd, tt = GDN_HEADS_PER_STEP, GDN_HEAD_DIM, GDN_TIME_TILE
    nb, nt = GDN_HEADS // hb, n_tokens // tt

    def when(t):
        return nt - 1 - t if reverse else t

    q = pl.BlockSpec((tt, hb * hd), lambda h, t: (when(t), h))
    k = pl.BlockSpec((tt, hb * hd), lambda h, t: (when(t), nb + h))
    v = pl.BlockSpec((tt, hb * hd), lambda h, t: (when(t), 2 * nb + h))
    vec = pl.BlockSpec((hb, tt, 1), lambda h, t: (h, when(t), 0))
    states = pl.BlockSpec((hb, tt // GDN_CHUNK, hd, hd), lambda h, t: (h, when(t), 0, 0))
    return q, k, v, vec, states


def _gdn_fwd(qkv, beta, g):
    n_tokens = qkv.shape[0]
    hb, hd, tt = GDN_HEADS_PER_STEP, GDN_HEAD_DIM, GDN_TIME_TILE
    n_chunks = tt // GDN_CHUNK
    q_s, k_s, v_s, vec, st = _gdn_specs(n_tokens, False)

    def body(q_ref, k_ref, v_ref, b_ref, g_ref, o_ref, st_ref, state):
        @pl.when(pl.program_id(1) == 0)
        def _():
            state[...] = jnp.zeros_like(state)

        def step(c, carry):
            r = pl.ds(pl.multiple_of(c * GDN_CHUNK, GDN_CHUNK), GDN_CHUNK)
            cols = [slice(h * hd, (h + 1) * hd) for h in range(hb)]
            old = [state[h] for h in range(hb)]
            o, new = _gdn_chunks(
                [q_ref[r, cs] for cs in cols], [k_ref[r, cs] for cs in cols], [v_ref[r, cs] for cs in cols],
                [b_ref[h, r, :] for h in range(hb)], [g_ref[h, r, :] for h in range(hb)], old)
            for h in range(hb):
                st_ref[h, c] = old[h]
                o_ref[r, cols[h]] = o[h]
                state[h] = new[h]
            return carry

        lax.fori_loop(0, n_chunks, step, 0)

    return pl.pallas_call(
        body, grid=(GDN_HEADS // hb, n_tokens // tt), in_specs=[q_s, k_s, v_s, vec, vec], out_specs=(q_s, st),
        out_shape=(jax.ShapeDtypeStruct((n_tokens, GDN_WIDTH), F32),
                   jax.ShapeDtypeStruct((GDN_HEADS, n_tokens // GDN_CHUNK, hd, hd), F32)),
        scratch_shapes=[pltpu.VMEM((hb, hd, hd), F32)],
        name="gdn_fwd", compiler_params=_params(2),
    )(qkv, qkv, qkv, beta, g)


def _gdn_bwd(qkv, beta, g, states, do):
    n_tokens = qkv.shape[0]
    hb, hd, tt = GDN_HEADS_PER_STEP, GDN_HEAD_DIM, GDN_TIME_TILE
    n_chunks = tt // GDN_CHUNK
    q_s, k_s, v_s, vec, st = _gdn_specs(n_tokens, True)

    def body(q_ref, k_ref, v_ref, b_ref, g_ref, st_ref, do_ref, dq_ref, dk_ref, dv_ref, db_ref, dg_ref, dstate):
        @pl.when(pl.program_id(1) == 0)
        def _():
            dstate[...] = jnp.zeros_like(dstate)

        def step(i, carry):
            c = n_chunks - 1 - i
            r = pl.ds(pl.multiple_of(c * GDN_CHUNK, GDN_CHUNK), GDN_CHUNK)
            cols = [slice(h * hd, (h + 1) * hd) for h in range(hb)]
            args = ([q_ref[r, cs] for cs in cols], [k_ref[r, cs] for cs in cols], [v_ref[r, cs] for cs in cols],
                    [b_ref[h, r, :] for h in range(hb)], [g_ref[h, r, :] for h in range(hb)],
                    [st_ref[h, c] for h in range(hb)])
            cts = ([do_ref[r, cs] for cs in cols], [dstate[h] for h in range(hb)])
            dq, dk, dv, db, dg, dst = jax.vjp(_gdn_chunks, *args)[1](cts)
            for h in range(hb):
                dq_ref[r, cols[h]] = dq[h]
                dk_ref[r, cols[h]] = dk[h]
                dv_ref[r, cols[h]] = dv[h]
                db_ref[h, r, :] = db[h]
                dg_ref[h, r, :] = dg[h]
                dstate[h] = dst[h]
            return carry

        lax.fori_loop(0, n_chunks, step, 0)

    wide = jax.ShapeDtypeStruct((n_tokens, GDN_WIDTH), F32)
    thin = jax.ShapeDtypeStruct(beta.shape, F32)
    dq, dk, dv, db, dg = pl.pallas_call(
        body, grid=(GDN_HEADS // hb, n_tokens // tt), in_specs=[q_s, k_s, v_s, vec, vec, st, q_s],
        out_specs=(q_s, q_s, q_s, vec, vec), out_shape=(wide, wide, wide, thin, thin),
        scratch_shapes=[pltpu.VMEM((hb, hd, hd), F32)],
        name="gdn_bwd", compiler_params=_params(2),
    )(qkv, qkv, qkv, beta, g, states, do)
    return jnp.concatenate([dq, dk, dv], axis=1), db, dg


@jax.custom_vjp
def gated_delta(qkv, beta, g):
    return _gdn_fwd(qkv, beta, g)[0]


def _gated_delta_fwd(qkv, beta, g):
    o, states = _gdn_fwd(qkv, beta, g)
    return o, (qkv, beta, g, states)


def _gated_delta_bwd(saved, do):
    return _gdn_bwd(*saved, do)


gated_delta.defvjp(_gated_delta_fwd, _gated_delta_bwd)


def _ffn(x, gain, w, slots, tag):
    h = rowwise(_rms_fn, tag + "_norm", (x,), params=(gain,))[0]
    g = mm_shards_out(h, w[tag + "_w_gate"], slots[tag + "_w_gate"], tag + "_gate")
    u = mm_shards_out(h, w[tag + "_w_up"], slots[tag + "_w_up"], tag + "_up")
    flat = (g.shape[0] * g.shape[1], g.shape[2])
    a = rowwise(_swiglu_fn, tag + "_act", (g.reshape(flat), u.reshape(flat)), tm=512)[0].reshape(g.shape)
    return mm_shards_sum(a, w[tag + "_w_down"], slots[tag + "_w_down"], x, 0.5, tag + "_down")


def _local_loss(diff, x_target, w):
    x, slots, small = diff
    target = x_target
    n_tokens = x.shape[0]
    x1 = _ffn(x, small["ffn1_norm"], w, slots, "ffn1")

    h = rowwise(_rms_fn, "mix_norm", (x1,), params=(small["mix_norm"],))[0]
    proj = {n: mm(h, w[n], slots[n], "in_" + n)
            for n in ("wq_a", "wk_a", "wv_a", "w_qkvb", "w_small", "w_ggate", "w_gatea", "w_gateb")}

    cos, sin = _rope_tables(n_tokens)
    q = _to_heads(rope(proj["wq_a"], cos, sin, "rope_q"))
    k = _to_heads(rope(proj["wk_a"], cos, sin, "rope_k"))
    v = _to_heads(proj["wv_a"])
    o, lse = attention(q, k, v)
    ya = rowwise(_combine_fn, "combine", tuple(_from_heads(o)) + tuple(_from_heads(lse)), tm=512)[0]
    pa = mm(ya, w["w_branch_a"], slots["w_branch_a"], "branch_a")

    qkv = conv_silu(proj["w_qkvb"], small["gdn_conv_w"])
    beta, g = rowwise(_beta_decay_fn, "beta_decay",
                      (proj["w_small"][:, :GDN_HEADS], proj["w_small"][:, GDN_HEADS:2 * GDN_HEADS]),
                      params=(small["gdn_a_log"], small["gdn_dt_bias"]), tm=512)
    ob = gated_delta(qkv, beta.T[:, :, None], g.T[:, :, None])
    yb = rowwise(_outnorm_gate_fn, "outnorm_gate", (ob, proj["w_ggate"]), params=(small["gdn_out_norm"],),
                 tm=512, nblk=GDN_HEADS)[0]
    pb = mm(yb, w["w_branch_b"], slots["w_branch_b"], "branch_b")

    merged = rowwise(_merge_fn, "merge", (proj["w_gatea"], proj["w_gateb"], pa, pb))[0]
    x2 = mm_res(merged, w["w_out"], slots["w_out"], x1, 1.0, "out")

    x3 = _ffn(x2, small["ffn2_norm"], w, slots, "ffn2")
    row_loss = rowwise(_loss_fn, "loss", (x3,), consts=(target,), params=(small["final_norm"],))[0]
    return jnp.sum(row_loss)


SHARDS = (
    ("ffn1_w_gate", 1024, 352), ("ffn1_w_up", 1024, 352), ("ffn1_w_down", 352, 1024), ("w_in", 1024, 1058),
    ("w_branch_a", 256, 128), ("w_branch_b", 128, 1024), ("w_out", 128, 1024),
    ("ffn2_w_gate", 1024, 352), ("ffn2_w_up", 1024, 352), ("ffn2_w_down", 352, 1024),
)
IN_SHARD = D_IN // N_DEV
CONV_SHARD = (GDN_CONV, 3 * GDN_WIDTH // N_DEV)
SMALL_ROWS = 24
ANY = pl.BlockSpec(memory_space=pl.ANY)


def _position():
    return lax.axis_index("x"), lax.axis_index("y"), lax.axis_index("c")


def all_gather_shards(shards, name):
    n = len(shards)

    def body(*refs):
        x_refs, out_refs = refs[:n], refs[n:2 * n]
        send_sems, recv_sems, local_sems = refs[2 * n:]
        x, y, c = _position()
        me, sibling = (x, y, c), (x, y, 1 - c)
        chips = [(1 - x, y), (x, 1 - y), (1 - x, 1 - y)]

        def slab(a, px, py, pc):
            return out_refs[a].at[4 * px + 2 * py + pc]

        def copy(a, k, block, to, src=None):
            return pltpu.make_async_remote_copy(
                src_ref=slab(a, *block) if src is None else src, dst_ref=slab(a, *block),
                send_sem=send_sems.at[7 * a + k], recv_sem=recv_sems.at[7 * a + k], device_id=to, device_id_type=MESH)

        mine = [pltpu.make_async_copy(x_refs[a], slab(a, *me), local_sems.at[a]) for a in range(n)]
        for cp in mine:
            cp.start()
        first = []
        for j, chip in enumerate(chips):
            first += [copy(a, 1 + j, me, (*chip, c), src=x_refs[a]) for a in range(n)]
        first += [copy(a, 0, me, sibling, src=x_refs[a]) for a in range(n)]
        for cp in first:
            cp.start()
        passed = []
        for j, chip in enumerate(chips):
            for a in range(n):
                copy(a, 1 + j, (*chip, c), me).wait_recv()
                cp = copy(a, 4 + j, (*chip, c), sibling)
                cp.start()
                passed.append(cp)
        for a in range(n):
            copy(a, 0, sibling, me).wait_recv()
        for j, chip in enumerate(chips):
            for a in range(n):
                copy(a, 4 + j, (*chip, 1 - c), me).wait_recv()
        for cp in first + passed:
            cp.wait_send()
        for cp in mine:
            cp.wait()

    return pl.pallas_call(
        body, out_shape=tuple(jax.ShapeDtypeStruct((N_DEV,) + s.shape, s.dtype) for s in shards),
        in_specs=[ANY] * n, out_specs=(ANY,) * n,
        scratch_shapes=[pltpu.SemaphoreType.DMA((7 * n,)), pltpu.SemaphoreType.DMA((7 * n,)),
                        pltpu.SemaphoreType.DMA((n,))],
        name=name,
    )(*shards)


def exchange_with_sibling(grads):
    n = len(grads)

    def body(*refs):
        g_refs, recv_refs = refs[:n], refs[n:2 * n]
        send_sems, recv_sems = refs[2 * n:]
        x, y, c = _position()
        copies = [pltpu.make_async_remote_copy(
            src_ref=g_refs[a].at[2 * k + 1 - c], dst_ref=recv_refs[a].at[k], send_sem=send_sems.at[4 * a + k],
            recv_sem=recv_sems.at[4 * a + k], device_id=(x, y, 1 - c), device_id_type=MESH)
            for k in range(4) for a in range(n)]
        for cp in copies:
            cp.start()
        for cp in copies:
            cp.wait()

    return pl.pallas_call(
        body, out_shape=tuple(jax.ShapeDtypeStruct((4,) + g.shape[1:], g.dtype) for g in grads),
        in_specs=[ANY] * n, out_specs=(ANY,) * n,
        scratch_shapes=[pltpu.SemaphoreType.DMA((4 * n,)), pltpu.SemaphoreType.DMA((4 * n,))], name="rs_sibling",
    )(*grads)


def _row_tile(rows):
    return 256 if rows % 256 == 0 else rows


def add_sibling(grads, received, core, name):
    _, rows, width = grads.shape
    tr = _row_tile(rows)

    def body(c_ref, g_ref, r_ref, o_ref):
        o_ref[...] = (g_ref[...] + r_ref[...]).astype(BF16)

    blk = (1, tr, width)
    return pl.pallas_call(
        body,
        grid_spec=pltpu.PrefetchScalarGridSpec(
            num_scalar_prefetch=1, grid=(4, rows // tr),
            in_specs=[pl.BlockSpec(blk, lambda k, i, c_ref: (2 * k + c_ref[0], i, 0)),
                      pl.BlockSpec(blk, lambda k, i, c_ref: (k, i, 0))],
            out_specs=pl.BlockSpec(blk, lambda k, i, c_ref: (k, i, 0))),
        out_shape=jax.ShapeDtypeStruct((4, rows, width), BF16), name=name, compiler_params=_params(2),
    )(core, grads, received)


def exchange_with_chips(partials):
    n = len(partials)

    def body(*refs):
        p_refs, recv_refs = refs[:n], refs[n:2 * n]
        send_sems, recv_sems = refs[2 * n:]
        x, y, c = _position()
        chips = [(1 - x, y), (x, 1 - y), (1 - x, 1 - y)]
        copies = [pltpu.make_async_remote_copy(
            src_ref=p_refs[a].at[2 * cx + cy], dst_ref=recv_refs[a].at[j], send_sem=send_sems.at[3 * a + j],
            recv_sem=recv_sems.at[3 * a + j], device_id=(cx, cy, c), device_id_type=MESH)
            for a in range(n) for j, (cx, cy) in enumerate(chips)]
        for cp in copies:
            cp.start()
        for cp in copies:
            cp.wait()

    return pl.pallas_call(
        body, out_shape=tuple(jax.ShapeDtypeStruct((3,) + p.shape[1:], p.dtype) for p in partials),
        in_specs=[ANY] * n, out_specs=(ANY,) * n,
        scratch_shapes=[pltpu.SemaphoreType.DMA((3 * n,)), pltpu.SemaphoreType.DMA((3 * n,))], name="rs_chips",
    )(*partials)


def all_reduce_small(vals):
    rows, width = vals.shape

    def body(x_ref, out_ref, all_ref, send_sems, recv_sems):
        x, y, c = _position()
        me, sibling = (x, y, c), (x, y, 1 - c)
        chips = [(1 - x, y), (x, 1 - y), (1 - x, 1 - y)]

        def slab(px, py, pc):
            return all_ref.at[4 * px + 2 * py + pc]

        def copy(k, block, to, src=None):
            return pltpu.make_async_remote_copy(
                src_ref=slab(*block) if src is None else src, dst_ref=slab(*block),
                send_sem=send_sems.at[k], recv_sem=recv_sems.at[k], device_id=to, device_id_type=MESH)

        first = [copy(0, me, sibling, src=x_ref)]
        first += [copy(1 + j, me, (*chip, c), src=x_ref) for j, chip in enumerate(chips)]
        for cp in first:
            cp.start()
        all_ref[4 * x + 2 * y + c] = x_ref[...]
        passed = [copy(4 + j, (*chip, c), sibling) for j, chip in enumerate(chips)]
        for j, chip in enumerate(chips):
            copy(1 + j, (*chip, c), me).wait_recv()
            passed[j].start()
        copy(0, sibling, me).wait_recv()
        for j, chip in enumerate(chips):
            copy(4 + j, (*chip, 1 - c), me).wait_recv()
        for cp in first + passed:
            cp.wait_send()
        total = all_ref[0]
        for d in range(1, N_DEV):
            total = total + all_ref[d]
        out_ref[...] = total

    vmem = pl.BlockSpec(memory_space=pltpu.VMEM)
    return pl.pallas_call(
        body, out_shape=(jax.ShapeDtypeStruct(vals.shape, F32), jax.ShapeDtypeStruct((N_DEV, rows, width), F32)),
        in_specs=[vmem], out_specs=(vmem, vmem),
        scratch_shapes=[pltpu.SemaphoreType.DMA((7,)), pltpu.SemaphoreType.DMA((7,))], name="small_allreduce",
    )(vals)[0]


def adamw(w, g, m, v, name):
    shape = w.shape
    w2, g2, m2, v2 = [a.reshape((-1, shape[-1])) for a in (w, g, m, v)]
    rows, cols = w2.shape
    tr = 256 if rows % 256 == 0 else rows

    def body(w_ref, g_ref, m_ref, v_ref, d_ref, nm_ref, nv_ref):
        gv = g_ref[...]
        nm = ADAM_B1 * m_ref[...] + (1.0 - ADAM_B1) * gv
        nv = ADAM_B2 * v_ref[...] + (1.0 - ADAM_B2) * (gv * gv)
        m_hat = nm / (1.0 - ADAM_B1 ** ADAM_STEP)
        v_hat = nv / (1.0 - ADAM_B2 ** ADAM_STEP)
        d_ref[...] = -ADAM_LR * (m_hat / (jnp.sqrt(v_hat) + ADAM_EPS) + ADAM_WD * w_ref[...])
        nm_ref[...] = nm
        nv_ref[...] = nv

    blk = pl.BlockSpec((tr, cols), lambda i: (i, 0))
    out = jax.ShapeDtypeStruct((rows, cols), F32)
    outs = pl.pallas_call(
        body, grid=(rows // tr,), in_specs=[blk] * 4, out_specs=(blk,) * 3, out_shape=(out,) * 3,
        name=name, compiler_params=_params(1),
    )(w2, g2, m2, v2)
    return tuple(o.reshape(shape) for o in outs)


def adamw_summed(w, m, v, grads, from_sibling, received, me, name):
    shape = w.shape
    rows, cols = shape[-2:]
    w3, m3, v3 = [a.reshape((1, rows, cols)) for a in (w, m, v)]
    tr = _row_tile(rows)

    def body(me_ref, w_ref, m_ref, v_ref, own_ref, sib_ref, r_ref, g_ref, d_ref, nm_ref, nv_ref):
        gv = own_ref[0] + sib_ref[0]
        for j in range(3):
            gv = gv + r_ref[j].astype(F32)
        nm = ADAM_B1 * m_ref[0] + (1.0 - ADAM_B1) * gv
        nv = ADAM_B2 * v_ref[0] + (1.0 - ADAM_B2) * (gv * gv)
        m_hat = nm / (1.0 - ADAM_B1 ** ADAM_STEP)
        v_hat = nv / (1.0 - ADAM_B2 ** ADAM_STEP)
        g_ref[0] = gv
        d_ref[0] = -ADAM_LR * (m_hat / (jnp.sqrt(v_hat) + ADAM_EPS) + ADAM_WD * w_ref[0])
        nm_ref[0] = nm
        nv_ref[0] = nv

    one = pl.BlockSpec((1, tr, cols), lambda i, me_ref: (0, i, 0))
    out = jax.ShapeDtypeStruct((1, rows, cols), F32)
    outs = pl.pallas_call(
        body,
        grid_spec=pltpu.PrefetchScalarGridSpec(
            num_scalar_prefetch=1, grid=(rows // tr,),
            in_specs=[one, one, one, pl.BlockSpec((1, tr, cols), lambda i, me_ref: (me_ref[0], i, 0)),
                      pl.BlockSpec((1, tr, cols), lambda i, me_ref: (me_ref[1], i, 0)),
                      pl.BlockSpec((3, tr, cols), lambda i, me_ref: (0, i, 0))],
            out_specs=(one,) * 4),
        out_shape=(out,) * 4, name=name, compiler_params=_params(1),
    )(me, w3, m3, v3, grads, from_sibling, received)
    return tuple(o.reshape(shape) for o in outs)


IN_PIECES = (("wq_a", 0, 768), ("wk_a", 768, 1536), ("wv_a", 1536, 2304), ("w_qkvb", 2304, 5376),
             ("w_small", 5376, 5392), ("w_ggate", 5392, 6416), ("w_gatea", 6416, 7440), ("w_gateb", 7440, 8464))
IN_ROW_TILE = 128


def _piece_width(lo, hi):
    return max(hi - lo, LANES)


def _piece_segments(lo, hi):
    out = []
    for j in range(N_DEV):
        a, b = max(lo, IN_SHARD * j), min(hi, IN_SHARD * (j + 1))
        if a < b:
            out.append((j, a - IN_SHARD * j, b - IN_SHARD * j, a - lo, b - lo))
    return out


def split_in_shards(gathered):
    rows = gathered.shape[1]

    def body(g_ref, *outs):
        for (name, lo, hi), o_ref in zip(IN_PIECES, outs):
            if hi - lo < LANES:
                o_ref[...] = jnp.zeros_like(o_ref)
            for j, s0, s1, d0, d1 in _piece_segments(lo, hi):
                o_ref[:, d0:d1] = g_ref[j, :, s0:s1]

    widths = [_piece_width(lo, hi) for _, lo, hi in IN_PIECES]
    outs = pl.pallas_call(
        body, grid=(rows // IN_ROW_TILE,),
        in_specs=[pl.BlockSpec((N_DEV, IN_ROW_TILE, IN_SHARD), lambda i: (0, i, 0))],
        out_specs=tuple(pl.BlockSpec((IN_ROW_TILE, wd), lambda i: (i, 0)) for wd in widths),
        out_shape=tuple(jax.ShapeDtypeStruct((rows, wd), gathered.dtype) for wd in widths),
        name="split_in_shards", compiler_params=_params(1),
    )(gathered)
    return {name: o for (name, _, _), o in zip(IN_PIECES, outs)}


def join_in_grads(grads):
    pieces = [grads[name] for name, _, _ in IN_PIECES]
    rows = pieces[0].shape[0]

    def body(*refs):
        o_ref = refs[-1]
        for (name, lo, hi), p_ref in zip(IN_PIECES, refs[:-1]):
            for j, s0, s1, d0, d1 in _piece_segments(lo, hi):
                o_ref[j, :, s0:s1] = p_ref[:, d0:d1]

    return pl.pallas_call(
        body, grid=(rows // IN_ROW_TILE,),
        in_specs=[pl.BlockSpec((IN_ROW_TILE, p.shape[1]), lambda i: (i, 0)) for p in pieces],
        out_specs=pl.BlockSpec((N_DEV, IN_ROW_TILE, IN_SHARD), lambda i: (0, i, 0)),
        out_shape=jax.ShapeDtypeStruct((N_DEV, rows, IN_SHARD), F32), name="join_in_grads", compiler_params=_params(1),
    )(*pieces)


SMALL_VECTORS = ("ffn1_norm", "mix_norm", "ffn2_norm", "final_norm")


def _pack_small(gs):
    row = jnp.concatenate([gs["gdn_a_log"].reshape(-1), gs["gdn_dt_bias"].reshape(-1), gs["gdn_out_norm"].reshape(-1)])
    rows = [gs[n].reshape(1, D_MODEL) for n in SMALL_VECTORS]
    rows.append(jnp.pad(row, (0, D_MODEL - row.shape[0])).reshape(1, D_MODEL))
    rows.append(gs["gdn_conv_w"].reshape(-1, D_MODEL))
    packed = jnp.concatenate(rows, axis=0)
    return jnp.pad(packed, ((0, SMALL_ROWS - packed.shape[0]), (0, 0)))


def _unpack_small(packed):
    out = {n: packed[i].reshape(1, D_MODEL) for i, n in enumerate(SMALL_VECTORS)}
    row = packed[len(SMALL_VECTORS)]
    out["gdn_a_log"] = row[:GDN_HEADS].reshape(1, GDN_HEADS)
    out["gdn_dt_bias"] = row[GDN_HEADS:2 * GDN_HEADS].reshape(1, GDN_HEADS)
    out["gdn_out_norm"] = row[2 * GDN_HEADS:2 * GDN_HEADS + GDN_HEAD_DIM].reshape(1, GDN_HEAD_DIM)
    first = len(SMALL_VECTORS) + 1
    out["gdn_conv_w"] = packed[first:first + GDN_CONV * 3].reshape(GDN_CONV, 3 * GDN_WIDTH)
    return out


WEIGHTS = ("ffn1_norm", "ffn1_w_gate", "ffn1_w_up", "ffn1_w_down", "mix_norm", "w_in", "gdn_conv_w", "gdn_a_log",
           "gdn_dt_bias", "gdn_out_norm", "w_branch_a", "w_branch_b", "w_out", "ffn2_norm", "ffn2_w_gate",
           "ffn2_w_up", "ffn2_w_down", "final_norm")


def kernel(x, ffn1_norm, ffn1_w_gate, ffn1_w_up, ffn1_w_down, mix_norm, w_in, gdn_conv_w, gdn_a_log, gdn_dt_bias, gdn_out_norm, w_branch_a, w_branch_b, w_out, ffn2_norm, ffn2_w_gate, ffn2_w_up, ffn2_w_down, final_norm, loss_target, m_ffn1_norm, m_ffn1_w_gate, m_ffn1_w_up, m_ffn1_w_down, m_mix_norm, m_w_in, m_gdn_conv_w, m_gdn_a_log, m_gdn_dt_bias, m_gdn_out_norm, m_w_branch_a, m_w_branch_b, m_w_out, m_ffn2_norm, m_ffn2_w_gate, m_ffn2_w_up, m_ffn2_w_down, m_final_norm, v_ffn1_norm, v_ffn1_w_gate, v_ffn1_w_up, v_ffn1_w_down, v_mix_norm, v_w_in, v_gdn_conv_w, v_gdn_a_log, v_gdn_dt_bias, v_gdn_out_norm, v_w_branch_a, v_w_branch_b, v_w_out, v_ffn2_norm, v_ffn2_w_gate, v_ffn2_w_up, v_ffn2_w_down, v_final_norm):
    given = dict(locals())
    px, py, pc = _position()
    big_names = [n for n, _, _ in SHARDS]

    shards = [given[n][0].astype(BF16) for n in big_names] + [gdn_conv_w[0]]
    gathered = dict(zip(big_names + ["gdn_conv_w"], all_gather_shards(shards, "gather_weights")))
    w = {n: gathered[n] for n in big_names if n.startswith("ffn")}
    w.update(split_in_shards(gathered["w_in"]))
    w["w_branch_a"] = gathered["w_branch_a"].transpose(1, 0, 2).reshape(256, D_MODEL)
    w["w_branch_b"] = gathered["w_branch_b"].reshape(D_MODEL, D_MODEL)
    w["w_out"] = gathered["w_out"].reshape(D_MODEL, D_MODEL)
    conv_full = gathered["gdn_conv_w"].transpose(1, 0, 2).reshape(GDN_CONV, 3 * GDN_WIDTH)
    slots = {n: jnp.zeros(a.shape, F32) for n, a in w.items()}
    small = dict(ffn1_norm=ffn1_norm, mix_norm=mix_norm, ffn2_norm=ffn2_norm, final_norm=final_norm.reshape(1, D_MODEL),
                 gdn_a_log=gdn_a_log, gdn_dt_bias=gdn_dt_bias, gdn_out_norm=gdn_out_norm, gdn_conv_w=conv_full)

    loss_local, (grad_x, g_w, g_small) = jax.value_and_grad(_local_loss)((x[0], slots, small), loss_target[0], w)
    loss = lax.psum(loss_local, ("x", "y", "c"))

    g_big = {n: g_w[n] for n in big_names if n.startswith("ffn")}
    g_big["w_in"] = join_in_grads(g_w)
    g_big["w_branch_a"] = g_w["w_branch_a"].reshape(256, N_DEV, 128).transpose(1, 0, 2)
    g_big["w_branch_b"] = g_w["w_branch_b"].reshape(N_DEV, 128, D_MODEL)
    g_big["w_out"] = g_w["w_out"].reshape(N_DEV, 128, D_MODEL)
    g_list = [g_big[n] for n in big_names]
    core = pc.astype(jnp.int32).reshape(1)
    me = 4 * px + 2 * py + pc
    me_and_chip = jnp.stack([me, 2 * px + py]).astype(jnp.int32)
    from_sibling = exchange_with_sibling(g_list)
    partials = [add_sibling(g, r, core, "rs_add_" + n) for n, g, r in zip(big_names, g_list, from_sibling)]
    from_chips = exchange_with_chips(partials)

    results = {}
    for n, g, sib, recv in zip(big_names, g_list, from_sibling, from_chips):
        results[n] = adamw_summed(given[n], given["m_" + n], given["v_" + n], g, sib, recv, me_and_chip, "adamw_" + n)

    small_sum = _unpack_small(all_reduce_small(_pack_small(g_small)))
    conv_cols = CONV_SHARD[1]
    small_sum["gdn_conv_w"] = lax.dynamic_slice(small_sum["gdn_conv_w"], (0, me * conv_cols), (GDN_CONV, conv_cols))
    for n in WEIGHTS:
        if n not in results:
            g = small_sum[n].reshape(given[n].shape)
            results[n] = (g,) + adamw(given[n], g, given["m_" + n], given["v_" + n], "adamw_" + n)

    outs = [[results[n][i] for n in WEIGHTS] for i in range(4)]
    return (loss, grad_x[None], *outs[0], *outs[1], *outs[2], *outs[3])
```

```python
import jax
import jax.numpy as jnp
from jax import lax
from jax.experimental import pallas as pl
from jax.experimental.pallas import tpu as pltpu

F32 = jnp.float32
BF16 = jnp.bfloat16
HI = lax.Precision.HIGHEST
MESH = pl.DeviceIdType.MESH

N_DEV = 8
D_MODEL = 1024
D_FF = 2816
EPS = 1e-6
ROPE_THETA = 10000.0
DSW_DILATIONS = (1, 4, 16)
DSW_HEADS_PER_GROUP = 4
DSW_HEAD_DIM = 64
DSW_BLOCK = 128
DSW_WIDTH = 768
N_DSW_HEADS = 12
GDN_HEADS = 8
GDN_HEAD_DIM = 128
GDN_WIDTH = 1024
GDN_CONV = 4
GDN_CHUNK = 64
IN_OFFSETS = dict(qa=0, ka=768, va=1536, qkvb=2304, small=5376, ggate=5392, gatea=6416, gateb=7440)
D_IN = 8464

ADAM_LR = 0.001
ADAM_B1 = 0.9
ADAM_B2 = 0.999
ADAM_EPS = 1e-08
ADAM_WD = 0.01
ADAM_STEP = 10

VMEM_LIMIT_BYTES = 56 * 1024 * 1024
LANES = 128

NN = (((1,), (0,)), ((), ()))
NT = (((1,), (1,)), ((), ()))
TN = (((0,), (0,)), ((), ()))


def _params(n_grid):
    return pltpu.CompilerParams(dimension_semantics=("arbitrary",) * n_grid, vmem_limit_bytes=VMEM_LIMIT_BYTES)


def _tile(n, pref):
    best = None
    t = LANES
    while t <= min(n, pref):
        if n % t == 0:
            best = t
        t += LANES
    return n if best is None else best


def _matmul(a, b, *, name, ta=False, tb=False, res=None, scale=1.0):
    K, M = a.shape if ta else a.shape[::-1]
    N = b.shape[0] if tb else b.shape[1]
    assert (b.shape[1] if tb else b.shape[0]) == K, (a.shape, b.shape, ta, tb)
    tm = _tile(M, 512)
    tn = _tile(N, 512)
    dn = (((0 if ta else 1,), (1 if tb else 0,)), ((), ()))

    def body(*refs):
        a_ref, b_ref = refs[:2]
        o_ref = refs[-1]
        acc = lax.dot_general(a_ref[...].astype(BF16), b_ref[...].astype(BF16), dn, preferred_element_type=F32)
        if scale != 1.0:
            acc = acc * scale
        if res is not None:
            acc = refs[2][...] + acc
        o_ref[...] = acc

    a_spec = pl.BlockSpec((K, tm), lambda i, j: (0, i)) if ta else pl.BlockSpec((tm, K), lambda i, j: (i, 0))
    b_spec = pl.BlockSpec((tn, K), lambda i, j: (j, 0)) if tb else pl.BlockSpec((K, tn), lambda i, j: (0, j))
    o_spec = pl.BlockSpec((tm, tn), lambda i, j: (i, j))
    ins, specs = [a, b], [a_spec, b_spec]
    if res is not None:
        ins.append(res)
        specs.append(o_spec)
    return pl.pallas_call(
        body, grid=(M // tm, N // tn), in_specs=specs, out_specs=o_spec,
        out_shape=jax.ShapeDtypeStruct((M, N), F32), name=name, compiler_params=_params(2),
    )(*ins)


def _make_mm(name, scale=1.0, with_res=False):
    @jax.custom_vjp
    def op(a, w, slot, res):
        return _matmul(a, w, name=name, res=res if with_res else None, scale=scale)

    def fwd(a, w, slot, res):
        return op(a, w, slot, res), (a, w)

    def bwd(saved, g):
        a, w = saved
        da = _matmul(g, w, name=name + "_da", tb=True, scale=scale)
        dw = _matmul(a, g, name=name + "_dw", ta=True, scale=scale)
        return da, None, dw, (g if with_res else None)

    op.defvjp(fwd, bwd)
    return op


def mm(a, w, slot, name):
    return _make_mm(name)(a, w, slot, None)


def mm_res(a, w, slot, res, scale, name):
    return _make_mm(name, scale=scale, with_res=True)(a, w, slot, res)


def _rw_specs(arrs, tm, nblk):
    return [pl.BlockSpec((tm, a.shape[1] // nblk), lambda i, j: (i, j)) for a in arrs]


def _rowwise_fwd(fn, name, rows, consts, params, tm, nblk):
    n_rows = rows[0].shape[0]
    tm = min(tm, n_rows)
    ins = list(rows) + list(consts)
    avals = [jax.ShapeDtypeStruct((tm, a.shape[1] // nblk), a.dtype) for a in ins]
    avals += [jax.ShapeDtypeStruct(p.shape, p.dtype) for p in params]
    out_avals = jax.eval_shape(fn, *avals)
    n_in = len(ins) + len(params)

    def body(*refs):
        outs = fn(*[r[...] for r in refs[:n_in]])
        for r, o in zip(refs[n_in:], outs):
            r[...] = o.astype(r.dtype)

    return pl.pallas_call(
        body, grid=(n_rows // tm, nblk),
        in_specs=_rw_specs(ins, tm, nblk) + [pl.BlockSpec(p.shape, lambda i, j: (0, 0)) for p in params],
        out_specs=tuple(pl.BlockSpec((tm, o.shape[1]), lambda i, j: (i, j)) for o in out_avals),
        out_shape=tuple(jax.ShapeDtypeStruct((n_rows, o.shape[1] * nblk), o.dtype) for o in out_avals),
        name=name, compiler_params=_params(2),
    )(*ins, *params)


def _rowwise_bwd(fn, name, rows, consts, params, cts, tm, nblk):
    n_rows = rows[0].shape[0]
    tm = min(tm, n_rows)
    nr, nc, npar, nct = len(rows), len(consts), len(params), len(cts)

    def body(*refs):
        rv = [r[...] for r in refs[:nr]]
        cv = [r[...] for r in refs[nr:nr + nc]]
        pv = [r[...] for r in refs[nr + nc:nr + nc + npar]]
        ctv = [r[...] for r in refs[nr + nc + npar:nr + nc + npar + nct]]
        outs = refs[nr + nc + npar + nct:]
        _, vjp = jax.vjp(lambda *d: fn(*d[:nr], *cv, *d[nr:]), *rv, *pv)
        grads = vjp(tuple(ctv))
        for k in range(nr):
            outs[k][...] = grads[k]
        first = jnp.logical_and(pl.program_id(0) == 0, pl.program_id(1) == 0)
        for k in range(npar):
            ref = outs[nr + k]

            @pl.when(first)
            def _(ref=ref):
                ref[...] = jnp.zeros_like(ref)

            ref[...] += grads[nr + k]

    ins = list(rows) + list(consts)
    return pl.pallas_call(
        body, grid=(n_rows // tm, nblk),
        in_specs=(_rw_specs(ins, tm, nblk) + [pl.BlockSpec(p.shape, lambda i, j: (0, 0)) for p in params]
                  + _rw_specs(cts, tm, nblk)),
        out_specs=tuple(_rw_specs(rows, tm, nblk) + [pl.BlockSpec(p.shape, lambda i, j: (0, 0)) for p in params]),
        out_shape=tuple([jax.ShapeDtypeStruct(a.shape, F32) for a in rows]
                        + [jax.ShapeDtypeStruct(p.shape, F32) for p in params]),
        name=name, compiler_params=_params(2),
    )(*ins, *params, *cts)


def rowwise(fn, name, rows, consts=(), params=(), tm=256, nblk=1):
    rows, consts, params = tuple(rows), tuple(consts), tuple(params)

    @jax.custom_vjp
    def op(rows, consts, params):
        return _rowwise_fwd(fn, name, rows, consts, params, tm, nblk)

    def fwd(rows, consts, params):
        return op(rows, consts, params), (rows, consts, params)

    def bwd(saved, cts):
        rows, consts, params = saved
        grads = _rowwise_bwd(fn, name + "_bwd", rows, consts, params, tuple(cts), tm, nblk)
        return tuple(grads[:len(rows)]), None, tuple(grads[len(rows):])

    op.defvjp(fwd, bwd)
    return op(rows, consts, params)


def _rms_fn(x, gain):
    return (x * lax.rsqrt(jnp.mean(x * x, axis=-1, keepdims=True) + EPS) * gain,)


def _merge_fn(ga, gb, pa, pb):
    return (jax.nn.sigmoid(ga) * pa + jax.nn.sigmoid(gb) * pb,)


def _outnorm_gate_fn(o, gate, gain):
    y = o * lax.rsqrt(jnp.mean(o * o, axis=-1, keepdims=True) + EPS) * gain
    return (y * (gate * jax.nn.sigmoid(gate)),)


def _beta_decay_fn(beta_raw, decay_raw, a_log, dt_bias):
    z = decay_raw + dt_bias
    softplus = jnp.maximum(z, 0.0) + jnp.log(1.0 + jnp.exp(-jnp.abs(z)))
    g = -jnp.exp(a_log) * softplus
    rows = g.shape[0]
    ii = lax.broadcasted_iota(jnp.int32, (rows, rows), 0)
    jj = lax.broadcasted_iota(jnp.int32, (rows, rows), 1)
    same_chunk_before = jnp.logical_and(jj <= ii, jj // GDN_CHUNK == ii // GDN_CHUNK).astype(F32)
    gcum = lax.dot_general(same_chunk_before, g, NN, precision=HI, preferred_element_type=F32)
    return jax.nn.sigmoid(beta_raw), gcum


def _combine_fn(o0, o1, o2, l0, l1, l2):
    m = lax.stop_gradient(jnp.maximum(jnp.maximum(l0, l1), l2))
    e0, e1, e2 = jnp.exp(l0 - m), jnp.exp(l1 - m), jnp.exp(l2 - m)
    return ((e0 * o0 + e1 * o1 + e2 * o2) / (e0 + e1 + e2),)


def _loss_fn(x, target, gain):
    y = x * lax.rsqrt(jnp.mean(x * x, axis=-1, keepdims=True) + EPS) * gain
    err = y - target
    return (0.5 * jnp.mean(err * err, axis=-1, keepdims=True),)


def _rope_call(x, cos, sin, name):
    n_rows, width = x.shape
    tm = 512

    def body(x_ref, c_ref, s_ref, o_ref):
        v = x_ref[...]
        lane = lax.broadcasted_iota(jnp.int32, v.shape, 1)
        low = (lane % DSW_HEAD_DIM) < DSW_HEAD_DIM // 2
        half = DSW_HEAD_DIM // 2
        swapped = jnp.where(low, pltpu.roll(v, LANES - half, 1), pltpu.roll(v, half, 1))
        o_ref[...] = v * c_ref[...] + swapped * s_ref[...]

    tab = pl.BlockSpec((tm, LANES), lambda i, j: (i, 0))
    blk = pl.BlockSpec((tm, LANES), lambda i, j: (i, j))
    return pl.pallas_call(
        body, grid=(n_rows // tm, width // LANES), in_specs=[blk, tab, tab], out_specs=blk,
        out_shape=jax.ShapeDtypeStruct(x.shape, F32), name=name, compiler_params=_params(2),
    )(x, cos, sin)


def rope(x, cos, sin, name):
    @jax.custom_vjp
    def op(x):
        return _rope_call(x, cos, sin, name)

    def fwd(x):
        return op(x), None

    def bwd(_, g):
        return (_rope_call(g, cos, -sin, name + "_bwd"),)

    op.defvjp(fwd, bwd)
    return op(x)


def _rope_tables(n_tokens):
    half = DSW_HEAD_DIM // 2
    inv_freq = ROPE_THETA ** (-jnp.arange(half, dtype=F32) / half)
    ang = jnp.arange(n_tokens, dtype=F32)[:, None] * inv_freq[None, :]
    cos, sin = jnp.cos(ang), jnp.sin(ang)
    return jnp.tile(jnp.concatenate([cos, cos], 1), (1, 2)), jnp.tile(jnp.concatenate([-sin, sin], 1), (1, 2))


def _attn_probs(q, kp, kc, h, n):
    blk = DSW_BLOCK
    k = jnp.concatenate([kp, kc], axis=0).astype(BF16)
    s = lax.dot_general(q.astype(BF16), k, NT, preferred_element_type=F32) * (DSW_HEAD_DIM ** -0.5)
    blocks_per_seq = jnp.where(h < 4, 16, jnp.where(h < 8, 4, 1))
    first = (n % blocks_per_seq) == 0
    qi = lax.broadcasted_iota(jnp.int32, (blk, 2 * blk), 0)
    kj = lax.broadcasted_iota(jnp.int32, (blk, 2 * blk), 1)
    dist = qi + blk - kj
    valid = (dist >= 0) & (dist <= blk) & jnp.logical_or(kj >= blk, jnp.logical_not(first))
    s = jnp.where(valid, s, -1e30)
    m = jnp.max(s, axis=-1, keepdims=True)
    p = jnp.exp(s - m)
    l = jnp.sum(p, axis=-1, keepdims=True)
    return p / l, m + jnp.log(l), k


def _attn_specs(n_tokens):
    blk = DSW_BLOCK
    cur = pl.BlockSpec((1, blk, DSW_HEAD_DIM), lambda h, n: (h, n, 0))
    prev = pl.BlockSpec((1, blk, DSW_HEAD_DIM), lambda h, n: (h, jnp.maximum(n - 1, 0), 0))
    return cur, prev


def _attn_fwd(q, k, v):
    nh, n_tokens, hd = q.shape
    cur, prev = _attn_specs(n_tokens)

    def body(q_ref, kp_ref, kc_ref, vp_ref, vc_ref, o_ref, l_ref):
        h, n = pl.program_id(0), pl.program_id(1)
        p, lse, _ = _attn_probs(q_ref[0], kp_ref[0], kc_ref[0], h, n)
        vv = jnp.concatenate([vp_ref[0], vc_ref[0]], axis=0).astype(BF16)
        o_ref[0] = lax.dot_general(p.astype(BF16), vv, NN, preferred_element_type=F32)
        l_ref[0] = jnp.broadcast_to(lse, (DSW_BLOCK, hd))

    return pl.pallas_call(
        body, grid=(nh, n_tokens // DSW_BLOCK), in_specs=[cur, prev, cur, prev, cur], out_specs=(cur, cur),
        out_shape=(jax.ShapeDtypeStruct(q.shape, F32), jax.ShapeDtypeStruct(q.shape, F32)),
        name="attn_fwd", compiler_params=_params(2),
    )(q, k, k, v, v)


def _attn_bwd(q, k, v, do, dlse):
    nh, n_tokens, hd = q.shape
    nblk = n_tokens // DSW_BLOCK
    cur, prev = _attn_specs(n_tokens)
    part = pl.BlockSpec((1, 1, 2 * DSW_BLOCK, hd), lambda h, n: (h, n, 0, 0))

    def body(q_ref, kp_ref, kc_ref, vp_ref, vc_ref, do_ref, dl_ref, dq_ref, dk_ref, dv_ref):
        h, n = pl.program_id(0), pl.program_id(1)
        qb = q_ref[0].astype(BF16)
        p, _, kb = _attn_probs(q_ref[0], kp_ref[0], kc_ref[0], h, n)
        vv = jnp.concatenate([vp_ref[0], vc_ref[0]], axis=0).astype(BF16)
        dob = do_ref[0].astype(BF16)
        dp = lax.dot_general(dob, vv, NT, preferred_element_type=F32)
        dv_ref[0, 0] = lax.dot_general(p.astype(BF16), dob, TN, preferred_element_type=F32)
        dlse = jnp.sum(dl_ref[0], axis=-1, keepdims=True)
        ds = p * (dp - jnp.sum(dp * p, axis=-1, keepdims=True) + dlse) * (DSW_HEAD_DIM ** -0.5)
        dsb = ds.astype(BF16)
        dq_ref[0] = lax.dot_general(dsb, kb, NN, preferred_element_type=F32)
        dk_ref[0, 0] = lax.dot_general(dsb, qb, TN, preferred_element_type=F32)

    dq, dkp, dvp = pl.pallas_call(
        body, grid=(nh, nblk), in_specs=[cur, prev, cur, prev, cur, cur, cur], out_specs=(cur, part, part),
        out_shape=(jax.ShapeDtypeStruct(q.shape, F32),
                   jax.ShapeDtypeStruct((nh, nblk, 2 * DSW_BLOCK, hd), F32),
                   jax.ShapeDtypeStruct((nh, nblk, 2 * DSW_BLOCK, hd), F32)),
        name="attn_bwd", compiler_params=_params(2),
    )(q, k, k, v, v, do, dlse)

    def fold(partial):
        own = partial[:, :, DSW_BLOCK:]
        from_next = jnp.pad(partial[:, 1:, :DSW_BLOCK], ((0, 0), (0, 1), (0, 0), (0, 0)))
        return (own + from_next).reshape(nh, n_tokens, hd)

    return dq, fold(dkp), fold(dvp)


@jax.custom_vjp
def attention(q, k, v):
    return _attn_fwd(q, k, v)


def _attention_fwd(q, k, v):
    return _attn_fwd(q, k, v), (q, k, v)


def _attention_bwd(saved, cts):
    q, k, v = saved
    return _attn_bwd(q, k, v, cts[0], cts[1])


attention.defvjp(_attention_fwd, _attention_bwd)


def _to_heads(a):
    n_tokens = a.shape[0]
    outs = []
    for gi, d in enumerate(DSW_DILATIONS):
        blk = a[:, gi * 256:(gi + 1) * 256].reshape(n_tokens // d, d, DSW_HEADS_PER_GROUP, DSW_HEAD_DIM)
        outs.append(blk.transpose(2, 1, 0, 3).reshape(DSW_HEADS_PER_GROUP, n_tokens, DSW_HEAD_DIM))
    return jnp.concatenate(outs, 0)


def _from_heads(a):
    n_tokens = a.shape[1]
    outs = []
    for gi, d in enumerate(DSW_DILATIONS):
        blk = a[gi * 4:(gi + 1) * 4].reshape(DSW_HEADS_PER_GROUP, d, n_tokens // d, DSW_HEAD_DIM)
        outs.append(blk.transpose(2, 1, 0, 3).reshape(n_tokens, DSW_HEADS_PER_GROUP * DSW_HEAD_DIM))
    return outs


CONV_TILE = 512


def _shift_down(x, k, rows):
    return x if k == 0 else jnp.where(rows >= k, pltpu.roll(x, k, 0), 0.0)


def _shift_up(x, k, rows):
    n = x.shape[0]
    return x if k == 0 else jnp.where(rows < n - k, pltpu.roll(x, n - k, 0), 0.0)


def _conv_pre(x, w):
    rows = lax.broadcasted_iota(jnp.int32, x.shape, 0)
    acc = x * w[GDN_CONV - 1:GDN_CONV]
    for k in range(1, GDN_CONV):
        acc = acc + _shift_down(x, k, rows) * w[GDN_CONV - 1 - k:GDN_CONV - k]
    return acc, rows


def _conv_fwd(x, w):
    n_tokens, width = x.shape
    big = pl.BlockSpec((n_tokens, CONV_TILE), lambda j: (0, j))
    wsp = pl.BlockSpec((GDN_CONV, CONV_TILE), lambda j: (0, j))

    def body(x_ref, w_ref, o_ref):
        acc, _ = _conv_pre(x_ref[...], w_ref[...])
        o_ref[...] = acc * jax.nn.sigmoid(acc)

    return pl.pallas_call(
        body, grid=(width // CONV_TILE,), in_specs=[big, wsp], out_specs=big,
        out_shape=jax.ShapeDtypeStruct(x.shape, F32), name="conv_fwd", compiler_params=_params(1),
    )(x, w)


def _conv_bwd(x, w, dy):
    n_tokens, width = x.shape
    big = pl.BlockSpec((n_tokens, CONV_TILE), lambda j: (0, j))
    wsp = pl.BlockSpec((GDN_CONV, CONV_TILE), lambda j: (0, j))

    def body(x_ref, w_ref, dy_ref, dx_ref, dw_ref):
        xv, wv = x_ref[...], w_ref[...]
        acc, rows = _conv_pre(xv, wv)
        sg = jax.nn.sigmoid(acc)
        dacc = dy_ref[...] * (sg + acc * sg * (1.0 - sg))
        dx = dacc * wv[GDN_CONV - 1:GDN_CONV]
        for k in range(1, GDN_CONV):
            dx = dx + _shift_up(dacc, k, rows) * wv[GDN_CONV - 1 - k:GDN_CONV - k]
        dx_ref[...] = dx
        for k in range(GDN_CONV):
            dw_ref[GDN_CONV - 1 - k:GDN_CONV - k, :] = jnp.sum(dacc * _shift_down(xv, k, rows), axis=0, keepdims=True)

    return pl.pallas_call(
        body, grid=(width // CONV_TILE,), in_specs=[big, wsp, big], out_specs=(big, wsp),
        out_shape=(jax.ShapeDtypeStruct(x.shape, F32), jax.ShapeDtypeStruct(w.shape, F32)),
        name="conv_bwd", compiler_params=_params(1),
    )(x, w, dy)


@jax.custom_vjp
def conv_silu(x, w):
    return _conv_fwd(x, w)


def _conv_silu_fwd(x, w):
    return _conv_fwd(x, w), (x, w)


def _conv_silu_bwd(saved, g):
    return _conv_bwd(saved[0], saved[1], g)


conv_silu.defvjp(_conv_silu_fwd, _conv_silu_bwd)


def _dot(a, b, dn=NN):
    return lax.dot_general(a, b, dn, precision=HI, preferred_element_type=F32)


def _dot3(a, b, dn=NN):
    return lax.dot_general(a, b, dn, precision=lax.Precision.HIGH, preferred_element_type=F32)


def _bf16_dot(a, b, dn):
    return lax.dot_general(a.astype(BF16), b.astype(BF16), dn, preferred_element_type=F32)


_DOT_GRADS = {NN: (("g", "b", NT), ("a", "g", TN)), NT: (("g", "b", NN), ("g", "a", TN)),
              TN: (("b", "g", NT), ("a", "g", NN))}


def _make_bdot(dn):
    @jax.custom_vjp
    def op(a, b):
        return _bf16_dot(a, b, dn)

    def fwd(a, b):
        return op(a, b), (a, b)

    def bwd(saved, g):
        vals = dict(a=saved[0], b=saved[1], g=g)
        return tuple(_bf16_dot(vals[x], vals[y], form) for x, y, form in _DOT_GRADS[dn])

    op.defvjp(fwd, bwd)
    return op


_BDOTS = {dn: _make_bdot(dn) for dn in (NN, NT, TN)}


def _bdot(a, b, dn=NN):
    return _BDOTS[dn](a, b)


def _each(fn, *lists):
    return [fn(*items) for items in zip(*lists)]


def _gdn_chunks(q, k, v, b, gcum, state):
    c = GDN_CHUNK
    ii = lax.broadcasted_iota(jnp.int32, (c, c), 0)
    jj = lax.broadcasted_iota(jnp.int32, (c, c), 1)
    eye = (ii == jj).astype(F32)
    qn = _each(lambda x: x * lax.rsqrt(jnp.sum(x * x, axis=-1, keepdims=True) + EPS) * (GDN_HEAD_DIM ** -0.5), q)
    kn = _each(lambda x: x * lax.rsqrt(jnp.sum(x * x, axis=-1, keepdims=True) + EPS), k)
    gcum_i = _each(lambda x: jnp.broadcast_to(x, (c, c)), gcum)
    gcum_j = _each(jnp.transpose, gcum_i)
    decay = _each(lambda x, y: jnp.exp(jnp.where(jj <= ii, x - y, -1e30)), gcum_i, gcum_j)
    g_last = _each(lambda x: x[c - 1:c, :], gcum)
    e_gcum = _each(jnp.exp, gcum)
    kbeta = _each(lambda x, y: x * y, kn, b)
    vbeta = _each(lambda x, y: x * y, v, b)
    m = _each(lambda x, y, d: jnp.where(jj < ii, _bdot(x, y, NT) * d, 0.0), kbeta, kn, decay)
    inv = _each(lambda x: eye - x, m)
    power = _each(lambda x: _dot3(x, x), m)
    for step in range(5):
        inv = _each(lambda x, p: x + _dot3(x, p), inv, power)
        if step < 4:
            power = _each(lambda p: _dot3(p, p), power)
    u = _each(_dot3, inv, vbeta)
    w = _each(lambda x, y, e: _dot3(x, y * e), inv, kbeta, e_gcum)
    a_qk = _each(lambda x, y, d: _bdot(x, y, NT) * d, qn, kn, decay)
    v_new = _each(lambda x, y, s: x - _bdot(y, s), u, w, state)
    o = _each(lambda x, e, s, a, vn: _bdot(x * e, s) + _bdot(a, vn), qn, e_gcum, state, a_qk, v_new)
    new_state = _each(lambda s, gl, x, gc, vn: s * jnp.exp(gl) + _bdot(x * jnp.exp(gl - gc), vn, TN),
                      state, g_last, kn, gcum, v_new)
    return o, new_state


GDN_HEADS_PER_STEP = 4


GDN_TIME_TILE = 512


def _gdn_specs(n_tokens, reverse):
    hb, hd, tt = GDN_HEADS_PER_STEP, GDN_HEAD_DIM, GDN_TIME_TILE
    nb, nt = GDN_HEADS // hb, n_tokens // tt

    def when(t):
        return nt - 1 - t if reverse else t

    q = pl.BlockSpec((tt, hb * hd), lambda h, t: (when(t), h))
    k = pl.BlockSpec((tt, hb * hd), lambda h, t: (when(t), nb + h))
    v = pl.BlockSpec((tt, hb * hd), lambda h, t: (when(t), 2 * nb + h))
    vec = pl.BlockSpec((hb, tt, 1), lambda h, t: (h, when(t), 0))
    states = pl.BlockSpec((hb, tt // GDN_CHUNK, hd, hd), lambda h, t: (h, when(t), 0, 0))
    return q, k, v, vec, states


def _gdn_fwd(qkv, beta, g):
    n_tokens = qkv.shape[0]
    hb, hd, tt = GDN_HEADS_PER_STEP, GDN_HEAD_DIM, GDN_TIME_TILE
    n_chunks = tt // GDN_CHUNK
    q_s, k_s, v_s, vec, st = _gdn_specs(n_tokens, False)

    def body(q_ref, k_ref, v_ref, b_ref, g_ref, o_ref, st_ref, state):
        @pl.when(pl.program_id(1) == 0)
        def _():
            state[...] = jnp.zeros_like(state)

        def step(c, carry):
            r = pl.ds(pl.multiple_of(c * GDN_CHUNK, GDN_CHUNK), GDN_CHUNK)
            cols = [slice(h * hd, (h + 1) * hd) for h in range(hb)]
            old = [state[h] for h in range(hb)]
            o, new = _gdn_chunks(
                [q_ref[r, cs] for cs in cols], [k_ref[r, cs] for cs in cols], [v_ref[r, cs] for cs in cols],
                [b_ref[h, r, :] for h in range(hb)], [g_ref[h, r, :] for h in range(hb)], old)
            for h in range(hb):
                st_ref[h, c] = old[h]
                o_ref[r, cols[h]] = o[h]
                state[h] = new[h]
            return carry

        lax.fori_loop(0, n_chunks, step, 0)

    return pl.pallas_call(
        body, grid=(GDN_HEADS // hb, n_tokens // tt), in_specs=[q_s, k_s, v_s, vec, vec], out_specs=(q_s, st),
        out_shape=(jax.ShapeDtypeStruct((n_tokens, GDN_WIDTH), F32),
                   jax.ShapeDtypeStruct((GDN_HEADS, n_tokens // GDN_CHUNK, hd, hd), F32)),
        scratch_shapes=[pltpu.VMEM((hb, hd, hd), F32)],
        name="gdn_fwd", compiler_params=_params(2),
    )(qkv, qkv, qkv, beta, g)


def _gdn_bwd(qkv, beta, g, states, do):
    n_tokens = qkv.shape[0]
    hb, hd, tt = GDN_HEADS_PER_STEP, GDN_HEAD_DIM, GDN_TIME_TILE
    n_chunks = tt // GDN_CHUNK
    q_s, k_s, v_s, vec, st = _gdn_specs(n_tokens, True)

    def body(q_ref, k_ref, v_ref, b_ref, g_ref, st_ref, do_ref, dq_ref, dk_ref, dv_ref, db_ref, dg_ref, dstate):
        @pl.when(pl.program_id(1) == 0)
        def _():
            dstate[...] = jnp.zeros_like(dstate)

        def step(i, carry):
            c = n_chunks - 1 - i
            r = pl.ds(pl.multiple_of(c * GDN_CHUNK, GDN_CHUNK), GDN_CHUNK)
            cols = [slice(h * hd, (h + 1) * hd) for h in range(hb)]
            args = ([q_ref[r, cs] for cs in cols], [k_ref[r, cs] for cs in cols], [v_ref[r, cs] for cs in cols],
                    [b_ref[h, r, :] for h in range(hb)], [g_ref[h, r, :] for h in range(hb)],
                    [st_ref[h, c] for h in range(hb)])
            cts = ([do_ref[r, cs] for cs in cols], [dstate[h] for h in range(hb)])
            dq, dk, dv, db, dg, dst = jax.vjp(_gdn_chunks, *args)[1](cts)
            for h in range(hb):
                dq_ref[r, cols[h]] = dq[h]
                dk_ref[r, cols[h]] = dk[h]
                dv_ref[r, cols[h]] = dv[h]
                db_ref[h, r, :] = db[h]
                dg_ref[h, r, :] = dg[h]
                dstate[h] = dst[h]
            return carry

        lax.fori_loop(0, n_chunks, step, 0)

    wide = jax.ShapeDtypeStruct((n_tokens, GDN_WIDTH), F32)
    thin = jax.ShapeDtypeStruct(beta.shape, F32)
    dq, dk, dv, db, dg = pl.pallas_call(
        body, grid=(GDN_HEADS // hb, n_tokens // tt), in_specs=[q_s, k_s, v_s, vec, vec, st, q_s],
        out_specs=(q_s, q_s, q_s, vec, vec), out_shape=(wide, wide, wide, thin, thin),
        scratch_shapes=[pltpu.VMEM((hb, hd, hd), F32)],
        name="gdn_bwd", compiler_params=_params(2),
    )(qkv, qkv, qkv, beta, g, states, do)
    return jnp.concatenate([dq, dk, dv], axis=1), db, dg


@jax.custom_vjp
def gated_delta(qkv, beta, g):
    return _gdn_fwd(qkv, beta, g)[0]


def _gated_delta_fwd(qkv, beta, g):
    o, states = _gdn_fwd(qkv, beta, g)
    return o, (qkv, beta, g, states)


def _gated_delta_bwd(saved, do):
    return _gdn_bwd(*saved, do)


gated_delta.defvjp(_gated_delta_fwd, _gated_delta_bwd)


FFN_ROW_TILE = 256


def _resident(shape):
    return pl.BlockSpec(shape, lambda i: (0,) * len(shape), pipeline_mode=pl.Buffered(1))


def _ffn_fwd(x, gain, wg, wu, wd, name):
    n_tokens, d = x.shape
    n_shards, _, n = wg.shape
    tm = FFN_ROW_TILE

    def body(x_ref, gain_ref, wg_ref, wu_ref, wd_ref, o_ref, g_ref, u_ref):
        xv = x_ref[...]
        h = (xv * lax.rsqrt(jnp.mean(xv * xv, axis=-1, keepdims=True) + EPS) * gain_ref[...]).astype(BF16)
        acc = jnp.zeros((tm, d), F32)
        for j in range(n_shards):
            g = lax.dot_general(h, wg_ref[j], NN, preferred_element_type=F32)
            u = lax.dot_general(h, wu_ref[j], NN, preferred_element_type=F32)
            g_ref[j] = g
            u_ref[j] = u
            a = (g * jax.nn.sigmoid(g) * u).astype(BF16)
            acc = acc + lax.dot_general(a, wd_ref[j], NN, preferred_element_type=F32)
        o_ref[...] = xv + 0.5 * acc

    row = pl.BlockSpec((tm, d), lambda i: (i, 0))
    hid = pl.BlockSpec((n_shards, tm, n), lambda i: (0, i, 0))
    return pl.pallas_call(
        body, grid=(n_tokens // tm,),
        in_specs=[row, _resident(gain.shape), _resident(wg.shape), _resident(wu.shape), _resident(wd.shape)],
        out_specs=(row, hid, hid),
        out_shape=(jax.ShapeDtypeStruct(x.shape, F32), jax.ShapeDtypeStruct((n_shards, n_tokens, n), F32),
                   jax.ShapeDtypeStruct((n_shards, n_tokens, n), F32)),
        name=name, compiler_params=_params(1),
    )(x, gain, wg, wu, wd)


def _ffn_bwd_rows(x, gain, dy, g, u, wg, wu, wd, name):
    n_tokens, d = x.shape
    n_shards, _, n = wg.shape
    tm = FFN_ROW_TILE

    def body(x_ref, gain_ref, dy_ref, g_ref, u_ref, wg_ref, wu_ref, wd_ref,
             dx_ref, dgain_ref, h_ref, dyh_ref, a_ref, dg_ref, du_ref):
        xv, dyv, gain_v = x_ref[...], dy_ref[...], gain_ref[...]
        r = lax.rsqrt(jnp.mean(xv * xv, axis=-1, keepdims=True) + EPS)
        xhat = xv * r
        h_ref[...] = (xhat * gain_v).astype(BF16)
        dyh = (0.5 * dyv).astype(BF16)
        dyh_ref[...] = dyh
        dh = jnp.zeros((tm, d), F32)
        for j in range(n_shards):
            da = lax.dot_general(dyh, wd_ref[j], NT, preferred_element_type=F32)
            gv, uv = g_ref[j], u_ref[j]
            sg = jax.nn.sigmoid(gv)
            silu = gv * sg
            a_ref[j] = (silu * uv).astype(BF16)
            dg = (da * uv * (sg + silu * (1.0 - sg))).astype(BF16)
            du = (da * silu).astype(BF16)
            dg_ref[j] = dg
            du_ref[j] = du
            dh = dh + lax.dot_general(dg, wg_ref[j], NT, preferred_element_type=F32)
            dh = dh + lax.dot_general(du, wu_ref[j], NT, preferred_element_type=F32)
        dxhat = dh * gain_v
        dx_ref[...] = dyv + r * (dxhat - xhat * jnp.mean(dxhat * xhat, axis=-1, keepdims=True))

        @pl.when(pl.program_id(0) == 0)
        def _():
            dgain_ref[...] = jnp.zeros_like(dgain_ref)

        dgain_ref[...] += jnp.sum(dh * xhat, axis=0, keepdims=True)

    row = pl.BlockSpec((tm, d), lambda i: (i, 0))
    hid = pl.BlockSpec((n_shards, tm, n), lambda i: (0, i, 0))
    hid_shape = (n_shards, n_tokens, n)
    return pl.pallas_call(
        body, grid=(n_tokens // tm,),
        in_specs=[row, _resident(gain.shape), row, hid, hid, _resident(wg.shape), _resident(wu.shape),
                  _resident(wd.shape)],
        out_specs=(row, pl.BlockSpec(gain.shape, lambda i: (0, 0)), row, row, hid, hid, hid),
        out_shape=(jax.ShapeDtypeStruct(x.shape, F32), jax.ShapeDtypeStruct(gain.shape, F32),
                   jax.ShapeDtypeStruct(x.shape, BF16), jax.ShapeDtypeStruct(x.shape, BF16),
                   jax.ShapeDtypeStruct(hid_shape, BF16), jax.ShapeDtypeStruct(hid_shape, BF16),
                   jax.ShapeDtypeStruct(hid_shape, BF16)),
        name=name, compiler_params=_params(1),
    )(x, gain, dy, g, u, wg, wu, wd)


def _ffn_bwd_weights(h, dyh, a, dg, du, name):
    n_shards, n_tokens, n = a.shape
    d = h.shape[1]

    def body(h_ref, dyh_ref, a_ref, dg_ref, du_ref, dwg_ref, dwu_ref, dwd_ref):
        hv = h_ref[...]
        dwg_ref[0] = lax.dot_general(hv, dg_ref[0], TN, preferred_element_type=F32)
        dwu_ref[0] = lax.dot_general(hv, du_ref[0], TN, preferred_element_type=F32)
        dwd_ref[0] = lax.dot_general(a_ref[0], dyh_ref[...], TN, preferred_element_type=F32)

    hid = pl.BlockSpec((1, n_tokens, n), lambda j: (j, 0, 0))
    return pl.pallas_call(
        body, grid=(n_shards,), in_specs=[_resident(h.shape), _resident(dyh.shape), hid, hid, hid],
        out_specs=(pl.BlockSpec((1, d, n), lambda j: (j, 0, 0)), pl.BlockSpec((1, d, n), lambda j: (j, 0, 0)),
                   pl.BlockSpec((1, n, d), lambda j: (j, 0, 0))),
        out_shape=(jax.ShapeDtypeStruct((n_shards, d, n), F32), jax.ShapeDtypeStruct((n_shards, d, n), F32),
                   jax.ShapeDtypeStruct((n_shards, n, d), F32)),
        name=name, compiler_params=_params(1),
    )(h, dyh, a, dg, du)


def _ffn(x, gain, w, slots, tag):
    names = [tag + "_w_gate", tag + "_w_up", tag + "_w_down"]

    @jax.custom_vjp
    def op(x, gain, weights, slot):
        return _ffn_fwd(x, gain, *weights, tag + "_fwd")[0]

    def fwd(x, gain, weights, slot):
        out, g, u = _ffn_fwd(x, gain, *weights, tag + "_fwd")
        return out, (x, gain, weights, g, u)

    def bwd(saved, dy):
        x, gain, weights, g, u = saved
        dx, dgain, h, dyh, a, dg, du = _ffn_bwd_rows(x, gain, dy, g, u, *weights, tag + "_bwd_rows")
        return dx, dgain, None, _ffn_bwd_weights(h, dyh, a, dg, du, tag + "_bwd_weights")

    op.defvjp(fwd, bwd)
    return op(x, gain, tuple(w[n] for n in names), tuple(slots[n] for n in names))


def _local_loss(diff, x_target, w):
    x, slots, small = diff
    target = x_target
    n_tokens = x.shape[0]
    x1 = _ffn(x, small["ffn1_norm"], w, slots, "ffn1")

    h = rowwise(_rms_fn, "mix_norm", (x1,), params=(small["mix_norm"],))[0]
    proj = {n: mm(h, w[n], slots[n], "in_" + n)
            for n in ("wq_a", "wk_a", "wv_a", "w_qkvb", "w_small", "w_ggate", "w_gatea", "w_gateb")}

    cos, sin = _rope_tables(n_tokens)
    q = _to_heads(rope(proj["wq_a"], cos, sin, "rope_q"))
    k = _to_heads(rope(proj["wk_a"], cos, sin, "rope_k"))
    v = _to_heads(proj["wv_a"])
    o, lse = attention(q, k, v)
    ya = rowwise(_combine_fn, "combine", tuple(_from_heads(o)) + tuple(_from_heads(lse)), tm=512)[0]
    pa = mm(ya, w["w_branch_a"], slots["w_branch_a"], "branch_a")

    qkv = conv_silu(proj["w_qkvb"], small["gdn_conv_w"])
    beta, g = rowwise(_beta_decay_fn, "beta_decay",
                      (proj["w_small"][:, :GDN_HEADS], proj["w_small"][:, GDN_HEADS:2 * GDN_HEADS]),
                      params=(small["gdn_a_log"], small["gdn_dt_bias"]), tm=512)
    ob = gated_delta(qkv, beta.T[:, :, None], g.T[:, :, None])
    yb = rowwise(_outnorm_gate_fn, "outnorm_gate", (ob, proj["w_ggate"]), params=(small["gdn_out_norm"],),
                 tm=512, nblk=GDN_HEADS)[0]
    pb = mm(yb, w["w_branch_b"], slots["w_branch_b"], "branch_b")

    merged = rowwise(_merge_fn, "merge", (proj["w_gatea"], proj["w_gateb"], pa, pb))[0]
    x2 = mm_res(merged, w["w_out"], slots["w_out"], x1, 1.0, "out")

    x3 = _ffn(x2, small["ffn2_norm"], w, slots, "ffn2")
    row_loss = rowwise(_loss_fn, "loss", (x3,), consts=(target,), params=(small["final_norm"],))[0]
    return jnp.sum(row_loss)


SHARDS = (
    ("ffn1_w_gate", 1024, 352), ("ffn1_w_up", 1024, 352), ("ffn1_w_down", 352, 1024), ("w_in", 1024, 1058),
    ("w_branch_a", 256, 128), ("w_branch_b", 128, 1024), ("w_out", 128, 1024),
    ("ffn2_w_gate", 1024, 352), ("ffn2_w_up", 1024, 352), ("ffn2_w_down", 352, 1024),
)
IN_SHARD = D_IN // N_DEV
CONV_SHARD = (GDN_CONV, 3 * GDN_WIDTH // N_DEV)
SMALL_ROWS = 24
ANY = pl.BlockSpec(memory_space=pl.ANY)


def _position():
    return lax.axis_index("x"), lax.axis_index("y"), lax.axis_index("c")


def all_gather_shards(shards, name):
    n = len(shards)

    def body(*refs):
        x_refs, out_refs = refs[:n], refs[n:2 * n]
        send_sems, recv_sems, local_sems = refs[2 * n:]
        x, y, c = _position()
        me, sibling = (x, y, c), (x, y, 1 - c)
        chips = [(1 - x, y), (x, 1 - y), (1 - x, 1 - y)]

        def slab(a, px, py, pc):
            return out_refs[a].at[4 * px + 2 * py + pc]

        def copy(a, k, block, to, src=None):
            return pltpu.make_async_remote_copy(
                src_ref=slab(a, *block) if src is None else src, dst_ref=slab(a, *block),
                send_sem=send_sems.at[7 * a + k], recv_sem=recv_sems.at[7 * a + k], device_id=to, device_id_type=MESH)

        mine = [pltpu.make_async_copy(x_refs[a], slab(a, *me), local_sems.at[a]) for a in range(n)]
        for cp in mine:
            cp.start()
        first = []
        for j, chip in enumerate(chips):
            first += [copy(a, 1 + j, me, (*chip, c), src=x_refs[a]) for a in range(n)]
        first += [copy(a, 0, me, sibling, src=x_refs[a]) for a in range(n)]
        for cp in first:
            cp.start()
        passed = []
        for j, chip in enumerate(chips):
            for a in range(n):
                copy(a, 1 + j, (*chip, c), me).wait_recv()
                cp = copy(a, 4 + j, (*chip, c), sibling)
                cp.start()
                passed.append(cp)
        for a in range(n):
            copy(a, 0, sibling, me).wait_recv()
        for j, chip in enumerate(chips):
            for a in range(n):
                copy(a, 4 + j, (*chip, 1 - c), me).wait_recv()
        for cp in first + passed:
            cp.wait_send()
        for cp in mine:
            cp.wait()

    return pl.pallas_call(
        body, out_shape=tuple(jax.ShapeDtypeStruct((N_DEV,) + s.shape, s.dtype) for s in shards),
        in_specs=[ANY] * n, out_specs=(ANY,) * n,
        scratch_shapes=[pltpu.SemaphoreType.DMA((7 * n,)), pltpu.SemaphoreType.DMA((7 * n,)),
                        pltpu.SemaphoreType.DMA((n,))],
        name=name,
    )(*shards)


def exchange_with_sibling(grads):
    n = len(grads)

    def body(*refs):
        g_refs, recv_refs = refs[:n], refs[n:2 * n]
        send_sems, recv_sems = refs[2 * n:]
        x, y, c = _position()
        copies = [pltpu.make_async_remote_copy(
            src_ref=g_refs[a].at[2 * k + 1 - c], dst_ref=recv_refs[a].at[k], send_sem=send_sems.at[4 * a + k],
            recv_sem=recv_sems.at[4 * a + k], device_id=(x, y, 1 - c), device_id_type=MESH)
            for k in range(4) for a in range(n)]
        for cp in copies:
            cp.start()
        for cp in copies:
            cp.wait()

    return pl.pallas_call(
        body, out_shape=tuple(jax.ShapeDtypeStruct((4,) + g.shape[1:], g.dtype) for g in grads),
        in_specs=[ANY] * n, out_specs=(ANY,) * n,
        scratch_shapes=[pltpu.SemaphoreType.DMA((4 * n,)), pltpu.SemaphoreType.DMA((4 * n,))], name="rs_sibling",
    )(*grads)


def _row_tile(rows):
    return 256 if rows % 256 == 0 else rows


def add_sibling(grads, received, core, name):
    _, rows, width = grads.shape
    tr = _row_tile(rows)

    def body(c_ref, g_ref, r_ref, o_ref):
        o_ref[...] = (g_ref[...] + r_ref[...]).astype(BF16)

    blk = (1, tr, width)
    return pl.pallas_call(
        body,
        grid_spec=pltpu.PrefetchScalarGridSpec(
            num_scalar_prefetch=1, grid=(4, rows // tr),
            in_specs=[pl.BlockSpec(blk, lambda k, i, c_ref: (2 * k + c_ref[0], i, 0)),
                      pl.BlockSpec(blk, lambda k, i, c_ref: (k, i, 0))],
            out_specs=pl.BlockSpec(blk, lambda k, i, c_ref: (k, i, 0))),
        out_shape=jax.ShapeDtypeStruct((4, rows, width), BF16), name=name, compiler_params=_params(2),
    )(core, grads, received)


def exchange_with_chips(partials):
    n = len(partials)

    def body(*refs):
        p_refs, recv_refs = refs[:n], refs[n:2 * n]
        send_sems, recv_sems = refs[2 * n:]
        x, y, c = _position()
        chips = [(1 - x, y), (x, 1 - y), (1 - x, 1 - y)]
        copies = [pltpu.make_async_remote_copy(
            src_ref=p_refs[a].at[2 * cx + cy], dst_ref=recv_refs[a].at[j], send_sem=send_sems.at[3 * a + j],
            recv_sem=recv_sems.at[3 * a + j], device_id=(cx, cy, c), device_id_type=MESH)
            for a in range(n) for j, (cx, cy) in enumerate(chips)]
        for cp in copies:
            cp.start()
        for cp in copies:
            cp.wait()

    return pl.pallas_call(
        body, out_shape=tuple(jax.ShapeDtypeStruct((3,) + p.shape[1:], p.dtype) for p in partials),
        in_specs=[ANY] * n, out_specs=(ANY,) * n,
        scratch_shapes=[pltpu.SemaphoreType.DMA((3 * n,)), pltpu.SemaphoreType.DMA((3 * n,))], name="rs_chips",
    )(*partials)


def all_reduce_small(vals):
    rows, width = vals.shape

    def body(x_ref, out_ref, all_ref, send_sems, recv_sems):
        x, y, c = _position()
        me, sibling = (x, y, c), (x, y, 1 - c)
        chips = [(1 - x, y), (x, 1 - y), (1 - x, 1 - y)]

        def slab(px, py, pc):
            return all_ref.at[4 * px + 2 * py + pc]

        def copy(k, block, to, src=None):
            return pltpu.make_async_remote_copy(
                src_ref=slab(*block) if src is None else src, dst_ref=slab(*block),
                send_sem=send_sems.at[k], recv_sem=recv_sems.at[k], device_id=to, device_id_type=MESH)

        first = [copy(0, me, sibling, src=x_ref)]
        first += [copy(1 + j, me, (*chip, c), src=x_ref) for j, chip in enumerate(chips)]
        for cp in first:
            cp.start()
        all_ref[4 * x + 2 * y + c] = x_ref[...]
        passed = [copy(4 + j, (*chip, c), sibling) for j, chip in enumerate(chips)]
        for j, chip in enumerate(chips):
            copy(1 + j, (*chip, c), me).wait_recv()
            passed[j].start()
        copy(0, sibling, me).wait_recv()
        for j, chip in enumerate(chips):
            copy(4 + j, (*chip, 1 - c), me).wait_recv()
        for cp in first + passed:
            cp.wait_send()
        total = all_ref[0]
        for d in range(1, N_DEV):
            total = total + all_ref[d]
        out_ref[...] = total

    vmem = pl.BlockSpec(memory_space=pltpu.VMEM)
    return pl.pallas_call(
        body, out_shape=(jax.ShapeDtypeStruct(vals.shape, F32), jax.ShapeDtypeStruct((N_DEV, rows, width), F32)),
        in_specs=[vmem], out_specs=(vmem, vmem),
        scratch_shapes=[pltpu.SemaphoreType.DMA((7,)), pltpu.SemaphoreType.DMA((7,))], name="small_allreduce",
    )(vals)[0]


def adamw(w, g, m, v, name):
    shape = w.shape
    w2, g2, m2, v2 = [a.reshape((-1, shape[-1])) for a in (w, g, m, v)]
    rows, cols = w2.shape
    tr = 256 if rows % 256 == 0 else rows

    def body(w_ref, g_ref, m_ref, v_ref, d_ref, nm_ref, nv_ref):
        gv = g_ref[...]
        nm = ADAM_B1 * m_ref[...] + (1.0 - ADAM_B1) * gv
        nv = ADAM_B2 * v_ref[...] + (1.0 - ADAM_B2) * (gv * gv)
        m_hat = nm / (1.0 - ADAM_B1 ** ADAM_STEP)
        v_hat = nv / (1.0 - ADAM_B2 ** ADAM_STEP)
        d_ref[...] = -ADAM_LR * (m_hat / (jnp.sqrt(v_hat) + ADAM_EPS) + ADAM_WD * w_ref[...])
        nm_ref[...] = nm
        nv_ref[...] = nv

    blk = pl.BlockSpec((tr, cols), lambda i: (i, 0))
    out = jax.ShapeDtypeStruct((rows, cols), F32)
    outs = pl.pallas_call(
        body, grid=(rows // tr,), in_specs=[blk] * 4, out_specs=(blk,) * 3, out_shape=(out,) * 3,
        name=name, compiler_params=_params(1),
    )(w2, g2, m2, v2)
    return tuple(o.reshape(shape) for o in outs)


def adamw_summed(w, m, v, grads, from_sibling, received, me, name):
    shape = w.shape
    rows, cols = shape[-2:]
    w3, m3, v3 = [a.reshape((1, rows, cols)) for a in (w, m, v)]
    tr = _row_tile(rows)

    def body(me_ref, w_ref, m_ref, v_ref, own_ref, sib_ref, r_ref, g_ref, d_ref, nm_ref, nv_ref):
        gv = own_ref[0] + sib_ref[0]
        for j in range(3):
            gv = gv + r_ref[j].astype(F32)
        nm = ADAM_B1 * m_ref[0] + (1.0 - ADAM_B1) * gv
        nv = ADAM_B2 * v_ref[0] + (1.0 - ADAM_B2) * (gv * gv)
        m_hat = nm / (1.0 - ADAM_B1 ** ADAM_STEP)
        v_hat = nv / (1.0 - ADAM_B2 ** ADAM_STEP)
        g_ref[0] = gv
        d_ref[0] = -ADAM_LR * (m_hat / (jnp.sqrt(v_hat) + ADAM_EPS) + ADAM_WD * w_ref[0])
        nm_ref[0] = nm
        nv_ref[0] = nv

    one = pl.BlockSpec((1, tr, cols), lambda i, me_ref: (0, i, 0))
    out = jax.ShapeDtypeStruct((1, rows, cols), F32)
    outs = pl.pallas_call(
        body,
        grid_spec=pltpu.PrefetchScalarGridSpec(
            num_scalar_prefetch=1, grid=(rows // tr,),
            in_specs=[one, one, one, pl.BlockSpec((1, tr, cols), lambda i, me_ref: (me_ref[0], i, 0)),
                      pl.BlockSpec((1, tr, cols), lambda i, me_ref: (me_ref[1], i, 0)),
                      pl.BlockSpec((3, tr, cols), lambda i, me_ref: (0, i, 0))],
            out_specs=(one,) * 4),
        out_shape=(out,) * 4, name=name, compiler_params=_params(1),
    )(me, w3, m3, v3, grads, from_sibling, received)
    return tuple(o.reshape(shape) for o in outs)


IN_PIECES = (("wq_a", 0, 768), ("wk_a", 768, 1536), ("wv_a", 1536, 2304), ("w_qkvb", 2304, 5376),
             ("w_small", 5376, 5392), ("w_ggate", 5392, 6416), ("w_gatea", 6416, 7440), ("w_gateb", 7440, 8464))
IN_ROW_TILE = 128


def _piece_width(lo, hi):
    return max(hi - lo, LANES)


def _piece_segments(lo, hi):
    out = []
    for j in range(N_DEV):
        a, b = max(lo, IN_SHARD * j), min(hi, IN_SHARD * (j + 1))
        if a < b:
            out.append((j, a - IN_SHARD * j, b - IN_SHARD * j, a - lo, b - lo))
    return out


def split_in_shards(gathered):
    rows = gathered.shape[1]

    def body(g_ref, *outs):
        for (name, lo, hi), o_ref in zip(IN_PIECES, outs):
            if hi - lo < LANES:
                o_ref[...] = jnp.zeros_like(o_ref)
            for j, s0, s1, d0, d1 in _piece_segments(lo, hi):
                o_ref[:, d0:d1] = g_ref[j, :, s0:s1]

    widths = [_piece_width(lo, hi) for _, lo, hi in IN_PIECES]
    outs = pl.pallas_call(
        body, grid=(rows // IN_ROW_TILE,),
        in_specs=[pl.BlockSpec((N_DEV, IN_ROW_TILE, IN_SHARD), lambda i: (0, i, 0))],
        out_specs=tuple(pl.BlockSpec((IN_ROW_TILE, wd), lambda i: (i, 0)) for wd in widths),
        out_shape=tuple(jax.ShapeDtypeStruct((rows, wd), gathered.dtype) for wd in widths),
        name="split_in_shards", compiler_params=_params(1),
    )(gathered)
    return {name: o for (name, _, _), o in zip(IN_PIECES, outs)}


def join_in_grads(grads):
    pieces = [grads[name] for name, _, _ in IN_PIECES]
    rows = pieces[0].shape[0]

    def body(*refs):
        o_ref = refs[-1]
        for (name, lo, hi), p_ref in zip(IN_PIECES, refs[:-1]):
            for j, s0, s1, d0, d1 in _piece_segments(lo, hi):
                o_ref[j, :, s0:s1] = p_ref[:, d0:d1]

    return pl.pallas_call(
        body, grid=(rows // IN_ROW_TILE,),
        in_specs=[pl.BlockSpec((IN_ROW_TILE, p.shape[1]), lambda i: (i, 0)) for p in pieces],
        out_specs=pl.BlockSpec((N_DEV, IN_ROW_TILE, IN_SHARD), lambda i: (0, i, 0)),
        out_shape=jax.ShapeDtypeStruct((N_DEV, rows, IN_SHARD), F32), name="join_in_grads", compiler_params=_params(1),
    )(*pieces)


SMALL_VECTORS = ("ffn1_norm", "mix_norm", "ffn2_norm", "final_norm")


def _pack_small(gs):
    row = jnp.concatenate([gs["gdn_a_log"].reshape(-1), gs["gdn_dt_bias"].reshape(-1), gs["gdn_out_norm"].reshape(-1)])
    rows = [gs[n].reshape(1, D_MODEL) for n in SMALL_VECTORS]
    rows.append(jnp.pad(row, (0, D_MODEL - row.shape[0])).reshape(1, D_MODEL))
    rows.append(gs["gdn_conv_w"].reshape(-1, D_MODEL))
    packed = jnp.concatenate(rows, axis=0)
    return jnp.pad(packed, ((0, SMALL_ROWS - packed.shape[0]), (0, 0)))


def _unpack_small(packed):
    out = {n: packed[i].reshape(1, D_MODEL) for i, n in enumerate(SMALL_VECTORS)}
    row = packed[len(SMALL_VECTORS)]
    out["gdn_a_log"] = row[:GDN_HEADS].reshape(1, GDN_HEADS)
    out["gdn_dt_bias"] = row[GDN_HEADS:2 * GDN_HEADS].reshape(1, GDN_HEADS)
    out["gdn_out_norm"] = row[2 * GDN_HEADS:2 * GDN_HEADS + GDN_HEAD_DIM].reshape(1, GDN_HEAD_DIM)
    first = len(SMALL_VECTORS) + 1
    out["gdn_conv_w"] = packed[first:first + GDN_CONV * 3].reshape(GDN_CONV, 3 * GDN_WIDTH)
    return out


WEIGHTS = ("ffn1_norm", "ffn1_w_gate", "ffn1_w_up", "ffn1_w_down", "mix_norm", "w_in", "gdn_conv_w", "gdn_a_log",
           "gdn_dt_bias", "gdn_out_norm", "w_branch_a", "w_branch_b", "w_out", "ffn2_norm", "ffn2_w_gate",
           "ffn2_w_up", "ffn2_w_down", "final_norm")


def kernel(x, ffn1_norm, ffn1_w_gate, ffn1_w_up, ffn1_w_down, mix_norm, w_in, gdn_conv_w, gdn_a_log, gdn_dt_bias, gdn_out_norm, w_branch_a, w_branch_b, w_out, ffn2_norm, ffn2_w_gate, ffn2_w_up, ffn2_w_down, final_norm, loss_target, m_ffn1_norm, m_ffn1_w_gate, m_ffn1_w_up, m_ffn1_w_down, m_mix_norm, m_w_in, m_gdn_conv_w, m_gdn_a_log, m_gdn_dt_bias, m_gdn_out_norm, m_w_branch_a, m_w_branch_b, m_w_out, m_ffn2_norm, m_ffn2_w_gate, m_ffn2_w_up, m_ffn2_w_down, m_final_norm, v_ffn1_norm, v_ffn1_w_gate, v_ffn1_w_up, v_ffn1_w_down, v_mix_norm, v_w_in, v_gdn_conv_w, v_gdn_a_log, v_gdn_dt_bias, v_gdn_out_norm, v_w_branch_a, v_w_branch_b, v_w_out, v_ffn2_norm, v_ffn2_w_gate, v_ffn2_w_up, v_ffn2_w_down, v_final_norm):
    given = dict(locals())
    px, py, pc = _position()
    big_names = [n for n, _, _ in SHARDS]

    shards = [given[n][0].astype(BF16) for n in big_names] + [gdn_conv_w[0]]
    gathered = dict(zip(big_names + ["gdn_conv_w"], all_gather_shards(shards, "gather_weights")))
    w = {n: gathered[n] for n in big_names if n.startswith("ffn")}
    w.update(split_in_shards(gathered["w_in"]))
    w["w_branch_a"] = gathered["w_branch_a"].transpose(1, 0, 2).reshape(256, D_MODEL)
    w["w_branch_b"] = gathered["w_branch_b"].reshape(D_MODEL, D_MODEL)
    w["w_out"] = gathered["w_out"].reshape(D_MODEL, D_MODEL)
    conv_full = gathered["gdn_conv_w"].transpose(1, 0, 2).reshape(GDN_CONV, 3 * GDN_WIDTH)
    slots = {n: jnp.zeros(a.shape, F32) for n, a in w.items()}
    small = dict(ffn1_norm=ffn1_norm, mix_norm=mix_norm, ffn2_norm=ffn2_norm, final_norm=final_norm.reshape(1, D_MODEL),
                 gdn_a_log=gdn_a_log, gdn_dt_bias=gdn_dt_bias, gdn_out_norm=gdn_out_norm, gdn_conv_w=conv_full)

    loss_local, (grad_x, g_w, g_small) = jax.value_and_grad(_local_loss)((x[0], slots, small), loss_target[0], w)
    loss = lax.psum(loss_local, ("x", "y", "c"))

    g_big = {n: g_w[n] for n in big_names if n.startswith("ffn")}
    g_big["w_in"] = join_in_grads(g_w)
    g_big["w_branch_a"] = g_w["w_branch_a"].reshape(256, N_DEV, 128).transpose(1, 0, 2)
    g_big["w_branch_b"] = g_w["w_branch_b"].reshape(N_DEV, 128, D_MODEL)
    g_big["w_out"] = g_w["w_out"].reshape(N_DEV, 128, D_MODEL)
    g_list = [g_big[n] for n in big_names]
    core = pc.astype(jnp.int32).reshape(1)
    me = 4 * px + 2 * py + pc
    me_and_chip = jnp.stack([me, 2 * px + py]).astype(jnp.int32)
    from_sibling = exchange_with_sibling(g_list)
    partials = [add_sibling(g, r, core, "rs_add_" + n) for n, g, r in zip(big_names, g_list, from_sibling)]
    from_chips = exchange_with_chips(partials)

    results = {}
    for n, g, sib, recv in zip(big_names, g_list, from_sibling, from_chips):
        results[n] = adamw_summed(given[n], given["m_" + n], given["v_" + n], g, sib, recv, me_and_chip, "adamw_" + n)

    small_sum = _unpack_small(all_reduce_small(_pack_small(g_small)))
    conv_cols = CONV_SHARD[1]
    small_sum["gdn_conv_w"] = lax.dynamic_slice(small_sum["gdn_conv_w"], (0, me * conv_cols), (GDN_CONV, conv_cols))
    for n in WEIGHTS:
        if n not in results:
            g = small_sum[n].reshape(given[n].shape)
            results[n] = (g,) + adamw(given[n], g, given["m_" + n], given["v_" + n], "adamw_" + n)

    outs = [[results[n][i] for n in WEIGHTS] for i in range(4)]
    return (loss, grad_x[None], *outs[0], *outs[1], *outs[2], *outs[3])
```

```python
import jax
import jax.numpy as jnp
from jax import lax
from jax.experimental import pallas as pl
from jax.experimental.pallas import tpu as pltpu

F32 = jnp.float32
BF16 = jnp.bfloat16
HI = lax.Precision.HIGHEST
MESH = pl.DeviceIdType.MESH

N_DEV = 8
D_MODEL = 1024
D_FF = 2816
EPS = 1e-6
ROPE_THETA = 10000.0
DSW_DILATIONS = (1, 4, 16)
DSW_HEADS_PER_GROUP = 4
DSW_HEAD_DIM = 64
DSW_BLOCK = 128
DSW_WIDTH = 768
N_DSW_HEADS = 12
GDN_HEADS = 8
GDN_HEAD_DIM = 128
GDN_WIDTH = 1024
GDN_CONV = 4
GDN_CHUNK = 64
IN_OFFSETS = dict(qa=0, ka=768, va=1536, qkvb=2304, small=5376, ggate=5392, gatea=6416, gateb=7440)
D_IN = 8464

ADAM_LR = 0.001
ADAM_B1 = 0.9
ADAM_B2 = 0.999
ADAM_EPS = 1e-08
ADAM_WD = 0.01
ADAM_STEP = 10

VMEM_LIMIT_BYTES = 56 * 1024 * 1024
LANES = 128

NN = (((1,), (0,)), ((), ()))
NT = (((1,), (1,)), ((), ()))
TN = (((0,), (0,)), ((), ()))


def _params(n_grid):
    return pltpu.CompilerParams(dimension_semantics=("arbitrary",) * n_grid, vmem_limit_bytes=VMEM_LIMIT_BYTES)


def _tile(n, pref):
    best = None
    t = LANES
    while t <= min(n, pref):
        if n % t == 0:
            best = t
        t += LANES
    return n if best is None else best


def _matmul(a, b, *, name, ta=False, tb=False, res=None, scale=1.0):
    K, M = a.shape if ta else a.shape[::-1]
    N = b.shape[0] if tb else b.shape[1]
    assert (b.shape[1] if tb else b.shape[0]) == K, (a.shape, b.shape, ta, tb)
    tm = _tile(M, 512)
    tn = _tile(N, 512)
    dn = (((0 if ta else 1,), (1 if tb else 0,)), ((), ()))

    def body(*refs):
        a_ref, b_ref = refs[:2]
        o_ref = refs[-1]
        acc = lax.dot_general(a_ref[...].astype(BF16), b_ref[...].astype(BF16), dn, preferred_element_type=F32)
        if scale != 1.0:
            acc = acc * scale
        if res is not None:
            acc = refs[2][...] + acc
        o_ref[...] = acc

    a_spec = pl.BlockSpec((K, tm), lambda i, j: (0, i)) if ta else pl.BlockSpec((tm, K), lambda i, j: (i, 0))
    b_spec = pl.BlockSpec((tn, K), lambda i, j: (j, 0)) if tb else pl.BlockSpec((K, tn), lambda i, j: (0, j))
    o_spec = pl.BlockSpec((tm, tn), lambda i, j: (i, j))
    ins, specs = [a, b], [a_spec, b_spec]
    if res is not None:
        ins.append(res)
        specs.append(o_spec)
    return pl.pallas_call(
        body, grid=(M // tm, N // tn), in_specs=specs, out_specs=o_spec,
        out_shape=jax.ShapeDtypeStruct((M, N), F32), name=name, compiler_params=_params(2),
    )(*ins)


def _make_mm(name, scale=1.0, with_res=False):
    @jax.custom_vjp
    def op(a, w, slot, res):
        return _matmul(a, w, name=name, res=res if with_res else None, scale=scale)

    def fwd(a, w, slot, res):
        return op(a, w, slot, res), (a, w)

    def bwd(saved, g):
        a, w = saved
        da = _matmul(g, w, name=name + "_da", tb=True, scale=scale)
        dw = _matmul(a, g, name=name + "_dw", ta=True, scale=scale)
        return da, None, dw, (g if with_res else None)

    op.defvjp(fwd, bwd)
    return op


def mm(a, w, slot, name):
    return _make_mm(name)(a, w, slot, None)


def mm_res(a, w, slot, res, scale, name):
    return _make_mm(name, scale=scale, with_res=True)(a, w, slot, res)


def _rw_specs(arrs, tm, nblk):
    return [pl.BlockSpec((tm, a.shape[1] // nblk), lambda i, j: (i, j)) for a in arrs]


def _rowwise_fwd(fn, name, rows, consts, params, tm, nblk):
    n_rows = rows[0].shape[0]
    tm = min(tm, n_rows)
    ins = list(rows) + list(consts)
    avals = [jax.ShapeDtypeStruct((tm, a.shape[1] // nblk), a.dtype) for a in ins]
    avals += [jax.ShapeDtypeStruct(p.shape, p.dtype) for p in params]
    out_avals = jax.eval_shape(fn, *avals)
    n_in = len(ins) + len(params)

    def body(*refs):
        outs = fn(*[r[...] for r in refs[:n_in]])
        for r, o in zip(refs[n_in:], outs):
            r[...] = o.astype(r.dtype)

    return pl.pallas_call(
        body, grid=(n_rows // tm, nblk),
        in_specs=_rw_specs(ins, tm, nblk) + [pl.BlockSpec(p.shape, lambda i, j: (0, 0)) for p in params],
        out_specs=tuple(pl.BlockSpec((tm, o.shape[1]), lambda i, j: (i, j)) for o in out_avals),
        out_shape=tuple(jax.ShapeDtypeStruct((n_rows, o.shape[1] * nblk), o.dtype) for o in out_avals),
        name=name, compiler_params=_params(2),
    )(*ins, *params)


def _rowwise_bwd(fn, name, rows, consts, params, cts, tm, nblk):
    n_rows = rows[0].shape[0]
    tm = min(tm, n_rows)
    nr, nc, npar, nct = len(rows), len(consts), len(params), len(cts)

    def body(*refs):
        rv = [r[...] for r in refs[:nr]]
        cv = [r[...] for r in refs[nr:nr + nc]]
        pv = [r[...] for r in refs[nr + nc:nr + nc + npar]]
        ctv = [r[...] for r in refs[nr + nc + npar:nr + nc + npar + nct]]
        outs = refs[nr + nc + npar + nct:]
        _, vjp = jax.vjp(lambda *d: fn(*d[:nr], *cv, *d[nr:]), *rv, *pv)
        grads = vjp(tuple(ctv))
        for k in range(nr):
            outs[k][...] = grads[k]
        first = jnp.logical_and(pl.program_id(0) == 0, pl.program_id(1) == 0)
        for k in range(npar):
            ref = outs[nr + k]

            @pl.when(first)
            def _(ref=ref):
                ref[...] = jnp.zeros_like(ref)

            ref[...] += grads[nr + k]

    ins = list(rows) + list(consts)
    return pl.pallas_call(
        body, grid=(n_rows // tm, nblk),
        in_specs=(_rw_specs(ins, tm, nblk) + [pl.BlockSpec(p.shape, lambda i, j: (0, 0)) for p in params]
                  + _rw_specs(cts, tm, nblk)),
        out_specs=tuple(_rw_specs(rows, tm, nblk) + [pl.BlockSpec(p.shape, lambda i, j: (0, 0)) for p in params]),
        out_shape=tuple([jax.ShapeDtypeStruct(a.shape, F32) for a in rows]
                        + [jax.ShapeDtypeStruct(p.shape, F32) for p in params]),
        name=name, compiler_params=_params(2),
    )(*ins, *params, *cts)


def rowwise(fn, name, rows, consts=(), params=(), tm=256, nblk=1):
    rows, consts, params = tuple(rows), tuple(consts), tuple(params)

    @jax.custom_vjp
    def op(rows, consts, params):
        return _rowwise_fwd(fn, name, rows, consts, params, tm, nblk)

    def fwd(rows, consts, params):
        return op(rows, consts, params), (rows, consts, params)

    def bwd(saved, cts):
        rows, consts, params = saved
        grads = _rowwise_bwd(fn, name + "_bwd", rows, consts, params, tuple(cts), tm, nblk)
        return tuple(grads[:len(rows)]), None, tuple(grads[len(rows):])

    op.defvjp(fwd, bwd)
    return op(rows, consts, params)


def _merge_fn(ga, gb, pa, pb):
    return (jax.nn.sigmoid(ga) * pa + jax.nn.sigmoid(gb) * pb,)


def _outnorm_gate_fn(o, gate, gain):
    y = o * lax.rsqrt(jnp.mean(o * o, axis=-1, keepdims=True) + EPS) * gain
    return (y * (gate * jax.nn.sigmoid(gate)),)


def _beta_decay_fn(beta_raw, decay_raw, a_log, dt_bias):
    z = decay_raw + dt_bias
    softplus = jnp.maximum(z, 0.0) + jnp.log(1.0 + jnp.exp(-jnp.abs(z)))
    g = -jnp.exp(a_log) * softplus
    rows = g.shape[0]
    ii = lax.broadcasted_iota(jnp.int32, (rows, rows), 0)
    jj = lax.broadcasted_iota(jnp.int32, (rows, rows), 1)
    same_chunk_before = jnp.logical_and(jj <= ii, jj // GDN_CHUNK == ii // GDN_CHUNK).astype(F32)
    gcum = lax.dot_general(same_chunk_before, g, NN, precision=HI, preferred_element_type=F32)
    return jax.nn.sigmoid(beta_raw), gcum


def _combine_fn(o0, o1, o2, l0, l1, l2):
    m = lax.stop_gradient(jnp.maximum(jnp.maximum(l0, l1), l2))
    e0, e1, e2 = jnp.exp(l0 - m), jnp.exp(l1 - m), jnp.exp(l2 - m)
    return ((e0 * o0 + e1 * o1 + e2 * o2) / (e0 + e1 + e2),)


def _loss_fn(x, target, gain):
    y = x * lax.rsqrt(jnp.mean(x * x, axis=-1, keepdims=True) + EPS) * gain
    err = y - target
    return (0.5 * jnp.mean(err * err, axis=-1, keepdims=True),)


def _rope_call(x, cos, sin, name):
    n_rows, width = x.shape
    tm = 512

    def body(x_ref, c_ref, s_ref, o_ref):
        v = x_ref[...]
        lane = lax.broadcasted_iota(jnp.int32, v.shape, 1)
        low = (lane % DSW_HEAD_DIM) < DSW_HEAD_DIM // 2
        half = DSW_HEAD_DIM // 2
        swapped = jnp.where(low, pltpu.roll(v, LANES - half, 1), pltpu.roll(v, half, 1))
        o_ref[...] = v * c_ref[...] + swapped * s_ref[...]

    tab = pl.BlockSpec((tm, LANES), lambda i, j: (i, 0))
    blk = pl.BlockSpec((tm, LANES), lambda i, j: (i, j))
    return pl.pallas_call(
        body, grid=(n_rows // tm, width // LANES), in_specs=[blk, tab, tab], out_specs=blk,
        out_shape=jax.ShapeDtypeStruct(x.shape, F32), name=name, compiler_params=_params(2),
    )(x, cos, sin)


def rope(x, cos, sin, name):
    @jax.custom_vjp
    def op(x):
        return _rope_call(x, cos, sin, name)

    def fwd(x):
        return op(x), None

    def bwd(_, g):
        return (_rope_call(g, cos, -sin, name + "_bwd"),)

    op.defvjp(fwd, bwd)
    return op(x)


def _rope_tables(n_tokens):
    half = DSW_HEAD_DIM // 2
    inv_freq = ROPE_THETA ** (-jnp.arange(half, dtype=F32) / half)
    ang = jnp.arange(n_tokens, dtype=F32)[:, None] * inv_freq[None, :]
    cos, sin = jnp.cos(ang), jnp.sin(ang)
    return jnp.tile(jnp.concatenate([cos, cos], 1), (1, 2)), jnp.tile(jnp.concatenate([-sin, sin], 1), (1, 2))


def _attn_probs(q, kp, kc, group, n):
    blk = DSW_BLOCK
    k = _each(lambda a, b: jnp.concatenate([a, b], axis=0).astype(BF16), kp, kc)
    s = _each(lambda a, b: lax.dot_general(a.astype(BF16), b, NT, preferred_element_type=F32)
              * (DSW_HEAD_DIM ** -0.5), q, k)
    blocks_per_seq = jnp.where(group == 0, 16, jnp.where(group == 1, 4, 1))
    first = (n % blocks_per_seq) == 0
    qi = lax.broadcasted_iota(jnp.int32, (blk, 2 * blk), 0)
    kj = lax.broadcasted_iota(jnp.int32, (blk, 2 * blk), 1)
    dist = qi + blk - kj
    valid = (dist >= 0) & (dist <= blk) & jnp.logical_or(kj >= blk, jnp.logical_not(first))
    s = _each(lambda a: jnp.where(valid, a, -1e30), s)
    m = _each(lambda a: jnp.max(a, axis=-1, keepdims=True), s)
    p = _each(lambda a, b: jnp.exp(a - b), s, m)
    l = _each(lambda a: jnp.sum(a, axis=-1, keepdims=True), p)
    return _each(lambda a, b: a / b, p, l), _each(lambda a, b: a + jnp.log(b), m, l), k


def _attn_specs(n_tokens):
    blk, hpg = DSW_BLOCK, DSW_HEADS_PER_GROUP
    cur = pl.BlockSpec((hpg, blk, DSW_HEAD_DIM), lambda g, n: (g, n, 0))
    prev = pl.BlockSpec((hpg, blk, DSW_HEAD_DIM), lambda g, n: (g, jnp.maximum(n - 1, 0), 0))
    return cur, prev


def _attn_fwd(q, k, v):
    nh, n_tokens, hd = q.shape
    hpg = DSW_HEADS_PER_GROUP
    cur, prev = _attn_specs(n_tokens)

    def body(q_ref, kp_ref, kc_ref, vp_ref, vc_ref, o_ref, l_ref):
        heads = range(hpg)
        p, lse, _ = _attn_probs([q_ref[h] for h in heads], [kp_ref[h] for h in heads], [kc_ref[h] for h in heads],
                                pl.program_id(0), pl.program_id(1))
        vv = [jnp.concatenate([vp_ref[h], vc_ref[h]], axis=0).astype(BF16) for h in heads]
        o = _each(lambda a, b: lax.dot_general(a.astype(BF16), b, NN, preferred_element_type=F32), p, vv)
        for h in heads:
            o_ref[h] = o[h]
            l_ref[h] = jnp.broadcast_to(lse[h], (DSW_BLOCK, hd))

    return pl.pallas_call(
        body, grid=(nh // hpg, n_tokens // DSW_BLOCK), in_specs=[cur, prev, cur, prev, cur], out_specs=(cur, cur),
        out_shape=(jax.ShapeDtypeStruct(q.shape, F32), jax.ShapeDtypeStruct(q.shape, F32)),
        name="attn_fwd", compiler_params=_params(2),
    )(q, k, k, v, v)


def _attn_bwd(q, k, v, do, dlse):
    nh, n_tokens, hd = q.shape
    hpg = DSW_HEADS_PER_GROUP
    nblk = n_tokens // DSW_BLOCK
    cur, prev = _attn_specs(n_tokens)
    part = pl.BlockSpec((hpg, 1, 2 * DSW_BLOCK, hd), lambda g, n: (g, n, 0, 0))
    scale = DSW_HEAD_DIM ** -0.5

    def body(q_ref, kp_ref, kc_ref, vp_ref, vc_ref, do_ref, dl_ref, dq_ref, dk_ref, dv_ref):
        heads = range(hpg)
        qs = [q_ref[h] for h in heads]
        p, _, kb = _attn_probs(qs, [kp_ref[h] for h in heads], [kc_ref[h] for h in heads],
                               pl.program_id(0), pl.program_id(1))
        qb = _each(lambda a: a.astype(BF16), qs)
        vv = [jnp.concatenate([vp_ref[h], vc_ref[h]], axis=0).astype(BF16) for h in heads]
        dob = [do_ref[h].astype(BF16) for h in heads]
        dp = _each(lambda a, b: lax.dot_general(a, b, NT, preferred_element_type=F32), dob, vv)
        dv = _each(lambda a, b: lax.dot_general(a.astype(BF16), b, TN, preferred_element_type=F32), p, dob)
        dl = [jnp.sum(dl_ref[h], axis=-1, keepdims=True) for h in heads]
        ds = _each(lambda a, b, c: (a * (b - jnp.sum(b * a, axis=-1, keepdims=True) + c) * scale).astype(BF16),
                   p, dp, dl)
        dq = _each(lambda a, b: lax.dot_general(a, b, NN, preferred_element_type=F32), ds, kb)
        dk = _each(lambda a, b: lax.dot_general(a, b, TN, preferred_element_type=F32), ds, qb)
        for h in heads:
            dq_ref[h] = dq[h]
            dk_ref[h, 0] = dk[h]
            dv_ref[h, 0] = dv[h]

    dq, dkp, dvp = pl.pallas_call(
        body, grid=(nh // hpg, nblk), in_specs=[cur, prev, cur, prev, cur, cur, cur], out_specs=(cur, part, part),
        out_shape=(jax.ShapeDtypeStruct(q.shape, F32),
                   jax.ShapeDtypeStruct((nh, nblk, 2 * DSW_BLOCK, hd), F32),
                   jax.ShapeDtypeStruct((nh, nblk, 2 * DSW_BLOCK, hd), F32)),
        name="attn_bwd", compiler_params=_params(2),
    )(q, k, k, v, v, do, dlse)

    def fold(partial):
        own = partial[:, :, DSW_BLOCK:]
        from_next = jnp.pad(partial[:, 1:, :DSW_BLOCK], ((0, 0), (0, 1), (0, 0), (0, 0)))
        return (own + from_next).reshape(nh, n_tokens, hd)

    return dq, fold(dkp), fold(dvp)


@jax.custom_vjp
def attention(q, k, v):
    return _attn_fwd(q, k, v)


def _attention_fwd(q, k, v):
    return _attn_fwd(q, k, v), (q, k, v)


def _attention_bwd(saved, cts):
    q, k, v = saved
    return _attn_bwd(q, k, v, cts[0], cts[1])


attention.defvjp(_attention_fwd, _attention_bwd)


def _to_heads(a):
    n_tokens = a.shape[0]
    outs = []
    for gi, d in enumerate(DSW_DILATIONS):
        blk = a[:, gi * 256:(gi + 1) * 256].reshape(n_tokens // d, d, DSW_HEADS_PER_GROUP, DSW_HEAD_DIM)
        outs.append(blk.transpose(2, 1, 0, 3).reshape(DSW_HEADS_PER_GROUP, n_tokens, DSW_HEAD_DIM))
    return jnp.concatenate(outs, 0)


def _from_heads(a):
    n_tokens = a.shape[1]
    outs = []
    for gi, d in enumerate(DSW_DILATIONS):
        blk = a[gi * 4:(gi + 1) * 4].reshape(DSW_HEADS_PER_GROUP, d, n_tokens // d, DSW_HEAD_DIM)
        outs.append(blk.transpose(2, 1, 0, 3).reshape(n_tokens, DSW_HEADS_PER_GROUP * DSW_HEAD_DIM))
    return outs


CONV_TILE = 512


def _shift_down(x, k, rows):
    return x if k == 0 else jnp.where(rows >= k, pltpu.roll(x, k, 0), 0.0)


def _shift_up(x, k, rows):
    n = x.shape[0]
    return x if k == 0 else jnp.where(rows < n - k, pltpu.roll(x, n - k, 0), 0.0)


def _conv_pre(x, w):
    rows = lax.broadcasted_iota(jnp.int32, x.shape, 0)
    acc = x * w[GDN_CONV - 1:GDN_CONV]
    for k in range(1, GDN_CONV):
        acc = acc + _shift_down(x, k, rows) * w[GDN_CONV - 1 - k:GDN_CONV - k]
    return acc, rows


def _conv_fwd(x, w):
    n_tokens, width = x.shape
    big = pl.BlockSpec((n_tokens, CONV_TILE), lambda j: (0, j))
    wsp = pl.BlockSpec((GDN_CONV, CONV_TILE), lambda j: (0, j))

    def body(x_ref, w_ref, o_ref):
        acc, _ = _conv_pre(x_ref[...], w_ref[...])
        o_ref[...] = acc * jax.nn.sigmoid(acc)

    return pl.pallas_call(
        body, grid=(width // CONV_TILE,), in_specs=[big, wsp], out_specs=big,
        out_shape=jax.ShapeDtypeStruct(x.shape, F32), name="conv_fwd", compiler_params=_params(1),
    )(x, w)


def _conv_bwd(x, w, dy):
    n_tokens, width = x.shape
    big = pl.BlockSpec((n_tokens, CONV_TILE), lambda j: (0, j))
    wsp = pl.BlockSpec((GDN_CONV, CONV_TILE), lambda j: (0, j))

    def body(x_ref, w_ref, dy_ref, dx_ref, dw_ref):
        xv, wv = x_ref[...], w_ref[...]
        acc, rows = _conv_pre(xv, wv)
        sg = jax.nn.sigmoid(acc)
        dacc = dy_ref[...] * (sg + acc * sg * (1.0 - sg))
        dx = dacc * wv[GDN_CONV - 1:GDN_CONV]
        for k in range(1, GDN_CONV):
            dx = dx + _shift_up(dacc, k, rows) * wv[GDN_CONV - 1 - k:GDN_CONV - k]
        dx_ref[...] = dx
        for k in range(GDN_CONV):
            dw_ref[GDN_CONV - 1 - k:GDN_CONV - k, :] = jnp.sum(dacc * _shift_down(xv, k, rows), axis=0, keepdims=True)

    return pl.pallas_call(
        body, grid=(width // CONV_TILE,), in_specs=[big, wsp, big], out_specs=(big, wsp),
        out_shape=(jax.ShapeDtypeStruct(x.shape, F32), jax.ShapeDtypeStruct(w.shape, F32)),
        name="conv_bwd", compiler_params=_params(1),
    )(x, w, dy)


@jax.custom_vjp
def conv_silu(x, w):
    return _conv_fwd(x, w)


def _conv_silu_fwd(x, w):
    return _conv_fwd(x, w), (x, w)


def _conv_silu_bwd(saved, g):
    return _conv_bwd(saved[0], saved[1], g)


conv_silu.defvjp(_conv_silu_fwd, _conv_silu_bwd)


def _dot(a, b, dn=NN):
    return lax.dot_general(a, b, dn, precision=HI, preferred_element_type=F32)


def _dot3(a, b, dn=NN):
    return lax.dot_general(a, b, dn, precision=lax.Precision.HIGH, preferred_element_type=F32)


def _bf16_dot(a, b, dn):
    return lax.dot_general(a.astype(BF16), b.astype(BF16), dn, preferred_element_type=F32)


_DOT_GRADS = {NN: (("g", "b", NT), ("a", "g", TN)), NT: (("g", "b", NN), ("g", "a", TN)),
              TN: (("b", "g", NT), ("a", "g", NN))}


def _make_bdot(dn):
    @jax.custom_vjp
    def op(a, b):
        return _bf16_dot(a, b, dn)

    def fwd(a, b):
        return op(a, b), (a, b)

    def bwd(saved, g):
        vals = dict(a=saved[0], b=saved[1], g=g)
        return tuple(_bf16_dot(vals[x], vals[y], form) for x, y, form in _DOT_GRADS[dn])

    op.defvjp(fwd, bwd)
    return op


_BDOTS = {dn: _make_bdot(dn) for dn in (NN, NT, TN)}


def _bdot(a, b, dn=NN):
    return _BDOTS[dn](a, b)


def _each(fn, *lists):
    return [fn(*items) for items in zip(*lists)]


def _gdn_chunks(q, k, v, b, gcum, state):
    c = GDN_CHUNK
    ii = lax.broadcasted_iota(jnp.int32, (c, c), 0)
    jj = lax.broadcasted_iota(jnp.int32, (c, c), 1)
    eye = (ii == jj).astype(F32)
    qn = _each(lambda x: x * lax.rsqrt(jnp.sum(x * x, axis=-1, keepdims=True) + EPS) * (GDN_HEAD_DIM ** -0.5), q)
    kn = _each(lambda x: x * lax.rsqrt(jnp.sum(x * x, axis=-1, keepdims=True) + EPS), k)
    gcum_i = _each(lambda x: jnp.broadcast_to(x, (c, c)), gcum)
    gcum_j = _each(jnp.transpose, gcum_i)
    decay = _each(lambda x, y: jnp.exp(jnp.where(jj <= ii, x - y, -1e30)), gcum_i, gcum_j)
    g_last = _each(lambda x: x[c - 1:c, :], gcum)
    e_gcum = _each(jnp.exp, gcum)
    kbeta = _each(lambda x, y: x * y, kn, b)
    vbeta = _each(lambda x, y: x * y, v, b)
    m = _each(lambda x, y, d: jnp.where(jj < ii, _bdot(x, y, NT) * d, 0.0), kbeta, kn, decay)
    inv = _each(lambda x: eye - x, m)
    power = _each(lambda x: _dot3(x, x), m)
    for step in range(5):
        inv = _each(lambda x, p: x + _dot3(x, p), inv, power)
        if step < 4:
            power = _each(lambda p: _dot3(p, p), power)
    u = _each(_dot3, inv, vbeta)
    w = _each(lambda x, y, e: _dot3(x, y * e), inv, kbeta, e_gcum)
    a_qk = _each(lambda x, y, d: _bdot(x, y, NT) * d, qn, kn, decay)
    v_new = _each(lambda x, y, s: x - _bdot(y, s), u, w, state)
    o = _each(lambda x, e, s, a, vn: _bdot(x * e, s) + _bdot(a, vn), qn, e_gcum, state, a_qk, v_new)
    new_state = _each(lambda s, gl, x, gc, vn: s * jnp.exp(gl) + _bdot(x * jnp.exp(gl - gc), vn, TN),
                      state, g_last, kn, gcum, v_new)
    return o, new_state


GDN_HEADS_PER_STEP = 8


GDN_TIME_TILE = 256


def _gdn_specs(n_tokens, reverse):
    hb, hd, tt = GDN_HEADS_PER_STEP, GDN_HEAD_DIM, GDN_TIME_TILE
    nb, nt = GDN_HEADS // hb, n_tokens // tt

    def when(t):
        return nt - 1 - t if reverse else t

    q = pl.BlockSpec((tt, hb * hd), lambda h, t: (when(t), h))
    k = pl.BlockSpec((tt, hb * hd), lambda h, t: (when(t), nb + h))
    v = pl.BlockSpec((tt, hb * hd), lambda h, t: (when(t), 2 * nb + h))
    vec = pl.BlockSpec((hb, tt, 1), lambda h, t: (h, when(t), 0))
    states = pl.BlockSpec((hb, tt // GDN_CHUNK, hd, hd), lambda h, t: (h, when(t), 0, 0))
    return q, k, v, vec, states


def _gdn_fwd(qkv, beta, g):
    n_tokens = qkv.shape[0]
    hb, hd, tt = GDN_HEADS_PER_STEP, GDN_HEAD_DIM, GDN_TIME_TILE
    n_chunks = tt // GDN_CHUNK
    q_s, k_s, v_s, vec, st = _gdn_specs(n_tokens, False)

    def body(q_ref, k_ref, v_ref, b_ref, g_ref, o_ref, st_ref, state):
        @pl.when(pl.program_id(1) == 0)
        def _():
            state[...] = jnp.zeros_like(state)

        def step(c, carry):
            r = pl.ds(pl.multiple_of(c * GDN_CHUNK, GDN_CHUNK), GDN_CHUNK)
            cols = [slice(h * hd, (h + 1) * hd) for h in range(hb)]
            old = [state[h] for h in range(hb)]
            o, new = _gdn_chunks(
                [q_ref[r, cs] for cs in cols], [k_ref[r, cs] for cs in cols], [v_ref[r, cs] for cs in cols],
                [b_ref[h, r, :] for h in range(hb)], [g_ref[h, r, :] for h in range(hb)], old)
            for h in range(hb):
                st_ref[h, c] = old[h]
                o_ref[r, cols[h]] = o[h]
                state[h] = new[h]
            return carry

        lax.fori_loop(0, n_chunks, step, 0)

    return pl.pallas_call(
        body, grid=(GDN_HEADS // hb, n_tokens // tt), in_specs=[q_s, k_s, v_s, vec, vec], out_specs=(q_s, st),
        out_shape=(jax.ShapeDtypeStruct((n_tokens, GDN_WIDTH), F32),
                   jax.ShapeDtypeStruct((GDN_HEADS, n_tokens // GDN_CHUNK, hd, hd), F32)),
        scratch_shapes=[pltpu.VMEM((hb, hd, hd), F32)],
        name="gdn_fwd", compiler_params=_params(2),
    )(qkv, qkv, qkv, beta, g)


def _gdn_bwd(qkv, beta, g, states, do):
    n_tokens = qkv.shape[0]
    hb, hd, tt = GDN_HEADS_PER_STEP, GDN_HEAD_DIM, GDN_TIME_TILE
    n_chunks = tt // GDN_CHUNK
    q_s, k_s, v_s, vec, st = _gdn_specs(n_tokens, True)

    def body(q_ref, k_ref, v_ref, b_ref, g_ref, st_ref, do_ref, dq_ref, dk_ref, dv_ref, db_ref, dg_ref, dstate):
        @pl.when(pl.program_id(1) == 0)
        def _():
            dstate[...] = jnp.zeros_like(dstate)

        def step(i, carry):
            c = n_chunks - 1 - i
            r = pl.ds(pl.multiple_of(c * GDN_CHUNK, GDN_CHUNK), GDN_CHUNK)
            cols = [slice(h * hd, (h + 1) * hd) for h in range(hb)]
            args = ([q_ref[r, cs] for cs in cols], [k_ref[r, cs] for cs in cols], [v_ref[r, cs] for cs in cols],
                    [b_ref[h, r, :] for h in range(hb)], [g_ref[h, r, :] for h in range(hb)],
                    [st_ref[h, c] for h in range(hb)])
            cts = ([do_ref[r, cs] for cs in cols], [dstate[h] for h in range(hb)])
            dq, dk, dv, db, dg, dst = jax.vjp(_gdn_chunks, *args)[1](cts)
            for h in range(hb):
                dq_ref[r, cols[h]] = dq[h]
                dk_ref[r, cols[h]] = dk[h]
                dv_ref[r, cols[h]] = dv[h]
                db_ref[h, r, :] = db[h]
                dg_ref[h, r, :] = dg[h]
                dstate[h] = dst[h]
            return carry

        lax.fori_loop(0, n_chunks, step, 0)

    wide = jax.ShapeDtypeStruct((n_tokens, GDN_WIDTH), F32)
    thin = jax.ShapeDtypeStruct(beta.shape, F32)
    dq, dk, dv, db, dg = pl.pallas_call(
        body, grid=(GDN_HEADS // hb, n_tokens // tt), in_specs=[q_s, k_s, v_s, vec, vec, st, q_s],
        out_specs=(q_s, q_s, q_s, vec, vec), out_shape=(wide, wide, wide, thin, thin),
        scratch_shapes=[pltpu.VMEM((hb, hd, hd), F32)],
        name="gdn_bwd", compiler_params=_params(2),
    )(qkv, qkv, qkv, beta, g, states, do)
    return jnp.concatenate([dq, dk, dv], axis=1), db, dg


@jax.custom_vjp
def gated_delta(qkv, beta, g):
    return _gdn_fwd(qkv, beta, g)[0]


def _gated_delta_fwd(qkv, beta, g):
    o, states = _gdn_fwd(qkv, beta, g)
    return o, (qkv, beta, g, states)


def _gated_delta_bwd(saved, do):
    return _gdn_bwd(*saved, do)


gated_delta.defvjp(_gated_delta_fwd, _gated_delta_bwd)


FFN_ROW_TILE = 256


def _resident(shape):
    return pl.BlockSpec(shape, lambda i: (0,) * len(shape), pipeline_mode=pl.Buffered(1))


def _ffn_fwd(x, gain, wg, wu, wd, name):
    n_tokens, d = x.shape
    n_shards, _, n = wg.shape
    tm = FFN_ROW_TILE

    def body(x_ref, gain_ref, wg_ref, wu_ref, wd_ref, o_ref, g_ref, u_ref):
        xv = x_ref[...]
        h = (xv * lax.rsqrt(jnp.mean(xv * xv, axis=-1, keepdims=True) + EPS) * gain_ref[...]).astype(BF16)
        acc = jnp.zeros((tm, d), F32)
        for j in range(n_shards):
            g = lax.dot_general(h, wg_ref[j], NN, preferred_element_type=F32)
            u = lax.dot_general(h, wu_ref[j], NN, preferred_element_type=F32)
            g_ref[j] = g
            u_ref[j] = u
            a = (g * jax.nn.sigmoid(g) * u).astype(BF16)
            acc = acc + lax.dot_general(a, wd_ref[j], NN, preferred_element_type=F32)
        o_ref[...] = xv + 0.5 * acc

    row = pl.BlockSpec((tm, d), lambda i: (i, 0))
    hid = pl.BlockSpec((n_shards, tm, n), lambda i: (0, i, 0))
    return pl.pallas_call(
        body, grid=(n_tokens // tm,),
        in_specs=[row, _resident(gain.shape), _resident(wg.shape), _resident(wu.shape), _resident(wd.shape)],
        out_specs=(row, hid, hid),
        out_shape=(jax.ShapeDtypeStruct(x.shape, F32), jax.ShapeDtypeStruct((n_shards, n_tokens, n), F32),
                   jax.ShapeDtypeStruct((n_shards, n_tokens, n), F32)),
        name=name, compiler_params=_params(1),
    )(x, gain, wg, wu, wd)


def _ffn_bwd_rows(x, gain, dy, g, u, wg, wu, wd, name):
    n_tokens, d = x.shape
    n_shards, _, n = wg.shape
    tm = FFN_ROW_TILE

    def body(x_ref, gain_ref, dy_ref, g_ref, u_ref, wg_ref, wu_ref, wd_ref,
             dx_ref, dgain_ref, h_ref, dyh_ref, a_ref, dg_ref, du_ref):
        xv, dyv, gain_v = x_ref[...], dy_ref[...], gain_ref[...]
        r = lax.rsqrt(jnp.mean(xv * xv, axis=-1, keepdims=True) + EPS)
        xhat = xv * r
        h_ref[...] = (xhat * gain_v).astype(BF16)
        dyh = (0.5 * dyv).astype(BF16)
        dyh_ref[...] = dyh
        dh = jnp.zeros((tm, d), F32)
        for j in range(n_shards):
            da = lax.dot_general(dyh, wd_ref[j], NT, preferred_element_type=F32)
            gv, uv = g_ref[j], u_ref[j]
            sg = jax.nn.sigmoid(gv)
            silu = gv * sg
            a_ref[j] = (silu * uv).astype(BF16)
            dg = (da * uv * (sg + silu * (1.0 - sg))).astype(BF16)
            du = (da * silu).astype(BF16)
            dg_ref[j] = dg
            du_ref[j] = du
            dh = dh + lax.dot_general(dg, wg_ref[j], NT, preferred_element_type=F32)
            dh = dh + lax.dot_general(du, wu_ref[j], NT, preferred_element_type=F32)
        dxhat = dh * gain_v
        dx_ref[...] = dyv + r * (dxhat - xhat * jnp.mean(dxhat * xhat, axis=-1, keepdims=True))

        @pl.when(pl.program_id(0) == 0)
        def _():
            dgain_ref[...] = jnp.zeros_like(dgain_ref)

        dgain_ref[...] += jnp.sum(dh * xhat, axis=0, keepdims=True)

    row = pl.BlockSpec((tm, d), lambda i: (i, 0))
    hid = pl.BlockSpec((n_shards, tm, n), lambda i: (0, i, 0))
    hid_shape = (n_shards, n_tokens, n)
    return pl.pallas_call(
        body, grid=(n_tokens // tm,),
        in_specs=[row, _resident(gain.shape), row, hid, hid, _resident(wg.shape), _resident(wu.shape),
                  _resident(wd.shape)],
        out_specs=(row, pl.BlockSpec(gain.shape, lambda i: (0, 0)), row, row, hid, hid, hid),
        out_shape=(jax.ShapeDtypeStruct(x.shape, F32), jax.ShapeDtypeStruct(gain.shape, F32),
                   jax.ShapeDtypeStruct(x.shape, BF16), jax.ShapeDtypeStruct(x.shape, BF16),
                   jax.ShapeDtypeStruct(hid_shape, BF16), jax.ShapeDtypeStruct(hid_shape, BF16),
                   jax.ShapeDtypeStruct(hid_shape, BF16)),
        name=name, compiler_params=_params(1),
    )(x, gain, dy, g, u, wg, wu, wd)


def _ffn_bwd_weights(h, dyh, a, dg, du, name):
    n_shards, n_tokens, n = a.shape
    d = h.shape[1]

    def body(h_ref, dyh_ref, a_ref, dg_ref, du_ref, dwg_ref, dwu_ref, dwd_ref):
        hv = h_ref[...]
        dwg_ref[0] = lax.dot_general(hv, dg_ref[0], TN, preferred_element_type=F32)
        dwu_ref[0] = lax.dot_general(hv, du_ref[0], TN, preferred_element_type=F32)
        dwd_ref[0] = lax.dot_general(a_ref[0], dyh_ref[...], TN, preferred_element_type=F32)

    hid = pl.BlockSpec((1, n_tokens, n), lambda j: (j, 0, 0))
    return pl.pallas_call(
        body, grid=(n_shards,), in_specs=[_resident(h.shape), _resident(dyh.shape), hid, hid, hid],
        out_specs=(pl.BlockSpec((1, d, n), lambda j: (j, 0, 0)), pl.BlockSpec((1, d, n), lambda j: (j, 0, 0)),
                   pl.BlockSpec((1, n, d), lambda j: (j, 0, 0))),
        out_shape=(jax.ShapeDtypeStruct((n_shards, d, n), F32), jax.ShapeDtypeStruct((n_shards, d, n), F32),
                   jax.ShapeDtypeStruct((n_shards, n, d), F32)),
        name=name, compiler_params=_params(1),
    )(h, dyh, a, dg, du)


def _ffn(x, gain, w, slots, tag):
    names = [tag + "_w_gate", tag + "_w_up", tag + "_w_down"]

    @jax.custom_vjp
    def op(x, gain, weights, slot):
        return _ffn_fwd(x, gain, *weights, tag + "_fwd")[0]

    def fwd(x, gain, weights, slot):
        out, g, u = _ffn_fwd(x, gain, *weights, tag + "_fwd")
        return out, (x, gain, weights, g, u)

    def bwd(saved, dy):
        x, gain, weights, g, u = saved
        dx, dgain, h, dyh, a, dg, du = _ffn_bwd_rows(x, gain, dy, g, u, *weights, tag + "_bwd_rows")
        return dx, dgain, None, _ffn_bwd_weights(h, dyh, a, dg, du, tag + "_bwd_weights")

    op.defvjp(fwd, bwd)
    return op(x, gain, tuple(w[n] for n in names), tuple(slots[n] for n in names))


IN_NAMES = ("wq_a", "wk_a", "wv_a", "w_qkvb", "w_small", "w_ggate", "w_gatea", "w_gateb")


def _in_proj_fwd(x, gain, ws):
    n_tokens, d = x.shape
    tm = FFN_ROW_TILE

    def body(x_ref, gain_ref, *refs):
        w_refs, o_refs = refs[:len(ws)], refs[len(ws):]
        xv = x_ref[...]
        h = (xv * lax.rsqrt(jnp.mean(xv * xv, axis=-1, keepdims=True) + EPS) * gain_ref[...]).astype(BF16)
        for w_ref, o_ref in zip(w_refs, o_refs):
            o_ref[...] = lax.dot_general(h, w_ref[...], NN, preferred_element_type=F32)

    row = pl.BlockSpec((tm, d), lambda i: (i, 0))
    return pl.pallas_call(
        body, grid=(n_tokens // tm,),
        in_specs=[row, _resident(gain.shape)] + [_resident(wt.shape) for wt in ws],
        out_specs=tuple(pl.BlockSpec((tm, wt.shape[1]), lambda i: (i, 0)) for wt in ws),
        out_shape=tuple(jax.ShapeDtypeStruct((n_tokens, wt.shape[1]), F32) for wt in ws),
        name="in_proj_fwd", compiler_params=_params(1),
    )(x, gain, *ws)


def _in_proj_bwd_rows(x, gain, dzs, ws):
    n_tokens, d = x.shape
    tm = FFN_ROW_TILE
    n = len(ws)

    def body(x_ref, gain_ref, *refs):
        dz_refs, w_refs = refs[:n], refs[n:2 * n]
        dx_ref, dgain_ref, h_ref = refs[2 * n:]
        xv, gain_v = x_ref[...], gain_ref[...]
        r = lax.rsqrt(jnp.mean(xv * xv, axis=-1, keepdims=True) + EPS)
        xhat = xv * r
        h_ref[...] = (xhat * gain_v).astype(BF16)
        dh = jnp.zeros((tm, d), F32)
        for dz_ref, w_ref in zip(dz_refs, w_refs):
            dh = dh + lax.dot_general(dz_ref[...].astype(BF16), w_ref[...], NT, preferred_element_type=F32)
        dxhat = dh * gain_v
        dx_ref[...] = r * (dxhat - xhat * jnp.mean(dxhat * xhat, axis=-1, keepdims=True))

        @pl.when(pl.program_id(0) == 0)
        def _():
            dgain_ref[...] = jnp.zeros_like(dgain_ref)

        dgain_ref[...] += jnp.sum(dh * xhat, axis=0, keepdims=True)

    row = pl.BlockSpec((tm, d), lambda i: (i, 0))
    return pl.pallas_call(
        body, grid=(n_tokens // tm,),
        in_specs=([row, _resident(gain.shape)] + [pl.BlockSpec((tm, dz.shape[1]), lambda i: (i, 0)) for dz in dzs]
                  + [_resident(wt.shape) for wt in ws]),
        out_specs=(row, pl.BlockSpec(gain.shape, lambda i: (0, 0)), row),
        out_shape=(jax.ShapeDtypeStruct(x.shape, F32), jax.ShapeDtypeStruct(gain.shape, F32),
                   jax.ShapeDtypeStruct(x.shape, BF16)),
        name="in_proj_bwd_rows", compiler_params=_params(1),
    )(x, gain, *dzs, *ws)


def _in_proj_bwd_weight(h, dz, name):
    n_tokens, d = h.shape
    width = dz.shape[1]
    tn = _tile(width, 512)

    def body(h_ref, dz_ref, o_ref):
        o_ref[...] = lax.dot_general(h_ref[...], dz_ref[...].astype(BF16), TN, preferred_element_type=F32)

    return pl.pallas_call(
        body, grid=(width // tn,),
        in_specs=[_resident(h.shape), pl.BlockSpec((n_tokens, tn), lambda j: (0, j))],
        out_specs=pl.BlockSpec((d, tn), lambda j: (0, j)),
        out_shape=jax.ShapeDtypeStruct((d, width), F32), name=name, compiler_params=_params(1),
    )(h, dz)


def in_proj(x, gain, w, slots):
    @jax.custom_vjp
    def op(x, gain, ws, slot):
        return _in_proj_fwd(x, gain, ws)

    def fwd(x, gain, ws, slot):
        return op(x, gain, ws, slot), (x, gain, ws)

    def bwd(saved, dzs):
        x, gain, ws = saved
        dx, dgain, h = _in_proj_bwd_rows(x, gain, dzs, ws)
        dws = tuple(_in_proj_bwd_weight(h, dz, "in_proj_dw_" + n) for n, dz in zip(IN_NAMES, dzs))
        return dx, dgain, None, dws

    op.defvjp(fwd, bwd)
    outs = op(x, gain, tuple(w[n] for n in IN_NAMES), tuple(slots[n] for n in IN_NAMES))
    return dict(zip(IN_NAMES, outs))


def _local_loss(diff, x_target, w):
    x, slots, small = diff
    target = x_target
    n_tokens = x.shape[0]
    x1 = _ffn(x, small["ffn1_norm"], w, slots, "ffn1")

    proj = in_proj(x1, small["mix_norm"], w, slots)

    cos, sin = _rope_tables(n_tokens)
    q = _to_heads(rope(proj["wq_a"], cos, sin, "rope_q"))
    k = _to_heads(rope(proj["wk_a"], cos, sin, "rope_k"))
    v = _to_heads(proj["wv_a"])
    o, lse = attention(q, k, v)
    ya = rowwise(_combine_fn, "combine", tuple(_from_heads(o)) + tuple(_from_heads(lse)), tm=512)[0]
    pa = mm(ya, w["w_branch_a"], slots["w_branch_a"], "branch_a")

    qkv = conv_silu(proj["w_qkvb"], small["gdn_conv_w"])
    beta, g = rowwise(_beta_decay_fn, "beta_decay",
                      (proj["w_small"][:, :GDN_HEADS], proj["w_small"][:, GDN_HEADS:2 * GDN_HEADS]),
                      params=(small["gdn_a_log"], small["gdn_dt_bias"]), tm=512)
    ob = gated_delta(qkv, beta.T[:, :, None], g.T[:, :, None])
    yb = rowwise(_outnorm_gate_fn, "outnorm_gate", (ob, proj["w_ggate"]), params=(small["gdn_out_norm"],),
                 tm=512, nblk=GDN_HEADS)[0]
    pb = mm(yb, w["w_branch_b"], slots["w_branch_b"], "branch_b")

    merged = rowwise(_merge_fn, "merge", (proj["w_gatea"], proj["w_gateb"], pa, pb))[0]
    x2 = mm_res(merged, w["w_out"], slots["w_out"], x1, 1.0, "out")

    x3 = _ffn(x2, small["ffn2_norm"], w, slots, "ffn2")
    row_loss = rowwise(_loss_fn, "loss", (x3,), consts=(target,), params=(small["final_norm"],))[0]
    return jnp.sum(row_loss)


SHARDS = (
    ("ffn1_w_gate", 1024, 352), ("ffn1_w_up", 1024, 352), ("ffn1_w_down", 352, 1024), ("w_in", 1024, 1058),
    ("w_branch_a", 256, 128), ("w_branch_b", 128, 1024), ("w_out", 128, 1024),
    ("ffn2_w_gate", 1024, 352), ("ffn2_w_up", 1024, 352), ("ffn2_w_down", 352, 1024),
)
IN_SHARD = D_IN // N_DEV
CONV_SHARD = (GDN_CONV, 3 * GDN_WIDTH // N_DEV)
SMALL_ROWS = 24
ANY = pl.BlockSpec(memory_space=pl.ANY)


def _position():
    return lax.axis_index("x"), lax.axis_index("y"), lax.axis_index("c")


def all_gather_shards(shards, name):
    n = len(shards)

    def body(*refs):
        x_refs, out_refs = refs[:n], refs[n:2 * n]
        send_sems, recv_sems, local_sems = refs[2 * n:]
        x, y, c = _position()
        me, sibling = (x, y, c), (x, y, 1 - c)
        chips = [(1 - x, y), (x, 1 - y), (1 - x, 1 - y)]

        def slab(a, px, py, pc):
            return out_refs[a].at[4 * px + 2 * py + pc]

        def copy(a, k, block, to, src=None):
            return pltpu.make_async_remote_copy(
                src_ref=slab(a, *block) if src is None else src, dst_ref=slab(a, *block),
                send_sem=send_sems.at[7 * a + k], recv_sem=recv_sems.at[7 * a + k], device_id=to, device_id_type=MESH)

        mine = [pltpu.make_async_copy(x_refs[a], slab(a, *me), local_sems.at[a]) for a in range(n)]
        for cp in mine:
            cp.start()
        first = []
        for j, chip in enumerate(chips):
            first += [copy(a, 1 + j, me, (*chip, c), src=x_refs[a]) for a in range(n)]
        first += [copy(a, 0, me, sibling, src=x_refs[a]) for a in range(n)]
        for cp in first:
            cp.start()
        passed = []
        for j, chip in enumerate(chips):
            for a in range(n):
                copy(a, 1 + j, (*chip, c), me).wait_recv()
                cp = copy(a, 4 + j, (*chip, c), sibling)
                cp.start()
                passed.append(cp)
        for a in range(n):
            copy(a, 0, sibling, me).wait_recv()
        for j, chip in enumerate(chips):
            for a in range(n):
                copy(a, 4 + j, (*chip, 1 - c), me).wait_recv()
        for cp in first + passed:
            cp.wait_send()
        for cp in mine:
            cp.wait()

    return pl.pallas_call(
        body, out_shape=tuple(jax.ShapeDtypeStruct((N_DEV,) + s.shape, s.dtype) for s in shards),
        in_specs=[ANY] * n, out_specs=(ANY,) * n,
        scratch_shapes=[pltpu.SemaphoreType.DMA((7 * n,)), pltpu.SemaphoreType.DMA((7 * n,)),
                        pltpu.SemaphoreType.DMA((n,))],
        name=name,
    )(*shards)


def exchange_with_sibling(grads):
    n = len(grads)

    def body(*refs):
        g_refs, recv_refs = refs[:n], refs[n:2 * n]
        send_sems, recv_sems = refs[2 * n:]
        x, y, c = _position()
        copies = [pltpu.make_async_remote_copy(
            src_ref=g_refs[a].at[2 * k + 1 - c], dst_ref=recv_refs[a].at[k], send_sem=send_sems.at[4 * a + k],
            recv_sem=recv_sems.at[4 * a + k], device_id=(x, y, 1 - c), device_id_type=MESH)
            for k in range(4) for a in range(n)]
        for cp in copies:
            cp.start()
        for cp in copies:
            cp.wait()

    return pl.pallas_call(
        body, out_shape=tuple(jax.ShapeDtypeStruct((4,) + g.shape[1:], g.dtype) for g in grads),
        in_specs=[ANY] * n, out_specs=(ANY,) * n,
        scratch_shapes=[pltpu.SemaphoreType.DMA((4 * n,)), pltpu.SemaphoreType.DMA((4 * n,))], name="rs_sibling",
    )(*grads)


def _row_tile(rows):
    return 256 if rows % 256 == 0 else rows


def add_sibling(grads, received, core, name):
    _, rows, width = grads.shape
    tr = _row_tile(rows)

    def body(c_ref, g_ref, r_ref, o_ref):
        o_ref[...] = (g_ref[...] + r_ref[...]).astype(BF16)

    blk = (1, tr, width)
    return pl.pallas_call(
        body,
        grid_spec=pltpu.PrefetchScalarGridSpec(
            num_scalar_prefetch=1, grid=(4, rows // tr),
            in_specs=[pl.BlockSpec(blk, lambda k, i, c_ref: (2 * k + c_ref[0], i, 0)),
                      pl.BlockSpec(blk, lambda k, i, c_ref: (k, i, 0))],
            out_specs=pl.BlockSpec(blk, lambda k, i, c_ref: (k, i, 0))),
        out_shape=jax.ShapeDtypeStruct((4, rows, width), BF16), name=name, compiler_params=_params(2),
    )(core, grads, received)


def exchange_with_chips(partials):
    n = len(partials)

    def body(*refs):
        p_refs, recv_refs = refs[:n], refs[n:2 * n]
        send_sems, recv_sems = refs[2 * n:]
        x, y, c = _position()
        chips = [(1 - x, y), (x, 1 - y), (1 - x, 1 - y)]
        copies = [pltpu.make_async_remote_copy(
            src_ref=p_refs[a].at[2 * cx + cy], dst_ref=recv_refs[a].at[j], send_sem=send_sems.at[3 * a + j],
            recv_sem=recv_sems.at[3 * a + j], device_id=(cx, cy, c), device_id_type=MESH)
            for a in range(n) for j, (cx, cy) in enumerate(chips)]
        for cp in copies:
            cp.start()
        for cp in copies:
            cp.wait()

    return pl.pallas_call(
        body, out_shape=tuple(jax.ShapeDtypeStruct((3,) + p.shape[1:], p.dtype) for p in partials),
        in_specs=[ANY] * n, out_specs=(ANY,) * n,
        scratch_shapes=[pltpu.SemaphoreType.DMA((3 * n,)), pltpu.SemaphoreType.DMA((3 * n,))], name="rs_chips",
    )(*partials)


def all_reduce_small(vals):
    rows, width = vals.shape

    def body(x_ref, out_ref, all_ref, send_sems, recv_sems):
        x, y, c = _position()
        me, sibling = (x, y, c), (x, y, 1 - c)
        chips = [(1 - x, y), (x, 1 - y), (1 - x, 1 - y)]

        def slab(px, py, pc):
            return all_ref.at[4 * px + 2 * py + pc]

        def copy(k, block, to, src=None):
            return pltpu.make_async_remote_copy(
                src_ref=slab(*block) if src is None else src, dst_ref=slab(*block),
                send_sem=send_sems.at[k], recv_sem=recv_sems.at[k], device_id=to, device_id_type=MESH)

        first = [copy(0, me, sibling, src=x_ref)]
        first += [copy(1 + j, me, (*chip, c), src=x_ref) for j, chip in enumerate(chips)]
        for cp in first:
            cp.start()
        all_ref[4 * x + 2 * y + c] = x_ref[...]
        passed = [copy(4 + j, (*chip, c), sibling) for j, chip in enumerate(chips)]
        for j, chip in enumerate(chips):
            copy(1 + j, (*chip, c), me).wait_recv()
            passed[j].start()
        copy(0, sibling, me).wait_recv()
        for j, chip in enumerate(chips):
            copy(4 + j, (*chip, 1 - c), me).wait_recv()
        for cp in first + passed:
            cp.wait_send()
        total = all_ref[0]
        for d in range(1, N_DEV):
            total = total + all_ref[d]
        out_ref[...] = total

    vmem = pl.BlockSpec(memory_space=pltpu.VMEM)
    return pl.pallas_call(
        body, out_shape=(jax.ShapeDtypeStruct(vals.shape, F32), jax.ShapeDtypeStruct((N_DEV, rows, width), F32)),
        in_specs=[vmem], out_specs=(vmem, vmem),
        scratch_shapes=[pltpu.SemaphoreType.DMA((7,)), pltpu.SemaphoreType.DMA((7,))], name="small_allreduce",
    )(vals)[0]


def adamw(w, g, m, v, name):
    shape = w.shape
    w2, g2, m2, v2 = [a.reshape((-1, shape[-1])) for a in (w, g, m, v)]
    rows, cols = w2.shape
    tr = 256 if rows % 256 == 0 else rows

    def body(w_ref, g_ref, m_ref, v_ref, d_ref, nm_ref, nv_ref):
        gv = g_ref[...]
        nm = ADAM_B1 * m_ref[...] + (1.0 - ADAM_B1) * gv
        nv = ADAM_B2 * v_ref[...] + (1.0 - ADAM_B2) * (gv * gv)
        m_hat = nm / (1.0 - ADAM_B1 ** ADAM_STEP)
        v_hat = nv / (1.0 - ADAM_B2 ** ADAM_STEP)
        d_ref[...] = -ADAM_LR * (m_hat / (jnp.sqrt(v_hat) + ADAM_EPS) + ADAM_WD * w_ref[...])
        nm_ref[...] = nm
        nv_ref[...] = nv

    blk = pl.BlockSpec((tr, cols), lambda i: (i, 0))
    out = jax.ShapeDtypeStruct((rows, cols), F32)
    outs = pl.pallas_call(
        body, grid=(rows // tr,), in_specs=[blk] * 4, out_specs=(blk,) * 3, out_shape=(out,) * 3,
        name=name, compiler_params=_params(1),
    )(w2, g2, m2, v2)
    return tuple(o.reshape(shape) for o in outs)


def adamw_summed(w, m, v, grads, from_sibling, received, me, name):
    shape = w.shape
    rows, cols = shape[-2:]
    w3, m3, v3 = [a.reshape((1, rows, cols)) for a in (w, m, v)]
    tr = _row_tile(rows)

    def body(me_ref, w_ref, m_ref, v_ref, own_ref, sib_ref, r_ref, g_ref, d_ref, nm_ref, nv_ref):
        gv = own_ref[0] + sib_ref[0]
        for j in range(3):
            gv = gv + r_ref[j].astype(F32)
        nm = ADAM_B1 * m_ref[0] + (1.0 - ADAM_B1) * gv
        nv = ADAM_B2 * v_ref[0] + (1.0 - ADAM_B2) * (gv * gv)
        m_hat = nm / (1.0 - ADAM_B1 ** ADAM_STEP)
        v_hat = nv / (1.0 - ADAM_B2 ** ADAM_STEP)
        g_ref[0] = gv
        d_ref[0] = -ADAM_LR * (m_hat / (jnp.sqrt(v_hat) + ADAM_EPS) + ADAM_WD * w_ref[0])
        nm_ref[0] = nm
        nv_ref[0] = nv

    one = pl.BlockSpec((1, tr, cols), lambda i, me_ref: (0, i, 0))
    out = jax.ShapeDtypeStruct((1, rows, cols), F32)
    outs = pl.pallas_call(
        body,
        grid_spec=pltpu.PrefetchScalarGridSpec(
            num_scalar_prefetch=1, grid=(rows // tr,),
            in_specs=[one, one, one, pl.BlockSpec((1, tr, cols), lambda i, me_ref: (me_ref[0], i, 0)),
                      pl.BlockSpec((1, tr, cols), lambda i, me_ref: (me_ref[1], i, 0)),
                      pl.BlockSpec((3, tr, cols), lambda i, me_ref: (0, i, 0))],
            out_specs=(one,) * 4),
        out_shape=(out,) * 4, name=name, compiler_params=_params(1),
    )(me, w3, m3, v3, grads, from_sibling, received)
    return tuple(o.reshape(shape) for o in outs)


IN_PIECES = (("wq_a", 0, 768), ("wk_a", 768, 1536), ("wv_a", 1536, 2304), ("w_qkvb", 2304, 5376),
             ("w_small", 5376, 5392), ("w_ggate", 5392, 6416), ("w_gatea", 6416, 7440), ("w_gateb", 7440, 8464))
IN_ROW_TILE = 128


def _piece_width(lo, hi):
    return max(hi - lo, LANES)


def _piece_segments(lo, hi):
    out = []
    for j in range(N_DEV):
        a, b = max(lo, IN_SHARD * j), min(hi, IN_SHARD * (j + 1))
        if a < b:
            out.append((j, a - IN_SHARD * j, b - IN_SHARD * j, a - lo, b - lo))
    return out


def split_in_shards(gathered):
    rows = gathered.shape[1]

    def body(g_ref, *outs):
        for (name, lo, hi), o_ref in zip(IN_PIECES, outs):
            if hi - lo < LANES:
                o_ref[...] = jnp.zeros_like(o_ref)
            for j, s0, s1, d0, d1 in _piece_segments(lo, hi):
                o_ref[:, d0:d1] = g_ref[j, :, s0:s1]

    widths = [_piece_width(lo, hi) for _, lo, hi in IN_PIECES]
    outs = pl.pallas_call(
        body, grid=(rows // IN_ROW_TILE,),
        in_specs=[pl.BlockSpec((N_DEV, IN_ROW_TILE, IN_SHARD), lambda i: (0, i, 0))],
        out_specs=tuple(pl.BlockSpec((IN_ROW_TILE, wd), lambda i: (i, 0)) for wd in widths),
        out_shape=tuple(jax.ShapeDtypeStruct((rows, wd), gathered.dtype) for wd in widths),
        name="split_in_shards", compiler_params=_params(1),
    )(gathered)
    return {name: o for (name, _, _), o in zip(IN_PIECES, outs)}


def join_in_grads(grads):
    pieces = [grads[name] for name, _, _ in IN_PIECES]
    rows = pieces[0].shape[0]

    def body(*refs):
        o_ref = refs[-1]
        for (name, lo, hi), p_ref in zip(IN_PIECES, refs[:-1]):
            for j, s0, s1, d0, d1 in _piece_segments(lo, hi):
                o_ref[j, :, s0:s1] = p_ref[:, d0:d1]

    return pl.pallas_call(
        body, grid=(rows // IN_ROW_TILE,),
        in_specs=[pl.BlockSpec((IN_ROW_TILE, p.shape[1]), lambda i: (i, 0)) for p in pieces],
        out_specs=pl.BlockSpec((N_DEV, IN_ROW_TILE, IN_SHARD), lambda i: (0, i, 0)),
        out_shape=jax.ShapeDtypeStruct((N_DEV, rows, IN_SHARD), F32), name="join_in_grads", compiler_params=_params(1),
    )(*pieces)


SMALL_VECTORS = ("ffn1_norm", "mix_norm", "ffn2_norm", "final_norm")


def _pack_small(gs):
    row = jnp.concatenate([gs["gdn_a_log"].reshape(-1), gs["gdn_dt_bias"].reshape(-1), gs["gdn_out_norm"].reshape(-1)])
    rows = [gs[n].reshape(1, D_MODEL) for n in SMALL_VECTORS]
    rows.append(jnp.pad(row, (0, D_MODEL - row.shape[0])).reshape(1, D_MODEL))
    rows.append(gs["gdn_conv_w"].reshape(-1, D_MODEL))
    packed = jnp.concatenate(rows, axis=0)
    return jnp.pad(packed, ((0, SMALL_ROWS - packed.shape[0]), (0, 0)))


def _unpack_small(packed):
    out = {n: packed[i].reshape(1, D_MODEL) for i, n in enumerate(SMALL_VECTORS)}
    row = packed[len(SMALL_VECTORS)]
    out["gdn_a_log"] = row[:GDN_HEADS].reshape(1, GDN_HEADS)
    out["gdn_dt_bias"] = row[GDN_HEADS:2 * GDN_HEADS].reshape(1, GDN_HEADS)
    out["gdn_out_norm"] = row[2 * GDN_HEADS:2 * GDN_HEADS + GDN_HEAD_DIM].reshape(1, GDN_HEAD_DIM)
    first = len(SMALL_VECTORS) + 1
    out["gdn_conv_w"] = packed[first:first + GDN_CONV * 3].reshape(GDN_CONV, 3 * GDN_WIDTH)
    return out


WEIGHTS = ("ffn1_norm", "ffn1_w_gate", "ffn1_w_up", "ffn1_w_down", "mix_norm", "w_in", "gdn_conv_w", "gdn_a_log",
           "gdn_dt_bias", "gdn_out_norm", "w_branch_a", "w_branch_b", "w_out", "ffn2_norm", "ffn2_w_gate",
           "ffn2_w_up", "ffn2_w_down", "final_norm")


def kernel(x, ffn1_norm, ffn1_w_gate, ffn1_w_up, ffn1_w_down, mix_norm, w_in, gdn_conv_w, gdn_a_log, gdn_dt_bias, gdn_out_norm, w_branch_a, w_branch_b, w_out, ffn2_norm, ffn2_w_gate, ffn2_w_up, ffn2_w_down, final_norm, loss_target, m_ffn1_norm, m_ffn1_w_gate, m_ffn1_w_up, m_ffn1_w_down, m_mix_norm, m_w_in, m_gdn_conv_w, m_gdn_a_log, m_gdn_dt_bias, m_gdn_out_norm, m_w_branch_a, m_w_branch_b, m_w_out, m_ffn2_norm, m_ffn2_w_gate, m_ffn2_w_up, m_ffn2_w_down, m_final_norm, v_ffn1_norm, v_ffn1_w_gate, v_ffn1_w_up, v_ffn1_w_down, v_mix_norm, v_w_in, v_gdn_conv_w, v_gdn_a_log, v_gdn_dt_bias, v_gdn_out_norm, v_w_branch_a, v_w_branch_b, v_w_out, v_ffn2_norm, v_ffn2_w_gate, v_ffn2_w_up, v_ffn2_w_down, v_final_norm):
    given = dict(locals())
    px, py, pc = _position()
    big_names = [n for n, _, _ in SHARDS]

    shards = [given[n][0].astype(BF16) for n in big_names] + [gdn_conv_w[0]]
    gathered = dict(zip(big_names + ["gdn_conv_w"], all_gather_shards(shards, "gather_weights")))
    w = {n: gathered[n] for n in big_names if n.startswith("ffn")}
    w.update(split_in_shards(gathered["w_in"]))
    w["w_branch_a"] = gathered["w_branch_a"].transpose(1, 0, 2).reshape(256, D_MODEL)
    w["w_branch_b"] = gathered["w_branch_b"].reshape(D_MODEL, D_MODEL)
    w["w_out"] = gathered["w_out"].reshape(D_MODEL, D_MODEL)
    conv_full = gathered["gdn_conv_w"].transpose(1, 0, 2).reshape(GDN_CONV, 3 * GDN_WIDTH)
    slots = {n: jnp.zeros(a.shape, F32) for n, a in w.items()}
    small = dict(ffn1_norm=ffn1_norm, mix_norm=mix_norm, ffn2_norm=ffn2_norm, final_norm=final_norm.reshape(1, D_MODEL),
                 gdn_a_log=gdn_a_log, gdn_dt_bias=gdn_dt_bias, gdn_out_norm=gdn_out_norm, gdn_conv_w=conv_full)

    loss_local, (grad_x, g_w, g_small) = jax.value_and_grad(_local_loss)((x[0], slots, small), loss_target[0], w)
    loss = lax.psum(loss_local, ("x", "y", "c"))

    g_big = {n: g_w[n] for n in big_names if n.startswith("ffn")}
    g_big["w_in"] = join_in_grads(g_w)
    g_big["w_branch_a"] = g_w["w_branch_a"].reshape(256, N_DEV, 128).transpose(1, 0, 2)
    g_big["w_branch_b"] = g_w["w_branch_b"].reshape(N_DEV, 128, D_MODEL)
    g_big["w_out"] = g_w["w_out"].reshape(N_DEV, 128, D_MODEL)
    g_list = [g_big[n] for n in big_names]
    core = pc.astype(jnp.int32).reshape(1)
    me = 4 * px + 2 * py + pc
    me_and_chip = jnp.stack([me, 2 * px + py]).astype(jnp.int32)
    from_sibling = exchange_with_sibling(g_list)
    partials = [add_sibling(g, r, core, "rs_add_" + n) for n, g, r in zip(big_names, g_list, from_sibling)]
    from_chips = exchange_with_chips(partials)

    results = {}
    for n, g, sib, recv in zip(big_names, g_list, from_sibling, from_chips):
        results[n] = adamw_summed(given[n], given["m_" + n], given["v_" + n], g, sib, recv, me_and_chip, "adamw_" + n)

    small_sum = _unpack_small(all_reduce_small(_pack_small(g_small)))
    conv_cols = CONV_SHARD[1]
    small_sum["gdn_conv_w"] = lax.dynamic_slice(small_sum["gdn_conv_w"], (0, me * conv_cols), (GDN_CONV, conv_cols))
    for n in WEIGHTS:
        if n not in results:
            g = small_sum[n].reshape(given[n].shape)
            results[n] = (g,) + adamw(given[n], g, given["m_" + n], given["v_" + n], "adamw_" + n)

    outs = [[results[n][i] for n in WEIGHTS] for i in range(4)]
    return (loss, grad_x[None], *outs[0], *outs[1], *outs[2], *outs[3])
```

```python
import jax
import jax.numpy as jnp
from jax import lax
from jax.experimental import pallas as pl
from jax.experimental.pallas import tpu as pltpu

F32 = jnp.float32
BF16 = jnp.bfloat16
HI = lax.Precision.HIGHEST
MESH = pl.DeviceIdType.MESH

N_DEV = 8
D_MODEL = 1024
EPS = 1e-6
ROPE_THETA = 10000.0
DSW_DILATIONS = (1, 4, 16)
DSW_HEADS_PER_GROUP = 4
DSW_HEAD_DIM = 64
DSW_BLOCK = 128
GDN_HEADS = 8
GDN_HEAD_DIM = 128
GDN_WIDTH = 1024
GDN_CONV = 4
GDN_CHUNK = 64

ADAM_LR = 0.001
ADAM_B1 = 0.9
ADAM_B2 = 0.999
ADAM_EPS = 1e-08
ADAM_WD = 0.01
ADAM_STEP = 10

VMEM_LIMIT_BYTES = 56 * 1024 * 1024
LANES = 128

NN = (((1,), (0,)), ((), ()))
NT = (((1,), (1,)), ((), ()))
TN = (((0,), (0,)), ((), ()))


def _params(n_grid):
    return pltpu.CompilerParams(dimension_semantics=("arbitrary",) * n_grid, vmem_limit_bytes=VMEM_LIMIT_BYTES)


def _tile(n, pref):
    best = None
    t = LANES
    while t <= min(n, pref):
        if n % t == 0:
            best = t
        t += LANES
    return n if best is None else best


def _matmul(a, b, *, name, ta=False, tb=False, res=None, scale=1.0):
    K, M = a.shape if ta else a.shape[::-1]
    N = b.shape[0] if tb else b.shape[1]
    assert (b.shape[1] if tb else b.shape[0]) == K, (a.shape, b.shape, ta, tb)
    tm = _tile(M, 512)
    tn = _tile(N, 512)
    dn = (((0 if ta else 1,), (1 if tb else 0,)), ((), ()))

    def body(*refs):
        a_ref, b_ref = refs[:2]
        o_ref = refs[-1]
        acc = lax.dot_general(a_ref[...].astype(BF16), b_ref[...].astype(BF16), dn, preferred_element_type=F32)
        if scale != 1.0:
            acc = acc * scale
        if res is not None:
            acc = refs[2][...] + acc
        o_ref[...] = acc

    a_spec = pl.BlockSpec((K, tm), lambda i, j: (0, i)) if ta else pl.BlockSpec((tm, K), lambda i, j: (i, 0))
    b_spec = pl.BlockSpec((tn, K), lambda i, j: (j, 0)) if tb else pl.BlockSpec((K, tn), lambda i, j: (0, j))
    o_spec = pl.BlockSpec((tm, tn), lambda i, j: (i, j))
    ins, specs = [a, b], [a_spec, b_spec]
    if res is not None:
        ins.append(res)
        specs.append(o_spec)
    return pl.pallas_call(
        body, grid=(M // tm, N // tn), in_specs=specs, out_specs=o_spec,
        out_shape=jax.ShapeDtypeStruct((M, N), F32), name=name, compiler_params=_params(2),
    )(*ins)


def _make_mm(name, scale=1.0, with_res=False):
    @jax.custom_vjp
    def op(a, w, slot, res):
        return _matmul(a, w, name=name, res=res if with_res else None, scale=scale)

    def fwd(a, w, slot, res):
        return op(a, w, slot, res), (a, w)

    def bwd(saved, g):
        a, w = saved
        da = _matmul(g, w, name=name + "_da", tb=True, scale=scale)
        dw = _matmul(a, g, name=name + "_dw", ta=True, scale=scale)
        return da, None, dw, (g if with_res else None)

    op.defvjp(fwd, bwd)
    return op


def mm(a, w, slot, name):
    return _make_mm(name)(a, w, slot, None)


def mm_res(a, w, slot, res, scale, name):
    return _make_mm(name, scale=scale, with_res=True)(a, w, slot, res)


def _rw_specs(arrs, tm, nblk):
    return [pl.BlockSpec((tm, a.shape[1] // nblk), lambda i, j: (i, j)) for a in arrs]


def _rowwise_fwd(fn, name, rows, consts, params, tm, nblk):
    n_rows = rows[0].shape[0]
    tm = min(tm, n_rows)
    ins = list(rows) + list(consts)
    avals = [jax.ShapeDtypeStruct((tm, a.shape[1] // nblk), a.dtype) for a in ins]
    avals += [jax.ShapeDtypeStruct(p.shape, p.dtype) for p in params]
    out_avals = jax.eval_shape(fn, *avals)
    n_in = len(ins) + len(params)

    def body(*refs):
        outs = fn(*[r[...] for r in refs[:n_in]])
        for r, o in zip(refs[n_in:], outs):
            r[...] = o.astype(r.dtype)

    return pl.pallas_call(
        body, grid=(n_rows // tm, nblk),
        in_specs=_rw_specs(ins, tm, nblk) + [pl.BlockSpec(p.shape, lambda i, j: (0, 0)) for p in params],
        out_specs=tuple(pl.BlockSpec((tm, o.shape[1]), lambda i, j: (i, j)) for o in out_avals),
        out_shape=tuple(jax.ShapeDtypeStruct((n_rows, o.shape[1] * nblk), o.dtype) for o in out_avals),
        name=name, compiler_params=_params(2),
    )(*ins, *params)


def _rowwise_bwd(fn, name, rows, consts, params, cts, tm, nblk):
    n_rows = rows[0].shape[0]
    tm = min(tm, n_rows)
    nr, nc, npar, nct = len(rows), len(consts), len(params), len(cts)

    def body(*refs):
        rv = [r[...] for r in refs[:nr]]
        cv = [r[...] for r in refs[nr:nr + nc]]
        pv = [r[...] for r in refs[nr + nc:nr + nc + npar]]
        ctv = [r[...] for r in refs[nr + nc + npar:nr + nc + npar + nct]]
        outs = refs[nr + nc + npar + nct:]
        _, vjp = jax.vjp(lambda *d: fn(*d[:nr], *cv, *d[nr:]), *rv, *pv)
        grads = vjp(tuple(ctv))
        for k in range(nr):
            outs[k][...] = grads[k]
        first = jnp.logical_and(pl.program_id(0) == 0, pl.program_id(1) == 0)
        for k in range(npar):
            ref = outs[nr + k]

            @pl.when(first)
            def _(ref=ref):
                ref[...] = jnp.zeros_like(ref)

            ref[...] += grads[nr + k]

    ins = list(rows) + list(consts)
    return pl.pallas_call(
        body, grid=(n_rows // tm, nblk),
        in_specs=(_rw_specs(ins, tm, nblk) + [pl.BlockSpec(p.shape, lambda i, j: (0, 0)) for p in params]
                  + _rw_specs(cts, tm, nblk)),
        out_specs=tuple(_rw_specs(rows, tm, nblk) + [pl.BlockSpec(p.shape, lambda i, j: (0, 0)) for p in params]),
        out_shape=tuple([jax.ShapeDtypeStruct(a.shape, F32) for a in rows]
                        + [jax.ShapeDtypeStruct(p.shape, F32) for p in params]),
        name=name, compiler_params=_params(2),
    )(*ins, *params, *cts)


def rowwise(fn, name, rows, consts=(), params=(), tm=256, nblk=1):
    rows, consts, params = tuple(rows), tuple(consts), tuple(params)

    @jax.custom_vjp
    def op(rows, consts, params):
        return _rowwise_fwd(fn, name, rows, consts, params, tm, nblk)

    def fwd(rows, consts, params):
        return op(rows, consts, params), (rows, consts, params)

    def bwd(saved, cts):
        rows, consts, params = saved
        grads = _rowwise_bwd(fn, name + "_bwd", rows, consts, params, tuple(cts), tm, nblk)
        return tuple(grads[:len(rows)]), None, tuple(grads[len(rows):])

    op.defvjp(fwd, bwd)
    return op(rows, consts, params)


def _merge_fn(ga, gb, pa, pb):
    return (jax.nn.sigmoid(ga) * pa + jax.nn.sigmoid(gb) * pb,)


def _outnorm_gate_fn(o, gate, gain):
    y = o * lax.rsqrt(jnp.mean(o * o, axis=-1, keepdims=True) + EPS) * gain
    return (y * (gate * jax.nn.sigmoid(gate)),)


def _beta_decay_fn(beta_raw, decay_raw, a_log, dt_bias):
    z = decay_raw + dt_bias
    softplus = jnp.maximum(z, 0.0) + jnp.log(1.0 + jnp.exp(-jnp.abs(z)))
    g = -jnp.exp(a_log) * softplus
    rows = g.shape[0]
    ii = lax.broadcasted_iota(jnp.int32, (rows, rows), 0)
    jj = lax.broadcasted_iota(jnp.int32, (rows, rows), 1)
    same_chunk_before = jnp.logical_and(jj <= ii, jj // GDN_CHUNK == ii // GDN_CHUNK).astype(F32)
    gcum = lax.dot_general(same_chunk_before, g, NN, precision=HI, preferred_element_type=F32)
    return jax.nn.sigmoid(beta_raw), gcum


def _combine_fn(o0, o1, o2, l0, l1, l2):
    m = lax.stop_gradient(jnp.maximum(jnp.maximum(l0, l1), l2))
    e0, e1, e2 = jnp.exp(l0 - m), jnp.exp(l1 - m), jnp.exp(l2 - m)
    return ((e0 * o0 + e1 * o1 + e2 * o2) / (e0 + e1 + e2),)


def _loss_fn(x, target, gain):
    y = x * lax.rsqrt(jnp.mean(x * x, axis=-1, keepdims=True) + EPS) * gain
    err = y - target
    return (0.5 * jnp.mean(err * err, axis=-1, keepdims=True),)


def _rope_call(x, cos, sin, name):
    n_rows, width = x.shape
    tm = 512

    def body(x_ref, c_ref, s_ref, o_ref):
        v = x_ref[...]
        lane = lax.broadcasted_iota(jnp.int32, v.shape, 1)
        low = (lane % DSW_HEAD_DIM) < DSW_HEAD_DIM // 2
        half = DSW_HEAD_DIM // 2
        swapped = jnp.where(low, pltpu.roll(v, LANES - half, 1), pltpu.roll(v, half, 1))
        o_ref[...] = v * c_ref[...] + swapped * s_ref[...]

    tab = pl.BlockSpec((tm, LANES), lambda i, j: (i, 0))
    blk = pl.BlockSpec((tm, LANES), lambda i, j: (i, j))
    return pl.pallas_call(
        body, grid=(n_rows // tm, width // LANES), in_specs=[blk, tab, tab], out_specs=blk,
        out_shape=jax.ShapeDtypeStruct(x.shape, F32), name=name, compiler_params=_params(2),
    )(x, cos, sin)


def rope(x, cos, sin, name):
    @jax.custom_vjp
    def op(x):
        return _rope_call(x, cos, sin, name)

    def fwd(x):
        return op(x), None

    def bwd(_, g):
        return (_rope_call(g, cos, -sin, name + "_bwd"),)

    op.defvjp(fwd, bwd)
    return op(x)


def _rope_tables(n_tokens):
    half = DSW_HEAD_DIM // 2
    inv_freq = ROPE_THETA ** (-jnp.arange(half, dtype=F32) / half)
    ang = jnp.arange(n_tokens, dtype=F32)[:, None] * inv_freq[None, :]
    cos, sin = jnp.cos(ang), jnp.sin(ang)
    return jnp.tile(jnp.concatenate([cos, cos], 1), (1, 2)), jnp.tile(jnp.concatenate([-sin, sin], 1), (1, 2))


def _attn_probs(q, kp, kc, group, n):
    blk = DSW_BLOCK
    k = _each(lambda a, b: jnp.concatenate([a, b], axis=0).astype(BF16), kp, kc)
    s = _each(lambda a, b: lax.dot_general(a.astype(BF16), b, NT, preferred_element_type=F32)
              * (DSW_HEAD_DIM ** -0.5), q, k)
    blocks_per_seq = jnp.where(group == 0, 16, jnp.where(group == 1, 4, 1))
    first = (n % blocks_per_seq) == 0
    qi = lax.broadcasted_iota(jnp.int32, (blk, 2 * blk), 0)
    kj = lax.broadcasted_iota(jnp.int32, (blk, 2 * blk), 1)
    dist = qi + blk - kj
    valid = (dist >= 0) & (dist <= blk) & jnp.logical_or(kj >= blk, jnp.logical_not(first))
    s = _each(lambda a: jnp.where(valid, a, -1e30), s)
    m = _each(lambda a: jnp.max(a, axis=-1, keepdims=True), s)
    p = _each(lambda a, b: jnp.exp(a - b), s, m)
    l = _each(lambda a: jnp.sum(a, axis=-1, keepdims=True), p)
    return _each(lambda a, b: a / b, p, l), _each(lambda a, b: a + jnp.log(b), m, l), k


def _attn_specs(n_tokens):
    blk, hpg = DSW_BLOCK, DSW_HEADS_PER_GROUP
    cur = pl.BlockSpec((hpg, blk, DSW_HEAD_DIM), lambda g, n: (g, n, 0))
    prev = pl.BlockSpec((hpg, blk, DSW_HEAD_DIM), lambda g, n: (g, jnp.maximum(n - 1, 0), 0))
    return cur, prev


def _attn_fwd(q, k, v):
    nh, n_tokens, hd = q.shape
    hpg = DSW_HEADS_PER_GROUP
    cur, prev = _attn_specs(n_tokens)

    def body(q_ref, kp_ref, kc_ref, vp_ref, vc_ref, o_ref, l_ref):
        heads = range(hpg)
        p, lse, _ = _attn_probs([q_ref[h] for h in heads], [kp_ref[h] for h in heads], [kc_ref[h] for h in heads],
                                pl.program_id(0), pl.program_id(1))
        vv = [jnp.concatenate([vp_ref[h], vc_ref[h]], axis=0).astype(BF16) for h in heads]
        o = _each(lambda a, b: lax.dot_general(a.astype(BF16), b, NN, preferred_element_type=F32), p, vv)
        for h in heads:
            o_ref[h] = o[h]
            l_ref[h] = jnp.broadcast_to(lse[h], (DSW_BLOCK, hd))

    return pl.pallas_call(
        body, grid=(nh // hpg, n_tokens // DSW_BLOCK), in_specs=[cur, prev, cur, prev, cur], out_specs=(cur, cur),
        out_shape=(jax.ShapeDtypeStruct(q.shape, F32), jax.ShapeDtypeStruct(q.shape, F32)),
        name="attn_fwd", compiler_params=_params(2),
    )(q, k, k, v, v)


def _attn_bwd(q, k, v, do, dlse):
    nh, n_tokens, hd = q.shape
    hpg = DSW_HEADS_PER_GROUP
    nblk = n_tokens // DSW_BLOCK
    cur, prev = _attn_specs(n_tokens)
    part = pl.BlockSpec((hpg, 1, 2 * DSW_BLOCK, hd), lambda g, n: (g, n, 0, 0))
    scale = DSW_HEAD_DIM ** -0.5

    def body(q_ref, kp_ref, kc_ref, vp_ref, vc_ref, do_ref, dl_ref, dq_ref, dk_ref, dv_ref):
        heads = range(hpg)
        qs = [q_ref[h] for h in heads]
        p, _, kb = _attn_probs(qs, [kp_ref[h] for h in heads], [kc_ref[h] for h in heads],
                               pl.program_id(0), pl.program_id(1))
        qb = _each(lambda a: a.astype(BF16), qs)
        vv = [jnp.concatenate([vp_ref[h], vc_ref[h]], axis=0).astype(BF16) for h in heads]
        dob = [do_ref[h].astype(BF16) for h in heads]
        dp = _each(lambda a, b: lax.dot_general(a, b, NT, preferred_element_type=F32), dob, vv)
        dv = _each(lambda a, b: lax.dot_general(a.astype(BF16), b, TN, preferred_element_type=F32), p, dob)
        dl = [jnp.sum(dl_ref[h], axis=-1, keepdims=True) for h in heads]
        ds = _each(lambda a, b, c: (a * (b - jnp.sum(b * a, axis=-1, keepdims=True) + c) * scale).astype(BF16),
                   p, dp, dl)
        dq = _each(lambda a, b: lax.dot_general(a, b, NN, preferred_element_type=F32), ds, kb)
        dk = _each(lambda a, b: lax.dot_general(a, b, TN, preferred_element_type=F32), ds, qb)
        for h in heads:
            dq_ref[h] = dq[h]
            dk_ref[h, 0] = dk[h]
            dv_ref[h, 0] = dv[h]

    dq, dkp, dvp = pl.pallas_call(
        body, grid=(nh // hpg, nblk), in_specs=[cur, prev, cur, prev, cur, cur, cur], out_specs=(cur, part, part),
        out_shape=(jax.ShapeDtypeStruct(q.shape, F32),
                   jax.ShapeDtypeStruct((nh, nblk, 2 * DSW_BLOCK, hd), F32),
                   jax.ShapeDtypeStruct((nh, nblk, 2 * DSW_BLOCK, hd), F32)),
        name="attn_bwd", compiler_params=_params(2),
    )(q, k, k, v, v, do, dlse)

    def fold(partial):
        own = partial[:, :, DSW_BLOCK:]
        from_next = jnp.pad(partial[:, 1:, :DSW_BLOCK], ((0, 0), (0, 1), (0, 0), (0, 0)))
        return (own + from_next).reshape(nh, n_tokens, hd)

    return dq, fold(dkp), fold(dvp)


@jax.custom_vjp
def attention(q, k, v):
    return _attn_fwd(q, k, v)


def _attention_fwd(q, k, v):
    return _attn_fwd(q, k, v), (q, k, v)


def _attention_bwd(saved, cts):
    q, k, v = saved
    return _attn_bwd(q, k, v, cts[0], cts[1])


attention.defvjp(_attention_fwd, _attention_bwd)


def _to_heads(a):
    n_tokens = a.shape[0]
    outs = []
    for gi, d in enumerate(DSW_DILATIONS):
        blk = a[:, gi * 256:(gi + 1) * 256].reshape(n_tokens // d, d, DSW_HEADS_PER_GROUP, DSW_HEAD_DIM)
        outs.append(blk.transpose(2, 1, 0, 3).reshape(DSW_HEADS_PER_GROUP, n_tokens, DSW_HEAD_DIM))
    return jnp.concatenate(outs, 0)


def _from_heads(a):
    n_tokens = a.shape[1]
    outs = []
    for gi, d in enumerate(DSW_DILATIONS):
        blk = a[gi * 4:(gi + 1) * 4].reshape(DSW_HEADS_PER_GROUP, d, n_tokens // d, DSW_HEAD_DIM)
        outs.append(blk.transpose(2, 1, 0, 3).reshape(n_tokens, DSW_HEADS_PER_GROUP * DSW_HEAD_DIM))
    return outs


CONV_TILE = 512


def _shift_down(x, k, rows):
    return x if k == 0 else jnp.where(rows >= k, pltpu.roll(x, k, 0), 0.0)


def _shift_up(x, k, rows):
    n = x.shape[0]
    return x if k == 0 else jnp.where(rows < n - k, pltpu.roll(x, n - k, 0), 0.0)


def _conv_pre(x, w):
    rows = lax.broadcasted_iota(jnp.int32, x.shape, 0)
    acc = x * w[GDN_CONV - 1:GDN_CONV]
    for k in range(1, GDN_CONV):
        acc = acc + _shift_down(x, k, rows) * w[GDN_CONV - 1 - k:GDN_CONV - k]
    return acc, rows


def _conv_fwd(x, w):
    n_tokens, width = x.shape
    big = pl.BlockSpec((n_tokens, CONV_TILE), lambda j: (0, j))
    wsp = pl.BlockSpec((GDN_CONV, CONV_TILE), lambda j: (0, j))

    def body(x_ref, w_ref, o_ref):
        acc, _ = _conv_pre(x_ref[...], w_ref[...])
        o_ref[...] = acc * jax.nn.sigmoid(acc)

    return pl.pallas_call(
        body, grid=(width // CONV_TILE,), in_specs=[big, wsp], out_specs=big,
        out_shape=jax.ShapeDtypeStruct(x.shape, F32), name="conv_fwd", compiler_params=_params(1),
    )(x, w)


def _conv_bwd(x, w, dy):
    n_tokens, width = x.shape
    big = pl.BlockSpec((n_tokens, CONV_TILE), lambda j: (0, j))
    wsp = pl.BlockSpec((GDN_CONV, CONV_TILE), lambda j: (0, j))

    def body(x_ref, w_ref, dy_ref, dx_ref, dw_ref):
        xv, wv = x_ref[...], w_ref[...]
        acc, rows = _conv_pre(xv, wv)
        sg = jax.nn.sigmoid(acc)
        dacc = dy_ref[...] * (sg + acc * sg * (1.0 - sg))
        dx = dacc * wv[GDN_CONV - 1:GDN_CONV]
        for k in range(1, GDN_CONV):
            dx = dx + _shift_up(dacc, k, rows) * wv[GDN_CONV - 1 - k:GDN_CONV - k]
        dx_ref[...] = dx
        for k in range(GDN_CONV):
            dw_ref[GDN_CONV - 1 - k:GDN_CONV - k, :] = jnp.sum(dacc * _shift_down(xv, k, rows), axis=0, keepdims=True)

    return pl.pallas_call(
        body, grid=(width // CONV_TILE,), in_specs=[big, wsp, big], out_specs=(big, wsp),
        out_shape=(jax.ShapeDtypeStruct(x.shape, F32), jax.ShapeDtypeStruct(w.shape, F32)),
        name="conv_bwd", compiler_params=_params(1),
    )(x, w, dy)


@jax.custom_vjp
def conv_silu(x, w):
    return _conv_fwd(x, w)


def _conv_silu_fwd(x, w):
    return _conv_fwd(x, w), (x, w)


def _conv_silu_bwd(saved, g):
    return _conv_bwd(saved[0], saved[1], g)


conv_silu.defvjp(_conv_silu_fwd, _conv_silu_bwd)


def _dot(a, b, dn=NN):
    return lax.dot_general(a, b, dn, precision=HI, preferred_element_type=F32)


def _dot3(a, b, dn=NN):
    return lax.dot_general(a, b, dn, precision=lax.Precision.HIGH, preferred_element_type=F32)


def _bf16_dot(a, b, dn):
    return lax.dot_general(a.astype(BF16), b.astype(BF16), dn, preferred_element_type=F32)


_DOT_GRADS = {NN: (("g", "b", NT), ("a", "g", TN)), NT: (("g", "b", NN), ("g", "a", TN)),
              TN: (("b", "g", NT), ("a", "g", NN))}


def _make_bdot(dn):
    @jax.custom_vjp
    def op(a, b):
        return _bf16_dot(a, b, dn)

    def fwd(a, b):
        return op(a, b), (a, b)

    def bwd(saved, g):
        vals = dict(a=saved[0], b=saved[1], g=g)
        return tuple(_bf16_dot(vals[x], vals[y], form) for x, y, form in _DOT_GRADS[dn])

    op.defvjp(fwd, bwd)
    return op


_BDOTS = {dn: _make_bdot(dn) for dn in (NN, NT, TN)}


def _bdot(a, b, dn=NN):
    return _BDOTS[dn](a, b)


def _each(fn, *lists):
    return [fn(*items) for items in zip(*lists)]


def _gdn_chunks(q, k, v, b, gcum, state):
    c = GDN_CHUNK
    ii = lax.broadcasted_iota(jnp.int32, (c, c), 0)
    jj = lax.broadcasted_iota(jnp.int32, (c, c), 1)
    eye = (ii == jj).astype(F32)
    qn = _each(lambda x: x * lax.rsqrt(jnp.sum(x * x, axis=-1, keepdims=True) + EPS) * (GDN_HEAD_DIM ** -0.5), q)
    kn = _each(lambda x: x * lax.rsqrt(jnp.sum(x * x, axis=-1, keepdims=True) + EPS), k)
    gcum_i = _each(lambda x: jnp.broadcast_to(x, (c, c)), gcum)
    gcum_j = _each(jnp.transpose, gcum_i)
    decay = _each(lambda x, y: jnp.exp(jnp.where(jj <= ii, x - y, -1e30)), gcum_i, gcum_j)
    g_last = _each(lambda x: x[c - 1:c, :], gcum)
    e_gcum = _each(jnp.exp, gcum)
    kbeta = _each(lambda x, y: x * y, kn, b)
    vbeta = _each(lambda x, y: x * y, v, b)
    m = _each(lambda x, y, d: jnp.where(jj < ii, _bdot(x, y, NT) * d, 0.0), kbeta, kn, decay)
    inv = _each(lambda x: eye - x, m)
    power = _each(lambda x: _dot3(x, x), m)
    for step in range(5):
        inv = _each(lambda x, p: x + _dot3(x, p), inv, power)
        if step < 4:
            power = _each(lambda p: _dot3(p, p), power)
    u = _each(_dot3, inv, vbeta)
    w = _each(lambda x, y, e: _dot3(x, y * e), inv, kbeta, e_gcum)
    a_qk = _each(lambda x, y, d: _bdot(x, y, NT) * d, qn, kn, decay)
    v_new = _each(lambda x, y, s: x - _bdot(y, s), u, w, state)
    o = _each(lambda x, e, s, a, vn: _bdot(x * e, s) + _bdot(a, vn), qn, e_gcum, state, a_qk, v_new)
    new_state = _each(lambda s, gl, x, gc, vn: s * jnp.exp(gl) + _bdot(x * jnp.exp(gl - gc), vn, TN),
                      state, g_last, kn, gcum, v_new)
    return o, new_state


GDN_HEADS_PER_STEP = 8


GDN_TIME_TILE = 256


def _gdn_specs(n_tokens, reverse):
    hb, hd, tt = GDN_HEADS_PER_STEP, GDN_HEAD_DIM, GDN_TIME_TILE
    nb, nt = GDN_HEADS // hb, n_tokens // tt

    def when(t):
        return nt - 1 - t if reverse else t

    q = pl.BlockSpec((tt, hb * hd), lambda h, t: (when(t), h))
    k = pl.BlockSpec((tt, hb * hd), lambda h, t: (when(t), nb + h))
    v = pl.BlockSpec((tt, hb * hd), lambda h, t: (when(t), 2 * nb + h))
    vec = pl.BlockSpec((tt, hb), lambda h, t: (when(t), h))
    states = pl.BlockSpec((hb, tt // GDN_CHUNK, hd, hd), lambda h, t: (h, when(t), 0, 0))
    return q, k, v, vec, states


def _gdn_fwd(qkv, beta, g):
    n_tokens = qkv.shape[0]
    hb, hd, tt = GDN_HEADS_PER_STEP, GDN_HEAD_DIM, GDN_TIME_TILE
    n_chunks = tt // GDN_CHUNK
    q_s, k_s, v_s, vec, st = _gdn_specs(n_tokens, False)

    def body(q_ref, k_ref, v_ref, b_ref, g_ref, o_ref, st_ref, state):
        @pl.when(pl.program_id(1) == 0)
        def _():
            state[...] = jnp.zeros_like(state)

        def step(c, carry):
            r = pl.ds(pl.multiple_of(c * GDN_CHUNK, GDN_CHUNK), GDN_CHUNK)
            cols = [slice(h * hd, (h + 1) * hd) for h in range(hb)]
            old = [state[h] for h in range(hb)]
            o, new = _gdn_chunks(
                [q_ref[r, cs] for cs in cols], [k_ref[r, cs] for cs in cols], [v_ref[r, cs] for cs in cols],
                [b_ref[r, h:h + 1] for h in range(hb)], [g_ref[r, h:h + 1] for h in range(hb)], old)
            for h in range(hb):
                st_ref[h, c] = old[h]
                o_ref[r, cols[h]] = o[h]
                state[h] = new[h]
            return carry

        lax.fori_loop(0, n_chunks, step, 0)

    return pl.pallas_call(
        body, grid=(GDN_HEADS // hb, n_tokens // tt), in_specs=[q_s, k_s, v_s, vec, vec], out_specs=(q_s, st),
        out_shape=(jax.ShapeDtypeStruct((n_tokens, GDN_WIDTH), F32),
                   jax.ShapeDtypeStruct((GDN_HEADS, n_tokens // GDN_CHUNK, hd, hd), F32)),
        scratch_shapes=[pltpu.VMEM((hb, hd, hd), F32)],
        name="gdn_fwd", compiler_params=_params(2),
    )(qkv, qkv, qkv, beta, g)


def _gdn_bwd(qkv, beta, g, states, do):
    n_tokens = qkv.shape[0]
    hb, hd, tt = GDN_HEADS_PER_STEP, GDN_HEAD_DIM, GDN_TIME_TILE
    n_chunks = tt // GDN_CHUNK
    q_s, k_s, v_s, vec, st = _gdn_specs(n_tokens, True)

    def body(q_ref, k_ref, v_ref, b_ref, g_ref, st_ref, do_ref, dq_ref, dk_ref, dv_ref, db_ref, dg_ref, dstate):
        @pl.when(pl.program_id(1) == 0)
        def _():
            dstate[...] = jnp.zeros_like(dstate)

        def step(i, carry):
            c = n_chunks - 1 - i
            r = pl.ds(pl.multiple_of(c * GDN_CHUNK, GDN_CHUNK), GDN_CHUNK)
            cols = [slice(h * hd, (h + 1) * hd) for h in range(hb)]
            args = ([q_ref[r, cs] for cs in cols], [k_ref[r, cs] for cs in cols], [v_ref[r, cs] for cs in cols],
                    [b_ref[r, h:h + 1] for h in range(hb)], [g_ref[r, h:h + 1] for h in range(hb)],
                    [st_ref[h, c] for h in range(hb)])
            cts = ([do_ref[r, cs] for cs in cols], [dstate[h] for h in range(hb)])
            dq, dk, dv, db, dg, dst = jax.vjp(_gdn_chunks, *args)[1](cts)
            for h in range(hb):
                dq_ref[r, cols[h]] = dq[h]
                dk_ref[r, cols[h]] = dk[h]
                dv_ref[r, cols[h]] = dv[h]
                db_ref[r, h:h + 1] = db[h]
                dg_ref[r, h:h + 1] = dg[h]
                dstate[h] = dst[h]
            return carry

        lax.fori_loop(0, n_chunks, step, 0)

    wide = jax.ShapeDtypeStruct((n_tokens, GDN_WIDTH), F32)
    thin = jax.ShapeDtypeStruct(beta.shape, F32)
    dq, dk, dv, db, dg = pl.pallas_call(
        body, grid=(GDN_HEADS // hb, n_tokens // tt), in_specs=[q_s, k_s, v_s, vec, vec, st, q_s],
        out_specs=(q_s, q_s, q_s, vec, vec), out_shape=(wide, wide, wide, thin, thin),
        scratch_shapes=[pltpu.VMEM((hb, hd, hd), F32)],
        name="gdn_bwd", compiler_params=_params(2),
    )(qkv, qkv, qkv, beta, g, states, do)
    return jnp.concatenate([dq, dk, dv], axis=1), db, dg


@jax.custom_vjp
def gated_delta(qkv, beta, g):
    return _gdn_fwd(qkv, beta, g)[0]


def _gated_delta_fwd(qkv, beta, g):
    o, states = _gdn_fwd(qkv, beta, g)
    return o, (qkv, beta, g, states)


def _gated_delta_bwd(saved, do):
    return _gdn_bwd(*saved, do)


gated_delta.defvjp(_gated_delta_fwd, _gated_delta_bwd)


FFN_ROW_TILE = 256


def _resident(shape):
    return pl.BlockSpec(shape, lambda i: (0,) * len(shape), pipeline_mode=pl.Buffered(1))


def _ffn_fwd(x, gain, wg, wu, wd, name):
    n_tokens, d = x.shape
    n_shards, n, _ = wg.shape
    tm = FFN_ROW_TILE

    def body(x_ref, gain_ref, wg_ref, wu_ref, wd_ref, o_ref, g_ref, u_ref):
        xv = x_ref[...]
        h = (xv * lax.rsqrt(jnp.mean(xv * xv, axis=-1, keepdims=True) + EPS) * gain_ref[...]).astype(BF16)
        acc = jnp.zeros((tm, d), F32)
        for j in range(n_shards):
            g = lax.dot_general(h, wg_ref[j], NT, preferred_element_type=F32)
            u = lax.dot_general(h, wu_ref[j], NT, preferred_element_type=F32)
            g_ref[j] = g
            u_ref[j] = u
            a = (g * jax.nn.sigmoid(g) * u).astype(BF16)
            acc = acc + lax.dot_general(a, wd_ref[j], NN, preferred_element_type=F32)
        o_ref[...] = xv + 0.5 * acc

    row = pl.BlockSpec((tm, d), lambda i: (i, 0))
    hid = pl.BlockSpec((n_shards, tm, n), lambda i: (0, i, 0))
    return pl.pallas_call(
        body, grid=(n_tokens // tm,),
        in_specs=[row, _resident(gain.shape), _resident(wg.shape), _resident(wu.shape), _resident(wd.shape)],
        out_specs=(row, hid, hid),
        out_shape=(jax.ShapeDtypeStruct(x.shape, F32), jax.ShapeDtypeStruct((n_shards, n_tokens, n), F32),
                   jax.ShapeDtypeStruct((n_shards, n_tokens, n), F32)),
        name=name, compiler_params=_params(1),
    )(x, gain, wg, wu, wd)


def _ffn_bwd_rows(x, gain, dy, g, u, wg, wu, wd, name):
    n_tokens, d = x.shape
    n_shards, n, _ = wg.shape
    tm = FFN_ROW_TILE

    def body(x_ref, gain_ref, dy_ref, g_ref, u_ref, wg_ref, wu_ref, wd_ref,
             dx_ref, dgain_ref, h_ref, dyh_ref, a_ref, dg_ref, du_ref):
        xv, dyv, gain_v = x_ref[...], dy_ref[...], gain_ref[...]
        r = lax.rsqrt(jnp.mean(xv * xv, axis=-1, keepdims=True) + EPS)
        xhat = xv * r
        h_ref[...] = (xhat * gain_v).astype(BF16)
        dyh = (0.5 * dyv).astype(BF16)
        dyh_ref[...] = dyh
        dh = jnp.zeros((tm, d), F32)
        for j in range(n_shards):
            da = lax.dot_general(dyh, wd_ref[j], NT, preferred_element_type=F32)
            gv, uv = g_ref[j], u_ref[j]
            sg = jax.nn.sigmoid(gv)
            silu = gv * sg
            a_ref[j] = (silu * uv).astype(BF16)
            dg = (da * uv * (sg + silu * (1.0 - sg))).astype(BF16)
            du = (da * silu).astype(BF16)
            dg_ref[j] = dg
            du_ref[j] = du
            dh = dh + lax.dot_general(dg, wg_ref[j], NN, preferred_element_type=F32)
            dh = dh + lax.dot_general(du, wu_ref[j], NN, preferred_element_type=F32)
        dxhat = dh * gain_v
        dx_ref[...] = dyv + r * (dxhat - xhat * jnp.mean(dxhat * xhat, axis=-1, keepdims=True))

        @pl.when(pl.program_id(0) == 0)
        def _():
            dgain_ref[...] = jnp.zeros_like(dgain_ref)

        dgain_ref[...] += jnp.sum(dh * xhat, axis=0, keepdims=True)

    row = pl.BlockSpec((tm, d), lambda i: (i, 0))
    hid = pl.BlockSpec((n_shards, tm, n), lambda i: (0, i, 0))
    hid_shape = (n_shards, n_tokens, n)
    return pl.pallas_call(
        body, grid=(n_tokens // tm,),
        in_specs=[row, _resident(gain.shape), row, hid, hid, _resident(wg.shape), _resident(wu.shape),
                  _resident(wd.shape)],
        out_specs=(row, pl.BlockSpec(gain.shape, lambda i: (0, 0)), row, row, hid, hid, hid),
        out_shape=(jax.ShapeDtypeStruct(x.shape, F32), jax.ShapeDtypeStruct(gain.shape, F32),
                   jax.ShapeDtypeStruct(x.shape, BF16), jax.ShapeDtypeStruct(x.shape, BF16),
                   jax.ShapeDtypeStruct(hid_shape, BF16), jax.ShapeDtypeStruct(hid_shape, BF16),
                   jax.ShapeDtypeStruct(hid_shape, BF16)),
        name=name, compiler_params=_params(1),
    )(x, gain, dy, g, u, wg, wu, wd)


def _ffn_bwd_weights(h, dyh, a, dg, du, name):
    n_shards, n_tokens, n = a.shape
    d = h.shape[1]

    def body(h_ref, dyh_ref, a_ref, dg_ref, du_ref, dwg_ref, dwu_ref, dwd_ref):
        hv = h_ref[...]
        dwg_ref[0] = lax.dot_general(dg_ref[0], hv, TN, preferred_element_type=F32)
        dwu_ref[0] = lax.dot_general(du_ref[0], hv, TN, preferred_element_type=F32)
        dwd_ref[0] = lax.dot_general(a_ref[0], dyh_ref[...], TN, preferred_element_type=F32)

    hid = pl.BlockSpec((1, n_tokens, n), lambda j: (j, 0, 0))
    out = pl.BlockSpec((1, n, d), lambda j: (j, 0, 0))
    return pl.pallas_call(
        body, grid=(n_shards,), in_specs=[_resident(h.shape), _resident(dyh.shape), hid, hid, hid],
        out_specs=(out, out, out), out_shape=(jax.ShapeDtypeStruct((n_shards, n, d), F32),) * 3,
        name=name, compiler_params=_params(1),
    )(h, dyh, a, dg, du)


def _ffn(x, gain, w, slots, tag):
    names = [tag + "_w_gate", tag + "_w_up", tag + "_w_down"]

    @jax.custom_vjp
    def op(x, gain, weights, slot):
        return _ffn_fwd(x, gain, *weights, tag + "_fwd")[0]

    def fwd(x, gain, weights, slot):
        out, g, u = _ffn_fwd(x, gain, *weights, tag + "_fwd")
        return out, (x, gain, weights, g, u)

    def bwd(saved, dy):
        x, gain, weights, g, u = saved
        dx, dgain, h, dyh, a, dg, du = _ffn_bwd_rows(x, gain, dy, g, u, *weights, tag + "_bwd_rows")
        return dx, dgain, None, _ffn_bwd_weights(h, dyh, a, dg, du, tag + "_bwd_weights")

    op.defvjp(fwd, bwd)
    return op(x, gain, tuple(w[n] for n in names), tuple(slots[n] for n in names))


IN_PIECES = (("wq_a", 0, 768), ("wk_a", 768, 1536), ("wv_a", 1536, 2304), ("w_qkvb", 2304, 5376),
             ("w_small", 5376, 5392), ("w_ggate", 5392, 6416), ("w_gatea", 6416, 7440), ("w_gateb", 7440, 8464))
IN_NAMES = tuple(name for name, _, _ in IN_PIECES)


def _in_rows(lo, hi):
    return lo, max(hi, lo + LANES)


def _in_proj_fwd(x, gain, wt):
    n_tokens, d = x.shape
    tm = FFN_ROW_TILE
    rows = [_in_rows(lo, hi) for _, lo, hi in IN_PIECES]

    def body(x_ref, gain_ref, wt_ref, *o_refs):
        xv = x_ref[...]
        h = (xv * lax.rsqrt(jnp.mean(xv * xv, axis=-1, keepdims=True) + EPS) * gain_ref[...]).astype(BF16)
        for (lo, hi), o_ref in zip(rows, o_refs):
            o_ref[...] = lax.dot_general(h, wt_ref[lo:hi, :], NT, preferred_element_type=F32)

    return pl.pallas_call(
        body, grid=(n_tokens // tm,),
        in_specs=[pl.BlockSpec((tm, d), lambda i: (i, 0)), _resident(gain.shape), _resident(wt.shape)],
        out_specs=tuple(pl.BlockSpec((tm, hi - lo), lambda i: (i, 0)) for lo, hi in rows),
        out_shape=tuple(jax.ShapeDtypeStruct((n_tokens, hi - lo), F32) for lo, hi in rows),
        name="in_proj_fwd", compiler_params=_params(1),
    )(x, gain, wt)


def _in_proj_bwd_rows(x, gain, dzs, wt):
    n_tokens, d = x.shape
    tm = FFN_ROW_TILE
    n = len(dzs)
    rows = [_in_rows(lo, hi) for _, lo, hi in IN_PIECES]

    def body(x_ref, gain_ref, *refs):
        dz_refs, wt_ref = refs[:n], refs[n]
        dx_ref, dgain_ref, h_ref = refs[n + 1:]
        xv, gain_v = x_ref[...], gain_ref[...]
        r = lax.rsqrt(jnp.mean(xv * xv, axis=-1, keepdims=True) + EPS)
        xhat = xv * r
        h_ref[...] = (xhat * gain_v).astype(BF16)
        dh = jnp.zeros((tm, d), F32)
        for dz_ref, (lo, hi) in zip(dz_refs, rows):
            dh = dh + lax.dot_general(dz_ref[...].astype(BF16), wt_ref[lo:hi, :], NN, preferred_element_type=F32)
        dxhat = dh * gain_v
        dx_ref[...] = r * (dxhat - xhat * jnp.mean(dxhat * xhat, axis=-1, keepdims=True))

        @pl.when(pl.program_id(0) == 0)
        def _():
            dgain_ref[...] = jnp.zeros_like(dgain_ref)

        dgain_ref[...] += jnp.sum(dh * xhat, axis=0, keepdims=True)

    row = pl.BlockSpec((tm, d), lambda i: (i, 0))
    return pl.pallas_call(
        body, grid=(n_tokens // tm,),
        in_specs=([row, _resident(gain.shape)] + [pl.BlockSpec((tm, dz.shape[1]), lambda i: (i, 0)) for dz in dzs]
                  + [_resident(wt.shape)]),
        out_specs=(row, pl.BlockSpec(gain.shape, lambda i: (0, 0)), row),
        out_shape=(jax.ShapeDtypeStruct(x.shape, F32), jax.ShapeDtypeStruct(gain.shape, F32),
                   jax.ShapeDtypeStruct(x.shape, BF16)),
        name="in_proj_bwd_rows", compiler_params=_params(1),
    )(x, gain, *dzs, wt)


def _in_proj_bwd_weight(dwt, h, dz, lo, hi, name):
    n_tokens, d = h.shape
    width = hi - lo
    tn = _tile(width, 512) if width >= LANES else width
    dz_tile = max(tn, LANES)

    def body(dwt_ref, h_ref, dz_ref, o_ref):
        o_ref[...] = lax.dot_general(dz_ref[:, :tn].astype(BF16), h_ref[...], TN, preferred_element_type=F32)

    return pl.pallas_call(
        body, grid=(width // tn,),
        in_specs=[ANY, _resident(h.shape), pl.BlockSpec((n_tokens, dz_tile), lambda j: (0, j))],
        out_specs=pl.BlockSpec((pl.Element(tn), pl.Element(d)), lambda j: (pl.multiple_of(lo + j * tn, 16), 0)),
        out_shape=jax.ShapeDtypeStruct(dwt.shape, F32), input_output_aliases={0: 0}, name=name,
        compiler_params=_params(1),
    )(dwt, h, dz)


def in_proj(x, gain, wt, slot):
    @jax.custom_vjp
    def op(x, gain, wt, slot):
        return _in_proj_fwd(x, gain, wt)

    def fwd(x, gain, wt, slot):
        return op(x, gain, wt, slot), (x, gain, wt)

    def bwd(saved, dzs):
        x, gain, wt = saved
        dx, dgain, h = _in_proj_bwd_rows(x, gain, dzs, wt)
        dwt = lax.empty(wt.shape, F32)
        for (name, lo, hi), dz in zip(IN_PIECES, dzs):
            dwt = _in_proj_bwd_weight(dwt, h, dz, lo, hi, "in_proj_dw_" + name)
        return dx, dgain, None, dwt

    op.defvjp(fwd, bwd)
    return dict(zip(IN_NAMES, op(x, gain, wt, slot)))


def _local_loss(diff, x_target, w):
    x, slots, small = diff
    target = x_target
    n_tokens = x.shape[0]
    x1 = _ffn(x, small["ffn1_norm"], w, slots, "ffn1")

    proj = in_proj(x1, small["mix_norm"], w["w_in_t"], slots["w_in_t"])

    cos, sin = _rope_tables(n_tokens)
    q = _to_heads(rope(proj["wq_a"], cos, sin, "rope_q"))
    k = _to_heads(rope(proj["wk_a"], cos, sin, "rope_k"))
    v = _to_heads(proj["wv_a"])
    o, lse = attention(q, k, v)
    ya = rowwise(_combine_fn, "combine", tuple(_from_heads(o)) + tuple(_from_heads(lse)), tm=512)[0]
    pa = mm(ya, w["w_branch_a"], slots["w_branch_a"], "branch_a")

    qkv = conv_silu(proj["w_qkvb"], small["gdn_conv_w"])
    beta, g = rowwise(_beta_decay_fn, "beta_decay",
                      (proj["w_small"][:, :GDN_HEADS], proj["w_small"][:, GDN_HEADS:2 * GDN_HEADS]),
                      params=(small["gdn_a_log"], small["gdn_dt_bias"]), tm=512)
    ob = gated_delta(qkv, beta, g)
    yb = rowwise(_outnorm_gate_fn, "outnorm_gate", (ob, proj["w_ggate"]), params=(small["gdn_out_norm"],),
                 tm=512, nblk=GDN_HEADS)[0]
    pb = mm(yb, w["w_branch_b"], slots["w_branch_b"], "branch_b")

    merged = rowwise(_merge_fn, "merge", (proj["w_gatea"], proj["w_gateb"], pa, pb))[0]
    x2 = mm_res(merged, w["w_out"], slots["w_out"], x1, 1.0, "out")

    x3 = _ffn(x2, small["ffn2_norm"], w, slots, "ffn2")
    row_loss = rowwise(_loss_fn, "loss", (x3,), consts=(target,), params=(small["final_norm"],))[0]
    return jnp.sum(row_loss)


BIG_WEIGHTS = ("ffn1_w_gate", "ffn1_w_up", "ffn1_w_down", "w_in", "w_branch_a", "w_branch_b", "w_out",
               "ffn2_w_gate", "ffn2_w_up", "ffn2_w_down")
TRANSPOSED = ("ffn1_w_gate", "ffn1_w_up", "w_in", "ffn2_w_gate", "ffn2_w_up")
CONV_SHARD = (GDN_CONV, 3 * GDN_WIDTH // N_DEV)
SMALL_ROWS = 24
ANY = pl.BlockSpec(memory_space=pl.ANY)


def _position():
    return lax.axis_index("x"), lax.axis_index("y"), lax.axis_index("c")


def all_gather_shards(shards, name):
    n = len(shards)

    def body(*refs):
        x_refs, out_refs = refs[:n], refs[n:2 * n]
        send_sems, recv_sems, local_sems = refs[2 * n:]
        x, y, c = _position()
        me, sibling = (x, y, c), (x, y, 1 - c)
        chips = [(1 - x, y), (x, 1 - y), (1 - x, 1 - y)]

        def slab(a, px, py, pc):
            return out_refs[a].at[4 * px + 2 * py + pc]

        def copy(a, k, block, to, src=None):
            return pltpu.make_async_remote_copy(
                src_ref=slab(a, *block) if src is None else src, dst_ref=slab(a, *block),
                send_sem=send_sems.at[7 * a + k], recv_sem=recv_sems.at[7 * a + k], device_id=to, device_id_type=MESH)

        mine = [pltpu.make_async_copy(x_refs[a], slab(a, *me), local_sems.at[a]) for a in range(n)]
        for cp in mine:
            cp.start()
        first = []
        for j, chip in enumerate(chips):
            first += [copy(a, 1 + j, me, (*chip, c), src=x_refs[a]) for a in range(n)]
        first += [copy(a, 0, me, sibling, src=x_refs[a]) for a in range(n)]
        for cp in first:
            cp.start()
        passed = []
        for j, chip in enumerate(chips):
            for a in range(n):
                copy(a, 1 + j, (*chip, c), me).wait_recv()
                cp = copy(a, 4 + j, (*chip, c), sibling)
                cp.start()
                passed.append(cp)
        for a in range(n):
            copy(a, 0, sibling, me).wait_recv()
        for j, chip in enumerate(chips):
            for a in range(n):
                copy(a, 4 + j, (*chip, 1 - c), me).wait_recv()
        for cp in first + passed:
            cp.wait_send()
        for cp in mine:
            cp.wait()

    return pl.pallas_call(
        body, out_shape=tuple(jax.ShapeDtypeStruct((N_DEV,) + s.shape, s.dtype) for s in shards),
        in_specs=[ANY] * n, out_specs=(ANY,) * n,
        scratch_shapes=[pltpu.SemaphoreType.DMA((7 * n,)), pltpu.SemaphoreType.DMA((7 * n,)),
                        pltpu.SemaphoreType.DMA((n,))],
        name=name,
    )(*shards)


def exchange_with_sibling(grads):
    n = len(grads)

    def body(*refs):
        g_refs, recv_refs = refs[:n], refs[n:2 * n]
        send_sems, recv_sems = refs[2 * n:]
        x, y, c = _position()
        copies = [pltpu.make_async_remote_copy(
            src_ref=g_refs[a].at[2 * k + 1 - c], dst_ref=recv_refs[a].at[k], send_sem=send_sems.at[4 * a + k],
            recv_sem=recv_sems.at[4 * a + k], device_id=(x, y, 1 - c), device_id_type=MESH)
            for k in range(4) for a in range(n)]
        for cp in copies:
            cp.start()
        for cp in copies:
            cp.wait()

    return pl.pallas_call(
        body, out_shape=tuple(jax.ShapeDtypeStruct((4,) + g.shape[1:], g.dtype) for g in grads),
        in_specs=[ANY] * n, out_specs=(ANY,) * n,
        scratch_shapes=[pltpu.SemaphoreType.DMA((4 * n,)), pltpu.SemaphoreType.DMA((4 * n,))], name="rs_sibling",
    )(*grads)


ELEMENTWISE_TILE_BYTES = 1536 * 1024


def _tile2(rows, cols):
    if rows % 256 == 0:
        return 256, cols
    if rows * cols * 4 > ELEMENTWISE_TILE_BYTES and cols % 256 == 0:
        return rows, 256
    return rows, cols


def add_sibling(grads, received, core, name):
    _, rows, width = grads.shape
    tr, tc = _tile2(rows, width)

    def body(c_ref, g_ref, r_ref, o_ref):
        o_ref[...] = (g_ref[...] + r_ref[...]).astype(BF16)

    blk = (1, tr, tc)
    return pl.pallas_call(
        body,
        grid_spec=pltpu.PrefetchScalarGridSpec(
            num_scalar_prefetch=1, grid=(4, rows // tr, width // tc),
            in_specs=[pl.BlockSpec(blk, lambda k, i, j, c_ref: (2 * k + c_ref[0], i, j)),
                      pl.BlockSpec(blk, lambda k, i, j, c_ref: (k, i, j))],
            out_specs=pl.BlockSpec(blk, lambda k, i, j, c_ref: (k, i, j))),
        out_shape=jax.ShapeDtypeStruct((4, rows, width), BF16), name=name, compiler_params=_params(3),
    )(core, grads, received)


def exchange_with_chips(partials):
    n = len(partials)

    def body(*refs):
        p_refs, recv_refs = refs[:n], refs[n:2 * n]
        send_sems, recv_sems = refs[2 * n:]
        x, y, c = _position()
        chips = [(1 - x, y), (x, 1 - y), (1 - x, 1 - y)]
        copies = [pltpu.make_async_remote_copy(
            src_ref=p_refs[a].at[2 * cx + cy], dst_ref=recv_refs[a].at[j], send_sem=send_sems.at[3 * a + j],
            recv_sem=recv_sems.at[3 * a + j], device_id=(cx, cy, c), device_id_type=MESH)
            for a in range(n) for j, (cx, cy) in enumerate(chips)]
        for cp in copies:
            cp.start()
        for cp in copies:
            cp.wait()

    return pl.pallas_call(
        body, out_shape=tuple(jax.ShapeDtypeStruct((3,) + p.shape[1:], p.dtype) for p in partials),
        in_specs=[ANY] * n, out_specs=(ANY,) * n,
        scratch_shapes=[pltpu.SemaphoreType.DMA((3 * n,)), pltpu.SemaphoreType.DMA((3 * n,))], name="rs_chips",
    )(*partials)


def all_reduce_small(vals):
    rows, width = vals.shape

    def body(x_ref, out_ref, all_ref, send_sems, recv_sems):
        x, y, c = _position()
        me, sibling = (x, y, c), (x, y, 1 - c)
        chips = [(1 - x, y), (x, 1 - y), (1 - x, 1 - y)]

        def slab(px, py, pc):
            return all_ref.at[4 * px + 2 * py + pc]

        def copy(k, block, to, src=None):
            return pltpu.make_async_remote_copy(
                src_ref=slab(*block) if src is None else src, dst_ref=slab(*block),
                send_sem=send_sems.at[k], recv_sem=recv_sems.at[k], device_id=to, device_id_type=MESH)

        first = [copy(0, me, sibling, src=x_ref)]
        first += [copy(1 + j, me, (*chip, c), src=x_ref) for j, chip in enumerate(chips)]
        for cp in first:
            cp.start()
        all_ref[4 * x + 2 * y + c] = x_ref[...]
        passed = [copy(4 + j, (*chip, c), sibling) for j, chip in enumerate(chips)]
        for j, chip in enumerate(chips):
            copy(1 + j, (*chip, c), me).wait_recv()
            passed[j].start()
        copy(0, sibling, me).wait_recv()
        for j, chip in enumerate(chips):
            copy(4 + j, (*chip, 1 - c), me).wait_recv()
        for cp in first + passed:
            cp.wait_send()
        total = all_ref[0]
        for d in range(1, N_DEV):
            total = total + all_ref[d]
        out_ref[...] = total

    vmem = pl.BlockSpec(memory_space=pltpu.VMEM)
    return pl.pallas_call(
        body, out_shape=(jax.ShapeDtypeStruct(vals.shape, F32), jax.ShapeDtypeStruct((N_DEV, rows, width), F32)),
        in_specs=[vmem], out_specs=(vmem, vmem),
        scratch_shapes=[pltpu.SemaphoreType.DMA((7,)), pltpu.SemaphoreType.DMA((7,))], name="small_allreduce",
    )(vals)[0]


def adamw(w, g, m, v, name):
    shape = w.shape
    w2, g2, m2, v2 = [a.reshape((-1, shape[-1])) for a in (w, g, m, v)]
    rows, cols = w2.shape
    tr = 256 if rows % 256 == 0 else rows

    def body(w_ref, g_ref, m_ref, v_ref, d_ref, nm_ref, nv_ref):
        gv = g_ref[...]
        nm = ADAM_B1 * m_ref[...] + (1.0 - ADAM_B1) * gv
        nv = ADAM_B2 * v_ref[...] + (1.0 - ADAM_B2) * (gv * gv)
        m_hat = nm / (1.0 - ADAM_B1 ** ADAM_STEP)
        v_hat = nv / (1.0 - ADAM_B2 ** ADAM_STEP)
        d_ref[...] = -ADAM_LR * (m_hat / (jnp.sqrt(v_hat) + ADAM_EPS) + ADAM_WD * w_ref[...])
        nm_ref[...] = nm
        nv_ref[...] = nv

    blk = pl.BlockSpec((tr, cols), lambda i: (i, 0))
    out = jax.ShapeDtypeStruct((rows, cols), F32)
    outs = pl.pallas_call(
        body, grid=(rows // tr,), in_specs=[blk] * 4, out_specs=(blk,) * 3, out_shape=(out,) * 3,
        name=name, compiler_params=_params(1),
    )(w2, g2, m2, v2)
    return tuple(o.reshape(shape) for o in outs)


def adamw_summed(w, m, v, grads, from_sibling, received, me, name):
    rows, cols = w.shape[-2:]
    tr, tc = _tile2(rows, cols)

    def body(me_ref, w_ref, m_ref, v_ref, own_ref, sib_ref, r_ref, g_ref, d_ref, nm_ref, nv_ref):
        gv = own_ref[0] + sib_ref[0]
        for j in range(3):
            gv = gv + r_ref[j].astype(F32)
        nm = ADAM_B1 * m_ref[0] + (1.0 - ADAM_B1) * gv
        nv = ADAM_B2 * v_ref[0] + (1.0 - ADAM_B2) * (gv * gv)
        m_hat = nm / (1.0 - ADAM_B1 ** ADAM_STEP)
        v_hat = nv / (1.0 - ADAM_B2 ** ADAM_STEP)
        g_ref[0] = gv
        d_ref[0] = -ADAM_LR * (m_hat / (jnp.sqrt(v_hat) + ADAM_EPS) + ADAM_WD * w_ref[0])
        nm_ref[0] = nm
        nv_ref[0] = nv

    one = pl.BlockSpec((1, tr, tc), lambda i, j, me_ref: (0, i, j))
    out = jax.ShapeDtypeStruct((1, rows, cols), F32)
    return pl.pallas_call(
        body,
        grid_spec=pltpu.PrefetchScalarGridSpec(
            num_scalar_prefetch=1, grid=(rows // tr, cols // tc),
            in_specs=[one, one, one, pl.BlockSpec((1, tr, tc), lambda i, j, me_ref: (me_ref[0], i, j)),
                      pl.BlockSpec((1, tr, tc), lambda i, j, me_ref: (me_ref[1], i, j)),
                      pl.BlockSpec((3, tr, tc), lambda i, j, me_ref: (0, i, j))],
            out_specs=(one,) * 4),
        out_shape=(out,) * 4, name=name, compiler_params=_params(2),
    )(me, w, m, v, grads, from_sibling, received)


SMALL_VECTORS = ("ffn1_norm", "mix_norm", "ffn2_norm", "final_norm")


def _pack_small(gs):
    row = jnp.concatenate([gs["gdn_a_log"].reshape(-1), gs["gdn_dt_bias"].reshape(-1), gs["gdn_out_norm"].reshape(-1)])
    rows = [gs[n].reshape(1, D_MODEL) for n in SMALL_VECTORS]
    rows.append(jnp.pad(row, (0, D_MODEL - row.shape[0])).reshape(1, D_MODEL))
    rows.append(gs["gdn_conv_w"].reshape(-1, D_MODEL))
    packed = jnp.concatenate(rows, axis=0)
    return jnp.pad(packed, ((0, SMALL_ROWS - packed.shape[0]), (0, 0)))


def _unpack_small(packed):
    out = {n: packed[i].reshape(1, D_MODEL) for i, n in enumerate(SMALL_VECTORS)}
    row = packed[len(SMALL_VECTORS)]
    out["gdn_a_log"] = row[:GDN_HEADS].reshape(1, GDN_HEADS)
    out["gdn_dt_bias"] = row[GDN_HEADS:2 * GDN_HEADS].reshape(1, GDN_HEADS)
    out["gdn_out_norm"] = row[2 * GDN_HEADS:2 * GDN_HEADS + GDN_HEAD_DIM].reshape(1, GDN_HEAD_DIM)
    first = len(SMALL_VECTORS) + 1
    out["gdn_conv_w"] = packed[first:first + GDN_CONV * 3].reshape(GDN_CONV, 3 * GDN_WIDTH)
    return out


WEIGHTS = ("ffn1_norm", "ffn1_w_gate", "ffn1_w_up", "ffn1_w_down", "mix_norm", "w_in", "gdn_conv_w", "gdn_a_log",
           "gdn_dt_bias", "gdn_out_norm", "w_branch_a", "w_branch_b", "w_out", "ffn2_norm", "ffn2_w_gate",
           "ffn2_w_up", "ffn2_w_down", "final_norm")


def kernel(x, ffn1_norm, ffn1_w_gate, ffn1_w_up, ffn1_w_down, mix_norm, w_in, gdn_conv_w, gdn_a_log, gdn_dt_bias, gdn_out_norm, w_branch_a, w_branch_b, w_out, ffn2_norm, ffn2_w_gate, ffn2_w_up, ffn2_w_down, final_norm, loss_target, m_ffn1_norm, m_ffn1_w_gate, m_ffn1_w_up, m_ffn1_w_down, m_mix_norm, m_w_in, m_gdn_conv_w, m_gdn_a_log, m_gdn_dt_bias, m_gdn_out_norm, m_w_branch_a, m_w_branch_b, m_w_out, m_ffn2_norm, m_ffn2_w_gate, m_ffn2_w_up, m_ffn2_w_down, m_final_norm, v_ffn1_norm, v_ffn1_w_gate, v_ffn1_w_up, v_ffn1_w_down, v_mix_norm, v_w_in, v_gdn_conv_w, v_gdn_a_log, v_gdn_dt_bias, v_gdn_out_norm, v_w_branch_a, v_w_branch_b, v_w_out, v_ffn2_norm, v_ffn2_w_gate, v_ffn2_w_up, v_ffn2_w_down, v_final_norm):
    given = dict(locals())
    px, py, pc = _position()
    big_names = list(BIG_WEIGHTS)

    def shard_view(a, n):
        return a.transpose(0, 2, 1) if n in TRANSPOSED else a

    shards = [shard_view(given[n], n)[0].astype(BF16) for n in big_names] + [gdn_conv_w[0]]
    gathered = dict(zip(big_names + ["gdn_conv_w"], all_gather_shards(shards, "gather_weights")))
    w = {n: gathered[n] for n in big_names if n.startswith("ffn")}
    w["w_in_t"] = gathered["w_in"].reshape(-1, D_MODEL)
    w["w_branch_a"] = gathered["w_branch_a"].transpose(1, 0, 2).reshape(256, D_MODEL)
    w["w_branch_b"] = gathered["w_branch_b"].reshape(D_MODEL, D_MODEL)
    w["w_out"] = gathered["w_out"].reshape(D_MODEL, D_MODEL)
    conv_full = gathered["gdn_conv_w"].transpose(1, 0, 2).reshape(GDN_CONV, 3 * GDN_WIDTH)
    slots = {n: jnp.zeros(a.shape, F32) for n, a in w.items()}
    small = dict(ffn1_norm=ffn1_norm, mix_norm=mix_norm, ffn2_norm=ffn2_norm, final_norm=final_norm.reshape(1, D_MODEL),
                 gdn_a_log=gdn_a_log, gdn_dt_bias=gdn_dt_bias, gdn_out_norm=gdn_out_norm, gdn_conv_w=conv_full)

    loss_local, (grad_x, g_w, g_small) = jax.value_and_grad(_local_loss)((x[0], slots, small), loss_target[0], w)
    loss = lax.psum(loss_local, ("x", "y", "c"))

    g_big = {n: g_w[n] for n in big_names if n.startswith("ffn")}
    g_big["w_in"] = g_w["w_in_t"].reshape(N_DEV, -1, D_MODEL)
    g_big["w_branch_a"] = g_w["w_branch_a"].reshape(256, N_DEV, 128).transpose(1, 0, 2)
    g_big["w_branch_b"] = g_w["w_branch_b"].reshape(N_DEV, 128, D_MODEL)
    g_big["w_out"] = g_w["w_out"].reshape(N_DEV, 128, D_MODEL)
    g_list = [g_big[n] for n in big_names]
    core = pc.astype(jnp.int32).reshape(1)
    me = 4 * px + 2 * py + pc
    me_and_chip = jnp.stack([me, 2 * px + py]).astype(jnp.int32)
    from_sibling = exchange_with_sibling(g_list)
    partials = [add_sibling(g, r, core, "rs_add_" + n) for n, g, r in zip(big_names, g_list, from_sibling)]
    from_chips = exchange_with_chips(partials)

    results = {}
    for n, g, sib, recv in zip(big_names, g_list, from_sibling, from_chips):
        state = [shard_view(given[p + n], n) for p in ("", "m_", "v_")]
        outs = adamw_summed(*state, g, sib, recv, me_and_chip, "adamw_" + n)
        results[n] = tuple(shard_view(o, n) for o in outs)

    small_sum = _unpack_small(all_reduce_small(_pack_small(g_small)))
    conv_cols = CONV_SHARD[1]
    small_sum["gdn_conv_w"] = lax.dynamic_slice(small_sum["gdn_conv_w"], (0, me * conv_cols), (GDN_CONV, conv_cols))
    for n in WEIGHTS:
        if n not in results:
            g = small_sum[n].reshape(given[n].shape)
            results[n] = (g,) + adamw(given[n], g, given["m_" + n], given["v_" + n], "adamw_" + n)

    outs = [[results[n][i] for n in WEIGHTS] for i in range(4)]
    return (loss, grad_x[None], *outs[0], *outs[1], *outs[2], *outs[3])
```

```python
import jax
import jax.numpy as jnp
from jax import lax
from jax.experimental import pallas as pl
from jax.experimental.pallas import tpu as pltpu

F32 = jnp.float32
BF16 = jnp.bfloat16
HI = lax.Precision.HIGHEST
MESH = pl.DeviceIdType.MESH

N_DEV = 8
D_MODEL = 1024
EPS = 1e-6
ROPE_THETA = 10000.0
DSW_DILATIONS = (1, 4, 16)
DSW_HEADS_PER_GROUP = 4
DSW_HEAD_DIM = 64
DSW_BLOCK = 128
GDN_HEADS = 8
GDN_HEAD_DIM = 128
GDN_WIDTH = 1024
GDN_CONV = 4
GDN_CHUNK = 64

ADAM_LR = 0.001
ADAM_B1 = 0.9
ADAM_B2 = 0.999
ADAM_EPS = 1e-08
ADAM_WD = 0.01
ADAM_STEP = 10

VMEM_LIMIT_BYTES = 56 * 1024 * 1024
LANES = 128

NN = (((1,), (0,)), ((), ()))
NT = (((1,), (1,)), ((), ()))
TN = (((0,), (0,)), ((), ()))


def _params(n_grid):
    return pltpu.CompilerParams(dimension_semantics=("arbitrary",) * n_grid, vmem_limit_bytes=VMEM_LIMIT_BYTES)


def _tile(n, pref):
    best = None
    t = LANES
    while t <= min(n, pref):
        if n % t == 0:
            best = t
        t += LANES
    return n if best is None else best


def _matmul(a, b, *, name, ta=False, tb=False, res=None, scale=1.0):
    K, M = a.shape if ta else a.shape[::-1]
    N = b.shape[0] if tb else b.shape[1]
    assert (b.shape[1] if tb else b.shape[0]) == K, (a.shape, b.shape, ta, tb)
    tm = _tile(M, 512)
    tn = _tile(N, 512)
    dn = (((0 if ta else 1,), (1 if tb else 0,)), ((), ()))

    def body(*refs):
        a_ref, b_ref = refs[:2]
        o_ref = refs[-1]
        acc = lax.dot_general(a_ref[...].astype(BF16), b_ref[...].astype(BF16), dn, preferred_element_type=F32)
        if scale != 1.0:
            acc = acc * scale
        if res is not None:
            acc = refs[2][...] + acc
        o_ref[...] = acc

    a_spec = pl.BlockSpec((K, tm), lambda i, j: (0, i)) if ta else pl.BlockSpec((tm, K), lambda i, j: (i, 0))
    b_spec = pl.BlockSpec((tn, K), lambda i, j: (j, 0)) if tb else pl.BlockSpec((K, tn), lambda i, j: (0, j))
    o_spec = pl.BlockSpec((tm, tn), lambda i, j: (i, j))
    ins, specs = [a, b], [a_spec, b_spec]
    if res is not None:
        ins.append(res)
        specs.append(o_spec)
    return pl.pallas_call(
        body, grid=(M // tm, N // tn), in_specs=specs, out_specs=o_spec,
        out_shape=jax.ShapeDtypeStruct((M, N), F32), name=name, compiler_params=_params(2),
    )(*ins)


def _rw_specs(arrs, tm, nblk):
    return [pl.BlockSpec((tm, a.shape[1] // nblk), lambda i, j: (i, j)) for a in arrs]


def _rowwise_fwd(fn, name, rows, consts, params, tm, nblk):
    n_rows = rows[0].shape[0]
    tm = min(tm, n_rows)
    ins = list(rows) + list(consts)
    avals = [jax.ShapeDtypeStruct((tm, a.shape[1] // nblk), a.dtype) for a in ins]
    avals += [jax.ShapeDtypeStruct(p.shape, p.dtype) for p in params]
    out_avals = jax.eval_shape(fn, *avals)
    n_in = len(ins) + len(params)

    def body(*refs):
        outs = fn(*[r[...] for r in refs[:n_in]])
        for r, o in zip(refs[n_in:], outs):
            r[...] = o.astype(r.dtype)

    return pl.pallas_call(
        body, grid=(n_rows // tm, nblk),
        in_specs=_rw_specs(ins, tm, nblk) + [pl.BlockSpec(p.shape, lambda i, j: (0, 0)) for p in params],
        out_specs=tuple(pl.BlockSpec((tm, o.shape[1]), lambda i, j: (i, j)) for o in out_avals),
        out_shape=tuple(jax.ShapeDtypeStruct((n_rows, o.shape[1] * nblk), o.dtype) for o in out_avals),
        name=name, compiler_params=_params(2),
    )(*ins, *params)


def _rowwise_bwd(fn, name, rows, consts, params, cts, tm, nblk):
    n_rows = rows[0].shape[0]
    tm = min(tm, n_rows)
    nr, nc, npar, nct = len(rows), len(consts), len(params), len(cts)

    def body(*refs):
        rv = [r[...] for r in refs[:nr]]
        cv = [r[...] for r in refs[nr:nr + nc]]
        pv = [r[...] for r in refs[nr + nc:nr + nc + npar]]
        ctv = [r[...] for r in refs[nr + nc + npar:nr + nc + npar + nct]]
        outs = refs[nr + nc + npar + nct:]
        _, vjp = jax.vjp(lambda *d: fn(*d[:nr], *cv, *d[nr:]), *rv, *pv)
        grads = vjp(tuple(ctv))
        for k in range(nr):
            outs[k][...] = grads[k]
        first = jnp.logical_and(pl.program_id(0) == 0, pl.program_id(1) == 0)
        for k in range(npar):
            ref = outs[nr + k]

            @pl.when(first)
            def _(ref=ref):
                ref[...] = jnp.zeros_like(ref)

            ref[...] += grads[nr + k]

    ins = list(rows) + list(consts)
    return pl.pallas_call(
        body, grid=(n_rows // tm, nblk),
        in_specs=(_rw_specs(ins, tm, nblk) + [pl.BlockSpec(p.shape, lambda i, j: (0, 0)) for p in params]
                  + _rw_specs(cts, tm, nblk)),
        out_specs=tuple(_rw_specs(rows, tm, nblk) + [pl.BlockSpec(p.shape, lambda i, j: (0, 0)) for p in params]),
        out_shape=tuple([jax.ShapeDtypeStruct(a.shape, F32) for a in rows]
                        + [jax.ShapeDtypeStruct(p.shape, F32) for p in params]),
        name=name, compiler_params=_params(2),
    )(*ins, *params, *cts)


def _merge_fn(ga, gb, pa, pb):
    return (jax.nn.sigmoid(ga) * pa + jax.nn.sigmoid(gb) * pb,)


def _outnorm_gate_fn(o, gate, gain):
    y = o * lax.rsqrt(jnp.mean(o * o, axis=-1, keepdims=True) + EPS) * gain
    return (y * (gate * jax.nn.sigmoid(gate)),)


def _beta_decay_fn(beta_raw, decay_raw, a_log, dt_bias):
    z = decay_raw + dt_bias
    softplus = jnp.maximum(z, 0.0) + jnp.log(1.0 + jnp.exp(-jnp.abs(z)))
    g = -jnp.exp(a_log) * softplus
    rows = g.shape[0]
    ii = lax.broadcasted_iota(jnp.int32, (rows, rows), 0)
    jj = lax.broadcasted_iota(jnp.int32, (rows, rows), 1)
    same_chunk_before = jnp.logical_and(jj <= ii, jj // GDN_CHUNK == ii // GDN_CHUNK).astype(F32)
    gcum = lax.dot_general(same_chunk_before, g, NN, precision=HI, preferred_element_type=F32)
    return jax.nn.sigmoid(beta_raw), gcum


def _combine_fn(o0, o1, o2, l0, l1, l2):
    m = lax.stop_gradient(jnp.maximum(jnp.maximum(l0, l1), l2))
    e0, e1, e2 = jnp.exp(l0 - m), jnp.exp(l1 - m), jnp.exp(l2 - m)
    return ((e0 * o0 + e1 * o1 + e2 * o2) / (e0 + e1 + e2),)


def _loss_fn(x, target, gain):
    y = x * lax.rsqrt(jnp.mean(x * x, axis=-1, keepdims=True) + EPS) * gain
    err = y - target
    return (0.5 * jnp.mean(err * err, axis=-1, keepdims=True),)


def _rope_call(x, cos, sin, name):
    n_rows, width = x.shape
    tm = 512

    def body(x_ref, c_ref, s_ref, o_ref):
        v = x_ref[...]
        lane = lax.broadcasted_iota(jnp.int32, v.shape, 1)
        low = (lane % DSW_HEAD_DIM) < DSW_HEAD_DIM // 2
        half = DSW_HEAD_DIM // 2
        swapped = jnp.where(low, pltpu.roll(v, LANES - half, 1), pltpu.roll(v, half, 1))
        o_ref[...] = v * c_ref[...] + swapped * s_ref[...]

    tab = pl.BlockSpec((tm, LANES), lambda i, j: (i, 0))
    blk = pl.BlockSpec((tm, LANES), lambda i, j: (i, j))
    return pl.pallas_call(
        body, grid=(n_rows // tm, width // LANES), in_specs=[blk, tab, tab], out_specs=blk,
        out_shape=jax.ShapeDtypeStruct(x.shape, F32), name=name, compiler_params=_params(2),
    )(x, cos, sin)


def _rope_tables(n_tokens):
    half = DSW_HEAD_DIM // 2
    inv_freq = ROPE_THETA ** (-jnp.arange(half, dtype=F32) / half)
    ang = jnp.arange(n_tokens, dtype=F32)[:, None] * inv_freq[None, :]
    cos, sin = jnp.cos(ang), jnp.sin(ang)
    return jnp.tile(jnp.concatenate([cos, cos], 1), (1, 2)), jnp.tile(jnp.concatenate([-sin, sin], 1), (1, 2))


def _attn_probs(q, kp, kc, group, n):
    blk = DSW_BLOCK
    k = _each(lambda a, b: jnp.concatenate([a, b], axis=0).astype(BF16), kp, kc)
    s = _each(lambda a, b: lax.dot_general(a.astype(BF16), b, NT, preferred_element_type=F32)
              * (DSW_HEAD_DIM ** -0.5), q, k)
    blocks_per_seq = jnp.where(group == 0, 16, jnp.where(group == 1, 4, 1))
    first = (n % blocks_per_seq) == 0
    qi = lax.broadcasted_iota(jnp.int32, (blk, 2 * blk), 0)
    kj = lax.broadcasted_iota(jnp.int32, (blk, 2 * blk), 1)
    dist = qi + blk - kj
    valid = (dist >= 0) & (dist <= blk) & jnp.logical_or(kj >= blk, jnp.logical_not(first))
    s = _each(lambda a: jnp.where(valid, a, -1e30), s)
    m = _each(lambda a: jnp.max(a, axis=-1, keepdims=True), s)
    p = _each(lambda a, b: jnp.exp(a - b), s, m)
    l = _each(lambda a: jnp.sum(a, axis=-1, keepdims=True), p)
    return _each(lambda a, b: a / b, p, l), _each(lambda a, b: a + jnp.log(b), m, l), k


def _attn_specs(n_tokens):
    blk, hpg = DSW_BLOCK, DSW_HEADS_PER_GROUP
    cur = pl.BlockSpec((hpg, blk, DSW_HEAD_DIM), lambda g, n: (g, n, 0))
    prev = pl.BlockSpec((hpg, blk, DSW_HEAD_DIM), lambda g, n: (g, jnp.maximum(n - 1, 0), 0))
    return cur, prev


def _attn_fwd(q, k, v):
    nh, n_tokens, hd = q.shape
    hpg = DSW_HEADS_PER_GROUP
    cur, prev = _attn_specs(n_tokens)

    def body(q_ref, kp_ref, kc_ref, vp_ref, vc_ref, o_ref, l_ref):
        heads = range(hpg)
        p, lse, _ = _attn_probs([q_ref[h] for h in heads], [kp_ref[h] for h in heads], [kc_ref[h] for h in heads],
                                pl.program_id(0), pl.program_id(1))
        vv = [jnp.concatenate([vp_ref[h], vc_ref[h]], axis=0).astype(BF16) for h in heads]
        o = _each(lambda a, b: lax.dot_general(a.astype(BF16), b, NN, preferred_element_type=F32), p, vv)
        for h in heads:
            o_ref[h] = o[h]
            l_ref[h] = jnp.broadcast_to(lse[h], (DSW_BLOCK, hd))

    return pl.pallas_call(
        body, grid=(nh // hpg, n_tokens // DSW_BLOCK), in_specs=[cur, prev, cur, prev, cur], out_specs=(cur, cur),
        out_shape=(jax.ShapeDtypeStruct(q.shape, F32), jax.ShapeDtypeStruct(q.shape, F32)),
        name="attn_fwd", compiler_params=_params(2),
    )(q, k, k, v, v)


def _attn_bwd(q, k, v, do, dlse):
    nh, n_tokens, hd = q.shape
    hpg = DSW_HEADS_PER_GROUP
    nblk = n_tokens // DSW_BLOCK
    cur, prev = _attn_specs(n_tokens)
    part = pl.BlockSpec((hpg, 1, 2 * DSW_BLOCK, hd), lambda g, n: (g, n, 0, 0))
    scale = DSW_HEAD_DIM ** -0.5

    def body(q_ref, kp_ref, kc_ref, vp_ref, vc_ref, do_ref, dl_ref, dq_ref, dk_ref, dv_ref):
        heads = range(hpg)
        qs = [q_ref[h] for h in heads]
        p, _, kb = _attn_probs(qs, [kp_ref[h] for h in heads], [kc_ref[h] for h in heads],
                               pl.program_id(0), pl.program_id(1))
        qb = _each(lambda a: a.astype(BF16), qs)
        vv = [jnp.concatenate([vp_ref[h], vc_ref[h]], axis=0).astype(BF16) for h in heads]
        dob = [do_ref[h].astype(BF16) for h in heads]
        dp = _each(lambda a, b: lax.dot_general(a, b, NT, preferred_element_type=F32), dob, vv)
        dv = _each(lambda a, b: lax.dot_general(a.astype(BF16), b, TN, preferred_element_type=F32), p, dob)
        dl = [jnp.sum(dl_ref[h], axis=-1, keepdims=True) for h in heads]
        ds = _each(lambda a, b, c: (a * (b - jnp.sum(b * a, axis=-1, keepdims=True) + c) * scale).astype(BF16),
                   p, dp, dl)
        dq = _each(lambda a, b: lax.dot_general(a, b, NN, preferred_element_type=F32), ds, kb)
        dk = _each(lambda a, b: lax.dot_general(a, b, TN, preferred_element_type=F32), ds, qb)
        for h in heads:
            dq_ref[h] = dq[h]
            dk_ref[h, 0] = dk[h]
            dv_ref[h, 0] = dv[h]

    dq, dkp, dvp = pl.pallas_call(
        body, grid=(nh // hpg, nblk), in_specs=[cur, prev, cur, prev, cur, cur, cur], out_specs=(cur, part, part),
        out_shape=(jax.ShapeDtypeStruct(q.shape, F32),
                   jax.ShapeDtypeStruct((nh, nblk, 2 * DSW_BLOCK, hd), F32),
                   jax.ShapeDtypeStruct((nh, nblk, 2 * DSW_BLOCK, hd), F32)),
        name="attn_bwd", compiler_params=_params(2),
    )(q, k, k, v, v, do, dlse)

    def fold(partial):
        own = partial[:, :, DSW_BLOCK:]
        from_next = jnp.pad(partial[:, 1:, :DSW_BLOCK], ((0, 0), (0, 1), (0, 0), (0, 0)))
        return (own + from_next).reshape(nh, n_tokens, hd)

    return dq, fold(dkp), fold(dvp)


def _to_heads(a):
    n_tokens = a.shape[0]
    outs = []
    for gi, d in enumerate(DSW_DILATIONS):
        blk = a[:, gi * 256:(gi + 1) * 256].reshape(n_tokens // d, d, DSW_HEADS_PER_GROUP, DSW_HEAD_DIM)
        outs.append(blk.transpose(2, 1, 0, 3).reshape(DSW_HEADS_PER_GROUP, n_tokens, DSW_HEAD_DIM))
    return jnp.concatenate(outs, 0)


def _from_heads(a):
    n_tokens = a.shape[1]
    outs = []
    for gi, d in enumerate(DSW_DILATIONS):
        blk = a[gi * 4:(gi + 1) * 4].reshape(DSW_HEADS_PER_GROUP, d, n_tokens // d, DSW_HEAD_DIM)
        outs.append(blk.transpose(2, 1, 0, 3).reshape(n_tokens, DSW_HEADS_PER_GROUP * DSW_HEAD_DIM))
    return outs


CONV_TILE = 512


def _shift_down(x, k, rows):
    return x if k == 0 else jnp.where(rows >= k, pltpu.roll(x, k, 0), 0.0)


def _shift_up(x, k, rows):
    n = x.shape[0]
    return x if k == 0 else jnp.where(rows < n - k, pltpu.roll(x, n - k, 0), 0.0)


def _conv_pre(x, w):
    rows = lax.broadcasted_iota(jnp.int32, x.shape, 0)
    acc = x * w[GDN_CONV - 1:GDN_CONV]
    for k in range(1, GDN_CONV):
        acc = acc + _shift_down(x, k, rows) * w[GDN_CONV - 1 - k:GDN_CONV - k]
    return acc, rows


def _conv_fwd(x, w):
    n_tokens, width = x.shape
    big = pl.BlockSpec((n_tokens, CONV_TILE), lambda j: (0, j))
    wsp = pl.BlockSpec((GDN_CONV, CONV_TILE), lambda j: (0, j))

    def body(x_ref, w_ref, o_ref):
        acc, _ = _conv_pre(x_ref[...], w_ref[...])
        o_ref[...] = acc * jax.nn.sigmoid(acc)

    return pl.pallas_call(
        body, grid=(width // CONV_TILE,), in_specs=[big, wsp], out_specs=big,
        out_shape=jax.ShapeDtypeStruct(x.shape, F32), name="conv_fwd", compiler_params=_params(1),
    )(x, w)


def _conv_bwd(x, w, dy):
    n_tokens, width = x.shape
    big = pl.BlockSpec((n_tokens, CONV_TILE), lambda j: (0, j))
    wsp = pl.BlockSpec((GDN_CONV, CONV_TILE), lambda j: (0, j))

    def body(x_ref, w_ref, dy_ref, dx_ref, dw_ref):
        xv, wv = x_ref[...], w_ref[...]
        acc, rows = _conv_pre(xv, wv)
        sg = jax.nn.sigmoid(acc)
        dacc = dy_ref[...] * (sg + acc * sg * (1.0 - sg))
        dx = dacc * wv[GDN_CONV - 1:GDN_CONV]
        for k in range(1, GDN_CONV):
            dx = dx + _shift_up(dacc, k, rows) * wv[GDN_CONV - 1 - k:GDN_CONV - k]
        dx_ref[...] = dx
        for k in range(GDN_CONV):
            dw_ref[GDN_CONV - 1 - k:GDN_CONV - k, :] = jnp.sum(dacc * _shift_down(xv, k, rows), axis=0, keepdims=True)

    return pl.pallas_call(
        body, grid=(width // CONV_TILE,), in_specs=[big, wsp, big], out_specs=(big, wsp),
        out_shape=(jax.ShapeDtypeStruct(x.shape, F32), jax.ShapeDtypeStruct(w.shape, F32)),
        name="conv_bwd", compiler_params=_params(1),
    )(x, w, dy)


def _dot(a, b, dn=NN):
    return lax.dot_general(a, b, dn, precision=HI, preferred_element_type=F32)


def _dot3(a, b, dn=NN):
    return lax.dot_general(a, b, dn, precision=lax.Precision.HIGH, preferred_element_type=F32)


def _bf16_dot(a, b, dn):
    return lax.dot_general(a.astype(BF16), b.astype(BF16), dn, preferred_element_type=F32)


_DOT_GRADS = {NN: (("g", "b", NT), ("a", "g", TN)), NT: (("g", "b", NN), ("g", "a", TN)),
              TN: (("b", "g", NT), ("a", "g", NN))}


def _make_bdot(dn):
    @jax.custom_vjp
    def op(a, b):
        return _bf16_dot(a, b, dn)

    def fwd(a, b):
        return op(a, b), (a, b)

    def bwd(saved, g):
        vals = dict(a=saved[0], b=saved[1], g=g)
        return tuple(_bf16_dot(vals[x], vals[y], form) for x, y, form in _DOT_GRADS[dn])

    op.defvjp(fwd, bwd)
    return op


_BDOTS = {dn: _make_bdot(dn) for dn in (NN, NT, TN)}


def _bdot(a, b, dn=NN):
    return _BDOTS[dn](a, b)


def _each(fn, *lists):
    return [fn(*items) for items in zip(*lists)]


def _gdn_chunks(q, k, v, b, gcum, state):
    c = GDN_CHUNK
    ii = lax.broadcasted_iota(jnp.int32, (c, c), 0)
    jj = lax.broadcasted_iota(jnp.int32, (c, c), 1)
    eye = (ii == jj).astype(F32)
    qn = _each(lambda x: x * lax.rsqrt(jnp.sum(x * x, axis=-1, keepdims=True) + EPS) * (GDN_HEAD_DIM ** -0.5), q)
    kn = _each(lambda x: x * lax.rsqrt(jnp.sum(x * x, axis=-1, keepdims=True) + EPS), k)
    gcum_i = _each(lambda x: jnp.broadcast_to(x, (c, c)), gcum)
    gcum_j = _each(jnp.transpose, gcum_i)
    decay = _each(lambda x, y: jnp.exp(jnp.where(jj <= ii, x - y, -1e30)), gcum_i, gcum_j)
    g_last = _each(lambda x: x[c - 1:c, :], gcum)
    e_gcum = _each(jnp.exp, gcum)
    kbeta = _each(lambda x, y: x * y, kn, b)
    vbeta = _each(lambda x, y: x * y, v, b)
    m = _each(lambda x, y, d: jnp.where(jj < ii, _bdot(x, y, NT) * d, 0.0), kbeta, kn, decay)
    inv = _each(lambda x: eye - x, m)
    power = _each(lambda x: _dot3(x, x), m)
    for step in range(5):
        inv = _each(lambda x, p: x + _dot3(x, p), inv, power)
        if step < 4:
            power = _each(lambda p: _dot3(p, p), power)
    u = _each(_dot3, inv, vbeta)
    w = _each(lambda x, y, e: _dot3(x, y * e), inv, kbeta, e_gcum)
    a_qk = _each(lambda x, y, d: _bdot(x, y, NT) * d, qn, kn, decay)
    v_new = _each(lambda x, y, s: x - _bdot(y, s), u, w, state)
    o = _each(lambda x, e, s, a, vn: _bdot(x * e, s) + _bdot(a, vn), qn, e_gcum, state, a_qk, v_new)
    new_state = _each(lambda s, gl, x, gc, vn: s * jnp.exp(gl) + _bdot(x * jnp.exp(gl - gc), vn, TN),
                      state, g_last, kn, gcum, v_new)
    return o, new_state


GDN_HEADS_PER_STEP = 8


GDN_TIME_TILE = 256


def _gdn_specs(n_tokens, reverse):
    hb, hd, tt = GDN_HEADS_PER_STEP, GDN_HEAD_DIM, GDN_TIME_TILE
    nb, nt = GDN_HEADS // hb, n_tokens // tt

    def when(t):
        return nt - 1 - t if reverse else t

    q = pl.BlockSpec((tt, hb * hd), lambda h, t: (when(t), h))
    k = pl.BlockSpec((tt, hb * hd), lambda h, t: (when(t), nb + h))
    v = pl.BlockSpec((tt, hb * hd), lambda h, t: (when(t), 2 * nb + h))
    vec = pl.BlockSpec((tt, hb), lambda h, t: (when(t), h))
    states = pl.BlockSpec((hb, tt // GDN_CHUNK, hd, hd), lambda h, t: (h, when(t), 0, 0))
    return q, k, v, vec, states


def _gdn_fwd(qkv, beta, g):
    n_tokens = qkv.shape[0]
    hb, hd, tt = GDN_HEADS_PER_STEP, GDN_HEAD_DIM, GDN_TIME_TILE
    n_chunks = tt // GDN_CHUNK
    q_s, k_s, v_s, vec, st = _gdn_specs(n_tokens, False)

    def body(q_ref, k_ref, v_ref, b_ref, g_ref, o_ref, st_ref, state):
        @pl.when(pl.program_id(1) == 0)
        def _():
            state[...] = jnp.zeros_like(state)

        def step(c, carry):
            r = pl.ds(pl.multiple_of(c * GDN_CHUNK, GDN_CHUNK), GDN_CHUNK)
            cols = [slice(h * hd, (h + 1) * hd) for h in range(hb)]
            old = [state[h] for h in range(hb)]
            o, new = _gdn_chunks(
                [q_ref[r, cs] for cs in cols], [k_ref[r, cs] for cs in cols], [v_ref[r, cs] for cs in cols],
                [b_ref[r, h:h + 1] for h in range(hb)], [g_ref[r, h:h + 1] for h in range(hb)], old)
            for h in range(hb):
                st_ref[h, c] = old[h]
                o_ref[r, cols[h]] = o[h]
                state[h] = new[h]
            return carry

        lax.fori_loop(0, n_chunks, step, 0)

    return pl.pallas_call(
        body, grid=(GDN_HEADS // hb, n_tokens // tt), in_specs=[q_s, k_s, v_s, vec, vec], out_specs=(q_s, st),
        out_shape=(jax.ShapeDtypeStruct((n_tokens, GDN_WIDTH), F32),
                   jax.ShapeDtypeStruct((GDN_HEADS, n_tokens // GDN_CHUNK, hd, hd), F32)),
        scratch_shapes=[pltpu.VMEM((hb, hd, hd), F32)],
        name="gdn_fwd", compiler_params=_params(2),
    )(qkv, qkv, qkv, beta, g)


def _gdn_bwd(qkv, beta, g, states, do):
    n_tokens = qkv.shape[0]
    hb, hd, tt = GDN_HEADS_PER_STEP, GDN_HEAD_DIM, GDN_TIME_TILE
    n_chunks = tt // GDN_CHUNK
    q_s, k_s, v_s, vec, st = _gdn_specs(n_tokens, True)

    def body(q_ref, k_ref, v_ref, b_ref, g_ref, st_ref, do_ref, dq_ref, dk_ref, dv_ref, db_ref, dg_ref, dstate):
        @pl.when(pl.program_id(1) == 0)
        def _():
            dstate[...] = jnp.zeros_like(dstate)

        def step(i, carry):
            c = n_chunks - 1 - i
            r = pl.ds(pl.multiple_of(c * GDN_CHUNK, GDN_CHUNK), GDN_CHUNK)
            cols = [slice(h * hd, (h + 1) * hd) for h in range(hb)]
            args = ([q_ref[r, cs] for cs in cols], [k_ref[r, cs] for cs in cols], [v_ref[r, cs] for cs in cols],
                    [b_ref[r, h:h + 1] for h in range(hb)], [g_ref[r, h:h + 1] for h in range(hb)],
                    [st_ref[h, c] for h in range(hb)])
            cts = ([do_ref[r, cs] for cs in cols], [dstate[h] for h in range(hb)])
            dq, dk, dv, db, dg, dst = jax.vjp(_gdn_chunks, *args)[1](cts)
            for h in range(hb):
                dq_ref[r, cols[h]] = dq[h]
                dk_ref[r, cols[h]] = dk[h]
                dv_ref[r, cols[h]] = dv[h]
                db_ref[r, h:h + 1] = db[h]
                dg_ref[r, h:h + 1] = dg[h]
                dstate[h] = dst[h]
            return carry

        lax.fori_loop(0, n_chunks, step, 0)

    wide = jax.ShapeDtypeStruct((n_tokens, GDN_WIDTH), F32)
    thin = jax.ShapeDtypeStruct(beta.shape, F32)
    dq, dk, dv, db, dg = pl.pallas_call(
        body, grid=(GDN_HEADS // hb, n_tokens // tt), in_specs=[q_s, k_s, v_s, vec, vec, st, q_s],
        out_specs=(q_s, q_s, q_s, vec, vec), out_shape=(wide, wide, wide, thin, thin),
        scratch_shapes=[pltpu.VMEM((hb, hd, hd), F32)],
        name="gdn_bwd", compiler_params=_params(2),
    )(qkv, qkv, qkv, beta, g, states, do)
    return jnp.concatenate([dq, dk, dv], axis=1), db, dg


FFN_ROW_TILE = 256


def _resident(shape):
    return pl.BlockSpec(shape, lambda i: (0,) * len(shape), pipeline_mode=pl.Buffered(1))


def _ffn_fwd(x, gain, wg, wu, wd, name):
    n_tokens, d = x.shape
    n_shards, n, _ = wg.shape
    tm = FFN_ROW_TILE

    def body(x_ref, gain_ref, wg_ref, wu_ref, wd_ref, o_ref, g_ref, u_ref):
        xv = x_ref[...]
        h = (xv * lax.rsqrt(jnp.mean(xv * xv, axis=-1, keepdims=True) + EPS) * gain_ref[...]).astype(BF16)
        acc = jnp.zeros((tm, d), F32)
        for j in range(n_shards):
            g = lax.dot_general(h, wg_ref[j], NT, preferred_element_type=F32)
            u = lax.dot_general(h, wu_ref[j], NT, preferred_element_type=F32)
            g_ref[j] = g
            u_ref[j] = u
            a = (g * jax.nn.sigmoid(g) * u).astype(BF16)
            acc = acc + lax.dot_general(a, wd_ref[j], NN, preferred_element_type=F32)
        o_ref[...] = xv + 0.5 * acc

    row = pl.BlockSpec((tm, d), lambda i: (i, 0))
    hid = pl.BlockSpec((n_shards, tm, n), lambda i: (0, i, 0))
    return pl.pallas_call(
        body, grid=(n_tokens // tm,),
        in_specs=[row, _resident(gain.shape), _resident(wg.shape), _resident(wu.shape), _resident(wd.shape)],
        out_specs=(row, hid, hid),
        out_shape=(jax.ShapeDtypeStruct(x.shape, F32), jax.ShapeDtypeStruct((n_shards, n_tokens, n), F32),
                   jax.ShapeDtypeStruct((n_shards, n_tokens, n), F32)),
        name=name, compiler_params=_params(1),
    )(x, gain, wg, wu, wd)


def _ffn_bwd_rows(x, gain, dy, g, u, wg, wu, wd, name):
    n_tokens, d = x.shape
    n_shards, n, _ = wg.shape
    tm = FFN_ROW_TILE

    def body(x_ref, gain_ref, dy_ref, g_ref, u_ref, wg_ref, wu_ref, wd_ref,
             dx_ref, dgain_ref, h_ref, dyh_ref, a_ref, dg_ref, du_ref):
        xv, dyv, gain_v = x_ref[...], dy_ref[...], gain_ref[...]
        r = lax.rsqrt(jnp.mean(xv * xv, axis=-1, keepdims=True) + EPS)
        xhat = xv * r
        h_ref[...] = (xhat * gain_v).astype(BF16)
        dyh = (0.5 * dyv).astype(BF16)
        dyh_ref[...] = dyh
        dh = jnp.zeros((tm, d), F32)
        for j in range(n_shards):
            da = lax.dot_general(dyh, wd_ref[j], NT, preferred_element_type=F32)
            gv, uv = g_ref[j], u_ref[j]
            sg = jax.nn.sigmoid(gv)
            silu = gv * sg
            a_ref[j] = (silu * uv).astype(BF16)
            dg = (da * uv * (sg + silu * (1.0 - sg))).astype(BF16)
            du = (da * silu).astype(BF16)
            dg_ref[j] = dg
            du_ref[j] = du
            dh = dh + lax.dot_general(dg, wg_ref[j], NN, preferred_element_type=F32)
            dh = dh + lax.dot_general(du, wu_ref[j], NN, preferred_element_type=F32)
        dxhat = dh * gain_v
        dx_ref[...] = dyv + r * (dxhat - xhat * jnp.mean(dxhat * xhat, axis=-1, keepdims=True))

        @pl.when(pl.program_id(0) == 0)
        def _():
            dgain_ref[...] = jnp.zeros_like(dgain_ref)

        dgain_ref[...] += jnp.sum(dh * xhat, axis=0, keepdims=True)

    row = pl.BlockSpec((tm, d), lambda i: (i, 0))
    hid = pl.BlockSpec((n_shards, tm, n), lambda i: (0, i, 0))
    hid_shape = (n_shards, n_tokens, n)
    return pl.pallas_call(
        body, grid=(n_tokens // tm,),
        in_specs=[row, _resident(gain.shape), row, hid, hid, _resident(wg.shape), _resident(wu.shape),
                  _resident(wd.shape)],
        out_specs=(row, pl.BlockSpec(gain.shape, lambda i: (0, 0)), row, row, hid, hid, hid),
        out_shape=(jax.ShapeDtypeStruct(x.shape, F32), jax.ShapeDtypeStruct(gain.shape, F32),
                   jax.ShapeDtypeStruct(x.shape, BF16), jax.ShapeDtypeStruct(x.shape, BF16),
                   jax.ShapeDtypeStruct(hid_shape, BF16), jax.ShapeDtypeStruct(hid_shape, BF16),
                   jax.ShapeDtypeStruct(hid_shape, BF16)),
        name=name, compiler_params=_params(1),
    )(x, gain, dy, g, u, wg, wu, wd)


def _ffn_bwd_weights(h, dyh, a, dg, du, name):
    n_shards, n_tokens, n = a.shape
    d = h.shape[1]

    def body(h_ref, dyh_ref, a_ref, dg_ref, du_ref, dwg_ref, dwu_ref, dwd_ref):
        hv = h_ref[...]
        dwg_ref[0] = lax.dot_general(dg_ref[0], hv, TN, preferred_element_type=F32)
        dwu_ref[0] = lax.dot_general(du_ref[0], hv, TN, preferred_element_type=F32)
        dwd_ref[0] = lax.dot_general(a_ref[0], dyh_ref[...], TN, preferred_element_type=F32)

    hid = pl.BlockSpec((1, n_tokens, n), lambda j: (j, 0, 0))
    out = pl.BlockSpec((1, n, d), lambda j: (j, 0, 0))
    return pl.pallas_call(
        body, grid=(n_shards,), in_specs=[_resident(h.shape), _resident(dyh.shape), hid, hid, hid],
        out_specs=(out, out, out), out_shape=(jax.ShapeDtypeStruct((n_shards, n, d), F32),) * 3,
        name=name, compiler_params=_params(1),
    )(h, dyh, a, dg, du)


IN_PIECES = (("wq_a", 0, 768), ("wk_a", 768, 1536), ("wv_a", 1536, 2304), ("w_qkvb", 2304, 5376),
             ("w_small", 5376, 5392), ("w_ggate", 5392, 6416), ("w_gatea", 6416, 7440), ("w_gateb", 7440, 8464))
IN_NAMES = tuple(name for name, _, _ in IN_PIECES)


def _in_rows(lo, hi):
    return lo, max(hi, lo + LANES)


def _in_proj_fwd(x, gain, wt):
    n_tokens, d = x.shape
    tm = FFN_ROW_TILE
    rows = [_in_rows(lo, hi) for _, lo, hi in IN_PIECES]

    def body(x_ref, gain_ref, wt_ref, *o_refs):
        xv = x_ref[...]
        h = (xv * lax.rsqrt(jnp.mean(xv * xv, axis=-1, keepdims=True) + EPS) * gain_ref[...]).astype(BF16)
        for (lo, hi), o_ref in zip(rows, o_refs):
            o_ref[...] = lax.dot_general(h, wt_ref[lo:hi, :], NT, preferred_element_type=F32)

    return pl.pallas_call(
        body, grid=(n_tokens // tm,),
        in_specs=[pl.BlockSpec((tm, d), lambda i: (i, 0)), _resident(gain.shape), _resident(wt.shape)],
        out_specs=tuple(pl.BlockSpec((tm, hi - lo), lambda i: (i, 0)) for lo, hi in rows),
        out_shape=tuple(jax.ShapeDtypeStruct((n_tokens, hi - lo), F32) for lo, hi in rows),
        name="in_proj_fwd", compiler_params=_params(1),
    )(x, gain, wt)


def _in_proj_bwd_rows(x, gain, dres, dzs, wt):
    n_tokens, d = x.shape
    tm = FFN_ROW_TILE
    n = len(dzs)
    rows = [_in_rows(lo, hi) for _, lo, hi in IN_PIECES]

    def body(x_ref, gain_ref, dres_ref, *refs):
        dz_refs, wt_ref = refs[:n], refs[n]
        dx_ref, dgain_ref, h_ref = refs[n + 1:]
        xv, gain_v = x_ref[...], gain_ref[...]
        r = lax.rsqrt(jnp.mean(xv * xv, axis=-1, keepdims=True) + EPS)
        xhat = xv * r
        h_ref[...] = (xhat * gain_v).astype(BF16)
        dh = jnp.zeros((tm, d), F32)
        for dz_ref, (lo, hi) in zip(dz_refs, rows):
            dh = dh + lax.dot_general(dz_ref[...].astype(BF16), wt_ref[lo:hi, :], NN, preferred_element_type=F32)
        dxhat = dh * gain_v
        dx_ref[...] = dres_ref[...] + r * (dxhat - xhat * jnp.mean(dxhat * xhat, axis=-1, keepdims=True))

        @pl.when(pl.program_id(0) == 0)
        def _():
            dgain_ref[...] = jnp.zeros_like(dgain_ref)

        dgain_ref[...] += jnp.sum(dh * xhat, axis=0, keepdims=True)

    row = pl.BlockSpec((tm, d), lambda i: (i, 0))
    return pl.pallas_call(
        body, grid=(n_tokens // tm,),
        in_specs=([row, _resident(gain.shape), row]
                  + [pl.BlockSpec((tm, dz.shape[1]), lambda i: (i, 0)) for dz in dzs] + [_resident(wt.shape)]),
        out_specs=(row, pl.BlockSpec(gain.shape, lambda i: (0, 0)), row),
        out_shape=(jax.ShapeDtypeStruct(x.shape, F32), jax.ShapeDtypeStruct(gain.shape, F32),
                   jax.ShapeDtypeStruct(x.shape, BF16)),
        name="in_proj_bwd_rows", compiler_params=_params(1),
    )(x, gain, dres, *dzs, wt)


def _in_proj_bwd_weight(dwt, h, dz, lo, hi, name):
    n_tokens, d = h.shape
    width = hi - lo
    tn = _tile(width, 512) if width >= LANES else width
    dz_tile = max(tn, LANES)

    def body(dwt_ref, h_ref, dz_ref, o_ref):
        o_ref[...] = lax.dot_general(dz_ref[:, :tn].astype(BF16), h_ref[...], TN, preferred_element_type=F32)

    return pl.pallas_call(
        body, grid=(width // tn,),
        in_specs=[ANY, _resident(h.shape), pl.BlockSpec((n_tokens, dz_tile), lambda j: (0, j))],
        out_specs=pl.BlockSpec((pl.Element(tn), pl.Element(d)), lambda j: (pl.multiple_of(lo + j * tn, 16), 0)),
        out_shape=jax.ShapeDtypeStruct(dwt.shape, F32), input_output_aliases={0: 0}, name=name,
        compiler_params=_params(1),
    )(dwt, h, dz)


def _split_small(z):
    return z[:, :GDN_HEADS], z[:, GDN_HEADS:2 * GDN_HEADS]


def _heads3(q, k, v):
    return _to_heads(q), _to_heads(k), _to_heads(v)


def _tokens6(o, lse):
    return tuple(_from_heads(o)) + tuple(_from_heads(lse))


def mixer_forward(x1, w, small):
    n_tokens = x1.shape[0]
    proj = dict(zip(IN_NAMES, _in_proj_fwd(x1, small["mix_norm"], w["w_in_t"])))
    cos, sin = _rope_tables(n_tokens)
    q_rot = _rope_call(proj["wq_a"], cos, sin, "rope_q")
    k_rot = _rope_call(proj["wk_a"], cos, sin, "rope_k")
    (qh, kh, vh), heads_vjp = jax.vjp(_heads3, q_rot, k_rot, proj["wv_a"])
    o, lse = _attn_fwd(qh, kh, vh)
    per_group, tokens_vjp = jax.vjp(_tokens6, o, lse)
    ya = _rowwise_fwd(_combine_fn, "combine", per_group, (), (), 512, 1)[0]
    pa = _matmul(ya, w["w_branch_a"], name="branch_a")
    qkv = _conv_fwd(proj["w_qkvb"], small["gdn_conv_w"])
    raw, small_vjp = jax.vjp(_split_small, proj["w_small"])
    gdn_params = (small["gdn_a_log"], small["gdn_dt_bias"])
    beta, gcum = _rowwise_fwd(_beta_decay_fn, "beta_decay", raw, (), gdn_params, 512, 1)
    ob, states = _gdn_fwd(qkv, beta, gcum)
    gate_in = (ob, proj["w_ggate"])
    yb = _rowwise_fwd(_outnorm_gate_fn, "outnorm_gate", gate_in, (), (small["gdn_out_norm"],), 512, GDN_HEADS)[0]
    pb = _matmul(yb, w["w_branch_b"], name="branch_b")
    merge_in = (proj["w_gatea"], proj["w_gateb"], pa, pb)
    merged = _rowwise_fwd(_merge_fn, "merge", merge_in, (), (), 256, 1)[0]
    x2 = _matmul(merged, w["w_out"], name="out", res=x1)
    saved = dict(x1=x1, proj=proj, cos=cos, sin=sin, heads_vjp=heads_vjp, heads=(qh, kh, vh), tokens_vjp=tokens_vjp,
                 per_group=per_group, ya=ya, qkv=qkv, raw=raw, small_vjp=small_vjp, beta=beta, gcum=gcum, states=states,
                 gate_in=gate_in, yb=yb, merge_in=merge_in, merged=merged)
    return x2, saved


def mixer_backward(dx2, s, w, small):
    proj = s["proj"]
    dmerged = _matmul(dx2, w["w_out"], name="out_da", tb=True)
    grads = dict(w_out=_matmul(s["merged"], dx2, name="out_dw", ta=True))
    dgate_a, dgate_b, dpa, dpb = _rowwise_bwd(_merge_fn, "merge_bwd", s["merge_in"], (), (), (dmerged,), 256, 1)
    dyb = _matmul(dpb, w["w_branch_b"], name="branch_b_da", tb=True)
    grads["w_branch_b"] = _matmul(s["yb"], dpb, name="branch_b_dw", ta=True)
    dya = _matmul(dpa, w["w_branch_a"], name="branch_a_da", tb=True)
    grads["w_branch_a"] = _matmul(s["ya"], dpa, name="branch_a_dw", ta=True)
    dob, dggate, grads["gdn_out_norm"] = _rowwise_bwd(
        _outnorm_gate_fn, "outnorm_gate_bwd", s["gate_in"], (), (small["gdn_out_norm"],), (dyb,), 512, GDN_HEADS)
    dqkv, dbeta, dgcum = _gdn_bwd(s["qkv"], s["beta"], s["gcum"], s["states"], dob)
    gdn_params = (small["gdn_a_log"], small["gdn_dt_bias"])
    dbeta_raw, ddecay_raw, grads["gdn_a_log"], grads["gdn_dt_bias"] = _rowwise_bwd(
        _beta_decay_fn, "beta_decay_bwd", s["raw"], (), gdn_params, (dbeta, dgcum), 512, 1)
    dsmall = s["small_vjp"]((dbeta_raw, ddecay_raw))[0]
    dqkvb, grads["gdn_conv_w"] = _conv_bwd(proj["w_qkvb"], small["gdn_conv_w"], dqkv)
    dper_group = _rowwise_bwd(_combine_fn, "combine_bwd", s["per_group"], (), (), (dya,), 512, 1)
    do, dlse = s["tokens_vjp"](tuple(dper_group))
    dqh, dkh, dvh = _attn_bwd(*s["heads"], do, dlse)
    dq_rot, dk_rot, dv = s["heads_vjp"]((dqh, dkh, dvh))
    dq = _rope_call(dq_rot, s["cos"], -s["sin"], "rope_q_bwd")
    dk = _rope_call(dk_rot, s["cos"], -s["sin"], "rope_k_bwd")
    dzs = (dq, dk, dv, dqkvb, dsmall, dggate, dgate_a, dgate_b)
    dx1, grads["mix_norm"], h = _in_proj_bwd_rows(s["x1"], small["mix_norm"], dx2, dzs, w["w_in_t"])
    dwt = lax.empty(w["w_in_t"].shape, F32)
    for (name, lo, hi), dz in zip(IN_PIECES, dzs):
        dwt = _in_proj_bwd_weight(dwt, h, dz, lo, hi, "in_proj_dw_" + name)
    grads["w_in_t"] = dwt
    return dx1, grads


def ffn_forward(x, gain, w, tag):
    out, g, u = _ffn_fwd(x, gain, w[tag + "_w_gate"], w[tag + "_w_up"], w[tag + "_w_down"], tag + "_fwd")
    return out, (x, g, u)


def ffn_backward(dy, saved, gain, w, tag):
    x, g, u = saved
    weights = (w[tag + "_w_gate"], w[tag + "_w_up"], w[tag + "_w_down"])
    dx, dgain, h, dyh, a, dg, du = _ffn_bwd_rows(x, gain, dy, g, u, *weights, tag + "_bwd_rows")
    return dx, dgain, _ffn_bwd_weights(h, dyh, a, dg, du, tag + "_bwd_weights")


def loss_head(x3, target, gain):
    row_loss = _rowwise_fwd(_loss_fn, "loss", (x3,), (target,), (gain,), 256, 1)[0]
    dx3, dgain = _rowwise_bwd(_loss_fn, "loss_bwd", (x3,), (target,), (gain,), (jnp.ones_like(row_loss),), 256, 1)
    return jnp.sum(row_loss), dx3, dgain


BIG_WEIGHTS = ("ffn1_w_gate", "ffn1_w_up", "ffn1_w_down", "w_in", "w_branch_a", "w_branch_b", "w_out",
               "ffn2_w_gate", "ffn2_w_up", "ffn2_w_down")
TRANSPOSED = ("ffn1_w_gate", "ffn1_w_up", "w_in", "ffn2_w_gate", "ffn2_w_up")
CONV_SHARD = (GDN_CONV, 3 * GDN_WIDTH // N_DEV)
SMALL_ROWS = 24
ANY = pl.BlockSpec(memory_space=pl.ANY)


def _position():
    return lax.axis_index("x"), lax.axis_index("y"), lax.axis_index("c")


def all_gather_shards(shards, name):
    n = len(shards)

    def body(*refs):
        x_refs, out_refs = refs[:n], refs[n:2 * n]
        send_sems, recv_sems, local_sems = refs[2 * n:]
        x, y, c = _position()
        me, sibling = (x, y, c), (x, y, 1 - c)
        chips = [(1 - x, y), (x, 1 - y), (1 - x, 1 - y)]

        def slab(a, px, py, pc):
            return out_refs[a].at[4 * px + 2 * py + pc]

        def copy(a, k, block, to, src=None):
            return pltpu.make_async_remote_copy(
                src_ref=slab(a, *block) if src is None else src, dst_ref=slab(a, *block),
                send_sem=send_sems.at[7 * a + k], recv_sem=recv_sems.at[7 * a + k], device_id=to, device_id_type=MESH)

        mine = [pltpu.make_async_copy(x_refs[a], slab(a, *me), local_sems.at[a]) for a in range(n)]
        for cp in mine:
            cp.start()
        first = []
        for j, chip in enumerate(chips):
            first += [copy(a, 1 + j, me, (*chip, c), src=x_refs[a]) for a in range(n)]
        first += [copy(a, 0, me, sibling, src=x_refs[a]) for a in range(n)]
        for cp in first:
            cp.start()
        passed = []
        for j, chip in enumerate(chips):
            for a in range(n):
                copy(a, 1 + j, (*chip, c), me).wait_recv()
                cp = copy(a, 4 + j, (*chip, c), sibling)
                cp.start()
                passed.append(cp)
        for a in range(n):
            copy(a, 0, sibling, me).wait_recv()
        for j, chip in enumerate(chips):
            for a in range(n):
                copy(a, 4 + j, (*chip, 1 - c), me).wait_recv()
        for cp in first + passed:
            cp.wait_send()
        for cp in mine:
            cp.wait()

    return pl.pallas_call(
        body, out_shape=tuple(jax.ShapeDtypeStruct((N_DEV,) + s.shape, s.dtype) for s in shards),
        in_specs=[ANY] * n, out_specs=(ANY,) * n,
        scratch_shapes=[pltpu.SemaphoreType.DMA((7 * n,)), pltpu.SemaphoreType.DMA((7 * n,)),
                        pltpu.SemaphoreType.DMA((n,))],
        name=name,
    )(*shards)


def exchange_with_sibling(grads):
    n = len(grads)

    def body(*refs):
        g_refs, recv_refs = refs[:n], refs[n:2 * n]
        send_sems, recv_sems = refs[2 * n:]
        x, y, c = _position()
        copies = [pltpu.make_async_remote_copy(
            src_ref=g_refs[a].at[2 * k + 1 - c], dst_ref=recv_refs[a].at[k], send_sem=send_sems.at[4 * a + k],
            recv_sem=recv_sems.at[4 * a + k], device_id=(x, y, 1 - c), device_id_type=MESH)
            for k in range(4) for a in range(n)]
        for cp in copies:
            cp.start()
        for cp in copies:
            cp.wait()

    return pl.pallas_call(
        body, out_shape=tuple(jax.ShapeDtypeStruct((4,) + g.shape[1:], g.dtype) for g in grads),
        in_specs=[ANY] * n, out_specs=(ANY,) * n,
        scratch_shapes=[pltpu.SemaphoreType.DMA((4 * n,)), pltpu.SemaphoreType.DMA((4 * n,))], name="rs_sibling",
    )(*grads)


ELEMENTWISE_TILE_BYTES = 1536 * 1024


def _tile2(rows, cols):
    if rows % 256 == 0:
        return 256, cols
    if rows * cols * 4 > ELEMENTWISE_TILE_BYTES and cols % 256 == 0:
        return rows, 256
    return rows, cols


def add_sibling(grads, received, core, name):
    _, rows, width = grads.shape
    tr, tc = _tile2(rows, width)

    def body(c_ref, g_ref, r_ref, o_ref):
        o_ref[...] = (g_ref[...] + r_ref[...]).astype(BF16)

    blk = (1, tr, tc)
    return pl.pallas_call(
        body,
        grid_spec=pltpu.PrefetchScalarGridSpec(
            num_scalar_prefetch=1, grid=(4, rows // tr, width // tc),
            in_specs=[pl.BlockSpec(blk, lambda k, i, j, c_ref: (2 * k + c_ref[0], i, j)),
                      pl.BlockSpec(blk, lambda k, i, j, c_ref: (k, i, j))],
            out_specs=pl.BlockSpec(blk, lambda k, i, j, c_ref: (k, i, j))),
        out_shape=jax.ShapeDtypeStruct((4, rows, width), BF16), name=name, compiler_params=_params(3),
    )(core, grads, received)


def exchange_with_chips(partials):
    n = len(partials)

    def body(*refs):
        p_refs, recv_refs = refs[:n], refs[n:2 * n]
        send_sems, recv_sems = refs[2 * n:]
        x, y, c = _position()
        chips = [(1 - x, y), (x, 1 - y), (1 - x, 1 - y)]
        copies = [pltpu.make_async_remote_copy(
            src_ref=p_refs[a].at[2 * cx + cy], dst_ref=recv_refs[a].at[j], send_sem=send_sems.at[3 * a + j],
            recv_sem=recv_sems.at[3 * a + j], device_id=(cx, cy, c), device_id_type=MESH)
            for a in range(n) for j, (cx, cy) in enumerate(chips)]
        for cp in copies:
            cp.start()
        for cp in copies:
            cp.wait()

    return pl.pallas_call(
        body, out_shape=tuple(jax.ShapeDtypeStruct((3,) + p.shape[1:], p.dtype) for p in partials),
        in_specs=[ANY] * n, out_specs=(ANY,) * n,
        scratch_shapes=[pltpu.SemaphoreType.DMA((3 * n,)), pltpu.SemaphoreType.DMA((3 * n,))], name="rs_chips",
    )(*partials)


HBM = pl.BlockSpec(memory_space=pltpu.HBM)
SEM = pl.BlockSpec(memory_space=pltpu.SEMAPHORE)
DATAFLOW_EFFECT = pltpu.SideEffectType.DATAFLOW_SIDE_EFFECTING
N_PEERS = N_DEV - 1


def _peer(mask):
    x, y, c = _position()
    px = 1 - x if mask & 4 else x
    py = 1 - y if mask & 2 else y
    pc = 1 - c if mask & 1 else c
    return (px, py, pc), 4 * px + 2 * py + pc


def _direct_copies(src_refs, land_refs, send_sems, recv_sems, scatter):
    x, y, c = _position()
    me = 4 * x + 2 * y + c
    copies = []
    for a, (src, land) in enumerate(zip(src_refs, land_refs)):
        for mask in range(1, N_DEV):
            peer, peer_index = _peer(mask)
            k = N_PEERS * a + mask - 1
            copies.append(pltpu.make_async_remote_copy(
                src_ref=src.at[peer_index] if scatter else src,
                dst_ref=land.at[mask - 1] if scatter else land.at[me],
                send_sem=send_sems.at[k], recv_sem=recv_sems.at[k], device_id=peer, device_id_type=MESH))
    return copies


def direct_exchange_start(arrays, scatter, name):
    n = len(arrays)
    slabs = N_PEERS if scatter else N_DEV
    lands = [lax.empty((slabs,) + (a.shape[1:] if scatter else a.shape), a.dtype) for a in arrays]

    def body(*refs):
        src_refs, land_refs = refs[:n], refs[n:2 * n]
        send_sems, recv_sems = refs[2 * n], refs[2 * n + 1]
        token = refs[-1]
        for cp in _direct_copies(src_refs, land_refs, send_sems, recv_sems, scatter):
            cp.start()
        token[...] = jnp.zeros_like(token)

    sems = pltpu.SemaphoreType.DMA((N_PEERS * n,))
    outs = pl.pallas_call(
        body, name=name,
        out_shape=(sems, sems) + tuple(pltpu.HBM(a.shape, a.dtype) for a in arrays)
        + tuple(pltpu.HBM(l.shape, l.dtype) for l in lands) + (jax.ShapeDtypeStruct((8, LANES), F32),),
        in_specs=[HBM] * (2 * n), out_specs=(SEM, SEM) + (HBM,) * (2 * n) + (pl.BlockSpec(memory_space=pltpu.VMEM),),
        input_output_aliases={i: 2 + i for i in range(2 * n)},
        compiler_params=pltpu.CompilerParams(has_side_effects=DATAFLOW_EFFECT),
    )(*[pltpu.with_memory_space_constraint(a, pltpu.HBM) for a in list(arrays) + lands])
    return outs[0], outs[1], outs[2:2 + n], outs[2 + n:2 + 2 * n], outs[-1]


def direct_exchange_wait(send_sems, recv_sems, arrays, lands, after, scatter, name):
    n = len(arrays)

    def body(*refs):
        src_refs, land_refs = refs[:n], refs[n:2 * n]
        send_sems, recv_sems = refs[2 * n], refs[2 * n + 1]
        for cp in _direct_copies(src_refs, land_refs, send_sems, recv_sems, scatter):
            cp.wait_send()
            cp.wait_recv()

    outs = pl.pallas_call(
        body, name=name,
        out_shape=tuple(pltpu.HBM(a.shape, a.dtype) for a in arrays) + tuple(pltpu.HBM(l.shape, l.dtype) for l in lands),
        in_specs=[HBM] * (2 * n) + [SEM, SEM, pl.BlockSpec(memory_space=pl.ANY)], out_specs=(HBM,) * (2 * n),
        input_output_aliases={i: i for i in range(2 * n)},
        compiler_params=pltpu.CompilerParams(has_side_effects=DATAFLOW_EFFECT),
    )(*arrays, *lands, send_sems, recv_sems, after)
    return outs[n:]


def adamw_direct(w, m, v, grads, received, me, name):
    rows, cols = w.shape[-2:]
    tr, tc = _tile2(rows, cols)

    def body(me_ref, w_ref, m_ref, v_ref, own_ref, r_ref, g_ref, d_ref, nm_ref, nv_ref):
        gv = own_ref[0]
        for j in range(N_PEERS):
            gv = gv + r_ref[j].astype(F32)
        nm = ADAM_B1 * m_ref[0] + (1.0 - ADAM_B1) * gv
        nv = ADAM_B2 * v_ref[0] + (1.0 - ADAM_B2) * (gv * gv)
        m_hat = nm / (1.0 - ADAM_B1 ** ADAM_STEP)
        v_hat = nv / (1.0 - ADAM_B2 ** ADAM_STEP)
        g_ref[0] = gv
        d_ref[0] = -ADAM_LR * (m_hat / (jnp.sqrt(v_hat) + ADAM_EPS) + ADAM_WD * w_ref[0])
        nm_ref[0] = nm
        nv_ref[0] = nv

    one = pl.BlockSpec((1, tr, tc), lambda i, j, me_ref: (0, i, j))
    out = jax.ShapeDtypeStruct((1, rows, cols), F32)
    return pl.pallas_call(
        body,
        grid_spec=pltpu.PrefetchScalarGridSpec(
            num_scalar_prefetch=1, grid=(rows // tr, cols // tc),
            in_specs=[one, one, one, pl.BlockSpec((1, tr, tc), lambda i, j, me_ref: (me_ref[0], i, j)),
                      pl.BlockSpec((N_PEERS, tr, tc), lambda i, j, me_ref: (0, i, j))],
            out_specs=(one,) * 4),
        out_shape=(out,) * 4, name=name, compiler_params=_params(2),
    )(me, w, m, v, grads, received)


def all_reduce_small(vals):
    rows, width = vals.shape

    def body(x_ref, out_ref, all_ref, send_sems, recv_sems):
        x, y, c = _position()
        me, sibling = (x, y, c), (x, y, 1 - c)
        chips = [(1 - x, y), (x, 1 - y), (1 - x, 1 - y)]

        def slab(px, py, pc):
            return all_ref.at[4 * px + 2 * py + pc]

        def copy(k, block, to, src=None):
            return pltpu.make_async_remote_copy(
                src_ref=slab(*block) if src is None else src, dst_ref=slab(*block),
                send_sem=send_sems.at[k], recv_sem=recv_sems.at[k], device_id=to, device_id_type=MESH)

        first = [copy(0, me, sibling, src=x_ref)]
        first += [copy(1 + j, me, (*chip, c), src=x_ref) for j, chip in enumerate(chips)]
        for cp in first:
            cp.start()
        all_ref[4 * x + 2 * y + c] = x_ref[...]
        passed = [copy(4 + j, (*chip, c), sibling) for j, chip in enumerate(chips)]
        for j, chip in enumerate(chips):
            copy(1 + j, (*chip, c), me).wait_recv()
            passed[j].start()
        copy(0, sibling, me).wait_recv()
        for j, chip in enumerate(chips):
            copy(4 + j, (*chip, 1 - c), me).wait_recv()
        for cp in first + passed:
            cp.wait_send()
        total = all_ref[0]
        for d in range(1, N_DEV):
            total = total + all_ref[d]
        out_ref[...] = total

    vmem = pl.BlockSpec(memory_space=pltpu.VMEM)
    return pl.pallas_call(
        body, out_shape=(jax.ShapeDtypeStruct(vals.shape, F32), jax.ShapeDtypeStruct((N_DEV, rows, width), F32)),
        in_specs=[vmem], out_specs=(vmem, vmem),
        scratch_shapes=[pltpu.SemaphoreType.DMA((7,)), pltpu.SemaphoreType.DMA((7,))], name="small_allreduce",
    )(vals)[0]


def adamw(w, g, m, v, name):
    shape = w.shape
    w2, g2, m2, v2 = [a.reshape((-1, shape[-1])) for a in (w, g, m, v)]
    rows, cols = w2.shape
    tr = 256 if rows % 256 == 0 else rows

    def body(w_ref, g_ref, m_ref, v_ref, d_ref, nm_ref, nv_ref):
        gv = g_ref[...]
        nm = ADAM_B1 * m_ref[...] + (1.0 - ADAM_B1) * gv
        nv = ADAM_B2 * v_ref[...] + (1.0 - ADAM_B2) * (gv * gv)
        m_hat = nm / (1.0 - ADAM_B1 ** ADAM_STEP)
        v_hat = nv / (1.0 - ADAM_B2 ** ADAM_STEP)
        d_ref[...] = -ADAM_LR * (m_hat / (jnp.sqrt(v_hat) + ADAM_EPS) + ADAM_WD * w_ref[...])
        nm_ref[...] = nm
        nv_ref[...] = nv

    blk = pl.BlockSpec((tr, cols), lambda i: (i, 0))
    out = jax.ShapeDtypeStruct((rows, cols), F32)
    outs = pl.pallas_call(
        body, grid=(rows // tr,), in_specs=[blk] * 4, out_specs=(blk,) * 3, out_shape=(out,) * 3,
        name=name, compiler_params=_params(1),
    )(w2, g2, m2, v2)
    return tuple(o.reshape(shape) for o in outs)


def adamw_summed(w, m, v, grads, from_sibling, received, me, name):
    rows, cols = w.shape[-2:]
    tr, tc = _tile2(rows, cols)

    def body(me_ref, w_ref, m_ref, v_ref, own_ref, sib_ref, r_ref, g_ref, d_ref, nm_ref, nv_ref):
        gv = own_ref[0] + sib_ref[0]
        for j in range(3):
            gv = gv + r_ref[j].astype(F32)
        nm = ADAM_B1 * m_ref[0] + (1.0 - ADAM_B1) * gv
        nv = ADAM_B2 * v_ref[0] + (1.0 - ADAM_B2) * (gv * gv)
        m_hat = nm / (1.0 - ADAM_B1 ** ADAM_STEP)
        v_hat = nv / (1.0 - ADAM_B2 ** ADAM_STEP)
        g_ref[0] = gv
        d_ref[0] = -ADAM_LR * (m_hat / (jnp.sqrt(v_hat) + ADAM_EPS) + ADAM_WD * w_ref[0])
        nm_ref[0] = nm
        nv_ref[0] = nv

    one = pl.BlockSpec((1, tr, tc), lambda i, j, me_ref: (0, i, j))
    out = jax.ShapeDtypeStruct((1, rows, cols), F32)
    return pl.pallas_call(
        body,
        grid_spec=pltpu.PrefetchScalarGridSpec(
            num_scalar_prefetch=1, grid=(rows // tr, cols // tc),
            in_specs=[one, one, one, pl.BlockSpec((1, tr, tc), lambda i, j, me_ref: (me_ref[0], i, j)),
                      pl.BlockSpec((1, tr, tc), lambda i, j, me_ref: (me_ref[1], i, j)),
                      pl.BlockSpec((3, tr, tc), lambda i, j, me_ref: (0, i, j))],
            out_specs=(one,) * 4),
        out_shape=(out,) * 4, name=name, compiler_params=_params(2),
    )(me, w, m, v, grads, from_sibling, received)


SMALL_VECTORS = ("ffn1_norm", "mix_norm", "ffn2_norm", "final_norm")


def _pack_small(gs):
    row = jnp.concatenate([gs["gdn_a_log"].reshape(-1), gs["gdn_dt_bias"].reshape(-1), gs["gdn_out_norm"].reshape(-1)])
    rows = [gs[n].reshape(1, D_MODEL) for n in SMALL_VECTORS]
    rows.append(jnp.pad(row, (0, D_MODEL - row.shape[0])).reshape(1, D_MODEL))
    rows.append(gs["gdn_conv_w"].reshape(-1, D_MODEL))
    packed = jnp.concatenate(rows, axis=0)
    return jnp.pad(packed, ((0, SMALL_ROWS - packed.shape[0]), (0, 0)))


def _unpack_small(packed):
    out = {n: packed[i].reshape(1, D_MODEL) for i, n in enumerate(SMALL_VECTORS)}
    row = packed[len(SMALL_VECTORS)]
    out["gdn_a_log"] = row[:GDN_HEADS].reshape(1, GDN_HEADS)
    out["gdn_dt_bias"] = row[GDN_HEADS:2 * GDN_HEADS].reshape(1, GDN_HEADS)
    out["gdn_out_norm"] = row[2 * GDN_HEADS:2 * GDN_HEADS + GDN_HEAD_DIM].reshape(1, GDN_HEAD_DIM)
    first = len(SMALL_VECTORS) + 1
    out["gdn_conv_w"] = packed[first:first + GDN_CONV * 3].reshape(GDN_CONV, 3 * GDN_WIDTH)
    return out


WEIGHTS = ("ffn1_norm", "ffn1_w_gate", "ffn1_w_up", "ffn1_w_down", "mix_norm", "w_in", "gdn_conv_w", "gdn_a_log",
           "gdn_dt_bias", "gdn_out_norm", "w_branch_a", "w_branch_b", "w_out", "ffn2_norm", "ffn2_w_gate",
           "ffn2_w_up", "ffn2_w_down", "final_norm")


def kernel(x, ffn1_norm, ffn1_w_gate, ffn1_w_up, ffn1_w_down, mix_norm, w_in, gdn_conv_w, gdn_a_log, gdn_dt_bias, gdn_out_norm, w_branch_a, w_branch_b, w_out, ffn2_norm, ffn2_w_gate, ffn2_w_up, ffn2_w_down, final_norm, loss_target, m_ffn1_norm, m_ffn1_w_gate, m_ffn1_w_up, m_ffn1_w_down, m_mix_norm, m_w_in, m_gdn_conv_w, m_gdn_a_log, m_gdn_dt_bias, m_gdn_out_norm, m_w_branch_a, m_w_branch_b, m_w_out, m_ffn2_norm, m_ffn2_w_gate, m_ffn2_w_up, m_ffn2_w_down, m_final_norm, v_ffn1_norm, v_ffn1_w_gate, v_ffn1_w_up, v_ffn1_w_down, v_mix_norm, v_w_in, v_gdn_conv_w, v_gdn_a_log, v_gdn_dt_bias, v_gdn_out_norm, v_w_branch_a, v_w_branch_b, v_w_out, v_ffn2_norm, v_ffn2_w_gate, v_ffn2_w_up, v_ffn2_w_down, v_final_norm):
    given = dict(locals())
    px, py, pc = _position()
    big_names = list(BIG_WEIGHTS)

    def shard_view(a, n):
        return a.transpose(0, 2, 1) if n in TRANSPOSED else a

    shards = [shard_view(given[n], n)[0].astype(BF16) for n in big_names] + [gdn_conv_w[0]]
    gathered = dict(zip(big_names + ["gdn_conv_w"], all_gather_shards(shards, "gather_weights")))
    w = {n: gathered[n] for n in big_names if n.startswith("ffn")}
    w["w_in_t"] = gathered["w_in"].reshape(-1, D_MODEL)
    w["w_branch_a"] = gathered["w_branch_a"].transpose(1, 0, 2).reshape(256, D_MODEL)
    w["w_branch_b"] = gathered["w_branch_b"].reshape(D_MODEL, D_MODEL)
    w["w_out"] = gathered["w_out"].reshape(D_MODEL, D_MODEL)
    conv_full = gathered["gdn_conv_w"].transpose(1, 0, 2).reshape(GDN_CONV, 3 * GDN_WIDTH)
    small = dict(mix_norm=mix_norm, gdn_a_log=gdn_a_log, gdn_dt_bias=gdn_dt_bias, gdn_out_norm=gdn_out_norm,
                 gdn_conv_w=conv_full)

    x1, ffn1_saved = ffn_forward(x[0], ffn1_norm, w, "ffn1")
    x2, mixer_saved = mixer_forward(x1, w, small)
    x3, ffn2_saved = ffn_forward(x2, ffn2_norm, w, "ffn2")
    loss_local, dx3, g_final = loss_head(x3, loss_target[0], final_norm.reshape(1, D_MODEL))
    loss = lax.psum(loss_local, ("x", "y", "c"))
    dx2, g_ffn2_norm, dw2 = ffn_backward(dx3, ffn2_saved, ffn2_norm, w, "ffn2")
    dx1, g_w = mixer_backward(dx2, mixer_saved, w, small)
    grad_x, g_ffn1_norm, dw1 = ffn_backward(dx1, ffn1_saved, ffn1_norm, w, "ffn1")
    g_small = dict(ffn1_norm=g_ffn1_norm, ffn2_norm=g_ffn2_norm, final_norm=g_final,
                   **{n: g_w[n] for n in ("mix_norm", "gdn_a_log", "gdn_dt_bias", "gdn_out_norm", "gdn_conv_w")})

    g_big = dict(zip(("ffn1_w_gate", "ffn1_w_up", "ffn1_w_down"), dw1))
    g_big.update(zip(("ffn2_w_gate", "ffn2_w_up", "ffn2_w_down"), dw2))
    g_big["w_in"] = g_w["w_in_t"].reshape(N_DEV, -1, D_MODEL)
    g_big["w_branch_a"] = g_w["w_branch_a"].reshape(256, N_DEV, 128).transpose(1, 0, 2)
    g_big["w_branch_b"] = g_w["w_branch_b"].reshape(N_DEV, 128, D_MODEL)
    g_big["w_out"] = g_w["w_out"].reshape(N_DEV, 128, D_MODEL)
    g_list = [g_big[n] for n in big_names]
    core = pc.astype(jnp.int32).reshape(1)
    me = 4 * px + 2 * py + pc
    me_and_chip = jnp.stack([me, 2 * px + py]).astype(jnp.int32)
    from_sibling = exchange_with_sibling(g_list)
    partials = [add_sibling(g, r, core, "rs_add_" + n) for n, g, r in zip(big_names, g_list, from_sibling)]
    from_chips = exchange_with_chips(partials)

    results = {}
    for n, g, sib, recv in zip(big_names, g_list, from_sibling, from_chips):
        state = [shard_view(given[p + n], n) for p in ("", "m_", "v_")]
        outs = adamw_summed(*state, g, sib, recv, me_and_chip, "adamw_" + n)
        results[n] = tuple(shard_view(o, n) for o in outs)

    small_sum = _unpack_small(all_reduce_small(_pack_small(g_small)))
    conv_cols = CONV_SHARD[1]
    small_sum["gdn_conv_w"] = lax.dynamic_slice(small_sum["gdn_conv_w"], (0, me * conv_cols), (GDN_CONV, conv_cols))
    for n in WEIGHTS:
        if n not in results:
            g = small_sum[n].reshape(given[n].shape)
            results[n] = (g,) + adamw(given[n], g, given["m_" + n], given["v_" + n], "adamw_" + n)

    outs = [[results[n][i] for n in WEIGHTS] for i in range(4)]
    return (loss, grad_x[None], *outs[0], *outs[1], *outs[2], *outs[3])
```

```python
import jax
import jax.numpy as jnp
from jax import lax
from jax.experimental import pallas as pl
from jax.experimental.pallas import tpu as pltpu

F32 = jnp.float32
BF16 = jnp.bfloat16
HI = lax.Precision.HIGHEST
MESH = pl.DeviceIdType.MESH

N_DEV = 8
D_MODEL = 1024
EPS = 1e-6
ROPE_THETA = 10000.0
DSW_DILATIONS = (1, 4, 16)
DSW_HEADS_PER_GROUP = 4
DSW_HEAD_DIM = 64
DSW_BLOCK = 128
GDN_HEADS = 8
GDN_HEAD_DIM = 128
GDN_WIDTH = 1024
GDN_CONV = 4
GDN_CHUNK = 64

ADAM_LR = 0.001
ADAM_B1 = 0.9
ADAM_B2 = 0.999
ADAM_EPS = 1e-08
ADAM_WD = 0.01
ADAM_STEP = 10

VMEM_LIMIT_BYTES = 56 * 1024 * 1024
LANES = 128

NN = (((1,), (0,)), ((), ()))
NT = (((1,), (1,)), ((), ()))
TN = (((0,), (0,)), ((), ()))


def _params(n_grid):
    return pltpu.CompilerParams(dimension_semantics=("arbitrary",) * n_grid, vmem_limit_bytes=VMEM_LIMIT_BYTES)


def _tile(n, pref):
    best = None
    t = LANES
    while t <= min(n, pref):
        if n % t == 0:
            best = t
        t += LANES
    return n if best is None else best


def _matmul(a, b, *, name, ta=False, tb=False, res=None, scale=1.0):
    K, M = a.shape if ta else a.shape[::-1]
    N = b.shape[0] if tb else b.shape[1]
    assert (b.shape[1] if tb else b.shape[0]) == K, (a.shape, b.shape, ta, tb)
    tm = _tile(M, 512)
    tn = _tile(N, 512)
    dn = (((0 if ta else 1,), (1 if tb else 0,)), ((), ()))

    def body(*refs):
        a_ref, b_ref = refs[:2]
        o_ref = refs[-1]
        acc = lax.dot_general(a_ref[...].astype(BF16), b_ref[...].astype(BF16), dn, preferred_element_type=F32)
        if scale != 1.0:
            acc = acc * scale
        if res is not None:
            acc = refs[2][...] + acc
        o_ref[...] = acc

    a_spec = pl.BlockSpec((K, tm), lambda i, j: (0, i)) if ta else pl.BlockSpec((tm, K), lambda i, j: (i, 0))
    b_spec = pl.BlockSpec((tn, K), lambda i, j: (j, 0)) if tb else pl.BlockSpec((K, tn), lambda i, j: (0, j))
    o_spec = pl.BlockSpec((tm, tn), lambda i, j: (i, j))
    ins, specs = [a, b], [a_spec, b_spec]
    if res is not None:
        ins.append(res)
        specs.append(o_spec)
    return pl.pallas_call(
        body, grid=(M // tm, N // tn), in_specs=specs, out_specs=o_spec,
        out_shape=jax.ShapeDtypeStruct((M, N), F32), name=name, compiler_params=_params(2),
    )(*ins)


def _rw_specs(arrs, tm, nblk):
    return [pl.BlockSpec((tm, a.shape[1] // nblk), lambda i, j: (i, j)) for a in arrs]


def _rowwise_fwd(fn, name, rows, consts, params, tm, nblk):
    n_rows = rows[0].shape[0]
    tm = min(tm, n_rows)
    ins = list(rows) + list(consts)
    avals = [jax.ShapeDtypeStruct((tm, a.shape[1] // nblk), a.dtype) for a in ins]
    avals += [jax.ShapeDtypeStruct(p.shape, p.dtype) for p in params]
    out_avals = jax.eval_shape(fn, *avals)
    n_in = len(ins) + len(params)

    def body(*refs):
        outs = fn(*[r[...] for r in refs[:n_in]])
        for r, o in zip(refs[n_in:], outs):
            r[...] = o.astype(r.dtype)

    return pl.pallas_call(
        body, grid=(n_rows // tm, nblk),
        in_specs=_rw_specs(ins, tm, nblk) + [pl.BlockSpec(p.shape, lambda i, j: (0, 0)) for p in params],
        out_specs=tuple(pl.BlockSpec((tm, o.shape[1]), lambda i, j: (i, j)) for o in out_avals),
        out_shape=tuple(jax.ShapeDtypeStruct((n_rows, o.shape[1] * nblk), o.dtype) for o in out_avals),
        name=name, compiler_params=_params(2),
    )(*ins, *params)


def _rowwise_bwd(fn, name, rows, consts, params, cts, tm, nblk):
    n_rows = rows[0].shape[0]
    tm = min(tm, n_rows)
    nr, nc, npar, nct = len(rows), len(consts), len(params), len(cts)

    def body(*refs):
        rv = [r[...] for r in refs[:nr]]
        cv = [r[...] for r in refs[nr:nr + nc]]
        pv = [r[...] for r in refs[nr + nc:nr + nc + npar]]
        ctv = [r[...] for r in refs[nr + nc + npar:nr + nc + npar + nct]]
        outs = refs[nr + nc + npar + nct:]
        _, vjp = jax.vjp(lambda *d: fn(*d[:nr], *cv, *d[nr:]), *rv, *pv)
        grads = vjp(tuple(ctv))
        for k in range(nr):
            outs[k][...] = grads[k]
        first = jnp.logical_and(pl.program_id(0) == 0, pl.program_id(1) == 0)
        for k in range(npar):
            ref = outs[nr + k]

            @pl.when(first)
            def _(ref=ref):
                ref[...] = jnp.zeros_like(ref)

            ref[...] += grads[nr + k]

    ins = list(rows) + list(consts)
    return pl.pallas_call(
        body, grid=(n_rows // tm, nblk),
        in_specs=(_rw_specs(ins, tm, nblk) + [pl.BlockSpec(p.shape, lambda i, j: (0, 0)) for p in params]
                  + _rw_specs(cts, tm, nblk)),
        out_specs=tuple(_rw_specs(rows, tm, nblk) + [pl.BlockSpec(p.shape, lambda i, j: (0, 0)) for p in params]),
        out_shape=tuple([jax.ShapeDtypeStruct(a.shape, F32) for a in rows]
                        + [jax.ShapeDtypeStruct(p.shape, F32) for p in params]),
        name=name, compiler_params=_params(2),
    )(*ins, *params, *cts)


def _merge_fn(ga, gb, pa, pb):
    return (jax.nn.sigmoid(ga) * pa + jax.nn.sigmoid(gb) * pb,)


def _outnorm_gate_fn(o, gate, gain):
    y = o * lax.rsqrt(jnp.mean(o * o, axis=-1, keepdims=True) + EPS) * gain
    return (y * (gate * jax.nn.sigmoid(gate)),)


def _beta_decay_fn(beta_raw, decay_raw, a_log, dt_bias):
    z = decay_raw + dt_bias
    softplus = jnp.maximum(z, 0.0) + jnp.log(1.0 + jnp.exp(-jnp.abs(z)))
    g = -jnp.exp(a_log) * softplus
    rows = g.shape[0]
    ii = lax.broadcasted_iota(jnp.int32, (rows, rows), 0)
    jj = lax.broadcasted_iota(jnp.int32, (rows, rows), 1)
    same_chunk_before = jnp.logical_and(jj <= ii, jj // GDN_CHUNK == ii // GDN_CHUNK).astype(F32)
    gcum = lax.dot_general(same_chunk_before, g, NN, precision=HI, preferred_element_type=F32)
    return jax.nn.sigmoid(beta_raw), gcum


def _combine_fn(o0, o1, o2, l0, l1, l2):
    m = lax.stop_gradient(jnp.maximum(jnp.maximum(l0, l1), l2))
    e0, e1, e2 = jnp.exp(l0 - m), jnp.exp(l1 - m), jnp.exp(l2 - m)
    return ((e0 * o0 + e1 * o1 + e2 * o2) / (e0 + e1 + e2),)


def _loss_fn(x, target, gain):
    y = x * lax.rsqrt(jnp.mean(x * x, axis=-1, keepdims=True) + EPS) * gain
    err = y - target
    return (0.5 * jnp.mean(err * err, axis=-1, keepdims=True),)


def _rope_call(x, cos, sin, name):
    n_rows, width = x.shape
    tm = 512

    def body(x_ref, c_ref, s_ref, o_ref):
        v = x_ref[...]
        lane = lax.broadcasted_iota(jnp.int32, v.shape, 1)
        low = (lane % DSW_HEAD_DIM) < DSW_HEAD_DIM // 2
        half = DSW_HEAD_DIM // 2
        swapped = jnp.where(low, pltpu.roll(v, LANES - half, 1), pltpu.roll(v, half, 1))
        o_ref[...] = v * c_ref[...] + swapped * s_ref[...]

    tab = pl.BlockSpec((tm, LANES), lambda i, j: (i, 0))
    blk = pl.BlockSpec((tm, LANES), lambda i, j: (i, j))
    return pl.pallas_call(
        body, grid=(n_rows // tm, width // LANES), in_specs=[blk, tab, tab], out_specs=blk,
        out_shape=jax.ShapeDtypeStruct(x.shape, F32), name=name, compiler_params=_params(2),
    )(x, cos, sin)


def _rope_tables(n_tokens):
    half = DSW_HEAD_DIM // 2
    inv_freq = ROPE_THETA ** (-jnp.arange(half, dtype=F32) / half)
    ang = jnp.arange(n_tokens, dtype=F32)[:, None] * inv_freq[None, :]
    cos, sin = jnp.cos(ang), jnp.sin(ang)
    return jnp.tile(jnp.concatenate([cos, cos], 1), (1, 2)), jnp.tile(jnp.concatenate([-sin, sin], 1), (1, 2))


def _attn_probs(q, kp, kc, group, n):
    blk = DSW_BLOCK
    k = _each(lambda a, b: jnp.concatenate([a, b], axis=0).astype(BF16), kp, kc)
    s = _each(lambda a, b: lax.dot_general(a.astype(BF16), b, NT, preferred_element_type=F32)
              * (DSW_HEAD_DIM ** -0.5), q, k)
    blocks_per_seq = jnp.where(group == 0, 16, jnp.where(group == 1, 4, 1))
    first = (n % blocks_per_seq) == 0
    qi = lax.broadcasted_iota(jnp.int32, (blk, 2 * blk), 0)
    kj = lax.broadcasted_iota(jnp.int32, (blk, 2 * blk), 1)
    dist = qi + blk - kj
    valid = (dist >= 0) & (dist <= blk) & jnp.logical_or(kj >= blk, jnp.logical_not(first))
    s = _each(lambda a: jnp.where(valid, a, -1e30), s)
    m = _each(lambda a: jnp.max(a, axis=-1, keepdims=True), s)
    p = _each(lambda a, b: jnp.exp(a - b), s, m)
    l = _each(lambda a: jnp.sum(a, axis=-1, keepdims=True), p)
    return _each(lambda a, b: a / b, p, l), _each(lambda a, b: a + jnp.log(b), m, l), k


def _attn_specs(n_tokens):
    blk, hpg = DSW_BLOCK, DSW_HEADS_PER_GROUP
    cur = pl.BlockSpec((hpg, blk, DSW_HEAD_DIM), lambda g, n: (g, n, 0))
    prev = pl.BlockSpec((hpg, blk, DSW_HEAD_DIM), lambda g, n: (g, jnp.maximum(n - 1, 0), 0))
    return cur, prev


def _attn_fwd(q, k, v):
    nh, n_tokens, hd = q.shape
    hpg = DSW_HEADS_PER_GROUP
    cur, prev = _attn_specs(n_tokens)

    def body(q_ref, kp_ref, kc_ref, vp_ref, vc_ref, o_ref, l_ref):
        heads = range(hpg)
        p, lse, _ = _attn_probs([q_ref[h] for h in heads], [kp_ref[h] for h in heads], [kc_ref[h] for h in heads],
                                pl.program_id(0), pl.program_id(1))
        vv = [jnp.concatenate([vp_ref[h], vc_ref[h]], axis=0).astype(BF16) for h in heads]
        o = _each(lambda a, b: lax.dot_general(a.astype(BF16), b, NN, preferred_element_type=F32), p, vv)
        for h in heads:
            o_ref[h] = o[h]
            l_ref[h] = jnp.broadcast_to(lse[h], (DSW_BLOCK, hd))

    return pl.pallas_call(
        body, grid=(nh // hpg, n_tokens // DSW_BLOCK), in_specs=[cur, prev, cur, prev, cur], out_specs=(cur, cur),
        out_shape=(jax.ShapeDtypeStruct(q.shape, F32), jax.ShapeDtypeStruct(q.shape, F32)),
        name="attn_fwd", compiler_params=_params(2),
    )(q, k, k, v, v)


def _attn_bwd(q, k, v, do, dlse):
    nh, n_tokens, hd = q.shape
    hpg = DSW_HEADS_PER_GROUP
    nblk = n_tokens // DSW_BLOCK
    cur, prev = _attn_specs(n_tokens)
    part = pl.BlockSpec((hpg, 1, 2 * DSW_BLOCK, hd), lambda g, n: (g, n, 0, 0))
    scale = DSW_HEAD_DIM ** -0.5

    def body(q_ref, kp_ref, kc_ref, vp_ref, vc_ref, do_ref, dl_ref, dq_ref, dk_ref, dv_ref):
        heads = range(hpg)
        qs = [q_ref[h] for h in heads]
        p, _, kb = _attn_probs(qs, [kp_ref[h] for h in heads], [kc_ref[h] for h in heads],
                               pl.program_id(0), pl.program_id(1))
        qb = _each(lambda a: a.astype(BF16), qs)
        vv = [jnp.concatenate([vp_ref[h], vc_ref[h]], axis=0).astype(BF16) for h in heads]
        dob = [do_ref[h].astype(BF16) for h in heads]
        dp = _each(lambda a, b: lax.dot_general(a, b, NT, preferred_element_type=F32), dob, vv)
        dv = _each(lambda a, b: lax.dot_general(a.astype(BF16), b, TN, preferred_element_type=F32), p, dob)
        dl = [jnp.sum(dl_ref[h], axis=-1, keepdims=True) for h in heads]
        ds = _each(lambda a, b, c: (a * (b - jnp.sum(b * a, axis=-1, keepdims=True) + c) * scale).astype(BF16),
                   p, dp, dl)
        dq = _each(lambda a, b: lax.dot_general(a, b, NN, preferred_element_type=F32), ds, kb)
        dk = _each(lambda a, b: lax.dot_general(a, b, TN, preferred_element_type=F32), ds, qb)
        for h in heads:
            dq_ref[h] = dq[h]
            dk_ref[h, 0] = dk[h]
            dv_ref[h, 0] = dv[h]

    dq, dkp, dvp = pl.pallas_call(
        body, grid=(nh // hpg, nblk), in_specs=[cur, prev, cur, prev, cur, cur, cur], out_specs=(cur, part, part),
        out_shape=(jax.ShapeDtypeStruct(q.shape, F32),
                   jax.ShapeDtypeStruct((nh, nblk, 2 * DSW_BLOCK, hd), F32),
                   jax.ShapeDtypeStruct((nh, nblk, 2 * DSW_BLOCK, hd), F32)),
        name="attn_bwd", compiler_params=_params(2),
    )(q, k, k, v, v, do, dlse)

    def fold(partial):
        own = partial[:, :, DSW_BLOCK:]
        from_next = jnp.pad(partial[:, 1:, :DSW_BLOCK], ((0, 0), (0, 1), (0, 0), (0, 0)))
        return (own + from_next).reshape(nh, n_tokens, hd)

    return dq, fold(dkp), fold(dvp)


def _to_heads(a):
    n_tokens = a.shape[0]
    outs = []
    for gi, d in enumerate(DSW_DILATIONS):
        blk = a[:, gi * 256:(gi + 1) * 256].reshape(n_tokens // d, d, DSW_HEADS_PER_GROUP, DSW_HEAD_DIM)
        outs.append(blk.transpose(2, 1, 0, 3).reshape(DSW_HEADS_PER_GROUP, n_tokens, DSW_HEAD_DIM))
    return jnp.concatenate(outs, 0)


def _from_heads(a):
    n_tokens = a.shape[1]
    outs = []
    for gi, d in enumerate(DSW_DILATIONS):
        blk = a[gi * 4:(gi + 1) * 4].reshape(DSW_HEADS_PER_GROUP, d, n_tokens // d, DSW_HEAD_DIM)
        outs.append(blk.transpose(2, 1, 0, 3).reshape(n_tokens, DSW_HEADS_PER_GROUP * DSW_HEAD_DIM))
    return outs


CONV_TILE = 512


def _shift_down(x, k, rows):
    return x if k == 0 else jnp.where(rows >= k, pltpu.roll(x, k, 0), 0.0)


def _shift_up(x, k, rows):
    n = x.shape[0]
    return x if k == 0 else jnp.where(rows < n - k, pltpu.roll(x, n - k, 0), 0.0)


def _conv_pre(x, w):
    rows = lax.broadcasted_iota(jnp.int32, x.shape, 0)
    acc = x * w[GDN_CONV - 1:GDN_CONV]
    for k in range(1, GDN_CONV):
        acc = acc + _shift_down(x, k, rows) * w[GDN_CONV - 1 - k:GDN_CONV - k]
    return acc, rows


def _conv_fwd(x, w):
    n_tokens, width = x.shape
    big = pl.BlockSpec((n_tokens, CONV_TILE), lambda j: (0, j))
    wsp = pl.BlockSpec((GDN_CONV, CONV_TILE), lambda j: (0, j))

    def body(x_ref, w_ref, o_ref):
        acc, _ = _conv_pre(x_ref[...], w_ref[...])
        o_ref[...] = acc * jax.nn.sigmoid(acc)

    return pl.pallas_call(
        body, grid=(width // CONV_TILE,), in_specs=[big, wsp], out_specs=big,
        out_shape=jax.ShapeDtypeStruct(x.shape, F32), name="conv_fwd", compiler_params=_params(1),
    )(x, w)


def _conv_bwd(x, w, dy):
    n_tokens, width = x.shape
    big = pl.BlockSpec((n_tokens, CONV_TILE), lambda j: (0, j))
    wsp = pl.BlockSpec((GDN_CONV, CONV_TILE), lambda j: (0, j))

    def body(x_ref, w_ref, dy_ref, dx_ref, dw_ref):
        xv, wv = x_ref[...], w_ref[...]
        acc, rows = _conv_pre(xv, wv)
        sg = jax.nn.sigmoid(acc)
        dacc = dy_ref[...] * (sg + acc * sg * (1.0 - sg))
        dx = dacc * wv[GDN_CONV - 1:GDN_CONV]
        for k in range(1, GDN_CONV):
            dx = dx + _shift_up(dacc, k, rows) * wv[GDN_CONV - 1 - k:GDN_CONV - k]
        dx_ref[...] = dx
        for k in range(GDN_CONV):
            dw_ref[GDN_CONV - 1 - k:GDN_CONV - k, :] = jnp.sum(dacc * _shift_down(xv, k, rows), axis=0, keepdims=True)

    return pl.pallas_call(
        body, grid=(width // CONV_TILE,), in_specs=[big, wsp, big], out_specs=(big, wsp),
        out_shape=(jax.ShapeDtypeStruct(x.shape, F32), jax.ShapeDtypeStruct(w.shape, F32)),
        name="conv_bwd", compiler_params=_params(1),
    )(x, w, dy)


def _dot(a, b, dn=NN):
    return lax.dot_general(a, b, dn, precision=HI, preferred_element_type=F32)


def _dot3(a, b, dn=NN):
    return lax.dot_general(a, b, dn, precision=lax.Precision.HIGH, preferred_element_type=F32)


def _bf16_dot(a, b, dn):
    return lax.dot_general(a.astype(BF16), b.astype(BF16), dn, preferred_element_type=F32)


_DOT_GRADS = {NN: (("g", "b", NT), ("a", "g", TN)), NT: (("g", "b", NN), ("g", "a", TN)),
              TN: (("b", "g", NT), ("a", "g", NN))}


def _make_bdot(dn):
    @jax.custom_vjp
    def op(a, b):
        return _bf16_dot(a, b, dn)

    def fwd(a, b):
        return op(a, b), (a, b)

    def bwd(saved, g):
        vals = dict(a=saved[0], b=saved[1], g=g)
        return tuple(_bf16_dot(vals[x], vals[y], form) for x, y, form in _DOT_GRADS[dn])

    op.defvjp(fwd, bwd)
    return op


_BDOTS = {dn: _make_bdot(dn) for dn in (NN, NT, TN)}


def _bdot(a, b, dn=NN):
    return _BDOTS[dn](a, b)


def _each(fn, *lists):
    return [fn(*items) for items in zip(*lists)]


def _gdn_chunks(q, k, v, b, gcum, state):
    c = GDN_CHUNK
    ii = lax.broadcasted_iota(jnp.int32, (c, c), 0)
    jj = lax.broadcasted_iota(jnp.int32, (c, c), 1)
    eye = (ii == jj).astype(F32)
    qn = _each(lambda x: x * lax.rsqrt(jnp.sum(x * x, axis=-1, keepdims=True) + EPS) * (GDN_HEAD_DIM ** -0.5), q)
    kn = _each(lambda x: x * lax.rsqrt(jnp.sum(x * x, axis=-1, keepdims=True) + EPS), k)
    gcum_i = _each(lambda x: jnp.broadcast_to(x, (c, c)), gcum)
    gcum_j = _each(jnp.transpose, gcum_i)
    decay = _each(lambda x, y: jnp.exp(jnp.where(jj <= ii, x - y, -1e30)), gcum_i, gcum_j)
    g_last = _each(lambda x: x[c - 1:c, :], gcum)
    e_gcum = _each(jnp.exp, gcum)
    kbeta = _each(lambda x, y: x * y, kn, b)
    vbeta = _each(lambda x, y: x * y, v, b)
    m = _each(lambda x, y, d: jnp.where(jj < ii, _bdot(x, y, NT) * d, 0.0), kbeta, kn, decay)
    inv = _each(lambda x: eye - x, m)
    power = _each(lambda x: _dot3(x, x), m)
    for step in range(5):
        inv = _each(lambda x, p: x + _dot3(x, p), inv, power)
        if step < 4:
            power = _each(lambda p: _dot3(p, p), power)
    u = _each(_dot3, inv, vbeta)
    w = _each(lambda x, y, e: _dot3(x, y * e), inv, kbeta, e_gcum)
    a_qk = _each(lambda x, y, d: _bdot(x, y, NT) * d, qn, kn, decay)
    v_new = _each(lambda x, y, s: x - _bdot(y, s), u, w, state)
    o = _each(lambda x, e, s, a, vn: _bdot(x * e, s) + _bdot(a, vn), qn, e_gcum, state, a_qk, v_new)
    new_state = _each(lambda s, gl, x, gc, vn: s * jnp.exp(gl) + _bdot(x * jnp.exp(gl - gc), vn, TN),
                      state, g_last, kn, gcum, v_new)
    return o, new_state


GDN_HEADS_PER_STEP = 8


GDN_TIME_TILE = 256


def _gdn_specs(n_tokens, reverse):
    hb, hd, tt = GDN_HEADS_PER_STEP, GDN_HEAD_DIM, GDN_TIME_TILE
    nb, nt = GDN_HEADS // hb, n_tokens // tt

    def when(t):
        return nt - 1 - t if reverse else t

    q = pl.BlockSpec((tt, hb * hd), lambda h, t: (when(t), h))
    k = pl.BlockSpec((tt, hb * hd), lambda h, t: (when(t), nb + h))
    v = pl.BlockSpec((tt, hb * hd), lambda h, t: (when(t), 2 * nb + h))
    vec = pl.BlockSpec((tt, hb), lambda h, t: (when(t), h))
    states = pl.BlockSpec((hb, tt // GDN_CHUNK, hd, hd), lambda h, t: (h, when(t), 0, 0))
    return q, k, v, vec, states


def _gdn_fwd(qkv, beta, g):
    n_tokens = qkv.shape[0]
    hb, hd, tt = GDN_HEADS_PER_STEP, GDN_HEAD_DIM, GDN_TIME_TILE
    n_chunks = tt // GDN_CHUNK
    q_s, k_s, v_s, vec, st = _gdn_specs(n_tokens, False)

    def body(q_ref, k_ref, v_ref, b_ref, g_ref, o_ref, st_ref, state):
        @pl.when(pl.program_id(1) == 0)
        def _():
            state[...] = jnp.zeros_like(state)

        def step(c, carry):
            r = pl.ds(pl.multiple_of(c * GDN_CHUNK, GDN_CHUNK), GDN_CHUNK)
            cols = [slice(h * hd, (h + 1) * hd) for h in range(hb)]
            old = [state[h] for h in range(hb)]
            o, new = _gdn_chunks(
                [q_ref[r, cs] for cs in cols], [k_ref[r, cs] for cs in cols], [v_ref[r, cs] for cs in cols],
                [b_ref[r, h:h + 1] for h in range(hb)], [g_ref[r, h:h + 1] for h in range(hb)], old)
            for h in range(hb):
                st_ref[h, c] = old[h]
                o_ref[r, cols[h]] = o[h]
                state[h] = new[h]
            return carry

        lax.fori_loop(0, n_chunks, step, 0)

    return pl.pallas_call(
        body, grid=(GDN_HEADS // hb, n_tokens // tt), in_specs=[q_s, k_s, v_s, vec, vec], out_specs=(q_s, st),
        out_shape=(jax.ShapeDtypeStruct((n_tokens, GDN_WIDTH), F32),
                   jax.ShapeDtypeStruct((GDN_HEADS, n_tokens // GDN_CHUNK, hd, hd), F32)),
        scratch_shapes=[pltpu.VMEM((hb, hd, hd), F32)],
        name="gdn_fwd", compiler_params=_params(2),
    )(qkv, qkv, qkv, beta, g)


def _gdn_bwd(qkv, beta, g, states, do):
    n_tokens = qkv.shape[0]
    hb, hd, tt = GDN_HEADS_PER_STEP, GDN_HEAD_DIM, GDN_TIME_TILE
    n_chunks = tt // GDN_CHUNK
    q_s, k_s, v_s, vec, st = _gdn_specs(n_tokens, True)

    def body(q_ref, k_ref, v_ref, b_ref, g_ref, st_ref, do_ref, dq_ref, dk_ref, dv_ref, db_ref, dg_ref, dstate):
        @pl.when(pl.program_id(1) == 0)
        def _():
            dstate[...] = jnp.zeros_like(dstate)

        def step(i, carry):
            c = n_chunks - 1 - i
            r = pl.ds(pl.multiple_of(c * GDN_CHUNK, GDN_CHUNK), GDN_CHUNK)
            cols = [slice(h * hd, (h + 1) * hd) for h in range(hb)]
            args = ([q_ref[r, cs] for cs in cols], [k_ref[r, cs] for cs in cols], [v_ref[r, cs] for cs in cols],
                    [b_ref[r, h:h + 1] for h in range(hb)], [g_ref[r, h:h + 1] for h in range(hb)],
                    [st_ref[h, c] for h in range(hb)])
            cts = ([do_ref[r, cs] for cs in cols], [dstate[h] for h in range(hb)])
            dq, dk, dv, db, dg, dst = jax.vjp(_gdn_chunks, *args)[1](cts)
            for h in range(hb):
                dq_ref[r, cols[h]] = dq[h]
                dk_ref[r, cols[h]] = dk[h]
                dv_ref[r, cols[h]] = dv[h]
                db_ref[r, h:h + 1] = db[h]
                dg_ref[r, h:h + 1] = dg[h]
                dstate[h] = dst[h]
            return carry

        lax.fori_loop(0, n_chunks, step, 0)

    wide = jax.ShapeDtypeStruct((n_tokens, GDN_WIDTH), F32)
    thin = jax.ShapeDtypeStruct(beta.shape, F32)
    dq, dk, dv, db, dg = pl.pallas_call(
        body, grid=(GDN_HEADS // hb, n_tokens // tt), in_specs=[q_s, k_s, v_s, vec, vec, st, q_s],
        out_specs=(q_s, q_s, q_s, vec, vec), out_shape=(wide, wide, wide, thin, thin),
        scratch_shapes=[pltpu.VMEM((hb, hd, hd), F32)],
        name="gdn_bwd", compiler_params=_params(2),
    )(qkv, qkv, qkv, beta, g, states, do)
    return jnp.concatenate([dq, dk, dv], axis=1), db, dg


FFN_ROW_TILE = 256


def _resident(shape):
    return pl.BlockSpec(shape, lambda i: (0,) * len(shape), pipeline_mode=pl.Buffered(1))


def _ffn_fwd(x, gain, wg, wu, wd, name):
    n_tokens, d = x.shape
    n_shards, n, _ = wg.shape
    tm = FFN_ROW_TILE

    def body(x_ref, gain_ref, wg_ref, wu_ref, wd_ref, o_ref, g_ref, u_ref):
        xv = x_ref[...]
        h = (xv * lax.rsqrt(jnp.mean(xv * xv, axis=-1, keepdims=True) + EPS) * gain_ref[...]).astype(BF16)
        acc = jnp.zeros((tm, d), F32)
        for j in range(n_shards):
            g = lax.dot_general(h, wg_ref[j], NT, preferred_element_type=F32)
            u = lax.dot_general(h, wu_ref[j], NT, preferred_element_type=F32)
            g_ref[j] = g
            u_ref[j] = u
            a = (g * jax.nn.sigmoid(g) * u).astype(BF16)
            acc = acc + lax.dot_general(a, wd_ref[j], NN, preferred_element_type=F32)
        o_ref[...] = xv + 0.5 * acc

    row = pl.BlockSpec((tm, d), lambda i: (i, 0))
    hid = pl.BlockSpec((n_shards, tm, n), lambda i: (0, i, 0))
    return pl.pallas_call(
        body, grid=(n_tokens // tm,),
        in_specs=[row, _resident(gain.shape), _resident(wg.shape), _resident(wu.shape), _resident(wd.shape)],
        out_specs=(row, hid, hid),
        out_shape=(jax.ShapeDtypeStruct(x.shape, F32), jax.ShapeDtypeStruct((n_shards, n_tokens, n), F32),
                   jax.ShapeDtypeStruct((n_shards, n_tokens, n), F32)),
        name=name, compiler_params=_params(1),
    )(x, gain, wg, wu, wd)


def _ffn_bwd_rows(x, gain, dy, g, u, wg, wu, wd, name):
    n_tokens, d = x.shape
    n_shards, n, _ = wg.shape
    tm = FFN_ROW_TILE

    def body(x_ref, gain_ref, dy_ref, g_ref, u_ref, wg_ref, wu_ref, wd_ref,
             dx_ref, dgain_ref, h_ref, dyh_ref, a_ref, dg_ref, du_ref):
        xv, dyv, gain_v = x_ref[...], dy_ref[...], gain_ref[...]
        r = lax.rsqrt(jnp.mean(xv * xv, axis=-1, keepdims=True) + EPS)
        xhat = xv * r
        h_ref[...] = (xhat * gain_v).astype(BF16)
        dyh = (0.5 * dyv).astype(BF16)
        dyh_ref[...] = dyh
        dh = jnp.zeros((tm, d), F32)
        for j in range(n_shards):
            da = lax.dot_general(dyh, wd_ref[j], NT, preferred_element_type=F32)
            gv, uv = g_ref[j], u_ref[j]
            sg = jax.nn.sigmoid(gv)
            silu = gv * sg
            a_ref[j] = (silu * uv).astype(BF16)
            dg = (da * uv * (sg + silu * (1.0 - sg))).astype(BF16)
            du = (da * silu).astype(BF16)
            dg_ref[j] = dg
            du_ref[j] = du
            dh = dh + lax.dot_general(dg, wg_ref[j], NN, preferred_element_type=F32)
            dh = dh + lax.dot_general(du, wu_ref[j], NN, preferred_element_type=F32)
        dxhat = dh * gain_v
        dx_ref[...] = dyv + r * (dxhat - xhat * jnp.mean(dxhat * xhat, axis=-1, keepdims=True))

        @pl.when(pl.program_id(0) == 0)
        def _():
            dgain_ref[...] = jnp.zeros_like(dgain_ref)

        dgain_ref[...] += jnp.sum(dh * xhat, axis=0, keepdims=True)

    row = pl.BlockSpec((tm, d), lambda i: (i, 0))
    hid = pl.BlockSpec((n_shards, tm, n), lambda i: (0, i, 0))
    hid_shape = (n_shards, n_tokens, n)
    return pl.pallas_call(
        body, grid=(n_tokens // tm,),
        in_specs=[row, _resident(gain.shape), row, hid, hid, _resident(wg.shape), _resident(wu.shape),
                  _resident(wd.shape)],
        out_specs=(row, pl.BlockSpec(gain.shape, lambda i: (0, 0)), row, row, hid, hid, hid),
        out_shape=(jax.ShapeDtypeStruct(x.shape, F32), jax.ShapeDtypeStruct(gain.shape, F32),
                   jax.ShapeDtypeStruct(x.shape, BF16), jax.ShapeDtypeStruct(x.shape, BF16),
                   jax.ShapeDtypeStruct(hid_shape, BF16), jax.ShapeDtypeStruct(hid_shape, BF16),
                   jax.ShapeDtypeStruct(hid_shape, BF16)),
        name=name, compiler_params=_params(1),
    )(x, gain, dy, g, u, wg, wu, wd)


def _ffn_bwd_weights(h, dyh, a, dg, du, name):
    n_shards, n_tokens, n = a.shape
    d = h.shape[1]

    def body(h_ref, dyh_ref, a_ref, dg_ref, du_ref, dwg_ref, dwu_ref, dwd_ref):
        hv = h_ref[...]
        dwg_ref[0] = lax.dot_general(dg_ref[0], hv, TN, preferred_element_type=F32)
        dwu_ref[0] = lax.dot_general(du_ref[0], hv, TN, preferred_element_type=F32)
        dwd_ref[0] = lax.dot_general(a_ref[0], dyh_ref[...], TN, preferred_element_type=F32)

    hid = pl.BlockSpec((1, n_tokens, n), lambda j: (j, 0, 0))
    out = pl.BlockSpec((1, n, d), lambda j: (j, 0, 0))
    return pl.pallas_call(
        body, grid=(n_shards,), in_specs=[_resident(h.shape), _resident(dyh.shape), hid, hid, hid],
        out_specs=(out, out, out), out_shape=(jax.ShapeDtypeStruct((n_shards, n, d), F32),) * 3,
        name=name, compiler_params=_params(1),
    )(h, dyh, a, dg, du)


IN_PIECES = (("wq_a", 0, 768), ("wk_a", 768, 1536), ("wv_a", 1536, 2304), ("w_qkvb", 2304, 5376),
             ("w_small", 5376, 5392), ("w_ggate", 5392, 6416), ("w_gatea", 6416, 7440), ("w_gateb", 7440, 8464))
IN_NAMES = tuple(name for name, _, _ in IN_PIECES)


def _in_rows(lo, hi):
    return lo, max(hi, lo + LANES)


def _in_proj_fwd(x, gain, wt):
    n_tokens, d = x.shape
    tm = FFN_ROW_TILE
    rows = [_in_rows(lo, hi) for _, lo, hi in IN_PIECES]

    def body(x_ref, gain_ref, wt_ref, *o_refs):
        xv = x_ref[...]
        h = (xv * lax.rsqrt(jnp.mean(xv * xv, axis=-1, keepdims=True) + EPS) * gain_ref[...]).astype(BF16)
        for (lo, hi), o_ref in zip(rows, o_refs):
            o_ref[...] = lax.dot_general(h, wt_ref[lo:hi, :], NT, preferred_element_type=F32)

    return pl.pallas_call(
        body, grid=(n_tokens // tm,),
        in_specs=[pl.BlockSpec((tm, d), lambda i: (i, 0)), _resident(gain.shape), _resident(wt.shape)],
        out_specs=tuple(pl.BlockSpec((tm, hi - lo), lambda i: (i, 0)) for lo, hi in rows),
        out_shape=tuple(jax.ShapeDtypeStruct((n_tokens, hi - lo), F32) for lo, hi in rows),
        name="in_proj_fwd", compiler_params=_params(1),
    )(x, gain, wt)


def _in_proj_bwd_rows(x, gain, dres, dzs, wt):
    n_tokens, d = x.shape
    tm = FFN_ROW_TILE
    n = len(dzs)
    rows = [_in_rows(lo, hi) for _, lo, hi in IN_PIECES]

    def body(x_ref, gain_ref, dres_ref, *refs):
        dz_refs, wt_ref = refs[:n], refs[n]
        dx_ref, dgain_ref, h_ref = refs[n + 1:]
        xv, gain_v = x_ref[...], gain_ref[...]
        r = lax.rsqrt(jnp.mean(xv * xv, axis=-1, keepdims=True) + EPS)
        xhat = xv * r
        h_ref[...] = (xhat * gain_v).astype(BF16)
        dh = jnp.zeros((tm, d), F32)
        for dz_ref, (lo, hi) in zip(dz_refs, rows):
            dh = dh + lax.dot_general(dz_ref[...].astype(BF16), wt_ref[lo:hi, :], NN, preferred_element_type=F32)
        dxhat = dh * gain_v
        dx_ref[...] = dres_ref[...] + r * (dxhat - xhat * jnp.mean(dxhat * xhat, axis=-1, keepdims=True))

        @pl.when(pl.program_id(0) == 0)
        def _():
            dgain_ref[...] = jnp.zeros_like(dgain_ref)

        dgain_ref[...] += jnp.sum(dh * xhat, axis=0, keepdims=True)

    row = pl.BlockSpec((tm, d), lambda i: (i, 0))
    return pl.pallas_call(
        body, grid=(n_tokens // tm,),
        in_specs=([row, _resident(gain.shape), row]
                  + [pl.BlockSpec((tm, dz.shape[1]), lambda i: (i, 0)) for dz in dzs] + [_resident(wt.shape)]),
        out_specs=(row, pl.BlockSpec(gain.shape, lambda i: (0, 0)), row),
        out_shape=(jax.ShapeDtypeStruct(x.shape, F32), jax.ShapeDtypeStruct(gain.shape, F32),
                   jax.ShapeDtypeStruct(x.shape, BF16)),
        name="in_proj_bwd_rows", compiler_params=_params(1),
    )(x, gain, dres, *dzs, wt)


def _in_proj_bwd_weight(dwt, h, dz, lo, hi, name):
    n_tokens, d = h.shape
    width = hi - lo
    tn = _tile(width, 512) if width >= LANES else width
    dz_tile = max(tn, LANES)

    def body(dwt_ref, h_ref, dz_ref, o_ref):
        o_ref[...] = lax.dot_general(dz_ref[:, :tn].astype(BF16), h_ref[...], TN, preferred_element_type=F32)

    return pl.pallas_call(
        body, grid=(width // tn,),
        in_specs=[ANY, _resident(h.shape), pl.BlockSpec((n_tokens, dz_tile), lambda j: (0, j))],
        out_specs=pl.BlockSpec((pl.Element(tn), pl.Element(d)), lambda j: (pl.multiple_of(lo + j * tn, 16), 0)),
        out_shape=jax.ShapeDtypeStruct(dwt.shape, F32), input_output_aliases={0: 0}, name=name,
        compiler_params=_params(1),
    )(dwt, h, dz)


def _split_small(z):
    return z[:, :GDN_HEADS], z[:, GDN_HEADS:2 * GDN_HEADS]


def _heads3(q, k, v):
    return _to_heads(q), _to_heads(k), _to_heads(v)


def _tokens6(o, lse):
    return tuple(_from_heads(o)) + tuple(_from_heads(lse))


def mixer_forward(x1, w, small):
    n_tokens = x1.shape[0]
    proj = dict(zip(IN_NAMES, _in_proj_fwd(x1, small["mix_norm"], w["w_in_t"])))
    cos, sin = _rope_tables(n_tokens)
    q_rot = _rope_call(proj["wq_a"], cos, sin, "rope_q")
    k_rot = _rope_call(proj["wk_a"], cos, sin, "rope_k")
    (qh, kh, vh), heads_vjp = jax.vjp(_heads3, q_rot, k_rot, proj["wv_a"])
    o, lse = _attn_fwd(qh, kh, vh)
    per_group, tokens_vjp = jax.vjp(_tokens6, o, lse)
    ya = _rowwise_fwd(_combine_fn, "combine", per_group, (), (), 512, 1)[0]
    pa = _matmul(ya, w["w_branch_a"], name="branch_a")
    qkv = _conv_fwd(proj["w_qkvb"], small["gdn_conv_w"])
    raw, small_vjp = jax.vjp(_split_small, proj["w_small"])
    gdn_params = (small["gdn_a_log"], small["gdn_dt_bias"])
    beta, gcum = _rowwise_fwd(_beta_decay_fn, "beta_decay", raw, (), gdn_params, 512, 1)
    ob, states = _gdn_fwd(qkv, beta, gcum)
    gate_in = (ob, proj["w_ggate"])
    yb = _rowwise_fwd(_outnorm_gate_fn, "outnorm_gate", gate_in, (), (small["gdn_out_norm"],), 512, GDN_HEADS)[0]
    pb = _matmul(yb, w["w_branch_b"], name="branch_b")
    merge_in = (proj["w_gatea"], proj["w_gateb"], pa, pb)
    merged = _rowwise_fwd(_merge_fn, "merge", merge_in, (), (), 256, 1)[0]
    x2 = _matmul(merged, w["w_out"], name="out", res=x1)
    saved = dict(x1=x1, proj=proj, cos=cos, sin=sin, heads_vjp=heads_vjp, heads=(qh, kh, vh), tokens_vjp=tokens_vjp,
                 per_group=per_group, ya=ya, qkv=qkv, raw=raw, small_vjp=small_vjp, beta=beta, gcum=gcum, states=states,
                 gate_in=gate_in, yb=yb, merge_in=merge_in, merged=merged)
    return x2, saved


def mixer_backward(dx2, s, w, small):
    proj = s["proj"]
    dmerged = _matmul(dx2, w["w_out"], name="out_da", tb=True)
    grads = dict(w_out=_matmul(s["merged"], dx2, name="out_dw", ta=True))
    dgate_a, dgate_b, dpa, dpb = _rowwise_bwd(_merge_fn, "merge_bwd", s["merge_in"], (), (), (dmerged,), 256, 1)
    dyb = _matmul(dpb, w["w_branch_b"], name="branch_b_da", tb=True)
    grads["w_branch_b"] = _matmul(s["yb"], dpb, name="branch_b_dw", ta=True)
    dya = _matmul(dpa, w["w_branch_a"], name="branch_a_da", tb=True)
    grads["w_branch_a"] = _matmul(s["ya"], dpa, name="branch_a_dw", ta=True)
    dob, dggate, grads["gdn_out_norm"] = _rowwise_bwd(
        _outnorm_gate_fn, "outnorm_gate_bwd", s["gate_in"], (), (small["gdn_out_norm"],), (dyb,), 512, GDN_HEADS)
    dqkv, dbeta, dgcum = _gdn_bwd(s["qkv"], s["beta"], s["gcum"], s["states"], dob)
    gdn_params = (small["gdn_a_log"], small["gdn_dt_bias"])
    dbeta_raw, ddecay_raw, grads["gdn_a_log"], grads["gdn_dt_bias"] = _rowwise_bwd(
        _beta_decay_fn, "beta_decay_bwd", s["raw"], (), gdn_params, (dbeta, dgcum), 512, 1)
    dsmall = s["small_vjp"]((dbeta_raw, ddecay_raw))[0]
    dqkvb, grads["gdn_conv_w"] = _conv_bwd(proj["w_qkvb"], small["gdn_conv_w"], dqkv)
    dper_group = _rowwise_bwd(_combine_fn, "combine_bwd", s["per_group"], (), (), (dya,), 512, 1)
    do, dlse = s["tokens_vjp"](tuple(dper_group))
    dqh, dkh, dvh = _attn_bwd(*s["heads"], do, dlse)
    dq_rot, dk_rot, dv = s["heads_vjp"]((dqh, dkh, dvh))
    dq = _rope_call(dq_rot, s["cos"], -s["sin"], "rope_q_bwd")
    dk = _rope_call(dk_rot, s["cos"], -s["sin"], "rope_k_bwd")
    dzs = (dq, dk, dv, dqkvb, dsmall, dggate, dgate_a, dgate_b)
    dx1, grads["mix_norm"], h = _in_proj_bwd_rows(s["x1"], small["mix_norm"], dx2, dzs, w["w_in_t"])
    dwt = lax.empty(w["w_in_t"].shape, F32)
    for (name, lo, hi), dz in zip(IN_PIECES, dzs):
        dwt = _in_proj_bwd_weight(dwt, h, dz, lo, hi, "in_proj_dw_" + name)
    grads["w_in_t"] = dwt
    return dx1, grads


def ffn_forward(x, gain, w, tag):
    out, g, u = _ffn_fwd(x, gain, w[tag + "_w_gate"], w[tag + "_w_up"], w[tag + "_w_down"], tag + "_fwd")
    return out, (x, g, u)


def ffn_backward(dy, saved, gain, w, tag):
    x, g, u = saved
    weights = (w[tag + "_w_gate"], w[tag + "_w_up"], w[tag + "_w_down"])
    dx, dgain, h, dyh, a, dg, du = _ffn_bwd_rows(x, gain, dy, g, u, *weights, tag + "_bwd_rows")
    return dx, dgain, _ffn_bwd_weights(h, dyh, a, dg, du, tag + "_bwd_weights")


def loss_head(x3, target, gain):
    row_loss = _rowwise_fwd(_loss_fn, "loss", (x3,), (target,), (gain,), 256, 1)[0]
    dx3, dgain = _rowwise_bwd(_loss_fn, "loss_bwd", (x3,), (target,), (gain,), (jnp.ones_like(row_loss),), 256, 1)
    return jnp.sum(row_loss), dx3, dgain


BIG_WEIGHTS = ("ffn1_w_gate", "ffn1_w_up", "ffn1_w_down", "w_in", "w_branch_a", "w_branch_b", "w_out",
               "ffn2_w_gate", "ffn2_w_up", "ffn2_w_down")
TRANSPOSED = ("ffn1_w_gate", "ffn1_w_up", "w_in", "ffn2_w_gate", "ffn2_w_up")
CONV_SHARD = (GDN_CONV, 3 * GDN_WIDTH // N_DEV)
SMALL_ROWS = 24
ANY = pl.BlockSpec(memory_space=pl.ANY)


def _position():
    return lax.axis_index("x"), lax.axis_index("y"), lax.axis_index("c")


def all_gather_shards(shards, name):
    n = len(shards)

    def body(*refs):
        x_refs, out_refs = refs[:n], refs[n:2 * n]
        send_sems, recv_sems, local_sems = refs[2 * n:]
        x, y, c = _position()
        me, sibling = (x, y, c), (x, y, 1 - c)
        chips = [(1 - x, y), (x, 1 - y), (1 - x, 1 - y)]

        def slab(a, px, py, pc):
            return out_refs[a].at[4 * px + 2 * py + pc]

        def copy(a, k, block, to, src=None):
            return pltpu.make_async_remote_copy(
                src_ref=slab(a, *block) if src is None else src, dst_ref=slab(a, *block),
                send_sem=send_sems.at[7 * a + k], recv_sem=recv_sems.at[7 * a + k], device_id=to, device_id_type=MESH)

        mine = [pltpu.make_async_copy(x_refs[a], slab(a, *me), local_sems.at[a]) for a in range(n)]
        for cp in mine:
            cp.start()
        first = []
        for j, chip in enumerate(chips):
            first += [copy(a, 1 + j, me, (*chip, c), src=x_refs[a]) for a in range(n)]
        first += [copy(a, 0, me, sibling, src=x_refs[a]) for a in range(n)]
        for cp in first:
            cp.start()
        passed = []
        for j, chip in enumerate(chips):
            for a in range(n):
                copy(a, 1 + j, (*chip, c), me).wait_recv()
                cp = copy(a, 4 + j, (*chip, c), sibling)
                cp.start()
                passed.append(cp)
        for a in range(n):
            copy(a, 0, sibling, me).wait_recv()
        for j, chip in enumerate(chips):
            for a in range(n):
                copy(a, 4 + j, (*chip, 1 - c), me).wait_recv()
        for cp in first + passed:
            cp.wait_send()
        for cp in mine:
            cp.wait()

    return pl.pallas_call(
        body, out_shape=tuple(jax.ShapeDtypeStruct((N_DEV,) + s.shape, s.dtype) for s in shards),
        in_specs=[ANY] * n, out_specs=(ANY,) * n,
        scratch_shapes=[pltpu.SemaphoreType.DMA((7 * n,)), pltpu.SemaphoreType.DMA((7 * n,)),
                        pltpu.SemaphoreType.DMA((n,))],
        name=name,
    )(*shards)


def exchange_with_sibling(grads):
    n = len(grads)

    def body(*refs):
        g_refs, recv_refs = refs[:n], refs[n:2 * n]
        send_sems, recv_sems = refs[2 * n:]
        x, y, c = _position()
        copies = [pltpu.make_async_remote_copy(
            src_ref=g_refs[a].at[2 * k + 1 - c], dst_ref=recv_refs[a].at[k], send_sem=send_sems.at[4 * a + k],
            recv_sem=recv_sems.at[4 * a + k], device_id=(x, y, 1 - c), device_id_type=MESH)
            for k in range(4) for a in range(n)]
        for cp in copies:
            cp.start()
        for cp in copies:
            cp.wait()

    return pl.pallas_call(
        body, out_shape=tuple(jax.ShapeDtypeStruct((4,) + g.shape[1:], g.dtype) for g in grads),
        in_specs=[ANY] * n, out_specs=(ANY,) * n,
        scratch_shapes=[pltpu.SemaphoreType.DMA((4 * n,)), pltpu.SemaphoreType.DMA((4 * n,))], name="rs_sibling",
    )(*grads)


ELEMENTWISE_TILE_BYTES = 1536 * 1024


def _tile2(rows, cols):
    if rows % 256 == 0:
        return 256, cols
    if rows * cols * 4 > ELEMENTWISE_TILE_BYTES and cols % 256 == 0:
        return rows, 256
    return rows, cols


def add_sibling(grads, received, core, name):
    _, rows, width = grads.shape
    tr, tc = _tile2(rows, width)

    def body(c_ref, g_ref, r_ref, o_ref):
        o_ref[...] = (g_ref[...] + r_ref[...]).astype(BF16)

    blk = (1, tr, tc)
    return pl.pallas_call(
        body,
        grid_spec=pltpu.PrefetchScalarGridSpec(
            num_scalar_prefetch=1, grid=(4, rows // tr, width // tc),
            in_specs=[pl.BlockSpec(blk, lambda k, i, j, c_ref: (2 * k + c_ref[0], i, j)),
                      pl.BlockSpec(blk, lambda k, i, j, c_ref: (k, i, j))],
            out_specs=pl.BlockSpec(blk, lambda k, i, j, c_ref: (k, i, j))),
        out_shape=jax.ShapeDtypeStruct((4, rows, width), BF16), name=name, compiler_params=_params(3),
    )(core, grads, received)


def exchange_with_chips(partials):
    n = len(partials)

    def body(*refs):
        p_refs, recv_refs = refs[:n], refs[n:2 * n]
        send_sems, recv_sems = refs[2 * n:]
        x, y, c = _position()
        chips = [(1 - x, y), (x, 1 - y), (1 - x, 1 - y)]
        copies = [pltpu.make_async_remote_copy(
            src_ref=p_refs[a].at[2 * cx + cy], dst_ref=recv_refs[a].at[j], send_sem=send_sems.at[3 * a + j],
            recv_sem=recv_sems.at[3 * a + j], device_id=(cx, cy, c), device_id_type=MESH)
            for a in range(n) for j, (cx, cy) in enumerate(chips)]
        for cp in copies:
            cp.start()
        for cp in copies:
            cp.wait()

    return pl.pallas_call(
        body, out_shape=tuple(jax.ShapeDtypeStruct((3,) + p.shape[1:], p.dtype) for p in partials),
        in_specs=[ANY] * n, out_specs=(ANY,) * n,
        scratch_shapes=[pltpu.SemaphoreType.DMA((3 * n,)), pltpu.SemaphoreType.DMA((3 * n,))], name="rs_chips",
    )(*partials)


HBM = pl.BlockSpec(memory_space=pltpu.HBM)
SEM = pl.BlockSpec(memory_space=pltpu.SEMAPHORE)
DATAFLOW_EFFECT = pltpu.SideEffectType.DATAFLOW_SIDE_EFFECTING
N_PEERS = N_DEV - 1


def _peer(mask):
    x, y, c = _position()
    px = 1 - x if mask & 4 else x
    py = 1 - y if mask & 2 else y
    pc = 1 - c if mask & 1 else c
    return (px, py, pc), 4 * px + 2 * py + pc


def _direct_copies(src_refs, land_refs, send_sems, recv_sems, scatter):
    x, y, c = _position()
    me = 4 * x + 2 * y + c
    copies = []
    for a, (src, land) in enumerate(zip(src_refs, land_refs)):
        for mask in range(1, N_DEV):
            peer, peer_index = _peer(mask)
            k = N_PEERS * a + mask - 1
            copies.append(pltpu.make_async_remote_copy(
                src_ref=src.at[peer_index] if scatter else src,
                dst_ref=land.at[mask - 1] if scatter else land.at[me],
                send_sem=send_sems.at[k], recv_sem=recv_sems.at[k], device_id=peer, device_id_type=MESH))
    return copies


def direct_exchange_start(arrays, scatter, name):
    n = len(arrays)
    slabs = N_PEERS if scatter else N_DEV
    lands = [lax.empty((slabs,) + (a.shape[1:] if scatter else a.shape), a.dtype) for a in arrays]

    def body(*refs):
        src_refs, land_refs = refs[:n], refs[n:2 * n]
        send_sems, recv_sems = refs[2 * n], refs[2 * n + 1]
        token = refs[-1]
        for cp in _direct_copies(src_refs, land_refs, send_sems, recv_sems, scatter):
            cp.start()
        token[...] = jnp.zeros_like(token)

    sems = pltpu.SemaphoreType.DMA((N_PEERS * n,))
    outs = pl.pallas_call(
        body, name=name,
        out_shape=(sems, sems) + tuple(pltpu.HBM(a.shape, a.dtype) for a in arrays)
        + tuple(pltpu.HBM(l.shape, l.dtype) for l in lands) + (jax.ShapeDtypeStruct((8, LANES), F32),),
        in_specs=[HBM] * (2 * n), out_specs=(SEM, SEM) + (HBM,) * (2 * n) + (pl.BlockSpec(memory_space=pltpu.VMEM),),
        input_output_aliases={i: 2 + i for i in range(2 * n)},
        compiler_params=pltpu.CompilerParams(has_side_effects=DATAFLOW_EFFECT),
    )(*[pltpu.with_memory_space_constraint(a, pltpu.HBM) for a in list(arrays) + lands])
    return outs[0], outs[1], outs[2:2 + n], outs[2 + n:2 + 2 * n], outs[-1]


def direct_exchange_wait(send_sems, recv_sems, arrays, lands, after, scatter, name):
    n = len(arrays)

    def body(*refs):
        src_refs, land_refs = refs[:n], refs[n:2 * n]
        send_sems, recv_sems = refs[2 * n], refs[2 * n + 1]
        for cp in _direct_copies(src_refs, land_refs, send_sems, recv_sems, scatter):
            cp.wait_send()
            cp.wait_recv()

    outs = pl.pallas_call(
        body, name=name,
        out_shape=tuple(pltpu.HBM(a.shape, a.dtype) for a in arrays) + tuple(pltpu.HBM(l.shape, l.dtype) for l in lands),
        in_specs=[HBM] * (2 * n) + [SEM, SEM, pl.BlockSpec(memory_space=pl.ANY)], out_specs=(HBM,) * (2 * n),
        input_output_aliases={i: i for i in range(2 * n)},
        compiler_params=pltpu.CompilerParams(has_side_effects=DATAFLOW_EFFECT),
    )(*arrays, *lands, send_sems, recv_sems, after)
    return outs[n:]


def adamw_direct(w, m, v, grads, received, me, name):
    rows, cols = w.shape[-2:]
    tr, tc = _tile2(rows, cols)

    def body(me_ref, w_ref, m_ref, v_ref, own_ref, r_ref, g_ref, d_ref, nm_ref, nv_ref):
        gv = own_ref[0]
        for j in range(N_PEERS):
            gv = gv + r_ref[j].astype(F32)
        nm = ADAM_B1 * m_ref[0] + (1.0 - ADAM_B1) * gv
        nv = ADAM_B2 * v_ref[0] + (1.0 - ADAM_B2) * (gv * gv)
        m_hat = nm / (1.0 - ADAM_B1 ** ADAM_STEP)
        v_hat = nv / (1.0 - ADAM_B2 ** ADAM_STEP)
        g_ref[0] = gv
        d_ref[0] = -ADAM_LR * (m_hat / (jnp.sqrt(v_hat) + ADAM_EPS) + ADAM_WD * w_ref[0])
        nm_ref[0] = nm
        nv_ref[0] = nv

    one = pl.BlockSpec((1, tr, tc), lambda i, j, me_ref: (0, i, j))
    out = jax.ShapeDtypeStruct((1, rows, cols), F32)
    return pl.pallas_call(
        body,
        grid_spec=pltpu.PrefetchScalarGridSpec(
            num_scalar_prefetch=1, grid=(rows // tr, cols // tc),
            in_specs=[one, one, one, pl.BlockSpec((1, tr, tc), lambda i, j, me_ref: (me_ref[0], i, j)),
                      pl.BlockSpec((N_PEERS, tr, tc), lambda i, j, me_ref: (0, i, j))],
            out_specs=(one,) * 4),
        out_shape=(out,) * 4, name=name, compiler_params=_params(2),
    )(me, w, m, v, grads, received)


def all_reduce_small(vals):
    rows, width = vals.shape

    def body(x_ref, out_ref, all_ref, send_sems, recv_sems):
        x, y, c = _position()
        me, sibling = (x, y, c), (x, y, 1 - c)
        chips = [(1 - x, y), (x, 1 - y), (1 - x, 1 - y)]

        def slab(px, py, pc):
            return all_ref.at[4 * px + 2 * py + pc]

        def copy(k, block, to, src=None):
            return pltpu.make_async_remote_copy(
                src_ref=slab(*block) if src is None else src, dst_ref=slab(*block),
                send_sem=send_sems.at[k], recv_sem=recv_sems.at[k], device_id=to, device_id_type=MESH)

        first = [copy(0, me, sibling, src=x_ref)]
        first += [copy(1 + j, me, (*chip, c), src=x_ref) for j, chip in enumerate(chips)]
        for cp in first:
            cp.start()
        all_ref[4 * x + 2 * y + c] = x_ref[...]
        passed = [copy(4 + j, (*chip, c), sibling) for j, chip in enumerate(chips)]
        for j, chip in enumerate(chips):
            copy(1 + j, (*chip, c), me).wait_recv()
            passed[j].start()
        copy(0, sibling, me).wait_recv()
        for j, chip in enumerate(chips):
            copy(4 + j, (*chip, 1 - c), me).wait_recv()
        for cp in first + passed:
            cp.wait_send()
        total = all_ref[0]
        for d in range(1, N_DEV):
            total = total + all_ref[d]
        out_ref[...] = total

    vmem = pl.BlockSpec(memory_space=pltpu.VMEM)
    return pl.pallas_call(
        body, out_shape=(jax.ShapeDtypeStruct(vals.shape, F32), jax.ShapeDtypeStruct((N_DEV, rows, width), F32)),
        in_specs=[vmem], out_specs=(vmem, vmem),
        scratch_shapes=[pltpu.SemaphoreType.DMA((7,)), pltpu.SemaphoreType.DMA((7,))], name="small_allreduce",
    )(vals)[0]


def adamw(w, g, m, v, name):
    shape = w.shape
    w2, g2, m2, v2 = [a.reshape((-1, shape[-1])) for a in (w, g, m, v)]
    rows, cols = w2.shape
    tr = 256 if rows % 256 == 0 else rows

    def body(w_ref, g_ref, m_ref, v_ref, d_ref, nm_ref, nv_ref):
        gv = g_ref[...]
        nm = ADAM_B1 * m_ref[...] + (1.0 - ADAM_B1) * gv
        nv = ADAM_B2 * v_ref[...] + (1.0 - ADAM_B2) * (gv * gv)
        m_hat = nm / (1.0 - ADAM_B1 ** ADAM_STEP)
        v_hat = nv / (1.0 - ADAM_B2 ** ADAM_STEP)
        d_ref[...] = -ADAM_LR * (m_hat / (jnp.sqrt(v_hat) + ADAM_EPS) + ADAM_WD * w_ref[...])
        nm_ref[...] = nm
        nv_ref[...] = nv

    blk = pl.BlockSpec((tr, cols), lambda i: (i, 0))
    out = jax.ShapeDtypeStruct((rows, cols), F32)
    outs = pl.pallas_call(
        body, grid=(rows // tr,), in_specs=[blk] * 4, out_specs=(blk,) * 3, out_shape=(out,) * 3,
        name=name, compiler_params=_params(1),
    )(w2, g2, m2, v2)
    return tuple(o.reshape(shape) for o in outs)


def adamw_summed(w, m, v, grads, from_sibling, received, me, name):
    rows, cols = w.shape[-2:]
    tr, tc = _tile2(rows, cols)

    def body(me_ref, w_ref, m_ref, v_ref, own_ref, sib_ref, r_ref, g_ref, d_ref, nm_ref, nv_ref):
        gv = own_ref[0] + sib_ref[0]
        for j in range(3):
            gv = gv + r_ref[j].astype(F32)
        nm = ADAM_B1 * m_ref[0] + (1.0 - ADAM_B1) * gv
        nv = ADAM_B2 * v_ref[0] + (1.0 - ADAM_B2) * (gv * gv)
        m_hat = nm / (1.0 - ADAM_B1 ** ADAM_STEP)
        v_hat = nv / (1.0 - ADAM_B2 ** ADAM_STEP)
        g_ref[0] = gv
        d_ref[0] = -ADAM_LR * (m_hat / (jnp.sqrt(v_hat) + ADAM_EPS) + ADAM_WD * w_ref[0])
        nm_ref[0] = nm
        nv_ref[0] = nv

    one = pl.BlockSpec((1, tr, tc), lambda i, j, me_ref: (0, i, j))
    out = jax.ShapeDtypeStruct((1, rows, cols), F32)
    return pl.pallas_call(
        body,
        grid_spec=pltpu.PrefetchScalarGridSpec(
            num_scalar_prefetch=1, grid=(rows // tr, cols // tc),
            in_specs=[one, one, one, pl.BlockSpec((1, tr, tc), lambda i, j, me_ref: (me_ref[0], i, j)),
                      pl.BlockSpec((1, tr, tc), lambda i, j, me_ref: (me_ref[1], i, j)),
                      pl.BlockSpec((3, tr, tc), lambda i, j, me_ref: (0, i, j))],
            out_specs=(one,) * 4),
        out_shape=(out,) * 4, name=name, compiler_params=_params(2),
    )(me, w, m, v, grads, from_sibling, received)


SMALL_VECTORS = ("ffn1_norm", "mix_norm", "ffn2_norm", "final_norm")


def _pack_small(gs):
    row = jnp.concatenate([gs["gdn_a_log"].reshape(-1), gs["gdn_dt_bias"].reshape(-1), gs["gdn_out_norm"].reshape(-1)])
    rows = [gs[n].reshape(1, D_MODEL) for n in SMALL_VECTORS]
    rows.append(jnp.pad(row, (0, D_MODEL - row.shape[0])).reshape(1, D_MODEL))
    rows.append(gs["gdn_conv_w"].reshape(-1, D_MODEL))
    packed = jnp.concatenate(rows, axis=0)
    return jnp.pad(packed, ((0, SMALL_ROWS - packed.shape[0]), (0, 0)))


def _unpack_small(packed):
    out = {n: packed[i].reshape(1, D_MODEL) for i, n in enumerate(SMALL_VECTORS)}
    row = packed[len(SMALL_VECTORS)]
    out["gdn_a_log"] = row[:GDN_HEADS].reshape(1, GDN_HEADS)
    out["gdn_dt_bias"] = row[GDN_HEADS:2 * GDN_HEADS].reshape(1, GDN_HEADS)
    out["gdn_out_norm"] = row[2 * GDN_HEADS:2 * GDN_HEADS + GDN_HEAD_DIM].reshape(1, GDN_HEAD_DIM)
    first = len(SMALL_VECTORS) + 1
    out["gdn_conv_w"] = packed[first:first + GDN_CONV * 3].reshape(GDN_CONV, 3 * GDN_WIDTH)
    return out


WEIGHTS = ("ffn1_norm", "ffn1_w_gate", "ffn1_w_up", "ffn1_w_down", "mix_norm", "w_in", "gdn_conv_w", "gdn_a_log",
           "gdn_dt_bias", "gdn_out_norm", "w_branch_a", "w_branch_b", "w_out", "ffn2_norm", "ffn2_w_gate",
           "ffn2_w_up", "ffn2_w_down", "final_norm")


def kernel(x, ffn1_norm, ffn1_w_gate, ffn1_w_up, ffn1_w_down, mix_norm, w_in, gdn_conv_w, gdn_a_log, gdn_dt_bias, gdn_out_norm, w_branch_a, w_branch_b, w_out, ffn2_norm, ffn2_w_gate, ffn2_w_up, ffn2_w_down, final_norm, loss_target, m_ffn1_norm, m_ffn1_w_gate, m_ffn1_w_up, m_ffn1_w_down, m_mix_norm, m_w_in, m_gdn_conv_w, m_gdn_a_log, m_gdn_dt_bias, m_gdn_out_norm, m_w_branch_a, m_w_branch_b, m_w_out, m_ffn2_norm, m_ffn2_w_gate, m_ffn2_w_up, m_ffn2_w_down, m_final_norm, v_ffn1_norm, v_ffn1_w_gate, v_ffn1_w_up, v_ffn1_w_down, v_mix_norm, v_w_in, v_gdn_conv_w, v_gdn_a_log, v_gdn_dt_bias, v_gdn_out_norm, v_w_branch_a, v_w_branch_b, v_w_out, v_ffn2_norm, v_ffn2_w_gate, v_ffn2_w_up, v_ffn2_w_down, v_final_norm):
    given = dict(locals())
    px, py, pc = _position()
    big_names = list(BIG_WEIGHTS)

    def shard_view(a, n):
        return a.transpose(0, 2, 1) if n in TRANSPOSED else a

    me = 4 * px + 2 * py + pc
    me_index = me.astype(jnp.int32).reshape(1)
    late = [n for n in big_names if n.startswith("ffn2")]
    early = [n for n in big_names if n not in late]
    shards = {n: shard_view(given[n], n)[0].astype(BF16) for n in big_names}
    gathered = dict(zip(early + ["gdn_conv_w"],
                        all_gather_shards([shards[n] for n in early] + [gdn_conv_w[0]], "gather_weights")))
    late_shards, _ = lax.optimization_barrier(([shards[n] for n in late], gathered[early[0]]))
    late_gather = direct_exchange_start(late_shards, False, "gather_ffn2_start")
    ffn1_norm, _ = lax.optimization_barrier((ffn1_norm, late_gather[4]))
    w = {n: gathered[n] for n in early if n.startswith("ffn")}
    w["w_in_t"] = gathered["w_in"].reshape(-1, D_MODEL)
    w["w_branch_a"] = gathered["w_branch_a"].transpose(1, 0, 2).reshape(256, D_MODEL)
    w["w_branch_b"] = gathered["w_branch_b"].reshape(D_MODEL, D_MODEL)
    w["w_out"] = gathered["w_out"].reshape(D_MODEL, D_MODEL)
    conv_full = gathered["gdn_conv_w"].transpose(1, 0, 2).reshape(GDN_CONV, 3 * GDN_WIDTH)
    small = dict(mix_norm=mix_norm, gdn_a_log=gdn_a_log, gdn_dt_bias=gdn_dt_bias, gdn_out_norm=gdn_out_norm,
                 gdn_conv_w=conv_full)

    x1, ffn1_saved = ffn_forward(x[0], ffn1_norm, w, "ffn1")
    x2, mixer_saved = mixer_forward(x1, w, small)
    late_lands = direct_exchange_wait(*late_gather[:4], x2, False, "gather_ffn2_wait")
    for n, land in zip(late, late_lands):
        w[n] = lax.dynamic_update_slice(land, shards[n][None], (me, 0, 0))
    x3, ffn2_saved = ffn_forward(x2, ffn2_norm, w, "ffn2")
    loss_local, dx3, g_final = loss_head(x3, loss_target[0], final_norm.reshape(1, D_MODEL))
    loss = lax.psum(loss_local, ("x", "y", "c"))
    dx2, g_ffn2_norm, dw2 = ffn_backward(dx3, ffn2_saved, ffn2_norm, w, "ffn2")
    late_scatter = direct_exchange_start([g.astype(BF16) for g in dw2], True, "rs_ffn2_start")
    dx2, _ = lax.optimization_barrier((dx2, late_scatter[4]))
    dx1, g_w = mixer_backward(dx2, mixer_saved, w, small)
    grad_x, g_ffn1_norm, dw1 = ffn_backward(dx1, ffn1_saved, ffn1_norm, w, "ffn1")
    g_small = dict(ffn1_norm=g_ffn1_norm, ffn2_norm=g_ffn2_norm, final_norm=g_final,
                   **{n: g_w[n] for n in ("mix_norm", "gdn_a_log", "gdn_dt_bias", "gdn_out_norm", "gdn_conv_w")})

    g_big = dict(zip(("ffn1_w_gate", "ffn1_w_up", "ffn1_w_down"), dw1))
    g_big["w_in"] = g_w["w_in_t"].reshape(N_DEV, -1, D_MODEL)
    g_big["w_branch_a"] = g_w["w_branch_a"].reshape(256, N_DEV, 128).transpose(1, 0, 2)
    g_big["w_branch_b"] = g_w["w_branch_b"].reshape(N_DEV, 128, D_MODEL)
    g_big["w_out"] = g_w["w_out"].reshape(N_DEV, 128, D_MODEL)
    g_list = [g_big[n] for n in early]
    core = pc.astype(jnp.int32).reshape(1)
    me_and_chip = jnp.stack([me, 2 * px + py]).astype(jnp.int32)
    from_sibling = exchange_with_sibling(g_list)
    partials = [add_sibling(g, r, core, "rs_add_" + n) for n, g, r in zip(early, g_list, from_sibling)]
    from_chips = exchange_with_chips(partials)

    def state_of(n):
        return [shard_view(given[p + n], n) for p in ("", "m_", "v_")]

    results = {}
    for n, g, sib, recv in zip(early, g_list, from_sibling, from_chips):
        outs = adamw_summed(*state_of(n), g, sib, recv, me_and_chip, "adamw_" + n)
        results[n] = tuple(shard_view(o, n) for o in outs)
    late_received = direct_exchange_wait(*late_scatter[:4], grad_x, True, "rs_ffn2_wait")
    for n, g, recv in zip(late, dw2, late_received):
        outs = adamw_direct(*state_of(n), g, recv, me_index, "adamw_" + n)
        results[n] = tuple(shard_view(o, n) for o in outs)

    small_sum = _unpack_small(all_reduce_small(_pack_small(g_small)))
    conv_cols = CONV_SHARD[1]
    small_sum["gdn_conv_w"] = lax.dynamic_slice(small_sum["gdn_conv_w"], (0, me * conv_cols), (GDN_CONV, conv_cols))
    for n in WEIGHTS:
        if n not in results:
            g = small_sum[n].reshape(given[n].shape)
            results[n] = (g,) + adamw(given[n], g, given["m_" + n], given["v_" + n], "adamw_" + n)

    outs = [[results[n][i] for n in WEIGHTS] for i in range(4)]
    return (loss, grad_x[None], *outs[0], *outs[1], *outs[2], *outs[3])
```

```python
import jax
import jax.numpy as jnp
from jax import lax
from jax.experimental import pallas as pl
from jax.experimental.pallas import tpu as pltpu

F32 = jnp.float32
BF16 = jnp.bfloat16
HI = lax.Precision.HIGHEST
MESH = pl.DeviceIdType.MESH

N_DEV = 8
D_MODEL = 1024
EPS = 1e-6
ROPE_THETA = 10000.0
DSW_DILATIONS = (1, 4, 16)
DSW_HEADS_PER_GROUP = 4
DSW_HEAD_DIM = 64
DSW_BLOCK = 128
GDN_HEADS = 8
GDN_HEAD_DIM = 128
GDN_WIDTH = 1024
GDN_CONV = 4
GDN_CHUNK = 64

ADAM_LR = 0.001
ADAM_B1 = 0.9
ADAM_B2 = 0.999
ADAM_EPS = 1e-08
ADAM_WD = 0.01
ADAM_STEP = 10

VMEM_LIMIT_BYTES = 56 * 1024 * 1024
LANES = 128

NN = (((1,), (0,)), ((), ()))
NT = (((1,), (1,)), ((), ()))
TN = (((0,), (0,)), ((), ()))


def _params(n_grid):
    return pltpu.CompilerParams(dimension_semantics=("arbitrary",) * n_grid, vmem_limit_bytes=VMEM_LIMIT_BYTES)


def _tile(n, pref):
    best = None
    t = LANES
    while t <= min(n, pref):
        if n % t == 0:
            best = t
        t += LANES
    return n if best is None else best


def _matmul(a, b, *, name, ta=False, tb=False, res=None, scale=1.0):
    K, M = a.shape if ta else a.shape[::-1]
    N = b.shape[0] if tb else b.shape[1]
    assert (b.shape[1] if tb else b.shape[0]) == K, (a.shape, b.shape, ta, tb)
    tm = _tile(M, 512)
    tn = _tile(N, 512)
    dn = (((0 if ta else 1,), (1 if tb else 0,)), ((), ()))

    def body(*refs):
        a_ref, b_ref = refs[:2]
        o_ref = refs[-1]
        acc = lax.dot_general(a_ref[...].astype(BF16), b_ref[...].astype(BF16), dn, preferred_element_type=F32)
        if scale != 1.0:
            acc = acc * scale
        if res is not None:
            acc = refs[2][...] + acc
        o_ref[...] = acc

    a_spec = pl.BlockSpec((K, tm), lambda i, j: (0, i)) if ta else pl.BlockSpec((tm, K), lambda i, j: (i, 0))
    b_spec = pl.BlockSpec((tn, K), lambda i, j: (j, 0)) if tb else pl.BlockSpec((K, tn), lambda i, j: (0, j))
    o_spec = pl.BlockSpec((tm, tn), lambda i, j: (i, j))
    ins, specs = [a, b], [a_spec, b_spec]
    if res is not None:
        ins.append(res)
        specs.append(o_spec)
    return pl.pallas_call(
        body, grid=(M // tm, N // tn), in_specs=specs, out_specs=o_spec,
        out_shape=jax.ShapeDtypeStruct((M, N), F32), name=name, compiler_params=_params(2),
    )(*ins)


def _rw_specs(arrs, tm, nblk):
    return [pl.BlockSpec((tm, a.shape[1] // nblk), lambda i, j: (i, j)) for a in arrs]


def _rowwise_fwd(fn, name, rows, consts, params, tm, nblk):
    n_rows = rows[0].shape[0]
    tm = min(tm, n_rows)
    ins = list(rows) + list(consts)
    avals = [jax.ShapeDtypeStruct((tm, a.shape[1] // nblk), a.dtype) for a in ins]
    avals += [jax.ShapeDtypeStruct(p.shape, p.dtype) for p in params]
    out_avals = jax.eval_shape(fn, *avals)
    n_in = len(ins) + len(params)

    def body(*refs):
        outs = fn(*[r[...] for r in refs[:n_in]])
        for r, o in zip(refs[n_in:], outs):
            r[...] = o.astype(r.dtype)

    return pl.pallas_call(
        body, grid=(n_rows // tm, nblk),
        in_specs=_rw_specs(ins, tm, nblk) + [pl.BlockSpec(p.shape, lambda i, j: (0, 0)) for p in params],
        out_specs=tuple(pl.BlockSpec((tm, o.shape[1]), lambda i, j: (i, j)) for o in out_avals),
        out_shape=tuple(jax.ShapeDtypeStruct((n_rows, o.shape[1] * nblk), o.dtype) for o in out_avals),
        name=name, compiler_params=_params(2),
    )(*ins, *params)


def _rowwise_bwd(fn, name, rows, consts, params, cts, tm, nblk):
    n_rows = rows[0].shape[0]
    tm = min(tm, n_rows)
    nr, nc, npar, nct = len(rows), len(consts), len(params), len(cts)

    def body(*refs):
        rv = [r[...] for r in refs[:nr]]
        cv = [r[...] for r in refs[nr:nr + nc]]
        pv = [r[...] for r in refs[nr + nc:nr + nc + npar]]
        ctv = [r[...] for r in refs[nr + nc + npar:nr + nc + npar + nct]]
        outs = refs[nr + nc + npar + nct:]
        _, vjp = jax.vjp(lambda *d: fn(*d[:nr], *cv, *d[nr:]), *rv, *pv)
        grads = vjp(tuple(ctv))
        for k in range(nr):
            outs[k][...] = grads[k]
        first = jnp.logical_and(pl.program_id(0) == 0, pl.program_id(1) == 0)
        for k in range(npar):
            ref = outs[nr + k]

            @pl.when(first)
            def _(ref=ref):
                ref[...] = jnp.zeros_like(ref)

            ref[...] += grads[nr + k]

    ins = list(rows) + list(consts)
    return pl.pallas_call(
        body, grid=(n_rows // tm, nblk),
        in_specs=(_rw_specs(ins, tm, nblk) + [pl.BlockSpec(p.shape, lambda i, j: (0, 0)) for p in params]
                  + _rw_specs(cts, tm, nblk)),
        out_specs=tuple(_rw_specs(rows, tm, nblk) + [pl.BlockSpec(p.shape, lambda i, j: (0, 0)) for p in params]),
        out_shape=tuple([jax.ShapeDtypeStruct(a.shape, F32) for a in rows]
                        + [jax.ShapeDtypeStruct(p.shape, F32) for p in params]),
        name=name, compiler_params=_params(2),
    )(*ins, *params, *cts)


def _merge_fn(ga, gb, pa, pb):
    return (jax.nn.sigmoid(ga) * pa + jax.nn.sigmoid(gb) * pb,)


def _outnorm_gate_fn(o, gate, gain):
    y = o * lax.rsqrt(jnp.mean(o * o, axis=-1, keepdims=True) + EPS) * gain
    return (y * (gate * jax.nn.sigmoid(gate)),)


def _beta_decay_fn(beta_raw, decay_raw, a_log, dt_bias):
    z = decay_raw + dt_bias
    softplus = jnp.maximum(z, 0.0) + jnp.log(1.0 + jnp.exp(-jnp.abs(z)))
    g = -jnp.exp(a_log) * softplus
    rows = g.shape[0]
    ii = lax.broadcasted_iota(jnp.int32, (rows, rows), 0)
    jj = lax.broadcasted_iota(jnp.int32, (rows, rows), 1)
    same_chunk_before = jnp.logical_and(jj <= ii, jj // GDN_CHUNK == ii // GDN_CHUNK).astype(F32)
    gcum = lax.dot_general(same_chunk_before, g, NN, precision=HI, preferred_element_type=F32)
    return jax.nn.sigmoid(beta_raw), gcum


def _combine_fn(o0, o1, o2, l0, l1, l2):
    m = lax.stop_gradient(jnp.maximum(jnp.maximum(l0, l1), l2))
    e0, e1, e2 = jnp.exp(l0 - m), jnp.exp(l1 - m), jnp.exp(l2 - m)
    return ((e0 * o0 + e1 * o1 + e2 * o2) / (e0 + e1 + e2),)


def _loss_fn(x, target, gain):
    y = x * lax.rsqrt(jnp.mean(x * x, axis=-1, keepdims=True) + EPS) * gain
    err = y - target
    return (0.5 * jnp.mean(err * err, axis=-1, keepdims=True),)


def _rope_call(x, cos, sin, name):
    n_rows, width = x.shape
    tm = 512

    def body(x_ref, c_ref, s_ref, o_ref):
        v = x_ref[...]
        lane = lax.broadcasted_iota(jnp.int32, v.shape, 1)
        low = (lane % DSW_HEAD_DIM) < DSW_HEAD_DIM // 2
        half = DSW_HEAD_DIM // 2
        swapped = jnp.where(low, pltpu.roll(v, LANES - half, 1), pltpu.roll(v, half, 1))
        o_ref[...] = v * c_ref[...] + swapped * s_ref[...]

    tab = pl.BlockSpec((tm, LANES), lambda i, j: (i, 0))
    blk = pl.BlockSpec((tm, LANES), lambda i, j: (i, j))
    return pl.pallas_call(
        body, grid=(n_rows // tm, width // LANES), in_specs=[blk, tab, tab], out_specs=blk,
        out_shape=jax.ShapeDtypeStruct(x.shape, F32), name=name, compiler_params=_params(2),
    )(x, cos, sin)


def _rope_tables(n_tokens):
    half = DSW_HEAD_DIM // 2
    inv_freq = ROPE_THETA ** (-jnp.arange(half, dtype=F32) / half)
    ang = jnp.arange(n_tokens, dtype=F32)[:, None] * inv_freq[None, :]
    cos, sin = jnp.cos(ang), jnp.sin(ang)
    return jnp.tile(jnp.concatenate([cos, cos], 1), (1, 2)), jnp.tile(jnp.concatenate([-sin, sin], 1), (1, 2))


def _attn_probs(q, kp, kc, group, n):
    blk = DSW_BLOCK
    k = _each(lambda a, b: jnp.concatenate([a, b], axis=0).astype(BF16), kp, kc)
    s = _each(lambda a, b: lax.dot_general(a.astype(BF16), b, NT, preferred_element_type=F32)
              * (DSW_HEAD_DIM ** -0.5), q, k)
    blocks_per_seq = jnp.where(group == 0, 16, jnp.where(group == 1, 4, 1))
    first = (n % blocks_per_seq) == 0
    qi = lax.broadcasted_iota(jnp.int32, (blk, 2 * blk), 0)
    kj = lax.broadcasted_iota(jnp.int32, (blk, 2 * blk), 1)
    dist = qi + blk - kj
    valid = (dist >= 0) & (dist <= blk) & jnp.logical_or(kj >= blk, jnp.logical_not(first))
    s = _each(lambda a: jnp.where(valid, a, -1e30), s)
    m = _each(lambda a: jnp.max(a, axis=-1, keepdims=True), s)
    p = _each(lambda a, b: jnp.exp(a - b), s, m)
    l = _each(lambda a: jnp.sum(a, axis=-1, keepdims=True), p)
    return _each(lambda a, b: a / b, p, l), _each(lambda a, b: a + jnp.log(b), m, l), k


def _attn_specs(n_tokens):
    blk, hpg = DSW_BLOCK, DSW_HEADS_PER_GROUP
    cur = pl.BlockSpec((hpg, blk, DSW_HEAD_DIM), lambda g, n: (g, n, 0))
    prev = pl.BlockSpec((hpg, blk, DSW_HEAD_DIM), lambda g, n: (g, jnp.maximum(n - 1, 0), 0))
    return cur, prev


def _attn_fwd(q, k, v):
    nh, n_tokens, hd = q.shape
    hpg = DSW_HEADS_PER_GROUP
    cur, prev = _attn_specs(n_tokens)

    def body(q_ref, kp_ref, kc_ref, vp_ref, vc_ref, o_ref, l_ref):
        heads = range(hpg)
        p, lse, _ = _attn_probs([q_ref[h] for h in heads], [kp_ref[h] for h in heads], [kc_ref[h] for h in heads],
                                pl.program_id(0), pl.program_id(1))
        vv = [jnp.concatenate([vp_ref[h], vc_ref[h]], axis=0).astype(BF16) for h in heads]
        o = _each(lambda a, b: lax.dot_general(a.astype(BF16), b, NN, preferred_element_type=F32), p, vv)
        for h in heads:
            o_ref[h] = o[h]
            l_ref[h] = jnp.broadcast_to(lse[h], (DSW_BLOCK, hd))

    return pl.pallas_call(
        body, grid=(nh // hpg, n_tokens // DSW_BLOCK), in_specs=[cur, prev, cur, prev, cur], out_specs=(cur, cur),
        out_shape=(jax.ShapeDtypeStruct(q.shape, F32), jax.ShapeDtypeStruct(q.shape, F32)),
        name="attn_fwd", compiler_params=_params(2),
    )(q, k, k, v, v)


def _attn_bwd(q, k, v, do, dlse):
    nh, n_tokens, hd = q.shape
    hpg = DSW_HEADS_PER_GROUP
    nblk = n_tokens // DSW_BLOCK
    cur, prev = _attn_specs(n_tokens)
    part = pl.BlockSpec((hpg, 1, 2 * DSW_BLOCK, hd), lambda g, n: (g, n, 0, 0))
    scale = DSW_HEAD_DIM ** -0.5

    def body(q_ref, kp_ref, kc_ref, vp_ref, vc_ref, do_ref, dl_ref, dq_ref, dk_ref, dv_ref):
        heads = range(hpg)
        qs = [q_ref[h] for h in heads]
        p, _, kb = _attn_probs(qs, [kp_ref[h] for h in heads], [kc_ref[h] for h in heads],
                               pl.program_id(0), pl.program_id(1))
        qb = _each(lambda a: a.astype(BF16), qs)
        vv = [jnp.concatenate([vp_ref[h], vc_ref[h]], axis=0).astype(BF16) for h in heads]
        dob = [do_ref[h].astype(BF16) for h in heads]
        dp = _each(lambda a, b: lax.dot_general(a, b, NT, preferred_element_type=F32), dob, vv)
        dv = _each(lambda a, b: lax.dot_general(a.astype(BF16), b, TN, preferred_element_type=F32), p, dob)
        dl = [jnp.sum(dl_ref[h], axis=-1, keepdims=True) for h in heads]
        ds = _each(lambda a, b, c: (a * (b - jnp.sum(b * a, axis=-1, keepdims=True) + c) * scale).astype(BF16),
                   p, dp, dl)
        dq = _each(lambda a, b: lax.dot_general(a, b, NN, preferred_element_type=F32), ds, kb)
        dk = _each(lambda a, b: lax.dot_general(a, b, TN, preferred_element_type=F32), ds, qb)
        for h in heads:
            dq_ref[h] = dq[h]
            dk_ref[h, 0] = dk[h]
            dv_ref[h, 0] = dv[h]

    dq, dkp, dvp = pl.pallas_call(
        body, grid=(nh // hpg, nblk), in_specs=[cur, prev, cur, prev, cur, cur, cur], out_specs=(cur, part, part),
        out_shape=(jax.ShapeDtypeStruct(q.shape, F32),
                   jax.ShapeDtypeStruct((nh, nblk, 2 * DSW_BLOCK, hd), F32),
                   jax.ShapeDtypeStruct((nh, nblk, 2 * DSW_BLOCK, hd), F32)),
        name="attn_bwd", compiler_params=_params(2),
    )(q, k, k, v, v, do, dlse)

    def fold(partial):
        own = partial[:, :, DSW_BLOCK:]
        from_next = jnp.pad(partial[:, 1:, :DSW_BLOCK], ((0, 0), (0, 1), (0, 0), (0, 0)))
        return (own + from_next).reshape(nh, n_tokens, hd)

    return dq, fold(dkp), fold(dvp)


def _to_heads(a):
    n_tokens = a.shape[0]
    outs = []
    for gi, d in enumerate(DSW_DILATIONS):
        blk = a[:, gi * 256:(gi + 1) * 256].reshape(n_tokens // d, d, DSW_HEADS_PER_GROUP, DSW_HEAD_DIM)
        outs.append(blk.transpose(2, 1, 0, 3).reshape(DSW_HEADS_PER_GROUP, n_tokens, DSW_HEAD_DIM))
    return jnp.concatenate(outs, 0)


def _from_heads(a):
    n_tokens = a.shape[1]
    outs = []
    for gi, d in enumerate(DSW_DILATIONS):
        blk = a[gi * 4:(gi + 1) * 4].reshape(DSW_HEADS_PER_GROUP, d, n_tokens // d, DSW_HEAD_DIM)
        outs.append(blk.transpose(2, 1, 0, 3).reshape(n_tokens, DSW_HEADS_PER_GROUP * DSW_HEAD_DIM))
    return outs


CONV_TILE = 512


def _shift_down(x, k, rows):
    return x if k == 0 else jnp.where(rows >= k, pltpu.roll(x, k, 0), 0.0)


def _shift_up(x, k, rows):
    n = x.shape[0]
    return x if k == 0 else jnp.where(rows < n - k, pltpu.roll(x, n - k, 0), 0.0)


def _conv_pre(x, w):
    rows = lax.broadcasted_iota(jnp.int32, x.shape, 0)
    acc = x * w[GDN_CONV - 1:GDN_CONV]
    for k in range(1, GDN_CONV):
        acc = acc + _shift_down(x, k, rows) * w[GDN_CONV - 1 - k:GDN_CONV - k]
    return acc, rows


def _conv_fwd(x, w):
    n_tokens, width = x.shape
    big = pl.BlockSpec((n_tokens, CONV_TILE), lambda j: (0, j))
    wsp = pl.BlockSpec((GDN_CONV, CONV_TILE), lambda j: (0, j))

    def body(x_ref, w_ref, o_ref):
        acc, _ = _conv_pre(x_ref[...], w_ref[...])
        o_ref[...] = acc * jax.nn.sigmoid(acc)

    return pl.pallas_call(
        body, grid=(width // CONV_TILE,), in_specs=[big, wsp], out_specs=big,
        out_shape=jax.ShapeDtypeStruct(x.shape, F32), name="conv_fwd", compiler_params=_params(1),
    )(x, w)


def _conv_bwd(x, w, dy):
    n_tokens, width = x.shape
    big = pl.BlockSpec((n_tokens, CONV_TILE), lambda j: (0, j))
    wsp = pl.BlockSpec((GDN_CONV, CONV_TILE), lambda j: (0, j))

    def body(x_ref, w_ref, dy_ref, dx_ref, dw_ref):
        xv, wv = x_ref[...], w_ref[...]
        acc, rows = _conv_pre(xv, wv)
        sg = jax.nn.sigmoid(acc)
        dacc = dy_ref[...] * (sg + acc * sg * (1.0 - sg))
        dx = dacc * wv[GDN_CONV - 1:GDN_CONV]
        for k in range(1, GDN_CONV):
            dx = dx + _shift_up(dacc, k, rows) * wv[GDN_CONV - 1 - k:GDN_CONV - k]
        dx_ref[...] = dx
        for k in range(GDN_CONV):
            dw_ref[GDN_CONV - 1 - k:GDN_CONV - k, :] = jnp.sum(dacc * _shift_down(xv, k, rows), axis=0, keepdims=True)

    return pl.pallas_call(
        body, grid=(width // CONV_TILE,), in_specs=[big, wsp, big], out_specs=(big, wsp),
        out_shape=(jax.ShapeDtypeStruct(x.shape, F32), jax.ShapeDtypeStruct(w.shape, F32)),
        name="conv_bwd", compiler_params=_params(1),
    )(x, w, dy)


def _dot(a, b, dn=NN):
    return lax.dot_general(a, b, dn, precision=HI, preferred_element_type=F32)


def _dot3(a, b, dn=NN):
    return lax.dot_general(a, b, dn, precision=lax.Precision.HIGH, preferred_element_type=F32)


def _bf16_dot(a, b, dn):
    return lax.dot_general(a.astype(BF16), b.astype(BF16), dn, preferred_element_type=F32)


_DOT_GRADS = {NN: (("g", "b", NT), ("a", "g", TN)), NT: (("g", "b", NN), ("g", "a", TN)),
              TN: (("b", "g", NT), ("a", "g", NN))}


def _make_bdot(dn):
    @jax.custom_vjp
    def op(a, b):
        return _bf16_dot(a, b, dn)

    def fwd(a, b):
        return op(a, b), (a, b)

    def bwd(saved, g):
        vals = dict(a=saved[0], b=saved[1], g=g)
        return tuple(_bf16_dot(vals[x], vals[y], form) for x, y, form in _DOT_GRADS[dn])

    op.defvjp(fwd, bwd)
    return op


_BDOTS = {dn: _make_bdot(dn) for dn in (NN, NT, TN)}


def _bdot(a, b, dn=NN):
    return _BDOTS[dn](a, b)


def _each(fn, *lists):
    return [fn(*items) for items in zip(*lists)]


def _gdn_chunks(q, k, v, b, gcum, state):
    c = GDN_CHUNK
    ii = lax.broadcasted_iota(jnp.int32, (c, c), 0)
    jj = lax.broadcasted_iota(jnp.int32, (c, c), 1)
    eye = (ii == jj).astype(F32)
    qn = _each(lambda x: x * lax.rsqrt(jnp.sum(x * x, axis=-1, keepdims=True) + EPS) * (GDN_HEAD_DIM ** -0.5), q)
    kn = _each(lambda x: x * lax.rsqrt(jnp.sum(x * x, axis=-1, keepdims=True) + EPS), k)
    gcum_i = _each(lambda x: jnp.broadcast_to(x, (c, c)), gcum)
    gcum_j = _each(jnp.transpose, gcum_i)
    decay = _each(lambda x, y: jnp.exp(jnp.where(jj <= ii, x - y, -1e30)), gcum_i, gcum_j)
    g_last = _each(lambda x: x[c - 1:c, :], gcum)
    e_gcum = _each(jnp.exp, gcum)
    kbeta = _each(lambda x, y: x * y, kn, b)
    vbeta = _each(lambda x, y: x * y, v, b)
    m = _each(lambda x, y, d: jnp.where(jj < ii, _bdot(x, y, NT) * d, 0.0), kbeta, kn, decay)
    inv = _each(lambda x: eye - x, m)
    power = _each(lambda x: _dot3(x, x), m)
    for step in range(5):
        inv = _each(lambda x, p: x + _dot3(x, p), inv, power)
        if step < 4:
            power = _each(lambda p: _dot3(p, p), power)
    u = _each(_dot3, inv, vbeta)
    w = _each(lambda x, y, e: _dot3(x, y * e), inv, kbeta, e_gcum)
    a_qk = _each(lambda x, y, d: _bdot(x, y, NT) * d, qn, kn, decay)
    v_new = _each(lambda x, y, s: x - _bdot(y, s), u, w, state)
    o = _each(lambda x, e, s, a, vn: _bdot(x * e, s) + _bdot(a, vn), qn, e_gcum, state, a_qk, v_new)
    new_state = _each(lambda s, gl, x, gc, vn: s * jnp.exp(gl) + _bdot(x * jnp.exp(gl - gc), vn, TN),
                      state, g_last, kn, gcum, v_new)
    return o, new_state


GDN_HEADS_PER_STEP = 8


GDN_TIME_TILE = 256


def _gdn_specs(n_tokens, reverse):
    hb, hd, tt = GDN_HEADS_PER_STEP, GDN_HEAD_DIM, GDN_TIME_TILE
    nb, nt = GDN_HEADS // hb, n_tokens // tt

    def when(t):
        return nt - 1 - t if reverse else t

    q = pl.BlockSpec((tt, hb * hd), lambda h, t: (when(t), h))
    k = pl.BlockSpec((tt, hb * hd), lambda h, t: (when(t), nb + h))
    v = pl.BlockSpec((tt, hb * hd), lambda h, t: (when(t), 2 * nb + h))
    vec = pl.BlockSpec((tt, hb), lambda h, t: (when(t), h))
    states = pl.BlockSpec((hb, tt // GDN_CHUNK, hd, hd), lambda h, t: (h, when(t), 0, 0))
    return q, k, v, vec, states


def _gdn_fwd(qkv, beta, g):
    n_tokens = qkv.shape[0]
    hb, hd, tt = GDN_HEADS_PER_STEP, GDN_HEAD_DIM, GDN_TIME_TILE
    n_chunks = tt // GDN_CHUNK
    q_s, k_s, v_s, vec, st = _gdn_specs(n_tokens, False)

    def body(q_ref, k_ref, v_ref, b_ref, g_ref, o_ref, st_ref, state):
        @pl.when(pl.program_id(1) == 0)
        def _():
            state[...] = jnp.zeros_like(state)

        def step(c, carry):
            r = pl.ds(pl.multiple_of(c * GDN_CHUNK, GDN_CHUNK), GDN_CHUNK)
            cols = [slice(h * hd, (h + 1) * hd) for h in range(hb)]
            old = [state[h] for h in range(hb)]
            o, new = _gdn_chunks(
                [q_ref[r, cs] for cs in cols], [k_ref[r, cs] for cs in cols], [v_ref[r, cs] for cs in cols],
                [b_ref[r, h:h + 1] for h in range(hb)], [g_ref[r, h:h + 1] for h in range(hb)], old)
            for h in range(hb):
                st_ref[h, c] = old[h]
                o_ref[r, cols[h]] = o[h]
                state[h] = new[h]
            return carry

        lax.fori_loop(0, n_chunks, step, 0)

    return pl.pallas_call(
        body, grid=(GDN_HEADS // hb, n_tokens // tt), in_specs=[q_s, k_s, v_s, vec, vec], out_specs=(q_s, st),
        out_shape=(jax.ShapeDtypeStruct((n_tokens, GDN_WIDTH), F32),
                   jax.ShapeDtypeStruct((GDN_HEADS, n_tokens // GDN_CHUNK, hd, hd), F32)),
        scratch_shapes=[pltpu.VMEM((hb, hd, hd), F32)],
        name="gdn_fwd", compiler_params=_params(2),
    )(qkv, qkv, qkv, beta, g)


def _gdn_bwd(qkv, beta, g, states, do):
    n_tokens = qkv.shape[0]
    hb, hd, tt = GDN_HEADS_PER_STEP, GDN_HEAD_DIM, GDN_TIME_TILE
    n_chunks = tt // GDN_CHUNK
    q_s, k_s, v_s, vec, st = _gdn_specs(n_tokens, True)

    def body(q_ref, k_ref, v_ref, b_ref, g_ref, st_ref, do_ref, dq_ref, dk_ref, dv_ref, db_ref, dg_ref, dstate):
        @pl.when(pl.program_id(1) == 0)
        def _():
            dstate[...] = jnp.zeros_like(dstate)

        def step(i, carry):
            c = n_chunks - 1 - i
            r = pl.ds(pl.multiple_of(c * GDN_CHUNK, GDN_CHUNK), GDN_CHUNK)
            cols = [slice(h * hd, (h + 1) * hd) for h in range(hb)]
            args = ([q_ref[r, cs] for cs in cols], [k_ref[r, cs] for cs in cols], [v_ref[r, cs] for cs in cols],
                    [b_ref[r, h:h + 1] for h in range(hb)], [g_ref[r, h:h + 1] for h in range(hb)],
                    [st_ref[h, c] for h in range(hb)])
            cts = ([do_ref[r, cs] for cs in cols], [dstate[h] for h in range(hb)])
            dq, dk, dv, db, dg, dst = jax.vjp(_gdn_chunks, *args)[1](cts)
            for h in range(hb):
                dq_ref[r, cols[h]] = dq[h]
                dk_ref[r, cols[h]] = dk[h]
                dv_ref[r, cols[h]] = dv[h]
                db_ref[r, h:h + 1] = db[h]
                dg_ref[r, h:h + 1] = dg[h]
                dstate[h] = dst[h]
            return carry

        lax.fori_loop(0, n_chunks, step, 0)

    wide = jax.ShapeDtypeStruct((n_tokens, GDN_WIDTH), F32)
    thin = jax.ShapeDtypeStruct(beta.shape, F32)
    dq, dk, dv, db, dg = pl.pallas_call(
        body, grid=(GDN_HEADS // hb, n_tokens // tt), in_specs=[q_s, k_s, v_s, vec, vec, st, q_s],
        out_specs=(q_s, q_s, q_s, vec, vec), out_shape=(wide, wide, wide, thin, thin),
        scratch_shapes=[pltpu.VMEM((hb, hd, hd), F32)],
        name="gdn_bwd", compiler_params=_params(2),
    )(qkv, qkv, qkv, beta, g, states, do)
    return jnp.concatenate([dq, dk, dv], axis=1), db, dg


FFN_ROW_TILE = 256


def _resident(shape):
    return pl.BlockSpec(shape, lambda i: (0,) * len(shape), pipeline_mode=pl.Buffered(1))


def _ffn_fwd(x, gain, wg, wu, wd, name):
    n_tokens, d = x.shape
    n_shards, n, _ = wg.shape
    tm = FFN_ROW_TILE

    def body(x_ref, gain_ref, wg_ref, wu_ref, wd_ref, o_ref, g_ref, u_ref):
        xv = x_ref[...]
        h = (xv * lax.rsqrt(jnp.mean(xv * xv, axis=-1, keepdims=True) + EPS) * gain_ref[...]).astype(BF16)
        acc = jnp.zeros((tm, d), F32)
        for j in range(n_shards):
            g = lax.dot_general(h, wg_ref[j], NT, preferred_element_type=F32)
            u = lax.dot_general(h, wu_ref[j], NT, preferred_element_type=F32)
            g_ref[j] = g
            u_ref[j] = u
            a = (g * jax.nn.sigmoid(g) * u).astype(BF16)
            acc = acc + lax.dot_general(a, wd_ref[j], NN, preferred_element_type=F32)
        o_ref[...] = xv + 0.5 * acc

    row = pl.BlockSpec((tm, d), lambda i: (i, 0))
    hid = pl.BlockSpec((n_shards, tm, n), lambda i: (0, i, 0))
    return pl.pallas_call(
        body, grid=(n_tokens // tm,),
        in_specs=[row, _resident(gain.shape), _resident(wg.shape), _resident(wu.shape), _resident(wd.shape)],
        out_specs=(row, hid, hid),
        out_shape=(jax.ShapeDtypeStruct(x.shape, F32), jax.ShapeDtypeStruct((n_shards, n_tokens, n), F32),
                   jax.ShapeDtypeStruct((n_shards, n_tokens, n), F32)),
        name=name, compiler_params=_params(1),
    )(x, gain, wg, wu, wd)


def _ffn_bwd_rows(x, gain, dy, g, u, wg, wu, wd, name):
    n_tokens, d = x.shape
    n_shards, n, _ = wg.shape
    tm = FFN_ROW_TILE

    def body(x_ref, gain_ref, dy_ref, g_ref, u_ref, wg_ref, wu_ref, wd_ref,
             dx_ref, dgain_ref, h_ref, dyh_ref, a_ref, dg_ref, du_ref):
        xv, dyv, gain_v = x_ref[...], dy_ref[...], gain_ref[...]
        r = lax.rsqrt(jnp.mean(xv * xv, axis=-1, keepdims=True) + EPS)
        xhat = xv * r
        h_ref[...] = (xhat * gain_v).astype(BF16)
        dyh = (0.5 * dyv).astype(BF16)
        dyh_ref[...] = dyh
        dh = jnp.zeros((tm, d), F32)
        for j in range(n_shards):
            da = lax.dot_general(dyh, wd_ref[j], NT, preferred_element_type=F32)
            gv, uv = g_ref[j], u_ref[j]
            sg = jax.nn.sigmoid(gv)
            silu = gv * sg
            a_ref[j] = (silu * uv).astype(BF16)
            dg = (da * uv * (sg + silu * (1.0 - sg))).astype(BF16)
            du = (da * silu).astype(BF16)
            dg_ref[j] = dg
            du_ref[j] = du
            dh = dh + lax.dot_general(dg, wg_ref[j], NN, preferred_element_type=F32)
            dh = dh + lax.dot_general(du, wu_ref[j], NN, preferred_element_type=F32)
        dxhat = dh * gain_v
        dx_ref[...] = dyv + r * (dxhat - xhat * jnp.mean(dxhat * xhat, axis=-1, keepdims=True))

        @pl.when(pl.program_id(0) == 0)
        def _():
            dgain_ref[...] = jnp.zeros_like(dgain_ref)

        dgain_ref[...] += jnp.sum(dh * xhat, axis=0, keepdims=True)

    row = pl.BlockSpec((tm, d), lambda i: (i, 0))
    hid = pl.BlockSpec((n_shards, tm, n), lambda i: (0, i, 0))
    hid_shape = (n_shards, n_tokens, n)
    return pl.pallas_call(
        body, grid=(n_tokens // tm,),
        in_specs=[row, _resident(gain.shape), row, hid, hid, _resident(wg.shape), _resident(wu.shape),
                  _resident(wd.shape)],
        out_specs=(row, pl.BlockSpec(gain.shape, lambda i: (0, 0)), row, row, hid, hid, hid),
        out_shape=(jax.ShapeDtypeStruct(x.shape, F32), jax.ShapeDtypeStruct(gain.shape, F32),
                   jax.ShapeDtypeStruct(x.shape, BF16), jax.ShapeDtypeStruct(x.shape, BF16),
                   jax.ShapeDtypeStruct(hid_shape, BF16), jax.ShapeDtypeStruct(hid_shape, BF16),
                   jax.ShapeDtypeStruct(hid_shape, BF16)),
        name=name, compiler_params=_params(1),
    )(x, gain, dy, g, u, wg, wu, wd)


def _ffn_bwd_weights(h, dyh, a, dg, du, name):
    n_shards, n_tokens, n = a.shape
    d = h.shape[1]

    def body(h_ref, dyh_ref, a_ref, dg_ref, du_ref, dwg_ref, dwu_ref, dwd_ref):
        hv = h_ref[...]
        dwg_ref[0] = lax.dot_general(dg_ref[0], hv, TN, preferred_element_type=F32)
        dwu_ref[0] = lax.dot_general(du_ref[0], hv, TN, preferred_element_type=F32)
        dwd_ref[0] = lax.dot_general(a_ref[0], dyh_ref[...], TN, preferred_element_type=F32)

    hid = pl.BlockSpec((1, n_tokens, n), lambda j: (j, 0, 0))
    out = pl.BlockSpec((1, n, d), lambda j: (j, 0, 0))
    return pl.pallas_call(
        body, grid=(n_shards,), in_specs=[_resident(h.shape), _resident(dyh.shape), hid, hid, hid],
        out_specs=(out, out, out), out_shape=(jax.ShapeDtypeStruct((n_shards, n, d), F32),) * 3,
        name=name, compiler_params=_params(1),
    )(h, dyh, a, dg, du)


IN_PIECES = (("wq_a", 0, 768), ("wk_a", 768, 1536), ("wv_a", 1536, 2304), ("w_qkvb", 2304, 5376),
             ("w_small", 5376, 5392), ("w_ggate", 5392, 6416), ("w_gatea", 6416, 7440), ("w_gateb", 7440, 8464))
IN_NAMES = tuple(name for name, _, _ in IN_PIECES)


def _in_rows(lo, hi):
    return lo, max(hi, lo + LANES)


def _in_proj_fwd(x, gain, wt):
    n_tokens, d = x.shape
    tm = FFN_ROW_TILE
    rows = [_in_rows(lo, hi) for _, lo, hi in IN_PIECES]

    def body(x_ref, gain_ref, wt_ref, *o_refs):
        xv = x_ref[...]
        h = (xv * lax.rsqrt(jnp.mean(xv * xv, axis=-1, keepdims=True) + EPS) * gain_ref[...]).astype(BF16)
        for (lo, hi), o_ref in zip(rows, o_refs):
            o_ref[...] = lax.dot_general(h, wt_ref[lo:hi, :], NT, preferred_element_type=F32)

    return pl.pallas_call(
        body, grid=(n_tokens // tm,),
        in_specs=[pl.BlockSpec((tm, d), lambda i: (i, 0)), _resident(gain.shape), _resident(wt.shape)],
        out_specs=tuple(pl.BlockSpec((tm, hi - lo), lambda i: (i, 0)) for lo, hi in rows),
        out_shape=tuple(jax.ShapeDtypeStruct((n_tokens, hi - lo), F32) for lo, hi in rows),
        name="in_proj_fwd", compiler_params=_params(1),
    )(x, gain, wt)


def _in_proj_bwd_rows(x, gain, dres, dzs, wt):
    n_tokens, d = x.shape
    tm = FFN_ROW_TILE
    n = len(dzs)
    rows = [_in_rows(lo, hi) for _, lo, hi in IN_PIECES]

    def body(x_ref, gain_ref, dres_ref, *refs):
        dz_refs, wt_ref = refs[:n], refs[n]
        dx_ref, dgain_ref, h_ref = refs[n + 1:]
        xv, gain_v = x_ref[...], gain_ref[...]
        r = lax.rsqrt(jnp.mean(xv * xv, axis=-1, keepdims=True) + EPS)
        xhat = xv * r
        h_ref[...] = (xhat * gain_v).astype(BF16)
        dh = jnp.zeros((tm, d), F32)
        for dz_ref, (lo, hi) in zip(dz_refs, rows):
            dh = dh + lax.dot_general(dz_ref[...].astype(BF16), wt_ref[lo:hi, :], NN, preferred_element_type=F32)
        dxhat = dh * gain_v
        dx_ref[...] = dres_ref[...] + r * (dxhat - xhat * jnp.mean(dxhat * xhat, axis=-1, keepdims=True))

        @pl.when(pl.program_id(0) == 0)
        def _():
            dgain_ref[...] = jnp.zeros_like(dgain_ref)

        dgain_ref[...] += jnp.sum(dh * xhat, axis=0, keepdims=True)

    row = pl.BlockSpec((tm, d), lambda i: (i, 0))
    return pl.pallas_call(
        body, grid=(n_tokens // tm,),
        in_specs=([row, _resident(gain.shape), row]
                  + [pl.BlockSpec((tm, dz.shape[1]), lambda i: (i, 0)) for dz in dzs] + [_resident(wt.shape)]),
        out_specs=(row, pl.BlockSpec(gain.shape, lambda i: (0, 0)), row),
        out_shape=(jax.ShapeDtypeStruct(x.shape, F32), jax.ShapeDtypeStruct(gain.shape, F32),
                   jax.ShapeDtypeStruct(x.shape, BF16)),
        name="in_proj_bwd_rows", compiler_params=_params(1),
    )(x, gain, dres, *dzs, wt)


def _in_proj_bwd_weight(dwt, h, dz, lo, hi, name):
    n_tokens, d = h.shape
    width = hi - lo
    tn = _tile(width, 512) if width >= LANES else width
    dz_tile = max(tn, LANES)

    def body(dwt_ref, h_ref, dz_ref, o_ref):
        o_ref[...] = lax.dot_general(dz_ref[:, :tn].astype(BF16), h_ref[...], TN, preferred_element_type=F32)

    return pl.pallas_call(
        body, grid=(width // tn,),
        in_specs=[ANY, _resident(h.shape), pl.BlockSpec((n_tokens, dz_tile), lambda j: (0, j))],
        out_specs=pl.BlockSpec((pl.Element(tn), pl.Element(d)), lambda j: (pl.multiple_of(lo + j * tn, 16), 0)),
        out_shape=jax.ShapeDtypeStruct(dwt.shape, F32), input_output_aliases={0: 0}, name=name,
        compiler_params=_params(1),
    )(dwt, h, dz)


def _split_small(z):
    return z[:, :GDN_HEADS], z[:, GDN_HEADS:2 * GDN_HEADS]


def _heads3(q, k, v):
    return _to_heads(q), _to_heads(k), _to_heads(v)


def _tokens6(o, lse):
    return tuple(_from_heads(o)) + tuple(_from_heads(lse))


def mixer_forward(x1, w, small):
    n_tokens = x1.shape[0]
    proj = dict(zip(IN_NAMES, _in_proj_fwd(x1, small["mix_norm"], w["w_in_t"])))
    cos, sin = _rope_tables(n_tokens)
    q_rot = _rope_call(proj["wq_a"], cos, sin, "rope_q")
    k_rot = _rope_call(proj["wk_a"], cos, sin, "rope_k")
    (qh, kh, vh), heads_vjp = jax.vjp(_heads3, q_rot, k_rot, proj["wv_a"])
    o, lse = _attn_fwd(qh, kh, vh)
    per_group, tokens_vjp = jax.vjp(_tokens6, o, lse)
    ya = _rowwise_fwd(_combine_fn, "combine", per_group, (), (), 512, 1)[0]
    pa = _matmul(ya, w["w_branch_a"], name="branch_a")
    qkv = _conv_fwd(proj["w_qkvb"], small["gdn_conv_w"])
    raw, small_vjp = jax.vjp(_split_small, proj["w_small"])
    gdn_params = (small["gdn_a_log"], small["gdn_dt_bias"])
    beta, gcum = _rowwise_fwd(_beta_decay_fn, "beta_decay", raw, (), gdn_params, 512, 1)
    ob, states = _gdn_fwd(qkv, beta, gcum)
    gate_in = (ob, proj["w_ggate"])
    yb = _rowwise_fwd(_outnorm_gate_fn, "outnorm_gate", gate_in, (), (small["gdn_out_norm"],), 512, GDN_HEADS)[0]
    pb = _matmul(yb, w["w_branch_b"], name="branch_b")
    merge_in = (proj["w_gatea"], proj["w_gateb"], pa, pb)
    merged = _rowwise_fwd(_merge_fn, "merge", merge_in, (), (), 256, 1)[0]
    x2 = _matmul(merged, w["w_out"], name="out", res=x1)
    saved = dict(x1=x1, proj=proj, cos=cos, sin=sin, heads_vjp=heads_vjp, heads=(qh, kh, vh), tokens_vjp=tokens_vjp,
                 per_group=per_group, ya=ya, qkv=qkv, raw=raw, small_vjp=small_vjp, beta=beta, gcum=gcum, states=states,
                 gate_in=gate_in, yb=yb, merge_in=merge_in, merged=merged)
    return x2, saved


def mixer_backward(dx2, s, w, small):
    proj = s["proj"]
    dmerged = _matmul(dx2, w["w_out"], name="out_da", tb=True)
    grads = dict(w_out=_matmul(s["merged"], dx2, name="out_dw", ta=True))
    dgate_a, dgate_b, dpa, dpb = _rowwise_bwd(_merge_fn, "merge_bwd", s["merge_in"], (), (), (dmerged,), 256, 1)
    dyb = _matmul(dpb, w["w_branch_b"], name="branch_b_da", tb=True)
    grads["w_branch_b"] = _matmul(s["yb"], dpb, name="branch_b_dw", ta=True)
    dya = _matmul(dpa, w["w_branch_a"], name="branch_a_da", tb=True)
    grads["w_branch_a"] = _matmul(s["ya"], dpa, name="branch_a_dw", ta=True)
    dob, dggate, grads["gdn_out_norm"] = _rowwise_bwd(
        _outnorm_gate_fn, "outnorm_gate_bwd", s["gate_in"], (), (small["gdn_out_norm"],), (dyb,), 512, GDN_HEADS)
    dqkv, dbeta, dgcum = _gdn_bwd(s["qkv"], s["beta"], s["gcum"], s["states"], dob)
    gdn_params = (small["gdn_a_log"], small["gdn_dt_bias"])
    dbeta_raw, ddecay_raw, grads["gdn_a_log"], grads["gdn_dt_bias"] = _rowwise_bwd(
        _beta_decay_fn, "beta_decay_bwd", s["raw"], (), gdn_params, (dbeta, dgcum), 512, 1)
    dsmall = s["small_vjp"]((dbeta_raw, ddecay_raw))[0]
    dqkvb, grads["gdn_conv_w"] = _conv_bwd(proj["w_qkvb"], small["gdn_conv_w"], dqkv)
    dper_group = _rowwise_bwd(_combine_fn, "combine_bwd", s["per_group"], (), (), (dya,), 512, 1)
    do, dlse = s["tokens_vjp"](tuple(dper_group))
    dqh, dkh, dvh = _attn_bwd(*s["heads"], do, dlse)
    dq_rot, dk_rot, dv = s["heads_vjp"]((dqh, dkh, dvh))
    dq = _rope_call(dq_rot, s["cos"], -s["sin"], "rope_q_bwd")
    dk = _rope_call(dk_rot, s["cos"], -s["sin"], "rope_k_bwd")
    dzs = (dq, dk, dv, dqkvb, dsmall, dggate, dgate_a, dgate_b)
    dx1, grads["mix_norm"], h = _in_proj_bwd_rows(s["x1"], small["mix_norm"], dx2, dzs, w["w_in_t"])
    dwt = lax.empty(w["w_in_t"].shape, F32)
    for (name, lo, hi), dz in zip(IN_PIECES, dzs):
        dwt = _in_proj_bwd_weight(dwt, h, dz, lo, hi, "in_proj_dw_" + name)
    grads["w_in_t"] = dwt
    return dx1, grads


def ffn_forward(x, gain, w, tag):
    out, g, u = _ffn_fwd(x, gain, w[tag + "_w_gate"], w[tag + "_w_up"], w[tag + "_w_down"], tag + "_fwd")
    return out, (x, g, u)


def ffn_backward(dy, saved, gain, w, tag):
    x, g, u = saved
    weights = (w[tag + "_w_gate"], w[tag + "_w_up"], w[tag + "_w_down"])
    dx, dgain, h, dyh, a, dg, du = _ffn_bwd_rows(x, gain, dy, g, u, *weights, tag + "_bwd_rows")
    return dx, dgain, _ffn_bwd_weights(h, dyh, a, dg, du, tag + "_bwd_weights")


def loss_head(x3, target, gain):
    row_loss = _rowwise_fwd(_loss_fn, "loss", (x3,), (target,), (gain,), 256, 1)[0]
    dx3, dgain = _rowwise_bwd(_loss_fn, "loss_bwd", (x3,), (target,), (gain,), (jnp.ones_like(row_loss),), 256, 1)
    return jnp.sum(row_loss), dx3, dgain


BIG_WEIGHTS = ("ffn1_w_gate", "ffn1_w_up", "ffn1_w_down", "w_in", "w_branch_a", "w_branch_b", "w_out",
               "ffn2_w_gate", "ffn2_w_up", "ffn2_w_down")
TRANSPOSED = ("ffn1_w_gate", "ffn1_w_up", "w_in", "ffn2_w_gate", "ffn2_w_up")
CONV_SHARD = (GDN_CONV, 3 * GDN_WIDTH // N_DEV)
SMALL_ROWS = 24
ANY = pl.BlockSpec(memory_space=pl.ANY)


TOKEN = jax.ShapeDtypeStruct((8, LANES), F32)


def _after(value, token):
    return value + token[0, 0].astype(value.dtype)


def _position():
    return lax.axis_index("x"), lax.axis_index("y"), lax.axis_index("c")


def all_gather_shards(shards, name):
    n = len(shards)

    def body(*refs):
        x_refs, out_refs = refs[:n], refs[n:2 * n]
        send_sems, recv_sems, local_sems = refs[2 * n + 1:]
        x, y, c = _position()
        me, sibling = (x, y, c), (x, y, 1 - c)
        chips = [(1 - x, y), (x, 1 - y), (1 - x, 1 - y)]

        def slab(a, px, py, pc):
            return out_refs[a].at[4 * px + 2 * py + pc]

        def copy(a, k, block, to, src=None):
            return pltpu.make_async_remote_copy(
                src_ref=slab(a, *block) if src is None else src, dst_ref=slab(a, *block),
                send_sem=send_sems.at[7 * a + k], recv_sem=recv_sems.at[7 * a + k], device_id=to, device_id_type=MESH)

        mine = [pltpu.make_async_copy(x_refs[a], slab(a, *me), local_sems.at[a]) for a in range(n)]
        for cp in mine:
            cp.start()
        first = []
        for j, chip in enumerate(chips):
            first += [copy(a, 1 + j, me, (*chip, c), src=x_refs[a]) for a in range(n)]
        first += [copy(a, 0, me, sibling, src=x_refs[a]) for a in range(n)]
        for cp in first:
            cp.start()
        passed = []
        for j, chip in enumerate(chips):
            for a in range(n):
                copy(a, 1 + j, (*chip, c), me).wait_recv()
                cp = copy(a, 4 + j, (*chip, c), sibling)
                cp.start()
                passed.append(cp)
        for a in range(n):
            copy(a, 0, sibling, me).wait_recv()
        for j, chip in enumerate(chips):
            for a in range(n):
                copy(a, 4 + j, (*chip, 1 - c), me).wait_recv()
        for cp in first + passed:
            cp.wait_send()
        for cp in mine:
            cp.wait()
        refs[2 * n][...] = jnp.zeros_like(refs[2 * n])

    outs = pl.pallas_call(
        body, out_shape=tuple(jax.ShapeDtypeStruct((N_DEV,) + s.shape, s.dtype) for s in shards) + (TOKEN,),
        in_specs=[ANY] * n, out_specs=(ANY,) * n + (pl.BlockSpec(memory_space=pltpu.VMEM),),
        scratch_shapes=[pltpu.SemaphoreType.DMA((7 * n,)), pltpu.SemaphoreType.DMA((7 * n,)),
                        pltpu.SemaphoreType.DMA((n,))],
        name=name,
    )(*shards)
    return outs[:n], outs[n]


def exchange_with_sibling(grads):
    n = len(grads)

    def body(*refs):
        g_refs, recv_refs = refs[:n], refs[n:2 * n]
        send_sems, recv_sems = refs[2 * n:]
        x, y, c = _position()
        copies = [pltpu.make_async_remote_copy(
            src_ref=g_refs[a].at[2 * k + 1 - c], dst_ref=recv_refs[a].at[k], send_sem=send_sems.at[4 * a + k],
            recv_sem=recv_sems.at[4 * a + k], device_id=(x, y, 1 - c), device_id_type=MESH)
            for k in range(4) for a in range(n)]
        for cp in copies:
            cp.start()
        for cp in copies:
            cp.wait()

    return pl.pallas_call(
        body, out_shape=tuple(jax.ShapeDtypeStruct((4,) + g.shape[1:], g.dtype) for g in grads),
        in_specs=[ANY] * n, out_specs=(ANY,) * n,
        scratch_shapes=[pltpu.SemaphoreType.DMA((4 * n,)), pltpu.SemaphoreType.DMA((4 * n,))], name="rs_sibling",
    )(*grads)


ELEMENTWISE_TILE_BYTES = 1536 * 1024


def _tile2(rows, cols):
    if rows % 256 == 0:
        return 256, cols
    if rows * cols * 4 > ELEMENTWISE_TILE_BYTES and cols % 256 == 0:
        return rows, 256
    return rows, cols


def add_sibling(grads, received, core, name):
    _, rows, width = grads.shape
    tr, tc = _tile2(rows, width)

    def body(c_ref, g_ref, r_ref, o_ref):
        o_ref[...] = (g_ref[...] + r_ref[...]).astype(BF16)

    blk = (1, tr, tc)
    return pl.pallas_call(
        body,
        grid_spec=pltpu.PrefetchScalarGridSpec(
            num_scalar_prefetch=1, grid=(4, rows // tr, width // tc),
            in_specs=[pl.BlockSpec(blk, lambda k, i, j, c_ref: (2 * k + c_ref[0], i, j)),
                      pl.BlockSpec(blk, lambda k, i, j, c_ref: (k, i, j))],
            out_specs=pl.BlockSpec(blk, lambda k, i, j, c_ref: (k, i, j))),
        out_shape=jax.ShapeDtypeStruct((4, rows, width), BF16), name=name, compiler_params=_params(3),
    )(core, grads, received)


def exchange_with_chips(partials):
    n = len(partials)

    def body(*refs):
        p_refs, recv_refs = refs[:n], refs[n:2 * n]
        send_sems, recv_sems = refs[2 * n:]
        x, y, c = _position()
        chips = [(1 - x, y), (x, 1 - y), (1 - x, 1 - y)]
        copies = [pltpu.make_async_remote_copy(
            src_ref=p_refs[a].at[2 * cx + cy], dst_ref=recv_refs[a].at[j], send_sem=send_sems.at[3 * a + j],
            recv_sem=recv_sems.at[3 * a + j], device_id=(cx, cy, c), device_id_type=MESH)
            for a in range(n) for j, (cx, cy) in enumerate(chips)]
        for cp in copies:
            cp.start()
        for cp in copies:
            cp.wait()

    return pl.pallas_call(
        body, out_shape=tuple(jax.ShapeDtypeStruct((3,) + p.shape[1:], p.dtype) for p in partials),
        in_specs=[ANY] * n, out_specs=(ANY,) * n,
        scratch_shapes=[pltpu.SemaphoreType.DMA((3 * n,)), pltpu.SemaphoreType.DMA((3 * n,))], name="rs_chips",
    )(*partials)


HBM = pl.BlockSpec(memory_space=pltpu.HBM)
SEM = pl.BlockSpec(memory_space=pltpu.SEMAPHORE)
DATAFLOW_EFFECT = pltpu.SideEffectType.DATAFLOW_SIDE_EFFECTING
N_PEERS = N_DEV - 1


def _peer(mask):
    x, y, c = _position()
    px = 1 - x if mask & 4 else x
    py = 1 - y if mask & 2 else y
    pc = 1 - c if mask & 1 else c
    return (px, py, pc), 4 * px + 2 * py + pc


def _direct_copies(src_refs, land_refs, send_sems, recv_sems, scatter):
    x, y, c = _position()
    me = 4 * x + 2 * y + c
    copies = []
    for a, (src, land) in enumerate(zip(src_refs, land_refs)):
        for mask in range(1, N_DEV):
            peer, peer_index = _peer(mask)
            k = N_PEERS * a + mask - 1
            copies.append(pltpu.make_async_remote_copy(
                src_ref=src.at[peer_index] if scatter else src,
                dst_ref=land.at[mask - 1] if scatter else land.at[me],
                send_sem=send_sems.at[k], recv_sem=recv_sems.at[k], device_id=peer, device_id_type=MESH))
    return copies


def direct_exchange_start(arrays, scatter, name):
    n = len(arrays)
    slabs = N_PEERS if scatter else N_DEV
    lands = [lax.empty((slabs,) + (a.shape[1:] if scatter else a.shape), a.dtype) for a in arrays]

    def body(*refs):
        src_refs, land_refs = refs[:n], refs[n:2 * n]
        send_sems, recv_sems = refs[2 * n], refs[2 * n + 1]
        token = refs[-1]
        for cp in _direct_copies(src_refs, land_refs, send_sems, recv_sems, scatter):
            cp.start()
        token[...] = jnp.zeros_like(token)

    sems = pltpu.SemaphoreType.DMA((N_PEERS * n,))
    outs = pl.pallas_call(
        body, name=name,
        out_shape=(sems, sems) + tuple(pltpu.HBM(a.shape, a.dtype) for a in arrays)
        + tuple(pltpu.HBM(l.shape, l.dtype) for l in lands) + (TOKEN,),
        in_specs=[HBM] * (2 * n), out_specs=(SEM, SEM) + (HBM,) * (2 * n) + (pl.BlockSpec(memory_space=pltpu.VMEM),),
        input_output_aliases={i: 2 + i for i in range(2 * n)},
        compiler_params=pltpu.CompilerParams(has_side_effects=DATAFLOW_EFFECT),
    )(*[pltpu.with_memory_space_constraint(a, pltpu.HBM) for a in list(arrays) + lands])
    return outs[0], outs[1], outs[2:2 + n], outs[2 + n:2 + 2 * n], outs[-1]


def direct_exchange_wait(send_sems, recv_sems, arrays, lands, after, scatter, name):
    n = len(arrays)

    def body(*refs):
        src_refs, land_refs = refs[:n], refs[n:2 * n]
        send_sems, recv_sems = refs[2 * n], refs[2 * n + 1]
        for cp in _direct_copies(src_refs, land_refs, send_sems, recv_sems, scatter):
            cp.wait_send()
            cp.wait_recv()

    outs = pl.pallas_call(
        body, name=name,
        out_shape=tuple(pltpu.HBM(a.shape, a.dtype) for a in arrays) + tuple(pltpu.HBM(l.shape, l.dtype) for l in lands),
        in_specs=[HBM] * (2 * n) + [SEM, SEM, pl.BlockSpec(memory_space=pl.ANY)], out_specs=(HBM,) * (2 * n),
        input_output_aliases={i: i for i in range(2 * n)},
        compiler_params=pltpu.CompilerParams(has_side_effects=DATAFLOW_EFFECT),
    )(*arrays, *lands, send_sems, recv_sems, after)
    return outs[n:]


def adamw_direct(w, m, v, grads, received, me, name):
    rows, cols = w.shape[-2:]
    tr, tc = _tile2(rows, cols)

    def body(me_ref, w_ref, m_ref, v_ref, own_ref, r_ref, g_ref, d_ref, nm_ref, nv_ref):
        gv = own_ref[0]
        for j in range(N_PEERS):
            gv = gv + r_ref[j].astype(F32)
        nm = ADAM_B1 * m_ref[0] + (1.0 - ADAM_B1) * gv
        nv = ADAM_B2 * v_ref[0] + (1.0 - ADAM_B2) * (gv * gv)
        m_hat = nm / (1.0 - ADAM_B1 ** ADAM_STEP)
        v_hat = nv / (1.0 - ADAM_B2 ** ADAM_STEP)
        g_ref[0] = gv
        d_ref[0] = -ADAM_LR * (m_hat / (jnp.sqrt(v_hat) + ADAM_EPS) + ADAM_WD * w_ref[0])
        nm_ref[0] = nm
        nv_ref[0] = nv

    one = pl.BlockSpec((1, tr, tc), lambda i, j, me_ref: (0, i, j))
    out = jax.ShapeDtypeStruct((1, rows, cols), F32)
    return pl.pallas_call(
        body,
        grid_spec=pltpu.PrefetchScalarGridSpec(
            num_scalar_prefetch=1, grid=(rows // tr, cols // tc),
            in_specs=[one, one, one, pl.BlockSpec((1, tr, tc), lambda i, j, me_ref: (me_ref[0], i, j)),
                      pl.BlockSpec((N_PEERS, tr, tc), lambda i, j, me_ref: (0, i, j))],
            out_specs=(one,) * 4),
        out_shape=(out,) * 4, name=name, compiler_params=_params(2),
    )(me, w, m, v, grads, received)


def all_reduce_small(vals):
    rows, width = vals.shape

    def body(x_ref, out_ref, all_ref, send_sems, recv_sems):
        x, y, c = _position()
        me, sibling = (x, y, c), (x, y, 1 - c)
        chips = [(1 - x, y), (x, 1 - y), (1 - x, 1 - y)]

        def slab(px, py, pc):
            return all_ref.at[4 * px + 2 * py + pc]

        def copy(k, block, to, src=None):
            return pltpu.make_async_remote_copy(
                src_ref=slab(*block) if src is None else src, dst_ref=slab(*block),
                send_sem=send_sems.at[k], recv_sem=recv_sems.at[k], device_id=to, device_id_type=MESH)

        first = [copy(0, me, sibling, src=x_ref)]
        first += [copy(1 + j, me, (*chip, c), src=x_ref) for j, chip in enumerate(chips)]
        for cp in first:
            cp.start()
        all_ref[4 * x + 2 * y + c] = x_ref[...]
        passed = [copy(4 + j, (*chip, c), sibling) for j, chip in enumerate(chips)]
        for j, chip in enumerate(chips):
            copy(1 + j, (*chip, c), me).wait_recv()
            passed[j].start()
        copy(0, sibling, me).wait_recv()
        for j, chip in enumerate(chips):
            copy(4 + j, (*chip, 1 - c), me).wait_recv()
        for cp in first + passed:
            cp.wait_send()
        total = all_ref[0]
        for d in range(1, N_DEV):
            total = total + all_ref[d]
        out_ref[...] = total

    vmem = pl.BlockSpec(memory_space=pltpu.VMEM)
    return pl.pallas_call(
        body, out_shape=(jax.ShapeDtypeStruct(vals.shape, F32), jax.ShapeDtypeStruct((N_DEV, rows, width), F32)),
        in_specs=[vmem], out_specs=(vmem, vmem),
        scratch_shapes=[pltpu.SemaphoreType.DMA((7,)), pltpu.SemaphoreType.DMA((7,))], name="small_allreduce",
    )(vals)[0]


def adamw(w, g, m, v, name):
    shape = w.shape
    w2, g2, m2, v2 = [a.reshape((-1, shape[-1])) for a in (w, g, m, v)]
    rows, cols = w2.shape
    tr = 256 if rows % 256 == 0 else rows

    def body(w_ref, g_ref, m_ref, v_ref, d_ref, nm_ref, nv_ref):
        gv = g_ref[...]
        nm = ADAM_B1 * m_ref[...] + (1.0 - ADAM_B1) * gv
        nv = ADAM_B2 * v_ref[...] + (1.0 - ADAM_B2) * (gv * gv)
        m_hat = nm / (1.0 - ADAM_B1 ** ADAM_STEP)
        v_hat = nv / (1.0 - ADAM_B2 ** ADAM_STEP)
        d_ref[...] = -ADAM_LR * (m_hat / (jnp.sqrt(v_hat) + ADAM_EPS) + ADAM_WD * w_ref[...])
        nm_ref[...] = nm
        nv_ref[...] = nv

    blk = pl.BlockSpec((tr, cols), lambda i: (i, 0))
    out = jax.ShapeDtypeStruct((rows, cols), F32)
    outs = pl.pallas_call(
        body, grid=(rows // tr,), in_specs=[blk] * 4, out_specs=(blk,) * 3, out_shape=(out,) * 3,
        name=name, compiler_params=_params(1),
    )(w2, g2, m2, v2)
    return tuple(o.reshape(shape) for o in outs)


def adamw_summed(w, m, v, grads, from_sibling, received, me, name):
    rows, cols = w.shape[-2:]
    tr, tc = _tile2(rows, cols)

    def body(me_ref, w_ref, m_ref, v_ref, own_ref, sib_ref, r_ref, g_ref, d_ref, nm_ref, nv_ref):
        gv = own_ref[0] + sib_ref[0]
        for j in range(3):
            gv = gv + r_ref[j].astype(F32)
        nm = ADAM_B1 * m_ref[0] + (1.0 - ADAM_B1) * gv
        nv = ADAM_B2 * v_ref[0] + (1.0 - ADAM_B2) * (gv * gv)
        m_hat = nm / (1.0 - ADAM_B1 ** ADAM_STEP)
        v_hat = nv / (1.0 - ADAM_B2 ** ADAM_STEP)
        g_ref[0] = gv
        d_ref[0] = -ADAM_LR * (m_hat / (jnp.sqrt(v_hat) + ADAM_EPS) + ADAM_WD * w_ref[0])
        nm_ref[0] = nm
        nv_ref[0] = nv

    one = pl.BlockSpec((1, tr, tc), lambda i, j, me_ref: (0, i, j))
    out = jax.ShapeDtypeStruct((1, rows, cols), F32)
    return pl.pallas_call(
        body,
        grid_spec=pltpu.PrefetchScalarGridSpec(
            num_scalar_prefetch=1, grid=(rows // tr, cols // tc),
            in_specs=[one, one, one, pl.BlockSpec((1, tr, tc), lambda i, j, me_ref: (me_ref[0], i, j)),
                      pl.BlockSpec((1, tr, tc), lambda i, j, me_ref: (me_ref[1], i, j)),
                      pl.BlockSpec((3, tr, tc), lambda i, j, me_ref: (0, i, j))],
            out_specs=(one,) * 4),
        out_shape=(out,) * 4, name=name, compiler_params=_params(2),
    )(me, w, m, v, grads, from_sibling, received)


SMALL_VECTORS = ("ffn1_norm", "mix_norm", "ffn2_norm", "final_norm")


def _pack_small(gs):
    row = jnp.concatenate([gs["gdn_a_log"].reshape(-1), gs["gdn_dt_bias"].reshape(-1), gs["gdn_out_norm"].reshape(-1)])
    rows = [gs[n].reshape(1, D_MODEL) for n in SMALL_VECTORS]
    rows.append(jnp.pad(row, (0, D_MODEL - row.shape[0])).reshape(1, D_MODEL))
    rows.append(gs["gdn_conv_w"].reshape(-1, D_MODEL))
    packed = jnp.concatenate(rows, axis=0)
    return jnp.pad(packed, ((0, SMALL_ROWS - packed.shape[0]), (0, 0)))


def _unpack_small(packed):
    out = {n: packed[i].reshape(1, D_MODEL) for i, n in enumerate(SMALL_VECTORS)}
    row = packed[len(SMALL_VECTORS)]
    out["gdn_a_log"] = row[:GDN_HEADS].reshape(1, GDN_HEADS)
    out["gdn_dt_bias"] = row[GDN_HEADS:2 * GDN_HEADS].reshape(1, GDN_HEADS)
    out["gdn_out_norm"] = row[2 * GDN_HEADS:2 * GDN_HEADS + GDN_HEAD_DIM].reshape(1, GDN_HEAD_DIM)
    first = len(SMALL_VECTORS) + 1
    out["gdn_conv_w"] = packed[first:first + GDN_CONV * 3].reshape(GDN_CONV, 3 * GDN_WIDTH)
    return out


WEIGHTS = ("ffn1_norm", "ffn1_w_gate", "ffn1_w_up", "ffn1_w_down", "mix_norm", "w_in", "gdn_conv_w", "gdn_a_log",
           "gdn_dt_bias", "gdn_out_norm", "w_branch_a", "w_branch_b", "w_out", "ffn2_norm", "ffn2_w_gate",
           "ffn2_w_up", "ffn2_w_down", "final_norm")


def kernel(x, ffn1_norm, ffn1_w_gate, ffn1_w_up, ffn1_w_down, mix_norm, w_in, gdn_conv_w, gdn_a_log, gdn_dt_bias, gdn_out_norm, w_branch_a, w_branch_b, w_out, ffn2_norm, ffn2_w_gate, ffn2_w_up, ffn2_w_down, final_norm, loss_target, m_ffn1_norm, m_ffn1_w_gate, m_ffn1_w_up, m_ffn1_w_down, m_mix_norm, m_w_in, m_gdn_conv_w, m_gdn_a_log, m_gdn_dt_bias, m_gdn_out_norm, m_w_branch_a, m_w_branch_b, m_w_out, m_ffn2_norm, m_ffn2_w_gate, m_ffn2_w_up, m_ffn2_w_down, m_final_norm, v_ffn1_norm, v_ffn1_w_gate, v_ffn1_w_up, v_ffn1_w_down, v_mix_norm, v_w_in, v_gdn_conv_w, v_gdn_a_log, v_gdn_dt_bias, v_gdn_out_norm, v_w_branch_a, v_w_branch_b, v_w_out, v_ffn2_norm, v_ffn2_w_gate, v_ffn2_w_up, v_ffn2_w_down, v_final_norm):
    given = dict(locals())
    px, py, pc = _position()
    big_names = list(BIG_WEIGHTS)

    def shard_view(a, n):
        return a.transpose(0, 2, 1) if n in TRANSPOSED else a

    me = 4 * px + 2 * py + pc
    me_index = me.astype(jnp.int32).reshape(1)
    late = [n for n in big_names if n.startswith("ffn2")]
    early = [n for n in big_names if n not in late]
    shards = {n: shard_view(given[n], n)[0].astype(BF16) for n in big_names}
    early_slabs, early_done = all_gather_shards([shards[n] for n in early] + [gdn_conv_w[0]], "gather_weights")
    gathered = dict(zip(early + ["gdn_conv_w"], early_slabs))
    late_gather = direct_exchange_start([_after(shards[n], early_done) for n in late], False, "gather_ffn2_start")
    ffn1_norm = _after(ffn1_norm, late_gather[4])
    w = {n: gathered[n] for n in early if n.startswith("ffn")}
    w["w_in_t"] = gathered["w_in"].reshape(-1, D_MODEL)
    w["w_branch_a"] = gathered["w_branch_a"].transpose(1, 0, 2).reshape(256, D_MODEL)
    w["w_branch_b"] = gathered["w_branch_b"].reshape(D_MODEL, D_MODEL)
    w["w_out"] = gathered["w_out"].reshape(D_MODEL, D_MODEL)
    conv_full = gathered["gdn_conv_w"].transpose(1, 0, 2).reshape(GDN_CONV, 3 * GDN_WIDTH)
    small = dict(mix_norm=mix_norm, gdn_a_log=gdn_a_log, gdn_dt_bias=gdn_dt_bias, gdn_out_norm=gdn_out_norm,
                 gdn_conv_w=conv_full)

    x1, ffn1_saved = ffn_forward(x[0], ffn1_norm, w, "ffn1")
    x2, mixer_saved = mixer_forward(x1, w, small)
    late_lands = direct_exchange_wait(*late_gather[:4], x2, False, "gather_ffn2_wait")
    for n, land in zip(late, late_lands):
        w[n] = lax.dynamic_update_slice(land, shards[n][None], (me, 0, 0))
    x3, ffn2_saved = ffn_forward(x2, ffn2_norm, w, "ffn2")
    loss_local, dx3, g_final = loss_head(x3, loss_target[0], final_norm.reshape(1, D_MODEL))
    loss = lax.psum(loss_local, ("x", "y", "c"))
    dx2, g_ffn2_norm, dw2 = ffn_backward(dx3, ffn2_saved, ffn2_norm, w, "ffn2")
    late_scatter = direct_exchange_start([g.astype(BF16) for g in dw2], True, "rs_ffn2_start")
    w_after = dict(w, w_out=_after(w["w_out"], late_scatter[4]))
    dx1, g_w = mixer_backward(dx2, mixer_saved, w_after, small)
    grad_x, g_ffn1_norm, dw1 = ffn_backward(dx1, ffn1_saved, ffn1_norm, w, "ffn1")
    g_small = dict(ffn1_norm=g_ffn1_norm, ffn2_norm=g_ffn2_norm, final_norm=g_final,
                   **{n: g_w[n] for n in ("mix_norm", "gdn_a_log", "gdn_dt_bias", "gdn_out_norm", "gdn_conv_w")})

    g_big = dict(zip(("ffn1_w_gate", "ffn1_w_up", "ffn1_w_down"), dw1))
    g_big["w_in"] = g_w["w_in_t"].reshape(N_DEV, -1, D_MODEL)
    g_big["w_branch_a"] = g_w["w_branch_a"].reshape(256, N_DEV, 128).transpose(1, 0, 2)
    g_big["w_branch_b"] = g_w["w_branch_b"].reshape(N_DEV, 128, D_MODEL)
    g_big["w_out"] = g_w["w_out"].reshape(N_DEV, 128, D_MODEL)
    g_list = [g_big[n] for n in early]
    core = pc.astype(jnp.int32).reshape(1)
    me_and_chip = jnp.stack([me, 2 * px + py]).astype(jnp.int32)
    from_sibling = exchange_with_sibling(g_list)
    partials = [add_sibling(g, r, core, "rs_add_" + n) for n, g, r in zip(early, g_list, from_sibling)]
    from_chips = exchange_with_chips(partials)

    def state_of(n):
        return [shard_view(given[p + n], n) for p in ("", "m_", "v_")]

    results = {}
    for n, g, sib, recv in zip(early, g_list, from_sibling, from_chips):
        outs = adamw_summed(*state_of(n), g, sib, recv, me_and_chip, "adamw_" + n)
        results[n] = tuple(shard_view(o, n) for o in outs)
    late_received = direct_exchange_wait(*late_scatter[:4], grad_x, True, "rs_ffn2_wait")
    for n, g, recv in zip(late, dw2, late_received):
        outs = adamw_direct(*state_of(n), g, recv, me_index, "adamw_" + n)
        results[n] = tuple(shard_view(o, n) for o in outs)

    small_sum = _unpack_small(all_reduce_small(_pack_small(g_small)))
    conv_cols = CONV_SHARD[1]
    small_sum["gdn_conv_w"] = lax.dynamic_slice(small_sum["gdn_conv_w"], (0, me * conv_cols), (GDN_CONV, conv_cols))
    for n in WEIGHTS:
        if n not in results:
            g = small_sum[n].reshape(given[n].shape)
            results[n] = (g,) + adamw(given[n], g, given["m_" + n], given["v_" + n], "adamw_" + n)

    outs = [[results[n][i] for n in WEIGHTS] for i in range(4)]
    return (loss, grad_x[None], *outs[0], *outs[1], *outs[2], *outs[3])
```

```python
import jax
import jax.numpy as jnp
from jax import lax
from jax.experimental import pallas as pl
from jax.experimental.pallas import tpu as pltpu

F32 = jnp.float32
BF16 = jnp.bfloat16
HI = lax.Precision.HIGHEST
MESH = pl.DeviceIdType.MESH

N_DEV = 8
D_MODEL = 1024
EPS = 1e-6
ROPE_THETA = 10000.0
DSW_DILATIONS = (1, 4, 16)
DSW_HEADS_PER_GROUP = 4
DSW_HEAD_DIM = 64
DSW_BLOCK = 128
GDN_HEADS = 8
GDN_HEAD_DIM = 128
GDN_WIDTH = 1024
GDN_CONV = 4
GDN_CHUNK = 64

ADAM_LR = 0.001
ADAM_B1 = 0.9
ADAM_B2 = 0.999
ADAM_EPS = 1e-08
ADAM_WD = 0.01
ADAM_STEP = 10

VMEM_LIMIT_BYTES = 56 * 1024 * 1024
LANES = 128

NN = (((1,), (0,)), ((), ()))
NT = (((1,), (1,)), ((), ()))
TN = (((0,), (0,)), ((), ()))


def _params(n_grid):
    return pltpu.CompilerParams(dimension_semantics=("arbitrary",) * n_grid, vmem_limit_bytes=VMEM_LIMIT_BYTES)


def _tile(n, pref):
    best = None
    t = LANES
    while t <= min(n, pref):
        if n % t == 0:
            best = t
        t += LANES
    return n if best is None else best


def _matmul(a, b, *, name, ta=False, tb=False, res=None, scale=1.0):
    K, M = a.shape if ta else a.shape[::-1]
    N = b.shape[0] if tb else b.shape[1]
    assert (b.shape[1] if tb else b.shape[0]) == K, (a.shape, b.shape, ta, tb)
    tm = _tile(M, 512)
    tn = _tile(N, 512)
    dn = (((0 if ta else 1,), (1 if tb else 0,)), ((), ()))

    def body(*refs):
        a_ref, b_ref = refs[:2]
        o_ref = refs[-1]
        acc = lax.dot_general(a_ref[...].astype(BF16), b_ref[...].astype(BF16), dn, preferred_element_type=F32)
        if scale != 1.0:
            acc = acc * scale
        if res is not None:
            acc = refs[2][...] + acc
        o_ref[...] = acc

    a_spec = pl.BlockSpec((K, tm), lambda i, j: (0, i)) if ta else pl.BlockSpec((tm, K), lambda i, j: (i, 0))
    b_spec = pl.BlockSpec((tn, K), lambda i, j: (j, 0)) if tb else pl.BlockSpec((K, tn), lambda i, j: (0, j))
    o_spec = pl.BlockSpec((tm, tn), lambda i, j: (i, j))
    ins, specs = [a, b], [a_spec, b_spec]
    if res is not None:
        ins.append(res)
        specs.append(o_spec)
    return pl.pallas_call(
        body, grid=(M // tm, N // tn), in_specs=specs, out_specs=o_spec,
        out_shape=jax.ShapeDtypeStruct((M, N), F32), name=name, compiler_params=_params(2),
    )(*ins)


def _rw_specs(arrs, tm, nblk):
    return [pl.BlockSpec((tm, a.shape[1] // nblk), lambda i, j: (i, j)) for a in arrs]


def _rowwise_fwd(fn, name, rows, consts, params, tm, nblk):
    n_rows = rows[0].shape[0]
    tm = min(tm, n_rows)
    ins = list(rows) + list(consts)
    avals = [jax.ShapeDtypeStruct((tm, a.shape[1] // nblk), a.dtype) for a in ins]
    avals += [jax.ShapeDtypeStruct(p.shape, p.dtype) for p in params]
    out_avals = jax.eval_shape(fn, *avals)
    n_in = len(ins) + len(params)

    def body(*refs):
        outs = fn(*[r[...] for r in refs[:n_in]])
        for r, o in zip(refs[n_in:], outs):
            r[...] = o.astype(r.dtype)

    return pl.pallas_call(
        body, grid=(n_rows // tm, nblk),
        in_specs=_rw_specs(ins, tm, nblk) + [pl.BlockSpec(p.shape, lambda i, j: (0, 0)) for p in params],
        out_specs=tuple(pl.BlockSpec((tm, o.shape[1]), lambda i, j: (i, j)) for o in out_avals),
        out_shape=tuple(jax.ShapeDtypeStruct((n_rows, o.shape[1] * nblk), o.dtype) for o in out_avals),
        name=name, compiler_params=_params(2),
    )(*ins, *params)


def _rowwise_bwd(fn, name, rows, consts, params, cts, tm, nblk):
    n_rows = rows[0].shape[0]
    tm = min(tm, n_rows)
    nr, nc, npar, nct = len(rows), len(consts), len(params), len(cts)

    def body(*refs):
        rv = [r[...] for r in refs[:nr]]
        cv = [r[...] for r in refs[nr:nr + nc]]
        pv = [r[...] for r in refs[nr + nc:nr + nc + npar]]
        ctv = [r[...] for r in refs[nr + nc + npar:nr + nc + npar + nct]]
        outs = refs[nr + nc + npar + nct:]
        _, vjp = jax.vjp(lambda *d: fn(*d[:nr], *cv, *d[nr:]), *rv, *pv)
        grads = vjp(tuple(ctv))
        for k in range(nr):
            outs[k][...] = grads[k]
        first = jnp.logical_and(pl.program_id(0) == 0, pl.program_id(1) == 0)
        for k in range(npar):
            ref = outs[nr + k]

            @pl.when(first)
            def _(ref=ref):
                ref[...] = jnp.zeros_like(ref)

            ref[...] += grads[nr + k]

    ins = list(rows) + list(consts)
    return pl.pallas_call(
        body, grid=(n_rows // tm, nblk),
        in_specs=(_rw_specs(ins, tm, nblk) + [pl.BlockSpec(p.shape, lambda i, j: (0, 0)) for p in params]
                  + _rw_specs(cts, tm, nblk)),
        out_specs=tuple(_rw_specs(rows, tm, nblk) + [pl.BlockSpec(p.shape, lambda i, j: (0, 0)) for p in params]),
        out_shape=tuple([jax.ShapeDtypeStruct(a.shape, F32) for a in rows]
                        + [jax.ShapeDtypeStruct(p.shape, F32) for p in params]),
        name=name, compiler_params=_params(2),
    )(*ins, *params, *cts)


def _merge_fn(ga, gb, pa, pb):
    return (jax.nn.sigmoid(ga) * pa + jax.nn.sigmoid(gb) * pb,)


def _outnorm_gate_fn(o, gate, gain):
    y = o * lax.rsqrt(jnp.mean(o * o, axis=-1, keepdims=True) + EPS) * gain
    return (y * (gate * jax.nn.sigmoid(gate)),)


def _beta_decay_fn(beta_raw, decay_raw, a_log, dt_bias):
    z = decay_raw + dt_bias
    softplus = jnp.maximum(z, 0.0) + jnp.log(1.0 + jnp.exp(-jnp.abs(z)))
    g = -jnp.exp(a_log) * softplus
    rows = g.shape[0]
    ii = lax.broadcasted_iota(jnp.int32, (rows, rows), 0)
    jj = lax.broadcasted_iota(jnp.int32, (rows, rows), 1)
    same_chunk_before = jnp.logical_and(jj <= ii, jj // GDN_CHUNK == ii // GDN_CHUNK).astype(F32)
    gcum = lax.dot_general(same_chunk_before, g, NN, precision=HI, preferred_element_type=F32)
    return jax.nn.sigmoid(beta_raw), gcum


def _combine_fn(o0, o1, o2, l0, l1, l2):
    m = lax.stop_gradient(jnp.maximum(jnp.maximum(l0, l1), l2))
    e0, e1, e2 = jnp.exp(l0 - m), jnp.exp(l1 - m), jnp.exp(l2 - m)
    return ((e0 * o0 + e1 * o1 + e2 * o2) / (e0 + e1 + e2),)


def _loss_fn(x, target, gain):
    y = x * lax.rsqrt(jnp.mean(x * x, axis=-1, keepdims=True) + EPS) * gain
    err = y - target
    return (0.5 * jnp.mean(err * err, axis=-1, keepdims=True),)


def _rope_call(x, cos, sin, name):
    n_rows, width = x.shape
    tm = 512

    def body(x_ref, c_ref, s_ref, o_ref):
        v = x_ref[...]
        lane = lax.broadcasted_iota(jnp.int32, v.shape, 1)
        low = (lane % DSW_HEAD_DIM) < DSW_HEAD_DIM // 2
        half = DSW_HEAD_DIM // 2
        swapped = jnp.where(low, pltpu.roll(v, LANES - half, 1), pltpu.roll(v, half, 1))
        o_ref[...] = v * c_ref[...] + swapped * s_ref[...]

    tab = pl.BlockSpec((tm, LANES), lambda i, j: (i, 0))
    blk = pl.BlockSpec((tm, LANES), lambda i, j: (i, j))
    return pl.pallas_call(
        body, grid=(n_rows // tm, width // LANES), in_specs=[blk, tab, tab], out_specs=blk,
        out_shape=jax.ShapeDtypeStruct(x.shape, F32), name=name, compiler_params=_params(2),
    )(x, cos, sin)


def _rope_tables(n_tokens):
    half = DSW_HEAD_DIM // 2
    inv_freq = ROPE_THETA ** (-jnp.arange(half, dtype=F32) / half)
    ang = jnp.arange(n_tokens, dtype=F32)[:, None] * inv_freq[None, :]
    cos, sin = jnp.cos(ang), jnp.sin(ang)
    return jnp.tile(jnp.concatenate([cos, cos], 1), (1, 2)), jnp.tile(jnp.concatenate([-sin, sin], 1), (1, 2))


def _attn_probs(q, kp, kc, group, n):
    blk = DSW_BLOCK
    k = _each(lambda a, b: jnp.concatenate([a, b], axis=0).astype(BF16), kp, kc)
    s = _each(lambda a, b: lax.dot_general(a.astype(BF16), b, NT, preferred_element_type=F32)
              * (DSW_HEAD_DIM ** -0.5), q, k)
    blocks_per_seq = jnp.where(group == 0, 16, jnp.where(group == 1, 4, 1))
    first = (n % blocks_per_seq) == 0
    qi = lax.broadcasted_iota(jnp.int32, (blk, 2 * blk), 0)
    kj = lax.broadcasted_iota(jnp.int32, (blk, 2 * blk), 1)
    dist = qi + blk - kj
    valid = (dist >= 0) & (dist <= blk) & jnp.logical_or(kj >= blk, jnp.logical_not(first))
    s = _each(lambda a: jnp.where(valid, a, -1e30), s)
    m = _each(lambda a: jnp.max(a, axis=-1, keepdims=True), s)
    p = _each(lambda a, b: jnp.exp(a - b), s, m)
    l = _each(lambda a: jnp.sum(a, axis=-1, keepdims=True), p)
    return _each(lambda a, b: a / b, p, l), _each(lambda a, b: a + jnp.log(b), m, l), k


def _attn_specs(n_tokens):
    blk, hpg = DSW_BLOCK, DSW_HEADS_PER_GROUP
    cur = pl.BlockSpec((hpg, blk, DSW_HEAD_DIM), lambda g, n: (g, n, 0))
    prev = pl.BlockSpec((hpg, blk, DSW_HEAD_DIM), lambda g, n: (g, jnp.maximum(n - 1, 0), 0))
    return cur, prev


def _attn_fwd(q, k, v):
    nh, n_tokens, hd = q.shape
    hpg = DSW_HEADS_PER_GROUP
    cur, prev = _attn_specs(n_tokens)

    def body(q_ref, kp_ref, kc_ref, vp_ref, vc_ref, o_ref, l_ref):
        heads = range(hpg)
        p, lse, _ = _attn_probs([q_ref[h] for h in heads], [kp_ref[h] for h in heads], [kc_ref[h] for h in heads],
                                pl.program_id(0), pl.program_id(1))
        vv = [jnp.concatenate([vp_ref[h], vc_ref[h]], axis=0).astype(BF16) for h in heads]
        o = _each(lambda a, b: lax.dot_general(a.astype(BF16), b, NN, preferred_element_type=F32), p, vv)
        for h in heads:
            o_ref[h] = o[h]
            l_ref[h] = jnp.broadcast_to(lse[h], (DSW_BLOCK, hd))

    return pl.pallas_call(
        body, grid=(nh // hpg, n_tokens // DSW_BLOCK), in_specs=[cur, prev, cur, prev, cur], out_specs=(cur, cur),
        out_shape=(jax.ShapeDtypeStruct(q.shape, F32), jax.ShapeDtypeStruct(q.shape, F32)),
        name="attn_fwd", compiler_params=_params(2),
    )(q, k, k, v, v)


def _attn_bwd(q, k, v, do, dlse):
    nh, n_tokens, hd = q.shape
    hpg = DSW_HEADS_PER_GROUP
    nblk = n_tokens // DSW_BLOCK
    cur, prev = _attn_specs(n_tokens)
    part = pl.BlockSpec((hpg, 1, 2 * DSW_BLOCK, hd), lambda g, n: (g, n, 0, 0))
    scale = DSW_HEAD_DIM ** -0.5

    def body(q_ref, kp_ref, kc_ref, vp_ref, vc_ref, do_ref, dl_ref, dq_ref, dk_ref, dv_ref):
        heads = range(hpg)
        qs = [q_ref[h] for h in heads]
        p, _, kb = _attn_probs(qs, [kp_ref[h] for h in heads], [kc_ref[h] for h in heads],
                               pl.program_id(0), pl.program_id(1))
        qb = _each(lambda a: a.astype(BF16), qs)
        vv = [jnp.concatenate([vp_ref[h], vc_ref[h]], axis=0).astype(BF16) for h in heads]
        dob = [do_ref[h].astype(BF16) for h in heads]
        dp = _each(lambda a, b: lax.dot_general(a, b, NT, preferred_element_type=F32), dob, vv)
        dv = _each(lambda a, b: lax.dot_general(a.astype(BF16), b, TN, preferred_element_type=F32), p, dob)
        dl = [jnp.sum(dl_ref[h], axis=-1, keepdims=True) for h in heads]
        ds = _each(lambda a, b, c: (a * (b - jnp.sum(b * a, axis=-1, keepdims=True) + c) * scale).astype(BF16),
                   p, dp, dl)
        dq = _each(lambda a, b: lax.dot_general(a, b, NN, preferred_element_type=F32), ds, kb)
        dk = _each(lambda a, b: lax.dot_general(a, b, TN, preferred_element_type=F32), ds, qb)
        for h in heads:
            dq_ref[h] = dq[h]
            dk_ref[h, 0] = dk[h]
            dv_ref[h, 0] = dv[h]

    dq, dkp, dvp = pl.pallas_call(
        body, grid=(nh // hpg, nblk), in_specs=[cur, prev, cur, prev, cur, cur, cur], out_specs=(cur, part, part),
        out_shape=(jax.ShapeDtypeStruct(q.shape, F32),
                   jax.ShapeDtypeStruct((nh, nblk, 2 * DSW_BLOCK, hd), F32),
                   jax.ShapeDtypeStruct((nh, nblk, 2 * DSW_BLOCK, hd), F32)),
        name="attn_bwd", compiler_params=_params(2),
    )(q, k, k, v, v, do, dlse)

    def fold(partial):
        own = partial[:, :, DSW_BLOCK:]
        from_next = jnp.pad(partial[:, 1:, :DSW_BLOCK], ((0, 0), (0, 1), (0, 0), (0, 0)))
        return (own + from_next).reshape(nh, n_tokens, hd)

    return dq, fold(dkp), fold(dvp)


def _to_heads(a):
    n_tokens = a.shape[0]
    outs = []
    for gi, d in enumerate(DSW_DILATIONS):
        blk = a[:, gi * 256:(gi + 1) * 256].reshape(n_tokens // d, d, DSW_HEADS_PER_GROUP, DSW_HEAD_DIM)
        outs.append(blk.transpose(2, 1, 0, 3).reshape(DSW_HEADS_PER_GROUP, n_tokens, DSW_HEAD_DIM))
    return jnp.concatenate(outs, 0)


def _from_heads(a):
    n_tokens = a.shape[1]
    outs = []
    for gi, d in enumerate(DSW_DILATIONS):
        blk = a[gi * 4:(gi + 1) * 4].reshape(DSW_HEADS_PER_GROUP, d, n_tokens // d, DSW_HEAD_DIM)
        outs.append(blk.transpose(2, 1, 0, 3).reshape(n_tokens, DSW_HEADS_PER_GROUP * DSW_HEAD_DIM))
    return outs


CONV_TILE = 512


def _shift_down(x, k, rows):
    return x if k == 0 else jnp.where(rows >= k, pltpu.roll(x, k, 0), 0.0)


def _shift_up(x, k, rows):
    n = x.shape[0]
    return x if k == 0 else jnp.where(rows < n - k, pltpu.roll(x, n - k, 0), 0.0)


def _conv_pre(x, w):
    rows = lax.broadcasted_iota(jnp.int32, x.shape, 0)
    acc = x * w[GDN_CONV - 1:GDN_CONV]
    for k in range(1, GDN_CONV):
        acc = acc + _shift_down(x, k, rows) * w[GDN_CONV - 1 - k:GDN_CONV - k]
    return acc, rows


def _conv_fwd(x, w):
    n_tokens, width = x.shape
    big = pl.BlockSpec((n_tokens, CONV_TILE), lambda j: (0, j))
    wsp = pl.BlockSpec((GDN_CONV, CONV_TILE), lambda j: (0, j))

    def body(x_ref, w_ref, o_ref):
        acc, _ = _conv_pre(x_ref[...], w_ref[...])
        o_ref[...] = acc * jax.nn.sigmoid(acc)

    return pl.pallas_call(
        body, grid=(width // CONV_TILE,), in_specs=[big, wsp], out_specs=big,
        out_shape=jax.ShapeDtypeStruct(x.shape, F32), name="conv_fwd", compiler_params=_params(1),
    )(x, w)


def _conv_bwd(x, w, dy):
    n_tokens, width = x.shape
    big = pl.BlockSpec((n_tokens, CONV_TILE), lambda j: (0, j))
    wsp = pl.BlockSpec((GDN_CONV, CONV_TILE), lambda j: (0, j))

    def body(x_ref, w_ref, dy_ref, dx_ref, dw_ref):
        xv, wv = x_ref[...], w_ref[...]
        acc, rows = _conv_pre(xv, wv)
        sg = jax.nn.sigmoid(acc)
        dacc = dy_ref[...] * (sg + acc * sg * (1.0 - sg))
        dx = dacc * wv[GDN_CONV - 1:GDN_CONV]
        for k in range(1, GDN_CONV):
            dx = dx + _shift_up(dacc, k, rows) * wv[GDN_CONV - 1 - k:GDN_CONV - k]
        dx_ref[...] = dx
        for k in range(GDN_CONV):
            dw_ref[GDN_CONV - 1 - k:GDN_CONV - k, :] = jnp.sum(dacc * _shift_down(xv, k, rows), axis=0, keepdims=True)

    return pl.pallas_call(
        body, grid=(width // CONV_TILE,), in_specs=[big, wsp, big], out_specs=(big, wsp),
        out_shape=(jax.ShapeDtypeStruct(x.shape, F32), jax.ShapeDtypeStruct(w.shape, F32)),
        name="conv_bwd", compiler_params=_params(1),
    )(x, w, dy)


def _dot(a, b, dn=NN):
    return lax.dot_general(a, b, dn, precision=HI, preferred_element_type=F32)


def _dot3(a, b, dn=NN):
    return lax.dot_general(a, b, dn, precision=lax.Precision.HIGH, preferred_element_type=F32)


def _bf16_dot(a, b, dn):
    return lax.dot_general(a.astype(BF16), b.astype(BF16), dn, preferred_element_type=F32)


_DOT_GRADS = {NN: (("g", "b", NT), ("a", "g", TN)), NT: (("g", "b", NN), ("g", "a", TN)),
              TN: (("b", "g", NT), ("a", "g", NN))}


def _make_bdot(dn):
    @jax.custom_vjp
    def op(a, b):
        return _bf16_dot(a, b, dn)

    def fwd(a, b):
        return op(a, b), (a, b)

    def bwd(saved, g):
        vals = dict(a=saved[0], b=saved[1], g=g)
        return tuple(_bf16_dot(vals[x], vals[y], form) for x, y, form in _DOT_GRADS[dn])

    op.defvjp(fwd, bwd)
    return op


_BDOTS = {dn: _make_bdot(dn) for dn in (NN, NT, TN)}


def _bdot(a, b, dn=NN):
    return _BDOTS[dn](a, b)


def _each(fn, *lists):
    return [fn(*items) for items in zip(*lists)]


def _gdn_chunks(q, k, v, b, gcum, state):
    c = GDN_CHUNK
    ii = lax.broadcasted_iota(jnp.int32, (c, c), 0)
    jj = lax.broadcasted_iota(jnp.int32, (c, c), 1)
    eye = (ii == jj).astype(F32)
    qn = _each(lambda x: x * lax.rsqrt(jnp.sum(x * x, axis=-1, keepdims=True) + EPS) * (GDN_HEAD_DIM ** -0.5), q)
    kn = _each(lambda x: x * lax.rsqrt(jnp.sum(x * x, axis=-1, keepdims=True) + EPS), k)
    gcum_i = _each(lambda x: jnp.broadcast_to(x, (c, c)), gcum)
    gcum_j = _each(jnp.transpose, gcum_i)
    decay = _each(lambda x, y: jnp.exp(jnp.where(jj <= ii, x - y, -1e30)), gcum_i, gcum_j)
    g_last = _each(lambda x: x[c - 1:c, :], gcum)
    e_gcum = _each(jnp.exp, gcum)
    kbeta = _each(lambda x, y: x * y, kn, b)
    vbeta = _each(lambda x, y: x * y, v, b)
    m = _each(lambda x, y, d: jnp.where(jj < ii, _bdot(x, y, NT) * d, 0.0), kbeta, kn, decay)
    inv = _each(lambda x: eye - x, m)
    power = _each(lambda x: _dot3(x, x), m)
    for step in range(5):
        inv = _each(lambda x, p: x + _dot3(x, p), inv, power)
        if step < 4:
            power = _each(lambda p: _dot3(p, p), power)
    u = _each(_dot3, inv, vbeta)
    w = _each(lambda x, y, e: _dot3(x, y * e), inv, kbeta, e_gcum)
    a_qk = _each(lambda x, y, d: _bdot(x, y, NT) * d, qn, kn, decay)
    v_new = _each(lambda x, y, s: x - _bdot(y, s), u, w, state)
    o = _each(lambda x, e, s, a, vn: _bdot(x * e, s) + _bdot(a, vn), qn, e_gcum, state, a_qk, v_new)
    new_state = _each(lambda s, gl, x, gc, vn: s * jnp.exp(gl) + _bdot(x * jnp.exp(gl - gc), vn, TN),
                      state, g_last, kn, gcum, v_new)
    return o, new_state


GDN_HEADS_PER_STEP = 8


GDN_TIME_TILE = 256


def _gdn_specs(n_tokens, reverse):
    hb, hd, tt = GDN_HEADS_PER_STEP, GDN_HEAD_DIM, GDN_TIME_TILE
    nb, nt = GDN_HEADS // hb, n_tokens // tt

    def when(t):
        return nt - 1 - t if reverse else t

    q = pl.BlockSpec((tt, hb * hd), lambda h, t: (when(t), h))
    k = pl.BlockSpec((tt, hb * hd), lambda h, t: (when(t), nb + h))
    v = pl.BlockSpec((tt, hb * hd), lambda h, t: (when(t), 2 * nb + h))
    vec = pl.BlockSpec((tt, hb), lambda h, t: (when(t), h))
    states = pl.BlockSpec((hb, tt // GDN_CHUNK, hd, hd), lambda h, t: (h, when(t), 0, 0))
    return q, k, v, vec, states


def _gdn_fwd(qkv, beta, g):
    n_tokens = qkv.shape[0]
    hb, hd, tt = GDN_HEADS_PER_STEP, GDN_HEAD_DIM, GDN_TIME_TILE
    n_chunks = tt // GDN_CHUNK
    q_s, k_s, v_s, vec, st = _gdn_specs(n_tokens, False)

    def body(q_ref, k_ref, v_ref, b_ref, g_ref, o_ref, st_ref, state):
        @pl.when(pl.program_id(1) == 0)
        def _():
            state[...] = jnp.zeros_like(state)

        def step(c, carry):
            r = pl.ds(pl.multiple_of(c * GDN_CHUNK, GDN_CHUNK), GDN_CHUNK)
            cols = [slice(h * hd, (h + 1) * hd) for h in range(hb)]
            old = [state[h] for h in range(hb)]
            o, new = _gdn_chunks(
                [q_ref[r, cs] for cs in cols], [k_ref[r, cs] for cs in cols], [v_ref[r, cs] for cs in cols],
                [b_ref[r, h:h + 1] for h in range(hb)], [g_ref[r, h:h + 1] for h in range(hb)], old)
            for h in range(hb):
                st_ref[h, c] = old[h]
                o_ref[r, cols[h]] = o[h]
                state[h] = new[h]
            return carry

        lax.fori_loop(0, n_chunks, step, 0)

    return pl.pallas_call(
        body, grid=(GDN_HEADS // hb, n_tokens // tt), in_specs=[q_s, k_s, v_s, vec, vec], out_specs=(q_s, st),
        out_shape=(jax.ShapeDtypeStruct((n_tokens, GDN_WIDTH), F32),
                   jax.ShapeDtypeStruct((GDN_HEADS, n_tokens // GDN_CHUNK, hd, hd), F32)),
        scratch_shapes=[pltpu.VMEM((hb, hd, hd), F32)],
        name="gdn_fwd", compiler_params=_params(2),
    )(qkv, qkv, qkv, beta, g)


def _gdn_bwd(qkv, beta, g, states, do):
    n_tokens = qkv.shape[0]
    hb, hd, tt = GDN_HEADS_PER_STEP, GDN_HEAD_DIM, GDN_TIME_TILE
    n_chunks = tt // GDN_CHUNK
    q_s, k_s, v_s, vec, st = _gdn_specs(n_tokens, True)

    def body(q_ref, k_ref, v_ref, b_ref, g_ref, st_ref, do_ref, dq_ref, dk_ref, dv_ref, db_ref, dg_ref, dstate):
        @pl.when(pl.program_id(1) == 0)
        def _():
            dstate[...] = jnp.zeros_like(dstate)

        def step(i, carry):
            c = n_chunks - 1 - i
            r = pl.ds(pl.multiple_of(c * GDN_CHUNK, GDN_CHUNK), GDN_CHUNK)
            cols = [slice(h * hd, (h + 1) * hd) for h in range(hb)]
            args = ([q_ref[r, cs] for cs in cols], [k_ref[r, cs] for cs in cols], [v_ref[r, cs] for cs in cols],
                    [b_ref[r, h:h + 1] for h in range(hb)], [g_ref[r, h:h + 1] for h in range(hb)],
                    [st_ref[h, c] for h in range(hb)])
            cts = ([do_ref[r, cs] for cs in cols], [dstate[h] for h in range(hb)])
            dq, dk, dv, db, dg, dst = jax.vjp(_gdn_chunks, *args)[1](cts)
            for h in range(hb):
                dq_ref[r, cols[h]] = dq[h]
                dk_ref[r, cols[h]] = dk[h]
                dv_ref[r, cols[h]] = dv[h]
                db_ref[r, h:h + 1] = db[h]
                dg_ref[r, h:h + 1] = dg[h]
                dstate[h] = dst[h]
            return carry

        lax.fori_loop(0, n_chunks, step, 0)

    wide = jax.ShapeDtypeStruct((n_tokens, GDN_WIDTH), F32)
    thin = jax.ShapeDtypeStruct(beta.shape, F32)
    dq, dk, dv, db, dg = pl.pallas_call(
        body, grid=(GDN_HEADS // hb, n_tokens // tt), in_specs=[q_s, k_s, v_s, vec, vec, st, q_s],
        out_specs=(q_s, q_s, q_s, vec, vec), out_shape=(wide, wide, wide, thin, thin),
        scratch_shapes=[pltpu.VMEM((hb, hd, hd), F32)],
        name="gdn_bwd", compiler_params=_params(2),
    )(qkv, qkv, qkv, beta, g, states, do)
    return jnp.concatenate([dq, dk, dv], axis=1), db, dg


FFN_ROW_TILE = 256


def _resident(shape):
    return pl.BlockSpec(shape, lambda i: (0,) * len(shape), pipeline_mode=pl.Buffered(1))


def _ffn_fwd(x, gain, wg, wu, wd, name):
    n_tokens, d = x.shape
    n_shards, n, _ = wg.shape
    tm = FFN_ROW_TILE

    def body(x_ref, gain_ref, wg_ref, wu_ref, wd_ref, o_ref, g_ref, u_ref):
        xv = x_ref[...]
        h = (xv * lax.rsqrt(jnp.mean(xv * xv, axis=-1, keepdims=True) + EPS) * gain_ref[...]).astype(BF16)
        acc = jnp.zeros((tm, d), F32)
        for j in range(n_shards):
            g = lax.dot_general(h, wg_ref[j], NT, preferred_element_type=F32)
            u = lax.dot_general(h, wu_ref[j], NT, preferred_element_type=F32)
            g_ref[j] = g
            u_ref[j] = u
            a = (g * jax.nn.sigmoid(g) * u).astype(BF16)
            acc = acc + lax.dot_general(a, wd_ref[j], NN, preferred_element_type=F32)
        o_ref[...] = xv + 0.5 * acc

    row = pl.BlockSpec((tm, d), lambda i: (i, 0))
    hid = pl.BlockSpec((n_shards, tm, n), lambda i: (0, i, 0))
    return pl.pallas_call(
        body, grid=(n_tokens // tm,),
        in_specs=[row, _resident(gain.shape), _resident(wg.shape), _resident(wu.shape), _resident(wd.shape)],
        out_specs=(row, hid, hid),
        out_shape=(jax.ShapeDtypeStruct(x.shape, F32), jax.ShapeDtypeStruct((n_shards, n_tokens, n), F32),
                   jax.ShapeDtypeStruct((n_shards, n_tokens, n), F32)),
        name=name, compiler_params=_params(1),
    )(x, gain, wg, wu, wd)


def _ffn_bwd_rows(x, gain, dy, g, u, wg, wu, wd, name):
    n_tokens, d = x.shape
    n_shards, n, _ = wg.shape
    tm = FFN_ROW_TILE

    def body(x_ref, gain_ref, dy_ref, g_ref, u_ref, wg_ref, wu_ref, wd_ref,
             dx_ref, dgain_ref, h_ref, dyh_ref, a_ref, dg_ref, du_ref):
        xv, dyv, gain_v = x_ref[...], dy_ref[...], gain_ref[...]
        r = lax.rsqrt(jnp.mean(xv * xv, axis=-1, keepdims=True) + EPS)
        xhat = xv * r
        h_ref[...] = (xhat * gain_v).astype(BF16)
        dyh = (0.5 * dyv).astype(BF16)
        dyh_ref[...] = dyh
        dh = jnp.zeros((tm, d), F32)
        for j in range(n_shards):
            da = lax.dot_general(dyh, wd_ref[j], NT, preferred_element_type=F32)
            gv, uv = g_ref[j], u_ref[j]
            sg = jax.nn.sigmoid(gv)
            silu = gv * sg
            a_ref[j] = (silu * uv).astype(BF16)
            dg = (da * uv * (sg + silu * (1.0 - sg))).astype(BF16)
            du = (da * silu).astype(BF16)
            dg_ref[j] = dg
            du_ref[j] = du
            dh = dh + lax.dot_general(dg, wg_ref[j], NN, preferred_element_type=F32)
            dh = dh + lax.dot_general(du, wu_ref[j], NN, preferred_element_type=F32)
        dxhat = dh * gain_v
        dx_ref[...] = dyv + r * (dxhat - xhat * jnp.mean(dxhat * xhat, axis=-1, keepdims=True))

        @pl.when(pl.program_id(0) == 0)
        def _():
            dgain_ref[...] = jnp.zeros_like(dgain_ref)

        dgain_ref[...] += jnp.sum(dh * xhat, axis=0, keepdims=True)

    row = pl.BlockSpec((tm, d), lambda i: (i, 0))
    hid = pl.BlockSpec((n_shards, tm, n), lambda i: (0, i, 0))
    hid_shape = (n_shards, n_tokens, n)
    return pl.pallas_call(
        body, grid=(n_tokens // tm,),
        in_specs=[row, _resident(gain.shape), row, hid, hid, _resident(wg.shape), _resident(wu.shape),
                  _resident(wd.shape)],
        out_specs=(row, pl.BlockSpec(gain.shape, lambda i: (0, 0)), row, row, hid, hid, hid),
        out_shape=(jax.ShapeDtypeStruct(x.shape, F32), jax.ShapeDtypeStruct(gain.shape, F32),
                   jax.ShapeDtypeStruct(x.shape, BF16), jax.ShapeDtypeStruct(x.shape, BF16),
                   jax.ShapeDtypeStruct(hid_shape, BF16), jax.ShapeDtypeStruct(hid_shape, BF16),
                   jax.ShapeDtypeStruct(hid_shape, BF16)),
        name=name, compiler_params=_params(1),
    )(x, gain, dy, g, u, wg, wu, wd)


def _ffn_bwd_weights(h, dyh, a, dg, du, name):
    n_shards, n_tokens, n = a.shape
    d = h.shape[1]

    def body(h_ref, dyh_ref, a_ref, dg_ref, du_ref, dwg_ref, dwu_ref, dwd_ref):
        hv = h_ref[...]
        dwg_ref[0] = lax.dot_general(dg_ref[0], hv, TN, preferred_element_type=F32)
        dwu_ref[0] = lax.dot_general(du_ref[0], hv, TN, preferred_element_type=F32)
        dwd_ref[0] = lax.dot_general(a_ref[0], dyh_ref[...], TN, preferred_element_type=F32)

    hid = pl.BlockSpec((1, n_tokens, n), lambda j: (j, 0, 0))
    out = pl.BlockSpec((1, n, d), lambda j: (j, 0, 0))
    return pl.pallas_call(
        body, grid=(n_shards,), in_specs=[_resident(h.shape), _resident(dyh.shape), hid, hid, hid],
        out_specs=(out, out, out), out_shape=(jax.ShapeDtypeStruct((n_shards, n, d), F32),) * 3,
        name=name, compiler_params=_params(1),
    )(h, dyh, a, dg, du)


IN_PIECES = (("wq_a", 0, 768), ("wk_a", 768, 1536), ("wv_a", 1536, 2304), ("w_qkvb", 2304, 5376),
             ("w_small", 5376, 5392), ("w_ggate", 5392, 6416), ("w_gatea", 6416, 7440), ("w_gateb", 7440, 8464))
IN_NAMES = tuple(name for name, _, _ in IN_PIECES)


def _in_rows(lo, hi):
    return lo, max(hi, lo + LANES)


def _in_proj_fwd(x, gain, wt):
    n_tokens, d = x.shape
    tm = FFN_ROW_TILE
    rows = [_in_rows(lo, hi) for _, lo, hi in IN_PIECES]

    def body(x_ref, gain_ref, wt_ref, *o_refs):
        xv = x_ref[...]
        h = (xv * lax.rsqrt(jnp.mean(xv * xv, axis=-1, keepdims=True) + EPS) * gain_ref[...]).astype(BF16)
        for (lo, hi), o_ref in zip(rows, o_refs):
            o_ref[...] = lax.dot_general(h, wt_ref[lo:hi, :], NT, preferred_element_type=F32)

    return pl.pallas_call(
        body, grid=(n_tokens // tm,),
        in_specs=[pl.BlockSpec((tm, d), lambda i: (i, 0)), _resident(gain.shape), _resident(wt.shape)],
        out_specs=tuple(pl.BlockSpec((tm, hi - lo), lambda i: (i, 0)) for lo, hi in rows),
        out_shape=tuple(jax.ShapeDtypeStruct((n_tokens, hi - lo), F32) for lo, hi in rows),
        name="in_proj_fwd", compiler_params=_params(1),
    )(x, gain, wt)


def _in_proj_bwd_rows(x, gain, dres, dzs, wt):
    n_tokens, d = x.shape
    tm = FFN_ROW_TILE
    n = len(dzs)
    rows = [_in_rows(lo, hi) for _, lo, hi in IN_PIECES]

    def body(x_ref, gain_ref, dres_ref, *refs):
        dz_refs, wt_ref = refs[:n], refs[n]
        dx_ref, dgain_ref, h_ref = refs[n + 1:]
        xv, gain_v = x_ref[...], gain_ref[...]
        r = lax.rsqrt(jnp.mean(xv * xv, axis=-1, keepdims=True) + EPS)
        xhat = xv * r
        h_ref[...] = (xhat * gain_v).astype(BF16)
        dh = jnp.zeros((tm, d), F32)
        for dz_ref, (lo, hi) in zip(dz_refs, rows):
            dh = dh + lax.dot_general(dz_ref[...].astype(BF16), wt_ref[lo:hi, :], NN, preferred_element_type=F32)
        dxhat = dh * gain_v
        dx_ref[...] = dres_ref[...] + r * (dxhat - xhat * jnp.mean(dxhat * xhat, axis=-1, keepdims=True))

        @pl.when(pl.program_id(0) == 0)
        def _():
            dgain_ref[...] = jnp.zeros_like(dgain_ref)

        dgain_ref[...] += jnp.sum(dh * xhat, axis=0, keepdims=True)

    row = pl.BlockSpec((tm, d), lambda i: (i, 0))
    return pl.pallas_call(
        body, grid=(n_tokens // tm,),
        in_specs=([row, _resident(gain.shape), row]
                  + [pl.BlockSpec((tm, dz.shape[1]), lambda i: (i, 0)) for dz in dzs] + [_resident(wt.shape)]),
        out_specs=(row, pl.BlockSpec(gain.shape, lambda i: (0, 0)), row),
        out_shape=(jax.ShapeDtypeStruct(x.shape, F32), jax.ShapeDtypeStruct(gain.shape, F32),
                   jax.ShapeDtypeStruct(x.shape, BF16)),
        name="in_proj_bwd_rows", compiler_params=_params(1),
    )(x, gain, dres, *dzs, wt)


def _in_proj_bwd_weight(dwt, h, dz, lo, hi, name):
    n_tokens, d = h.shape
    width = hi - lo
    tn = _tile(width, 512) if width >= LANES else width
    dz_tile = max(tn, LANES)

    def body(dwt_ref, h_ref, dz_ref, o_ref):
        o_ref[...] = lax.dot_general(dz_ref[:, :tn].astype(BF16), h_ref[...], TN, preferred_element_type=F32)

    return pl.pallas_call(
        body, grid=(width // tn,),
        in_specs=[ANY, _resident(h.shape), pl.BlockSpec((n_tokens, dz_tile), lambda j: (0, j))],
        out_specs=pl.BlockSpec((pl.Element(tn), pl.Element(d)), lambda j: (pl.multiple_of(lo + j * tn, 16), 0)),
        out_shape=jax.ShapeDtypeStruct(dwt.shape, F32), input_output_aliases={0: 0}, name=name,
        compiler_params=_params(1),
    )(dwt, h, dz)


def _split_small(z):
    return z[:, :GDN_HEADS], z[:, GDN_HEADS:2 * GDN_HEADS]


def _heads3(q, k, v):
    return _to_heads(q), _to_heads(k), _to_heads(v)


def _tokens6(o, lse):
    return tuple(_from_heads(o)) + tuple(_from_heads(lse))


def mixer_forward(x1, w, small):
    n_tokens = x1.shape[0]
    proj = dict(zip(IN_NAMES, _in_proj_fwd(x1, small["mix_norm"], w["w_in_t"])))
    cos, sin = _rope_tables(n_tokens)
    q_rot = _rope_call(proj["wq_a"], cos, sin, "rope_q")
    k_rot = _rope_call(proj["wk_a"], cos, sin, "rope_k")
    (qh, kh, vh), heads_vjp = jax.vjp(_heads3, q_rot, k_rot, proj["wv_a"])
    o, lse = _attn_fwd(qh, kh, vh)
    per_group, tokens_vjp = jax.vjp(_tokens6, o, lse)
    ya = _rowwise_fwd(_combine_fn, "combine", per_group, (), (), 512, 1)[0]
    pa = _matmul(ya, w["w_branch_a"], name="branch_a")
    qkv = _conv_fwd(proj["w_qkvb"], small["gdn_conv_w"])
    raw, small_vjp = jax.vjp(_split_small, proj["w_small"])
    gdn_params = (small["gdn_a_log"], small["gdn_dt_bias"])
    beta, gcum = _rowwise_fwd(_beta_decay_fn, "beta_decay", raw, (), gdn_params, 512, 1)
    ob, states = _gdn_fwd(qkv, beta, gcum)
    gate_in = (ob, proj["w_ggate"])
    yb = _rowwise_fwd(_outnorm_gate_fn, "outnorm_gate", gate_in, (), (small["gdn_out_norm"],), 512, GDN_HEADS)[0]
    pb = _matmul(yb, w["w_branch_b"], name="branch_b")
    merge_in = (proj["w_gatea"], proj["w_gateb"], pa, pb)
    merged = _rowwise_fwd(_merge_fn, "merge", merge_in, (), (), 256, 1)[0]
    x2 = _matmul(merged, w["w_out"], name="out", res=x1)
    saved = dict(x1=x1, proj=proj, cos=cos, sin=sin, heads_vjp=heads_vjp, heads=(qh, kh, vh), tokens_vjp=tokens_vjp,
                 per_group=per_group, ya=ya, qkv=qkv, raw=raw, small_vjp=small_vjp, beta=beta, gcum=gcum, states=states,
                 gate_in=gate_in, yb=yb, merge_in=merge_in, merged=merged)
    return x2, saved


def mixer_backward(dx2, s, w, small):
    proj = s["proj"]
    dmerged = _matmul(dx2, w["w_out"], name="out_da", tb=True)
    grads = dict(w_out=_matmul(s["merged"], dx2, name="out_dw", ta=True))
    dgate_a, dgate_b, dpa, dpb = _rowwise_bwd(_merge_fn, "merge_bwd", s["merge_in"], (), (), (dmerged,), 256, 1)
    dyb = _matmul(dpb, w["w_branch_b"], name="branch_b_da", tb=True)
    grads["w_branch_b"] = _matmul(s["yb"], dpb, name="branch_b_dw", ta=True)
    dya = _matmul(dpa, w["w_branch_a"], name="branch_a_da", tb=True)
    grads["w_branch_a"] = _matmul(s["ya"], dpa, name="branch_a_dw", ta=True)
    dob, dggate, grads["gdn_out_norm"] = _rowwise_bwd(
        _outnorm_gate_fn, "outnorm_gate_bwd", s["gate_in"], (), (small["gdn_out_norm"],), (dyb,), 512, GDN_HEADS)
    dqkv, dbeta, dgcum = _gdn_bwd(s["qkv"], s["beta"], s["gcum"], s["states"], dob)
    gdn_params = (small["gdn_a_log"], small["gdn_dt_bias"])
    dbeta_raw, ddecay_raw, grads["gdn_a_log"], grads["gdn_dt_bias"] = _rowwise_bwd(
        _beta_decay_fn, "beta_decay_bwd", s["raw"], (), gdn_params, (dbeta, dgcum), 512, 1)
    dsmall = s["small_vjp"]((dbeta_raw, ddecay_raw))[0]
    dqkvb, grads["gdn_conv_w"] = _conv_bwd(proj["w_qkvb"], small["gdn_conv_w"], dqkv)
    dper_group = _rowwise_bwd(_combine_fn, "combine_bwd", s["per_group"], (), (), (dya,), 512, 1)
    do, dlse = s["tokens_vjp"](tuple(dper_group))
    dqh, dkh, dvh = _attn_bwd(*s["heads"], do, dlse)
    dq_rot, dk_rot, dv = s["heads_vjp"]((dqh, dkh, dvh))
    dq = _rope_call(dq_rot, s["cos"], -s["sin"], "rope_q_bwd")
    dk = _rope_call(dk_rot, s["cos"], -s["sin"], "rope_k_bwd")
    dzs = (dq, dk, dv, dqkvb, dsmall, dggate, dgate_a, dgate_b)
    dx1, grads["mix_norm"], h = _in_proj_bwd_rows(s["x1"], small["mix_norm"], dx2, dzs, w["w_in_t"])
    dwt = lax.empty(w["w_in_t"].shape, F32)
    for (name, lo, hi), dz in zip(IN_PIECES, dzs):
        dwt = _in_proj_bwd_weight(dwt, h, dz, lo, hi, "in_proj_dw_" + name)
    grads["w_in_t"] = dwt
    return dx1, grads


def ffn_forward(x, gain, w, tag):
    out, g, u = _ffn_fwd(x, gain, w[tag + "_w_gate"], w[tag + "_w_up"], w[tag + "_w_down"], tag + "_fwd")
    return out, (x, g, u)


def ffn_backward(dy, saved, gain, w, tag):
    x, g, u = saved
    weights = (w[tag + "_w_gate"], w[tag + "_w_up"], w[tag + "_w_down"])
    dx, dgain, h, dyh, a, dg, du = _ffn_bwd_rows(x, gain, dy, g, u, *weights, tag + "_bwd_rows")
    return dx, dgain, _ffn_bwd_weights(h, dyh, a, dg, du, tag + "_bwd_weights")


def loss_head(x3, target, gain):
    row_loss = _rowwise_fwd(_loss_fn, "loss", (x3,), (target,), (gain,), 256, 1)[0]
    dx3, dgain = _rowwise_bwd(_loss_fn, "loss_bwd", (x3,), (target,), (gain,), (jnp.ones_like(row_loss),), 256, 1)
    return jnp.sum(row_loss), dx3, dgain


BIG_WEIGHTS = ("ffn1_w_gate", "ffn1_w_up", "ffn1_w_down", "w_in", "w_branch_a", "w_branch_b", "w_out",
               "ffn2_w_gate", "ffn2_w_up", "ffn2_w_down")
TRANSPOSED = ("ffn1_w_gate", "ffn1_w_up", "w_in", "ffn2_w_gate", "ffn2_w_up")
CONV_SHARD = (GDN_CONV, 3 * GDN_WIDTH // N_DEV)
SMALL_ROWS = 24
ANY = pl.BlockSpec(memory_space=pl.ANY)


TOKEN = jax.ShapeDtypeStruct((8, LANES), F32)


def _after(value, token):
    return value + token[0, 0].astype(value.dtype)


def _position():
    return lax.axis_index("x"), lax.axis_index("y"), lax.axis_index("c")


def all_gather_shards(shards, name):
    n = len(shards)

    def body(*refs):
        x_refs, out_refs = refs[:n], refs[n:2 * n]
        send_sems, recv_sems, local_sems = refs[2 * n + 1:]
        x, y, c = _position()
        me, sibling = (x, y, c), (x, y, 1 - c)
        chips = [(1 - x, y), (x, 1 - y), (1 - x, 1 - y)]

        def slab(a, px, py, pc):
            return out_refs[a].at[4 * px + 2 * py + pc]

        def copy(a, k, block, to, src=None):
            return pltpu.make_async_remote_copy(
                src_ref=slab(a, *block) if src is None else src, dst_ref=slab(a, *block),
                send_sem=send_sems.at[7 * a + k], recv_sem=recv_sems.at[7 * a + k], device_id=to, device_id_type=MESH)

        mine = [pltpu.make_async_copy(x_refs[a], slab(a, *me), local_sems.at[a]) for a in range(n)]
        for cp in mine:
            cp.start()
        first = []
        for j, chip in enumerate(chips):
            first += [copy(a, 1 + j, me, (*chip, c), src=x_refs[a]) for a in range(n)]
        first += [copy(a, 0, me, sibling, src=x_refs[a]) for a in range(n)]
        for cp in first:
            cp.start()
        passed = []
        for j, chip in enumerate(chips):
            for a in range(n):
                copy(a, 1 + j, (*chip, c), me).wait_recv()
                cp = copy(a, 4 + j, (*chip, c), sibling)
                cp.start()
                passed.append(cp)
        for a in range(n):
            copy(a, 0, sibling, me).wait_recv()
        for j, chip in enumerate(chips):
            for a in range(n):
                copy(a, 4 + j, (*chip, 1 - c), me).wait_recv()
        for cp in first + passed:
            cp.wait_send()
        for cp in mine:
            cp.wait()
        refs[2 * n][...] = jnp.zeros_like(refs[2 * n])

    outs = pl.pallas_call(
        body, out_shape=tuple(jax.ShapeDtypeStruct((N_DEV,) + s.shape, s.dtype) for s in shards) + (TOKEN,),
        in_specs=[ANY] * n, out_specs=(ANY,) * n + (pl.BlockSpec(memory_space=pltpu.VMEM),),
        scratch_shapes=[pltpu.SemaphoreType.DMA((7 * n,)), pltpu.SemaphoreType.DMA((7 * n,)),
                        pltpu.SemaphoreType.DMA((n,))],
        name=name,
    )(*shards)
    return outs[:n], outs[n]


def exchange_with_sibling(grads):
    n = len(grads)

    def body(*refs):
        g_refs, recv_refs = refs[:n], refs[n:2 * n]
        send_sems, recv_sems = refs[2 * n:]
        x, y, c = _position()
        copies = [pltpu.make_async_remote_copy(
            src_ref=g_refs[a].at[2 * k + 1 - c], dst_ref=recv_refs[a].at[k], send_sem=send_sems.at[4 * a + k],
            recv_sem=recv_sems.at[4 * a + k], device_id=(x, y, 1 - c), device_id_type=MESH)
            for k in range(4) for a in range(n)]
        for cp in copies:
            cp.start()
        for cp in copies:
            cp.wait()

    return pl.pallas_call(
        body, out_shape=tuple(jax.ShapeDtypeStruct((4,) + g.shape[1:], g.dtype) for g in grads),
        in_specs=[ANY] * n, out_specs=(ANY,) * n,
        scratch_shapes=[pltpu.SemaphoreType.DMA((4 * n,)), pltpu.SemaphoreType.DMA((4 * n,))], name="rs_sibling",
    )(*grads)


ELEMENTWISE_TILE_BYTES = 1536 * 1024


def _tile2(rows, cols):
    if rows % 256 == 0:
        return 256, cols
    if rows * cols * 4 > ELEMENTWISE_TILE_BYTES and cols % 256 == 0:
        return rows, 256
    return rows, cols


def add_sibling(grads, received, core, name):
    _, rows, width = grads.shape
    tr, tc = _tile2(rows, width)

    def body(c_ref, g_ref, r_ref, o_ref):
        o_ref[...] = (g_ref[...] + r_ref[...]).astype(BF16)

    blk = (1, tr, tc)
    return pl.pallas_call(
        body,
        grid_spec=pltpu.PrefetchScalarGridSpec(
            num_scalar_prefetch=1, grid=(4, rows // tr, width // tc),
            in_specs=[pl.BlockSpec(blk, lambda k, i, j, c_ref: (2 * k + c_ref[0], i, j)),
                      pl.BlockSpec(blk, lambda k, i, j, c_ref: (k, i, j))],
            out_specs=pl.BlockSpec(blk, lambda k, i, j, c_ref: (k, i, j))),
        out_shape=jax.ShapeDtypeStruct((4, rows, width), BF16), name=name, compiler_params=_params(3),
    )(core, grads, received)


def exchange_with_chips(partials):
    n = len(partials)

    def body(*refs):
        p_refs, recv_refs = refs[:n], refs[n:2 * n]
        send_sems, recv_sems = refs[2 * n:]
        x, y, c = _position()
        chips = [(1 - x, y), (x, 1 - y), (1 - x, 1 - y)]
        copies = [pltpu.make_async_remote_copy(
            src_ref=p_refs[a].at[2 * cx + cy], dst_ref=recv_refs[a].at[j], send_sem=send_sems.at[3 * a + j],
            recv_sem=recv_sems.at[3 * a + j], device_id=(cx, cy, c), device_id_type=MESH)
            for a in range(n) for j, (cx, cy) in enumerate(chips)]
        for cp in copies:
            cp.start()
        for cp in copies:
            cp.wait()

    return pl.pallas_call(
        body, out_shape=tuple(jax.ShapeDtypeStruct((3,) + p.shape[1:], p.dtype) for p in partials),
        in_specs=[ANY] * n, out_specs=(ANY,) * n,
        scratch_shapes=[pltpu.SemaphoreType.DMA((3 * n,)), pltpu.SemaphoreType.DMA((3 * n,))], name="rs_chips",
    )(*partials)


HBM = pl.BlockSpec(memory_space=pltpu.HBM)
SEM = pl.BlockSpec(memory_space=pltpu.SEMAPHORE)
DATAFLOW_EFFECT = pltpu.SideEffectType.DATAFLOW_SIDE_EFFECTING
N_PEERS = N_DEV - 1


def _peer(mask):
    x, y, c = _position()
    px = 1 - x if mask & 4 else x
    py = 1 - y if mask & 2 else y
    pc = 1 - c if mask & 1 else c
    return (px, py, pc), 4 * px + 2 * py + pc


def _direct_copies(src_refs, land_refs, send_sems, recv_sems, scatter):
    x, y, c = _position()
    me = 4 * x + 2 * y + c
    copies = []
    for a, (src, land) in enumerate(zip(src_refs, land_refs)):
        for mask in range(1, N_DEV):
            peer, peer_index = _peer(mask)
            k = N_PEERS * a + mask - 1
            copies.append(pltpu.make_async_remote_copy(
                src_ref=src.at[peer_index] if scatter else src,
                dst_ref=land.at[mask - 1] if scatter else land.at[me],
                send_sem=send_sems.at[k], recv_sem=recv_sems.at[k], device_id=peer, device_id_type=MESH))
    return copies


def direct_exchange_start(arrays, scatter, name):
    n = len(arrays)
    slabs = N_PEERS if scatter else N_DEV
    lands = [lax.empty((slabs,) + (a.shape[1:] if scatter else a.shape), a.dtype) for a in arrays]

    def body(*refs):
        src_refs, land_refs = refs[:n], refs[n:2 * n]
        send_sems, recv_sems = refs[2 * n], refs[2 * n + 1]
        token = refs[-1]
        for cp in _direct_copies(src_refs, land_refs, send_sems, recv_sems, scatter):
            cp.start()
        token[...] = jnp.zeros_like(token)

    sems = pltpu.SemaphoreType.DMA((N_PEERS * n,))
    outs = pl.pallas_call(
        body, name=name,
        out_shape=(sems, sems) + tuple(pltpu.HBM(a.shape, a.dtype) for a in arrays)
        + tuple(pltpu.HBM(l.shape, l.dtype) for l in lands) + (TOKEN,),
        in_specs=[HBM] * (2 * n), out_specs=(SEM, SEM) + (HBM,) * (2 * n) + (pl.BlockSpec(memory_space=pltpu.VMEM),),
        input_output_aliases={i: 2 + i for i in range(2 * n)},
        compiler_params=pltpu.CompilerParams(has_side_effects=DATAFLOW_EFFECT),
    )(*[pltpu.with_memory_space_constraint(a, pltpu.HBM) for a in list(arrays) + lands])
    return outs[0], outs[1], outs[2:2 + n], outs[2 + n:2 + 2 * n], outs[-1]


def direct_exchange_wait(send_sems, recv_sems, arrays, lands, after, scatter, name):
    n = len(arrays)

    def body(*refs):
        src_refs, land_refs = refs[:n], refs[n:2 * n]
        send_sems, recv_sems = refs[2 * n], refs[2 * n + 1]
        for cp in _direct_copies(src_refs, land_refs, send_sems, recv_sems, scatter):
            cp.wait_send()
            cp.wait_recv()

    outs = pl.pallas_call(
        body, name=name,
        out_shape=tuple(pltpu.HBM(a.shape, a.dtype) for a in arrays) + tuple(pltpu.HBM(l.shape, l.dtype) for l in lands),
        in_specs=[HBM] * (2 * n) + [SEM, SEM, pl.BlockSpec(memory_space=pl.ANY)], out_specs=(HBM,) * (2 * n),
        input_output_aliases={i: i for i in range(2 * n)},
        compiler_params=pltpu.CompilerParams(has_side_effects=DATAFLOW_EFFECT),
    )(*arrays, *lands, send_sems, recv_sems, after)
    return outs[n:]


def adamw_direct(w, m, v, grads, received, me, name):
    rows, cols = w.shape[-2:]
    tr, tc = _tile2(rows, cols)

    def body(me_ref, w_ref, m_ref, v_ref, own_ref, r_ref, g_ref, d_ref, nm_ref, nv_ref):
        gv = own_ref[0]
        for j in range(N_PEERS):
            gv = gv + r_ref[j].astype(F32)
        nm = ADAM_B1 * m_ref[0] + (1.0 - ADAM_B1) * gv
        nv = ADAM_B2 * v_ref[0] + (1.0 - ADAM_B2) * (gv * gv)
        m_hat = nm / (1.0 - ADAM_B1 ** ADAM_STEP)
        v_hat = nv / (1.0 - ADAM_B2 ** ADAM_STEP)
        g_ref[0] = gv
        d_ref[0] = -ADAM_LR * (m_hat / (jnp.sqrt(v_hat) + ADAM_EPS) + ADAM_WD * w_ref[0])
        nm_ref[0] = nm
        nv_ref[0] = nv

    one = pl.BlockSpec((1, tr, tc), lambda i, j, me_ref: (0, i, j))
    out = jax.ShapeDtypeStruct((1, rows, cols), F32)
    return pl.pallas_call(
        body,
        grid_spec=pltpu.PrefetchScalarGridSpec(
            num_scalar_prefetch=1, grid=(rows // tr, cols // tc),
            in_specs=[one, one, one, pl.BlockSpec((1, tr, tc), lambda i, j, me_ref: (me_ref[0], i, j)),
                      pl.BlockSpec((N_PEERS, tr, tc), lambda i, j, me_ref: (0, i, j))],
            out_specs=(one,) * 4),
        out_shape=(out,) * 4, name=name, compiler_params=_params(2),
    )(me, w, m, v, grads, received)


def all_reduce_small(vals):
    rows, width = vals.shape

    def body(x_ref, out_ref, all_ref, send_sems, recv_sems):
        x, y, c = _position()
        me, sibling = (x, y, c), (x, y, 1 - c)
        chips = [(1 - x, y), (x, 1 - y), (1 - x, 1 - y)]

        def slab(px, py, pc):
            return all_ref.at[4 * px + 2 * py + pc]

        def copy(k, block, to, src=None):
            return pltpu.make_async_remote_copy(
                src_ref=slab(*block) if src is None else src, dst_ref=slab(*block),
                send_sem=send_sems.at[k], recv_sem=recv_sems.at[k], device_id=to, device_id_type=MESH)

        first = [copy(0, me, sibling, src=x_ref)]
        first += [copy(1 + j, me, (*chip, c), src=x_ref) for j, chip in enumerate(chips)]
        for cp in first:
            cp.start()
        all_ref[4 * x + 2 * y + c] = x_ref[...]
        passed = [copy(4 + j, (*chip, c), sibling) for j, chip in enumerate(chips)]
        for j, chip in enumerate(chips):
            copy(1 + j, (*chip, c), me).wait_recv()
            passed[j].start()
        copy(0, sibling, me).wait_recv()
        for j, chip in enumerate(chips):
            copy(4 + j, (*chip, 1 - c), me).wait_recv()
        for cp in first + passed:
            cp.wait_send()
        total = all_ref[0]
        for d in range(1, N_DEV):
            total = total + all_ref[d]
        out_ref[...] = total

    vmem = pl.BlockSpec(memory_space=pltpu.VMEM)
    return pl.pallas_call(
        body, out_shape=(jax.ShapeDtypeStruct(vals.shape, F32), jax.ShapeDtypeStruct((N_DEV, rows, width), F32)),
        in_specs=[vmem], out_specs=(vmem, vmem),
        scratch_shapes=[pltpu.SemaphoreType.DMA((7,)), pltpu.SemaphoreType.DMA((7,))], name="small_allreduce",
    )(vals)[0]


def adamw(w, g, m, v, name):
    shape = w.shape
    w2, g2, m2, v2 = [a.reshape((-1, shape[-1])) for a in (w, g, m, v)]
    rows, cols = w2.shape
    tr = 256 if rows % 256 == 0 else rows

    def body(w_ref, g_ref, m_ref, v_ref, d_ref, nm_ref, nv_ref):
        gv = g_ref[...]
        nm = ADAM_B1 * m_ref[...] + (1.0 - ADAM_B1) * gv
        nv = ADAM_B2 * v_ref[...] + (1.0 - ADAM_B2) * (gv * gv)
        m_hat = nm / (1.0 - ADAM_B1 ** ADAM_STEP)
        v_hat = nv / (1.0 - ADAM_B2 ** ADAM_STEP)
        d_ref[...] = -ADAM_LR * (m_hat / (jnp.sqrt(v_hat) + ADAM_EPS) + ADAM_WD * w_ref[...])
        nm_ref[...] = nm
        nv_ref[...] = nv

    blk = pl.BlockSpec((tr, cols), lambda i: (i, 0))
    out = jax.ShapeDtypeStruct((rows, cols), F32)
    outs = pl.pallas_call(
        body, grid=(rows // tr,), in_specs=[blk] * 4, out_specs=(blk,) * 3, out_shape=(out,) * 3,
        name=name, compiler_params=_params(1),
    )(w2, g2, m2, v2)
    return tuple(o.reshape(shape) for o in outs)


def adamw_summed(w, m, v, grads, from_sibling, received, me, name):
    rows, cols = w.shape[-2:]
    tr, tc = _tile2(rows, cols)

    def body(me_ref, w_ref, m_ref, v_ref, own_ref, sib_ref, r_ref, g_ref, d_ref, nm_ref, nv_ref):
        gv = own_ref[0] + sib_ref[0]
        for j in range(3):
            gv = gv + r_ref[j].astype(F32)
        nm = ADAM_B1 * m_ref[0] + (1.0 - ADAM_B1) * gv
        nv = ADAM_B2 * v_ref[0] + (1.0 - ADAM_B2) * (gv * gv)
        m_hat = nm / (1.0 - ADAM_B1 ** ADAM_STEP)
        v_hat = nv / (1.0 - ADAM_B2 ** ADAM_STEP)
        g_ref[0] = gv
        d_ref[0] = -ADAM_LR * (m_hat / (jnp.sqrt(v_hat) + ADAM_EPS) + ADAM_WD * w_ref[0])
        nm_ref[0] = nm
        nv_ref[0] = nv

    one = pl.BlockSpec((1, tr, tc), lambda i, j, me_ref: (0, i, j))
    out = jax.ShapeDtypeStruct((1, rows, cols), F32)
    return pl.pallas_call(
        body,
        grid_spec=pltpu.PrefetchScalarGridSpec(
            num_scalar_prefetch=1, grid=(rows // tr, cols // tc),
            in_specs=[one, one, one, pl.BlockSpec((1, tr, tc), lambda i, j, me_ref: (me_ref[0], i, j)),
                      pl.BlockSpec((1, tr, tc), lambda i, j, me_ref: (me_ref[1], i, j)),
                      pl.BlockSpec((3, tr, tc), lambda i, j, me_ref: (0, i, j))],
            out_specs=(one,) * 4),
        out_shape=(out,) * 4, name=name, compiler_params=_params(2),
    )(me, w, m, v, grads, from_sibling, received)


SMALL_VECTORS = ("ffn1_norm", "mix_norm", "ffn2_norm", "final_norm")


def _pack_small(gs):
    row = jnp.concatenate([gs["gdn_a_log"].reshape(-1), gs["gdn_dt_bias"].reshape(-1), gs["gdn_out_norm"].reshape(-1)])
    rows = [gs[n].reshape(1, D_MODEL) for n in SMALL_VECTORS]
    rows.append(jnp.pad(row, (0, D_MODEL - row.shape[0])).reshape(1, D_MODEL))
    rows.append(gs["gdn_conv_w"].reshape(-1, D_MODEL))
    packed = jnp.concatenate(rows, axis=0)
    return jnp.pad(packed, ((0, SMALL_ROWS - packed.shape[0]), (0, 0)))


def _unpack_small(packed):
    out = {n: packed[i].reshape(1, D_MODEL) for i, n in enumerate(SMALL_VECTORS)}
    row = packed[len(SMALL_VECTORS)]
    out["gdn_a_log"] = row[:GDN_HEADS].reshape(1, GDN_HEADS)
    out["gdn_dt_bias"] = row[GDN_HEADS:2 * GDN_HEADS].reshape(1, GDN_HEADS)
    out["gdn_out_norm"] = row[2 * GDN_HEADS:2 * GDN_HEADS + GDN_HEAD_DIM].reshape(1, GDN_HEAD_DIM)
    first = len(SMALL_VECTORS) + 1
    out["gdn_conv_w"] = packed[first:first + GDN_CONV * 3].reshape(GDN_CONV, 3 * GDN_WIDTH)
    return out


WEIGHTS = ("ffn1_norm", "ffn1_w_gate", "ffn1_w_up", "ffn1_w_down", "mix_norm", "w_in", "gdn_conv_w", "gdn_a_log",
           "gdn_dt_bias", "gdn_out_norm", "w_branch_a", "w_branch_b", "w_out", "ffn2_norm", "ffn2_w_gate",
           "ffn2_w_up", "ffn2_w_down", "final_norm")


def kernel(x, ffn1_norm, ffn1_w_gate, ffn1_w_up, ffn1_w_down, mix_norm, w_in, gdn_conv_w, gdn_a_log, gdn_dt_bias, gdn_out_norm, w_branch_a, w_branch_b, w_out, ffn2_norm, ffn2_w_gate, ffn2_w_up, ffn2_w_down, final_norm, loss_target, m_ffn1_norm, m_ffn1_w_gate, m_ffn1_w_up, m_ffn1_w_down, m_mix_norm, m_w_in, m_gdn_conv_w, m_gdn_a_log, m_gdn_dt_bias, m_gdn_out_norm, m_w_branch_a, m_w_branch_b, m_w_out, m_ffn2_norm, m_ffn2_w_gate, m_ffn2_w_up, m_ffn2_w_down, m_final_norm, v_ffn1_norm, v_ffn1_w_gate, v_ffn1_w_up, v_ffn1_w_down, v_mix_norm, v_w_in, v_gdn_conv_w, v_gdn_a_log, v_gdn_dt_bias, v_gdn_out_norm, v_w_branch_a, v_w_branch_b, v_w_out, v_ffn2_norm, v_ffn2_w_gate, v_ffn2_w_up, v_ffn2_w_down, v_final_norm):
    given = dict(locals())
    px, py, pc = _position()
    big_names = list(BIG_WEIGHTS)

    def shard_view(a, n):
        return a.transpose(0, 2, 1) if n in TRANSPOSED else a

    me = 4 * px + 2 * py + pc
    me_index = me.astype(jnp.int32).reshape(1)
    late = [n for n in big_names if n.startswith("ffn2")]
    early = [n for n in big_names if n not in late]
    shards = {n: shard_view(given[n], n)[0].astype(BF16) for n in big_names}
    early_slabs, early_done = all_gather_shards([shards[n] for n in early] + [gdn_conv_w[0]], "gather_weights")
    gathered = dict(zip(early + ["gdn_conv_w"], early_slabs))
    late_gather = direct_exchange_start([_after(shards[n], early_done) for n in late], False, "gather_ffn2_start")
    ffn1_norm = _after(ffn1_norm, late_gather[4])
    w = {n: gathered[n] for n in early if n.startswith("ffn")}
    w["w_in_t"] = gathered["w_in"].reshape(-1, D_MODEL)
    w["w_branch_a"] = gathered["w_branch_a"].transpose(1, 0, 2).reshape(256, D_MODEL)
    w["w_branch_b"] = gathered["w_branch_b"].reshape(D_MODEL, D_MODEL)
    w["w_out"] = gathered["w_out"].reshape(D_MODEL, D_MODEL)
    conv_full = gathered["gdn_conv_w"].transpose(1, 0, 2).reshape(GDN_CONV, 3 * GDN_WIDTH)
    small = dict(mix_norm=mix_norm, gdn_a_log=gdn_a_log, gdn_dt_bias=gdn_dt_bias, gdn_out_norm=gdn_out_norm,
                 gdn_conv_w=conv_full)

    x1, ffn1_saved = ffn_forward(x[0], ffn1_norm, w, "ffn1")
    x2, mixer_saved = mixer_forward(x1, w, small)
    late_lands = direct_exchange_wait(*late_gather[:4], x2, False, "gather_ffn2_wait")
    for n, land in zip(late, late_lands):
        w[n] = lax.dynamic_update_slice(land, shards[n][None], (me, 0, 0))
    x3, ffn2_saved = ffn_forward(x2, ffn2_norm, w, "ffn2")
    loss_local, dx3, g_final = loss_head(x3, loss_target[0], final_norm.reshape(1, D_MODEL))
    loss = lax.psum(loss_local, ("x", "y", "c"))
    dx2, g_ffn2_norm, dw2 = ffn_backward(dx3, ffn2_saved, ffn2_norm, w, "ffn2")
    late_scatter = direct_exchange_start([g.astype(BF16) for g in dw2], True, "rs_ffn2_start")
    w_after = dict(w, w_out=_after(w["w_out"], late_scatter[4]))
    dx1, g_w = mixer_backward(dx2, mixer_saved, w_after, small)
    g_big = dict(zip(late, dw2))
    g_big["w_in"] = g_w["w_in_t"].reshape(N_DEV, -1, D_MODEL)
    g_big["w_branch_a"] = g_w["w_branch_a"].reshape(256, N_DEV, 128).transpose(1, 0, 2)
    g_big["w_branch_b"] = g_w["w_branch_b"].reshape(N_DEV, 128, D_MODEL)
    g_big["w_out"] = g_w["w_out"].reshape(N_DEV, 128, D_MODEL)
    middle = ["w_in", "w_branch_a", "w_branch_b", "w_out"]
    middle_scatter = direct_exchange_start([g_big[n].astype(BF16) for n in middle], True, "rs_mixer_start")
    grad_x, g_ffn1_norm, dw1 = ffn_backward(dx1, ffn1_saved, _after(ffn1_norm, middle_scatter[4]), w, "ffn1")
    g_small = dict(ffn1_norm=g_ffn1_norm, ffn2_norm=g_ffn2_norm, final_norm=g_final,
                   **{n: g_w[n] for n in ("mix_norm", "gdn_a_log", "gdn_dt_bias", "gdn_out_norm", "gdn_conv_w")})

    first = [n for n in early if n.startswith("ffn1")]
    g_big.update(zip(first, dw1))
    g_list = [g_big[n] for n in first]
    core = pc.astype(jnp.int32).reshape(1)
    me_and_chip = jnp.stack([me, 2 * px + py]).astype(jnp.int32)
    from_sibling = exchange_with_sibling(g_list)
    partials = [add_sibling(g, r, core, "rs_add_" + n) for n, g, r in zip(first, g_list, from_sibling)]
    from_chips = exchange_with_chips(partials)

    def state_of(n):
        return [shard_view(given[p + n], n) for p in ("", "m_", "v_")]

    results = {}
    for n, g, sib, recv in zip(first, g_list, from_sibling, from_chips):
        outs = adamw_summed(*state_of(n), g, sib, recv, me_and_chip, "adamw_" + n)
        results[n] = tuple(shard_view(o, n) for o in outs)
    late_received = direct_exchange_wait(*late_scatter[:4], grad_x, True, "rs_ffn2_wait")
    middle_received = direct_exchange_wait(*middle_scatter[:4], from_chips[0], True, "rs_mixer_wait")
    for n, recv in zip(late + middle, list(late_received) + list(middle_received)):
        outs = adamw_direct(*state_of(n), g_big[n], recv, me_index, "adamw_" + n)
        results[n] = tuple(shard_view(o, n) for o in outs)

    small_sum = _unpack_small(all_reduce_small(_pack_small(g_small)))
    conv_cols = CONV_SHARD[1]
    small_sum["gdn_conv_w"] = lax.dynamic_slice(small_sum["gdn_conv_w"], (0, me * conv_cols), (GDN_CONV, conv_cols))
    for n in WEIGHTS:
        if n not in results:
            g = small_sum[n].reshape(given[n].shape)
            results[n] = (g,) + adamw(given[n], g, given["m_" + n], given["v_" + n], "adamw_" + n)

    outs = [[results[n][i] for n in WEIGHTS] for i in range(4)]
    return (loss, grad_x[None], *outs[0], *outs[1], *outs[2], *outs[3])
```

```python
import jax
import jax.numpy as jnp
from jax import lax
from jax.experimental import pallas as pl
from jax.experimental.pallas import tpu as pltpu

F32 = jnp.float32
BF16 = jnp.bfloat16
HI = lax.Precision.HIGHEST
MESH = pl.DeviceIdType.MESH

N_DEV = 8
D_MODEL = 1024
EPS = 1e-6
ROPE_THETA = 10000.0
DSW_DILATIONS = (1, 4, 16)
DSW_HEADS_PER_GROUP = 4
DSW_HEAD_DIM = 64
DSW_BLOCK = 128
GDN_HEADS = 8
GDN_HEAD_DIM = 128
GDN_WIDTH = 1024
GDN_CONV = 4
GDN_CHUNK = 64

ADAM_LR = 0.001
ADAM_B1 = 0.9
ADAM_B2 = 0.999
ADAM_EPS = 1e-08
ADAM_WD = 0.01
ADAM_STEP = 10

VMEM_LIMIT_BYTES = 56 * 1024 * 1024
LANES = 128

NN = (((1,), (0,)), ((), ()))
NT = (((1,), (1,)), ((), ()))
TN = (((0,), (0,)), ((), ()))


def _params(n_grid):
    return pltpu.CompilerParams(dimension_semantics=("arbitrary",) * n_grid, vmem_limit_bytes=VMEM_LIMIT_BYTES)


def _tile(n, pref):
    best = None
    t = LANES
    while t <= min(n, pref):
        if n % t == 0:
            best = t
        t += LANES
    return n if best is None else best


def _matmul(a, b, *, name, ta=False, tb=False, res=None, scale=1.0):
    K, M = a.shape if ta else a.shape[::-1]
    N = b.shape[0] if tb else b.shape[1]
    assert (b.shape[1] if tb else b.shape[0]) == K, (a.shape, b.shape, ta, tb)
    tm = _tile(M, 512)
    tn = _tile(N, 512)
    dn = (((0 if ta else 1,), (1 if tb else 0,)), ((), ()))

    def body(*refs):
        a_ref, b_ref = refs[:2]
        o_ref = refs[-1]
        acc = lax.dot_general(a_ref[...].astype(BF16), b_ref[...].astype(BF16), dn, preferred_element_type=F32)
        if scale != 1.0:
            acc = acc * scale
        if res is not None:
            acc = refs[2][...] + acc
        o_ref[...] = acc

    a_spec = pl.BlockSpec((K, tm), lambda i, j: (0, i)) if ta else pl.BlockSpec((tm, K), lambda i, j: (i, 0))
    b_spec = pl.BlockSpec((tn, K), lambda i, j: (j, 0)) if tb else pl.BlockSpec((K, tn), lambda i, j: (0, j))
    o_spec = pl.BlockSpec((tm, tn), lambda i, j: (i, j))
    ins, specs = [a, b], [a_spec, b_spec]
    if res is not None:
        ins.append(res)
        specs.append(o_spec)
    return pl.pallas_call(
        body, grid=(M // tm, N // tn), in_specs=specs, out_specs=o_spec,
        out_shape=jax.ShapeDtypeStruct((M, N), F32), name=name, compiler_params=_params(2),
    )(*ins)


def _rw_specs(arrs, tm, nblk):
    return [pl.BlockSpec((tm, a.shape[1] // nblk), lambda i, j: (i, j)) for a in arrs]


def _rowwise_fwd(fn, name, rows, consts, params, tm, nblk):
    n_rows = rows[0].shape[0]
    tm = min(tm, n_rows)
    ins = list(rows) + list(consts)
    avals = [jax.ShapeDtypeStruct((tm, a.shape[1] // nblk), a.dtype) for a in ins]
    avals += [jax.ShapeDtypeStruct(p.shape, p.dtype) for p in params]
    out_avals = jax.eval_shape(fn, *avals)
    n_in = len(ins) + len(params)

    def body(*refs):
        outs = fn(*[r[...] for r in refs[:n_in]])
        for r, o in zip(refs[n_in:], outs):
            r[...] = o.astype(r.dtype)

    return pl.pallas_call(
        body, grid=(n_rows // tm, nblk),
        in_specs=_rw_specs(ins, tm, nblk) + [pl.BlockSpec(p.shape, lambda i, j: (0, 0)) for p in params],
        out_specs=tuple(pl.BlockSpec((tm, o.shape[1]), lambda i, j: (i, j)) for o in out_avals),
        out_shape=tuple(jax.ShapeDtypeStruct((n_rows, o.shape[1] * nblk), o.dtype) for o in out_avals),
        name=name, compiler_params=_params(2),
    )(*ins, *params)


def _rowwise_bwd(fn, name, rows, consts, params, cts, tm, nblk):
    n_rows = rows[0].shape[0]
    tm = min(tm, n_rows)
    nr, nc, npar, nct = len(rows), len(consts), len(params), len(cts)

    def body(*refs):
        rv = [r[...] for r in refs[:nr]]
        cv = [r[...] for r in refs[nr:nr + nc]]
        pv = [r[...] for r in refs[nr + nc:nr + nc + npar]]
        ctv = [r[...] for r in refs[nr + nc + npar:nr + nc + npar + nct]]
        outs = refs[nr + nc + npar + nct:]
        _, vjp = jax.vjp(lambda *d: fn(*d[:nr], *cv, *d[nr:]), *rv, *pv)
        grads = vjp(tuple(ctv))
        for k in range(nr):
            outs[k][...] = grads[k]
        first = jnp.logical_and(pl.program_id(0) == 0, pl.program_id(1) == 0)
        for k in range(npar):
            ref = outs[nr + k]

            @pl.when(first)
            def _(ref=ref):
                ref[...] = jnp.zeros_like(ref)

            ref[...] += grads[nr + k]

    ins = list(rows) + list(consts)
    return pl.pallas_call(
        body, grid=(n_rows // tm, nblk),
        in_specs=(_rw_specs(ins, tm, nblk) + [pl.BlockSpec(p.shape, lambda i, j: (0, 0)) for p in params]
                  + _rw_specs(cts, tm, nblk)),
        out_specs=tuple(_rw_specs(rows, tm, nblk) + [pl.BlockSpec(p.shape, lambda i, j: (0, 0)) for p in params]),
        out_shape=tuple([jax.ShapeDtypeStruct(a.shape, F32) for a in rows]
                        + [jax.ShapeDtypeStruct(p.shape, F32) for p in params]),
        name=name, compiler_params=_params(2),
    )(*ins, *params, *cts)


def _merge_fn(ga, gb, pa, pb):
    return (jax.nn.sigmoid(ga) * pa + jax.nn.sigmoid(gb) * pb,)


def _outnorm_gate_fn(o, gate, gain):
    y = o * lax.rsqrt(jnp.mean(o * o, axis=-1, keepdims=True) + EPS) * gain
    return (y * (gate * jax.nn.sigmoid(gate)),)


def _beta_decay_fn(beta_raw, decay_raw, a_log, dt_bias):
    z = decay_raw + dt_bias
    softplus = jnp.maximum(z, 0.0) + jnp.log(1.0 + jnp.exp(-jnp.abs(z)))
    g = -jnp.exp(a_log) * softplus
    rows = g.shape[0]
    ii = lax.broadcasted_iota(jnp.int32, (rows, rows), 0)
    jj = lax.broadcasted_iota(jnp.int32, (rows, rows), 1)
    same_chunk_before = jnp.logical_and(jj <= ii, jj // GDN_CHUNK == ii // GDN_CHUNK).astype(F32)
    gcum = lax.dot_general(same_chunk_before, g, NN, precision=HI, preferred_element_type=F32)
    return jax.nn.sigmoid(beta_raw), gcum


def _combine_fn(o0, o1, o2, l0, l1, l2):
    m = lax.stop_gradient(jnp.maximum(jnp.maximum(l0, l1), l2))
    e0, e1, e2 = jnp.exp(l0 - m), jnp.exp(l1 - m), jnp.exp(l2 - m)
    return ((e0 * o0 + e1 * o1 + e2 * o2) / (e0 + e1 + e2),)


def _loss_fn(x, target, gain):
    y = x * lax.rsqrt(jnp.mean(x * x, axis=-1, keepdims=True) + EPS) * gain
    err = y - target
    return (0.5 * jnp.mean(err * err, axis=-1, keepdims=True),)


def _rope_call(x, cos, sin, name):
    n_rows, width = x.shape
    tm = 512

    def body(x_ref, c_ref, s_ref, o_ref):
        v = x_ref[...]
        lane = lax.broadcasted_iota(jnp.int32, v.shape, 1)
        low = (lane % DSW_HEAD_DIM) < DSW_HEAD_DIM // 2
        half = DSW_HEAD_DIM // 2
        swapped = jnp.where(low, pltpu.roll(v, LANES - half, 1), pltpu.roll(v, half, 1))
        o_ref[...] = v * c_ref[...] + swapped * s_ref[...]

    tab = pl.BlockSpec((tm, LANES), lambda i, j: (i, 0))
    blk = pl.BlockSpec((tm, LANES), lambda i, j: (i, j))
    return pl.pallas_call(
        body, grid=(n_rows // tm, width // LANES), in_specs=[blk, tab, tab], out_specs=blk,
        out_shape=jax.ShapeDtypeStruct(x.shape, F32), name=name, compiler_params=_params(2),
    )(x, cos, sin)


def _rope_tables(n_tokens):
    half = DSW_HEAD_DIM // 2
    inv_freq = ROPE_THETA ** (-jnp.arange(half, dtype=F32) / half)
    ang = jnp.arange(n_tokens, dtype=F32)[:, None] * inv_freq[None, :]
    cos, sin = jnp.cos(ang), jnp.sin(ang)
    return jnp.tile(jnp.concatenate([cos, cos], 1), (1, 2)), jnp.tile(jnp.concatenate([-sin, sin], 1), (1, 2))


def _attn_probs(q, kp, kc, group, n):
    blk = DSW_BLOCK
    k = _each(lambda a, b: jnp.concatenate([a, b], axis=0).astype(BF16), kp, kc)
    s = _each(lambda a, b: lax.dot_general(a.astype(BF16), b, NT, preferred_element_type=F32)
              * (DSW_HEAD_DIM ** -0.5), q, k)
    blocks_per_seq = jnp.where(group == 0, 16, jnp.where(group == 1, 4, 1))
    first = (n % blocks_per_seq) == 0
    qi = lax.broadcasted_iota(jnp.int32, (blk, 2 * blk), 0)
    kj = lax.broadcasted_iota(jnp.int32, (blk, 2 * blk), 1)
    dist = qi + blk - kj
    valid = (dist >= 0) & (dist <= blk) & jnp.logical_or(kj >= blk, jnp.logical_not(first))
    s = _each(lambda a: jnp.where(valid, a, -1e30), s)
    m = _each(lambda a: jnp.max(a, axis=-1, keepdims=True), s)
    p = _each(lambda a, b: jnp.exp(a - b), s, m)
    l = _each(lambda a: jnp.sum(a, axis=-1, keepdims=True), p)
    return _each(lambda a, b: a / b, p, l), _each(lambda a, b: a + jnp.log(b), m, l), k


def _attn_specs(n_tokens):
    blk, hpg = DSW_BLOCK, DSW_HEADS_PER_GROUP
    cur = pl.BlockSpec((hpg, blk, DSW_HEAD_DIM), lambda g, n: (g, n, 0))
    prev = pl.BlockSpec((hpg, blk, DSW_HEAD_DIM), lambda g, n: (g, jnp.maximum(n - 1, 0), 0))
    return cur, prev


def _attn_fwd(q, k, v):
    nh, n_tokens, hd = q.shape
    hpg = DSW_HEADS_PER_GROUP
    cur, prev = _attn_specs(n_tokens)

    def body(q_ref, kp_ref, kc_ref, vp_ref, vc_ref, o_ref, l_ref):
        heads = range(hpg)
        p, lse, _ = _attn_probs([q_ref[h] for h in heads], [kp_ref[h] for h in heads], [kc_ref[h] for h in heads],
                                pl.program_id(0), pl.program_id(1))
        vv = [jnp.concatenate([vp_ref[h], vc_ref[h]], axis=0).astype(BF16) for h in heads]
        o = _each(lambda a, b: lax.dot_general(a.astype(BF16), b, NN, preferred_element_type=F32), p, vv)
        for h in heads:
            o_ref[h] = o[h]
            l_ref[h] = jnp.broadcast_to(lse[h], (DSW_BLOCK, hd))

    return pl.pallas_call(
        body, grid=(nh // hpg, n_tokens // DSW_BLOCK), in_specs=[cur, prev, cur, prev, cur], out_specs=(cur, cur),
        out_shape=(jax.ShapeDtypeStruct(q.shape, F32), jax.ShapeDtypeStruct(q.shape, F32)),
        name="attn_fwd", compiler_params=_params(2),
    )(q, k, k, v, v)


def _attn_bwd(q, k, v, do, dlse):
    nh, n_tokens, hd = q.shape
    hpg = DSW_HEADS_PER_GROUP
    nblk = n_tokens // DSW_BLOCK
    cur, prev = _attn_specs(n_tokens)
    part = pl.BlockSpec((hpg, 1, 2 * DSW_BLOCK, hd), lambda g, n: (g, n, 0, 0))
    scale = DSW_HEAD_DIM ** -0.5

    def body(q_ref, kp_ref, kc_ref, vp_ref, vc_ref, do_ref, dl_ref, dq_ref, dk_ref, dv_ref):
        heads = range(hpg)
        qs = [q_ref[h] for h in heads]
        p, _, kb = _attn_probs(qs, [kp_ref[h] for h in heads], [kc_ref[h] for h in heads],
                               pl.program_id(0), pl.program_id(1))
        qb = _each(lambda a: a.astype(BF16), qs)
        vv = [jnp.concatenate([vp_ref[h], vc_ref[h]], axis=0).astype(BF16) for h in heads]
        dob = [do_ref[h].astype(BF16) for h in heads]
        dp = _each(lambda a, b: lax.dot_general(a, b, NT, preferred_element_type=F32), dob, vv)
        dv = _each(lambda a, b: lax.dot_general(a.astype(BF16), b, TN, preferred_element_type=F32), p, dob)
        dl = [jnp.sum(dl_ref[h], axis=-1, keepdims=True) for h in heads]
        ds = _each(lambda a, b, c: (a * (b - jnp.sum(b * a, axis=-1, keepdims=True) + c) * scale).astype(BF16),
                   p, dp, dl)
        dq = _each(lambda a, b: lax.dot_general(a, b, NN, preferred_element_type=F32), ds, kb)
        dk = _each(lambda a, b: lax.dot_general(a, b, TN, preferred_element_type=F32), ds, qb)
        for h in heads:
            dq_ref[h] = dq[h]
            dk_ref[h, 0] = dk[h]
            dv_ref[h, 0] = dv[h]

    dq, dkp, dvp = pl.pallas_call(
        body, grid=(nh // hpg, nblk), in_specs=[cur, prev, cur, prev, cur, cur, cur], out_specs=(cur, part, part),
        out_shape=(jax.ShapeDtypeStruct(q.shape, F32),
                   jax.ShapeDtypeStruct((nh, nblk, 2 * DSW_BLOCK, hd), F32),
                   jax.ShapeDtypeStruct((nh, nblk, 2 * DSW_BLOCK, hd), F32)),
        name="attn_bwd", compiler_params=_params(2),
    )(q, k, k, v, v, do, dlse)

    def fold(partial):
        own = partial[:, :, DSW_BLOCK:]
        from_next = jnp.pad(partial[:, 1:, :DSW_BLOCK], ((0, 0), (0, 1), (0, 0), (0, 0)))
        return (own + from_next).reshape(nh, n_tokens, hd)

    return dq, fold(dkp), fold(dvp)


def _to_heads(a):
    n_tokens = a.shape[0]
    outs = []
    for gi, d in enumerate(DSW_DILATIONS):
        blk = a[:, gi * 256:(gi + 1) * 256].reshape(n_tokens // d, d, DSW_HEADS_PER_GROUP, DSW_HEAD_DIM)
        outs.append(blk.transpose(2, 1, 0, 3).reshape(DSW_HEADS_PER_GROUP, n_tokens, DSW_HEAD_DIM))
    return jnp.concatenate(outs, 0)


def _from_heads(a):
    n_tokens = a.shape[1]
    outs = []
    for gi, d in enumerate(DSW_DILATIONS):
        blk = a[gi * 4:(gi + 1) * 4].reshape(DSW_HEADS_PER_GROUP, d, n_tokens // d, DSW_HEAD_DIM)
        outs.append(blk.transpose(2, 1, 0, 3).reshape(n_tokens, DSW_HEADS_PER_GROUP * DSW_HEAD_DIM))
    return outs


CONV_TILE = 512


def _shift_down(x, k, rows):
    return x if k == 0 else jnp.where(rows >= k, pltpu.roll(x, k, 0), 0.0)


def _shift_up(x, k, rows):
    n = x.shape[0]
    return x if k == 0 else jnp.where(rows < n - k, pltpu.roll(x, n - k, 0), 0.0)


def _conv_pre(x, w):
    rows = lax.broadcasted_iota(jnp.int32, x.shape, 0)
    acc = x * w[GDN_CONV - 1:GDN_CONV]
    for k in range(1, GDN_CONV):
        acc = acc + _shift_down(x, k, rows) * w[GDN_CONV - 1 - k:GDN_CONV - k]
    return acc, rows


def _conv_fwd(x, w):
    n_tokens, width = x.shape
    big = pl.BlockSpec((n_tokens, CONV_TILE), lambda j: (0, j))
    wsp = pl.BlockSpec((GDN_CONV, CONV_TILE), lambda j: (0, j))

    def body(x_ref, w_ref, o_ref):
        acc, _ = _conv_pre(x_ref[...], w_ref[...])
        o_ref[...] = acc * jax.nn.sigmoid(acc)

    return pl.pallas_call(
        body, grid=(width // CONV_TILE,), in_specs=[big, wsp], out_specs=big,
        out_shape=jax.ShapeDtypeStruct(x.shape, F32), name="conv_fwd", compiler_params=_params(1),
    )(x, w)


def _conv_bwd(x, w, dy):
    n_tokens, width = x.shape
    big = pl.BlockSpec((n_tokens, CONV_TILE), lambda j: (0, j))
    wsp = pl.BlockSpec((GDN_CONV, CONV_TILE), lambda j: (0, j))

    def body(x_ref, w_ref, dy_ref, dx_ref, dw_ref):
        xv, wv = x_ref[...], w_ref[...]
        acc, rows = _conv_pre(xv, wv)
        sg = jax.nn.sigmoid(acc)
        dacc = dy_ref[...] * (sg + acc * sg * (1.0 - sg))
        dx = dacc * wv[GDN_CONV - 1:GDN_CONV]
        for k in range(1, GDN_CONV):
            dx = dx + _shift_up(dacc, k, rows) * wv[GDN_CONV - 1 - k:GDN_CONV - k]
        dx_ref[...] = dx
        for k in range(GDN_CONV):
            dw_ref[GDN_CONV - 1 - k:GDN_CONV - k, :] = jnp.sum(dacc * _shift_down(xv, k, rows), axis=0, keepdims=True)

    return pl.pallas_call(
        body, grid=(width // CONV_TILE,), in_specs=[big, wsp, big], out_specs=(big, wsp),
        out_shape=(jax.ShapeDtypeStruct(x.shape, F32), jax.ShapeDtypeStruct(w.shape, F32)),
        name="conv_bwd", compiler_params=_params(1),
    )(x, w, dy)


def _dot(a, b, dn=NN):
    return lax.dot_general(a, b, dn, precision=HI, preferred_element_type=F32)


def _dot3(a, b, dn=NN):
    return lax.dot_general(a, b, dn, precision=lax.Precision.HIGH, preferred_element_type=F32)


def _bf16_dot(a, b, dn):
    return lax.dot_general(a.astype(BF16), b.astype(BF16), dn, preferred_element_type=F32)


_DOT_GRADS = {NN: (("g", "b", NT), ("a", "g", TN)), NT: (("g", "b", NN), ("g", "a", TN)),
              TN: (("b", "g", NT), ("a", "g", NN))}


def _make_bdot(dn):
    @jax.custom_vjp
    def op(a, b):
        return _bf16_dot(a, b, dn)

    def fwd(a, b):
        return op(a, b), (a, b)

    def bwd(saved, g):
        vals = dict(a=saved[0], b=saved[1], g=g)
        return tuple(_bf16_dot(vals[x], vals[y], form) for x, y, form in _DOT_GRADS[dn])

    op.defvjp(fwd, bwd)
    return op


_BDOTS = {dn: _make_bdot(dn) for dn in (NN, NT, TN)}


def _bdot(a, b, dn=NN):
    return _BDOTS[dn](a, b)


def _each(fn, *lists):
    return [fn(*items) for items in zip(*lists)]


def _gdn_chunks(q, k, v, b, gcum, state):
    c = GDN_CHUNK
    ii = lax.broadcasted_iota(jnp.int32, (c, c), 0)
    jj = lax.broadcasted_iota(jnp.int32, (c, c), 1)
    eye = (ii == jj).astype(F32)
    qn = _each(lambda x: x * lax.rsqrt(jnp.sum(x * x, axis=-1, keepdims=True) + EPS) * (GDN_HEAD_DIM ** -0.5), q)
    kn = _each(lambda x: x * lax.rsqrt(jnp.sum(x * x, axis=-1, keepdims=True) + EPS), k)
    gcum_i = _each(lambda x: jnp.broadcast_to(x, (c, c)), gcum)
    gcum_j = _each(jnp.transpose, gcum_i)
    decay = _each(lambda x, y: jnp.exp(jnp.where(jj <= ii, x - y, -1e30)), gcum_i, gcum_j)
    g_last = _each(lambda x: x[c - 1:c, :], gcum)
    e_gcum = _each(jnp.exp, gcum)
    kbeta = _each(lambda x, y: x * y, kn, b)
    vbeta = _each(lambda x, y: x * y, v, b)
    m = _each(lambda x, y, d: jnp.where(jj < ii, _bdot(x, y, NT) * d, 0.0), kbeta, kn, decay)
    inv = _each(lambda x: eye - x, m)
    power = _each(lambda x: _dot3(x, x), m)
    for step in range(5):
        inv = _each(lambda x, p: x + _dot3(x, p), inv, power)
        if step < 4:
            power = _each(lambda p: _dot3(p, p), power)
    u = _each(_dot3, inv, vbeta)
    w = _each(lambda x, y, e: _dot3(x, y * e), inv, kbeta, e_gcum)
    a_qk = _each(lambda x, y, d: _bdot(x, y, NT) * d, qn, kn, decay)
    v_new = _each(lambda x, y, s: x - _bdot(y, s), u, w, state)
    o = _each(lambda x, e, s, a, vn: _bdot(x * e, s) + _bdot(a, vn), qn, e_gcum, state, a_qk, v_new)
    new_state = _each(lambda s, gl, x, gc, vn: s * jnp.exp(gl) + _bdot(x * jnp.exp(gl - gc), vn, TN),
                      state, g_last, kn, gcum, v_new)
    return o, new_state


GDN_HEADS_PER_STEP = 8


GDN_TIME_TILE = 256


def _gdn_specs(n_tokens, reverse):
    hb, hd, tt = GDN_HEADS_PER_STEP, GDN_HEAD_DIM, GDN_TIME_TILE
    nb, nt = GDN_HEADS // hb, n_tokens // tt

    def when(t):
        return nt - 1 - t if reverse else t

    q = pl.BlockSpec((tt, hb * hd), lambda h, t: (when(t), h))
    k = pl.BlockSpec((tt, hb * hd), lambda h, t: (when(t), nb + h))
    v = pl.BlockSpec((tt, hb * hd), lambda h, t: (when(t), 2 * nb + h))
    vec = pl.BlockSpec((tt, hb), lambda h, t: (when(t), h))
    states = pl.BlockSpec((hb, tt // GDN_CHUNK, hd, hd), lambda h, t: (h, when(t), 0, 0))
    return q, k, v, vec, states


def _gdn_fwd(qkv, beta, g):
    n_tokens = qkv.shape[0]
    hb, hd, tt = GDN_HEADS_PER_STEP, GDN_HEAD_DIM, GDN_TIME_TILE
    n_chunks = tt // GDN_CHUNK
    q_s, k_s, v_s, vec, st = _gdn_specs(n_tokens, False)

    def body(q_ref, k_ref, v_ref, b_ref, g_ref, o_ref, st_ref, state):
        @pl.when(pl.program_id(1) == 0)
        def _():
            state[...] = jnp.zeros_like(state)

        def step(c, carry):
            r = pl.ds(pl.multiple_of(c * GDN_CHUNK, GDN_CHUNK), GDN_CHUNK)
            cols = [slice(h * hd, (h + 1) * hd) for h in range(hb)]
            old = [state[h] for h in range(hb)]
            o, new = _gdn_chunks(
                [q_ref[r, cs] for cs in cols], [k_ref[r, cs] for cs in cols], [v_ref[r, cs] for cs in cols],
                [b_ref[r, h:h + 1] for h in range(hb)], [g_ref[r, h:h + 1] for h in range(hb)], old)
            for h in range(hb):
                st_ref[h, c] = old[h]
                o_ref[r, cols[h]] = o[h]
                state[h] = new[h]
            return carry

        lax.fori_loop(0, n_chunks, step, 0)

    return pl.pallas_call(
        body, grid=(GDN_HEADS // hb, n_tokens // tt), in_specs=[q_s, k_s, v_s, vec, vec], out_specs=(q_s, st),
        out_shape=(jax.ShapeDtypeStruct((n_tokens, GDN_WIDTH), F32),
                   jax.ShapeDtypeStruct((GDN_HEADS, n_tokens // GDN_CHUNK, hd, hd), F32)),
        scratch_shapes=[pltpu.VMEM((hb, hd, hd), F32)],
        name="gdn_fwd", compiler_params=_params(2),
    )(qkv, qkv, qkv, beta, g)


def _gdn_bwd(qkv, beta, g, states, do):
    n_tokens = qkv.shape[0]
    hb, hd, tt = GDN_HEADS_PER_STEP, GDN_HEAD_DIM, GDN_TIME_TILE
    n_chunks = tt // GDN_CHUNK
    q_s, k_s, v_s, vec, st = _gdn_specs(n_tokens, True)

    def body(q_ref, k_ref, v_ref, b_ref, g_ref, st_ref, do_ref, dq_ref, dk_ref, dv_ref, db_ref, dg_ref, dstate):
        @pl.when(pl.program_id(1) == 0)
        def _():
            dstate[...] = jnp.zeros_like(dstate)

        def step(i, carry):
            c = n_chunks - 1 - i
            r = pl.ds(pl.multiple_of(c * GDN_CHUNK, GDN_CHUNK), GDN_CHUNK)
            cols = [slice(h * hd, (h + 1) * hd) for h in range(hb)]
            args = ([q_ref[r, cs] for cs in cols], [k_ref[r, cs] for cs in cols], [v_ref[r, cs] for cs in cols],
                    [b_ref[r, h:h + 1] for h in range(hb)], [g_ref[r, h:h + 1] for h in range(hb)],
                    [st_ref[h, c] for h in range(hb)])
            cts = ([do_ref[r, cs] for cs in cols], [dstate[h] for h in range(hb)])
            dq, dk, dv, db, dg, dst = jax.vjp(_gdn_chunks, *args)[1](cts)
            for h in range(hb):
                dq_ref[r, cols[h]] = dq[h]
                dk_ref[r, cols[h]] = dk[h]
                dv_ref[r, cols[h]] = dv[h]
                db_ref[r, h:h + 1] = db[h]
                dg_ref[r, h:h + 1] = dg[h]
                dstate[h] = dst[h]
            return carry

        lax.fori_loop(0, n_chunks, step, 0)

    wide = jax.ShapeDtypeStruct((n_tokens, GDN_WIDTH), F32)
    thin = jax.ShapeDtypeStruct(beta.shape, F32)
    dq, dk, dv, db, dg = pl.pallas_call(
        body, grid=(GDN_HEADS // hb, n_tokens // tt), in_specs=[q_s, k_s, v_s, vec, vec, st, q_s],
        out_specs=(q_s, q_s, q_s, vec, vec), out_shape=(wide, wide, wide, thin, thin),
        scratch_shapes=[pltpu.VMEM((hb, hd, hd), F32)],
        name="gdn_bwd", compiler_params=_params(2),
    )(qkv, qkv, qkv, beta, g, states, do)
    return jnp.concatenate([dq, dk, dv], axis=1), db, dg


FFN_ROW_TILE = 256


def _resident(shape):
    return pl.BlockSpec(shape, lambda i: (0,) * len(shape), pipeline_mode=pl.Buffered(1))


def _ffn_fwd(x, gain, wg, wu, wd, name):
    n_tokens, d = x.shape
    n_shards, n, _ = wg.shape
    tm = FFN_ROW_TILE

    def body(x_ref, gain_ref, wg_ref, wu_ref, wd_ref, o_ref, g_ref, u_ref):
        xv = x_ref[...]
        h = (xv * lax.rsqrt(jnp.mean(xv * xv, axis=-1, keepdims=True) + EPS) * gain_ref[...]).astype(BF16)
        acc = jnp.zeros((tm, d), F32)
        for j in range(n_shards):
            g = lax.dot_general(h, wg_ref[j], NT, preferred_element_type=F32)
            u = lax.dot_general(h, wu_ref[j], NT, preferred_element_type=F32)
            g_ref[j] = g
            u_ref[j] = u
            a = (g * jax.nn.sigmoid(g) * u).astype(BF16)
            acc = acc + lax.dot_general(a, wd_ref[j], NN, preferred_element_type=F32)
        o_ref[...] = xv + 0.5 * acc

    row = pl.BlockSpec((tm, d), lambda i: (i, 0))
    hid = pl.BlockSpec((n_shards, tm, n), lambda i: (0, i, 0))
    return pl.pallas_call(
        body, grid=(n_tokens // tm,),
        in_specs=[row, _resident(gain.shape), _resident(wg.shape), _resident(wu.shape), _resident(wd.shape)],
        out_specs=(row, hid, hid),
        out_shape=(jax.ShapeDtypeStruct(x.shape, F32), jax.ShapeDtypeStruct((n_shards, n_tokens, n), F32),
                   jax.ShapeDtypeStruct((n_shards, n_tokens, n), F32)),
        name=name, compiler_params=_params(1),
    )(x, gain, wg, wu, wd)


def _ffn_bwd_rows(x, gain, dy, g, u, wg, wu, wd, name):
    n_tokens, d = x.shape
    n_shards, n, _ = wg.shape
    tm = FFN_ROW_TILE

    def body(x_ref, gain_ref, dy_ref, g_ref, u_ref, wg_ref, wu_ref, wd_ref,
             dx_ref, dgain_ref, h_ref, dyh_ref, a_ref, dg_ref, du_ref):
        xv, dyv, gain_v = x_ref[...], dy_ref[...], gain_ref[...]
        r = lax.rsqrt(jnp.mean(xv * xv, axis=-1, keepdims=True) + EPS)
        xhat = xv * r
        h_ref[...] = (xhat * gain_v).astype(BF16)
        dyh = (0.5 * dyv).astype(BF16)
        dyh_ref[...] = dyh
        dh = jnp.zeros((tm, d), F32)
        for j in range(n_shards):
            da = lax.dot_general(dyh, wd_ref[j], NT, preferred_element_type=F32)
            gv, uv = g_ref[j], u_ref[j]
            sg = jax.nn.sigmoid(gv)
            silu = gv * sg
            a_ref[j] = (silu * uv).astype(BF16)
            dg = (da * uv * (sg + silu * (1.0 - sg))).astype(BF16)
            du = (da * silu).astype(BF16)
            dg_ref[j] = dg
            du_ref[j] = du
            dh = dh + lax.dot_general(dg, wg_ref[j], NN, preferred_element_type=F32)
            dh = dh + lax.dot_general(du, wu_ref[j], NN, preferred_element_type=F32)
        dxhat = dh * gain_v
        dx_ref[...] = dyv + r * (dxhat - xhat * jnp.mean(dxhat * xhat, axis=-1, keepdims=True))

        @pl.when(pl.program_id(0) == 0)
        def _():
            dgain_ref[...] = jnp.zeros_like(dgain_ref)

        dgain_ref[...] += jnp.sum(dh * xhat, axis=0, keepdims=True)

    row = pl.BlockSpec((tm, d), lambda i: (i, 0))
    hid = pl.BlockSpec((n_shards, tm, n), lambda i: (0, i, 0))
    hid_shape = (n_shards, n_tokens, n)
    return pl.pallas_call(
        body, grid=(n_tokens // tm,),
        in_specs=[row, _resident(gain.shape), row, hid, hid, _resident(wg.shape), _resident(wu.shape),
                  _resident(wd.shape)],
        out_specs=(row, pl.BlockSpec(gain.shape, lambda i: (0, 0)), row, row, hid, hid, hid),
        out_shape=(jax.ShapeDtypeStruct(x.shape, F32), jax.ShapeDtypeStruct(gain.shape, F32),
                   jax.ShapeDtypeStruct(x.shape, BF16), jax.ShapeDtypeStruct(x.shape, BF16),
                   jax.ShapeDtypeStruct(hid_shape, BF16), jax.ShapeDtypeStruct(hid_shape, BF16),
                   jax.ShapeDtypeStruct(hid_shape, BF16)),
        name=name, compiler_params=_params(1),
    )(x, gain, dy, g, u, wg, wu, wd)


def _ffn_bwd_weights(h, dyh, a, dg, du, name):
    n_shards, n_tokens, n = a.shape
    d = h.shape[1]

    def body(h_ref, dyh_ref, a_ref, dg_ref, du_ref, dwg_ref, dwu_ref, dwd_ref):
        hv = h_ref[...]
        dwg_ref[0] = lax.dot_general(dg_ref[0], hv, TN, preferred_element_type=F32)
        dwu_ref[0] = lax.dot_general(du_ref[0], hv, TN, preferred_element_type=F32)
        dwd_ref[0] = lax.dot_general(a_ref[0], dyh_ref[...], TN, preferred_element_type=F32)

    hid = pl.BlockSpec((1, n_tokens, n), lambda j: (j, 0, 0))
    out = pl.BlockSpec((1, n, d), lambda j: (j, 0, 0))
    return pl.pallas_call(
        body, grid=(n_shards,), in_specs=[_resident(h.shape), _resident(dyh.shape), hid, hid, hid],
        out_specs=(out, out, out), out_shape=(jax.ShapeDtypeStruct((n_shards, n, d), F32),) * 3,
        name=name, compiler_params=_params(1),
    )(h, dyh, a, dg, du)


IN_PIECES = (("wq_a", 0, 768), ("wk_a", 768, 1536), ("wv_a", 1536, 2304), ("w_qkvb", 2304, 5376),
             ("w_small", 5376, 5392), ("w_ggate", 5392, 6416), ("w_gatea", 6416, 7440), ("w_gateb", 7440, 8464))
IN_NAMES = tuple(name for name, _, _ in IN_PIECES)


def _in_rows(lo, hi):
    return lo, max(hi, lo + LANES)


def _in_proj_fwd(x, gain, wt):
    n_tokens, d = x.shape
    tm = FFN_ROW_TILE
    rows = [_in_rows(lo, hi) for _, lo, hi in IN_PIECES]

    def body(x_ref, gain_ref, wt_ref, *o_refs):
        xv = x_ref[...]
        h = (xv * lax.rsqrt(jnp.mean(xv * xv, axis=-1, keepdims=True) + EPS) * gain_ref[...]).astype(BF16)
        for (lo, hi), o_ref in zip(rows, o_refs):
            o_ref[...] = lax.dot_general(h, wt_ref[lo:hi, :], NT, preferred_element_type=F32)

    return pl.pallas_call(
        body, grid=(n_tokens // tm,),
        in_specs=[pl.BlockSpec((tm, d), lambda i: (i, 0)), _resident(gain.shape), _resident(wt.shape)],
        out_specs=tuple(pl.BlockSpec((tm, hi - lo), lambda i: (i, 0)) for lo, hi in rows),
        out_shape=tuple(jax.ShapeDtypeStruct((n_tokens, hi - lo), F32) for lo, hi in rows),
        name="in_proj_fwd", compiler_params=_params(1),
    )(x, gain, wt)


def _in_proj_bwd_rows(x, gain, dres, dzs, wt):
    n_tokens, d = x.shape
    tm = FFN_ROW_TILE
    n = len(dzs)
    rows = [_in_rows(lo, hi) for _, lo, hi in IN_PIECES]

    def body(x_ref, gain_ref, dres_ref, *refs):
        dz_refs, wt_ref = refs[:n], refs[n]
        dx_ref, dgain_ref, h_ref = refs[n + 1:]
        xv, gain_v = x_ref[...], gain_ref[...]
        r = lax.rsqrt(jnp.mean(xv * xv, axis=-1, keepdims=True) + EPS)
        xhat = xv * r
        h_ref[...] = (xhat * gain_v).astype(BF16)
        dh = jnp.zeros((tm, d), F32)
        for dz_ref, (lo, hi) in zip(dz_refs, rows):
            dh = dh + lax.dot_general(dz_ref[...].astype(BF16), wt_ref[lo:hi, :], NN, preferred_element_type=F32)
        dxhat = dh * gain_v
        dx_ref[...] = dres_ref[...] + r * (dxhat - xhat * jnp.mean(dxhat * xhat, axis=-1, keepdims=True))

        @pl.when(pl.program_id(0) == 0)
        def _():
            dgain_ref[...] = jnp.zeros_like(dgain_ref)

        dgain_ref[...] += jnp.sum(dh * xhat, axis=0, keepdims=True)

    row = pl.BlockSpec((tm, d), lambda i: (i, 0))
    return pl.pallas_call(
        body, grid=(n_tokens // tm,),
        in_specs=([row, _resident(gain.shape), row]
                  + [pl.BlockSpec((tm, dz.shape[1]), lambda i: (i, 0)) for dz in dzs] + [_resident(wt.shape)]),
        out_specs=(row, pl.BlockSpec(gain.shape, lambda i: (0, 0)), row),
        out_shape=(jax.ShapeDtypeStruct(x.shape, F32), jax.ShapeDtypeStruct(gain.shape, F32),
                   jax.ShapeDtypeStruct(x.shape, BF16)),
        name="in_proj_bwd_rows", compiler_params=_params(1),
    )(x, gain, dres, *dzs, wt)


def _in_proj_bwd_weight(dwt, h, dz, lo, hi, name):
    n_tokens, d = h.shape
    width = hi - lo
    tn = _tile(width, 512) if width >= LANES else width
    dz_tile = max(tn, LANES)

    def body(dwt_ref, h_ref, dz_ref, o_ref):
        o_ref[...] = lax.dot_general(dz_ref[:, :tn].astype(BF16), h_ref[...], TN, preferred_element_type=F32)

    return pl.pallas_call(
        body, grid=(width // tn,),
        in_specs=[ANY, _resident(h.shape), pl.BlockSpec((n_tokens, dz_tile), lambda j: (0, j))],
        out_specs=pl.BlockSpec((pl.Element(tn), pl.Element(d)), lambda j: (pl.multiple_of(lo + j * tn, 16), 0)),
        out_shape=jax.ShapeDtypeStruct(dwt.shape, F32), input_output_aliases={0: 0}, name=name,
        compiler_params=_params(1),
    )(dwt, h, dz)


def _split_small(z):
    return z[:, :GDN_HEADS], z[:, GDN_HEADS:2 * GDN_HEADS]


def _heads3(q, k, v):
    return _to_heads(q), _to_heads(k), _to_heads(v)


def _tokens6(o, lse):
    return tuple(_from_heads(o)) + tuple(_from_heads(lse))


def mixer_forward(x1, w, small):
    n_tokens = x1.shape[0]
    proj = dict(zip(IN_NAMES, _in_proj_fwd(x1, small["mix_norm"], w["w_in_t"])))
    cos, sin = _rope_tables(n_tokens)
    q_rot = _rope_call(proj["wq_a"], cos, sin, "rope_q")
    k_rot = _rope_call(proj["wk_a"], cos, sin, "rope_k")
    (qh, kh, vh), heads_vjp = jax.vjp(_heads3, q_rot, k_rot, proj["wv_a"])
    o, lse = _attn_fwd(qh, kh, vh)
    per_group, tokens_vjp = jax.vjp(_tokens6, o, lse)
    ya = _rowwise_fwd(_combine_fn, "combine", per_group, (), (), 512, 1)[0]
    pa = _matmul(ya, w["w_branch_a"], name="branch_a")
    qkv = _conv_fwd(proj["w_qkvb"], small["gdn_conv_w"])
    raw, small_vjp = jax.vjp(_split_small, proj["w_small"])
    gdn_params = (small["gdn_a_log"], small["gdn_dt_bias"])
    beta, gcum = _rowwise_fwd(_beta_decay_fn, "beta_decay", raw, (), gdn_params, 512, 1)
    ob, states = _gdn_fwd(qkv, beta, gcum)
    gate_in = (ob, proj["w_ggate"])
    yb = _rowwise_fwd(_outnorm_gate_fn, "outnorm_gate", gate_in, (), (small["gdn_out_norm"],), 512, GDN_HEADS)[0]
    pb = _matmul(yb, w["w_branch_b"], name="branch_b")
    merge_in = (proj["w_gatea"], proj["w_gateb"], pa, pb)
    merged = _rowwise_fwd(_merge_fn, "merge", merge_in, (), (), 256, 1)[0]
    x2 = _matmul(merged, w["w_out"], name="out", res=x1)
    saved = dict(x1=x1, proj=proj, cos=cos, sin=sin, heads_vjp=heads_vjp, heads=(qh, kh, vh), tokens_vjp=tokens_vjp,
                 per_group=per_group, ya=ya, qkv=qkv, raw=raw, small_vjp=small_vjp, beta=beta, gcum=gcum, states=states,
                 gate_in=gate_in, yb=yb, merge_in=merge_in, merged=merged)
    return x2, saved


def mixer_backward(dx2, s, w, small):
    proj = s["proj"]
    dmerged = _matmul(dx2, w["w_out"], name="out_da", tb=True)
    grads = dict(w_out=_matmul(s["merged"], dx2, name="out_dw", ta=True))
    dgate_a, dgate_b, dpa, dpb = _rowwise_bwd(_merge_fn, "merge_bwd", s["merge_in"], (), (), (dmerged,), 256, 1)
    dyb = _matmul(dpb, w["w_branch_b"], name="branch_b_da", tb=True)
    grads["w_branch_b"] = _matmul(s["yb"], dpb, name="branch_b_dw", ta=True)
    dya = _matmul(dpa, w["w_branch_a"], name="branch_a_da", tb=True)
    grads["w_branch_a"] = _matmul(s["ya"], dpa, name="branch_a_dw", ta=True)
    dob, dggate, grads["gdn_out_norm"] = _rowwise_bwd(
        _outnorm_gate_fn, "outnorm_gate_bwd", s["gate_in"], (), (small["gdn_out_norm"],), (dyb,), 512, GDN_HEADS)
    dqkv, dbeta, dgcum = _gdn_bwd(s["qkv"], s["beta"], s["gcum"], s["states"], dob)
    gdn_params = (small["gdn_a_log"], small["gdn_dt_bias"])
    dbeta_raw, ddecay_raw, grads["gdn_a_log"], grads["gdn_dt_bias"] = _rowwise_bwd(
        _beta_decay_fn, "beta_decay_bwd", s["raw"], (), gdn_params, (dbeta, dgcum), 512, 1)
    dsmall = s["small_vjp"]((dbeta_raw, ddecay_raw))[0]
    dqkvb, grads["gdn_conv_w"] = _conv_bwd(proj["w_qkvb"], small["gdn_conv_w"], dqkv)
    dper_group = _rowwise_bwd(_combine_fn, "combine_bwd", s["per_group"], (), (), (dya,), 512, 1)
    do, dlse = s["tokens_vjp"](tuple(dper_group))
    dqh, dkh, dvh = _attn_bwd(*s["heads"], do, dlse)
    dq_rot, dk_rot, dv = s["heads_vjp"]((dqh, dkh, dvh))
    dq = _rope_call(dq_rot, s["cos"], -s["sin"], "rope_q_bwd")
    dk = _rope_call(dk_rot, s["cos"], -s["sin"], "rope_k_bwd")
    dzs = (dq, dk, dv, dqkvb, dsmall, dggate, dgate_a, dgate_b)
    dx1, grads["mix_norm"], h = _in_proj_bwd_rows(s["x1"], small["mix_norm"], dx2, dzs, w["w_in_t"])
    dwt = lax.empty(w["w_in_t"].shape, F32)
    for (name, lo, hi), dz in zip(IN_PIECES, dzs):
        dwt = _in_proj_bwd_weight(dwt, h, dz, lo, hi, "in_proj_dw_" + name)
    grads["w_in_t"] = dwt
    return dx1, grads


def ffn_forward(x, gain, w, tag):
    out, g, u = _ffn_fwd(x, gain, w[tag + "_w_gate"], w[tag + "_w_up"], w[tag + "_w_down"], tag + "_fwd")
    return out, (x, g, u)


def ffn_backward(dy, saved, gain, w, tag):
    x, g, u = saved
    weights = (w[tag + "_w_gate"], w[tag + "_w_up"], w[tag + "_w_down"])
    dx, dgain, h, dyh, a, dg, du = _ffn_bwd_rows(x, gain, dy, g, u, *weights, tag + "_bwd_rows")
    return dx, dgain, _ffn_bwd_weights(h, dyh, a, dg, du, tag + "_bwd_weights")


def loss_head(x3, target, gain):
    row_loss = _rowwise_fwd(_loss_fn, "loss", (x3,), (target,), (gain,), 256, 1)[0]
    dx3, dgain = _rowwise_bwd(_loss_fn, "loss_bwd", (x3,), (target,), (gain,), (jnp.ones_like(row_loss),), 256, 1)
    return jnp.sum(row_loss), dx3, dgain


BIG_WEIGHTS = ("ffn1_w_gate", "ffn1_w_up", "ffn1_w_down", "w_in", "w_branch_a", "w_branch_b", "w_out",
               "ffn2_w_gate", "ffn2_w_up", "ffn2_w_down")
TRANSPOSED = ("ffn1_w_gate", "ffn1_w_up", "w_in", "ffn2_w_gate", "ffn2_w_up")
CONV_SHARD = (GDN_CONV, 3 * GDN_WIDTH // N_DEV)
SMALL_ROWS = 24
ANY = pl.BlockSpec(memory_space=pl.ANY)


TOKEN = jax.ShapeDtypeStruct((8, LANES), F32)


def _after(value, token):
    return value + token[0, 0].astype(value.dtype)


def _position():
    return lax.axis_index("x"), lax.axis_index("y"), lax.axis_index("c")


def all_gather_shards(shards, name):
    n = len(shards)

    def body(*refs):
        x_refs, out_refs = refs[:n], refs[n:2 * n]
        send_sems, recv_sems, local_sems = refs[2 * n + 1:]
        x, y, c = _position()
        me, sibling = (x, y, c), (x, y, 1 - c)
        chips = [(1 - x, y), (x, 1 - y), (1 - x, 1 - y)]

        def slab(a, px, py, pc):
            return out_refs[a].at[4 * px + 2 * py + pc]

        def copy(a, k, block, to, src=None):
            return pltpu.make_async_remote_copy(
                src_ref=slab(a, *block) if src is None else src, dst_ref=slab(a, *block),
                send_sem=send_sems.at[7 * a + k], recv_sem=recv_sems.at[7 * a + k], device_id=to, device_id_type=MESH)

        mine = [pltpu.make_async_copy(x_refs[a], slab(a, *me), local_sems.at[a]) for a in range(n)]
        for cp in mine:
            cp.start()
        first = []
        for j, chip in enumerate(chips):
            first += [copy(a, 1 + j, me, (*chip, c), src=x_refs[a]) for a in range(n)]
        first += [copy(a, 0, me, sibling, src=x_refs[a]) for a in range(n)]
        for cp in first:
            cp.start()
        passed = []
        for j, chip in enumerate(chips):
            for a in range(n):
                copy(a, 1 + j, (*chip, c), me).wait_recv()
                cp = copy(a, 4 + j, (*chip, c), sibling)
                cp.start()
                passed.append(cp)
        for a in range(n):
            copy(a, 0, sibling, me).wait_recv()
        for j, chip in enumerate(chips):
            for a in range(n):
                copy(a, 4 + j, (*chip, 1 - c), me).wait_recv()
        for cp in first + passed:
            cp.wait_send()
        for cp in mine:
            cp.wait()
        refs[2 * n][...] = jnp.zeros_like(refs[2 * n])

    outs = pl.pallas_call(
        body, out_shape=tuple(jax.ShapeDtypeStruct((N_DEV,) + s.shape, s.dtype) for s in shards) + (TOKEN,),
        in_specs=[ANY] * n, out_specs=(ANY,) * n + (pl.BlockSpec(memory_space=pltpu.VMEM),),
        scratch_shapes=[pltpu.SemaphoreType.DMA((7 * n,)), pltpu.SemaphoreType.DMA((7 * n,)),
                        pltpu.SemaphoreType.DMA((n,))],
        name=name,
    )(*shards)
    return outs[:n], outs[n]


def exchange_with_sibling(grads):
    n = len(grads)

    def body(*refs):
        g_refs, recv_refs = refs[:n], refs[n:2 * n]
        send_sems, recv_sems = refs[2 * n:]
        x, y, c = _position()
        copies = [pltpu.make_async_remote_copy(
            src_ref=g_refs[a].at[2 * k + 1 - c], dst_ref=recv_refs[a].at[k], send_sem=send_sems.at[4 * a + k],
            recv_sem=recv_sems.at[4 * a + k], device_id=(x, y, 1 - c), device_id_type=MESH)
            for k in range(4) for a in range(n)]
        for cp in copies:
            cp.start()
        for cp in copies:
            cp.wait()

    return pl.pallas_call(
        body, out_shape=tuple(jax.ShapeDtypeStruct((4,) + g.shape[1:], g.dtype) for g in grads),
        in_specs=[ANY] * n, out_specs=(ANY,) * n,
        scratch_shapes=[pltpu.SemaphoreType.DMA((4 * n,)), pltpu.SemaphoreType.DMA((4 * n,))], name="rs_sibling",
    )(*grads)


ELEMENTWISE_TILE_BYTES = 1536 * 1024


def _tile2(rows, cols):
    if rows % 256 == 0:
        return 256, cols
    if rows * cols * 4 > ELEMENTWISE_TILE_BYTES and cols % 256 == 0:
        return rows, 256
    return rows, cols


def add_sibling(grads, received, core, name):
    _, rows, width = grads.shape
    tr, tc = _tile2(rows, width)

    def body(c_ref, g_ref, r_ref, o_ref):
        o_ref[...] = (g_ref[...] + r_ref[...]).astype(BF16)

    blk = (1, tr, tc)
    return pl.pallas_call(
        body,
        grid_spec=pltpu.PrefetchScalarGridSpec(
            num_scalar_prefetch=1, grid=(4, rows // tr, width // tc),
            in_specs=[pl.BlockSpec(blk, lambda k, i, j, c_ref: (2 * k + c_ref[0], i, j)),
                      pl.BlockSpec(blk, lambda k, i, j, c_ref: (k, i, j))],
            out_specs=pl.BlockSpec(blk, lambda k, i, j, c_ref: (k, i, j))),
        out_shape=jax.ShapeDtypeStruct((4, rows, width), BF16), name=name, compiler_params=_params(3),
    )(core, grads, received)


HBM = pl.BlockSpec(memory_space=pltpu.HBM)
SEM = pl.BlockSpec(memory_space=pltpu.SEMAPHORE)
DATAFLOW_EFFECT = pltpu.SideEffectType.DATAFLOW_SIDE_EFFECTING
N_PEERS = N_DEV - 1


def _peer(mask):
    x, y, c = _position()
    px = 1 - x if mask & 4 else x
    py = 1 - y if mask & 2 else y
    pc = 1 - c if mask & 1 else c
    return (px, py, pc), 4 * px + 2 * py + pc


ALL_PEERS = tuple(range(1, N_DEV))
OTHER_CHIPS = (4, 2, 6)


def _exchange_peers(mode):
    return OTHER_CHIPS if mode == "chips" else ALL_PEERS


def _direct_copies(src_refs, land_refs, send_sems, recv_sems, mode):
    x, y, c = _position()
    me = 4 * x + 2 * y + c
    masks = _exchange_peers(mode)
    copies = []
    for a, (src, land) in enumerate(zip(src_refs, land_refs)):
        for slot, mask in enumerate(masks):
            peer, peer_index = _peer(mask)
            k = len(masks) * a + slot
            source = {"gather": lambda: src, "scatter": lambda: src.at[peer_index],
                      "chips": lambda: src.at[2 * peer[0] + peer[1]]}[mode]()
            copies.append(pltpu.make_async_remote_copy(
                src_ref=source, dst_ref=land.at[me] if mode == "gather" else land.at[slot],
                send_sem=send_sems.at[k], recv_sem=recv_sems.at[k], device_id=peer, device_id_type=MESH))
    return copies


def direct_exchange_start(arrays, mode, name):
    n = len(arrays)
    n_peers = len(_exchange_peers(mode))
    lands = [lax.empty((N_DEV,) + a.shape if mode == "gather" else (n_peers,) + a.shape[1:], a.dtype) for a in arrays]

    def body(*refs):
        src_refs, land_refs = refs[:n], refs[n:2 * n]
        send_sems, recv_sems = refs[2 * n], refs[2 * n + 1]
        token = refs[-1]
        for cp in _direct_copies(src_refs, land_refs, send_sems, recv_sems, mode):
            cp.start()
        token[...] = jnp.zeros_like(token)

    sems = pltpu.SemaphoreType.DMA((n_peers * n,))
    outs = pl.pallas_call(
        body, name=name,
        out_shape=(sems, sems) + tuple(pltpu.HBM(a.shape, a.dtype) for a in arrays)
        + tuple(pltpu.HBM(l.shape, l.dtype) for l in lands) + (TOKEN,),
        in_specs=[HBM] * (2 * n), out_specs=(SEM, SEM) + (HBM,) * (2 * n) + (pl.BlockSpec(memory_space=pltpu.VMEM),),
        input_output_aliases={i: 2 + i for i in range(2 * n)},
        compiler_params=pltpu.CompilerParams(has_side_effects=DATAFLOW_EFFECT),
    )(*[pltpu.with_memory_space_constraint(a, pltpu.HBM) for a in list(arrays) + lands])
    return outs[0], outs[1], outs[2:2 + n], outs[2 + n:2 + 2 * n], outs[-1]


def direct_exchange_wait(send_sems, recv_sems, arrays, lands, after, mode, name):
    n = len(arrays)

    def body(*refs):
        src_refs, land_refs = refs[:n], refs[n:2 * n]
        send_sems, recv_sems = refs[2 * n], refs[2 * n + 1]
        for cp in _direct_copies(src_refs, land_refs, send_sems, recv_sems, mode):
            cp.wait_send()
            cp.wait_recv()

    outs = pl.pallas_call(
        body, name=name,
        out_shape=tuple(pltpu.HBM(a.shape, a.dtype) for a in arrays) + tuple(pltpu.HBM(l.shape, l.dtype) for l in lands),
        in_specs=[HBM] * (2 * n) + [SEM, SEM, pl.BlockSpec(memory_space=pl.ANY)], out_specs=(HBM,) * (2 * n),
        input_output_aliases={i: i for i in range(2 * n)},
        compiler_params=pltpu.CompilerParams(has_side_effects=DATAFLOW_EFFECT),
    )(*arrays, *lands, send_sems, recv_sems, after)
    return outs[n:]


def adamw_direct(w, m, v, grads, received, me, name):
    row_per_tile = w.shape[0] != 1
    rows, cols = (w.shape[0], w.shape[2]) if row_per_tile else w.shape[-2:]
    tr, tc = _tile2(rows, cols)

    def body(me_ref, w_ref, m_ref, v_ref, own_ref, r_ref, g_ref, d_ref, nm_ref, nv_ref):
        gv = own_ref[0]
        for j in range(N_PEERS):
            gv = gv + r_ref[j].astype(F32)
        nm = ADAM_B1 * m_ref[...] + (1.0 - ADAM_B1) * gv
        nv = ADAM_B2 * v_ref[...] + (1.0 - ADAM_B2) * (gv * gv)
        m_hat = nm / (1.0 - ADAM_B1 ** ADAM_STEP)
        v_hat = nv / (1.0 - ADAM_B2 ** ADAM_STEP)
        g_ref[...] = gv
        d_ref[...] = -ADAM_LR * (m_hat / (jnp.sqrt(v_hat) + ADAM_EPS) + ADAM_WD * w_ref[...])
        nm_ref[...] = nm
        nv_ref[...] = nv

    if row_per_tile:
        one = pl.BlockSpec((tr, None, tc), lambda i, j, me_ref: (i, 0, j))
    else:
        one = pl.BlockSpec((None, tr, tc), lambda i, j, me_ref: (0, i, j))
    out = jax.ShapeDtypeStruct(w.shape, F32)
    return pl.pallas_call(
        body,
        grid_spec=pltpu.PrefetchScalarGridSpec(
            num_scalar_prefetch=1, grid=(rows // tr, cols // tc),
            in_specs=[one, one, one, pl.BlockSpec((1, tr, tc), lambda i, j, me_ref: (me_ref[0], i, j)),
                      pl.BlockSpec((N_PEERS, tr, tc), lambda i, j, me_ref: (0, i, j))],
            out_specs=(one,) * 4),
        out_shape=(out,) * 4, name=name, compiler_params=_params(2),
    )(me, w, m, v, grads, received)


def all_reduce_small(vals):
    rows, width = vals.shape

    def body(x_ref, out_ref, all_ref, send_sems, recv_sems):
        x, y, c = _position()
        me, sibling = (x, y, c), (x, y, 1 - c)
        chips = [(1 - x, y), (x, 1 - y), (1 - x, 1 - y)]

        def slab(px, py, pc):
            return all_ref.at[4 * px + 2 * py + pc]

        def copy(k, block, to, src=None):
            return pltpu.make_async_remote_copy(
                src_ref=slab(*block) if src is None else src, dst_ref=slab(*block),
                send_sem=send_sems.at[k], recv_sem=recv_sems.at[k], device_id=to, device_id_type=MESH)

        first = [copy(0, me, sibling, src=x_ref)]
        first += [copy(1 + j, me, (*chip, c), src=x_ref) for j, chip in enumerate(chips)]
        for cp in first:
            cp.start()
        all_ref[4 * x + 2 * y + c] = x_ref[...]
        passed = [copy(4 + j, (*chip, c), sibling) for j, chip in enumerate(chips)]
        for j, chip in enumerate(chips):
            copy(1 + j, (*chip, c), me).wait_recv()
            passed[j].start()
        copy(0, sibling, me).wait_recv()
        for j, chip in enumerate(chips):
            copy(4 + j, (*chip, 1 - c), me).wait_recv()
        for cp in first + passed:
            cp.wait_send()
        total = all_ref[0]
        for d in range(1, N_DEV):
            total = total + all_ref[d]
        out_ref[...] = total

    vmem = pl.BlockSpec(memory_space=pltpu.VMEM)
    return pl.pallas_call(
        body, out_shape=(jax.ShapeDtypeStruct(vals.shape, F32), jax.ShapeDtypeStruct((N_DEV, rows, width), F32)),
        in_specs=[vmem], out_specs=(vmem, vmem),
        scratch_shapes=[pltpu.SemaphoreType.DMA((7,)), pltpu.SemaphoreType.DMA((7,))], name="small_allreduce",
    )(vals)[0]


def adamw(w, g, m, v, name):
    shape = w.shape
    w2, g2, m2, v2 = [a.reshape((-1, shape[-1])) for a in (w, g, m, v)]
    rows, cols = w2.shape
    tr = 256 if rows % 256 == 0 else rows

    def body(w_ref, g_ref, m_ref, v_ref, d_ref, nm_ref, nv_ref):
        gv = g_ref[...]
        nm = ADAM_B1 * m_ref[...] + (1.0 - ADAM_B1) * gv
        nv = ADAM_B2 * v_ref[...] + (1.0 - ADAM_B2) * (gv * gv)
        m_hat = nm / (1.0 - ADAM_B1 ** ADAM_STEP)
        v_hat = nv / (1.0 - ADAM_B2 ** ADAM_STEP)
        d_ref[...] = -ADAM_LR * (m_hat / (jnp.sqrt(v_hat) + ADAM_EPS) + ADAM_WD * w_ref[...])
        nm_ref[...] = nm
        nv_ref[...] = nv

    blk = pl.BlockSpec((tr, cols), lambda i: (i, 0))
    out = jax.ShapeDtypeStruct((rows, cols), F32)
    outs = pl.pallas_call(
        body, grid=(rows // tr,), in_specs=[blk] * 4, out_specs=(blk,) * 3, out_shape=(out,) * 3,
        name=name, compiler_params=_params(1),
    )(w2, g2, m2, v2)
    return tuple(o.reshape(shape) for o in outs)


def adamw_summed(w, m, v, grads, from_sibling, received, me, name):
    rows, cols = w.shape[-2:]
    tr, tc = _tile2(rows, cols)

    def body(me_ref, w_ref, m_ref, v_ref, own_ref, sib_ref, r_ref, g_ref, d_ref, nm_ref, nv_ref):
        gv = own_ref[0] + sib_ref[0]
        for j in range(3):
            gv = gv + r_ref[j].astype(F32)
        nm = ADAM_B1 * m_ref[0] + (1.0 - ADAM_B1) * gv
        nv = ADAM_B2 * v_ref[0] + (1.0 - ADAM_B2) * (gv * gv)
        m_hat = nm / (1.0 - ADAM_B1 ** ADAM_STEP)
        v_hat = nv / (1.0 - ADAM_B2 ** ADAM_STEP)
        g_ref[0] = gv
        d_ref[0] = -ADAM_LR * (m_hat / (jnp.sqrt(v_hat) + ADAM_EPS) + ADAM_WD * w_ref[0])
        nm_ref[0] = nm
        nv_ref[0] = nv

    one = pl.BlockSpec((1, tr, tc), lambda i, j, me_ref: (0, i, j))
    out = jax.ShapeDtypeStruct((1, rows, cols), F32)
    return pl.pallas_call(
        body,
        grid_spec=pltpu.PrefetchScalarGridSpec(
            num_scalar_prefetch=1, grid=(rows // tr, cols // tc),
            in_specs=[one, one, one, pl.BlockSpec((1, tr, tc), lambda i, j, me_ref: (me_ref[0], i, j)),
                      pl.BlockSpec((1, tr, tc), lambda i, j, me_ref: (me_ref[1], i, j)),
                      pl.BlockSpec((3, tr, tc), lambda i, j, me_ref: (0, i, j))],
            out_specs=(one,) * 4),
        out_shape=(out,) * 4, name=name, compiler_params=_params(2),
    )(me, w, m, v, grads, from_sibling, received)


SMALL_VECTORS = ("ffn1_norm", "mix_norm", "ffn2_norm", "final_norm")


def _pack_small(gs):
    row = jnp.concatenate([gs["gdn_a_log"].reshape(-1), gs["gdn_dt_bias"].reshape(-1), gs["gdn_out_norm"].reshape(-1)])
    rows = [gs[n].reshape(1, D_MODEL) for n in SMALL_VECTORS]
    rows.append(jnp.pad(row, (0, D_MODEL - row.shape[0])).reshape(1, D_MODEL))
    rows.append(gs["gdn_conv_w"].reshape(-1, D_MODEL))
    packed = jnp.concatenate(rows, axis=0)
    return jnp.pad(packed, ((0, SMALL_ROWS - packed.shape[0]), (0, 0)))


def _unpack_small(packed):
    out = {n: packed[i].reshape(1, D_MODEL) for i, n in enumerate(SMALL_VECTORS)}
    row = packed[len(SMALL_VECTORS)]
    out["gdn_a_log"] = row[:GDN_HEADS].reshape(1, GDN_HEADS)
    out["gdn_dt_bias"] = row[GDN_HEADS:2 * GDN_HEADS].reshape(1, GDN_HEADS)
    out["gdn_out_norm"] = row[2 * GDN_HEADS:2 * GDN_HEADS + GDN_HEAD_DIM].reshape(1, GDN_HEAD_DIM)
    first = len(SMALL_VECTORS) + 1
    out["gdn_conv_w"] = packed[first:first + GDN_CONV * 3].reshape(GDN_CONV, 3 * GDN_WIDTH)
    return out


WEIGHTS = ("ffn1_norm", "ffn1_w_gate", "ffn1_w_up", "ffn1_w_down", "mix_norm", "w_in", "gdn_conv_w", "gdn_a_log",
           "gdn_dt_bias", "gdn_out_norm", "w_branch_a", "w_branch_b", "w_out", "ffn2_norm", "ffn2_w_gate",
           "ffn2_w_up", "ffn2_w_down", "final_norm")


def kernel(x, ffn1_norm, ffn1_w_gate, ffn1_w_up, ffn1_w_down, mix_norm, w_in, gdn_conv_w, gdn_a_log, gdn_dt_bias, gdn_out_norm, w_branch_a, w_branch_b, w_out, ffn2_norm, ffn2_w_gate, ffn2_w_up, ffn2_w_down, final_norm, loss_target, m_ffn1_norm, m_ffn1_w_gate, m_ffn1_w_up, m_ffn1_w_down, m_mix_norm, m_w_in, m_gdn_conv_w, m_gdn_a_log, m_gdn_dt_bias, m_gdn_out_norm, m_w_branch_a, m_w_branch_b, m_w_out, m_ffn2_norm, m_ffn2_w_gate, m_ffn2_w_up, m_ffn2_w_down, m_final_norm, v_ffn1_norm, v_ffn1_w_gate, v_ffn1_w_up, v_ffn1_w_down, v_mix_norm, v_w_in, v_gdn_conv_w, v_gdn_a_log, v_gdn_dt_bias, v_gdn_out_norm, v_w_branch_a, v_w_branch_b, v_w_out, v_ffn2_norm, v_ffn2_w_gate, v_ffn2_w_up, v_ffn2_w_down, v_final_norm):
    given = dict(locals())
    px, py, pc = _position()
    big_names = list(BIG_WEIGHTS)

    def shard_view(a, n):
        if n == "w_in":
            return a.transpose(2, 0, 1)
        return a.transpose(0, 2, 1) if n in TRANSPOSED else a

    def shard_unview(a, n):
        if n == "w_in":
            return a.transpose(1, 2, 0)
        return a.transpose(0, 2, 1) if n in TRANSPOSED else a

    me = 4 * px + 2 * py + pc
    me_index = me.astype(jnp.int32).reshape(1)
    late = [n for n in big_names if n.startswith("ffn2")]
    early = [n for n in big_names if n not in late]
    shards = {n: shard_view(given[n], n).reshape(given[n].shape[-1 if n in TRANSPOSED else -2], -1).astype(BF16)
              for n in big_names}
    early_slabs, early_done = all_gather_shards([shards[n] for n in early] + [gdn_conv_w[0]], "gather_weights")
    gathered = dict(zip(early + ["gdn_conv_w"], early_slabs))
    late_gather = direct_exchange_start([_after(shards[n], early_done) for n in late], "gather", "gather_ffn2_start")
    ffn1_norm = _after(ffn1_norm, late_gather[4])
    w = {n: gathered[n] for n in early if n.startswith("ffn")}
    w["w_in_t"] = gathered["w_in"].reshape(-1, D_MODEL)
    w["w_branch_a"] = gathered["w_branch_a"].transpose(1, 0, 2).reshape(256, D_MODEL)
    w["w_branch_b"] = gathered["w_branch_b"].reshape(D_MODEL, D_MODEL)
    w["w_out"] = gathered["w_out"].reshape(D_MODEL, D_MODEL)
    conv_full = gathered["gdn_conv_w"].transpose(1, 0, 2).reshape(GDN_CONV, 3 * GDN_WIDTH)
    small = dict(mix_norm=mix_norm, gdn_a_log=gdn_a_log, gdn_dt_bias=gdn_dt_bias, gdn_out_norm=gdn_out_norm,
                 gdn_conv_w=conv_full)

    x1, ffn1_saved = ffn_forward(x[0], ffn1_norm, w, "ffn1")
    x2, mixer_saved = mixer_forward(x1, w, small)
    late_lands = direct_exchange_wait(*late_gather[:4], x2, "gather", "gather_ffn2_wait")
    for n, land in zip(late, late_lands):
        w[n] = lax.dynamic_update_slice(land, shards[n][None], (me, 0, 0))
    x3, ffn2_saved = ffn_forward(x2, ffn2_norm, w, "ffn2")
    loss_local, dx3, g_final = loss_head(x3, loss_target[0], final_norm.reshape(1, D_MODEL))
    loss = lax.psum(loss_local, ("x", "y", "c"))
    dx2, g_ffn2_norm, dw2 = ffn_backward(dx3, ffn2_saved, ffn2_norm, w, "ffn2")
    late_scatter = direct_exchange_start([g.astype(BF16) for g in dw2], "scatter", "rs_ffn2_start")
    w_after = dict(w, w_out=_after(w["w_out"], late_scatter[4]))
    dx1, g_w = mixer_backward(dx2, mixer_saved, w_after, small)
    g_big = dict(zip(late, dw2))
    g_big["w_in"] = g_w["w_in_t"].reshape(N_DEV, -1, D_MODEL)
    g_big["w_branch_a"] = g_w["w_branch_a"].reshape(256, N_DEV, 128).transpose(1, 0, 2)
    g_big["w_branch_b"] = g_w["w_branch_b"].reshape(N_DEV, 128, D_MODEL)
    g_big["w_out"] = g_w["w_out"].reshape(N_DEV, 128, D_MODEL)
    middle = ["w_in", "w_branch_a", "w_branch_b", "w_out"]
    middle_scatter = direct_exchange_start([g_big[n].astype(BF16) for n in middle], "scatter", "rs_mixer_start")
    grad_x, g_ffn1_norm, dw1 = ffn_backward(dx1, ffn1_saved, _after(ffn1_norm, middle_scatter[4]), w, "ffn1")
    g_small = dict(ffn1_norm=g_ffn1_norm, ffn2_norm=g_ffn2_norm, final_norm=g_final,
                   **{n: g_w[n] for n in ("mix_norm", "gdn_a_log", "gdn_dt_bias", "gdn_out_norm", "gdn_conv_w")})

    first = [n for n in early if n.startswith("ffn1")]
    g_big.update(zip(first, dw1))
    g_list = [g_big[n] for n in first]
    core = pc.astype(jnp.int32).reshape(1)
    me_and_chip = jnp.stack([me, 2 * px + py]).astype(jnp.int32)
    from_sibling = exchange_with_sibling(g_list)
    partials = [add_sibling(g, r, core, "rs_add_" + n) for n, g, r in zip(first, g_list, from_sibling)]
    first_chips = direct_exchange_start(partials, "chips", "rs_ffn1_start")

    def state_of(n):
        return [shard_view(given[p + n], n) for p in ("", "m_", "v_")]

    results = {}
    late_received = direct_exchange_wait(*late_scatter[:4], first_chips[4], "scatter", "rs_ffn2_wait")
    middle_received = direct_exchange_wait(*middle_scatter[:4], first_chips[4], "scatter", "rs_mixer_wait")
    for n, recv in zip(late + middle, list(late_received) + list(middle_received)):
        outs = adamw_direct(*state_of(n), g_big[n], recv, me_index, "adamw_" + n)
        results[n] = tuple(shard_unview(o, n) for o in outs)

    small_sum = _unpack_small(all_reduce_small(_after(_pack_small(g_small), first_chips[4])))
    conv_cols = CONV_SHARD[1]
    small_sum["gdn_conv_w"] = lax.dynamic_slice(small_sum["gdn_conv_w"], (0, me * conv_cols), (GDN_CONV, conv_cols))
    for n in WEIGHTS:
        if n not in results and n not in first:
            g = small_sum[n].reshape(given[n].shape)
            results[n] = (g,) + adamw(given[n], g, given["m_" + n], given["v_" + n], "adamw_" + n)

    done = results["final_norm"][1] + results["w_in"][1][0, 0, 0]
    from_chips = direct_exchange_wait(*first_chips[:4], done, "chips", "rs_ffn1_wait")
    for n, g, sib, recv in zip(first, g_list, from_sibling, from_chips):
        outs = adamw_summed(*state_of(n), g, sib, recv, me_and_chip, "adamw_" + n)
        results[n] = tuple(shard_unview(o, n) for o in outs)

    outs = [[results[n][i] for n in WEIGHTS] for i in range(4)]
    return (loss, grad_x[None], *outs[0], *outs[1], *outs[2], *outs[3])
```

```python
import jax
import jax.numpy as jnp
from jax import lax
from jax.experimental import pallas as pl
from jax.experimental.pallas import tpu as pltpu

F32 = jnp.float32
BF16 = jnp.bfloat16
HI = lax.Precision.HIGHEST
MESH = pl.DeviceIdType.MESH

N_DEV = 8
D_MODEL = 1024
EPS = 1e-6
ROPE_THETA = 10000.0
DSW_DILATIONS = (1, 4, 16)
DSW_HEADS_PER_GROUP = 4
DSW_HEAD_DIM = 64
DSW_BLOCK = 128
GDN_HEADS = 8
GDN_HEAD_DIM = 128
GDN_WIDTH = 1024
GDN_CONV = 4
GDN_CHUNK = 64

ADAM_LR = 0.001
ADAM_B1 = 0.9
ADAM_B2 = 0.999
ADAM_EPS = 1e-08
ADAM_WD = 0.01
ADAM_STEP = 10

VMEM_LIMIT_BYTES = 56 * 1024 * 1024
LANES = 128

NN = (((1,), (0,)), ((), ()))
NT = (((1,), (1,)), ((), ()))
TN = (((0,), (0,)), ((), ()))


def _params(n_grid):
    return pltpu.CompilerParams(dimension_semantics=("arbitrary",) * n_grid, vmem_limit_bytes=VMEM_LIMIT_BYTES)


def _tile(n, pref):
    best = None
    t = LANES
    while t <= min(n, pref):
        if n % t == 0:
            best = t
        t += LANES
    return n if best is None else best


def _matmul(a, b, *, name, ta=False, tb=False, res=None, scale=1.0):
    K, M = a.shape if ta else a.shape[::-1]
    N = b.shape[0] if tb else b.shape[1]
    assert (b.shape[1] if tb else b.shape[0]) == K, (a.shape, b.shape, ta, tb)
    tm = _tile(M, 512)
    tn = _tile(N, 512)
    dn = (((0 if ta else 1,), (1 if tb else 0,)), ((), ()))

    def body(*refs):
        a_ref, b_ref = refs[:2]
        o_ref = refs[-1]
        acc = lax.dot_general(a_ref[...].astype(BF16), b_ref[...].astype(BF16), dn, preferred_element_type=F32)
        if scale != 1.0:
            acc = acc * scale
        if res is not None:
            acc = refs[2][...] + acc
        o_ref[...] = acc

    a_spec = pl.BlockSpec((K, tm), lambda i, j: (0, i)) if ta else pl.BlockSpec((tm, K), lambda i, j: (i, 0))
    b_spec = pl.BlockSpec((tn, K), lambda i, j: (j, 0)) if tb else pl.BlockSpec((K, tn), lambda i, j: (0, j))
    o_spec = pl.BlockSpec((tm, tn), lambda i, j: (i, j))
    ins, specs = [a, b], [a_spec, b_spec]
    if res is not None:
        ins.append(res)
        specs.append(o_spec)
    return pl.pallas_call(
        body, grid=(M // tm, N // tn), in_specs=specs, out_specs=o_spec,
        out_shape=jax.ShapeDtypeStruct((M, N), F32), name=name, compiler_params=_params(2),
    )(*ins)


def _rw_specs(arrs, tm, nblk):
    return [pl.BlockSpec((tm, a.shape[1] // nblk), lambda i, j: (i, j)) for a in arrs]


def _rowwise_fwd(fn, name, rows, consts, params, tm, nblk):
    n_rows = rows[0].shape[0]
    tm = min(tm, n_rows)
    ins = list(rows) + list(consts)
    avals = [jax.ShapeDtypeStruct((tm, a.shape[1] // nblk), a.dtype) for a in ins]
    avals += [jax.ShapeDtypeStruct(p.shape, p.dtype) for p in params]
    out_avals = jax.eval_shape(fn, *avals)
    n_in = len(ins) + len(params)

    def body(*refs):
        outs = fn(*[r[...] for r in refs[:n_in]])
        for r, o in zip(refs[n_in:], outs):
            r[...] = o.astype(r.dtype)

    return pl.pallas_call(
        body, grid=(n_rows // tm, nblk),
        in_specs=_rw_specs(ins, tm, nblk) + [pl.BlockSpec(p.shape, lambda i, j: (0, 0)) for p in params],
        out_specs=tuple(pl.BlockSpec((tm, o.shape[1]), lambda i, j: (i, j)) for o in out_avals),
        out_shape=tuple(jax.ShapeDtypeStruct((n_rows, o.shape[1] * nblk), o.dtype) for o in out_avals),
        name=name, compiler_params=_params(2),
    )(*ins, *params)


def _rowwise_bwd(fn, name, rows, consts, params, cts, tm, nblk):
    n_rows = rows[0].shape[0]
    tm = min(tm, n_rows)
    nr, nc, npar, nct = len(rows), len(consts), len(params), len(cts)

    def body(*refs):
        rv = [r[...] for r in refs[:nr]]
        cv = [r[...] for r in refs[nr:nr + nc]]
        pv = [r[...] for r in refs[nr + nc:nr + nc + npar]]
        ctv = [r[...] for r in refs[nr + nc + npar:nr + nc + npar + nct]]
        outs = refs[nr + nc + npar + nct:]
        _, vjp = jax.vjp(lambda *d: fn(*d[:nr], *cv, *d[nr:]), *rv, *pv)
        grads = vjp(tuple(ctv))
        for k in range(nr):
            outs[k][...] = grads[k]
        first = jnp.logical_and(pl.program_id(0) == 0, pl.program_id(1) == 0)
        for k in range(npar):
            ref = outs[nr + k]

            @pl.when(first)
            def _(ref=ref):
                ref[...] = jnp.zeros_like(ref)

            ref[...] += grads[nr + k]

    ins = list(rows) + list(consts)
    return pl.pallas_call(
        body, grid=(n_rows // tm, nblk),
        in_specs=(_rw_specs(ins, tm, nblk) + [pl.BlockSpec(p.shape, lambda i, j: (0, 0)) for p in params]
                  + _rw_specs(cts, tm, nblk)),
        out_specs=tuple(_rw_specs(rows, tm, nblk) + [pl.BlockSpec(p.shape, lambda i, j: (0, 0)) for p in params]),
        out_shape=tuple([jax.ShapeDtypeStruct(a.shape, F32) for a in rows]
                        + [jax.ShapeDtypeStruct(p.shape, F32) for p in params]),
        name=name, compiler_params=_params(2),
    )(*ins, *params, *cts)


def _merge_fn(ga, gb, pa, pb):
    return (jax.nn.sigmoid(ga) * pa + jax.nn.sigmoid(gb) * pb,)


def _outnorm_gate_fn(o, gate, gain):
    y = o * lax.rsqrt(jnp.mean(o * o, axis=-1, keepdims=True) + EPS) * gain
    return (y * (gate * jax.nn.sigmoid(gate)),)


def _beta_decay_fn(beta_raw, decay_raw, a_log, dt_bias):
    z = decay_raw + dt_bias
    softplus = jnp.maximum(z, 0.0) + jnp.log(1.0 + jnp.exp(-jnp.abs(z)))
    g = -jnp.exp(a_log) * softplus
    rows = g.shape[0]
    ii = lax.broadcasted_iota(jnp.int32, (rows, rows), 0)
    jj = lax.broadcasted_iota(jnp.int32, (rows, rows), 1)
    same_chunk_before = jnp.logical_and(jj <= ii, jj // GDN_CHUNK == ii // GDN_CHUNK).astype(F32)
    gcum = lax.dot_general(same_chunk_before, g, NN, precision=HI, preferred_element_type=F32)
    return jax.nn.sigmoid(beta_raw), gcum


def _combine_fn(o0, o1, o2, l0, l1, l2):
    m = lax.stop_gradient(jnp.maximum(jnp.maximum(l0, l1), l2))
    e0, e1, e2 = jnp.exp(l0 - m), jnp.exp(l1 - m), jnp.exp(l2 - m)
    return ((e0 * o0 + e1 * o1 + e2 * o2) / (e0 + e1 + e2),)


def _loss_fn(x, target, gain):
    y = x * lax.rsqrt(jnp.mean(x * x, axis=-1, keepdims=True) + EPS) * gain
    err = y - target
    return (0.5 * jnp.mean(err * err, axis=-1, keepdims=True),)


def _rope_call(x, cos, sin, name):
    n_rows, width = x.shape
    tm = 512

    def body(x_ref, c_ref, s_ref, o_ref):
        v = x_ref[...]
        lane = lax.broadcasted_iota(jnp.int32, v.shape, 1)
        low = (lane % DSW_HEAD_DIM) < DSW_HEAD_DIM // 2
        half = DSW_HEAD_DIM // 2
        swapped = jnp.where(low, pltpu.roll(v, LANES - half, 1), pltpu.roll(v, half, 1))
        o_ref[...] = v * c_ref[...] + swapped * s_ref[...]

    tab = pl.BlockSpec((tm, LANES), lambda i, j: (i, 0))
    blk = pl.BlockSpec((tm, LANES), lambda i, j: (i, j))
    return pl.pallas_call(
        body, grid=(n_rows // tm, width // LANES), in_specs=[blk, tab, tab], out_specs=blk,
        out_shape=jax.ShapeDtypeStruct(x.shape, F32), name=name, compiler_params=_params(2),
    )(x, cos, sin)


def _rope_tables(n_tokens):
    half = DSW_HEAD_DIM // 2
    inv_freq = ROPE_THETA ** (-jnp.arange(half, dtype=F32) / half)
    ang = jnp.arange(n_tokens, dtype=F32)[:, None] * inv_freq[None, :]
    cos, sin = jnp.cos(ang), jnp.sin(ang)
    return jnp.tile(jnp.concatenate([cos, cos], 1), (1, 2)), jnp.tile(jnp.concatenate([-sin, sin], 1), (1, 2))


def _attn_probs(q, kp, kc, group, n):
    blk = DSW_BLOCK
    k = _each(lambda a, b: jnp.concatenate([a, b], axis=0).astype(BF16), kp, kc)
    s = _each(lambda a, b: lax.dot_general(a.astype(BF16), b, NT, preferred_element_type=F32)
              * (DSW_HEAD_DIM ** -0.5), q, k)
    blocks_per_seq = jnp.where(group == 0, 16, jnp.where(group == 1, 4, 1))
    first = (n % blocks_per_seq) == 0
    qi = lax.broadcasted_iota(jnp.int32, (blk, 2 * blk), 0)
    kj = lax.broadcasted_iota(jnp.int32, (blk, 2 * blk), 1)
    dist = qi + blk - kj
    valid = (dist >= 0) & (dist <= blk) & jnp.logical_or(kj >= blk, jnp.logical_not(first))
    s = _each(lambda a: jnp.where(valid, a, -1e30), s)
    m = _each(lambda a: jnp.max(a, axis=-1, keepdims=True), s)
    p = _each(lambda a, b: jnp.exp(a - b), s, m)
    l = _each(lambda a: jnp.sum(a, axis=-1, keepdims=True), p)
    return _each(lambda a, b: a / b, p, l), _each(lambda a, b: a + jnp.log(b), m, l), k


PAIRS_PER_GROUP = DSW_HEADS_PER_GROUP // 2


def _attn_specs(n_tokens):
    blk = DSW_BLOCK
    cur = pl.BlockSpec((PAIRS_PER_GROUP, blk, LANES), lambda g, n: (g, n, 0))
    prev = pl.BlockSpec((PAIRS_PER_GROUP, blk, LANES), lambda g, n: (g, jnp.maximum(n - 1, 0), 0))
    return cur, prev


def _heads_of(ref):
    pairs = [ref[p] for p in range(PAIRS_PER_GROUP)]
    return [x[:, s * DSW_HEAD_DIM:(s + 1) * DSW_HEAD_DIM] for x in pairs for s in range(2)]


def _pairs_of(heads):
    return [jnp.concatenate(heads[2 * p:2 * p + 2], axis=1) for p in range(PAIRS_PER_GROUP)]


def _attn_fwd(q, k, v):
    n_pairs, n_tokens, _ = q.shape
    cur, prev = _attn_specs(n_tokens)

    def body(q_ref, kp_ref, kc_ref, vp_ref, vc_ref, o_ref, l_ref):
        p, lse, _ = _attn_probs(_heads_of(q_ref), _heads_of(kp_ref), _heads_of(kc_ref),
                                pl.program_id(0), pl.program_id(1))
        vv = _each(lambda a, b: jnp.concatenate([a, b], axis=0).astype(BF16), _heads_of(vp_ref), _heads_of(vc_ref))
        o = _each(lambda a, b: lax.dot_general(a.astype(BF16), b, NN, preferred_element_type=F32), p, vv)
        lse_wide = _each(lambda a: jnp.broadcast_to(a, (DSW_BLOCK, DSW_HEAD_DIM)), lse)
        for pair, (o_pair, l_pair) in enumerate(zip(_pairs_of(o), _pairs_of(lse_wide))):
            o_ref[pair] = o_pair
            l_ref[pair] = l_pair

    return pl.pallas_call(
        body, grid=(n_pairs // PAIRS_PER_GROUP, n_tokens // DSW_BLOCK), in_specs=[cur, prev, cur, prev, cur],
        out_specs=(cur, cur), out_shape=(jax.ShapeDtypeStruct(q.shape, F32), jax.ShapeDtypeStruct(q.shape, F32)),
        name="attn_fwd", compiler_params=_params(2),
    )(q, k, k, v, v)


def _attn_bwd(q, k, v, do, dlse):
    n_pairs, n_tokens, _ = q.shape
    nblk = n_tokens // DSW_BLOCK
    cur, prev = _attn_specs(n_tokens)
    part = pl.BlockSpec((PAIRS_PER_GROUP, 1, 2 * DSW_BLOCK, LANES), lambda g, n: (g, n, 0, 0))
    scale = DSW_HEAD_DIM ** -0.5

    def body(q_ref, kp_ref, kc_ref, vp_ref, vc_ref, do_ref, dl_ref, dq_ref, dk_ref, dv_ref):
        qs = _heads_of(q_ref)
        p, _, kb = _attn_probs(qs, _heads_of(kp_ref), _heads_of(kc_ref), pl.program_id(0), pl.program_id(1))
        qb = _each(lambda a: a.astype(BF16), qs)
        vv = _each(lambda a, b: jnp.concatenate([a, b], axis=0).astype(BF16), _heads_of(vp_ref), _heads_of(vc_ref))
        dob = _each(lambda a: a.astype(BF16), _heads_of(do_ref))
        dp = _each(lambda a, b: lax.dot_general(a, b, NT, preferred_element_type=F32), dob, vv)
        dv = _each(lambda a, b: lax.dot_general(a.astype(BF16), b, TN, preferred_element_type=F32), p, dob)
        dl = _each(lambda a: jnp.sum(a, axis=-1, keepdims=True), _heads_of(dl_ref))
        ds = _each(lambda a, b, c: (a * (b - jnp.sum(b * a, axis=-1, keepdims=True) + c) * scale).astype(BF16),
                   p, dp, dl)
        dq = _each(lambda a, b: lax.dot_general(a, b, NN, preferred_element_type=F32), ds, kb)
        dk = _each(lambda a, b: lax.dot_general(a, b, TN, preferred_element_type=F32), ds, qb)
        for pair, (dq_pair, dk_pair, dv_pair) in enumerate(zip(_pairs_of(dq), _pairs_of(dk), _pairs_of(dv))):
            dq_ref[pair] = dq_pair
            dk_ref[pair, 0] = dk_pair
            dv_ref[pair, 0] = dv_pair

    partial_shape = jax.ShapeDtypeStruct((n_pairs, nblk, 2 * DSW_BLOCK, LANES), F32)
    dq, dkp, dvp = pl.pallas_call(
        body, grid=(n_pairs // PAIRS_PER_GROUP, nblk), in_specs=[cur, prev, cur, prev, cur, cur, cur],
        out_specs=(cur, part, part), out_shape=(jax.ShapeDtypeStruct(q.shape, F32), partial_shape, partial_shape),
        name="attn_bwd", compiler_params=_params(2),
    )(q, k, k, v, v, do, dlse)

    def fold(partial):
        own = partial[:, :, DSW_BLOCK:]
        from_next = jnp.pad(partial[:, 1:, :DSW_BLOCK], ((0, 0), (0, 1), (0, 0), (0, 0)))
        return (own + from_next).reshape(n_pairs, n_tokens, LANES)

    return dq, fold(dkp), fold(dvp)


def _to_heads(a):
    n_tokens = a.shape[0]
    outs = []
    for gi, d in enumerate(DSW_DILATIONS):
        blk = a[:, gi * 256:(gi + 1) * 256].reshape(n_tokens // d, d, PAIRS_PER_GROUP, LANES)
        outs.append(blk.transpose(2, 1, 0, 3).reshape(PAIRS_PER_GROUP, n_tokens, LANES))
    return jnp.concatenate(outs, 0)


def _from_heads(a):
    n_tokens = a.shape[1]
    outs = []
    for gi, d in enumerate(DSW_DILATIONS):
        blk = a[gi * PAIRS_PER_GROUP:(gi + 1) * PAIRS_PER_GROUP].reshape(PAIRS_PER_GROUP, d, n_tokens // d, LANES)
        outs.append(blk.transpose(2, 1, 0, 3).reshape(n_tokens, PAIRS_PER_GROUP * LANES))
    return outs


CONV_TILE = 512


def _shift_down(x, k, rows):
    return x if k == 0 else jnp.where(rows >= k, pltpu.roll(x, k, 0), 0.0)


def _shift_up(x, k, rows):
    n = x.shape[0]
    return x if k == 0 else jnp.where(rows < n - k, pltpu.roll(x, n - k, 0), 0.0)


def _conv_pre(x, w):
    rows = lax.broadcasted_iota(jnp.int32, x.shape, 0)
    acc = x * w[GDN_CONV - 1:GDN_CONV]
    for k in range(1, GDN_CONV):
        acc = acc + _shift_down(x, k, rows) * w[GDN_CONV - 1 - k:GDN_CONV - k]
    return acc, rows


def _conv_fwd(x, w):
    n_tokens, width = x.shape
    big = pl.BlockSpec((n_tokens, CONV_TILE), lambda j: (0, j))
    wsp = pl.BlockSpec((GDN_CONV, CONV_TILE), lambda j: (0, j))

    def body(x_ref, w_ref, o_ref):
        acc, _ = _conv_pre(x_ref[...], w_ref[...])
        o_ref[...] = acc * jax.nn.sigmoid(acc)

    return pl.pallas_call(
        body, grid=(width // CONV_TILE,), in_specs=[big, wsp], out_specs=big,
        out_shape=jax.ShapeDtypeStruct(x.shape, F32), name="conv_fwd", compiler_params=_params(1),
    )(x, w)


def _conv_bwd(x, w, dy):
    n_tokens, width = x.shape
    big = pl.BlockSpec((n_tokens, CONV_TILE), lambda j: (0, j))
    wsp = pl.BlockSpec((GDN_CONV, CONV_TILE), lambda j: (0, j))

    def body(x_ref, w_ref, dy_ref, dx_ref, dw_ref):
        xv, wv = x_ref[...], w_ref[...]
        acc, rows = _conv_pre(xv, wv)
        sg = jax.nn.sigmoid(acc)
        dacc = dy_ref[...] * (sg + acc * sg * (1.0 - sg))
        dx = dacc * wv[GDN_CONV - 1:GDN_CONV]
        for k in range(1, GDN_CONV):
            dx = dx + _shift_up(dacc, k, rows) * wv[GDN_CONV - 1 - k:GDN_CONV - k]
        dx_ref[...] = dx
        for k in range(GDN_CONV):
            dw_ref[GDN_CONV - 1 - k:GDN_CONV - k, :] = jnp.sum(dacc * _shift_down(xv, k, rows), axis=0, keepdims=True)

    return pl.pallas_call(
        body, grid=(width // CONV_TILE,), in_specs=[big, wsp, big], out_specs=(big, wsp),
        out_shape=(jax.ShapeDtypeStruct(x.shape, F32), jax.ShapeDtypeStruct(w.shape, F32)),
        name="conv_bwd", compiler_params=_params(1),
    )(x, w, dy)


def _dot(a, b, dn=NN):
    return lax.dot_general(a, b, dn, precision=HI, preferred_element_type=F32)


def _dot3(a, b, dn=NN):
    return lax.dot_general(a, b, dn, precision=lax.Precision.HIGH, preferred_element_type=F32)


def _bf16_dot(a, b, dn):
    return lax.dot_general(a.astype(BF16), b.astype(BF16), dn, preferred_element_type=F32)


_DOT_GRADS = {NN: (("g", "b", NT), ("a", "g", TN)), NT: (("g", "b", NN), ("g", "a", TN)),
              TN: (("b", "g", NT), ("a", "g", NN))}


def _make_bdot(dn):
    @jax.custom_vjp
    def op(a, b):
        return _bf16_dot(a, b, dn)

    def fwd(a, b):
        return op(a, b), (a, b)

    def bwd(saved, g):
        vals = dict(a=saved[0], b=saved[1], g=g)
        return tuple(_bf16_dot(vals[x], vals[y], form) for x, y, form in _DOT_GRADS[dn])

    op.defvjp(fwd, bwd)
    return op


_BDOTS = {dn: _make_bdot(dn) for dn in (NN, NT, TN)}


def _bdot(a, b, dn=NN):
    return _BDOTS[dn](a, b)


def _each(fn, *lists):
    return [fn(*items) for items in zip(*lists)]


def _gdn_chunks(q, k, v, b, gcum, state):
    c = GDN_CHUNK
    ii = lax.broadcasted_iota(jnp.int32, (c, c), 0)
    jj = lax.broadcasted_iota(jnp.int32, (c, c), 1)
    eye = (ii == jj).astype(F32)
    qn = _each(lambda x: x * lax.rsqrt(jnp.sum(x * x, axis=-1, keepdims=True) + EPS) * (GDN_HEAD_DIM ** -0.5), q)
    kn = _each(lambda x: x * lax.rsqrt(jnp.sum(x * x, axis=-1, keepdims=True) + EPS), k)
    gcum_i = _each(lambda x: jnp.broadcast_to(x, (c, c)), gcum)
    gcum_j = _each(jnp.transpose, gcum_i)
    decay = _each(lambda x, y: jnp.exp(jnp.where(jj <= ii, x - y, -1e30)), gcum_i, gcum_j)
    g_last = _each(lambda x: x[c - 1:c, :], gcum)
    e_gcum = _each(jnp.exp, gcum)
    kbeta = _each(lambda x, y: x * y, kn, b)
    vbeta = _each(lambda x, y: x * y, v, b)
    m = _each(lambda x, y, d: jnp.where(jj < ii, _bdot(x, y, NT) * d, 0.0), kbeta, kn, decay)
    inv = _each(lambda x: eye - x, m)
    power = _each(lambda x: _dot3(x, x), m)
    for step in range(5):
        inv = _each(lambda x, p: x + _dot3(x, p), inv, power)
        if step < 4:
            power = _each(lambda p: _dot3(p, p), power)
    u = _each(_dot3, inv, vbeta)
    w = _each(lambda x, y, e: _dot3(x, y * e), inv, kbeta, e_gcum)
    a_qk = _each(lambda x, y, d: _bdot(x, y, NT) * d, qn, kn, decay)
    v_new = _each(lambda x, y, s: x - _bdot(y, s), u, w, state)
    o = _each(lambda x, e, s, a, vn: _bdot(x * e, s) + _bdot(a, vn), qn, e_gcum, state, a_qk, v_new)
    new_state = _each(lambda s, gl, x, gc, vn: s * jnp.exp(gl) + _bdot(x * jnp.exp(gl - gc), vn, TN),
                      state, g_last, kn, gcum, v_new)
    return o, new_state


GDN_HEADS_PER_STEP = 8


GDN_TIME_TILE = 256


def _gdn_specs(n_tokens, reverse):
    hb, hd, tt = GDN_HEADS_PER_STEP, GDN_HEAD_DIM, GDN_TIME_TILE
    nb, nt = GDN_HEADS // hb, n_tokens // tt

    def when(t):
        return nt - 1 - t if reverse else t

    q = pl.BlockSpec((tt, hb * hd), lambda h, t: (when(t), h))
    k = pl.BlockSpec((tt, hb * hd), lambda h, t: (when(t), nb + h))
    v = pl.BlockSpec((tt, hb * hd), lambda h, t: (when(t), 2 * nb + h))
    vec = pl.BlockSpec((tt, hb), lambda h, t: (when(t), h))
    states = pl.BlockSpec((hb, tt // GDN_CHUNK, hd, hd), lambda h, t: (h, when(t), 0, 0))
    return q, k, v, vec, states


def _gdn_fwd(qkv, beta, g):
    n_tokens = qkv.shape[0]
    hb, hd, tt = GDN_HEADS_PER_STEP, GDN_HEAD_DIM, GDN_TIME_TILE
    n_chunks = tt // GDN_CHUNK
    q_s, k_s, v_s, vec, st = _gdn_specs(n_tokens, False)

    def body(q_ref, k_ref, v_ref, b_ref, g_ref, o_ref, st_ref, state):
        @pl.when(pl.program_id(1) == 0)
        def _():
            state[...] = jnp.zeros_like(state)

        def step(c, carry):
            r = pl.ds(pl.multiple_of(c * GDN_CHUNK, GDN_CHUNK), GDN_CHUNK)
            cols = [slice(h * hd, (h + 1) * hd) for h in range(hb)]
            old = [state[h] for h in range(hb)]
            o, new = _gdn_chunks(
                [q_ref[r, cs] for cs in cols], [k_ref[r, cs] for cs in cols], [v_ref[r, cs] for cs in cols],
                [b_ref[r, h:h + 1] for h in range(hb)], [g_ref[r, h:h + 1] for h in range(hb)], old)
            for h in range(hb):
                st_ref[h, c] = old[h]
                o_ref[r, cols[h]] = o[h]
                state[h] = new[h]
            return carry

        lax.fori_loop(0, n_chunks, step, 0)

    return pl.pallas_call(
        body, grid=(GDN_HEADS // hb, n_tokens // tt), in_specs=[q_s, k_s, v_s, vec, vec], out_specs=(q_s, st),
        out_shape=(jax.ShapeDtypeStruct((n_tokens, GDN_WIDTH), F32),
                   jax.ShapeDtypeStruct((GDN_HEADS, n_tokens // GDN_CHUNK, hd, hd), F32)),
        scratch_shapes=[pltpu.VMEM((hb, hd, hd), F32)],
        name="gdn_fwd", compiler_params=_params(2),
    )(qkv, qkv, qkv, beta, g)


def _gdn_bwd(qkv, beta, g, states, do):
    n_tokens = qkv.shape[0]
    hb, hd, tt = GDN_HEADS_PER_STEP, GDN_HEAD_DIM, GDN_TIME_TILE
    n_chunks = tt // GDN_CHUNK
    q_s, k_s, v_s, vec, st = _gdn_specs(n_tokens, True)

    def body(q_ref, k_ref, v_ref, b_ref, g_ref, st_ref, do_ref, dq_ref, dk_ref, dv_ref, db_ref, dg_ref, dstate):
        @pl.when(pl.program_id(1) == 0)
        def _():
            dstate[...] = jnp.zeros_like(dstate)

        def step(i, carry):
            c = n_chunks - 1 - i
            r = pl.ds(pl.multiple_of(c * GDN_CHUNK, GDN_CHUNK), GDN_CHUNK)
            cols = [slice(h * hd, (h + 1) * hd) for h in range(hb)]
            args = ([q_ref[r, cs] for cs in cols], [k_ref[r, cs] for cs in cols], [v_ref[r, cs] for cs in cols],
                    [b_ref[r, h:h + 1] for h in range(hb)], [g_ref[r, h:h + 1] for h in range(hb)],
                    [st_ref[h, c] for h in range(hb)])
            cts = ([do_ref[r, cs] for cs in cols], [dstate[h] for h in range(hb)])
            dq, dk, dv, db, dg, dst = jax.vjp(_gdn_chunks, *args)[1](cts)
            for h in range(hb):
                dq_ref[r, cols[h]] = dq[h]
                dk_ref[r, cols[h]] = dk[h]
                dv_ref[r, cols[h]] = dv[h]
                db_ref[r, h:h + 1] = db[h]
                dg_ref[r, h:h + 1] = dg[h]
                dstate[h] = dst[h]
            return carry

        lax.fori_loop(0, n_chunks, step, 0)

    wide = jax.ShapeDtypeStruct((n_tokens, GDN_WIDTH), F32)
    thin = jax.ShapeDtypeStruct(beta.shape, F32)
    dq, dk, dv, db, dg = pl.pallas_call(
        body, grid=(GDN_HEADS // hb, n_tokens // tt), in_specs=[q_s, k_s, v_s, vec, vec, st, q_s],
        out_specs=(q_s, q_s, q_s, vec, vec), out_shape=(wide, wide, wide, thin, thin),
        scratch_shapes=[pltpu.VMEM((hb, hd, hd), F32)],
        name="gdn_bwd", compiler_params=_params(2),
    )(qkv, qkv, qkv, beta, g, states, do)
    return jnp.concatenate([dq, dk, dv], axis=1), db, dg


FFN_ROW_TILE = 256


def _resident(shape):
    return pl.BlockSpec(shape, lambda i: (0,) * len(shape), pipeline_mode=pl.Buffered(1))


def _ffn_fwd(x, gain, wg, wu, wd, name):
    n_tokens, d = x.shape
    n_shards, n, _ = wg.shape
    tm = FFN_ROW_TILE

    def body(x_ref, gain_ref, wg_ref, wu_ref, wd_ref, o_ref, g_ref, u_ref):
        xv = x_ref[...]
        h = (xv * lax.rsqrt(jnp.mean(xv * xv, axis=-1, keepdims=True) + EPS) * gain_ref[...]).astype(BF16)
        acc = jnp.zeros((tm, d), F32)
        for j in range(n_shards):
            g = lax.dot_general(h, wg_ref[j], NT, preferred_element_type=F32)
            u = lax.dot_general(h, wu_ref[j], NT, preferred_element_type=F32)
            g_ref[j] = g
            u_ref[j] = u
            a = (g * jax.nn.sigmoid(g) * u).astype(BF16)
            acc = acc + lax.dot_general(a, wd_ref[j], NN, preferred_element_type=F32)
        o_ref[...] = xv + 0.5 * acc

    row = pl.BlockSpec((tm, d), lambda i: (i, 0))
    hid = pl.BlockSpec((n_shards, tm, n), lambda i: (0, i, 0))
    return pl.pallas_call(
        body, grid=(n_tokens // tm,),
        in_specs=[row, _resident(gain.shape), _resident(wg.shape), _resident(wu.shape), _resident(wd.shape)],
        out_specs=(row, hid, hid),
        out_shape=(jax.ShapeDtypeStruct(x.shape, F32), jax.ShapeDtypeStruct((n_shards, n_tokens, n), F32),
                   jax.ShapeDtypeStruct((n_shards, n_tokens, n), F32)),
        name=name, compiler_params=_params(1),
    )(x, gain, wg, wu, wd)


def _ffn_bwd_rows(x, gain, dy, g, u, wg, wu, wd, name):
    n_tokens, d = x.shape
    n_shards, n, _ = wg.shape
    tm = FFN_ROW_TILE

    def body(x_ref, gain_ref, dy_ref, g_ref, u_ref, wg_ref, wu_ref, wd_ref,
             dx_ref, dgain_ref, h_ref, dyh_ref, a_ref, dg_ref, du_ref):
        xv, dyv, gain_v = x_ref[...], dy_ref[...], gain_ref[...]
        r = lax.rsqrt(jnp.mean(xv * xv, axis=-1, keepdims=True) + EPS)
        xhat = xv * r
        h_ref[...] = (xhat * gain_v).astype(BF16)
        dyh = (0.5 * dyv).astype(BF16)
        dyh_ref[...] = dyh
        dh = jnp.zeros((tm, d), F32)
        for j in range(n_shards):
            da = lax.dot_general(dyh, wd_ref[j], NT, preferred_element_type=F32)
            gv, uv = g_ref[j], u_ref[j]
            sg = jax.nn.sigmoid(gv)
            silu = gv * sg
            a_ref[j] = (silu * uv).astype(BF16)
            dg = (da * uv * (sg + silu * (1.0 - sg))).astype(BF16)
            du = (da * silu).astype(BF16)
            dg_ref[j] = dg
            du_ref[j] = du
            dh = dh + lax.dot_general(dg, wg_ref[j], NN, preferred_element_type=F32)
            dh = dh + lax.dot_general(du, wu_ref[j], NN, preferred_element_type=F32)
        dxhat = dh * gain_v
        dx_ref[...] = dyv + r * (dxhat - xhat * jnp.mean(dxhat * xhat, axis=-1, keepdims=True))

        @pl.when(pl.program_id(0) == 0)
        def _():
            dgain_ref[...] = jnp.zeros_like(dgain_ref)

        dgain_ref[...] += jnp.sum(dh * xhat, axis=0, keepdims=True)

    row = pl.BlockSpec((tm, d), lambda i: (i, 0))
    hid = pl.BlockSpec((n_shards, tm, n), lambda i: (0, i, 0))
    hid_shape = (n_shards, n_tokens, n)
    return pl.pallas_call(
        body, grid=(n_tokens // tm,),
        in_specs=[row, _resident(gain.shape), row, hid, hid, _resident(wg.shape), _resident(wu.shape),
                  _resident(wd.shape)],
        out_specs=(row, pl.BlockSpec(gain.shape, lambda i: (0, 0)), row, row, hid, hid, hid),
        out_shape=(jax.ShapeDtypeStruct(x.shape, F32), jax.ShapeDtypeStruct(gain.shape, F32),
                   jax.ShapeDtypeStruct(x.shape, BF16), jax.ShapeDtypeStruct(x.shape, BF16),
                   jax.ShapeDtypeStruct(hid_shape, BF16), jax.ShapeDtypeStruct(hid_shape, BF16),
                   jax.ShapeDtypeStruct(hid_shape, BF16)),
        name=name, compiler_params=_params(1),
    )(x, gain, dy, g, u, wg, wu, wd)


def _ffn_bwd_weights(h, dyh, a, dg, du, name):
    n_shards, n_tokens, n = a.shape
    d = h.shape[1]

    def body(h_ref, dyh_ref, a_ref, dg_ref, du_ref, dwg_ref, dwu_ref, dwd_ref):
        hv = h_ref[...]
        dwg_ref[0] = lax.dot_general(dg_ref[0], hv, TN, preferred_element_type=F32)
        dwu_ref[0] = lax.dot_general(du_ref[0], hv, TN, preferred_element_type=F32)
        dwd_ref[0] = lax.dot_general(a_ref[0], dyh_ref[...], TN, preferred_element_type=F32)

    hid = pl.BlockSpec((1, n_tokens, n), lambda j: (j, 0, 0))
    out = pl.BlockSpec((1, n, d), lambda j: (j, 0, 0))
    return pl.pallas_call(
        body, grid=(n_shards,), in_specs=[_resident(h.shape), _resident(dyh.shape), hid, hid, hid],
        out_specs=(out, out, out), out_shape=(jax.ShapeDtypeStruct((n_shards, n, d), F32),) * 3,
        name=name, compiler_params=_params(1),
    )(h, dyh, a, dg, du)


IN_PIECES = (("wq_a", 0, 768), ("wk_a", 768, 1536), ("wv_a", 1536, 2304), ("w_qkvb", 2304, 5376),
             ("w_small", 5376, 5392), ("w_ggate", 5392, 6416), ("w_gatea", 6416, 7440), ("w_gateb", 7440, 8464))
IN_NAMES = tuple(name for name, _, _ in IN_PIECES)


def _in_rows(lo, hi):
    return lo, max(hi, lo + LANES)


def _in_proj_fwd(x, gain, wt):
    n_tokens, d = x.shape
    tm = FFN_ROW_TILE
    rows = [_in_rows(lo, hi) for _, lo, hi in IN_PIECES]

    def body(x_ref, gain_ref, wt_ref, *o_refs):
        xv = x_ref[...]
        h = (xv * lax.rsqrt(jnp.mean(xv * xv, axis=-1, keepdims=True) + EPS) * gain_ref[...]).astype(BF16)
        for (lo, hi), o_ref in zip(rows, o_refs):
            o_ref[...] = lax.dot_general(h, wt_ref[lo:hi, :], NT, preferred_element_type=F32)

    return pl.pallas_call(
        body, grid=(n_tokens // tm,),
        in_specs=[pl.BlockSpec((tm, d), lambda i: (i, 0)), _resident(gain.shape), _resident(wt.shape)],
        out_specs=tuple(pl.BlockSpec((tm, hi - lo), lambda i: (i, 0)) for lo, hi in rows),
        out_shape=tuple(jax.ShapeDtypeStruct((n_tokens, hi - lo), F32) for lo, hi in rows),
        name="in_proj_fwd", compiler_params=_params(1),
    )(x, gain, wt)


def _in_proj_bwd_rows(x, gain, dres, dzs, wt):
    n_tokens, d = x.shape
    tm = FFN_ROW_TILE
    n = len(dzs)
    rows = [_in_rows(lo, hi) for _, lo, hi in IN_PIECES]

    def body(x_ref, gain_ref, dres_ref, *refs):
        dz_refs, wt_ref = refs[:n], refs[n]
        dx_ref, dgain_ref, h_ref = refs[n + 1:]
        xv, gain_v = x_ref[...], gain_ref[...]
        r = lax.rsqrt(jnp.mean(xv * xv, axis=-1, keepdims=True) + EPS)
        xhat = xv * r
        h_ref[...] = (xhat * gain_v).astype(BF16)
        dh = jnp.zeros((tm, d), F32)
        for dz_ref, (lo, hi) in zip(dz_refs, rows):
            dh = dh + lax.dot_general(dz_ref[...].astype(BF16), wt_ref[lo:hi, :], NN, preferred_element_type=F32)
        dxhat = dh * gain_v
        dx_ref[...] = dres_ref[...] + r * (dxhat - xhat * jnp.mean(dxhat * xhat, axis=-1, keepdims=True))

        @pl.when(pl.program_id(0) == 0)
        def _():
            dgain_ref[...] = jnp.zeros_like(dgain_ref)

        dgain_ref[...] += jnp.sum(dh * xhat, axis=0, keepdims=True)

    row = pl.BlockSpec((tm, d), lambda i: (i, 0))
    return pl.pallas_call(
        body, grid=(n_tokens // tm,),
        in_specs=([row, _resident(gain.shape), row]
                  + [pl.BlockSpec((tm, dz.shape[1]), lambda i: (i, 0)) for dz in dzs] + [_resident(wt.shape)]),
        out_specs=(row, pl.BlockSpec(gain.shape, lambda i: (0, 0)), row),
        out_shape=(jax.ShapeDtypeStruct(x.shape, F32), jax.ShapeDtypeStruct(gain.shape, F32),
                   jax.ShapeDtypeStruct(x.shape, BF16)),
        name="in_proj_bwd_rows", compiler_params=_params(1),
    )(x, gain, dres, *dzs, wt)


def _in_proj_bwd_weight(dwt, h, dz, lo, hi, name):
    n_tokens, d = h.shape
    width = hi - lo
    tn = _tile(width, 512) if width >= LANES else width
    dz_tile = max(tn, LANES)

    def body(dwt_ref, h_ref, dz_ref, o_ref):
        o_ref[...] = lax.dot_general(dz_ref[:, :tn].astype(BF16), h_ref[...], TN, preferred_element_type=F32)

    return pl.pallas_call(
        body, grid=(width // tn,),
        in_specs=[ANY, _resident(h.shape), pl.BlockSpec((n_tokens, dz_tile), lambda j: (0, j))],
        out_specs=pl.BlockSpec((pl.Element(tn), pl.Element(d)), lambda j: (pl.multiple_of(lo + j * tn, 16), 0)),
        out_shape=jax.ShapeDtypeStruct(dwt.shape, F32), input_output_aliases={0: 0}, name=name,
        compiler_params=_params(1),
    )(dwt, h, dz)


def _split_small(z):
    return z[:, :GDN_HEADS], z[:, GDN_HEADS:2 * GDN_HEADS]


def _heads3(q, k, v):
    return _to_heads(q), _to_heads(k), _to_heads(v)


def _tokens6(o, lse):
    return tuple(_from_heads(o)) + tuple(_from_heads(lse))


def mixer_forward(x1, w, small):
    n_tokens = x1.shape[0]
    proj = dict(zip(IN_NAMES, _in_proj_fwd(x1, small["mix_norm"], w["w_in_t"])))
    cos, sin = _rope_tables(n_tokens)
    q_rot = _rope_call(proj["wq_a"], cos, sin, "rope_q")
    k_rot = _rope_call(proj["wk_a"], cos, sin, "rope_k")
    (qh, kh, vh), heads_vjp = jax.vjp(_heads3, q_rot, k_rot, proj["wv_a"])
    o, lse = _attn_fwd(qh, kh, vh)
    per_group, tokens_vjp = jax.vjp(_tokens6, o, lse)
    ya = _rowwise_fwd(_combine_fn, "combine", per_group, (), (), 512, 1)[0]
    pa = _matmul(ya, w["w_branch_a"], name="branch_a")
    qkv = _conv_fwd(proj["w_qkvb"], small["gdn_conv_w"])
    raw, small_vjp = jax.vjp(_split_small, proj["w_small"])
    gdn_params = (small["gdn_a_log"], small["gdn_dt_bias"])
    beta, gcum = _rowwise_fwd(_beta_decay_fn, "beta_decay", raw, (), gdn_params, 512, 1)
    ob, states = _gdn_fwd(qkv, beta, gcum)
    gate_in = (ob, proj["w_ggate"])
    yb = _rowwise_fwd(_outnorm_gate_fn, "outnorm_gate", gate_in, (), (small["gdn_out_norm"],), 512, GDN_HEADS)[0]
    pb = _matmul(yb, w["w_branch_b"], name="branch_b")
    merge_in = (proj["w_gatea"], proj["w_gateb"], pa, pb)
    merged = _rowwise_fwd(_merge_fn, "merge", merge_in, (), (), 256, 1)[0]
    x2 = _matmul(merged, w["w_out"], name="out", res=x1)
    saved = dict(x1=x1, proj=proj, cos=cos, sin=sin, heads_vjp=heads_vjp, heads=(qh, kh, vh), tokens_vjp=tokens_vjp,
                 per_group=per_group, ya=ya, qkv=qkv, raw=raw, small_vjp=small_vjp, beta=beta, gcum=gcum, states=states,
                 gate_in=gate_in, yb=yb, merge_in=merge_in, merged=merged)
    return x2, saved


def mixer_backward(dx2, s, w, small):
    proj = s["proj"]
    dmerged = _matmul(dx2, w["w_out"], name="out_da", tb=True)
    grads = dict(w_out=_matmul(s["merged"], dx2, name="out_dw", ta=True))
    dgate_a, dgate_b, dpa, dpb = _rowwise_bwd(_merge_fn, "merge_bwd", s["merge_in"], (), (), (dmerged,), 256, 1)
    dyb = _matmul(dpb, w["w_branch_b"], name="branch_b_da", tb=True)
    grads["w_branch_b"] = _matmul(s["yb"], dpb, name="branch_b_dw", ta=True)
    dya = _matmul(dpa, w["w_branch_a"], name="branch_a_da", tb=True)
    grads["w_branch_a"] = _matmul(s["ya"], dpa, name="branch_a_dw", ta=True)
    dob, dggate, grads["gdn_out_norm"] = _rowwise_bwd(
        _outnorm_gate_fn, "outnorm_gate_bwd", s["gate_in"], (), (small["gdn_out_norm"],), (dyb,), 512, GDN_HEADS)
    dqkv, dbeta, dgcum = _gdn_bwd(s["qkv"], s["beta"], s["gcum"], s["states"], dob)
    gdn_params = (small["gdn_a_log"], small["gdn_dt_bias"])
    dbeta_raw, ddecay_raw, grads["gdn_a_log"], grads["gdn_dt_bias"] = _rowwise_bwd(
        _beta_decay_fn, "beta_decay_bwd", s["raw"], (), gdn_params, (dbeta, dgcum), 512, 1)
    dsmall = s["small_vjp"]((dbeta_raw, ddecay_raw))[0]
    dqkvb, grads["gdn_conv_w"] = _conv_bwd(proj["w_qkvb"], small["gdn_conv_w"], dqkv)
    dper_group = _rowwise_bwd(_combine_fn, "combine_bwd", s["per_group"], (), (), (dya,), 512, 1)
    do, dlse = s["tokens_vjp"](tuple(dper_group))
    dqh, dkh, dvh = _attn_bwd(*s["heads"], do, dlse)
    dq_rot, dk_rot, dv = s["heads_vjp"]((dqh, dkh, dvh))
    dq = _rope_call(dq_rot, s["cos"], -s["sin"], "rope_q_bwd")
    dk = _rope_call(dk_rot, s["cos"], -s["sin"], "rope_k_bwd")
    dzs = (dq, dk, dv, dqkvb, dsmall, dggate, dgate_a, dgate_b)
    dx1, grads["mix_norm"], h = _in_proj_bwd_rows(s["x1"], small["mix_norm"], dx2, dzs, w["w_in_t"])
    dwt = lax.empty(w["w_in_t"].shape, F32)
    for (name, lo, hi), dz in zip(IN_PIECES, dzs):
        dwt = _in_proj_bwd_weight(dwt, h, dz, lo, hi, "in_proj_dw_" + name)
    grads["w_in_t"] = dwt
    return dx1, grads


def ffn_forward(x, gain, w, tag):
    out, g, u = _ffn_fwd(x, gain, w[tag + "_w_gate"], w[tag + "_w_up"], w[tag + "_w_down"], tag + "_fwd")
    return out, (x, g, u)


def ffn_backward(dy, saved, gain, w, tag):
    x, g, u = saved
    weights = (w[tag + "_w_gate"], w[tag + "_w_up"], w[tag + "_w_down"])
    dx, dgain, h, dyh, a, dg, du = _ffn_bwd_rows(x, gain, dy, g, u, *weights, tag + "_bwd_rows")
    return dx, dgain, _ffn_bwd_weights(h, dyh, a, dg, du, tag + "_bwd_weights")


def loss_head(x3, target, gain):
    row_loss = _rowwise_fwd(_loss_fn, "loss", (x3,), (target,), (gain,), 256, 1)[0]
    dx3, dgain = _rowwise_bwd(_loss_fn, "loss_bwd", (x3,), (target,), (gain,), (jnp.ones_like(row_loss),), 256, 1)
    return jnp.sum(row_loss), dx3, dgain


BIG_WEIGHTS = ("ffn1_w_gate", "ffn1_w_up", "ffn1_w_down", "w_in", "w_branch_a", "w_branch_b", "w_out",
               "ffn2_w_gate", "ffn2_w_up", "ffn2_w_down")
TRANSPOSED = ("ffn1_w_gate", "ffn1_w_up", "w_in", "ffn2_w_gate", "ffn2_w_up")
CONV_SHARD = (GDN_CONV, 3 * GDN_WIDTH // N_DEV)
SMALL_ROWS = 24
ANY = pl.BlockSpec(memory_space=pl.ANY)


TOKEN = jax.ShapeDtypeStruct((8, LANES), F32)


def _after(value, token):
    return value + token[0, 0].astype(value.dtype)


def _position():
    return lax.axis_index("x"), lax.axis_index("y"), lax.axis_index("c")


def all_gather_shards(shards, name):
    n = len(shards)

    def body(*refs):
        x_refs, out_refs = refs[:n], refs[n:2 * n]
        send_sems, recv_sems, local_sems = refs[2 * n + 1:]
        x, y, c = _position()
        me, sibling = (x, y, c), (x, y, 1 - c)
        chips = [(1 - x, y), (x, 1 - y), (1 - x, 1 - y)]

        def slab(a, px, py, pc):
            return out_refs[a].at[4 * px + 2 * py + pc]

        def copy(a, k, block, to, src=None):
            return pltpu.make_async_remote_copy(
                src_ref=slab(a, *block) if src is None else src, dst_ref=slab(a, *block),
                send_sem=send_sems.at[7 * a + k], recv_sem=recv_sems.at[7 * a + k], device_id=to, device_id_type=MESH)

        mine = [pltpu.make_async_copy(x_refs[a], slab(a, *me), local_sems.at[a]) for a in range(n)]
        for cp in mine:
            cp.start()
        first = []
        for j, chip in enumerate(chips):
            first += [copy(a, 1 + j, me, (*chip, c), src=x_refs[a]) for a in range(n)]
        first += [copy(a, 0, me, sibling, src=x_refs[a]) for a in range(n)]
        for cp in first:
            cp.start()
        passed = []
        for j, chip in enumerate(chips):
            for a in range(n):
                copy(a, 1 + j, (*chip, c), me).wait_recv()
                cp = copy(a, 4 + j, (*chip, c), sibling)
                cp.start()
                passed.append(cp)
        for a in range(n):
            copy(a, 0, sibling, me).wait_recv()
        for j, chip in enumerate(chips):
            for a in range(n):
                copy(a, 4 + j, (*chip, 1 - c), me).wait_recv()
        for cp in first + passed:
            cp.wait_send()
        for cp in mine:
            cp.wait()
        refs[2 * n][...] = jnp.zeros_like(refs[2 * n])

    outs = pl.pallas_call(
        body, out_shape=tuple(jax.ShapeDtypeStruct((N_DEV,) + s.shape, s.dtype) for s in shards) + (TOKEN,),
        in_specs=[ANY] * n, out_specs=(ANY,) * n + (pl.BlockSpec(memory_space=pltpu.VMEM),),
        scratch_shapes=[pltpu.SemaphoreType.DMA((7 * n,)), pltpu.SemaphoreType.DMA((7 * n,)),
                        pltpu.SemaphoreType.DMA((n,))],
        name=name,
    )(*shards)
    return outs[:n], outs[n]


def exchange_with_sibling(grads):
    n = len(grads)

    def body(*refs):
        g_refs, recv_refs = refs[:n], refs[n:2 * n]
        send_sems, recv_sems = refs[2 * n:]
        x, y, c = _position()
        copies = [pltpu.make_async_remote_copy(
            src_ref=g_refs[a].at[2 * k + 1 - c], dst_ref=recv_refs[a].at[k], send_sem=send_sems.at[4 * a + k],
            recv_sem=recv_sems.at[4 * a + k], device_id=(x, y, 1 - c), device_id_type=MESH)
            for k in range(4) for a in range(n)]
        for cp in copies:
            cp.start()
        for cp in copies:
            cp.wait()

    return pl.pallas_call(
        body, out_shape=tuple(jax.ShapeDtypeStruct((4,) + g.shape[1:], g.dtype) for g in grads),
        in_specs=[ANY] * n, out_specs=(ANY,) * n,
        scratch_shapes=[pltpu.SemaphoreType.DMA((4 * n,)), pltpu.SemaphoreType.DMA((4 * n,))], name="rs_sibling",
    )(*grads)


ELEMENTWISE_TILE_BYTES = 1536 * 1024


def _tile2(rows, cols):
    if rows % 256 == 0:
        return 256, cols
    if rows * cols * 4 > ELEMENTWISE_TILE_BYTES and cols % 256 == 0:
        return rows, 256
    return rows, cols


def add_sibling(grads, received, core, name):
    _, rows, width = grads.shape
    tr, tc = _tile2(rows, width)

    def body(c_ref, g_ref, r_ref, o_ref):
        o_ref[...] = (g_ref[...] + r_ref[...]).astype(BF16)

    blk = (1, tr, tc)
    return pl.pallas_call(
        body,
        grid_spec=pltpu.PrefetchScalarGridSpec(
            num_scalar_prefetch=1, grid=(4, rows // tr, width // tc),
            in_specs=[pl.BlockSpec(blk, lambda k, i, j, c_ref: (2 * k + c_ref[0], i, j)),
                      pl.BlockSpec(blk, lambda k, i, j, c_ref: (k, i, j))],
            out_specs=pl.BlockSpec(blk, lambda k, i, j, c_ref: (k, i, j))),
        out_shape=jax.ShapeDtypeStruct((4, rows, width), BF16), name=name, compiler_params=_params(3),
    )(core, grads, received)


HBM = pl.BlockSpec(memory_space=pltpu.HBM)
SEM = pl.BlockSpec(memory_space=pltpu.SEMAPHORE)
DATAFLOW_EFFECT = pltpu.SideEffectType.DATAFLOW_SIDE_EFFECTING
N_PEERS = N_DEV - 1


def _peer(mask):
    x, y, c = _position()
    px = 1 - x if mask & 4 else x
    py = 1 - y if mask & 2 else y
    pc = 1 - c if mask & 1 else c
    return (px, py, pc), 4 * px + 2 * py + pc


ALL_PEERS = tuple(range(1, N_DEV))
OTHER_CHIPS = (4, 2, 6)


def _exchange_peers(mode):
    return OTHER_CHIPS if mode == "chips" else ALL_PEERS


def _direct_copies(src_refs, land_refs, send_sems, recv_sems, mode):
    x, y, c = _position()
    me = 4 * x + 2 * y + c
    masks = _exchange_peers(mode)
    copies = []
    for a, (src, land) in enumerate(zip(src_refs, land_refs)):
        for slot, mask in enumerate(masks):
            peer, peer_index = _peer(mask)
            k = len(masks) * a + slot
            source = {"gather": lambda: src, "scatter": lambda: src.at[peer_index],
                      "chips": lambda: src.at[2 * peer[0] + peer[1]]}[mode]()
            copies.append(pltpu.make_async_remote_copy(
                src_ref=source, dst_ref=land.at[me] if mode == "gather" else land.at[slot],
                send_sem=send_sems.at[k], recv_sem=recv_sems.at[k], device_id=peer, device_id_type=MESH))
    return copies


def direct_exchange_start(arrays, mode, name):
    n = len(arrays)
    n_peers = len(_exchange_peers(mode))
    lands = [lax.empty((N_DEV,) + a.shape if mode == "gather" else (n_peers,) + a.shape[1:], a.dtype) for a in arrays]

    def body(*refs):
        src_refs, land_refs = refs[:n], refs[n:2 * n]
        send_sems, recv_sems = refs[2 * n], refs[2 * n + 1]
        token = refs[-1]
        for cp in _direct_copies(src_refs, land_refs, send_sems, recv_sems, mode):
            cp.start()
        token[...] = jnp.zeros_like(token)

    sems = pltpu.SemaphoreType.DMA((n_peers * n,))
    outs = pl.pallas_call(
        body, name=name,
        out_shape=(sems, sems) + tuple(pltpu.HBM(a.shape, a.dtype) for a in arrays)
        + tuple(pltpu.HBM(l.shape, l.dtype) for l in lands) + (TOKEN,),
        in_specs=[HBM] * (2 * n), out_specs=(SEM, SEM) + (HBM,) * (2 * n) + (pl.BlockSpec(memory_space=pltpu.VMEM),),
        input_output_aliases={i: 2 + i for i in range(2 * n)},
        compiler_params=pltpu.CompilerParams(has_side_effects=DATAFLOW_EFFECT),
    )(*[pltpu.with_memory_space_constraint(a, pltpu.HBM) for a in list(arrays) + lands])
    return outs[0], outs[1], outs[2:2 + n], outs[2 + n:2 + 2 * n], outs[-1]


def direct_exchange_wait(send_sems, recv_sems, arrays, lands, after, mode, name):
    n = len(arrays)

    def body(*refs):
        src_refs, land_refs = refs[:n], refs[n:2 * n]
        send_sems, recv_sems = refs[2 * n], refs[2 * n + 1]
        for cp in _direct_copies(src_refs, land_refs, send_sems, recv_sems, mode):
            cp.wait_send()
            cp.wait_recv()
        refs[-1][...] = jnp.zeros_like(refs[-1])

    outs = pl.pallas_call(
        body, name=name,
        out_shape=tuple(pltpu.HBM(a.shape, a.dtype) for a in arrays) + tuple(pltpu.HBM(l.shape, l.dtype) for l in lands)
        + (TOKEN,),
        in_specs=[HBM] * (2 * n) + [SEM, SEM, pl.BlockSpec(memory_space=pl.ANY)],
        out_specs=(HBM,) * (2 * n) + (pl.BlockSpec(memory_space=pltpu.VMEM),),
        input_output_aliases={i: i for i in range(2 * n)},
        compiler_params=pltpu.CompilerParams(has_side_effects=DATAFLOW_EFFECT),
    )(*arrays, *lands, send_sems, recv_sems, after)
    return outs[n:]


def adamw_direct(w, m, v, grads, received, me, name):
    row_per_tile = w.shape[0] != 1
    rows, cols = (w.shape[0], w.shape[2]) if row_per_tile else w.shape[-2:]
    tr, tc = _tile2(rows, cols)

    def body(me_ref, w_ref, m_ref, v_ref, own_ref, r_ref, g_ref, d_ref, nm_ref, nv_ref):
        gv = own_ref[0]
        for j in range(N_PEERS):
            gv = gv + r_ref[j].astype(F32)
        nm = ADAM_B1 * m_ref[...] + (1.0 - ADAM_B1) * gv
        nv = ADAM_B2 * v_ref[...] + (1.0 - ADAM_B2) * (gv * gv)
        m_hat = nm / (1.0 - ADAM_B1 ** ADAM_STEP)
        v_hat = nv / (1.0 - ADAM_B2 ** ADAM_STEP)
        g_ref[...] = gv
        d_ref[...] = -ADAM_LR * (m_hat / (jnp.sqrt(v_hat) + ADAM_EPS) + ADAM_WD * w_ref[...])
        nm_ref[...] = nm
        nv_ref[...] = nv

    if row_per_tile:
        one = pl.BlockSpec((tr, None, tc), lambda i, j, me_ref: (i, 0, j))
    else:
        one = pl.BlockSpec((None, tr, tc), lambda i, j, me_ref: (0, i, j))
    out = jax.ShapeDtypeStruct(w.shape, F32)
    return pl.pallas_call(
        body,
        grid_spec=pltpu.PrefetchScalarGridSpec(
            num_scalar_prefetch=1, grid=(rows // tr, cols // tc),
            in_specs=[one, one, one, pl.BlockSpec((1, tr, tc), lambda i, j, me_ref: (me_ref[0], i, j)),
                      pl.BlockSpec((N_PEERS, tr, tc), lambda i, j, me_ref: (0, i, j))],
            out_specs=(one,) * 4),
        out_shape=(out,) * 4, name=name, compiler_params=_params(2),
    )(me, w, m, v, grads, received)


def all_reduce_small(vals):
    rows, width = vals.shape

    def body(x_ref, out_ref, all_ref, send_sems, recv_sems):
        x, y, c = _position()
        me, sibling = (x, y, c), (x, y, 1 - c)
        chips = [(1 - x, y), (x, 1 - y), (1 - x, 1 - y)]

        def slab(px, py, pc):
            return all_ref.at[4 * px + 2 * py + pc]

        def copy(k, block, to, src=None):
            return pltpu.make_async_remote_copy(
                src_ref=slab(*block) if src is None else src, dst_ref=slab(*block),
                send_sem=send_sems.at[k], recv_sem=recv_sems.at[k], device_id=to, device_id_type=MESH)

        first = [copy(0, me, sibling, src=x_ref)]
        first += [copy(1 + j, me, (*chip, c), src=x_ref) for j, chip in enumerate(chips)]
        for cp in first:
            cp.start()
        all_ref[4 * x + 2 * y + c] = x_ref[...]
        passed = [copy(4 + j, (*chip, c), sibling) for j, chip in enumerate(chips)]
        for j, chip in enumerate(chips):
            copy(1 + j, (*chip, c), me).wait_recv()
            passed[j].start()
        copy(0, sibling, me).wait_recv()
        for j, chip in enumerate(chips):
            copy(4 + j, (*chip, 1 - c), me).wait_recv()
        for cp in first + passed:
            cp.wait_send()
        total = all_ref[0]
        for d in range(1, N_DEV):
            total = total + all_ref[d]
        out_ref[...] = total

    vmem = pl.BlockSpec(memory_space=pltpu.VMEM)
    return pl.pallas_call(
        body, out_shape=(jax.ShapeDtypeStruct(vals.shape, F32), jax.ShapeDtypeStruct((N_DEV, rows, width), F32)),
        in_specs=[vmem], out_specs=(vmem, vmem),
        scratch_shapes=[pltpu.SemaphoreType.DMA((7,)), pltpu.SemaphoreType.DMA((7,))], name="small_allreduce",
    )(vals)[0]


def adamw(w, g, m, v, name):
    shape = w.shape
    w2, g2, m2, v2 = [a.reshape((-1, shape[-1])) for a in (w, g, m, v)]
    rows, cols = w2.shape
    tr = 256 if rows % 256 == 0 else rows

    def body(w_ref, g_ref, m_ref, v_ref, d_ref, nm_ref, nv_ref):
        gv = g_ref[...]
        nm = ADAM_B1 * m_ref[...] + (1.0 - ADAM_B1) * gv
        nv = ADAM_B2 * v_ref[...] + (1.0 - ADAM_B2) * (gv * gv)
        m_hat = nm / (1.0 - ADAM_B1 ** ADAM_STEP)
        v_hat = nv / (1.0 - ADAM_B2 ** ADAM_STEP)
        d_ref[...] = -ADAM_LR * (m_hat / (jnp.sqrt(v_hat) + ADAM_EPS) + ADAM_WD * w_ref[...])
        nm_ref[...] = nm
        nv_ref[...] = nv

    blk = pl.BlockSpec((tr, cols), lambda i: (i, 0))
    out = jax.ShapeDtypeStruct((rows, cols), F32)
    outs = pl.pallas_call(
        body, grid=(rows // tr,), in_specs=[blk] * 4, out_specs=(blk,) * 3, out_shape=(out,) * 3,
        name=name, compiler_params=_params(1),
    )(w2, g2, m2, v2)
    return tuple(o.reshape(shape) for o in outs)


def adamw_summed(w, m, v, grads, from_sibling, received, me, name):
    rows, cols = w.shape[-2:]
    tr, tc = _tile2(rows, cols)

    def body(me_ref, w_ref, m_ref, v_ref, own_ref, sib_ref, r_ref, g_ref, d_ref, nm_ref, nv_ref):
        gv = own_ref[0] + sib_ref[0]
        for j in range(3):
            gv = gv + r_ref[j].astype(F32)
        nm = ADAM_B1 * m_ref[0] + (1.0 - ADAM_B1) * gv
        nv = ADAM_B2 * v_ref[0] + (1.0 - ADAM_B2) * (gv * gv)
        m_hat = nm / (1.0 - ADAM_B1 ** ADAM_STEP)
        v_hat = nv / (1.0 - ADAM_B2 ** ADAM_STEP)
        g_ref[0] = gv
        d_ref[0] = -ADAM_LR * (m_hat / (jnp.sqrt(v_hat) + ADAM_EPS) + ADAM_WD * w_ref[0])
        nm_ref[0] = nm
        nv_ref[0] = nv

    one = pl.BlockSpec((1, tr, tc), lambda i, j, me_ref: (0, i, j))
    out = jax.ShapeDtypeStruct((1, rows, cols), F32)
    return pl.pallas_call(
        body,
        grid_spec=pltpu.PrefetchScalarGridSpec(
            num_scalar_prefetch=1, grid=(rows // tr, cols // tc),
            in_specs=[one, one, one, pl.BlockSpec((1, tr, tc), lambda i, j, me_ref: (me_ref[0], i, j)),
                      pl.BlockSpec((1, tr, tc), lambda i, j, me_ref: (me_ref[1], i, j)),
                      pl.BlockSpec((3, tr, tc), lambda i, j, me_ref: (0, i, j))],
            out_specs=(one,) * 4),
        out_shape=(out,) * 4, name=name, compiler_params=_params(2),
    )(me, w, m, v, grads, from_sibling, received)


SMALL_VECTORS = ("ffn1_norm", "mix_norm", "ffn2_norm", "final_norm")


def _pack_small(gs):
    row = jnp.concatenate([gs["gdn_a_log"].reshape(-1), gs["gdn_dt_bias"].reshape(-1), gs["gdn_out_norm"].reshape(-1)])
    rows = [gs[n].reshape(1, D_MODEL) for n in SMALL_VECTORS]
    rows.append(jnp.pad(row, (0, D_MODEL - row.shape[0])).reshape(1, D_MODEL))
    rows.append(gs["gdn_conv_w"].reshape(-1, D_MODEL))
    packed = jnp.concatenate(rows, axis=0)
    return jnp.pad(packed, ((0, SMALL_ROWS - packed.shape[0]), (0, 0)))


def _unpack_small(packed):
    out = {n: packed[i].reshape(1, D_MODEL) for i, n in enumerate(SMALL_VECTORS)}
    row = packed[len(SMALL_VECTORS)]
    out["gdn_a_log"] = row[:GDN_HEADS].reshape(1, GDN_HEADS)
    out["gdn_dt_bias"] = row[GDN_HEADS:2 * GDN_HEADS].reshape(1, GDN_HEADS)
    out["gdn_out_norm"] = row[2 * GDN_HEADS:2 * GDN_HEADS + GDN_HEAD_DIM].reshape(1, GDN_HEAD_DIM)
    first = len(SMALL_VECTORS) + 1
    out["gdn_conv_w"] = packed[first:first + GDN_CONV * 3].reshape(GDN_CONV, 3 * GDN_WIDTH)
    return out


WEIGHTS = ("ffn1_norm", "ffn1_w_gate", "ffn1_w_up", "ffn1_w_down", "mix_norm", "w_in", "gdn_conv_w", "gdn_a_log",
           "gdn_dt_bias", "gdn_out_norm", "w_branch_a", "w_branch_b", "w_out", "ffn2_norm", "ffn2_w_gate",
           "ffn2_w_up", "ffn2_w_down", "final_norm")


def kernel(x, ffn1_norm, ffn1_w_gate, ffn1_w_up, ffn1_w_down, mix_norm, w_in, gdn_conv_w, gdn_a_log, gdn_dt_bias, gdn_out_norm, w_branch_a, w_branch_b, w_out, ffn2_norm, ffn2_w_gate, ffn2_w_up, ffn2_w_down, final_norm, loss_target, m_ffn1_norm, m_ffn1_w_gate, m_ffn1_w_up, m_ffn1_w_down, m_mix_norm, m_w_in, m_gdn_conv_w, m_gdn_a_log, m_gdn_dt_bias, m_gdn_out_norm, m_w_branch_a, m_w_branch_b, m_w_out, m_ffn2_norm, m_ffn2_w_gate, m_ffn2_w_up, m_ffn2_w_down, m_final_norm, v_ffn1_norm, v_ffn1_w_gate, v_ffn1_w_up, v_ffn1_w_down, v_mix_norm, v_w_in, v_gdn_conv_w, v_gdn_a_log, v_gdn_dt_bias, v_gdn_out_norm, v_w_branch_a, v_w_branch_b, v_w_out, v_ffn2_norm, v_ffn2_w_gate, v_ffn2_w_up, v_ffn2_w_down, v_final_norm):
    given = dict(locals())
    px, py, pc = _position()
    big_names = list(BIG_WEIGHTS)

    def shard_view(a, n):
        if n == "w_in":
            return a.transpose(2, 0, 1)
        return a.transpose(0, 2, 1) if n in TRANSPOSED else a

    def shard_unview(a, n):
        if n == "w_in":
            return a.transpose(1, 2, 0)
        return a.transpose(0, 2, 1) if n in TRANSPOSED else a

    me = 4 * px + 2 * py + pc
    me_index = me.astype(jnp.int32).reshape(1)
    late = [n for n in big_names if n.startswith("ffn2")]
    early = [n for n in big_names if n not in late]
    shards = {n: shard_view(given[n], n).reshape(given[n].shape[-1 if n in TRANSPOSED else -2], -1).astype(BF16)
              for n in big_names}
    early_slabs, early_done = all_gather_shards([shards[n] for n in early] + [gdn_conv_w[0]], "gather_weights")
    gathered = dict(zip(early + ["gdn_conv_w"], early_slabs))
    late_gather = direct_exchange_start([_after(shards[n], early_done) for n in late], "gather", "gather_ffn2_start")
    ffn1_norm = _after(ffn1_norm, late_gather[4])
    w = {n: gathered[n] for n in early if n.startswith("ffn")}
    w["w_in_t"] = gathered["w_in"].reshape(-1, D_MODEL)
    w["w_branch_a"] = gathered["w_branch_a"].transpose(1, 0, 2).reshape(256, D_MODEL)
    w["w_branch_b"] = gathered["w_branch_b"].reshape(D_MODEL, D_MODEL)
    w["w_out"] = gathered["w_out"].reshape(D_MODEL, D_MODEL)
    conv_full = gathered["gdn_conv_w"].transpose(1, 0, 2).reshape(GDN_CONV, 3 * GDN_WIDTH)
    small = dict(mix_norm=mix_norm, gdn_a_log=gdn_a_log, gdn_dt_bias=gdn_dt_bias, gdn_out_norm=gdn_out_norm,
                 gdn_conv_w=conv_full)

    x1, ffn1_saved = ffn_forward(x[0], ffn1_norm, w, "ffn1")
    x2, mixer_saved = mixer_forward(x1, w, small)
    late_lands = direct_exchange_wait(*late_gather[:4], x2, "gather", "gather_ffn2_wait")
    for n, land in zip(late, late_lands):
        w[n] = lax.dynamic_update_slice(land, shards[n][None], (me, 0, 0))
    x3, ffn2_saved = ffn_forward(x2, ffn2_norm, w, "ffn2")
    loss_local, dx3, g_final = loss_head(x3, loss_target[0], final_norm.reshape(1, D_MODEL))
    loss = lax.psum(loss_local, ("x", "y", "c"))
    dx2, g_ffn2_norm, dw2 = ffn_backward(dx3, ffn2_saved, ffn2_norm, w, "ffn2")
    late_scatter = direct_exchange_start([g.astype(BF16) for g in dw2], "scatter", "rs_ffn2_start")
    w_after = dict(w, w_out=_after(w["w_out"], late_scatter[4]))
    dx1, g_w = mixer_backward(dx2, mixer_saved, w_after, small)
    g_big = dict(zip(late, dw2))
    g_big["w_in"] = g_w["w_in_t"].reshape(N_DEV, -1, D_MODEL)
    g_big["w_branch_a"] = g_w["w_branch_a"].reshape(256, N_DEV, 128).transpose(1, 0, 2)
    g_big["w_branch_b"] = g_w["w_branch_b"].reshape(N_DEV, 128, D_MODEL)
    g_big["w_out"] = g_w["w_out"].reshape(N_DEV, 128, D_MODEL)
    middle = ["w_in", "w_branch_a", "w_branch_b", "w_out"]
    middle_scatter = direct_exchange_start([g_big[n].astype(BF16) for n in middle], "scatter", "rs_mixer_start")
    grad_x, g_ffn1_norm, dw1 = ffn_backward(dx1, ffn1_saved, _after(ffn1_norm, middle_scatter[4]), w, "ffn1")
    g_small = dict(ffn1_norm=g_ffn1_norm, ffn2_norm=g_ffn2_norm, final_norm=g_final,
                   **{n: g_w[n] for n in ("mix_norm", "gdn_a_log", "gdn_dt_bias", "gdn_out_norm", "gdn_conv_w")})

    first = [n for n in early if n.startswith("ffn1")]
    g_big.update(zip(first, dw1))
    g_list = [g_big[n] for n in first]
    core = pc.astype(jnp.int32).reshape(1)
    me_and_chip = jnp.stack([me, 2 * px + py]).astype(jnp.int32)
    from_sibling = exchange_with_sibling(g_list)
    partials = [add_sibling(g, r, core, "rs_add_" + n) for n, g, r in zip(first, g_list, from_sibling)]
    first_chips = direct_exchange_start(partials, "chips", "rs_ffn1_start")

    def state_of(n):
        return [shard_view(given[p + n], n) for p in ("", "m_", "v_")]

    results = {}
    late_received = direct_exchange_wait(*late_scatter[:4], first_chips[4], "scatter", "rs_ffn2_wait")
    middle_received = direct_exchange_wait(*middle_scatter[:4], first_chips[4], "scatter", "rs_mixer_wait")
    for n, recv in zip(late + middle, list(late_received[:-1]) + list(middle_received[:-1])):
        outs = adamw_direct(*state_of(n), g_big[n], recv, me_index, "adamw_" + n)
        results[n] = tuple(shard_unview(o, n) for o in outs)

    done = results["w_out"][1]
    from_chips = direct_exchange_wait(*first_chips[:4], done, "chips", "rs_ffn1_wait")
    for n, g, sib, recv in zip(first, g_list, from_sibling, from_chips):
        outs = adamw_summed(*state_of(n), g, sib, recv, me_and_chip, "adamw_" + n)
        results[n] = tuple(shard_unview(o, n) for o in outs)

    small_sum = _unpack_small(all_reduce_small(_after(_pack_small(g_small), from_chips[-1])))
    conv_cols = CONV_SHARD[1]
    small_sum["gdn_conv_w"] = lax.dynamic_slice(small_sum["gdn_conv_w"], (0, me * conv_cols), (GDN_CONV, conv_cols))
    for n in WEIGHTS:
        if n not in results:
            g = small_sum[n].reshape(given[n].shape)
            results[n] = (g,) + adamw(given[n], g, given["m_" + n], given["v_" + n], "adamw_" + n)

    outs = [[results[n][i] for n in WEIGHTS] for i in range(4)]
    return (loss, grad_x[None], *outs[0], *outs[1], *outs[2], *outs[3])
```

```python
import jax
import jax.numpy as jnp
from jax import lax
from jax.experimental import pallas as pl
from jax.experimental.pallas import tpu as pltpu

F32 = jnp.float32
BF16 = jnp.bfloat16
HI = lax.Precision.HIGHEST
MESH = pl.DeviceIdType.MESH

N_DEV = 8
D_MODEL = 1024
EPS = 1e-6
ROPE_THETA = 10000.0
DSW_DILATIONS = (1, 4, 16)
DSW_HEADS_PER_GROUP = 4
DSW_HEAD_DIM = 64
DSW_BLOCK = 128
GDN_HEADS = 8
GDN_HEAD_DIM = 128
GDN_WIDTH = 1024
GDN_CONV = 4
GDN_CHUNK = 64

ADAM_LR = 0.001
ADAM_B1 = 0.9
ADAM_B2 = 0.999
ADAM_EPS = 1e-08
ADAM_WD = 0.01
ADAM_STEP = 10

VMEM_LIMIT_BYTES = 56 * 1024 * 1024
LANES = 128

NN = (((1,), (0,)), ((), ()))
NT = (((1,), (1,)), ((), ()))
TN = (((0,), (0,)), ((), ()))


def _params(n_grid):
    return pltpu.CompilerParams(dimension_semantics=("arbitrary",) * n_grid, vmem_limit_bytes=VMEM_LIMIT_BYTES)


def _tile(n, pref):
    best = None
    t = LANES
    while t <= min(n, pref):
        if n % t == 0:
            best = t
        t += LANES
    return n if best is None else best


def _matmul(a, b, *, name, ta=False, tb=False, res=None, scale=1.0):
    K, M = a.shape if ta else a.shape[::-1]
    N = b.shape[0] if tb else b.shape[1]
    assert (b.shape[1] if tb else b.shape[0]) == K, (a.shape, b.shape, ta, tb)
    tm = _tile(M, 512)
    tn = _tile(N, 512)
    dn = (((0 if ta else 1,), (1 if tb else 0,)), ((), ()))

    def body(*refs):
        a_ref, b_ref = refs[:2]
        o_ref = refs[-1]
        acc = lax.dot_general(a_ref[...].astype(BF16), b_ref[...].astype(BF16), dn, preferred_element_type=F32)
        if scale != 1.0:
            acc = acc * scale
        if res is not None:
            acc = refs[2][...] + acc
        o_ref[...] = acc

    a_spec = pl.BlockSpec((K, tm), lambda i, j: (0, i)) if ta else pl.BlockSpec((tm, K), lambda i, j: (i, 0))
    b_spec = pl.BlockSpec((tn, K), lambda i, j: (j, 0)) if tb else pl.BlockSpec((K, tn), lambda i, j: (0, j))
    o_spec = pl.BlockSpec((tm, tn), lambda i, j: (i, j))
    ins, specs = [a, b], [a_spec, b_spec]
    if res is not None:
        ins.append(res)
        specs.append(o_spec)
    return pl.pallas_call(
        body, grid=(M // tm, N // tn), in_specs=specs, out_specs=o_spec,
        out_shape=jax.ShapeDtypeStruct((M, N), F32), name=name, compiler_params=_params(2),
    )(*ins)


def _rw_specs(arrs, tm, nblk):
    return [pl.BlockSpec((tm, a.shape[1] // nblk), lambda i, j: (i, j)) for a in arrs]


def _rowwise_fwd(fn, name, rows, consts, params, tm, nblk):
    n_rows = rows[0].shape[0]
    tm = min(tm, n_rows)
    ins = list(rows) + list(consts)
    avals = [jax.ShapeDtypeStruct((tm, a.shape[1] // nblk), a.dtype) for a in ins]
    avals += [jax.ShapeDtypeStruct(p.shape, p.dtype) for p in params]
    out_avals = jax.eval_shape(fn, *avals)
    n_in = len(ins) + len(params)

    def body(*refs):
        outs = fn(*[r[...] for r in refs[:n_in]])
        for r, o in zip(refs[n_in:], outs):
            r[...] = o.astype(r.dtype)

    return pl.pallas_call(
        body, grid=(n_rows // tm, nblk),
        in_specs=_rw_specs(ins, tm, nblk) + [pl.BlockSpec(p.shape, lambda i, j: (0, 0)) for p in params],
        out_specs=tuple(pl.BlockSpec((tm, o.shape[1]), lambda i, j: (i, j)) for o in out_avals),
        out_shape=tuple(jax.ShapeDtypeStruct((n_rows, o.shape[1] * nblk), o.dtype) for o in out_avals),
        name=name, compiler_params=_params(2),
    )(*ins, *params)


def _rowwise_bwd(fn, name, rows, consts, params, cts, tm, nblk):
    n_rows = rows[0].shape[0]
    tm = min(tm, n_rows)
    nr, nc, npar, nct = len(rows), len(consts), len(params), len(cts)

    def body(*refs):
        rv = [r[...] for r in refs[:nr]]
        cv = [r[...] for r in refs[nr:nr + nc]]
        pv = [r[...] for r in refs[nr + nc:nr + nc + npar]]
        ctv = [r[...] for r in refs[nr + nc + npar:nr + nc + npar + nct]]
        outs = refs[nr + nc + npar + nct:]
        _, vjp = jax.vjp(lambda *d: fn(*d[:nr], *cv, *d[nr:]), *rv, *pv)
        grads = vjp(tuple(ctv))
        for k in range(nr):
            outs[k][...] = grads[k]
        first = jnp.logical_and(pl.program_id(0) == 0, pl.program_id(1) == 0)
        for k in range(npar):
            ref = outs[nr + k]

            @pl.when(first)
            def _(ref=ref):
                ref[...] = jnp.zeros_like(ref)

            ref[...] += grads[nr + k]

    ins = list(rows) + list(consts)
    return pl.pallas_call(
        body, grid=(n_rows // tm, nblk),
        in_specs=(_rw_specs(ins, tm, nblk) + [pl.BlockSpec(p.shape, lambda i, j: (0, 0)) for p in params]
                  + _rw_specs(cts, tm, nblk)),
        out_specs=tuple(_rw_specs(rows, tm, nblk) + [pl.BlockSpec(p.shape, lambda i, j: (0, 0)) for p in params]),
        out_shape=tuple([jax.ShapeDtypeStruct(a.shape, F32) for a in rows]
                        + [jax.ShapeDtypeStruct(p.shape, F32) for p in params]),
        name=name, compiler_params=_params(2),
    )(*ins, *params, *cts)


def _merge_fn(ga, gb, pa, pb):
    return (jax.nn.sigmoid(ga) * pa + jax.nn.sigmoid(gb) * pb,)


def _outnorm_gate_fn(o, gate, gain):
    y = o * lax.rsqrt(jnp.mean(o * o, axis=-1, keepdims=True) + EPS) * gain
    return (y * (gate * jax.nn.sigmoid(gate)),)


def _beta_decay_fn(beta_raw, decay_raw, a_log, dt_bias):
    z = decay_raw + dt_bias
    softplus = jnp.maximum(z, 0.0) + jnp.log(1.0 + jnp.exp(-jnp.abs(z)))
    g = -jnp.exp(a_log) * softplus
    rows = g.shape[0]
    ii = lax.broadcasted_iota(jnp.int32, (rows, rows), 0)
    jj = lax.broadcasted_iota(jnp.int32, (rows, rows), 1)
    same_chunk_before = jnp.logical_and(jj <= ii, jj // GDN_CHUNK == ii // GDN_CHUNK).astype(F32)
    gcum = lax.dot_general(same_chunk_before, g, NN, precision=HI, preferred_element_type=F32)
    return jax.nn.sigmoid(beta_raw), gcum


def _combine_fn(o0, o1, o2, l0, l1, l2):
    m = lax.stop_gradient(jnp.maximum(jnp.maximum(l0, l1), l2))
    e0, e1, e2 = jnp.exp(l0 - m), jnp.exp(l1 - m), jnp.exp(l2 - m)
    return ((e0 * o0 + e1 * o1 + e2 * o2) / (e0 + e1 + e2),)


def _loss_fn(x, target, gain):
    y = x * lax.rsqrt(jnp.mean(x * x, axis=-1, keepdims=True) + EPS) * gain
    err = y - target
    return (0.5 * jnp.mean(err * err, axis=-1, keepdims=True),)


def _rope_call(x, cos, sin, name):
    n_rows, width = x.shape
    tm = 512

    def body(x_ref, c_ref, s_ref, o_ref):
        v = x_ref[...]
        lane = lax.broadcasted_iota(jnp.int32, v.shape, 1)
        low = (lane % DSW_HEAD_DIM) < DSW_HEAD_DIM // 2
        half = DSW_HEAD_DIM // 2
        swapped = jnp.where(low, pltpu.roll(v, LANES - half, 1), pltpu.roll(v, half, 1))
        o_ref[...] = v * c_ref[...] + swapped * s_ref[...]

    tab = pl.BlockSpec((tm, LANES), lambda i, j: (i, 0))
    blk = pl.BlockSpec((tm, LANES), lambda i, j: (i, j))
    return pl.pallas_call(
        body, grid=(n_rows // tm, width // LANES), in_specs=[blk, tab, tab], out_specs=blk,
        out_shape=jax.ShapeDtypeStruct(x.shape, F32), name=name, compiler_params=_params(2),
    )(x, cos, sin)


def _rope_tables(n_tokens):
    half = DSW_HEAD_DIM // 2
    inv_freq = ROPE_THETA ** (-jnp.arange(half, dtype=F32) / half)
    ang = jnp.arange(n_tokens, dtype=F32)[:, None] * inv_freq[None, :]
    cos, sin = jnp.cos(ang), jnp.sin(ang)
    return jnp.tile(jnp.concatenate([cos, cos], 1), (1, 2)), jnp.tile(jnp.concatenate([-sin, sin], 1), (1, 2))


def _attn_probs(q, kp, kc, group, n):
    blk = DSW_BLOCK
    k = _each(lambda a, b: jnp.concatenate([a, b], axis=0).astype(BF16), kp, kc)
    s = _each(lambda a, b: lax.dot_general(a.astype(BF16), b, NT, preferred_element_type=F32)
              * (DSW_HEAD_DIM ** -0.5), q, k)
    blocks_per_seq = jnp.where(group == 0, 16, jnp.where(group == 1, 4, 1))
    first = (n % blocks_per_seq) == 0
    qi = lax.broadcasted_iota(jnp.int32, (blk, 2 * blk), 0)
    kj = lax.broadcasted_iota(jnp.int32, (blk, 2 * blk), 1)
    dist = qi + blk - kj
    valid = (dist >= 0) & (dist <= blk) & jnp.logical_or(kj >= blk, jnp.logical_not(first))
    s = _each(lambda a: jnp.where(valid, a, -1e30), s)
    m = _each(lambda a: jnp.max(a, axis=-1, keepdims=True), s)
    p = _each(lambda a, b: jnp.exp(a - b), s, m)
    l = _each(lambda a: jnp.sum(a, axis=-1, keepdims=True), p)
    return _each(lambda a, b: a / b, p, l), _each(lambda a, b: a + jnp.log(b), m, l), k


PAIRS_PER_GROUP = DSW_HEADS_PER_GROUP // 2


def _attn_specs(n_tokens):
    blk = DSW_BLOCK
    cur = pl.BlockSpec((PAIRS_PER_GROUP, blk, LANES), lambda g, n: (g, n, 0))
    prev = pl.BlockSpec((PAIRS_PER_GROUP, blk, LANES), lambda g, n: (g, jnp.maximum(n - 1, 0), 0))
    return cur, prev


def _heads_of(ref):
    pairs = [ref[p] for p in range(PAIRS_PER_GROUP)]
    return [x[:, s * DSW_HEAD_DIM:(s + 1) * DSW_HEAD_DIM] for x in pairs for s in range(2)]


def _pairs_of(heads):
    return [jnp.concatenate(heads[2 * p:2 * p + 2], axis=1) for p in range(PAIRS_PER_GROUP)]


def _attn_fwd(q, k, v):
    n_pairs, n_tokens, _ = q.shape
    cur, prev = _attn_specs(n_tokens)

    def body(q_ref, kp_ref, kc_ref, vp_ref, vc_ref, o_ref, l_ref):
        p, lse, _ = _attn_probs(_heads_of(q_ref), _heads_of(kp_ref), _heads_of(kc_ref),
                                pl.program_id(0), pl.program_id(1))
        vv = _each(lambda a, b: jnp.concatenate([a, b], axis=0).astype(BF16), _heads_of(vp_ref), _heads_of(vc_ref))
        o = _each(lambda a, b: lax.dot_general(a.astype(BF16), b, NN, preferred_element_type=F32), p, vv)
        lse_wide = _each(lambda a: jnp.broadcast_to(a, (DSW_BLOCK, DSW_HEAD_DIM)), lse)
        for pair, (o_pair, l_pair) in enumerate(zip(_pairs_of(o), _pairs_of(lse_wide))):
            o_ref[pair] = o_pair
            l_ref[pair] = l_pair

    return pl.pallas_call(
        body, grid=(n_pairs // PAIRS_PER_GROUP, n_tokens // DSW_BLOCK), in_specs=[cur, prev, cur, prev, cur],
        out_specs=(cur, cur), out_shape=(jax.ShapeDtypeStruct(q.shape, F32), jax.ShapeDtypeStruct(q.shape, F32)),
        name="attn_fwd", compiler_params=_params(2),
    )(q, k, k, v, v)


def _attn_bwd(q, k, v, do, dlse):
    n_pairs, n_tokens, _ = q.shape
    nblk = n_tokens // DSW_BLOCK
    cur, prev = _attn_specs(n_tokens)
    part = pl.BlockSpec((PAIRS_PER_GROUP, 1, 2 * DSW_BLOCK, LANES), lambda g, n: (g, n, 0, 0))
    scale = DSW_HEAD_DIM ** -0.5

    def body(q_ref, kp_ref, kc_ref, vp_ref, vc_ref, do_ref, dl_ref, dq_ref, dk_ref, dv_ref):
        qs = _heads_of(q_ref)
        p, _, kb = _attn_probs(qs, _heads_of(kp_ref), _heads_of(kc_ref), pl.program_id(0), pl.program_id(1))
        qb = _each(lambda a: a.astype(BF16), qs)
        vv = _each(lambda a, b: jnp.concatenate([a, b], axis=0).astype(BF16), _heads_of(vp_ref), _heads_of(vc_ref))
        dob = _each(lambda a: a.astype(BF16), _heads_of(do_ref))
        dp = _each(lambda a, b: lax.dot_general(a, b, NT, preferred_element_type=F32), dob, vv)
        dv = _each(lambda a, b: lax.dot_general(a.astype(BF16), b, TN, preferred_element_type=F32), p, dob)
        dl = _each(lambda a: jnp.sum(a, axis=-1, keepdims=True), _heads_of(dl_ref))
        ds = _each(lambda a, b, c: (a * (b - jnp.sum(b * a, axis=-1, keepdims=True) + c) * scale).astype(BF16),
                   p, dp, dl)
        dq = _each(lambda a, b: lax.dot_general(a, b, NN, preferred_element_type=F32), ds, kb)
        dk = _each(lambda a, b: lax.dot_general(a, b, TN, preferred_element_type=F32), ds, qb)
        for pair, (dq_pair, dk_pair, dv_pair) in enumerate(zip(_pairs_of(dq), _pairs_of(dk), _pairs_of(dv))):
            dq_ref[pair] = dq_pair
            dk_ref[pair, 0] = dk_pair
            dv_ref[pair, 0] = dv_pair

    partial_shape = jax.ShapeDtypeStruct((n_pairs, nblk, 2 * DSW_BLOCK, LANES), F32)
    dq, dkp, dvp = pl.pallas_call(
        body, grid=(n_pairs // PAIRS_PER_GROUP, nblk), in_specs=[cur, prev, cur, prev, cur, cur, cur],
        out_specs=(cur, part, part), out_shape=(jax.ShapeDtypeStruct(q.shape, F32), partial_shape, partial_shape),
        name="attn_bwd", compiler_params=_params(2),
    )(q, k, k, v, v, do, dlse)

    def fold(partial):
        own = partial[:, :, DSW_BLOCK:]
        from_next = jnp.pad(partial[:, 1:, :DSW_BLOCK], ((0, 0), (0, 1), (0, 0), (0, 0)))
        return (own + from_next).reshape(n_pairs, n_tokens, LANES)

    return dq, fold(dkp), fold(dvp)


def _to_heads(a):
    n_tokens = a.shape[0]
    outs = []
    for gi, d in enumerate(DSW_DILATIONS):
        blk = a[:, gi * 256:(gi + 1) * 256].reshape(n_tokens // d, d, PAIRS_PER_GROUP, LANES)
        outs.append(blk.transpose(2, 1, 0, 3).reshape(PAIRS_PER_GROUP, n_tokens, LANES))
    return jnp.concatenate(outs, 0)


def _from_heads(a):
    n_tokens = a.shape[1]
    outs = []
    for gi, d in enumerate(DSW_DILATIONS):
        blk = a[gi * PAIRS_PER_GROUP:(gi + 1) * PAIRS_PER_GROUP].reshape(PAIRS_PER_GROUP, d, n_tokens // d, LANES)
        outs.append(blk.transpose(2, 1, 0, 3).reshape(n_tokens, PAIRS_PER_GROUP * LANES))
    return outs


CONV_TILE = 512


def _shift_down(x, k, rows):
    return x if k == 0 else jnp.where(rows >= k, pltpu.roll(x, k, 0), 0.0)


def _shift_up(x, k, rows):
    n = x.shape[0]
    return x if k == 0 else jnp.where(rows < n - k, pltpu.roll(x, n - k, 0), 0.0)


def _conv_pre(x, w):
    rows = lax.broadcasted_iota(jnp.int32, x.shape, 0)
    acc = x * w[GDN_CONV - 1:GDN_CONV]
    for k in range(1, GDN_CONV):
        acc = acc + _shift_down(x, k, rows) * w[GDN_CONV - 1 - k:GDN_CONV - k]
    return acc, rows


def _conv_fwd(x, w):
    n_tokens, width = x.shape
    big = pl.BlockSpec((n_tokens, CONV_TILE), lambda j: (0, j))
    wsp = pl.BlockSpec((GDN_CONV, CONV_TILE), lambda j: (0, j))

    def body(x_ref, w_ref, o_ref):
        acc, _ = _conv_pre(x_ref[...], w_ref[...])
        o_ref[...] = acc * jax.nn.sigmoid(acc)

    return pl.pallas_call(
        body, grid=(width // CONV_TILE,), in_specs=[big, wsp], out_specs=big,
        out_shape=jax.ShapeDtypeStruct(x.shape, F32), name="conv_fwd", compiler_params=_params(1),
    )(x, w)


def _conv_bwd(x, w, dy):
    n_tokens, width = x.shape
    big = pl.BlockSpec((n_tokens, CONV_TILE), lambda j: (0, j))
    wsp = pl.BlockSpec((GDN_CONV, CONV_TILE), lambda j: (0, j))

    def body(x_ref, w_ref, dy_ref, dx_ref, dw_ref):
        xv, wv = x_ref[...], w_ref[...]
        acc, rows = _conv_pre(xv, wv)
        sg = jax.nn.sigmoid(acc)
        dacc = dy_ref[...] * (sg + acc * sg * (1.0 - sg))
        dx = dacc * wv[GDN_CONV - 1:GDN_CONV]
        for k in range(1, GDN_CONV):
            dx = dx + _shift_up(dacc, k, rows) * wv[GDN_CONV - 1 - k:GDN_CONV - k]
        dx_ref[...] = dx
        for k in range(GDN_CONV):
            dw_ref[GDN_CONV - 1 - k:GDN_CONV - k, :] = jnp.sum(dacc * _shift_down(xv, k, rows), axis=0, keepdims=True)

    return pl.pallas_call(
        body, grid=(width // CONV_TILE,), in_specs=[big, wsp, big], out_specs=(big, wsp),
        out_shape=(jax.ShapeDtypeStruct(x.shape, F32), jax.ShapeDtypeStruct(w.shape, F32)),
        name="conv_bwd", compiler_params=_params(1),
    )(x, w, dy)


def _dot(a, b, dn=NN):
    return lax.dot_general(a, b, dn, precision=HI, preferred_element_type=F32)


def _dot3(a, b, dn=NN):
    return lax.dot_general(a, b, dn, precision=lax.Precision.HIGH, preferred_element_type=F32)


def _bf16_dot(a, b, dn):
    return lax.dot_general(a.astype(BF16), b.astype(BF16), dn, preferred_element_type=F32)


_DOT_GRADS = {NN: (("g", "b", NT), ("a", "g", TN)), NT: (("g", "b", NN), ("g", "a", TN)),
              TN: (("b", "g", NT), ("a", "g", NN))}


def _make_bdot(dn):
    @jax.custom_vjp
    def op(a, b):
        return _bf16_dot(a, b, dn)

    def fwd(a, b):
        return op(a, b), (a, b)

    def bwd(saved, g):
        vals = dict(a=saved[0], b=saved[1], g=g)
        return tuple(_bf16_dot(vals[x], vals[y], form) for x, y, form in _DOT_GRADS[dn])

    op.defvjp(fwd, bwd)
    return op


_BDOTS = {dn: _make_bdot(dn) for dn in (NN, NT, TN)}


def _bdot(a, b, dn=NN):
    return _BDOTS[dn](a, b)


def _each(fn, *lists):
    return [fn(*items) for items in zip(*lists)]


def _gdn_chunks(q, k, v, b, gcum, state):
    c = GDN_CHUNK
    ii = lax.broadcasted_iota(jnp.int32, (c, c), 0)
    jj = lax.broadcasted_iota(jnp.int32, (c, c), 1)
    eye = (ii == jj).astype(F32)
    qn = _each(lambda x: x * lax.rsqrt(jnp.sum(x * x, axis=-1, keepdims=True) + EPS) * (GDN_HEAD_DIM ** -0.5), q)
    kn = _each(lambda x: x * lax.rsqrt(jnp.sum(x * x, axis=-1, keepdims=True) + EPS), k)
    gcum_i = _each(lambda x: jnp.broadcast_to(x, (c, c)), gcum)
    gcum_j = _each(jnp.transpose, gcum_i)
    decay = _each(lambda x, y: jnp.exp(jnp.where(jj <= ii, x - y, -1e30)), gcum_i, gcum_j)
    g_last = _each(lambda x: x[c - 1:c, :], gcum)
    e_gcum = _each(jnp.exp, gcum)
    kbeta = _each(lambda x, y: x * y, kn, b)
    vbeta = _each(lambda x, y: x * y, v, b)
    m = _each(lambda x, y, d: jnp.where(jj < ii, _bdot(x, y, NT) * d, 0.0), kbeta, kn, decay)
    inv = _each(lambda x: eye - x, m)
    power = _each(lambda x: _dot3(x, x), m)
    for step in range(5):
        inv = _each(lambda x, p: x + _dot3(x, p), inv, power)
        if step < 4:
            power = _each(lambda p: _dot3(p, p), power)
    u = _each(_dot3, inv, vbeta)
    w = _each(lambda x, y, e: _dot3(x, y * e), inv, kbeta, e_gcum)
    a_qk = _each(lambda x, y, d: _bdot(x, y, NT) * d, qn, kn, decay)
    v_new = _each(lambda x, y, s: x - _bdot(y, s), u, w, state)
    o = _each(lambda x, e, s, a, vn: _bdot(x * e, s) + _bdot(a, vn), qn, e_gcum, state, a_qk, v_new)
    new_state = _each(lambda s, gl, x, gc, vn: s * jnp.exp(gl) + _bdot(x * jnp.exp(gl - gc), vn, TN),
                      state, g_last, kn, gcum, v_new)
    return o, new_state


GDN_HEADS_PER_STEP = 8


GDN_TIME_TILE = 256


def _gdn_specs(n_tokens, reverse):
    hb, hd, tt = GDN_HEADS_PER_STEP, GDN_HEAD_DIM, GDN_TIME_TILE
    nb, nt = GDN_HEADS // hb, n_tokens // tt

    def when(t):
        return nt - 1 - t if reverse else t

    q = pl.BlockSpec((tt, hb * hd), lambda h, t: (when(t), h))
    k = pl.BlockSpec((tt, hb * hd), lambda h, t: (when(t), nb + h))
    v = pl.BlockSpec((tt, hb * hd), lambda h, t: (when(t), 2 * nb + h))
    vec = pl.BlockSpec((tt, hb), lambda h, t: (when(t), h))
    states = pl.BlockSpec((hb, tt // GDN_CHUNK, hd, hd), lambda h, t: (h, when(t), 0, 0))
    return q, k, v, vec, states


def _gdn_fwd(qkv, beta, g):
    n_tokens = qkv.shape[0]
    hb, hd, tt = GDN_HEADS_PER_STEP, GDN_HEAD_DIM, GDN_TIME_TILE
    n_chunks = tt // GDN_CHUNK
    q_s, k_s, v_s, vec, st = _gdn_specs(n_tokens, False)

    def body(q_ref, k_ref, v_ref, b_ref, g_ref, o_ref, st_ref, state):
        @pl.when(pl.program_id(1) == 0)
        def _():
            state[...] = jnp.zeros_like(state)

        def step(c, carry):
            r = pl.ds(pl.multiple_of(c * GDN_CHUNK, GDN_CHUNK), GDN_CHUNK)
            cols = [slice(h * hd, (h + 1) * hd) for h in range(hb)]
            old = [state[h] for h in range(hb)]
            o, new = _gdn_chunks(
                [q_ref[r, cs] for cs in cols], [k_ref[r, cs] for cs in cols], [v_ref[r, cs] for cs in cols],
                [b_ref[r, h:h + 1] for h in range(hb)], [g_ref[r, h:h + 1] for h in range(hb)], old)
            for h in range(hb):
                st_ref[h, c] = old[h]
                o_ref[r, cols[h]] = o[h]
                state[h] = new[h]
            return carry

        lax.fori_loop(0, n_chunks, step, 0)

    return pl.pallas_call(
        body, grid=(GDN_HEADS // hb, n_tokens // tt), in_specs=[q_s, k_s, v_s, vec, vec], out_specs=(q_s, st),
        out_shape=(jax.ShapeDtypeStruct((n_tokens, GDN_WIDTH), F32),
                   jax.ShapeDtypeStruct((GDN_HEADS, n_tokens // GDN_CHUNK, hd, hd), F32)),
        scratch_shapes=[pltpu.VMEM((hb, hd, hd), F32)],
        name="gdn_fwd", compiler_params=_params(2),
    )(qkv, qkv, qkv, beta, g)


def _gdn_bwd(qkv, beta, g, states, do):
    n_tokens = qkv.shape[0]
    hb, hd, tt = GDN_HEADS_PER_STEP, GDN_HEAD_DIM, GDN_TIME_TILE
    n_chunks = tt // GDN_CHUNK
    q_s, k_s, v_s, vec, st = _gdn_specs(n_tokens, True)

    def body(q_ref, k_ref, v_ref, b_ref, g_ref, st_ref, do_ref, dq_ref, dk_ref, dv_ref, db_ref, dg_ref, dstate):
        @pl.when(pl.program_id(1) == 0)
        def _():
            dstate[...] = jnp.zeros_like(dstate)

        def step(i, carry):
            c = n_chunks - 1 - i
            r = pl.ds(pl.multiple_of(c * GDN_CHUNK, GDN_CHUNK), GDN_CHUNK)
            cols = [slice(h * hd, (h + 1) * hd) for h in range(hb)]
            args = ([q_ref[r, cs] for cs in cols], [k_ref[r, cs] for cs in cols], [v_ref[r, cs] for cs in cols],
                    [b_ref[r, h:h + 1] for h in range(hb)], [g_ref[r, h:h + 1] for h in range(hb)],
                    [st_ref[h, c] for h in range(hb)])
            cts = ([do_ref[r, cs] for cs in cols], [dstate[h] for h in range(hb)])
            dq, dk, dv, db, dg, dst = jax.vjp(_gdn_chunks, *args)[1](cts)
            for h in range(hb):
                dq_ref[r, cols[h]] = dq[h]
                dk_ref[r, cols[h]] = dk[h]
                dv_ref[r, cols[h]] = dv[h]
                db_ref[r, h:h + 1] = db[h]
                dg_ref[r, h:h + 1] = dg[h]
                dstate[h] = dst[h]
            return carry

        lax.fori_loop(0, n_chunks, step, 0)

    wide = jax.ShapeDtypeStruct((n_tokens, GDN_WIDTH), F32)
    thin = jax.ShapeDtypeStruct(beta.shape, F32)
    dq, dk, dv, db, dg = pl.pallas_call(
        body, grid=(GDN_HEADS // hb, n_tokens // tt), in_specs=[q_s, k_s, v_s, vec, vec, st, q_s],
        out_specs=(q_s, q_s, q_s, vec, vec), out_shape=(wide, wide, wide, thin, thin),
        scratch_shapes=[pltpu.VMEM((hb, hd, hd), F32)],
        name="gdn_bwd", compiler_params=_params(2),
    )(qkv, qkv, qkv, beta, g, states, do)
    return jnp.concatenate([dq, dk, dv], axis=1), db, dg


FFN_ROW_TILE = 256
FFN_FWD_ROW_TILE = 512


def _resident(shape):
    return pl.BlockSpec(shape, lambda i: (0,) * len(shape), pipeline_mode=pl.Buffered(1))


def _ffn_fwd(x, gain, wg, wu, wd, name):
    n_tokens, d = x.shape
    n_shards, n, _ = wg.shape
    tm = FFN_FWD_ROW_TILE

    def body(x_ref, gain_ref, wg_ref, wu_ref, wd_ref, o_ref, g_ref, u_ref):
        xv = x_ref[...]
        h = (xv * lax.rsqrt(jnp.mean(xv * xv, axis=-1, keepdims=True) + EPS) * gain_ref[...]).astype(BF16)
        acc = jnp.zeros((tm, d), F32)
        for j in range(n_shards):
            g = lax.dot_general(h, wg_ref[j], NT, preferred_element_type=F32)
            u = lax.dot_general(h, wu_ref[j], NT, preferred_element_type=F32)
            g_ref[j] = g
            u_ref[j] = u
            a = (g * jax.nn.sigmoid(g) * u).astype(BF16)
            acc = acc + lax.dot_general(a, wd_ref[j], NN, preferred_element_type=F32)
        o_ref[...] = xv + 0.5 * acc

    row = pl.BlockSpec((tm, d), lambda i: (i, 0))
    hid = pl.BlockSpec((n_shards, tm, n), lambda i: (0, i, 0))
    return pl.pallas_call(
        body, grid=(n_tokens // tm,),
        in_specs=[row, _resident(gain.shape), _resident(wg.shape), _resident(wu.shape), _resident(wd.shape)],
        out_specs=(row, hid, hid),
        out_shape=(jax.ShapeDtypeStruct(x.shape, F32), jax.ShapeDtypeStruct((n_shards, n_tokens, n), F32),
                   jax.ShapeDtypeStruct((n_shards, n_tokens, n), F32)),
        name=name, compiler_params=_params(1),
    )(x, gain, wg, wu, wd)


def _ffn_bwd_rows(x, gain, dy, g, u, wg, wu, wd, name):
    n_tokens, d = x.shape
    n_shards, n, _ = wg.shape
    tm = FFN_ROW_TILE

    def body(x_ref, gain_ref, dy_ref, g_ref, u_ref, wg_ref, wu_ref, wd_ref,
             dx_ref, dgain_ref, h_ref, dyh_ref, a_ref, dg_ref, du_ref):
        xv, dyv, gain_v = x_ref[...], dy_ref[...], gain_ref[...]
        r = lax.rsqrt(jnp.mean(xv * xv, axis=-1, keepdims=True) + EPS)
        xhat = xv * r
        h_ref[...] = (xhat * gain_v).astype(BF16)
        dyh = (0.5 * dyv).astype(BF16)
        dyh_ref[...] = dyh
        dh = jnp.zeros((tm, d), F32)
        for j in range(n_shards):
            da = lax.dot_general(dyh, wd_ref[j], NT, preferred_element_type=F32)
            gv, uv = g_ref[j], u_ref[j]
            sg = jax.nn.sigmoid(gv)
            silu = gv * sg
            a_ref[j] = (silu * uv).astype(BF16)
            dg = (da * uv * (sg + silu * (1.0 - sg))).astype(BF16)
            du = (da * silu).astype(BF16)
            dg_ref[j] = dg
            du_ref[j] = du
            dh = dh + lax.dot_general(dg, wg_ref[j], NN, preferred_element_type=F32)
            dh = dh + lax.dot_general(du, wu_ref[j], NN, preferred_element_type=F32)
        dxhat = dh * gain_v
        dx_ref[...] = dyv + r * (dxhat - xhat * jnp.mean(dxhat * xhat, axis=-1, keepdims=True))

        @pl.when(pl.program_id(0) == 0)
        def _():
            dgain_ref[...] = jnp.zeros_like(dgain_ref)

        dgain_ref[...] += jnp.sum(dh * xhat, axis=0, keepdims=True)

    row = pl.BlockSpec((tm, d), lambda i: (i, 0))
    hid = pl.BlockSpec((n_shards, tm, n), lambda i: (0, i, 0))
    hid_shape = (n_shards, n_tokens, n)
    return pl.pallas_call(
        body, grid=(n_tokens // tm,),
        in_specs=[row, _resident(gain.shape), row, hid, hid, _resident(wg.shape), _resident(wu.shape),
                  _resident(wd.shape)],
        out_specs=(row, pl.BlockSpec(gain.shape, lambda i: (0, 0)), row, row, hid, hid, hid),
        out_shape=(jax.ShapeDtypeStruct(x.shape, F32), jax.ShapeDtypeStruct(gain.shape, F32),
                   jax.ShapeDtypeStruct(x.shape, BF16), jax.ShapeDtypeStruct(x.shape, BF16),
                   jax.ShapeDtypeStruct(hid_shape, BF16), jax.ShapeDtypeStruct(hid_shape, BF16),
                   jax.ShapeDtypeStruct(hid_shape, BF16)),
        name=name, compiler_params=_params(1),
    )(x, gain, dy, g, u, wg, wu, wd)


def _ffn_bwd_weights(h, dyh, a, dg, du, name, owner=None):
    n_shards, n_tokens, n = a.shape
    d = h.shape[1]

    def products(h_ref, dyh_ref, a_ref, dg_ref, du_ref):
        hv = h_ref[...]
        return (lax.dot_general(dg_ref[0], hv, TN, preferred_element_type=F32),
                lax.dot_general(du_ref[0], hv, TN, preferred_element_type=F32),
                lax.dot_general(a_ref[0], dyh_ref[...], TN, preferred_element_type=F32))

    hid = pl.BlockSpec((1, n_tokens, n), lambda j, *_: (j, 0, 0))
    out = pl.BlockSpec((1, n, d), lambda j, *_: (j, 0, 0))
    ins = [pl.BlockSpec(h.shape, lambda j, *_: (0, 0), pipeline_mode=pl.Buffered(1)),
           pl.BlockSpec(dyh.shape, lambda j, *_: (0, 0), pipeline_mode=pl.Buffered(1)), hid, hid, hid]
    if owner is None:
        def body(*refs):
            for ref, val in zip(refs[5:], products(*refs[:5])):
                ref[0] = val

        return pl.pallas_call(
            body, grid=(n_shards,), in_specs=ins, out_specs=(out, out, out),
            out_shape=(jax.ShapeDtypeStruct((n_shards, n, d), F32),) * 3, name=name, compiler_params=_params(1),
        )(h, dyh, a, dg, du)

    def body(owner_ref, *refs):
        vals = products(*refs[:5])
        for ref, val in zip(refs[5:8], vals):
            ref[0] = val.astype(BF16)

        @pl.when(pl.program_id(0) == owner_ref[0])
        def _():
            for ref, val in zip(refs[8:], vals):
                ref[0] = val

    mine = pl.BlockSpec((1, n, d), lambda j, *_: (0, 0, 0))
    outs = pl.pallas_call(
        body,
        grid_spec=pltpu.PrefetchScalarGridSpec(num_scalar_prefetch=1, grid=(n_shards,), in_specs=ins,
                                               out_specs=(out, out, out, mine, mine, mine)),
        out_shape=(jax.ShapeDtypeStruct((n_shards, n, d), BF16),) * 3 + (jax.ShapeDtypeStruct((1, n, d), F32),) * 3,
        name=name, compiler_params=_params(1),
    )(owner, h, dyh, a, dg, du)
    return outs[:3], outs[3:]


IN_PIECES = (("wq_a", 0, 768), ("wk_a", 768, 1536), ("wv_a", 1536, 2304), ("w_qkvb", 2304, 5376),
             ("w_small", 5376, 5392), ("w_ggate", 5392, 6416), ("w_gatea", 6416, 7440), ("w_gateb", 7440, 8464))
IN_NAMES = tuple(name for name, _, _ in IN_PIECES)


def _in_rows(lo, hi):
    return lo, max(hi, lo + LANES)


def _in_proj_fwd(x, gain, wt):
    n_tokens, d = x.shape
    tm = FFN_ROW_TILE
    rows = [_in_rows(lo, hi) for _, lo, hi in IN_PIECES]

    def body(x_ref, gain_ref, wt_ref, *o_refs):
        xv = x_ref[...]
        h = (xv * lax.rsqrt(jnp.mean(xv * xv, axis=-1, keepdims=True) + EPS) * gain_ref[...]).astype(BF16)
        for (lo, hi), o_ref in zip(rows, o_refs):
            o_ref[...] = lax.dot_general(h, wt_ref[lo:hi, :], NT, preferred_element_type=F32)

    return pl.pallas_call(
        body, grid=(n_tokens // tm,),
        in_specs=[pl.BlockSpec((tm, d), lambda i: (i, 0)), _resident(gain.shape), _resident(wt.shape)],
        out_specs=tuple(pl.BlockSpec((tm, hi - lo), lambda i: (i, 0)) for lo, hi in rows),
        out_shape=tuple(jax.ShapeDtypeStruct((n_tokens, hi - lo), F32) for lo, hi in rows),
        name="in_proj_fwd", compiler_params=_params(1),
    )(x, gain, wt)


def _in_proj_bwd_rows(x, gain, dres, dzs, wt):
    n_tokens, d = x.shape
    tm = FFN_ROW_TILE
    n = len(dzs)
    rows = [_in_rows(lo, hi) for _, lo, hi in IN_PIECES]

    def body(x_ref, gain_ref, dres_ref, *refs):
        dz_refs, wt_ref = refs[:n], refs[n]
        dx_ref, dgain_ref, h_ref = refs[n + 1:]
        xv, gain_v = x_ref[...], gain_ref[...]
        r = lax.rsqrt(jnp.mean(xv * xv, axis=-1, keepdims=True) + EPS)
        xhat = xv * r
        h_ref[...] = (xhat * gain_v).astype(BF16)
        dh = jnp.zeros((tm, d), F32)
        for dz_ref, (lo, hi) in zip(dz_refs, rows):
            dh = dh + lax.dot_general(dz_ref[...].astype(BF16), wt_ref[lo:hi, :], NN, preferred_element_type=F32)
        dxhat = dh * gain_v
        dx_ref[...] = dres_ref[...] + r * (dxhat - xhat * jnp.mean(dxhat * xhat, axis=-1, keepdims=True))

        @pl.when(pl.program_id(0) == 0)
        def _():
            dgain_ref[...] = jnp.zeros_like(dgain_ref)

        dgain_ref[...] += jnp.sum(dh * xhat, axis=0, keepdims=True)

    row = pl.BlockSpec((tm, d), lambda i: (i, 0))
    return pl.pallas_call(
        body, grid=(n_tokens // tm,),
        in_specs=([row, _resident(gain.shape), row]
                  + [pl.BlockSpec((tm, dz.shape[1]), lambda i: (i, 0)) for dz in dzs] + [_resident(wt.shape)]),
        out_specs=(row, pl.BlockSpec(gain.shape, lambda i: (0, 0)), row),
        out_shape=(jax.ShapeDtypeStruct(x.shape, F32), jax.ShapeDtypeStruct(gain.shape, F32),
                   jax.ShapeDtypeStruct(x.shape, BF16)),
        name="in_proj_bwd_rows", compiler_params=_params(1),
    )(x, gain, dres, *dzs, wt)


def _in_proj_bwd_weight(dwt, h, dz, lo, hi, name):
    n_tokens, d = h.shape
    width = hi - lo
    tn = _tile(width, 512) if width >= LANES else width
    dz_tile = max(tn, LANES)

    def body(dwt_ref, h_ref, dz_ref, o_ref):
        o_ref[...] = lax.dot_general(dz_ref[:, :tn].astype(BF16), h_ref[...], TN, preferred_element_type=F32)

    return pl.pallas_call(
        body, grid=(width // tn,),
        in_specs=[ANY, _resident(h.shape), pl.BlockSpec((n_tokens, dz_tile), lambda j: (0, j))],
        out_specs=pl.BlockSpec((pl.Element(tn), pl.Element(d)), lambda j: (pl.multiple_of(lo + j * tn, 16), 0)),
        out_shape=jax.ShapeDtypeStruct(dwt.shape, F32), input_output_aliases={0: 0}, name=name,
        compiler_params=_params(1),
    )(dwt, h, dz)


def _split_small(z):
    return z[:, :GDN_HEADS], z[:, GDN_HEADS:2 * GDN_HEADS]


def _heads3(q, k, v):
    return _to_heads(q), _to_heads(k), _to_heads(v)


def _tokens6(o, lse):
    return tuple(_from_heads(o)) + tuple(_from_heads(lse))


def mixer_forward(x1, w, small):
    n_tokens = x1.shape[0]
    proj = dict(zip(IN_NAMES, _in_proj_fwd(x1, small["mix_norm"], w["w_in_t"])))
    cos, sin = _rope_tables(n_tokens)
    q_rot = _rope_call(proj["wq_a"], cos, sin, "rope_q")
    k_rot = _rope_call(proj["wk_a"], cos, sin, "rope_k")
    (qh, kh, vh), heads_vjp = jax.vjp(_heads3, q_rot, k_rot, proj["wv_a"])
    o, lse = _attn_fwd(qh, kh, vh)
    per_group, tokens_vjp = jax.vjp(_tokens6, o, lse)
    ya = _rowwise_fwd(_combine_fn, "combine", per_group, (), (), 512, 1)[0]
    pa = _matmul(ya, w["w_branch_a"], name="branch_a")
    qkv = _conv_fwd(proj["w_qkvb"], small["gdn_conv_w"])
    raw, small_vjp = jax.vjp(_split_small, proj["w_small"])
    gdn_params = (small["gdn_a_log"], small["gdn_dt_bias"])
    beta, gcum = _rowwise_fwd(_beta_decay_fn, "beta_decay", raw, (), gdn_params, 512, 1)
    ob, states = _gdn_fwd(qkv, beta, gcum)
    gate_in = (ob, proj["w_ggate"])
    yb = _rowwise_fwd(_outnorm_gate_fn, "outnorm_gate", gate_in, (), (small["gdn_out_norm"],), 512, GDN_HEADS)[0]
    pb = _matmul(yb, w["w_branch_b"], name="branch_b")
    merge_in = (proj["w_gatea"], proj["w_gateb"], pa, pb)
    merged = _rowwise_fwd(_merge_fn, "merge", merge_in, (), (), 256, 1)[0]
    x2 = _matmul(merged, w["w_out"], name="out", res=x1)
    saved = dict(x1=x1, proj=proj, cos=cos, sin=sin, heads_vjp=heads_vjp, heads=(qh, kh, vh), tokens_vjp=tokens_vjp,
                 per_group=per_group, ya=ya, qkv=qkv, raw=raw, small_vjp=small_vjp, beta=beta, gcum=gcum, states=states,
                 gate_in=gate_in, yb=yb, merge_in=merge_in, merged=merged)
    return x2, saved


def mixer_backward(dx2, s, w, small):
    proj = s["proj"]
    dmerged = _matmul(dx2, w["w_out"], name="out_da", tb=True)
    grads = dict(w_out=_matmul(s["merged"], dx2, name="out_dw", ta=True))
    dgate_a, dgate_b, dpa, dpb = _rowwise_bwd(_merge_fn, "merge_bwd", s["merge_in"], (), (), (dmerged,), 256, 1)
    dyb = _matmul(dpb, w["w_branch_b"], name="branch_b_da", tb=True)
    grads["w_branch_b"] = _matmul(s["yb"], dpb, name="branch_b_dw", ta=True)
    dya = _matmul(dpa, w["w_branch_a"], name="branch_a_da", tb=True)
    grads["w_branch_a"] = _matmul(s["ya"], dpa, name="branch_a_dw", ta=True)
    dob, dggate, grads["gdn_out_norm"] = _rowwise_bwd(
        _outnorm_gate_fn, "outnorm_gate_bwd", s["gate_in"], (), (small["gdn_out_norm"],), (dyb,), 512, GDN_HEADS)
    dqkv, dbeta, dgcum = _gdn_bwd(s["qkv"], s["beta"], s["gcum"], s["states"], dob)
    gdn_params = (small["gdn_a_log"], small["gdn_dt_bias"])
    dbeta_raw, ddecay_raw, grads["gdn_a_log"], grads["gdn_dt_bias"] = _rowwise_bwd(
        _beta_decay_fn, "beta_decay_bwd", s["raw"], (), gdn_params, (dbeta, dgcum), 512, 1)
    dsmall = s["small_vjp"]((dbeta_raw, ddecay_raw))[0]
    dqkvb, grads["gdn_conv_w"] = _conv_bwd(proj["w_qkvb"], small["gdn_conv_w"], dqkv)
    dper_group = _rowwise_bwd(_combine_fn, "combine_bwd", s["per_group"], (), (), (dya,), 512, 1)
    do, dlse = s["tokens_vjp"](tuple(dper_group))
    dqh, dkh, dvh = _attn_bwd(*s["heads"], do, dlse)
    dq_rot, dk_rot, dv = s["heads_vjp"]((dqh, dkh, dvh))
    dq = _rope_call(dq_rot, s["cos"], -s["sin"], "rope_q_bwd")
    dk = _rope_call(dk_rot, s["cos"], -s["sin"], "rope_k_bwd")
    dzs = (dq, dk, dv, dqkvb, dsmall, dggate, dgate_a, dgate_b)
    dx1, grads["mix_norm"], h = _in_proj_bwd_rows(s["x1"], small["mix_norm"], dx2, dzs, w["w_in_t"])
    dwt = lax.empty(w["w_in_t"].shape, F32)
    for (name, lo, hi), dz in zip(IN_PIECES, dzs):
        dwt = _in_proj_bwd_weight(dwt, h, dz, lo, hi, "in_proj_dw_" + name)
    grads["w_in_t"] = dwt
    return dx1, grads


def ffn_forward(x, gain, w, tag):
    out, g, u = _ffn_fwd(x, gain, w[tag + "_w_gate"], w[tag + "_w_up"], w[tag + "_w_down"], tag + "_fwd")
    return out, (x, g, u)


def ffn_backward(dy, saved, gain, w, tag, owner=None):
    x, g, u = saved
    weights = (w[tag + "_w_gate"], w[tag + "_w_up"], w[tag + "_w_down"])
    dx, dgain, h, dyh, a, dg, du = _ffn_bwd_rows(x, gain, dy, g, u, *weights, tag + "_bwd_rows")
    return dx, dgain, _ffn_bwd_weights(h, dyh, a, dg, du, tag + "_bwd_weights", owner)


def loss_head(x3, target, gain):
    row_loss = _rowwise_fwd(_loss_fn, "loss", (x3,), (target,), (gain,), 256, 1)[0]
    dx3, dgain = _rowwise_bwd(_loss_fn, "loss_bwd", (x3,), (target,), (gain,), (jnp.ones_like(row_loss),), 256, 1)
    return jnp.sum(row_loss), dx3, dgain


BIG_WEIGHTS = ("ffn1_w_gate", "ffn1_w_up", "ffn1_w_down", "w_in", "w_branch_a", "w_branch_b", "w_out",
               "ffn2_w_gate", "ffn2_w_up", "ffn2_w_down")
TRANSPOSED = ("ffn1_w_gate", "ffn1_w_up", "w_in", "ffn2_w_gate", "ffn2_w_up")
CONV_SHARD = (GDN_CONV, 3 * GDN_WIDTH // N_DEV)
SMALL_ROWS = 24
ANY = pl.BlockSpec(memory_space=pl.ANY)


TOKEN = jax.ShapeDtypeStruct((8, LANES), F32)


def _after(value, token):
    return value + token[0, 0].astype(value.dtype)


def _position():
    return lax.axis_index("x"), lax.axis_index("y"), lax.axis_index("c")


def all_gather_shards(shards, name):
    n = len(shards)

    def body(*refs):
        x_refs, out_refs = refs[:n], refs[n:2 * n]
        send_sems, recv_sems, local_sems = refs[2 * n + 1:]
        x, y, c = _position()
        me, sibling = (x, y, c), (x, y, 1 - c)
        chips = [(1 - x, y), (x, 1 - y), (1 - x, 1 - y)]

        def slab(a, px, py, pc):
            return out_refs[a].at[4 * px + 2 * py + pc]

        def copy(a, k, block, to, src=None):
            return pltpu.make_async_remote_copy(
                src_ref=slab(a, *block) if src is None else src, dst_ref=slab(a, *block),
                send_sem=send_sems.at[7 * a + k], recv_sem=recv_sems.at[7 * a + k], device_id=to, device_id_type=MESH)

        mine = [pltpu.make_async_copy(x_refs[a], slab(a, *me), local_sems.at[a]) for a in range(n)]
        for cp in mine:
            cp.start()
        first = []
        for j, chip in enumerate(chips):
            first += [copy(a, 1 + j, me, (*chip, c), src=x_refs[a]) for a in range(n)]
        first += [copy(a, 0, me, sibling, src=x_refs[a]) for a in range(n)]
        for cp in first:
            cp.start()
        passed = []
        for j, chip in enumerate(chips):
            for a in range(n):
                copy(a, 1 + j, (*chip, c), me).wait_recv()
                cp = copy(a, 4 + j, (*chip, c), sibling)
                cp.start()
                passed.append(cp)
        for a in range(n):
            copy(a, 0, sibling, me).wait_recv()
        for j, chip in enumerate(chips):
            for a in range(n):
                copy(a, 4 + j, (*chip, 1 - c), me).wait_recv()
        for cp in first + passed:
            cp.wait_send()
        for cp in mine:
            cp.wait()
        refs[2 * n][...] = jnp.zeros_like(refs[2 * n])

    outs = pl.pallas_call(
        body, out_shape=tuple(jax.ShapeDtypeStruct((N_DEV,) + s.shape, s.dtype) for s in shards) + (TOKEN,),
        in_specs=[ANY] * n, out_specs=(ANY,) * n + (pl.BlockSpec(memory_space=pltpu.VMEM),),
        scratch_shapes=[pltpu.SemaphoreType.DMA((7 * n,)), pltpu.SemaphoreType.DMA((7 * n,)),
                        pltpu.SemaphoreType.DMA((n,))],
        name=name,
    )(*shards)
    return outs[:n], outs[n]


def exchange_with_sibling(grads):
    n = len(grads)

    def body(*refs):
        g_refs, recv_refs = refs[:n], refs[n:2 * n]
        send_sems, recv_sems = refs[2 * n:]
        x, y, c = _position()
        copies = [pltpu.make_async_remote_copy(
            src_ref=g_refs[a].at[2 * k + 1 - c], dst_ref=recv_refs[a].at[k], send_sem=send_sems.at[4 * a + k],
            recv_sem=recv_sems.at[4 * a + k], device_id=(x, y, 1 - c), device_id_type=MESH)
            for k in range(4) for a in range(n)]
        for cp in copies:
            cp.start()
        for cp in copies:
            cp.wait()

    return pl.pallas_call(
        body, out_shape=tuple(jax.ShapeDtypeStruct((4,) + g.shape[1:], g.dtype) for g in grads),
        in_specs=[ANY] * n, out_specs=(ANY,) * n,
        scratch_shapes=[pltpu.SemaphoreType.DMA((4 * n,)), pltpu.SemaphoreType.DMA((4 * n,))], name="rs_sibling",
    )(*grads)


ELEMENTWISE_TILE_BYTES = 1536 * 1024


def _tile2(rows, cols):
    if rows % 256 == 0:
        return 256, cols
    if rows * cols * 4 > ELEMENTWISE_TILE_BYTES and cols % 256 == 0:
        return rows, 256
    return rows, cols


def add_sibling(grads, received, core, name):
    _, rows, width = grads.shape
    tr, tc = _tile2(rows, width)

    def body(c_ref, g_ref, r_ref, o_ref):
        o_ref[...] = (g_ref[...] + r_ref[...]).astype(BF16)

    blk = (1, tr, tc)
    return pl.pallas_call(
        body,
        grid_spec=pltpu.PrefetchScalarGridSpec(
            num_scalar_prefetch=1, grid=(4, rows // tr, width // tc),
            in_specs=[pl.BlockSpec(blk, lambda k, i, j, c_ref: (2 * k + c_ref[0], i, j)),
                      pl.BlockSpec(blk, lambda k, i, j, c_ref: (k, i, j))],
            out_specs=pl.BlockSpec(blk, lambda k, i, j, c_ref: (k, i, j))),
        out_shape=jax.ShapeDtypeStruct((4, rows, width), BF16), name=name, compiler_params=_params(3),
    )(core, grads, received)


HBM = pl.BlockSpec(memory_space=pltpu.HBM)
SEM = pl.BlockSpec(memory_space=pltpu.SEMAPHORE)
DATAFLOW_EFFECT = pltpu.SideEffectType.DATAFLOW_SIDE_EFFECTING
N_PEERS = N_DEV - 1


def _peer(mask):
    x, y, c = _position()
    px = 1 - x if mask & 4 else x
    py = 1 - y if mask & 2 else y
    pc = 1 - c if mask & 1 else c
    return (px, py, pc), 4 * px + 2 * py + pc


ALL_PEERS = tuple(range(1, N_DEV))
OTHER_CHIPS = (4, 2, 6)


def _exchange_peers(mode):
    return OTHER_CHIPS if mode == "chips" else ALL_PEERS


def _direct_copies(src_refs, land_refs, send_sems, recv_sems, mode):
    x, y, c = _position()
    me = 4 * x + 2 * y + c
    masks = _exchange_peers(mode)
    copies = []
    for a, (src, land) in enumerate(zip(src_refs, land_refs)):
        for slot, mask in enumerate(masks):
            peer, peer_index = _peer(mask)
            k = len(masks) * a + slot
            source = {"gather": lambda: src, "scatter": lambda: src.at[peer_index],
                      "chips": lambda: src.at[2 * peer[0] + peer[1]]}[mode]()
            copies.append(pltpu.make_async_remote_copy(
                src_ref=source, dst_ref=land.at[me] if mode == "gather" else land.at[slot],
                send_sem=send_sems.at[k], recv_sem=recv_sems.at[k], device_id=peer, device_id_type=MESH))
    return copies


def direct_exchange_start(arrays, mode, name):
    n = len(arrays)
    n_peers = len(_exchange_peers(mode))
    lands = [lax.empty((N_DEV,) + a.shape if mode == "gather" else (n_peers,) + a.shape[1:], a.dtype) for a in arrays]

    def body(*refs):
        src_refs, land_refs = refs[:n], refs[n:2 * n]
        send_sems, recv_sems = refs[2 * n], refs[2 * n + 1]
        token = refs[-1]
        for cp in _direct_copies(src_refs, land_refs, send_sems, recv_sems, mode):
            cp.start()
        token[...] = jnp.zeros_like(token)

    sems = pltpu.SemaphoreType.DMA((n_peers * n,))
    outs = pl.pallas_call(
        body, name=name,
        out_shape=(sems, sems) + tuple(pltpu.HBM(a.shape, a.dtype) for a in arrays)
        + tuple(pltpu.HBM(l.shape, l.dtype) for l in lands) + (TOKEN,),
        in_specs=[HBM] * (2 * n), out_specs=(SEM, SEM) + (HBM,) * (2 * n) + (pl.BlockSpec(memory_space=pltpu.VMEM),),
        input_output_aliases={i: 2 + i for i in range(2 * n)},
        compiler_params=pltpu.CompilerParams(has_side_effects=DATAFLOW_EFFECT),
    )(*[pltpu.with_memory_space_constraint(a, pltpu.HBM) for a in list(arrays) + lands])
    return outs[0], outs[1], outs[2:2 + n], outs[2 + n:2 + 2 * n], outs[-1]


def direct_exchange_wait(send_sems, recv_sems, arrays, lands, after, mode, name):
    n = len(arrays)

    def body(*refs):
        src_refs, land_refs = refs[:n], refs[n:2 * n]
        send_sems, recv_sems = refs[2 * n], refs[2 * n + 1]
        for cp in _direct_copies(src_refs, land_refs, send_sems, recv_sems, mode):
            cp.wait_send()
            cp.wait_recv()
        refs[-1][...] = jnp.zeros_like(refs[-1])

    outs = pl.pallas_call(
        body, name=name,
        out_shape=tuple(pltpu.HBM(a.shape, a.dtype) for a in arrays) + tuple(pltpu.HBM(l.shape, l.dtype) for l in lands)
        + (TOKEN,),
        in_specs=[HBM] * (2 * n) + [SEM, SEM, pl.BlockSpec(memory_space=pl.ANY)],
        out_specs=(HBM,) * (2 * n) + (pl.BlockSpec(memory_space=pltpu.VMEM),),
        input_output_aliases={i: i for i in range(2 * n)},
        compiler_params=pltpu.CompilerParams(has_side_effects=DATAFLOW_EFFECT),
    )(*arrays, *lands, send_sems, recv_sems, after)
    return outs[n:]


def adamw_direct(w, m, v, own, received, name):
    row_per_tile = w.shape[0] != 1
    rows, cols = (w.shape[0], w.shape[2]) if row_per_tile else w.shape[-2:]
    tr, tc = _tile2(rows, cols)

    def body(w_ref, m_ref, v_ref, own_ref, r_ref, g_ref, d_ref, nm_ref, nv_ref):
        gv = own_ref[0]
        for j in range(N_PEERS):
            gv = gv + r_ref[j].astype(F32)
        nm = ADAM_B1 * m_ref[...] + (1.0 - ADAM_B1) * gv
        nv = ADAM_B2 * v_ref[...] + (1.0 - ADAM_B2) * (gv * gv)
        m_hat = nm / (1.0 - ADAM_B1 ** ADAM_STEP)
        v_hat = nv / (1.0 - ADAM_B2 ** ADAM_STEP)
        g_ref[...] = gv
        d_ref[...] = -ADAM_LR * (m_hat / (jnp.sqrt(v_hat) + ADAM_EPS) + ADAM_WD * w_ref[...])
        nm_ref[...] = nm
        nv_ref[...] = nv

    if row_per_tile:
        one = pl.BlockSpec((tr, None, tc), lambda i, j: (i, 0, j))
    else:
        one = pl.BlockSpec((None, tr, tc), lambda i, j: (0, i, j))
    out = jax.ShapeDtypeStruct(w.shape, F32)
    return pl.pallas_call(
        body, grid=(rows // tr, cols // tc),
        in_specs=[one, one, one, pl.BlockSpec((1, tr, tc), lambda i, j: (0, i, j)),
                  pl.BlockSpec((N_PEERS, tr, tc), lambda i, j: (0, i, j))],
        out_specs=(one,) * 4, out_shape=(out,) * 4, name=name, compiler_params=_params(2),
    )(w, m, v, own, received)


def all_reduce_small(vals):
    rows, width = vals.shape

    def body(x_ref, out_ref, all_ref, send_sems, recv_sems):
        x, y, c = _position()
        me, sibling = (x, y, c), (x, y, 1 - c)
        chips = [(1 - x, y), (x, 1 - y), (1 - x, 1 - y)]

        def slab(px, py, pc):
            return all_ref.at[4 * px + 2 * py + pc]

        def copy(k, block, to, src=None):
            return pltpu.make_async_remote_copy(
                src_ref=slab(*block) if src is None else src, dst_ref=slab(*block),
                send_sem=send_sems.at[k], recv_sem=recv_sems.at[k], device_id=to, device_id_type=MESH)

        first = [copy(0, me, sibling, src=x_ref)]
        first += [copy(1 + j, me, (*chip, c), src=x_ref) for j, chip in enumerate(chips)]
        for cp in first:
            cp.start()
        all_ref[4 * x + 2 * y + c] = x_ref[...]
        passed = [copy(4 + j, (*chip, c), sibling) for j, chip in enumerate(chips)]
        for j, chip in enumerate(chips):
            copy(1 + j, (*chip, c), me).wait_recv()
            passed[j].start()
        copy(0, sibling, me).wait_recv()
        for j, chip in enumerate(chips):
            copy(4 + j, (*chip, 1 - c), me).wait_recv()
        for cp in first + passed:
            cp.wait_send()
        total = all_ref[0]
        for d in range(1, N_DEV):
            total = total + all_ref[d]
        out_ref[...] = total

    vmem = pl.BlockSpec(memory_space=pltpu.VMEM)
    return pl.pallas_call(
        body, out_shape=(jax.ShapeDtypeStruct(vals.shape, F32), jax.ShapeDtypeStruct((N_DEV, rows, width), F32)),
        in_specs=[vmem], out_specs=(vmem, vmem),
        scratch_shapes=[pltpu.SemaphoreType.DMA((7,)), pltpu.SemaphoreType.DMA((7,))], name="small_allreduce",
    )(vals)[0]


def adamw(w, g, m, v, name):
    shape = w.shape
    w2, g2, m2, v2 = [a.reshape((-1, shape[-1])) for a in (w, g, m, v)]
    rows, cols = w2.shape
    tr = 256 if rows % 256 == 0 else rows

    def body(w_ref, g_ref, m_ref, v_ref, d_ref, nm_ref, nv_ref):
        gv = g_ref[...]
        nm = ADAM_B1 * m_ref[...] + (1.0 - ADAM_B1) * gv
        nv = ADAM_B2 * v_ref[...] + (1.0 - ADAM_B2) * (gv * gv)
        m_hat = nm / (1.0 - ADAM_B1 ** ADAM_STEP)
        v_hat = nv / (1.0 - ADAM_B2 ** ADAM_STEP)
        d_ref[...] = -ADAM_LR * (m_hat / (jnp.sqrt(v_hat) + ADAM_EPS) + ADAM_WD * w_ref[...])
        nm_ref[...] = nm
        nv_ref[...] = nv

    blk = pl.BlockSpec((tr, cols), lambda i: (i, 0))
    out = jax.ShapeDtypeStruct((rows, cols), F32)
    outs = pl.pallas_call(
        body, grid=(rows // tr,), in_specs=[blk] * 4, out_specs=(blk,) * 3, out_shape=(out,) * 3,
        name=name, compiler_params=_params(1),
    )(w2, g2, m2, v2)
    return tuple(o.reshape(shape) for o in outs)


def adamw_summed(w, m, v, grads, from_sibling, received, me, name):
    rows, cols = w.shape[-2:]
    tr, tc = _tile2(rows, cols)

    def body(me_ref, w_ref, m_ref, v_ref, own_ref, sib_ref, r_ref, g_ref, d_ref, nm_ref, nv_ref):
        gv = own_ref[0] + sib_ref[0]
        for j in range(3):
            gv = gv + r_ref[j].astype(F32)
        nm = ADAM_B1 * m_ref[0] + (1.0 - ADAM_B1) * gv
        nv = ADAM_B2 * v_ref[0] + (1.0 - ADAM_B2) * (gv * gv)
        m_hat = nm / (1.0 - ADAM_B1 ** ADAM_STEP)
        v_hat = nv / (1.0 - ADAM_B2 ** ADAM_STEP)
        g_ref[0] = gv
        d_ref[0] = -ADAM_LR * (m_hat / (jnp.sqrt(v_hat) + ADAM_EPS) + ADAM_WD * w_ref[0])
        nm_ref[0] = nm
        nv_ref[0] = nv

    one = pl.BlockSpec((1, tr, tc), lambda i, j, me_ref: (0, i, j))
    out = jax.ShapeDtypeStruct((1, rows, cols), F32)
    return pl.pallas_call(
        body,
        grid_spec=pltpu.PrefetchScalarGridSpec(
            num_scalar_prefetch=1, grid=(rows // tr, cols // tc),
            in_specs=[one, one, one, pl.BlockSpec((1, tr, tc), lambda i, j, me_ref: (me_ref[0], i, j)),
                      pl.BlockSpec((1, tr, tc), lambda i, j, me_ref: (me_ref[1], i, j)),
                      pl.BlockSpec((3, tr, tc), lambda i, j, me_ref: (0, i, j))],
            out_specs=(one,) * 4),
        out_shape=(out,) * 4, name=name, compiler_params=_params(2),
    )(me, w, m, v, grads, from_sibling, received)


SMALL_VECTORS = ("ffn1_norm", "mix_norm", "ffn2_norm", "final_norm")


def _pack_small(gs):
    row = jnp.concatenate([gs["gdn_a_log"].reshape(-1), gs["gdn_dt_bias"].reshape(-1), gs["gdn_out_norm"].reshape(-1)])
    rows = [gs[n].reshape(1, D_MODEL) for n in SMALL_VECTORS]
    rows.append(jnp.pad(row, (0, D_MODEL - row.shape[0])).reshape(1, D_MODEL))
    rows.append(gs["gdn_conv_w"].reshape(-1, D_MODEL))
    packed = jnp.concatenate(rows, axis=0)
    return jnp.pad(packed, ((0, SMALL_ROWS - packed.shape[0]), (0, 0)))


def _unpack_small(packed):
    out = {n: packed[i].reshape(1, D_MODEL) for i, n in enumerate(SMALL_VECTORS)}
    row = packed[len(SMALL_VECTORS)]
    out["gdn_a_log"] = row[:GDN_HEADS].reshape(1, GDN_HEADS)
    out["gdn_dt_bias"] = row[GDN_HEADS:2 * GDN_HEADS].reshape(1, GDN_HEADS)
    out["gdn_out_norm"] = row[2 * GDN_HEADS:2 * GDN_HEADS + GDN_HEAD_DIM].reshape(1, GDN_HEAD_DIM)
    first = len(SMALL_VECTORS) + 1
    out["gdn_conv_w"] = packed[first:first + GDN_CONV * 3].reshape(GDN_CONV, 3 * GDN_WIDTH)
    return out


WEIGHTS = ("ffn1_norm", "ffn1_w_gate", "ffn1_w_up", "ffn1_w_down", "mix_norm", "w_in", "gdn_conv_w", "gdn_a_log",
           "gdn_dt_bias", "gdn_out_norm", "w_branch_a", "w_branch_b", "w_out", "ffn2_norm", "ffn2_w_gate",
           "ffn2_w_up", "ffn2_w_down", "final_norm")


def kernel(x, ffn1_norm, ffn1_w_gate, ffn1_w_up, ffn1_w_down, mix_norm, w_in, gdn_conv_w, gdn_a_log, gdn_dt_bias, gdn_out_norm, w_branch_a, w_branch_b, w_out, ffn2_norm, ffn2_w_gate, ffn2_w_up, ffn2_w_down, final_norm, loss_target, m_ffn1_norm, m_ffn1_w_gate, m_ffn1_w_up, m_ffn1_w_down, m_mix_norm, m_w_in, m_gdn_conv_w, m_gdn_a_log, m_gdn_dt_bias, m_gdn_out_norm, m_w_branch_a, m_w_branch_b, m_w_out, m_ffn2_norm, m_ffn2_w_gate, m_ffn2_w_up, m_ffn2_w_down, m_final_norm, v_ffn1_norm, v_ffn1_w_gate, v_ffn1_w_up, v_ffn1_w_down, v_mix_norm, v_w_in, v_gdn_conv_w, v_gdn_a_log, v_gdn_dt_bias, v_gdn_out_norm, v_w_branch_a, v_w_branch_b, v_w_out, v_ffn2_norm, v_ffn2_w_gate, v_ffn2_w_up, v_ffn2_w_down, v_final_norm):
    given = dict(locals())
    px, py, pc = _position()
    big_names = list(BIG_WEIGHTS)

    def shard_view(a, n):
        if n == "w_in":
            return a.transpose(2, 0, 1)
        return a.transpose(0, 2, 1) if n in TRANSPOSED else a

    def shard_unview(a, n):
        if n == "w_in":
            return a.transpose(1, 2, 0)
        return a.transpose(0, 2, 1) if n in TRANSPOSED else a

    me = 4 * px + 2 * py + pc
    me_index = me.astype(jnp.int32).reshape(1)
    late = [n for n in big_names if n.startswith("ffn2")]
    early = [n for n in big_names if n not in late]
    shards = {n: shard_view(given[n], n).reshape(given[n].shape[-1 if n in TRANSPOSED else -2], -1).astype(BF16)
              for n in big_names}
    early_slabs, early_done = all_gather_shards([shards[n] for n in early] + [gdn_conv_w[0]], "gather_weights")
    gathered = dict(zip(early + ["gdn_conv_w"], early_slabs))
    late_gather = direct_exchange_start([_after(shards[n], early_done) for n in late], "gather", "gather_ffn2_start")
    ffn1_norm = _after(ffn1_norm, late_gather[4])
    w = {n: gathered[n] for n in early if n.startswith("ffn")}
    w["w_in_t"] = gathered["w_in"].reshape(-1, D_MODEL)
    w["w_branch_a"] = gathered["w_branch_a"].transpose(1, 0, 2).reshape(256, D_MODEL)
    w["w_branch_b"] = gathered["w_branch_b"].reshape(D_MODEL, D_MODEL)
    w["w_out"] = gathered["w_out"].reshape(D_MODEL, D_MODEL)
    conv_full = gathered["gdn_conv_w"].transpose(1, 0, 2).reshape(GDN_CONV, 3 * GDN_WIDTH)
    small = dict(mix_norm=mix_norm, gdn_a_log=gdn_a_log, gdn_dt_bias=gdn_dt_bias, gdn_out_norm=gdn_out_norm,
                 gdn_conv_w=conv_full)

    x1, ffn1_saved = ffn_forward(x[0], ffn1_norm, w, "ffn1")
    x2, mixer_saved = mixer_forward(x1, w, small)
    late_lands = direct_exchange_wait(*late_gather[:4], x2, "gather", "gather_ffn2_wait")
    for n, land in zip(late, late_lands):
        w[n] = lax.dynamic_update_slice(land, shards[n][None], (me, 0, 0))
    x3, ffn2_saved = ffn_forward(x2, ffn2_norm, w, "ffn2")
    loss_local, dx3, g_final = loss_head(x3, loss_target[0], final_norm.reshape(1, D_MODEL))
    loss = lax.psum(loss_local, ("x", "y", "c"))
    dx2, g_ffn2_norm, (dw2, dw2_own) = ffn_backward(dx3, ffn2_saved, ffn2_norm, w, "ffn2", me_index)
    late_scatter = direct_exchange_start(list(dw2), "scatter", "rs_ffn2_start")
    w_after = dict(w, w_out=_after(w["w_out"], late_scatter[4]))
    dx1, g_w = mixer_backward(dx2, mixer_saved, w_after, small)
    middle = ["w_in", "w_branch_a", "w_branch_b", "w_out"]
    g_big = dict(w_in=g_w["w_in_t"].reshape(N_DEV, -1, D_MODEL),
                 w_branch_a=g_w["w_branch_a"].reshape(256, N_DEV, 128).transpose(1, 0, 2),
                 w_branch_b=g_w["w_branch_b"].reshape(N_DEV, 128, D_MODEL),
                 w_out=g_w["w_out"].reshape(N_DEV, 128, D_MODEL))
    own = dict(zip(late, dw2_own))
    own.update({n: lax.dynamic_index_in_dim(g_big[n], me, 0, keepdims=True) for n in middle[1:]})
    in_rows = g_w["w_in_t"].shape[0] // N_DEV
    own["w_in"] = lax.dynamic_slice(g_w["w_in_t"], (me * in_rows, 0), (in_rows, D_MODEL))[None]
    middle_scatter = direct_exchange_start([g_big[n].astype(BF16) for n in middle], "scatter", "rs_mixer_start")
    grad_x, g_ffn1_norm, dw1 = ffn_backward(dx1, ffn1_saved, _after(ffn1_norm, middle_scatter[4]), w, "ffn1")
    g_small = dict(ffn1_norm=g_ffn1_norm, ffn2_norm=g_ffn2_norm, final_norm=g_final,
                   **{n: g_w[n] for n in ("mix_norm", "gdn_a_log", "gdn_dt_bias", "gdn_out_norm", "gdn_conv_w")})

    first = [n for n in early if n.startswith("ffn1")]
    g_list = list(dw1)
    core = pc.astype(jnp.int32).reshape(1)
    me_and_chip = jnp.stack([me, 2 * px + py]).astype(jnp.int32)
    from_sibling = exchange_with_sibling(g_list)
    partials = [add_sibling(g, r, core, "rs_add_" + n) for n, g, r in zip(first, g_list, from_sibling)]
    first_chips = direct_exchange_start(partials, "chips", "rs_ffn1_start")

    def state_of(n):
        return [shard_view(given[p + n], n) for p in ("", "m_", "v_")]

    results = {}
    late_received = direct_exchange_wait(*late_scatter[:4], first_chips[4], "scatter", "rs_ffn2_wait")
    middle_received = direct_exchange_wait(*middle_scatter[:4], first_chips[4], "scatter", "rs_mixer_wait")
    for n, recv in zip(late + middle, list(late_received[:-1]) + list(middle_received[:-1])):
        outs = adamw_direct(*state_of(n), own[n], recv, "adamw_" + n)
        results[n] = tuple(shard_unview(o, n) for o in outs)

    done = results["w_out"][1]
    from_chips = direct_exchange_wait(*first_chips[:4], done, "chips", "rs_ffn1_wait")
    for n, g, sib, recv in zip(first, g_list, from_sibling, from_chips):
        outs = adamw_summed(*state_of(n), g, sib, recv, me_and_chip, "adamw_" + n)
        results[n] = tuple(shard_unview(o, n) for o in outs)

    small_sum = _unpack_small(all_reduce_small(_after(_pack_small(g_small), from_chips[-1])))
    conv_cols = CONV_SHARD[1]
    small_sum["gdn_conv_w"] = lax.dynamic_slice(small_sum["gdn_conv_w"], (0, me * conv_cols), (GDN_CONV, conv_cols))
    for n in WEIGHTS:
        if n not in results:
            g = small_sum[n].reshape(given[n].shape)
            results[n] = (g,) + adamw(given[n], g, given["m_" + n], given["v_" + n], "adamw_" + n)

    outs = [[results[n][i] for n in WEIGHTS] for i in range(4)]
    return (loss, grad_x[None], *outs[0], *outs[1], *outs[2], *outs[3])
```

```python
import jax
import jax.numpy as jnp
from jax import lax
from jax.experimental import pallas as pl
from jax.experimental.pallas import tpu as pltpu

F32 = jnp.float32
BF16 = jnp.bfloat16
HI = lax.Precision.HIGHEST
MESH = pl.DeviceIdType.MESH

N_DEV = 8
D_MODEL = 1024
EPS = 1e-6
ROPE_THETA = 10000.0
DSW_DILATIONS = (1, 4, 16)
DSW_HEADS_PER_GROUP = 4
DSW_HEAD_DIM = 64
DSW_BLOCK = 128
GDN_HEADS = 8
GDN_HEAD_DIM = 128
GDN_WIDTH = 1024
GDN_CONV = 4
GDN_CHUNK = 64

ADAM_LR = 0.001
ADAM_B1 = 0.9
ADAM_B2 = 0.999
ADAM_EPS = 1e-08
ADAM_WD = 0.01
ADAM_STEP = 10

VMEM_LIMIT_BYTES = 56 * 1024 * 1024
LANES = 128

NN = (((1,), (0,)), ((), ()))
NT = (((1,), (1,)), ((), ()))
TN = (((0,), (0,)), ((), ()))


def _params(n_grid):
    return pltpu.CompilerParams(dimension_semantics=("arbitrary",) * n_grid, vmem_limit_bytes=VMEM_LIMIT_BYTES)


def _tile(n, pref):
    best = None
    t = LANES
    while t <= min(n, pref):
        if n % t == 0:
            best = t
        t += LANES
    return n if best is None else best


def _matmul(a, b, *, name, ta=False, tb=False, res=None, scale=1.0):
    K, M = a.shape if ta else a.shape[::-1]
    N = b.shape[0] if tb else b.shape[1]
    assert (b.shape[1] if tb else b.shape[0]) == K, (a.shape, b.shape, ta, tb)
    tm = _tile(M, 512)
    tn = _tile(N, 512)
    dn = (((0 if ta else 1,), (1 if tb else 0,)), ((), ()))

    def body(*refs):
        a_ref, b_ref = refs[:2]
        o_ref = refs[-1]
        acc = lax.dot_general(a_ref[...].astype(BF16), b_ref[...].astype(BF16), dn, preferred_element_type=F32)
        if scale != 1.0:
            acc = acc * scale
        if res is not None:
            acc = refs[2][...] + acc
        o_ref[...] = acc

    a_spec = pl.BlockSpec((K, tm), lambda i, j: (0, i)) if ta else pl.BlockSpec((tm, K), lambda i, j: (i, 0))
    b_spec = pl.BlockSpec((tn, K), lambda i, j: (j, 0)) if tb else pl.BlockSpec((K, tn), lambda i, j: (0, j))
    o_spec = pl.BlockSpec((tm, tn), lambda i, j: (i, j))
    ins, specs = [a, b], [a_spec, b_spec]
    if res is not None:
        ins.append(res)
        specs.append(o_spec)
    return pl.pallas_call(
        body, grid=(M // tm, N // tn), in_specs=specs, out_specs=o_spec,
        out_shape=jax.ShapeDtypeStruct((M, N), F32), name=name, compiler_params=_params(2),
    )(*ins)


def _rw_specs(arrs, tm, nblk):
    return [pl.BlockSpec((tm, a.shape[1] // nblk), lambda i, j: (i, j)) for a in arrs]


def _rowwise_fwd(fn, name, rows, consts, params, tm, nblk):
    n_rows = rows[0].shape[0]
    tm = min(tm, n_rows)
    ins = list(rows) + list(consts)
    avals = [jax.ShapeDtypeStruct((tm, a.shape[1] // nblk), a.dtype) for a in ins]
    avals += [jax.ShapeDtypeStruct(p.shape, p.dtype) for p in params]
    out_avals = jax.eval_shape(fn, *avals)
    n_in = len(ins) + len(params)

    def body(*refs):
        outs = fn(*[r[...] for r in refs[:n_in]])
        for r, o in zip(refs[n_in:], outs):
            r[...] = o.astype(r.dtype)

    return pl.pallas_call(
        body, grid=(n_rows // tm, nblk),
        in_specs=_rw_specs(ins, tm, nblk) + [pl.BlockSpec(p.shape, lambda i, j: (0, 0)) for p in params],
        out_specs=tuple(pl.BlockSpec((tm, o.shape[1]), lambda i, j: (i, j)) for o in out_avals),
        out_shape=tuple(jax.ShapeDtypeStruct((n_rows, o.shape[1] * nblk), o.dtype) for o in out_avals),
        name=name, compiler_params=_params(2),
    )(*ins, *params)


def _rowwise_bwd(fn, name, rows, consts, params, cts, tm, nblk):
    n_rows = rows[0].shape[0]
    tm = min(tm, n_rows)
    nr, nc, npar, nct = len(rows), len(consts), len(params), len(cts)

    def body(*refs):
        rv = [r[...] for r in refs[:nr]]
        cv = [r[...] for r in refs[nr:nr + nc]]
        pv = [r[...] for r in refs[nr + nc:nr + nc + npar]]
        ctv = [r[...] for r in refs[nr + nc + npar:nr + nc + npar + nct]]
        outs = refs[nr + nc + npar + nct:]
        _, vjp = jax.vjp(lambda *d: fn(*d[:nr], *cv, *d[nr:]), *rv, *pv)
        grads = vjp(tuple(ctv))
        for k in range(nr):
            outs[k][...] = grads[k]
        first = jnp.logical_and(pl.program_id(0) == 0, pl.program_id(1) == 0)
        for k in range(npar):
            ref = outs[nr + k]

            @pl.when(first)
            def _(ref=ref):
                ref[...] = jnp.zeros_like(ref)

            ref[...] += grads[nr + k]

    ins = list(rows) + list(consts)
    return pl.pallas_call(
        body, grid=(n_rows // tm, nblk),
        in_specs=(_rw_specs(ins, tm, nblk) + [pl.BlockSpec(p.shape, lambda i, j: (0, 0)) for p in params]
                  + _rw_specs(cts, tm, nblk)),
        out_specs=tuple(_rw_specs(rows, tm, nblk) + [pl.BlockSpec(p.shape, lambda i, j: (0, 0)) for p in params]),
        out_shape=tuple([jax.ShapeDtypeStruct(a.shape, F32) for a in rows]
                        + [jax.ShapeDtypeStruct(p.shape, F32) for p in params]),
        name=name, compiler_params=_params(2),
    )(*ins, *params, *cts)


def _merge_fn(ga, gb, pa, pb):
    return (jax.nn.sigmoid(ga) * pa + jax.nn.sigmoid(gb) * pb,)


def _outnorm_gate_fn(o, gate, gain):
    y = o * lax.rsqrt(jnp.mean(o * o, axis=-1, keepdims=True) + EPS) * gain
    return (y * (gate * jax.nn.sigmoid(gate)),)


def _beta_decay_fn(beta_raw, decay_raw, a_log, dt_bias):
    z = decay_raw + dt_bias
    softplus = jnp.maximum(z, 0.0) + jnp.log(1.0 + jnp.exp(-jnp.abs(z)))
    g = -jnp.exp(a_log) * softplus
    rows = g.shape[0]
    ii = lax.broadcasted_iota(jnp.int32, (rows, rows), 0)
    jj = lax.broadcasted_iota(jnp.int32, (rows, rows), 1)
    same_chunk_before = jnp.logical_and(jj <= ii, jj // GDN_CHUNK == ii // GDN_CHUNK).astype(F32)
    gcum = lax.dot_general(same_chunk_before, g, NN, precision=HI, preferred_element_type=F32)
    return jax.nn.sigmoid(beta_raw), gcum


def _combine_fn(o0, o1, o2, l0, l1, l2):
    m = lax.stop_gradient(jnp.maximum(jnp.maximum(l0, l1), l2))
    e0, e1, e2 = jnp.exp(l0 - m), jnp.exp(l1 - m), jnp.exp(l2 - m)
    return ((e0 * o0 + e1 * o1 + e2 * o2) / (e0 + e1 + e2),)


def _loss_fn(x, target, gain):
    y = x * lax.rsqrt(jnp.mean(x * x, axis=-1, keepdims=True) + EPS) * gain
    err = y - target
    return (0.5 * jnp.mean(err * err, axis=-1, keepdims=True),)


def _rotate(v, cos, sin):
    half = DSW_HEAD_DIM // 2
    lane = lax.broadcasted_iota(jnp.int32, cos.shape, 1)
    low = (lane % DSW_HEAD_DIM) < half
    slabs = []
    for s in range(v.shape[1] // LANES):
        x = v[:, s * LANES:(s + 1) * LANES]
        swapped = jnp.where(low, pltpu.roll(x, LANES - half, 1), pltpu.roll(x, half, 1))
        slabs.append(x * cos + swapped * sin)
    return jnp.concatenate(slabs, axis=1)


def _rope_tables(n_tokens):
    half = DSW_HEAD_DIM // 2
    inv_freq = ROPE_THETA ** (-jnp.arange(half, dtype=F32) / half)
    ang = jnp.arange(n_tokens, dtype=F32)[:, None] * inv_freq[None, :]
    cos, sin = jnp.cos(ang), jnp.sin(ang)
    return jnp.tile(jnp.concatenate([cos, cos], 1), (1, 2)), jnp.tile(jnp.concatenate([-sin, sin], 1), (1, 2))


def _attn_probs(q, kp, kc, group, n):
    blk = DSW_BLOCK
    k = _each(lambda a, b: jnp.concatenate([a, b], axis=0).astype(BF16), kp, kc)
    s = _each(lambda a, b: lax.dot_general(a.astype(BF16), b, NT, preferred_element_type=F32)
              * (DSW_HEAD_DIM ** -0.5), q, k)
    blocks_per_seq = jnp.where(group == 0, 16, jnp.where(group == 1, 4, 1))
    first = (n % blocks_per_seq) == 0
    qi = lax.broadcasted_iota(jnp.int32, (blk, 2 * blk), 0)
    kj = lax.broadcasted_iota(jnp.int32, (blk, 2 * blk), 1)
    dist = qi + blk - kj
    valid = (dist >= 0) & (dist <= blk) & jnp.logical_or(kj >= blk, jnp.logical_not(first))
    s = _each(lambda a: jnp.where(valid, a, -1e30), s)
    m = _each(lambda a: jnp.max(a, axis=-1, keepdims=True), s)
    p = _each(lambda a, b: jnp.exp(a - b), s, m)
    l = _each(lambda a: jnp.sum(a, axis=-1, keepdims=True), p)
    return _each(lambda a, b: a / b, p, l), _each(lambda a, b: a + jnp.log(b), m, l), k


PAIRS_PER_GROUP = DSW_HEADS_PER_GROUP // 2


def _attn_specs(n_tokens):
    blk = DSW_BLOCK
    cur = pl.BlockSpec((PAIRS_PER_GROUP, blk, LANES), lambda g, n: (g, n, 0))
    prev = pl.BlockSpec((PAIRS_PER_GROUP, blk, LANES), lambda g, n: (g, jnp.maximum(n - 1, 0), 0))
    return cur, prev


def _heads_of(ref):
    pairs = [ref[p] for p in range(PAIRS_PER_GROUP)]
    return [x[:, s * DSW_HEAD_DIM:(s + 1) * DSW_HEAD_DIM] for x in pairs for s in range(2)]


def _pairs_of(heads):
    return [jnp.concatenate(heads[2 * p:2 * p + 2], axis=1) for p in range(PAIRS_PER_GROUP)]


def _attn_fwd(q, k, v):
    n_pairs, n_tokens, _ = q.shape
    cur, prev = _attn_specs(n_tokens)

    def body(q_ref, kp_ref, kc_ref, vp_ref, vc_ref, o_ref, l_ref):
        p, lse, _ = _attn_probs(_heads_of(q_ref), _heads_of(kp_ref), _heads_of(kc_ref),
                                pl.program_id(0), pl.program_id(1))
        vv = _each(lambda a, b: jnp.concatenate([a, b], axis=0).astype(BF16), _heads_of(vp_ref), _heads_of(vc_ref))
        o = _each(lambda a, b: lax.dot_general(a.astype(BF16), b, NN, preferred_element_type=F32), p, vv)
        lse_wide = _each(lambda a: jnp.broadcast_to(a, (DSW_BLOCK, DSW_HEAD_DIM)), lse)
        for pair, (o_pair, l_pair) in enumerate(zip(_pairs_of(o), _pairs_of(lse_wide))):
            o_ref[pair] = o_pair
            l_ref[pair] = l_pair

    return pl.pallas_call(
        body, grid=(n_pairs // PAIRS_PER_GROUP, n_tokens // DSW_BLOCK), in_specs=[cur, prev, cur, prev, cur],
        out_specs=(cur, cur), out_shape=(jax.ShapeDtypeStruct(q.shape, F32), jax.ShapeDtypeStruct(q.shape, F32)),
        name="attn_fwd", compiler_params=_params(2),
    )(q, k, k, v, v)


def _attn_bwd(q, k, v, do, dlse):
    n_pairs, n_tokens, _ = q.shape
    nblk = n_tokens // DSW_BLOCK
    cur, prev = _attn_specs(n_tokens)
    part = pl.BlockSpec((PAIRS_PER_GROUP, 1, 2 * DSW_BLOCK, LANES), lambda g, n: (g, n, 0, 0))
    scale = DSW_HEAD_DIM ** -0.5

    def body(q_ref, kp_ref, kc_ref, vp_ref, vc_ref, do_ref, dl_ref, dq_ref, dk_ref, dv_ref):
        qs = _heads_of(q_ref)
        p, _, kb = _attn_probs(qs, _heads_of(kp_ref), _heads_of(kc_ref), pl.program_id(0), pl.program_id(1))
        qb = _each(lambda a: a.astype(BF16), qs)
        vv = _each(lambda a, b: jnp.concatenate([a, b], axis=0).astype(BF16), _heads_of(vp_ref), _heads_of(vc_ref))
        dob = _each(lambda a: a.astype(BF16), _heads_of(do_ref))
        dp = _each(lambda a, b: lax.dot_general(a, b, NT, preferred_element_type=F32), dob, vv)
        dv = _each(lambda a, b: lax.dot_general(a.astype(BF16), b, TN, preferred_element_type=F32), p, dob)
        dl = _each(lambda a: jnp.sum(a, axis=-1, keepdims=True), _heads_of(dl_ref))
        ds = _each(lambda a, b, c: (a * (b - jnp.sum(b * a, axis=-1, keepdims=True) + c) * scale).astype(BF16),
                   p, dp, dl)
        dq = _each(lambda a, b: lax.dot_general(a, b, NN, preferred_element_type=F32), ds, kb)
        dk = _each(lambda a, b: lax.dot_general(a, b, TN, preferred_element_type=F32), ds, qb)
        for pair, (dq_pair, dk_pair, dv_pair) in enumerate(zip(_pairs_of(dq), _pairs_of(dk), _pairs_of(dv))):
            dq_ref[pair] = dq_pair
            dk_ref[pair, 0] = dk_pair
            dv_ref[pair, 0] = dv_pair

    partial_shape = jax.ShapeDtypeStruct((n_pairs, nblk, 2 * DSW_BLOCK, LANES), F32)
    dq, dkp, dvp = pl.pallas_call(
        body, grid=(n_pairs // PAIRS_PER_GROUP, nblk), in_specs=[cur, prev, cur, prev, cur, cur, cur],
        out_specs=(cur, part, part), out_shape=(jax.ShapeDtypeStruct(q.shape, F32), partial_shape, partial_shape),
        name="attn_bwd", compiler_params=_params(2),
    )(q, k, k, v, v, do, dlse)

    def fold(partial):
        own = partial[:, :, DSW_BLOCK:]
        from_next = jnp.pad(partial[:, 1:, :DSW_BLOCK], ((0, 0), (0, 1), (0, 0), (0, 0)))
        return (own + from_next).reshape(n_pairs, n_tokens, LANES)

    return dq, fold(dkp), fold(dvp)


def _to_heads(a):
    n_tokens = a.shape[0]
    outs = []
    for gi, d in enumerate(DSW_DILATIONS):
        blk = a[:, gi * 256:(gi + 1) * 256].reshape(n_tokens // d, d, PAIRS_PER_GROUP, LANES)
        outs.append(blk.transpose(2, 1, 0, 3).reshape(PAIRS_PER_GROUP, n_tokens, LANES))
    return jnp.concatenate(outs, 0)


def _from_heads(a):
    n_tokens = a.shape[1]
    outs = []
    for gi, d in enumerate(DSW_DILATIONS):
        blk = a[gi * PAIRS_PER_GROUP:(gi + 1) * PAIRS_PER_GROUP].reshape(PAIRS_PER_GROUP, d, n_tokens // d, LANES)
        outs.append(blk.transpose(2, 1, 0, 3).reshape(n_tokens, PAIRS_PER_GROUP * LANES))
    return outs


CONV_TILE = 512


def _shift_down(x, k, rows):
    return x if k == 0 else jnp.where(rows >= k, pltpu.roll(x, k, 0), 0.0)


def _shift_up(x, k, rows):
    n = x.shape[0]
    return x if k == 0 else jnp.where(rows < n - k, pltpu.roll(x, n - k, 0), 0.0)


def _conv_pre(x, w):
    rows = lax.broadcasted_iota(jnp.int32, x.shape, 0)
    acc = x * w[GDN_CONV - 1:GDN_CONV]
    for k in range(1, GDN_CONV):
        acc = acc + _shift_down(x, k, rows) * w[GDN_CONV - 1 - k:GDN_CONV - k]
    return acc, rows


def _conv_fwd(x, w):
    n_tokens, width = x.shape
    big = pl.BlockSpec((n_tokens, CONV_TILE), lambda j: (0, j))
    wsp = pl.BlockSpec((GDN_CONV, CONV_TILE), lambda j: (0, j))

    def body(x_ref, w_ref, o_ref):
        acc, _ = _conv_pre(x_ref[...], w_ref[...])
        o_ref[...] = acc * jax.nn.sigmoid(acc)

    return pl.pallas_call(
        body, grid=(width // CONV_TILE,), in_specs=[big, wsp], out_specs=big,
        out_shape=jax.ShapeDtypeStruct(x.shape, F32), name="conv_fwd", compiler_params=_params(1),
    )(x, w)


def _conv_bwd(x, w, dy):
    n_tokens, width = x.shape
    big = pl.BlockSpec((n_tokens, CONV_TILE), lambda j: (0, j))
    wsp = pl.BlockSpec((GDN_CONV, CONV_TILE), lambda j: (0, j))

    def body(x_ref, w_ref, dy_ref, dx_ref, dw_ref):
        xv, wv = x_ref[...], w_ref[...]
        acc, rows = _conv_pre(xv, wv)
        sg = jax.nn.sigmoid(acc)
        dacc = dy_ref[...] * (sg + acc * sg * (1.0 - sg))
        dx = dacc * wv[GDN_CONV - 1:GDN_CONV]
        for k in range(1, GDN_CONV):
            dx = dx + _shift_up(dacc, k, rows) * wv[GDN_CONV - 1 - k:GDN_CONV - k]
        dx_ref[...] = dx
        for k in range(GDN_CONV):
            dw_ref[GDN_CONV - 1 - k:GDN_CONV - k, :] = jnp.sum(dacc * _shift_down(xv, k, rows), axis=0, keepdims=True)

    return pl.pallas_call(
        body, grid=(width // CONV_TILE,), in_specs=[big, wsp, big], out_specs=(big, wsp),
        out_shape=(jax.ShapeDtypeStruct(x.shape, F32), jax.ShapeDtypeStruct(w.shape, F32)),
        name="conv_bwd", compiler_params=_params(1),
    )(x, w, dy)


def _dot(a, b, dn=NN):
    return lax.dot_general(a, b, dn, precision=HI, preferred_element_type=F32)


def _dot3(a, b, dn=NN):
    return lax.dot_general(a, b, dn, precision=lax.Precision.HIGH, preferred_element_type=F32)


def _bf16_dot(a, b, dn):
    return lax.dot_general(a.astype(BF16), b.astype(BF16), dn, preferred_element_type=F32)


_DOT_GRADS = {NN: (("g", "b", NT), ("a", "g", TN)), NT: (("g", "b", NN), ("g", "a", TN)),
              TN: (("b", "g", NT), ("a", "g", NN))}


def _make_bdot(dn):
    @jax.custom_vjp
    def op(a, b):
        return _bf16_dot(a, b, dn)

    def fwd(a, b):
        return op(a, b), (a, b)

    def bwd(saved, g):
        vals = dict(a=saved[0], b=saved[1], g=g)
        return tuple(_bf16_dot(vals[x], vals[y], form) for x, y, form in _DOT_GRADS[dn])

    op.defvjp(fwd, bwd)
    return op


_BDOTS = {dn: _make_bdot(dn) for dn in (NN, NT, TN)}


def _bdot(a, b, dn=NN):
    return _BDOTS[dn](a, b)


def _each(fn, *lists):
    return [fn(*items) for items in zip(*lists)]


def _gdn_chunks(q, k, v, b, gcum, state):
    c = GDN_CHUNK
    ii = lax.broadcasted_iota(jnp.int32, (c, c), 0)
    jj = lax.broadcasted_iota(jnp.int32, (c, c), 1)
    eye = (ii == jj).astype(F32)
    qn = _each(lambda x: x * lax.rsqrt(jnp.sum(x * x, axis=-1, keepdims=True) + EPS) * (GDN_HEAD_DIM ** -0.5), q)
    kn = _each(lambda x: x * lax.rsqrt(jnp.sum(x * x, axis=-1, keepdims=True) + EPS), k)
    gcum_i = _each(lambda x: jnp.broadcast_to(x, (c, c)), gcum)
    gcum_j = _each(jnp.transpose, gcum_i)
    decay = _each(lambda x, y: jnp.exp(jnp.where(jj <= ii, x - y, -1e30)), gcum_i, gcum_j)
    g_last = _each(lambda x: x[c - 1:c, :], gcum)
    e_gcum = _each(jnp.exp, gcum)
    kbeta = _each(lambda x, y: x * y, kn, b)
    vbeta = _each(lambda x, y: x * y, v, b)
    m = _each(lambda x, y, d: jnp.where(jj < ii, _bdot(x, y, NT) * d, 0.0), kbeta, kn, decay)
    inv = _each(lambda x: eye - x, m)
    power = _each(lambda x: _dot3(x, x), m)
    for step in range(5):
        inv = _each(lambda x, p: x + _dot3(x, p), inv, power)
        if step < 4:
            power = _each(lambda p: _dot3(p, p), power)
    u = _each(_dot3, inv, vbeta)
    w = _each(lambda x, y, e: _dot3(x, y * e), inv, kbeta, e_gcum)
    a_qk = _each(lambda x, y, d: _bdot(x, y, NT) * d, qn, kn, decay)
    v_new = _each(lambda x, y, s: x - _bdot(y, s), u, w, state)
    o = _each(lambda x, e, s, a, vn: _bdot(x * e, s) + _bdot(a, vn), qn, e_gcum, state, a_qk, v_new)
    new_state = _each(lambda s, gl, x, gc, vn: s * jnp.exp(gl) + _bdot(x * jnp.exp(gl - gc), vn, TN),
                      state, g_last, kn, gcum, v_new)
    return o, new_state


GDN_HEADS_PER_STEP = 8


GDN_TIME_TILE = 256


def _gdn_specs(n_tokens, reverse):
    hb, hd, tt = GDN_HEADS_PER_STEP, GDN_HEAD_DIM, GDN_TIME_TILE
    nb, nt = GDN_HEADS // hb, n_tokens // tt

    def when(t):
        return nt - 1 - t if reverse else t

    q = pl.BlockSpec((tt, hb * hd), lambda h, t: (when(t), h))
    k = pl.BlockSpec((tt, hb * hd), lambda h, t: (when(t), nb + h))
    v = pl.BlockSpec((tt, hb * hd), lambda h, t: (when(t), 2 * nb + h))
    vec = pl.BlockSpec((tt, hb), lambda h, t: (when(t), h))
    states = pl.BlockSpec((hb, tt // GDN_CHUNK, hd, hd), lambda h, t: (h, when(t), 0, 0))
    return q, k, v, vec, states


def _gdn_fwd(qkv, beta, g):
    n_tokens = qkv.shape[0]
    hb, hd, tt = GDN_HEADS_PER_STEP, GDN_HEAD_DIM, GDN_TIME_TILE
    n_chunks = tt // GDN_CHUNK
    q_s, k_s, v_s, vec, st = _gdn_specs(n_tokens, False)

    def body(q_ref, k_ref, v_ref, b_ref, g_ref, o_ref, st_ref, state):
        @pl.when(pl.program_id(1) == 0)
        def _():
            state[...] = jnp.zeros_like(state)

        def step(c, carry):
            r = pl.ds(pl.multiple_of(c * GDN_CHUNK, GDN_CHUNK), GDN_CHUNK)
            cols = [slice(h * hd, (h + 1) * hd) for h in range(hb)]
            old = [state[h] for h in range(hb)]
            o, new = _gdn_chunks(
                [q_ref[r, cs] for cs in cols], [k_ref[r, cs] for cs in cols], [v_ref[r, cs] for cs in cols],
                [b_ref[r, h:h + 1] for h in range(hb)], [g_ref[r, h:h + 1] for h in range(hb)], old)
            for h in range(hb):
                st_ref[h, c] = old[h]
                o_ref[r, cols[h]] = o[h]
                state[h] = new[h]
            return carry

        lax.fori_loop(0, n_chunks, step, 0)

    return pl.pallas_call(
        body, grid=(GDN_HEADS // hb, n_tokens // tt), in_specs=[q_s, k_s, v_s, vec, vec], out_specs=(q_s, st),
        out_shape=(jax.ShapeDtypeStruct((n_tokens, GDN_WIDTH), F32),
                   jax.ShapeDtypeStruct((GDN_HEADS, n_tokens // GDN_CHUNK, hd, hd), F32)),
        scratch_shapes=[pltpu.VMEM((hb, hd, hd), F32)],
        name="gdn_fwd", compiler_params=_params(2),
    )(qkv, qkv, qkv, beta, g)


def _gdn_bwd(qkv, beta, g, states, do):
    n_tokens = qkv.shape[0]
    hb, hd, tt = GDN_HEADS_PER_STEP, GDN_HEAD_DIM, GDN_TIME_TILE
    n_chunks = tt // GDN_CHUNK
    q_s, k_s, v_s, vec, st = _gdn_specs(n_tokens, True)

    assert hb == GDN_HEADS

    def body(q_ref, k_ref, v_ref, b_ref, g_ref, st_ref, do_ref, dqkv_ref, db_ref, dg_ref, dstate):
        @pl.when(pl.program_id(1) == 0)
        def _():
            dstate[...] = jnp.zeros_like(dstate)

        def step(i, carry):
            c = n_chunks - 1 - i
            r = pl.ds(pl.multiple_of(c * GDN_CHUNK, GDN_CHUNK), GDN_CHUNK)
            cols = [slice(h * hd, (h + 1) * hd) for h in range(hb)]
            args = ([q_ref[r, cs] for cs in cols], [k_ref[r, cs] for cs in cols], [v_ref[r, cs] for cs in cols],
                    [b_ref[r, h:h + 1] for h in range(hb)], [g_ref[r, h:h + 1] for h in range(hb)],
                    [st_ref[h, c] for h in range(hb)])
            cts = ([do_ref[r, cs] for cs in cols], [dstate[h] for h in range(hb)])
            dq, dk, dv, db, dg, dst = jax.vjp(_gdn_chunks, *args)[1](cts)
            for h in range(hb):
                for part, grad in enumerate((dq, dk, dv)):
                    dqkv_ref[r, pl.ds(part * GDN_WIDTH + h * hd, hd)] = grad[h]
                db_ref[r, h:h + 1] = db[h]
                dg_ref[r, h:h + 1] = dg[h]
                dstate[h] = dst[h]
            return carry

        lax.fori_loop(0, n_chunks, step, 0)

    n_t = n_tokens // tt
    thin = jax.ShapeDtypeStruct(beta.shape, F32)
    return pl.pallas_call(
        body, grid=(GDN_HEADS // hb, n_t), in_specs=[q_s, k_s, v_s, vec, vec, st, q_s],
        out_specs=(pl.BlockSpec((tt, 3 * GDN_WIDTH), lambda h, t: (n_t - 1 - t, 0)), vec, vec),
        out_shape=(jax.ShapeDtypeStruct(qkv.shape, F32), thin, thin),
        scratch_shapes=[pltpu.VMEM((hb, hd, hd), F32)],
        name="gdn_bwd", compiler_params=_params(2),
    )(qkv, qkv, qkv, beta, g, states, do)


FFN_ROW_TILE = 256
FFN_FWD_ROW_TILE = 512


def _resident(shape):
    return pl.BlockSpec(shape, lambda i: (0,) * len(shape), pipeline_mode=pl.Buffered(1))


def _ffn_fwd(x, gain, wg, wu, wd, name):
    n_tokens, d = x.shape
    n_shards, n, _ = wg.shape
    tm = FFN_FWD_ROW_TILE

    def body(x_ref, gain_ref, wg_ref, wu_ref, wd_ref, o_ref, g_ref, u_ref):
        xv = x_ref[...]
        h = (xv * lax.rsqrt(jnp.mean(xv * xv, axis=-1, keepdims=True) + EPS) * gain_ref[...]).astype(BF16)
        acc = jnp.zeros((tm, d), F32)
        for j in range(n_shards):
            g = lax.dot_general(h, wg_ref[j], NT, preferred_element_type=F32)
            u = lax.dot_general(h, wu_ref[j], NT, preferred_element_type=F32)
            g_ref[j] = g
            u_ref[j] = u
            a = (g * jax.nn.sigmoid(g) * u).astype(BF16)
            acc = acc + lax.dot_general(a, wd_ref[j], NN, preferred_element_type=F32)
        o_ref[...] = xv + 0.5 * acc

    row = pl.BlockSpec((tm, d), lambda i: (i, 0))
    hid = pl.BlockSpec((n_shards, tm, n), lambda i: (0, i, 0))
    return pl.pallas_call(
        body, grid=(n_tokens // tm,),
        in_specs=[row, _resident(gain.shape), _resident(wg.shape), _resident(wu.shape), _resident(wd.shape)],
        out_specs=(row, hid, hid),
        out_shape=(jax.ShapeDtypeStruct(x.shape, F32), jax.ShapeDtypeStruct((n_shards, n_tokens, n), F32),
                   jax.ShapeDtypeStruct((n_shards, n_tokens, n), F32)),
        name=name, compiler_params=_params(1),
    )(x, gain, wg, wu, wd)


def _ffn_bwd_rows(x, gain, dy, g, u, wg, wu, wd, name):
    n_tokens, d = x.shape
    n_shards, n, _ = wg.shape
    tm = FFN_ROW_TILE

    def body(x_ref, gain_ref, dy_ref, g_ref, u_ref, wg_ref, wu_ref, wd_ref,
             dx_ref, dgain_ref, h_ref, dyh_ref, a_ref, dg_ref, du_ref):
        xv, dyv, gain_v = x_ref[...], dy_ref[...], gain_ref[...]
        r = lax.rsqrt(jnp.mean(xv * xv, axis=-1, keepdims=True) + EPS)
        xhat = xv * r
        h_ref[...] = (xhat * gain_v).astype(BF16)
        dyh = (0.5 * dyv).astype(BF16)
        dyh_ref[...] = dyh
        dh = jnp.zeros((tm, d), F32)
        for j in range(n_shards):
            da = lax.dot_general(dyh, wd_ref[j], NT, preferred_element_type=F32)
            gv, uv = g_ref[j], u_ref[j]
            sg = jax.nn.sigmoid(gv)
            silu = gv * sg
            a_ref[j] = (silu * uv).astype(BF16)
            dg = (da * uv * (sg + silu * (1.0 - sg))).astype(BF16)
            du = (da * silu).astype(BF16)
            dg_ref[j] = dg
            du_ref[j] = du
            dh = dh + lax.dot_general(dg, wg_ref[j], NN, preferred_element_type=F32)
            dh = dh + lax.dot_general(du, wu_ref[j], NN, preferred_element_type=F32)
        dxhat = dh * gain_v
        dx_ref[...] = dyv + r * (dxhat - xhat * jnp.mean(dxhat * xhat, axis=-1, keepdims=True))

        @pl.when(pl.program_id(0) == 0)
        def _():
            dgain_ref[...] = jnp.zeros_like(dgain_ref)

        dgain_ref[...] += jnp.sum(dh * xhat, axis=0, keepdims=True)

    row = pl.BlockSpec((tm, d), lambda i: (i, 0))
    hid = pl.BlockSpec((n_shards, tm, n), lambda i: (0, i, 0))
    hid_shape = (n_shards, n_tokens, n)
    return pl.pallas_call(
        body, grid=(n_tokens // tm,),
        in_specs=[row, _resident(gain.shape), row, hid, hid, _resident(wg.shape), _resident(wu.shape),
                  _resident(wd.shape)],
        out_specs=(row, pl.BlockSpec(gain.shape, lambda i: (0, 0)), row, row, hid, hid, hid),
        out_shape=(jax.ShapeDtypeStruct(x.shape, F32), jax.ShapeDtypeStruct(gain.shape, F32),
                   jax.ShapeDtypeStruct(x.shape, BF16), jax.ShapeDtypeStruct(x.shape, BF16),
                   jax.ShapeDtypeStruct(hid_shape, BF16), jax.ShapeDtypeStruct(hid_shape, BF16),
                   jax.ShapeDtypeStruct(hid_shape, BF16)),
        name=name, compiler_params=_params(1),
    )(x, gain, dy, g, u, wg, wu, wd)


def _ffn_bwd_weights(h, dyh, a, dg, du, name, owner=None):
    n_shards, n_tokens, n = a.shape
    d = h.shape[1]

    def products(h_ref, dyh_ref, a_ref, dg_ref, du_ref):
        hv = h_ref[...]
        return (lax.dot_general(dg_ref[0], hv, TN, preferred_element_type=F32),
                lax.dot_general(du_ref[0], hv, TN, preferred_element_type=F32),
                lax.dot_general(a_ref[0], dyh_ref[...], TN, preferred_element_type=F32))

    hid = pl.BlockSpec((1, n_tokens, n), lambda j, *_: (j, 0, 0))
    out = pl.BlockSpec((1, n, d), lambda j, *_: (j, 0, 0))
    ins = [pl.BlockSpec(h.shape, lambda j, *_: (0, 0), pipeline_mode=pl.Buffered(1)),
           pl.BlockSpec(dyh.shape, lambda j, *_: (0, 0), pipeline_mode=pl.Buffered(1)), hid, hid, hid]
    if owner is None:
        def body(*refs):
            for ref, val in zip(refs[5:], products(*refs[:5])):
                ref[0] = val

        return pl.pallas_call(
            body, grid=(n_shards,), in_specs=ins, out_specs=(out, out, out),
            out_shape=(jax.ShapeDtypeStruct((n_shards, n, d), F32),) * 3, name=name, compiler_params=_params(1),
        )(h, dyh, a, dg, du)

    def body(owner_ref, *refs):
        vals = products(*refs[:5])
        for ref, val in zip(refs[5:8], vals):
            ref[0] = val.astype(BF16)

        @pl.when(pl.program_id(0) == owner_ref[0])
        def _():
            for ref, val in zip(refs[8:], vals):
                ref[0] = val

    mine = pl.BlockSpec((1, n, d), lambda j, *_: (0, 0, 0))
    outs = pl.pallas_call(
        body,
        grid_spec=pltpu.PrefetchScalarGridSpec(num_scalar_prefetch=1, grid=(n_shards,), in_specs=ins,
                                               out_specs=(out, out, out, mine, mine, mine)),
        out_shape=(jax.ShapeDtypeStruct((n_shards, n, d), BF16),) * 3 + (jax.ShapeDtypeStruct((1, n, d), F32),) * 3,
        name=name, compiler_params=_params(1),
    )(owner, h, dyh, a, dg, du)
    return outs[:3], outs[3:]


IN_PIECES = (("wq_a", 0, 768), ("wk_a", 768, 1536), ("wv_a", 1536, 2304), ("w_qkvb", 2304, 5376),
             ("w_small", 5376, 5392), ("w_ggate", 5392, 6416), ("w_gatea", 6416, 7440), ("w_gateb", 7440, 8464))
IN_NAMES = tuple(name for name, _, _ in IN_PIECES)


def _in_rows(lo, hi):
    return lo, max(hi, lo + LANES)


N_ROTATED = 2


def _in_proj_fwd(x, gain, wt, cos, sin):
    n_tokens, d = x.shape
    tm = FFN_ROW_TILE
    rows = [_in_rows(lo, hi) for _, lo, hi in IN_PIECES]

    def body(x_ref, gain_ref, wt_ref, cos_ref, sin_ref, *o_refs):
        xv = x_ref[...]
        h = (xv * lax.rsqrt(jnp.mean(xv * xv, axis=-1, keepdims=True) + EPS) * gain_ref[...]).astype(BF16)
        for k, ((lo, hi), o_ref) in enumerate(zip(rows, o_refs)):
            z = lax.dot_general(h, wt_ref[lo:hi, :], NT, preferred_element_type=F32)
            o_ref[...] = _rotate(z, cos_ref[...], sin_ref[...]) if k < N_ROTATED else z

    tab = pl.BlockSpec((tm, LANES), lambda i: (i, 0))
    return pl.pallas_call(
        body, grid=(n_tokens // tm,),
        in_specs=[pl.BlockSpec((tm, d), lambda i: (i, 0)), _resident(gain.shape), _resident(wt.shape), tab, tab],
        out_specs=tuple(pl.BlockSpec((tm, hi - lo), lambda i: (i, 0)) for lo, hi in rows),
        out_shape=tuple(jax.ShapeDtypeStruct((n_tokens, hi - lo), F32) for lo, hi in rows),
        name="in_proj_fwd", compiler_params=_params(1),
    )(x, gain, wt, cos, sin)


def _in_proj_bwd_rows(x, gain, dres, dzs, wt, cos, sin):
    n_tokens, d = x.shape
    tm = FFN_ROW_TILE
    n = len(dzs)
    rows = [_in_rows(lo, hi) for _, lo, hi in IN_PIECES]

    def body(x_ref, gain_ref, dres_ref, cos_ref, sin_ref, *refs):
        dz_refs, wt_ref = refs[:n], refs[n]
        dx_ref, dgain_ref, h_ref = refs[n + 1:n + 4]
        unrotated_refs = refs[n + 4:]
        xv, gain_v = x_ref[...], gain_ref[...]
        r = lax.rsqrt(jnp.mean(xv * xv, axis=-1, keepdims=True) + EPS)
        xhat = xv * r
        h_ref[...] = (xhat * gain_v).astype(BF16)
        dh = jnp.zeros((tm, d), F32)
        for k, (dz_ref, (lo, hi)) in enumerate(zip(dz_refs, rows)):
            dz = dz_ref[...]
            if k < N_ROTATED:
                dz = _rotate(dz, cos_ref[...], -sin_ref[...]).astype(BF16)
                unrotated_refs[k][...] = dz
            dh = dh + lax.dot_general(dz.astype(BF16), wt_ref[lo:hi, :], NN, preferred_element_type=F32)
        dxhat = dh * gain_v
        dx_ref[...] = dres_ref[...] + r * (dxhat - xhat * jnp.mean(dxhat * xhat, axis=-1, keepdims=True))

        @pl.when(pl.program_id(0) == 0)
        def _():
            dgain_ref[...] = jnp.zeros_like(dgain_ref)

        dgain_ref[...] += jnp.sum(dh * xhat, axis=0, keepdims=True)

    row = pl.BlockSpec((tm, d), lambda i: (i, 0))
    tab = pl.BlockSpec((tm, LANES), lambda i: (i, 0))
    dz_specs = [pl.BlockSpec((tm, dz.shape[1]), lambda i: (i, 0)) for dz in dzs]
    outs = pl.pallas_call(
        body, grid=(n_tokens // tm,),
        in_specs=[row, _resident(gain.shape), row, tab, tab] + dz_specs + [_resident(wt.shape)],
        out_specs=(row, pl.BlockSpec(gain.shape, lambda i: (0, 0)), row) + tuple(dz_specs[:N_ROTATED]),
        out_shape=(jax.ShapeDtypeStruct(x.shape, F32), jax.ShapeDtypeStruct(gain.shape, F32),
                   jax.ShapeDtypeStruct(x.shape, BF16))
        + tuple(jax.ShapeDtypeStruct(dz.shape, BF16) for dz in dzs[:N_ROTATED]),
        name="in_proj_bwd_rows", compiler_params=_params(1),
    )(x, gain, dres, cos, sin, *dzs, wt)
    return outs[0], outs[1], outs[2], outs[3:]


def _in_proj_bwd_weight(dwt, h, dz, lo, hi, name):
    n_tokens, d = h.shape
    width = hi - lo
    tn = _tile(width, 512) if width >= LANES else width
    dz_tile = max(tn, LANES)

    def body(dwt_ref, h_ref, dz_ref, o_ref):
        o_ref[...] = lax.dot_general(dz_ref[:, :tn].astype(BF16), h_ref[...], TN, preferred_element_type=F32)

    return pl.pallas_call(
        body, grid=(width // tn,),
        in_specs=[ANY, _resident(h.shape), pl.BlockSpec((n_tokens, dz_tile), lambda j: (0, j))],
        out_specs=pl.BlockSpec((pl.Element(tn), pl.Element(d)), lambda j: (pl.multiple_of(lo + j * tn, 16), 0)),
        out_shape=jax.ShapeDtypeStruct(dwt.shape, F32), input_output_aliases={0: 0}, name=name,
        compiler_params=_params(1),
    )(dwt, h, dz)


def _split_small(z):
    return z[:, :GDN_HEADS], z[:, GDN_HEADS:2 * GDN_HEADS]


def _heads3(q, k, v):
    return _to_heads(q), _to_heads(k), _to_heads(v)


def _tokens6(o, lse):
    return tuple(_from_heads(o)) + tuple(_from_heads(lse))


def mixer_forward(x1, w, small):
    n_tokens = x1.shape[0]
    cos, sin = _rope_tables(n_tokens)
    proj = dict(zip(IN_NAMES, _in_proj_fwd(x1, small["mix_norm"], w["w_in_t"], cos, sin)))
    (qh, kh, vh), heads_vjp = jax.vjp(_heads3, proj["wq_a"], proj["wk_a"], proj["wv_a"])
    o, lse = _attn_fwd(qh, kh, vh)
    per_group, tokens_vjp = jax.vjp(_tokens6, o, lse)
    ya = _rowwise_fwd(_combine_fn, "combine", per_group, (), (), 512, 1)[0]
    pa = _matmul(ya, w["w_branch_a"], name="branch_a")
    qkv = _conv_fwd(proj["w_qkvb"], small["gdn_conv_w"])
    raw, small_vjp = jax.vjp(_split_small, proj["w_small"])
    gdn_params = (small["gdn_a_log"], small["gdn_dt_bias"])
    beta, gcum = _rowwise_fwd(_beta_decay_fn, "beta_decay", raw, (), gdn_params, 512, 1)
    ob, states = _gdn_fwd(qkv, beta, gcum)
    gate_in = (ob, proj["w_ggate"])
    yb = _rowwise_fwd(_outnorm_gate_fn, "outnorm_gate", gate_in, (), (small["gdn_out_norm"],), 512, GDN_HEADS)[0]
    pb = _matmul(yb, w["w_branch_b"], name="branch_b")
    merge_in = (proj["w_gatea"], proj["w_gateb"], pa, pb)
    merged = _rowwise_fwd(_merge_fn, "merge", merge_in, (), (), 256, 1)[0]
    x2 = _matmul(merged, w["w_out"], name="out", res=x1)
    saved = dict(x1=x1, proj=proj, cos=cos, sin=sin, heads_vjp=heads_vjp, heads=(qh, kh, vh), tokens_vjp=tokens_vjp,
                 per_group=per_group, ya=ya, qkv=qkv, raw=raw, small_vjp=small_vjp, beta=beta, gcum=gcum, states=states,
                 gate_in=gate_in, yb=yb, merge_in=merge_in, merged=merged)
    return x2, saved


def mixer_backward(dx2, s, w, small):
    proj = s["proj"]
    dmerged = _matmul(dx2, w["w_out"], name="out_da", tb=True)
    grads = dict(w_out=_matmul(s["merged"], dx2, name="out_dw", ta=True))
    dgate_a, dgate_b, dpa, dpb = _rowwise_bwd(_merge_fn, "merge_bwd", s["merge_in"], (), (), (dmerged,), 256, 1)
    dyb = _matmul(dpb, w["w_branch_b"], name="branch_b_da", tb=True)
    grads["w_branch_b"] = _matmul(s["yb"], dpb, name="branch_b_dw", ta=True)
    dya = _matmul(dpa, w["w_branch_a"], name="branch_a_da", tb=True)
    grads["w_branch_a"] = _matmul(s["ya"], dpa, name="branch_a_dw", ta=True)
    dob, dggate, grads["gdn_out_norm"] = _rowwise_bwd(
        _outnorm_gate_fn, "outnorm_gate_bwd", s["gate_in"], (), (small["gdn_out_norm"],), (dyb,), 512, GDN_HEADS)
    dqkv, dbeta, dgcum = _gdn_bwd(s["qkv"], s["beta"], s["gcum"], s["states"], dob)
    gdn_params = (small["gdn_a_log"], small["gdn_dt_bias"])
    dbeta_raw, ddecay_raw, grads["gdn_a_log"], grads["gdn_dt_bias"] = _rowwise_bwd(
        _beta_decay_fn, "beta_decay_bwd", s["raw"], (), gdn_params, (dbeta, dgcum), 512, 1)
    dsmall = s["small_vjp"]((dbeta_raw, ddecay_raw))[0]
    dqkvb, grads["gdn_conv_w"] = _conv_bwd(proj["w_qkvb"], small["gdn_conv_w"], dqkv)
    dper_group = _rowwise_bwd(_combine_fn, "combine_bwd", s["per_group"], (), (), (dya,), 512, 1)
    do, dlse = s["tokens_vjp"](tuple(dper_group))
    dqh, dkh, dvh = _attn_bwd(*s["heads"], do, dlse)
    dq_rot, dk_rot, dv = s["heads_vjp"]((dqh, dkh, dvh))
    dzs = (dq_rot, dk_rot, dv, dqkvb, dsmall, dggate, dgate_a, dgate_b)
    dx1, grads["mix_norm"], h, unrotated = _in_proj_bwd_rows(
        s["x1"], small["mix_norm"], dx2, dzs, w["w_in_t"], s["cos"], s["sin"])
    dzs = tuple(unrotated) + dzs[N_ROTATED:]
    dwt = lax.empty(w["w_in_t"].shape, F32)
    for (name, lo, hi), dz in zip(IN_PIECES, dzs):
        dwt = _in_proj_bwd_weight(dwt, h, dz, lo, hi, "in_proj_dw_" + name)
    grads["w_in_t"] = dwt
    return dx1, grads


def ffn_forward(x, gain, w, tag):
    out, g, u = _ffn_fwd(x, gain, w[tag + "_w_gate"], w[tag + "_w_up"], w[tag + "_w_down"], tag + "_fwd")
    return out, (x, g, u)


def ffn_backward(dy, saved, gain, w, tag, owner=None):
    x, g, u = saved
    weights = (w[tag + "_w_gate"], w[tag + "_w_up"], w[tag + "_w_down"])
    dx, dgain, h, dyh, a, dg, du = _ffn_bwd_rows(x, gain, dy, g, u, *weights, tag + "_bwd_rows")
    return dx, dgain, _ffn_bwd_weights(h, dyh, a, dg, du, tag + "_bwd_weights", owner)


def loss_head(x3, target, gain):
    row_loss = _rowwise_fwd(_loss_fn, "loss", (x3,), (target,), (gain,), 256, 1)[0]
    dx3, dgain = _rowwise_bwd(_loss_fn, "loss_bwd", (x3,), (target,), (gain,), (jnp.ones_like(row_loss),), 256, 1)
    return jnp.sum(row_loss), dx3, dgain


BIG_WEIGHTS = ("ffn1_w_gate", "ffn1_w_up", "ffn1_w_down", "w_in", "w_branch_a", "w_branch_b", "w_out",
               "ffn2_w_gate", "ffn2_w_up", "ffn2_w_down")
TRANSPOSED = ("ffn1_w_gate", "ffn1_w_up", "w_in", "ffn2_w_gate", "ffn2_w_up")
CONV_SHARD = (GDN_CONV, 3 * GDN_WIDTH // N_DEV)
SMALL_ROWS = 24
ANY = pl.BlockSpec(memory_space=pl.ANY)


TOKEN = jax.ShapeDtypeStruct((8, LANES), F32)


def _after(value, token):
    return value + token[0, 0].astype(value.dtype)


def _position():
    return lax.axis_index("x"), lax.axis_index("y"), lax.axis_index("c")


def all_gather_shards(shards, name):
    n = len(shards)

    def body(*refs):
        x_refs, out_refs = refs[:n], refs[n:2 * n]
        send_sems, recv_sems, local_sems = refs[2 * n + 1:]
        x, y, c = _position()
        me, sibling = (x, y, c), (x, y, 1 - c)
        chips = [(1 - x, y), (x, 1 - y), (1 - x, 1 - y)]

        def slab(a, px, py, pc):
            return out_refs[a].at[4 * px + 2 * py + pc]

        def copy(a, k, block, to, src=None):
            return pltpu.make_async_remote_copy(
                src_ref=slab(a, *block) if src is None else src, dst_ref=slab(a, *block),
                send_sem=send_sems.at[7 * a + k], recv_sem=recv_sems.at[7 * a + k], device_id=to, device_id_type=MESH)

        mine = [pltpu.make_async_copy(x_refs[a], slab(a, *me), local_sems.at[a]) for a in range(n)]
        for cp in mine:
            cp.start()
        first = []
        for j, chip in enumerate(chips):
            first += [copy(a, 1 + j, me, (*chip, c), src=x_refs[a]) for a in range(n)]
        first += [copy(a, 0, me, sibling, src=x_refs[a]) for a in range(n)]
        for cp in first:
            cp.start()
        passed = []
        for j, chip in enumerate(chips):
            for a in range(n):
                copy(a, 1 + j, (*chip, c), me).wait_recv()
                cp = copy(a, 4 + j, (*chip, c), sibling)
                cp.start()
                passed.append(cp)
        for a in range(n):
            copy(a, 0, sibling, me).wait_recv()
        for j, chip in enumerate(chips):
            for a in range(n):
                copy(a, 4 + j, (*chip, 1 - c), me).wait_recv()
        for cp in first + passed:
            cp.wait_send()
        for cp in mine:
            cp.wait()
        refs[2 * n][...] = jnp.zeros_like(refs[2 * n])

    outs = pl.pallas_call(
        body, out_shape=tuple(jax.ShapeDtypeStruct((N_DEV,) + s.shape, s.dtype) for s in shards) + (TOKEN,),
        in_specs=[ANY] * n, out_specs=(ANY,) * n + (pl.BlockSpec(memory_space=pltpu.VMEM),),
        scratch_shapes=[pltpu.SemaphoreType.DMA((7 * n,)), pltpu.SemaphoreType.DMA((7 * n,)),
                        pltpu.SemaphoreType.DMA((n,))],
        name=name,
    )(*shards)
    return outs[:n], outs[n]


def exchange_with_sibling(grads):
    n = len(grads)

    def body(*refs):
        g_refs, recv_refs = refs[:n], refs[n:2 * n]
        send_sems, recv_sems = refs[2 * n:]
        x, y, c = _position()
        copies = [pltpu.make_async_remote_copy(
            src_ref=g_refs[a].at[2 * k + 1 - c], dst_ref=recv_refs[a].at[k], send_sem=send_sems.at[4 * a + k],
            recv_sem=recv_sems.at[4 * a + k], device_id=(x, y, 1 - c), device_id_type=MESH)
            for k in range(4) for a in range(n)]
        for cp in copies:
            cp.start()
        for cp in copies:
            cp.wait()

    return pl.pallas_call(
        body, out_shape=tuple(jax.ShapeDtypeStruct((4,) + g.shape[1:], g.dtype) for g in grads),
        in_specs=[ANY] * n, out_specs=(ANY,) * n,
        scratch_shapes=[pltpu.SemaphoreType.DMA((4 * n,)), pltpu.SemaphoreType.DMA((4 * n,))], name="rs_sibling",
    )(*grads)


ELEMENTWISE_TILE_BYTES = 1536 * 1024


def _tile2(rows, cols):
    if rows % 256 == 0:
        return 256, cols
    if rows * cols * 4 > ELEMENTWISE_TILE_BYTES and cols % 256 == 0:
        return rows, 256
    return rows, cols


def add_sibling(grads, received, core, name):
    _, rows, width = grads.shape
    tr, tc = _tile2(rows, width)

    def body(c_ref, g_ref, r_ref, o_ref):
        o_ref[...] = (g_ref[...] + r_ref[...]).astype(BF16)

    blk = (1, tr, tc)
    return pl.pallas_call(
        body,
        grid_spec=pltpu.PrefetchScalarGridSpec(
            num_scalar_prefetch=1, grid=(4, rows // tr, width // tc),
            in_specs=[pl.BlockSpec(blk, lambda k, i, j, c_ref: (2 * k + c_ref[0], i, j)),
                      pl.BlockSpec(blk, lambda k, i, j, c_ref: (k, i, j))],
            out_specs=pl.BlockSpec(blk, lambda k, i, j, c_ref: (k, i, j))),
        out_shape=jax.ShapeDtypeStruct((4, rows, width), BF16), name=name, compiler_params=_params(3),
    )(core, grads, received)


HBM = pl.BlockSpec(memory_space=pltpu.HBM)
SEM = pl.BlockSpec(memory_space=pltpu.SEMAPHORE)
DATAFLOW_EFFECT = pltpu.SideEffectType.DATAFLOW_SIDE_EFFECTING
N_PEERS = N_DEV - 1


def _peer(mask):
    x, y, c = _position()
    px = 1 - x if mask & 4 else x
    py = 1 - y if mask & 2 else y
    pc = 1 - c if mask & 1 else c
    return (px, py, pc), 4 * px + 2 * py + pc


ALL_PEERS = tuple(range(1, N_DEV))
OTHER_CHIPS = (4, 2, 6)


def _exchange_peers(mode):
    return OTHER_CHIPS if mode == "chips" else ALL_PEERS


def _direct_copies(src_refs, land_refs, send_sems, recv_sems, mode):
    x, y, c = _position()
    me = 4 * x + 2 * y + c
    masks = _exchange_peers(mode)
    copies = []
    for a, (src, land) in enumerate(zip(src_refs, land_refs)):
        for slot, mask in enumerate(masks):
            peer, peer_index = _peer(mask)
            k = len(masks) * a + slot
            source = {"gather": lambda: src, "scatter": lambda: src.at[peer_index],
                      "chips": lambda: src.at[2 * peer[0] + peer[1]]}[mode]()
            copies.append(pltpu.make_async_remote_copy(
                src_ref=source, dst_ref=land.at[me] if mode == "gather" else land.at[slot],
                send_sem=send_sems.at[k], recv_sem=recv_sems.at[k], device_id=peer, device_id_type=MESH))
    return copies


def direct_exchange_start(arrays, mode, name):
    n = len(arrays)
    n_peers = len(_exchange_peers(mode))
    lands = [lax.empty((N_DEV,) + a.shape if mode == "gather" else (n_peers,) + a.shape[1:], a.dtype) for a in arrays]

    def body(*refs):
        src_refs, land_refs = refs[:n], refs[n:2 * n]
        send_sems, recv_sems = refs[2 * n], refs[2 * n + 1]
        token = refs[-1]
        for cp in _direct_copies(src_refs, land_refs, send_sems, recv_sems, mode):
            cp.start()
        token[...] = jnp.zeros_like(token)

    sems = pltpu.SemaphoreType.DMA((n_peers * n,))
    outs = pl.pallas_call(
        body, name=name,
        out_shape=(sems, sems) + tuple(pltpu.HBM(a.shape, a.dtype) for a in arrays)
        + tuple(pltpu.HBM(l.shape, l.dtype) for l in lands) + (TOKEN,),
        in_specs=[HBM] * (2 * n), out_specs=(SEM, SEM) + (HBM,) * (2 * n) + (pl.BlockSpec(memory_space=pltpu.VMEM),),
        input_output_aliases={i: 2 + i for i in range(2 * n)},
        compiler_params=pltpu.CompilerParams(has_side_effects=DATAFLOW_EFFECT),
    )(*[pltpu.with_memory_space_constraint(a, pltpu.HBM) for a in list(arrays) + lands])
    return outs[0], outs[1], outs[2:2 + n], outs[2 + n:2 + 2 * n], outs[-1]


def direct_exchange_wait(send_sems, recv_sems, arrays, lands, after, mode, name):
    n = len(arrays)

    def body(*refs):
        src_refs, land_refs = refs[:n], refs[n:2 * n]
        send_sems, recv_sems = refs[2 * n], refs[2 * n + 1]
        for cp in _direct_copies(src_refs, land_refs, send_sems, recv_sems, mode):
            cp.wait_send()
            cp.wait_recv()
        refs[-1][...] = jnp.zeros_like(refs[-1])

    outs = pl.pallas_call(
        body, name=name,
        out_shape=tuple(pltpu.HBM(a.shape, a.dtype) for a in arrays) + tuple(pltpu.HBM(l.shape, l.dtype) for l in lands)
        + (TOKEN,),
        in_specs=[HBM] * (2 * n) + [SEM, SEM, pl.BlockSpec(memory_space=pl.ANY)],
        out_specs=(HBM,) * (2 * n) + (pl.BlockSpec(memory_space=pltpu.VMEM),),
        input_output_aliases={i: i for i in range(2 * n)},
        compiler_params=pltpu.CompilerParams(has_side_effects=DATAFLOW_EFFECT),
    )(*arrays, *lands, send_sems, recv_sems, after)
    return outs[n:]


def adamw_direct(w, m, v, own, received, name):
    row_per_tile = w.shape[0] != 1
    rows, cols = (w.shape[0], w.shape[2]) if row_per_tile else w.shape[-2:]
    tr, tc = _tile2(rows, cols)

    def body(w_ref, m_ref, v_ref, own_ref, r_ref, g_ref, d_ref, nm_ref, nv_ref):
        gv = own_ref[0]
        for j in range(N_PEERS):
            gv = gv + r_ref[j].astype(F32)
        nm = ADAM_B1 * m_ref[...] + (1.0 - ADAM_B1) * gv
        nv = ADAM_B2 * v_ref[...] + (1.0 - ADAM_B2) * (gv * gv)
        m_hat = nm / (1.0 - ADAM_B1 ** ADAM_STEP)
        v_hat = nv / (1.0 - ADAM_B2 ** ADAM_STEP)
        g_ref[...] = gv
        d_ref[...] = -ADAM_LR * (m_hat / (jnp.sqrt(v_hat) + ADAM_EPS) + ADAM_WD * w_ref[...])
        nm_ref[...] = nm
        nv_ref[...] = nv

    if row_per_tile:
        one = pl.BlockSpec((tr, None, tc), lambda i, j: (i, 0, j))
    else:
        one = pl.BlockSpec((None, tr, tc), lambda i, j: (0, i, j))
    out = jax.ShapeDtypeStruct(w.shape, F32)
    return pl.pallas_call(
        body, grid=(rows // tr, cols // tc),
        in_specs=[one, one, one, pl.BlockSpec((1, tr, tc), lambda i, j: (0, i, j)),
                  pl.BlockSpec((N_PEERS, tr, tc), lambda i, j: (0, i, j))],
        out_specs=(one,) * 4, out_shape=(out,) * 4, name=name, compiler_params=_params(2),
    )(w, m, v, own, received)


def all_reduce_small(vals):
    rows, width = vals.shape

    def body(x_ref, out_ref, all_ref, send_sems, recv_sems):
        x, y, c = _position()
        me, sibling = (x, y, c), (x, y, 1 - c)
        chips = [(1 - x, y), (x, 1 - y), (1 - x, 1 - y)]

        def slab(px, py, pc):
            return all_ref.at[4 * px + 2 * py + pc]

        def copy(k, block, to, src=None):
            return pltpu.make_async_remote_copy(
                src_ref=slab(*block) if src is None else src, dst_ref=slab(*block),
                send_sem=send_sems.at[k], recv_sem=recv_sems.at[k], device_id=to, device_id_type=MESH)

        first = [copy(0, me, sibling, src=x_ref)]
        first += [copy(1 + j, me, (*chip, c), src=x_ref) for j, chip in enumerate(chips)]
        for cp in first:
            cp.start()
        all_ref[4 * x + 2 * y + c] = x_ref[...]
        passed = [copy(4 + j, (*chip, c), sibling) for j, chip in enumerate(chips)]
        for j, chip in enumerate(chips):
            copy(1 + j, (*chip, c), me).wait_recv()
            passed[j].start()
        copy(0, sibling, me).wait_recv()
        for j, chip in enumerate(chips):
            copy(4 + j, (*chip, 1 - c), me).wait_recv()
        for cp in first + passed:
            cp.wait_send()
        total = all_ref[0]
        for d in range(1, N_DEV):
            total = total + all_ref[d]
        out_ref[...] = total

    vmem = pl.BlockSpec(memory_space=pltpu.VMEM)
    return pl.pallas_call(
        body, out_shape=(jax.ShapeDtypeStruct(vals.shape, F32), jax.ShapeDtypeStruct((N_DEV, rows, width), F32)),
        in_specs=[vmem], out_specs=(vmem, vmem),
        scratch_shapes=[pltpu.SemaphoreType.DMA((7,)), pltpu.SemaphoreType.DMA((7,))], name="small_allreduce",
    )(vals)[0]


def adamw(w, g, m, v, name):
    shape = w.shape
    w2, g2, m2, v2 = [a.reshape((-1, shape[-1])) for a in (w, g, m, v)]
    rows, cols = w2.shape
    tr = 256 if rows % 256 == 0 else rows

    def body(w_ref, g_ref, m_ref, v_ref, d_ref, nm_ref, nv_ref):
        gv = g_ref[...]
        nm = ADAM_B1 * m_ref[...] + (1.0 - ADAM_B1) * gv
        nv = ADAM_B2 * v_ref[...] + (1.0 - ADAM_B2) * (gv * gv)
        m_hat = nm / (1.0 - ADAM_B1 ** ADAM_STEP)
        v_hat = nv / (1.0 - ADAM_B2 ** ADAM_STEP)
        d_ref[...] = -ADAM_LR * (m_hat / (jnp.sqrt(v_hat) + ADAM_EPS) + ADAM_WD * w_ref[...])
        nm_ref[...] = nm
        nv_ref[...] = nv

    blk = pl.BlockSpec((tr, cols), lambda i: (i, 0))
    out = jax.ShapeDtypeStruct((rows, cols), F32)
    outs = pl.pallas_call(
        body, grid=(rows // tr,), in_specs=[blk] * 4, out_specs=(blk,) * 3, out_shape=(out,) * 3,
        name=name, compiler_params=_params(1),
    )(w2, g2, m2, v2)
    return tuple(o.reshape(shape) for o in outs)


def adamw_summed(w, m, v, grads, from_sibling, received, me, name):
    rows, cols = w.shape[-2:]
    tr, tc = _tile2(rows, cols)

    def body(me_ref, w_ref, m_ref, v_ref, own_ref, sib_ref, r_ref, g_ref, d_ref, nm_ref, nv_ref):
        gv = own_ref[0] + sib_ref[0]
        for j in range(3):
            gv = gv + r_ref[j].astype(F32)
        nm = ADAM_B1 * m_ref[0] + (1.0 - ADAM_B1) * gv
        nv = ADAM_B2 * v_ref[0] + (1.0 - ADAM_B2) * (gv * gv)
        m_hat = nm / (1.0 - ADAM_B1 ** ADAM_STEP)
        v_hat = nv / (1.0 - ADAM_B2 ** ADAM_STEP)
        g_ref[0] = gv
        d_ref[0] = -ADAM_LR * (m_hat / (jnp.sqrt(v_hat) + ADAM_EPS) + ADAM_WD * w_ref[0])
        nm_ref[0] = nm
        nv_ref[0] = nv

    one = pl.BlockSpec((1, tr, tc), lambda i, j, me_ref: (0, i, j))
    out = jax.ShapeDtypeStruct((1, rows, cols), F32)
    return pl.pallas_call(
        body,
        grid_spec=pltpu.PrefetchScalarGridSpec(
            num_scalar_prefetch=1, grid=(rows // tr, cols // tc),
            in_specs=[one, one, one, pl.BlockSpec((1, tr, tc), lambda i, j, me_ref: (me_ref[0], i, j)),
                      pl.BlockSpec((1, tr, tc), lambda i, j, me_ref: (me_ref[1], i, j)),
                      pl.BlockSpec((3, tr, tc), lambda i, j, me_ref: (0, i, j))],
            out_specs=(one,) * 4),
        out_shape=(out,) * 4, name=name, compiler_params=_params(2),
    )(me, w, m, v, grads, from_sibling, received)


SMALL_VECTORS = ("ffn1_norm", "mix_norm", "ffn2_norm", "final_norm")


def _pack_small(gs):
    row = jnp.concatenate([gs["gdn_a_log"].reshape(-1), gs["gdn_dt_bias"].reshape(-1), gs["gdn_out_norm"].reshape(-1)])
    rows = [gs[n].reshape(1, D_MODEL) for n in SMALL_VECTORS]
    rows.append(jnp.pad(row, (0, D_MODEL - row.shape[0])).reshape(1, D_MODEL))
    rows.append(gs["gdn_conv_w"].reshape(-1, D_MODEL))
    packed = jnp.concatenate(rows, axis=0)
    return jnp.pad(packed, ((0, SMALL_ROWS - packed.shape[0]), (0, 0)))


def _unpack_small(packed):
    out = {n: packed[i].reshape(1, D_MODEL) for i, n in enumerate(SMALL_VECTORS)}
    row = packed[len(SMALL_VECTORS)]
    out["gdn_a_log"] = row[:GDN_HEADS].reshape(1, GDN_HEADS)
    out["gdn_dt_bias"] = row[GDN_HEADS:2 * GDN_HEADS].reshape(1, GDN_HEADS)
    out["gdn_out_norm"] = row[2 * GDN_HEADS:2 * GDN_HEADS + GDN_HEAD_DIM].reshape(1, GDN_HEAD_DIM)
    first = len(SMALL_VECTORS) + 1
    out["gdn_conv_w"] = packed[first:first + GDN_CONV * 3].reshape(GDN_CONV, 3 * GDN_WIDTH)
    return out


WEIGHTS = ("ffn1_norm", "ffn1_w_gate", "ffn1_w_up", "ffn1_w_down", "mix_norm", "w_in", "gdn_conv_w", "gdn_a_log",
           "gdn_dt_bias", "gdn_out_norm", "w_branch_a", "w_branch_b", "w_out", "ffn2_norm", "ffn2_w_gate",
           "ffn2_w_up", "ffn2_w_down", "final_norm")


def kernel(x, ffn1_norm, ffn1_w_gate, ffn1_w_up, ffn1_w_down, mix_norm, w_in, gdn_conv_w, gdn_a_log, gdn_dt_bias, gdn_out_norm, w_branch_a, w_branch_b, w_out, ffn2_norm, ffn2_w_gate, ffn2_w_up, ffn2_w_down, final_norm, loss_target, m_ffn1_norm, m_ffn1_w_gate, m_ffn1_w_up, m_ffn1_w_down, m_mix_norm, m_w_in, m_gdn_conv_w, m_gdn_a_log, m_gdn_dt_bias, m_gdn_out_norm, m_w_branch_a, m_w_branch_b, m_w_out, m_ffn2_norm, m_ffn2_w_gate, m_ffn2_w_up, m_ffn2_w_down, m_final_norm, v_ffn1_norm, v_ffn1_w_gate, v_ffn1_w_up, v_ffn1_w_down, v_mix_norm, v_w_in, v_gdn_conv_w, v_gdn_a_log, v_gdn_dt_bias, v_gdn_out_norm, v_w_branch_a, v_w_branch_b, v_w_out, v_ffn2_norm, v_ffn2_w_gate, v_ffn2_w_up, v_ffn2_w_down, v_final_norm):
    given = dict(locals())
    px, py, pc = _position()
    big_names = list(BIG_WEIGHTS)

    def shard_view(a, n):
        if n == "w_in":
            return a.transpose(2, 0, 1)
        return a.transpose(0, 2, 1) if n in TRANSPOSED else a

    def shard_unview(a, n):
        if n == "w_in":
            return a.transpose(1, 2, 0)
        return a.transpose(0, 2, 1) if n in TRANSPOSED else a

    me = 4 * px + 2 * py + pc
    me_index = me.astype(jnp.int32).reshape(1)
    late = [n for n in big_names if n.startswith("ffn2")]
    early = [n for n in big_names if n not in late]
    shards = {n: shard_view(given[n], n).reshape(given[n].shape[-1 if n in TRANSPOSED else -2], -1).astype(BF16)
              for n in big_names}
    early_slabs, early_done = all_gather_shards([shards[n] for n in early] + [gdn_conv_w[0]], "gather_weights")
    gathered = dict(zip(early + ["gdn_conv_w"], early_slabs))
    late_gather = direct_exchange_start([_after(shards[n], early_done) for n in late], "gather", "gather_ffn2_start")
    ffn1_norm = _after(ffn1_norm, late_gather[4])
    w = {n: gathered[n] for n in early if n.startswith("ffn")}
    w["w_in_t"] = gathered["w_in"].reshape(-1, D_MODEL)
    w["w_branch_a"] = gathered["w_branch_a"].transpose(1, 0, 2).reshape(256, D_MODEL)
    w["w_branch_b"] = gathered["w_branch_b"].reshape(D_MODEL, D_MODEL)
    w["w_out"] = gathered["w_out"].reshape(D_MODEL, D_MODEL)
    conv_full = gathered["gdn_conv_w"].transpose(1, 0, 2).reshape(GDN_CONV, 3 * GDN_WIDTH)
    small = dict(mix_norm=mix_norm, gdn_a_log=gdn_a_log, gdn_dt_bias=gdn_dt_bias, gdn_out_norm=gdn_out_norm,
                 gdn_conv_w=conv_full)

    x1, ffn1_saved = ffn_forward(x[0], ffn1_norm, w, "ffn1")
    x2, mixer_saved = mixer_forward(x1, w, small)
    late_lands = direct_exchange_wait(*late_gather[:4], x2, "gather", "gather_ffn2_wait")
    for n, land in zip(late, late_lands):
        w[n] = lax.dynamic_update_slice(land, shards[n][None], (me, 0, 0))
    x3, ffn2_saved = ffn_forward(x2, ffn2_norm, w, "ffn2")
    loss_local, dx3, g_final = loss_head(x3, loss_target[0], final_norm.reshape(1, D_MODEL))
    loss = lax.psum(loss_local, ("x", "y", "c"))
    dx2, g_ffn2_norm, (dw2, dw2_own) = ffn_backward(dx3, ffn2_saved, ffn2_norm, w, "ffn2", me_index)
    late_scatter = direct_exchange_start(list(dw2), "scatter", "rs_ffn2_start")
    w_after = dict(w, w_out=_after(w["w_out"], late_scatter[4]))
    dx1, g_w = mixer_backward(dx2, mixer_saved, w_after, small)
    middle = ["w_in", "w_branch_a", "w_branch_b", "w_out"]
    g_big = dict(w_in=g_w["w_in_t"].reshape(N_DEV, -1, D_MODEL),
                 w_branch_a=g_w["w_branch_a"].reshape(256, N_DEV, 128).transpose(1, 0, 2),
                 w_branch_b=g_w["w_branch_b"].reshape(N_DEV, 128, D_MODEL),
                 w_out=g_w["w_out"].reshape(N_DEV, 128, D_MODEL))
    own = dict(zip(late, dw2_own))
    own.update({n: lax.dynamic_index_in_dim(g_big[n], me, 0, keepdims=True) for n in middle[1:]})
    in_rows = g_w["w_in_t"].shape[0] // N_DEV
    own["w_in"] = lax.dynamic_slice(g_w["w_in_t"], (me * in_rows, 0), (in_rows, D_MODEL))[None]
    middle_scatter = direct_exchange_start([g_big[n].astype(BF16) for n in middle], "scatter", "rs_mixer_start")
    grad_x, g_ffn1_norm, dw1 = ffn_backward(dx1, ffn1_saved, _after(ffn1_norm, middle_scatter[4]), w, "ffn1")
    g_small = dict(ffn1_norm=g_ffn1_norm, ffn2_norm=g_ffn2_norm, final_norm=g_final,
                   **{n: g_w[n] for n in ("mix_norm", "gdn_a_log", "gdn_dt_bias", "gdn_out_norm", "gdn_conv_w")})

    first = [n for n in early if n.startswith("ffn1")]
    g_list = list(dw1)
    core = pc.astype(jnp.int32).reshape(1)
    me_and_chip = jnp.stack([me, 2 * px + py]).astype(jnp.int32)
    from_sibling = exchange_with_sibling(g_list)
    partials = [add_sibling(g, r, core, "rs_add_" + n) for n, g, r in zip(first, g_list, from_sibling)]
    first_chips = direct_exchange_start(partials, "chips", "rs_ffn1_start")

    def state_of(n):
        return [shard_view(given[p + n], n) for p in ("", "m_", "v_")]

    results = {}
    late_received = direct_exchange_wait(*late_scatter[:4], first_chips[4], "scatter", "rs_ffn2_wait")
    middle_received = direct_exchange_wait(*middle_scatter[:4], first_chips[4], "scatter", "rs_mixer_wait")
    for n, recv in zip(late + middle, list(late_received[:-1]) + list(middle_received[:-1])):
        outs = adamw_direct(*state_of(n), own[n], recv, "adamw_" + n)
        results[n] = tuple(shard_unview(o, n) for o in outs)

    done = results["w_out"][1]
    from_chips = direct_exchange_wait(*first_chips[:4], done, "chips", "rs_ffn1_wait")
    for n, g, sib, recv in zip(first, g_list, from_sibling, from_chips):
        outs = adamw_summed(*state_of(n), g, sib, recv, me_and_chip, "adamw_" + n)
        results[n] = tuple(shard_unview(o, n) for o in outs)

    small_sum = _unpack_small(all_reduce_small(_after(_pack_small(g_small), from_chips[-1])))
    conv_cols = CONV_SHARD[1]
    small_sum["gdn_conv_w"] = lax.dynamic_slice(small_sum["gdn_conv_w"], (0, me * conv_cols), (GDN_CONV, conv_cols))
    for n in WEIGHTS:
        if n not in results:
            g = small_sum[n].reshape(given[n].shape)
            results[n] = (g,) + adamw(given[n], g, given["m_" + n], given["v_" + n], "adamw_" + n)

    outs = [[results[n][i] for n in WEIGHTS] for i in range(4)]
    return (loss, grad_x[None], *outs[0], *outs[1], *outs[2], *outs[3])
```

```python
import jax
import jax.numpy as jnp
from jax import lax
from jax.experimental import pallas as pl
from jax.experimental.pallas import tpu as pltpu

F32 = jnp.float32
BF16 = jnp.bfloat16
HI = lax.Precision.HIGHEST
MESH = pl.DeviceIdType.MESH

N_DEV = 8
D_MODEL = 1024
EPS = 1e-6
ROPE_THETA = 10000.0
DSW_DILATIONS = (1, 4, 16)
DSW_HEADS_PER_GROUP = 4
DSW_HEAD_DIM = 64
DSW_BLOCK = 128
GDN_HEADS = 8
GDN_HEAD_DIM = 128
GDN_WIDTH = 1024
GDN_CONV = 4
GDN_CHUNK = 64

ADAM_LR = 0.001
ADAM_B1 = 0.9
ADAM_B2 = 0.999
ADAM_EPS = 1e-08
ADAM_WD = 0.01
ADAM_STEP = 10

VMEM_LIMIT_BYTES = 56 * 1024 * 1024
LANES = 128

NN = (((1,), (0,)), ((), ()))
NT = (((1,), (1,)), ((), ()))
TN = (((0,), (0,)), ((), ()))


def _params(n_grid):
    return pltpu.CompilerParams(dimension_semantics=("arbitrary",) * n_grid, vmem_limit_bytes=VMEM_LIMIT_BYTES)


def _tile(n, pref):
    best = None
    t = LANES
    while t <= min(n, pref):
        if n % t == 0:
            best = t
        t += LANES
    return n if best is None else best


def _matmul(a, b, *, name, ta=False, tb=False, res=None, scale=1.0):
    K, M = a.shape if ta else a.shape[::-1]
    N = b.shape[0] if tb else b.shape[1]
    assert (b.shape[1] if tb else b.shape[0]) == K, (a.shape, b.shape, ta, tb)
    tm = _tile(M, 512)
    tn = _tile(N, 512)
    dn = (((0 if ta else 1,), (1 if tb else 0,)), ((), ()))

    def body(*refs):
        a_ref, b_ref = refs[:2]
        o_ref = refs[-1]
        acc = lax.dot_general(a_ref[...].astype(BF16), b_ref[...].astype(BF16), dn, preferred_element_type=F32)
        if scale != 1.0:
            acc = acc * scale
        if res is not None:
            acc = refs[2][...] + acc
        o_ref[...] = acc

    a_spec = pl.BlockSpec((K, tm), lambda i, j: (0, i)) if ta else pl.BlockSpec((tm, K), lambda i, j: (i, 0))
    b_spec = pl.BlockSpec((tn, K), lambda i, j: (j, 0)) if tb else pl.BlockSpec((K, tn), lambda i, j: (0, j))
    o_spec = pl.BlockSpec((tm, tn), lambda i, j: (i, j))
    ins, specs = [a, b], [a_spec, b_spec]
    if res is not None:
        ins.append(res)
        specs.append(o_spec)
    return pl.pallas_call(
        body, grid=(M // tm, N // tn), in_specs=specs, out_specs=o_spec,
        out_shape=jax.ShapeDtypeStruct((M, N), F32), name=name, compiler_params=_params(2),
    )(*ins)


def _rw_specs(arrs, tm, nblk):
    return [pl.BlockSpec((tm, a.shape[1] // nblk), lambda i, j: (i, j)) for a in arrs]


def _rowwise_fwd(fn, name, rows, consts, params, tm, nblk):
    n_rows = rows[0].shape[0]
    tm = min(tm, n_rows)
    ins = list(rows) + list(consts)
    avals = [jax.ShapeDtypeStruct((tm, a.shape[1] // nblk), a.dtype) for a in ins]
    avals += [jax.ShapeDtypeStruct(p.shape, p.dtype) for p in params]
    out_avals = jax.eval_shape(fn, *avals)
    n_in = len(ins) + len(params)

    def body(*refs):
        outs = fn(*[r[...] for r in refs[:n_in]])
        for r, o in zip(refs[n_in:], outs):
            r[...] = o.astype(r.dtype)

    return pl.pallas_call(
        body, grid=(n_rows // tm, nblk),
        in_specs=_rw_specs(ins, tm, nblk) + [pl.BlockSpec(p.shape, lambda i, j: (0, 0)) for p in params],
        out_specs=tuple(pl.BlockSpec((tm, o.shape[1]), lambda i, j: (i, j)) for o in out_avals),
        out_shape=tuple(jax.ShapeDtypeStruct((n_rows, o.shape[1] * nblk), o.dtype) for o in out_avals),
        name=name, compiler_params=_params(2),
    )(*ins, *params)


def _rowwise_bwd(fn, name, rows, consts, params, cts, tm, nblk):
    n_rows = rows[0].shape[0]
    tm = min(tm, n_rows)
    nr, nc, npar, nct = len(rows), len(consts), len(params), len(cts)

    def body(*refs):
        rv = [r[...] for r in refs[:nr]]
        cv = [r[...] for r in refs[nr:nr + nc]]
        pv = [r[...] for r in refs[nr + nc:nr + nc + npar]]
        ctv = [r[...] for r in refs[nr + nc + npar:nr + nc + npar + nct]]
        outs = refs[nr + nc + npar + nct:]
        _, vjp = jax.vjp(lambda *d: fn(*d[:nr], *cv, *d[nr:]), *rv, *pv)
        grads = vjp(tuple(ctv))
        for k in range(nr):
            outs[k][...] = grads[k]
        first = jnp.logical_and(pl.program_id(0) == 0, pl.program_id(1) == 0)
        for k in range(npar):
            ref = outs[nr + k]

            @pl.when(first)
            def _(ref=ref):
                ref[...] = jnp.zeros_like(ref)

            ref[...] += grads[nr + k]

    ins = list(rows) + list(consts)
    return pl.pallas_call(
        body, grid=(n_rows // tm, nblk),
        in_specs=(_rw_specs(ins, tm, nblk) + [pl.BlockSpec(p.shape, lambda i, j: (0, 0)) for p in params]
                  + _rw_specs(cts, tm, nblk)),
        out_specs=tuple(_rw_specs(rows, tm, nblk) + [pl.BlockSpec(p.shape, lambda i, j: (0, 0)) for p in params]),
        out_shape=tuple([jax.ShapeDtypeStruct(a.shape, F32) for a in rows]
                        + [jax.ShapeDtypeStruct(p.shape, F32) for p in params]),
        name=name, compiler_params=_params(2),
    )(*ins, *params, *cts)


def _merge_fn(ga, gb, pa, pb):
    return (jax.nn.sigmoid(ga) * pa + jax.nn.sigmoid(gb) * pb,)


def _outnorm_gate_fn(o, gate, gain):
    y = o * lax.rsqrt(jnp.mean(o * o, axis=-1, keepdims=True) + EPS) * gain
    return (y * (gate * jax.nn.sigmoid(gate)),)


def _beta_decay_fn(beta_raw, decay_raw, a_log, dt_bias):
    z = decay_raw + dt_bias
    softplus = jnp.maximum(z, 0.0) + jnp.log(1.0 + jnp.exp(-jnp.abs(z)))
    g = -jnp.exp(a_log) * softplus
    rows = g.shape[0]
    ii = lax.broadcasted_iota(jnp.int32, (rows, rows), 0)
    jj = lax.broadcasted_iota(jnp.int32, (rows, rows), 1)
    same_chunk_before = jnp.logical_and(jj <= ii, jj // GDN_CHUNK == ii // GDN_CHUNK).astype(F32)
    gcum = lax.dot_general(same_chunk_before, g, NN, precision=HI, preferred_element_type=F32)
    return jax.nn.sigmoid(beta_raw), gcum


def _combine_fn(o0, o1, o2, l0, l1, l2):
    m = lax.stop_gradient(jnp.maximum(jnp.maximum(l0, l1), l2))
    e0, e1, e2 = jnp.exp(l0 - m), jnp.exp(l1 - m), jnp.exp(l2 - m)
    return ((e0 * o0 + e1 * o1 + e2 * o2) / (e0 + e1 + e2),)


def _loss_fn(x, target, gain):
    y = x * lax.rsqrt(jnp.mean(x * x, axis=-1, keepdims=True) + EPS) * gain
    err = y - target
    return (0.5 * jnp.mean(err * err, axis=-1, keepdims=True),)


def _rotate(v, cos, sin):
    half = DSW_HEAD_DIM // 2
    lane = lax.broadcasted_iota(jnp.int32, cos.shape, 1)
    low = (lane % DSW_HEAD_DIM) < half
    slabs = []
    for s in range(v.shape[1] // LANES):
        x = v[:, s * LANES:(s + 1) * LANES]
        swapped = jnp.where(low, pltpu.roll(x, LANES - half, 1), pltpu.roll(x, half, 1))
        slabs.append(x * cos + swapped * sin)
    return jnp.concatenate(slabs, axis=1)


def _rope_tables(n_tokens):
    half = DSW_HEAD_DIM // 2
    inv_freq = ROPE_THETA ** (-jnp.arange(half, dtype=F32) / half)
    ang = jnp.arange(n_tokens, dtype=F32)[:, None] * inv_freq[None, :]
    cos, sin = jnp.cos(ang), jnp.sin(ang)
    return jnp.tile(jnp.concatenate([cos, cos], 1), (1, 2)), jnp.tile(jnp.concatenate([-sin, sin], 1), (1, 2))


def _attn_probs(q, kp, kc, group, n):
    blk = DSW_BLOCK
    k = _each(lambda a, b: jnp.concatenate([a, b], axis=0).astype(BF16), kp, kc)
    s = _each(lambda a, b: lax.dot_general(a.astype(BF16), b, NT, preferred_element_type=F32)
              * (DSW_HEAD_DIM ** -0.5), q, k)
    blocks_per_seq = jnp.where(group == 0, 16, jnp.where(group == 1, 4, 1))
    first = (n % blocks_per_seq) == 0
    qi = lax.broadcasted_iota(jnp.int32, (blk, 2 * blk), 0)
    kj = lax.broadcasted_iota(jnp.int32, (blk, 2 * blk), 1)
    dist = qi + blk - kj
    valid = (dist >= 0) & (dist <= blk) & jnp.logical_or(kj >= blk, jnp.logical_not(first))
    s = _each(lambda a: jnp.where(valid, a, -1e30), s)
    m = _each(lambda a: jnp.max(a, axis=-1, keepdims=True), s)
    p = _each(lambda a, b: jnp.exp(a - b), s, m)
    l = _each(lambda a: jnp.sum(a, axis=-1, keepdims=True), p)
    return _each(lambda a, b: a / b, p, l), _each(lambda a, b: a + jnp.log(b), m, l), k


PAIRS_PER_GROUP = DSW_HEADS_PER_GROUP // 2


def _attn_specs(n_tokens):
    blk = DSW_BLOCK
    cur = pl.BlockSpec((PAIRS_PER_GROUP, blk, LANES), lambda g, n: (g, n, 0))
    prev = pl.BlockSpec((PAIRS_PER_GROUP, blk, LANES), lambda g, n: (g, jnp.maximum(n - 1, 0), 0))
    return cur, prev


def _heads_of(ref):
    pairs = [ref[p] for p in range(PAIRS_PER_GROUP)]
    return [x[:, s * DSW_HEAD_DIM:(s + 1) * DSW_HEAD_DIM] for x in pairs for s in range(2)]


def _pairs_of(heads):
    return [jnp.concatenate(heads[2 * p:2 * p + 2], axis=1) for p in range(PAIRS_PER_GROUP)]


def _attn_fwd(q, k, v):
    n_pairs, n_tokens, _ = q.shape
    cur, prev = _attn_specs(n_tokens)

    def body(q_ref, kp_ref, kc_ref, vp_ref, vc_ref, o_ref, l_ref):
        p, lse, _ = _attn_probs(_heads_of(q_ref), _heads_of(kp_ref), _heads_of(kc_ref),
                                pl.program_id(0), pl.program_id(1))
        vv = _each(lambda a, b: jnp.concatenate([a, b], axis=0).astype(BF16), _heads_of(vp_ref), _heads_of(vc_ref))
        o = _each(lambda a, b: lax.dot_general(a.astype(BF16), b, NN, preferred_element_type=F32), p, vv)
        lse_wide = _each(lambda a: jnp.broadcast_to(a, (DSW_BLOCK, DSW_HEAD_DIM)), lse)
        for pair, (o_pair, l_pair) in enumerate(zip(_pairs_of(o), _pairs_of(lse_wide))):
            o_ref[pair] = o_pair
            l_ref[pair] = l_pair

    return pl.pallas_call(
        body, grid=(n_pairs // PAIRS_PER_GROUP, n_tokens // DSW_BLOCK), in_specs=[cur, prev, cur, prev, cur],
        out_specs=(cur, cur), out_shape=(jax.ShapeDtypeStruct(q.shape, F32), jax.ShapeDtypeStruct(q.shape, F32)),
        name="attn_fwd", compiler_params=_params(2),
    )(q, k, k, v, v)


def _attn_bwd(q, k, v, do, dlse):
    n_pairs, n_tokens, _ = q.shape
    nblk = n_tokens // DSW_BLOCK
    cur, prev = _attn_specs(n_tokens)
    part = pl.BlockSpec((PAIRS_PER_GROUP, 1, 2 * DSW_BLOCK, LANES), lambda g, n: (g, n, 0, 0))
    scale = DSW_HEAD_DIM ** -0.5

    def body(q_ref, kp_ref, kc_ref, vp_ref, vc_ref, do_ref, dl_ref, dq_ref, dk_ref, dv_ref):
        qs = _heads_of(q_ref)
        p, _, kb = _attn_probs(qs, _heads_of(kp_ref), _heads_of(kc_ref), pl.program_id(0), pl.program_id(1))
        qb = _each(lambda a: a.astype(BF16), qs)
        vv = _each(lambda a, b: jnp.concatenate([a, b], axis=0).astype(BF16), _heads_of(vp_ref), _heads_of(vc_ref))
        dob = _each(lambda a: a.astype(BF16), _heads_of(do_ref))
        dp = _each(lambda a, b: lax.dot_general(a, b, NT, preferred_element_type=F32), dob, vv)
        dv = _each(lambda a, b: lax.dot_general(a.astype(BF16), b, TN, preferred_element_type=F32), p, dob)
        dl = _each(lambda a: jnp.sum(a, axis=-1, keepdims=True), _heads_of(dl_ref))
        ds = _each(lambda a, b, c: (a * (b - jnp.sum(b * a, axis=-1, keepdims=True) + c) * scale).astype(BF16),
                   p, dp, dl)
        dq = _each(lambda a, b: lax.dot_general(a, b, NN, preferred_element_type=F32), ds, kb)
        dk = _each(lambda a, b: lax.dot_general(a, b, TN, preferred_element_type=F32), ds, qb)
        for pair, (dq_pair, dk_pair, dv_pair) in enumerate(zip(_pairs_of(dq), _pairs_of(dk), _pairs_of(dv))):
            dq_ref[pair] = dq_pair
            dk_ref[pair, 0] = dk_pair
            dv_ref[pair, 0] = dv_pair

    partial_shape = jax.ShapeDtypeStruct((n_pairs, nblk, 2 * DSW_BLOCK, LANES), F32)
    dq, dkp, dvp = pl.pallas_call(
        body, grid=(n_pairs // PAIRS_PER_GROUP, nblk), in_specs=[cur, prev, cur, prev, cur, cur, cur],
        out_specs=(cur, part, part), out_shape=(jax.ShapeDtypeStruct(q.shape, F32), partial_shape, partial_shape),
        name="attn_bwd", compiler_params=_params(2),
    )(q, k, k, v, v, do, dlse)

    def fold(partial):
        own = partial[:, :, DSW_BLOCK:]
        from_next = jnp.pad(partial[:, 1:, :DSW_BLOCK], ((0, 0), (0, 1), (0, 0), (0, 0)))
        return (own + from_next).reshape(n_pairs, n_tokens, LANES)

    return dq, fold(dkp), fold(dvp)


def _to_heads(a):
    n_tokens = a.shape[0]
    outs = []
    for gi, d in enumerate(DSW_DILATIONS):
        blk = a[:, gi * 256:(gi + 1) * 256].reshape(n_tokens // d, d, PAIRS_PER_GROUP, LANES)
        outs.append(blk.transpose(2, 1, 0, 3).reshape(PAIRS_PER_GROUP, n_tokens, LANES))
    return jnp.concatenate(outs, 0)


def _from_heads(a):
    n_tokens = a.shape[1]
    outs = []
    for gi, d in enumerate(DSW_DILATIONS):
        blk = a[gi * PAIRS_PER_GROUP:(gi + 1) * PAIRS_PER_GROUP].reshape(PAIRS_PER_GROUP, d, n_tokens // d, LANES)
        outs.append(blk.transpose(2, 1, 0, 3).reshape(n_tokens, PAIRS_PER_GROUP * LANES))
    return outs


CONV_TILE = 512


def _shift_down(x, k, rows):
    return x if k == 0 else jnp.where(rows >= k, pltpu.roll(x, k, 0), 0.0)


def _shift_up(x, k, rows):
    n = x.shape[0]
    return x if k == 0 else jnp.where(rows < n - k, pltpu.roll(x, n - k, 0), 0.0)


def _conv_pre(x, w):
    rows = lax.broadcasted_iota(jnp.int32, x.shape, 0)
    acc = x * w[GDN_CONV - 1:GDN_CONV]
    for k in range(1, GDN_CONV):
        acc = acc + _shift_down(x, k, rows) * w[GDN_CONV - 1 - k:GDN_CONV - k]
    return acc, rows


def _conv_fwd(x, w):
    n_tokens, width = x.shape
    big = pl.BlockSpec((n_tokens, CONV_TILE), lambda j: (0, j))
    wsp = pl.BlockSpec((GDN_CONV, CONV_TILE), lambda j: (0, j))

    def body(x_ref, w_ref, o_ref):
        acc, _ = _conv_pre(x_ref[...], w_ref[...])
        o_ref[...] = acc * jax.nn.sigmoid(acc)

    return pl.pallas_call(
        body, grid=(width // CONV_TILE,), in_specs=[big, wsp], out_specs=big,
        out_shape=jax.ShapeDtypeStruct(x.shape, F32), name="conv_fwd", compiler_params=_params(1),
    )(x, w)


def _conv_bwd(x, w, dy):
    n_tokens, width = x.shape
    big = pl.BlockSpec((n_tokens, CONV_TILE), lambda j: (0, j))
    wsp = pl.BlockSpec((GDN_CONV, CONV_TILE), lambda j: (0, j))

    def body(x_ref, w_ref, dy_ref, dx_ref, dw_ref):
        xv, wv = x_ref[...], w_ref[...]
        acc, rows = _conv_pre(xv, wv)
        sg = jax.nn.sigmoid(acc)
        dacc = dy_ref[...] * (sg + acc * sg * (1.0 - sg))
        dx = dacc * wv[GDN_CONV - 1:GDN_CONV]
        for k in range(1, GDN_CONV):
            dx = dx + _shift_up(dacc, k, rows) * wv[GDN_CONV - 1 - k:GDN_CONV - k]
        dx_ref[...] = dx
        for k in range(GDN_CONV):
            dw_ref[GDN_CONV - 1 - k:GDN_CONV - k, :] = jnp.sum(dacc * _shift_down(xv, k, rows), axis=0, keepdims=True)

    return pl.pallas_call(
        body, grid=(width // CONV_TILE,), in_specs=[big, wsp, big], out_specs=(big, wsp),
        out_shape=(jax.ShapeDtypeStruct(x.shape, F32), jax.ShapeDtypeStruct(w.shape, F32)),
        name="conv_bwd", compiler_params=_params(1),
    )(x, w, dy)


def _dot(a, b, dn=NN):
    return lax.dot_general(a, b, dn, precision=HI, preferred_element_type=F32)


def _dot3(a, b, dn=NN):
    return lax.dot_general(a, b, dn, precision=lax.Precision.HIGH, preferred_element_type=F32)


def _bf16_dot(a, b, dn):
    return lax.dot_general(a.astype(BF16), b.astype(BF16), dn, preferred_element_type=F32)


_DOT_GRADS = {NN: (("g", "b", NT), ("a", "g", TN)), NT: (("g", "b", NN), ("g", "a", TN)),
              TN: (("b", "g", NT), ("a", "g", NN))}


def _make_bdot(dn):
    @jax.custom_vjp
    def op(a, b):
        return _bf16_dot(a, b, dn)

    def fwd(a, b):
        return op(a, b), (a, b)

    def bwd(saved, g):
        vals = dict(a=saved[0], b=saved[1], g=g)
        return tuple(_bf16_dot(vals[x], vals[y], form) for x, y, form in _DOT_GRADS[dn])

    op.defvjp(fwd, bwd)
    return op


_BDOTS = {dn: _make_bdot(dn) for dn in (NN, NT, TN)}


def _bdot(a, b, dn=NN):
    return _BDOTS[dn](a, b)


def _each(fn, *lists):
    return [fn(*items) for items in zip(*lists)]


def _gdn_chunks(q, k, v, b, gcum, state):
    c = GDN_CHUNK
    ii = lax.broadcasted_iota(jnp.int32, (c, c), 0)
    jj = lax.broadcasted_iota(jnp.int32, (c, c), 1)
    eye = (ii == jj).astype(F32)
    qn = _each(lambda x: x * lax.rsqrt(jnp.sum(x * x, axis=-1, keepdims=True) + EPS) * (GDN_HEAD_DIM ** -0.5), q)
    kn = _each(lambda x: x * lax.rsqrt(jnp.sum(x * x, axis=-1, keepdims=True) + EPS), k)
    gcum_i = _each(lambda x: jnp.broadcast_to(x, (c, c)), gcum)
    gcum_j = _each(jnp.transpose, gcum_i)
    decay = _each(lambda x, y: jnp.exp(jnp.where(jj <= ii, x - y, -1e30)), gcum_i, gcum_j)
    g_last = _each(lambda x: x[c - 1:c, :], gcum)
    e_gcum = _each(jnp.exp, gcum)
    kbeta = _each(lambda x, y: x * y, kn, b)
    vbeta = _each(lambda x, y: x * y, v, b)
    m = _each(lambda x, y, d: jnp.where(jj < ii, _bdot(x, y, NT) * d, 0.0), kbeta, kn, decay)
    inv = _each(lambda x: eye - x, m)
    power = _each(lambda x: _dot3(x, x), m)
    for step in range(5):
        inv = _each(lambda x, p: x + _dot3(x, p), inv, power)
        if step < 4:
            power = _each(lambda p: _dot3(p, p), power)
    u = _each(_dot3, inv, vbeta)
    w = _each(lambda x, y, e: _dot3(x, y * e), inv, kbeta, e_gcum)
    a_qk = _each(lambda x, y, d: _bdot(x, y, NT) * d, qn, kn, decay)
    v_new = _each(lambda x, y, s: x - _bdot(y, s), u, w, state)
    o = _each(lambda x, e, s, a, vn: _bdot(x * e, s) + _bdot(a, vn), qn, e_gcum, state, a_qk, v_new)
    new_state = _each(lambda s, gl, x, gc, vn: s * jnp.exp(gl) + _bdot(x * jnp.exp(gl - gc), vn, TN),
                      state, g_last, kn, gcum, v_new)
    return o, new_state


GDN_HEADS_PER_STEP = 8


GDN_TIME_TILE = 256


def _gdn_specs(n_tokens, reverse):
    hb, hd, tt = GDN_HEADS_PER_STEP, GDN_HEAD_DIM, GDN_TIME_TILE
    nb, nt = GDN_HEADS // hb, n_tokens // tt

    def when(t):
        return nt - 1 - t if reverse else t

    q = pl.BlockSpec((tt, hb * hd), lambda h, t: (when(t), h))
    k = pl.BlockSpec((tt, hb * hd), lambda h, t: (when(t), nb + h))
    v = pl.BlockSpec((tt, hb * hd), lambda h, t: (when(t), 2 * nb + h))
    vec = pl.BlockSpec((tt, hb), lambda h, t: (when(t), h))
    states = pl.BlockSpec((hb, tt // GDN_CHUNK, hd, hd), lambda h, t: (h, when(t), 0, 0))
    return q, k, v, vec, states


def _gdn_fwd(qkv, beta, g):
    n_tokens = qkv.shape[0]
    hb, hd, tt = GDN_HEADS_PER_STEP, GDN_HEAD_DIM, GDN_TIME_TILE
    n_chunks = tt // GDN_CHUNK
    q_s, k_s, v_s, vec, st = _gdn_specs(n_tokens, False)

    def body(q_ref, k_ref, v_ref, b_ref, g_ref, o_ref, st_ref, state):
        @pl.when(pl.program_id(1) == 0)
        def _():
            state[...] = jnp.zeros_like(state)

        def step(c, carry):
            r = pl.ds(pl.multiple_of(c * GDN_CHUNK, GDN_CHUNK), GDN_CHUNK)
            cols = [slice(h * hd, (h + 1) * hd) for h in range(hb)]
            old = [state[h] for h in range(hb)]
            o, new = _gdn_chunks(
                [q_ref[r, cs] for cs in cols], [k_ref[r, cs] for cs in cols], [v_ref[r, cs] for cs in cols],
                [b_ref[r, h:h + 1] for h in range(hb)], [g_ref[r, h:h + 1] for h in range(hb)], old)
            for h in range(hb):
                st_ref[h, c] = old[h]
                o_ref[r, cols[h]] = o[h]
                state[h] = new[h]
            return carry

        lax.fori_loop(0, n_chunks, step, 0)

    return pl.pallas_call(
        body, grid=(GDN_HEADS // hb, n_tokens // tt), in_specs=[q_s, k_s, v_s, vec, vec], out_specs=(q_s, st),
        out_shape=(jax.ShapeDtypeStruct((n_tokens, GDN_WIDTH), F32),
                   jax.ShapeDtypeStruct((GDN_HEADS, n_tokens // GDN_CHUNK, hd, hd), F32)),
        scratch_shapes=[pltpu.VMEM((hb, hd, hd), F32)],
        name="gdn_fwd", compiler_params=_params(2),
    )(qkv, qkv, qkv, beta, g)


def _gdn_bwd(qkv, beta, g, states, do):
    n_tokens = qkv.shape[0]
    hb, hd, tt = GDN_HEADS_PER_STEP, GDN_HEAD_DIM, GDN_TIME_TILE
    n_chunks = tt // GDN_CHUNK
    q_s, k_s, v_s, vec, st = _gdn_specs(n_tokens, True)

    assert hb == GDN_HEADS

    def body(q_ref, k_ref, v_ref, b_ref, g_ref, st_ref, do_ref, dqkv_ref, db_ref, dg_ref, dstate):
        @pl.when(pl.program_id(1) == 0)
        def _():
            dstate[...] = jnp.zeros_like(dstate)

        def step(i, carry):
            c = n_chunks - 1 - i
            r = pl.ds(pl.multiple_of(c * GDN_CHUNK, GDN_CHUNK), GDN_CHUNK)
            cols = [slice(h * hd, (h + 1) * hd) for h in range(hb)]
            args = ([q_ref[r, cs] for cs in cols], [k_ref[r, cs] for cs in cols], [v_ref[r, cs] for cs in cols],
                    [b_ref[r, h:h + 1] for h in range(hb)], [g_ref[r, h:h + 1] for h in range(hb)],
                    [st_ref[h, c] for h in range(hb)])
            cts = ([do_ref[r, cs] for cs in cols], [dstate[h] for h in range(hb)])
            dq, dk, dv, db, dg, dst = jax.vjp(_gdn_chunks, *args)[1](cts)
            for h in range(hb):
                for part, grad in enumerate((dq, dk, dv)):
                    dqkv_ref[r, pl.ds(part * GDN_WIDTH + h * hd, hd)] = grad[h]
                db_ref[r, h:h + 1] = db[h]
                dg_ref[r, h:h + 1] = dg[h]
                dstate[h] = dst[h]
            return carry

        lax.fori_loop(0, n_chunks, step, 0)

    n_t = n_tokens // tt
    thin = jax.ShapeDtypeStruct(beta.shape, F32)
    return pl.pallas_call(
        body, grid=(GDN_HEADS // hb, n_t), in_specs=[q_s, k_s, v_s, vec, vec, st, q_s],
        out_specs=(pl.BlockSpec((tt, 3 * GDN_WIDTH), lambda h, t: (n_t - 1 - t, 0)), vec, vec),
        out_shape=(jax.ShapeDtypeStruct(qkv.shape, F32), thin, thin),
        scratch_shapes=[pltpu.VMEM((hb, hd, hd), F32)],
        name="gdn_bwd", compiler_params=_params(2),
    )(qkv, qkv, qkv, beta, g, states, do)


FFN_ROW_TILE = 256
FFN_FWD_ROW_TILE = 512


def _resident(shape):
    return pl.BlockSpec(shape, lambda i: (0,) * len(shape), pipeline_mode=pl.Buffered(1))


def _ffn_fwd(x, gain, wg, wu, wd, name):
    n_tokens, d = x.shape
    n_shards, n, _ = wg.shape
    tm = FFN_FWD_ROW_TILE

    def body(x_ref, gain_ref, wg_ref, wu_ref, wd_ref, o_ref, g_ref, u_ref):
        xv = x_ref[...]
        h = (xv * lax.rsqrt(jnp.mean(xv * xv, axis=-1, keepdims=True) + EPS) * gain_ref[...]).astype(BF16)
        acc = jnp.zeros((tm, d), F32)
        for j in range(n_shards):
            g = lax.dot_general(h, wg_ref[j], NT, preferred_element_type=F32)
            u = lax.dot_general(h, wu_ref[j], NT, preferred_element_type=F32)
            g_ref[j] = g
            u_ref[j] = u
            a = (g * jax.nn.sigmoid(g) * u).astype(BF16)
            acc = acc + lax.dot_general(a, wd_ref[j], NN, preferred_element_type=F32)
        o_ref[...] = xv + 0.5 * acc

    row = pl.BlockSpec((tm, d), lambda i: (i, 0))
    hid = pl.BlockSpec((n_shards, tm, n), lambda i: (0, i, 0))
    return pl.pallas_call(
        body, grid=(n_tokens // tm,),
        in_specs=[row, _resident(gain.shape), _resident(wg.shape), _resident(wu.shape), _resident(wd.shape)],
        out_specs=(row, hid, hid),
        out_shape=(jax.ShapeDtypeStruct(x.shape, F32), jax.ShapeDtypeStruct((n_shards, n_tokens, n), F32),
                   jax.ShapeDtypeStruct((n_shards, n_tokens, n), F32)),
        name=name, compiler_params=_params(1),
    )(x, gain, wg, wu, wd)


def _ffn_bwd_rows(x, gain, dy, g, u, wg, wu, wd, name):
    n_tokens, d = x.shape
    n_shards, n, _ = wg.shape
    tm = FFN_ROW_TILE

    def body(x_ref, gain_ref, dy_ref, g_ref, u_ref, wg_ref, wu_ref, wd_ref,
             dx_ref, dgain_ref, h_ref, dyh_ref, a_ref, dg_ref, du_ref):
        xv, dyv, gain_v = x_ref[...], dy_ref[...], gain_ref[...]
        r = lax.rsqrt(jnp.mean(xv * xv, axis=-1, keepdims=True) + EPS)
        xhat = xv * r
        h_ref[...] = (xhat * gain_v).astype(BF16)
        dyh = (0.5 * dyv).astype(BF16)
        dyh_ref[...] = dyh
        dh = jnp.zeros((tm, d), F32)
        for j in range(n_shards):
            da = lax.dot_general(dyh, wd_ref[j], NT, preferred_element_type=F32)
            gv, uv = g_ref[j], u_ref[j]
            sg = jax.nn.sigmoid(gv)
            silu = gv * sg
            a_ref[j] = (silu * uv).astype(BF16)
            dg = (da * uv * (sg + silu * (1.0 - sg))).astype(BF16)
            du = (da * silu).astype(BF16)
            dg_ref[j] = dg
            du_ref[j] = du
            dh = dh + lax.dot_general(dg, wg_ref[j], NN, preferred_element_type=F32)
            dh = dh + lax.dot_general(du, wu_ref[j], NN, preferred_element_type=F32)
        dxhat = dh * gain_v
        dx_ref[...] = dyv + r * (dxhat - xhat * jnp.mean(dxhat * xhat, axis=-1, keepdims=True))

        @pl.when(pl.program_id(0) == 0)
        def _():
            dgain_ref[...] = jnp.zeros_like(dgain_ref)

        dgain_ref[...] += jnp.sum(dh * xhat, axis=0, keepdims=True)

    row = pl.BlockSpec((tm, d), lambda i: (i, 0))
    hid = pl.BlockSpec((n_shards, tm, n), lambda i: (0, i, 0))
    hid_shape = (n_shards, n_tokens, n)
    return pl.pallas_call(
        body, grid=(n_tokens // tm,),
        in_specs=[row, _resident(gain.shape), row, hid, hid, _resident(wg.shape), _resident(wu.shape),
                  _resident(wd.shape)],
        out_specs=(row, pl.BlockSpec(gain.shape, lambda i: (0, 0)), row, row, hid, hid, hid),
        out_shape=(jax.ShapeDtypeStruct(x.shape, F32), jax.ShapeDtypeStruct(gain.shape, F32),
                   jax.ShapeDtypeStruct(x.shape, BF16), jax.ShapeDtypeStruct(x.shape, BF16),
                   jax.ShapeDtypeStruct(hid_shape, BF16), jax.ShapeDtypeStruct(hid_shape, BF16),
                   jax.ShapeDtypeStruct(hid_shape, BF16)),
        name=name, compiler_params=_params(1),
    )(x, gain, dy, g, u, wg, wu, wd)


def _ffn_bwd_weights(h, dyh, a, dg, du, name, owner=None):
    n_shards, n_tokens, n = a.shape
    d = h.shape[1]

    def products(h_ref, dyh_ref, a_ref, dg_ref, du_ref):
        hv = h_ref[...]
        return (lax.dot_general(dg_ref[0], hv, TN, preferred_element_type=F32),
                lax.dot_general(du_ref[0], hv, TN, preferred_element_type=F32),
                lax.dot_general(a_ref[0], dyh_ref[...], TN, preferred_element_type=F32))

    hid = pl.BlockSpec((1, n_tokens, n), lambda j, *_: (j, 0, 0))
    out = pl.BlockSpec((1, n, d), lambda j, *_: (j, 0, 0))
    ins = [pl.BlockSpec(h.shape, lambda j, *_: (0, 0), pipeline_mode=pl.Buffered(1)),
           pl.BlockSpec(dyh.shape, lambda j, *_: (0, 0), pipeline_mode=pl.Buffered(1)), hid, hid, hid]
    if owner is None:
        def body(*refs):
            for ref, val in zip(refs[5:], products(*refs[:5])):
                ref[0] = val

        return pl.pallas_call(
            body, grid=(n_shards,), in_specs=ins, out_specs=(out, out, out),
            out_shape=(jax.ShapeDtypeStruct((n_shards, n, d), F32),) * 3, name=name, compiler_params=_params(1),
        )(h, dyh, a, dg, du)

    def body(owner_ref, *refs):
        vals = products(*refs[:5])
        for ref, val in zip(refs[5:8], vals):
            ref[0] = val.astype(BF16)

        @pl.when(pl.program_id(0) == owner_ref[0])
        def _():
            for ref, val in zip(refs[8:], vals):
                ref[0] = val

    mine = pl.BlockSpec((1, n, d), lambda j, *_: (0, 0, 0))
    outs = pl.pallas_call(
        body,
        grid_spec=pltpu.PrefetchScalarGridSpec(num_scalar_prefetch=1, grid=(n_shards,), in_specs=ins,
                                               out_specs=(out, out, out, mine, mine, mine)),
        out_shape=(jax.ShapeDtypeStruct((n_shards, n, d), BF16),) * 3 + (jax.ShapeDtypeStruct((1, n, d), F32),) * 3,
        name=name, compiler_params=_params(1),
    )(owner, h, dyh, a, dg, du)
    return outs[:3], outs[3:]


IN_PIECES = (("wq_a", 0, 768), ("wk_a", 768, 1536), ("wv_a", 1536, 2304), ("w_qkvb", 2304, 5376),
             ("w_small", 5376, 5392), ("w_ggate", 5392, 6416), ("w_gatea", 6416, 7440), ("w_gateb", 7440, 8464))
IN_NAMES = tuple(name for name, _, _ in IN_PIECES)


def _in_rows(lo, hi):
    return lo, max(hi, lo + LANES)


N_ROTATED = 2


def _in_proj_fwd(x, gain, wt, cos, sin):
    n_tokens, d = x.shape
    tm = FFN_ROW_TILE
    rows = [_in_rows(lo, hi) for _, lo, hi in IN_PIECES]

    def body(x_ref, gain_ref, wt_ref, cos_ref, sin_ref, *o_refs):
        xv = x_ref[...]
        h = (xv * lax.rsqrt(jnp.mean(xv * xv, axis=-1, keepdims=True) + EPS) * gain_ref[...]).astype(BF16)
        for k, ((lo, hi), o_ref) in enumerate(zip(rows, o_refs)):
            z = lax.dot_general(h, wt_ref[lo:hi, :], NT, preferred_element_type=F32)
            o_ref[...] = _rotate(z, cos_ref[...], sin_ref[...]) if k < N_ROTATED else z

    tab = pl.BlockSpec((tm, LANES), lambda i: (i, 0))
    return pl.pallas_call(
        body, grid=(n_tokens // tm,),
        in_specs=[pl.BlockSpec((tm, d), lambda i: (i, 0)), _resident(gain.shape), _resident(wt.shape), tab, tab],
        out_specs=tuple(pl.BlockSpec((tm, hi - lo), lambda i: (i, 0)) for lo, hi in rows),
        out_shape=tuple(jax.ShapeDtypeStruct((n_tokens, hi - lo), F32) for lo, hi in rows),
        name="in_proj_fwd", compiler_params=_params(1),
    )(x, gain, wt, cos, sin)


def _in_proj_bwd_rows(x, gain, dres, dzs, wt, cos, sin):
    n_tokens, d = x.shape
    tm = FFN_ROW_TILE
    n = len(dzs)
    rows = [_in_rows(lo, hi) for _, lo, hi in IN_PIECES]

    def body(x_ref, gain_ref, dres_ref, cos_ref, sin_ref, *refs):
        dz_refs, wt_ref = refs[:n], refs[n]
        dx_ref, dgain_ref, h_ref = refs[n + 1:n + 4]
        unrotated_refs = refs[n + 4:]
        xv, gain_v = x_ref[...], gain_ref[...]
        r = lax.rsqrt(jnp.mean(xv * xv, axis=-1, keepdims=True) + EPS)
        xhat = xv * r
        h_ref[...] = (xhat * gain_v).astype(BF16)
        dh = jnp.zeros((tm, d), F32)
        for k, (dz_ref, (lo, hi)) in enumerate(zip(dz_refs, rows)):
            dz = dz_ref[...]
            if k < N_ROTATED:
                dz = _rotate(dz, cos_ref[...], -sin_ref[...]).astype(BF16)
                unrotated_refs[k][...] = dz
            dh = dh + lax.dot_general(dz.astype(BF16), wt_ref[lo:hi, :], NN, preferred_element_type=F32)
        dxhat = dh * gain_v
        dx_ref[...] = dres_ref[...] + r * (dxhat - xhat * jnp.mean(dxhat * xhat, axis=-1, keepdims=True))

        @pl.when(pl.program_id(0) == 0)
        def _():
            dgain_ref[...] = jnp.zeros_like(dgain_ref)

        dgain_ref[...] += jnp.sum(dh * xhat, axis=0, keepdims=True)

    row = pl.BlockSpec((tm, d), lambda i: (i, 0))
    tab = pl.BlockSpec((tm, LANES), lambda i: (i, 0))
    dz_specs = [pl.BlockSpec((tm, dz.shape[1]), lambda i: (i, 0)) for dz in dzs]
    outs = pl.pallas_call(
        body, grid=(n_tokens // tm,),
        in_specs=[row, _resident(gain.shape), row, tab, tab] + dz_specs + [_resident(wt.shape)],
        out_specs=(row, pl.BlockSpec(gain.shape, lambda i: (0, 0)), row) + tuple(dz_specs[:N_ROTATED]),
        out_shape=(jax.ShapeDtypeStruct(x.shape, F32), jax.ShapeDtypeStruct(gain.shape, F32),
                   jax.ShapeDtypeStruct(x.shape, BF16))
        + tuple(jax.ShapeDtypeStruct(dz.shape, BF16) for dz in dzs[:N_ROTATED]),
        name="in_proj_bwd_rows", compiler_params=_params(1),
    )(x, gain, dres, cos, sin, *dzs, wt)
    return outs[0], outs[1], outs[2], outs[3:]


def _in_proj_bwd_weight(dwt, h, dz, lo, hi, name):
    n_tokens, d = h.shape
    width = hi - lo
    tn = _tile(width, 512) if width >= LANES else width
    dz_tile = max(tn, LANES)

    def body(dwt_ref, h_ref, dz_ref, o_ref):
        o_ref[...] = lax.dot_general(dz_ref[:, :tn].astype(BF16), h_ref[...], TN, preferred_element_type=F32)

    return pl.pallas_call(
        body, grid=(width // tn,),
        in_specs=[ANY, _resident(h.shape), pl.BlockSpec((n_tokens, dz_tile), lambda j: (0, j))],
        out_specs=pl.BlockSpec((pl.Element(tn), pl.Element(d)), lambda j: (pl.multiple_of(lo + j * tn, 16), 0)),
        out_shape=jax.ShapeDtypeStruct(dwt.shape, F32), input_output_aliases={0: 0}, name=name,
        compiler_params=_params(1),
    )(dwt, h, dz)


def _split_small(z):
    return z[:, :GDN_HEADS], z[:, GDN_HEADS:2 * GDN_HEADS]


def _heads3(q, k, v):
    return _to_heads(q), _to_heads(k), _to_heads(v)


def _tokens6(o, lse):
    return tuple(_from_heads(o)) + tuple(_from_heads(lse))


def mixer_forward(x1, w, small):
    n_tokens = x1.shape[0]
    cos, sin = _rope_tables(n_tokens)
    proj = dict(zip(IN_NAMES, _in_proj_fwd(x1, small["mix_norm"], w["w_in_t"], cos, sin)))
    (qh, kh, vh), heads_vjp = jax.vjp(_heads3, proj["wq_a"], proj["wk_a"], proj["wv_a"])
    o, lse = _attn_fwd(qh, kh, vh)
    per_group, tokens_vjp = jax.vjp(_tokens6, o, lse)
    ya = _rowwise_fwd(_combine_fn, "combine", per_group, (), (), 512, 1)[0]
    pa = _matmul(ya, w["w_branch_a"], name="branch_a")
    qkv = _conv_fwd(proj["w_qkvb"], small["gdn_conv_w"])
    raw, small_vjp = jax.vjp(_split_small, proj["w_small"])
    gdn_params = (small["gdn_a_log"], small["gdn_dt_bias"])
    beta, gcum = _rowwise_fwd(_beta_decay_fn, "beta_decay", raw, (), gdn_params, 512, 1)
    ob, states = _gdn_fwd(qkv, beta, gcum)
    gate_in = (ob, proj["w_ggate"])
    yb = _rowwise_fwd(_outnorm_gate_fn, "outnorm_gate", gate_in, (), (small["gdn_out_norm"],), 512, GDN_HEADS)[0]
    pb = _matmul(yb, w["w_branch_b"], name="branch_b")
    merge_in = (proj["w_gatea"], proj["w_gateb"], pa, pb)
    merged = _rowwise_fwd(_merge_fn, "merge", merge_in, (), (), 256, 1)[0]
    x2 = _matmul(merged, w["w_out"], name="out", res=x1)
    saved = dict(x1=x1, proj=proj, cos=cos, sin=sin, heads_vjp=heads_vjp, heads=(qh, kh, vh), tokens_vjp=tokens_vjp,
                 per_group=per_group, ya=ya, qkv=qkv, raw=raw, small_vjp=small_vjp, beta=beta, gcum=gcum, states=states,
                 gate_in=gate_in, yb=yb, merge_in=merge_in, merged=merged)
    return x2, saved


def mixer_backward(dx2, s, w, small):
    proj = s["proj"]
    dmerged = _matmul(dx2, w["w_out"], name="out_da", tb=True)
    grads = dict(w_out=_matmul(s["merged"], dx2, name="out_dw", ta=True))
    dgate_a, dgate_b, dpa, dpb = _rowwise_bwd(_merge_fn, "merge_bwd", s["merge_in"], (), (), (dmerged,), 256, 1)
    dyb = _matmul(dpb, w["w_branch_b"], name="branch_b_da", tb=True)
    grads["w_branch_b"] = _matmul(s["yb"], dpb, name="branch_b_dw", ta=True)
    dya = _matmul(dpa, w["w_branch_a"], name="branch_a_da", tb=True)
    grads["w_branch_a"] = _matmul(s["ya"], dpa, name="branch_a_dw", ta=True)
    dob, dggate, grads["gdn_out_norm"] = _rowwise_bwd(
        _outnorm_gate_fn, "outnorm_gate_bwd", s["gate_in"], (), (small["gdn_out_norm"],), (dyb,), 512, GDN_HEADS)
    dqkv, dbeta, dgcum = _gdn_bwd(s["qkv"], s["beta"], s["gcum"], s["states"], dob)
    gdn_params = (small["gdn_a_log"], small["gdn_dt_bias"])
    dbeta_raw, ddecay_raw, grads["gdn_a_log"], grads["gdn_dt_bias"] = _rowwise_bwd(
        _beta_decay_fn, "beta_decay_bwd", s["raw"], (), gdn_params, (dbeta, dgcum), 512, 1)
    dsmall = s["small_vjp"]((dbeta_raw, ddecay_raw))[0]
    dqkvb, grads["gdn_conv_w"] = _conv_bwd(proj["w_qkvb"], small["gdn_conv_w"], dqkv)
    dper_group = _rowwise_bwd(_combine_fn, "combine_bwd", s["per_group"], (), (), (dya,), 512, 1)
    do, dlse = s["tokens_vjp"](tuple(dper_group))
    dqh, dkh, dvh = _attn_bwd(*s["heads"], do, dlse)
    dq_rot, dk_rot, dv = s["heads_vjp"]((dqh, dkh, dvh))
    dzs = (dq_rot, dk_rot, dv, dqkvb, dsmall, dggate, dgate_a, dgate_b)
    dx1, grads["mix_norm"], h, unrotated = _in_proj_bwd_rows(
        s["x1"], small["mix_norm"], dx2, dzs, w["w_in_t"], s["cos"], s["sin"])
    dzs = tuple(unrotated) + dzs[N_ROTATED:]
    dwt = lax.empty(w["w_in_t"].shape, F32)
    for (name, lo, hi), dz in zip(IN_PIECES, dzs):
        dwt = _in_proj_bwd_weight(dwt, h, dz, lo, hi, "in_proj_dw_" + name)
    grads["w_in_t"] = dwt
    return dx1, grads


def ffn_forward(x, gain, w, tag):
    out, g, u = _ffn_fwd(x, gain, w[tag + "_w_gate"], w[tag + "_w_up"], w[tag + "_w_down"], tag + "_fwd")
    return out, (x, g, u)


def ffn_backward(dy, saved, gain, w, tag, owner=None):
    x, g, u = saved
    weights = (w[tag + "_w_gate"], w[tag + "_w_up"], w[tag + "_w_down"])
    dx, dgain, h, dyh, a, dg, du = _ffn_bwd_rows(x, gain, dy, g, u, *weights, tag + "_bwd_rows")
    return dx, dgain, _ffn_bwd_weights(h, dyh, a, dg, du, tag + "_bwd_weights", owner)


def loss_head(x3, target, gain):
    row_loss = _rowwise_fwd(_loss_fn, "loss", (x3,), (target,), (gain,), 256, 1)[0]
    dx3, dgain = _rowwise_bwd(_loss_fn, "loss_bwd", (x3,), (target,), (gain,), (jnp.ones_like(row_loss),), 256, 1)
    return jnp.sum(row_loss), dx3, dgain


BIG_WEIGHTS = ("ffn1_w_gate", "ffn1_w_up", "ffn1_w_down", "w_in", "w_branch_a", "w_branch_b", "w_out",
               "ffn2_w_gate", "ffn2_w_up", "ffn2_w_down")
TRANSPOSED = ("ffn1_w_gate", "ffn1_w_up", "w_in", "ffn2_w_gate", "ffn2_w_up")
CONV_SHARD = (GDN_CONV, 3 * GDN_WIDTH // N_DEV)
SMALL_ROWS = 24
ANY = pl.BlockSpec(memory_space=pl.ANY)


TOKEN = jax.ShapeDtypeStruct((8, LANES), F32)


def _after(value, token):
    return value + token[0, 0].astype(value.dtype)


def _position():
    return lax.axis_index("x"), lax.axis_index("y"), lax.axis_index("c")


def all_gather_shards(shards, name):
    n = len(shards)

    def body(*refs):
        x_refs, out_refs = refs[:n], refs[n:2 * n]
        send_sems, recv_sems, local_sems = refs[2 * n + 1:]
        x, y, c = _position()
        me, sibling = (x, y, c), (x, y, 1 - c)
        chips = [(1 - x, y), (x, 1 - y), (1 - x, 1 - y)]

        def slab(a, px, py, pc):
            return out_refs[a].at[4 * px + 2 * py + pc]

        def copy(a, k, block, to, src=None):
            return pltpu.make_async_remote_copy(
                src_ref=slab(a, *block) if src is None else src, dst_ref=slab(a, *block),
                send_sem=send_sems.at[7 * a + k], recv_sem=recv_sems.at[7 * a + k], device_id=to, device_id_type=MESH)

        mine = [pltpu.make_async_copy(x_refs[a], slab(a, *me), local_sems.at[a]) for a in range(n)]
        for cp in mine:
            cp.start()
        first = []
        for j, chip in enumerate(chips):
            first += [copy(a, 1 + j, me, (*chip, c), src=x_refs[a]) for a in range(n)]
        first += [copy(a, 0, me, sibling, src=x_refs[a]) for a in range(n)]
        for cp in first:
            cp.start()
        passed = []
        for j, chip in enumerate(chips):
            for a in range(n):
                copy(a, 1 + j, (*chip, c), me).wait_recv()
                cp = copy(a, 4 + j, (*chip, c), sibling)
                cp.start()
                passed.append(cp)
        for a in range(n):
            copy(a, 0, sibling, me).wait_recv()
        for j, chip in enumerate(chips):
            for a in range(n):
                copy(a, 4 + j, (*chip, 1 - c), me).wait_recv()
        for cp in first + passed:
            cp.wait_send()
        for cp in mine:
            cp.wait()
        refs[2 * n][...] = jnp.zeros_like(refs[2 * n])

    outs = pl.pallas_call(
        body, out_shape=tuple(jax.ShapeDtypeStruct((N_DEV,) + s.shape, s.dtype) for s in shards) + (TOKEN,),
        in_specs=[ANY] * n, out_specs=(ANY,) * n + (pl.BlockSpec(memory_space=pltpu.VMEM),),
        scratch_shapes=[pltpu.SemaphoreType.DMA((7 * n,)), pltpu.SemaphoreType.DMA((7 * n,)),
                        pltpu.SemaphoreType.DMA((n,))],
        name=name,
    )(*shards)
    return outs[:n], outs[n]


def exchange_with_sibling(grads):
    n = len(grads)

    def body(*refs):
        g_refs, recv_refs = refs[:n], refs[n:2 * n]
        send_sems, recv_sems = refs[2 * n:]
        x, y, c = _position()
        copies = [pltpu.make_async_remote_copy(
            src_ref=g_refs[a].at[2 * k + 1 - c], dst_ref=recv_refs[a].at[k], send_sem=send_sems.at[4 * a + k],
            recv_sem=recv_sems.at[4 * a + k], device_id=(x, y, 1 - c), device_id_type=MESH)
            for k in range(4) for a in range(n)]
        for cp in copies:
            cp.start()
        for cp in copies:
            cp.wait()

    return pl.pallas_call(
        body, out_shape=tuple(jax.ShapeDtypeStruct((4,) + g.shape[1:], g.dtype) for g in grads),
        in_specs=[ANY] * n, out_specs=(ANY,) * n,
        scratch_shapes=[pltpu.SemaphoreType.DMA((4 * n,)), pltpu.SemaphoreType.DMA((4 * n,))], name="rs_sibling",
    )(*grads)


ELEMENTWISE_TILE_BYTES = 1536 * 1024


def _tile2(rows, cols):
    if rows % 256 == 0:
        return 256, cols
    if rows * cols * 4 > ELEMENTWISE_TILE_BYTES and cols % 256 == 0:
        return rows, 256
    return rows, cols


def add_sibling(grads, received, core, name):
    _, rows, width = grads.shape
    tr, tc = _tile2(rows, width)

    def body(c_ref, g_ref, r_ref, o_ref):
        o_ref[...] = (g_ref[...] + r_ref[...]).astype(BF16)

    blk = (1, tr, tc)
    return pl.pallas_call(
        body,
        grid_spec=pltpu.PrefetchScalarGridSpec(
            num_scalar_prefetch=1, grid=(4, rows // tr, width // tc),
            in_specs=[pl.BlockSpec(blk, lambda k, i, j, c_ref: (2 * k + c_ref[0], i, j)),
                      pl.BlockSpec(blk, lambda k, i, j, c_ref: (k, i, j))],
            out_specs=pl.BlockSpec(blk, lambda k, i, j, c_ref: (k, i, j))),
        out_shape=jax.ShapeDtypeStruct((4, rows, width), BF16), name=name, compiler_params=_params(3),
    )(core, grads, received)


HBM = pl.BlockSpec(memory_space=pltpu.HBM)
SEM = pl.BlockSpec(memory_space=pltpu.SEMAPHORE)
DATAFLOW_EFFECT = pltpu.SideEffectType.DATAFLOW_SIDE_EFFECTING
N_PEERS = N_DEV - 1


def _peer(mask):
    x, y, c = _position()
    px = 1 - x if mask & 4 else x
    py = 1 - y if mask & 2 else y
    pc = 1 - c if mask & 1 else c
    return (px, py, pc), 4 * px + 2 * py + pc


ALL_PEERS = tuple(range(1, N_DEV))
OTHER_CHIPS = (4, 2, 6)


SIBLING = 1
GATHER_MODES = ("gather", "near")


def _exchange_peers(mode):
    return {"chips": OTHER_CHIPS, "near": (SIBLING,) + OTHER_CHIPS}.get(mode, ALL_PEERS)


def _direct_copies(src_refs, land_refs, send_sems, recv_sems, mode):
    x, y, c = _position()
    me = 4 * x + 2 * y + c
    masks = _exchange_peers(mode)
    copies = []
    for a, (src, land) in enumerate(zip(src_refs, land_refs)):
        for slot, mask in enumerate(masks):
            peer, peer_index = _peer(mask)
            k = len(masks) * a + slot
            if mode in GATHER_MODES:
                source, dest = src, land.at[me]
            elif mode == "scatter":
                source, dest = src.at[peer_index], land.at[slot]
            else:
                source, dest = src.at[2 * peer[0] + peer[1]], land.at[slot]
            copies.append(pltpu.make_async_remote_copy(
                src_ref=source, dst_ref=dest, send_sem=send_sems.at[k], recv_sem=recv_sems.at[k], device_id=peer,
                device_id_type=MESH))
    return copies


def forward_to_sibling(slabs, name):
    n = len(slabs)

    def body(*refs):
        out_refs = refs[n:2 * n]
        send_sems, recv_sems = refs[2 * n + 1:]
        x, y, c = _position()
        copies = []
        for a in range(n):
            for slot, mask in enumerate(OTHER_CHIPS):
                _, held = _peer(mask)
                copies.append(pltpu.make_async_remote_copy(
                    src_ref=out_refs[a].at[held], dst_ref=out_refs[a].at[held], send_sem=send_sems.at[3 * a + slot],
                    recv_sem=recv_sems.at[3 * a + slot], device_id=(x, y, 1 - c), device_id_type=MESH))
        for cp in copies:
            cp.start()
        for cp in copies:
            cp.wait()
        refs[2 * n][...] = jnp.zeros_like(refs[2 * n])

    outs = pl.pallas_call(
        body, out_shape=tuple(jax.ShapeDtypeStruct(s.shape, s.dtype) for s in slabs) + (TOKEN,),
        in_specs=[ANY] * n, out_specs=(ANY,) * n + (pl.BlockSpec(memory_space=pltpu.VMEM),),
        input_output_aliases={i: i for i in range(n)},
        scratch_shapes=[pltpu.SemaphoreType.DMA((3 * n,)), pltpu.SemaphoreType.DMA((3 * n,))], name=name,
    )(*slabs)
    return outs[:n], outs[n]


def direct_exchange_start(arrays, mode, name):
    n = len(arrays)
    n_peers = len(_exchange_peers(mode))
    lands = [lax.empty((N_DEV,) + a.shape if mode in GATHER_MODES else (n_peers,) + a.shape[1:], a.dtype)
             for a in arrays]

    def body(*refs):
        src_refs, land_refs = refs[:n], refs[n:2 * n]
        send_sems, recv_sems = refs[2 * n], refs[2 * n + 1]
        token = refs[-1]
        for cp in _direct_copies(src_refs, land_refs, send_sems, recv_sems, mode):
            cp.start()
        token[...] = jnp.zeros_like(token)

    sems = pltpu.SemaphoreType.DMA((n_peers * n,))
    outs = pl.pallas_call(
        body, name=name,
        out_shape=(sems, sems) + tuple(pltpu.HBM(a.shape, a.dtype) for a in arrays)
        + tuple(pltpu.HBM(l.shape, l.dtype) for l in lands) + (TOKEN,),
        in_specs=[HBM] * (2 * n), out_specs=(SEM, SEM) + (HBM,) * (2 * n) + (pl.BlockSpec(memory_space=pltpu.VMEM),),
        input_output_aliases={i: 2 + i for i in range(2 * n)},
        compiler_params=pltpu.CompilerParams(has_side_effects=DATAFLOW_EFFECT),
    )(*[pltpu.with_memory_space_constraint(a, pltpu.HBM) for a in list(arrays) + lands])
    return outs[0], outs[1], outs[2:2 + n], outs[2 + n:2 + 2 * n], outs[-1]


def direct_exchange_wait(send_sems, recv_sems, arrays, lands, after, mode, name):
    n = len(arrays)

    def body(*refs):
        src_refs, land_refs = refs[:n], refs[n:2 * n]
        send_sems, recv_sems = refs[2 * n], refs[2 * n + 1]
        for cp in _direct_copies(src_refs, land_refs, send_sems, recv_sems, mode):
            cp.wait_send()
            cp.wait_recv()
        refs[-1][...] = jnp.zeros_like(refs[-1])

    outs = pl.pallas_call(
        body, name=name,
        out_shape=tuple(pltpu.HBM(a.shape, a.dtype) for a in arrays) + tuple(pltpu.HBM(l.shape, l.dtype) for l in lands)
        + (TOKEN,),
        in_specs=[HBM] * (2 * n) + [SEM, SEM, pl.BlockSpec(memory_space=pl.ANY)],
        out_specs=(HBM,) * (2 * n) + (pl.BlockSpec(memory_space=pltpu.VMEM),),
        input_output_aliases={i: i for i in range(2 * n)},
        compiler_params=pltpu.CompilerParams(has_side_effects=DATAFLOW_EFFECT),
    )(*arrays, *lands, send_sems, recv_sems, after)
    return outs[n:]


def adamw_direct(w, m, v, own, received, name):
    row_per_tile = w.shape[0] != 1
    rows, cols = (w.shape[0], w.shape[2]) if row_per_tile else w.shape[-2:]
    tr, tc = _tile2(rows, cols)

    def body(w_ref, m_ref, v_ref, own_ref, r_ref, g_ref, d_ref, nm_ref, nv_ref):
        gv = own_ref[0]
        for j in range(N_PEERS):
            gv = gv + r_ref[j].astype(F32)
        nm = ADAM_B1 * m_ref[...] + (1.0 - ADAM_B1) * gv
        nv = ADAM_B2 * v_ref[...] + (1.0 - ADAM_B2) * (gv * gv)
        m_hat = nm / (1.0 - ADAM_B1 ** ADAM_STEP)
        v_hat = nv / (1.0 - ADAM_B2 ** ADAM_STEP)
        g_ref[...] = gv
        d_ref[...] = -ADAM_LR * (m_hat / (jnp.sqrt(v_hat) + ADAM_EPS) + ADAM_WD * w_ref[...])
        nm_ref[...] = nm
        nv_ref[...] = nv

    if row_per_tile:
        one = pl.BlockSpec((tr, None, tc), lambda i, j: (i, 0, j))
    else:
        one = pl.BlockSpec((None, tr, tc), lambda i, j: (0, i, j))
    out = jax.ShapeDtypeStruct(w.shape, F32)
    return pl.pallas_call(
        body, grid=(rows // tr, cols // tc),
        in_specs=[one, one, one, pl.BlockSpec((1, tr, tc), lambda i, j: (0, i, j)),
                  pl.BlockSpec((N_PEERS, tr, tc), lambda i, j: (0, i, j))],
        out_specs=(one,) * 4, out_shape=(out,) * 4, name=name, compiler_params=_params(2),
    )(w, m, v, own, received)


def all_reduce_small(vals):
    rows, width = vals.shape

    def body(x_ref, out_ref, all_ref, send_sems, recv_sems):
        x, y, c = _position()
        me, sibling = (x, y, c), (x, y, 1 - c)
        chips = [(1 - x, y), (x, 1 - y), (1 - x, 1 - y)]

        def slab(px, py, pc):
            return all_ref.at[4 * px + 2 * py + pc]

        def copy(k, block, to, src=None):
            return pltpu.make_async_remote_copy(
                src_ref=slab(*block) if src is None else src, dst_ref=slab(*block),
                send_sem=send_sems.at[k], recv_sem=recv_sems.at[k], device_id=to, device_id_type=MESH)

        first = [copy(0, me, sibling, src=x_ref)]
        first += [copy(1 + j, me, (*chip, c), src=x_ref) for j, chip in enumerate(chips)]
        for cp in first:
            cp.start()
        all_ref[4 * x + 2 * y + c] = x_ref[...]
        passed = [copy(4 + j, (*chip, c), sibling) for j, chip in enumerate(chips)]
        for j, chip in enumerate(chips):
            copy(1 + j, (*chip, c), me).wait_recv()
            passed[j].start()
        copy(0, sibling, me).wait_recv()
        for j, chip in enumerate(chips):
            copy(4 + j, (*chip, 1 - c), me).wait_recv()
        for cp in first + passed:
            cp.wait_send()
        total = all_ref[0]
        for d in range(1, N_DEV):
            total = total + all_ref[d]
        out_ref[...] = total

    vmem = pl.BlockSpec(memory_space=pltpu.VMEM)
    return pl.pallas_call(
        body, out_shape=(jax.ShapeDtypeStruct(vals.shape, F32), jax.ShapeDtypeStruct((N_DEV, rows, width), F32)),
        in_specs=[vmem], out_specs=(vmem, vmem),
        scratch_shapes=[pltpu.SemaphoreType.DMA((7,)), pltpu.SemaphoreType.DMA((7,))], name="small_allreduce",
    )(vals)[0]


def adamw(w, g, m, v, name):
    shape = w.shape
    w2, g2, m2, v2 = [a.reshape((-1, shape[-1])) for a in (w, g, m, v)]
    rows, cols = w2.shape
    tr = 256 if rows % 256 == 0 else rows

    def body(w_ref, g_ref, m_ref, v_ref, d_ref, nm_ref, nv_ref):
        gv = g_ref[...]
        nm = ADAM_B1 * m_ref[...] + (1.0 - ADAM_B1) * gv
        nv = ADAM_B2 * v_ref[...] + (1.0 - ADAM_B2) * (gv * gv)
        m_hat = nm / (1.0 - ADAM_B1 ** ADAM_STEP)
        v_hat = nv / (1.0 - ADAM_B2 ** ADAM_STEP)
        d_ref[...] = -ADAM_LR * (m_hat / (jnp.sqrt(v_hat) + ADAM_EPS) + ADAM_WD * w_ref[...])
        nm_ref[...] = nm
        nv_ref[...] = nv

    blk = pl.BlockSpec((tr, cols), lambda i: (i, 0))
    out = jax.ShapeDtypeStruct((rows, cols), F32)
    outs = pl.pallas_call(
        body, grid=(rows // tr,), in_specs=[blk] * 4, out_specs=(blk,) * 3, out_shape=(out,) * 3,
        name=name, compiler_params=_params(1),
    )(w2, g2, m2, v2)
    return tuple(o.reshape(shape) for o in outs)


def adamw_summed(w, m, v, grads, from_sibling, received, me, name):
    rows, cols = w.shape[-2:]
    tr, tc = _tile2(rows, cols)

    def body(me_ref, w_ref, m_ref, v_ref, own_ref, sib_ref, r_ref, g_ref, d_ref, nm_ref, nv_ref):
        gv = own_ref[0] + sib_ref[0]
        for j in range(3):
            gv = gv + r_ref[j].astype(F32)
        nm = ADAM_B1 * m_ref[0] + (1.0 - ADAM_B1) * gv
        nv = ADAM_B2 * v_ref[0] + (1.0 - ADAM_B2) * (gv * gv)
        m_hat = nm / (1.0 - ADAM_B1 ** ADAM_STEP)
        v_hat = nv / (1.0 - ADAM_B2 ** ADAM_STEP)
        g_ref[0] = gv
        d_ref[0] = -ADAM_LR * (m_hat / (jnp.sqrt(v_hat) + ADAM_EPS) + ADAM_WD * w_ref[0])
        nm_ref[0] = nm
        nv_ref[0] = nv

    one = pl.BlockSpec((1, tr, tc), lambda i, j, me_ref: (0, i, j))
    out = jax.ShapeDtypeStruct((1, rows, cols), F32)
    return pl.pallas_call(
        body,
        grid_spec=pltpu.PrefetchScalarGridSpec(
            num_scalar_prefetch=1, grid=(rows // tr, cols // tc),
            in_specs=[one, one, one, pl.BlockSpec((1, tr, tc), lambda i, j, me_ref: (me_ref[0], i, j)),
                      pl.BlockSpec((1, tr, tc), lambda i, j, me_ref: (me_ref[1], i, j)),
                      pl.BlockSpec((3, tr, tc), lambda i, j, me_ref: (0, i, j))],
            out_specs=(one,) * 4),
        out_shape=(out,) * 4, name=name, compiler_params=_params(2),
    )(me, w, m, v, grads, from_sibling, received)


SMALL_VECTORS = ("ffn1_norm", "mix_norm", "ffn2_norm", "final_norm")


def _pack_small(gs):
    row = jnp.concatenate([gs["gdn_a_log"].reshape(-1), gs["gdn_dt_bias"].reshape(-1), gs["gdn_out_norm"].reshape(-1)])
    rows = [gs[n].reshape(1, D_MODEL) for n in SMALL_VECTORS]
    rows.append(jnp.pad(row, (0, D_MODEL - row.shape[0])).reshape(1, D_MODEL))
    rows.append(gs["gdn_conv_w"].reshape(-1, D_MODEL))
    packed = jnp.concatenate(rows, axis=0)
    return jnp.pad(packed, ((0, SMALL_ROWS - packed.shape[0]), (0, 0)))


def _unpack_small(packed):
    out = {n: packed[i].reshape(1, D_MODEL) for i, n in enumerate(SMALL_VECTORS)}
    row = packed[len(SMALL_VECTORS)]
    out["gdn_a_log"] = row[:GDN_HEADS].reshape(1, GDN_HEADS)
    out["gdn_dt_bias"] = row[GDN_HEADS:2 * GDN_HEADS].reshape(1, GDN_HEADS)
    out["gdn_out_norm"] = row[2 * GDN_HEADS:2 * GDN_HEADS + GDN_HEAD_DIM].reshape(1, GDN_HEAD_DIM)
    first = len(SMALL_VECTORS) + 1
    out["gdn_conv_w"] = packed[first:first + GDN_CONV * 3].reshape(GDN_CONV, 3 * GDN_WIDTH)
    return out


WEIGHTS = ("ffn1_norm", "ffn1_w_gate", "ffn1_w_up", "ffn1_w_down", "mix_norm", "w_in", "gdn_conv_w", "gdn_a_log",
           "gdn_dt_bias", "gdn_out_norm", "w_branch_a", "w_branch_b", "w_out", "ffn2_norm", "ffn2_w_gate",
           "ffn2_w_up", "ffn2_w_down", "final_norm")


def kernel(x, ffn1_norm, ffn1_w_gate, ffn1_w_up, ffn1_w_down, mix_norm, w_in, gdn_conv_w, gdn_a_log, gdn_dt_bias, gdn_out_norm, w_branch_a, w_branch_b, w_out, ffn2_norm, ffn2_w_gate, ffn2_w_up, ffn2_w_down, final_norm, loss_target, m_ffn1_norm, m_ffn1_w_gate, m_ffn1_w_up, m_ffn1_w_down, m_mix_norm, m_w_in, m_gdn_conv_w, m_gdn_a_log, m_gdn_dt_bias, m_gdn_out_norm, m_w_branch_a, m_w_branch_b, m_w_out, m_ffn2_norm, m_ffn2_w_gate, m_ffn2_w_up, m_ffn2_w_down, m_final_norm, v_ffn1_norm, v_ffn1_w_gate, v_ffn1_w_up, v_ffn1_w_down, v_mix_norm, v_w_in, v_gdn_conv_w, v_gdn_a_log, v_gdn_dt_bias, v_gdn_out_norm, v_w_branch_a, v_w_branch_b, v_w_out, v_ffn2_norm, v_ffn2_w_gate, v_ffn2_w_up, v_ffn2_w_down, v_final_norm):
    given = dict(locals())
    px, py, pc = _position()
    big_names = list(BIG_WEIGHTS)

    def shard_view(a, n):
        if n == "w_in":
            return a.transpose(2, 0, 1)
        return a.transpose(0, 2, 1) if n in TRANSPOSED else a

    def shard_unview(a, n):
        if n == "w_in":
            return a.transpose(1, 2, 0)
        return a.transpose(0, 2, 1) if n in TRANSPOSED else a

    me = 4 * px + 2 * py + pc
    me_index = me.astype(jnp.int32).reshape(1)
    late = [n for n in big_names if n.startswith("ffn2")]
    early = [n for n in big_names if n not in late]
    shards = {n: shard_view(given[n], n).reshape(given[n].shape[-1 if n in TRANSPOSED else -2], -1).astype(BF16)
              for n in big_names}
    first = [n for n in early if n.startswith("ffn1")]
    middle = [n for n in early if n not in first]
    first_slabs, first_done = all_gather_shards([shards[n] for n in first], "gather_ffn1")
    shards["gdn_conv_w"] = gdn_conv_w[0]
    middle_all = middle + ["gdn_conv_w"]
    middle_gather = direct_exchange_start([_after(shards[n], first_done) for n in middle_all], "near",
                                          "gather_mixer_start")
    ffn1_norm = _after(ffn1_norm, middle_gather[4])
    w = dict(zip(first, first_slabs))
    x1, ffn1_saved = ffn_forward(x[0], ffn1_norm, w, "ffn1")
    near_lands = direct_exchange_wait(*middle_gather[:4], x1, "near", "gather_mixer_wait")[:-1]
    near_lands = [lax.dynamic_update_slice(land, shards[n][None], (me, 0, 0)) for n, land in zip(middle_all, near_lands)]
    middle_slabs, middle_done = forward_to_sibling(near_lands, "gather_mixer_forward")
    gathered = dict(zip(middle_all, middle_slabs))
    late_gather = direct_exchange_start([_after(shards[n], middle_done) for n in late], "gather", "gather_ffn2_start")
    w["w_in_t"] = gathered["w_in"].reshape(-1, D_MODEL)
    w["w_branch_a"] = gathered["w_branch_a"].transpose(1, 0, 2).reshape(256, D_MODEL)
    w["w_branch_b"] = gathered["w_branch_b"].reshape(D_MODEL, D_MODEL)
    w["w_out"] = gathered["w_out"].reshape(D_MODEL, D_MODEL)
    conv_full = gathered["gdn_conv_w"].transpose(1, 0, 2).reshape(GDN_CONV, 3 * GDN_WIDTH)
    small = dict(mix_norm=_after(mix_norm, late_gather[4]), gdn_a_log=gdn_a_log, gdn_dt_bias=gdn_dt_bias,
                 gdn_out_norm=gdn_out_norm, gdn_conv_w=conv_full)

    x2, mixer_saved = mixer_forward(x1, w, small)
    late_lands = direct_exchange_wait(*late_gather[:4], x2, "gather", "gather_ffn2_wait")
    for n, land in zip(late, late_lands):
        w[n] = lax.dynamic_update_slice(land, shards[n][None], (me, 0, 0))
    x3, ffn2_saved = ffn_forward(x2, ffn2_norm, w, "ffn2")
    loss_local, dx3, g_final = loss_head(x3, loss_target[0], final_norm.reshape(1, D_MODEL))
    loss = lax.psum(loss_local, ("x", "y", "c"))
    dx2, g_ffn2_norm, (dw2, dw2_own) = ffn_backward(dx3, ffn2_saved, ffn2_norm, w, "ffn2", me_index)
    late_scatter = direct_exchange_start(list(dw2), "scatter", "rs_ffn2_start")
    w_after = dict(w, w_out=_after(w["w_out"], late_scatter[4]))
    dx1, g_w = mixer_backward(dx2, mixer_saved, w_after, small)
    middle = ["w_in", "w_branch_a", "w_branch_b", "w_out"]
    g_big = dict(w_in=g_w["w_in_t"].reshape(N_DEV, -1, D_MODEL),
                 w_branch_a=g_w["w_branch_a"].reshape(256, N_DEV, 128).transpose(1, 0, 2),
                 w_branch_b=g_w["w_branch_b"].reshape(N_DEV, 128, D_MODEL),
                 w_out=g_w["w_out"].reshape(N_DEV, 128, D_MODEL))
    own = dict(zip(late, dw2_own))
    own.update({n: lax.dynamic_index_in_dim(g_big[n], me, 0, keepdims=True) for n in middle[1:]})
    in_rows = g_w["w_in_t"].shape[0] // N_DEV
    own["w_in"] = lax.dynamic_slice(g_w["w_in_t"], (me * in_rows, 0), (in_rows, D_MODEL))[None]
    middle_scatter = direct_exchange_start([g_big[n].astype(BF16) for n in middle], "scatter", "rs_mixer_start")
    grad_x, g_ffn1_norm, dw1 = ffn_backward(dx1, ffn1_saved, _after(ffn1_norm, middle_scatter[4]), w, "ffn1")
    g_small = dict(ffn1_norm=g_ffn1_norm, ffn2_norm=g_ffn2_norm, final_norm=g_final,
                   **{n: g_w[n] for n in ("mix_norm", "gdn_a_log", "gdn_dt_bias", "gdn_out_norm", "gdn_conv_w")})

    first = [n for n in early if n.startswith("ffn1")]
    g_list = list(dw1)
    core = pc.astype(jnp.int32).reshape(1)
    me_and_chip = jnp.stack([me, 2 * px + py]).astype(jnp.int32)
    from_sibling = exchange_with_sibling(g_list)
    partials = [add_sibling(g, r, core, "rs_add_" + n) for n, g, r in zip(first, g_list, from_sibling)]
    first_chips = direct_exchange_start(partials, "chips", "rs_ffn1_start")

    def state_of(n):
        return [shard_view(given[p + n], n) for p in ("", "m_", "v_")]

    results = {}
    late_received = direct_exchange_wait(*late_scatter[:4], first_chips[4], "scatter", "rs_ffn2_wait")
    middle_received = direct_exchange_wait(*middle_scatter[:4], first_chips[4], "scatter", "rs_mixer_wait")
    for n, recv in zip(late + middle, list(late_received[:-1]) + list(middle_received[:-1])):
        outs = adamw_direct(*state_of(n), own[n], recv, "adamw_" + n)
        results[n] = tuple(shard_unview(o, n) for o in outs)

    done = results["w_out"][1]
    from_chips = direct_exchange_wait(*first_chips[:4], done, "chips", "rs_ffn1_wait")
    for n, g, sib, recv in zip(first, g_list, from_sibling, from_chips):
        outs = adamw_summed(*state_of(n), g, sib, recv, me_and_chip, "adamw_" + n)
        results[n] = tuple(shard_unview(o, n) for o in outs)

    small_sum = _unpack_small(all_reduce_small(_after(_pack_small(g_small), from_chips[-1])))
    conv_cols = CONV_SHARD[1]
    small_sum["gdn_conv_w"] = lax.dynamic_slice(small_sum["gdn_conv_w"], (0, me * conv_cols), (GDN_CONV, conv_cols))
    for n in WEIGHTS:
        if n not in results:
            g = small_sum[n].reshape(given[n].shape)
            results[n] = (g,) + adamw(given[n], g, given["m_" + n], given["v_" + n], "adamw_" + n)

    outs = [[results[n][i] for n in WEIGHTS] for i in range(4)]
    return (loss, grad_x[None], *outs[0], *outs[1], *outs[2], *outs[3])
```

```python
import jax
import jax.numpy as jnp
from jax import lax
from jax.experimental import pallas as pl
from jax.experimental.pallas import tpu as pltpu

F32 = jnp.float32
BF16 = jnp.bfloat16
HI = lax.Precision.HIGHEST
MESH = pl.DeviceIdType.MESH

N_DEV = 8
D_MODEL = 1024
EPS = 1e-6
ROPE_THETA = 10000.0
DSW_DILATIONS = (1, 4, 16)
DSW_HEADS_PER_GROUP = 4
DSW_HEAD_DIM = 64
DSW_BLOCK = 128
GDN_HEADS = 8
GDN_HEAD_DIM = 128
GDN_WIDTH = 1024
GDN_CONV = 4
GDN_CHUNK = 64

ADAM_LR = 0.001
ADAM_B1 = 0.9
ADAM_B2 = 0.999
ADAM_EPS = 1e-08
ADAM_WD = 0.01
ADAM_STEP = 10

VMEM_LIMIT_BYTES = 56 * 1024 * 1024
LANES = 128

NN = (((1,), (0,)), ((), ()))
NT = (((1,), (1,)), ((), ()))
TN = (((0,), (0,)), ((), ()))


def _params(n_grid):
    return pltpu.CompilerParams(dimension_semantics=("arbitrary",) * n_grid, vmem_limit_bytes=VMEM_LIMIT_BYTES)


def _tile(n, pref):
    best = None
    t = LANES
    while t <= min(n, pref):
        if n % t == 0:
            best = t
        t += LANES
    return n if best is None else best


def _matmul(a, b, *, name, ta=False, tb=False, res=None, scale=1.0):
    K, M = a.shape if ta else a.shape[::-1]
    N = b.shape[0] if tb else b.shape[1]
    assert (b.shape[1] if tb else b.shape[0]) == K, (a.shape, b.shape, ta, tb)
    tm = _tile(M, 512)
    tn = _tile(N, 512)
    dn = (((0 if ta else 1,), (1 if tb else 0,)), ((), ()))

    def body(*refs):
        a_ref, b_ref = refs[:2]
        o_ref = refs[-1]
        acc = lax.dot_general(a_ref[...].astype(BF16), b_ref[...].astype(BF16), dn, preferred_element_type=F32)
        if scale != 1.0:
            acc = acc * scale
        if res is not None:
            acc = refs[2][...] + acc
        o_ref[...] = acc

    a_spec = pl.BlockSpec((K, tm), lambda i, j: (0, i)) if ta else pl.BlockSpec((tm, K), lambda i, j: (i, 0))
    b_spec = pl.BlockSpec((tn, K), lambda i, j: (j, 0)) if tb else pl.BlockSpec((K, tn), lambda i, j: (0, j))
    o_spec = pl.BlockSpec((tm, tn), lambda i, j: (i, j))
    ins, specs = [a, b], [a_spec, b_spec]
    if res is not None:
        ins.append(res)
        specs.append(o_spec)
    return pl.pallas_call(
        body, grid=(M // tm, N // tn), in_specs=specs, out_specs=o_spec,
        out_shape=jax.ShapeDtypeStruct((M, N), F32), name=name, compiler_params=_params(2),
    )(*ins)


def _rw_specs(arrs, tm, nblk):
    return [pl.BlockSpec((tm, a.shape[1] // nblk), lambda i, j: (i, j)) for a in arrs]


def _rowwise_fwd(fn, name, rows, consts, params, tm, nblk):
    n_rows = rows[0].shape[0]
    tm = min(tm, n_rows)
    ins = list(rows) + list(consts)
    avals = [jax.ShapeDtypeStruct((tm, a.shape[1] // nblk), a.dtype) for a in ins]
    avals += [jax.ShapeDtypeStruct(p.shape, p.dtype) for p in params]
    out_avals = jax.eval_shape(fn, *avals)
    n_in = len(ins) + len(params)

    def body(*refs):
        outs = fn(*[r[...] for r in refs[:n_in]])
        for r, o in zip(refs[n_in:], outs):
            r[...] = o.astype(r.dtype)

    return pl.pallas_call(
        body, grid=(n_rows // tm, nblk),
        in_specs=_rw_specs(ins, tm, nblk) + [pl.BlockSpec(p.shape, lambda i, j: (0, 0)) for p in params],
        out_specs=tuple(pl.BlockSpec((tm, o.shape[1]), lambda i, j: (i, j)) for o in out_avals),
        out_shape=tuple(jax.ShapeDtypeStruct((n_rows, o.shape[1] * nblk), o.dtype) for o in out_avals),
        name=name, compiler_params=_params(2),
    )(*ins, *params)


def _rowwise_bwd(fn, name, rows, consts, params, cts, tm, nblk):
    n_rows = rows[0].shape[0]
    tm = min(tm, n_rows)
    nr, nc, npar, nct = len(rows), len(consts), len(params), len(cts)

    def body(*refs):
        rv = [r[...] for r in refs[:nr]]
        cv = [r[...] for r in refs[nr:nr + nc]]
        pv = [r[...] for r in refs[nr + nc:nr + nc + npar]]
        ctv = [r[...] for r in refs[nr + nc + npar:nr + nc + npar + nct]]
        outs = refs[nr + nc + npar + nct:]
        _, vjp = jax.vjp(lambda *d: fn(*d[:nr], *cv, *d[nr:]), *rv, *pv)
        grads = vjp(tuple(ctv))
        for k in range(nr):
            outs[k][...] = grads[k]
        first = jnp.logical_and(pl.program_id(0) == 0, pl.program_id(1) == 0)
        for k in range(npar):
            ref = outs[nr + k]

            @pl.when(first)
            def _(ref=ref):
                ref[...] = jnp.zeros_like(ref)

            ref[...] += grads[nr + k]

    ins = list(rows) + list(consts)
    return pl.pallas_call(
        body, grid=(n_rows // tm, nblk),
        in_specs=(_rw_specs(ins, tm, nblk) + [pl.BlockSpec(p.shape, lambda i, j: (0, 0)) for p in params]
                  + _rw_specs(cts, tm, nblk)),
        out_specs=tuple(_rw_specs(rows, tm, nblk) + [pl.BlockSpec(p.shape, lambda i, j: (0, 0)) for p in params]),
        out_shape=tuple([jax.ShapeDtypeStruct(a.shape, F32) for a in rows]
                        + [jax.ShapeDtypeStruct(p.shape, F32) for p in params]),
        name=name, compiler_params=_params(2),
    )(*ins, *params, *cts)


def _merge_fn(ga, gb, pa, pb):
    return (jax.nn.sigmoid(ga) * pa + jax.nn.sigmoid(gb) * pb,)


def _outnorm_gate_fn(o, gate, gain):
    y = o * lax.rsqrt(jnp.mean(o * o, axis=-1, keepdims=True) + EPS) * gain
    return (y * (gate * jax.nn.sigmoid(gate)),)


def _beta_decay_fn(beta_raw, decay_raw, a_log, dt_bias):
    z = decay_raw + dt_bias
    softplus = jnp.maximum(z, 0.0) + jnp.log(1.0 + jnp.exp(-jnp.abs(z)))
    g = -jnp.exp(a_log) * softplus
    rows = g.shape[0]
    ii = lax.broadcasted_iota(jnp.int32, (rows, rows), 0)
    jj = lax.broadcasted_iota(jnp.int32, (rows, rows), 1)
    same_chunk_before = jnp.logical_and(jj <= ii, jj // GDN_CHUNK == ii // GDN_CHUNK).astype(F32)
    gcum = lax.dot_general(same_chunk_before, g, NN, precision=HI, preferred_element_type=F32)
    return jax.nn.sigmoid(beta_raw), gcum


def _combine_fn(o0, o1, o2, l0, l1, l2):
    m = lax.stop_gradient(jnp.maximum(jnp.maximum(l0, l1), l2))
    e0, e1, e2 = jnp.exp(l0 - m), jnp.exp(l1 - m), jnp.exp(l2 - m)
    return ((e0 * o0 + e1 * o1 + e2 * o2) / (e0 + e1 + e2),)


def _loss_fn(x, target, gain):
    y = x * lax.rsqrt(jnp.mean(x * x, axis=-1, keepdims=True) + EPS) * gain
    err = y - target
    return (0.5 * jnp.mean(err * err, axis=-1, keepdims=True),)


def _rotate(v, cos, sin):
    half = DSW_HEAD_DIM // 2
    lane = lax.broadcasted_iota(jnp.int32, cos.shape, 1)
    low = (lane % DSW_HEAD_DIM) < half
    slabs = []
    for s in range(v.shape[1] // LANES):
        x = v[:, s * LANES:(s + 1) * LANES]
        swapped = jnp.where(low, pltpu.roll(x, LANES - half, 1), pltpu.roll(x, half, 1))
        slabs.append(x * cos + swapped * sin)
    return jnp.concatenate(slabs, axis=1)


def _rope_tables(n_tokens):
    half = DSW_HEAD_DIM // 2
    inv_freq = ROPE_THETA ** (-jnp.arange(half, dtype=F32) / half)
    ang = jnp.arange(n_tokens, dtype=F32)[:, None] * inv_freq[None, :]
    cos, sin = jnp.cos(ang), jnp.sin(ang)
    return jnp.tile(jnp.concatenate([cos, cos], 1), (1, 2)), jnp.tile(jnp.concatenate([-sin, sin], 1), (1, 2))


def _attn_probs(q, kp, kc, group, n):
    blk = DSW_BLOCK
    k = _each(lambda a, b: jnp.concatenate([a, b], axis=0).astype(BF16), kp, kc)
    s = _each(lambda a, b: lax.dot_general(a.astype(BF16), b, NT, preferred_element_type=F32)
              * (DSW_HEAD_DIM ** -0.5), q, k)
    blocks_per_seq = jnp.where(group == 0, 16, jnp.where(group == 1, 4, 1))
    first = (n % blocks_per_seq) == 0
    qi = lax.broadcasted_iota(jnp.int32, (blk, 2 * blk), 0)
    kj = lax.broadcasted_iota(jnp.int32, (blk, 2 * blk), 1)
    dist = qi + blk - kj
    valid = (dist >= 0) & (dist <= blk) & jnp.logical_or(kj >= blk, jnp.logical_not(first))
    s = _each(lambda a: jnp.where(valid, a, -1e30), s)
    m = _each(lambda a: jnp.max(a, axis=-1, keepdims=True), s)
    p = _each(lambda a, b: jnp.exp(a - b), s, m)
    l = _each(lambda a: jnp.sum(a, axis=-1, keepdims=True), p)
    return _each(lambda a, b: a / b, p, l), _each(lambda a, b: a + jnp.log(b), m, l), k


PAIRS_PER_GROUP = DSW_HEADS_PER_GROUP // 2


def _attn_specs(n_tokens):
    blk = DSW_BLOCK
    cur = pl.BlockSpec((PAIRS_PER_GROUP, blk, LANES), lambda g, n: (g, n, 0))
    prev = pl.BlockSpec((PAIRS_PER_GROUP, blk, LANES), lambda g, n: (g, jnp.maximum(n - 1, 0), 0))
    return cur, prev


def _heads_of(ref):
    pairs = [ref[p] for p in range(PAIRS_PER_GROUP)]
    return [x[:, s * DSW_HEAD_DIM:(s + 1) * DSW_HEAD_DIM] for x in pairs for s in range(2)]


def _pairs_of(heads):
    return [jnp.concatenate(heads[2 * p:2 * p + 2], axis=1) for p in range(PAIRS_PER_GROUP)]


def _attn_fwd(q, k, v):
    n_pairs, n_tokens, _ = q.shape
    cur, prev = _attn_specs(n_tokens)

    def body(q_ref, kp_ref, kc_ref, vp_ref, vc_ref, o_ref, l_ref):
        p, lse, _ = _attn_probs(_heads_of(q_ref), _heads_of(kp_ref), _heads_of(kc_ref),
                                pl.program_id(0), pl.program_id(1))
        vv = _each(lambda a, b: jnp.concatenate([a, b], axis=0).astype(BF16), _heads_of(vp_ref), _heads_of(vc_ref))
        o = _each(lambda a, b: lax.dot_general(a.astype(BF16), b, NN, preferred_element_type=F32), p, vv)
        lse_wide = _each(lambda a: jnp.broadcast_to(a, (DSW_BLOCK, DSW_HEAD_DIM)), lse)
        for pair, (o_pair, l_pair) in enumerate(zip(_pairs_of(o), _pairs_of(lse_wide))):
            o_ref[pair] = o_pair
            l_ref[pair] = l_pair

    return pl.pallas_call(
        body, grid=(n_pairs // PAIRS_PER_GROUP, n_tokens // DSW_BLOCK), in_specs=[cur, prev, cur, prev, cur],
        out_specs=(cur, cur), out_shape=(jax.ShapeDtypeStruct(q.shape, F32), jax.ShapeDtypeStruct(q.shape, F32)),
        name="attn_fwd", compiler_params=_params(2),
    )(q, k, k, v, v)


def _attn_bwd(q, k, v, do, dlse):
    n_pairs, n_tokens, _ = q.shape
    nblk = n_tokens // DSW_BLOCK
    cur, prev = _attn_specs(n_tokens)
    part = pl.BlockSpec((PAIRS_PER_GROUP, 1, 2 * DSW_BLOCK, LANES), lambda g, n: (g, n, 0, 0))
    scale = DSW_HEAD_DIM ** -0.5

    def body(q_ref, kp_ref, kc_ref, vp_ref, vc_ref, do_ref, dl_ref, dq_ref, dk_ref, dv_ref):
        qs = _heads_of(q_ref)
        p, _, kb = _attn_probs(qs, _heads_of(kp_ref), _heads_of(kc_ref), pl.program_id(0), pl.program_id(1))
        qb = _each(lambda a: a.astype(BF16), qs)
        vv = _each(lambda a, b: jnp.concatenate([a, b], axis=0).astype(BF16), _heads_of(vp_ref), _heads_of(vc_ref))
        dob = _each(lambda a: a.astype(BF16), _heads_of(do_ref))
        dp = _each(lambda a, b: lax.dot_general(a, b, NT, preferred_element_type=F32), dob, vv)
        dv = _each(lambda a, b: lax.dot_general(a.astype(BF16), b, TN, preferred_element_type=F32), p, dob)
        dl = _each(lambda a: jnp.sum(a, axis=-1, keepdims=True), _heads_of(dl_ref))
        ds = _each(lambda a, b, c: (a * (b - jnp.sum(b * a, axis=-1, keepdims=True) + c) * scale).astype(BF16),
                   p, dp, dl)
        dq = _each(lambda a, b: lax.dot_general(a, b, NN, preferred_element_type=F32), ds, kb)
        dk = _each(lambda a, b: lax.dot_general(a, b, TN, preferred_element_type=F32), ds, qb)
        for pair, (dq_pair, dk_pair, dv_pair) in enumerate(zip(_pairs_of(dq), _pairs_of(dk), _pairs_of(dv))):
            dq_ref[pair] = dq_pair
            dk_ref[pair, 0] = dk_pair
            dv_ref[pair, 0] = dv_pair

    partial_shape = jax.ShapeDtypeStruct((n_pairs, nblk, 2 * DSW_BLOCK, LANES), F32)
    dq, dkp, dvp = pl.pallas_call(
        body, grid=(n_pairs // PAIRS_PER_GROUP, nblk), in_specs=[cur, prev, cur, prev, cur, cur, cur],
        out_specs=(cur, part, part), out_shape=(jax.ShapeDtypeStruct(q.shape, F32), partial_shape, partial_shape),
        name="attn_bwd", compiler_params=_params(2),
    )(q, k, k, v, v, do, dlse)

    def fold(partial):
        own = partial[:, :, DSW_BLOCK:]
        from_next = jnp.pad(partial[:, 1:, :DSW_BLOCK], ((0, 0), (0, 1), (0, 0), (0, 0)))
        return (own + from_next).reshape(n_pairs, n_tokens, LANES)

    return dq, fold(dkp), fold(dvp)


def _to_heads(a):
    n_tokens = a.shape[0]
    outs = []
    for gi, d in enumerate(DSW_DILATIONS):
        blk = a[:, gi * 256:(gi + 1) * 256].reshape(n_tokens // d, d, PAIRS_PER_GROUP, LANES)
        outs.append(blk.transpose(2, 1, 0, 3).reshape(PAIRS_PER_GROUP, n_tokens, LANES))
    return jnp.concatenate(outs, 0)


def _from_heads(a):
    n_tokens = a.shape[1]
    outs = []
    for gi, d in enumerate(DSW_DILATIONS):
        blk = a[gi * PAIRS_PER_GROUP:(gi + 1) * PAIRS_PER_GROUP].reshape(PAIRS_PER_GROUP, d, n_tokens // d, LANES)
        outs.append(blk.transpose(2, 1, 0, 3).reshape(n_tokens, PAIRS_PER_GROUP * LANES))
    return outs


CONV_TILE = 512


def _shift_down(x, k, rows):
    return x if k == 0 else jnp.where(rows >= k, pltpu.roll(x, k, 0), 0.0)


def _shift_up(x, k, rows):
    n = x.shape[0]
    return x if k == 0 else jnp.where(rows < n - k, pltpu.roll(x, n - k, 0), 0.0)


def _conv_pre(x, w):
    rows = lax.broadcasted_iota(jnp.int32, x.shape, 0)
    acc = x * w[GDN_CONV - 1:GDN_CONV]
    for k in range(1, GDN_CONV):
        acc = acc + _shift_down(x, k, rows) * w[GDN_CONV - 1 - k:GDN_CONV - k]
    return acc, rows


def _conv_fwd(x, w):
    n_tokens, width = x.shape
    big = pl.BlockSpec((n_tokens, CONV_TILE), lambda j: (0, j))
    wsp = pl.BlockSpec((GDN_CONV, CONV_TILE), lambda j: (0, j))

    def body(x_ref, w_ref, o_ref):
        acc, _ = _conv_pre(x_ref[...], w_ref[...])
        o_ref[...] = acc * jax.nn.sigmoid(acc)

    return pl.pallas_call(
        body, grid=(width // CONV_TILE,), in_specs=[big, wsp], out_specs=big,
        out_shape=jax.ShapeDtypeStruct(x.shape, F32), name="conv_fwd", compiler_params=_params(1),
    )(x, w)


def _conv_bwd(x, w, dy):
    n_tokens, width = x.shape
    big = pl.BlockSpec((n_tokens, CONV_TILE), lambda j: (0, j))
    wsp = pl.BlockSpec((GDN_CONV, CONV_TILE), lambda j: (0, j))

    def body(x_ref, w_ref, dy_ref, dx_ref, dw_ref):
        xv, wv = x_ref[...], w_ref[...]
        acc, rows = _conv_pre(xv, wv)
        sg = jax.nn.sigmoid(acc)
        dacc = dy_ref[...] * (sg + acc * sg * (1.0 - sg))
        dx = dacc * wv[GDN_CONV - 1:GDN_CONV]
        for k in range(1, GDN_CONV):
            dx = dx + _shift_up(dacc, k, rows) * wv[GDN_CONV - 1 - k:GDN_CONV - k]
        dx_ref[...] = dx
        for k in range(GDN_CONV):
            dw_ref[GDN_CONV - 1 - k:GDN_CONV - k, :] = jnp.sum(dacc * _shift_down(xv, k, rows), axis=0, keepdims=True)

    return pl.pallas_call(
        body, grid=(width // CONV_TILE,), in_specs=[big, wsp, big], out_specs=(big, wsp),
        out_shape=(jax.ShapeDtypeStruct(x.shape, F32), jax.ShapeDtypeStruct(w.shape, F32)),
        name="conv_bwd", compiler_params=_params(1),
    )(x, w, dy)


def _dot(a, b, dn=NN):
    return lax.dot_general(a, b, dn, precision=HI, preferred_element_type=F32)


def _dot3(a, b, dn=NN):
    return lax.dot_general(a, b, dn, precision=lax.Precision.HIGH, preferred_element_type=F32)


def _bf16_dot(a, b, dn):
    return lax.dot_general(a.astype(BF16), b.astype(BF16), dn, preferred_element_type=F32)


_DOT_GRADS = {NN: (("g", "b", NT), ("a", "g", TN)), NT: (("g", "b", NN), ("g", "a", TN)),
              TN: (("b", "g", NT), ("a", "g", NN))}


def _make_bdot(dn):
    @jax.custom_vjp
    def op(a, b):
        return _bf16_dot(a, b, dn)

    def fwd(a, b):
        return op(a, b), (a, b)

    def bwd(saved, g):
        vals = dict(a=saved[0], b=saved[1], g=g)
        return tuple(_bf16_dot(vals[x], vals[y], form) for x, y, form in _DOT_GRADS[dn])

    op.defvjp(fwd, bwd)
    return op


_BDOTS = {dn: _make_bdot(dn) for dn in (NN, NT, TN)}


def _bdot(a, b, dn=NN):
    return _BDOTS[dn](a, b)


def _each(fn, *lists):
    return [fn(*items) for items in zip(*lists)]


@jax.custom_vjp
def _known_inverse(m, inverse):
    return inverse


def _known_inverse_fwd(m, inverse):
    return inverse, inverse


def _known_inverse_bwd(inverse, d_inverse):
    return -_dot3(_dot3(inverse, d_inverse, TN), inverse, NT), jnp.zeros_like(inverse)


_known_inverse.defvjp(_known_inverse_fwd, _known_inverse_bwd)


def _gdn_chunks(q, k, v, b, gcum, state, inverse=None):
    c = GDN_CHUNK
    ii = lax.broadcasted_iota(jnp.int32, (c, c), 0)
    jj = lax.broadcasted_iota(jnp.int32, (c, c), 1)
    qn = _each(lambda x: x * lax.rsqrt(jnp.sum(x * x, axis=-1, keepdims=True) + EPS) * (GDN_HEAD_DIM ** -0.5), q)
    kn = _each(lambda x: x * lax.rsqrt(jnp.sum(x * x, axis=-1, keepdims=True) + EPS), k)
    gcum_i = _each(lambda x: jnp.broadcast_to(x, (c, c)), gcum)
    gcum_j = _each(jnp.transpose, gcum_i)
    decay = _each(lambda x, y: jnp.exp(jnp.where(jj <= ii, x - y, -1e30)), gcum_i, gcum_j)
    g_last = _each(lambda x: x[c - 1:c, :], gcum)
    e_gcum = _each(jnp.exp, gcum)
    kbeta = _each(lambda x, y: x * y, kn, b)
    vbeta = _each(lambda x, y: x * y, v, b)
    m = _each(lambda x, y, d: jnp.where(jj < ii, _bdot(x, y, NT) * d, 0.0), kbeta, kn, decay)
    if inverse is not None:
        inv = _each(_known_inverse, m, inverse)
    else:
        eye = (ii == jj).astype(F32)
        inv = _each(lambda x: eye - x, m)
        power = _each(lambda x: _dot3(x, x), m)
        for step in range(5):
            inv = _each(lambda x, p: x + _dot3(x, p), inv, power)
            if step < 4:
                power = _each(lambda p: _dot3(p, p), power)
    u = _each(_dot3, inv, vbeta)
    w = _each(lambda x, y, e: _dot3(x, y * e), inv, kbeta, e_gcum)
    a_qk = _each(lambda x, y, d: _bdot(x, y, NT) * d, qn, kn, decay)
    v_new = _each(lambda x, y, s: x - _bdot(y, s), u, w, state)
    o = _each(lambda x, e, s, a, vn: _bdot(x * e, s) + _bdot(a, vn), qn, e_gcum, state, a_qk, v_new)
    new_state = _each(lambda s, gl, x, gc, vn: s * jnp.exp(gl) + _bdot(x * jnp.exp(gl - gc), vn, TN),
                      state, g_last, kn, gcum, v_new)
    return o, new_state, inv


GDN_HEADS_PER_STEP = 8


GDN_TIME_TILE = 256


def _gdn_specs(n_tokens, reverse):
    hb, hd, tt = GDN_HEADS_PER_STEP, GDN_HEAD_DIM, GDN_TIME_TILE
    nb, nt = GDN_HEADS // hb, n_tokens // tt

    def when(t):
        return nt - 1 - t if reverse else t

    q = pl.BlockSpec((tt, hb * hd), lambda h, t: (when(t), h))
    k = pl.BlockSpec((tt, hb * hd), lambda h, t: (when(t), nb + h))
    v = pl.BlockSpec((tt, hb * hd), lambda h, t: (when(t), 2 * nb + h))
    vec = pl.BlockSpec((tt, hb), lambda h, t: (when(t), h))
    states = pl.BlockSpec((hb, tt // GDN_CHUNK, hd, hd), lambda h, t: (h, when(t), 0, 0))
    inverses = pl.BlockSpec((hb, tt // GDN_CHUNK, GDN_CHUNK, GDN_CHUNK), lambda h, t: (h, when(t), 0, 0))
    return q, k, v, vec, states, inverses


def _gdn_fwd(qkv, beta, g):
    n_tokens = qkv.shape[0]
    hb, hd, tt = GDN_HEADS_PER_STEP, GDN_HEAD_DIM, GDN_TIME_TILE
    n_chunks = tt // GDN_CHUNK
    q_s, k_s, v_s, vec, st, inv_s = _gdn_specs(n_tokens, False)

    def body(q_ref, k_ref, v_ref, b_ref, g_ref, o_ref, st_ref, inv_ref, state):
        @pl.when(pl.program_id(1) == 0)
        def _():
            state[...] = jnp.zeros_like(state)

        def step(c, carry):
            r = pl.ds(pl.multiple_of(c * GDN_CHUNK, GDN_CHUNK), GDN_CHUNK)
            cols = [slice(h * hd, (h + 1) * hd) for h in range(hb)]
            old = [state[h] for h in range(hb)]
            o, new, inv = _gdn_chunks(
                [q_ref[r, cs] for cs in cols], [k_ref[r, cs] for cs in cols], [v_ref[r, cs] for cs in cols],
                [b_ref[r, h:h + 1] for h in range(hb)], [g_ref[r, h:h + 1] for h in range(hb)], old)
            for h in range(hb):
                st_ref[h, c] = old[h]
                inv_ref[h, c] = inv[h]
                o_ref[r, cols[h]] = o[h]
                state[h] = new[h]
            return carry

        lax.fori_loop(0, n_chunks, step, 0)

    n_all = n_tokens // GDN_CHUNK
    return pl.pallas_call(
        body, grid=(GDN_HEADS // hb, n_tokens // tt), in_specs=[q_s, k_s, v_s, vec, vec], out_specs=(q_s, st, inv_s),
        out_shape=(jax.ShapeDtypeStruct((n_tokens, GDN_WIDTH), F32),
                   jax.ShapeDtypeStruct((GDN_HEADS, n_all, hd, hd), F32),
                   jax.ShapeDtypeStruct((GDN_HEADS, n_all, GDN_CHUNK, GDN_CHUNK), F32)),
        scratch_shapes=[pltpu.VMEM((hb, hd, hd), F32)],
        name="gdn_fwd", compiler_params=_params(2),
    )(qkv, qkv, qkv, beta, g)


def _gdn_bwd(qkv, beta, g, states, inverses, do):
    n_tokens = qkv.shape[0]
    hb, hd, tt = GDN_HEADS_PER_STEP, GDN_HEAD_DIM, GDN_TIME_TILE
    n_chunks = tt // GDN_CHUNK
    q_s, k_s, v_s, vec, st, inv_s = _gdn_specs(n_tokens, True)

    assert hb == GDN_HEADS

    def body(q_ref, k_ref, v_ref, b_ref, g_ref, st_ref, inv_ref, do_ref, dqkv_ref, db_ref, dg_ref, dstate):
        @pl.when(pl.program_id(1) == 0)
        def _():
            dstate[...] = jnp.zeros_like(dstate)

        def step(i, carry):
            c = n_chunks - 1 - i
            r = pl.ds(pl.multiple_of(c * GDN_CHUNK, GDN_CHUNK), GDN_CHUNK)
            cols = [slice(h * hd, (h + 1) * hd) for h in range(hb)]
            args = ([q_ref[r, cs] for cs in cols], [k_ref[r, cs] for cs in cols], [v_ref[r, cs] for cs in cols],
                    [b_ref[r, h:h + 1] for h in range(hb)], [g_ref[r, h:h + 1] for h in range(hb)],
                    [st_ref[h, c] for h in range(hb)])
            saved = [inv_ref[h, c] for h in range(hb)]
            cts = ([do_ref[r, cs] for cs in cols], [dstate[h] for h in range(hb)])
            dq, dk, dv, db, dg, dst = jax.vjp(lambda *a: _gdn_chunks(*a, inverse=saved)[:2], *args)[1](cts)
            for h in range(hb):
                for part, grad in enumerate((dq, dk, dv)):
                    dqkv_ref[r, pl.ds(part * GDN_WIDTH + h * hd, hd)] = grad[h]
                db_ref[r, h:h + 1] = db[h]
                dg_ref[r, h:h + 1] = dg[h]
                dstate[h] = dst[h]
            return carry

        lax.fori_loop(0, n_chunks, step, 0)

    n_t = n_tokens // tt
    thin = jax.ShapeDtypeStruct(beta.shape, F32)
    return pl.pallas_call(
        body, grid=(GDN_HEADS // hb, n_t), in_specs=[q_s, k_s, v_s, vec, vec, st, inv_s, q_s],
        out_specs=(pl.BlockSpec((tt, 3 * GDN_WIDTH), lambda h, t: (n_t - 1 - t, 0)), vec, vec),
        out_shape=(jax.ShapeDtypeStruct(qkv.shape, F32), thin, thin),
        scratch_shapes=[pltpu.VMEM((hb, hd, hd), F32)],
        name="gdn_bwd", compiler_params=_params(2),
    )(qkv, qkv, qkv, beta, g, states, inverses, do)


FFN_ROW_TILE = 256
FFN_FWD_ROW_TILE = 512


def _resident(shape):
    return pl.BlockSpec(shape, lambda i: (0,) * len(shape), pipeline_mode=pl.Buffered(1))


def _ffn_fwd(x, gain, wg, wu, wd, name):
    n_tokens, d = x.shape
    n_shards, n, _ = wg.shape
    tm = FFN_FWD_ROW_TILE

    def body(x_ref, gain_ref, wg_ref, wu_ref, wd_ref, o_ref, g_ref, u_ref):
        xv = x_ref[...]
        h = (xv * lax.rsqrt(jnp.mean(xv * xv, axis=-1, keepdims=True) + EPS) * gain_ref[...]).astype(BF16)
        acc = jnp.zeros((tm, d), F32)
        for j in range(n_shards):
            g = lax.dot_general(h, wg_ref[j], NT, preferred_element_type=F32)
            u = lax.dot_general(h, wu_ref[j], NT, preferred_element_type=F32)
            g_ref[j] = g
            u_ref[j] = u
            a = (g * jax.nn.sigmoid(g) * u).astype(BF16)
            acc = acc + lax.dot_general(a, wd_ref[j], NN, preferred_element_type=F32)
        o_ref[...] = xv + 0.5 * acc

    row = pl.BlockSpec((tm, d), lambda i: (i, 0))
    hid = pl.BlockSpec((n_shards, tm, n), lambda i: (0, i, 0))
    return pl.pallas_call(
        body, grid=(n_tokens // tm,),
        in_specs=[row, _resident(gain.shape), _resident(wg.shape), _resident(wu.shape), _resident(wd.shape)],
        out_specs=(row, hid, hid),
        out_shape=(jax.ShapeDtypeStruct(x.shape, F32), jax.ShapeDtypeStruct((n_shards, n_tokens, n), F32),
                   jax.ShapeDtypeStruct((n_shards, n_tokens, n), F32)),
        name=name, compiler_params=_params(1),
    )(x, gain, wg, wu, wd)


def _ffn_bwd_rows(x, gain, dy, g, u, wg, wu, wd, name):
    n_tokens, d = x.shape
    n_shards, n, _ = wg.shape
    tm = FFN_ROW_TILE

    def body(x_ref, gain_ref, dy_ref, g_ref, u_ref, wg_ref, wu_ref, wd_ref,
             dx_ref, dgain_ref, h_ref, dyh_ref, a_ref, dg_ref, du_ref):
        xv, dyv, gain_v = x_ref[...], dy_ref[...], gain_ref[...]
        r = lax.rsqrt(jnp.mean(xv * xv, axis=-1, keepdims=True) + EPS)
        xhat = xv * r
        h_ref[...] = (xhat * gain_v).astype(BF16)
        dyh = (0.5 * dyv).astype(BF16)
        dyh_ref[...] = dyh
        dh = jnp.zeros((tm, d), F32)
        for j in range(n_shards):
            da = lax.dot_general(dyh, wd_ref[j], NT, preferred_element_type=F32)
            gv, uv = g_ref[j], u_ref[j]
            sg = jax.nn.sigmoid(gv)
            silu = gv * sg
            a_ref[j] = (silu * uv).astype(BF16)
            dg = (da * uv * (sg + silu * (1.0 - sg))).astype(BF16)
            du = (da * silu).astype(BF16)
            dg_ref[j] = dg
            du_ref[j] = du
            dh = dh + lax.dot_general(dg, wg_ref[j], NN, preferred_element_type=F32)
            dh = dh + lax.dot_general(du, wu_ref[j], NN, preferred_element_type=F32)
        dxhat = dh * gain_v
        dx_ref[...] = dyv + r * (dxhat - xhat * jnp.mean(dxhat * xhat, axis=-1, keepdims=True))

        @pl.when(pl.program_id(0) == 0)
        def _():
            dgain_ref[...] = jnp.zeros_like(dgain_ref)

        dgain_ref[...] += jnp.sum(dh * xhat, axis=0, keepdims=True)

    row = pl.BlockSpec((tm, d), lambda i: (i, 0))
    hid = pl.BlockSpec((n_shards, tm, n), lambda i: (0, i, 0))
    hid_shape = (n_shards, n_tokens, n)
    return pl.pallas_call(
        body, grid=(n_tokens // tm,),
        in_specs=[row, _resident(gain.shape), row, hid, hid, _resident(wg.shape), _resident(wu.shape),
                  _resident(wd.shape)],
        out_specs=(row, pl.BlockSpec(gain.shape, lambda i: (0, 0)), row, row, hid, hid, hid),
        out_shape=(jax.ShapeDtypeStruct(x.shape, F32), jax.ShapeDtypeStruct(gain.shape, F32),
                   jax.ShapeDtypeStruct(x.shape, BF16), jax.ShapeDtypeStruct(x.shape, BF16),
                   jax.ShapeDtypeStruct(hid_shape, BF16), jax.ShapeDtypeStruct(hid_shape, BF16),
                   jax.ShapeDtypeStruct(hid_shape, BF16)),
        name=name, compiler_params=_params(1),
    )(x, gain, dy, g, u, wg, wu, wd)


def _ffn_bwd_weights(h, dyh, a, dg, du, name, owner=None):
    n_shards, n_tokens, n = a.shape
    d = h.shape[1]

    def products(h_ref, dyh_ref, a_ref, dg_ref, du_ref):
        hv = h_ref[...]
        return (lax.dot_general(dg_ref[0], hv, TN, preferred_element_type=F32),
                lax.dot_general(du_ref[0], hv, TN, preferred_element_type=F32),
                lax.dot_general(a_ref[0], dyh_ref[...], TN, preferred_element_type=F32))

    hid = pl.BlockSpec((1, n_tokens, n), lambda j, *_: (j, 0, 0))
    out = pl.BlockSpec((1, n, d), lambda j, *_: (j, 0, 0))
    ins = [pl.BlockSpec(h.shape, lambda j, *_: (0, 0), pipeline_mode=pl.Buffered(1)),
           pl.BlockSpec(dyh.shape, lambda j, *_: (0, 0), pipeline_mode=pl.Buffered(1)), hid, hid, hid]
    if owner is None:
        def body(*refs):
            for ref, val in zip(refs[5:], products(*refs[:5])):
                ref[0] = val

        return pl.pallas_call(
            body, grid=(n_shards,), in_specs=ins, out_specs=(out, out, out),
            out_shape=(jax.ShapeDtypeStruct((n_shards, n, d), F32),) * 3, name=name, compiler_params=_params(1),
        )(h, dyh, a, dg, du)

    def body(owner_ref, *refs):
        vals = products(*refs[:5])
        for ref, val in zip(refs[5:8], vals):
            ref[0] = val.astype(BF16)

        @pl.when(pl.program_id(0) == owner_ref[0])
        def _():
            for ref, val in zip(refs[8:], vals):
                ref[0] = val

    mine = pl.BlockSpec((1, n, d), lambda j, *_: (0, 0, 0))
    outs = pl.pallas_call(
        body,
        grid_spec=pltpu.PrefetchScalarGridSpec(num_scalar_prefetch=1, grid=(n_shards,), in_specs=ins,
                                               out_specs=(out, out, out, mine, mine, mine)),
        out_shape=(jax.ShapeDtypeStruct((n_shards, n, d), BF16),) * 3 + (jax.ShapeDtypeStruct((1, n, d), F32),) * 3,
        name=name, compiler_params=_params(1),
    )(owner, h, dyh, a, dg, du)
    return outs[:3], outs[3:]


IN_PIECES = (("wq_a", 0, 768), ("wk_a", 768, 1536), ("wv_a", 1536, 2304), ("w_qkvb", 2304, 5376),
             ("w_small", 5376, 5392), ("w_ggate", 5392, 6416), ("w_gatea", 6416, 7440), ("w_gateb", 7440, 8464))
IN_NAMES = tuple(name for name, _, _ in IN_PIECES)


def _in_rows(lo, hi):
    return lo, max(hi, lo + LANES)


N_ROTATED = 2


def _in_proj_fwd(x, gain, wt, cos, sin):
    n_tokens, d = x.shape
    tm = FFN_ROW_TILE
    rows = [_in_rows(lo, hi) for _, lo, hi in IN_PIECES]

    def body(x_ref, gain_ref, wt_ref, cos_ref, sin_ref, *o_refs):
        xv = x_ref[...]
        h = (xv * lax.rsqrt(jnp.mean(xv * xv, axis=-1, keepdims=True) + EPS) * gain_ref[...]).astype(BF16)
        for k, ((lo, hi), o_ref) in enumerate(zip(rows, o_refs)):
            z = lax.dot_general(h, wt_ref[lo:hi, :], NT, preferred_element_type=F32)
            o_ref[...] = _rotate(z, cos_ref[...], sin_ref[...]) if k < N_ROTATED else z

    tab = pl.BlockSpec((tm, LANES), lambda i: (i, 0))
    return pl.pallas_call(
        body, grid=(n_tokens // tm,),
        in_specs=[pl.BlockSpec((tm, d), lambda i: (i, 0)), _resident(gain.shape), _resident(wt.shape), tab, tab],
        out_specs=tuple(pl.BlockSpec((tm, hi - lo), lambda i: (i, 0)) for lo, hi in rows),
        out_shape=tuple(jax.ShapeDtypeStruct((n_tokens, hi - lo), F32) for lo, hi in rows),
        name="in_proj_fwd", compiler_params=_params(1),
    )(x, gain, wt, cos, sin)


def _in_proj_bwd_rows(x, gain, dres, dzs, wt, cos, sin):
    n_tokens, d = x.shape
    tm = FFN_ROW_TILE
    n = len(dzs)
    rows = [_in_rows(lo, hi) for _, lo, hi in IN_PIECES]

    def body(x_ref, gain_ref, dres_ref, cos_ref, sin_ref, *refs):
        dz_refs, wt_ref = refs[:n], refs[n]
        dx_ref, dgain_ref, h_ref = refs[n + 1:n + 4]
        unrotated_refs = refs[n + 4:]
        xv, gain_v = x_ref[...], gain_ref[...]
        r = lax.rsqrt(jnp.mean(xv * xv, axis=-1, keepdims=True) + EPS)
        xhat = xv * r
        h_ref[...] = (xhat * gain_v).astype(BF16)
        dh = jnp.zeros((tm, d), F32)
        for k, (dz_ref, (lo, hi)) in enumerate(zip(dz_refs, rows)):
            dz = dz_ref[...]
            if k < N_ROTATED:
                dz = _rotate(dz, cos_ref[...], -sin_ref[...]).astype(BF16)
                unrotated_refs[k][...] = dz
            dh = dh + lax.dot_general(dz.astype(BF16), wt_ref[lo:hi, :], NN, preferred_element_type=F32)
        dxhat = dh * gain_v
        dx_ref[...] = dres_ref[...] + r * (dxhat - xhat * jnp.mean(dxhat * xhat, axis=-1, keepdims=True))

        @pl.when(pl.program_id(0) == 0)
        def _():
            dgain_ref[...] = jnp.zeros_like(dgain_ref)

        dgain_ref[...] += jnp.sum(dh * xhat, axis=0, keepdims=True)

    row = pl.BlockSpec((tm, d), lambda i: (i, 0))
    tab = pl.BlockSpec((tm, LANES), lambda i: (i, 0))
    dz_specs = [pl.BlockSpec((tm, dz.shape[1]), lambda i: (i, 0)) for dz in dzs]
    outs = pl.pallas_call(
        body, grid=(n_tokens // tm,),
        in_specs=[row, _resident(gain.shape), row, tab, tab] + dz_specs + [_resident(wt.shape)],
        out_specs=(row, pl.BlockSpec(gain.shape, lambda i: (0, 0)), row) + tuple(dz_specs[:N_ROTATED]),
        out_shape=(jax.ShapeDtypeStruct(x.shape, F32), jax.ShapeDtypeStruct(gain.shape, F32),
                   jax.ShapeDtypeStruct(x.shape, BF16))
        + tuple(jax.ShapeDtypeStruct(dz.shape, BF16) for dz in dzs[:N_ROTATED]),
        name="in_proj_bwd_rows", compiler_params=_params(1),
    )(x, gain, dres, cos, sin, *dzs, wt)
    return outs[0], outs[1], outs[2], outs[3:]


def _in_proj_bwd_weight(dwt, h, dz, lo, hi, name):
    n_tokens, d = h.shape
    width = hi - lo
    tn = _tile(width, 512) if width >= LANES else width
    dz_tile = max(tn, LANES)

    def body(dwt_ref, h_ref, dz_ref, o_ref):
        o_ref[...] = lax.dot_general(dz_ref[:, :tn].astype(BF16), h_ref[...], TN, preferred_element_type=F32)

    return pl.pallas_call(
        body, grid=(width // tn,),
        in_specs=[ANY, _resident(h.shape), pl.BlockSpec((n_tokens, dz_tile), lambda j: (0, j))],
        out_specs=pl.BlockSpec((pl.Element(tn), pl.Element(d)), lambda j: (pl.multiple_of(lo + j * tn, 16), 0)),
        out_shape=jax.ShapeDtypeStruct(dwt.shape, F32), input_output_aliases={0: 0}, name=name,
        compiler_params=_params(1),
    )(dwt, h, dz)


def _split_small(z):
    return z[:, :GDN_HEADS], z[:, GDN_HEADS:2 * GDN_HEADS]


def _heads3(q, k, v):
    return _to_heads(q), _to_heads(k), _to_heads(v)


def _tokens6(o, lse):
    return tuple(_from_heads(o)) + tuple(_from_heads(lse))


def mixer_forward(x1, w, small):
    n_tokens = x1.shape[0]
    cos, sin = _rope_tables(n_tokens)
    proj = dict(zip(IN_NAMES, _in_proj_fwd(x1, small["mix_norm"], w["w_in_t"], cos, sin)))
    (qh, kh, vh), heads_vjp = jax.vjp(_heads3, proj["wq_a"], proj["wk_a"], proj["wv_a"])
    o, lse = _attn_fwd(qh, kh, vh)
    per_group, tokens_vjp = jax.vjp(_tokens6, o, lse)
    ya = _rowwise_fwd(_combine_fn, "combine", per_group, (), (), 512, 1)[0]
    pa = _matmul(ya, w["w_branch_a"], name="branch_a")
    qkv = _conv_fwd(proj["w_qkvb"], small["gdn_conv_w"])
    raw, small_vjp = jax.vjp(_split_small, proj["w_small"])
    gdn_params = (small["gdn_a_log"], small["gdn_dt_bias"])
    beta, gcum = _rowwise_fwd(_beta_decay_fn, "beta_decay", raw, (), gdn_params, 512, 1)
    ob, *states = _gdn_fwd(qkv, beta, gcum)
    gate_in = (ob, proj["w_ggate"])
    yb = _rowwise_fwd(_outnorm_gate_fn, "outnorm_gate", gate_in, (), (small["gdn_out_norm"],), 512, GDN_HEADS)[0]
    pb = _matmul(yb, w["w_branch_b"], name="branch_b")
    merge_in = (proj["w_gatea"], proj["w_gateb"], pa, pb)
    merged = _rowwise_fwd(_merge_fn, "merge", merge_in, (), (), 256, 1)[0]
    x2 = _matmul(merged, w["w_out"], name="out", res=x1)
    saved = dict(x1=x1, proj=proj, cos=cos, sin=sin, heads_vjp=heads_vjp, heads=(qh, kh, vh), tokens_vjp=tokens_vjp,
                 per_group=per_group, ya=ya, qkv=qkv, raw=raw, small_vjp=small_vjp, beta=beta, gcum=gcum, states=states,
                 gate_in=gate_in, yb=yb, merge_in=merge_in, merged=merged)
    return x2, saved


def mixer_backward(dx2, s, w, small):
    proj = s["proj"]
    dmerged = _matmul(dx2, w["w_out"], name="out_da", tb=True)
    grads = dict(w_out=_matmul(s["merged"], dx2, name="out_dw", ta=True))
    dgate_a, dgate_b, dpa, dpb = _rowwise_bwd(_merge_fn, "merge_bwd", s["merge_in"], (), (), (dmerged,), 256, 1)
    dyb = _matmul(dpb, w["w_branch_b"], name="branch_b_da", tb=True)
    grads["w_branch_b"] = _matmul(s["yb"], dpb, name="branch_b_dw", ta=True)
    dya = _matmul(dpa, w["w_branch_a"], name="branch_a_da", tb=True)
    grads["w_branch_a"] = _matmul(s["ya"], dpa, name="branch_a_dw", ta=True)
    dob, dggate, grads["gdn_out_norm"] = _rowwise_bwd(
        _outnorm_gate_fn, "outnorm_gate_bwd", s["gate_in"], (), (small["gdn_out_norm"],), (dyb,), 512, GDN_HEADS)
    dqkv, dbeta, dgcum = _gdn_bwd(s["qkv"], s["beta"], s["gcum"], *s["states"], dob)
    gdn_params = (small["gdn_a_log"], small["gdn_dt_bias"])
    dbeta_raw, ddecay_raw, grads["gdn_a_log"], grads["gdn_dt_bias"] = _rowwise_bwd(
        _beta_decay_fn, "beta_decay_bwd", s["raw"], (), gdn_params, (dbeta, dgcum), 512, 1)
    dsmall = s["small_vjp"]((dbeta_raw, ddecay_raw))[0]
    dqkvb, grads["gdn_conv_w"] = _conv_bwd(proj["w_qkvb"], small["gdn_conv_w"], dqkv)
    dper_group = _rowwise_bwd(_combine_fn, "combine_bwd", s["per_group"], (), (), (dya,), 512, 1)
    do, dlse = s["tokens_vjp"](tuple(dper_group))
    dqh, dkh, dvh = _attn_bwd(*s["heads"], do, dlse)
    dq_rot, dk_rot, dv = s["heads_vjp"]((dqh, dkh, dvh))
    dzs = (dq_rot, dk_rot, dv, dqkvb, dsmall, dggate, dgate_a, dgate_b)
    dx1, grads["mix_norm"], h, unrotated = _in_proj_bwd_rows(
        s["x1"], small["mix_norm"], dx2, dzs, w["w_in_t"], s["cos"], s["sin"])
    dzs = tuple(unrotated) + dzs[N_ROTATED:]
    dwt = lax.empty(w["w_in_t"].shape, F32)
    for (name, lo, hi), dz in zip(IN_PIECES, dzs):
        dwt = _in_proj_bwd_weight(dwt, h, dz, lo, hi, "in_proj_dw_" + name)
    grads["w_in_t"] = dwt
    return dx1, grads


def ffn_forward(x, gain, w, tag):
    out, g, u = _ffn_fwd(x, gain, w[tag + "_w_gate"], w[tag + "_w_up"], w[tag + "_w_down"], tag + "_fwd")
    return out, (x, g, u)


def ffn_backward(dy, saved, gain, w, tag, owner=None):
    x, g, u = saved
    weights = (w[tag + "_w_gate"], w[tag + "_w_up"], w[tag + "_w_down"])
    dx, dgain, h, dyh, a, dg, du = _ffn_bwd_rows(x, gain, dy, g, u, *weights, tag + "_bwd_rows")
    return dx, dgain, _ffn_bwd_weights(h, dyh, a, dg, du, tag + "_bwd_weights", owner)


def loss_head(x3, target, gain):
    row_loss = _rowwise_fwd(_loss_fn, "loss", (x3,), (target,), (gain,), 256, 1)[0]
    dx3, dgain = _rowwise_bwd(_loss_fn, "loss_bwd", (x3,), (target,), (gain,), (jnp.ones_like(row_loss),), 256, 1)
    return jnp.sum(row_loss), dx3, dgain


BIG_WEIGHTS = ("ffn1_w_gate", "ffn1_w_up", "ffn1_w_down", "w_in", "w_branch_a", "w_branch_b", "w_out",
               "ffn2_w_gate", "ffn2_w_up", "ffn2_w_down")
TRANSPOSED = ("ffn1_w_gate", "ffn1_w_up", "w_in", "ffn2_w_gate", "ffn2_w_up")
CONV_SHARD = (GDN_CONV, 3 * GDN_WIDTH // N_DEV)
SMALL_ROWS = 24
ANY = pl.BlockSpec(memory_space=pl.ANY)


TOKEN = jax.ShapeDtypeStruct((8, LANES), F32)


def _after(value, token):
    return value + token[0, 0].astype(value.dtype)


def _position():
    return lax.axis_index("x"), lax.axis_index("y"), lax.axis_index("c")


def all_gather_shards(shards, name):
    n = len(shards)

    def body(*refs):
        x_refs, out_refs = refs[:n], refs[n:2 * n]
        send_sems, recv_sems, local_sems = refs[2 * n + 1:]
        x, y, c = _position()
        me, sibling = (x, y, c), (x, y, 1 - c)
        chips = [(1 - x, y), (x, 1 - y), (1 - x, 1 - y)]

        def slab(a, px, py, pc):
            return out_refs[a].at[4 * px + 2 * py + pc]

        def copy(a, k, block, to, src=None):
            return pltpu.make_async_remote_copy(
                src_ref=slab(a, *block) if src is None else src, dst_ref=slab(a, *block),
                send_sem=send_sems.at[7 * a + k], recv_sem=recv_sems.at[7 * a + k], device_id=to, device_id_type=MESH)

        mine = [pltpu.make_async_copy(x_refs[a], slab(a, *me), local_sems.at[a]) for a in range(n)]
        for cp in mine:
            cp.start()
        first = []
        for j, chip in enumerate(chips):
            first += [copy(a, 1 + j, me, (*chip, c), src=x_refs[a]) for a in range(n)]
        first += [copy(a, 0, me, sibling, src=x_refs[a]) for a in range(n)]
        for cp in first:
            cp.start()
        passed = []
        for j, chip in enumerate(chips):
            for a in range(n):
                copy(a, 1 + j, (*chip, c), me).wait_recv()
                cp = copy(a, 4 + j, (*chip, c), sibling)
                cp.start()
                passed.append(cp)
        for a in range(n):
            copy(a, 0, sibling, me).wait_recv()
        for j, chip in enumerate(chips):
            for a in range(n):
                copy(a, 4 + j, (*chip, 1 - c), me).wait_recv()
        for cp in first + passed:
            cp.wait_send()
        for cp in mine:
            cp.wait()
        refs[2 * n][...] = jnp.zeros_like(refs[2 * n])

    outs = pl.pallas_call(
        body, out_shape=tuple(jax.ShapeDtypeStruct((N_DEV,) + s.shape, s.dtype) for s in shards) + (TOKEN,),
        in_specs=[ANY] * n, out_specs=(ANY,) * n + (pl.BlockSpec(memory_space=pltpu.VMEM),),
        scratch_shapes=[pltpu.SemaphoreType.DMA((7 * n,)), pltpu.SemaphoreType.DMA((7 * n,)),
                        pltpu.SemaphoreType.DMA((n,))],
        name=name,
    )(*shards)
    return outs[:n], outs[n]


def exchange_with_sibling(grads):
    n = len(grads)

    def body(*refs):
        g_refs, recv_refs = refs[:n], refs[n:2 * n]
        send_sems, recv_sems = refs[2 * n:]
        x, y, c = _position()
        copies = [pltpu.make_async_remote_copy(
            src_ref=g_refs[a].at[2 * k + 1 - c], dst_ref=recv_refs[a].at[k], send_sem=send_sems.at[4 * a + k],
            recv_sem=recv_sems.at[4 * a + k], device_id=(x, y, 1 - c), device_id_type=MESH)
            for k in range(4) for a in range(n)]
        for cp in copies:
            cp.start()
        for cp in copies:
            cp.wait()

    return pl.pallas_call(
        body, out_shape=tuple(jax.ShapeDtypeStruct((4,) + g.shape[1:], g.dtype) for g in grads),
        in_specs=[ANY] * n, out_specs=(ANY,) * n,
        scratch_shapes=[pltpu.SemaphoreType.DMA((4 * n,)), pltpu.SemaphoreType.DMA((4 * n,))], name="rs_sibling",
    )(*grads)


ELEMENTWISE_TILE_BYTES = 1536 * 1024


def _tile2(rows, cols):
    if rows % 256 == 0:
        return 256, cols
    if rows * cols * 4 > ELEMENTWISE_TILE_BYTES and cols % 256 == 0:
        return rows, 256
    return rows, cols


def add_sibling(grads, received, core, name):
    _, rows, width = grads.shape
    tr, tc = _tile2(rows, width)

    def body(c_ref, g_ref, r_ref, o_ref):
        o_ref[...] = (g_ref[...] + r_ref[...]).astype(BF16)

    blk = (1, tr, tc)
    return pl.pallas_call(
        body,
        grid_spec=pltpu.PrefetchScalarGridSpec(
            num_scalar_prefetch=1, grid=(4, rows // tr, width // tc),
            in_specs=[pl.BlockSpec(blk, lambda k, i, j, c_ref: (2 * k + c_ref[0], i, j)),
                      pl.BlockSpec(blk, lambda k, i, j, c_ref: (k, i, j))],
            out_specs=pl.BlockSpec(blk, lambda k, i, j, c_ref: (k, i, j))),
        out_shape=jax.ShapeDtypeStruct((4, rows, width), BF16), name=name, compiler_params=_params(3),
    )(core, grads, received)


HBM = pl.BlockSpec(memory_space=pltpu.HBM)
SEM = pl.BlockSpec(memory_space=pltpu.SEMAPHORE)
DATAFLOW_EFFECT = pltpu.SideEffectType.DATAFLOW_SIDE_EFFECTING
N_PEERS = N_DEV - 1


def _peer(mask):
    x, y, c = _position()
    px = 1 - x if mask & 4 else x
    py = 1 - y if mask & 2 else y
    pc = 1 - c if mask & 1 else c
    return (px, py, pc), 4 * px + 2 * py + pc


ALL_PEERS = tuple(range(1, N_DEV))
OTHER_CHIPS = (4, 2, 6)


SIBLING = 1
GATHER_MODES = ("gather", "near")


def _exchange_peers(mode):
    return {"chips": OTHER_CHIPS, "near": (SIBLING,) + OTHER_CHIPS}.get(mode, ALL_PEERS)


def _direct_copies(src_refs, land_refs, send_sems, recv_sems, mode):
    x, y, c = _position()
    me = 4 * x + 2 * y + c
    masks = _exchange_peers(mode)
    copies = []
    for a, (src, land) in enumerate(zip(src_refs, land_refs)):
        for slot, mask in enumerate(masks):
            peer, peer_index = _peer(mask)
            k = len(masks) * a + slot
            if mode in GATHER_MODES:
                source, dest = src, land.at[me]
            elif mode == "scatter":
                source, dest = src.at[peer_index], land.at[slot]
            else:
                source, dest = src.at[2 * peer[0] + peer[1]], land.at[slot]
            copies.append(pltpu.make_async_remote_copy(
                src_ref=source, dst_ref=dest, send_sem=send_sems.at[k], recv_sem=recv_sems.at[k], device_id=peer,
                device_id_type=MESH))
    return copies


def forward_to_sibling(slabs, name):
    n = len(slabs)

    def body(*refs):
        out_refs = refs[n:2 * n]
        send_sems, recv_sems = refs[2 * n + 1:]
        x, y, c = _position()
        copies = []
        for a in range(n):
            for slot, mask in enumerate(OTHER_CHIPS):
                _, held = _peer(mask)
                copies.append(pltpu.make_async_remote_copy(
                    src_ref=out_refs[a].at[held], dst_ref=out_refs[a].at[held], send_sem=send_sems.at[3 * a + slot],
                    recv_sem=recv_sems.at[3 * a + slot], device_id=(x, y, 1 - c), device_id_type=MESH))
        for cp in copies:
            cp.start()
        for cp in copies:
            cp.wait()
        refs[2 * n][...] = jnp.zeros_like(refs[2 * n])

    outs = pl.pallas_call(
        body, out_shape=tuple(jax.ShapeDtypeStruct(s.shape, s.dtype) for s in slabs) + (TOKEN,),
        in_specs=[ANY] * n, out_specs=(ANY,) * n + (pl.BlockSpec(memory_space=pltpu.VMEM),),
        input_output_aliases={i: i for i in range(n)},
        scratch_shapes=[pltpu.SemaphoreType.DMA((3 * n,)), pltpu.SemaphoreType.DMA((3 * n,))], name=name,
    )(*slabs)
    return outs[:n], outs[n]


def direct_exchange_start(arrays, mode, name):
    n = len(arrays)
    n_peers = len(_exchange_peers(mode))
    lands = [lax.empty((N_DEV,) + a.shape if mode in GATHER_MODES else (n_peers,) + a.shape[1:], a.dtype)
             for a in arrays]

    def body(*refs):
        src_refs, land_refs = refs[:n], refs[n:2 * n]
        send_sems, recv_sems = refs[2 * n], refs[2 * n + 1]
        token = refs[-1]
        for cp in _direct_copies(src_refs, land_refs, send_sems, recv_sems, mode):
            cp.start()
        token[...] = jnp.zeros_like(token)

    sems = pltpu.SemaphoreType.DMA((n_peers * n,))
    outs = pl.pallas_call(
        body, name=name,
        out_shape=(sems, sems) + tuple(pltpu.HBM(a.shape, a.dtype) for a in arrays)
        + tuple(pltpu.HBM(l.shape, l.dtype) for l in lands) + (TOKEN,),
        in_specs=[HBM] * (2 * n), out_specs=(SEM, SEM) + (HBM,) * (2 * n) + (pl.BlockSpec(memory_space=pltpu.VMEM),),
        input_output_aliases={i: 2 + i for i in range(2 * n)},
        compiler_params=pltpu.CompilerParams(has_side_effects=DATAFLOW_EFFECT),
    )(*[pltpu.with_memory_space_constraint(a, pltpu.HBM) for a in list(arrays) + lands])
    return outs[0], outs[1], outs[2:2 + n], outs[2 + n:2 + 2 * n], outs[-1]


def direct_exchange_wait(send_sems, recv_sems, arrays, lands, after, mode, name):
    n = len(arrays)

    def body(*refs):
        src_refs, land_refs = refs[:n], refs[n:2 * n]
        send_sems, recv_sems = refs[2 * n], refs[2 * n + 1]
        for cp in _direct_copies(src_refs, land_refs, send_sems, recv_sems, mode):
            cp.wait_send()
            cp.wait_recv()
        refs[-1][...] = jnp.zeros_like(refs[-1])

    outs = pl.pallas_call(
        body, name=name,
        out_shape=tuple(pltpu.HBM(a.shape, a.dtype) for a in arrays) + tuple(pltpu.HBM(l.shape, l.dtype) for l in lands)
        + (TOKEN,),
        in_specs=[HBM] * (2 * n) + [SEM, SEM, pl.BlockSpec(memory_space=pl.ANY)],
        out_specs=(HBM,) * (2 * n) + (pl.BlockSpec(memory_space=pltpu.VMEM),),
        input_output_aliases={i: i for i in range(2 * n)},
        compiler_params=pltpu.CompilerParams(has_side_effects=DATAFLOW_EFFECT),
    )(*arrays, *lands, send_sems, recv_sems, after)
    return outs[n:]


def adamw_direct(w, m, v, own, received, name):
    row_per_tile = w.shape[0] != 1
    rows, cols = (w.shape[0], w.shape[2]) if row_per_tile else w.shape[-2:]
    tr, tc = _tile2(rows, cols)

    def body(w_ref, m_ref, v_ref, own_ref, r_ref, g_ref, d_ref, nm_ref, nv_ref):
        gv = own_ref[0]
        for j in range(N_PEERS):
            gv = gv + r_ref[j].astype(F32)
        nm = ADAM_B1 * m_ref[...] + (1.0 - ADAM_B1) * gv
        nv = ADAM_B2 * v_ref[...] + (1.0 - ADAM_B2) * (gv * gv)
        m_hat = nm / (1.0 - ADAM_B1 ** ADAM_STEP)
        v_hat = nv / (1.0 - ADAM_B2 ** ADAM_STEP)
        g_ref[...] = gv
        d_ref[...] = -ADAM_LR * (m_hat / (jnp.sqrt(v_hat) + ADAM_EPS) + ADAM_WD * w_ref[...])
        nm_ref[...] = nm
        nv_ref[...] = nv

    if row_per_tile:
        one = pl.BlockSpec((tr, None, tc), lambda i, j: (i, 0, j))
    else:
        one = pl.BlockSpec((None, tr, tc), lambda i, j: (0, i, j))
    out = jax.ShapeDtypeStruct(w.shape, F32)
    return pl.pallas_call(
        body, grid=(rows // tr, cols // tc),
        in_specs=[one, one, one, pl.BlockSpec((1, tr, tc), lambda i, j: (0, i, j)),
                  pl.BlockSpec((N_PEERS, tr, tc), lambda i, j: (0, i, j))],
        out_specs=(one,) * 4, out_shape=(out,) * 4, name=name, compiler_params=_params(2),
    )(w, m, v, own, received)


def all_reduce_small(vals):
    rows, width = vals.shape

    def body(x_ref, out_ref, all_ref, send_sems, recv_sems):
        x, y, c = _position()
        me, sibling = (x, y, c), (x, y, 1 - c)
        chips = [(1 - x, y), (x, 1 - y), (1 - x, 1 - y)]

        def slab(px, py, pc):
            return all_ref.at[4 * px + 2 * py + pc]

        def copy(k, block, to, src=None):
            return pltpu.make_async_remote_copy(
                src_ref=slab(*block) if src is None else src, dst_ref=slab(*block),
                send_sem=send_sems.at[k], recv_sem=recv_sems.at[k], device_id=to, device_id_type=MESH)

        first = [copy(0, me, sibling, src=x_ref)]
        first += [copy(1 + j, me, (*chip, c), src=x_ref) for j, chip in enumerate(chips)]
        for cp in first:
            cp.start()
        all_ref[4 * x + 2 * y + c] = x_ref[...]
        passed = [copy(4 + j, (*chip, c), sibling) for j, chip in enumerate(chips)]
        for j, chip in enumerate(chips):
            copy(1 + j, (*chip, c), me).wait_recv()
            passed[j].start()
        copy(0, sibling, me).wait_recv()
        for j, chip in enumerate(chips):
            copy(4 + j, (*chip, 1 - c), me).wait_recv()
        for cp in first + passed:
            cp.wait_send()
        total = all_ref[0]
        for d in range(1, N_DEV):
            total = total + all_ref[d]
        out_ref[...] = total

    vmem = pl.BlockSpec(memory_space=pltpu.VMEM)
    return pl.pallas_call(
        body, out_shape=(jax.ShapeDtypeStruct(vals.shape, F32), jax.ShapeDtypeStruct((N_DEV, rows, width), F32)),
        in_specs=[vmem], out_specs=(vmem, vmem),
        scratch_shapes=[pltpu.SemaphoreType.DMA((7,)), pltpu.SemaphoreType.DMA((7,))], name="small_allreduce",
    )(vals)[0]


def adamw(w, g, m, v, name):
    shape = w.shape
    w2, g2, m2, v2 = [a.reshape((-1, shape[-1])) for a in (w, g, m, v)]
    rows, cols = w2.shape
    tr = 256 if rows % 256 == 0 else rows

    def body(w_ref, g_ref, m_ref, v_ref, d_ref, nm_ref, nv_ref):
        gv = g_ref[...]
        nm = ADAM_B1 * m_ref[...] + (1.0 - ADAM_B1) * gv
        nv = ADAM_B2 * v_ref[...] + (1.0 - ADAM_B2) * (gv * gv)
        m_hat = nm / (1.0 - ADAM_B1 ** ADAM_STEP)
        v_hat = nv / (1.0 - ADAM_B2 ** ADAM_STEP)
        d_ref[...] = -ADAM_LR * (m_hat / (jnp.sqrt(v_hat) + ADAM_EPS) + ADAM_WD * w_ref[...])
        nm_ref[...] = nm
        nv_ref[...] = nv

    blk = pl.BlockSpec((tr, cols), lambda i: (i, 0))
    out = jax.ShapeDtypeStruct((rows, cols), F32)
    outs = pl.pallas_call(
        body, grid=(rows // tr,), in_specs=[blk] * 4, out_specs=(blk,) * 3, out_shape=(out,) * 3,
        name=name, compiler_params=_params(1),
    )(w2, g2, m2, v2)
    return tuple(o.reshape(shape) for o in outs)


def adamw_summed(w, m, v, grads, from_sibling, received, me, name):
    rows, cols = w.shape[-2:]
    tr, tc = _tile2(rows, cols)

    def body(me_ref, w_ref, m_ref, v_ref, own_ref, sib_ref, r_ref, g_ref, d_ref, nm_ref, nv_ref):
        gv = own_ref[0] + sib_ref[0]
        for j in range(3):
            gv = gv + r_ref[j].astype(F32)
        nm = ADAM_B1 * m_ref[0] + (1.0 - ADAM_B1) * gv
        nv = ADAM_B2 * v_ref[0] + (1.0 - ADAM_B2) * (gv * gv)
        m_hat = nm / (1.0 - ADAM_B1 ** ADAM_STEP)
        v_hat = nv / (1.0 - ADAM_B2 ** ADAM_STEP)
        g_ref[0] = gv
        d_ref[0] = -ADAM_LR * (m_hat / (jnp.sqrt(v_hat) + ADAM_EPS) + ADAM_WD * w_ref[0])
        nm_ref[0] = nm
        nv_ref[0] = nv

    one = pl.BlockSpec((1, tr, tc), lambda i, j, me_ref: (0, i, j))
    out = jax.ShapeDtypeStruct((1, rows, cols), F32)
    return pl.pallas_call(
        body,
        grid_spec=pltpu.PrefetchScalarGridSpec(
            num_scalar_prefetch=1, grid=(rows // tr, cols // tc),
            in_specs=[one, one, one, pl.BlockSpec((1, tr, tc), lambda i, j, me_ref: (me_ref[0], i, j)),
                      pl.BlockSpec((1, tr, tc), lambda i, j, me_ref: (me_ref[1], i, j)),
                      pl.BlockSpec((3, tr, tc), lambda i, j, me_ref: (0, i, j))],
            out_specs=(one,) * 4),
        out_shape=(out,) * 4, name=name, compiler_params=_params(2),
    )(me, w, m, v, grads, from_sibling, received)


SMALL_VECTORS = ("ffn1_norm", "mix_norm", "ffn2_norm", "final_norm")


def _pack_small(gs):
    row = jnp.concatenate([gs["gdn_a_log"].reshape(-1), gs["gdn_dt_bias"].reshape(-1), gs["gdn_out_norm"].reshape(-1)])
    rows = [gs[n].reshape(1, D_MODEL) for n in SMALL_VECTORS]
    rows.append(jnp.pad(row, (0, D_MODEL - row.shape[0])).reshape(1, D_MODEL))
    rows.append(gs["gdn_conv_w"].reshape(-1, D_MODEL))
    packed = jnp.concatenate(rows, axis=0)
    return jnp.pad(packed, ((0, SMALL_ROWS - packed.shape[0]), (0, 0)))


def _unpack_small(packed):
    out = {n: packed[i].reshape(1, D_MODEL) for i, n in enumerate(SMALL_VECTORS)}
    row = packed[len(SMALL_VECTORS)]
    out["gdn_a_log"] = row[:GDN_HEADS].reshape(1, GDN_HEADS)
    out["gdn_dt_bias"] = row[GDN_HEADS:2 * GDN_HEADS].reshape(1, GDN_HEADS)
    out["gdn_out_norm"] = row[2 * GDN_HEADS:2 * GDN_HEADS + GDN_HEAD_DIM].reshape(1, GDN_HEAD_DIM)
    first = len(SMALL_VECTORS) + 1
    out["gdn_conv_w"] = packed[first:first + GDN_CONV * 3].reshape(GDN_CONV, 3 * GDN_WIDTH)
    return out


WEIGHTS = ("ffn1_norm", "ffn1_w_gate", "ffn1_w_up", "ffn1_w_down", "mix_norm", "w_in", "gdn_conv_w", "gdn_a_log",
           "gdn_dt_bias", "gdn_out_norm", "w_branch_a", "w_branch_b", "w_out", "ffn2_norm", "ffn2_w_gate",
           "ffn2_w_up", "ffn2_w_down", "final_norm")


def kernel(x, ffn1_norm, ffn1_w_gate, ffn1_w_up, ffn1_w_down, mix_norm, w_in, gdn_conv_w, gdn_a_log, gdn_dt_bias, gdn_out_norm, w_branch_a, w_branch_b, w_out, ffn2_norm, ffn2_w_gate, ffn2_w_up, ffn2_w_down, final_norm, loss_target, m_ffn1_norm, m_ffn1_w_gate, m_ffn1_w_up, m_ffn1_w_down, m_mix_norm, m_w_in, m_gdn_conv_w, m_gdn_a_log, m_gdn_dt_bias, m_gdn_out_norm, m_w_branch_a, m_w_branch_b, m_w_out, m_ffn2_norm, m_ffn2_w_gate, m_ffn2_w_up, m_ffn2_w_down, m_final_norm, v_ffn1_norm, v_ffn1_w_gate, v_ffn1_w_up, v_ffn1_w_down, v_mix_norm, v_w_in, v_gdn_conv_w, v_gdn_a_log, v_gdn_dt_bias, v_gdn_out_norm, v_w_branch_a, v_w_branch_b, v_w_out, v_ffn2_norm, v_ffn2_w_gate, v_ffn2_w_up, v_ffn2_w_down, v_final_norm):
    given = dict(locals())
    px, py, pc = _position()
    big_names = list(BIG_WEIGHTS)

    def shard_view(a, n):
        if n == "w_in":
            return a.transpose(2, 0, 1)
        return a.transpose(0, 2, 1) if n in TRANSPOSED else a

    def shard_unview(a, n):
        if n == "w_in":
            return a.transpose(1, 2, 0)
        return a.transpose(0, 2, 1) if n in TRANSPOSED else a

    me = 4 * px + 2 * py + pc
    me_index = me.astype(jnp.int32).reshape(1)
    late = [n for n in big_names if n.startswith("ffn2")]
    early = [n for n in big_names if n not in late]
    shards = {n: shard_view(given[n], n).reshape(given[n].shape[-1 if n in TRANSPOSED else -2], -1).astype(BF16)
              for n in big_names}
    first = [n for n in early if n.startswith("ffn1")]
    middle = [n for n in early if n not in first]
    first_slabs, first_done = all_gather_shards([shards[n] for n in first], "gather_ffn1")
    shards["gdn_conv_w"] = gdn_conv_w[0]
    middle_all = middle + ["gdn_conv_w"]
    middle_gather = direct_exchange_start([_after(shards[n], first_done) for n in middle_all], "near",
                                          "gather_mixer_start")
    ffn1_norm = _after(ffn1_norm, middle_gather[4])
    w = dict(zip(first, first_slabs))
    x1, ffn1_saved = ffn_forward(x[0], ffn1_norm, w, "ffn1")
    near_lands = direct_exchange_wait(*middle_gather[:4], x1, "near", "gather_mixer_wait")[:-1]
    near_lands = [lax.dynamic_update_slice(land, shards[n][None], (me, 0, 0)) for n, land in zip(middle_all, near_lands)]
    middle_slabs, middle_done = forward_to_sibling(near_lands, "gather_mixer_forward")
    gathered = dict(zip(middle_all, middle_slabs))
    late_gather = direct_exchange_start([_after(shards[n], middle_done) for n in late], "gather", "gather_ffn2_start")
    w["w_in_t"] = gathered["w_in"].reshape(-1, D_MODEL)
    w["w_branch_a"] = gathered["w_branch_a"].transpose(1, 0, 2).reshape(256, D_MODEL)
    w["w_branch_b"] = gathered["w_branch_b"].reshape(D_MODEL, D_MODEL)
    w["w_out"] = gathered["w_out"].reshape(D_MODEL, D_MODEL)
    conv_full = gathered["gdn_conv_w"].transpose(1, 0, 2).reshape(GDN_CONV, 3 * GDN_WIDTH)
    small = dict(mix_norm=_after(mix_norm, late_gather[4]), gdn_a_log=gdn_a_log, gdn_dt_bias=gdn_dt_bias,
                 gdn_out_norm=gdn_out_norm, gdn_conv_w=conv_full)

    x2, mixer_saved = mixer_forward(x1, w, small)
    late_lands = direct_exchange_wait(*late_gather[:4], x2, "gather", "gather_ffn2_wait")
    for n, land in zip(late, late_lands):
        w[n] = lax.dynamic_update_slice(land, shards[n][None], (me, 0, 0))
    x3, ffn2_saved = ffn_forward(x2, ffn2_norm, w, "ffn2")
    loss_local, dx3, g_final = loss_head(x3, loss_target[0], final_norm.reshape(1, D_MODEL))
    loss = lax.psum(loss_local, ("x", "y", "c"))
    dx2, g_ffn2_norm, (dw2, dw2_own) = ffn_backward(dx3, ffn2_saved, ffn2_norm, w, "ffn2", me_index)
    late_scatter = direct_exchange_start(list(dw2), "scatter", "rs_ffn2_start")
    w_after = dict(w, w_out=_after(w["w_out"], late_scatter[4]))
    dx1, g_w = mixer_backward(dx2, mixer_saved, w_after, small)
    middle = ["w_in", "w_branch_a", "w_branch_b", "w_out"]
    g_big = dict(w_in=g_w["w_in_t"].reshape(N_DEV, -1, D_MODEL),
                 w_branch_a=g_w["w_branch_a"].reshape(256, N_DEV, 128).transpose(1, 0, 2),
                 w_branch_b=g_w["w_branch_b"].reshape(N_DEV, 128, D_MODEL),
                 w_out=g_w["w_out"].reshape(N_DEV, 128, D_MODEL))
    own = dict(zip(late, dw2_own))
    own.update({n: lax.dynamic_index_in_dim(g_big[n], me, 0, keepdims=True) for n in middle[1:]})
    in_rows = g_w["w_in_t"].shape[0] // N_DEV
    own["w_in"] = lax.dynamic_slice(g_w["w_in_t"], (me * in_rows, 0), (in_rows, D_MODEL))[None]
    middle_scatter = direct_exchange_start([g_big[n].astype(BF16) for n in middle], "scatter", "rs_mixer_start")
    grad_x, g_ffn1_norm, dw1 = ffn_backward(dx1, ffn1_saved, _after(ffn1_norm, middle_scatter[4]), w, "ffn1")
    g_small = dict(ffn1_norm=g_ffn1_norm, ffn2_norm=g_ffn2_norm, final_norm=g_final,
                   **{n: g_w[n] for n in ("mix_norm", "gdn_a_log", "gdn_dt_bias", "gdn_out_norm", "gdn_conv_w")})

    first = [n for n in early if n.startswith("ffn1")]
    g_list = list(dw1)
    core = pc.astype(jnp.int32).reshape(1)
    me_and_chip = jnp.stack([me, 2 * px + py]).astype(jnp.int32)
    from_sibling = exchange_with_sibling(g_list)
    partials = [add_sibling(g, r, core, "rs_add_" + n) for n, g, r in zip(first, g_list, from_sibling)]
    first_chips = direct_exchange_start(partials, "chips", "rs_ffn1_start")

    def state_of(n):
        return [shard_view(given[p + n], n) for p in ("", "m_", "v_")]

    results = {}
    late_received = direct_exchange_wait(*late_scatter[:4], first_chips[4], "scatter", "rs_ffn2_wait")
    middle_received = direct_exchange_wait(*middle_scatter[:4], first_chips[4], "scatter", "rs_mixer_wait")
    for n, recv in zip(late + middle, list(late_received[:-1]) + list(middle_received[:-1])):
        outs = adamw_direct(*state_of(n), own[n], recv, "adamw_" + n)
        results[n] = tuple(shard_unview(o, n) for o in outs)

    done = results["w_out"][1]
    from_chips = direct_exchange_wait(*first_chips[:4], done, "chips", "rs_ffn1_wait")
    for n, g, sib, recv in zip(first, g_list, from_sibling, from_chips):
        outs = adamw_summed(*state_of(n), g, sib, recv, me_and_chip, "adamw_" + n)
        results[n] = tuple(shard_unview(o, n) for o in outs)

    small_sum = _unpack_small(all_reduce_small(_after(_pack_small(g_small), from_chips[-1])))
    conv_cols = CONV_SHARD[1]
    small_sum["gdn_conv_w"] = lax.dynamic_slice(small_sum["gdn_conv_w"], (0, me * conv_cols), (GDN_CONV, conv_cols))
    for n in WEIGHTS:
        if n not in results:
            g = small_sum[n].reshape(given[n].shape)
            results[n] = (g,) + adamw(given[n], g, given["m_" + n], given["v_" + n], "adamw_" + n)

    outs = [[results[n][i] for n in WEIGHTS] for i in range(4)]
    return (loss, grad_x[None], *outs[0], *outs[1], *outs[2], *outs[3])
```

```python
import jax
import jax.numpy as jnp
from jax import lax
from jax.experimental import pallas as pl
from jax.experimental.pallas import tpu as pltpu

F32 = jnp.float32
BF16 = jnp.bfloat16
HI = lax.Precision.HIGHEST
MESH = pl.DeviceIdType.MESH

N_DEV = 8
D_MODEL = 1024
EPS = 1e-6
ROPE_THETA = 10000.0
DSW_DILATIONS = (1, 4, 16)
DSW_HEADS_PER_GROUP = 4
DSW_HEAD_DIM = 64
DSW_BLOCK = 128
GDN_HEADS = 8
GDN_HEAD_DIM = 128
GDN_WIDTH = 1024
GDN_CONV = 4
GDN_CHUNK = 64

ADAM_LR = 0.001
ADAM_B1 = 0.9
ADAM_B2 = 0.999
ADAM_EPS = 1e-08
ADAM_WD = 0.01
ADAM_STEP = 10

VMEM_LIMIT_BYTES = 56 * 1024 * 1024
LANES = 128

NN = (((1,), (0,)), ((), ()))
NT = (((1,), (1,)), ((), ()))
TN = (((0,), (0,)), ((), ()))


def _params(n_grid):
    return pltpu.CompilerParams(dimension_semantics=("arbitrary",) * n_grid, vmem_limit_bytes=VMEM_LIMIT_BYTES)


def _tile(n, pref):
    best = None
    t = LANES
    while t <= min(n, pref):
        if n % t == 0:
            best = t
        t += LANES
    return n if best is None else best


def _matmul(a, b, *, name, ta=False, tb=False, res=None, scale=1.0):
    K, M = a.shape if ta else a.shape[::-1]
    N = b.shape[0] if tb else b.shape[1]
    assert (b.shape[1] if tb else b.shape[0]) == K, (a.shape, b.shape, ta, tb)
    tm = _tile(M, 512)
    tn = _tile(N, 512)
    dn = (((0 if ta else 1,), (1 if tb else 0,)), ((), ()))

    def body(*refs):
        a_ref, b_ref = refs[:2]
        o_ref = refs[-1]
        acc = lax.dot_general(a_ref[...].astype(BF16), b_ref[...].astype(BF16), dn, preferred_element_type=F32)
        if scale != 1.0:
            acc = acc * scale
        if res is not None:
            acc = refs[2][...] + acc
        o_ref[...] = acc

    a_spec = pl.BlockSpec((K, tm), lambda i, j: (0, i)) if ta else pl.BlockSpec((tm, K), lambda i, j: (i, 0))
    b_spec = pl.BlockSpec((tn, K), lambda i, j: (j, 0)) if tb else pl.BlockSpec((K, tn), lambda i, j: (0, j))
    o_spec = pl.BlockSpec((tm, tn), lambda i, j: (i, j))
    ins, specs = [a, b], [a_spec, b_spec]
    if res is not None:
        ins.append(res)
        specs.append(o_spec)
    return pl.pallas_call(
        body, grid=(M // tm, N // tn), in_specs=specs, out_specs=o_spec,
        out_shape=jax.ShapeDtypeStruct((M, N), F32), name=name, compiler_params=_params(2),
    )(*ins)


def _rw_specs(arrs, tm, nblk):
    return [pl.BlockSpec((tm, a.shape[1] // nblk), lambda i, j: (i, j)) for a in arrs]


def _rowwise_fwd(fn, name, rows, consts, params, tm, nblk):
    n_rows = rows[0].shape[0]
    tm = min(tm, n_rows)
    ins = list(rows) + list(consts)
    avals = [jax.ShapeDtypeStruct((tm, a.shape[1] // nblk), a.dtype) for a in ins]
    avals += [jax.ShapeDtypeStruct(p.shape, p.dtype) for p in params]
    out_avals = jax.eval_shape(fn, *avals)
    n_in = len(ins) + len(params)

    def body(*refs):
        outs = fn(*[r[...] for r in refs[:n_in]])
        for r, o in zip(refs[n_in:], outs):
            r[...] = o.astype(r.dtype)

    return pl.pallas_call(
        body, grid=(n_rows // tm, nblk),
        in_specs=_rw_specs(ins, tm, nblk) + [pl.BlockSpec(p.shape, lambda i, j: (0, 0)) for p in params],
        out_specs=tuple(pl.BlockSpec((tm, o.shape[1]), lambda i, j: (i, j)) for o in out_avals),
        out_shape=tuple(jax.ShapeDtypeStruct((n_rows, o.shape[1] * nblk), o.dtype) for o in out_avals),
        name=name, compiler_params=_params(2),
    )(*ins, *params)


def _rowwise_bwd(fn, name, rows, consts, params, cts, tm, nblk):
    n_rows = rows[0].shape[0]
    tm = min(tm, n_rows)
    nr, nc, npar, nct = len(rows), len(consts), len(params), len(cts)

    def body(*refs):
        rv = [r[...] for r in refs[:nr]]
        cv = [r[...] for r in refs[nr:nr + nc]]
        pv = [r[...] for r in refs[nr + nc:nr + nc + npar]]
        ctv = [r[...] for r in refs[nr + nc + npar:nr + nc + npar + nct]]
        outs = refs[nr + nc + npar + nct:]
        _, vjp = jax.vjp(lambda *d: fn(*d[:nr], *cv, *d[nr:]), *rv, *pv)
        grads = vjp(tuple(ctv))
        for k in range(nr):
            outs[k][...] = grads[k]
        first = jnp.logical_and(pl.program_id(0) == 0, pl.program_id(1) == 0)
        for k in range(npar):
            ref = outs[nr + k]

            @pl.when(first)
            def _(ref=ref):
                ref[...] = jnp.zeros_like(ref)

            ref[...] += grads[nr + k]

    ins = list(rows) + list(consts)
    return pl.pallas_call(
        body, grid=(n_rows // tm, nblk),
        in_specs=(_rw_specs(ins, tm, nblk) + [pl.BlockSpec(p.shape, lambda i, j: (0, 0)) for p in params]
                  + _rw_specs(cts, tm, nblk)),
        out_specs=tuple(_rw_specs(rows, tm, nblk) + [pl.BlockSpec(p.shape, lambda i, j: (0, 0)) for p in params]),
        out_shape=tuple([jax.ShapeDtypeStruct(a.shape, F32) for a in rows]
                        + [jax.ShapeDtypeStruct(p.shape, F32) for p in params]),
        name=name, compiler_params=_params(2),
    )(*ins, *params, *cts)


def _merge_fn(ga, gb, pa, pb):
    return (jax.nn.sigmoid(ga) * pa + jax.nn.sigmoid(gb) * pb,)


def _outnorm_gate_fn(o, gate, gain):
    y = o * lax.rsqrt(jnp.mean(o * o, axis=-1, keepdims=True) + EPS) * gain
    return (y * (gate * jax.nn.sigmoid(gate)),)


def _beta_decay_fn(beta_raw, decay_raw, a_log, dt_bias):
    z = decay_raw + dt_bias
    softplus = jnp.maximum(z, 0.0) + jnp.log(1.0 + jnp.exp(-jnp.abs(z)))
    g = -jnp.exp(a_log) * softplus
    rows = g.shape[0]
    ii = lax.broadcasted_iota(jnp.int32, (rows, rows), 0)
    jj = lax.broadcasted_iota(jnp.int32, (rows, rows), 1)
    same_chunk_before = jnp.logical_and(jj <= ii, jj // GDN_CHUNK == ii // GDN_CHUNK).astype(F32)
    gcum = lax.dot_general(same_chunk_before, g, NN, precision=HI, preferred_element_type=F32)
    return jax.nn.sigmoid(beta_raw), gcum


def _combine_fn(o0, o1, o2, l0, l1, l2):
    m = lax.stop_gradient(jnp.maximum(jnp.maximum(l0, l1), l2))
    e0, e1, e2 = jnp.exp(l0 - m), jnp.exp(l1 - m), jnp.exp(l2 - m)
    return ((e0 * o0 + e1 * o1 + e2 * o2) / (e0 + e1 + e2),)


def _loss_fn(x, target, gain):
    y = x * lax.rsqrt(jnp.mean(x * x, axis=-1, keepdims=True) + EPS) * gain
    err = y - target
    return (0.5 * jnp.mean(err * err, axis=-1, keepdims=True),)


def _rotate(v, cos, sin):
    half = DSW_HEAD_DIM // 2
    lane = lax.broadcasted_iota(jnp.int32, cos.shape, 1)
    low = (lane % DSW_HEAD_DIM) < half
    slabs = []
    for s in range(v.shape[1] // LANES):
        x = v[:, s * LANES:(s + 1) * LANES]
        swapped = jnp.where(low, pltpu.roll(x, LANES - half, 1), pltpu.roll(x, half, 1))
        slabs.append(x * cos + swapped * sin)
    return jnp.concatenate(slabs, axis=1)


def _rope_tables(n_tokens):
    half = DSW_HEAD_DIM // 2
    inv_freq = ROPE_THETA ** (-jnp.arange(half, dtype=F32) / half)
    ang = jnp.arange(n_tokens, dtype=F32)[:, None] * inv_freq[None, :]
    cos, sin = jnp.cos(ang), jnp.sin(ang)
    return jnp.tile(jnp.concatenate([cos, cos], 1), (1, 2)), jnp.tile(jnp.concatenate([-sin, sin], 1), (1, 2))


def _attn_probs(q, kp, kc, group, n):
    blk = DSW_BLOCK
    k = _each(lambda a, b: jnp.concatenate([a, b], axis=0).astype(BF16), kp, kc)
    s = _each(lambda a, b: lax.dot_general(a.astype(BF16), b, NT, preferred_element_type=F32)
              * (DSW_HEAD_DIM ** -0.5), q, k)
    blocks_per_seq = jnp.where(group == 0, 16, jnp.where(group == 1, 4, 1))
    first = (n % blocks_per_seq) == 0
    qi = lax.broadcasted_iota(jnp.int32, (blk, 2 * blk), 0)
    kj = lax.broadcasted_iota(jnp.int32, (blk, 2 * blk), 1)
    dist = qi + blk - kj
    valid = (dist >= 0) & (dist <= blk) & jnp.logical_or(kj >= blk, jnp.logical_not(first))
    s = _each(lambda a: jnp.where(valid, a, -1e30), s)
    m = _each(lambda a: jnp.max(a, axis=-1, keepdims=True), s)
    p = _each(lambda a, b: jnp.exp(a - b), s, m)
    l = _each(lambda a: jnp.sum(a, axis=-1, keepdims=True), p)
    return _each(lambda a, b: a / b, p, l), _each(lambda a, b: a + jnp.log(b), m, l), k


PAIRS_PER_GROUP = DSW_HEADS_PER_GROUP // 2


def _attn_specs(n_tokens):
    blk = DSW_BLOCK
    cur = pl.BlockSpec((PAIRS_PER_GROUP, blk, LANES), lambda g, n: (g, n, 0))
    prev = pl.BlockSpec((PAIRS_PER_GROUP, blk, LANES), lambda g, n: (g, jnp.maximum(n - 1, 0), 0))
    return cur, prev


def _heads_of(ref):
    pairs = [ref[p] for p in range(PAIRS_PER_GROUP)]
    return [x[:, s * DSW_HEAD_DIM:(s + 1) * DSW_HEAD_DIM] for x in pairs for s in range(2)]


def _pairs_of(heads):
    return [jnp.concatenate(heads[2 * p:2 * p + 2], axis=1) for p in range(PAIRS_PER_GROUP)]


def _attn_fwd(q, k, v):
    n_pairs, n_tokens, _ = q.shape
    cur, prev = _attn_specs(n_tokens)

    def body(q_ref, kp_ref, kc_ref, vp_ref, vc_ref, o_ref, l_ref):
        p, lse, _ = _attn_probs(_heads_of(q_ref), _heads_of(kp_ref), _heads_of(kc_ref),
                                pl.program_id(0), pl.program_id(1))
        vv = _each(lambda a, b: jnp.concatenate([a, b], axis=0).astype(BF16), _heads_of(vp_ref), _heads_of(vc_ref))
        o = _each(lambda a, b: lax.dot_general(a.astype(BF16), b, NN, preferred_element_type=F32), p, vv)
        lse_wide = _each(lambda a: jnp.broadcast_to(a, (DSW_BLOCK, DSW_HEAD_DIM)), lse)
        for pair, (o_pair, l_pair) in enumerate(zip(_pairs_of(o), _pairs_of(lse_wide))):
            o_ref[pair] = o_pair
            l_ref[pair] = l_pair

    return pl.pallas_call(
        body, grid=(n_pairs // PAIRS_PER_GROUP, n_tokens // DSW_BLOCK), in_specs=[cur, prev, cur, prev, cur],
        out_specs=(cur, cur), out_shape=(jax.ShapeDtypeStruct(q.shape, F32), jax.ShapeDtypeStruct(q.shape, F32)),
        name="attn_fwd", compiler_params=_params(2),
    )(q, k, k, v, v)


def _attn_bwd(q, k, v, do, dlse):
    n_pairs, n_tokens, _ = q.shape
    nblk = n_tokens // DSW_BLOCK
    cur, prev = _attn_specs(n_tokens)
    part = pl.BlockSpec((PAIRS_PER_GROUP, 1, 2 * DSW_BLOCK, LANES), lambda g, n: (g, n, 0, 0))
    scale = DSW_HEAD_DIM ** -0.5

    def body(q_ref, kp_ref, kc_ref, vp_ref, vc_ref, do_ref, dl_ref, dq_ref, dk_ref, dv_ref):
        qs = _heads_of(q_ref)
        p, _, kb = _attn_probs(qs, _heads_of(kp_ref), _heads_of(kc_ref), pl.program_id(0), pl.program_id(1))
        qb = _each(lambda a: a.astype(BF16), qs)
        vv = _each(lambda a, b: jnp.concatenate([a, b], axis=0).astype(BF16), _heads_of(vp_ref), _heads_of(vc_ref))
        dob = _each(lambda a: a.astype(BF16), _heads_of(do_ref))
        dp = _each(lambda a, b: lax.dot_general(a, b, NT, preferred_element_type=F32), dob, vv)
        dv = _each(lambda a, b: lax.dot_general(a.astype(BF16), b, TN, preferred_element_type=F32), p, dob)
        dl = _each(lambda a: jnp.sum(a, axis=-1, keepdims=True), _heads_of(dl_ref))
        ds = _each(lambda a, b, c: (a * (b - jnp.sum(b * a, axis=-1, keepdims=True) + c) * scale).astype(BF16),
                   p, dp, dl)
        dq = _each(lambda a, b: lax.dot_general(a, b, NN, preferred_element_type=F32), ds, kb)
        dk = _each(lambda a, b: lax.dot_general(a, b, TN, preferred_element_type=F32), ds, qb)
        for pair, (dq_pair, dk_pair, dv_pair) in enumerate(zip(_pairs_of(dq), _pairs_of(dk), _pairs_of(dv))):
            dq_ref[pair] = dq_pair
            dk_ref[pair, 0] = dk_pair
            dv_ref[pair, 0] = dv_pair

    partial_shape = jax.ShapeDtypeStruct((n_pairs, nblk, 2 * DSW_BLOCK, LANES), F32)
    dq, dkp, dvp = pl.pallas_call(
        body, grid=(n_pairs // PAIRS_PER_GROUP, nblk), in_specs=[cur, prev, cur, prev, cur, cur, cur],
        out_specs=(cur, part, part), out_shape=(jax.ShapeDtypeStruct(q.shape, F32), partial_shape, partial_shape),
        name="attn_bwd", compiler_params=_params(2),
    )(q, k, k, v, v, do, dlse)

    def fold(partial):
        own = partial[:, :, DSW_BLOCK:]
        from_next = jnp.pad(partial[:, 1:, :DSW_BLOCK], ((0, 0), (0, 1), (0, 0), (0, 0)))
        return (own + from_next).reshape(n_pairs, n_tokens, LANES)

    return dq, fold(dkp), fold(dvp)


def _to_heads(a):
    n_tokens = a.shape[0]
    outs = []
    for gi, d in enumerate(DSW_DILATIONS):
        blk = a[:, gi * 256:(gi + 1) * 256].reshape(n_tokens // d, d, PAIRS_PER_GROUP, LANES)
        outs.append(blk.transpose(2, 1, 0, 3).reshape(PAIRS_PER_GROUP, n_tokens, LANES))
    return jnp.concatenate(outs, 0)


def _from_heads(a):
    n_tokens = a.shape[1]
    outs = []
    for gi, d in enumerate(DSW_DILATIONS):
        blk = a[gi * PAIRS_PER_GROUP:(gi + 1) * PAIRS_PER_GROUP].reshape(PAIRS_PER_GROUP, d, n_tokens // d, LANES)
        outs.append(blk.transpose(2, 1, 0, 3).reshape(n_tokens, PAIRS_PER_GROUP * LANES))
    return outs


CONV_TILE = 512


def _shift_down(x, k, rows):
    return x if k == 0 else jnp.where(rows >= k, pltpu.roll(x, k, 0), 0.0)


def _shift_up(x, k, rows):
    n = x.shape[0]
    return x if k == 0 else jnp.where(rows < n - k, pltpu.roll(x, n - k, 0), 0.0)


def _conv_pre(x, w):
    rows = lax.broadcasted_iota(jnp.int32, x.shape, 0)
    acc = x * w[GDN_CONV - 1:GDN_CONV]
    for k in range(1, GDN_CONV):
        acc = acc + _shift_down(x, k, rows) * w[GDN_CONV - 1 - k:GDN_CONV - k]
    return acc, rows


def _conv_fwd(x, w):
    n_tokens, width = x.shape
    big = pl.BlockSpec((n_tokens, CONV_TILE), lambda j: (0, j))
    wsp = pl.BlockSpec((GDN_CONV, CONV_TILE), lambda j: (0, j))

    def body(x_ref, w_ref, o_ref):
        acc, _ = _conv_pre(x_ref[...], w_ref[...])
        o_ref[...] = acc * jax.nn.sigmoid(acc)

    return pl.pallas_call(
        body, grid=(width // CONV_TILE,), in_specs=[big, wsp], out_specs=big,
        out_shape=jax.ShapeDtypeStruct(x.shape, F32), name="conv_fwd", compiler_params=_params(1),
    )(x, w)


def _conv_bwd(x, w, dy):
    n_tokens, width = x.shape
    big = pl.BlockSpec((n_tokens, CONV_TILE), lambda j: (0, j))
    wsp = pl.BlockSpec((GDN_CONV, CONV_TILE), lambda j: (0, j))

    def body(x_ref, w_ref, dy_ref, dx_ref, dw_ref):
        xv, wv = x_ref[...], w_ref[...]
        acc, rows = _conv_pre(xv, wv)
        sg = jax.nn.sigmoid(acc)
        dacc = dy_ref[...] * (sg + acc * sg * (1.0 - sg))
        dx = dacc * wv[GDN_CONV - 1:GDN_CONV]
        for k in range(1, GDN_CONV):
            dx = dx + _shift_up(dacc, k, rows) * wv[GDN_CONV - 1 - k:GDN_CONV - k]
        dx_ref[...] = dx
        for k in range(GDN_CONV):
            dw_ref[GDN_CONV - 1 - k:GDN_CONV - k, :] = jnp.sum(dacc * _shift_down(xv, k, rows), axis=0, keepdims=True)

    return pl.pallas_call(
        body, grid=(width // CONV_TILE,), in_specs=[big, wsp, big], out_specs=(big, wsp),
        out_shape=(jax.ShapeDtypeStruct(x.shape, F32), jax.ShapeDtypeStruct(w.shape, F32)),
        name="conv_bwd", compiler_params=_params(1),
    )(x, w, dy)


def _dot(a, b, dn=NN):
    return lax.dot_general(a, b, dn, precision=HI, preferred_element_type=F32)


def _dot3(a, b, dn=NN):
    return lax.dot_general(a, b, dn, precision=lax.Precision.HIGH, preferred_element_type=F32)


def _bf16_dot(a, b, dn):
    return lax.dot_general(a.astype(BF16), b.astype(BF16), dn, preferred_element_type=F32)


_DOT_GRADS = {NN: (("g", "b", NT), ("a", "g", TN)), NT: (("g", "b", NN), ("g", "a", TN)),
              TN: (("b", "g", NT), ("a", "g", NN))}


def _make_bdot(dn):
    @jax.custom_vjp
    def op(a, b):
        return _bf16_dot(a, b, dn)

    def fwd(a, b):
        return op(a, b), (a, b)

    def bwd(saved, g):
        vals = dict(a=saved[0], b=saved[1], g=g)
        return tuple(_bf16_dot(vals[x], vals[y], form) for x, y, form in _DOT_GRADS[dn])

    op.defvjp(fwd, bwd)
    return op


_BDOTS = {dn: _make_bdot(dn) for dn in (NN, NT, TN)}


def _bdot(a, b, dn=NN):
    return _BDOTS[dn](a, b)


def _each(fn, *lists):
    return [fn(*items) for items in zip(*lists)]


@jax.custom_vjp
def _known_inverse(m, inverse):
    return inverse


def _known_inverse_fwd(m, inverse):
    return inverse, inverse


def _known_inverse_bwd(inverse, d_inverse):
    return -_dot3(_dot3(inverse, d_inverse, TN), inverse, NT), jnp.zeros_like(inverse)


_known_inverse.defvjp(_known_inverse_fwd, _known_inverse_bwd)


def _gdn_chunks(q, k, v, b, gcum, state, inverse=None):
    c = GDN_CHUNK
    ii = lax.broadcasted_iota(jnp.int32, (c, c), 0)
    jj = lax.broadcasted_iota(jnp.int32, (c, c), 1)
    qn = _each(lambda x: x * lax.rsqrt(jnp.sum(x * x, axis=-1, keepdims=True) + EPS) * (GDN_HEAD_DIM ** -0.5), q)
    kn = _each(lambda x: x * lax.rsqrt(jnp.sum(x * x, axis=-1, keepdims=True) + EPS), k)
    gcum_i = _each(lambda x: jnp.broadcast_to(x, (c, c)), gcum)
    gcum_j = _each(jnp.transpose, gcum_i)
    decay = _each(lambda x, y: jnp.exp(jnp.where(jj <= ii, x - y, -1e30)), gcum_i, gcum_j)
    g_last = _each(lambda x: x[c - 1:c, :], gcum)
    e_gcum = _each(jnp.exp, gcum)
    kbeta = _each(lambda x, y: x * y, kn, b)
    vbeta = _each(lambda x, y: x * y, v, b)
    m = _each(lambda x, y, d: jnp.where(jj < ii, _bdot(x, y, NT) * d, 0.0), kbeta, kn, decay)
    if inverse is not None:
        inv = _each(_known_inverse, m, inverse)
    else:
        eye = (ii == jj).astype(F32)
        inv = _each(lambda x: eye - x, m)
        power = _each(lambda x: _dot3(x, x), m)
        for step in range(5):
            inv = _each(lambda x, p: x + _dot3(x, p), inv, power)
            if step < 4:
                power = _each(lambda p: _dot3(p, p), power)
    u = _each(_dot3, inv, vbeta)
    w = _each(lambda x, y, e: _dot3(x, y * e), inv, kbeta, e_gcum)
    a_qk = _each(lambda x, y, d: _bdot(x, y, NT) * d, qn, kn, decay)
    v_new = _each(lambda x, y, s: x - _bdot(y, s), u, w, state)
    o = _each(lambda x, e, s, a, vn: _bdot(x * e, s) + _bdot(a, vn), qn, e_gcum, state, a_qk, v_new)
    new_state = _each(lambda s, gl, x, gc, vn: s * jnp.exp(gl) + _bdot(x * jnp.exp(gl - gc), vn, TN),
                      state, g_last, kn, gcum, v_new)
    return o, new_state, inv


GDN_HEADS_PER_STEP = 8


GDN_TIME_TILE = 256


def _gdn_specs(n_tokens, reverse):
    hb, hd, tt = GDN_HEADS_PER_STEP, GDN_HEAD_DIM, GDN_TIME_TILE
    nb, nt = GDN_HEADS // hb, n_tokens // tt

    def when(t):
        return nt - 1 - t if reverse else t

    q = pl.BlockSpec((tt, hb * hd), lambda h, t: (when(t), h))
    k = pl.BlockSpec((tt, hb * hd), lambda h, t: (when(t), nb + h))
    v = pl.BlockSpec((tt, hb * hd), lambda h, t: (when(t), 2 * nb + h))
    vec = pl.BlockSpec((tt, hb), lambda h, t: (when(t), h))
    states = pl.BlockSpec((hb, tt // GDN_CHUNK, hd, hd), lambda h, t: (h, when(t), 0, 0))
    inverses = pl.BlockSpec((hb, tt // GDN_CHUNK, GDN_CHUNK, GDN_CHUNK), lambda h, t: (h, when(t), 0, 0))
    return q, k, v, vec, states, inverses


def _gdn_fwd(qkv, beta, g):
    n_tokens = qkv.shape[0]
    hb, hd, tt = GDN_HEADS_PER_STEP, GDN_HEAD_DIM, GDN_TIME_TILE
    n_chunks = tt // GDN_CHUNK
    q_s, k_s, v_s, vec, st, inv_s = _gdn_specs(n_tokens, False)

    def body(q_ref, k_ref, v_ref, b_ref, g_ref, o_ref, st_ref, inv_ref, state):
        @pl.when(pl.program_id(1) == 0)
        def _():
            state[...] = jnp.zeros_like(state)

        def step(c, carry):
            r = pl.ds(pl.multiple_of(c * GDN_CHUNK, GDN_CHUNK), GDN_CHUNK)
            cols = [slice(h * hd, (h + 1) * hd) for h in range(hb)]
            old = [state[h] for h in range(hb)]
            o, new, inv = _gdn_chunks(
                [q_ref[r, cs] for cs in cols], [k_ref[r, cs] for cs in cols], [v_ref[r, cs] for cs in cols],
                [b_ref[r, h:h + 1] for h in range(hb)], [g_ref[r, h:h + 1] for h in range(hb)], old)
            for h in range(hb):
                st_ref[h, c] = old[h]
                inv_ref[h, c] = inv[h]
                o_ref[r, cols[h]] = o[h]
                state[h] = new[h]
            return carry

        lax.fori_loop(0, n_chunks, step, 0)

    n_all = n_tokens // GDN_CHUNK
    return pl.pallas_call(
        body, grid=(GDN_HEADS // hb, n_tokens // tt), in_specs=[q_s, k_s, v_s, vec, vec], out_specs=(q_s, st, inv_s),
        out_shape=(jax.ShapeDtypeStruct((n_tokens, GDN_WIDTH), F32),
                   jax.ShapeDtypeStruct((GDN_HEADS, n_all, hd, hd), F32),
                   jax.ShapeDtypeStruct((GDN_HEADS, n_all, GDN_CHUNK, GDN_CHUNK), F32)),
        scratch_shapes=[pltpu.VMEM((hb, hd, hd), F32)],
        name="gdn_fwd", compiler_params=_params(2),
    )(qkv, qkv, qkv, beta, g)


def _gdn_bwd(qkv, beta, g, states, inverses, do):
    n_tokens = qkv.shape[0]
    hb, hd, tt = GDN_HEADS_PER_STEP, GDN_HEAD_DIM, GDN_TIME_TILE
    n_chunks = tt // GDN_CHUNK
    q_s, k_s, v_s, vec, st, inv_s = _gdn_specs(n_tokens, True)

    assert hb == GDN_HEADS

    def body(q_ref, k_ref, v_ref, b_ref, g_ref, st_ref, inv_ref, do_ref, dqkv_ref, db_ref, dg_ref, dstate):
        @pl.when(pl.program_id(1) == 0)
        def _():
            dstate[...] = jnp.zeros_like(dstate)

        def step(i, carry):
            c = n_chunks - 1 - i
            r = pl.ds(pl.multiple_of(c * GDN_CHUNK, GDN_CHUNK), GDN_CHUNK)
            cols = [slice(h * hd, (h + 1) * hd) for h in range(hb)]
            args = ([q_ref[r, cs] for cs in cols], [k_ref[r, cs] for cs in cols], [v_ref[r, cs] for cs in cols],
                    [b_ref[r, h:h + 1] for h in range(hb)], [g_ref[r, h:h + 1] for h in range(hb)],
                    [st_ref[h, c] for h in range(hb)])
            saved = [inv_ref[h, c] for h in range(hb)]
            cts = ([do_ref[r, cs] for cs in cols], [dstate[h] for h in range(hb)])
            dq, dk, dv, db, dg, dst = jax.vjp(lambda *a: _gdn_chunks(*a, inverse=saved)[:2], *args)[1](cts)
            for h in range(hb):
                for part, grad in enumerate((dq, dk, dv)):
                    dqkv_ref[r, pl.ds(part * GDN_WIDTH + h * hd, hd)] = grad[h]
                db_ref[r, h:h + 1] = db[h]
                dg_ref[r, h:h + 1] = dg[h]
                dstate[h] = dst[h]
            return carry

        lax.fori_loop(0, n_chunks, step, 0)

    n_t = n_tokens // tt
    thin = jax.ShapeDtypeStruct(beta.shape, F32)
    return pl.pallas_call(
        body, grid=(GDN_HEADS // hb, n_t), in_specs=[q_s, k_s, v_s, vec, vec, st, inv_s, q_s],
        out_specs=(pl.BlockSpec((tt, 3 * GDN_WIDTH), lambda h, t: (n_t - 1 - t, 0)), vec, vec),
        out_shape=(jax.ShapeDtypeStruct(qkv.shape, F32), thin, thin),
        scratch_shapes=[pltpu.VMEM((hb, hd, hd), F32)],
        name="gdn_bwd", compiler_params=_params(2),
    )(qkv, qkv, qkv, beta, g, states, inverses, do)


FFN_ROW_TILE = 256
FFN_FWD_ROW_TILE = 512


def _resident(shape):
    return pl.BlockSpec(shape, lambda i: (0,) * len(shape), pipeline_mode=pl.Buffered(1))


def _ffn_fwd(x, gain, wg, wu, wd, name):
    n_tokens, d = x.shape
    n_shards, n, _ = wg.shape
    tm = FFN_FWD_ROW_TILE

    def body(x_ref, gain_ref, wg_ref, wu_ref, wd_ref, o_ref, g_ref, u_ref):
        xv = x_ref[...]
        h = (xv * lax.rsqrt(jnp.mean(xv * xv, axis=-1, keepdims=True) + EPS) * gain_ref[...]).astype(BF16)
        acc = jnp.zeros((tm, d), F32)
        for j in range(n_shards):
            g = lax.dot_general(h, wg_ref[j], NT, preferred_element_type=F32)
            u = lax.dot_general(h, wu_ref[j], NT, preferred_element_type=F32)
            g_ref[j] = g
            u_ref[j] = u
            a = (g * jax.nn.sigmoid(g) * u).astype(BF16)
            acc = acc + lax.dot_general(a, wd_ref[j], NN, preferred_element_type=F32)
        o_ref[...] = xv + 0.5 * acc

    row = pl.BlockSpec((tm, d), lambda i: (i, 0))
    hid = pl.BlockSpec((n_shards, tm, n), lambda i: (0, i, 0))
    return pl.pallas_call(
        body, grid=(n_tokens // tm,),
        in_specs=[row, _resident(gain.shape), _resident(wg.shape), _resident(wu.shape), _resident(wd.shape)],
        out_specs=(row, hid, hid),
        out_shape=(jax.ShapeDtypeStruct(x.shape, F32), jax.ShapeDtypeStruct((n_shards, n_tokens, n), F32),
                   jax.ShapeDtypeStruct((n_shards, n_tokens, n), F32)),
        name=name, compiler_params=_params(1),
    )(x, gain, wg, wu, wd)


def _ffn_bwd_rows(x, gain, dy, g, u, wg, wu, wd, name):
    n_tokens, d = x.shape
    n_shards, n, _ = wg.shape
    tm = FFN_ROW_TILE

    def body(x_ref, gain_ref, dy_ref, g_ref, u_ref, wg_ref, wu_ref, wd_ref,
             dx_ref, dgain_ref, h_ref, dyh_ref, a_ref, dg_ref, du_ref):
        xv, dyv, gain_v = x_ref[...], dy_ref[...], gain_ref[...]
        r = lax.rsqrt(jnp.mean(xv * xv, axis=-1, keepdims=True) + EPS)
        xhat = xv * r
        h_ref[...] = (xhat * gain_v).astype(BF16)
        dyh = (0.5 * dyv).astype(BF16)
        dyh_ref[...] = dyh
        dh = jnp.zeros((tm, d), F32)
        for j in range(n_shards):
            da = lax.dot_general(dyh, wd_ref[j], NT, preferred_element_type=F32)
            gv, uv = g_ref[j], u_ref[j]
            sg = jax.nn.sigmoid(gv)
            silu = gv * sg
            a_ref[j] = (silu * uv).astype(BF16)
            dg = (da * uv * (sg + silu * (1.0 - sg))).astype(BF16)
            du = (da * silu).astype(BF16)
            dg_ref[j] = dg
            du_ref[j] = du
            dh = dh + lax.dot_general(dg, wg_ref[j], NN, preferred_element_type=F32)
            dh = dh + lax.dot_general(du, wu_ref[j], NN, preferred_element_type=F32)
        dxhat = dh * gain_v
        dx_ref[...] = dyv + r * (dxhat - xhat * jnp.mean(dxhat * xhat, axis=-1, keepdims=True))

        @pl.when(pl.program_id(0) == 0)
        def _():
            dgain_ref[...] = jnp.zeros_like(dgain_ref)

        dgain_ref[...] += jnp.sum(dh * xhat, axis=0, keepdims=True)

    row = pl.BlockSpec((tm, d), lambda i: (i, 0))
    hid = pl.BlockSpec((n_shards, tm, n), lambda i: (0, i, 0))
    hid_shape = (n_shards, n_tokens, n)
    return pl.pallas_call(
        body, grid=(n_tokens // tm,),
        in_specs=[row, _resident(gain.shape), row, hid, hid, _resident(wg.shape), _resident(wu.shape),
                  _resident(wd.shape)],
        out_specs=(row, pl.BlockSpec(gain.shape, lambda i: (0, 0)), row, row, hid, hid, hid),
        out_shape=(jax.ShapeDtypeStruct(x.shape, F32), jax.ShapeDtypeStruct(gain.shape, F32),
                   jax.ShapeDtypeStruct(x.shape, BF16), jax.ShapeDtypeStruct(x.shape, BF16),
                   jax.ShapeDtypeStruct(hid_shape, BF16), jax.ShapeDtypeStruct(hid_shape, BF16),
                   jax.ShapeDtypeStruct(hid_shape, BF16)),
        name=name, compiler_params=_params(1),
    )(x, gain, dy, g, u, wg, wu, wd)


def _ffn_bwd_weights(h, dyh, a, dg, du, name, owner=None):
    n_shards, n_tokens, n = a.shape
    d = h.shape[1]

    def products(h_ref, dyh_ref, a_ref, dg_ref, du_ref):
        hv = h_ref[...]
        return (lax.dot_general(dg_ref[0], hv, TN, preferred_element_type=F32),
                lax.dot_general(du_ref[0], hv, TN, preferred_element_type=F32),
                lax.dot_general(a_ref[0], dyh_ref[...], TN, preferred_element_type=F32))

    hid = pl.BlockSpec((1, n_tokens, n), lambda j, *_: (j, 0, 0))
    out = pl.BlockSpec((1, n, d), lambda j, *_: (j, 0, 0))
    ins = [pl.BlockSpec(h.shape, lambda j, *_: (0, 0), pipeline_mode=pl.Buffered(1)),
           pl.BlockSpec(dyh.shape, lambda j, *_: (0, 0), pipeline_mode=pl.Buffered(1)), hid, hid, hid]
    if owner is None:
        def body(*refs):
            for ref, val in zip(refs[5:], products(*refs[:5])):
                ref[0] = val

        return pl.pallas_call(
            body, grid=(n_shards,), in_specs=ins, out_specs=(out, out, out),
            out_shape=(jax.ShapeDtypeStruct((n_shards, n, d), F32),) * 3, name=name, compiler_params=_params(1),
        )(h, dyh, a, dg, du)

    def body(owner_ref, *refs):
        vals = products(*refs[:5])
        for ref, val in zip(refs[5:8], vals):
            ref[0] = val.astype(BF16)

        @pl.when(pl.program_id(0) == owner_ref[0])
        def _():
            for ref, val in zip(refs[8:], vals):
                ref[0] = val

    mine = pl.BlockSpec((1, n, d), lambda j, *_: (0, 0, 0))
    outs = pl.pallas_call(
        body,
        grid_spec=pltpu.PrefetchScalarGridSpec(num_scalar_prefetch=1, grid=(n_shards,), in_specs=ins,
                                               out_specs=(out, out, out, mine, mine, mine)),
        out_shape=(jax.ShapeDtypeStruct((n_shards, n, d), BF16),) * 3 + (jax.ShapeDtypeStruct((1, n, d), F32),) * 3,
        name=name, compiler_params=_params(1),
    )(owner, h, dyh, a, dg, du)
    return outs[:3], outs[3:]


IN_PIECES = (("wq_a", 0, 768), ("wk_a", 768, 1536), ("wv_a", 1536, 2304), ("w_qkvb", 2304, 5376),
             ("w_small", 5376, 5392), ("w_ggate", 5392, 6416), ("w_gatea", 6416, 7440), ("w_gateb", 7440, 8464))
IN_NAMES = tuple(name for name, _, _ in IN_PIECES)


def _in_rows(lo, hi):
    return lo, max(hi, lo + LANES)


N_ROTATED = 2


def _in_proj_fwd(x, gain, wt, cos, sin):
    n_tokens, d = x.shape
    tm = FFN_ROW_TILE
    rows = [_in_rows(lo, hi) for _, lo, hi in IN_PIECES]

    def body(x_ref, gain_ref, wt_ref, cos_ref, sin_ref, *o_refs):
        xv = x_ref[...]
        h = (xv * lax.rsqrt(jnp.mean(xv * xv, axis=-1, keepdims=True) + EPS) * gain_ref[...]).astype(BF16)
        for k, ((lo, hi), o_ref) in enumerate(zip(rows, o_refs)):
            z = lax.dot_general(h, wt_ref[lo:hi, :], NT, preferred_element_type=F32)
            o_ref[...] = _rotate(z, cos_ref[...], sin_ref[...]) if k < N_ROTATED else z

    tab = pl.BlockSpec((tm, LANES), lambda i: (i, 0))
    return pl.pallas_call(
        body, grid=(n_tokens // tm,),
        in_specs=[pl.BlockSpec((tm, d), lambda i: (i, 0)), _resident(gain.shape), _resident(wt.shape), tab, tab],
        out_specs=tuple(pl.BlockSpec((tm, hi - lo), lambda i: (i, 0)) for lo, hi in rows),
        out_shape=tuple(jax.ShapeDtypeStruct((n_tokens, hi - lo), F32) for lo, hi in rows),
        name="in_proj_fwd", compiler_params=_params(1),
    )(x, gain, wt, cos, sin)


def _in_proj_bwd_rows(x, gain, dres, dzs, wt, cos, sin):
    n_tokens, d = x.shape
    tm = FFN_ROW_TILE
    n = len(dzs)
    rows = [_in_rows(lo, hi) for _, lo, hi in IN_PIECES]

    def body(x_ref, gain_ref, dres_ref, cos_ref, sin_ref, *refs):
        dz_refs, wt_ref = refs[:n], refs[n]
        dx_ref, dgain_ref, h_ref = refs[n + 1:n + 4]
        unrotated_refs = refs[n + 4:]
        xv, gain_v = x_ref[...], gain_ref[...]
        r = lax.rsqrt(jnp.mean(xv * xv, axis=-1, keepdims=True) + EPS)
        xhat = xv * r
        h_ref[...] = (xhat * gain_v).astype(BF16)
        dh = jnp.zeros((tm, d), F32)
        for k, (dz_ref, (lo, hi)) in enumerate(zip(dz_refs, rows)):
            dz = dz_ref[...]
            if k < N_ROTATED:
                dz = _rotate(dz, cos_ref[...], -sin_ref[...]).astype(BF16)
                unrotated_refs[k][...] = dz
            dh = dh + lax.dot_general(dz.astype(BF16), wt_ref[lo:hi, :], NN, preferred_element_type=F32)
        dxhat = dh * gain_v
        dx_ref[...] = dres_ref[...] + r * (dxhat - xhat * jnp.mean(dxhat * xhat, axis=-1, keepdims=True))

        @pl.when(pl.program_id(0) == 0)
        def _():
            dgain_ref[...] = jnp.zeros_like(dgain_ref)

        dgain_ref[...] += jnp.sum(dh * xhat, axis=0, keepdims=True)

    row = pl.BlockSpec((tm, d), lambda i: (i, 0))
    tab = pl.BlockSpec((tm, LANES), lambda i: (i, 0))
    dz_specs = [pl.BlockSpec((tm, dz.shape[1]), lambda i: (i, 0)) for dz in dzs]
    outs = pl.pallas_call(
        body, grid=(n_tokens // tm,),
        in_specs=[row, _resident(gain.shape), row, tab, tab] + dz_specs + [_resident(wt.shape)],
        out_specs=(row, pl.BlockSpec(gain.shape, lambda i: (0, 0)), row) + tuple(dz_specs[:N_ROTATED]),
        out_shape=(jax.ShapeDtypeStruct(x.shape, F32), jax.ShapeDtypeStruct(gain.shape, F32),
                   jax.ShapeDtypeStruct(x.shape, BF16))
        + tuple(jax.ShapeDtypeStruct(dz.shape, BF16) for dz in dzs[:N_ROTATED]),
        name="in_proj_bwd_rows", compiler_params=_params(1),
    )(x, gain, dres, cos, sin, *dzs, wt)
    return outs[0], outs[1], outs[2], outs[3:]


def _in_proj_bwd_weight(dwt, h, dz, lo, hi, name):
    n_tokens, d = h.shape
    width = hi - lo
    tn = _tile(width, 512) if width >= LANES else width
    dz_tile = max(tn, LANES)

    def body(dwt_ref, h_ref, dz_ref, o_ref):
        o_ref[...] = lax.dot_general(dz_ref[:, :tn].astype(BF16), h_ref[...], TN, preferred_element_type=F32)

    return pl.pallas_call(
        body, grid=(width // tn,),
        in_specs=[ANY, _resident(h.shape), pl.BlockSpec((n_tokens, dz_tile), lambda j: (0, j))],
        out_specs=pl.BlockSpec((pl.Element(tn), pl.Element(d)), lambda j: (pl.multiple_of(lo + j * tn, 16), 0)),
        out_shape=jax.ShapeDtypeStruct(dwt.shape, F32), input_output_aliases={0: 0}, name=name,
        compiler_params=_params(1),
    )(dwt, h, dz)


def _split_small(z):
    return z[:, :GDN_HEADS], z[:, GDN_HEADS:2 * GDN_HEADS]


def _heads3(q, k, v):
    return _to_heads(q), _to_heads(k), _to_heads(v)


def _tokens6(o, lse):
    return tuple(_from_heads(o)) + tuple(_from_heads(lse))


def _blocks_of(vals, nblk):
    return [[v[:, b * (v.shape[1] // nblk):(b + 1) * (v.shape[1] // nblk)] for v in vals] for b in range(nblk)]


def _rowwise_matmul_fwd(fn, name, rows, params, wt, nblk, res=None):
    n_rows = rows[0].shape[0]
    tm = FFN_ROW_TILE
    k, n = wt.shape
    nr, npar = len(rows), len(params)

    def body(*refs):
        row_vals = [r[...] for r in refs[:nr]]
        par_vals = [r[...] for r in refs[nr:nr + npar]]
        wt_ref = refs[nr + npar]
        o_ref, y_ref = refs[-2:]
        y = jnp.concatenate([fn(*blk, *par_vals)[0] for blk in _blocks_of(row_vals, nblk)], axis=1).astype(BF16)
        y_ref[...] = y
        acc = lax.dot_general(y, wt_ref[...], NN, preferred_element_type=F32)
        o_ref[...] = acc if res is None else refs[nr + npar + 1][...] + acc

    row_specs = [pl.BlockSpec((tm, a.shape[1]), lambda i: (i, 0)) for a in rows]
    ins = list(rows) + list(params) + [wt] + ([] if res is None else [res])
    specs = row_specs + [_resident(p.shape) for p in params] + [_resident(wt.shape)]
    if res is not None:
        specs.append(pl.BlockSpec((tm, n), lambda i: (i, 0)))
    return pl.pallas_call(
        body, grid=(n_rows // tm,), in_specs=specs,
        out_specs=(pl.BlockSpec((tm, n), lambda i: (i, 0)), pl.BlockSpec((tm, k), lambda i: (i, 0))),
        out_shape=(jax.ShapeDtypeStruct((n_rows, n), F32), jax.ShapeDtypeStruct((n_rows, k), BF16)),
        name=name, compiler_params=_params(1),
    )(*ins)


def _rowwise_matmul_bwd(fn, name, rows, params, wt, dout, nblk):
    n_rows = rows[0].shape[0]
    tm = FFN_ROW_TILE
    nr, npar = len(rows), len(params)

    def body(*refs):
        row_vals = [r[...] for r in refs[:nr]]
        par_vals = [r[...] for r in refs[nr:nr + npar]]
        wt_ref, dout_ref = refs[nr + npar], refs[nr + npar + 1]
        outs = refs[nr + npar + 2:]
        dy = lax.dot_general(dout_ref[...].astype(BF16), wt_ref[...], NT, preferred_element_type=F32)
        grads = [jax.vjp(fn, *blk, *par_vals)[1]((dy_blk,))
                 for blk, (dy_blk,) in zip(_blocks_of(row_vals, nblk), _blocks_of([dy], nblk))]
        for j in range(nr):
            outs[j][...] = jnp.concatenate([g[j] for g in grads], axis=1)
        for j in range(npar):
            ref = outs[nr + j]

            @pl.when(pl.program_id(0) == 0)
            def _(ref=ref):
                ref[...] = jnp.zeros_like(ref)

            for g in grads:
                ref[...] += g[nr + j]

    row_specs = [pl.BlockSpec((tm, a.shape[1]), lambda i: (i, 0)) for a in rows]
    par_specs = [_resident(p.shape) for p in params]
    return pl.pallas_call(
        body, grid=(n_rows // tm,),
        in_specs=row_specs + par_specs + [_resident(wt.shape), pl.BlockSpec((tm, dout.shape[1]), lambda i: (i, 0))],
        out_specs=tuple(row_specs + [pl.BlockSpec(p.shape, lambda i: (0, 0)) for p in params]),
        out_shape=tuple([jax.ShapeDtypeStruct(a.shape, F32) for a in rows]
                        + [jax.ShapeDtypeStruct(p.shape, F32) for p in params]),
        name=name, compiler_params=_params(1),
    )(*rows, *params, wt, dout)


def mixer_forward(x1, w, small):
    n_tokens = x1.shape[0]
    cos, sin = _rope_tables(n_tokens)
    proj = dict(zip(IN_NAMES, _in_proj_fwd(x1, small["mix_norm"], w["w_in_t"], cos, sin)))
    (qh, kh, vh), heads_vjp = jax.vjp(_heads3, proj["wq_a"], proj["wk_a"], proj["wv_a"])
    o, lse = _attn_fwd(qh, kh, vh)
    per_group, tokens_vjp = jax.vjp(_tokens6, o, lse)
    ya = _rowwise_fwd(_combine_fn, "combine", per_group, (), (), 512, 1)[0]
    pa = _matmul(ya, w["w_branch_a"], name="branch_a")
    qkv = _conv_fwd(proj["w_qkvb"], small["gdn_conv_w"])
    raw, small_vjp = jax.vjp(_split_small, proj["w_small"])
    gdn_params = (small["gdn_a_log"], small["gdn_dt_bias"])
    beta, gcum = _rowwise_fwd(_beta_decay_fn, "beta_decay", raw, (), gdn_params, 512, 1)
    ob, *states = _gdn_fwd(qkv, beta, gcum)
    gate_in = (ob, proj["w_ggate"])
    pb, yb = _rowwise_matmul_fwd(_outnorm_gate_fn, "branch_b", gate_in, (small["gdn_out_norm"],), w["w_branch_b"],
                                 GDN_HEADS)
    merge_in = (proj["w_gatea"], proj["w_gateb"], pa, pb)
    x2, merged = _rowwise_matmul_fwd(_merge_fn, "out", merge_in, (), w["w_out"], 1, res=x1)
    saved = dict(x1=x1, proj=proj, cos=cos, sin=sin, heads_vjp=heads_vjp, heads=(qh, kh, vh), tokens_vjp=tokens_vjp,
                 per_group=per_group, ya=ya, qkv=qkv, raw=raw, small_vjp=small_vjp, beta=beta, gcum=gcum, states=states,
                 gate_in=gate_in, yb=yb, merge_in=merge_in, merged=merged)
    return x2, saved


def mixer_backward(dx2, s, w, small):
    proj = s["proj"]
    grads = dict(w_out=_matmul(s["merged"], dx2, name="out_dw", ta=True))
    dgate_a, dgate_b, dpa, dpb = _rowwise_matmul_bwd(_merge_fn, "out_bwd", s["merge_in"], (), w["w_out"], dx2, 1)
    grads["w_branch_b"] = _matmul(s["yb"], dpb, name="branch_b_dw", ta=True)
    dya = _matmul(dpa, w["w_branch_a"], name="branch_a_da", tb=True)
    grads["w_branch_a"] = _matmul(s["ya"], dpa, name="branch_a_dw", ta=True)
    dob, dggate, grads["gdn_out_norm"] = _rowwise_matmul_bwd(
        _outnorm_gate_fn, "branch_b_bwd", s["gate_in"], (small["gdn_out_norm"],), w["w_branch_b"], dpb, GDN_HEADS)
    dqkv, dbeta, dgcum = _gdn_bwd(s["qkv"], s["beta"], s["gcum"], *s["states"], dob)
    gdn_params = (small["gdn_a_log"], small["gdn_dt_bias"])
    dbeta_raw, ddecay_raw, grads["gdn_a_log"], grads["gdn_dt_bias"] = _rowwise_bwd(
        _beta_decay_fn, "beta_decay_bwd", s["raw"], (), gdn_params, (dbeta, dgcum), 512, 1)
    dsmall = s["small_vjp"]((dbeta_raw, ddecay_raw))[0]
    dqkvb, grads["gdn_conv_w"] = _conv_bwd(proj["w_qkvb"], small["gdn_conv_w"], dqkv)
    dper_group = _rowwise_bwd(_combine_fn, "combine_bwd", s["per_group"], (), (), (dya,), 512, 1)
    do, dlse = s["tokens_vjp"](tuple(dper_group))
    dqh, dkh, dvh = _attn_bwd(*s["heads"], do, dlse)
    dq_rot, dk_rot, dv = s["heads_vjp"]((dqh, dkh, dvh))
    dzs = (dq_rot, dk_rot, dv, dqkvb, dsmall, dggate, dgate_a, dgate_b)
    dx1, grads["mix_norm"], h, unrotated = _in_proj_bwd_rows(
        s["x1"], small["mix_norm"], dx2, dzs, w["w_in_t"], s["cos"], s["sin"])
    dzs = tuple(unrotated) + dzs[N_ROTATED:]
    dwt = lax.empty(w["w_in_t"].shape, F32)
    for (name, lo, hi), dz in zip(IN_PIECES, dzs):
        dwt = _in_proj_bwd_weight(dwt, h, dz, lo, hi, "in_proj_dw_" + name)
    grads["w_in_t"] = dwt
    return dx1, grads


def ffn_forward(x, gain, w, tag):
    out, g, u = _ffn_fwd(x, gain, w[tag + "_w_gate"], w[tag + "_w_up"], w[tag + "_w_down"], tag + "_fwd")
    return out, (x, g, u)


def ffn_backward(dy, saved, gain, w, tag, owner=None):
    x, g, u = saved
    weights = (w[tag + "_w_gate"], w[tag + "_w_up"], w[tag + "_w_down"])
    dx, dgain, h, dyh, a, dg, du = _ffn_bwd_rows(x, gain, dy, g, u, *weights, tag + "_bwd_rows")
    return dx, dgain, _ffn_bwd_weights(h, dyh, a, dg, du, tag + "_bwd_weights", owner)


def loss_head(x3, target, gain):
    row_loss = _rowwise_fwd(_loss_fn, "loss", (x3,), (target,), (gain,), 256, 1)[0]
    dx3, dgain = _rowwise_bwd(_loss_fn, "loss_bwd", (x3,), (target,), (gain,), (jnp.ones_like(row_loss),), 256, 1)
    return jnp.sum(row_loss), dx3, dgain


BIG_WEIGHTS = ("ffn1_w_gate", "ffn1_w_up", "ffn1_w_down", "w_in", "w_branch_a", "w_branch_b", "w_out",
               "ffn2_w_gate", "ffn2_w_up", "ffn2_w_down")
TRANSPOSED = ("ffn1_w_gate", "ffn1_w_up", "w_in", "ffn2_w_gate", "ffn2_w_up")
CONV_SHARD = (GDN_CONV, 3 * GDN_WIDTH // N_DEV)
SMALL_ROWS = 24
ANY = pl.BlockSpec(memory_space=pl.ANY)


TOKEN = jax.ShapeDtypeStruct((8, LANES), F32)


def _after(value, token):
    return value + token[0, 0].astype(value.dtype)


def _position():
    return lax.axis_index("x"), lax.axis_index("y"), lax.axis_index("c")


def all_gather_shards(shards, name):
    n = len(shards)

    def body(*refs):
        x_refs, out_refs = refs[:n], refs[n:2 * n]
        send_sems, recv_sems, local_sems = refs[2 * n + 1:]
        x, y, c = _position()
        me, sibling = (x, y, c), (x, y, 1 - c)
        chips = [(1 - x, y), (x, 1 - y), (1 - x, 1 - y)]

        def slab(a, px, py, pc):
            return out_refs[a].at[4 * px + 2 * py + pc]

        def copy(a, k, block, to, src=None):
            return pltpu.make_async_remote_copy(
                src_ref=slab(a, *block) if src is None else src, dst_ref=slab(a, *block),
                send_sem=send_sems.at[7 * a + k], recv_sem=recv_sems.at[7 * a + k], device_id=to, device_id_type=MESH)

        mine = [pltpu.make_async_copy(x_refs[a], slab(a, *me), local_sems.at[a]) for a in range(n)]
        for cp in mine:
            cp.start()
        first = []
        for j, chip in enumerate(chips):
            first += [copy(a, 1 + j, me, (*chip, c), src=x_refs[a]) for a in range(n)]
        first += [copy(a, 0, me, sibling, src=x_refs[a]) for a in range(n)]
        for cp in first:
            cp.start()
        passed = []
        for j, chip in enumerate(chips):
            for a in range(n):
                copy(a, 1 + j, (*chip, c), me).wait_recv()
                cp = copy(a, 4 + j, (*chip, c), sibling)
                cp.start()
                passed.append(cp)
        for a in range(n):
            copy(a, 0, sibling, me).wait_recv()
        for j, chip in enumerate(chips):
            for a in range(n):
                copy(a, 4 + j, (*chip, 1 - c), me).wait_recv()
        for cp in first + passed:
            cp.wait_send()
        for cp in mine:
            cp.wait()
        refs[2 * n][...] = jnp.zeros_like(refs[2 * n])

    outs = pl.pallas_call(
        body, out_shape=tuple(jax.ShapeDtypeStruct((N_DEV,) + s.shape, s.dtype) for s in shards) + (TOKEN,),
        in_specs=[ANY] * n, out_specs=(ANY,) * n + (pl.BlockSpec(memory_space=pltpu.VMEM),),
        scratch_shapes=[pltpu.SemaphoreType.DMA((7 * n,)), pltpu.SemaphoreType.DMA((7 * n,)),
                        pltpu.SemaphoreType.DMA((n,))],
        name=name,
    )(*shards)
    return outs[:n], outs[n]


def exchange_with_sibling(grads):
    n = len(grads)

    def body(*refs):
        g_refs, recv_refs = refs[:n], refs[n:2 * n]
        send_sems, recv_sems = refs[2 * n:]
        x, y, c = _position()
        copies = [pltpu.make_async_remote_copy(
            src_ref=g_refs[a].at[2 * k + 1 - c], dst_ref=recv_refs[a].at[k], send_sem=send_sems.at[4 * a + k],
            recv_sem=recv_sems.at[4 * a + k], device_id=(x, y, 1 - c), device_id_type=MESH)
            for k in range(4) for a in range(n)]
        for cp in copies:
            cp.start()
        for cp in copies:
            cp.wait()

    return pl.pallas_call(
        body, out_shape=tuple(jax.ShapeDtypeStruct((4,) + g.shape[1:], g.dtype) for g in grads),
        in_specs=[ANY] * n, out_specs=(ANY,) * n,
        scratch_shapes=[pltpu.SemaphoreType.DMA((4 * n,)), pltpu.SemaphoreType.DMA((4 * n,))], name="rs_sibling",
    )(*grads)


ELEMENTWISE_TILE_BYTES = 1536 * 1024


def _tile2(rows, cols):
    if rows % 256 == 0:
        return 256, cols
    if rows * cols * 4 > ELEMENTWISE_TILE_BYTES and cols % 256 == 0:
        return rows, 256
    return rows, cols


def add_sibling(grads, received, core, name):
    _, rows, width = grads.shape
    tr, tc = _tile2(rows, width)

    def body(c_ref, g_ref, r_ref, o_ref):
        o_ref[...] = (g_ref[...] + r_ref[...]).astype(BF16)

    blk = (1, tr, tc)
    return pl.pallas_call(
        body,
        grid_spec=pltpu.PrefetchScalarGridSpec(
            num_scalar_prefetch=1, grid=(4, rows // tr, width // tc),
            in_specs=[pl.BlockSpec(blk, lambda k, i, j, c_ref: (2 * k + c_ref[0], i, j)),
                      pl.BlockSpec(blk, lambda k, i, j, c_ref: (k, i, j))],
            out_specs=pl.BlockSpec(blk, lambda k, i, j, c_ref: (k, i, j))),
        out_shape=jax.ShapeDtypeStruct((4, rows, width), BF16), name=name, compiler_params=_params(3),
    )(core, grads, received)


HBM = pl.BlockSpec(memory_space=pltpu.HBM)
SEM = pl.BlockSpec(memory_space=pltpu.SEMAPHORE)
DATAFLOW_EFFECT = pltpu.SideEffectType.DATAFLOW_SIDE_EFFECTING
N_PEERS = N_DEV - 1


def _peer(mask):
    x, y, c = _position()
    px = 1 - x if mask & 4 else x
    py = 1 - y if mask & 2 else y
    pc = 1 - c if mask & 1 else c
    return (px, py, pc), 4 * px + 2 * py + pc


ALL_PEERS = tuple(range(1, N_DEV))
OTHER_CHIPS = (4, 2, 6)


SIBLING = 1
GATHER_MODES = ("gather", "near")


def _exchange_peers(mode):
    return {"chips": OTHER_CHIPS, "near": (SIBLING,) + OTHER_CHIPS}.get(mode, ALL_PEERS)


def _direct_copies(src_refs, land_refs, send_sems, recv_sems, mode):
    x, y, c = _position()
    me = 4 * x + 2 * y + c
    masks = _exchange_peers(mode)
    copies = []
    for a, (src, land) in enumerate(zip(src_refs, land_refs)):
        for slot, mask in enumerate(masks):
            peer, peer_index = _peer(mask)
            k = len(masks) * a + slot
            if mode in GATHER_MODES:
                source, dest = src, land.at[me]
            elif mode == "scatter":
                source, dest = src.at[peer_index], land.at[slot]
            else:
                source, dest = src.at[2 * peer[0] + peer[1]], land.at[slot]
            copies.append(pltpu.make_async_remote_copy(
                src_ref=source, dst_ref=dest, send_sem=send_sems.at[k], recv_sem=recv_sems.at[k], device_id=peer,
                device_id_type=MESH))
    return copies


def forward_to_sibling(slabs, name):
    n = len(slabs)

    def body(*refs):
        out_refs = refs[n:2 * n]
        send_sems, recv_sems = refs[2 * n + 1:]
        x, y, c = _position()
        copies = []
        for a in range(n):
            for slot, mask in enumerate(OTHER_CHIPS):
                _, held = _peer(mask)
                copies.append(pltpu.make_async_remote_copy(
                    src_ref=out_refs[a].at[held], dst_ref=out_refs[a].at[held], send_sem=send_sems.at[3 * a + slot],
                    recv_sem=recv_sems.at[3 * a + slot], device_id=(x, y, 1 - c), device_id_type=MESH))
        for cp in copies:
            cp.start()
        for cp in copies:
            cp.wait()
        refs[2 * n][...] = jnp.zeros_like(refs[2 * n])

    outs = pl.pallas_call(
        body, out_shape=tuple(jax.ShapeDtypeStruct(s.shape, s.dtype) for s in slabs) + (TOKEN,),
        in_specs=[ANY] * n, out_specs=(ANY,) * n + (pl.BlockSpec(memory_space=pltpu.VMEM),),
        input_output_aliases={i: i for i in range(n)},
        scratch_shapes=[pltpu.SemaphoreType.DMA((3 * n,)), pltpu.SemaphoreType.DMA((3 * n,))], name=name,
    )(*slabs)
    return outs[:n], outs[n]


def direct_exchange_start(arrays, mode, name):
    n = len(arrays)
    n_peers = len(_exchange_peers(mode))
    lands = [lax.empty((N_DEV,) + a.shape if mode in GATHER_MODES else (n_peers,) + a.shape[1:], a.dtype)
             for a in arrays]

    def body(*refs):
        src_refs, land_refs = refs[:n], refs[n:2 * n]
        send_sems, recv_sems = refs[2 * n], refs[2 * n + 1]
        token = refs[-1]
        for cp in _direct_copies(src_refs, land_refs, send_sems, recv_sems, mode):
            cp.start()
        token[...] = jnp.zeros_like(token)

    sems = pltpu.SemaphoreType.DMA((n_peers * n,))
    outs = pl.pallas_call(
        body, name=name,
        out_shape=(sems, sems) + tuple(pltpu.HBM(a.shape, a.dtype) for a in arrays)
        + tuple(pltpu.HBM(l.shape, l.dtype) for l in lands) + (TOKEN,),
        in_specs=[HBM] * (2 * n), out_specs=(SEM, SEM) + (HBM,) * (2 * n) + (pl.BlockSpec(memory_space=pltpu.VMEM),),
        input_output_aliases={i: 2 + i for i in range(2 * n)},
        compiler_params=pltpu.CompilerParams(has_side_effects=DATAFLOW_EFFECT),
    )(*[pltpu.with_memory_space_constraint(a, pltpu.HBM) for a in list(arrays) + lands])
    return outs[0], outs[1], outs[2:2 + n], outs[2 + n:2 + 2 * n], outs[-1]


def direct_exchange_wait(send_sems, recv_sems, arrays, lands, after, mode, name):
    n = len(arrays)

    def body(*refs):
        src_refs, land_refs = refs[:n], refs[n:2 * n]
        send_sems, recv_sems = refs[2 * n], refs[2 * n + 1]
        for cp in _direct_copies(src_refs, land_refs, send_sems, recv_sems, mode):
            cp.wait_send()
            cp.wait_recv()
        refs[-1][...] = jnp.zeros_like(refs[-1])

    outs = pl.pallas_call(
        body, name=name,
        out_shape=tuple(pltpu.HBM(a.shape, a.dtype) for a in arrays) + tuple(pltpu.HBM(l.shape, l.dtype) for l in lands)
        + (TOKEN,),
        in_specs=[HBM] * (2 * n) + [SEM, SEM, pl.BlockSpec(memory_space=pl.ANY)],
        out_specs=(HBM,) * (2 * n) + (pl.BlockSpec(memory_space=pltpu.VMEM),),
        input_output_aliases={i: i for i in range(2 * n)},
        compiler_params=pltpu.CompilerParams(has_side_effects=DATAFLOW_EFFECT),
    )(*arrays, *lands, send_sems, recv_sems, after)
    return outs[n:]


def adamw_direct(w, m, v, own, received, name):
    row_per_tile = w.shape[0] != 1
    rows, cols = (w.shape[0], w.shape[2]) if row_per_tile else w.shape[-2:]
    tr, tc = _tile2(rows, cols)

    def body(w_ref, m_ref, v_ref, own_ref, r_ref, g_ref, d_ref, nm_ref, nv_ref):
        gv = own_ref[0]
        for j in range(N_PEERS):
            gv = gv + r_ref[j].astype(F32)
        nm = ADAM_B1 * m_ref[...] + (1.0 - ADAM_B1) * gv
        nv = ADAM_B2 * v_ref[...] + (1.0 - ADAM_B2) * (gv * gv)
        m_hat = nm / (1.0 - ADAM_B1 ** ADAM_STEP)
        v_hat = nv / (1.0 - ADAM_B2 ** ADAM_STEP)
        g_ref[...] = gv
        d_ref[...] = -ADAM_LR * (m_hat / (jnp.sqrt(v_hat) + ADAM_EPS) + ADAM_WD * w_ref[...])
        nm_ref[...] = nm
        nv_ref[...] = nv

    if row_per_tile:
        one = pl.BlockSpec((tr, None, tc), lambda i, j: (i, 0, j))
    else:
        one = pl.BlockSpec((None, tr, tc), lambda i, j: (0, i, j))
    out = jax.ShapeDtypeStruct(w.shape, F32)
    return pl.pallas_call(
        body, grid=(rows // tr, cols // tc),
        in_specs=[one, one, one, pl.BlockSpec((1, tr, tc), lambda i, j: (0, i, j)),
                  pl.BlockSpec((N_PEERS, tr, tc), lambda i, j: (0, i, j))],
        out_specs=(one,) * 4, out_shape=(out,) * 4, name=name, compiler_params=_params(2),
    )(w, m, v, own, received)


def all_reduce_small(vals):
    rows, width = vals.shape

    def body(x_ref, out_ref, all_ref, send_sems, recv_sems):
        x, y, c = _position()
        me, sibling = (x, y, c), (x, y, 1 - c)
        chips = [(1 - x, y), (x, 1 - y), (1 - x, 1 - y)]

        def slab(px, py, pc):
            return all_ref.at[4 * px + 2 * py + pc]

        def copy(k, block, to, src=None):
            return pltpu.make_async_remote_copy(
                src_ref=slab(*block) if src is None else src, dst_ref=slab(*block),
                send_sem=send_sems.at[k], recv_sem=recv_sems.at[k], device_id=to, device_id_type=MESH)

        first = [copy(0, me, sibling, src=x_ref)]
        first += [copy(1 + j, me, (*chip, c), src=x_ref) for j, chip in enumerate(chips)]
        for cp in first:
            cp.start()
        all_ref[4 * x + 2 * y + c] = x_ref[...]
        passed = [copy(4 + j, (*chip, c), sibling) for j, chip in enumerate(chips)]
        for j, chip in enumerate(chips):
            copy(1 + j, (*chip, c), me).wait_recv()
            passed[j].start()
        copy(0, sibling, me).wait_recv()
        for j, chip in enumerate(chips):
            copy(4 + j, (*chip, 1 - c), me).wait_recv()
        for cp in first + passed:
            cp.wait_send()
        total = all_ref[0]
        for d in range(1, N_DEV):
            total = total + all_ref[d]
        out_ref[...] = total

    vmem = pl.BlockSpec(memory_space=pltpu.VMEM)
    return pl.pallas_call(
        body, out_shape=(jax.ShapeDtypeStruct(vals.shape, F32), jax.ShapeDtypeStruct((N_DEV, rows, width), F32)),
        in_specs=[vmem], out_specs=(vmem, vmem),
        scratch_shapes=[pltpu.SemaphoreType.DMA((7,)), pltpu.SemaphoreType.DMA((7,))], name="small_allreduce",
    )(vals)[0]


def adamw(w, g, m, v, name):
    shape = w.shape
    w2, g2, m2, v2 = [a.reshape((-1, shape[-1])) for a in (w, g, m, v)]
    rows, cols = w2.shape
    tr = 256 if rows % 256 == 0 else rows

    def body(w_ref, g_ref, m_ref, v_ref, d_ref, nm_ref, nv_ref):
        gv = g_ref[...]
        nm = ADAM_B1 * m_ref[...] + (1.0 - ADAM_B1) * gv
        nv = ADAM_B2 * v_ref[...] + (1.0 - ADAM_B2) * (gv * gv)
        m_hat = nm / (1.0 - ADAM_B1 ** ADAM_STEP)
        v_hat = nv / (1.0 - ADAM_B2 ** ADAM_STEP)
        d_ref[...] = -ADAM_LR * (m_hat / (jnp.sqrt(v_hat) + ADAM_EPS) + ADAM_WD * w_ref[...])
        nm_ref[...] = nm
        nv_ref[...] = nv

    blk = pl.BlockSpec((tr, cols), lambda i: (i, 0))
    out = jax.ShapeDtypeStruct((rows, cols), F32)
    outs = pl.pallas_call(
        body, grid=(rows // tr,), in_specs=[blk] * 4, out_specs=(blk,) * 3, out_shape=(out,) * 3,
        name=name, compiler_params=_params(1),
    )(w2, g2, m2, v2)
    return tuple(o.reshape(shape) for o in outs)


def adamw_summed(w, m, v, grads, from_sibling, received, me, name):
    rows, cols = w.shape[-2:]
    tr, tc = _tile2(rows, cols)

    def body(me_ref, w_ref, m_ref, v_ref, own_ref, sib_ref, r_ref, g_ref, d_ref, nm_ref, nv_ref):
        gv = own_ref[0] + sib_ref[0]
        for j in range(3):
            gv = gv + r_ref[j].astype(F32)
        nm = ADAM_B1 * m_ref[0] + (1.0 - ADAM_B1) * gv
        nv = ADAM_B2 * v_ref[0] + (1.0 - ADAM_B2) * (gv * gv)
        m_hat = nm / (1.0 - ADAM_B1 ** ADAM_STEP)
        v_hat = nv / (1.0 - ADAM_B2 ** ADAM_STEP)
        g_ref[0] = gv
        d_ref[0] = -ADAM_LR * (m_hat / (jnp.sqrt(v_hat) + ADAM_EPS) + ADAM_WD * w_ref[0])
        nm_ref[0] = nm
        nv_ref[0] = nv

    one = pl.BlockSpec((1, tr, tc), lambda i, j, me_ref: (0, i, j))
    out = jax.ShapeDtypeStruct((1, rows, cols), F32)
    return pl.pallas_call(
        body,
        grid_spec=pltpu.PrefetchScalarGridSpec(
            num_scalar_prefetch=1, grid=(rows // tr, cols // tc),
            in_specs=[one, one, one, pl.BlockSpec((1, tr, tc), lambda i, j, me_ref: (me_ref[0], i, j)),
                      pl.BlockSpec((1, tr, tc), lambda i, j, me_ref: (me_ref[1], i, j)),
                      pl.BlockSpec((3, tr, tc), lambda i, j, me_ref: (0, i, j))],
            out_specs=(one,) * 4),
        out_shape=(out,) * 4, name=name, compiler_params=_params(2),
    )(me, w, m, v, grads, from_sibling, received)


SMALL_VECTORS = ("ffn1_norm", "mix_norm", "ffn2_norm", "final_norm")


def _pack_small(gs):
    row = jnp.concatenate([gs["gdn_a_log"].reshape(-1), gs["gdn_dt_bias"].reshape(-1), gs["gdn_out_norm"].reshape(-1)])
    rows = [gs[n].reshape(1, D_MODEL) for n in SMALL_VECTORS]
    rows.append(jnp.pad(row, (0, D_MODEL - row.shape[0])).reshape(1, D_MODEL))
    rows.append(gs["gdn_conv_w"].reshape(-1, D_MODEL))
    packed = jnp.concatenate(rows, axis=0)
    return jnp.pad(packed, ((0, SMALL_ROWS - packed.shape[0]), (0, 0)))


def _unpack_small(packed):
    out = {n: packed[i].reshape(1, D_MODEL) for i, n in enumerate(SMALL_VECTORS)}
    row = packed[len(SMALL_VECTORS)]
    out["gdn_a_log"] = row[:GDN_HEADS].reshape(1, GDN_HEADS)
    out["gdn_dt_bias"] = row[GDN_HEADS:2 * GDN_HEADS].reshape(1, GDN_HEADS)
    out["gdn_out_norm"] = row[2 * GDN_HEADS:2 * GDN_HEADS + GDN_HEAD_DIM].reshape(1, GDN_HEAD_DIM)
    first = len(SMALL_VECTORS) + 1
    out["gdn_conv_w"] = packed[first:first + GDN_CONV * 3].reshape(GDN_CONV, 3 * GDN_WIDTH)
    return out


WEIGHTS = ("ffn1_norm", "ffn1_w_gate", "ffn1_w_up", "ffn1_w_down", "mix_norm", "w_in", "gdn_conv_w", "gdn_a_log",
           "gdn_dt_bias", "gdn_out_norm", "w_branch_a", "w_branch_b", "w_out", "ffn2_norm", "ffn2_w_gate",
           "ffn2_w_up", "ffn2_w_down", "final_norm")


def kernel(x, ffn1_norm, ffn1_w_gate, ffn1_w_up, ffn1_w_down, mix_norm, w_in, gdn_conv_w, gdn_a_log, gdn_dt_bias, gdn_out_norm, w_branch_a, w_branch_b, w_out, ffn2_norm, ffn2_w_gate, ffn2_w_up, ffn2_w_down, final_norm, loss_target, m_ffn1_norm, m_ffn1_w_gate, m_ffn1_w_up, m_ffn1_w_down, m_mix_norm, m_w_in, m_gdn_conv_w, m_gdn_a_log, m_gdn_dt_bias, m_gdn_out_norm, m_w_branch_a, m_w_branch_b, m_w_out, m_ffn2_norm, m_ffn2_w_gate, m_ffn2_w_up, m_ffn2_w_down, m_final_norm, v_ffn1_norm, v_ffn1_w_gate, v_ffn1_w_up, v_ffn1_w_down, v_mix_norm, v_w_in, v_gdn_conv_w, v_gdn_a_log, v_gdn_dt_bias, v_gdn_out_norm, v_w_branch_a, v_w_branch_b, v_w_out, v_ffn2_norm, v_ffn2_w_gate, v_ffn2_w_up, v_ffn2_w_down, v_final_norm):
    given = dict(locals())
    px, py, pc = _position()
    big_names = list(BIG_WEIGHTS)

    def shard_view(a, n):
        if n == "w_in":
            return a.transpose(2, 0, 1)
        return a.transpose(0, 2, 1) if n in TRANSPOSED else a

    def shard_unview(a, n):
        if n == "w_in":
            return a.transpose(1, 2, 0)
        return a.transpose(0, 2, 1) if n in TRANSPOSED else a

    me = 4 * px + 2 * py + pc
    me_index = me.astype(jnp.int32).reshape(1)
    late = [n for n in big_names if n.startswith("ffn2")]
    early = [n for n in big_names if n not in late]
    shards = {n: shard_view(given[n], n).reshape(given[n].shape[-1 if n in TRANSPOSED else -2], -1).astype(BF16)
              for n in big_names}
    first = [n for n in early if n.startswith("ffn1")]
    middle = [n for n in early if n not in first]
    first_slabs, first_done = all_gather_shards([shards[n] for n in first], "gather_ffn1")
    shards["gdn_conv_w"] = gdn_conv_w[0]
    middle_all = middle + ["gdn_conv_w"]
    middle_gather = direct_exchange_start([_after(shards[n], first_done) for n in middle_all], "near",
                                          "gather_mixer_start")
    ffn1_norm = _after(ffn1_norm, middle_gather[4])
    w = dict(zip(first, first_slabs))
    x1, ffn1_saved = ffn_forward(x[0], ffn1_norm, w, "ffn1")
    near_lands = direct_exchange_wait(*middle_gather[:4], x1, "near", "gather_mixer_wait")[:-1]
    near_lands = [lax.dynamic_update_slice(land, shards[n][None], (me, 0, 0)) for n, land in zip(middle_all, near_lands)]
    middle_slabs, middle_done = forward_to_sibling(near_lands, "gather_mixer_forward")
    gathered = dict(zip(middle_all, middle_slabs))
    late_gather = direct_exchange_start([_after(shards[n], middle_done) for n in late], "gather", "gather_ffn2_start")
    w["w_in_t"] = gathered["w_in"].reshape(-1, D_MODEL)
    w["w_branch_a"] = gathered["w_branch_a"].transpose(1, 0, 2).reshape(256, D_MODEL)
    w["w_branch_b"] = gathered["w_branch_b"].reshape(D_MODEL, D_MODEL)
    w["w_out"] = gathered["w_out"].reshape(D_MODEL, D_MODEL)
    conv_full = gathered["gdn_conv_w"].transpose(1, 0, 2).reshape(GDN_CONV, 3 * GDN_WIDTH)
    small = dict(mix_norm=_after(mix_norm, late_gather[4]), gdn_a_log=gdn_a_log, gdn_dt_bias=gdn_dt_bias,
                 gdn_out_norm=gdn_out_norm, gdn_conv_w=conv_full)

    x2, mixer_saved = mixer_forward(x1, w, small)
    late_lands = direct_exchange_wait(*late_gather[:4], x2, "gather", "gather_ffn2_wait")
    for n, land in zip(late, late_lands):
        w[n] = lax.dynamic_update_slice(land, shards[n][None], (me, 0, 0))
    x3, ffn2_saved = ffn_forward(x2, ffn2_norm, w, "ffn2")
    loss_local, dx3, g_final = loss_head(x3, loss_target[0], final_norm.reshape(1, D_MODEL))
    loss = lax.psum(loss_local, ("x", "y", "c"))
    dx2, g_ffn2_norm, (dw2, dw2_own) = ffn_backward(dx3, ffn2_saved, ffn2_norm, w, "ffn2", me_index)
    late_scatter = direct_exchange_start(list(dw2), "scatter", "rs_ffn2_start")
    w_after = dict(w, w_out=_after(w["w_out"], late_scatter[4]))
    dx1, g_w = mixer_backward(dx2, mixer_saved, w_after, small)
    middle = ["w_in", "w_branch_a", "w_branch_b", "w_out"]
    g_big = dict(w_in=g_w["w_in_t"].reshape(N_DEV, -1, D_MODEL),
                 w_branch_a=g_w["w_branch_a"].reshape(256, N_DEV, 128).transpose(1, 0, 2),
                 w_branch_b=g_w["w_branch_b"].reshape(N_DEV, 128, D_MODEL),
                 w_out=g_w["w_out"].reshape(N_DEV, 128, D_MODEL))
    own = dict(zip(late, dw2_own))
    own.update({n: lax.dynamic_index_in_dim(g_big[n], me, 0, keepdims=True) for n in middle[1:]})
    in_rows = g_w["w_in_t"].shape[0] // N_DEV
    own["w_in"] = lax.dynamic_slice(g_w["w_in_t"], (me * in_rows, 0), (in_rows, D_MODEL))[None]
    middle_scatter = direct_exchange_start([g_big[n].astype(BF16) for n in middle], "scatter", "rs_mixer_start")
    grad_x, g_ffn1_norm, dw1 = ffn_backward(dx1, ffn1_saved, _after(ffn1_norm, middle_scatter[4]), w, "ffn1")
    g_small = dict(ffn1_norm=g_ffn1_norm, ffn2_norm=g_ffn2_norm, final_norm=g_final,
                   **{n: g_w[n] for n in ("mix_norm", "gdn_a_log", "gdn_dt_bias", "gdn_out_norm", "gdn_conv_w")})

    first = [n for n in early if n.startswith("ffn1")]
    g_list = list(dw1)
    core = pc.astype(jnp.int32).reshape(1)
    me_and_chip = jnp.stack([me, 2 * px + py]).astype(jnp.int32)
    from_sibling = exchange_with_sibling(g_list)
    partials = [add_sibling(g, r, core, "rs_add_" + n) for n, g, r in zip(first, g_list, from_sibling)]
    first_chips = direct_exchange_start(partials, "chips", "rs_ffn1_start")

    def state_of(n):
        return [shard_view(given[p + n], n) for p in ("", "m_", "v_")]

    results = {}
    late_received = direct_exchange_wait(*late_scatter[:4], first_chips[4], "scatter", "rs_ffn2_wait")
    middle_received = direct_exchange_wait(*middle_scatter[:4], first_chips[4], "scatter", "rs_mixer_wait")
    for n, recv in zip(late + middle, list(late_received[:-1]) + list(middle_received[:-1])):
        outs = adamw_direct(*state_of(n), own[n], recv, "adamw_" + n)
        results[n] = tuple(shard_unview(o, n) for o in outs)

    done = results["w_out"][1]
    from_chips = direct_exchange_wait(*first_chips[:4], done, "chips", "rs_ffn1_wait")
    for n, g, sib, recv in zip(first, g_list, from_sibling, from_chips):
        outs = adamw_summed(*state_of(n), g, sib, recv, me_and_chip, "adamw_" + n)
        results[n] = tuple(shard_unview(o, n) for o in outs)

    small_sum = _unpack_small(all_reduce_small(_after(_pack_small(g_small), from_chips[-1])))
    conv_cols = CONV_SHARD[1]
    small_sum["gdn_conv_w"] = lax.dynamic_slice(small_sum["gdn_conv_w"], (0, me * conv_cols), (GDN_CONV, conv_cols))
    for n in WEIGHTS:
        if n not in results:
            g = small_sum[n].reshape(given[n].shape)
            results[n] = (g,) + adamw(given[n], g, given["m_" + n], given["v_" + n], "adamw_" + n)

    outs = [[results[n][i] for n in WEIGHTS] for i in range(4)]
    return (loss, grad_x[None], *outs[0], *outs[1], *outs[2], *outs[3])
```

```python
import jax
import jax.numpy as jnp
from jax import lax
from jax.experimental import pallas as pl
from jax.experimental.pallas import tpu as pltpu

F32 = jnp.float32
BF16 = jnp.bfloat16
HI = lax.Precision.HIGHEST
MESH = pl.DeviceIdType.MESH

N_DEV = 8
D_MODEL = 1024
EPS = 1e-6
ROPE_THETA = 10000.0
DSW_DILATIONS = (1, 4, 16)
DSW_HEADS_PER_GROUP = 4
DSW_HEAD_DIM = 64
DSW_BLOCK = 128
GDN_HEADS = 8
GDN_HEAD_DIM = 128
GDN_WIDTH = 1024
GDN_CONV = 4
GDN_CHUNK = 64

ADAM_LR = 0.001
ADAM_B1 = 0.9
ADAM_B2 = 0.999
ADAM_EPS = 1e-08
ADAM_WD = 0.01
ADAM_STEP = 10

VMEM_LIMIT_BYTES = 56 * 1024 * 1024
LANES = 128

NN = (((1,), (0,)), ((), ()))
NT = (((1,), (1,)), ((), ()))
TN = (((0,), (0,)), ((), ()))


def _params(n_grid):
    return pltpu.CompilerParams(dimension_semantics=("arbitrary",) * n_grid, vmem_limit_bytes=VMEM_LIMIT_BYTES)


def _tile(n, pref):
    best = None
    t = LANES
    while t <= min(n, pref):
        if n % t == 0:
            best = t
        t += LANES
    return n if best is None else best


def _matmul(a, b, *, name, ta=False, tb=False, res=None, scale=1.0):
    K, M = a.shape if ta else a.shape[::-1]
    N = b.shape[0] if tb else b.shape[1]
    assert (b.shape[1] if tb else b.shape[0]) == K, (a.shape, b.shape, ta, tb)
    tm = _tile(M, 512)
    tn = _tile(N, 512)
    dn = (((0 if ta else 1,), (1 if tb else 0,)), ((), ()))

    def body(*refs):
        a_ref, b_ref = refs[:2]
        o_ref = refs[-1]
        acc = lax.dot_general(a_ref[...].astype(BF16), b_ref[...].astype(BF16), dn, preferred_element_type=F32)
        if scale != 1.0:
            acc = acc * scale
        if res is not None:
            acc = refs[2][...] + acc
        o_ref[...] = acc

    a_spec = pl.BlockSpec((K, tm), lambda i, j: (0, i)) if ta else pl.BlockSpec((tm, K), lambda i, j: (i, 0))
    b_spec = pl.BlockSpec((tn, K), lambda i, j: (j, 0)) if tb else pl.BlockSpec((K, tn), lambda i, j: (0, j))
    o_spec = pl.BlockSpec((tm, tn), lambda i, j: (i, j))
    ins, specs = [a, b], [a_spec, b_spec]
    if res is not None:
        ins.append(res)
        specs.append(o_spec)
    return pl.pallas_call(
        body, grid=(M // tm, N // tn), in_specs=specs, out_specs=o_spec,
        out_shape=jax.ShapeDtypeStruct((M, N), F32), name=name, compiler_params=_params(2),
    )(*ins)


def _rw_specs(arrs, tm, nblk):
    return [pl.BlockSpec((tm, a.shape[1] // nblk), lambda i, j: (i, j)) for a in arrs]


def _rowwise_fwd(fn, name, rows, consts, params, tm, nblk):
    n_rows = rows[0].shape[0]
    tm = min(tm, n_rows)
    ins = list(rows) + list(consts)
    avals = [jax.ShapeDtypeStruct((tm, a.shape[1] // nblk), a.dtype) for a in ins]
    avals += [jax.ShapeDtypeStruct(p.shape, p.dtype) for p in params]
    out_avals = jax.eval_shape(fn, *avals)
    n_in = len(ins) + len(params)

    def body(*refs):
        outs = fn(*[r[...] for r in refs[:n_in]])
        for r, o in zip(refs[n_in:], outs):
            r[...] = o.astype(r.dtype)

    return pl.pallas_call(
        body, grid=(n_rows // tm, nblk),
        in_specs=_rw_specs(ins, tm, nblk) + [pl.BlockSpec(p.shape, lambda i, j: (0, 0)) for p in params],
        out_specs=tuple(pl.BlockSpec((tm, o.shape[1]), lambda i, j: (i, j)) for o in out_avals),
        out_shape=tuple(jax.ShapeDtypeStruct((n_rows, o.shape[1] * nblk), o.dtype) for o in out_avals),
        name=name, compiler_params=_params(2),
    )(*ins, *params)


def _rowwise_bwd(fn, name, rows, consts, params, cts, tm, nblk):
    n_rows = rows[0].shape[0]
    tm = min(tm, n_rows)
    nr, nc, npar, nct = len(rows), len(consts), len(params), len(cts)

    def body(*refs):
        rv = [r[...] for r in refs[:nr]]
        cv = [r[...] for r in refs[nr:nr + nc]]
        pv = [r[...] for r in refs[nr + nc:nr + nc + npar]]
        ctv = [r[...] for r in refs[nr + nc + npar:nr + nc + npar + nct]]
        outs = refs[nr + nc + npar + nct:]
        _, vjp = jax.vjp(lambda *d: fn(*d[:nr], *cv, *d[nr:]), *rv, *pv)
        grads = vjp(tuple(ctv))
        for k in range(nr):
            outs[k][...] = grads[k]
        first = jnp.logical_and(pl.program_id(0) == 0, pl.program_id(1) == 0)
        for k in range(npar):
            ref = outs[nr + k]

            @pl.when(first)
            def _(ref=ref):
                ref[...] = jnp.zeros_like(ref)

            ref[...] += grads[nr + k]

    ins = list(rows) + list(consts)
    return pl.pallas_call(
        body, grid=(n_rows // tm, nblk),
        in_specs=(_rw_specs(ins, tm, nblk) + [pl.BlockSpec(p.shape, lambda i, j: (0, 0)) for p in params]
                  + _rw_specs(cts, tm, nblk)),
        out_specs=tuple(_rw_specs(rows, tm, nblk) + [pl.BlockSpec(p.shape, lambda i, j: (0, 0)) for p in params]),
        out_shape=tuple([jax.ShapeDtypeStruct(a.shape, F32) for a in rows]
                        + [jax.ShapeDtypeStruct(p.shape, F32) for p in params]),
        name=name, compiler_params=_params(2),
    )(*ins, *params, *cts)


def _merge_fn(ga, gb, pa, pb):
    return (jax.nn.sigmoid(ga) * pa + jax.nn.sigmoid(gb) * pb,)


def _outnorm_gate_fn(o, gate, gain):
    y = o * lax.rsqrt(jnp.mean(o * o, axis=-1, keepdims=True) + EPS) * gain
    return (y * (gate * jax.nn.sigmoid(gate)),)


def _beta_decay_fn(beta_raw, decay_raw, a_log, dt_bias):
    z = decay_raw + dt_bias
    softplus = jnp.maximum(z, 0.0) + jnp.log(1.0 + jnp.exp(-jnp.abs(z)))
    g = -jnp.exp(a_log) * softplus
    rows = g.shape[0]
    ii = lax.broadcasted_iota(jnp.int32, (rows, rows), 0)
    jj = lax.broadcasted_iota(jnp.int32, (rows, rows), 1)
    same_chunk_before = jnp.logical_and(jj <= ii, jj // GDN_CHUNK == ii // GDN_CHUNK).astype(F32)
    gcum = lax.dot_general(same_chunk_before, g, NN, precision=HI, preferred_element_type=F32)
    return jax.nn.sigmoid(beta_raw), gcum


def _combine_fn(o0, o1, o2, l0, l1, l2):
    m = lax.stop_gradient(jnp.maximum(jnp.maximum(l0, l1), l2))
    e0, e1, e2 = jnp.exp(l0 - m), jnp.exp(l1 - m), jnp.exp(l2 - m)
    return ((e0 * o0 + e1 * o1 + e2 * o2) / (e0 + e1 + e2),)


def _loss_fn(x, target, gain):
    y = x * lax.rsqrt(jnp.mean(x * x, axis=-1, keepdims=True) + EPS) * gain
    err = y - target
    return (0.5 * jnp.mean(err * err, axis=-1, keepdims=True),)


def _rotate(v, cos, sin):
    half = DSW_HEAD_DIM // 2
    lane = lax.broadcasted_iota(jnp.int32, cos.shape, 1)
    low = (lane % DSW_HEAD_DIM) < half
    slabs = []
    for s in range(v.shape[1] // LANES):
        x = v[:, s * LANES:(s + 1) * LANES]
        swapped = jnp.where(low, pltpu.roll(x, LANES - half, 1), pltpu.roll(x, half, 1))
        slabs.append(x * cos + swapped * sin)
    return jnp.concatenate(slabs, axis=1)


def _rope_tables(n_tokens):
    half = DSW_HEAD_DIM // 2
    inv_freq = ROPE_THETA ** (-jnp.arange(half, dtype=F32) / half)
    ang = jnp.arange(n_tokens, dtype=F32)[:, None] * inv_freq[None, :]
    cos, sin = jnp.cos(ang), jnp.sin(ang)
    return jnp.tile(jnp.concatenate([cos, cos], 1), (1, 2)), jnp.tile(jnp.concatenate([-sin, sin], 1), (1, 2))


def _attn_probs(q, kp, kc, group, n):
    blk = DSW_BLOCK
    k = _each(lambda a, b: jnp.concatenate([a, b], axis=0).astype(BF16), kp, kc)
    s = _each(lambda a, b: lax.dot_general(a.astype(BF16), b, NT, preferred_element_type=F32)
              * (DSW_HEAD_DIM ** -0.5), q, k)
    blocks_per_seq = jnp.where(group == 0, 16, jnp.where(group == 1, 4, 1))
    first = (n % blocks_per_seq) == 0
    qi = lax.broadcasted_iota(jnp.int32, (blk, 2 * blk), 0)
    kj = lax.broadcasted_iota(jnp.int32, (blk, 2 * blk), 1)
    dist = qi + blk - kj
    valid = (dist >= 0) & (dist <= blk) & jnp.logical_or(kj >= blk, jnp.logical_not(first))
    s = _each(lambda a: jnp.where(valid, a, -1e30), s)
    m = _each(lambda a: jnp.max(a, axis=-1, keepdims=True), s)
    p = _each(lambda a, b: jnp.exp(a - b), s, m)
    l = _each(lambda a: jnp.sum(a, axis=-1, keepdims=True), p)
    return _each(lambda a, b: a / b, p, l), _each(lambda a, b: a + jnp.log(b), m, l), k


GROUP_WIDTH = DSW_HEADS_PER_GROUP * DSW_HEAD_DIM


def _attn_specs(n_tokens):
    blk = DSW_BLOCK
    cur = pl.BlockSpec((1, blk, GROUP_WIDTH), lambda g, n: (g, n, 0))
    prev = pl.BlockSpec((1, blk, GROUP_WIDTH), lambda g, n: (g, jnp.maximum(n - 1, 0), 0))
    return cur, prev


def _heads_of(ref):
    x = ref[0]
    return [x[:, h * DSW_HEAD_DIM:(h + 1) * DSW_HEAD_DIM] for h in range(DSW_HEADS_PER_GROUP)]


def _group_of(heads):
    return jnp.concatenate(heads, axis=1)


def _attn_fwd(q, k, v):
    n_groups, n_tokens, _ = q.shape
    cur, prev = _attn_specs(n_tokens)

    def body(q_ref, kp_ref, kc_ref, vp_ref, vc_ref, o_ref, l_ref):
        p, lse, _ = _attn_probs(_heads_of(q_ref), _heads_of(kp_ref), _heads_of(kc_ref),
                                pl.program_id(0), pl.program_id(1))
        vv = _each(lambda a, b: jnp.concatenate([a, b], axis=0).astype(BF16), _heads_of(vp_ref), _heads_of(vc_ref))
        o = _each(lambda a, b: lax.dot_general(a.astype(BF16), b, NN, preferred_element_type=F32), p, vv)
        lse_wide = _each(lambda a: jnp.broadcast_to(a, (DSW_BLOCK, DSW_HEAD_DIM)), lse)
        o_ref[0] = _group_of(o)
        l_ref[0] = _group_of(lse_wide)

    return pl.pallas_call(
        body, grid=(n_groups, n_tokens // DSW_BLOCK), in_specs=[cur, prev, cur, prev, cur],
        out_specs=(cur, cur), out_shape=(jax.ShapeDtypeStruct(q.shape, F32), jax.ShapeDtypeStruct(q.shape, F32)),
        name="attn_fwd", compiler_params=_params(2),
    )(q, k, k, v, v)


def _attn_bwd(q, k, v, do, dlse):
    n_groups, n_tokens, _ = q.shape
    nblk = n_tokens // DSW_BLOCK
    cur, prev = _attn_specs(n_tokens)
    part = pl.BlockSpec((1, 1, 2 * DSW_BLOCK, GROUP_WIDTH), lambda g, n: (g, n, 0, 0))
    scale = DSW_HEAD_DIM ** -0.5

    def body(q_ref, kp_ref, kc_ref, vp_ref, vc_ref, do_ref, dl_ref, dq_ref, dk_ref, dv_ref):
        qs = _heads_of(q_ref)
        p, _, kb = _attn_probs(qs, _heads_of(kp_ref), _heads_of(kc_ref), pl.program_id(0), pl.program_id(1))
        qb = _each(lambda a: a.astype(BF16), qs)
        vv = _each(lambda a, b: jnp.concatenate([a, b], axis=0).astype(BF16), _heads_of(vp_ref), _heads_of(vc_ref))
        dob = _each(lambda a: a.astype(BF16), _heads_of(do_ref))
        dp = _each(lambda a, b: lax.dot_general(a, b, NT, preferred_element_type=F32), dob, vv)
        dv = _each(lambda a, b: lax.dot_general(a.astype(BF16), b, TN, preferred_element_type=F32), p, dob)
        dl = _each(lambda a: jnp.sum(a, axis=-1, keepdims=True), _heads_of(dl_ref))
        ds = _each(lambda a, b, c: (a * (b - jnp.sum(b * a, axis=-1, keepdims=True) + c) * scale).astype(BF16),
                   p, dp, dl)
        dq = _each(lambda a, b: lax.dot_general(a, b, NN, preferred_element_type=F32), ds, kb)
        dk = _each(lambda a, b: lax.dot_general(a, b, TN, preferred_element_type=F32), ds, qb)
        dq_ref[0] = _group_of(dq)
        dk_ref[0, 0] = _group_of(dk)
        dv_ref[0, 0] = _group_of(dv)

    partial_shape = jax.ShapeDtypeStruct((n_groups, nblk, 2 * DSW_BLOCK, GROUP_WIDTH), F32)
    dq, dkp, dvp = pl.pallas_call(
        body, grid=(n_groups, nblk), in_specs=[cur, prev, cur, prev, cur, cur, cur],
        out_specs=(cur, part, part), out_shape=(jax.ShapeDtypeStruct(q.shape, F32), partial_shape, partial_shape),
        name="attn_bwd", compiler_params=_params(2),
    )(q, k, k, v, v, do, dlse)

    def fold(partial):
        own = partial[:, :, DSW_BLOCK:]
        from_next = jnp.pad(partial[:, 1:, :DSW_BLOCK], ((0, 0), (0, 1), (0, 0), (0, 0)))
        return (own + from_next).reshape(n_groups, n_tokens, GROUP_WIDTH)

    return dq, fold(dkp), fold(dvp)


def _to_heads(a):
    n_tokens = a.shape[0]
    outs = []
    for gi, d in enumerate(DSW_DILATIONS):
        blk = a[:, gi * GROUP_WIDTH:(gi + 1) * GROUP_WIDTH].reshape(n_tokens // d, d, GROUP_WIDTH)
        outs.append(blk.transpose(1, 0, 2).reshape(1, n_tokens, GROUP_WIDTH))
    return jnp.concatenate(outs, 0)


def _from_heads(a):
    n_tokens = a.shape[1]
    return [a[gi].reshape(d, n_tokens // d, GROUP_WIDTH).transpose(1, 0, 2).reshape(n_tokens, GROUP_WIDTH)
            for gi, d in enumerate(DSW_DILATIONS)]


CONV_TILE = 512


def _shift_down(x, k, rows):
    return x if k == 0 else jnp.where(rows >= k, pltpu.roll(x, k, 0), 0.0)


def _shift_up(x, k, rows):
    n = x.shape[0]
    return x if k == 0 else jnp.where(rows < n - k, pltpu.roll(x, n - k, 0), 0.0)


def _conv_pre(x, w):
    rows = lax.broadcasted_iota(jnp.int32, x.shape, 0)
    acc = x * w[GDN_CONV - 1:GDN_CONV]
    for k in range(1, GDN_CONV):
        acc = acc + _shift_down(x, k, rows) * w[GDN_CONV - 1 - k:GDN_CONV - k]
    return acc, rows


def _conv_fwd(x, w):
    n_tokens, width = x.shape
    big = pl.BlockSpec((n_tokens, CONV_TILE), lambda j: (0, j))
    wsp = pl.BlockSpec((GDN_CONV, CONV_TILE), lambda j: (0, j))

    def body(x_ref, w_ref, o_ref):
        acc, _ = _conv_pre(x_ref[...], w_ref[...])
        o_ref[...] = acc * jax.nn.sigmoid(acc)

    return pl.pallas_call(
        body, grid=(width // CONV_TILE,), in_specs=[big, wsp], out_specs=big,
        out_shape=jax.ShapeDtypeStruct(x.shape, F32), name="conv_fwd", compiler_params=_params(1),
    )(x, w)


def _conv_bwd(x, w, dy):
    n_tokens, width = x.shape
    big = pl.BlockSpec((n_tokens, CONV_TILE), lambda j: (0, j))
    wsp = pl.BlockSpec((GDN_CONV, CONV_TILE), lambda j: (0, j))

    def body(x_ref, w_ref, dy_ref, dx_ref, dw_ref):
        xv, wv = x_ref[...], w_ref[...]
        acc, rows = _conv_pre(xv, wv)
        sg = jax.nn.sigmoid(acc)
        dacc = dy_ref[...] * (sg + acc * sg * (1.0 - sg))
        dx = dacc * wv[GDN_CONV - 1:GDN_CONV]
        for k in range(1, GDN_CONV):
            dx = dx + _shift_up(dacc, k, rows) * wv[GDN_CONV - 1 - k:GDN_CONV - k]
        dx_ref[...] = dx
        for k in range(GDN_CONV):
            dw_ref[GDN_CONV - 1 - k:GDN_CONV - k, :] = jnp.sum(dacc * _shift_down(xv, k, rows), axis=0, keepdims=True)

    return pl.pallas_call(
        body, grid=(width // CONV_TILE,), in_specs=[big, wsp, big], out_specs=(big, wsp),
        out_shape=(jax.ShapeDtypeStruct(x.shape, F32), jax.ShapeDtypeStruct(w.shape, F32)),
        name="conv_bwd", compiler_params=_params(1),
    )(x, w, dy)


def _dot(a, b, dn=NN):
    return lax.dot_general(a, b, dn, precision=HI, preferred_element_type=F32)


def _dot3(a, b, dn=NN):
    return lax.dot_general(a, b, dn, precision=lax.Precision.HIGH, preferred_element_type=F32)


def _bf16_dot(a, b, dn):
    return lax.dot_general(a.astype(BF16), b.astype(BF16), dn, preferred_element_type=F32)


_DOT_GRADS = {NN: (("g", "b", NT), ("a", "g", TN)), NT: (("g", "b", NN), ("g", "a", TN)),
              TN: (("b", "g", NT), ("a", "g", NN))}


def _make_bdot(dn):
    @jax.custom_vjp
    def op(a, b):
        return _bf16_dot(a, b, dn)

    def fwd(a, b):
        return op(a, b), (a, b)

    def bwd(saved, g):
        vals = dict(a=saved[0], b=saved[1], g=g)
        return tuple(_bf16_dot(vals[x], vals[y], form) for x, y, form in _DOT_GRADS[dn])

    op.defvjp(fwd, bwd)
    return op


_BDOTS = {dn: _make_bdot(dn) for dn in (NN, NT, TN)}


def _bdot(a, b, dn=NN):
    return _BDOTS[dn](a, b)


def _each(fn, *lists):
    return [fn(*items) for items in zip(*lists)]


@jax.custom_vjp
def _known_inverse(m, inverse):
    return inverse


def _known_inverse_fwd(m, inverse):
    return inverse, inverse


def _known_inverse_bwd(inverse, d_inverse):
    return -_dot3(_dot3(inverse, d_inverse, TN), inverse, NT), jnp.zeros_like(inverse)


_known_inverse.defvjp(_known_inverse_fwd, _known_inverse_bwd)


def _gdn_chunks(q, k, v, b, gcum, state, inverse=None):
    c = GDN_CHUNK
    ii = lax.broadcasted_iota(jnp.int32, (c, c), 0)
    jj = lax.broadcasted_iota(jnp.int32, (c, c), 1)
    qn = _each(lambda x: x * lax.rsqrt(jnp.sum(x * x, axis=-1, keepdims=True) + EPS) * (GDN_HEAD_DIM ** -0.5), q)
    kn = _each(lambda x: x * lax.rsqrt(jnp.sum(x * x, axis=-1, keepdims=True) + EPS), k)
    gcum_i = _each(lambda x: jnp.broadcast_to(x, (c, c)), gcum)
    gcum_j = _each(jnp.transpose, gcum_i)
    decay = _each(lambda x, y: jnp.exp(jnp.where(jj <= ii, x - y, -1e30)), gcum_i, gcum_j)
    g_last = _each(lambda x: x[c - 1:c, :], gcum)
    e_gcum = _each(jnp.exp, gcum)
    kbeta = _each(lambda x, y: x * y, kn, b)
    vbeta = _each(lambda x, y: x * y, v, b)
    m = _each(lambda x, y, d: jnp.where(jj < ii, _bdot(x, y, NT) * d, 0.0), kbeta, kn, decay)
    if inverse is not None:
        inv = _each(_known_inverse, m, inverse)
    else:
        eye = (ii == jj).astype(F32)
        inv = _each(lambda x: eye - x, m)
        power = _each(lambda x: _dot3(x, x), m)
        for step in range(5):
            inv = _each(lambda x, p: x + _dot3(x, p), inv, power)
            if step < 4:
                power = _each(lambda p: _dot3(p, p), power)
    u = _each(_dot3, inv, vbeta)
    w = _each(lambda x, y, e: _dot3(x, y * e), inv, kbeta, e_gcum)
    a_qk = _each(lambda x, y, d: _bdot(x, y, NT) * d, qn, kn, decay)
    v_new = _each(lambda x, y, s: x - _bdot(y, s), u, w, state)
    o = _each(lambda x, e, s, a, vn: _bdot(x * e, s) + _bdot(a, vn), qn, e_gcum, state, a_qk, v_new)
    new_state = _each(lambda s, gl, x, gc, vn: s * jnp.exp(gl) + _bdot(x * jnp.exp(gl - gc), vn, TN),
                      state, g_last, kn, gcum, v_new)
    return o, new_state, inv


GDN_HEADS_PER_STEP = 8


GDN_TIME_TILE = 256


def _gdn_specs(n_tokens, reverse):
    hb, hd, tt = GDN_HEADS_PER_STEP, GDN_HEAD_DIM, GDN_TIME_TILE
    nb, nt = GDN_HEADS // hb, n_tokens // tt

    def when(t):
        return nt - 1 - t if reverse else t

    q = pl.BlockSpec((tt, hb * hd), lambda h, t: (when(t), h))
    k = pl.BlockSpec((tt, hb * hd), lambda h, t: (when(t), nb + h))
    v = pl.BlockSpec((tt, hb * hd), lambda h, t: (when(t), 2 * nb + h))
    vec = pl.BlockSpec((tt, hb), lambda h, t: (when(t), h))
    states = pl.BlockSpec((hb, tt // GDN_CHUNK, hd, hd), lambda h, t: (h, when(t), 0, 0))
    inverses = pl.BlockSpec((hb, tt // GDN_CHUNK, GDN_CHUNK, GDN_CHUNK), lambda h, t: (h, when(t), 0, 0))
    return q, k, v, vec, states, inverses


def _gdn_fwd(qkv, beta, g):
    n_tokens = qkv.shape[0]
    hb, hd, tt = GDN_HEADS_PER_STEP, GDN_HEAD_DIM, GDN_TIME_TILE
    n_chunks = tt // GDN_CHUNK
    q_s, k_s, v_s, vec, st, inv_s = _gdn_specs(n_tokens, False)

    def body(q_ref, k_ref, v_ref, b_ref, g_ref, o_ref, st_ref, inv_ref, state):
        @pl.when(pl.program_id(1) == 0)
        def _():
            state[...] = jnp.zeros_like(state)

        def step(c, carry):
            r = pl.ds(pl.multiple_of(c * GDN_CHUNK, GDN_CHUNK), GDN_CHUNK)
            cols = [slice(h * hd, (h + 1) * hd) for h in range(hb)]
            old = [state[h] for h in range(hb)]
            o, new, inv = _gdn_chunks(
                [q_ref[r, cs] for cs in cols], [k_ref[r, cs] for cs in cols], [v_ref[r, cs] for cs in cols],
                [b_ref[r, h:h + 1] for h in range(hb)], [g_ref[r, h:h + 1] for h in range(hb)], old)
            for h in range(hb):
                st_ref[h, c] = old[h]
                inv_ref[h, c] = inv[h]
                o_ref[r, cols[h]] = o[h]
                state[h] = new[h]
            return carry

        lax.fori_loop(0, n_chunks, step, 0)

    n_all = n_tokens // GDN_CHUNK
    return pl.pallas_call(
        body, grid=(GDN_HEADS // hb, n_tokens // tt), in_specs=[q_s, k_s, v_s, vec, vec], out_specs=(q_s, st, inv_s),
        out_shape=(jax.ShapeDtypeStruct((n_tokens, GDN_WIDTH), F32),
                   jax.ShapeDtypeStruct((GDN_HEADS, n_all, hd, hd), F32),
                   jax.ShapeDtypeStruct((GDN_HEADS, n_all, GDN_CHUNK, GDN_CHUNK), F32)),
        scratch_shapes=[pltpu.VMEM((hb, hd, hd), F32)],
        name="gdn_fwd", compiler_params=_params(2),
    )(qkv, qkv, qkv, beta, g)


def _gdn_bwd(qkv, beta, g, states, inverses, do):
    n_tokens = qkv.shape[0]
    hb, hd, tt = GDN_HEADS_PER_STEP, GDN_HEAD_DIM, GDN_TIME_TILE
    n_chunks = tt // GDN_CHUNK
    q_s, k_s, v_s, vec, st, inv_s = _gdn_specs(n_tokens, True)

    assert hb == GDN_HEADS

    def body(q_ref, k_ref, v_ref, b_ref, g_ref, st_ref, inv_ref, do_ref, dqkv_ref, db_ref, dg_ref, dstate):
        @pl.when(pl.program_id(1) == 0)
        def _():
            dstate[...] = jnp.zeros_like(dstate)

        def step(i, carry):
            c = n_chunks - 1 - i
            r = pl.ds(pl.multiple_of(c * GDN_CHUNK, GDN_CHUNK), GDN_CHUNK)
            cols = [slice(h * hd, (h + 1) * hd) for h in range(hb)]
            args = ([q_ref[r, cs] for cs in cols], [k_ref[r, cs] for cs in cols], [v_ref[r, cs] for cs in cols],
                    [b_ref[r, h:h + 1] for h in range(hb)], [g_ref[r, h:h + 1] for h in range(hb)],
                    [st_ref[h, c] for h in range(hb)])
            saved = [inv_ref[h, c] for h in range(hb)]
            cts = ([do_ref[r, cs] for cs in cols], [dstate[h] for h in range(hb)])
            dq, dk, dv, db, dg, dst = jax.vjp(lambda *a: _gdn_chunks(*a, inverse=saved)[:2], *args)[1](cts)
            for h in range(hb):
                for part, grad in enumerate((dq, dk, dv)):
                    dqkv_ref[r, pl.ds(part * GDN_WIDTH + h * hd, hd)] = grad[h]
                db_ref[r, h:h + 1] = db[h]
                dg_ref[r, h:h + 1] = dg[h]
                dstate[h] = dst[h]
            return carry

        lax.fori_loop(0, n_chunks, step, 0)

    n_t = n_tokens // tt
    thin = jax.ShapeDtypeStruct(beta.shape, F32)
    return pl.pallas_call(
        body, grid=(GDN_HEADS // hb, n_t), in_specs=[q_s, k_s, v_s, vec, vec, st, inv_s, q_s],
        out_specs=(pl.BlockSpec((tt, 3 * GDN_WIDTH), lambda h, t: (n_t - 1 - t, 0)), vec, vec),
        out_shape=(jax.ShapeDtypeStruct(qkv.shape, F32), thin, thin),
        scratch_shapes=[pltpu.VMEM((hb, hd, hd), F32)],
        name="gdn_bwd", compiler_params=_params(2),
    )(qkv, qkv, qkv, beta, g, states, inverses, do)


FFN_ROW_TILE = 256
FFN_FWD_ROW_TILE = 512


def _resident(shape):
    return pl.BlockSpec(shape, lambda i: (0,) * len(shape), pipeline_mode=pl.Buffered(1))


def _ffn_fwd(x, gain, wg, wu, wd, name):
    n_tokens, d = x.shape
    n_shards, n, _ = wg.shape
    tm = FFN_FWD_ROW_TILE

    def body(x_ref, gain_ref, wg_ref, wu_ref, wd_ref, o_ref, g_ref, u_ref):
        xv = x_ref[...]
        h = (xv * lax.rsqrt(jnp.mean(xv * xv, axis=-1, keepdims=True) + EPS) * gain_ref[...]).astype(BF16)
        acc = jnp.zeros((tm, d), F32)
        for j in range(n_shards):
            g = lax.dot_general(h, wg_ref[j], NT, preferred_element_type=F32)
            u = lax.dot_general(h, wu_ref[j], NT, preferred_element_type=F32)
            g_ref[j] = g
            u_ref[j] = u
            a = (g * jax.nn.sigmoid(g) * u).astype(BF16)
            acc = acc + lax.dot_general(a, wd_ref[j], NN, preferred_element_type=F32)
        o_ref[...] = xv + 0.5 * acc

    row = pl.BlockSpec((tm, d), lambda i: (i, 0))
    hid = pl.BlockSpec((n_shards, tm, n), lambda i: (0, i, 0))
    return pl.pallas_call(
        body, grid=(n_tokens // tm,),
        in_specs=[row, _resident(gain.shape), _resident(wg.shape), _resident(wu.shape), _resident(wd.shape)],
        out_specs=(row, hid, hid),
        out_shape=(jax.ShapeDtypeStruct(x.shape, F32), jax.ShapeDtypeStruct((n_shards, n_tokens, n), F32),
                   jax.ShapeDtypeStruct((n_shards, n_tokens, n), F32)),
        name=name, compiler_params=_params(1),
    )(x, gain, wg, wu, wd)


def _ffn_bwd_rows(x, gain, dy, g, u, wg, wu, wd, name):
    n_tokens, d = x.shape
    n_shards, n, _ = wg.shape
    tm = FFN_ROW_TILE

    def body(x_ref, gain_ref, dy_ref, g_ref, u_ref, wg_ref, wu_ref, wd_ref,
             dx_ref, dgain_ref, h_ref, dyh_ref, a_ref, dg_ref, du_ref):
        xv, dyv, gain_v = x_ref[...], dy_ref[...], gain_ref[...]
        r = lax.rsqrt(jnp.mean(xv * xv, axis=-1, keepdims=True) + EPS)
        xhat = xv * r
        h_ref[...] = (xhat * gain_v).astype(BF16)
        dyh = (0.5 * dyv).astype(BF16)
        dyh_ref[...] = dyh
        dh = jnp.zeros((tm, d), F32)
        for j in range(n_shards):
            da = lax.dot_general(dyh, wd_ref[j], NT, preferred_element_type=F32)
            gv, uv = g_ref[j], u_ref[j]
            sg = jax.nn.sigmoid(gv)
            silu = gv * sg
            a_ref[j] = (silu * uv).astype(BF16)
            dg = (da * uv * (sg + silu * (1.0 - sg))).astype(BF16)
            du = (da * silu).astype(BF16)
            dg_ref[j] = dg
            du_ref[j] = du
            dh = dh + lax.dot_general(dg, wg_ref[j], NN, preferred_element_type=F32)
            dh = dh + lax.dot_general(du, wu_ref[j], NN, preferred_element_type=F32)
        dxhat = dh * gain_v
        dx_ref[...] = dyv + r * (dxhat - xhat * jnp.mean(dxhat * xhat, axis=-1, keepdims=True))

        @pl.when(pl.program_id(0) == 0)
        def _():
            dgain_ref[...] = jnp.zeros_like(dgain_ref)

        dgain_ref[...] += jnp.sum(dh * xhat, axis=0, keepdims=True)

    row = pl.BlockSpec((tm, d), lambda i: (i, 0))
    hid = pl.BlockSpec((n_shards, tm, n), lambda i: (0, i, 0))
    hid_shape = (n_shards, n_tokens, n)
    return pl.pallas_call(
        body, grid=(n_tokens // tm,),
        in_specs=[row, _resident(gain.shape), row, hid, hid, _resident(wg.shape), _resident(wu.shape),
                  _resident(wd.shape)],
        out_specs=(row, pl.BlockSpec(gain.shape, lambda i: (0, 0)), row, row, hid, hid, hid),
        out_shape=(jax.ShapeDtypeStruct(x.shape, F32), jax.ShapeDtypeStruct(gain.shape, F32),
                   jax.ShapeDtypeStruct(x.shape, BF16), jax.ShapeDtypeStruct(x.shape, BF16),
                   jax.ShapeDtypeStruct(hid_shape, BF16), jax.ShapeDtypeStruct(hid_shape, BF16),
                   jax.ShapeDtypeStruct(hid_shape, BF16)),
        name=name, compiler_params=_params(1),
    )(x, gain, dy, g, u, wg, wu, wd)


def _ffn_bwd_weights(h, dyh, a, dg, du, name, owner=None):
    n_shards, n_tokens, n = a.shape
    d = h.shape[1]

    def products(h_ref, dyh_ref, a_ref, dg_ref, du_ref):
        hv = h_ref[...]
        return (lax.dot_general(dg_ref[0], hv, TN, preferred_element_type=F32),
                lax.dot_general(du_ref[0], hv, TN, preferred_element_type=F32),
                lax.dot_general(a_ref[0], dyh_ref[...], TN, preferred_element_type=F32))

    hid = pl.BlockSpec((1, n_tokens, n), lambda j, *_: (j, 0, 0))
    out = pl.BlockSpec((1, n, d), lambda j, *_: (j, 0, 0))
    ins = [pl.BlockSpec(h.shape, lambda j, *_: (0, 0), pipeline_mode=pl.Buffered(1)),
           pl.BlockSpec(dyh.shape, lambda j, *_: (0, 0), pipeline_mode=pl.Buffered(1)), hid, hid, hid]
    if owner is None:
        def body(*refs):
            for ref, val in zip(refs[5:], products(*refs[:5])):
                ref[0] = val

        return pl.pallas_call(
            body, grid=(n_shards,), in_specs=ins, out_specs=(out, out, out),
            out_shape=(jax.ShapeDtypeStruct((n_shards, n, d), F32),) * 3, name=name, compiler_params=_params(1),
        )(h, dyh, a, dg, du)

    def body(owner_ref, *refs):
        vals = products(*refs[:5])
        for ref, val in zip(refs[5:8], vals):
            ref[0] = val.astype(BF16)

        @pl.when(pl.program_id(0) == owner_ref[0])
        def _():
            for ref, val in zip(refs[8:], vals):
                ref[0] = val

    mine = pl.BlockSpec((1, n, d), lambda j, *_: (0, 0, 0))
    outs = pl.pallas_call(
        body,
        grid_spec=pltpu.PrefetchScalarGridSpec(num_scalar_prefetch=1, grid=(n_shards,), in_specs=ins,
                                               out_specs=(out, out, out, mine, mine, mine)),
        out_shape=(jax.ShapeDtypeStruct((n_shards, n, d), BF16),) * 3 + (jax.ShapeDtypeStruct((1, n, d), F32),) * 3,
        name=name, compiler_params=_params(1),
    )(owner, h, dyh, a, dg, du)
    return outs[:3], outs[3:]


IN_PIECES = (("wq_a", 0, 768), ("wk_a", 768, 1536), ("wv_a", 1536, 2304), ("w_qkvb", 2304, 5376),
             ("w_small", 5376, 5392), ("w_ggate", 5392, 6416), ("w_gatea", 6416, 7440), ("w_gateb", 7440, 8464))
IN_NAMES = tuple(name for name, _, _ in IN_PIECES)


def _in_rows(lo, hi):
    return lo, max(hi, lo + LANES)


N_ROTATED = 2


def _in_proj_fwd(x, gain, wt, cos, sin):
    n_tokens, d = x.shape
    tm = FFN_ROW_TILE
    rows = [_in_rows(lo, hi) for _, lo, hi in IN_PIECES]

    def body(x_ref, gain_ref, wt_ref, cos_ref, sin_ref, *o_refs):
        xv = x_ref[...]
        h = (xv * lax.rsqrt(jnp.mean(xv * xv, axis=-1, keepdims=True) + EPS) * gain_ref[...]).astype(BF16)
        for k, ((lo, hi), o_ref) in enumerate(zip(rows, o_refs)):
            z = lax.dot_general(h, wt_ref[lo:hi, :], NT, preferred_element_type=F32)
            o_ref[...] = _rotate(z, cos_ref[...], sin_ref[...]) if k < N_ROTATED else z

    tab = pl.BlockSpec((tm, LANES), lambda i: (i, 0))
    return pl.pallas_call(
        body, grid=(n_tokens // tm,),
        in_specs=[pl.BlockSpec((tm, d), lambda i: (i, 0)), _resident(gain.shape), _resident(wt.shape), tab, tab],
        out_specs=tuple(pl.BlockSpec((tm, hi - lo), lambda i: (i, 0)) for lo, hi in rows),
        out_shape=tuple(jax.ShapeDtypeStruct((n_tokens, hi - lo), F32) for lo, hi in rows),
        name="in_proj_fwd", compiler_params=_params(1),
    )(x, gain, wt, cos, sin)


def _in_proj_bwd_rows(x, gain, dres, dzs, wt, cos, sin):
    n_tokens, d = x.shape
    tm = FFN_ROW_TILE
    n = len(dzs)
    rows = [_in_rows(lo, hi) for _, lo, hi in IN_PIECES]

    def body(x_ref, gain_ref, dres_ref, cos_ref, sin_ref, *refs):
        dz_refs, wt_ref = refs[:n], refs[n]
        dx_ref, dgain_ref, h_ref = refs[n + 1:n + 4]
        unrotated_refs = refs[n + 4:]
        xv, gain_v = x_ref[...], gain_ref[...]
        r = lax.rsqrt(jnp.mean(xv * xv, axis=-1, keepdims=True) + EPS)
        xhat = xv * r
        h_ref[...] = (xhat * gain_v).astype(BF16)
        dh = jnp.zeros((tm, d), F32)
        for k, (dz_ref, (lo, hi)) in enumerate(zip(dz_refs, rows)):
            dz = dz_ref[...]
            if k < N_ROTATED:
                dz = _rotate(dz, cos_ref[...], -sin_ref[...]).astype(BF16)
                unrotated_refs[k][...] = dz
            dh = dh + lax.dot_general(dz.astype(BF16), wt_ref[lo:hi, :], NN, preferred_element_type=F32)
        dxhat = dh * gain_v
        dx_ref[...] = dres_ref[...] + r * (dxhat - xhat * jnp.mean(dxhat * xhat, axis=-1, keepdims=True))

        @pl.when(pl.program_id(0) == 0)
        def _():
            dgain_ref[...] = jnp.zeros_like(dgain_ref)

        dgain_ref[...] += jnp.sum(dh * xhat, axis=0, keepdims=True)

    row = pl.BlockSpec((tm, d), lambda i: (i, 0))
    tab = pl.BlockSpec((tm, LANES), lambda i: (i, 0))
    dz_specs = [pl.BlockSpec((tm, dz.shape[1]), lambda i: (i, 0)) for dz in dzs]
    outs = pl.pallas_call(
        body, grid=(n_tokens // tm,),
        in_specs=[row, _resident(gain.shape), row, tab, tab] + dz_specs + [_resident(wt.shape)],
        out_specs=(row, pl.BlockSpec(gain.shape, lambda i: (0, 0)), row) + tuple(dz_specs[:N_ROTATED]),
        out_shape=(jax.ShapeDtypeStruct(x.shape, F32), jax.ShapeDtypeStruct(gain.shape, F32),
                   jax.ShapeDtypeStruct(x.shape, BF16))
        + tuple(jax.ShapeDtypeStruct(dz.shape, BF16) for dz in dzs[:N_ROTATED]),
        name="in_proj_bwd_rows", compiler_params=_params(1),
    )(x, gain, dres, cos, sin, *dzs, wt)
    return outs[0], outs[1], outs[2], outs[3:]


def _in_proj_bwd_weight(dwt, h, dz, lo, hi, name):
    n_tokens, d = h.shape
    width = hi - lo
    tn = _tile(width, 512) if width >= LANES else width
    dz_tile = max(tn, LANES)

    def body(dwt_ref, h_ref, dz_ref, o_ref):
        o_ref[...] = lax.dot_general(dz_ref[:, :tn].astype(BF16), h_ref[...], TN, preferred_element_type=F32)

    return pl.pallas_call(
        body, grid=(width // tn,),
        in_specs=[ANY, _resident(h.shape), pl.BlockSpec((n_tokens, dz_tile), lambda j: (0, j))],
        out_specs=pl.BlockSpec((pl.Element(tn), pl.Element(d)), lambda j: (pl.multiple_of(lo + j * tn, 16), 0)),
        out_shape=jax.ShapeDtypeStruct(dwt.shape, F32), input_output_aliases={0: 0}, name=name,
        compiler_params=_params(1),
    )(dwt, h, dz)


def _split_small(z):
    return z[:, :GDN_HEADS], z[:, GDN_HEADS:2 * GDN_HEADS]


def _heads3(q, k, v):
    return _to_heads(q), _to_heads(k), _to_heads(v)


def _tokens6(o, lse):
    return tuple(_from_heads(o)) + tuple(_from_heads(lse))


def _blocks_of(vals, nblk):
    return [[v[:, b * (v.shape[1] // nblk):(b + 1) * (v.shape[1] // nblk)] for v in vals] for b in range(nblk)]


def _rowwise_matmul_fwd(fn, name, rows, params, wt, nblk, res=None):
    n_rows = rows[0].shape[0]
    tm = FFN_ROW_TILE
    k, n = wt.shape
    nr, npar = len(rows), len(params)

    def body(*refs):
        row_vals = [r[...] for r in refs[:nr]]
        par_vals = [r[...] for r in refs[nr:nr + npar]]
        wt_ref = refs[nr + npar]
        o_ref, y_ref = refs[-2:]
        y = jnp.concatenate([fn(*blk, *par_vals)[0] for blk in _blocks_of(row_vals, nblk)], axis=1).astype(BF16)
        y_ref[...] = y
        acc = lax.dot_general(y, wt_ref[...], NN, preferred_element_type=F32)
        o_ref[...] = acc if res is None else refs[nr + npar + 1][...] + acc

    row_specs = [pl.BlockSpec((tm, a.shape[1]), lambda i: (i, 0)) for a in rows]
    ins = list(rows) + list(params) + [wt] + ([] if res is None else [res])
    specs = row_specs + [_resident(p.shape) for p in params] + [_resident(wt.shape)]
    if res is not None:
        specs.append(pl.BlockSpec((tm, n), lambda i: (i, 0)))
    return pl.pallas_call(
        body, grid=(n_rows // tm,), in_specs=specs,
        out_specs=(pl.BlockSpec((tm, n), lambda i: (i, 0)), pl.BlockSpec((tm, k), lambda i: (i, 0))),
        out_shape=(jax.ShapeDtypeStruct((n_rows, n), F32), jax.ShapeDtypeStruct((n_rows, k), BF16)),
        name=name, compiler_params=_params(1),
    )(*ins)


def _rowwise_matmul_bwd(fn, name, rows, params, wt, dout, nblk):
    n_rows = rows[0].shape[0]
    tm = FFN_ROW_TILE
    nr, npar = len(rows), len(params)

    def body(*refs):
        row_vals = [r[...] for r in refs[:nr]]
        par_vals = [r[...] for r in refs[nr:nr + npar]]
        wt_ref, dout_ref = refs[nr + npar], refs[nr + npar + 1]
        outs = refs[nr + npar + 2:]
        dy = lax.dot_general(dout_ref[...].astype(BF16), wt_ref[...], NT, preferred_element_type=F32)
        grads = [jax.vjp(fn, *blk, *par_vals)[1]((dy_blk,))
                 for blk, (dy_blk,) in zip(_blocks_of(row_vals, nblk), _blocks_of([dy], nblk))]
        for j in range(nr):
            outs[j][...] = jnp.concatenate([g[j] for g in grads], axis=1)
        for j in range(npar):
            ref = outs[nr + j]

            @pl.when(pl.program_id(0) == 0)
            def _(ref=ref):
                ref[...] = jnp.zeros_like(ref)

            for g in grads:
                ref[...] += g[nr + j]

    row_specs = [pl.BlockSpec((tm, a.shape[1]), lambda i: (i, 0)) for a in rows]
    par_specs = [_resident(p.shape) for p in params]
    return pl.pallas_call(
        body, grid=(n_rows // tm,),
        in_specs=row_specs + par_specs + [_resident(wt.shape), pl.BlockSpec((tm, dout.shape[1]), lambda i: (i, 0))],
        out_specs=tuple(row_specs + [pl.BlockSpec(p.shape, lambda i: (0, 0)) for p in params]),
        out_shape=tuple([jax.ShapeDtypeStruct(a.shape, F32) for a in rows]
                        + [jax.ShapeDtypeStruct(p.shape, F32) for p in params]),
        name=name, compiler_params=_params(1),
    )(*rows, *params, wt, dout)


def mixer_forward(x1, w, small):
    n_tokens = x1.shape[0]
    cos, sin = _rope_tables(n_tokens)
    proj = dict(zip(IN_NAMES, _in_proj_fwd(x1, small["mix_norm"], w["w_in_t"], cos, sin)))
    (qh, kh, vh), heads_vjp = jax.vjp(_heads3, proj["wq_a"], proj["wk_a"], proj["wv_a"])
    o, lse = _attn_fwd(qh, kh, vh)
    per_group, tokens_vjp = jax.vjp(_tokens6, o, lse)
    pa, ya = _rowwise_matmul_fwd(_combine_fn, "branch_a", per_group, (), w["w_branch_a"], 1)
    qkv = _conv_fwd(proj["w_qkvb"], small["gdn_conv_w"])
    raw, small_vjp = jax.vjp(_split_small, proj["w_small"])
    gdn_params = (small["gdn_a_log"], small["gdn_dt_bias"])
    beta, gcum = _rowwise_fwd(_beta_decay_fn, "beta_decay", raw, (), gdn_params, 512, 1)
    ob, *states = _gdn_fwd(qkv, beta, gcum)
    gate_in = (ob, proj["w_ggate"])
    pb, yb = _rowwise_matmul_fwd(_outnorm_gate_fn, "branch_b", gate_in, (small["gdn_out_norm"],), w["w_branch_b"],
                                 GDN_HEADS)
    merge_in = (proj["w_gatea"], proj["w_gateb"], pa, pb)
    x2, merged = _rowwise_matmul_fwd(_merge_fn, "out", merge_in, (), w["w_out"], 1, res=x1)
    saved = dict(x1=x1, proj=proj, cos=cos, sin=sin, heads_vjp=heads_vjp, heads=(qh, kh, vh), tokens_vjp=tokens_vjp,
                 per_group=per_group, ya=ya, qkv=qkv, raw=raw, small_vjp=small_vjp, beta=beta, gcum=gcum, states=states,
                 gate_in=gate_in, yb=yb, merge_in=merge_in, merged=merged)
    return x2, saved


def mixer_backward(dx2, s, w, small):
    proj = s["proj"]
    grads = dict(w_out=_matmul(s["merged"], dx2, name="out_dw", ta=True))
    dgate_a, dgate_b, dpa, dpb = _rowwise_matmul_bwd(_merge_fn, "out_bwd", s["merge_in"], (), w["w_out"], dx2, 1)
    grads["w_branch_b"] = _matmul(s["yb"], dpb, name="branch_b_dw", ta=True)
    grads["w_branch_a"] = _matmul(s["ya"], dpa, name="branch_a_dw", ta=True)
    dob, dggate, grads["gdn_out_norm"] = _rowwise_matmul_bwd(
        _outnorm_gate_fn, "branch_b_bwd", s["gate_in"], (small["gdn_out_norm"],), w["w_branch_b"], dpb, GDN_HEADS)
    dqkv, dbeta, dgcum = _gdn_bwd(s["qkv"], s["beta"], s["gcum"], *s["states"], dob)
    gdn_params = (small["gdn_a_log"], small["gdn_dt_bias"])
    dbeta_raw, ddecay_raw, grads["gdn_a_log"], grads["gdn_dt_bias"] = _rowwise_bwd(
        _beta_decay_fn, "beta_decay_bwd", s["raw"], (), gdn_params, (dbeta, dgcum), 512, 1)
    dsmall = s["small_vjp"]((dbeta_raw, ddecay_raw))[0]
    dqkvb, grads["gdn_conv_w"] = _conv_bwd(proj["w_qkvb"], small["gdn_conv_w"], dqkv)
    dper_group = _rowwise_matmul_bwd(_combine_fn, "branch_a_bwd", s["per_group"], (), w["w_branch_a"], dpa, 1)
    do, dlse = s["tokens_vjp"](tuple(dper_group))
    dqh, dkh, dvh = _attn_bwd(*s["heads"], do, dlse)
    dq_rot, dk_rot, dv = s["heads_vjp"]((dqh, dkh, dvh))
    dzs = (dq_rot, dk_rot, dv, dqkvb, dsmall, dggate, dgate_a, dgate_b)
    dx1, grads["mix_norm"], h, unrotated = _in_proj_bwd_rows(
        s["x1"], small["mix_norm"], dx2, dzs, w["w_in_t"], s["cos"], s["sin"])
    dzs = tuple(unrotated) + dzs[N_ROTATED:]
    dwt = lax.empty(w["w_in_t"].shape, F32)
    for (name, lo, hi), dz in zip(IN_PIECES, dzs):
        dwt = _in_proj_bwd_weight(dwt, h, dz, lo, hi, "in_proj_dw_" + name)
    grads["w_in_t"] = dwt
    return dx1, grads


def ffn_forward(x, gain, w, tag):
    out, g, u = _ffn_fwd(x, gain, w[tag + "_w_gate"], w[tag + "_w_up"], w[tag + "_w_down"], tag + "_fwd")
    return out, (x, g, u)


def ffn_backward(dy, saved, gain, w, tag, owner=None):
    x, g, u = saved
    weights = (w[tag + "_w_gate"], w[tag + "_w_up"], w[tag + "_w_down"])
    dx, dgain, h, dyh, a, dg, du = _ffn_bwd_rows(x, gain, dy, g, u, *weights, tag + "_bwd_rows")
    return dx, dgain, _ffn_bwd_weights(h, dyh, a, dg, du, tag + "_bwd_weights", owner)


def loss_head(x3, target, gain):
    row_loss = _rowwise_fwd(_loss_fn, "loss", (x3,), (target,), (gain,), 256, 1)[0]
    dx3, dgain = _rowwise_bwd(_loss_fn, "loss_bwd", (x3,), (target,), (gain,), (jnp.ones_like(row_loss),), 256, 1)
    return jnp.sum(row_loss), dx3, dgain


BIG_WEIGHTS = ("ffn1_w_gate", "ffn1_w_up", "ffn1_w_down", "w_in", "w_branch_a", "w_branch_b", "w_out",
               "ffn2_w_gate", "ffn2_w_up", "ffn2_w_down")
TRANSPOSED = ("ffn1_w_gate", "ffn1_w_up", "w_in", "ffn2_w_gate", "ffn2_w_up")
CONV_SHARD = (GDN_CONV, 3 * GDN_WIDTH // N_DEV)
SMALL_ROWS = 24
ANY = pl.BlockSpec(memory_space=pl.ANY)


TOKEN = jax.ShapeDtypeStruct((8, LANES), F32)


def _after(value, token):
    return value + token[0, 0].astype(value.dtype)


def _position():
    return lax.axis_index("x"), lax.axis_index("y"), lax.axis_index("c")


def all_gather_shards(shards, name):
    n = len(shards)

    def body(*refs):
        x_refs, out_refs = refs[:n], refs[n:2 * n]
        send_sems, recv_sems, local_sems = refs[2 * n + 1:]
        x, y, c = _position()
        me, sibling = (x, y, c), (x, y, 1 - c)
        chips = [(1 - x, y), (x, 1 - y), (1 - x, 1 - y)]

        def slab(a, px, py, pc):
            return out_refs[a].at[4 * px + 2 * py + pc]

        def copy(a, k, block, to, src=None):
            return pltpu.make_async_remote_copy(
                src_ref=slab(a, *block) if src is None else src, dst_ref=slab(a, *block),
                send_sem=send_sems.at[7 * a + k], recv_sem=recv_sems.at[7 * a + k], device_id=to, device_id_type=MESH)

        mine = [pltpu.make_async_copy(x_refs[a], slab(a, *me), local_sems.at[a]) for a in range(n)]
        for cp in mine:
            cp.start()
        first = []
        for j, chip in enumerate(chips):
            first += [copy(a, 1 + j, me, (*chip, c), src=x_refs[a]) for a in range(n)]
        first += [copy(a, 0, me, sibling, src=x_refs[a]) for a in range(n)]
        for cp in first:
            cp.start()
        passed = []
        for j, chip in enumerate(chips):
            for a in range(n):
                copy(a, 1 + j, (*chip, c), me).wait_recv()
                cp = copy(a, 4 + j, (*chip, c), sibling)
                cp.start()
                passed.append(cp)
        for a in range(n):
            copy(a, 0, sibling, me).wait_recv()
        for j, chip in enumerate(chips):
            for a in range(n):
                copy(a, 4 + j, (*chip, 1 - c), me).wait_recv()
        for cp in first + passed:
            cp.wait_send()
        for cp in mine:
            cp.wait()
        refs[2 * n][...] = jnp.zeros_like(refs[2 * n])

    outs = pl.pallas_call(
        body, out_shape=tuple(jax.ShapeDtypeStruct((N_DEV,) + s.shape, s.dtype) for s in shards) + (TOKEN,),
        in_specs=[ANY] * n, out_specs=(ANY,) * n + (pl.BlockSpec(memory_space=pltpu.VMEM),),
        scratch_shapes=[pltpu.SemaphoreType.DMA((7 * n,)), pltpu.SemaphoreType.DMA((7 * n,)),
                        pltpu.SemaphoreType.DMA((n,))],
        name=name,
    )(*shards)
    return outs[:n], outs[n]


def exchange_with_sibling(grads):
    n = len(grads)

    def body(*refs):
        g_refs, recv_refs = refs[:n], refs[n:2 * n]
        send_sems, recv_sems = refs[2 * n:]
        x, y, c = _position()
        copies = [pltpu.make_async_remote_copy(
            src_ref=g_refs[a].at[2 * k + 1 - c], dst_ref=recv_refs[a].at[k], send_sem=send_sems.at[4 * a + k],
            recv_sem=recv_sems.at[4 * a + k], device_id=(x, y, 1 - c), device_id_type=MESH)
            for k in range(4) for a in range(n)]
        for cp in copies:
            cp.start()
        for cp in copies:
            cp.wait()

    return pl.pallas_call(
        body, out_shape=tuple(jax.ShapeDtypeStruct((4,) + g.shape[1:], g.dtype) for g in grads),
        in_specs=[ANY] * n, out_specs=(ANY,) * n,
        scratch_shapes=[pltpu.SemaphoreType.DMA((4 * n,)), pltpu.SemaphoreType.DMA((4 * n,))], name="rs_sibling",
    )(*grads)


ELEMENTWISE_TILE_BYTES = 1536 * 1024


def _tile2(rows, cols):
    if rows % 256 == 0:
        return 256, cols
    if rows * cols * 4 > ELEMENTWISE_TILE_BYTES and cols % 256 == 0:
        return rows, 256
    return rows, cols


def add_sibling(grads, received, core, name):
    _, rows, width = grads.shape
    tr, tc = _tile2(rows, width)

    def body(c_ref, g_ref, r_ref, o_ref):
        o_ref[...] = (g_ref[...] + r_ref[...]).astype(BF16)

    blk = (1, tr, tc)
    return pl.pallas_call(
        body,
        grid_spec=pltpu.PrefetchScalarGridSpec(
            num_scalar_prefetch=1, grid=(4, rows // tr, width // tc),
            in_specs=[pl.BlockSpec(blk, lambda k, i, j, c_ref: (2 * k + c_ref[0], i, j)),
                      pl.BlockSpec(blk, lambda k, i, j, c_ref: (k, i, j))],
            out_specs=pl.BlockSpec(blk, lambda k, i, j, c_ref: (k, i, j))),
        out_shape=jax.ShapeDtypeStruct((4, rows, width), BF16), name=name, compiler_params=_params(3),
    )(core, grads, received)


HBM = pl.BlockSpec(memory_space=pltpu.HBM)
SEM = pl.BlockSpec(memory_space=pltpu.SEMAPHORE)
DATAFLOW_EFFECT = pltpu.SideEffectType.DATAFLOW_SIDE_EFFECTING
N_PEERS = N_DEV - 1


def _peer(mask):
    x, y, c = _position()
    px = 1 - x if mask & 4 else x
    py = 1 - y if mask & 2 else y
    pc = 1 - c if mask & 1 else c
    return (px, py, pc), 4 * px + 2 * py + pc


ALL_PEERS = tuple(range(1, N_DEV))
OTHER_CHIPS = (4, 2, 6)


SIBLING = 1
GATHER_MODES = ("gather", "near")


def _exchange_peers(mode):
    return {"chips": OTHER_CHIPS, "near": (SIBLING,) + OTHER_CHIPS}.get(mode, ALL_PEERS)


def _direct_copies(src_refs, land_refs, send_sems, recv_sems, mode):
    x, y, c = _position()
    me = 4 * x + 2 * y + c
    masks = _exchange_peers(mode)
    copies = []
    for a, (src, land) in enumerate(zip(src_refs, land_refs)):
        for slot, mask in enumerate(masks):
            peer, peer_index = _peer(mask)
            k = len(masks) * a + slot
            if mode in GATHER_MODES:
                source, dest = src, land.at[me]
            elif mode == "scatter":
                source, dest = src.at[peer_index], land.at[slot]
            else:
                source, dest = src.at[2 * peer[0] + peer[1]], land.at[slot]
            copies.append(pltpu.make_async_remote_copy(
                src_ref=source, dst_ref=dest, send_sem=send_sems.at[k], recv_sem=recv_sems.at[k], device_id=peer,
                device_id_type=MESH))
    return copies


def forward_to_sibling(slabs, name):
    n = len(slabs)

    def body(*refs):
        out_refs = refs[n:2 * n]
        send_sems, recv_sems = refs[2 * n + 1:]
        x, y, c = _position()
        copies = []
        for a in range(n):
            for slot, mask in enumerate(OTHER_CHIPS):
                _, held = _peer(mask)
                copies.append(pltpu.make_async_remote_copy(
                    src_ref=out_refs[a].at[held], dst_ref=out_refs[a].at[held], send_sem=send_sems.at[3 * a + slot],
                    recv_sem=recv_sems.at[3 * a + slot], device_id=(x, y, 1 - c), device_id_type=MESH))
        for cp in copies:
            cp.start()
        for cp in copies:
            cp.wait()
        refs[2 * n][...] = jnp.zeros_like(refs[2 * n])

    outs = pl.pallas_call(
        body, out_shape=tuple(jax.ShapeDtypeStruct(s.shape, s.dtype) for s in slabs) + (TOKEN,),
        in_specs=[ANY] * n, out_specs=(ANY,) * n + (pl.BlockSpec(memory_space=pltpu.VMEM),),
        input_output_aliases={i: i for i in range(n)},
        scratch_shapes=[pltpu.SemaphoreType.DMA((3 * n,)), pltpu.SemaphoreType.DMA((3 * n,))], name=name,
    )(*slabs)
    return outs[:n], outs[n]


def direct_exchange_start(arrays, mode, name):
    n = len(arrays)
    n_peers = len(_exchange_peers(mode))
    lands = [lax.empty((N_DEV,) + a.shape if mode in GATHER_MODES else (n_peers,) + a.shape[1:], a.dtype)
             for a in arrays]

    def body(*refs):
        src_refs, land_refs = refs[:n], refs[n:2 * n]
        send_sems, recv_sems = refs[2 * n], refs[2 * n + 1]
        token = refs[-1]
        for cp in _direct_copies(src_refs, land_refs, send_sems, recv_sems, mode):
            cp.start()
        token[...] = jnp.zeros_like(token)

    sems = pltpu.SemaphoreType.DMA((n_peers * n,))
    outs = pl.pallas_call(
        body, name=name,
        out_shape=(sems, sems) + tuple(pltpu.HBM(a.shape, a.dtype) for a in arrays)
        + tuple(pltpu.HBM(l.shape, l.dtype) for l in lands) + (TOKEN,),
        in_specs=[HBM] * (2 * n), out_specs=(SEM, SEM) + (HBM,) * (2 * n) + (pl.BlockSpec(memory_space=pltpu.VMEM),),
        input_output_aliases={i: 2 + i for i in range(2 * n)},
        compiler_params=pltpu.CompilerParams(has_side_effects=DATAFLOW_EFFECT),
    )(*[pltpu.with_memory_space_constraint(a, pltpu.HBM) for a in list(arrays) + lands])
    return outs[0], outs[1], outs[2:2 + n], outs[2 + n:2 + 2 * n], outs[-1]


def direct_exchange_wait(send_sems, recv_sems, arrays, lands, after, mode, name):
    n = len(arrays)

    def body(*refs):
        src_refs, land_refs = refs[:n], refs[n:2 * n]
        send_sems, recv_sems = refs[2 * n], refs[2 * n + 1]
        for cp in _direct_copies(src_refs, land_refs, send_sems, recv_sems, mode):
            cp.wait_send()
            cp.wait_recv()
        refs[-1][...] = jnp.zeros_like(refs[-1])

    outs = pl.pallas_call(
        body, name=name,
        out_shape=tuple(pltpu.HBM(a.shape, a.dtype) for a in arrays) + tuple(pltpu.HBM(l.shape, l.dtype) for l in lands)
        + (TOKEN,),
        in_specs=[HBM] * (2 * n) + [SEM, SEM, pl.BlockSpec(memory_space=pl.ANY)],
        out_specs=(HBM,) * (2 * n) + (pl.BlockSpec(memory_space=pltpu.VMEM),),
        input_output_aliases={i: i for i in range(2 * n)},
        compiler_params=pltpu.CompilerParams(has_side_effects=DATAFLOW_EFFECT),
    )(*arrays, *lands, send_sems, recv_sems, after)
    return outs[n:]


def adamw_direct(w, m, v, own, received, name):
    row_per_tile = w.shape[0] != 1
    rows, cols = (w.shape[0], w.shape[2]) if row_per_tile else w.shape[-2:]
    tr, tc = _tile2(rows, cols)

    def body(w_ref, m_ref, v_ref, own_ref, r_ref, g_ref, d_ref, nm_ref, nv_ref):
        gv = own_ref[0]
        for j in range(N_PEERS):
            gv = gv + r_ref[j].astype(F32)
        nm = ADAM_B1 * m_ref[...] + (1.0 - ADAM_B1) * gv
        nv = ADAM_B2 * v_ref[...] + (1.0 - ADAM_B2) * (gv * gv)
        m_hat = nm / (1.0 - ADAM_B1 ** ADAM_STEP)
        v_hat = nv / (1.0 - ADAM_B2 ** ADAM_STEP)
        g_ref[...] = gv
        d_ref[...] = -ADAM_LR * (m_hat / (jnp.sqrt(v_hat) + ADAM_EPS) + ADAM_WD * w_ref[...])
        nm_ref[...] = nm
        nv_ref[...] = nv

    if row_per_tile:
        one = pl.BlockSpec((tr, None, tc), lambda i, j: (i, 0, j))
    else:
        one = pl.BlockSpec((None, tr, tc), lambda i, j: (0, i, j))
    out = jax.ShapeDtypeStruct(w.shape, F32)
    return pl.pallas_call(
        body, grid=(rows // tr, cols // tc),
        in_specs=[one, one, one, pl.BlockSpec((1, tr, tc), lambda i, j: (0, i, j)),
                  pl.BlockSpec((N_PEERS, tr, tc), lambda i, j: (0, i, j))],
        out_specs=(one,) * 4, out_shape=(out,) * 4, name=name, compiler_params=_params(2),
    )(w, m, v, own, received)


def all_reduce_small(vals):
    rows, width = vals.shape

    def body(x_ref, out_ref, all_ref, send_sems, recv_sems):
        x, y, c = _position()
        me, sibling = (x, y, c), (x, y, 1 - c)
        chips = [(1 - x, y), (x, 1 - y), (1 - x, 1 - y)]

        def slab(px, py, pc):
            return all_ref.at[4 * px + 2 * py + pc]

        def copy(k, block, to, src=None):
            return pltpu.make_async_remote_copy(
                src_ref=slab(*block) if src is None else src, dst_ref=slab(*block),
                send_sem=send_sems.at[k], recv_sem=recv_sems.at[k], device_id=to, device_id_type=MESH)

        first = [copy(0, me, sibling, src=x_ref)]
        first += [copy(1 + j, me, (*chip, c), src=x_ref) for j, chip in enumerate(chips)]
        for cp in first:
            cp.start()
        all_ref[4 * x + 2 * y + c] = x_ref[...]
        passed = [copy(4 + j, (*chip, c), sibling) for j, chip in enumerate(chips)]
        for j, chip in enumerate(chips):
            copy(1 + j, (*chip, c), me).wait_recv()
            passed[j].start()
        copy(0, sibling, me).wait_recv()
        for j, chip in enumerate(chips):
            copy(4 + j, (*chip, 1 - c), me).wait_recv()
        for cp in first + passed:
            cp.wait_send()
        total = all_ref[0]
        for d in range(1, N_DEV):
            total = total + all_ref[d]
        out_ref[...] = total

    vmem = pl.BlockSpec(memory_space=pltpu.VMEM)
    return pl.pallas_call(
        body, out_shape=(jax.ShapeDtypeStruct(vals.shape, F32), jax.ShapeDtypeStruct((N_DEV, rows, width), F32)),
        in_specs=[vmem], out_specs=(vmem, vmem),
        scratch_shapes=[pltpu.SemaphoreType.DMA((7,)), pltpu.SemaphoreType.DMA((7,))], name="small_allreduce",
    )(vals)[0]


def adamw(w, g, m, v, name):
    shape = w.shape
    w2, g2, m2, v2 = [a.reshape((-1, shape[-1])) for a in (w, g, m, v)]
    rows, cols = w2.shape
    tr = 256 if rows % 256 == 0 else rows

    def body(w_ref, g_ref, m_ref, v_ref, d_ref, nm_ref, nv_ref):
        gv = g_ref[...]
        nm = ADAM_B1 * m_ref[...] + (1.0 - ADAM_B1) * gv
        nv = ADAM_B2 * v_ref[...] + (1.0 - ADAM_B2) * (gv * gv)
        m_hat = nm / (1.0 - ADAM_B1 ** ADAM_STEP)
        v_hat = nv / (1.0 - ADAM_B2 ** ADAM_STEP)
        d_ref[...] = -ADAM_LR * (m_hat / (jnp.sqrt(v_hat) + ADAM_EPS) + ADAM_WD * w_ref[...])
        nm_ref[...] = nm
        nv_ref[...] = nv

    blk = pl.BlockSpec((tr, cols), lambda i: (i, 0))
    out = jax.ShapeDtypeStruct((rows, cols), F32)
    outs = pl.pallas_call(
        body, grid=(rows // tr,), in_specs=[blk] * 4, out_specs=(blk,) * 3, out_shape=(out,) * 3,
        name=name, compiler_params=_params(1),
    )(w2, g2, m2, v2)
    return tuple(o.reshape(shape) for o in outs)


def adamw_summed(w, m, v, grads, from_sibling, received, me, name):
    rows, cols = w.shape[-2:]
    tr, tc = _tile2(rows, cols)

    def body(me_ref, w_ref, m_ref, v_ref, own_ref, sib_ref, r_ref, g_ref, d_ref, nm_ref, nv_ref):
        gv = own_ref[0] + sib_ref[0]
        for j in range(3):
            gv = gv + r_ref[j].astype(F32)
        nm = ADAM_B1 * m_ref[0] + (1.0 - ADAM_B1) * gv
        nv = ADAM_B2 * v_ref[0] + (1.0 - ADAM_B2) * (gv * gv)
        m_hat = nm / (1.0 - ADAM_B1 ** ADAM_STEP)
        v_hat = nv / (1.0 - ADAM_B2 ** ADAM_STEP)
        g_ref[0] = gv
        d_ref[0] = -ADAM_LR * (m_hat / (jnp.sqrt(v_hat) + ADAM_EPS) + ADAM_WD * w_ref[0])
        nm_ref[0] = nm
        nv_ref[0] = nv

    one = pl.BlockSpec((1, tr, tc), lambda i, j, me_ref: (0, i, j))
    out = jax.ShapeDtypeStruct((1, rows, cols), F32)
    return pl.pallas_call(
        body,
        grid_spec=pltpu.PrefetchScalarGridSpec(
            num_scalar_prefetch=1, grid=(rows // tr, cols // tc),
            in_specs=[one, one, one, pl.BlockSpec((1, tr, tc), lambda i, j, me_ref: (me_ref[0], i, j)),
                      pl.BlockSpec((1, tr, tc), lambda i, j, me_ref: (me_ref[1], i, j)),
                      pl.BlockSpec((3, tr, tc), lambda i, j, me_ref: (0, i, j))],
            out_specs=(one,) * 4),
        out_shape=(out,) * 4, name=name, compiler_params=_params(2),
    )(me, w, m, v, grads, from_sibling, received)


SMALL_VECTORS = ("ffn1_norm", "mix_norm", "ffn2_norm", "final_norm")


def _pack_small(gs):
    row = jnp.concatenate([gs["gdn_a_log"].reshape(-1), gs["gdn_dt_bias"].reshape(-1), gs["gdn_out_norm"].reshape(-1)])
    rows = [gs[n].reshape(1, D_MODEL) for n in SMALL_VECTORS]
    rows.append(jnp.pad(row, (0, D_MODEL - row.shape[0])).reshape(1, D_MODEL))
    rows.append(gs["gdn_conv_w"].reshape(-1, D_MODEL))
    packed = jnp.concatenate(rows, axis=0)
    return jnp.pad(packed, ((0, SMALL_ROWS - packed.shape[0]), (0, 0)))


def _unpack_small(packed):
    out = {n: packed[i].reshape(1, D_MODEL) for i, n in enumerate(SMALL_VECTORS)}
    row = packed[len(SMALL_VECTORS)]
    out["gdn_a_log"] = row[:GDN_HEADS].reshape(1, GDN_HEADS)
    out["gdn_dt_bias"] = row[GDN_HEADS:2 * GDN_HEADS].reshape(1, GDN_HEADS)
    out["gdn_out_norm"] = row[2 * GDN_HEADS:2 * GDN_HEADS + GDN_HEAD_DIM].reshape(1, GDN_HEAD_DIM)
    first = len(SMALL_VECTORS) + 1
    out["gdn_conv_w"] = packed[first:first + GDN_CONV * 3].reshape(GDN_CONV, 3 * GDN_WIDTH)
    return out


WEIGHTS = ("ffn1_norm", "ffn1_w_gate", "ffn1_w_up", "ffn1_w_down", "mix_norm", "w_in", "gdn_conv_w", "gdn_a_log",
           "gdn_dt_bias", "gdn_out_norm", "w_branch_a", "w_branch_b", "w_out", "ffn2_norm", "ffn2_w_gate",
           "ffn2_w_up", "ffn2_w_down", "final_norm")


def kernel(x, ffn1_norm, ffn1_w_gate, ffn1_w_up, ffn1_w_down, mix_norm, w_in, gdn_conv_w, gdn_a_log, gdn_dt_bias, gdn_out_norm, w_branch_a, w_branch_b, w_out, ffn2_norm, ffn2_w_gate, ffn2_w_up, ffn2_w_down, final_norm, loss_target, m_ffn1_norm, m_ffn1_w_gate, m_ffn1_w_up, m_ffn1_w_down, m_mix_norm, m_w_in, m_gdn_conv_w, m_gdn_a_log, m_gdn_dt_bias, m_gdn_out_norm, m_w_branch_a, m_w_branch_b, m_w_out, m_ffn2_norm, m_ffn2_w_gate, m_ffn2_w_up, m_ffn2_w_down, m_final_norm, v_ffn1_norm, v_ffn1_w_gate, v_ffn1_w_up, v_ffn1_w_down, v_mix_norm, v_w_in, v_gdn_conv_w, v_gdn_a_log, v_gdn_dt_bias, v_gdn_out_norm, v_w_branch_a, v_w_branch_b, v_w_out, v_ffn2_norm, v_ffn2_w_gate, v_ffn2_w_up, v_ffn2_w_down, v_final_norm):
    given = dict(locals())
    px, py, pc = _position()
    big_names = list(BIG_WEIGHTS)

    def shard_view(a, n):
        if n == "w_in":
            return a.transpose(2, 0, 1)
        return a.transpose(0, 2, 1) if n in TRANSPOSED else a

    def shard_unview(a, n):
        if n == "w_in":
            return a.transpose(1, 2, 0)
        return a.transpose(0, 2, 1) if n in TRANSPOSED else a

    me = 4 * px + 2 * py + pc
    me_index = me.astype(jnp.int32).reshape(1)
    late = [n for n in big_names if n.startswith("ffn2")]
    early = [n for n in big_names if n not in late]
    shards = {n: shard_view(given[n], n).reshape(given[n].shape[-1 if n in TRANSPOSED else -2], -1).astype(BF16)
              for n in big_names}
    first = [n for n in early if n.startswith("ffn1")]
    middle = [n for n in early if n not in first]
    first_slabs, first_done = all_gather_shards([shards[n] for n in first], "gather_ffn1")
    shards["gdn_conv_w"] = gdn_conv_w[0]
    middle_all = middle + ["gdn_conv_w"]
    middle_gather = direct_exchange_start([_after(shards[n], first_done) for n in middle_all], "near",
                                          "gather_mixer_start")
    ffn1_norm = _after(ffn1_norm, middle_gather[4])
    w = dict(zip(first, first_slabs))
    x1, ffn1_saved = ffn_forward(x[0], ffn1_norm, w, "ffn1")
    near_lands = direct_exchange_wait(*middle_gather[:4], x1, "near", "gather_mixer_wait")[:-1]
    near_lands = [lax.dynamic_update_slice(land, shards[n][None], (me, 0, 0)) for n, land in zip(middle_all, near_lands)]
    middle_slabs, middle_done = forward_to_sibling(near_lands, "gather_mixer_forward")
    gathered = dict(zip(middle_all, middle_slabs))
    late_gather = direct_exchange_start([_after(shards[n], middle_done) for n in late], "gather", "gather_ffn2_start")
    w["w_in_t"] = gathered["w_in"].reshape(-1, D_MODEL)
    w["w_branch_a"] = gathered["w_branch_a"].transpose(1, 0, 2).reshape(256, D_MODEL)
    w["w_branch_b"] = gathered["w_branch_b"].reshape(D_MODEL, D_MODEL)
    w["w_out"] = gathered["w_out"].reshape(D_MODEL, D_MODEL)
    conv_full = gathered["gdn_conv_w"].transpose(1, 0, 2).reshape(GDN_CONV, 3 * GDN_WIDTH)
    small = dict(mix_norm=_after(mix_norm, late_gather[4]), gdn_a_log=gdn_a_log, gdn_dt_bias=gdn_dt_bias,
                 gdn_out_norm=gdn_out_norm, gdn_conv_w=conv_full)

    x2, mixer_saved = mixer_forward(x1, w, small)
    late_lands = direct_exchange_wait(*late_gather[:4], x2, "gather", "gather_ffn2_wait")
    for n, land in zip(late, late_lands):
        w[n] = lax.dynamic_update_slice(land, shards[n][None], (me, 0, 0))
    x3, ffn2_saved = ffn_forward(x2, ffn2_norm, w, "ffn2")
    loss_local, dx3, g_final = loss_head(x3, loss_target[0], final_norm.reshape(1, D_MODEL))
    loss = lax.psum(loss_local, ("x", "y", "c"))
    dx2, g_ffn2_norm, (dw2, dw2_own) = ffn_backward(dx3, ffn2_saved, ffn2_norm, w, "ffn2", me_index)
    late_scatter = direct_exchange_start(list(dw2), "scatter", "rs_ffn2_start")
    w_after = dict(w, w_out=_after(w["w_out"], late_scatter[4]))
    dx1, g_w = mixer_backward(dx2, mixer_saved, w_after, small)
    middle = ["w_in", "w_branch_a", "w_branch_b", "w_out"]
    g_big = dict(w_in=g_w["w_in_t"].reshape(N_DEV, -1, D_MODEL),
                 w_branch_a=g_w["w_branch_a"].reshape(256, N_DEV, 128).transpose(1, 0, 2),
                 w_branch_b=g_w["w_branch_b"].reshape(N_DEV, 128, D_MODEL),
                 w_out=g_w["w_out"].reshape(N_DEV, 128, D_MODEL))
    own = dict(zip(late, dw2_own))
    own.update({n: lax.dynamic_index_in_dim(g_big[n], me, 0, keepdims=True) for n in middle[1:]})
    in_rows = g_w["w_in_t"].shape[0] // N_DEV
    own["w_in"] = lax.dynamic_slice(g_w["w_in_t"], (me * in_rows, 0), (in_rows, D_MODEL))[None]
    middle_scatter = direct_exchange_start([g_big[n].astype(BF16) for n in middle], "scatter", "rs_mixer_start")
    grad_x, g_ffn1_norm, dw1 = ffn_backward(dx1, ffn1_saved, _after(ffn1_norm, middle_scatter[4]), w, "ffn1")
    g_small = dict(ffn1_norm=g_ffn1_norm, ffn2_norm=g_ffn2_norm, final_norm=g_final,
                   **{n: g_w[n] for n in ("mix_norm", "gdn_a_log", "gdn_dt_bias", "gdn_out_norm", "gdn_conv_w")})

    first = [n for n in early if n.startswith("ffn1")]
    g_list = list(dw1)
    core = pc.astype(jnp.int32).reshape(1)
    me_and_chip = jnp.stack([me, 2 * px + py]).astype(jnp.int32)
    from_sibling = exchange_with_sibling(g_list)
    partials = [add_sibling(g, r, core, "rs_add_" + n) for n, g, r in zip(first, g_list, from_sibling)]
    first_chips = direct_exchange_start(partials, "chips", "rs_ffn1_start")

    def state_of(n):
        return [shard_view(given[p + n], n) for p in ("", "m_", "v_")]

    results = {}
    late_received = direct_exchange_wait(*late_scatter[:4], first_chips[4], "scatter", "rs_ffn2_wait")
    middle_received = direct_exchange_wait(*middle_scatter[:4], first_chips[4], "scatter", "rs_mixer_wait")
    for n, recv in zip(late + middle, list(late_received[:-1]) + list(middle_received[:-1])):
        outs = adamw_direct(*state_of(n), own[n], recv, "adamw_" + n)
        results[n] = tuple(shard_unview(o, n) for o in outs)

    done = results["w_out"][1]
    from_chips = direct_exchange_wait(*first_chips[:4], done, "chips", "rs_ffn1_wait")
    for n, g, sib, recv in zip(first, g_list, from_sibling, from_chips):
        outs = adamw_summed(*state_of(n), g, sib, recv, me_and_chip, "adamw_" + n)
        results[n] = tuple(shard_unview(o, n) for o in outs)

    small_sum = _unpack_small(all_reduce_small(_after(_pack_small(g_small), from_chips[-1])))
    conv_cols = CONV_SHARD[1]
    small_sum["gdn_conv_w"] = lax.dynamic_slice(small_sum["gdn_conv_w"], (0, me * conv_cols), (GDN_CONV, conv_cols))
    for n in WEIGHTS:
        if n not in results:
            g = small_sum[n].reshape(given[n].shape)
            results[n] = (g,) + adamw(given[n], g, given["m_" + n], given["v_" + n], "adamw_" + n)

    outs = [[results[n][i] for n in WEIGHTS] for i in range(4)]
    return (loss, grad_x[None], *outs[0], *outs[1], *outs[2], *outs[3])
```

```python
import jax
import jax.numpy as jnp
from jax import lax
from jax.experimental import pallas as pl
from jax.experimental.pallas import tpu as pltpu

F32 = jnp.float32
BF16 = jnp.bfloat16
HI = lax.Precision.HIGHEST
MESH = pl.DeviceIdType.MESH

N_DEV = 8
D_MODEL = 1024
EPS = 1e-6
ROPE_THETA = 10000.0
DSW_DILATIONS = (1, 4, 16)
DSW_HEADS_PER_GROUP = 4
DSW_HEAD_DIM = 64
DSW_BLOCK = 128
GDN_HEADS = 8
GDN_HEAD_DIM = 128
GDN_WIDTH = 1024
GDN_CONV = 4
GDN_CHUNK = 64

ADAM_LR = 0.001
ADAM_B1 = 0.9
ADAM_B2 = 0.999
ADAM_EPS = 1e-08
ADAM_WD = 0.01
ADAM_STEP = 10

VMEM_LIMIT_BYTES = 56 * 1024 * 1024
LANES = 128

NN = (((1,), (0,)), ((), ()))
NT = (((1,), (1,)), ((), ()))
TN = (((0,), (0,)), ((), ()))


def _params(n_grid):
    return pltpu.CompilerParams(dimension_semantics=("arbitrary",) * n_grid, vmem_limit_bytes=VMEM_LIMIT_BYTES)


def _tile(n, pref):
    best = None
    t = LANES
    while t <= min(n, pref):
        if n % t == 0:
            best = t
        t += LANES
    return n if best is None else best


def _matmul(a, b, *, name, ta=False, tb=False, res=None, scale=1.0):
    K, M = a.shape if ta else a.shape[::-1]
    N = b.shape[0] if tb else b.shape[1]
    assert (b.shape[1] if tb else b.shape[0]) == K, (a.shape, b.shape, ta, tb)
    tm = _tile(M, 512)
    tn = _tile(N, 512)
    dn = (((0 if ta else 1,), (1 if tb else 0,)), ((), ()))

    def body(*refs):
        a_ref, b_ref = refs[:2]
        o_ref = refs[-1]
        acc = lax.dot_general(a_ref[...].astype(BF16), b_ref[...].astype(BF16), dn, preferred_element_type=F32)
        if scale != 1.0:
            acc = acc * scale
        if res is not None:
            acc = refs[2][...] + acc
        o_ref[...] = acc

    a_spec = pl.BlockSpec((K, tm), lambda i, j: (0, i)) if ta else pl.BlockSpec((tm, K), lambda i, j: (i, 0))
    b_spec = pl.BlockSpec((tn, K), lambda i, j: (j, 0)) if tb else pl.BlockSpec((K, tn), lambda i, j: (0, j))
    o_spec = pl.BlockSpec((tm, tn), lambda i, j: (i, j))
    ins, specs = [a, b], [a_spec, b_spec]
    if res is not None:
        ins.append(res)
        specs.append(o_spec)
    return pl.pallas_call(
        body, grid=(M // tm, N // tn), in_specs=specs, out_specs=o_spec,
        out_shape=jax.ShapeDtypeStruct((M, N), F32), name=name, compiler_params=_params(2),
    )(*ins)


def _rw_specs(arrs, tm, nblk):
    return [pl.BlockSpec((tm, a.shape[1] // nblk), lambda i, j: (i, j)) for a in arrs]


def _rowwise_fwd(fn, name, rows, consts, params, tm, nblk):
    n_rows = rows[0].shape[0]
    tm = min(tm, n_rows)
    ins = list(rows) + list(consts)
    avals = [jax.ShapeDtypeStruct((tm, a.shape[1] // nblk), a.dtype) for a in ins]
    avals += [jax.ShapeDtypeStruct(p.shape, p.dtype) for p in params]
    out_avals = jax.eval_shape(fn, *avals)
    n_in = len(ins) + len(params)

    def body(*refs):
        outs = fn(*[r[...] for r in refs[:n_in]])
        for r, o in zip(refs[n_in:], outs):
            r[...] = o.astype(r.dtype)

    return pl.pallas_call(
        body, grid=(n_rows // tm, nblk),
        in_specs=_rw_specs(ins, tm, nblk) + [pl.BlockSpec(p.shape, lambda i, j: (0, 0)) for p in params],
        out_specs=tuple(pl.BlockSpec((tm, o.shape[1]), lambda i, j: (i, j)) for o in out_avals),
        out_shape=tuple(jax.ShapeDtypeStruct((n_rows, o.shape[1] * nblk), o.dtype) for o in out_avals),
        name=name, compiler_params=_params(2),
    )(*ins, *params)


def _rowwise_bwd(fn, name, rows, consts, params, cts, tm, nblk):
    n_rows = rows[0].shape[0]
    tm = min(tm, n_rows)
    nr, nc, npar, nct = len(rows), len(consts), len(params), len(cts)

    def body(*refs):
        rv = [r[...] for r in refs[:nr]]
        cv = [r[...] for r in refs[nr:nr + nc]]
        pv = [r[...] for r in refs[nr + nc:nr + nc + npar]]
        ctv = [r[...] for r in refs[nr + nc + npar:nr + nc + npar + nct]]
        outs = refs[nr + nc + npar + nct:]
        _, vjp = jax.vjp(lambda *d: fn(*d[:nr], *cv, *d[nr:]), *rv, *pv)
        grads = vjp(tuple(ctv))
        for k in range(nr):
            outs[k][...] = grads[k]
        first = jnp.logical_and(pl.program_id(0) == 0, pl.program_id(1) == 0)
        for k in range(npar):
            ref = outs[nr + k]

            @pl.when(first)
            def _(ref=ref):
                ref[...] = jnp.zeros_like(ref)

            ref[...] += grads[nr + k]

    ins = list(rows) + list(consts)
    return pl.pallas_call(
        body, grid=(n_rows // tm, nblk),
        in_specs=(_rw_specs(ins, tm, nblk) + [pl.BlockSpec(p.shape, lambda i, j: (0, 0)) for p in params]
                  + _rw_specs(cts, tm, nblk)),
        out_specs=tuple(_rw_specs(rows, tm, nblk) + [pl.BlockSpec(p.shape, lambda i, j: (0, 0)) for p in params]),
        out_shape=tuple([jax.ShapeDtypeStruct(a.shape, F32) for a in rows]
                        + [jax.ShapeDtypeStruct(p.shape, F32) for p in params]),
        name=name, compiler_params=_params(2),
    )(*ins, *params, *cts)


def _merge_fn(ga, gb, pa, pb):
    return (jax.nn.sigmoid(ga) * pa + jax.nn.sigmoid(gb) * pb,)


def _outnorm_gate_fn(o, gate, gain):
    y = o * lax.rsqrt(jnp.mean(o * o, axis=-1, keepdims=True) + EPS) * gain
    return (y * (gate * jax.nn.sigmoid(gate)),)


def _beta_decay_fn(beta_raw, decay_raw, a_log, dt_bias):
    z = decay_raw + dt_bias
    softplus = jnp.maximum(z, 0.0) + jnp.log(1.0 + jnp.exp(-jnp.abs(z)))
    g = -jnp.exp(a_log) * softplus
    rows = g.shape[0]
    ii = lax.broadcasted_iota(jnp.int32, (rows, rows), 0)
    jj = lax.broadcasted_iota(jnp.int32, (rows, rows), 1)
    same_chunk_before = jnp.logical_and(jj <= ii, jj // GDN_CHUNK == ii // GDN_CHUNK).astype(F32)
    gcum = lax.dot_general(same_chunk_before, g, NN, precision=HI, preferred_element_type=F32)
    return jax.nn.sigmoid(beta_raw), gcum


def _combine_fn(o0, o1, o2, l0, l1, l2):
    m = lax.stop_gradient(jnp.maximum(jnp.maximum(l0, l1), l2))
    e0, e1, e2 = jnp.exp(l0 - m), jnp.exp(l1 - m), jnp.exp(l2 - m)
    return ((e0 * o0 + e1 * o1 + e2 * o2) / (e0 + e1 + e2),)


def _loss_fn(x, target, gain):
    y = x * lax.rsqrt(jnp.mean(x * x, axis=-1, keepdims=True) + EPS) * gain
    err = y - target
    return (0.5 * jnp.mean(err * err, axis=-1, keepdims=True),)


def _rotate(v, cos, sin):
    half = DSW_HEAD_DIM // 2
    lane = lax.broadcasted_iota(jnp.int32, cos.shape, 1)
    low = (lane % DSW_HEAD_DIM) < half
    slabs = []
    for s in range(v.shape[1] // LANES):
        x = v[:, s * LANES:(s + 1) * LANES]
        swapped = jnp.where(low, pltpu.roll(x, LANES - half, 1), pltpu.roll(x, half, 1))
        slabs.append(x * cos + swapped * sin)
    return jnp.concatenate(slabs, axis=1)


def _rope_tables(n_tokens):
    half = DSW_HEAD_DIM // 2
    inv_freq = ROPE_THETA ** (-jnp.arange(half, dtype=F32) / half)
    ang = jnp.arange(n_tokens, dtype=F32)[:, None] * inv_freq[None, :]
    cos, sin = jnp.cos(ang), jnp.sin(ang)
    return jnp.tile(jnp.concatenate([cos, cos], 1), (1, 2)), jnp.tile(jnp.concatenate([-sin, sin], 1), (1, 2))


def _attn_probs(q, kp, kc, group, n):
    blk = DSW_BLOCK
    k = _each(lambda a, b: jnp.concatenate([a, b], axis=0).astype(BF16), kp, kc)
    s = _each(lambda a, b: lax.dot_general(a.astype(BF16), b, NT, preferred_element_type=F32)
              * (DSW_HEAD_DIM ** -0.5), q, k)
    blocks_per_seq = jnp.where(group == 0, 16, jnp.where(group == 1, 4, 1))
    first = (n % blocks_per_seq) == 0
    qi = lax.broadcasted_iota(jnp.int32, (blk, 2 * blk), 0)
    kj = lax.broadcasted_iota(jnp.int32, (blk, 2 * blk), 1)
    dist = qi + blk - kj
    valid = (dist >= 0) & (dist <= blk) & jnp.logical_or(kj >= blk, jnp.logical_not(first))
    s = _each(lambda a: jnp.where(valid, a, -1e30), s)
    m = _each(lambda a: jnp.max(a, axis=-1, keepdims=True), s)
    p = _each(lambda a, b: jnp.exp(a - b), s, m)
    l = _each(lambda a: jnp.sum(a, axis=-1, keepdims=True), p)
    return _each(lambda a, b: a / b, p, l), _each(lambda a, b: a + jnp.log(b), m, l), k


GROUP_WIDTH = DSW_HEADS_PER_GROUP * DSW_HEAD_DIM


def _attn_specs(n_tokens):
    blk = DSW_BLOCK
    cur = pl.BlockSpec((1, blk, GROUP_WIDTH), lambda g, n: (g, n, 0))
    prev = pl.BlockSpec((1, blk, GROUP_WIDTH), lambda g, n: (g, jnp.maximum(n - 1, 0), 0))
    return cur, prev


def _heads_of(ref):
    x = ref[0]
    return [x[:, h * DSW_HEAD_DIM:(h + 1) * DSW_HEAD_DIM] for h in range(DSW_HEADS_PER_GROUP)]


def _group_of(heads):
    return jnp.concatenate(heads, axis=1)


def _attn_fwd(q, k, v):
    n_groups, n_tokens, _ = q.shape
    cur, prev = _attn_specs(n_tokens)

    def body(q_ref, kp_ref, kc_ref, vp_ref, vc_ref, o_ref, l_ref):
        p, lse, _ = _attn_probs(_heads_of(q_ref), _heads_of(kp_ref), _heads_of(kc_ref),
                                pl.program_id(0), pl.program_id(1))
        vv = _each(lambda a, b: jnp.concatenate([a, b], axis=0).astype(BF16), _heads_of(vp_ref), _heads_of(vc_ref))
        o = _each(lambda a, b: lax.dot_general(a.astype(BF16), b, NN, preferred_element_type=F32), p, vv)
        lse_wide = _each(lambda a: jnp.broadcast_to(a, (DSW_BLOCK, DSW_HEAD_DIM)), lse)
        o_ref[0] = _group_of(o)
        l_ref[0] = _group_of(lse_wide)

    return pl.pallas_call(
        body, grid=(n_groups, n_tokens // DSW_BLOCK), in_specs=[cur, prev, cur, prev, cur],
        out_specs=(cur, cur), out_shape=(jax.ShapeDtypeStruct(q.shape, F32), jax.ShapeDtypeStruct(q.shape, F32)),
        name="attn_fwd", compiler_params=_params(2),
    )(q, k, k, v, v)


def _attn_bwd(q, k, v, do, dlse):
    n_groups, n_tokens, _ = q.shape
    nblk = n_tokens // DSW_BLOCK
    cur, prev = _attn_specs(n_tokens)
    part = pl.BlockSpec((1, 1, 2 * DSW_BLOCK, GROUP_WIDTH), lambda g, n: (g, n, 0, 0))
    scale = DSW_HEAD_DIM ** -0.5

    def body(q_ref, kp_ref, kc_ref, vp_ref, vc_ref, do_ref, dl_ref, dq_ref, dk_ref, dv_ref):
        qs = _heads_of(q_ref)
        p, _, kb = _attn_probs(qs, _heads_of(kp_ref), _heads_of(kc_ref), pl.program_id(0), pl.program_id(1))
        qb = _each(lambda a: a.astype(BF16), qs)
        vv = _each(lambda a, b: jnp.concatenate([a, b], axis=0).astype(BF16), _heads_of(vp_ref), _heads_of(vc_ref))
        dob = _each(lambda a: a.astype(BF16), _heads_of(do_ref))
        dp = _each(lambda a, b: lax.dot_general(a, b, NT, preferred_element_type=F32), dob, vv)
        dv = _each(lambda a, b: lax.dot_general(a.astype(BF16), b, TN, preferred_element_type=F32), p, dob)
        dl = _each(lambda a: jnp.sum(a, axis=-1, keepdims=True), _heads_of(dl_ref))
        ds = _each(lambda a, b, c: (a * (b - jnp.sum(b * a, axis=-1, keepdims=True) + c) * scale).astype(BF16),
                   p, dp, dl)
        dq = _each(lambda a, b: lax.dot_general(a, b, NN, preferred_element_type=F32), ds, kb)
        dk = _each(lambda a, b: lax.dot_general(a, b, TN, preferred_element_type=F32), ds, qb)
        dq_ref[0] = _group_of(dq)
        dk_ref[0, 0] = _group_of(dk)
        dv_ref[0, 0] = _group_of(dv)

    partial_shape = jax.ShapeDtypeStruct((n_groups, nblk, 2 * DSW_BLOCK, GROUP_WIDTH), F32)
    dq, dkp, dvp = pl.pallas_call(
        body, grid=(n_groups, nblk), in_specs=[cur, prev, cur, prev, cur, cur, cur],
        out_specs=(cur, part, part), out_shape=(jax.ShapeDtypeStruct(q.shape, F32), partial_shape, partial_shape),
        name="attn_bwd", compiler_params=_params(2),
    )(q, k, k, v, v, do, dlse)

    def fold(partial):
        own = partial[:, :, DSW_BLOCK:]
        from_next = jnp.pad(partial[:, 1:, :DSW_BLOCK], ((0, 0), (0, 1), (0, 0), (0, 0)))
        return (own + from_next).reshape(n_groups, n_tokens, GROUP_WIDTH)

    return dq, fold(dkp), fold(dvp)


def _to_heads(a):
    n_tokens = a.shape[0]
    outs = []
    for gi, d in enumerate(DSW_DILATIONS):
        blk = a[:, gi * GROUP_WIDTH:(gi + 1) * GROUP_WIDTH].reshape(n_tokens // d, d, GROUP_WIDTH)
        outs.append(blk.transpose(1, 0, 2).reshape(1, n_tokens, GROUP_WIDTH))
    return jnp.concatenate(outs, 0)


def _from_heads(a):
    n_tokens = a.shape[1]
    return [a[gi].reshape(d, n_tokens // d, GROUP_WIDTH).transpose(1, 0, 2).reshape(n_tokens, GROUP_WIDTH)
            for gi, d in enumerate(DSW_DILATIONS)]


CONV_TILE = 512


def _shift_down(x, k, rows):
    return x if k == 0 else jnp.where(rows >= k, pltpu.roll(x, k, 0), 0.0)


def _shift_up(x, k, rows):
    n = x.shape[0]
    return x if k == 0 else jnp.where(rows < n - k, pltpu.roll(x, n - k, 0), 0.0)


def _conv_pre(x, w):
    rows = lax.broadcasted_iota(jnp.int32, x.shape, 0)
    acc = x * w[GDN_CONV - 1:GDN_CONV]
    for k in range(1, GDN_CONV):
        acc = acc + _shift_down(x, k, rows) * w[GDN_CONV - 1 - k:GDN_CONV - k]
    return acc, rows


def _conv_fwd(x, w):
    n_tokens, width = x.shape
    big = pl.BlockSpec((n_tokens, CONV_TILE), lambda j: (0, j))
    wsp = pl.BlockSpec((GDN_CONV, CONV_TILE), lambda j: (0, j))

    def body(x_ref, w_ref, o_ref):
        acc, _ = _conv_pre(x_ref[...], w_ref[...])
        o_ref[...] = acc * jax.nn.sigmoid(acc)

    return pl.pallas_call(
        body, grid=(width // CONV_TILE,), in_specs=[big, wsp], out_specs=big,
        out_shape=jax.ShapeDtypeStruct(x.shape, F32), name="conv_fwd", compiler_params=_params(1),
    )(x, w)


def _conv_bwd(x, w, dy):
    n_tokens, width = x.shape
    big = pl.BlockSpec((n_tokens, CONV_TILE), lambda j: (0, j))
    wsp = pl.BlockSpec((GDN_CONV, CONV_TILE), lambda j: (0, j))

    def body(x_ref, w_ref, dy_ref, dx_ref, dw_ref):
        xv, wv = x_ref[...], w_ref[...]
        acc, rows = _conv_pre(xv, wv)
        sg = jax.nn.sigmoid(acc)
        dacc = dy_ref[...] * (sg + acc * sg * (1.0 - sg))
        dx = dacc * wv[GDN_CONV - 1:GDN_CONV]
        for k in range(1, GDN_CONV):
            dx = dx + _shift_up(dacc, k, rows) * wv[GDN_CONV - 1 - k:GDN_CONV - k]
        dx_ref[...] = dx
        for k in range(GDN_CONV):
            dw_ref[GDN_CONV - 1 - k:GDN_CONV - k, :] = jnp.sum(dacc * _shift_down(xv, k, rows), axis=0, keepdims=True)

    return pl.pallas_call(
        body, grid=(width // CONV_TILE,), in_specs=[big, wsp, big], out_specs=(big, wsp),
        out_shape=(jax.ShapeDtypeStruct(x.shape, F32), jax.ShapeDtypeStruct(w.shape, F32)),
        name="conv_bwd", compiler_params=_params(1),
    )(x, w, dy)


def _dot(a, b, dn=NN):
    return lax.dot_general(a, b, dn, precision=HI, preferred_element_type=F32)


def _dot3(a, b, dn=NN):
    return lax.dot_general(a, b, dn, precision=lax.Precision.HIGH, preferred_element_type=F32)


def _bf16_dot(a, b, dn):
    return lax.dot_general(a.astype(BF16), b.astype(BF16), dn, preferred_element_type=F32)


_DOT_GRADS = {NN: (("g", "b", NT), ("a", "g", TN)), NT: (("g", "b", NN), ("g", "a", TN)),
              TN: (("b", "g", NT), ("a", "g", NN))}


def _make_bdot(dn):
    @jax.custom_vjp
    def op(a, b):
        return _bf16_dot(a, b, dn)

    def fwd(a, b):
        return op(a, b), (a, b)

    def bwd(saved, g):
        vals = dict(a=saved[0], b=saved[1], g=g)
        return tuple(_bf16_dot(vals[x], vals[y], form) for x, y, form in _DOT_GRADS[dn])

    op.defvjp(fwd, bwd)
    return op


_BDOTS = {dn: _make_bdot(dn) for dn in (NN, NT, TN)}


def _bdot(a, b, dn=NN):
    return _BDOTS[dn](a, b)


def _each(fn, *lists):
    return [fn(*items) for items in zip(*lists)]


@jax.custom_vjp
def _known_inverse(m, inverse):
    return inverse


def _known_inverse_fwd(m, inverse):
    return inverse, inverse


def _known_inverse_bwd(inverse, d_inverse):
    return -_dot3(_dot3(inverse, d_inverse, TN), inverse, NT), jnp.zeros_like(inverse)


_known_inverse.defvjp(_known_inverse_fwd, _known_inverse_bwd)


def _gdn_chunks(q, k, v, b, gcum, state, inverse=None):
    c = GDN_CHUNK
    ii = lax.broadcasted_iota(jnp.int32, (c, c), 0)
    jj = lax.broadcasted_iota(jnp.int32, (c, c), 1)
    qn = _each(lambda x: x * lax.rsqrt(jnp.sum(x * x, axis=-1, keepdims=True) + EPS) * (GDN_HEAD_DIM ** -0.5), q)
    kn = _each(lambda x: x * lax.rsqrt(jnp.sum(x * x, axis=-1, keepdims=True) + EPS), k)
    gcum_i = _each(lambda x: jnp.broadcast_to(x, (c, c)), gcum)
    gcum_j = _each(jnp.transpose, gcum_i)
    decay = _each(lambda x, y: jnp.exp(jnp.where(jj <= ii, x - y, -1e30)), gcum_i, gcum_j)
    g_last = _each(lambda x: x[c - 1:c, :], gcum)
    e_gcum = _each(jnp.exp, gcum)
    kbeta = _each(lambda x, y: x * y, kn, b)
    vbeta = _each(lambda x, y: x * y, v, b)
    m = _each(lambda x, y, d: jnp.where(jj < ii, _bdot(x, y, NT) * d, 0.0), kbeta, kn, decay)
    if inverse is not None:
        inv = _each(_known_inverse, m, inverse)
    else:
        eye = (ii == jj).astype(F32)
        inv = _each(lambda x: eye - x, m)
        power = _each(lambda x: _dot3(x, x), m)
        for step in range(5):
            inv = _each(lambda x, p: x + _dot3(x, p), inv, power)
            if step < 4:
                power = _each(lambda p: _dot3(p, p), power)
    u = _each(_dot3, inv, vbeta)
    w = _each(lambda x, y, e: _dot3(x, y * e), inv, kbeta, e_gcum)
    a_qk = _each(lambda x, y, d: _bdot(x, y, NT) * d, qn, kn, decay)
    v_new = _each(lambda x, y, s: x - _bdot(y, s), u, w, state)
    o = _each(lambda x, e, s, a, vn: _bdot(x * e, s) + _bdot(a, vn), qn, e_gcum, state, a_qk, v_new)
    new_state = _each(lambda s, gl, x, gc, vn: s * jnp.exp(gl) + _bdot(x * jnp.exp(gl - gc), vn, TN),
                      state, g_last, kn, gcum, v_new)
    return o, new_state, inv


GDN_HEADS_PER_STEP = 8


GDN_TIME_TILE = 256


def _gdn_specs(n_tokens, reverse):
    hb, hd, tt = GDN_HEADS_PER_STEP, GDN_HEAD_DIM, GDN_TIME_TILE
    nb, nt = GDN_HEADS // hb, n_tokens // tt

    def when(t):
        return nt - 1 - t if reverse else t

    q = pl.BlockSpec((tt, hb * hd), lambda h, t: (when(t), h))
    k = pl.BlockSpec((tt, hb * hd), lambda h, t: (when(t), nb + h))
    v = pl.BlockSpec((tt, hb * hd), lambda h, t: (when(t), 2 * nb + h))
    vec = pl.BlockSpec((tt, hb), lambda h, t: (when(t), h))
    states = pl.BlockSpec((hb, tt // GDN_CHUNK, hd, hd), lambda h, t: (h, when(t), 0, 0))
    inverses = pl.BlockSpec((hb, tt // GDN_CHUNK, GDN_CHUNK, GDN_CHUNK), lambda h, t: (h, when(t), 0, 0))
    return q, k, v, vec, states, inverses


def _gdn_fwd(qkv, beta, g):
    n_tokens = qkv.shape[0]
    hb, hd, tt = GDN_HEADS_PER_STEP, GDN_HEAD_DIM, GDN_TIME_TILE
    n_chunks = tt // GDN_CHUNK
    q_s, k_s, v_s, vec, st, inv_s = _gdn_specs(n_tokens, False)

    def body(q_ref, k_ref, v_ref, b_ref, g_ref, o_ref, st_ref, inv_ref, state):
        @pl.when(pl.program_id(1) == 0)
        def _():
            state[...] = jnp.zeros_like(state)

        def step(c, carry):
            r = pl.ds(pl.multiple_of(c * GDN_CHUNK, GDN_CHUNK), GDN_CHUNK)
            cols = [slice(h * hd, (h + 1) * hd) for h in range(hb)]
            old = [state[h] for h in range(hb)]
            o, new, inv = _gdn_chunks(
                [q_ref[r, cs] for cs in cols], [k_ref[r, cs] for cs in cols], [v_ref[r, cs] for cs in cols],
                [b_ref[r, h:h + 1] for h in range(hb)], [g_ref[r, h:h + 1] for h in range(hb)], old)
            for h in range(hb):
                st_ref[h, c] = old[h]
                inv_ref[h, c] = inv[h]
                o_ref[r, cols[h]] = o[h]
                state[h] = new[h]
            return carry

        lax.fori_loop(0, n_chunks, step, 0)

    n_all = n_tokens // GDN_CHUNK
    return pl.pallas_call(
        body, grid=(GDN_HEADS // hb, n_tokens // tt), in_specs=[q_s, k_s, v_s, vec, vec], out_specs=(q_s, st, inv_s),
        out_shape=(jax.ShapeDtypeStruct((n_tokens, GDN_WIDTH), F32),
                   jax.ShapeDtypeStruct((GDN_HEADS, n_all, hd, hd), F32),
                   jax.ShapeDtypeStruct((GDN_HEADS, n_all, GDN_CHUNK, GDN_CHUNK), F32)),
        scratch_shapes=[pltpu.VMEM((hb, hd, hd), F32)],
        name="gdn_fwd", compiler_params=_params(2),
    )(qkv, qkv, qkv, beta, g)


def _gdn_bwd(qkv, beta, g, states, inverses, do):
    n_tokens = qkv.shape[0]
    hb, hd, tt = GDN_HEADS_PER_STEP, GDN_HEAD_DIM, GDN_TIME_TILE
    n_chunks = tt // GDN_CHUNK
    q_s, k_s, v_s, vec, st, inv_s = _gdn_specs(n_tokens, True)

    assert hb == GDN_HEADS

    def body(q_ref, k_ref, v_ref, b_ref, g_ref, st_ref, inv_ref, do_ref, dqkv_ref, db_ref, dg_ref, dstate):
        @pl.when(pl.program_id(1) == 0)
        def _():
            dstate[...] = jnp.zeros_like(dstate)

        def step(i, carry):
            c = n_chunks - 1 - i
            r = pl.ds(pl.multiple_of(c * GDN_CHUNK, GDN_CHUNK), GDN_CHUNK)
            cols = [slice(h * hd, (h + 1) * hd) for h in range(hb)]
            args = ([q_ref[r, cs] for cs in cols], [k_ref[r, cs] for cs in cols], [v_ref[r, cs] for cs in cols],
                    [b_ref[r, h:h + 1] for h in range(hb)], [g_ref[r, h:h + 1] for h in range(hb)],
                    [st_ref[h, c] for h in range(hb)])
            saved = [inv_ref[h, c] for h in range(hb)]
            cts = ([do_ref[r, cs] for cs in cols], [dstate[h] for h in range(hb)])
            dq, dk, dv, db, dg, dst = jax.vjp(lambda *a: _gdn_chunks(*a, inverse=saved)[:2], *args)[1](cts)
            for h in range(hb):
                for part, grad in enumerate((dq, dk, dv)):
                    dqkv_ref[r, pl.ds(part * GDN_WIDTH + h * hd, hd)] = grad[h]
                db_ref[r, h:h + 1] = db[h]
                dg_ref[r, h:h + 1] = dg[h]
                dstate[h] = dst[h]
            return carry

        lax.fori_loop(0, n_chunks, step, 0)

    n_t = n_tokens // tt
    thin = jax.ShapeDtypeStruct(beta.shape, F32)
    return pl.pallas_call(
        body, grid=(GDN_HEADS // hb, n_t), in_specs=[q_s, k_s, v_s, vec, vec, st, inv_s, q_s],
        out_specs=(pl.BlockSpec((tt, 3 * GDN_WIDTH), lambda h, t: (n_t - 1 - t, 0)), vec, vec),
        out_shape=(jax.ShapeDtypeStruct(qkv.shape, F32), thin, thin),
        scratch_shapes=[pltpu.VMEM((hb, hd, hd), F32)],
        name="gdn_bwd", compiler_params=_params(2),
    )(qkv, qkv, qkv, beta, g, states, inverses, do)


FFN_ROW_TILE = 256
FFN_CHUNK = 256
FFN_FWD_ROW_TILE = 512


def _resident(shape):
    return pl.BlockSpec(shape, lambda i: (0,) * len(shape), pipeline_mode=pl.Buffered(1))


def _ffn_fwd(x, gain, wg, wu, wd, name):
    n_tokens, d = x.shape
    n_shards, n, _ = wg.shape
    tm = FFN_FWD_ROW_TILE

    def body(x_ref, gain_ref, wg_ref, wu_ref, wd_ref, o_ref, g_ref, u_ref):
        xv = x_ref[...]
        h = (xv * lax.rsqrt(jnp.mean(xv * xv, axis=-1, keepdims=True) + EPS) * gain_ref[...]).astype(BF16)
        acc = jnp.zeros((tm, d), F32)
        for j in range(n_shards):
            g = lax.dot_general(h, wg_ref[j], NT, preferred_element_type=F32)
            u = lax.dot_general(h, wu_ref[j], NT, preferred_element_type=F32)
            g_ref[j] = g
            u_ref[j] = u
            a = (g * jax.nn.sigmoid(g) * u).astype(BF16)
            acc = acc + lax.dot_general(a, wd_ref[j], NN, preferred_element_type=F32)
        o_ref[...] = xv + 0.5 * acc

    row = pl.BlockSpec((tm, d), lambda i: (i, 0))
    hid = pl.BlockSpec((n_shards, tm, n), lambda i: (0, i, 0))
    return pl.pallas_call(
        body, grid=(n_tokens // tm,),
        in_specs=[row, _resident(gain.shape), _resident(wg.shape), _resident(wu.shape), _resident(wd.shape)],
        out_specs=(row, hid, hid),
        out_shape=(jax.ShapeDtypeStruct(x.shape, F32), jax.ShapeDtypeStruct((n_shards, n_tokens, n), F32),
                   jax.ShapeDtypeStruct((n_shards, n_tokens, n), F32)),
        name=name, compiler_params=_params(1),
    )(x, gain, wg, wu, wd)


def _ffn_bwd_rows(x, gain, dy, g, u, wg, wu, wd, name):
    n_tokens, d = x.shape
    n_shards, n, _ = wg.shape
    tm = FFN_ROW_TILE

    def body(x_ref, gain_ref, dy_ref, g_ref, u_ref, wg_ref, wu_ref, wd_ref,
             dx_ref, dgain_ref, h_ref, dyh_ref, a_ref, dg_ref, du_ref):
        xv, dyv, gain_v = x_ref[...], dy_ref[...], gain_ref[...]
        r = lax.rsqrt(jnp.mean(xv * xv, axis=-1, keepdims=True) + EPS)
        xhat = xv * r
        h_ref[...] = (xhat * gain_v).astype(BF16)
        dyh = (0.5 * dyv).astype(BF16)
        dyh_ref[...] = dyh
        dh = jnp.zeros((tm, d), F32)
        for j in range(n_shards):
            da = lax.dot_general(dyh, wd_ref[j], NT, preferred_element_type=F32)
            gv, uv = g_ref[j], u_ref[j]
            sg = jax.nn.sigmoid(gv)
            silu = gv * sg
            a_ref[j] = (silu * uv).astype(BF16)
            dg = (da * uv * (sg + silu * (1.0 - sg))).astype(BF16)
            du = (da * silu).astype(BF16)
            dg_ref[j] = dg
            du_ref[j] = du
            dh = dh + lax.dot_general(dg, wg_ref[j], NN, preferred_element_type=F32)
            dh = dh + lax.dot_general(du, wu_ref[j], NN, preferred_element_type=F32)
        dxhat = dh * gain_v
        dx_ref[...] = dyv + r * (dxhat - xhat * jnp.mean(dxhat * xhat, axis=-1, keepdims=True))

        @pl.when(pl.program_id(0) == 0)
        def _():
            dgain_ref[...] = jnp.zeros_like(dgain_ref)

        dgain_ref[...] += jnp.sum(dh * xhat, axis=0, keepdims=True)

    row = pl.BlockSpec((tm, d), lambda i: (i, 0))
    hid = pl.BlockSpec((n_shards, tm, n), lambda i: (0, i, 0))
    hid_shape = (n_shards, n_tokens, n)
    return pl.pallas_call(
        body, grid=(n_tokens // tm,),
        in_specs=[row, _resident(gain.shape), row, hid, hid, _resident(wg.shape), _resident(wu.shape),
                  _resident(wd.shape)],
        out_specs=(row, pl.BlockSpec(gain.shape, lambda i: (0, 0)), row, row, hid, hid, hid),
        out_shape=(jax.ShapeDtypeStruct(x.shape, F32), jax.ShapeDtypeStruct(gain.shape, F32),
                   jax.ShapeDtypeStruct(x.shape, BF16), jax.ShapeDtypeStruct(x.shape, BF16),
                   jax.ShapeDtypeStruct(hid_shape, BF16), jax.ShapeDtypeStruct(hid_shape, BF16),
                   jax.ShapeDtypeStruct(hid_shape, BF16)),
        name=name, compiler_params=_params(1),
    )(x, gain, dy, g, u, wg, wu, wd)


def _ffn_bwd_weights(h, dyh, a, dg, du, name, with_payload=False):
    n_chunks, n_tokens, n = a.shape
    d = h.shape[1]

    def body(h_ref, dyh_ref, a_ref, dg_ref, du_ref, *out_refs):
        hv = h_ref[...]
        vals = (lax.dot_general(dg_ref[0], hv, TN, preferred_element_type=F32),
                lax.dot_general(du_ref[0], hv, TN, preferred_element_type=F32),
                lax.dot_general(a_ref[0], dyh_ref[...], TN, preferred_element_type=F32))
        for ref, val in zip(out_refs[-3:], vals):
            ref[0] = val
        if with_payload:
            for ref, val in zip(out_refs[:3], vals):
                ref[0] = val.astype(BF16)

    hid = pl.BlockSpec((1, n_tokens, n), lambda j: (j, 0, 0))
    out = pl.BlockSpec((1, n, d), lambda j: (j, 0, 0))
    shapes = (jax.ShapeDtypeStruct((n_chunks, n, d), F32),) * 3
    if with_payload:
        shapes = (jax.ShapeDtypeStruct((n_chunks, n, d), BF16),) * 3 + shapes
    outs = pl.pallas_call(
        body, grid=(n_chunks,), in_specs=[_resident(h.shape), _resident(dyh.shape), hid, hid, hid],
        out_specs=(out,) * len(shapes), out_shape=shapes, name=name, compiler_params=_params(1),
    )(h, dyh, a, dg, du)
    return (outs[:3], outs[3:]) if with_payload else outs


IN_PIECES = (("wq_a", 0, 768), ("wk_a", 768, 1536), ("wv_a", 1536, 2304), ("w_qkvb", 2304, 5376),
             ("w_small", 5376, 5392), ("w_ggate", 5392, 6416), ("w_gatea", 6416, 7440), ("w_gateb", 7440, 8464))
IN_NAMES = tuple(name for name, _, _ in IN_PIECES)


def _in_rows(lo, hi):
    return lo, max(hi, lo + LANES)


N_ROTATED = 2


def _in_proj_fwd(x, gain, wt, cos, sin):
    n_tokens, d = x.shape
    tm = FFN_ROW_TILE
    rows = [_in_rows(lo, hi) for _, lo, hi in IN_PIECES]

    def body(x_ref, gain_ref, wt_ref, cos_ref, sin_ref, *o_refs):
        xv = x_ref[...]
        h = (xv * lax.rsqrt(jnp.mean(xv * xv, axis=-1, keepdims=True) + EPS) * gain_ref[...]).astype(BF16)
        for k, ((lo, hi), o_ref) in enumerate(zip(rows, o_refs)):
            z = lax.dot_general(h, wt_ref[lo:hi, :], NT, preferred_element_type=F32)
            o_ref[...] = _rotate(z, cos_ref[...], sin_ref[...]) if k < N_ROTATED else z

    tab = pl.BlockSpec((tm, LANES), lambda i: (i, 0))
    return pl.pallas_call(
        body, grid=(n_tokens // tm,),
        in_specs=[pl.BlockSpec((tm, d), lambda i: (i, 0)), _resident(gain.shape), _resident(wt.shape), tab, tab],
        out_specs=tuple(pl.BlockSpec((tm, hi - lo), lambda i: (i, 0)) for lo, hi in rows),
        out_shape=tuple(jax.ShapeDtypeStruct((n_tokens, hi - lo), F32) for lo, hi in rows),
        name="in_proj_fwd", compiler_params=_params(1),
    )(x, gain, wt, cos, sin)


def _in_proj_bwd_rows(x, gain, dres, dzs, wt, cos, sin):
    n_tokens, d = x.shape
    tm = FFN_ROW_TILE
    n = len(dzs)
    rows = [_in_rows(lo, hi) for _, lo, hi in IN_PIECES]

    def body(x_ref, gain_ref, dres_ref, cos_ref, sin_ref, *refs):
        dz_refs, wt_ref = refs[:n], refs[n]
        dx_ref, dgain_ref, h_ref = refs[n + 1:n + 4]
        unrotated_refs = refs[n + 4:]
        xv, gain_v = x_ref[...], gain_ref[...]
        r = lax.rsqrt(jnp.mean(xv * xv, axis=-1, keepdims=True) + EPS)
        xhat = xv * r
        h_ref[...] = (xhat * gain_v).astype(BF16)
        dh = jnp.zeros((tm, d), F32)
        for k, (dz_ref, (lo, hi)) in enumerate(zip(dz_refs, rows)):
            dz = dz_ref[...]
            if k < N_ROTATED:
                dz = _rotate(dz, cos_ref[...], -sin_ref[...]).astype(BF16)
                unrotated_refs[k][...] = dz
            dh = dh + lax.dot_general(dz.astype(BF16), wt_ref[lo:hi, :], NN, preferred_element_type=F32)
        dxhat = dh * gain_v
        dx_ref[...] = dres_ref[...] + r * (dxhat - xhat * jnp.mean(dxhat * xhat, axis=-1, keepdims=True))

        @pl.when(pl.program_id(0) == 0)
        def _():
            dgain_ref[...] = jnp.zeros_like(dgain_ref)

        dgain_ref[...] += jnp.sum(dh * xhat, axis=0, keepdims=True)

    row = pl.BlockSpec((tm, d), lambda i: (i, 0))
    tab = pl.BlockSpec((tm, LANES), lambda i: (i, 0))
    dz_specs = [pl.BlockSpec((tm, dz.shape[1]), lambda i: (i, 0)) for dz in dzs]
    outs = pl.pallas_call(
        body, grid=(n_tokens // tm,),
        in_specs=[row, _resident(gain.shape), row, tab, tab] + dz_specs + [_resident(wt.shape)],
        out_specs=(row, pl.BlockSpec(gain.shape, lambda i: (0, 0)), row) + tuple(dz_specs[:N_ROTATED]),
        out_shape=(jax.ShapeDtypeStruct(x.shape, F32), jax.ShapeDtypeStruct(gain.shape, F32),
                   jax.ShapeDtypeStruct(x.shape, BF16))
        + tuple(jax.ShapeDtypeStruct(dz.shape, BF16) for dz in dzs[:N_ROTATED]),
        name="in_proj_bwd_rows", compiler_params=_params(1),
    )(x, gain, dres, cos, sin, *dzs, wt)
    return outs[0], outs[1], outs[2], outs[3:]


def _in_proj_bwd_weight(dwt, h, dz, lo, hi, name):
    n_tokens, d = h.shape
    width = hi - lo
    tn = _tile(width, 512) if width >= LANES else width
    dz_tile = max(tn, LANES)

    def body(dwt_ref, h_ref, dz_ref, o_ref):
        o_ref[...] = lax.dot_general(dz_ref[:, :tn].astype(BF16), h_ref[...], TN, preferred_element_type=F32)

    return pl.pallas_call(
        body, grid=(width // tn,),
        in_specs=[ANY, _resident(h.shape), pl.BlockSpec((n_tokens, dz_tile), lambda j: (0, j))],
        out_specs=pl.BlockSpec((pl.Element(tn), pl.Element(d)), lambda j: (pl.multiple_of(lo + j * tn, 16), 0)),
        out_shape=jax.ShapeDtypeStruct(dwt.shape, F32), input_output_aliases={0: 0}, name=name,
        compiler_params=_params(1),
    )(dwt, h, dz)


def _split_small(z):
    return z[:, :GDN_HEADS], z[:, GDN_HEADS:2 * GDN_HEADS]


def _heads3(q, k, v):
    return _to_heads(q), _to_heads(k), _to_heads(v)


def _tokens6(o, lse):
    return tuple(_from_heads(o)) + tuple(_from_heads(lse))


def _blocks_of(vals, nblk):
    return [[v[:, b * (v.shape[1] // nblk):(b + 1) * (v.shape[1] // nblk)] for v in vals] for b in range(nblk)]


def _rowwise_matmul_fwd(fn, name, rows, params, wt, nblk, res=None):
    n_rows = rows[0].shape[0]
    tm = FFN_ROW_TILE
    k, n = wt.shape
    nr, npar = len(rows), len(params)

    def body(*refs):
        row_vals = [r[...] for r in refs[:nr]]
        par_vals = [r[...] for r in refs[nr:nr + npar]]
        wt_ref = refs[nr + npar]
        o_ref, y_ref = refs[-2:]
        y = jnp.concatenate([fn(*blk, *par_vals)[0] for blk in _blocks_of(row_vals, nblk)], axis=1).astype(BF16)
        y_ref[...] = y
        acc = lax.dot_general(y, wt_ref[...], NN, preferred_element_type=F32)
        o_ref[...] = acc if res is None else refs[nr + npar + 1][...] + acc

    row_specs = [pl.BlockSpec((tm, a.shape[1]), lambda i: (i, 0)) for a in rows]
    ins = list(rows) + list(params) + [wt] + ([] if res is None else [res])
    specs = row_specs + [_resident(p.shape) for p in params] + [_resident(wt.shape)]
    if res is not None:
        specs.append(pl.BlockSpec((tm, n), lambda i: (i, 0)))
    return pl.pallas_call(
        body, grid=(n_rows // tm,), in_specs=specs,
        out_specs=(pl.BlockSpec((tm, n), lambda i: (i, 0)), pl.BlockSpec((tm, k), lambda i: (i, 0))),
        out_shape=(jax.ShapeDtypeStruct((n_rows, n), F32), jax.ShapeDtypeStruct((n_rows, k), BF16)),
        name=name, compiler_params=_params(1),
    )(*ins)


def _rowwise_matmul_bwd(fn, name, rows, params, wt, dout, nblk):
    n_rows = rows[0].shape[0]
    tm = FFN_ROW_TILE
    nr, npar = len(rows), len(params)

    def body(*refs):
        row_vals = [r[...] for r in refs[:nr]]
        par_vals = [r[...] for r in refs[nr:nr + npar]]
        wt_ref, dout_ref = refs[nr + npar], refs[nr + npar + 1]
        outs = refs[nr + npar + 2:]
        dy = lax.dot_general(dout_ref[...].astype(BF16), wt_ref[...], NT, preferred_element_type=F32)
        grads = [jax.vjp(fn, *blk, *par_vals)[1]((dy_blk,))
                 for blk, (dy_blk,) in zip(_blocks_of(row_vals, nblk), _blocks_of([dy], nblk))]
        for j in range(nr):
            outs[j][...] = jnp.concatenate([g[j] for g in grads], axis=1)
        for j in range(npar):
            ref = outs[nr + j]

            @pl.when(pl.program_id(0) == 0)
            def _(ref=ref):
                ref[...] = jnp.zeros_like(ref)

            for g in grads:
                ref[...] += g[nr + j]

    row_specs = [pl.BlockSpec((tm, a.shape[1]), lambda i: (i, 0)) for a in rows]
    par_specs = [_resident(p.shape) for p in params]
    return pl.pallas_call(
        body, grid=(n_rows // tm,),
        in_specs=row_specs + par_specs + [_resident(wt.shape), pl.BlockSpec((tm, dout.shape[1]), lambda i: (i, 0))],
        out_specs=tuple(row_specs + [pl.BlockSpec(p.shape, lambda i: (0, 0)) for p in params]),
        out_shape=tuple([jax.ShapeDtypeStruct(a.shape, F32) for a in rows]
                        + [jax.ShapeDtypeStruct(p.shape, F32) for p in params]),
        name=name, compiler_params=_params(1),
    )(*rows, *params, wt, dout)


def mixer_forward(x1, w, small):
    n_tokens = x1.shape[0]
    cos, sin = _rope_tables(n_tokens)
    proj = dict(zip(IN_NAMES, _in_proj_fwd(x1, small["mix_norm"], w["w_in_t"], cos, sin)))
    (qh, kh, vh), heads_vjp = jax.vjp(_heads3, proj["wq_a"], proj["wk_a"], proj["wv_a"])
    o, lse = _attn_fwd(qh, kh, vh)
    per_group, tokens_vjp = jax.vjp(_tokens6, o, lse)
    pa, ya = _rowwise_matmul_fwd(_combine_fn, "branch_a", per_group, (), w["w_branch_a"], 1)
    qkv = _conv_fwd(proj["w_qkvb"], small["gdn_conv_w"])
    raw, small_vjp = jax.vjp(_split_small, proj["w_small"])
    gdn_params = (small["gdn_a_log"], small["gdn_dt_bias"])
    beta, gcum = _rowwise_fwd(_beta_decay_fn, "beta_decay", raw, (), gdn_params, 512, 1)
    ob, *states = _gdn_fwd(qkv, beta, gcum)
    gate_in = (ob, proj["w_ggate"])
    pb, yb = _rowwise_matmul_fwd(_outnorm_gate_fn, "branch_b", gate_in, (small["gdn_out_norm"],), w["w_branch_b"],
                                 GDN_HEADS)
    merge_in = (proj["w_gatea"], proj["w_gateb"], pa, pb)
    x2, merged = _rowwise_matmul_fwd(_merge_fn, "out", merge_in, (), w["w_out"], 1, res=x1)
    saved = dict(x1=x1, proj=proj, cos=cos, sin=sin, heads_vjp=heads_vjp, heads=(qh, kh, vh), tokens_vjp=tokens_vjp,
                 per_group=per_group, ya=ya, qkv=qkv, raw=raw, small_vjp=small_vjp, beta=beta, gcum=gcum, states=states,
                 gate_in=gate_in, yb=yb, merge_in=merge_in, merged=merged)
    return x2, saved


def mixer_backward(dx2, s, w, small):
    proj = s["proj"]
    grads = dict(w_out=_matmul(s["merged"], dx2, name="out_dw", ta=True))
    dgate_a, dgate_b, dpa, dpb = _rowwise_matmul_bwd(_merge_fn, "out_bwd", s["merge_in"], (), w["w_out"], dx2, 1)
    grads["w_branch_b"] = _matmul(s["yb"], dpb, name="branch_b_dw", ta=True)
    grads["w_branch_a"] = _matmul(s["ya"], dpa, name="branch_a_dw", ta=True)
    dob, dggate, grads["gdn_out_norm"] = _rowwise_matmul_bwd(
        _outnorm_gate_fn, "branch_b_bwd", s["gate_in"], (small["gdn_out_norm"],), w["w_branch_b"], dpb, GDN_HEADS)
    dqkv, dbeta, dgcum = _gdn_bwd(s["qkv"], s["beta"], s["gcum"], *s["states"], dob)
    gdn_params = (small["gdn_a_log"], small["gdn_dt_bias"])
    dbeta_raw, ddecay_raw, grads["gdn_a_log"], grads["gdn_dt_bias"] = _rowwise_bwd(
        _beta_decay_fn, "beta_decay_bwd", s["raw"], (), gdn_params, (dbeta, dgcum), 512, 1)
    dsmall = s["small_vjp"]((dbeta_raw, ddecay_raw))[0]
    dqkvb, grads["gdn_conv_w"] = _conv_bwd(proj["w_qkvb"], small["gdn_conv_w"], dqkv)
    dper_group = _rowwise_matmul_bwd(_combine_fn, "branch_a_bwd", s["per_group"], (), w["w_branch_a"], dpa, 1)
    do, dlse = s["tokens_vjp"](tuple(dper_group))
    dqh, dkh, dvh = _attn_bwd(*s["heads"], do, dlse)
    dq_rot, dk_rot, dv = s["heads_vjp"]((dqh, dkh, dvh))
    dzs = (dq_rot, dk_rot, dv, dqkvb, dsmall, dggate, dgate_a, dgate_b)
    dx1, grads["mix_norm"], h, unrotated = _in_proj_bwd_rows(
        s["x1"], small["mix_norm"], dx2, dzs, w["w_in_t"], s["cos"], s["sin"])
    dzs = tuple(unrotated) + dzs[N_ROTATED:]
    dwt = lax.empty(w["w_in_t"].shape, F32)
    for (name, lo, hi), dz in zip(IN_PIECES, dzs):
        dwt = _in_proj_bwd_weight(dwt, h, dz, lo, hi, "in_proj_dw_" + name)
    grads["w_in_t"] = dwt
    return dx1, grads


def ffn_forward(x, gain, w, tag):
    out, g, u = _ffn_fwd(x, gain, w[tag + "_w_gate"], w[tag + "_w_up"], w[tag + "_w_down"], tag + "_fwd")
    return out, (x, g, u)


def ffn_backward(dy, saved, gain, w, tag, with_payload=False):
    x, g, u = saved
    weights = (w[tag + "_w_gate"], w[tag + "_w_up"], w[tag + "_w_down"])
    dx, dgain, h, dyh, a, dg, du = _ffn_bwd_rows(x, gain, dy, g, u, *weights, tag + "_bwd_rows")
    return dx, dgain, _ffn_bwd_weights(h, dyh, a, dg, du, tag + "_bwd_weights", with_payload)


def loss_head(x3, target, gain):
    row_loss = _rowwise_fwd(_loss_fn, "loss", (x3,), (target,), (gain,), 256, 1)[0]
    dx3, dgain = _rowwise_bwd(_loss_fn, "loss_bwd", (x3,), (target,), (gain,), (jnp.ones_like(row_loss),), 256, 1)
    return jnp.sum(row_loss), dx3, dgain


BIG_WEIGHTS = ("ffn1_w_gate", "ffn1_w_up", "ffn1_w_down", "w_in", "w_branch_a", "w_branch_b", "w_out",
               "ffn2_w_gate", "ffn2_w_up", "ffn2_w_down")
TRANSPOSED = ("ffn1_w_gate", "ffn1_w_up", "w_in", "ffn2_w_gate", "ffn2_w_up")
CONV_SHARD = (GDN_CONV, 3 * GDN_WIDTH // N_DEV)
SMALL_ROWS = 24
ANY = pl.BlockSpec(memory_space=pl.ANY)


TOKEN = jax.ShapeDtypeStruct((8, LANES), F32)


def _after(value, token):
    return value + token[0, 0].astype(value.dtype)


def _position():
    return lax.axis_index("x"), lax.axis_index("y"), lax.axis_index("c")


def all_gather_shards(shards, name):
    n = len(shards)

    def body(*refs):
        x_refs, out_refs = refs[:n], refs[n:2 * n]
        send_sems, recv_sems, local_sems = refs[2 * n + 1:]
        x, y, c = _position()
        me, sibling = (x, y, c), (x, y, 1 - c)
        chips = [(1 - x, y), (x, 1 - y), (1 - x, 1 - y)]

        def slab(a, px, py, pc):
            return out_refs[a].at[4 * px + 2 * py + pc]

        def copy(a, k, block, to, src=None):
            return pltpu.make_async_remote_copy(
                src_ref=slab(a, *block) if src is None else src, dst_ref=slab(a, *block),
                send_sem=send_sems.at[7 * a + k], recv_sem=recv_sems.at[7 * a + k], device_id=to, device_id_type=MESH)

        mine = [pltpu.make_async_copy(x_refs[a], slab(a, *me), local_sems.at[a]) for a in range(n)]
        for cp in mine:
            cp.start()
        first = []
        for j, chip in enumerate(chips):
            first += [copy(a, 1 + j, me, (*chip, c), src=x_refs[a]) for a in range(n)]
        first += [copy(a, 0, me, sibling, src=x_refs[a]) for a in range(n)]
        for cp in first:
            cp.start()
        passed = []
        for j, chip in enumerate(chips):
            for a in range(n):
                copy(a, 1 + j, (*chip, c), me).wait_recv()
                cp = copy(a, 4 + j, (*chip, c), sibling)
                cp.start()
                passed.append(cp)
        for a in range(n):
            copy(a, 0, sibling, me).wait_recv()
        for j, chip in enumerate(chips):
            for a in range(n):
                copy(a, 4 + j, (*chip, 1 - c), me).wait_recv()
        for cp in first + passed:
            cp.wait_send()
        for cp in mine:
            cp.wait()
        refs[2 * n][...] = jnp.zeros_like(refs[2 * n])

    outs = pl.pallas_call(
        body, out_shape=tuple(jax.ShapeDtypeStruct((N_DEV,) + s.shape, s.dtype) for s in shards) + (TOKEN,),
        in_specs=[ANY] * n, out_specs=(ANY,) * n + (pl.BlockSpec(memory_space=pltpu.VMEM),),
        scratch_shapes=[pltpu.SemaphoreType.DMA((7 * n,)), pltpu.SemaphoreType.DMA((7 * n,)),
                        pltpu.SemaphoreType.DMA((n,))],
        name=name,
    )(*shards)
    return outs[:n], outs[n]


def exchange_with_sibling(grads):
    n = len(grads)

    def body(*refs):
        g_refs, recv_refs = refs[:n], refs[n:2 * n]
        send_sems, recv_sems = refs[2 * n:]
        x, y, c = _position()
        copies = [pltpu.make_async_remote_copy(
            src_ref=g_refs[a].at[2 * k + 1 - c], dst_ref=recv_refs[a].at[k], send_sem=send_sems.at[4 * a + k],
            recv_sem=recv_sems.at[4 * a + k], device_id=(x, y, 1 - c), device_id_type=MESH)
            for k in range(4) for a in range(n)]
        for cp in copies:
            cp.start()
        for cp in copies:
            cp.wait()

    return pl.pallas_call(
        body, out_shape=tuple(jax.ShapeDtypeStruct((4,) + g.shape[1:], g.dtype) for g in grads),
        in_specs=[ANY] * n, out_specs=(ANY,) * n,
        scratch_shapes=[pltpu.SemaphoreType.DMA((4 * n,)), pltpu.SemaphoreType.DMA((4 * n,))], name="rs_sibling",
    )(*grads)


ELEMENTWISE_TILE_BYTES = 1536 * 1024


def _tile2(rows, cols):
    if rows % 256 == 0:
        return 256, cols
    if rows * cols * 4 > ELEMENTWISE_TILE_BYTES and cols % 256 == 0:
        return rows, 256
    return rows, cols


def add_sibling(grads, received, core, name):
    _, rows, width = grads.shape
    tr, tc = _tile2(rows, width)

    def body(c_ref, g_ref, r_ref, o_ref):
        o_ref[...] = (g_ref[...] + r_ref[...]).astype(BF16)

    blk = (1, tr, tc)
    return pl.pallas_call(
        body,
        grid_spec=pltpu.PrefetchScalarGridSpec(
            num_scalar_prefetch=1, grid=(4, rows // tr, width // tc),
            in_specs=[pl.BlockSpec(blk, lambda k, i, j, c_ref: (2 * k + c_ref[0], i, j)),
                      pl.BlockSpec(blk, lambda k, i, j, c_ref: (k, i, j))],
            out_specs=pl.BlockSpec(blk, lambda k, i, j, c_ref: (k, i, j))),
        out_shape=jax.ShapeDtypeStruct((4, rows, width), BF16), name=name, compiler_params=_params(3),
    )(core, grads, received)


HBM = pl.BlockSpec(memory_space=pltpu.HBM)
SEM = pl.BlockSpec(memory_space=pltpu.SEMAPHORE)
DATAFLOW_EFFECT = pltpu.SideEffectType.DATAFLOW_SIDE_EFFECTING
N_PEERS = N_DEV - 1


def _peer(mask):
    x, y, c = _position()
    px = 1 - x if mask & 4 else x
    py = 1 - y if mask & 2 else y
    pc = 1 - c if mask & 1 else c
    return (px, py, pc), 4 * px + 2 * py + pc


ALL_PEERS = tuple(range(1, N_DEV))
OTHER_CHIPS = (4, 2, 6)


SIBLING = 1
GATHER_MODES = ("gather", "near")


def _exchange_peers(mode):
    return {"chips": OTHER_CHIPS, "near": (SIBLING,) + OTHER_CHIPS}.get(mode, ALL_PEERS)


def _direct_copies(src_refs, land_refs, send_sems, recv_sems, mode):
    x, y, c = _position()
    me = 4 * x + 2 * y + c
    masks = _exchange_peers(mode)
    copies = []
    for a, (src, land) in enumerate(zip(src_refs, land_refs)):
        for slot, mask in enumerate(masks):
            peer, peer_index = _peer(mask)
            k = len(masks) * a + slot
            if mode in GATHER_MODES:
                source, dest = src, land.at[me]
            elif mode == "scatter":
                source, dest = src.at[peer_index], land.at[slot]
            else:
                source, dest = src.at[2 * peer[0] + peer[1]], land.at[slot]
            copies.append(pltpu.make_async_remote_copy(
                src_ref=source, dst_ref=dest, send_sem=send_sems.at[k], recv_sem=recv_sems.at[k], device_id=peer,
                device_id_type=MESH))
    return copies


def forward_to_sibling(slabs, name):
    n = len(slabs)

    def body(*refs):
        out_refs = refs[n:2 * n]
        send_sems, recv_sems = refs[2 * n + 1:]
        x, y, c = _position()
        copies = []
        for a in range(n):
            for slot, mask in enumerate(OTHER_CHIPS):
                _, held = _peer(mask)
                copies.append(pltpu.make_async_remote_copy(
                    src_ref=out_refs[a].at[held], dst_ref=out_refs[a].at[held], send_sem=send_sems.at[3 * a + slot],
                    recv_sem=recv_sems.at[3 * a + slot], device_id=(x, y, 1 - c), device_id_type=MESH))
        for cp in copies:
            cp.start()
        for cp in copies:
            cp.wait()
        refs[2 * n][...] = jnp.zeros_like(refs[2 * n])

    outs = pl.pallas_call(
        body, out_shape=tuple(jax.ShapeDtypeStruct(s.shape, s.dtype) for s in slabs) + (TOKEN,),
        in_specs=[ANY] * n, out_specs=(ANY,) * n + (pl.BlockSpec(memory_space=pltpu.VMEM),),
        input_output_aliases={i: i for i in range(n)},
        scratch_shapes=[pltpu.SemaphoreType.DMA((3 * n,)), pltpu.SemaphoreType.DMA((3 * n,))], name=name,
    )(*slabs)
    return outs[:n], outs[n]


def direct_exchange_start(arrays, mode, name):
    n = len(arrays)
    n_peers = len(_exchange_peers(mode))
    lands = [lax.empty((N_DEV,) + a.shape if mode in GATHER_MODES else (n_peers,) + a.shape[1:], a.dtype)
             for a in arrays]

    def body(*refs):
        src_refs, land_refs = refs[:n], refs[n:2 * n]
        send_sems, recv_sems = refs[2 * n], refs[2 * n + 1]
        token = refs[-1]
        for cp in _direct_copies(src_refs, land_refs, send_sems, recv_sems, mode):
            cp.start()
        token[...] = jnp.zeros_like(token)

    sems = pltpu.SemaphoreType.DMA((n_peers * n,))
    outs = pl.pallas_call(
        body, name=name,
        out_shape=(sems, sems) + tuple(pltpu.HBM(a.shape, a.dtype) for a in arrays)
        + tuple(pltpu.HBM(l.shape, l.dtype) for l in lands) + (TOKEN,),
        in_specs=[HBM] * (2 * n), out_specs=(SEM, SEM) + (HBM,) * (2 * n) + (pl.BlockSpec(memory_space=pltpu.VMEM),),
        input_output_aliases={i: 2 + i for i in range(2 * n)},
        compiler_params=pltpu.CompilerParams(has_side_effects=DATAFLOW_EFFECT),
    )(*[pltpu.with_memory_space_constraint(a, pltpu.HBM) for a in list(arrays) + lands])
    return outs[0], outs[1], outs[2:2 + n], outs[2 + n:2 + 2 * n], outs[-1]


def direct_exchange_wait(send_sems, recv_sems, arrays, lands, after, mode, name):
    n = len(arrays)

    def body(*refs):
        src_refs, land_refs = refs[:n], refs[n:2 * n]
        send_sems, recv_sems = refs[2 * n], refs[2 * n + 1]
        for cp in _direct_copies(src_refs, land_refs, send_sems, recv_sems, mode):
            cp.wait_send()
            cp.wait_recv()
        refs[-1][...] = jnp.zeros_like(refs[-1])

    outs = pl.pallas_call(
        body, name=name,
        out_shape=tuple(pltpu.HBM(a.shape, a.dtype) for a in arrays) + tuple(pltpu.HBM(l.shape, l.dtype) for l in lands)
        + (TOKEN,),
        in_specs=[HBM] * (2 * n) + [SEM, SEM, pl.BlockSpec(memory_space=pl.ANY)],
        out_specs=(HBM,) * (2 * n) + (pl.BlockSpec(memory_space=pltpu.VMEM),),
        input_output_aliases={i: i for i in range(2 * n)},
        compiler_params=pltpu.CompilerParams(has_side_effects=DATAFLOW_EFFECT),
    )(*arrays, *lands, send_sems, recv_sems, after)
    return outs[n:]


def adamw_direct(w, m, v, own, received, name):
    row_per_tile = w.shape[0] != 1
    rows, cols = (w.shape[0], w.shape[2]) if row_per_tile else w.shape[-2:]
    tr, tc = _tile2(rows, cols)

    def body(w_ref, m_ref, v_ref, own_ref, r_ref, g_ref, d_ref, nm_ref, nv_ref):
        gv = own_ref[0]
        for j in range(N_PEERS):
            gv = gv + r_ref[j].astype(F32)
        nm = ADAM_B1 * m_ref[...] + (1.0 - ADAM_B1) * gv
        nv = ADAM_B2 * v_ref[...] + (1.0 - ADAM_B2) * (gv * gv)
        m_hat = nm / (1.0 - ADAM_B1 ** ADAM_STEP)
        v_hat = nv / (1.0 - ADAM_B2 ** ADAM_STEP)
        g_ref[...] = gv
        d_ref[...] = -ADAM_LR * (m_hat / (jnp.sqrt(v_hat) + ADAM_EPS) + ADAM_WD * w_ref[...])
        nm_ref[...] = nm
        nv_ref[...] = nv

    if row_per_tile:
        one = pl.BlockSpec((tr, None, tc), lambda i, j: (i, 0, j))
    else:
        one = pl.BlockSpec((None, tr, tc), lambda i, j: (0, i, j))
    out = jax.ShapeDtypeStruct(w.shape, F32)
    return pl.pallas_call(
        body, grid=(rows // tr, cols // tc),
        in_specs=[one, one, one, pl.BlockSpec((1, tr, tc), lambda i, j: (0, i, j)),
                  pl.BlockSpec((N_PEERS, tr, tc), lambda i, j: (0, i, j))],
        out_specs=(one,) * 4, out_shape=(out,) * 4, name=name, compiler_params=_params(2),
    )(w, m, v, own, received)


def all_reduce_small(vals):
    rows, width = vals.shape

    def body(x_ref, out_ref, all_ref, send_sems, recv_sems):
        x, y, c = _position()
        me, sibling = (x, y, c), (x, y, 1 - c)
        chips = [(1 - x, y), (x, 1 - y), (1 - x, 1 - y)]

        def slab(px, py, pc):
            return all_ref.at[4 * px + 2 * py + pc]

        def copy(k, block, to, src=None):
            return pltpu.make_async_remote_copy(
                src_ref=slab(*block) if src is None else src, dst_ref=slab(*block),
                send_sem=send_sems.at[k], recv_sem=recv_sems.at[k], device_id=to, device_id_type=MESH)

        first = [copy(0, me, sibling, src=x_ref)]
        first += [copy(1 + j, me, (*chip, c), src=x_ref) for j, chip in enumerate(chips)]
        for cp in first:
            cp.start()
        all_ref[4 * x + 2 * y + c] = x_ref[...]
        passed = [copy(4 + j, (*chip, c), sibling) for j, chip in enumerate(chips)]
        for j, chip in enumerate(chips):
            copy(1 + j, (*chip, c), me).wait_recv()
            passed[j].start()
        copy(0, sibling, me).wait_recv()
        for j, chip in enumerate(chips):
            copy(4 + j, (*chip, 1 - c), me).wait_recv()
        for cp in first + passed:
            cp.wait_send()
        total = all_ref[0]
        for d in range(1, N_DEV):
            total = total + all_ref[d]
        out_ref[...] = total

    vmem = pl.BlockSpec(memory_space=pltpu.VMEM)
    return pl.pallas_call(
        body, out_shape=(jax.ShapeDtypeStruct(vals.shape, F32), jax.ShapeDtypeStruct((N_DEV, rows, width), F32)),
        in_specs=[vmem], out_specs=(vmem, vmem),
        scratch_shapes=[pltpu.SemaphoreType.DMA((7,)), pltpu.SemaphoreType.DMA((7,))], name="small_allreduce",
    )(vals)[0]


def adamw(w, g, m, v, name):
    shape = w.shape
    w2, g2, m2, v2 = [a.reshape((-1, shape[-1])) for a in (w, g, m, v)]
    rows, cols = w2.shape
    tr = 256 if rows % 256 == 0 else rows

    def body(w_ref, g_ref, m_ref, v_ref, d_ref, nm_ref, nv_ref):
        gv = g_ref[...]
        nm = ADAM_B1 * m_ref[...] + (1.0 - ADAM_B1) * gv
        nv = ADAM_B2 * v_ref[...] + (1.0 - ADAM_B2) * (gv * gv)
        m_hat = nm / (1.0 - ADAM_B1 ** ADAM_STEP)
        v_hat = nv / (1.0 - ADAM_B2 ** ADAM_STEP)
        d_ref[...] = -ADAM_LR * (m_hat / (jnp.sqrt(v_hat) + ADAM_EPS) + ADAM_WD * w_ref[...])
        nm_ref[...] = nm
        nv_ref[...] = nv

    blk = pl.BlockSpec((tr, cols), lambda i: (i, 0))
    out = jax.ShapeDtypeStruct((rows, cols), F32)
    outs = pl.pallas_call(
        body, grid=(rows // tr,), in_specs=[blk] * 4, out_specs=(blk,) * 3, out_shape=(out,) * 3,
        name=name, compiler_params=_params(1),
    )(w2, g2, m2, v2)
    return tuple(o.reshape(shape) for o in outs)


def adamw_summed(w, m, v, grads, from_sibling, received, me, name):
    rows, cols = w.shape[-2:]
    tr, tc = _tile2(rows, cols)

    def body(me_ref, w_ref, m_ref, v_ref, own_ref, sib_ref, r_ref, g_ref, d_ref, nm_ref, nv_ref):
        gv = own_ref[0] + sib_ref[0]
        for j in range(3):
            gv = gv + r_ref[j].astype(F32)
        nm = ADAM_B1 * m_ref[0] + (1.0 - ADAM_B1) * gv
        nv = ADAM_B2 * v_ref[0] + (1.0 - ADAM_B2) * (gv * gv)
        m_hat = nm / (1.0 - ADAM_B1 ** ADAM_STEP)
        v_hat = nv / (1.0 - ADAM_B2 ** ADAM_STEP)
        g_ref[0] = gv
        d_ref[0] = -ADAM_LR * (m_hat / (jnp.sqrt(v_hat) + ADAM_EPS) + ADAM_WD * w_ref[0])
        nm_ref[0] = nm
        nv_ref[0] = nv

    one = pl.BlockSpec((1, tr, tc), lambda i, j, me_ref: (0, i, j))
    out = jax.ShapeDtypeStruct((1, rows, cols), F32)
    return pl.pallas_call(
        body,
        grid_spec=pltpu.PrefetchScalarGridSpec(
            num_scalar_prefetch=1, grid=(rows // tr, cols // tc),
            in_specs=[one, one, one, pl.BlockSpec((1, tr, tc), lambda i, j, me_ref: (me_ref[0], i, j)),
                      pl.BlockSpec((1, tr, tc), lambda i, j, me_ref: (me_ref[1], i, j)),
                      pl.BlockSpec((3, tr, tc), lambda i, j, me_ref: (0, i, j))],
            out_specs=(one,) * 4),
        out_shape=(out,) * 4, name=name, compiler_params=_params(2),
    )(me, w, m, v, grads, from_sibling, received)


SMALL_VECTORS = ("ffn1_norm", "mix_norm", "ffn2_norm", "final_norm")


def _pack_small(gs):
    row = jnp.concatenate([gs["gdn_a_log"].reshape(-1), gs["gdn_dt_bias"].reshape(-1), gs["gdn_out_norm"].reshape(-1)])
    rows = [gs[n].reshape(1, D_MODEL) for n in SMALL_VECTORS]
    rows.append(jnp.pad(row, (0, D_MODEL - row.shape[0])).reshape(1, D_MODEL))
    rows.append(gs["gdn_conv_w"].reshape(-1, D_MODEL))
    packed = jnp.concatenate(rows, axis=0)
    return jnp.pad(packed, ((0, SMALL_ROWS - packed.shape[0]), (0, 0)))


def _unpack_small(packed):
    out = {n: packed[i].reshape(1, D_MODEL) for i, n in enumerate(SMALL_VECTORS)}
    row = packed[len(SMALL_VECTORS)]
    out["gdn_a_log"] = row[:GDN_HEADS].reshape(1, GDN_HEADS)
    out["gdn_dt_bias"] = row[GDN_HEADS:2 * GDN_HEADS].reshape(1, GDN_HEADS)
    out["gdn_out_norm"] = row[2 * GDN_HEADS:2 * GDN_HEADS + GDN_HEAD_DIM].reshape(1, GDN_HEAD_DIM)
    first = len(SMALL_VECTORS) + 1
    out["gdn_conv_w"] = packed[first:first + GDN_CONV * 3].reshape(GDN_CONV, 3 * GDN_WIDTH)
    return out


WEIGHTS = ("ffn1_norm", "ffn1_w_gate", "ffn1_w_up", "ffn1_w_down", "mix_norm", "w_in", "gdn_conv_w", "gdn_a_log",
           "gdn_dt_bias", "gdn_out_norm", "w_branch_a", "w_branch_b", "w_out", "ffn2_norm", "ffn2_w_gate",
           "ffn2_w_up", "ffn2_w_down", "final_norm")


def kernel(x, ffn1_norm, ffn1_w_gate, ffn1_w_up, ffn1_w_down, mix_norm, w_in, gdn_conv_w, gdn_a_log, gdn_dt_bias, gdn_out_norm, w_branch_a, w_branch_b, w_out, ffn2_norm, ffn2_w_gate, ffn2_w_up, ffn2_w_down, final_norm, loss_target, m_ffn1_norm, m_ffn1_w_gate, m_ffn1_w_up, m_ffn1_w_down, m_mix_norm, m_w_in, m_gdn_conv_w, m_gdn_a_log, m_gdn_dt_bias, m_gdn_out_norm, m_w_branch_a, m_w_branch_b, m_w_out, m_ffn2_norm, m_ffn2_w_gate, m_ffn2_w_up, m_ffn2_w_down, m_final_norm, v_ffn1_norm, v_ffn1_w_gate, v_ffn1_w_up, v_ffn1_w_down, v_mix_norm, v_w_in, v_gdn_conv_w, v_gdn_a_log, v_gdn_dt_bias, v_gdn_out_norm, v_w_branch_a, v_w_branch_b, v_w_out, v_ffn2_norm, v_ffn2_w_gate, v_ffn2_w_up, v_ffn2_w_down, v_final_norm):
    given = dict(locals())
    px, py, pc = _position()
    big_names = list(BIG_WEIGHTS)

    def shard_view(a, n):
        if n == "w_in":
            return a.transpose(2, 0, 1)
        return a.transpose(0, 2, 1) if n in TRANSPOSED else a

    def shard_unview(a, n):
        if n == "w_in":
            return a.transpose(1, 2, 0)
        return a.transpose(0, 2, 1) if n in TRANSPOSED else a

    me = 4 * px + 2 * py + pc
    me_index = me.astype(jnp.int32).reshape(1)
    late = [n for n in big_names if n.startswith("ffn2")]
    early = [n for n in big_names if n not in late]
    shards = {n: shard_view(given[n], n).reshape(given[n].shape[-1 if n in TRANSPOSED else -2], -1).astype(BF16)
              for n in big_names}
    first = [n for n in early if n.startswith("ffn1")]
    middle = [n for n in early if n not in first]
    first_slabs, first_done = all_gather_shards([shards[n] for n in first], "gather_ffn1")
    shards["gdn_conv_w"] = gdn_conv_w[0]
    middle_all = middle + ["gdn_conv_w"]
    middle_gather = direct_exchange_start([_after(shards[n], first_done) for n in middle_all], "near",
                                          "gather_mixer_start")
    ffn1_norm = _after(ffn1_norm, middle_gather[4])
    def in_chunks(slabs):
        return slabs.reshape(-1, FFN_CHUNK, D_MODEL)

    def in_slabs(chunks):
        return chunks.reshape(N_DEV, -1, D_MODEL)

    w = {n: in_chunks(slab) for n, slab in zip(first, first_slabs)}
    x1, ffn1_saved = ffn_forward(x[0], ffn1_norm, w, "ffn1")
    near_lands = direct_exchange_wait(*middle_gather[:4], x1, "near", "gather_mixer_wait")[:-1]
    near_lands = [lax.dynamic_update_slice(land, shards[n][None], (me, 0, 0)) for n, land in zip(middle_all, near_lands)]
    middle_slabs, middle_done = forward_to_sibling(near_lands, "gather_mixer_forward")
    gathered = dict(zip(middle_all, middle_slabs))
    late_gather = direct_exchange_start([_after(shards[n], middle_done) for n in late], "gather", "gather_ffn2_start")
    w["w_in_t"] = gathered["w_in"].reshape(-1, D_MODEL)
    w["w_branch_a"] = gathered["w_branch_a"].transpose(1, 0, 2).reshape(256, D_MODEL)
    w["w_branch_b"] = gathered["w_branch_b"].reshape(D_MODEL, D_MODEL)
    w["w_out"] = gathered["w_out"].reshape(D_MODEL, D_MODEL)
    conv_full = gathered["gdn_conv_w"].transpose(1, 0, 2).reshape(GDN_CONV, 3 * GDN_WIDTH)
    small = dict(mix_norm=_after(mix_norm, late_gather[4]), gdn_a_log=gdn_a_log, gdn_dt_bias=gdn_dt_bias,
                 gdn_out_norm=gdn_out_norm, gdn_conv_w=conv_full)

    x2, mixer_saved = mixer_forward(x1, w, small)
    late_lands = direct_exchange_wait(*late_gather[:4], x2, "gather", "gather_ffn2_wait")
    for n, land in zip(late, late_lands):
        w[n] = in_chunks(lax.dynamic_update_slice(land, shards[n][None], (me, 0, 0)))
    x3, ffn2_saved = ffn_forward(x2, ffn2_norm, w, "ffn2")
    loss_local, dx3, g_final = loss_head(x3, loss_target[0], final_norm.reshape(1, D_MODEL))
    loss = lax.psum(loss_local, ("x", "y", "c"))
    dx2, g_ffn2_norm, (dw2, dw2_f32) = ffn_backward(dx3, ffn2_saved, ffn2_norm, w, "ffn2", with_payload=True)
    late_scatter = direct_exchange_start([in_slabs(g) for g in dw2], "scatter", "rs_ffn2_start")
    w_after = dict(w, w_out=_after(w["w_out"], late_scatter[4]))
    dx1, g_w = mixer_backward(dx2, mixer_saved, w_after, small)
    middle = ["w_in", "w_branch_a", "w_branch_b", "w_out"]
    g_big = dict(w_in=g_w["w_in_t"].reshape(N_DEV, -1, D_MODEL),
                 w_branch_a=g_w["w_branch_a"].reshape(256, N_DEV, 128).transpose(1, 0, 2),
                 w_branch_b=g_w["w_branch_b"].reshape(N_DEV, 128, D_MODEL),
                 w_out=g_w["w_out"].reshape(N_DEV, 128, D_MODEL))
    own = {n: lax.dynamic_index_in_dim(in_slabs(g), me, 0, keepdims=True) for n, g in zip(late, dw2_f32)}
    own.update({n: lax.dynamic_index_in_dim(g_big[n], me, 0, keepdims=True) for n in middle[1:]})
    in_rows = g_w["w_in_t"].shape[0] // N_DEV
    own["w_in"] = lax.dynamic_slice(g_w["w_in_t"], (me * in_rows, 0), (in_rows, D_MODEL))[None]
    middle_scatter = direct_exchange_start([g_big[n].astype(BF16) for n in middle], "scatter", "rs_mixer_start")
    grad_x, g_ffn1_norm, dw1 = ffn_backward(dx1, ffn1_saved, _after(ffn1_norm, middle_scatter[4]), w, "ffn1")
    g_small = dict(ffn1_norm=g_ffn1_norm, ffn2_norm=g_ffn2_norm, final_norm=g_final,
                   **{n: g_w[n] for n in ("mix_norm", "gdn_a_log", "gdn_dt_bias", "gdn_out_norm", "gdn_conv_w")})

    first = [n for n in early if n.startswith("ffn1")]
    g_list = [in_slabs(g) for g in dw1]
    core = pc.astype(jnp.int32).reshape(1)
    me_and_chip = jnp.stack([me, 2 * px + py]).astype(jnp.int32)
    from_sibling = exchange_with_sibling(g_list)
    partials = [add_sibling(g, r, core, "rs_add_" + n) for n, g, r in zip(first, g_list, from_sibling)]
    first_chips = direct_exchange_start(partials, "chips", "rs_ffn1_start")

    def state_of(n):
        return [shard_view(given[p + n], n) for p in ("", "m_", "v_")]

    results = {}
    late_received = direct_exchange_wait(*late_scatter[:4], first_chips[4], "scatter", "rs_ffn2_wait")
    middle_received = direct_exchange_wait(*middle_scatter[:4], first_chips[4], "scatter", "rs_mixer_wait")
    for n, recv in zip(late + middle, list(late_received[:-1]) + list(middle_received[:-1])):
        outs = adamw_direct(*state_of(n), own[n], recv, "adamw_" + n)
        results[n] = tuple(shard_unview(o, n) for o in outs)

    done = results["w_out"][1]
    from_chips = direct_exchange_wait(*first_chips[:4], done, "chips", "rs_ffn1_wait")
    for n, g, sib, recv in zip(first, g_list, from_sibling, from_chips):
        outs = adamw_summed(*state_of(n), g, sib, recv, me_and_chip, "adamw_" + n)
        results[n] = tuple(shard_unview(o, n) for o in outs)

    small_sum = _unpack_small(all_reduce_small(_after(_pack_small(g_small), from_chips[-1])))
    conv_cols = CONV_SHARD[1]
    small_sum["gdn_conv_w"] = lax.dynamic_slice(small_sum["gdn_conv_w"], (0, me * conv_cols), (GDN_CONV, conv_cols))
    for n in WEIGHTS:
        if n not in results:
            g = small_sum[n].reshape(given[n].shape)
            results[n] = (g,) + adamw(given[n], g, given["m_" + n], given["v_" + n], "adamw_" + n)

    outs = [[results[n][i] for n in WEIGHTS] for i in range(4)]
    return (loss, grad_x[None], *outs[0], *outs[1], *outs[2], *outs[3])
```

```python
import jax
import jax.numpy as jnp
from jax import lax
from jax.experimental import pallas as pl
from jax.experimental.pallas import tpu as pltpu

F32 = jnp.float32
BF16 = jnp.bfloat16
HI = lax.Precision.HIGHEST
MESH = pl.DeviceIdType.MESH

N_DEV = 8
D_MODEL = 1024
EPS = 1e-6
ROPE_THETA = 10000.0
DSW_DILATIONS = (1, 4, 16)
DSW_HEADS_PER_GROUP = 4
DSW_HEAD_DIM = 64
DSW_BLOCK = 128
GDN_HEADS = 8
GDN_HEAD_DIM = 128
GDN_WIDTH = 1024
GDN_CONV = 4
GDN_CHUNK = 64

ADAM_LR = 0.001
ADAM_B1 = 0.9
ADAM_B2 = 0.999
ADAM_EPS = 1e-08
ADAM_WD = 0.01
ADAM_STEP = 10

VMEM_LIMIT_BYTES = 56 * 1024 * 1024
LANES = 128

NN = (((1,), (0,)), ((), ()))
NT = (((1,), (1,)), ((), ()))
TN = (((0,), (0,)), ((), ()))


def _params(n_grid):
    return pltpu.CompilerParams(dimension_semantics=("arbitrary",) * n_grid, vmem_limit_bytes=VMEM_LIMIT_BYTES)


def _tile(n, pref):
    best = None
    t = LANES
    while t <= min(n, pref):
        if n % t == 0:
            best = t
        t += LANES
    return n if best is None else best


def _weight_grad(a, g, name):
    n_tokens, m = a.shape
    n = g.shape[1]
    tm, tn = _tile(m, 512), _tile(n, 512)

    def body(a_ref, g_ref, o_ref):
        o_ref[...] = lax.dot_general(a_ref[...].astype(BF16), g_ref[...].astype(BF16), TN, preferred_element_type=F32)

    return pl.pallas_call(
        body, grid=(m // tm, n // tn),
        in_specs=[pl.BlockSpec((n_tokens, tm), lambda i, j: (0, i)), pl.BlockSpec((n_tokens, tn), lambda i, j: (0, j))],
        out_specs=pl.BlockSpec((tm, tn), lambda i, j: (i, j)),
        out_shape=jax.ShapeDtypeStruct((m, n), F32), name=name, compiler_params=_params(2),
    )(a, g)


def _rw_specs(arrs, tm, nblk):
    return [pl.BlockSpec((tm, a.shape[1] // nblk), lambda i, j: (i, j)) for a in arrs]


def _rowwise_fwd(fn, name, rows, consts, params, tm, nblk):
    n_rows = rows[0].shape[0]
    tm = min(tm, n_rows)
    ins = list(rows) + list(consts)
    avals = [jax.ShapeDtypeStruct((tm, a.shape[1] // nblk), a.dtype) for a in ins]
    avals += [jax.ShapeDtypeStruct(p.shape, p.dtype) for p in params]
    out_avals = jax.eval_shape(fn, *avals)
    n_in = len(ins) + len(params)

    def body(*refs):
        outs = fn(*[r[...] for r in refs[:n_in]])
        for r, o in zip(refs[n_in:], outs):
            r[...] = o.astype(r.dtype)

    return pl.pallas_call(
        body, grid=(n_rows // tm, nblk),
        in_specs=_rw_specs(ins, tm, nblk) + [pl.BlockSpec(p.shape, lambda i, j: (0, 0)) for p in params],
        out_specs=tuple(pl.BlockSpec((tm, o.shape[1]), lambda i, j: (i, j)) for o in out_avals),
        out_shape=tuple(jax.ShapeDtypeStruct((n_rows, o.shape[1] * nblk), o.dtype) for o in out_avals),
        name=name, compiler_params=_params(2),
    )(*ins, *params)


def _rowwise_bwd(fn, name, rows, consts, params, cts, tm, nblk):
    n_rows = rows[0].shape[0]
    tm = min(tm, n_rows)
    nr, nc, npar, nct = len(rows), len(consts), len(params), len(cts)

    def body(*refs):
        rv = [r[...] for r in refs[:nr]]
        cv = [r[...] for r in refs[nr:nr + nc]]
        pv = [r[...] for r in refs[nr + nc:nr + nc + npar]]
        ctv = [r[...] for r in refs[nr + nc + npar:nr + nc + npar + nct]]
        outs = refs[nr + nc + npar + nct:]
        _, vjp = jax.vjp(lambda *d: fn(*d[:nr], *cv, *d[nr:]), *rv, *pv)
        grads = vjp(tuple(ctv))
        for k in range(nr):
            outs[k][...] = grads[k]
        first = jnp.logical_and(pl.program_id(0) == 0, pl.program_id(1) == 0)
        for k in range(npar):
            ref = outs[nr + k]

            @pl.when(first)
            def _(ref=ref):
                ref[...] = jnp.zeros_like(ref)

            ref[...] += grads[nr + k]

    ins = list(rows) + list(consts)
    return pl.pallas_call(
        body, grid=(n_rows // tm, nblk),
        in_specs=(_rw_specs(ins, tm, nblk) + [pl.BlockSpec(p.shape, lambda i, j: (0, 0)) for p in params]
                  + _rw_specs(cts, tm, nblk)),
        out_specs=tuple(_rw_specs(rows, tm, nblk) + [pl.BlockSpec(p.shape, lambda i, j: (0, 0)) for p in params]),
        out_shape=tuple([jax.ShapeDtypeStruct(a.shape, F32) for a in rows]
                        + [jax.ShapeDtypeStruct(p.shape, F32) for p in params]),
        name=name, compiler_params=_params(2),
    )(*ins, *params, *cts)


def _merge_fn(ga, gb, pa, pb):
    return (jax.nn.sigmoid(ga) * pa + jax.nn.sigmoid(gb) * pb,)


def _outnorm_gate_fn(o, gate, gain):
    y = o * lax.rsqrt(jnp.mean(o * o, axis=-1, keepdims=True) + EPS) * gain
    return (y * (gate * jax.nn.sigmoid(gate)),)


def _beta_decay_fn(beta_raw, decay_raw, a_log, dt_bias):
    z = decay_raw + dt_bias
    softplus = jnp.maximum(z, 0.0) + jnp.log(1.0 + jnp.exp(-jnp.abs(z)))
    g = -jnp.exp(a_log) * softplus
    rows = g.shape[0]
    ii = lax.broadcasted_iota(jnp.int32, (rows, rows), 0)
    jj = lax.broadcasted_iota(jnp.int32, (rows, rows), 1)
    same_chunk_before = jnp.logical_and(jj <= ii, jj // GDN_CHUNK == ii // GDN_CHUNK).astype(F32)
    gcum = lax.dot_general(same_chunk_before, g, NN, precision=HI, preferred_element_type=F32)
    return jax.nn.sigmoid(beta_raw), gcum


def _combine_fn(o0, o1, o2, l0, l1, l2):
    m = lax.stop_gradient(jnp.maximum(jnp.maximum(l0, l1), l2))
    e0, e1, e2 = jnp.exp(l0 - m), jnp.exp(l1 - m), jnp.exp(l2 - m)
    return ((e0 * o0 + e1 * o1 + e2 * o2) / (e0 + e1 + e2),)


def _loss_fn(x, target, gain):
    y = x * lax.rsqrt(jnp.mean(x * x, axis=-1, keepdims=True) + EPS) * gain
    err = y - target
    return (0.5 * jnp.mean(err * err, axis=-1, keepdims=True),)


def _rotate(v, cos, sin):
    half = DSW_HEAD_DIM // 2
    lane = lax.broadcasted_iota(jnp.int32, cos.shape, 1)
    low = (lane % DSW_HEAD_DIM) < half
    slabs = []
    for s in range(v.shape[1] // LANES):
        x = v[:, s * LANES:(s + 1) * LANES]
        swapped = jnp.where(low, pltpu.roll(x, LANES - half, 1), pltpu.roll(x, half, 1))
        slabs.append(x * cos + swapped * sin)
    return jnp.concatenate(slabs, axis=1)


def _rope_tables(n_tokens):
    half = DSW_HEAD_DIM // 2
    inv_freq = ROPE_THETA ** (-jnp.arange(half, dtype=F32) / half)
    ang = jnp.arange(n_tokens, dtype=F32)[:, None] * inv_freq[None, :]
    cos, sin = jnp.cos(ang), jnp.sin(ang)
    return jnp.tile(jnp.concatenate([cos, cos], 1), (1, 2)), jnp.tile(jnp.concatenate([-sin, sin], 1), (1, 2))


def _attn_probs(q, kp, kc, group, n):
    blk = DSW_BLOCK
    k = _each(lambda a, b: jnp.concatenate([a, b], axis=0).astype(BF16), kp, kc)
    s = _each(lambda a, b: lax.dot_general(a.astype(BF16), b, NT, preferred_element_type=F32)
              * (DSW_HEAD_DIM ** -0.5), q, k)
    blocks_per_seq = jnp.where(group == 0, 16, jnp.where(group == 1, 4, 1))
    first = (n % blocks_per_seq) == 0
    qi = lax.broadcasted_iota(jnp.int32, (blk, 2 * blk), 0)
    kj = lax.broadcasted_iota(jnp.int32, (blk, 2 * blk), 1)
    dist = qi + blk - kj
    valid = (dist >= 0) & (dist <= blk) & jnp.logical_or(kj >= blk, jnp.logical_not(first))
    s = _each(lambda a: jnp.where(valid, a, -1e30), s)
    m = _each(lambda a: jnp.max(a, axis=-1, keepdims=True), s)
    p = _each(lambda a, b: jnp.exp(a - b), s, m)
    l = _each(lambda a: jnp.sum(a, axis=-1, keepdims=True), p)
    return _each(lambda a, b: a / b, p, l), _each(lambda a, b: a + jnp.log(b), m, l), k


GROUP_WIDTH = DSW_HEADS_PER_GROUP * DSW_HEAD_DIM


def _attn_specs(n_tokens):
    blk = DSW_BLOCK
    cur = pl.BlockSpec((1, blk, GROUP_WIDTH), lambda g, n: (g, n, 0))
    prev = pl.BlockSpec((1, blk, GROUP_WIDTH), lambda g, n: (g, jnp.maximum(n - 1, 0), 0))
    return cur, prev


def _heads_of(ref):
    x = ref[0]
    return [x[:, h * DSW_HEAD_DIM:(h + 1) * DSW_HEAD_DIM] for h in range(DSW_HEADS_PER_GROUP)]


def _group_of(heads):
    return jnp.concatenate(heads, axis=1)


def _attn_fwd(q, k, v):
    n_groups, n_tokens, _ = q.shape
    cur, prev = _attn_specs(n_tokens)

    def body(q_ref, kp_ref, kc_ref, vp_ref, vc_ref, o_ref, l_ref):
        p, lse, _ = _attn_probs(_heads_of(q_ref), _heads_of(kp_ref), _heads_of(kc_ref),
                                pl.program_id(0), pl.program_id(1))
        vv = _each(lambda a, b: jnp.concatenate([a, b], axis=0).astype(BF16), _heads_of(vp_ref), _heads_of(vc_ref))
        o = _each(lambda a, b: lax.dot_general(a.astype(BF16), b, NN, preferred_element_type=F32), p, vv)
        lse_wide = _each(lambda a: jnp.broadcast_to(a, (DSW_BLOCK, DSW_HEAD_DIM)), lse)
        o_ref[0] = _group_of(o)
        l_ref[0] = _group_of(lse_wide)

    return pl.pallas_call(
        body, grid=(n_groups, n_tokens // DSW_BLOCK), in_specs=[cur, prev, cur, prev, cur],
        out_specs=(cur, cur), out_shape=(jax.ShapeDtypeStruct(q.shape, F32), jax.ShapeDtypeStruct(q.shape, F32)),
        name="attn_fwd", compiler_params=_params(2),
    )(q, k, k, v, v)


def _attn_bwd(q, k, v, do, dlse):
    n_groups, n_tokens, _ = q.shape
    nblk = n_tokens // DSW_BLOCK
    cur, prev = _attn_specs(n_tokens)
    part = pl.BlockSpec((1, 1, 2 * DSW_BLOCK, GROUP_WIDTH), lambda g, n: (g, n, 0, 0))
    scale = DSW_HEAD_DIM ** -0.5

    def body(q_ref, kp_ref, kc_ref, vp_ref, vc_ref, do_ref, dl_ref, dq_ref, dk_ref, dv_ref):
        qs = _heads_of(q_ref)
        p, _, kb = _attn_probs(qs, _heads_of(kp_ref), _heads_of(kc_ref), pl.program_id(0), pl.program_id(1))
        qb = _each(lambda a: a.astype(BF16), qs)
        vv = _each(lambda a, b: jnp.concatenate([a, b], axis=0).astype(BF16), _heads_of(vp_ref), _heads_of(vc_ref))
        dob = _each(lambda a: a.astype(BF16), _heads_of(do_ref))
        dp = _each(lambda a, b: lax.dot_general(a, b, NT, preferred_element_type=F32), dob, vv)
        dv = _each(lambda a, b: lax.dot_general(a.astype(BF16), b, TN, preferred_element_type=F32), p, dob)
        dl = _each(lambda a: jnp.sum(a, axis=-1, keepdims=True), _heads_of(dl_ref))
        ds = _each(lambda a, b, c: (a * (b - jnp.sum(b * a, axis=-1, keepdims=True) + c) * scale).astype(BF16),
                   p, dp, dl)
        dq = _each(lambda a, b: lax.dot_general(a, b, NN, preferred_element_type=F32), ds, kb)
        dk = _each(lambda a, b: lax.dot_general(a, b, TN, preferred_element_type=F32), ds, qb)
        dq_ref[0] = _group_of(dq)
        dk_ref[0, 0] = _group_of(dk)
        dv_ref[0, 0] = _group_of(dv)

    partial_shape = jax.ShapeDtypeStruct((n_groups, nblk, 2 * DSW_BLOCK, GROUP_WIDTH), F32)
    dq, dkp, dvp = pl.pallas_call(
        body, grid=(n_groups, nblk), in_specs=[cur, prev, cur, prev, cur, cur, cur],
        out_specs=(cur, part, part), out_shape=(jax.ShapeDtypeStruct(q.shape, F32), partial_shape, partial_shape),
        name="attn_bwd", compiler_params=_params(2),
    )(q, k, k, v, v, do, dlse)

    def fold(partial):
        own = partial[:, :, DSW_BLOCK:]
        from_next = jnp.pad(partial[:, 1:, :DSW_BLOCK], ((0, 0), (0, 1), (0, 0), (0, 0)))
        return (own + from_next).reshape(n_groups, n_tokens, GROUP_WIDTH)

    return dq, fold(dkp), fold(dvp)


def _to_heads(a):
    n_tokens = a.shape[0]
    outs = []
    for gi, d in enumerate(DSW_DILATIONS):
        blk = a[:, gi * GROUP_WIDTH:(gi + 1) * GROUP_WIDTH].reshape(n_tokens // d, d, GROUP_WIDTH)
        outs.append(blk.transpose(1, 0, 2).reshape(1, n_tokens, GROUP_WIDTH))
    return jnp.concatenate(outs, 0)


def _from_heads(a):
    n_tokens = a.shape[1]
    return [a[gi].reshape(d, n_tokens // d, GROUP_WIDTH).transpose(1, 0, 2).reshape(n_tokens, GROUP_WIDTH)
            for gi, d in enumerate(DSW_DILATIONS)]


CONV_TILE = 512


def _shift_down(x, k, rows):
    return x if k == 0 else jnp.where(rows >= k, pltpu.roll(x, k, 0), 0.0)


def _shift_up(x, k, rows):
    n = x.shape[0]
    return x if k == 0 else jnp.where(rows < n - k, pltpu.roll(x, n - k, 0), 0.0)


def _conv_pre(x, w):
    rows = lax.broadcasted_iota(jnp.int32, x.shape, 0)
    acc = x * w[GDN_CONV - 1:GDN_CONV]
    for k in range(1, GDN_CONV):
        acc = acc + _shift_down(x, k, rows) * w[GDN_CONV - 1 - k:GDN_CONV - k]
    return acc, rows


def _conv_fwd(x, w):
    n_tokens, width = x.shape
    big = pl.BlockSpec((n_tokens, CONV_TILE), lambda j: (0, j))
    wsp = pl.BlockSpec((GDN_CONV, CONV_TILE), lambda j: (0, j))

    def body(x_ref, w_ref, o_ref):
        acc, _ = _conv_pre(x_ref[...], w_ref[...])
        o_ref[...] = acc * jax.nn.sigmoid(acc)

    return pl.pallas_call(
        body, grid=(width // CONV_TILE,), in_specs=[big, wsp], out_specs=big,
        out_shape=jax.ShapeDtypeStruct(x.shape, F32), name="conv_fwd", compiler_params=_params(1),
    )(x, w)


def _conv_bwd(x, w, dy):
    n_tokens, width = x.shape
    big = pl.BlockSpec((n_tokens, CONV_TILE), lambda j: (0, j))
    wsp = pl.BlockSpec((GDN_CONV, CONV_TILE), lambda j: (0, j))

    def body(x_ref, w_ref, dy_ref, dx_ref, dw_ref):
        xv, wv = x_ref[...], w_ref[...]
        acc, rows = _conv_pre(xv, wv)
        sg = jax.nn.sigmoid(acc)
        dacc = dy_ref[...] * (sg + acc * sg * (1.0 - sg))
        dx = dacc * wv[GDN_CONV - 1:GDN_CONV]
        for k in range(1, GDN_CONV):
            dx = dx + _shift_up(dacc, k, rows) * wv[GDN_CONV - 1 - k:GDN_CONV - k]
        dx_ref[...] = dx
        for k in range(GDN_CONV):
            dw_ref[GDN_CONV - 1 - k:GDN_CONV - k, :] = jnp.sum(dacc * _shift_down(xv, k, rows), axis=0, keepdims=True)

    return pl.pallas_call(
        body, grid=(width // CONV_TILE,), in_specs=[big, wsp, big], out_specs=(big, wsp),
        out_shape=(jax.ShapeDtypeStruct(x.shape, F32), jax.ShapeDtypeStruct(w.shape, F32)),
        name="conv_bwd", compiler_params=_params(1),
    )(x, w, dy)


def _dot3(a, b, dn=NN):
    return lax.dot_general(a, b, dn, precision=lax.Precision.HIGH, preferred_element_type=F32)


def _bf16_dot(a, b, dn):
    return lax.dot_general(a.astype(BF16), b.astype(BF16), dn, preferred_element_type=F32)


_DOT_GRADS = {NN: (("g", "b", NT), ("a", "g", TN)), NT: (("g", "b", NN), ("g", "a", TN)),
              TN: (("b", "g", NT), ("a", "g", NN))}


def _make_bdot(dn):
    @jax.custom_vjp
    def op(a, b):
        return _bf16_dot(a, b, dn)

    def fwd(a, b):
        return op(a, b), (a, b)

    def bwd(saved, g):
        vals = dict(a=saved[0], b=saved[1], g=g)
        return tuple(_bf16_dot(vals[x], vals[y], form) for x, y, form in _DOT_GRADS[dn])

    op.defvjp(fwd, bwd)
    return op


_BDOTS = {dn: _make_bdot(dn) for dn in (NN, NT, TN)}


def _bdot(a, b, dn=NN):
    return _BDOTS[dn](a, b)


def _each(fn, *lists):
    return [fn(*items) for items in zip(*lists)]


@jax.custom_vjp
def _known_inverse(m, inverse):
    return inverse


def _known_inverse_fwd(m, inverse):
    return inverse, inverse


def _known_inverse_bwd(inverse, d_inverse):
    return -_dot3(_dot3(inverse, d_inverse, TN), inverse, NT), jnp.zeros_like(inverse)


_known_inverse.defvjp(_known_inverse_fwd, _known_inverse_bwd)


def _gdn_chunks(q, k, v, b, gcum, state, inverse=None):
    c = GDN_CHUNK
    ii = lax.broadcasted_iota(jnp.int32, (c, c), 0)
    jj = lax.broadcasted_iota(jnp.int32, (c, c), 1)
    qn = _each(lambda x: x * lax.rsqrt(jnp.sum(x * x, axis=-1, keepdims=True) + EPS) * (GDN_HEAD_DIM ** -0.5), q)
    kn = _each(lambda x: x * lax.rsqrt(jnp.sum(x * x, axis=-1, keepdims=True) + EPS), k)
    gcum_i = _each(lambda x: jnp.broadcast_to(x, (c, c)), gcum)
    gcum_j = _each(jnp.transpose, gcum_i)
    decay = _each(lambda x, y: jnp.exp(jnp.where(jj <= ii, x - y, -1e30)), gcum_i, gcum_j)
    g_last = _each(lambda x: x[c - 1:c, :], gcum)
    e_gcum = _each(jnp.exp, gcum)
    kbeta = _each(lambda x, y: x * y, kn, b)
    vbeta = _each(lambda x, y: x * y, v, b)
    m = _each(lambda x, y, d: jnp.where(jj < ii, _bdot(x, y, NT) * d, 0.0), kbeta, kn, decay)
    if inverse is not None:
        inv = _each(_known_inverse, m, inverse)
    else:
        eye = (ii == jj).astype(F32)
        inv = _each(lambda x: eye - x, m)
        power = _each(lambda x: _dot3(x, x), m)
        for step in range(5):
            inv = _each(lambda x, p: x + _dot3(x, p), inv, power)
            if step < 4:
                power = _each(lambda p: _dot3(p, p), power)
    u = _each(_dot3, inv, vbeta)
    w = _each(lambda x, y, e: _dot3(x, y * e), inv, kbeta, e_gcum)
    a_qk = _each(lambda x, y, d: _bdot(x, y, NT) * d, qn, kn, decay)
    v_new = _each(lambda x, y, s: x - _bdot(y, s), u, w, state)
    o = _each(lambda x, e, s, a, vn: _bdot(x * e, s) + _bdot(a, vn), qn, e_gcum, state, a_qk, v_new)
    new_state = _each(lambda s, gl, x, gc, vn: s * jnp.exp(gl) + _bdot(x * jnp.exp(gl - gc), vn, TN),
                      state, g_last, kn, gcum, v_new)
    return o, new_state, inv


GDN_HEADS_PER_STEP = 8


GDN_TIME_TILE = 256


def _gdn_specs(n_tokens, reverse):
    hb, hd, tt = GDN_HEADS_PER_STEP, GDN_HEAD_DIM, GDN_TIME_TILE
    nb, nt = GDN_HEADS // hb, n_tokens // tt

    def when(t):
        return nt - 1 - t if reverse else t

    q = pl.BlockSpec((tt, hb * hd), lambda h, t: (when(t), h))
    k = pl.BlockSpec((tt, hb * hd), lambda h, t: (when(t), nb + h))
    v = pl.BlockSpec((tt, hb * hd), lambda h, t: (when(t), 2 * nb + h))
    vec = pl.BlockSpec((tt, hb), lambda h, t: (when(t), h))
    states = pl.BlockSpec((hb, tt // GDN_CHUNK, hd, hd), lambda h, t: (h, when(t), 0, 0))
    inverses = pl.BlockSpec((hb, tt // GDN_CHUNK, GDN_CHUNK, GDN_CHUNK), lambda h, t: (h, when(t), 0, 0))
    return q, k, v, vec, states, inverses


def _gdn_fwd(qkv, beta, g):
    n_tokens = qkv.shape[0]
    hb, hd, tt = GDN_HEADS_PER_STEP, GDN_HEAD_DIM, GDN_TIME_TILE
    n_chunks = tt // GDN_CHUNK
    q_s, k_s, v_s, vec, st, inv_s = _gdn_specs(n_tokens, False)

    def body(q_ref, k_ref, v_ref, b_ref, g_ref, o_ref, st_ref, inv_ref, state):
        @pl.when(pl.program_id(1) == 0)
        def _():
            state[...] = jnp.zeros_like(state)

        def step(c, carry):
            r = pl.ds(pl.multiple_of(c * GDN_CHUNK, GDN_CHUNK), GDN_CHUNK)
            cols = [slice(h * hd, (h + 1) * hd) for h in range(hb)]
            old = [state[h] for h in range(hb)]
            o, new, inv = _gdn_chunks(
                [q_ref[r, cs] for cs in cols], [k_ref[r, cs] for cs in cols], [v_ref[r, cs] for cs in cols],
                [b_ref[r, h:h + 1] for h in range(hb)], [g_ref[r, h:h + 1] for h in range(hb)], old)
            for h in range(hb):
                st_ref[h, c] = old[h]
                inv_ref[h, c] = inv[h]
                o_ref[r, cols[h]] = o[h]
                state[h] = new[h]
            return carry

        lax.fori_loop(0, n_chunks, step, 0)

    n_all = n_tokens // GDN_CHUNK
    return pl.pallas_call(
        body, grid=(GDN_HEADS // hb, n_tokens // tt), in_specs=[q_s, k_s, v_s, vec, vec], out_specs=(q_s, st, inv_s),
        out_shape=(jax.ShapeDtypeStruct((n_tokens, GDN_WIDTH), F32),
                   jax.ShapeDtypeStruct((GDN_HEADS, n_all, hd, hd), F32),
                   jax.ShapeDtypeStruct((GDN_HEADS, n_all, GDN_CHUNK, GDN_CHUNK), F32)),
        scratch_shapes=[pltpu.VMEM((hb, hd, hd), F32)],
        name="gdn_fwd", compiler_params=_params(2),
    )(qkv, qkv, qkv, beta, g)


def _gdn_bwd(qkv, beta, g, states, inverses, do):
    n_tokens = qkv.shape[0]
    hb, hd, tt = GDN_HEADS_PER_STEP, GDN_HEAD_DIM, GDN_TIME_TILE
    n_chunks = tt // GDN_CHUNK
    q_s, k_s, v_s, vec, st, inv_s = _gdn_specs(n_tokens, True)

    assert hb == GDN_HEADS

    def body(q_ref, k_ref, v_ref, b_ref, g_ref, st_ref, inv_ref, do_ref, dqkv_ref, db_ref, dg_ref, dstate):
        @pl.when(pl.program_id(1) == 0)
        def _():
            dstate[...] = jnp.zeros_like(dstate)

        def step(i, carry):
            c = n_chunks - 1 - i
            r = pl.ds(pl.multiple_of(c * GDN_CHUNK, GDN_CHUNK), GDN_CHUNK)
            cols = [slice(h * hd, (h + 1) * hd) for h in range(hb)]
            args = ([q_ref[r, cs] for cs in cols], [k_ref[r, cs] for cs in cols], [v_ref[r, cs] for cs in cols],
                    [b_ref[r, h:h + 1] for h in range(hb)], [g_ref[r, h:h + 1] for h in range(hb)],
                    [st_ref[h, c] for h in range(hb)])
            saved = [inv_ref[h, c] for h in range(hb)]
            cts = ([do_ref[r, cs] for cs in cols], [dstate[h] for h in range(hb)])
            dq, dk, dv, db, dg, dst = jax.vjp(lambda *a: _gdn_chunks(*a, inverse=saved)[:2], *args)[1](cts)
            for h in range(hb):
                for part, grad in enumerate((dq, dk, dv)):
                    dqkv_ref[r, pl.ds(part * GDN_WIDTH + h * hd, hd)] = grad[h]
                db_ref[r, h:h + 1] = db[h]
                dg_ref[r, h:h + 1] = dg[h]
                dstate[h] = dst[h]
            return carry

        lax.fori_loop(0, n_chunks, step, 0)

    n_t = n_tokens // tt
    thin = jax.ShapeDtypeStruct(beta.shape, F32)
    return pl.pallas_call(
        body, grid=(GDN_HEADS // hb, n_t), in_specs=[q_s, k_s, v_s, vec, vec, st, inv_s, q_s],
        out_specs=(pl.BlockSpec((tt, 3 * GDN_WIDTH), lambda h, t: (n_t - 1 - t, 0)), vec, vec),
        out_shape=(jax.ShapeDtypeStruct(qkv.shape, F32), thin, thin),
        scratch_shapes=[pltpu.VMEM((hb, hd, hd), F32)],
        name="gdn_bwd", compiler_params=_params(2),
    )(qkv, qkv, qkv, beta, g, states, inverses, do)


FFN_ROW_TILE = 256
FFN_CHUNK = 256
FFN_FWD_ROW_TILE = 512


def _resident(shape):
    return pl.BlockSpec(shape, lambda i: (0,) * len(shape), pipeline_mode=pl.Buffered(1))


def _ffn_fwd(x, gain, wg, wu, wd, name):
    n_tokens, d = x.shape
    n_shards, n, _ = wg.shape
    tm = FFN_FWD_ROW_TILE

    def body(x_ref, gain_ref, wg_ref, wu_ref, wd_ref, o_ref, g_ref, u_ref):
        xv = x_ref[...]
        h = (xv * lax.rsqrt(jnp.mean(xv * xv, axis=-1, keepdims=True) + EPS) * gain_ref[...]).astype(BF16)
        acc = jnp.zeros((tm, d), F32)
        for j in range(n_shards):
            g = lax.dot_general(h, wg_ref[j], NT, preferred_element_type=F32)
            u = lax.dot_general(h, wu_ref[j], NT, preferred_element_type=F32)
            g_ref[j] = g
            u_ref[j] = u
            a = (g * jax.nn.sigmoid(g) * u).astype(BF16)
            acc = acc + lax.dot_general(a, wd_ref[j], NN, preferred_element_type=F32)
        o_ref[...] = xv + 0.5 * acc

    row = pl.BlockSpec((tm, d), lambda i: (i, 0))
    hid = pl.BlockSpec((n_shards, tm, n), lambda i: (0, i, 0))
    return pl.pallas_call(
        body, grid=(n_tokens // tm,),
        in_specs=[row, _resident(gain.shape), _resident(wg.shape), _resident(wu.shape), _resident(wd.shape)],
        out_specs=(row, hid, hid),
        out_shape=(jax.ShapeDtypeStruct(x.shape, F32), jax.ShapeDtypeStruct((n_shards, n_tokens, n), F32),
                   jax.ShapeDtypeStruct((n_shards, n_tokens, n), F32)),
        name=name, compiler_params=_params(1),
    )(x, gain, wg, wu, wd)


def _ffn_bwd_rows(x, gain, dy, g, u, wg, wu, wd, name):
    n_tokens, d = x.shape
    n_shards, n, _ = wg.shape
    tm = FFN_ROW_TILE

    def body(x_ref, gain_ref, dy_ref, g_ref, u_ref, wg_ref, wu_ref, wd_ref,
             dx_ref, dgain_ref, h_ref, dyh_ref, a_ref, dg_ref, du_ref):
        xv, dyv, gain_v = x_ref[...], dy_ref[...], gain_ref[...]
        r = lax.rsqrt(jnp.mean(xv * xv, axis=-1, keepdims=True) + EPS)
        xhat = xv * r
        h_ref[...] = (xhat * gain_v).astype(BF16)
        dyh = (0.5 * dyv).astype(BF16)
        dyh_ref[...] = dyh
        dh = jnp.zeros((tm, d), F32)
        for j in range(n_shards):
            da = lax.dot_general(dyh, wd_ref[j], NT, preferred_element_type=F32)
            gv, uv = g_ref[j], u_ref[j]
            sg = jax.nn.sigmoid(gv)
            silu = gv * sg
            a_ref[j] = (silu * uv).astype(BF16)
            dg = (da * uv * (sg + silu * (1.0 - sg))).astype(BF16)
            du = (da * silu).astype(BF16)
            dg_ref[j] = dg
            du_ref[j] = du
            dh = dh + lax.dot_general(dg, wg_ref[j], NN, preferred_element_type=F32)
            dh = dh + lax.dot_general(du, wu_ref[j], NN, preferred_element_type=F32)
        dxhat = dh * gain_v
        dx_ref[...] = dyv + r * (dxhat - xhat * jnp.mean(dxhat * xhat, axis=-1, keepdims=True))

        @pl.when(pl.program_id(0) == 0)
        def _():
            dgain_ref[...] = jnp.zeros_like(dgain_ref)

        dgain_ref[...] += jnp.sum(dh * xhat, axis=0, keepdims=True)

    row = pl.BlockSpec((tm, d), lambda i: (i, 0))
    hid = pl.BlockSpec((n_shards, tm, n), lambda i: (0, i, 0))
    hid_shape = (n_shards, n_tokens, n)
    return pl.pallas_call(
        body, grid=(n_tokens // tm,),
        in_specs=[row, _resident(gain.shape), row, hid, hid, _resident(wg.shape), _resident(wu.shape),
                  _resident(wd.shape)],
        out_specs=(row, pl.BlockSpec(gain.shape, lambda i: (0, 0)), row, row, hid, hid, hid),
        out_shape=(jax.ShapeDtypeStruct(x.shape, F32), jax.ShapeDtypeStruct(gain.shape, F32),
                   jax.ShapeDtypeStruct(x.shape, BF16), jax.ShapeDtypeStruct(x.shape, BF16),
                   jax.ShapeDtypeStruct(hid_shape, BF16), jax.ShapeDtypeStruct(hid_shape, BF16),
                   jax.ShapeDtypeStruct(hid_shape, BF16)),
        name=name, compiler_params=_params(1),
    )(x, gain, dy, g, u, wg, wu, wd)


def _ffn_bwd_weights(h, dyh, a, dg, du, name, with_payload=False):
    n_chunks, n_tokens, n = a.shape
    d = h.shape[1]

    def body(h_ref, dyh_ref, a_ref, dg_ref, du_ref, *out_refs):
        hv = h_ref[...]
        vals = (lax.dot_general(dg_ref[0], hv, TN, preferred_element_type=F32),
                lax.dot_general(du_ref[0], hv, TN, preferred_element_type=F32),
                lax.dot_general(a_ref[0], dyh_ref[...], TN, preferred_element_type=F32))
        for ref, val in zip(out_refs[-3:], vals):
            ref[0] = val
        if with_payload:
            for ref, val in zip(out_refs[:3], vals):
                ref[0] = val.astype(BF16)

    hid = pl.BlockSpec((1, n_tokens, n), lambda j: (j, 0, 0))
    out = pl.BlockSpec((1, n, d), lambda j: (j, 0, 0))
    shapes = (jax.ShapeDtypeStruct((n_chunks, n, d), F32),) * 3
    if with_payload:
        shapes = (jax.ShapeDtypeStruct((n_chunks, n, d), BF16),) * 3 + shapes
    outs = pl.pallas_call(
        body, grid=(n_chunks,), in_specs=[_resident(h.shape), _resident(dyh.shape), hid, hid, hid],
        out_specs=(out,) * len(shapes), out_shape=shapes, name=name, compiler_params=_params(1),
    )(h, dyh, a, dg, du)
    return (outs[:3], outs[3:]) if with_payload else outs


IN_PIECES = (("wq_a", 0, 768), ("wk_a", 768, 1536), ("wv_a", 1536, 2304), ("w_qkvb", 2304, 5376),
             ("w_small", 5376, 5392), ("w_ggate", 5392, 6416), ("w_gatea", 6416, 7440), ("w_gateb", 7440, 8464))
IN_NAMES = tuple(name for name, _, _ in IN_PIECES)


def _in_rows(lo, hi):
    return lo, max(hi, lo + LANES)


N_ROTATED = 2


def _in_proj_fwd(x, gain, wt, cos, sin):
    n_tokens, d = x.shape
    tm = FFN_ROW_TILE
    rows = [_in_rows(lo, hi) for _, lo, hi in IN_PIECES]

    def body(x_ref, gain_ref, wt_ref, cos_ref, sin_ref, *o_refs):
        xv = x_ref[...]
        h = (xv * lax.rsqrt(jnp.mean(xv * xv, axis=-1, keepdims=True) + EPS) * gain_ref[...]).astype(BF16)
        for k, ((lo, hi), o_ref) in enumerate(zip(rows, o_refs)):
            z = lax.dot_general(h, wt_ref[lo:hi, :], NT, preferred_element_type=F32)
            o_ref[...] = _rotate(z, cos_ref[...], sin_ref[...]) if k < N_ROTATED else z

    tab = pl.BlockSpec((tm, LANES), lambda i: (i, 0))
    return pl.pallas_call(
        body, grid=(n_tokens // tm,),
        in_specs=[pl.BlockSpec((tm, d), lambda i: (i, 0)), _resident(gain.shape), _resident(wt.shape), tab, tab],
        out_specs=tuple(pl.BlockSpec((tm, hi - lo), lambda i: (i, 0)) for lo, hi in rows),
        out_shape=tuple(jax.ShapeDtypeStruct((n_tokens, hi - lo), F32) for lo, hi in rows),
        name="in_proj_fwd", compiler_params=_params(1),
    )(x, gain, wt, cos, sin)


def _in_proj_bwd_rows(x, gain, dres, dzs, wt, cos, sin):
    n_tokens, d = x.shape
    tm = FFN_ROW_TILE
    n = len(dzs)
    rows = [_in_rows(lo, hi) for _, lo, hi in IN_PIECES]

    def body(x_ref, gain_ref, dres_ref, cos_ref, sin_ref, *refs):
        dz_refs, wt_ref = refs[:n], refs[n]
        dx_ref, dgain_ref, h_ref = refs[n + 1:n + 4]
        unrotated_refs = refs[n + 4:]
        xv, gain_v = x_ref[...], gain_ref[...]
        r = lax.rsqrt(jnp.mean(xv * xv, axis=-1, keepdims=True) + EPS)
        xhat = xv * r
        h_ref[...] = (xhat * gain_v).astype(BF16)
        dh = jnp.zeros((tm, d), F32)
        for k, (dz_ref, (lo, hi)) in enumerate(zip(dz_refs, rows)):
            dz = dz_ref[...]
            if k < N_ROTATED:
                dz = _rotate(dz, cos_ref[...], -sin_ref[...]).astype(BF16)
                unrotated_refs[k][...] = dz
            dh = dh + lax.dot_general(dz.astype(BF16), wt_ref[lo:hi, :], NN, preferred_element_type=F32)
        dxhat = dh * gain_v
        dx_ref[...] = dres_ref[...] + r * (dxhat - xhat * jnp.mean(dxhat * xhat, axis=-1, keepdims=True))

        @pl.when(pl.program_id(0) == 0)
        def _():
            dgain_ref[...] = jnp.zeros_like(dgain_ref)

        dgain_ref[...] += jnp.sum(dh * xhat, axis=0, keepdims=True)

    row = pl.BlockSpec((tm, d), lambda i: (i, 0))
    tab = pl.BlockSpec((tm, LANES), lambda i: (i, 0))
    dz_specs = [pl.BlockSpec((tm, dz.shape[1]), lambda i: (i, 0)) for dz in dzs]
    outs = pl.pallas_call(
        body, grid=(n_tokens // tm,),
        in_specs=[row, _resident(gain.shape), row, tab, tab] + dz_specs + [_resident(wt.shape)],
        out_specs=(row, pl.BlockSpec(gain.shape, lambda i: (0, 0)), row) + tuple(dz_specs[:N_ROTATED]),
        out_shape=(jax.ShapeDtypeStruct(x.shape, F32), jax.ShapeDtypeStruct(gain.shape, F32),
                   jax.ShapeDtypeStruct(x.shape, BF16))
        + tuple(jax.ShapeDtypeStruct(dz.shape, BF16) for dz in dzs[:N_ROTATED]),
        name="in_proj_bwd_rows", compiler_params=_params(1),
    )(x, gain, dres, cos, sin, *dzs, wt)
    return outs[0], outs[1], outs[2], outs[3:]


def _in_proj_bwd_weight(dwt, h, dz, lo, hi, name):
    n_tokens, d = h.shape
    width = hi - lo
    tn = _tile(width, 512) if width >= LANES else width
    dz_tile = max(tn, LANES)

    def body(dwt_ref, h_ref, dz_ref, o_ref):
        o_ref[...] = lax.dot_general(dz_ref[:, :tn].astype(BF16), h_ref[...], TN, preferred_element_type=F32)

    return pl.pallas_call(
        body, grid=(width // tn,),
        in_specs=[ANY, _resident(h.shape), pl.BlockSpec((n_tokens, dz_tile), lambda j: (0, j))],
        out_specs=pl.BlockSpec((pl.Element(tn), pl.Element(d)), lambda j: (pl.multiple_of(lo + j * tn, 16), 0)),
        out_shape=jax.ShapeDtypeStruct(dwt.shape, F32), input_output_aliases={0: 0}, name=name,
        compiler_params=_params(1),
    )(dwt, h, dz)


def _split_small(z):
    return z[:, :GDN_HEADS], z[:, GDN_HEADS:2 * GDN_HEADS]


def _heads3(q, k, v):
    return _to_heads(q), _to_heads(k), _to_heads(v)


def _tokens6(o, lse):
    return tuple(_from_heads(o)) + tuple(_from_heads(lse))


def _blocks_of(vals, nblk):
    return [[v[:, b * (v.shape[1] // nblk):(b + 1) * (v.shape[1] // nblk)] for v in vals] for b in range(nblk)]


def _rowwise_matmul_fwd(fn, name, rows, params, wt, nblk, res=None):
    n_rows = rows[0].shape[0]
    tm = FFN_ROW_TILE
    k, n = wt.shape
    nr, npar = len(rows), len(params)

    def body(*refs):
        row_vals = [r[...] for r in refs[:nr]]
        par_vals = [r[...] for r in refs[nr:nr + npar]]
        wt_ref = refs[nr + npar]
        o_ref, y_ref = refs[-2:]
        y = jnp.concatenate([fn(*blk, *par_vals)[0] for blk in _blocks_of(row_vals, nblk)], axis=1).astype(BF16)
        y_ref[...] = y
        acc = lax.dot_general(y, wt_ref[...], NN, preferred_element_type=F32)
        o_ref[...] = acc if res is None else refs[nr + npar + 1][...] + acc

    row_specs = [pl.BlockSpec((tm, a.shape[1]), lambda i: (i, 0)) for a in rows]
    ins = list(rows) + list(params) + [wt] + ([] if res is None else [res])
    specs = row_specs + [_resident(p.shape) for p in params] + [_resident(wt.shape)]
    if res is not None:
        specs.append(pl.BlockSpec((tm, n), lambda i: (i, 0)))
    return pl.pallas_call(
        body, grid=(n_rows // tm,), in_specs=specs,
        out_specs=(pl.BlockSpec((tm, n), lambda i: (i, 0)), pl.BlockSpec((tm, k), lambda i: (i, 0))),
        out_shape=(jax.ShapeDtypeStruct((n_rows, n), F32), jax.ShapeDtypeStruct((n_rows, k), BF16)),
        name=name, compiler_params=_params(1),
    )(*ins)


def _rowwise_matmul_bwd(fn, name, rows, params, wt, dout, nblk):
    n_rows = rows[0].shape[0]
    tm = FFN_ROW_TILE
    nr, npar = len(rows), len(params)

    def body(*refs):
        row_vals = [r[...] for r in refs[:nr]]
        par_vals = [r[...] for r in refs[nr:nr + npar]]
        wt_ref, dout_ref = refs[nr + npar], refs[nr + npar + 1]
        outs = refs[nr + npar + 2:]
        dy = lax.dot_general(dout_ref[...].astype(BF16), wt_ref[...], NT, preferred_element_type=F32)
        grads = [jax.vjp(fn, *blk, *par_vals)[1]((dy_blk,))
                 for blk, (dy_blk,) in zip(_blocks_of(row_vals, nblk), _blocks_of([dy], nblk))]
        for j in range(nr):
            outs[j][...] = jnp.concatenate([g[j] for g in grads], axis=1)
        for j in range(npar):
            ref = outs[nr + j]

            @pl.when(pl.program_id(0) == 0)
            def _(ref=ref):
                ref[...] = jnp.zeros_like(ref)

            for g in grads:
                ref[...] += g[nr + j]

    row_specs = [pl.BlockSpec((tm, a.shape[1]), lambda i: (i, 0)) for a in rows]
    par_specs = [_resident(p.shape) for p in params]
    return pl.pallas_call(
        body, grid=(n_rows // tm,),
        in_specs=row_specs + par_specs + [_resident(wt.shape), pl.BlockSpec((tm, dout.shape[1]), lambda i: (i, 0))],
        out_specs=tuple(row_specs + [pl.BlockSpec(p.shape, lambda i: (0, 0)) for p in params]),
        out_shape=tuple([jax.ShapeDtypeStruct(a.shape, F32) for a in rows]
                        + [jax.ShapeDtypeStruct(p.shape, F32) for p in params]),
        name=name, compiler_params=_params(1),
    )(*rows, *params, wt, dout)


def mixer_forward(x1, w, small):
    n_tokens = x1.shape[0]
    cos, sin = _rope_tables(n_tokens)
    proj = dict(zip(IN_NAMES, _in_proj_fwd(x1, small["mix_norm"], w["w_in_t"], cos, sin)))
    (qh, kh, vh), heads_vjp = jax.vjp(_heads3, proj["wq_a"], proj["wk_a"], proj["wv_a"])
    o, lse = _attn_fwd(qh, kh, vh)
    per_group, tokens_vjp = jax.vjp(_tokens6, o, lse)
    pa, ya = _rowwise_matmul_fwd(_combine_fn, "branch_a", per_group, (), w["w_branch_a"], 1)
    qkv = _conv_fwd(proj["w_qkvb"], small["gdn_conv_w"])
    raw, small_vjp = jax.vjp(_split_small, proj["w_small"])
    gdn_params = (small["gdn_a_log"], small["gdn_dt_bias"])
    beta, gcum = _rowwise_fwd(_beta_decay_fn, "beta_decay", raw, (), gdn_params, 512, 1)
    ob, *states = _gdn_fwd(qkv, beta, gcum)
    gate_in = (ob, proj["w_ggate"])
    pb, yb = _rowwise_matmul_fwd(_outnorm_gate_fn, "branch_b", gate_in, (small["gdn_out_norm"],), w["w_branch_b"],
                                 GDN_HEADS)
    merge_in = (proj["w_gatea"], proj["w_gateb"], pa, pb)
    x2, merged = _rowwise_matmul_fwd(_merge_fn, "out", merge_in, (), w["w_out"], 1, res=x1)
    saved = dict(x1=x1, proj=proj, cos=cos, sin=sin, heads_vjp=heads_vjp, heads=(qh, kh, vh), tokens_vjp=tokens_vjp,
                 per_group=per_group, ya=ya, qkv=qkv, raw=raw, small_vjp=small_vjp, beta=beta, gcum=gcum, states=states,
                 gate_in=gate_in, yb=yb, merge_in=merge_in, merged=merged)
    return x2, saved


def mixer_backward(dx2, s, w, small):
    proj = s["proj"]
    grads = dict(w_out=_weight_grad(s["merged"], dx2, "out_dw"))
    dgate_a, dgate_b, dpa, dpb = _rowwise_matmul_bwd(_merge_fn, "out_bwd", s["merge_in"], (), w["w_out"], dx2, 1)
    grads["w_branch_b"] = _weight_grad(s["yb"], dpb, "branch_b_dw")
    grads["w_branch_a"] = _weight_grad(s["ya"], dpa, "branch_a_dw")
    dob, dggate, grads["gdn_out_norm"] = _rowwise_matmul_bwd(
        _outnorm_gate_fn, "branch_b_bwd", s["gate_in"], (small["gdn_out_norm"],), w["w_branch_b"], dpb, GDN_HEADS)
    dqkv, dbeta, dgcum = _gdn_bwd(s["qkv"], s["beta"], s["gcum"], *s["states"], dob)
    gdn_params = (small["gdn_a_log"], small["gdn_dt_bias"])
    dbeta_raw, ddecay_raw, grads["gdn_a_log"], grads["gdn_dt_bias"] = _rowwise_bwd(
        _beta_decay_fn, "beta_decay_bwd", s["raw"], (), gdn_params, (dbeta, dgcum), 512, 1)
    dsmall = s["small_vjp"]((dbeta_raw, ddecay_raw))[0]
    dqkvb, grads["gdn_conv_w"] = _conv_bwd(proj["w_qkvb"], small["gdn_conv_w"], dqkv)
    dper_group = _rowwise_matmul_bwd(_combine_fn, "branch_a_bwd", s["per_group"], (), w["w_branch_a"], dpa, 1)
    do, dlse = s["tokens_vjp"](tuple(dper_group))
    dqh, dkh, dvh = _attn_bwd(*s["heads"], do, dlse)
    dq_rot, dk_rot, dv = s["heads_vjp"]((dqh, dkh, dvh))
    dzs = (dq_rot, dk_rot, dv, dqkvb, dsmall, dggate, dgate_a, dgate_b)
    dx1, grads["mix_norm"], h, unrotated = _in_proj_bwd_rows(
        s["x1"], small["mix_norm"], dx2, dzs, w["w_in_t"], s["cos"], s["sin"])
    dzs = tuple(unrotated) + dzs[N_ROTATED:]
    dwt = lax.empty(w["w_in_t"].shape, F32)
    for (name, lo, hi), dz in zip(IN_PIECES, dzs):
        dwt = _in_proj_bwd_weight(dwt, h, dz, lo, hi, "in_proj_dw_" + name)
    grads["w_in_t"] = dwt
    return dx1, grads


def ffn_forward(x, gain, w, tag):
    out, g, u = _ffn_fwd(x, gain, w[tag + "_w_gate"], w[tag + "_w_up"], w[tag + "_w_down"], tag + "_fwd")
    return out, (x, g, u)


def ffn_backward(dy, saved, gain, w, tag, with_payload=False):
    x, g, u = saved
    weights = (w[tag + "_w_gate"], w[tag + "_w_up"], w[tag + "_w_down"])
    dx, dgain, h, dyh, a, dg, du = _ffn_bwd_rows(x, gain, dy, g, u, *weights, tag + "_bwd_rows")
    return dx, dgain, _ffn_bwd_weights(h, dyh, a, dg, du, tag + "_bwd_weights", with_payload)


def loss_head(x3, target, gain):
    n_tokens, d = x3.shape
    tm = FFN_ROW_TILE

    def body(x_ref, t_ref, gain_ref, loss_ref, dx_ref, dgain_ref):
        target_v = t_ref[...]
        (row_loss,), vjp = jax.vjp(lambda xv, gv: _loss_fn(xv, target_v, gv), x_ref[...], gain_ref[...])
        dx, dgain = vjp((jnp.ones_like(row_loss),))
        loss_ref[...] = row_loss
        dx_ref[...] = dx

        @pl.when(pl.program_id(0) == 0)
        def _():
            dgain_ref[...] = jnp.zeros_like(dgain_ref)

        dgain_ref[...] += dgain

    row = pl.BlockSpec((tm, d), lambda i: (i, 0))
    row_loss, dx3, dgain = pl.pallas_call(
        body, grid=(n_tokens // tm,), in_specs=[row, row, _resident(gain.shape)],
        out_specs=(pl.BlockSpec((tm, 1), lambda i: (i, 0)), row, pl.BlockSpec(gain.shape, lambda i: (0, 0))),
        out_shape=(jax.ShapeDtypeStruct((n_tokens, 1), F32), jax.ShapeDtypeStruct(x3.shape, F32),
                   jax.ShapeDtypeStruct(gain.shape, F32)),
        name="loss_head", compiler_params=_params(1),
    )(x3, target, gain)
    return jnp.sum(row_loss), dx3, dgain


BIG_WEIGHTS = ("ffn1_w_gate", "ffn1_w_up", "ffn1_w_down", "w_in", "w_branch_a", "w_branch_b", "w_out",
               "ffn2_w_gate", "ffn2_w_up", "ffn2_w_down")
TRANSPOSED = ("ffn1_w_gate", "ffn1_w_up", "w_in", "ffn2_w_gate", "ffn2_w_up")
CONV_SHARD = (GDN_CONV, 3 * GDN_WIDTH // N_DEV)
SMALL_ROWS = 24
ANY = pl.BlockSpec(memory_space=pl.ANY)


TOKEN = jax.ShapeDtypeStruct((8, LANES), F32)


def _after(value, token):
    return value + token[0, 0].astype(value.dtype)


def _position():
    return lax.axis_index("x"), lax.axis_index("y"), lax.axis_index("c")


def all_gather_shards(shards, name):
    n = len(shards)

    def body(*refs):
        x_refs, out_refs = refs[:n], refs[n:2 * n]
        send_sems, recv_sems, local_sems = refs[2 * n + 1:]
        x, y, c = _position()
        me, sibling = (x, y, c), (x, y, 1 - c)
        chips = [(1 - x, y), (x, 1 - y), (1 - x, 1 - y)]

        def slab(a, px, py, pc):
            return out_refs[a].at[4 * px + 2 * py + pc]

        def copy(a, k, block, to, src=None):
            return pltpu.make_async_remote_copy(
                src_ref=slab(a, *block) if src is None else src, dst_ref=slab(a, *block),
                send_sem=send_sems.at[7 * a + k], recv_sem=recv_sems.at[7 * a + k], device_id=to, device_id_type=MESH)

        mine = [pltpu.make_async_copy(x_refs[a], slab(a, *me), local_sems.at[a]) for a in range(n)]
        for cp in mine:
            cp.start()
        first = []
        for j, chip in enumerate(chips):
            first += [copy(a, 1 + j, me, (*chip, c), src=x_refs[a]) for a in range(n)]
        first += [copy(a, 0, me, sibling, src=x_refs[a]) for a in range(n)]
        for cp in first:
            cp.start()
        passed = []
        for j, chip in enumerate(chips):
            for a in range(n):
                copy(a, 1 + j, (*chip, c), me).wait_recv()
                cp = copy(a, 4 + j, (*chip, c), sibling)
                cp.start()
                passed.append(cp)
        for a in range(n):
            copy(a, 0, sibling, me).wait_recv()
        for j, chip in enumerate(chips):
            for a in range(n):
                copy(a, 4 + j, (*chip, 1 - c), me).wait_recv()
        for cp in first + passed:
            cp.wait_send()
        for cp in mine:
            cp.wait()
        refs[2 * n][...] = jnp.zeros_like(refs[2 * n])

    outs = pl.pallas_call(
        body, out_shape=tuple(jax.ShapeDtypeStruct((N_DEV,) + s.shape, s.dtype) for s in shards) + (TOKEN,),
        in_specs=[ANY] * n, out_specs=(ANY,) * n + (pl.BlockSpec(memory_space=pltpu.VMEM),),
        scratch_shapes=[pltpu.SemaphoreType.DMA((7 * n,)), pltpu.SemaphoreType.DMA((7 * n,)),
                        pltpu.SemaphoreType.DMA((n,))],
        name=name,
    )(*shards)
    return outs[:n], outs[n]


def exchange_with_sibling(grads):
    n = len(grads)

    def body(*refs):
        g_refs, recv_refs = refs[:n], refs[n:2 * n]
        send_sems, recv_sems = refs[2 * n:]
        x, y, c = _position()
        copies = [pltpu.make_async_remote_copy(
            src_ref=g_refs[a].at[2 * k + 1 - c], dst_ref=recv_refs[a].at[k], send_sem=send_sems.at[4 * a + k],
            recv_sem=recv_sems.at[4 * a + k], device_id=(x, y, 1 - c), device_id_type=MESH)
            for k in range(4) for a in range(n)]
        for cp in copies:
            cp.start()
        for cp in copies:
            cp.wait()

    return pl.pallas_call(
        body, out_shape=tuple(jax.ShapeDtypeStruct((4,) + g.shape[1:], g.dtype) for g in grads),
        in_specs=[ANY] * n, out_specs=(ANY,) * n,
        scratch_shapes=[pltpu.SemaphoreType.DMA((4 * n,)), pltpu.SemaphoreType.DMA((4 * n,))], name="rs_sibling",
    )(*grads)


ELEMENTWISE_TILE_BYTES = 1536 * 1024


def _tile2(rows, cols):
    if rows % 256 == 0:
        return 256, cols
    if rows * cols * 4 > ELEMENTWISE_TILE_BYTES and cols % 256 == 0:
        return rows, 256
    return rows, cols


def add_sibling(grads, received, core, name):
    _, rows, width = grads.shape
    tr, tc = _tile2(rows, width)

    def body(c_ref, g_ref, r_ref, o_ref):
        o_ref[...] = (g_ref[...] + r_ref[...]).astype(BF16)

    blk = (1, tr, tc)
    return pl.pallas_call(
        body,
        grid_spec=pltpu.PrefetchScalarGridSpec(
            num_scalar_prefetch=1, grid=(4, rows // tr, width // tc),
            in_specs=[pl.BlockSpec(blk, lambda k, i, j, c_ref: (2 * k + c_ref[0], i, j)),
                      pl.BlockSpec(blk, lambda k, i, j, c_ref: (k, i, j))],
            out_specs=pl.BlockSpec(blk, lambda k, i, j, c_ref: (k, i, j))),
        out_shape=jax.ShapeDtypeStruct((4, rows, width), BF16), name=name, compiler_params=_params(3),
    )(core, grads, received)


HBM = pl.BlockSpec(memory_space=pltpu.HBM)
SEM = pl.BlockSpec(memory_space=pltpu.SEMAPHORE)
DATAFLOW_EFFECT = pltpu.SideEffectType.DATAFLOW_SIDE_EFFECTING
N_PEERS = N_DEV - 1


def _peer(mask):
    x, y, c = _position()
    px = 1 - x if mask & 4 else x
    py = 1 - y if mask & 2 else y
    pc = 1 - c if mask & 1 else c
    return (px, py, pc), 4 * px + 2 * py + pc


ALL_PEERS = tuple(range(1, N_DEV))
OTHER_CHIPS = (4, 2, 6)


SIBLING = 1
GATHER_MODES = ("gather", "near")


def _exchange_peers(mode):
    return {"chips": OTHER_CHIPS, "near": (SIBLING,) + OTHER_CHIPS}.get(mode, ALL_PEERS)


def _direct_copies(src_refs, land_refs, send_sems, recv_sems, mode):
    x, y, c = _position()
    me = 4 * x + 2 * y + c
    masks = _exchange_peers(mode)
    copies = []
    for a, (src, land) in enumerate(zip(src_refs, land_refs)):
        for slot, mask in enumerate(masks):
            peer, peer_index = _peer(mask)
            k = len(masks) * a + slot
            if mode in GATHER_MODES:
                source, dest = src, land.at[me]
            elif mode == "scatter":
                source, dest = src.at[peer_index], land.at[slot]
            else:
                source, dest = src.at[2 * peer[0] + peer[1]], land.at[slot]
            copies.append(pltpu.make_async_remote_copy(
                src_ref=source, dst_ref=dest, send_sem=send_sems.at[k], recv_sem=recv_sems.at[k], device_id=peer,
                device_id_type=MESH))
    return copies


def forward_to_sibling(slabs, name):
    n = len(slabs)

    def body(*refs):
        out_refs = refs[n:2 * n]
        send_sems, recv_sems = refs[2 * n + 1:]
        x, y, c = _position()
        copies = []
        for a in range(n):
            for slot, mask in enumerate(OTHER_CHIPS):
                _, held = _peer(mask)
                copies.append(pltpu.make_async_remote_copy(
                    src_ref=out_refs[a].at[held], dst_ref=out_refs[a].at[held], send_sem=send_sems.at[3 * a + slot],
                    recv_sem=recv_sems.at[3 * a + slot], device_id=(x, y, 1 - c), device_id_type=MESH))
        for cp in copies:
            cp.start()
        for cp in copies:
            cp.wait()
        refs[2 * n][...] = jnp.zeros_like(refs[2 * n])

    outs = pl.pallas_call(
        body, out_shape=tuple(jax.ShapeDtypeStruct(s.shape, s.dtype) for s in slabs) + (TOKEN,),
        in_specs=[ANY] * n, out_specs=(ANY,) * n + (pl.BlockSpec(memory_space=pltpu.VMEM),),
        input_output_aliases={i: i for i in range(n)},
        scratch_shapes=[pltpu.SemaphoreType.DMA((3 * n,)), pltpu.SemaphoreType.DMA((3 * n,))], name=name,
    )(*slabs)
    return outs[:n], outs[n]


def direct_exchange_start(arrays, mode, name):
    n = len(arrays)
    n_peers = len(_exchange_peers(mode))
    lands = [lax.empty((N_DEV,) + a.shape if mode in GATHER_MODES else (n_peers,) + a.shape[1:], a.dtype)
             for a in arrays]

    def body(*refs):
        src_refs, land_refs = refs[:n], refs[n:2 * n]
        send_sems, recv_sems = refs[2 * n], refs[2 * n + 1]
        token = refs[-1]
        for cp in _direct_copies(src_refs, land_refs, send_sems, recv_sems, mode):
            cp.start()
        token[...] = jnp.zeros_like(token)

    sems = pltpu.SemaphoreType.DMA((n_peers * n,))
    outs = pl.pallas_call(
        body, name=name,
        out_shape=(sems, sems) + tuple(pltpu.HBM(a.shape, a.dtype) for a in arrays)
        + tuple(pltpu.HBM(l.shape, l.dtype) for l in lands) + (TOKEN,),
        in_specs=[HBM] * (2 * n), out_specs=(SEM, SEM) + (HBM,) * (2 * n) + (pl.BlockSpec(memory_space=pltpu.VMEM),),
        input_output_aliases={i: 2 + i for i in range(2 * n)},
        compiler_params=pltpu.CompilerParams(has_side_effects=DATAFLOW_EFFECT),
    )(*[pltpu.with_memory_space_constraint(a, pltpu.HBM) for a in list(arrays) + lands])
    return outs[0], outs[1], outs[2:2 + n], outs[2 + n:2 + 2 * n], outs[-1]


def direct_exchange_wait(send_sems, recv_sems, arrays, lands, after, mode, name):
    n = len(arrays)

    def body(*refs):
        src_refs, land_refs = refs[:n], refs[n:2 * n]
        send_sems, recv_sems = refs[2 * n], refs[2 * n + 1]
        for cp in _direct_copies(src_refs, land_refs, send_sems, recv_sems, mode):
            cp.wait_send()
            cp.wait_recv()
        refs[-1][...] = jnp.zeros_like(refs[-1])

    outs = pl.pallas_call(
        body, name=name,
        out_shape=tuple(pltpu.HBM(a.shape, a.dtype) for a in arrays) + tuple(pltpu.HBM(l.shape, l.dtype) for l in lands)
        + (TOKEN,),
        in_specs=[HBM] * (2 * n) + [SEM, SEM, pl.BlockSpec(memory_space=pl.ANY)],
        out_specs=(HBM,) * (2 * n) + (pl.BlockSpec(memory_space=pltpu.VMEM),),
        input_output_aliases={i: i for i in range(2 * n)},
        compiler_params=pltpu.CompilerParams(has_side_effects=DATAFLOW_EFFECT),
    )(*arrays, *lands, send_sems, recv_sems, after)
    return outs[n:]


def adamw_direct(w, m, v, own, received, name):
    row_per_tile = w.shape[0] != 1
    rows, cols = (w.shape[0], w.shape[2]) if row_per_tile else w.shape[-2:]
    tr, tc = _tile2(rows, cols)

    def body(w_ref, m_ref, v_ref, own_ref, r_ref, g_ref, d_ref, nm_ref, nv_ref):
        gv = own_ref[0]
        for j in range(N_PEERS):
            gv = gv + r_ref[j].astype(F32)
        nm = ADAM_B1 * m_ref[...] + (1.0 - ADAM_B1) * gv
        nv = ADAM_B2 * v_ref[...] + (1.0 - ADAM_B2) * (gv * gv)
        m_hat = nm / (1.0 - ADAM_B1 ** ADAM_STEP)
        v_hat = nv / (1.0 - ADAM_B2 ** ADAM_STEP)
        g_ref[...] = gv
        d_ref[...] = -ADAM_LR * (m_hat / (jnp.sqrt(v_hat) + ADAM_EPS) + ADAM_WD * w_ref[...])
        nm_ref[...] = nm
        nv_ref[...] = nv

    if row_per_tile:
        one = pl.BlockSpec((tr, None, tc), lambda i, j: (i, 0, j))
    else:
        one = pl.BlockSpec((None, tr, tc), lambda i, j: (0, i, j))
    out = jax.ShapeDtypeStruct(w.shape, F32)
    return pl.pallas_call(
        body, grid=(rows // tr, cols // tc),
        in_specs=[one, one, one, pl.BlockSpec((1, tr, tc), lambda i, j: (0, i, j)),
                  pl.BlockSpec((N_PEERS, tr, tc), lambda i, j: (0, i, j))],
        out_specs=(one,) * 4, out_shape=(out,) * 4, name=name, compiler_params=_params(2),
    )(w, m, v, own, received)


def all_reduce_small(vals):
    rows, width = vals.shape

    def body(x_ref, out_ref, all_ref, send_sems, recv_sems):
        x, y, c = _position()
        me, sibling = (x, y, c), (x, y, 1 - c)
        chips = [(1 - x, y), (x, 1 - y), (1 - x, 1 - y)]

        def slab(px, py, pc):
            return all_ref.at[4 * px + 2 * py + pc]

        def copy(k, block, to, src=None):
            return pltpu.make_async_remote_copy(
                src_ref=slab(*block) if src is None else src, dst_ref=slab(*block),
                send_sem=send_sems.at[k], recv_sem=recv_sems.at[k], device_id=to, device_id_type=MESH)

        first = [copy(0, me, sibling, src=x_ref)]
        first += [copy(1 + j, me, (*chip, c), src=x_ref) for j, chip in enumerate(chips)]
        for cp in first:
            cp.start()
        all_ref[4 * x + 2 * y + c] = x_ref[...]
        passed = [copy(4 + j, (*chip, c), sibling) for j, chip in enumerate(chips)]
        for j, chip in enumerate(chips):
            copy(1 + j, (*chip, c), me).wait_recv()
            passed[j].start()
        copy(0, sibling, me).wait_recv()
        for j, chip in enumerate(chips):
            copy(4 + j, (*chip, 1 - c), me).wait_recv()
        for cp in first + passed:
            cp.wait_send()
        total = all_ref[0]
        for d in range(1, N_DEV):
            total = total + all_ref[d]
        out_ref[...] = total

    vmem = pl.BlockSpec(memory_space=pltpu.VMEM)
    return pl.pallas_call(
        body, out_shape=(jax.ShapeDtypeStruct(vals.shape, F32), jax.ShapeDtypeStruct((N_DEV, rows, width), F32)),
        in_specs=[vmem], out_specs=(vmem, vmem),
        scratch_shapes=[pltpu.SemaphoreType.DMA((7,)), pltpu.SemaphoreType.DMA((7,))], name="small_allreduce",
    )(vals)[0]


def adamw(w, g, m, v, name):
    shape = w.shape
    w2, g2, m2, v2 = [a.reshape((-1, shape[-1])) for a in (w, g, m, v)]
    rows, cols = w2.shape
    tr = 256 if rows % 256 == 0 else rows

    def body(w_ref, g_ref, m_ref, v_ref, d_ref, nm_ref, nv_ref):
        gv = g_ref[...]
        nm = ADAM_B1 * m_ref[...] + (1.0 - ADAM_B1) * gv
        nv = ADAM_B2 * v_ref[...] + (1.0 - ADAM_B2) * (gv * gv)
        m_hat = nm / (1.0 - ADAM_B1 ** ADAM_STEP)
        v_hat = nv / (1.0 - ADAM_B2 ** ADAM_STEP)
        d_ref[...] = -ADAM_LR * (m_hat / (jnp.sqrt(v_hat) + ADAM_EPS) + ADAM_WD * w_ref[...])
        nm_ref[...] = nm
        nv_ref[...] = nv

    blk = pl.BlockSpec((tr, cols), lambda i: (i, 0))
    out = jax.ShapeDtypeStruct((rows, cols), F32)
    outs = pl.pallas_call(
        body, grid=(rows // tr,), in_specs=[blk] * 4, out_specs=(blk,) * 3, out_shape=(out,) * 3,
        name=name, compiler_params=_params(1),
    )(w2, g2, m2, v2)
    return tuple(o.reshape(shape) for o in outs)


def adamw_summed(w, m, v, grads, from_sibling, received, me, name):
    rows, cols = w.shape[-2:]
    tr, tc = _tile2(rows, cols)

    def body(me_ref, w_ref, m_ref, v_ref, own_ref, sib_ref, r_ref, g_ref, d_ref, nm_ref, nv_ref):
        gv = own_ref[0] + sib_ref[0]
        for j in range(3):
            gv = gv + r_ref[j].astype(F32)
        nm = ADAM_B1 * m_ref[0] + (1.0 - ADAM_B1) * gv
        nv = ADAM_B2 * v_ref[0] + (1.0 - ADAM_B2) * (gv * gv)
        m_hat = nm / (1.0 - ADAM_B1 ** ADAM_STEP)
        v_hat = nv / (1.0 - ADAM_B2 ** ADAM_STEP)
        g_ref[0] = gv
        d_ref[0] = -ADAM_LR * (m_hat / (jnp.sqrt(v_hat) + ADAM_EPS) + ADAM_WD * w_ref[0])
        nm_ref[0] = nm
        nv_ref[0] = nv

    one = pl.BlockSpec((1, tr, tc), lambda i, j, me_ref: (0, i, j))
    out = jax.ShapeDtypeStruct((1, rows, cols), F32)
    return pl.pallas_call(
        body,
        grid_spec=pltpu.PrefetchScalarGridSpec(
            num_scalar_prefetch=1, grid=(rows // tr, cols // tc),
            in_specs=[one, one, one, pl.BlockSpec((1, tr, tc), lambda i, j, me_ref: (me_ref[0], i, j)),
                      pl.BlockSpec((1, tr, tc), lambda i, j, me_ref: (me_ref[1], i, j)),
                      pl.BlockSpec((3, tr, tc), lambda i, j, me_ref: (0, i, j))],
            out_specs=(one,) * 4),
        out_shape=(out,) * 4, name=name, compiler_params=_params(2),
    )(me, w, m, v, grads, from_sibling, received)


SMALL_VECTORS = ("ffn1_norm", "mix_norm", "ffn2_norm", "final_norm")


def _pack_small(gs):
    row = jnp.concatenate([gs["gdn_a_log"].reshape(-1), gs["gdn_dt_bias"].reshape(-1), gs["gdn_out_norm"].reshape(-1)])
    rows = [gs[n].reshape(1, D_MODEL) for n in SMALL_VECTORS]
    rows.append(jnp.pad(row, (0, D_MODEL - row.shape[0])).reshape(1, D_MODEL))
    rows.append(gs["gdn_conv_w"].reshape(-1, D_MODEL))
    packed = jnp.concatenate(rows, axis=0)
    return jnp.pad(packed, ((0, SMALL_ROWS - packed.shape[0]), (0, 0)))


def _unpack_small(packed):
    out = {n: packed[i].reshape(1, D_MODEL) for i, n in enumerate(SMALL_VECTORS)}
    row = packed[len(SMALL_VECTORS)]
    out["gdn_a_log"] = row[:GDN_HEADS].reshape(1, GDN_HEADS)
    out["gdn_dt_bias"] = row[GDN_HEADS:2 * GDN_HEADS].reshape(1, GDN_HEADS)
    out["gdn_out_norm"] = row[2 * GDN_HEADS:2 * GDN_HEADS + GDN_HEAD_DIM].reshape(1, GDN_HEAD_DIM)
    first = len(SMALL_VECTORS) + 1
    out["gdn_conv_w"] = packed[first:first + GDN_CONV * 3].reshape(GDN_CONV, 3 * GDN_WIDTH)
    return out


WEIGHTS = ("ffn1_norm", "ffn1_w_gate", "ffn1_w_up", "ffn1_w_down", "mix_norm", "w_in", "gdn_conv_w", "gdn_a_log",
           "gdn_dt_bias", "gdn_out_norm", "w_branch_a", "w_branch_b", "w_out", "ffn2_norm", "ffn2_w_gate",
           "ffn2_w_up", "ffn2_w_down", "final_norm")


def kernel(x, ffn1_norm, ffn1_w_gate, ffn1_w_up, ffn1_w_down, mix_norm, w_in, gdn_conv_w, gdn_a_log, gdn_dt_bias, gdn_out_norm, w_branch_a, w_branch_b, w_out, ffn2_norm, ffn2_w_gate, ffn2_w_up, ffn2_w_down, final_norm, loss_target, m_ffn1_norm, m_ffn1_w_gate, m_ffn1_w_up, m_ffn1_w_down, m_mix_norm, m_w_in, m_gdn_conv_w, m_gdn_a_log, m_gdn_dt_bias, m_gdn_out_norm, m_w_branch_a, m_w_branch_b, m_w_out, m_ffn2_norm, m_ffn2_w_gate, m_ffn2_w_up, m_ffn2_w_down, m_final_norm, v_ffn1_norm, v_ffn1_w_gate, v_ffn1_w_up, v_ffn1_w_down, v_mix_norm, v_w_in, v_gdn_conv_w, v_gdn_a_log, v_gdn_dt_bias, v_gdn_out_norm, v_w_branch_a, v_w_branch_b, v_w_out, v_ffn2_norm, v_ffn2_w_gate, v_ffn2_w_up, v_ffn2_w_down, v_final_norm):
    given = dict(locals())
    px, py, pc = _position()
    big_names = list(BIG_WEIGHTS)

    def shard_view(a, n):
        if n == "w_in":
            return a.transpose(2, 0, 1)
        return a.transpose(0, 2, 1) if n in TRANSPOSED else a

    def shard_unview(a, n):
        if n == "w_in":
            return a.transpose(1, 2, 0)
        return a.transpose(0, 2, 1) if n in TRANSPOSED else a

    me = 4 * px + 2 * py + pc
    me_index = me.astype(jnp.int32).reshape(1)
    late = [n for n in big_names if n.startswith("ffn2")]
    early = [n for n in big_names if n not in late]
    shards = {n: shard_view(given[n], n).reshape(given[n].shape[-1 if n in TRANSPOSED else -2], -1).astype(BF16)
              for n in big_names}
    first = [n for n in early if n.startswith("ffn1")]
    middle = [n for n in early if n not in first]
    first_slabs, first_done = all_gather_shards([shards[n] for n in first], "gather_ffn1")
    shards["gdn_conv_w"] = gdn_conv_w[0]
    middle_all = middle + ["gdn_conv_w"]
    middle_gather = direct_exchange_start([_after(shards[n], first_done) for n in middle_all], "near",
                                          "gather_mixer_start")
    ffn1_norm = _after(ffn1_norm, middle_gather[4])
    def in_chunks(slabs):
        return slabs.reshape(-1, FFN_CHUNK, D_MODEL)

    def in_slabs(chunks):
        return chunks.reshape(N_DEV, -1, D_MODEL)

    w = {n: in_chunks(slab) for n, slab in zip(first, first_slabs)}
    x1, ffn1_saved = ffn_forward(x[0], ffn1_norm, w, "ffn1")
    near_lands = direct_exchange_wait(*middle_gather[:4], x1, "near", "gather_mixer_wait")[:-1]
    near_lands = [lax.dynamic_update_slice(land, shards[n][None], (me, 0, 0)) for n, land in zip(middle_all, near_lands)]
    middle_slabs, middle_done = forward_to_sibling(near_lands, "gather_mixer_forward")
    gathered = dict(zip(middle_all, middle_slabs))
    late_gather = direct_exchange_start([_after(shards[n], middle_done) for n in late], "gather", "gather_ffn2_start")
    w["w_in_t"] = gathered["w_in"].reshape(-1, D_MODEL)
    w["w_branch_a"] = gathered["w_branch_a"].transpose(1, 0, 2).reshape(256, D_MODEL)
    w["w_branch_b"] = gathered["w_branch_b"].reshape(D_MODEL, D_MODEL)
    w["w_out"] = gathered["w_out"].reshape(D_MODEL, D_MODEL)
    conv_full = gathered["gdn_conv_w"].transpose(1, 0, 2).reshape(GDN_CONV, 3 * GDN_WIDTH)
    small = dict(mix_norm=_after(mix_norm, late_gather[4]), gdn_a_log=gdn_a_log, gdn_dt_bias=gdn_dt_bias,
                 gdn_out_norm=gdn_out_norm, gdn_conv_w=conv_full)

    x2, mixer_saved = mixer_forward(x1, w, small)
    late_lands = direct_exchange_wait(*late_gather[:4], x2, "gather", "gather_ffn2_wait")
    for n, land in zip(late, late_lands):
        w[n] = in_chunks(lax.dynamic_update_slice(land, shards[n][None], (me, 0, 0)))
    x3, ffn2_saved = ffn_forward(x2, ffn2_norm, w, "ffn2")
    loss_local, dx3, g_final = loss_head(x3, loss_target[0], final_norm.reshape(1, D_MODEL))
    loss = lax.psum(loss_local, ("x", "y", "c"))
    dx2, g_ffn2_norm, (dw2, dw2_f32) = ffn_backward(dx3, ffn2_saved, ffn2_norm, w, "ffn2", with_payload=True)
    late_scatter = direct_exchange_start([in_slabs(g) for g in dw2], "scatter", "rs_ffn2_start")
    w_after = dict(w, w_out=_after(w["w_out"], late_scatter[4]))
    dx1, g_w = mixer_backward(dx2, mixer_saved, w_after, small)
    middle = ["w_in", "w_branch_a", "w_branch_b", "w_out"]
    g_big = dict(w_in=g_w["w_in_t"].reshape(N_DEV, -1, D_MODEL),
                 w_branch_a=g_w["w_branch_a"].reshape(256, N_DEV, 128).transpose(1, 0, 2),
                 w_branch_b=g_w["w_branch_b"].reshape(N_DEV, 128, D_MODEL),
                 w_out=g_w["w_out"].reshape(N_DEV, 128, D_MODEL))
    own = {n: lax.dynamic_index_in_dim(in_slabs(g), me, 0, keepdims=True) for n, g in zip(late, dw2_f32)}
    own.update({n: lax.dynamic_index_in_dim(g_big[n], me, 0, keepdims=True) for n in middle[1:]})
    in_rows = g_w["w_in_t"].shape[0] // N_DEV
    own["w_in"] = lax.dynamic_slice(g_w["w_in_t"], (me * in_rows, 0), (in_rows, D_MODEL))[None]
    middle_scatter = direct_exchange_start([g_big[n].astype(BF16) for n in middle], "scatter", "rs_mixer_start")
    grad_x, g_ffn1_norm, dw1 = ffn_backward(dx1, ffn1_saved, _after(ffn1_norm, middle_scatter[4]), w, "ffn1")
    g_small = dict(ffn1_norm=g_ffn1_norm, ffn2_norm=g_ffn2_norm, final_norm=g_final,
                   **{n: g_w[n] for n in ("mix_norm", "gdn_a_log", "gdn_dt_bias", "gdn_out_norm", "gdn_conv_w")})

    first = [n for n in early if n.startswith("ffn1")]
    g_list = [in_slabs(g) for g in dw1]
    core = pc.astype(jnp.int32).reshape(1)
    me_and_chip = jnp.stack([me, 2 * px + py]).astype(jnp.int32)
    from_sibling = exchange_with_sibling(g_list)
    partials = [add_sibling(g, r, core, "rs_add_" + n) for n, g, r in zip(first, g_list, from_sibling)]
    first_chips = direct_exchange_start(partials, "chips", "rs_ffn1_start")

    def state_of(n):
        return [shard_view(given[p + n], n) for p in ("", "m_", "v_")]

    results = {}
    late_received = direct_exchange_wait(*late_scatter[:4], first_chips[4], "scatter", "rs_ffn2_wait")
    middle_received = direct_exchange_wait(*middle_scatter[:4], first_chips[4], "scatter", "rs_mixer_wait")
    for n, recv in zip(late + middle, list(late_received[:-1]) + list(middle_received[:-1])):
        outs = adamw_direct(*state_of(n), own[n], recv, "adamw_" + n)
        results[n] = tuple(shard_unview(o, n) for o in outs)

    done = results["w_out"][1]
    from_chips = direct_exchange_wait(*first_chips[:4], done, "chips", "rs_ffn1_wait")
    for n, g, sib, recv in zip(first, g_list, from_sibling, from_chips):
        outs = adamw_summed(*state_of(n), g, sib, recv, me_and_chip, "adamw_" + n)
        results[n] = tuple(shard_unview(o, n) for o in outs)

    small_sum = _unpack_small(all_reduce_small(_after(_pack_small(g_small), from_chips[-1])))
    conv_cols = CONV_SHARD[1]
    small_sum["gdn_conv_w"] = lax.dynamic_slice(small_sum["gdn_conv_w"], (0, me * conv_cols), (GDN_CONV, conv_cols))
    for n in WEIGHTS:
        if n not in results:
            g = small_sum[n].reshape(given[n].shape)
            results[n] = (g,) + adamw(given[n], g, given["m_" + n], given["v_" + n], "adamw_" + n)

    outs = [[results[n][i] for n in WEIGHTS] for i in range(4)]
    return (loss, grad_x[None], *outs[0], *outs[1], *outs[2], *outs[3])
```

```python
import jax
import jax.numpy as jnp
from jax import lax
from jax.experimental import pallas as pl
from jax.experimental.pallas import tpu as pltpu

F32 = jnp.float32
BF16 = jnp.bfloat16
HI = lax.Precision.HIGHEST
MESH = pl.DeviceIdType.MESH

N_DEV = 8
D_MODEL = 1024
EPS = 1e-6
ROPE_THETA = 10000.0
DSW_DILATIONS = (1, 4, 16)
DSW_HEADS_PER_GROUP = 4
DSW_HEAD_DIM = 64
DSW_BLOCK = 128
GDN_HEADS = 8
GDN_HEAD_DIM = 128
GDN_WIDTH = 1024
GDN_CONV = 4
GDN_CHUNK = 64

ADAM_LR = 0.001
ADAM_B1 = 0.9
ADAM_B2 = 0.999
ADAM_EPS = 1e-08
ADAM_WD = 0.01
ADAM_STEP = 10

VMEM_LIMIT_BYTES = 56 * 1024 * 1024
LANES = 128

NN = (((1,), (0,)), ((), ()))
NT = (((1,), (1,)), ((), ()))
TN = (((0,), (0,)), ((), ()))


def _params(n_grid):
    return pltpu.CompilerParams(dimension_semantics=("arbitrary",) * n_grid, vmem_limit_bytes=VMEM_LIMIT_BYTES)


def _tile(n, pref):
    best = None
    t = LANES
    while t <= min(n, pref):
        if n % t == 0:
            best = t
        t += LANES
    return n if best is None else best


def _weight_grad(a, g, name):
    n_tokens, m = a.shape
    n = g.shape[1]
    tm, tn = _tile(m, 512), _tile(n, 512)

    def body(a_ref, g_ref, o_ref):
        o_ref[...] = lax.dot_general(a_ref[...].astype(BF16), g_ref[...].astype(BF16), TN, preferred_element_type=F32)

    return pl.pallas_call(
        body, grid=(m // tm, n // tn),
        in_specs=[pl.BlockSpec((n_tokens, tm), lambda i, j: (0, i)), pl.BlockSpec((n_tokens, tn), lambda i, j: (0, j))],
        out_specs=pl.BlockSpec((tm, tn), lambda i, j: (i, j)),
        out_shape=jax.ShapeDtypeStruct((m, n), F32), name=name, compiler_params=_params(2),
    )(a, g)


def _rw_specs(arrs, tm, nblk):
    return [pl.BlockSpec((tm, a.shape[1] // nblk), lambda i, j: (i, j)) for a in arrs]


def _rowwise_fwd(fn, name, rows, consts, params, tm, nblk):
    n_rows = rows[0].shape[0]
    tm = min(tm, n_rows)
    ins = list(rows) + list(consts)
    avals = [jax.ShapeDtypeStruct((tm, a.shape[1] // nblk), a.dtype) for a in ins]
    avals += [jax.ShapeDtypeStruct(p.shape, p.dtype) for p in params]
    out_avals = jax.eval_shape(fn, *avals)
    n_in = len(ins) + len(params)

    def body(*refs):
        outs = fn(*[r[...] for r in refs[:n_in]])
        for r, o in zip(refs[n_in:], outs):
            r[...] = o.astype(r.dtype)

    return pl.pallas_call(
        body, grid=(n_rows // tm, nblk),
        in_specs=_rw_specs(ins, tm, nblk) + [pl.BlockSpec(p.shape, lambda i, j: (0, 0)) for p in params],
        out_specs=tuple(pl.BlockSpec((tm, o.shape[1]), lambda i, j: (i, j)) for o in out_avals),
        out_shape=tuple(jax.ShapeDtypeStruct((n_rows, o.shape[1] * nblk), o.dtype) for o in out_avals),
        name=name, compiler_params=_params(2),
    )(*ins, *params)


def _rowwise_bwd(fn, name, rows, consts, params, cts, tm, nblk):
    n_rows = rows[0].shape[0]
    tm = min(tm, n_rows)
    nr, nc, npar, nct = len(rows), len(consts), len(params), len(cts)

    def body(*refs):
        rv = [r[...] for r in refs[:nr]]
        cv = [r[...] for r in refs[nr:nr + nc]]
        pv = [r[...] for r in refs[nr + nc:nr + nc + npar]]
        ctv = [r[...] for r in refs[nr + nc + npar:nr + nc + npar + nct]]
        outs = refs[nr + nc + npar + nct:]
        _, vjp = jax.vjp(lambda *d: fn(*d[:nr], *cv, *d[nr:]), *rv, *pv)
        grads = vjp(tuple(ctv))
        for k in range(nr):
            outs[k][...] = grads[k]
        first = jnp.logical_and(pl.program_id(0) == 0, pl.program_id(1) == 0)
        for k in range(npar):
            ref = outs[nr + k]

            @pl.when(first)
            def _(ref=ref):
                ref[...] = jnp.zeros_like(ref)

            ref[...] += grads[nr + k]

    ins = list(rows) + list(consts)
    return pl.pallas_call(
        body, grid=(n_rows // tm, nblk),
        in_specs=(_rw_specs(ins, tm, nblk) + [pl.BlockSpec(p.shape, lambda i, j: (0, 0)) for p in params]
                  + _rw_specs(cts, tm, nblk)),
        out_specs=tuple(_rw_specs(rows, tm, nblk) + [pl.BlockSpec(p.shape, lambda i, j: (0, 0)) for p in params]),
        out_shape=tuple([jax.ShapeDtypeStruct(a.shape, F32) for a in rows]
                        + [jax.ShapeDtypeStruct(p.shape, F32) for p in params]),
        name=name, compiler_params=_params(2),
    )(*ins, *params, *cts)


def _merge_fn(ga, gb, pa, pb):
    return (jax.nn.sigmoid(ga) * pa + jax.nn.sigmoid(gb) * pb,)


def _outnorm_gate_fn(o, gate, gain):
    y = o * lax.rsqrt(jnp.mean(o * o, axis=-1, keepdims=True) + EPS) * gain
    return (y * (gate * jax.nn.sigmoid(gate)),)


def _beta_decay_fn(beta_raw, decay_raw, a_log, dt_bias):
    z = decay_raw + dt_bias
    softplus = jnp.maximum(z, 0.0) + jnp.log(1.0 + jnp.exp(-jnp.abs(z)))
    g = -jnp.exp(a_log) * softplus
    rows = g.shape[0]
    ii = lax.broadcasted_iota(jnp.int32, (rows, rows), 0)
    jj = lax.broadcasted_iota(jnp.int32, (rows, rows), 1)
    same_chunk_before = jnp.logical_and(jj <= ii, jj // GDN_CHUNK == ii // GDN_CHUNK).astype(F32)
    gcum = lax.dot_general(same_chunk_before, g, NN, precision=HI, preferred_element_type=F32)
    return jax.nn.sigmoid(beta_raw), gcum


def _combine_fn(o0, o1, o2, l0, l1, l2):
    m = lax.stop_gradient(jnp.maximum(jnp.maximum(l0, l1), l2))
    e0, e1, e2 = jnp.exp(l0 - m), jnp.exp(l1 - m), jnp.exp(l2 - m)
    return ((e0 * o0 + e1 * o1 + e2 * o2) / (e0 + e1 + e2),)


def _loss_fn(x, target, gain):
    y = x * lax.rsqrt(jnp.mean(x * x, axis=-1, keepdims=True) + EPS) * gain
    err = y - target
    return (0.5 * jnp.mean(err * err, axis=-1, keepdims=True),)


def _rotate(v, cos, sin):
    half = DSW_HEAD_DIM // 2
    lane = lax.broadcasted_iota(jnp.int32, cos.shape, 1)
    low = (lane % DSW_HEAD_DIM) < half
    slabs = []
    for s in range(v.shape[1] // LANES):
        x = v[:, s * LANES:(s + 1) * LANES]
        swapped = jnp.where(low, pltpu.roll(x, LANES - half, 1), pltpu.roll(x, half, 1))
        slabs.append(x * cos + swapped * sin)
    return jnp.concatenate(slabs, axis=1)


def _rope_tables(n_tokens):
    half = DSW_HEAD_DIM // 2
    inv_freq = ROPE_THETA ** (-jnp.arange(half, dtype=F32) / half)
    ang = jnp.arange(n_tokens, dtype=F32)[:, None] * inv_freq[None, :]
    cos, sin = jnp.cos(ang), jnp.sin(ang)
    return jnp.tile(jnp.concatenate([cos, cos], 1), (1, 2)), jnp.tile(jnp.concatenate([-sin, sin], 1), (1, 2))


def _attn_probs(q, kp, kc, group, n, n_blocks):
    blk = DSW_BLOCK
    k = _each(lambda a, b: jnp.concatenate([a, b], axis=0).astype(BF16), kp, kc)
    s = _each(lambda a, b: lax.dot_general(a.astype(BF16), b, NT, preferred_element_type=F32)
              * (DSW_HEAD_DIM ** -0.5), q, k)
    per_seq = [n_blocks // d for d in DSW_DILATIONS]
    blocks_per_seq = jnp.where(group == 0, per_seq[0], jnp.where(group == 1, per_seq[1], per_seq[2]))
    first = (n % blocks_per_seq) == 0
    qi = lax.broadcasted_iota(jnp.int32, (blk, 2 * blk), 0)
    kj = lax.broadcasted_iota(jnp.int32, (blk, 2 * blk), 1)
    dist = qi + blk - kj
    valid = (dist >= 0) & (dist <= blk) & jnp.logical_or(kj >= blk, jnp.logical_not(first))
    s = _each(lambda a: jnp.where(valid, a, -1e30), s)
    m = _each(lambda a: jnp.max(a, axis=-1, keepdims=True), s)
    p = _each(lambda a, b: jnp.exp(a - b), s, m)
    l = _each(lambda a: jnp.sum(a, axis=-1, keepdims=True), p)
    return _each(lambda a, b: a / b, p, l), _each(lambda a, b: a + jnp.log(b), m, l), k


GROUP_WIDTH = DSW_HEADS_PER_GROUP * DSW_HEAD_DIM


def _attn_specs(n_tokens):
    blk = DSW_BLOCK
    cur = pl.BlockSpec((1, blk, GROUP_WIDTH), lambda g, n: (g, n, 0))
    prev = pl.BlockSpec((1, blk, GROUP_WIDTH), lambda g, n: (g, jnp.maximum(n - 1, 0), 0))
    return cur, prev


def _heads_of(ref):
    x = ref[0]
    return [x[:, h * DSW_HEAD_DIM:(h + 1) * DSW_HEAD_DIM] for h in range(DSW_HEADS_PER_GROUP)]


def _group_of(heads):
    return jnp.concatenate(heads, axis=1)


def _attn_fwd(q, k, v):
    n_groups, n_tokens, _ = q.shape
    cur, prev = _attn_specs(n_tokens)

    def body(q_ref, kp_ref, kc_ref, vp_ref, vc_ref, o_ref, l_ref):
        p, lse, _ = _attn_probs(_heads_of(q_ref), _heads_of(kp_ref), _heads_of(kc_ref),
                                pl.program_id(0), pl.program_id(1), n_tokens // DSW_BLOCK)
        vv = _each(lambda a, b: jnp.concatenate([a, b], axis=0).astype(BF16), _heads_of(vp_ref), _heads_of(vc_ref))
        o = _each(lambda a, b: lax.dot_general(a.astype(BF16), b, NN, preferred_element_type=F32), p, vv)
        lse_wide = _each(lambda a: jnp.broadcast_to(a, (DSW_BLOCK, DSW_HEAD_DIM)), lse)
        o_ref[0] = _group_of(o)
        l_ref[0] = _group_of(lse_wide)

    return pl.pallas_call(
        body, grid=(n_groups, n_tokens // DSW_BLOCK), in_specs=[cur, prev, cur, prev, cur],
        out_specs=(cur, cur), out_shape=(jax.ShapeDtypeStruct(q.shape, F32), jax.ShapeDtypeStruct(q.shape, F32)),
        name="attn_fwd", compiler_params=_params(2),
    )(q, k, k, v, v)


def _attn_bwd(q, k, v, do, dlse):
    n_groups, n_tokens, _ = q.shape
    nblk = n_tokens // DSW_BLOCK
    cur, prev = _attn_specs(n_tokens)
    part = pl.BlockSpec((1, 1, 2 * DSW_BLOCK, GROUP_WIDTH), lambda g, n: (g, n, 0, 0))
    scale = DSW_HEAD_DIM ** -0.5

    def body(q_ref, kp_ref, kc_ref, vp_ref, vc_ref, do_ref, dl_ref, dq_ref, dk_ref, dv_ref):
        qs = _heads_of(q_ref)
        p, _, kb = _attn_probs(qs, _heads_of(kp_ref), _heads_of(kc_ref), pl.program_id(0), pl.program_id(1), nblk)
        qb = _each(lambda a: a.astype(BF16), qs)
        vv = _each(lambda a, b: jnp.concatenate([a, b], axis=0).astype(BF16), _heads_of(vp_ref), _heads_of(vc_ref))
        dob = _each(lambda a: a.astype(BF16), _heads_of(do_ref))
        dp = _each(lambda a, b: lax.dot_general(a, b, NT, preferred_element_type=F32), dob, vv)
        dv = _each(lambda a, b: lax.dot_general(a.astype(BF16), b, TN, preferred_element_type=F32), p, dob)
        dl = _each(lambda a: jnp.sum(a, axis=-1, keepdims=True), _heads_of(dl_ref))
        ds = _each(lambda a, b, c: (a * (b - jnp.sum(b * a, axis=-1, keepdims=True) + c) * scale).astype(BF16),
                   p, dp, dl)
        dq = _each(lambda a, b: lax.dot_general(a, b, NN, preferred_element_type=F32), ds, kb)
        dk = _each(lambda a, b: lax.dot_general(a, b, TN, preferred_element_type=F32), ds, qb)
        dq_ref[0] = _group_of(dq)
        dk_ref[0, 0] = _group_of(dk)
        dv_ref[0, 0] = _group_of(dv)

    partial_shape = jax.ShapeDtypeStruct((n_groups, nblk, 2 * DSW_BLOCK, GROUP_WIDTH), F32)
    dq, dkp, dvp = pl.pallas_call(
        body, grid=(n_groups, nblk), in_specs=[cur, prev, cur, prev, cur, cur, cur],
        out_specs=(cur, part, part), out_shape=(jax.ShapeDtypeStruct(q.shape, F32), partial_shape, partial_shape),
        name="attn_bwd", compiler_params=_params(2),
    )(q, k, k, v, v, do, dlse)

    def fold(partial):
        own = partial[:, :, DSW_BLOCK:]
        from_next = jnp.pad(partial[:, 1:, :DSW_BLOCK], ((0, 0), (0, 1), (0, 0), (0, 0)))
        return (own + from_next).reshape(n_groups, n_tokens, GROUP_WIDTH)

    return dq, fold(dkp), fold(dvp)


def _to_heads(a):
    n_tokens = a.shape[0]
    outs = []
    for gi, d in enumerate(DSW_DILATIONS):
        blk = a[:, gi * GROUP_WIDTH:(gi + 1) * GROUP_WIDTH].reshape(n_tokens // d, d, GROUP_WIDTH)
        outs.append(blk.transpose(1, 0, 2).reshape(1, n_tokens, GROUP_WIDTH))
    return jnp.concatenate(outs, 0)


def _from_heads(a):
    n_tokens = a.shape[1]
    return [a[gi].reshape(d, n_tokens // d, GROUP_WIDTH).transpose(1, 0, 2).reshape(n_tokens, GROUP_WIDTH)
            for gi, d in enumerate(DSW_DILATIONS)]


CONV_TILE = 512


def _shift_down(x, k, rows):
    return x if k == 0 else jnp.where(rows >= k, pltpu.roll(x, k, 0), 0.0)


def _shift_up(x, k, rows):
    n = x.shape[0]
    return x if k == 0 else jnp.where(rows < n - k, pltpu.roll(x, n - k, 0), 0.0)


def _conv_pre(x, w):
    rows = lax.broadcasted_iota(jnp.int32, x.shape, 0)
    acc = x * w[GDN_CONV - 1:GDN_CONV]
    for k in range(1, GDN_CONV):
        acc = acc + _shift_down(x, k, rows) * w[GDN_CONV - 1 - k:GDN_CONV - k]
    return acc, rows


def _conv_fwd(x, w):
    n_tokens, width = x.shape
    big = pl.BlockSpec((n_tokens, CONV_TILE), lambda j: (0, j))
    wsp = pl.BlockSpec((GDN_CONV, CONV_TILE), lambda j: (0, j))

    def body(x_ref, w_ref, o_ref):
        acc, _ = _conv_pre(x_ref[...], w_ref[...])
        o_ref[...] = acc * jax.nn.sigmoid(acc)

    return pl.pallas_call(
        body, grid=(width // CONV_TILE,), in_specs=[big, wsp], out_specs=big,
        out_shape=jax.ShapeDtypeStruct(x.shape, F32), name="conv_fwd", compiler_params=_params(1),
    )(x, w)


def _conv_bwd(x, w, dy):
    n_tokens, width = x.shape
    big = pl.BlockSpec((n_tokens, CONV_TILE), lambda j: (0, j))
    wsp = pl.BlockSpec((GDN_CONV, CONV_TILE), lambda j: (0, j))

    def body(x_ref, w_ref, dy_ref, dx_ref, dw_ref):
        xv, wv = x_ref[...], w_ref[...]
        acc, rows = _conv_pre(xv, wv)
        sg = jax.nn.sigmoid(acc)
        dacc = dy_ref[...] * (sg + acc * sg * (1.0 - sg))
        dx = dacc * wv[GDN_CONV - 1:GDN_CONV]
        for k in range(1, GDN_CONV):
            dx = dx + _shift_up(dacc, k, rows) * wv[GDN_CONV - 1 - k:GDN_CONV - k]
        dx_ref[...] = dx
        for k in range(GDN_CONV):
            dw_ref[GDN_CONV - 1 - k:GDN_CONV - k, :] = jnp.sum(dacc * _shift_down(xv, k, rows), axis=0, keepdims=True)

    return pl.pallas_call(
        body, grid=(width // CONV_TILE,), in_specs=[big, wsp, big], out_specs=(big, wsp),
        out_shape=(jax.ShapeDtypeStruct(x.shape, F32), jax.ShapeDtypeStruct(w.shape, F32)),
        name="conv_bwd", compiler_params=_params(1),
    )(x, w, dy)


def _dot3(a, b, dn=NN):
    return lax.dot_general(a, b, dn, precision=lax.Precision.HIGH, preferred_element_type=F32)


def _bf16_dot(a, b, dn):
    return lax.dot_general(a.astype(BF16), b.astype(BF16), dn, preferred_element_type=F32)


_DOT_GRADS = {NN: (("g", "b", NT), ("a", "g", TN)), NT: (("g", "b", NN), ("g", "a", TN)),
              TN: (("b", "g", NT), ("a", "g", NN))}


def _make_bdot(dn):
    @jax.custom_vjp
    def op(a, b):
        return _bf16_dot(a, b, dn)

    def fwd(a, b):
        return op(a, b), (a, b)

    def bwd(saved, g):
        vals = dict(a=saved[0], b=saved[1], g=g)
        return tuple(_bf16_dot(vals[x], vals[y], form) for x, y, form in _DOT_GRADS[dn])

    op.defvjp(fwd, bwd)
    return op


_BDOTS = {dn: _make_bdot(dn) for dn in (NN, NT, TN)}


def _bdot(a, b, dn=NN):
    return _BDOTS[dn](a, b)


def _each(fn, *lists):
    return [fn(*items) for items in zip(*lists)]


@jax.custom_vjp
def _known_inverse(m, inverse):
    return inverse


def _known_inverse_fwd(m, inverse):
    return inverse, inverse


def _known_inverse_bwd(inverse, d_inverse):
    return -_dot3(_dot3(inverse, d_inverse, TN), inverse, NT), jnp.zeros_like(inverse)


_known_inverse.defvjp(_known_inverse_fwd, _known_inverse_bwd)


def _gdn_chunks(q, k, v, b, gcum, state, inverse=None):
    c = GDN_CHUNK
    ii = lax.broadcasted_iota(jnp.int32, (c, c), 0)
    jj = lax.broadcasted_iota(jnp.int32, (c, c), 1)
    qn = _each(lambda x: x * lax.rsqrt(jnp.sum(x * x, axis=-1, keepdims=True) + EPS) * (GDN_HEAD_DIM ** -0.5), q)
    kn = _each(lambda x: x * lax.rsqrt(jnp.sum(x * x, axis=-1, keepdims=True) + EPS), k)
    gcum_i = _each(lambda x: jnp.broadcast_to(x, (c, c)), gcum)
    gcum_j = _each(jnp.transpose, gcum_i)
    decay = _each(lambda x, y: jnp.exp(jnp.where(jj <= ii, x - y, -1e30)), gcum_i, gcum_j)
    g_last = _each(lambda x: x[c - 1:c, :], gcum)
    e_gcum = _each(jnp.exp, gcum)
    kbeta = _each(lambda x, y: x * y, kn, b)
    vbeta = _each(lambda x, y: x * y, v, b)
    m = _each(lambda x, y, d: jnp.where(jj < ii, _bdot(x, y, NT) * d, 0.0), kbeta, kn, decay)
    if inverse is not None:
        inv = _each(_known_inverse, m, inverse)
    else:
        eye = (ii == jj).astype(F32)
        inv = _each(lambda x: eye - x, m)
        power = _each(lambda x: _dot3(x, x), m)
        for step in range(5):
            inv = _each(lambda x, p: x + _dot3(x, p), inv, power)
            if step < 4:
                power = _each(lambda p: _dot3(p, p), power)
    u = _each(_dot3, inv, vbeta)
    w = _each(lambda x, y, e: _dot3(x, y * e), inv, kbeta, e_gcum)
    a_qk = _each(lambda x, y, d: _bdot(x, y, NT) * d, qn, kn, decay)
    v_new = _each(lambda x, y, s: x - _bdot(y, s), u, w, state)
    o = _each(lambda x, e, s, a, vn: _bdot(x * e, s) + _bdot(a, vn), qn, e_gcum, state, a_qk, v_new)
    new_state = _each(lambda s, gl, x, gc, vn: s * jnp.exp(gl) + _bdot(x * jnp.exp(gl - gc), vn, TN),
                      state, g_last, kn, gcum, v_new)
    return o, new_state, inv


GDN_HEADS_PER_STEP = 8


GDN_TIME_TILE = 256


def _gdn_specs(n_tokens, reverse):
    hb, hd, tt = GDN_HEADS_PER_STEP, GDN_HEAD_DIM, GDN_TIME_TILE
    nb, nt = GDN_HEADS // hb, n_tokens // tt

    def when(t):
        return nt - 1 - t if reverse else t

    q = pl.BlockSpec((tt, hb * hd), lambda h, t: (when(t), h))
    k = pl.BlockSpec((tt, hb * hd), lambda h, t: (when(t), nb + h))
    v = pl.BlockSpec((tt, hb * hd), lambda h, t: (when(t), 2 * nb + h))
    vec = pl.BlockSpec((tt, hb), lambda h, t: (when(t), h))
    states = pl.BlockSpec((hb, tt // GDN_CHUNK, hd, hd), lambda h, t: (h, when(t), 0, 0))
    inverses = pl.BlockSpec((hb, tt // GDN_CHUNK, GDN_CHUNK, GDN_CHUNK), lambda h, t: (h, when(t), 0, 0))
    return q, k, v, vec, states, inverses


def _gdn_fwd(qkv, beta, g):
    n_tokens = qkv.shape[0]
    hb, hd, tt = GDN_HEADS_PER_STEP, GDN_HEAD_DIM, GDN_TIME_TILE
    n_chunks = tt // GDN_CHUNK
    q_s, k_s, v_s, vec, st, inv_s = _gdn_specs(n_tokens, False)

    def body(q_ref, k_ref, v_ref, b_ref, g_ref, o_ref, st_ref, inv_ref, state):
        @pl.when(pl.program_id(1) == 0)
        def _():
            state[...] = jnp.zeros_like(state)

        def step(c, carry):
            r = pl.ds(pl.multiple_of(c * GDN_CHUNK, GDN_CHUNK), GDN_CHUNK)
            cols = [slice(h * hd, (h + 1) * hd) for h in range(hb)]
            old = [state[h] for h in range(hb)]
            o, new, inv = _gdn_chunks(
                [q_ref[r, cs] for cs in cols], [k_ref[r, cs] for cs in cols], [v_ref[r, cs] for cs in cols],
                [b_ref[r, h:h + 1] for h in range(hb)], [g_ref[r, h:h + 1] for h in range(hb)], old)
            for h in range(hb):
                st_ref[h, c] = old[h]
                inv_ref[h, c] = inv[h]
                o_ref[r, cols[h]] = o[h]
                state[h] = new[h]
            return carry

        lax.fori_loop(0, n_chunks, step, 0)

    n_all = n_tokens // GDN_CHUNK
    return pl.pallas_call(
        body, grid=(GDN_HEADS // hb, n_tokens // tt), in_specs=[q_s, k_s, v_s, vec, vec], out_specs=(q_s, st, inv_s),
        out_shape=(jax.ShapeDtypeStruct((n_tokens, GDN_WIDTH), F32),
                   jax.ShapeDtypeStruct((GDN_HEADS, n_all, hd, hd), F32),
                   jax.ShapeDtypeStruct((GDN_HEADS, n_all, GDN_CHUNK, GDN_CHUNK), F32)),
        scratch_shapes=[pltpu.VMEM((hb, hd, hd), F32)],
        name="gdn_fwd", compiler_params=_params(2),
    )(qkv, qkv, qkv, beta, g)


def _gdn_bwd(qkv, beta, g, states, inverses, do):
    n_tokens = qkv.shape[0]
    hb, hd, tt = GDN_HEADS_PER_STEP, GDN_HEAD_DIM, GDN_TIME_TILE
    n_chunks = tt // GDN_CHUNK
    q_s, k_s, v_s, vec, st, inv_s = _gdn_specs(n_tokens, True)

    assert hb == GDN_HEADS

    def body(q_ref, k_ref, v_ref, b_ref, g_ref, st_ref, inv_ref, do_ref, dqkv_ref, db_ref, dg_ref, dstate):
        @pl.when(pl.program_id(1) == 0)
        def _():
            dstate[...] = jnp.zeros_like(dstate)

        def step(i, carry):
            c = n_chunks - 1 - i
            r = pl.ds(pl.multiple_of(c * GDN_CHUNK, GDN_CHUNK), GDN_CHUNK)
            cols = [slice(h * hd, (h + 1) * hd) for h in range(hb)]
            args = ([q_ref[r, cs] for cs in cols], [k_ref[r, cs] for cs in cols], [v_ref[r, cs] for cs in cols],
                    [b_ref[r, h:h + 1] for h in range(hb)], [g_ref[r, h:h + 1] for h in range(hb)],
                    [st_ref[h, c] for h in range(hb)])
            saved = [inv_ref[h, c] for h in range(hb)]
            cts = ([do_ref[r, cs] for cs in cols], [dstate[h] for h in range(hb)])
            dq, dk, dv, db, dg, dst = jax.vjp(lambda *a: _gdn_chunks(*a, inverse=saved)[:2], *args)[1](cts)
            for h in range(hb):
                for part, grad in enumerate((dq, dk, dv)):
                    dqkv_ref[r, pl.ds(part * GDN_WIDTH + h * hd, hd)] = grad[h]
                db_ref[r, h:h + 1] = db[h]
                dg_ref[r, h:h + 1] = dg[h]
                dstate[h] = dst[h]
            return carry

        lax.fori_loop(0, n_chunks, step, 0)

    n_t = n_tokens // tt
    thin = jax.ShapeDtypeStruct(beta.shape, F32)
    return pl.pallas_call(
        body, grid=(GDN_HEADS // hb, n_t), in_specs=[q_s, k_s, v_s, vec, vec, st, inv_s, q_s],
        out_specs=(pl.BlockSpec((tt, 3 * GDN_WIDTH), lambda h, t: (n_t - 1 - t, 0)), vec, vec),
        out_shape=(jax.ShapeDtypeStruct(qkv.shape, F32), thin, thin),
        scratch_shapes=[pltpu.VMEM((hb, hd, hd), F32)],
        name="gdn_bwd", compiler_params=_params(2),
    )(qkv, qkv, qkv, beta, g, states, inverses, do)


FFN_ROW_TILE = 256
FFN_CHUNK = 256
FFN_FWD_ROW_TILE = 512


def _resident(shape):
    return pl.BlockSpec(shape, lambda i: (0,) * len(shape), pipeline_mode=pl.Buffered(1))


def _ffn_fwd(x, gain, wg, wu, wd, name):
    n_tokens, d = x.shape
    n_shards, n, _ = wg.shape
    tm = FFN_FWD_ROW_TILE

    def body(x_ref, gain_ref, wg_ref, wu_ref, wd_ref, o_ref, g_ref, u_ref):
        xv = x_ref[...]
        h = (xv * lax.rsqrt(jnp.mean(xv * xv, axis=-1, keepdims=True) + EPS) * gain_ref[...]).astype(BF16)
        acc = jnp.zeros((tm, d), F32)
        for j in range(n_shards):
            g = lax.dot_general(h, wg_ref[j], NT, preferred_element_type=F32)
            u = lax.dot_general(h, wu_ref[j], NT, preferred_element_type=F32)
            g_ref[j] = g
            u_ref[j] = u
            a = (g * jax.nn.sigmoid(g) * u).astype(BF16)
            acc = acc + lax.dot_general(a, wd_ref[j], NN, preferred_element_type=F32)
        o_ref[...] = xv + 0.5 * acc

    row = pl.BlockSpec((tm, d), lambda i: (i, 0))
    hid = pl.BlockSpec((n_shards, tm, n), lambda i: (0, i, 0))
    return pl.pallas_call(
        body, grid=(n_tokens // tm,),
        in_specs=[row, _resident(gain.shape), _resident(wg.shape), _resident(wu.shape), _resident(wd.shape)],
        out_specs=(row, hid, hid),
        out_shape=(jax.ShapeDtypeStruct(x.shape, F32), jax.ShapeDtypeStruct((n_shards, n_tokens, n), F32),
                   jax.ShapeDtypeStruct((n_shards, n_tokens, n), F32)),
        name=name, compiler_params=_params(1),
    )(x, gain, wg, wu, wd)


def _ffn_bwd_rows(x, gain, dy, g, u, wg, wu, wd, name):
    n_tokens, d = x.shape
    n_shards, n, _ = wg.shape
    tm = FFN_ROW_TILE

    def body(x_ref, gain_ref, dy_ref, g_ref, u_ref, wg_ref, wu_ref, wd_ref,
             dx_ref, dgain_ref, h_ref, dyh_ref, a_ref, dg_ref, du_ref):
        xv, dyv, gain_v = x_ref[...], dy_ref[...], gain_ref[...]
        r = lax.rsqrt(jnp.mean(xv * xv, axis=-1, keepdims=True) + EPS)
        xhat = xv * r
        h_ref[...] = (xhat * gain_v).astype(BF16)
        dyh = (0.5 * dyv).astype(BF16)
        dyh_ref[...] = dyh
        dh = jnp.zeros((tm, d), F32)
        for j in range(n_shards):
            da = lax.dot_general(dyh, wd_ref[j], NT, preferred_element_type=F32)
            gv, uv = g_ref[j], u_ref[j]
            sg = jax.nn.sigmoid(gv)
            silu = gv * sg
            a_ref[j] = (silu * uv).astype(BF16)
            dg = (da * uv * (sg + silu * (1.0 - sg))).astype(BF16)
            du = (da * silu).astype(BF16)
            dg_ref[j] = dg
            du_ref[j] = du
            dh = dh + lax.dot_general(dg, wg_ref[j], NN, preferred_element_type=F32)
            dh = dh + lax.dot_general(du, wu_ref[j], NN, preferred_element_type=F32)
        dxhat = dh * gain_v
        dx_ref[...] = dyv + r * (dxhat - xhat * jnp.mean(dxhat * xhat, axis=-1, keepdims=True))

        @pl.when(pl.program_id(0) == 0)
        def _():
            dgain_ref[...] = jnp.zeros_like(dgain_ref)

        dgain_ref[...] += jnp.sum(dh * xhat, axis=0, keepdims=True)

    row = pl.BlockSpec((tm, d), lambda i: (i, 0))
    hid = pl.BlockSpec((n_shards, tm, n), lambda i: (0, i, 0))
    hid_shape = (n_shards, n_tokens, n)
    return pl.pallas_call(
        body, grid=(n_tokens // tm,),
        in_specs=[row, _resident(gain.shape), row, hid, hid, _resident(wg.shape), _resident(wu.shape),
                  _resident(wd.shape)],
        out_specs=(row, pl.BlockSpec(gain.shape, lambda i: (0, 0)), row, row, hid, hid, hid),
        out_shape=(jax.ShapeDtypeStruct(x.shape, F32), jax.ShapeDtypeStruct(gain.shape, F32),
                   jax.ShapeDtypeStruct(x.shape, BF16), jax.ShapeDtypeStruct(x.shape, BF16),
                   jax.ShapeDtypeStruct(hid_shape, BF16), jax.ShapeDtypeStruct(hid_shape, BF16),
                   jax.ShapeDtypeStruct(hid_shape, BF16)),
        name=name, compiler_params=_params(1),
    )(x, gain, dy, g, u, wg, wu, wd)


def _ffn_bwd_weights(h, dyh, a, dg, du, name, with_payload=False):
    n_chunks, n_tokens, n = a.shape
    d = h.shape[1]

    def body(h_ref, dyh_ref, a_ref, dg_ref, du_ref, *out_refs):
        hv = h_ref[...]
        vals = (lax.dot_general(dg_ref[0], hv, TN, preferred_element_type=F32),
                lax.dot_general(du_ref[0], hv, TN, preferred_element_type=F32),
                lax.dot_general(a_ref[0], dyh_ref[...], TN, preferred_element_type=F32))
        for ref, val in zip(out_refs[-3:], vals):
            ref[0] = val
        if with_payload:
            for ref, val in zip(out_refs[:3], vals):
                ref[0] = val.astype(BF16)

    hid = pl.BlockSpec((1, n_tokens, n), lambda j: (j, 0, 0))
    out = pl.BlockSpec((1, n, d), lambda j: (j, 0, 0))
    shapes = (jax.ShapeDtypeStruct((n_chunks, n, d), F32),) * 3
    if with_payload:
        shapes = (jax.ShapeDtypeStruct((n_chunks, n, d), BF16),) * 3 + shapes
    outs = pl.pallas_call(
        body, grid=(n_chunks,), in_specs=[_resident(h.shape), _resident(dyh.shape), hid, hid, hid],
        out_specs=(out,) * len(shapes), out_shape=shapes, name=name, compiler_params=_params(1),
    )(h, dyh, a, dg, du)
    return (outs[:3], outs[3:]) if with_payload else outs


IN_PIECES = (("wq_a", 0, 768), ("wk_a", 768, 1536), ("wv_a", 1536, 2304), ("w_qkvb", 2304, 5376),
             ("w_small", 5376, 5392), ("w_ggate", 5392, 6416), ("w_gatea", 6416, 7440), ("w_gateb", 7440, 8464))
IN_NAMES = tuple(name for name, _, _ in IN_PIECES)


def _in_rows(lo, hi):
    return lo, max(hi, lo + LANES)


N_ROTATED = 2


def _in_proj_fwd(x, gain, wt, cos, sin):
    n_tokens, d = x.shape
    tm = FFN_ROW_TILE
    rows = [_in_rows(lo, hi) for _, lo, hi in IN_PIECES]

    def body(x_ref, gain_ref, wt_ref, cos_ref, sin_ref, *o_refs):
        xv = x_ref[...]
        h = (xv * lax.rsqrt(jnp.mean(xv * xv, axis=-1, keepdims=True) + EPS) * gain_ref[...]).astype(BF16)
        for k, ((lo, hi), o_ref) in enumerate(zip(rows, o_refs)):
            z = lax.dot_general(h, wt_ref[lo:hi, :], NT, preferred_element_type=F32)
            o_ref[...] = _rotate(z, cos_ref[...], sin_ref[...]) if k < N_ROTATED else z

    tab = pl.BlockSpec((tm, LANES), lambda i: (i, 0))
    return pl.pallas_call(
        body, grid=(n_tokens // tm,),
        in_specs=[pl.BlockSpec((tm, d), lambda i: (i, 0)), _resident(gain.shape), _resident(wt.shape), tab, tab],
        out_specs=tuple(pl.BlockSpec((tm, hi - lo), lambda i: (i, 0)) for lo, hi in rows),
        out_shape=tuple(jax.ShapeDtypeStruct((n_tokens, hi - lo), F32) for lo, hi in rows),
        name="in_proj_fwd", compiler_params=_params(1),
    )(x, gain, wt, cos, sin)


def _in_proj_bwd_rows(x, gain, dres, dzs, wt, cos, sin):
    n_tokens, d = x.shape
    tm = FFN_ROW_TILE
    n = len(dzs)
    rows = [_in_rows(lo, hi) for _, lo, hi in IN_PIECES]

    def body(x_ref, gain_ref, dres_ref, cos_ref, sin_ref, *refs):
        dz_refs, wt_ref = refs[:n], refs[n]
        dx_ref, dgain_ref, h_ref = refs[n + 1:n + 4]
        unrotated_refs = refs[n + 4:]
        xv, gain_v = x_ref[...], gain_ref[...]
        r = lax.rsqrt(jnp.mean(xv * xv, axis=-1, keepdims=True) + EPS)
        xhat = xv * r
        h_ref[...] = (xhat * gain_v).astype(BF16)
        dh = jnp.zeros((tm, d), F32)
        for k, (dz_ref, (lo, hi)) in enumerate(zip(dz_refs, rows)):
            dz = dz_ref[...]
            if k < N_ROTATED:
                dz = _rotate(dz, cos_ref[...], -sin_ref[...]).astype(BF16)
                unrotated_refs[k][...] = dz
            dh = dh + lax.dot_general(dz.astype(BF16), wt_ref[lo:hi, :], NN, preferred_element_type=F32)
        dxhat = dh * gain_v
        dx_ref[...] = dres_ref[...] + r * (dxhat - xhat * jnp.mean(dxhat * xhat, axis=-1, keepdims=True))

        @pl.when(pl.program_id(0) == 0)
        def _():
            dgain_ref[...] = jnp.zeros_like(dgain_ref)

        dgain_ref[...] += jnp.sum(dh * xhat, axis=0, keepdims=True)

    row = pl.BlockSpec((tm, d), lambda i: (i, 0))
    tab = pl.BlockSpec((tm, LANES), lambda i: (i, 0))
    dz_specs = [pl.BlockSpec((tm, dz.shape[1]), lambda i: (i, 0)) for dz in dzs]
    outs = pl.pallas_call(
        body, grid=(n_tokens // tm,),
        in_specs=[row, _resident(gain.shape), row, tab, tab] + dz_specs + [_resident(wt.shape)],
        out_specs=(row, pl.BlockSpec(gain.shape, lambda i: (0, 0)), row) + tuple(dz_specs[:N_ROTATED]),
        out_shape=(jax.ShapeDtypeStruct(x.shape, F32), jax.ShapeDtypeStruct(gain.shape, F32),
                   jax.ShapeDtypeStruct(x.shape, BF16))
        + tuple(jax.ShapeDtypeStruct(dz.shape, BF16) for dz in dzs[:N_ROTATED]),
        name="in_proj_bwd_rows", compiler_params=_params(1),
    )(x, gain, dres, cos, sin, *dzs, wt)
    return outs[0], outs[1], outs[2], outs[3:]


def _in_proj_bwd_weight(dwt, h, dz, lo, hi, name):
    n_tokens, d = h.shape
    width = hi - lo
    tn = _tile(width, 512) if width >= LANES else width
    dz_tile = max(tn, LANES)

    def body(dwt_ref, h_ref, dz_ref, o_ref):
        o_ref[...] = lax.dot_general(dz_ref[:, :tn].astype(BF16), h_ref[...], TN, preferred_element_type=F32)

    return pl.pallas_call(
        body, grid=(width // tn,),
        in_specs=[ANY, _resident(h.shape), pl.BlockSpec((n_tokens, dz_tile), lambda j: (0, j))],
        out_specs=pl.BlockSpec((pl.Element(tn), pl.Element(d)), lambda j: (pl.multiple_of(lo + j * tn, 16), 0)),
        out_shape=jax.ShapeDtypeStruct(dwt.shape, F32), input_output_aliases={0: 0}, name=name,
        compiler_params=_params(1),
    )(dwt, h, dz)


def _split_small(z):
    return z[:, :GDN_HEADS], z[:, GDN_HEADS:2 * GDN_HEADS]


def _heads3(q, k, v):
    return _to_heads(q), _to_heads(k), _to_heads(v)


def _tokens6(o, lse):
    return tuple(_from_heads(o)) + tuple(_from_heads(lse))


def _blocks_of(vals, nblk):
    return [[v[:, b * (v.shape[1] // nblk):(b + 1) * (v.shape[1] // nblk)] for v in vals] for b in range(nblk)]


def _rowwise_matmul_fwd(fn, name, rows, params, wt, nblk, res=None):
    n_rows = rows[0].shape[0]
    tm = FFN_ROW_TILE
    k, n = wt.shape
    nr, npar = len(rows), len(params)

    def body(*refs):
        row_vals = [r[...] for r in refs[:nr]]
        par_vals = [r[...] for r in refs[nr:nr + npar]]
        wt_ref = refs[nr + npar]
        o_ref, y_ref = refs[-2:]
        y = jnp.concatenate([fn(*blk, *par_vals)[0] for blk in _blocks_of(row_vals, nblk)], axis=1).astype(BF16)
        y_ref[...] = y
        acc = lax.dot_general(y, wt_ref[...], NN, preferred_element_type=F32)
        o_ref[...] = acc if res is None else refs[nr + npar + 1][...] + acc

    row_specs = [pl.BlockSpec((tm, a.shape[1]), lambda i: (i, 0)) for a in rows]
    ins = list(rows) + list(params) + [wt] + ([] if res is None else [res])
    specs = row_specs + [_resident(p.shape) for p in params] + [_resident(wt.shape)]
    if res is not None:
        specs.append(pl.BlockSpec((tm, n), lambda i: (i, 0)))
    return pl.pallas_call(
        body, grid=(n_rows // tm,), in_specs=specs,
        out_specs=(pl.BlockSpec((tm, n), lambda i: (i, 0)), pl.BlockSpec((tm, k), lambda i: (i, 0))),
        out_shape=(jax.ShapeDtypeStruct((n_rows, n), F32), jax.ShapeDtypeStruct((n_rows, k), BF16)),
        name=name, compiler_params=_params(1),
    )(*ins)


def _rowwise_matmul_bwd(fn, name, rows, params, wt, dout, nblk):
    n_rows = rows[0].shape[0]
    tm = FFN_ROW_TILE
    nr, npar = len(rows), len(params)

    def body(*refs):
        row_vals = [r[...] for r in refs[:nr]]
        par_vals = [r[...] for r in refs[nr:nr + npar]]
        wt_ref, dout_ref = refs[nr + npar], refs[nr + npar + 1]
        outs = refs[nr + npar + 2:]
        dy = lax.dot_general(dout_ref[...].astype(BF16), wt_ref[...], NT, preferred_element_type=F32)
        grads = [jax.vjp(fn, *blk, *par_vals)[1]((dy_blk,))
                 for blk, (dy_blk,) in zip(_blocks_of(row_vals, nblk), _blocks_of([dy], nblk))]
        for j in range(nr):
            outs[j][...] = jnp.concatenate([g[j] for g in grads], axis=1)
        for j in range(npar):
            ref = outs[nr + j]

            @pl.when(pl.program_id(0) == 0)
            def _(ref=ref):
                ref[...] = jnp.zeros_like(ref)

            for g in grads:
                ref[...] += g[nr + j]

    row_specs = [pl.BlockSpec((tm, a.shape[1]), lambda i: (i, 0)) for a in rows]
    par_specs = [_resident(p.shape) for p in params]
    return pl.pallas_call(
        body, grid=(n_rows // tm,),
        in_specs=row_specs + par_specs + [_resident(wt.shape), pl.BlockSpec((tm, dout.shape[1]), lambda i: (i, 0))],
        out_specs=tuple(row_specs + [pl.BlockSpec(p.shape, lambda i: (0, 0)) for p in params]),
        out_shape=tuple([jax.ShapeDtypeStruct(a.shape, F32) for a in rows]
                        + [jax.ShapeDtypeStruct(p.shape, F32) for p in params]),
        name=name, compiler_params=_params(1),
    )(*rows, *params, wt, dout)


def mixer_forward(x1, w, small):
    n_tokens = x1.shape[0]
    cos, sin = _rope_tables(n_tokens)
    proj = dict(zip(IN_NAMES, _in_proj_fwd(x1, small["mix_norm"], w["w_in_t"], cos, sin)))
    (qh, kh, vh), heads_vjp = jax.vjp(_heads3, proj["wq_a"], proj["wk_a"], proj["wv_a"])
    o, lse = _attn_fwd(qh, kh, vh)
    per_group, tokens_vjp = jax.vjp(_tokens6, o, lse)
    pa, ya = _rowwise_matmul_fwd(_combine_fn, "branch_a", per_group, (), w["w_branch_a"], 1)
    qkv = _conv_fwd(proj["w_qkvb"], small["gdn_conv_w"])
    raw, small_vjp = jax.vjp(_split_small, proj["w_small"])
    gdn_params = (small["gdn_a_log"], small["gdn_dt_bias"])
    beta, gcum = _rowwise_fwd(_beta_decay_fn, "beta_decay", raw, (), gdn_params, 512, 1)
    ob, *states = _gdn_fwd(qkv, beta, gcum)
    gate_in = (ob, proj["w_ggate"])
    pb, yb = _rowwise_matmul_fwd(_outnorm_gate_fn, "branch_b", gate_in, (small["gdn_out_norm"],), w["w_branch_b"],
                                 GDN_HEADS)
    merge_in = (proj["w_gatea"], proj["w_gateb"], pa, pb)
    x2, merged = _rowwise_matmul_fwd(_merge_fn, "out", merge_in, (), w["w_out"], 1, res=x1)
    saved = dict(x1=x1, proj=proj, cos=cos, sin=sin, heads_vjp=heads_vjp, heads=(qh, kh, vh), tokens_vjp=tokens_vjp,
                 per_group=per_group, ya=ya, qkv=qkv, raw=raw, small_vjp=small_vjp, beta=beta, gcum=gcum, states=states,
                 gate_in=gate_in, yb=yb, merge_in=merge_in, merged=merged)
    return x2, saved


def mixer_backward(dx2, s, w, small):
    proj = s["proj"]
    grads = dict(w_out=_weight_grad(s["merged"], dx2, "out_dw"))
    dgate_a, dgate_b, dpa, dpb = _rowwise_matmul_bwd(_merge_fn, "out_bwd", s["merge_in"], (), w["w_out"], dx2, 1)
    grads["w_branch_b"] = _weight_grad(s["yb"], dpb, "branch_b_dw")
    grads["w_branch_a"] = _weight_grad(s["ya"], dpa, "branch_a_dw")
    dob, dggate, grads["gdn_out_norm"] = _rowwise_matmul_bwd(
        _outnorm_gate_fn, "branch_b_bwd", s["gate_in"], (small["gdn_out_norm"],), w["w_branch_b"], dpb, GDN_HEADS)
    dqkv, dbeta, dgcum = _gdn_bwd(s["qkv"], s["beta"], s["gcum"], *s["states"], dob)
    gdn_params = (small["gdn_a_log"], small["gdn_dt_bias"])
    dbeta_raw, ddecay_raw, grads["gdn_a_log"], grads["gdn_dt_bias"] = _rowwise_bwd(
        _beta_decay_fn, "beta_decay_bwd", s["raw"], (), gdn_params, (dbeta, dgcum), 512, 1)
    dsmall = s["small_vjp"]((dbeta_raw, ddecay_raw))[0]
    dqkvb, grads["gdn_conv_w"] = _conv_bwd(proj["w_qkvb"], small["gdn_conv_w"], dqkv)
    dper_group = _rowwise_matmul_bwd(_combine_fn, "branch_a_bwd", s["per_group"], (), w["w_branch_a"], dpa, 1)
    do, dlse = s["tokens_vjp"](tuple(dper_group))
    dqh, dkh, dvh = _attn_bwd(*s["heads"], do, dlse)
    dq_rot, dk_rot, dv = s["heads_vjp"]((dqh, dkh, dvh))
    dzs = (dq_rot, dk_rot, dv, dqkvb, dsmall, dggate, dgate_a, dgate_b)
    dx1, grads["mix_norm"], h, unrotated = _in_proj_bwd_rows(
        s["x1"], small["mix_norm"], dx2, dzs, w["w_in_t"], s["cos"], s["sin"])
    dzs = tuple(unrotated) + dzs[N_ROTATED:]
    dwt = lax.empty(w["w_in_t"].shape, F32)
    for (name, lo, hi), dz in zip(IN_PIECES, dzs):
        dwt = _in_proj_bwd_weight(dwt, h, dz, lo, hi, "in_proj_dw_" + name)
    grads["w_in_t"] = dwt
    return dx1, grads


def ffn_forward(x, gain, w, tag):
    out, g, u = _ffn_fwd(x, gain, w[tag + "_w_gate"], w[tag + "_w_up"], w[tag + "_w_down"], tag + "_fwd")
    return out, (x, g, u)


def ffn_backward(dy, saved, gain, w, tag, with_payload=False):
    x, g, u = saved
    weights = (w[tag + "_w_gate"], w[tag + "_w_up"], w[tag + "_w_down"])
    dx, dgain, h, dyh, a, dg, du = _ffn_bwd_rows(x, gain, dy, g, u, *weights, tag + "_bwd_rows")
    return dx, dgain, _ffn_bwd_weights(h, dyh, a, dg, du, tag + "_bwd_weights", with_payload)


def loss_head(x3, target, gain):
    n_tokens, d = x3.shape
    tm = FFN_ROW_TILE

    def body(x_ref, t_ref, gain_ref, loss_ref, dx_ref, dgain_ref):
        target_v = t_ref[...]
        (row_loss,), vjp = jax.vjp(lambda xv, gv: _loss_fn(xv, target_v, gv), x_ref[...], gain_ref[...])
        dx, dgain = vjp((jnp.ones_like(row_loss),))
        loss_ref[...] = row_loss
        dx_ref[...] = dx

        @pl.when(pl.program_id(0) == 0)
        def _():
            dgain_ref[...] = jnp.zeros_like(dgain_ref)

        dgain_ref[...] += dgain

    row = pl.BlockSpec((tm, d), lambda i: (i, 0))
    row_loss, dx3, dgain = pl.pallas_call(
        body, grid=(n_tokens // tm,), in_specs=[row, row, _resident(gain.shape)],
        out_specs=(pl.BlockSpec((tm, 1), lambda i: (i, 0)), row, pl.BlockSpec(gain.shape, lambda i: (0, 0))),
        out_shape=(jax.ShapeDtypeStruct((n_tokens, 1), F32), jax.ShapeDtypeStruct(x3.shape, F32),
                   jax.ShapeDtypeStruct(gain.shape, F32)),
        name="loss_head", compiler_params=_params(1),
    )(x3, target, gain)
    return jnp.sum(row_loss), dx3, dgain


BIG_WEIGHTS = ("ffn1_w_gate", "ffn1_w_up", "ffn1_w_down", "w_in", "w_branch_a", "w_branch_b", "w_out",
               "ffn2_w_gate", "ffn2_w_up", "ffn2_w_down")
TRANSPOSED = ("ffn1_w_gate", "ffn1_w_up", "w_in", "ffn2_w_gate", "ffn2_w_up")
CONV_SHARD = (GDN_CONV, 3 * GDN_WIDTH // N_DEV)
SMALL_ROWS = 24
ANY = pl.BlockSpec(memory_space=pl.ANY)


TOKEN = jax.ShapeDtypeStruct((8, LANES), F32)


def _after(value, token):
    return value + token[0, 0].astype(value.dtype)


def _position():
    return lax.axis_index("x"), lax.axis_index("y"), lax.axis_index("c")


def all_gather_shards(shards, name):
    n = len(shards)
    per = 8

    def body(*refs):
        x_refs, out_refs = refs[:n], refs[n:2 * n]
        send_sems, recv_sems, local_sems = refs[2 * n + 1:]
        x, y, c = _position()
        me, sibling = (x, y, c), (x, y, 1 - c)
        x_chip, y_chip, far_chip = (1 - x, y), (x, 1 - y), (1 - x, 1 - y)

        def slab(a, px, py, pc):
            return out_refs[a].at[4 * px + 2 * py + pc]

        def copy(a, k, src, dst, to):
            return pltpu.make_async_remote_copy(
                src_ref=src, dst_ref=dst, send_sem=send_sems.at[per * a + k], recv_sem=recv_sems.at[per * a + k],
                device_id=to, device_id_type=MESH)

        def whole(a, k, block, to, src=None):
            return copy(a, k, slab(a, *block) if src is None else src, slab(a, *block), to)

        def half(a, k, block, which, to):
            rows = shards[a].shape[0] // 2
            part = slab(a, *block).at[pl.ds(which * rows, rows)]
            return copy(a, k, part, part, to)

        arrays = range(n)
        mine = [pltpu.make_async_copy(x_refs[a], slab(a, *me), local_sems.at[a]) for a in arrays]
        for cp in mine:
            cp.start()
        started = [whole(a, 1, me, (*x_chip, c), src=x_refs[a]) for a in arrays]
        started += [whole(a, 2, me, (*y_chip, c), src=x_refs[a]) for a in arrays]
        started += [whole(a, 0, me, sibling, src=x_refs[a]) for a in arrays]
        for cp in started:
            cp.start()

        def start(cp):
            cp.start()
            started.append(cp)

        for a in arrays:
            whole(a, 1, (*x_chip, c), me).wait_recv()
            start(half(a, 3, (*x_chip, c), 0, (*y_chip, c)))
            start(whole(a, 5, (*x_chip, c), sibling))
        for a in arrays:
            whole(a, 2, (*y_chip, c), me).wait_recv()
            start(half(a, 4, (*y_chip, c), 1, (*x_chip, c)))
            start(whole(a, 6, (*y_chip, c), sibling))
        for a in arrays:
            half(a, 3, (*far_chip, c), 0, me).wait_recv()
            half(a, 4, (*far_chip, c), 1, me).wait_recv()
            start(whole(a, 7, (*far_chip, c), sibling))
        for a in arrays:
            whole(a, 0, sibling, me).wait_recv()
            for k, chip in ((5, x_chip), (6, y_chip), (7, far_chip)):
                whole(a, k, (*chip, 1 - c), me).wait_recv()
        for cp in started:
            cp.wait_send()
        for cp in mine:
            cp.wait()
        refs[2 * n][...] = jnp.zeros_like(refs[2 * n])

    outs = pl.pallas_call(
        body, out_shape=tuple(jax.ShapeDtypeStruct((N_DEV,) + s.shape, s.dtype) for s in shards) + (TOKEN,),
        in_specs=[ANY] * n, out_specs=(ANY,) * n + (pl.BlockSpec(memory_space=pltpu.VMEM),),
        scratch_shapes=[pltpu.SemaphoreType.DMA((per * n,)), pltpu.SemaphoreType.DMA((per * n,)),
                        pltpu.SemaphoreType.DMA((n,))],
        name=name,
    )(*shards)
    return outs[:n], outs[n]


def exchange_with_sibling(grads):
    n = len(grads)

    def body(*refs):
        g_refs, recv_refs = refs[:n], refs[n:2 * n]
        send_sems, recv_sems = refs[2 * n:]
        x, y, c = _position()
        copies = [pltpu.make_async_remote_copy(
            src_ref=g_refs[a].at[2 * k + 1 - c], dst_ref=recv_refs[a].at[k], send_sem=send_sems.at[4 * a + k],
            recv_sem=recv_sems.at[4 * a + k], device_id=(x, y, 1 - c), device_id_type=MESH)
            for k in range(4) for a in range(n)]
        for cp in copies:
            cp.start()
        for cp in copies:
            cp.wait()

    return pl.pallas_call(
        body, out_shape=tuple(jax.ShapeDtypeStruct((4,) + g.shape[1:], g.dtype) for g in grads),
        in_specs=[ANY] * n, out_specs=(ANY,) * n,
        scratch_shapes=[pltpu.SemaphoreType.DMA((4 * n,)), pltpu.SemaphoreType.DMA((4 * n,))], name="rs_sibling",
    )(*grads)


ELEMENTWISE_TILE_BYTES = 1536 * 1024


def _tile2(rows, cols):
    if rows % 256 == 0:
        return 256, cols
    if rows * cols * 4 > ELEMENTWISE_TILE_BYTES and cols % 256 == 0:
        return rows, 256
    return rows, cols


def add_sibling(grads, received, core, name):
    _, rows, width = grads.shape
    tr, tc = _tile2(rows, width)

    def body(c_ref, g_ref, r_ref, o_ref):
        o_ref[...] = (g_ref[...] + r_ref[...]).astype(BF16)

    blk = (1, tr, tc)
    return pl.pallas_call(
        body,
        grid_spec=pltpu.PrefetchScalarGridSpec(
            num_scalar_prefetch=1, grid=(4, rows // tr, width // tc),
            in_specs=[pl.BlockSpec(blk, lambda k, i, j, c_ref: (2 * k + c_ref[0], i, j)),
                      pl.BlockSpec(blk, lambda k, i, j, c_ref: (k, i, j))],
            out_specs=pl.BlockSpec(blk, lambda k, i, j, c_ref: (k, i, j))),
        out_shape=jax.ShapeDtypeStruct((4, rows, width), BF16), name=name, compiler_params=_params(3),
    )(core, grads, received)


HBM = pl.BlockSpec(memory_space=pltpu.HBM)
SEM = pl.BlockSpec(memory_space=pltpu.SEMAPHORE)
DATAFLOW_EFFECT = pltpu.SideEffectType.DATAFLOW_SIDE_EFFECTING
N_PEERS = N_DEV - 1


def _peer(mask):
    x, y, c = _position()
    px = 1 - x if mask & 4 else x
    py = 1 - y if mask & 2 else y
    pc = 1 - c if mask & 1 else c
    return (px, py, pc), 4 * px + 2 * py + pc


ALL_PEERS = tuple(range(1, N_DEV))
OTHER_CHIPS = (4, 2, 6)


SIBLING = 1
GATHER_MODES = ("gather", "near")


def _exchange_peers(mode):
    return {"chips": OTHER_CHIPS, "near": (SIBLING,) + OTHER_CHIPS}.get(mode, ALL_PEERS)


def _direct_copies(src_refs, land_refs, send_sems, recv_sems, mode):
    x, y, c = _position()
    me = 4 * x + 2 * y + c
    masks = _exchange_peers(mode)
    copies = []
    for a, (src, land) in enumerate(zip(src_refs, land_refs)):
        for slot, mask in enumerate(masks):
            peer, peer_index = _peer(mask)
            k = len(masks) * a + slot
            if mode in GATHER_MODES:
                source, dest = src, land.at[me]
            elif mode == "scatter":
                source, dest = src.at[peer_index], land.at[slot]
            else:
                source, dest = src.at[2 * peer[0] + peer[1]], land.at[slot]
            copies.append(pltpu.make_async_remote_copy(
                src_ref=source, dst_ref=dest, send_sem=send_sems.at[k], recv_sem=recv_sems.at[k], device_id=peer,
                device_id_type=MESH))
    return copies


def forward_to_sibling(slabs, name):
    n = len(slabs)

    def body(*refs):
        out_refs = refs[n:2 * n]
        send_sems, recv_sems = refs[2 * n + 1:]
        x, y, c = _position()
        copies = []
        for a in range(n):
            for slot, mask in enumerate(OTHER_CHIPS):
                _, held = _peer(mask)
                copies.append(pltpu.make_async_remote_copy(
                    src_ref=out_refs[a].at[held], dst_ref=out_refs[a].at[held], send_sem=send_sems.at[3 * a + slot],
                    recv_sem=recv_sems.at[3 * a + slot], device_id=(x, y, 1 - c), device_id_type=MESH))
        for cp in copies:
            cp.start()
        for cp in copies:
            cp.wait()
        refs[2 * n][...] = jnp.zeros_like(refs[2 * n])

    outs = pl.pallas_call(
        body, out_shape=tuple(jax.ShapeDtypeStruct(s.shape, s.dtype) for s in slabs) + (TOKEN,),
        in_specs=[ANY] * n, out_specs=(ANY,) * n + (pl.BlockSpec(memory_space=pltpu.VMEM),),
        input_output_aliases={i: i for i in range(n)},
        scratch_shapes=[pltpu.SemaphoreType.DMA((3 * n,)), pltpu.SemaphoreType.DMA((3 * n,))], name=name,
    )(*slabs)
    return outs[:n], outs[n]


def direct_exchange_start(arrays, mode, name):
    n = len(arrays)
    n_peers = len(_exchange_peers(mode))
    lands = [lax.empty((N_DEV,) + a.shape if mode in GATHER_MODES else (n_peers,) + a.shape[1:], a.dtype)
             for a in arrays]

    def body(*refs):
        src_refs, land_refs = refs[:n], refs[n:2 * n]
        send_sems, recv_sems = refs[2 * n], refs[2 * n + 1]
        token = refs[-1]
        for cp in _direct_copies(src_refs, land_refs, send_sems, recv_sems, mode):
            cp.start()
        token[...] = jnp.zeros_like(token)

    sems = pltpu.SemaphoreType.DMA((n_peers * n,))
    outs = pl.pallas_call(
        body, name=name,
        out_shape=(sems, sems) + tuple(pltpu.HBM(a.shape, a.dtype) for a in arrays)
        + tuple(pltpu.HBM(l.shape, l.dtype) for l in lands) + (TOKEN,),
        in_specs=[HBM] * (2 * n), out_specs=(SEM, SEM) + (HBM,) * (2 * n) + (pl.BlockSpec(memory_space=pltpu.VMEM),),
        input_output_aliases={i: 2 + i for i in range(2 * n)},
        compiler_params=pltpu.CompilerParams(has_side_effects=DATAFLOW_EFFECT),
    )(*[pltpu.with_memory_space_constraint(a, pltpu.HBM) for a in list(arrays) + lands])
    return outs[0], outs[1], outs[2:2 + n], outs[2 + n:2 + 2 * n], outs[-1]


def direct_exchange_wait(send_sems, recv_sems, arrays, lands, after, mode, name):
    n = len(arrays)

    def body(*refs):
        src_refs, land_refs = refs[:n], refs[n:2 * n]
        send_sems, recv_sems = refs[2 * n], refs[2 * n + 1]
        for cp in _direct_copies(src_refs, land_refs, send_sems, recv_sems, mode):
            cp.wait_send()
            cp.wait_recv()
        refs[-1][...] = jnp.zeros_like(refs[-1])

    outs = pl.pallas_call(
        body, name=name,
        out_shape=tuple(pltpu.HBM(a.shape, a.dtype) for a in arrays) + tuple(pltpu.HBM(l.shape, l.dtype) for l in lands)
        + (TOKEN,),
        in_specs=[HBM] * (2 * n) + [SEM, SEM, pl.BlockSpec(memory_space=pl.ANY)],
        out_specs=(HBM,) * (2 * n) + (pl.BlockSpec(memory_space=pltpu.VMEM),),
        input_output_aliases={i: i for i in range(2 * n)},
        compiler_params=pltpu.CompilerParams(has_side_effects=DATAFLOW_EFFECT),
    )(*arrays, *lands, send_sems, recv_sems, after)
    return outs[n:]


def adamw_direct(w, m, v, own, received, name):
    row_per_tile = w.shape[0] != 1
    rows, cols = (w.shape[0], w.shape[2]) if row_per_tile else w.shape[-2:]
    tr, tc = _tile2(rows, cols)

    def body(w_ref, m_ref, v_ref, own_ref, r_ref, g_ref, d_ref, nm_ref, nv_ref):
        gv = own_ref[0]
        for j in range(N_PEERS):
            gv = gv + r_ref[j].astype(F32)
        nm = ADAM_B1 * m_ref[...] + (1.0 - ADAM_B1) * gv
        nv = ADAM_B2 * v_ref[...] + (1.0 - ADAM_B2) * (gv * gv)
        m_hat = nm / (1.0 - ADAM_B1 ** ADAM_STEP)
        v_hat = nv / (1.0 - ADAM_B2 ** ADAM_STEP)
        g_ref[...] = gv
        d_ref[...] = -ADAM_LR * (m_hat / (jnp.sqrt(v_hat) + ADAM_EPS) + ADAM_WD * w_ref[...])
        nm_ref[...] = nm
        nv_ref[...] = nv

    if row_per_tile:
        one = pl.BlockSpec((tr, None, tc), lambda i, j: (i, 0, j))
    else:
        one = pl.BlockSpec((None, tr, tc), lambda i, j: (0, i, j))
    out = jax.ShapeDtypeStruct(w.shape, F32)
    return pl.pallas_call(
        body, grid=(rows // tr, cols // tc),
        in_specs=[one, one, one, pl.BlockSpec((1, tr, tc), lambda i, j: (0, i, j)),
                  pl.BlockSpec((N_PEERS, tr, tc), lambda i, j: (0, i, j))],
        out_specs=(one,) * 4, out_shape=(out,) * 4, name=name, compiler_params=_params(2),
    )(w, m, v, own, received)


def all_reduce_small(vals):
    rows, width = vals.shape

    def body(x_ref, out_ref, all_ref, send_sems, recv_sems):
        x, y, c = _position()
        me, sibling = (x, y, c), (x, y, 1 - c)
        chips = [(1 - x, y), (x, 1 - y), (1 - x, 1 - y)]

        def slab(px, py, pc):
            return all_ref.at[4 * px + 2 * py + pc]

        def copy(k, block, to, src=None):
            return pltpu.make_async_remote_copy(
                src_ref=slab(*block) if src is None else src, dst_ref=slab(*block),
                send_sem=send_sems.at[k], recv_sem=recv_sems.at[k], device_id=to, device_id_type=MESH)

        first = [copy(0, me, sibling, src=x_ref)]
        first += [copy(1 + j, me, (*chip, c), src=x_ref) for j, chip in enumerate(chips)]
        for cp in first:
            cp.start()
        all_ref[4 * x + 2 * y + c] = x_ref[...]
        passed = [copy(4 + j, (*chip, c), sibling) for j, chip in enumerate(chips)]
        for j, chip in enumerate(chips):
            copy(1 + j, (*chip, c), me).wait_recv()
            passed[j].start()
        copy(0, sibling, me).wait_recv()
        for j, chip in enumerate(chips):
            copy(4 + j, (*chip, 1 - c), me).wait_recv()
        for cp in first + passed:
            cp.wait_send()
        total = all_ref[0]
        for d in range(1, N_DEV):
            total = total + all_ref[d]
        out_ref[...] = total

    vmem = pl.BlockSpec(memory_space=pltpu.VMEM)
    return pl.pallas_call(
        body, out_shape=(jax.ShapeDtypeStruct(vals.shape, F32), jax.ShapeDtypeStruct((N_DEV, rows, width), F32)),
        in_specs=[vmem], out_specs=(vmem, vmem),
        scratch_shapes=[pltpu.SemaphoreType.DMA((7,)), pltpu.SemaphoreType.DMA((7,))], name="small_allreduce",
    )(vals)[0]


def adamw(w, g, m, v, name):
    shape = w.shape
    w2, g2, m2, v2 = [a.reshape((-1, shape[-1])) for a in (w, g, m, v)]
    rows, cols = w2.shape
    tr = 256 if rows % 256 == 0 else rows

    def body(w_ref, g_ref, m_ref, v_ref, d_ref, nm_ref, nv_ref):
        gv = g_ref[...]
        nm = ADAM_B1 * m_ref[...] + (1.0 - ADAM_B1) * gv
        nv = ADAM_B2 * v_ref[...] + (1.0 - ADAM_B2) * (gv * gv)
        m_hat = nm / (1.0 - ADAM_B1 ** ADAM_STEP)
        v_hat = nv / (1.0 - ADAM_B2 ** ADAM_STEP)
        d_ref[...] = -ADAM_LR * (m_hat / (jnp.sqrt(v_hat) + ADAM_EPS) + ADAM_WD * w_ref[...])
        nm_ref[...] = nm
        nv_ref[...] = nv

    blk = pl.BlockSpec((tr, cols), lambda i: (i, 0))
    out = jax.ShapeDtypeStruct((rows, cols), F32)
    outs = pl.pallas_call(
        body, grid=(rows // tr,), in_specs=[blk] * 4, out_specs=(blk,) * 3, out_shape=(out,) * 3,
        name=name, compiler_params=_params(1),
    )(w2, g2, m2, v2)
    return tuple(o.reshape(shape) for o in outs)


def adamw_summed(w, m, v, grads, from_sibling, received, me, name):
    rows, cols = w.shape[-2:]
    tr, tc = _tile2(rows, cols)

    def body(me_ref, w_ref, m_ref, v_ref, own_ref, sib_ref, r_ref, g_ref, d_ref, nm_ref, nv_ref):
        gv = own_ref[0] + sib_ref[0]
        for j in range(3):
            gv = gv + r_ref[j].astype(F32)
        nm = ADAM_B1 * m_ref[0] + (1.0 - ADAM_B1) * gv
        nv = ADAM_B2 * v_ref[0] + (1.0 - ADAM_B2) * (gv * gv)
        m_hat = nm / (1.0 - ADAM_B1 ** ADAM_STEP)
        v_hat = nv / (1.0 - ADAM_B2 ** ADAM_STEP)
        g_ref[0] = gv
        d_ref[0] = -ADAM_LR * (m_hat / (jnp.sqrt(v_hat) + ADAM_EPS) + ADAM_WD * w_ref[0])
        nm_ref[0] = nm
        nv_ref[0] = nv

    one = pl.BlockSpec((1, tr, tc), lambda i, j, me_ref: (0, i, j))
    out = jax.ShapeDtypeStruct((1, rows, cols), F32)
    return pl.pallas_call(
        body,
        grid_spec=pltpu.PrefetchScalarGridSpec(
            num_scalar_prefetch=1, grid=(rows // tr, cols // tc),
            in_specs=[one, one, one, pl.BlockSpec((1, tr, tc), lambda i, j, me_ref: (me_ref[0], i, j)),
                      pl.BlockSpec((1, tr, tc), lambda i, j, me_ref: (me_ref[1], i, j)),
                      pl.BlockSpec((3, tr, tc), lambda i, j, me_ref: (0, i, j))],
            out_specs=(one,) * 4),
        out_shape=(out,) * 4, name=name, compiler_params=_params(2),
    )(me, w, m, v, grads, from_sibling, received)


SMALL_VECTORS = ("ffn1_norm", "mix_norm", "ffn2_norm", "final_norm")


def _pack_small(gs):
    row = jnp.concatenate([gs["gdn_a_log"].reshape(-1), gs["gdn_dt_bias"].reshape(-1), gs["gdn_out_norm"].reshape(-1)])
    rows = [gs[n].reshape(1, D_MODEL) for n in SMALL_VECTORS]
    rows.append(jnp.pad(row, (0, D_MODEL - row.shape[0])).reshape(1, D_MODEL))
    rows.append(gs["gdn_conv_w"].reshape(-1, D_MODEL))
    packed = jnp.concatenate(rows, axis=0)
    return jnp.pad(packed, ((0, SMALL_ROWS - packed.shape[0]), (0, 0)))


def _unpack_small(packed):
    out = {n: packed[i].reshape(1, D_MODEL) for i, n in enumerate(SMALL_VECTORS)}
    row = packed[len(SMALL_VECTORS)]
    out["gdn_a_log"] = row[:GDN_HEADS].reshape(1, GDN_HEADS)
    out["gdn_dt_bias"] = row[GDN_HEADS:2 * GDN_HEADS].reshape(1, GDN_HEADS)
    out["gdn_out_norm"] = row[2 * GDN_HEADS:2 * GDN_HEADS + GDN_HEAD_DIM].reshape(1, GDN_HEAD_DIM)
    first = len(SMALL_VECTORS) + 1
    out["gdn_conv_w"] = packed[first:first + GDN_CONV * 3].reshape(GDN_CONV, 3 * GDN_WIDTH)
    return out


WEIGHTS = ("ffn1_norm", "ffn1_w_gate", "ffn1_w_up", "ffn1_w_down", "mix_norm", "w_in", "gdn_conv_w", "gdn_a_log",
           "gdn_dt_bias", "gdn_out_norm", "w_branch_a", "w_branch_b", "w_out", "ffn2_norm", "ffn2_w_gate",
           "ffn2_w_up", "ffn2_w_down", "final_norm")


def kernel(x, ffn1_norm, ffn1_w_gate, ffn1_w_up, ffn1_w_down, mix_norm, w_in, gdn_conv_w, gdn_a_log, gdn_dt_bias, gdn_out_norm, w_branch_a, w_branch_b, w_out, ffn2_norm, ffn2_w_gate, ffn2_w_up, ffn2_w_down, final_norm, loss_target, m_ffn1_norm, m_ffn1_w_gate, m_ffn1_w_up, m_ffn1_w_down, m_mix_norm, m_w_in, m_gdn_conv_w, m_gdn_a_log, m_gdn_dt_bias, m_gdn_out_norm, m_w_branch_a, m_w_branch_b, m_w_out, m_ffn2_norm, m_ffn2_w_gate, m_ffn2_w_up, m_ffn2_w_down, m_final_norm, v_ffn1_norm, v_ffn1_w_gate, v_ffn1_w_up, v_ffn1_w_down, v_mix_norm, v_w_in, v_gdn_conv_w, v_gdn_a_log, v_gdn_dt_bias, v_gdn_out_norm, v_w_branch_a, v_w_branch_b, v_w_out, v_ffn2_norm, v_ffn2_w_gate, v_ffn2_w_up, v_ffn2_w_down, v_final_norm):
    given = dict(locals())
    px, py, pc = _position()
    big_names = list(BIG_WEIGHTS)

    def shard_view(a, n):
        if n == "w_in":
            return a.transpose(2, 0, 1)
        return a.transpose(0, 2, 1) if n in TRANSPOSED else a

    def shard_unview(a, n):
        if n == "w_in":
            return a.transpose(1, 2, 0)
        return a.transpose(0, 2, 1) if n in TRANSPOSED else a

    me = 4 * px + 2 * py + pc
    me_index = me.astype(jnp.int32).reshape(1)
    late = [n for n in big_names if n.startswith("ffn2")]
    early = [n for n in big_names if n not in late]
    shards = {n: shard_view(given[n], n).reshape(given[n].shape[-1 if n in TRANSPOSED else -2], -1).astype(BF16)
              for n in big_names}
    first = [n for n in early if n.startswith("ffn1")]
    middle = [n for n in early if n not in first]
    first_slabs, first_done = all_gather_shards([shards[n] for n in first], "gather_ffn1")
    shards["gdn_conv_w"] = gdn_conv_w[0]
    middle_all = middle + ["gdn_conv_w"]
    middle_gather = direct_exchange_start([_after(shards[n], first_done) for n in middle_all], "near",
                                          "gather_mixer_start")
    ffn1_norm = _after(ffn1_norm, middle_gather[4])
    def in_chunks(slabs):
        return slabs.reshape(-1, FFN_CHUNK, D_MODEL)

    def in_slabs(chunks):
        return chunks.reshape(N_DEV, -1, D_MODEL)

    w = {n: in_chunks(slab) for n, slab in zip(first, first_slabs)}
    x1, ffn1_saved = ffn_forward(x[0], ffn1_norm, w, "ffn1")
    near_lands = direct_exchange_wait(*middle_gather[:4], x1, "near", "gather_mixer_wait")[:-1]
    near_lands = [lax.dynamic_update_slice(land, shards[n][None], (me, 0, 0)) for n, land in zip(middle_all, near_lands)]
    middle_slabs, middle_done = forward_to_sibling(near_lands, "gather_mixer_forward")
    gathered = dict(zip(middle_all, middle_slabs))
    late_gather = direct_exchange_start([_after(shards[n], middle_done) for n in late], "gather", "gather_ffn2_start")
    w["w_in_t"] = gathered["w_in"].reshape(-1, D_MODEL)
    w["w_branch_a"] = gathered["w_branch_a"].transpose(1, 0, 2).reshape(-1, D_MODEL)
    w["w_branch_b"] = gathered["w_branch_b"].reshape(D_MODEL, D_MODEL)
    w["w_out"] = gathered["w_out"].reshape(D_MODEL, D_MODEL)
    conv_full = gathered["gdn_conv_w"].transpose(1, 0, 2).reshape(GDN_CONV, 3 * GDN_WIDTH)
    small = dict(mix_norm=_after(mix_norm, late_gather[4]), gdn_a_log=gdn_a_log, gdn_dt_bias=gdn_dt_bias,
                 gdn_out_norm=gdn_out_norm, gdn_conv_w=conv_full)

    x2, mixer_saved = mixer_forward(x1, w, small)
    late_lands = direct_exchange_wait(*late_gather[:4], x2, "gather", "gather_ffn2_wait")
    for n, land in zip(late, late_lands):
        w[n] = in_chunks(lax.dynamic_update_slice(land, shards[n][None], (me, 0, 0)))
    x3, ffn2_saved = ffn_forward(x2, ffn2_norm, w, "ffn2")
    loss_local, dx3, g_final = loss_head(x3, loss_target[0], final_norm.reshape(1, D_MODEL))
    loss = lax.psum(loss_local, ("x", "y", "c"))
    dx2, g_ffn2_norm, (dw2, dw2_f32) = ffn_backward(dx3, ffn2_saved, ffn2_norm, w, "ffn2", with_payload=True)
    late_scatter = direct_exchange_start([in_slabs(g) for g in dw2], "scatter", "rs_ffn2_start")
    w_after = dict(w, w_out=_after(w["w_out"], late_scatter[4]))
    dx1, g_w = mixer_backward(dx2, mixer_saved, w_after, small)
    middle = ["w_in", "w_branch_a", "w_branch_b", "w_out"]
    g_big = dict(w_in=g_w["w_in_t"].reshape(N_DEV, -1, D_MODEL),
                 w_branch_a=g_w["w_branch_a"].reshape(-1, N_DEV, D_MODEL // N_DEV).transpose(1, 0, 2),
                 w_branch_b=g_w["w_branch_b"].reshape(N_DEV, -1, D_MODEL),
                 w_out=g_w["w_out"].reshape(N_DEV, -1, D_MODEL))
    own = {n: lax.dynamic_index_in_dim(in_slabs(g), me, 0, keepdims=True) for n, g in zip(late, dw2_f32)}
    own.update({n: lax.dynamic_index_in_dim(g_big[n], me, 0, keepdims=True) for n in middle[1:]})
    in_rows = g_w["w_in_t"].shape[0] // N_DEV
    own["w_in"] = lax.dynamic_slice(g_w["w_in_t"], (me * in_rows, 0), (in_rows, D_MODEL))[None]
    middle_scatter = direct_exchange_start([g_big[n].astype(BF16) for n in middle], "scatter", "rs_mixer_start")
    grad_x, g_ffn1_norm, dw1 = ffn_backward(dx1, ffn1_saved, _after(ffn1_norm, middle_scatter[4]), w, "ffn1")
    g_small = dict(ffn1_norm=g_ffn1_norm, ffn2_norm=g_ffn2_norm, final_norm=g_final,
                   **{n: g_w[n] for n in ("mix_norm", "gdn_a_log", "gdn_dt_bias", "gdn_out_norm", "gdn_conv_w")})

    first = [n for n in early if n.startswith("ffn1")]
    g_list = [in_slabs(g) for g in dw1]
    core = pc.astype(jnp.int32).reshape(1)
    me_and_chip = jnp.stack([me, 2 * px + py]).astype(jnp.int32)
    from_sibling = exchange_with_sibling(g_list)
    partials = [add_sibling(g, r, core, "rs_add_" + n) for n, g, r in zip(first, g_list, from_sibling)]
    first_chips = direct_exchange_start(partials, "chips", "rs_ffn1_start")

    def state_of(n):
        return [shard_view(given[p + n], n) for p in ("", "m_", "v_")]

    results = {}
    late_received = direct_exchange_wait(*late_scatter[:4], first_chips[4], "scatter", "rs_ffn2_wait")
    middle_received = direct_exchange_wait(*middle_scatter[:4], first_chips[4], "scatter", "rs_mixer_wait")
    for n, recv in zip(late + middle, list(late_received[:-1]) + list(middle_received[:-1])):
        outs = adamw_direct(*state_of(n), own[n], recv, "adamw_" + n)
        results[n] = tuple(shard_unview(o, n) for o in outs)

    done = results["w_out"][1]
    from_chips = direct_exchange_wait(*first_chips[:4], done, "chips", "rs_ffn1_wait")
    for n, g, sib, recv in zip(first, g_list, from_sibling, from_chips):
        outs = adamw_summed(*state_of(n), g, sib, recv, me_and_chip, "adamw_" + n)
        results[n] = tuple(shard_unview(o, n) for o in outs)

    small_sum = _unpack_small(all_reduce_small(_after(_pack_small(g_small), from_chips[-1])))
    conv_cols = CONV_SHARD[1]
    small_sum["gdn_conv_w"] = lax.dynamic_slice(small_sum["gdn_conv_w"], (0, me * conv_cols), (GDN_CONV, conv_cols))
    for n in WEIGHTS:
        if n not in results:
            g = small_sum[n].reshape(given[n].shape)
            results[n] = (g,) + adamw(given[n], g, given["m_" + n], given["v_" + n], "adamw_" + n)

    outs = [[results[n][i] for n in WEIGHTS] for i in range(4)]
    return (loss, grad_x[None], *outs[0], *outs[1], *outs[2], *outs[3])
```

```python
import jax
import jax.numpy as jnp
from jax import lax
from jax.experimental import pallas as pl
from jax.experimental.pallas import tpu as pltpu

F32 = jnp.float32
BF16 = jnp.bfloat16
HI = lax.Precision.HIGHEST
MESH = pl.DeviceIdType.MESH

N_DEV = 8
D_MODEL = 1024
EPS = 1e-6
ROPE_THETA = 10000.0
DSW_DILATIONS = (1, 4, 16)
DSW_HEADS_PER_GROUP = 4
DSW_HEAD_DIM = 64
DSW_BLOCK = 128
GDN_HEADS = 8
GDN_HEAD_DIM = 128
GDN_WIDTH = 1024
GDN_CONV = 4
GDN_CHUNK = 64

ADAM_LR = 0.001
ADAM_B1 = 0.9
ADAM_B2 = 0.999
ADAM_EPS = 1e-08
ADAM_WD = 0.01
ADAM_STEP = 10

VMEM_LIMIT_BYTES = 56 * 1024 * 1024
LANES = 128

NN = (((1,), (0,)), ((), ()))
NT = (((1,), (1,)), ((), ()))
TN = (((0,), (0,)), ((), ()))


def _params(n_grid):
    return pltpu.CompilerParams(dimension_semantics=("arbitrary",) * n_grid, vmem_limit_bytes=VMEM_LIMIT_BYTES)


def _tile(n, pref):
    best = None
    t = LANES
    while t <= min(n, pref):
        if n % t == 0:
            best = t
        t += LANES
    return n if best is None else best


def _weight_grad(a, g, name):
    n_tokens, m = a.shape
    n = g.shape[1]
    tm, tn = _tile(m, 512), _tile(n, 512)

    def body(a_ref, g_ref, o_ref):
        o_ref[...] = lax.dot_general(a_ref[...].astype(BF16), g_ref[...].astype(BF16), TN, preferred_element_type=F32)

    return pl.pallas_call(
        body, grid=(m // tm, n // tn),
        in_specs=[pl.BlockSpec((n_tokens, tm), lambda i, j: (0, i)), pl.BlockSpec((n_tokens, tn), lambda i, j: (0, j))],
        out_specs=pl.BlockSpec((tm, tn), lambda i, j: (i, j)),
        out_shape=jax.ShapeDtypeStruct((m, n), F32), name=name, compiler_params=_params(2),
    )(a, g)


def _rw_specs(arrs, tm, nblk):
    return [pl.BlockSpec((tm, a.shape[1] // nblk), lambda i, j: (i, j)) for a in arrs]


def _rowwise_fwd(fn, name, rows, consts, params, tm, nblk):
    n_rows = rows[0].shape[0]
    tm = min(tm, n_rows)
    ins = list(rows) + list(consts)
    avals = [jax.ShapeDtypeStruct((tm, a.shape[1] // nblk), a.dtype) for a in ins]
    avals += [jax.ShapeDtypeStruct(p.shape, p.dtype) for p in params]
    out_avals = jax.eval_shape(fn, *avals)
    n_in = len(ins) + len(params)

    def body(*refs):
        outs = fn(*[r[...] for r in refs[:n_in]])
        for r, o in zip(refs[n_in:], outs):
            r[...] = o.astype(r.dtype)

    return pl.pallas_call(
        body, grid=(n_rows // tm, nblk),
        in_specs=_rw_specs(ins, tm, nblk) + [pl.BlockSpec(p.shape, lambda i, j: (0, 0)) for p in params],
        out_specs=tuple(pl.BlockSpec((tm, o.shape[1]), lambda i, j: (i, j)) for o in out_avals),
        out_shape=tuple(jax.ShapeDtypeStruct((n_rows, o.shape[1] * nblk), o.dtype) for o in out_avals),
        name=name, compiler_params=_params(2),
    )(*ins, *params)


def _rowwise_bwd(fn, name, rows, consts, params, cts, tm, nblk):
    n_rows = rows[0].shape[0]
    tm = min(tm, n_rows)
    nr, nc, npar, nct = len(rows), len(consts), len(params), len(cts)

    def body(*refs):
        rv = [r[...] for r in refs[:nr]]
        cv = [r[...] for r in refs[nr:nr + nc]]
        pv = [r[...] for r in refs[nr + nc:nr + nc + npar]]
        ctv = [r[...] for r in refs[nr + nc + npar:nr + nc + npar + nct]]
        outs = refs[nr + nc + npar + nct:]
        _, vjp = jax.vjp(lambda *d: fn(*d[:nr], *cv, *d[nr:]), *rv, *pv)
        grads = vjp(tuple(ctv))
        for k in range(nr):
            outs[k][...] = grads[k]
        first = jnp.logical_and(pl.program_id(0) == 0, pl.program_id(1) == 0)
        for k in range(npar):
            ref = outs[nr + k]

            @pl.when(first)
            def _(ref=ref):
                ref[...] = jnp.zeros_like(ref)

            ref[...] += grads[nr + k]

    ins = list(rows) + list(consts)
    return pl.pallas_call(
        body, grid=(n_rows // tm, nblk),
        in_specs=(_rw_specs(ins, tm, nblk) + [pl.BlockSpec(p.shape, lambda i, j: (0, 0)) for p in params]
                  + _rw_specs(cts, tm, nblk)),
        out_specs=tuple(_rw_specs(rows, tm, nblk) + [pl.BlockSpec(p.shape, lambda i, j: (0, 0)) for p in params]),
        out_shape=tuple([jax.ShapeDtypeStruct(a.shape, F32) for a in rows]
                        + [jax.ShapeDtypeStruct(p.shape, F32) for p in params]),
        name=name, compiler_params=_params(2),
    )(*ins, *params, *cts)


def _merge_fn(ga, gb, pa, pb):
    return (jax.nn.sigmoid(ga) * pa + jax.nn.sigmoid(gb) * pb,)


def _outnorm_gate_fn(o, gate, gain):
    y = o * lax.rsqrt(jnp.mean(o * o, axis=-1, keepdims=True) + EPS) * gain
    return (y * (gate * jax.nn.sigmoid(gate)),)


def _beta_decay_fn(beta_raw, decay_raw, a_log, dt_bias):
    z = decay_raw + dt_bias
    softplus = jnp.maximum(z, 0.0) + jnp.log(1.0 + jnp.exp(-jnp.abs(z)))
    g = -jnp.exp(a_log) * softplus
    rows = g.shape[0]
    ii = lax.broadcasted_iota(jnp.int32, (rows, rows), 0)
    jj = lax.broadcasted_iota(jnp.int32, (rows, rows), 1)
    same_chunk_before = jnp.logical_and(jj <= ii, jj // GDN_CHUNK == ii // GDN_CHUNK).astype(F32)
    gcum = lax.dot_general(same_chunk_before, g, NN, precision=HI, preferred_element_type=F32)
    return jax.nn.sigmoid(beta_raw), gcum


def _combine_fn(o0, o1, o2, l0, l1, l2):
    m = lax.stop_gradient(jnp.maximum(jnp.maximum(l0, l1), l2))
    e0, e1, e2 = jnp.exp(l0 - m), jnp.exp(l1 - m), jnp.exp(l2 - m)
    return ((e0 * o0 + e1 * o1 + e2 * o2) / (e0 + e1 + e2),)


def _loss_fn(x, target, gain):
    y = x * lax.rsqrt(jnp.mean(x * x, axis=-1, keepdims=True) + EPS) * gain
    err = y - target
    return (0.5 * jnp.mean(err * err, axis=-1, keepdims=True),)


def _rotate(v, cos, sin):
    half = DSW_HEAD_DIM // 2
    lane = lax.broadcasted_iota(jnp.int32, cos.shape, 1)
    low = (lane % DSW_HEAD_DIM) < half
    slabs = []
    for s in range(v.shape[1] // LANES):
        x = v[:, s * LANES:(s + 1) * LANES]
        swapped = jnp.where(low, pltpu.roll(x, LANES - half, 1), pltpu.roll(x, half, 1))
        slabs.append(x * cos + swapped * sin)
    return jnp.concatenate(slabs, axis=1)


def _rope_tables(n_tokens):
    half = DSW_HEAD_DIM // 2
    inv_freq = ROPE_THETA ** (-jnp.arange(half, dtype=F32) / half)
    ang = jnp.arange(n_tokens, dtype=F32)[:, None] * inv_freq[None, :]
    cos, sin = jnp.cos(ang), jnp.sin(ang)
    return jnp.tile(jnp.concatenate([cos, cos], 1), (1, 2)), jnp.tile(jnp.concatenate([-sin, sin], 1), (1, 2))


def _attn_probs(q, kp, kc, group, n, n_blocks):
    blk = DSW_BLOCK
    k = _each(lambda a, b: jnp.concatenate([a, b], axis=0).astype(BF16), kp, kc)
    s = _each(lambda a, b: lax.dot_general(a.astype(BF16), b, NT, preferred_element_type=F32)
              * (DSW_HEAD_DIM ** -0.5), q, k)
    per_seq = [n_blocks // d for d in DSW_DILATIONS]
    blocks_per_seq = jnp.where(group == 0, per_seq[0], jnp.where(group == 1, per_seq[1], per_seq[2]))
    first = (n % blocks_per_seq) == 0
    qi = lax.broadcasted_iota(jnp.int32, (blk, 2 * blk), 0)
    kj = lax.broadcasted_iota(jnp.int32, (blk, 2 * blk), 1)
    dist = qi + blk - kj
    valid = (dist >= 0) & (dist <= blk) & jnp.logical_or(kj >= blk, jnp.logical_not(first))
    s = _each(lambda a: jnp.where(valid, a, -1e30), s)
    m = _each(lambda a: jnp.max(a, axis=-1, keepdims=True), s)
    p = _each(lambda a, b: jnp.exp(a - b), s, m)
    l = _each(lambda a: jnp.sum(a, axis=-1, keepdims=True), p)
    return _each(lambda a, b: a / b, p, l), _each(lambda a, b: a + jnp.log(b), m, l), k


GROUP_WIDTH = DSW_HEADS_PER_GROUP * DSW_HEAD_DIM


def _attn_specs(n_tokens):
    blk = DSW_BLOCK
    cur = pl.BlockSpec((1, blk, GROUP_WIDTH), lambda g, n: (g, n, 0))
    prev = pl.BlockSpec((1, blk, GROUP_WIDTH), lambda g, n: (g, jnp.maximum(n - 1, 0), 0))
    return cur, prev


def _heads_of(ref):
    x = ref[0]
    return [x[:, h * DSW_HEAD_DIM:(h + 1) * DSW_HEAD_DIM] for h in range(DSW_HEADS_PER_GROUP)]


def _group_of(heads):
    return jnp.concatenate(heads, axis=1)


def _attn_fwd(q, k, v):
    n_groups, n_tokens, _ = q.shape
    cur, prev = _attn_specs(n_tokens)

    def body(q_ref, kp_ref, kc_ref, vp_ref, vc_ref, o_ref, l_ref):
        p, lse, _ = _attn_probs(_heads_of(q_ref), _heads_of(kp_ref), _heads_of(kc_ref),
                                pl.program_id(0), pl.program_id(1), n_tokens // DSW_BLOCK)
        vv = _each(lambda a, b: jnp.concatenate([a, b], axis=0).astype(BF16), _heads_of(vp_ref), _heads_of(vc_ref))
        o = _each(lambda a, b: lax.dot_general(a.astype(BF16), b, NN, preferred_element_type=F32), p, vv)
        lse_wide = _each(lambda a: jnp.broadcast_to(a, (DSW_BLOCK, DSW_HEAD_DIM)), lse)
        o_ref[0] = _group_of(o)
        l_ref[0] = _group_of(lse_wide)

    return pl.pallas_call(
        body, grid=(n_groups, n_tokens // DSW_BLOCK), in_specs=[cur, prev, cur, prev, cur],
        out_specs=(cur, cur), out_shape=(jax.ShapeDtypeStruct(q.shape, F32), jax.ShapeDtypeStruct(q.shape, F32)),
        name="attn_fwd", compiler_params=_params(2),
    )(q, k, k, v, v)


def _attn_bwd(q, k, v, do, dlse):
    n_groups, n_tokens, _ = q.shape
    nblk = n_tokens // DSW_BLOCK
    cur, prev = _attn_specs(n_tokens)
    part = pl.BlockSpec((1, 1, 2 * DSW_BLOCK, GROUP_WIDTH), lambda g, n: (g, n, 0, 0))
    scale = DSW_HEAD_DIM ** -0.5

    def body(q_ref, kp_ref, kc_ref, vp_ref, vc_ref, do_ref, dl_ref, dq_ref, dk_ref, dv_ref):
        qs = _heads_of(q_ref)
        p, _, kb = _attn_probs(qs, _heads_of(kp_ref), _heads_of(kc_ref), pl.program_id(0), pl.program_id(1), nblk)
        qb = _each(lambda a: a.astype(BF16), qs)
        vv = _each(lambda a, b: jnp.concatenate([a, b], axis=0).astype(BF16), _heads_of(vp_ref), _heads_of(vc_ref))
        dob = _each(lambda a: a.astype(BF16), _heads_of(do_ref))
        dp = _each(lambda a, b: lax.dot_general(a, b, NT, preferred_element_type=F32), dob, vv)
        dv = _each(lambda a, b: lax.dot_general(a.astype(BF16), b, TN, preferred_element_type=F32), p, dob)
        dl = _each(lambda a: jnp.sum(a, axis=-1, keepdims=True), _heads_of(dl_ref))
        ds = _each(lambda a, b, c: (a * (b - jnp.sum(b * a, axis=-1, keepdims=True) + c) * scale).astype(BF16),
                   p, dp, dl)
        dq = _each(lambda a, b: lax.dot_general(a, b, NN, preferred_element_type=F32), ds, kb)
        dk = _each(lambda a, b: lax.dot_general(a, b, TN, preferred_element_type=F32), ds, qb)
        dq_ref[0] = _group_of(dq)
        dk_ref[0, 0] = _group_of(dk)
        dv_ref[0, 0] = _group_of(dv)

    partial_shape = jax.ShapeDtypeStruct((n_groups, nblk, 2 * DSW_BLOCK, GROUP_WIDTH), F32)
    dq, dkp, dvp = pl.pallas_call(
        body, grid=(n_groups, nblk), in_specs=[cur, prev, cur, prev, cur, cur, cur],
        out_specs=(cur, part, part), out_shape=(jax.ShapeDtypeStruct(q.shape, F32), partial_shape, partial_shape),
        name="attn_bwd", compiler_params=_params(2),
    )(q, k, k, v, v, do, dlse)

    def fold(partial):
        own = partial[:, :, DSW_BLOCK:]
        from_next = jnp.pad(partial[:, 1:, :DSW_BLOCK], ((0, 0), (0, 1), (0, 0), (0, 0)))
        return (own + from_next).reshape(n_groups, n_tokens, GROUP_WIDTH)

    return dq, fold(dkp), fold(dvp)


def _to_heads(a):
    n_tokens = a.shape[0]
    outs = []
    for gi, d in enumerate(DSW_DILATIONS):
        blk = a[:, gi * GROUP_WIDTH:(gi + 1) * GROUP_WIDTH].reshape(n_tokens // d, d, GROUP_WIDTH)
        outs.append(blk.transpose(1, 0, 2).reshape(1, n_tokens, GROUP_WIDTH))
    return jnp.concatenate(outs, 0)


def _from_heads(a):
    n_tokens = a.shape[1]
    return [a[gi].reshape(d, n_tokens // d, GROUP_WIDTH).transpose(1, 0, 2).reshape(n_tokens, GROUP_WIDTH)
            for gi, d in enumerate(DSW_DILATIONS)]


CONV_TILE = 512


def _shift_down(x, k, rows):
    return x if k == 0 else jnp.where(rows >= k, pltpu.roll(x, k, 0), 0.0)


def _shift_up(x, k, rows):
    n = x.shape[0]
    return x if k == 0 else jnp.where(rows < n - k, pltpu.roll(x, n - k, 0), 0.0)


def _conv_pre(x, w):
    rows = lax.broadcasted_iota(jnp.int32, x.shape, 0)
    acc = x * w[GDN_CONV - 1:GDN_CONV]
    for k in range(1, GDN_CONV):
        acc = acc + _shift_down(x, k, rows) * w[GDN_CONV - 1 - k:GDN_CONV - k]
    return acc, rows


def _conv_fwd(x, w):
    n_tokens, width = x.shape
    big = pl.BlockSpec((n_tokens, CONV_TILE), lambda j: (0, j))
    wsp = pl.BlockSpec((GDN_CONV, CONV_TILE), lambda j: (0, j))

    def body(x_ref, w_ref, o_ref):
        acc, _ = _conv_pre(x_ref[...], w_ref[...])
        o_ref[...] = acc * jax.nn.sigmoid(acc)

    return pl.pallas_call(
        body, grid=(width // CONV_TILE,), in_specs=[big, wsp], out_specs=big,
        out_shape=jax.ShapeDtypeStruct(x.shape, F32), name="conv_fwd", compiler_params=_params(1),
    )(x, w)


def _conv_bwd(x, w, dy):
    n_tokens, width = x.shape
    big = pl.BlockSpec((n_tokens, CONV_TILE), lambda j: (0, j))
    wsp = pl.BlockSpec((GDN_CONV, CONV_TILE), lambda j: (0, j))

    def body(x_ref, w_ref, dy_ref, dx_ref, dw_ref):
        xv, wv = x_ref[...], w_ref[...]
        acc, rows = _conv_pre(xv, wv)
        sg = jax.nn.sigmoid(acc)
        dacc = dy_ref[...] * (sg + acc * sg * (1.0 - sg))
        dx = dacc * wv[GDN_CONV - 1:GDN_CONV]
        for k in range(1, GDN_CONV):
            dx = dx + _shift_up(dacc, k, rows) * wv[GDN_CONV - 1 - k:GDN_CONV - k]
        dx_ref[...] = dx.astype(BF16)
        for k in range(GDN_CONV):
            dw_ref[GDN_CONV - 1 - k:GDN_CONV - k, :] = jnp.sum(dacc * _shift_down(xv, k, rows), axis=0, keepdims=True)

    return pl.pallas_call(
        body, grid=(width // CONV_TILE,), in_specs=[big, wsp, big], out_specs=(big, wsp),
        out_shape=(jax.ShapeDtypeStruct(x.shape, BF16), jax.ShapeDtypeStruct(w.shape, F32)),
        name="conv_bwd", compiler_params=_params(1),
    )(x, w, dy)


def _dot3(a, b, dn=NN):
    return lax.dot_general(a, b, dn, precision=lax.Precision.HIGH, preferred_element_type=F32)


def _bf16_dot(a, b, dn):
    return lax.dot_general(a.astype(BF16), b.astype(BF16), dn, preferred_element_type=F32)


_DOT_GRADS = {NN: (("g", "b", NT), ("a", "g", TN)), NT: (("g", "b", NN), ("g", "a", TN)),
              TN: (("b", "g", NT), ("a", "g", NN))}


def _make_bdot(dn):
    @jax.custom_vjp
    def op(a, b):
        return _bf16_dot(a, b, dn)

    def fwd(a, b):
        return op(a, b), (a, b)

    def bwd(saved, g):
        vals = dict(a=saved[0], b=saved[1], g=g)
        return tuple(_bf16_dot(vals[x], vals[y], form) for x, y, form in _DOT_GRADS[dn])

    op.defvjp(fwd, bwd)
    return op


_BDOTS = {dn: _make_bdot(dn) for dn in (NN, NT, TN)}


def _bdot(a, b, dn=NN):
    return _BDOTS[dn](a, b)


def _each(fn, *lists):
    return [fn(*items) for items in zip(*lists)]


@jax.custom_vjp
def _known_inverse(m, inverse):
    return inverse


def _known_inverse_fwd(m, inverse):
    return inverse, inverse


def _known_inverse_bwd(inverse, d_inverse):
    return -_dot3(_dot3(inverse, d_inverse, TN), inverse, NT), jnp.zeros_like(inverse)


_known_inverse.defvjp(_known_inverse_fwd, _known_inverse_bwd)


def _gdn_chunks(q, k, v, b, gcum, state, inverse=None):
    c = GDN_CHUNK
    ii = lax.broadcasted_iota(jnp.int32, (c, c), 0)
    jj = lax.broadcasted_iota(jnp.int32, (c, c), 1)
    qn = _each(lambda x: x * lax.rsqrt(jnp.sum(x * x, axis=-1, keepdims=True) + EPS) * (GDN_HEAD_DIM ** -0.5), q)
    kn = _each(lambda x: x * lax.rsqrt(jnp.sum(x * x, axis=-1, keepdims=True) + EPS), k)
    gcum_i = _each(lambda x: jnp.broadcast_to(x, (c, c)), gcum)
    gcum_j = _each(jnp.transpose, gcum_i)
    decay = _each(lambda x, y: jnp.exp(jnp.where(jj <= ii, x - y, -1e30)), gcum_i, gcum_j)
    g_last = _each(lambda x: x[c - 1:c, :], gcum)
    e_gcum = _each(jnp.exp, gcum)
    kbeta = _each(lambda x, y: x * y, kn, b)
    vbeta = _each(lambda x, y: x * y, v, b)
    m = _each(lambda x, y, d: jnp.where(jj < ii, _bdot(x, y, NT) * d, 0.0), kbeta, kn, decay)
    if inverse is not None:
        inv = _each(_known_inverse, m, inverse)
    else:
        eye = (ii == jj).astype(F32)
        inv = _each(lambda x: eye - x, m)
        power = _each(lambda x: _dot3(x, x), m)
        for step in range(5):
            inv = _each(lambda x, p: x + _dot3(x, p), inv, power)
            if step < 4:
                power = _each(lambda p: _dot3(p, p), power)
    u = _each(_dot3, inv, vbeta)
    w = _each(lambda x, y, e: _dot3(x, y * e), inv, kbeta, e_gcum)
    a_qk = _each(lambda x, y, d: _bdot(x, y, NT) * d, qn, kn, decay)
    v_new = _each(lambda x, y, s: x - _bdot(y, s), u, w, state)
    o = _each(lambda x, e, s, a, vn: _bdot(x * e, s) + _bdot(a, vn), qn, e_gcum, state, a_qk, v_new)
    new_state = _each(lambda s, gl, x, gc, vn: s * jnp.exp(gl) + _bdot(x * jnp.exp(gl - gc), vn, TN),
                      state, g_last, kn, gcum, v_new)
    return o, new_state, inv


GDN_HEADS_PER_STEP = 8


GDN_TIME_TILE = 256


def _gdn_specs(n_tokens, reverse):
    hb, hd, tt = GDN_HEADS_PER_STEP, GDN_HEAD_DIM, GDN_TIME_TILE
    nb, nt = GDN_HEADS // hb, n_tokens // tt

    def when(t):
        return nt - 1 - t if reverse else t

    q = pl.BlockSpec((tt, hb * hd), lambda h, t: (when(t), h))
    k = pl.BlockSpec((tt, hb * hd), lambda h, t: (when(t), nb + h))
    v = pl.BlockSpec((tt, hb * hd), lambda h, t: (when(t), 2 * nb + h))
    vec = pl.BlockSpec((tt, hb), lambda h, t: (when(t), h))
    states = pl.BlockSpec((hb, tt // GDN_CHUNK, hd, hd), lambda h, t: (h, when(t), 0, 0))
    inverses = pl.BlockSpec((hb, tt // GDN_CHUNK, GDN_CHUNK, GDN_CHUNK), lambda h, t: (h, when(t), 0, 0))
    return q, k, v, vec, states, inverses


def _gdn_fwd(qkv, beta, g):
    n_tokens = qkv.shape[0]
    hb, hd, tt = GDN_HEADS_PER_STEP, GDN_HEAD_DIM, GDN_TIME_TILE
    n_chunks = tt // GDN_CHUNK
    q_s, k_s, v_s, vec, st, inv_s = _gdn_specs(n_tokens, False)

    def body(q_ref, k_ref, v_ref, b_ref, g_ref, o_ref, st_ref, inv_ref, state):
        @pl.when(pl.program_id(1) == 0)
        def _():
            state[...] = jnp.zeros_like(state)

        def step(c, carry):
            r = pl.ds(pl.multiple_of(c * GDN_CHUNK, GDN_CHUNK), GDN_CHUNK)
            cols = [slice(h * hd, (h + 1) * hd) for h in range(hb)]
            old = [state[h] for h in range(hb)]
            o, new, inv = _gdn_chunks(
                [q_ref[r, cs] for cs in cols], [k_ref[r, cs] for cs in cols], [v_ref[r, cs] for cs in cols],
                [b_ref[r, h:h + 1] for h in range(hb)], [g_ref[r, h:h + 1] for h in range(hb)], old)
            for h in range(hb):
                st_ref[h, c] = old[h]
                inv_ref[h, c] = inv[h]
                o_ref[r, cols[h]] = o[h]
                state[h] = new[h]
            return carry

        lax.fori_loop(0, n_chunks, step, 0)

    n_all = n_tokens // GDN_CHUNK
    return pl.pallas_call(
        body, grid=(GDN_HEADS // hb, n_tokens // tt), in_specs=[q_s, k_s, v_s, vec, vec], out_specs=(q_s, st, inv_s),
        out_shape=(jax.ShapeDtypeStruct((n_tokens, GDN_WIDTH), F32),
                   jax.ShapeDtypeStruct((GDN_HEADS, n_all, hd, hd), F32),
                   jax.ShapeDtypeStruct((GDN_HEADS, n_all, GDN_CHUNK, GDN_CHUNK), F32)),
        scratch_shapes=[pltpu.VMEM((hb, hd, hd), F32)],
        name="gdn_fwd", compiler_params=_params(2),
    )(qkv, qkv, qkv, beta, g)


def _gdn_bwd(qkv, beta, g, states, inverses, do):
    n_tokens = qkv.shape[0]
    hb, hd, tt = GDN_HEADS_PER_STEP, GDN_HEAD_DIM, GDN_TIME_TILE
    n_chunks = tt // GDN_CHUNK
    q_s, k_s, v_s, vec, st, inv_s = _gdn_specs(n_tokens, True)

    assert hb == GDN_HEADS

    def body(q_ref, k_ref, v_ref, b_ref, g_ref, st_ref, inv_ref, do_ref, dqkv_ref, db_ref, dg_ref, dstate):
        @pl.when(pl.program_id(1) == 0)
        def _():
            dstate[...] = jnp.zeros_like(dstate)

        def step(i, carry):
            c = n_chunks - 1 - i
            r = pl.ds(pl.multiple_of(c * GDN_CHUNK, GDN_CHUNK), GDN_CHUNK)
            cols = [slice(h * hd, (h + 1) * hd) for h in range(hb)]
            args = ([q_ref[r, cs] for cs in cols], [k_ref[r, cs] for cs in cols], [v_ref[r, cs] for cs in cols],
                    [b_ref[r, h:h + 1] for h in range(hb)], [g_ref[r, h:h + 1] for h in range(hb)],
                    [st_ref[h, c] for h in range(hb)])
            saved = [inv_ref[h, c] for h in range(hb)]
            cts = ([do_ref[r, cs] for cs in cols], [dstate[h] for h in range(hb)])
            dq, dk, dv, db, dg, dst = jax.vjp(lambda *a: _gdn_chunks(*a, inverse=saved)[:2], *args)[1](cts)
            for h in range(hb):
                for part, grad in enumerate((dq, dk, dv)):
                    dqkv_ref[r, pl.ds(part * GDN_WIDTH + h * hd, hd)] = grad[h]
                db_ref[r, h:h + 1] = db[h]
                dg_ref[r, h:h + 1] = dg[h]
                dstate[h] = dst[h]
            return carry

        lax.fori_loop(0, n_chunks, step, 0)

    n_t = n_tokens // tt
    thin = jax.ShapeDtypeStruct(beta.shape, F32)
    return pl.pallas_call(
        body, grid=(GDN_HEADS // hb, n_t), in_specs=[q_s, k_s, v_s, vec, vec, st, inv_s, q_s],
        out_specs=(pl.BlockSpec((tt, 3 * GDN_WIDTH), lambda h, t: (n_t - 1 - t, 0)), vec, vec),
        out_shape=(jax.ShapeDtypeStruct(qkv.shape, F32), thin, thin),
        scratch_shapes=[pltpu.VMEM((hb, hd, hd), F32)],
        name="gdn_bwd", compiler_params=_params(2),
    )(qkv, qkv, qkv, beta, g, states, inverses, do)


FFN_ROW_TILE = 256
FFN_CHUNK = 256
FFN_FWD_ROW_TILE = 512


def _resident(shape):
    return pl.BlockSpec(shape, lambda i: (0,) * len(shape), pipeline_mode=pl.Buffered(1))


def _ffn_fwd(x, gain, wg, wu, wd, name):
    n_tokens, d = x.shape
    n_shards, n, _ = wg.shape
    tm = FFN_FWD_ROW_TILE

    def body(x_ref, gain_ref, wg_ref, wu_ref, wd_ref, o_ref, g_ref, u_ref):
        xv = x_ref[...]
        h = (xv * lax.rsqrt(jnp.mean(xv * xv, axis=-1, keepdims=True) + EPS) * gain_ref[...]).astype(BF16)
        acc = jnp.zeros((tm, d), F32)
        for j in range(n_shards):
            g = lax.dot_general(h, wg_ref[j], NT, preferred_element_type=F32)
            u = lax.dot_general(h, wu_ref[j], NT, preferred_element_type=F32)
            g_ref[j] = g
            u_ref[j] = u
            a = (g * jax.nn.sigmoid(g) * u).astype(BF16)
            acc = acc + lax.dot_general(a, wd_ref[j], NN, preferred_element_type=F32)
        o_ref[...] = xv + 0.5 * acc

    row = pl.BlockSpec((tm, d), lambda i: (i, 0))
    hid = pl.BlockSpec((n_shards, tm, n), lambda i: (0, i, 0))
    return pl.pallas_call(
        body, grid=(n_tokens // tm,),
        in_specs=[row, _resident(gain.shape), _resident(wg.shape), _resident(wu.shape), _resident(wd.shape)],
        out_specs=(row, hid, hid),
        out_shape=(jax.ShapeDtypeStruct(x.shape, F32), jax.ShapeDtypeStruct((n_shards, n_tokens, n), F32),
                   jax.ShapeDtypeStruct((n_shards, n_tokens, n), F32)),
        name=name, compiler_params=_params(1),
    )(x, gain, wg, wu, wd)


def _ffn_bwd_rows(x, gain, dy, g, u, wg, wu, wd, name):
    n_tokens, d = x.shape
    n_shards, n, _ = wg.shape
    tm = FFN_ROW_TILE

    def body(x_ref, gain_ref, dy_ref, g_ref, u_ref, wg_ref, wu_ref, wd_ref,
             dx_ref, dgain_ref, h_ref, dyh_ref, a_ref, dg_ref, du_ref):
        xv, dyv, gain_v = x_ref[...], dy_ref[...], gain_ref[...]
        r = lax.rsqrt(jnp.mean(xv * xv, axis=-1, keepdims=True) + EPS)
        xhat = xv * r
        h_ref[...] = (xhat * gain_v).astype(BF16)
        dyh = (0.5 * dyv).astype(BF16)
        dyh_ref[...] = dyh
        dh = jnp.zeros((tm, d), F32)
        for j in range(n_shards):
            da = lax.dot_general(dyh, wd_ref[j], NT, preferred_element_type=F32)
            gv, uv = g_ref[j], u_ref[j]
            sg = jax.nn.sigmoid(gv)
            silu = gv * sg
            a_ref[j] = (silu * uv).astype(BF16)
            dg = (da * uv * (sg + silu * (1.0 - sg))).astype(BF16)
            du = (da * silu).astype(BF16)
            dg_ref[j] = dg
            du_ref[j] = du
            dh = dh + lax.dot_general(dg, wg_ref[j], NN, preferred_element_type=F32)
            dh = dh + lax.dot_general(du, wu_ref[j], NN, preferred_element_type=F32)
        dxhat = dh * gain_v
        dx_ref[...] = dyv + r * (dxhat - xhat * jnp.mean(dxhat * xhat, axis=-1, keepdims=True))

        @pl.when(pl.program_id(0) == 0)
        def _():
            dgain_ref[...] = jnp.zeros_like(dgain_ref)

        dgain_ref[...] += jnp.sum(dh * xhat, axis=0, keepdims=True)

    row = pl.BlockSpec((tm, d), lambda i: (i, 0))
    hid = pl.BlockSpec((n_shards, tm, n), lambda i: (0, i, 0))
    hid_shape = (n_shards, n_tokens, n)
    return pl.pallas_call(
        body, grid=(n_tokens // tm,),
        in_specs=[row, _resident(gain.shape), row, hid, hid, _resident(wg.shape), _resident(wu.shape),
                  _resident(wd.shape)],
        out_specs=(row, pl.BlockSpec(gain.shape, lambda i: (0, 0)), row, row, hid, hid, hid),
        out_shape=(jax.ShapeDtypeStruct(x.shape, F32), jax.ShapeDtypeStruct(gain.shape, F32),
                   jax.ShapeDtypeStruct(x.shape, BF16), jax.ShapeDtypeStruct(x.shape, BF16),
                   jax.ShapeDtypeStruct(hid_shape, BF16), jax.ShapeDtypeStruct(hid_shape, BF16),
                   jax.ShapeDtypeStruct(hid_shape, BF16)),
        name=name, compiler_params=_params(1),
    )(x, gain, dy, g, u, wg, wu, wd)


def _ffn_bwd_weights(h, dyh, a, dg, du, name, with_payload=False):
    n_chunks, n_tokens, n = a.shape
    d = h.shape[1]

    def body(h_ref, dyh_ref, a_ref, dg_ref, du_ref, *out_refs):
        hv = h_ref[...]
        vals = (lax.dot_general(dg_ref[0], hv, TN, preferred_element_type=F32),
                lax.dot_general(du_ref[0], hv, TN, preferred_element_type=F32),
                lax.dot_general(a_ref[0], dyh_ref[...], TN, preferred_element_type=F32))
        for ref, val in zip(out_refs[-3:], vals):
            ref[0] = val
        if with_payload:
            for ref, val in zip(out_refs[:3], vals):
                ref[0] = val.astype(BF16)

    hid = pl.BlockSpec((1, n_tokens, n), lambda j: (j, 0, 0))
    out = pl.BlockSpec((1, n, d), lambda j: (j, 0, 0))
    shapes = (jax.ShapeDtypeStruct((n_chunks, n, d), F32),) * 3
    if with_payload:
        shapes = (jax.ShapeDtypeStruct((n_chunks, n, d), BF16),) * 3 + shapes
    outs = pl.pallas_call(
        body, grid=(n_chunks,), in_specs=[_resident(h.shape), _resident(dyh.shape), hid, hid, hid],
        out_specs=(out,) * len(shapes), out_shape=shapes, name=name, compiler_params=_params(1),
    )(h, dyh, a, dg, du)
    return (outs[:3], outs[3:]) if with_payload else outs


IN_PIECES = (("wq_a", 0, 768), ("wk_a", 768, 1536), ("wv_a", 1536, 2304), ("w_qkvb", 2304, 5376),
             ("w_small", 5376, 5392), ("w_ggate", 5392, 6416), ("w_gatea", 6416, 7440), ("w_gateb", 7440, 8464))
IN_NAMES = tuple(name for name, _, _ in IN_PIECES)


def _in_rows(lo, hi):
    return lo, max(hi, lo + LANES)


N_ROTATED = 2


def _in_proj_fwd(x, gain, wt, cos, sin):
    n_tokens, d = x.shape
    tm = FFN_ROW_TILE
    rows = [_in_rows(lo, hi) for _, lo, hi in IN_PIECES]

    def body(x_ref, gain_ref, wt_ref, cos_ref, sin_ref, *o_refs):
        xv = x_ref[...]
        h = (xv * lax.rsqrt(jnp.mean(xv * xv, axis=-1, keepdims=True) + EPS) * gain_ref[...]).astype(BF16)
        for k, ((lo, hi), o_ref) in enumerate(zip(rows, o_refs)):
            z = lax.dot_general(h, wt_ref[lo:hi, :], NT, preferred_element_type=F32)
            o_ref[...] = _rotate(z, cos_ref[...], sin_ref[...]) if k < N_ROTATED else z

    tab = pl.BlockSpec((tm, LANES), lambda i: (i, 0))
    return pl.pallas_call(
        body, grid=(n_tokens // tm,),
        in_specs=[pl.BlockSpec((tm, d), lambda i: (i, 0)), _resident(gain.shape), _resident(wt.shape), tab, tab],
        out_specs=tuple(pl.BlockSpec((tm, hi - lo), lambda i: (i, 0)) for lo, hi in rows),
        out_shape=tuple(jax.ShapeDtypeStruct((n_tokens, hi - lo), F32) for lo, hi in rows),
        name="in_proj_fwd", compiler_params=_params(1),
    )(x, gain, wt, cos, sin)


def _in_proj_bwd_rows(x, gain, dres, dzs, wt, cos, sin):
    n_tokens, d = x.shape
    tm = FFN_ROW_TILE
    n = len(dzs)
    rows = [_in_rows(lo, hi) for _, lo, hi in IN_PIECES]

    def body(x_ref, gain_ref, dres_ref, cos_ref, sin_ref, *refs):
        dz_refs, wt_ref = refs[:n], refs[n]
        dx_ref, dgain_ref, h_ref = refs[n + 1:n + 4]
        unrotated_refs = refs[n + 4:]
        xv, gain_v = x_ref[...], gain_ref[...]
        r = lax.rsqrt(jnp.mean(xv * xv, axis=-1, keepdims=True) + EPS)
        xhat = xv * r
        h_ref[...] = (xhat * gain_v).astype(BF16)
        dh = jnp.zeros((tm, d), F32)
        for k, (dz_ref, (lo, hi)) in enumerate(zip(dz_refs, rows)):
            dz = dz_ref[...]
            if k < N_ROTATED:
                dz = _rotate(dz, cos_ref[...], -sin_ref[...]).astype(BF16)
                unrotated_refs[k][...] = dz
            dh = dh + lax.dot_general(dz.astype(BF16), wt_ref[lo:hi, :], NN, preferred_element_type=F32)
        dxhat = dh * gain_v
        dx_ref[...] = dres_ref[...] + r * (dxhat - xhat * jnp.mean(dxhat * xhat, axis=-1, keepdims=True))

        @pl.when(pl.program_id(0) == 0)
        def _():
            dgain_ref[...] = jnp.zeros_like(dgain_ref)

        dgain_ref[...] += jnp.sum(dh * xhat, axis=0, keepdims=True)

    row = pl.BlockSpec((tm, d), lambda i: (i, 0))
    tab = pl.BlockSpec((tm, LANES), lambda i: (i, 0))
    dz_specs = [pl.BlockSpec((tm, dz.shape[1]), lambda i: (i, 0)) for dz in dzs]
    outs = pl.pallas_call(
        body, grid=(n_tokens // tm,),
        in_specs=[row, _resident(gain.shape), row, tab, tab] + dz_specs + [_resident(wt.shape)],
        out_specs=(row, pl.BlockSpec(gain.shape, lambda i: (0, 0)), row) + tuple(dz_specs[:N_ROTATED]),
        out_shape=(jax.ShapeDtypeStruct(x.shape, F32), jax.ShapeDtypeStruct(gain.shape, F32),
                   jax.ShapeDtypeStruct(x.shape, BF16))
        + tuple(jax.ShapeDtypeStruct(dz.shape, BF16) for dz in dzs[:N_ROTATED]),
        name="in_proj_bwd_rows", compiler_params=_params(1),
    )(x, gain, dres, cos, sin, *dzs, wt)
    return outs[0], outs[1], outs[2], outs[3:]


def _in_proj_bwd_weight(dwt, h, dz, lo, hi, name):
    n_tokens, d = h.shape
    width = hi - lo
    tn = _tile(width, 512) if width >= LANES else width
    dz_tile = max(tn, LANES)

    def body(dwt_ref, h_ref, dz_ref, o_ref):
        o_ref[...] = lax.dot_general(dz_ref[:, :tn].astype(BF16), h_ref[...], TN, preferred_element_type=F32)

    return pl.pallas_call(
        body, grid=(width // tn,),
        in_specs=[ANY, _resident(h.shape), pl.BlockSpec((n_tokens, dz_tile), lambda j: (0, j))],
        out_specs=pl.BlockSpec((pl.Element(tn), pl.Element(d)), lambda j: (pl.multiple_of(lo + j * tn, 16), 0)),
        out_shape=jax.ShapeDtypeStruct(dwt.shape, F32), input_output_aliases={0: 0}, name=name,
        compiler_params=_params(1),
    )(dwt, h, dz)


def _split_small(z):
    return z[:, :GDN_HEADS], z[:, GDN_HEADS:2 * GDN_HEADS]


def _heads3(q, k, v):
    return _to_heads(q), _to_heads(k), _to_heads(v)


def _tokens6(o, lse):
    return tuple(_from_heads(o)) + tuple(_from_heads(lse))


def _blocks_of(vals, nblk):
    return [[v[:, b * (v.shape[1] // nblk):(b + 1) * (v.shape[1] // nblk)] for v in vals] for b in range(nblk)]


def _rowwise_matmul_fwd(fn, name, rows, params, wt, nblk, res=None):
    n_rows = rows[0].shape[0]
    tm = FFN_ROW_TILE
    k, n = wt.shape
    nr, npar = len(rows), len(params)

    def body(*refs):
        row_vals = [r[...] for r in refs[:nr]]
        par_vals = [r[...] for r in refs[nr:nr + npar]]
        wt_ref = refs[nr + npar]
        o_ref, y_ref = refs[-2:]
        y = jnp.concatenate([fn(*blk, *par_vals)[0] for blk in _blocks_of(row_vals, nblk)], axis=1).astype(BF16)
        y_ref[...] = y
        acc = lax.dot_general(y, wt_ref[...], NN, preferred_element_type=F32)
        o_ref[...] = acc if res is None else refs[nr + npar + 1][...] + acc

    row_specs = [pl.BlockSpec((tm, a.shape[1]), lambda i: (i, 0)) for a in rows]
    ins = list(rows) + list(params) + [wt] + ([] if res is None else [res])
    specs = row_specs + [_resident(p.shape) for p in params] + [_resident(wt.shape)]
    if res is not None:
        specs.append(pl.BlockSpec((tm, n), lambda i: (i, 0)))
    return pl.pallas_call(
        body, grid=(n_rows // tm,), in_specs=specs,
        out_specs=(pl.BlockSpec((tm, n), lambda i: (i, 0)), pl.BlockSpec((tm, k), lambda i: (i, 0))),
        out_shape=(jax.ShapeDtypeStruct((n_rows, n), F32), jax.ShapeDtypeStruct((n_rows, k), BF16)),
        name=name, compiler_params=_params(1),
    )(*ins)


def _rowwise_matmul_bwd(fn, name, rows, params, wt, dout, nblk, row_dtypes=None):
    n_rows = rows[0].shape[0]
    row_dtypes = row_dtypes or (F32,) * len(rows)
    tm = FFN_ROW_TILE
    nr, npar = len(rows), len(params)

    def body(*refs):
        row_vals = [r[...] for r in refs[:nr]]
        par_vals = [r[...] for r in refs[nr:nr + npar]]
        wt_ref, dout_ref = refs[nr + npar], refs[nr + npar + 1]
        outs = refs[nr + npar + 2:]
        dy = lax.dot_general(dout_ref[...].astype(BF16), wt_ref[...], NT, preferred_element_type=F32)
        grads = [jax.vjp(fn, *blk, *par_vals)[1]((dy_blk,))
                 for blk, (dy_blk,) in zip(_blocks_of(row_vals, nblk), _blocks_of([dy], nblk))]
        for j in range(nr):
            outs[j][...] = jnp.concatenate([g[j] for g in grads], axis=1).astype(row_dtypes[j])
        for j in range(npar):
            ref = outs[nr + j]

            @pl.when(pl.program_id(0) == 0)
            def _(ref=ref):
                ref[...] = jnp.zeros_like(ref)

            for g in grads:
                ref[...] += g[nr + j]

    row_specs = [pl.BlockSpec((tm, a.shape[1]), lambda i: (i, 0)) for a in rows]
    par_specs = [_resident(p.shape) for p in params]
    return pl.pallas_call(
        body, grid=(n_rows // tm,),
        in_specs=row_specs + par_specs + [_resident(wt.shape), pl.BlockSpec((tm, dout.shape[1]), lambda i: (i, 0))],
        out_specs=tuple(row_specs + [pl.BlockSpec(p.shape, lambda i: (0, 0)) for p in params]),
        out_shape=tuple([jax.ShapeDtypeStruct(a.shape, dt) for a, dt in zip(rows, row_dtypes)]
                        + [jax.ShapeDtypeStruct(p.shape, F32) for p in params]),
        name=name, compiler_params=_params(1),
    )(*rows, *params, wt, dout)


def mixer_forward(x1, w, small):
    n_tokens = x1.shape[0]
    cos, sin = _rope_tables(n_tokens)
    proj = dict(zip(IN_NAMES, _in_proj_fwd(x1, small["mix_norm"], w["w_in_t"], cos, sin)))
    (qh, kh, vh), heads_vjp = jax.vjp(_heads3, proj["wq_a"], proj["wk_a"], proj["wv_a"])
    o, lse = _attn_fwd(qh, kh, vh)
    per_group, tokens_vjp = jax.vjp(_tokens6, o, lse)
    pa, ya = _rowwise_matmul_fwd(_combine_fn, "branch_a", per_group, (), w["w_branch_a"], 1)
    qkv = _conv_fwd(proj["w_qkvb"], small["gdn_conv_w"])
    raw, small_vjp = jax.vjp(_split_small, proj["w_small"])
    gdn_params = (small["gdn_a_log"], small["gdn_dt_bias"])
    beta, gcum = _rowwise_fwd(_beta_decay_fn, "beta_decay", raw, (), gdn_params, 512, 1)
    ob, *states = _gdn_fwd(qkv, beta, gcum)
    gate_in = (ob, proj["w_ggate"])
    pb, yb = _rowwise_matmul_fwd(_outnorm_gate_fn, "branch_b", gate_in, (small["gdn_out_norm"],), w["w_branch_b"],
                                 GDN_HEADS)
    merge_in = (proj["w_gatea"], proj["w_gateb"], pa, pb)
    x2, merged = _rowwise_matmul_fwd(_merge_fn, "out", merge_in, (), w["w_out"], 1, res=x1)
    saved = dict(x1=x1, proj=proj, cos=cos, sin=sin, heads_vjp=heads_vjp, heads=(qh, kh, vh), tokens_vjp=tokens_vjp,
                 per_group=per_group, ya=ya, qkv=qkv, raw=raw, small_vjp=small_vjp, beta=beta, gcum=gcum, states=states,
                 gate_in=gate_in, yb=yb, merge_in=merge_in, merged=merged)
    return x2, saved


def mixer_backward(dx2, s, w, small):
    proj = s["proj"]
    grads = dict(w_out=_weight_grad(s["merged"], dx2, "out_dw"))
    dgate_a, dgate_b, dpa, dpb = _rowwise_matmul_bwd(_merge_fn, "out_bwd", s["merge_in"], (), w["w_out"], dx2, 1,
                                                     (BF16,) * 4)
    grads["w_branch_b"] = _weight_grad(s["yb"], dpb, "branch_b_dw")
    grads["w_branch_a"] = _weight_grad(s["ya"], dpa, "branch_a_dw")
    dob, dggate, grads["gdn_out_norm"] = _rowwise_matmul_bwd(
        _outnorm_gate_fn, "branch_b_bwd", s["gate_in"], (small["gdn_out_norm"],), w["w_branch_b"], dpb, GDN_HEADS,
        (F32, BF16))
    dqkv, dbeta, dgcum = _gdn_bwd(s["qkv"], s["beta"], s["gcum"], *s["states"], dob)
    gdn_params = (small["gdn_a_log"], small["gdn_dt_bias"])
    dbeta_raw, ddecay_raw, grads["gdn_a_log"], grads["gdn_dt_bias"] = _rowwise_bwd(
        _beta_decay_fn, "beta_decay_bwd", s["raw"], (), gdn_params, (dbeta, dgcum), 512, 1)
    dsmall = s["small_vjp"]((dbeta_raw, ddecay_raw))[0]
    dqkvb, grads["gdn_conv_w"] = _conv_bwd(proj["w_qkvb"], small["gdn_conv_w"], dqkv)
    dper_group = _rowwise_matmul_bwd(_combine_fn, "branch_a_bwd", s["per_group"], (), w["w_branch_a"], dpa, 1)
    do, dlse = s["tokens_vjp"](tuple(dper_group))
    dqh, dkh, dvh = _attn_bwd(*s["heads"], do, dlse)
    dq_rot, dk_rot, dv = s["heads_vjp"]((dqh, dkh, dvh))
    dzs = (dq_rot, dk_rot, dv, dqkvb, dsmall, dggate, dgate_a, dgate_b)
    dx1, grads["mix_norm"], h, unrotated = _in_proj_bwd_rows(
        s["x1"], small["mix_norm"], dx2, dzs, w["w_in_t"], s["cos"], s["sin"])
    dzs = tuple(unrotated) + dzs[N_ROTATED:]
    dwt = lax.empty(w["w_in_t"].shape, F32)
    for (name, lo, hi), dz in zip(IN_PIECES, dzs):
        dwt = _in_proj_bwd_weight(dwt, h, dz, lo, hi, "in_proj_dw_" + name)
    grads["w_in_t"] = dwt
    return dx1, grads


def ffn_forward(x, gain, w, tag):
    out, g, u = _ffn_fwd(x, gain, w[tag + "_w_gate"], w[tag + "_w_up"], w[tag + "_w_down"], tag + "_fwd")
    return out, (x, g, u)


def ffn_backward(dy, saved, gain, w, tag, with_payload=False):
    x, g, u = saved
    weights = (w[tag + "_w_gate"], w[tag + "_w_up"], w[tag + "_w_down"])
    dx, dgain, h, dyh, a, dg, du = _ffn_bwd_rows(x, gain, dy, g, u, *weights, tag + "_bwd_rows")
    return dx, dgain, _ffn_bwd_weights(h, dyh, a, dg, du, tag + "_bwd_weights", with_payload)


def loss_head(x3, target, gain):
    n_tokens, d = x3.shape
    tm = FFN_ROW_TILE

    def body(x_ref, t_ref, gain_ref, loss_ref, dx_ref, dgain_ref):
        target_v = t_ref[...]
        (row_loss,), vjp = jax.vjp(lambda xv, gv: _loss_fn(xv, target_v, gv), x_ref[...], gain_ref[...])
        dx, dgain = vjp((jnp.ones_like(row_loss),))
        loss_ref[...] = row_loss
        dx_ref[...] = dx

        @pl.when(pl.program_id(0) == 0)
        def _():
            dgain_ref[...] = jnp.zeros_like(dgain_ref)

        dgain_ref[...] += dgain

    row = pl.BlockSpec((tm, d), lambda i: (i, 0))
    row_loss, dx3, dgain = pl.pallas_call(
        body, grid=(n_tokens // tm,), in_specs=[row, row, _resident(gain.shape)],
        out_specs=(pl.BlockSpec((tm, 1), lambda i: (i, 0)), row, pl.BlockSpec(gain.shape, lambda i: (0, 0))),
        out_shape=(jax.ShapeDtypeStruct((n_tokens, 1), F32), jax.ShapeDtypeStruct(x3.shape, F32),
                   jax.ShapeDtypeStruct(gain.shape, F32)),
        name="loss_head", compiler_params=_params(1),
    )(x3, target, gain)
    return jnp.sum(row_loss), dx3, dgain


BIG_WEIGHTS = ("ffn1_w_gate", "ffn1_w_up", "ffn1_w_down", "w_in", "w_branch_a", "w_branch_b", "w_out",
               "ffn2_w_gate", "ffn2_w_up", "ffn2_w_down")
TRANSPOSED = ("ffn1_w_gate", "ffn1_w_up", "w_in", "ffn2_w_gate", "ffn2_w_up")
CONV_SHARD = (GDN_CONV, 3 * GDN_WIDTH // N_DEV)
SMALL_ROWS = 24
ANY = pl.BlockSpec(memory_space=pl.ANY)


TOKEN = jax.ShapeDtypeStruct((8, LANES), F32)


def _after(value, token):
    return value + token[0, 0].astype(value.dtype)


def _position():
    return lax.axis_index("x"), lax.axis_index("y"), lax.axis_index("c")


def all_gather_shards(shards, name):
    n = len(shards)
    per = 8

    def body(*refs):
        x_refs, out_refs = refs[:n], refs[n:2 * n]
        send_sems, recv_sems, local_sems = refs[2 * n + 1:]
        x, y, c = _position()
        me, sibling = (x, y, c), (x, y, 1 - c)
        x_chip, y_chip, far_chip = (1 - x, y), (x, 1 - y), (1 - x, 1 - y)

        def slab(a, px, py, pc):
            return out_refs[a].at[4 * px + 2 * py + pc]

        def copy(a, k, src, dst, to):
            return pltpu.make_async_remote_copy(
                src_ref=src, dst_ref=dst, send_sem=send_sems.at[per * a + k], recv_sem=recv_sems.at[per * a + k],
                device_id=to, device_id_type=MESH)

        def whole(a, k, block, to, src=None):
            return copy(a, k, slab(a, *block) if src is None else src, slab(a, *block), to)

        def half(a, k, block, which, to):
            rows = shards[a].shape[0] // 2
            part = slab(a, *block).at[pl.ds(which * rows, rows)]
            return copy(a, k, part, part, to)

        arrays = range(n)
        mine = [pltpu.make_async_copy(x_refs[a], slab(a, *me), local_sems.at[a]) for a in arrays]
        for cp in mine:
            cp.start()
        started = [whole(a, 1, me, (*x_chip, c), src=x_refs[a]) for a in arrays]
        started += [whole(a, 2, me, (*y_chip, c), src=x_refs[a]) for a in arrays]
        started += [whole(a, 0, me, sibling, src=x_refs[a]) for a in arrays]
        for cp in started:
            cp.start()

        def start(cp):
            cp.start()
            started.append(cp)

        for a in arrays:
            whole(a, 1, (*x_chip, c), me).wait_recv()
            start(half(a, 3, (*x_chip, c), 0, (*y_chip, c)))
            start(whole(a, 5, (*x_chip, c), sibling))
        for a in arrays:
            whole(a, 2, (*y_chip, c), me).wait_recv()
            start(half(a, 4, (*y_chip, c), 1, (*x_chip, c)))
            start(whole(a, 6, (*y_chip, c), sibling))
        for a in arrays:
            half(a, 3, (*far_chip, c), 0, me).wait_recv()
            half(a, 4, (*far_chip, c), 1, me).wait_recv()
            start(whole(a, 7, (*far_chip, c), sibling))
        for a in arrays:
            whole(a, 0, sibling, me).wait_recv()
            for k, chip in ((5, x_chip), (6, y_chip), (7, far_chip)):
                whole(a, k, (*chip, 1 - c), me).wait_recv()
        for cp in started:
            cp.wait_send()
        for cp in mine:
            cp.wait()
        refs[2 * n][...] = jnp.zeros_like(refs[2 * n])

    outs = pl.pallas_call(
        body, out_shape=tuple(jax.ShapeDtypeStruct((N_DEV,) + s.shape, s.dtype) for s in shards) + (TOKEN,),
        in_specs=[ANY] * n, out_specs=(ANY,) * n + (pl.BlockSpec(memory_space=pltpu.VMEM),),
        scratch_shapes=[pltpu.SemaphoreType.DMA((per * n,)), pltpu.SemaphoreType.DMA((per * n,)),
                        pltpu.SemaphoreType.DMA((n,))],
        name=name,
    )(*shards)
    return outs[:n], outs[n]


def exchange_with_sibling(grads):
    n = len(grads)

    def body(*refs):
        g_refs, recv_refs = refs[:n], refs[n:2 * n]
        send_sems, recv_sems = refs[2 * n:]
        x, y, c = _position()
        copies = [pltpu.make_async_remote_copy(
            src_ref=g_refs[a].at[2 * k + 1 - c], dst_ref=recv_refs[a].at[k], send_sem=send_sems.at[4 * a + k],
            recv_sem=recv_sems.at[4 * a + k], device_id=(x, y, 1 - c), device_id_type=MESH)
            for k in range(4) for a in range(n)]
        for cp in copies:
            cp.start()
        for cp in copies:
            cp.wait()

    return pl.pallas_call(
        body, out_shape=tuple(jax.ShapeDtypeStruct((4,) + g.shape[1:], g.dtype) for g in grads),
        in_specs=[ANY] * n, out_specs=(ANY,) * n,
        scratch_shapes=[pltpu.SemaphoreType.DMA((4 * n,)), pltpu.SemaphoreType.DMA((4 * n,))], name="rs_sibling",
    )(*grads)


ELEMENTWISE_TILE_BYTES = 1536 * 1024


def _tile2(rows, cols):
    if rows % 256 == 0:
        return 256, cols
    if rows * cols * 4 > ELEMENTWISE_TILE_BYTES and cols % 256 == 0:
        return rows, 256
    return rows, cols


def add_sibling(grads, received, core, name):
    _, rows, width = grads.shape
    tr, tc = _tile2(rows, width)

    def body(c_ref, g_ref, r_ref, o_ref):
        o_ref[...] = (g_ref[...] + r_ref[...]).astype(BF16)

    blk = (1, tr, tc)
    return pl.pallas_call(
        body,
        grid_spec=pltpu.PrefetchScalarGridSpec(
            num_scalar_prefetch=1, grid=(4, rows // tr, width // tc),
            in_specs=[pl.BlockSpec(blk, lambda k, i, j, c_ref: (2 * k + c_ref[0], i, j)),
                      pl.BlockSpec(blk, lambda k, i, j, c_ref: (k, i, j))],
            out_specs=pl.BlockSpec(blk, lambda k, i, j, c_ref: (k, i, j))),
        out_shape=jax.ShapeDtypeStruct((4, rows, width), BF16), name=name, compiler_params=_params(3),
    )(core, grads, received)


HBM = pl.BlockSpec(memory_space=pltpu.HBM)
SEM = pl.BlockSpec(memory_space=pltpu.SEMAPHORE)
DATAFLOW_EFFECT = pltpu.SideEffectType.DATAFLOW_SIDE_EFFECTING
N_PEERS = N_DEV - 1


def _peer(mask):
    x, y, c = _position()
    px = 1 - x if mask & 4 else x
    py = 1 - y if mask & 2 else y
    pc = 1 - c if mask & 1 else c
    return (px, py, pc), 4 * px + 2 * py + pc


ALL_PEERS = tuple(range(1, N_DEV))
OTHER_CHIPS = (4, 2, 6)


SIBLING = 1
GATHER_MODES = ("gather", "near")


def _exchange_peers(mode):
    return {"chips": OTHER_CHIPS, "near": (SIBLING,) + OTHER_CHIPS}.get(mode, ALL_PEERS)


def _direct_copies(src_refs, land_refs, send_sems, recv_sems, mode):
    x, y, c = _position()
    me = 4 * x + 2 * y + c
    masks = _exchange_peers(mode)
    copies = []
    for a, (src, land) in enumerate(zip(src_refs, land_refs)):
        for slot, mask in enumerate(masks):
            peer, peer_index = _peer(mask)
            k = len(masks) * a + slot
            if mode in GATHER_MODES:
                source, dest = src, land.at[me]
            elif mode == "scatter":
                source, dest = src.at[peer_index], land.at[slot]
            else:
                source, dest = src.at[2 * peer[0] + peer[1]], land.at[slot]
            copies.append(pltpu.make_async_remote_copy(
                src_ref=source, dst_ref=dest, send_sem=send_sems.at[k], recv_sem=recv_sems.at[k], device_id=peer,
                device_id_type=MESH))
    return copies


def forward_to_sibling(slabs, name):
    n = len(slabs)

    def body(*refs):
        out_refs = refs[n:2 * n]
        send_sems, recv_sems = refs[2 * n + 1:]
        x, y, c = _position()
        copies = []
        for a in range(n):
            for slot, mask in enumerate(OTHER_CHIPS):
                _, held = _peer(mask)
                copies.append(pltpu.make_async_remote_copy(
                    src_ref=out_refs[a].at[held], dst_ref=out_refs[a].at[held], send_sem=send_sems.at[3 * a + slot],
                    recv_sem=recv_sems.at[3 * a + slot], device_id=(x, y, 1 - c), device_id_type=MESH))
        for cp in copies:
            cp.start()
        for cp in copies:
            cp.wait()
        refs[2 * n][...] = jnp.zeros_like(refs[2 * n])

    outs = pl.pallas_call(
        body, out_shape=tuple(jax.ShapeDtypeStruct(s.shape, s.dtype) for s in slabs) + (TOKEN,),
        in_specs=[ANY] * n, out_specs=(ANY,) * n + (pl.BlockSpec(memory_space=pltpu.VMEM),),
        input_output_aliases={i: i for i in range(n)},
        scratch_shapes=[pltpu.SemaphoreType.DMA((3 * n,)), pltpu.SemaphoreType.DMA((3 * n,))], name=name,
    )(*slabs)
    return outs[:n], outs[n]


def direct_exchange_start(arrays, mode, name):
    n = len(arrays)
    n_peers = len(_exchange_peers(mode))
    lands = [lax.empty((N_DEV,) + a.shape if mode in GATHER_MODES else (n_peers,) + a.shape[1:], a.dtype)
             for a in arrays]

    def body(*refs):
        src_refs, land_refs = refs[:n], refs[n:2 * n]
        send_sems, recv_sems = refs[2 * n], refs[2 * n + 1]
        token = refs[-1]
        for cp in _direct_copies(src_refs, land_refs, send_sems, recv_sems, mode):
            cp.start()
        token[...] = jnp.zeros_like(token)

    sems = pltpu.SemaphoreType.DMA((n_peers * n,))
    outs = pl.pallas_call(
        body, name=name,
        out_shape=(sems, sems) + tuple(pltpu.HBM(a.shape, a.dtype) for a in arrays)
        + tuple(pltpu.HBM(l.shape, l.dtype) for l in lands) + (TOKEN,),
        in_specs=[HBM] * (2 * n), out_specs=(SEM, SEM) + (HBM,) * (2 * n) + (pl.BlockSpec(memory_space=pltpu.VMEM),),
        input_output_aliases={i: 2 + i for i in range(2 * n)},
        compiler_params=pltpu.CompilerParams(has_side_effects=DATAFLOW_EFFECT),
    )(*[pltpu.with_memory_space_constraint(a, pltpu.HBM) for a in list(arrays) + lands])
    return outs[0], outs[1], outs[2:2 + n], outs[2 + n:2 + 2 * n], outs[-1]


def direct_exchange_wait(send_sems, recv_sems, arrays, lands, after, mode, name):
    n = len(arrays)

    def body(*refs):
        src_refs, land_refs = refs[:n], refs[n:2 * n]
        send_sems, recv_sems = refs[2 * n], refs[2 * n + 1]
        for cp in _direct_copies(src_refs, land_refs, send_sems, recv_sems, mode):
            cp.wait_send()
            cp.wait_recv()
        refs[-1][...] = jnp.zeros_like(refs[-1])

    outs = pl.pallas_call(
        body, name=name,
        out_shape=tuple(pltpu.HBM(a.shape, a.dtype) for a in arrays) + tuple(pltpu.HBM(l.shape, l.dtype) for l in lands)
        + (TOKEN,),
        in_specs=[HBM] * (2 * n) + [SEM, SEM, pl.BlockSpec(memory_space=pl.ANY)],
        out_specs=(HBM,) * (2 * n) + (pl.BlockSpec(memory_space=pltpu.VMEM),),
        input_output_aliases={i: i for i in range(2 * n)},
        compiler_params=pltpu.CompilerParams(has_side_effects=DATAFLOW_EFFECT),
    )(*arrays, *lands, send_sems, recv_sems, after)
    return outs[n:]


def adamw_direct(w, m, v, own, received, name):
    row_per_tile = w.shape[0] != 1
    rows, cols = (w.shape[0], w.shape[2]) if row_per_tile else w.shape[-2:]
    tr, tc = _tile2(rows, cols)

    def body(w_ref, m_ref, v_ref, own_ref, r_ref, g_ref, d_ref, nm_ref, nv_ref):
        gv = own_ref[0]
        for j in range(N_PEERS):
            gv = gv + r_ref[j].astype(F32)
        nm = ADAM_B1 * m_ref[...] + (1.0 - ADAM_B1) * gv
        nv = ADAM_B2 * v_ref[...] + (1.0 - ADAM_B2) * (gv * gv)
        m_hat = nm / (1.0 - ADAM_B1 ** ADAM_STEP)
        v_hat = nv / (1.0 - ADAM_B2 ** ADAM_STEP)
        g_ref[...] = gv
        d_ref[...] = -ADAM_LR * (m_hat / (jnp.sqrt(v_hat) + ADAM_EPS) + ADAM_WD * w_ref[...])
        nm_ref[...] = nm
        nv_ref[...] = nv

    if row_per_tile:
        one = pl.BlockSpec((tr, None, tc), lambda i, j: (i, 0, j))
    else:
        one = pl.BlockSpec((None, tr, tc), lambda i, j: (0, i, j))
    out = jax.ShapeDtypeStruct(w.shape, F32)
    return pl.pallas_call(
        body, grid=(rows // tr, cols // tc),
        in_specs=[one, one, one, pl.BlockSpec((1, tr, tc), lambda i, j: (0, i, j)),
                  pl.BlockSpec((N_PEERS, tr, tc), lambda i, j: (0, i, j))],
        out_specs=(one,) * 4, out_shape=(out,) * 4, name=name, compiler_params=_params(2),
    )(w, m, v, own, received)


def all_reduce_small(vals):
    rows, width = vals.shape

    def body(x_ref, out_ref, all_ref, send_sems, recv_sems):
        x, y, c = _position()
        me, sibling = (x, y, c), (x, y, 1 - c)
        chips = [(1 - x, y), (x, 1 - y), (1 - x, 1 - y)]

        def slab(px, py, pc):
            return all_ref.at[4 * px + 2 * py + pc]

        def copy(k, block, to, src=None):
            return pltpu.make_async_remote_copy(
                src_ref=slab(*block) if src is None else src, dst_ref=slab(*block),
                send_sem=send_sems.at[k], recv_sem=recv_sems.at[k], device_id=to, device_id_type=MESH)

        first = [copy(0, me, sibling, src=x_ref)]
        first += [copy(1 + j, me, (*chip, c), src=x_ref) for j, chip in enumerate(chips)]
        for cp in first:
            cp.start()
        all_ref[4 * x + 2 * y + c] = x_ref[...]
        passed = [copy(4 + j, (*chip, c), sibling) for j, chip in enumerate(chips)]
        for j, chip in enumerate(chips):
            copy(1 + j, (*chip, c), me).wait_recv()
            passed[j].start()
        copy(0, sibling, me).wait_recv()
        for j, chip in enumerate(chips):
            copy(4 + j, (*chip, 1 - c), me).wait_recv()
        for cp in first + passed:
            cp.wait_send()
        total = all_ref[0]
        for d in range(1, N_DEV):
            total = total + all_ref[d]
        out_ref[...] = total

    vmem = pl.BlockSpec(memory_space=pltpu.VMEM)
    return pl.pallas_call(
        body, out_shape=(jax.ShapeDtypeStruct(vals.shape, F32), jax.ShapeDtypeStruct((N_DEV, rows, width), F32)),
        in_specs=[vmem], out_specs=(vmem, vmem),
        scratch_shapes=[pltpu.SemaphoreType.DMA((7,)), pltpu.SemaphoreType.DMA((7,))], name="small_allreduce",
    )(vals)[0]


def adamw(w, g, m, v, name):
    shape = w.shape
    w2, g2, m2, v2 = [a.reshape((-1, shape[-1])) for a in (w, g, m, v)]
    rows, cols = w2.shape
    tr = 256 if rows % 256 == 0 else rows

    def body(w_ref, g_ref, m_ref, v_ref, d_ref, nm_ref, nv_ref):
        gv = g_ref[...]
        nm = ADAM_B1 * m_ref[...] + (1.0 - ADAM_B1) * gv
        nv = ADAM_B2 * v_ref[...] + (1.0 - ADAM_B2) * (gv * gv)
        m_hat = nm / (1.0 - ADAM_B1 ** ADAM_STEP)
        v_hat = nv / (1.0 - ADAM_B2 ** ADAM_STEP)
        d_ref[...] = -ADAM_LR * (m_hat / (jnp.sqrt(v_hat) + ADAM_EPS) + ADAM_WD * w_ref[...])
        nm_ref[...] = nm
        nv_ref[...] = nv

    blk = pl.BlockSpec((tr, cols), lambda i: (i, 0))
    out = jax.ShapeDtypeStruct((rows, cols), F32)
    outs = pl.pallas_call(
        body, grid=(rows // tr,), in_specs=[blk] * 4, out_specs=(blk,) * 3, out_shape=(out,) * 3,
        name=name, compiler_params=_params(1),
    )(w2, g2, m2, v2)
    return tuple(o.reshape(shape) for o in outs)


def adamw_summed(w, m, v, grads, from_sibling, received, me, name):
    rows, cols = w.shape[-2:]
    tr, tc = _tile2(rows, cols)

    def body(me_ref, w_ref, m_ref, v_ref, own_ref, sib_ref, r_ref, g_ref, d_ref, nm_ref, nv_ref):
        gv = own_ref[0] + sib_ref[0]
        for j in range(3):
            gv = gv + r_ref[j].astype(F32)
        nm = ADAM_B1 * m_ref[0] + (1.0 - ADAM_B1) * gv
        nv = ADAM_B2 * v_ref[0] + (1.0 - ADAM_B2) * (gv * gv)
        m_hat = nm / (1.0 - ADAM_B1 ** ADAM_STEP)
        v_hat = nv / (1.0 - ADAM_B2 ** ADAM_STEP)
        g_ref[0] = gv
        d_ref[0] = -ADAM_LR * (m_hat / (jnp.sqrt(v_hat) + ADAM_EPS) + ADAM_WD * w_ref[0])
        nm_ref[0] = nm
        nv_ref[0] = nv

    one = pl.BlockSpec((1, tr, tc), lambda i, j, me_ref: (0, i, j))
    out = jax.ShapeDtypeStruct((1, rows, cols), F32)
    return pl.pallas_call(
        body,
        grid_spec=pltpu.PrefetchScalarGridSpec(
            num_scalar_prefetch=1, grid=(rows // tr, cols // tc),
            in_specs=[one, one, one, pl.BlockSpec((1, tr, tc), lambda i, j, me_ref: (me_ref[0], i, j)),
                      pl.BlockSpec((1, tr, tc), lambda i, j, me_ref: (me_ref[1], i, j)),
                      pl.BlockSpec((3, tr, tc), lambda i, j, me_ref: (0, i, j))],
            out_specs=(one,) * 4),
        out_shape=(out,) * 4, name=name, compiler_params=_params(2),
    )(me, w, m, v, grads, from_sibling, received)


SMALL_VECTORS = ("ffn1_norm", "mix_norm", "ffn2_norm", "final_norm")


def _pack_small(gs):
    row = jnp.concatenate([gs["gdn_a_log"].reshape(-1), gs["gdn_dt_bias"].reshape(-1), gs["gdn_out_norm"].reshape(-1)])
    rows = [gs[n].reshape(1, D_MODEL) for n in SMALL_VECTORS]
    rows.append(jnp.pad(row, (0, D_MODEL - row.shape[0])).reshape(1, D_MODEL))
    rows.append(gs["gdn_conv_w"].reshape(-1, D_MODEL))
    packed = jnp.concatenate(rows, axis=0)
    return jnp.pad(packed, ((0, SMALL_ROWS - packed.shape[0]), (0, 0)))


def _unpack_small(packed):
    out = {n: packed[i].reshape(1, D_MODEL) for i, n in enumerate(SMALL_VECTORS)}
    row = packed[len(SMALL_VECTORS)]
    out["gdn_a_log"] = row[:GDN_HEADS].reshape(1, GDN_HEADS)
    out["gdn_dt_bias"] = row[GDN_HEADS:2 * GDN_HEADS].reshape(1, GDN_HEADS)
    out["gdn_out_norm"] = row[2 * GDN_HEADS:2 * GDN_HEADS + GDN_HEAD_DIM].reshape(1, GDN_HEAD_DIM)
    first = len(SMALL_VECTORS) + 1
    out["gdn_conv_w"] = packed[first:first + GDN_CONV * 3].reshape(GDN_CONV, 3 * GDN_WIDTH)
    return out


WEIGHTS = ("ffn1_norm", "ffn1_w_gate", "ffn1_w_up", "ffn1_w_down", "mix_norm", "w_in", "gdn_conv_w", "gdn_a_log",
           "gdn_dt_bias", "gdn_out_norm", "w_branch_a", "w_branch_b", "w_out", "ffn2_norm", "ffn2_w_gate",
           "ffn2_w_up", "ffn2_w_down", "final_norm")


def kernel(x, ffn1_norm, ffn1_w_gate, ffn1_w_up, ffn1_w_down, mix_norm, w_in, gdn_conv_w, gdn_a_log, gdn_dt_bias, gdn_out_norm, w_branch_a, w_branch_b, w_out, ffn2_norm, ffn2_w_gate, ffn2_w_up, ffn2_w_down, final_norm, loss_target, m_ffn1_norm, m_ffn1_w_gate, m_ffn1_w_up, m_ffn1_w_down, m_mix_norm, m_w_in, m_gdn_conv_w, m_gdn_a_log, m_gdn_dt_bias, m_gdn_out_norm, m_w_branch_a, m_w_branch_b, m_w_out, m_ffn2_norm, m_ffn2_w_gate, m_ffn2_w_up, m_ffn2_w_down, m_final_norm, v_ffn1_norm, v_ffn1_w_gate, v_ffn1_w_up, v_ffn1_w_down, v_mix_norm, v_w_in, v_gdn_conv_w, v_gdn_a_log, v_gdn_dt_bias, v_gdn_out_norm, v_w_branch_a, v_w_branch_b, v_w_out, v_ffn2_norm, v_ffn2_w_gate, v_ffn2_w_up, v_ffn2_w_down, v_final_norm):
    given = dict(locals())
    px, py, pc = _position()
    big_names = list(BIG_WEIGHTS)

    def shard_view(a, n):
        if n == "w_in":
            return a.transpose(2, 0, 1)
        return a.transpose(0, 2, 1) if n in TRANSPOSED else a

    def shard_unview(a, n):
        if n == "w_in":
            return a.transpose(1, 2, 0)
        return a.transpose(0, 2, 1) if n in TRANSPOSED else a

    me = 4 * px + 2 * py + pc
    me_index = me.astype(jnp.int32).reshape(1)
    late = [n for n in big_names if n.startswith("ffn2")]
    early = [n for n in big_names if n not in late]
    shards = {n: shard_view(given[n], n).reshape(given[n].shape[-1 if n in TRANSPOSED else -2], -1).astype(BF16)
              for n in big_names}
    first = [n for n in early if n.startswith("ffn1")]
    middle = [n for n in early if n not in first]
    first_slabs, first_done = all_gather_shards([shards[n] for n in first], "gather_ffn1")
    shards["gdn_conv_w"] = gdn_conv_w[0]
    middle_all = middle + ["gdn_conv_w"]
    middle_gather = direct_exchange_start([_after(shards[n], first_done) for n in middle_all], "near",
                                          "gather_mixer_start")
    ffn1_norm = _after(ffn1_norm, middle_gather[4])
    def in_chunks(slabs):
        return slabs.reshape(-1, FFN_CHUNK, D_MODEL)

    def in_slabs(chunks):
        return chunks.reshape(N_DEV, -1, D_MODEL)

    w = {n: in_chunks(slab) for n, slab in zip(first, first_slabs)}
    x1, ffn1_saved = ffn_forward(x[0], ffn1_norm, w, "ffn1")
    near_lands = direct_exchange_wait(*middle_gather[:4], x1, "near", "gather_mixer_wait")[:-1]
    near_lands = [lax.dynamic_update_slice(land, shards[n][None], (me, 0, 0)) for n, land in zip(middle_all, near_lands)]
    middle_slabs, middle_done = forward_to_sibling(near_lands, "gather_mixer_forward")
    gathered = dict(zip(middle_all, middle_slabs))
    late_gather = direct_exchange_start([_after(shards[n], middle_done) for n in late], "gather", "gather_ffn2_start")
    w["w_in_t"] = gathered["w_in"].reshape(-1, D_MODEL)
    w["w_branch_a"] = gathered["w_branch_a"].transpose(1, 0, 2).reshape(-1, D_MODEL)
    w["w_branch_b"] = gathered["w_branch_b"].reshape(D_MODEL, D_MODEL)
    w["w_out"] = gathered["w_out"].reshape(D_MODEL, D_MODEL)
    conv_full = gathered["gdn_conv_w"].transpose(1, 0, 2).reshape(GDN_CONV, 3 * GDN_WIDTH)
    small = dict(mix_norm=_after(mix_norm, late_gather[4]), gdn_a_log=gdn_a_log, gdn_dt_bias=gdn_dt_bias,
                 gdn_out_norm=gdn_out_norm, gdn_conv_w=conv_full)

    x2, mixer_saved = mixer_forward(x1, w, small)
    late_lands = direct_exchange_wait(*late_gather[:4], x2, "gather", "gather_ffn2_wait")
    for n, land in zip(late, late_lands):
        w[n] = in_chunks(lax.dynamic_update_slice(land, shards[n][None], (me, 0, 0)))
    x3, ffn2_saved = ffn_forward(x2, ffn2_norm, w, "ffn2")
    loss_local, dx3, g_final = loss_head(x3, loss_target[0], final_norm.reshape(1, D_MODEL))
    loss = lax.psum(loss_local, ("x", "y", "c"))
    dx2, g_ffn2_norm, (dw2, dw2_f32) = ffn_backward(dx3, ffn2_saved, ffn2_norm, w, "ffn2", with_payload=True)
    late_scatter = direct_exchange_start([in_slabs(g) for g in dw2], "scatter", "rs_ffn2_start")
    w_after = dict(w, w_out=_after(w["w_out"], late_scatter[4]))
    dx1, g_w = mixer_backward(dx2, mixer_saved, w_after, small)
    middle = ["w_in", "w_branch_a", "w_branch_b", "w_out"]
    g_big = dict(w_in=g_w["w_in_t"].reshape(N_DEV, -1, D_MODEL),
                 w_branch_a=g_w["w_branch_a"].reshape(-1, N_DEV, D_MODEL // N_DEV).transpose(1, 0, 2),
                 w_branch_b=g_w["w_branch_b"].reshape(N_DEV, -1, D_MODEL),
                 w_out=g_w["w_out"].reshape(N_DEV, -1, D_MODEL))
    own = {n: lax.dynamic_index_in_dim(in_slabs(g), me, 0, keepdims=True) for n, g in zip(late, dw2_f32)}
    own.update({n: lax.dynamic_index_in_dim(g_big[n], me, 0, keepdims=True) for n in middle[1:]})
    in_rows = g_w["w_in_t"].shape[0] // N_DEV
    own["w_in"] = lax.dynamic_slice(g_w["w_in_t"], (me * in_rows, 0), (in_rows, D_MODEL))[None]
    middle_scatter = direct_exchange_start([g_big[n].astype(BF16) for n in middle], "scatter", "rs_mixer_start")
    grad_x, g_ffn1_norm, dw1 = ffn_backward(dx1, ffn1_saved, _after(ffn1_norm, middle_scatter[4]), w, "ffn1")
    g_small = dict(ffn1_norm=g_ffn1_norm, ffn2_norm=g_ffn2_norm, final_norm=g_final,
                   **{n: g_w[n] for n in ("mix_norm", "gdn_a_log", "gdn_dt_bias", "gdn_out_norm", "gdn_conv_w")})

    first = [n for n in early if n.startswith("ffn1")]
    g_list = [in_slabs(g) for g in dw1]
    core = pc.astype(jnp.int32).reshape(1)
    me_and_chip = jnp.stack([me, 2 * px + py]).astype(jnp.int32)
    from_sibling = exchange_with_sibling(g_list)
    partials = [add_sibling(g, r, core, "rs_add_" + n) for n, g, r in zip(first, g_list, from_sibling)]
    first_chips = direct_exchange_start(partials, "chips", "rs_ffn1_start")

    def state_of(n):
        return [shard_view(given[p + n], n) for p in ("", "m_", "v_")]

    results = {}
    late_received = direct_exchange_wait(*late_scatter[:4], first_chips[4], "scatter", "rs_ffn2_wait")
    middle_received = direct_exchange_wait(*middle_scatter[:4], first_chips[4], "scatter", "rs_mixer_wait")
    for n, recv in zip(late + middle, list(late_received[:-1]) + list(middle_received[:-1])):
        outs = adamw_direct(*state_of(n), own[n], recv, "adamw_" + n)
        results[n] = tuple(shard_unview(o, n) for o in outs)

    done = results["w_out"][1]
    from_chips = direct_exchange_wait(*first_chips[:4], done, "chips", "rs_ffn1_wait")
    for n, g, sib, recv in zip(first, g_list, from_sibling, from_chips):
        outs = adamw_summed(*state_of(n), g, sib, recv, me_and_chip, "adamw_" + n)
        results[n] = tuple(shard_unview(o, n) for o in outs)

    small_sum = _unpack_small(all_reduce_small(_after(_pack_small(g_small), from_chips[-1])))
    conv_cols = CONV_SHARD[1]
    small_sum["gdn_conv_w"] = lax.dynamic_slice(small_sum["gdn_conv_w"], (0, me * conv_cols), (GDN_CONV, conv_cols))
    for n in WEIGHTS:
        if n not in results:
            g = small_sum[n].reshape(given[n].shape)
            results[n] = (g,) + adamw(given[n], g, given["m_" + n], given["v_" + n], "adamw_" + n)

    outs = [[results[n][i] for n in WEIGHTS] for i in range(4)]
    return (loss, grad_x[None], *outs[0], *outs[1], *outs[2], *outs[3])
```

```python
import jax
import jax.numpy as jnp
from jax import lax
from jax.experimental import pallas as pl
from jax.experimental.pallas import tpu as pltpu

F32 = jnp.float32
BF16 = jnp.bfloat16
HI = lax.Precision.HIGHEST
MESH = pl.DeviceIdType.MESH

N_DEV = 8
D_MODEL = 1024
EPS = 1e-6
ROPE_THETA = 10000.0
DSW_DILATIONS = (1, 4, 16)
DSW_HEADS_PER_GROUP = 4
DSW_HEAD_DIM = 64
DSW_BLOCK = 128
GDN_HEADS = 8
GDN_HEAD_DIM = 128
GDN_WIDTH = 1024
GDN_CONV = 4
GDN_CHUNK = 64

ADAM_LR = 0.001
ADAM_B1 = 0.9
ADAM_B2 = 0.999
ADAM_EPS = 1e-08
ADAM_WD = 0.01
ADAM_STEP = 10

VMEM_LIMIT_BYTES = 56 * 1024 * 1024
LANES = 128

NN = (((1,), (0,)), ((), ()))
NT = (((1,), (1,)), ((), ()))
TN = (((0,), (0,)), ((), ()))


def _params(n_grid):
    return pltpu.CompilerParams(dimension_semantics=("arbitrary",) * n_grid, vmem_limit_bytes=VMEM_LIMIT_BYTES)


def _tile(n, pref):
    best = None
    t = LANES
    while t <= min(n, pref):
        if n % t == 0:
            best = t
        t += LANES
    return n if best is None else best


def _weight_grad(a, g, name):
    n_tokens, m = a.shape
    n = g.shape[1]
    tm, tn = _tile(m, 512), _tile(n, 512)

    def body(a_ref, g_ref, o_ref):
        o_ref[...] = lax.dot_general(a_ref[...].astype(BF16), g_ref[...].astype(BF16), TN, preferred_element_type=F32)

    return pl.pallas_call(
        body, grid=(m // tm, n // tn),
        in_specs=[pl.BlockSpec((n_tokens, tm), lambda i, j: (0, i)), pl.BlockSpec((n_tokens, tn), lambda i, j: (0, j))],
        out_specs=pl.BlockSpec((tm, tn), lambda i, j: (i, j)),
        out_shape=jax.ShapeDtypeStruct((m, n), F32), name=name, compiler_params=_params(2),
    )(a, g)


def _rw_specs(arrs, tm, nblk):
    return [pl.BlockSpec((tm, a.shape[1] // nblk), lambda i, j: (i, j)) for a in arrs]


def _rowwise_fwd(fn, name, rows, consts, params, tm, nblk):
    n_rows = rows[0].shape[0]
    tm = min(tm, n_rows)
    ins = list(rows) + list(consts)
    avals = [jax.ShapeDtypeStruct((tm, a.shape[1] // nblk), a.dtype) for a in ins]
    avals += [jax.ShapeDtypeStruct(p.shape, p.dtype) for p in params]
    out_avals = jax.eval_shape(fn, *avals)
    n_in = len(ins) + len(params)

    def body(*refs):
        outs = fn(*[r[...] for r in refs[:n_in]])
        for r, o in zip(refs[n_in:], outs):
            r[...] = o.astype(r.dtype)

    return pl.pallas_call(
        body, grid=(n_rows // tm, nblk),
        in_specs=_rw_specs(ins, tm, nblk) + [pl.BlockSpec(p.shape, lambda i, j: (0, 0)) for p in params],
        out_specs=tuple(pl.BlockSpec((tm, o.shape[1]), lambda i, j: (i, j)) for o in out_avals),
        out_shape=tuple(jax.ShapeDtypeStruct((n_rows, o.shape[1] * nblk), o.dtype) for o in out_avals),
        name=name, compiler_params=_params(2),
    )(*ins, *params)


def _rowwise_bwd(fn, name, rows, consts, params, cts, tm, nblk):
    n_rows = rows[0].shape[0]
    tm = min(tm, n_rows)
    nr, nc, npar, nct = len(rows), len(consts), len(params), len(cts)

    def body(*refs):
        rv = [r[...] for r in refs[:nr]]
        cv = [r[...] for r in refs[nr:nr + nc]]
        pv = [r[...] for r in refs[nr + nc:nr + nc + npar]]
        ctv = [r[...] for r in refs[nr + nc + npar:nr + nc + npar + nct]]
        outs = refs[nr + nc + npar + nct:]
        _, vjp = jax.vjp(lambda *d: fn(*d[:nr], *cv, *d[nr:]), *rv, *pv)
        grads = vjp(tuple(ctv))
        for k in range(nr):
            outs[k][...] = grads[k]
        first = jnp.logical_and(pl.program_id(0) == 0, pl.program_id(1) == 0)
        for k in range(npar):
            ref = outs[nr + k]

            @pl.when(first)
            def _(ref=ref):
                ref[...] = jnp.zeros_like(ref)

            ref[...] += grads[nr + k]

    ins = list(rows) + list(consts)
    return pl.pallas_call(
        body, grid=(n_rows // tm, nblk),
        in_specs=(_rw_specs(ins, tm, nblk) + [pl.BlockSpec(p.shape, lambda i, j: (0, 0)) for p in params]
                  + _rw_specs(cts, tm, nblk)),
        out_specs=tuple(_rw_specs(rows, tm, nblk) + [pl.BlockSpec(p.shape, lambda i, j: (0, 0)) for p in params]),
        out_shape=tuple([jax.ShapeDtypeStruct(a.shape, F32) for a in rows]
                        + [jax.ShapeDtypeStruct(p.shape, F32) for p in params]),
        name=name, compiler_params=_params(2),
    )(*ins, *params, *cts)


def _merge_fn(ga, gb, pa, pb):
    return (jax.nn.sigmoid(ga) * pa + jax.nn.sigmoid(gb) * pb,)


def _outnorm_gate_fn(o, gate, gain):
    y = o * lax.rsqrt(jnp.mean(o * o, axis=-1, keepdims=True) + EPS) * gain
    return (y * (gate * jax.nn.sigmoid(gate)),)


def _beta_decay_fn(beta_raw, decay_raw, a_log, dt_bias):
    z = decay_raw + dt_bias
    softplus = jnp.maximum(z, 0.0) + jnp.log(1.0 + jnp.exp(-jnp.abs(z)))
    g = -jnp.exp(a_log) * softplus
    rows = g.shape[0]
    ii = lax.broadcasted_iota(jnp.int32, (rows, rows), 0)
    jj = lax.broadcasted_iota(jnp.int32, (rows, rows), 1)
    same_chunk_before = jnp.logical_and(jj <= ii, jj // GDN_CHUNK == ii // GDN_CHUNK).astype(F32)
    gcum = lax.dot_general(same_chunk_before, g, NN, precision=HI, preferred_element_type=F32)
    return jax.nn.sigmoid(beta_raw), gcum


def _combine_fn(o0, o1, o2, l0, l1, l2):
    m = lax.stop_gradient(jnp.maximum(jnp.maximum(l0, l1), l2))
    e0, e1, e2 = jnp.exp(l0 - m), jnp.exp(l1 - m), jnp.exp(l2 - m)
    return ((e0 * o0 + e1 * o1 + e2 * o2) / (e0 + e1 + e2),)


def _loss_fn(x, target, gain):
    y = x * lax.rsqrt(jnp.mean(x * x, axis=-1, keepdims=True) + EPS) * gain
    err = y - target
    return (0.5 * jnp.mean(err * err, axis=-1, keepdims=True),)


def _rotate(v, cos, sin):
    half = DSW_HEAD_DIM // 2
    lane = lax.broadcasted_iota(jnp.int32, cos.shape, 1)
    low = (lane % DSW_HEAD_DIM) < half
    slabs = []
    for s in range(v.shape[1] // LANES):
        x = v[:, s * LANES:(s + 1) * LANES]
        swapped = jnp.where(low, pltpu.roll(x, LANES - half, 1), pltpu.roll(x, half, 1))
        slabs.append(x * cos + swapped * sin)
    return jnp.concatenate(slabs, axis=1)


def _rope_tables(n_tokens):
    half = DSW_HEAD_DIM // 2
    inv_freq = ROPE_THETA ** (-jnp.arange(half, dtype=F32) / half)
    ang = jnp.arange(n_tokens, dtype=F32)[:, None] * inv_freq[None, :]
    cos, sin = jnp.cos(ang), jnp.sin(ang)
    return jnp.tile(jnp.concatenate([cos, cos], 1), (1, 2)), jnp.tile(jnp.concatenate([-sin, sin], 1), (1, 2))


def _attn_probs(q, kp, kc, group, n, n_blocks):
    blk = DSW_BLOCK
    k = _each(lambda a, b: jnp.concatenate([a, b], axis=0).astype(BF16), kp, kc)
    s = _each(lambda a, b: lax.dot_general(a.astype(BF16), b, NT, preferred_element_type=F32)
              * (DSW_HEAD_DIM ** -0.5), q, k)
    per_seq = [n_blocks // d for d in DSW_DILATIONS]
    blocks_per_seq = jnp.where(group == 0, per_seq[0], jnp.where(group == 1, per_seq[1], per_seq[2]))
    first = (n % blocks_per_seq) == 0
    qi = lax.broadcasted_iota(jnp.int32, (blk, 2 * blk), 0)
    kj = lax.broadcasted_iota(jnp.int32, (blk, 2 * blk), 1)
    dist = qi + blk - kj
    valid = (dist >= 0) & (dist <= blk) & jnp.logical_or(kj >= blk, jnp.logical_not(first))
    s = _each(lambda a: jnp.where(valid, a, -1e30), s)
    m = _each(lambda a: jnp.max(a, axis=-1, keepdims=True), s)
    p = _each(lambda a, b: jnp.exp(a - b), s, m)
    l = _each(lambda a: jnp.sum(a, axis=-1, keepdims=True), p)
    return _each(lambda a, b: a / b, p, l), _each(lambda a, b: a + jnp.log(b), m, l), k


GROUP_WIDTH = DSW_HEADS_PER_GROUP * DSW_HEAD_DIM


def _attn_specs(n_tokens):
    blk = DSW_BLOCK
    cur = pl.BlockSpec((1, blk, GROUP_WIDTH), lambda g, n: (g, n, 0))
    prev = pl.BlockSpec((1, blk, GROUP_WIDTH), lambda g, n: (g, jnp.maximum(n - 1, 0), 0))
    return cur, prev


def _heads_of(ref):
    x = ref[0]
    return [x[:, h * DSW_HEAD_DIM:(h + 1) * DSW_HEAD_DIM] for h in range(DSW_HEADS_PER_GROUP)]


def _group_of(heads):
    return jnp.concatenate(heads, axis=1)


def _attn_fwd(q, k, v):
    n_groups, n_tokens, _ = q.shape
    cur, prev = _attn_specs(n_tokens)

    def body(q_ref, kp_ref, kc_ref, vp_ref, vc_ref, o_ref, l_ref):
        p, lse, _ = _attn_probs(_heads_of(q_ref), _heads_of(kp_ref), _heads_of(kc_ref),
                                pl.program_id(0), pl.program_id(1), n_tokens // DSW_BLOCK)
        vv = _each(lambda a, b: jnp.concatenate([a, b], axis=0).astype(BF16), _heads_of(vp_ref), _heads_of(vc_ref))
        o = _each(lambda a, b: lax.dot_general(a.astype(BF16), b, NN, preferred_element_type=F32), p, vv)
        lse_wide = _each(lambda a: jnp.broadcast_to(a, (DSW_BLOCK, DSW_HEAD_DIM)), lse)
        o_ref[0] = _group_of(o)
        l_ref[0] = _group_of(lse_wide)

    return pl.pallas_call(
        body, grid=(n_groups, n_tokens // DSW_BLOCK), in_specs=[cur, prev, cur, prev, cur],
        out_specs=(cur, cur), out_shape=(jax.ShapeDtypeStruct(q.shape, F32), jax.ShapeDtypeStruct(q.shape, F32)),
        name="attn_fwd", compiler_params=_params(2),
    )(q, k, k, v, v)


def _attn_bwd(q, k, v, do, dlse):
    n_groups, n_tokens, _ = q.shape
    nblk = n_tokens // DSW_BLOCK
    cur, prev = _attn_specs(n_tokens)
    part = pl.BlockSpec((1, 1, 2 * DSW_BLOCK, GROUP_WIDTH), lambda g, n: (g, n, 0, 0))
    scale = DSW_HEAD_DIM ** -0.5

    def body(q_ref, kp_ref, kc_ref, vp_ref, vc_ref, do_ref, dl_ref, dq_ref, dk_ref, dv_ref):
        qs = _heads_of(q_ref)
        p, _, kb = _attn_probs(qs, _heads_of(kp_ref), _heads_of(kc_ref), pl.program_id(0), pl.program_id(1), nblk)
        qb = _each(lambda a: a.astype(BF16), qs)
        vv = _each(lambda a, b: jnp.concatenate([a, b], axis=0).astype(BF16), _heads_of(vp_ref), _heads_of(vc_ref))
        dob = _each(lambda a: a.astype(BF16), _heads_of(do_ref))
        dp = _each(lambda a, b: lax.dot_general(a, b, NT, preferred_element_type=F32), dob, vv)
        dv = _each(lambda a, b: lax.dot_general(a.astype(BF16), b, TN, preferred_element_type=F32), p, dob)
        dl = _each(lambda a: jnp.sum(a, axis=-1, keepdims=True), _heads_of(dl_ref))
        ds = _each(lambda a, b, c: (a * (b - jnp.sum(b * a, axis=-1, keepdims=True) + c) * scale).astype(BF16),
                   p, dp, dl)
        dq = _each(lambda a, b: lax.dot_general(a, b, NN, preferred_element_type=F32), ds, kb)
        dk = _each(lambda a, b: lax.dot_general(a, b, TN, preferred_element_type=F32), ds, qb)
        dq_ref[0] = _group_of(dq)
        dk_ref[0, 0] = _group_of(dk)
        dv_ref[0, 0] = _group_of(dv)

    partial_shape = jax.ShapeDtypeStruct((n_groups, nblk, 2 * DSW_BLOCK, GROUP_WIDTH), F32)
    dq, dkp, dvp = pl.pallas_call(
        body, grid=(n_groups, nblk), in_specs=[cur, prev, cur, prev, cur, cur, cur],
        out_specs=(cur, part, part), out_shape=(jax.ShapeDtypeStruct(q.shape, F32), partial_shape, partial_shape),
        name="attn_bwd", compiler_params=_params(2),
    )(q, k, k, v, v, do, dlse)

    def fold(partial):
        own = partial[:, :, DSW_BLOCK:]
        from_next = jnp.pad(partial[:, 1:, :DSW_BLOCK], ((0, 0), (0, 1), (0, 0), (0, 0)))
        return (own + from_next).reshape(n_groups, n_tokens, GROUP_WIDTH)

    return dq, fold(dkp), fold(dvp)


def _to_heads(a):
    n_tokens = a.shape[0]
    outs = []
    for gi, d in enumerate(DSW_DILATIONS):
        blk = a[:, gi * GROUP_WIDTH:(gi + 1) * GROUP_WIDTH].reshape(n_tokens // d, d, GROUP_WIDTH)
        outs.append(blk.transpose(1, 0, 2).reshape(1, n_tokens, GROUP_WIDTH))
    return jnp.concatenate(outs, 0)


def _from_heads(a):
    n_tokens = a.shape[1]
    return [a[gi].reshape(d, n_tokens // d, GROUP_WIDTH).transpose(1, 0, 2).reshape(n_tokens, GROUP_WIDTH)
            for gi, d in enumerate(DSW_DILATIONS)]


CONV_TILE = 512


def _shift_down(x, k, rows):
    return x if k == 0 else jnp.where(rows >= k, pltpu.roll(x, k, 0), 0.0)


def _shift_up(x, k, rows):
    n = x.shape[0]
    return x if k == 0 else jnp.where(rows < n - k, pltpu.roll(x, n - k, 0), 0.0)


def _conv_pre(x, w):
    rows = lax.broadcasted_iota(jnp.int32, x.shape, 0)
    acc = x * w[GDN_CONV - 1:GDN_CONV]
    for k in range(1, GDN_CONV):
        acc = acc + _shift_down(x, k, rows) * w[GDN_CONV - 1 - k:GDN_CONV - k]
    return acc, rows


def _conv_fwd(x, w):
    n_tokens, width = x.shape
    big = pl.BlockSpec((n_tokens, CONV_TILE), lambda j: (0, j))
    wsp = pl.BlockSpec((GDN_CONV, CONV_TILE), lambda j: (0, j))

    def body(x_ref, w_ref, o_ref):
        acc, _ = _conv_pre(x_ref[...], w_ref[...])
        o_ref[...] = acc * jax.nn.sigmoid(acc)

    return pl.pallas_call(
        body, grid=(width // CONV_TILE,), in_specs=[big, wsp], out_specs=big,
        out_shape=jax.ShapeDtypeStruct(x.shape, F32), name="conv_fwd", compiler_params=_params(1),
    )(x, w)


def _conv_bwd(x, w, dy):
    n_tokens, width = x.shape
    big = pl.BlockSpec((n_tokens, CONV_TILE), lambda j: (0, j))
    wsp = pl.BlockSpec((GDN_CONV, CONV_TILE), lambda j: (0, j))

    def body(x_ref, w_ref, dy_ref, dx_ref, dw_ref):
        xv, wv = x_ref[...], w_ref[...]
        acc, rows = _conv_pre(xv, wv)
        sg = jax.nn.sigmoid(acc)
        dacc = dy_ref[...] * (sg + acc * sg * (1.0 - sg))
        dx = dacc * wv[GDN_CONV - 1:GDN_CONV]
        for k in range(1, GDN_CONV):
            dx = dx + _shift_up(dacc, k, rows) * wv[GDN_CONV - 1 - k:GDN_CONV - k]
        dx_ref[...] = dx.astype(BF16)
        for k in range(GDN_CONV):
            dw_ref[GDN_CONV - 1 - k:GDN_CONV - k, :] = jnp.sum(dacc * _shift_down(xv, k, rows), axis=0, keepdims=True)

    return pl.pallas_call(
        body, grid=(width // CONV_TILE,), in_specs=[big, wsp, big], out_specs=(big, wsp),
        out_shape=(jax.ShapeDtypeStruct(x.shape, BF16), jax.ShapeDtypeStruct(w.shape, F32)),
        name="conv_bwd", compiler_params=_params(1),
    )(x, w, dy)


def _dot3(a, b, dn=NN):
    return lax.dot_general(a, b, dn, precision=lax.Precision.HIGH, preferred_element_type=F32)


def _bf16_dot(a, b, dn):
    return lax.dot_general(a.astype(BF16), b.astype(BF16), dn, preferred_element_type=F32)


_DOT_GRADS = {NN: (("g", "b", NT), ("a", "g", TN)), NT: (("g", "b", NN), ("g", "a", TN)),
              TN: (("b", "g", NT), ("a", "g", NN))}


def _make_bdot(dn):
    @jax.custom_vjp
    def op(a, b):
        return _bf16_dot(a, b, dn)

    def fwd(a, b):
        return op(a, b), (a, b)

    def bwd(saved, g):
        vals = dict(a=saved[0], b=saved[1], g=g)
        return tuple(_bf16_dot(vals[x], vals[y], form) for x, y, form in _DOT_GRADS[dn])

    op.defvjp(fwd, bwd)
    return op


_BDOTS = {dn: _make_bdot(dn) for dn in (NN, NT, TN)}


def _bdot(a, b, dn=NN):
    return _BDOTS[dn](a, b)


def _each(fn, *lists):
    return [fn(*items) for items in zip(*lists)]


@jax.custom_vjp
def _known_inverse(m, inverse):
    return inverse


def _known_inverse_fwd(m, inverse):
    return inverse, inverse


def _known_inverse_bwd(inverse, d_inverse):
    return -_dot3(_dot3(inverse, d_inverse, TN), inverse, NT), jnp.zeros_like(inverse)


_known_inverse.defvjp(_known_inverse_fwd, _known_inverse_bwd)


def _gdn_chunks(q, k, v, b, gcum, state, inverse=None):
    c = GDN_CHUNK
    ii = lax.broadcasted_iota(jnp.int32, (c, c), 0)
    jj = lax.broadcasted_iota(jnp.int32, (c, c), 1)
    qn = _each(lambda x: x * lax.rsqrt(jnp.sum(x * x, axis=-1, keepdims=True) + EPS) * (GDN_HEAD_DIM ** -0.5), q)
    kn = _each(lambda x: x * lax.rsqrt(jnp.sum(x * x, axis=-1, keepdims=True) + EPS), k)
    gcum_i = _each(lambda x: jnp.broadcast_to(x, (c, c)), gcum)
    gcum_j = _each(jnp.transpose, gcum_i)
    decay = _each(lambda x, y: jnp.exp(jnp.where(jj <= ii, x - y, -1e30)), gcum_i, gcum_j)
    g_last = _each(lambda x: x[c - 1:c, :], gcum)
    e_gcum = _each(jnp.exp, gcum)
    kbeta = _each(lambda x, y: x * y, kn, b)
    vbeta = _each(lambda x, y: x * y, v, b)
    m = _each(lambda x, y, d: jnp.where(jj < ii, _bdot(x, y, NT) * d, 0.0), kbeta, kn, decay)
    if inverse is not None:
        inv = _each(_known_inverse, m, inverse)
    else:
        eye = (ii == jj).astype(F32)
        inv = _each(lambda x: eye - x, m)
        power = _each(lambda x: _dot3(x, x), m)
        for step in range(5):
            inv = _each(lambda x, p: x + _dot3(x, p), inv, power)
            if step < 4:
                power = _each(lambda p: _dot3(p, p), power)
    u = _each(_dot3, inv, vbeta)
    w = _each(lambda x, y, e: _dot3(x, y * e), inv, kbeta, e_gcum)
    a_qk = _each(lambda x, y, d: _bdot(x, y, NT) * d, qn, kn, decay)
    v_new = _each(lambda x, y, s: x - _bdot(y, s), u, w, state)
    o = _each(lambda x, e, s, a, vn: _bdot(x * e, s) + _bdot(a, vn), qn, e_gcum, state, a_qk, v_new)
    new_state = _each(lambda s, gl, x, gc, vn: s * jnp.exp(gl) + _bdot(x * jnp.exp(gl - gc), vn, TN),
                      state, g_last, kn, gcum, v_new)
    return o, new_state, inv


GDN_HEADS_PER_STEP = 8


GDN_TIME_TILE = 256


def _gdn_specs(n_tokens, reverse):
    hb, hd, tt = GDN_HEADS_PER_STEP, GDN_HEAD_DIM, GDN_TIME_TILE
    nb, nt = GDN_HEADS // hb, n_tokens // tt

    def when(t):
        return nt - 1 - t if reverse else t

    q = pl.BlockSpec((tt, hb * hd), lambda h, t: (when(t), h))
    k = pl.BlockSpec((tt, hb * hd), lambda h, t: (when(t), nb + h))
    v = pl.BlockSpec((tt, hb * hd), lambda h, t: (when(t), 2 * nb + h))
    vec = pl.BlockSpec((tt, hb), lambda h, t: (when(t), h))
    states = pl.BlockSpec((hb, tt // GDN_CHUNK, hd, hd), lambda h, t: (h, when(t), 0, 0))
    inverses = pl.BlockSpec((hb, tt // GDN_CHUNK, GDN_CHUNK, GDN_CHUNK), lambda h, t: (h, when(t), 0, 0))
    return q, k, v, vec, states, inverses


def _gdn_fwd(qkv, beta, g):
    n_tokens = qkv.shape[0]
    hb, hd, tt = GDN_HEADS_PER_STEP, GDN_HEAD_DIM, GDN_TIME_TILE
    n_chunks = tt // GDN_CHUNK
    q_s, k_s, v_s, vec, st, inv_s = _gdn_specs(n_tokens, False)

    def body(q_ref, k_ref, v_ref, b_ref, g_ref, o_ref, st_ref, inv_ref, state):
        @pl.when(pl.program_id(1) == 0)
        def _():
            state[...] = jnp.zeros_like(state)

        def step(c, carry):
            r = pl.ds(pl.multiple_of(c * GDN_CHUNK, GDN_CHUNK), GDN_CHUNK)
            cols = [slice(h * hd, (h + 1) * hd) for h in range(hb)]
            old = [state[h] for h in range(hb)]
            o, new, inv = _gdn_chunks(
                [q_ref[r, cs] for cs in cols], [k_ref[r, cs] for cs in cols], [v_ref[r, cs] for cs in cols],
                [b_ref[r, h:h + 1] for h in range(hb)], [g_ref[r, h:h + 1] for h in range(hb)], old)
            for h in range(hb):
                st_ref[h, c] = old[h]
                inv_ref[h, c] = inv[h]
                o_ref[r, cols[h]] = o[h]
                state[h] = new[h]
            return carry

        lax.fori_loop(0, n_chunks, step, 0)

    n_all = n_tokens // GDN_CHUNK
    return pl.pallas_call(
        body, grid=(GDN_HEADS // hb, n_tokens // tt), in_specs=[q_s, k_s, v_s, vec, vec], out_specs=(q_s, st, inv_s),
        out_shape=(jax.ShapeDtypeStruct((n_tokens, GDN_WIDTH), F32),
                   jax.ShapeDtypeStruct((GDN_HEADS, n_all, hd, hd), F32),
                   jax.ShapeDtypeStruct((GDN_HEADS, n_all, GDN_CHUNK, GDN_CHUNK), F32)),
        scratch_shapes=[pltpu.VMEM((hb, hd, hd), F32)],
        name="gdn_fwd", compiler_params=_params(2),
    )(qkv, qkv, qkv, beta, g)


def _gdn_bwd(qkv, beta, g, states, inverses, do):
    n_tokens = qkv.shape[0]
    hb, hd, tt = GDN_HEADS_PER_STEP, GDN_HEAD_DIM, GDN_TIME_TILE
    n_chunks = tt // GDN_CHUNK
    q_s, k_s, v_s, vec, st, inv_s = _gdn_specs(n_tokens, True)

    assert hb == GDN_HEADS

    def body(q_ref, k_ref, v_ref, b_ref, g_ref, st_ref, inv_ref, do_ref, dqkv_ref, db_ref, dg_ref, dstate):
        @pl.when(pl.program_id(1) == 0)
        def _():
            dstate[...] = jnp.zeros_like(dstate)

        def step(i, carry):
            c = n_chunks - 1 - i
            r = pl.ds(pl.multiple_of(c * GDN_CHUNK, GDN_CHUNK), GDN_CHUNK)
            cols = [slice(h * hd, (h + 1) * hd) for h in range(hb)]
            args = ([q_ref[r, cs] for cs in cols], [k_ref[r, cs] for cs in cols], [v_ref[r, cs] for cs in cols],
                    [b_ref[r, h:h + 1] for h in range(hb)], [g_ref[r, h:h + 1] for h in range(hb)],
                    [st_ref[h, c] for h in range(hb)])
            saved = [inv_ref[h, c] for h in range(hb)]
            cts = ([do_ref[r, cs] for cs in cols], [dstate[h] for h in range(hb)])
            dq, dk, dv, db, dg, dst = jax.vjp(lambda *a: _gdn_chunks(*a, inverse=saved)[:2], *args)[1](cts)
            for h in range(hb):
                for part, grad in enumerate((dq, dk, dv)):
                    dqkv_ref[r, pl.ds(part * GDN_WIDTH + h * hd, hd)] = grad[h]
                db_ref[r, h:h + 1] = db[h]
                dg_ref[r, h:h + 1] = dg[h]
                dstate[h] = dst[h]
            return carry

        lax.fori_loop(0, n_chunks, step, 0)

    n_t = n_tokens // tt
    thin = jax.ShapeDtypeStruct(beta.shape, F32)
    return pl.pallas_call(
        body, grid=(GDN_HEADS // hb, n_t), in_specs=[q_s, k_s, v_s, vec, vec, st, inv_s, q_s],
        out_specs=(pl.BlockSpec((tt, 3 * GDN_WIDTH), lambda h, t: (n_t - 1 - t, 0)), vec, vec),
        out_shape=(jax.ShapeDtypeStruct(qkv.shape, F32), thin, thin),
        scratch_shapes=[pltpu.VMEM((hb, hd, hd), F32)],
        name="gdn_bwd", compiler_params=_params(2),
    )(qkv, qkv, qkv, beta, g, states, inverses, do)


FFN_ROW_TILE = 256
FFN_CHUNK = 256
FFN_FWD_ROW_TILE = 512


def _resident(shape):
    return pl.BlockSpec(shape, lambda i: (0,) * len(shape), pipeline_mode=pl.Buffered(1))


def _ffn_fwd(x, gain, wg, wu, wd, name):
    n_tokens, d = x.shape
    n_shards, n, _ = wg.shape
    tm = FFN_FWD_ROW_TILE

    def body(x_ref, gain_ref, wg_ref, wu_ref, wd_ref, o_ref, g_ref, u_ref):
        xv = x_ref[...]
        h = (xv * lax.rsqrt(jnp.mean(xv * xv, axis=-1, keepdims=True) + EPS) * gain_ref[...]).astype(BF16)
        acc = jnp.zeros((tm, d), F32)
        for j in range(n_shards):
            g = lax.dot_general(h, wg_ref[j], NT, preferred_element_type=F32)
            u = lax.dot_general(h, wu_ref[j], NT, preferred_element_type=F32)
            g_ref[j] = g.astype(BF16)
            u_ref[j] = u.astype(BF16)
            a = (g * jax.nn.sigmoid(g) * u).astype(BF16)
            acc = acc + lax.dot_general(a, wd_ref[j], NN, preferred_element_type=F32)
        o_ref[...] = xv + 0.5 * acc

    row = pl.BlockSpec((tm, d), lambda i: (i, 0))
    hid = pl.BlockSpec((n_shards, tm, n), lambda i: (0, i, 0))
    return pl.pallas_call(
        body, grid=(n_tokens // tm,),
        in_specs=[row, _resident(gain.shape), _resident(wg.shape), _resident(wu.shape), _resident(wd.shape)],
        out_specs=(row, hid, hid),
        out_shape=(jax.ShapeDtypeStruct(x.shape, F32), jax.ShapeDtypeStruct((n_shards, n_tokens, n), BF16),
                   jax.ShapeDtypeStruct((n_shards, n_tokens, n), BF16)),
        name=name, compiler_params=_params(1),
    )(x, gain, wg, wu, wd)


def _ffn_bwd_rows(x, gain, dy, g, u, wg, wu, wd, name):
    n_tokens, d = x.shape
    n_shards, n, _ = wg.shape
    tm = FFN_ROW_TILE

    def body(x_ref, gain_ref, dy_ref, g_ref, u_ref, wg_ref, wu_ref, wd_ref,
             dx_ref, dgain_ref, h_ref, dyh_ref, a_ref, dg_ref, du_ref):
        xv, dyv, gain_v = x_ref[...], dy_ref[...], gain_ref[...]
        r = lax.rsqrt(jnp.mean(xv * xv, axis=-1, keepdims=True) + EPS)
        xhat = xv * r
        h_ref[...] = (xhat * gain_v).astype(BF16)
        dyh = (0.5 * dyv).astype(BF16)
        dyh_ref[...] = dyh
        dh = jnp.zeros((tm, d), F32)
        for j in range(n_shards):
            da = lax.dot_general(dyh, wd_ref[j], NT, preferred_element_type=F32)
            gv, uv = g_ref[j].astype(F32), u_ref[j].astype(F32)
            sg = jax.nn.sigmoid(gv)
            silu = gv * sg
            a_ref[j] = (silu * uv).astype(BF16)
            dg = (da * uv * (sg + silu * (1.0 - sg))).astype(BF16)
            du = (da * silu).astype(BF16)
            dg_ref[j] = dg
            du_ref[j] = du
            dh = dh + lax.dot_general(dg, wg_ref[j], NN, preferred_element_type=F32)
            dh = dh + lax.dot_general(du, wu_ref[j], NN, preferred_element_type=F32)
        dxhat = dh * gain_v
        dx_ref[...] = dyv + r * (dxhat - xhat * jnp.mean(dxhat * xhat, axis=-1, keepdims=True))

        @pl.when(pl.program_id(0) == 0)
        def _():
            dgain_ref[...] = jnp.zeros_like(dgain_ref)

        dgain_ref[...] += jnp.sum(dh * xhat, axis=0, keepdims=True)

    row = pl.BlockSpec((tm, d), lambda i: (i, 0))
    hid = pl.BlockSpec((n_shards, tm, n), lambda i: (0, i, 0))
    hid_shape = (n_shards, n_tokens, n)
    return pl.pallas_call(
        body, grid=(n_tokens // tm,),
        in_specs=[row, _resident(gain.shape), row, hid, hid, _resident(wg.shape), _resident(wu.shape),
                  _resident(wd.shape)],
        out_specs=(row, pl.BlockSpec(gain.shape, lambda i: (0, 0)), row, row, hid, hid, hid),
        out_shape=(jax.ShapeDtypeStruct(x.shape, F32), jax.ShapeDtypeStruct(gain.shape, F32),
                   jax.ShapeDtypeStruct(x.shape, BF16), jax.ShapeDtypeStruct(x.shape, BF16),
                   jax.ShapeDtypeStruct(hid_shape, BF16), jax.ShapeDtypeStruct(hid_shape, BF16),
                   jax.ShapeDtypeStruct(hid_shape, BF16)),
        name=name, compiler_params=_params(1),
    )(x, gain, dy, g, u, wg, wu, wd)


def _ffn_bwd_weights(h, dyh, a, dg, du, name, with_payload=False):
    n_chunks, n_tokens, n = a.shape
    d = h.shape[1]

    def body(h_ref, dyh_ref, a_ref, dg_ref, du_ref, *out_refs):
        hv = h_ref[...]
        vals = (lax.dot_general(dg_ref[0], hv, TN, preferred_element_type=F32),
                lax.dot_general(du_ref[0], hv, TN, preferred_element_type=F32),
                lax.dot_general(a_ref[0], dyh_ref[...], TN, preferred_element_type=F32))
        for ref, val in zip(out_refs[-3:], vals):
            ref[0] = val
        if with_payload:
            for ref, val in zip(out_refs[:3], vals):
                ref[0] = val.astype(BF16)

    hid = pl.BlockSpec((1, n_tokens, n), lambda j: (j, 0, 0))
    out = pl.BlockSpec((1, n, d), lambda j: (j, 0, 0))
    shapes = (jax.ShapeDtypeStruct((n_chunks, n, d), F32),) * 3
    if with_payload:
        shapes = (jax.ShapeDtypeStruct((n_chunks, n, d), BF16),) * 3 + shapes
    outs = pl.pallas_call(
        body, grid=(n_chunks,), in_specs=[_resident(h.shape), _resident(dyh.shape), hid, hid, hid],
        out_specs=(out,) * len(shapes), out_shape=shapes, name=name, compiler_params=_params(1),
    )(h, dyh, a, dg, du)
    return (outs[:3], outs[3:]) if with_payload else outs


IN_PIECES = (("wq_a", 0, 768), ("wk_a", 768, 1536), ("wv_a", 1536, 2304), ("w_qkvb", 2304, 5376),
             ("w_small", 5376, 5392), ("w_ggate", 5392, 6416), ("w_gatea", 6416, 7440), ("w_gateb", 7440, 8464))
IN_NAMES = tuple(name for name, _, _ in IN_PIECES)


def _in_rows(lo, hi):
    return lo, max(hi, lo + LANES)


N_ROTATED = 2


def _in_proj_fwd(x, gain, wt, cos, sin):
    n_tokens, d = x.shape
    tm = FFN_ROW_TILE
    rows = [_in_rows(lo, hi) for _, lo, hi in IN_PIECES]

    def body(x_ref, gain_ref, wt_ref, cos_ref, sin_ref, *o_refs):
        xv = x_ref[...]
        h = (xv * lax.rsqrt(jnp.mean(xv * xv, axis=-1, keepdims=True) + EPS) * gain_ref[...]).astype(BF16)
        for k, ((lo, hi), o_ref) in enumerate(zip(rows, o_refs)):
            z = lax.dot_general(h, wt_ref[lo:hi, :], NT, preferred_element_type=F32)
            o_ref[...] = _rotate(z, cos_ref[...], sin_ref[...]) if k < N_ROTATED else z

    tab = pl.BlockSpec((tm, LANES), lambda i: (i, 0))
    return pl.pallas_call(
        body, grid=(n_tokens // tm,),
        in_specs=[pl.BlockSpec((tm, d), lambda i: (i, 0)), _resident(gain.shape), _resident(wt.shape), tab, tab],
        out_specs=tuple(pl.BlockSpec((tm, hi - lo), lambda i: (i, 0)) for lo, hi in rows),
        out_shape=tuple(jax.ShapeDtypeStruct((n_tokens, hi - lo), F32) for lo, hi in rows),
        name="in_proj_fwd", compiler_params=_params(1),
    )(x, gain, wt, cos, sin)


def _in_proj_bwd_rows(x, gain, dres, dzs, wt, cos, sin):
    n_tokens, d = x.shape
    tm = FFN_ROW_TILE
    n = len(dzs)
    rows = [_in_rows(lo, hi) for _, lo, hi in IN_PIECES]

    def body(x_ref, gain_ref, dres_ref, cos_ref, sin_ref, *refs):
        dz_refs, wt_ref = refs[:n], refs[n]
        dx_ref, dgain_ref, h_ref = refs[n + 1:n + 4]
        unrotated_refs = refs[n + 4:]
        xv, gain_v = x_ref[...], gain_ref[...]
        r = lax.rsqrt(jnp.mean(xv * xv, axis=-1, keepdims=True) + EPS)
        xhat = xv * r
        h_ref[...] = (xhat * gain_v).astype(BF16)
        dh = jnp.zeros((tm, d), F32)
        for k, (dz_ref, (lo, hi)) in enumerate(zip(dz_refs, rows)):
            dz = dz_ref[...]
            if k < N_ROTATED:
                dz = _rotate(dz, cos_ref[...], -sin_ref[...]).astype(BF16)
                unrotated_refs[k][...] = dz
            dh = dh + lax.dot_general(dz.astype(BF16), wt_ref[lo:hi, :], NN, preferred_element_type=F32)
        dxhat = dh * gain_v
        dx_ref[...] = dres_ref[...] + r * (dxhat - xhat * jnp.mean(dxhat * xhat, axis=-1, keepdims=True))

        @pl.when(pl.program_id(0) == 0)
        def _():
            dgain_ref[...] = jnp.zeros_like(dgain_ref)

        dgain_ref[...] += jnp.sum(dh * xhat, axis=0, keepdims=True)

    row = pl.BlockSpec((tm, d), lambda i: (i, 0))
    tab = pl.BlockSpec((tm, LANES), lambda i: (i, 0))
    dz_specs = [pl.BlockSpec((tm, dz.shape[1]), lambda i: (i, 0)) for dz in dzs]
    outs = pl.pallas_call(
        body, grid=(n_tokens // tm,),
        in_specs=[row, _resident(gain.shape), row, tab, tab] + dz_specs + [_resident(wt.shape)],
        out_specs=(row, pl.BlockSpec(gain.shape, lambda i: (0, 0)), row) + tuple(dz_specs[:N_ROTATED]),
        out_shape=(jax.ShapeDtypeStruct(x.shape, F32), jax.ShapeDtypeStruct(gain.shape, F32),
                   jax.ShapeDtypeStruct(x.shape, BF16))
        + tuple(jax.ShapeDtypeStruct(dz.shape, BF16) for dz in dzs[:N_ROTATED]),
        name="in_proj_bwd_rows", compiler_params=_params(1),
    )(x, gain, dres, cos, sin, *dzs, wt)
    return outs[0], outs[1], outs[2], outs[3:]


def _in_proj_bwd_weight(dwt, h, dz, lo, hi, name):
    n_tokens, d = h.shape
    width = hi - lo
    tn = _tile(width, 512) if width >= LANES else width
    dz_tile = max(tn, LANES)

    def body(dwt_ref, h_ref, dz_ref, o_ref):
        o_ref[...] = lax.dot_general(dz_ref[:, :tn].astype(BF16), h_ref[...], TN, preferred_element_type=F32)

    return pl.pallas_call(
        body, grid=(width // tn,),
        in_specs=[ANY, _resident(h.shape), pl.BlockSpec((n_tokens, dz_tile), lambda j: (0, j))],
        out_specs=pl.BlockSpec((pl.Element(tn), pl.Element(d)), lambda j: (pl.multiple_of(lo + j * tn, 16), 0)),
        out_shape=jax.ShapeDtypeStruct(dwt.shape, F32), input_output_aliases={0: 0}, name=name,
        compiler_params=_params(1),
    )(dwt, h, dz)


def _split_small(z):
    return z[:, :GDN_HEADS], z[:, GDN_HEADS:2 * GDN_HEADS]


def _heads3(q, k, v):
    return _to_heads(q), _to_heads(k), _to_heads(v)


def _tokens6(o, lse):
    return tuple(_from_heads(o)) + tuple(_from_heads(lse))


def _blocks_of(vals, nblk):
    return [[v[:, b * (v.shape[1] // nblk):(b + 1) * (v.shape[1] // nblk)] for v in vals] for b in range(nblk)]


def _rowwise_matmul_fwd(fn, name, rows, params, wt, nblk, res=None):
    n_rows = rows[0].shape[0]
    tm = FFN_ROW_TILE
    k, n = wt.shape
    nr, npar = len(rows), len(params)

    def body(*refs):
        row_vals = [r[...] for r in refs[:nr]]
        par_vals = [r[...] for r in refs[nr:nr + npar]]
        wt_ref = refs[nr + npar]
        o_ref, y_ref = refs[-2:]
        y = jnp.concatenate([fn(*blk, *par_vals)[0] for blk in _blocks_of(row_vals, nblk)], axis=1).astype(BF16)
        y_ref[...] = y
        acc = lax.dot_general(y, wt_ref[...], NN, preferred_element_type=F32)
        o_ref[...] = acc if res is None else refs[nr + npar + 1][...] + acc

    row_specs = [pl.BlockSpec((tm, a.shape[1]), lambda i: (i, 0)) for a in rows]
    ins = list(rows) + list(params) + [wt] + ([] if res is None else [res])
    specs = row_specs + [_resident(p.shape) for p in params] + [_resident(wt.shape)]
    if res is not None:
        specs.append(pl.BlockSpec((tm, n), lambda i: (i, 0)))
    return pl.pallas_call(
        body, grid=(n_rows // tm,), in_specs=specs,
        out_specs=(pl.BlockSpec((tm, n), lambda i: (i, 0)), pl.BlockSpec((tm, k), lambda i: (i, 0))),
        out_shape=(jax.ShapeDtypeStruct((n_rows, n), F32), jax.ShapeDtypeStruct((n_rows, k), BF16)),
        name=name, compiler_params=_params(1),
    )(*ins)


def _rowwise_matmul_bwd(fn, name, rows, params, wt, dout, nblk, row_dtypes=None):
    n_rows = rows[0].shape[0]
    row_dtypes = row_dtypes or (F32,) * len(rows)
    tm = FFN_ROW_TILE
    nr, npar = len(rows), len(params)

    def body(*refs):
        row_vals = [r[...] for r in refs[:nr]]
        par_vals = [r[...] for r in refs[nr:nr + npar]]
        wt_ref, dout_ref = refs[nr + npar], refs[nr + npar + 1]
        outs = refs[nr + npar + 2:]
        dy = lax.dot_general(dout_ref[...].astype(BF16), wt_ref[...], NT, preferred_element_type=F32)
        grads = [jax.vjp(fn, *blk, *par_vals)[1]((dy_blk,))
                 for blk, (dy_blk,) in zip(_blocks_of(row_vals, nblk), _blocks_of([dy], nblk))]
        for j in range(nr):
            outs[j][...] = jnp.concatenate([g[j] for g in grads], axis=1).astype(row_dtypes[j])
        for j in range(npar):
            ref = outs[nr + j]

            @pl.when(pl.program_id(0) == 0)
            def _(ref=ref):
                ref[...] = jnp.zeros_like(ref)

            for g in grads:
                ref[...] += g[nr + j]

    row_specs = [pl.BlockSpec((tm, a.shape[1]), lambda i: (i, 0)) for a in rows]
    par_specs = [_resident(p.shape) for p in params]
    return pl.pallas_call(
        body, grid=(n_rows // tm,),
        in_specs=row_specs + par_specs + [_resident(wt.shape), pl.BlockSpec((tm, dout.shape[1]), lambda i: (i, 0))],
        out_specs=tuple(row_specs + [pl.BlockSpec(p.shape, lambda i: (0, 0)) for p in params]),
        out_shape=tuple([jax.ShapeDtypeStruct(a.shape, dt) for a, dt in zip(rows, row_dtypes)]
                        + [jax.ShapeDtypeStruct(p.shape, F32) for p in params]),
        name=name, compiler_params=_params(1),
    )(*rows, *params, wt, dout)


def mixer_forward(x1, w, small):
    n_tokens = x1.shape[0]
    cos, sin = _rope_tables(n_tokens)
    proj = dict(zip(IN_NAMES, _in_proj_fwd(x1, small["mix_norm"], w["w_in_t"], cos, sin)))
    (qh, kh, vh), heads_vjp = jax.vjp(_heads3, proj["wq_a"], proj["wk_a"], proj["wv_a"])
    o, lse = _attn_fwd(qh, kh, vh)
    per_group, tokens_vjp = jax.vjp(_tokens6, o, lse)
    pa, ya = _rowwise_matmul_fwd(_combine_fn, "branch_a", per_group, (), w["w_branch_a"], 1)
    qkv = _conv_fwd(proj["w_qkvb"], small["gdn_conv_w"])
    raw, small_vjp = jax.vjp(_split_small, proj["w_small"])
    gdn_params = (small["gdn_a_log"], small["gdn_dt_bias"])
    beta, gcum = _rowwise_fwd(_beta_decay_fn, "beta_decay", raw, (), gdn_params, 512, 1)
    ob, *states = _gdn_fwd(qkv, beta, gcum)
    gate_in = (ob, proj["w_ggate"])
    pb, yb = _rowwise_matmul_fwd(_outnorm_gate_fn, "branch_b", gate_in, (small["gdn_out_norm"],), w["w_branch_b"],
                                 GDN_HEADS)
    merge_in = (proj["w_gatea"], proj["w_gateb"], pa, pb)
    x2, merged = _rowwise_matmul_fwd(_merge_fn, "out", merge_in, (), w["w_out"], 1, res=x1)
    saved = dict(x1=x1, proj=proj, cos=cos, sin=sin, heads_vjp=heads_vjp, heads=(qh, kh, vh), tokens_vjp=tokens_vjp,
                 per_group=per_group, ya=ya, qkv=qkv, raw=raw, small_vjp=small_vjp, beta=beta, gcum=gcum, states=states,
                 gate_in=gate_in, yb=yb, merge_in=merge_in, merged=merged)
    return x2, saved


def mixer_backward(dx2, s, w, small):
    proj = s["proj"]
    grads = dict(w_out=_weight_grad(s["merged"], dx2, "out_dw"))
    dgate_a, dgate_b, dpa, dpb = _rowwise_matmul_bwd(_merge_fn, "out_bwd", s["merge_in"], (), w["w_out"], dx2, 1,
                                                     (BF16,) * 4)
    grads["w_branch_b"] = _weight_grad(s["yb"], dpb, "branch_b_dw")
    grads["w_branch_a"] = _weight_grad(s["ya"], dpa, "branch_a_dw")
    dob, dggate, grads["gdn_out_norm"] = _rowwise_matmul_bwd(
        _outnorm_gate_fn, "branch_b_bwd", s["gate_in"], (small["gdn_out_norm"],), w["w_branch_b"], dpb, GDN_HEADS,
        (F32, BF16))
    dqkv, dbeta, dgcum = _gdn_bwd(s["qkv"], s["beta"], s["gcum"], *s["states"], dob)
    gdn_params = (small["gdn_a_log"], small["gdn_dt_bias"])
    dbeta_raw, ddecay_raw, grads["gdn_a_log"], grads["gdn_dt_bias"] = _rowwise_bwd(
        _beta_decay_fn, "beta_decay_bwd", s["raw"], (), gdn_params, (dbeta, dgcum), 512, 1)
    dsmall = s["small_vjp"]((dbeta_raw, ddecay_raw))[0]
    dqkvb, grads["gdn_conv_w"] = _conv_bwd(proj["w_qkvb"], small["gdn_conv_w"], dqkv)
    dper_group = _rowwise_matmul_bwd(_combine_fn, "branch_a_bwd", s["per_group"], (), w["w_branch_a"], dpa, 1)
    do, dlse = s["tokens_vjp"](tuple(dper_group))
    dqh, dkh, dvh = _attn_bwd(*s["heads"], do, dlse)
    dq_rot, dk_rot, dv = s["heads_vjp"]((dqh, dkh, dvh))
    dzs = (dq_rot, dk_rot, dv, dqkvb, dsmall, dggate, dgate_a, dgate_b)
    dx1, grads["mix_norm"], h, unrotated = _in_proj_bwd_rows(
        s["x1"], small["mix_norm"], dx2, dzs, w["w_in_t"], s["cos"], s["sin"])
    dzs = tuple(unrotated) + dzs[N_ROTATED:]
    dwt = lax.empty(w["w_in_t"].shape, F32)
    for (name, lo, hi), dz in zip(IN_PIECES, dzs):
        dwt = _in_proj_bwd_weight(dwt, h, dz, lo, hi, "in_proj_dw_" + name)
    grads["w_in_t"] = dwt
    return dx1, grads


def ffn_forward(x, gain, w, tag):
    out, g, u = _ffn_fwd(x, gain, w[tag + "_w_gate"], w[tag + "_w_up"], w[tag + "_w_down"], tag + "_fwd")
    return out, (x, g, u)


def ffn_backward(dy, saved, gain, w, tag, with_payload=False):
    x, g, u = saved
    weights = (w[tag + "_w_gate"], w[tag + "_w_up"], w[tag + "_w_down"])
    dx, dgain, h, dyh, a, dg, du = _ffn_bwd_rows(x, gain, dy, g, u, *weights, tag + "_bwd_rows")
    return dx, dgain, _ffn_bwd_weights(h, dyh, a, dg, du, tag + "_bwd_weights", with_payload)


def loss_head(x3, target, gain):
    n_tokens, d = x3.shape
    tm = FFN_ROW_TILE

    def body(x_ref, t_ref, gain_ref, loss_ref, dx_ref, dgain_ref):
        target_v = t_ref[...]
        (row_loss,), vjp = jax.vjp(lambda xv, gv: _loss_fn(xv, target_v, gv), x_ref[...], gain_ref[...])
        dx, dgain = vjp((jnp.ones_like(row_loss),))
        loss_ref[...] = row_loss
        dx_ref[...] = dx

        @pl.when(pl.program_id(0) == 0)
        def _():
            dgain_ref[...] = jnp.zeros_like(dgain_ref)

        dgain_ref[...] += dgain

    row = pl.BlockSpec((tm, d), lambda i: (i, 0))
    row_loss, dx3, dgain = pl.pallas_call(
        body, grid=(n_tokens // tm,), in_specs=[row, row, _resident(gain.shape)],
        out_specs=(pl.BlockSpec((tm, 1), lambda i: (i, 0)), row, pl.BlockSpec(gain.shape, lambda i: (0, 0))),
        out_shape=(jax.ShapeDtypeStruct((n_tokens, 1), F32), jax.ShapeDtypeStruct(x3.shape, F32),
                   jax.ShapeDtypeStruct(gain.shape, F32)),
        name="loss_head", compiler_params=_params(1),
    )(x3, target, gain)
    return jnp.sum(row_loss), dx3, dgain


BIG_WEIGHTS = ("ffn1_w_gate", "ffn1_w_up", "ffn1_w_down", "w_in", "w_branch_a", "w_branch_b", "w_out",
               "ffn2_w_gate", "ffn2_w_up", "ffn2_w_down")
TRANSPOSED = ("ffn1_w_gate", "ffn1_w_up", "w_in", "ffn2_w_gate", "ffn2_w_up")
CONV_SHARD = (GDN_CONV, 3 * GDN_WIDTH // N_DEV)
SMALL_ROWS = 24
ANY = pl.BlockSpec(memory_space=pl.ANY)


TOKEN = jax.ShapeDtypeStruct((8, LANES), F32)


def _after(value, token):
    return value + token[0, 0].astype(value.dtype)


def _position():
    return lax.axis_index("x"), lax.axis_index("y"), lax.axis_index("c")


def all_gather_shards(shards, name):
    n = len(shards)
    per = 8

    def body(*refs):
        x_refs, out_refs = refs[:n], refs[n:2 * n]
        send_sems, recv_sems, local_sems = refs[2 * n + 1:]
        x, y, c = _position()
        me, sibling = (x, y, c), (x, y, 1 - c)
        x_chip, y_chip, far_chip = (1 - x, y), (x, 1 - y), (1 - x, 1 - y)

        def slab(a, px, py, pc):
            return out_refs[a].at[4 * px + 2 * py + pc]

        def copy(a, k, src, dst, to):
            return pltpu.make_async_remote_copy(
                src_ref=src, dst_ref=dst, send_sem=send_sems.at[per * a + k], recv_sem=recv_sems.at[per * a + k],
                device_id=to, device_id_type=MESH)

        def whole(a, k, block, to, src=None):
            return copy(a, k, slab(a, *block) if src is None else src, slab(a, *block), to)

        def half(a, k, block, which, to):
            rows = shards[a].shape[0] // 2
            part = slab(a, *block).at[pl.ds(which * rows, rows)]
            return copy(a, k, part, part, to)

        arrays = range(n)
        mine = [pltpu.make_async_copy(x_refs[a], slab(a, *me), local_sems.at[a]) for a in arrays]
        for cp in mine:
            cp.start()
        started = [whole(a, 1, me, (*x_chip, c), src=x_refs[a]) for a in arrays]
        started += [whole(a, 2, me, (*y_chip, c), src=x_refs[a]) for a in arrays]
        started += [whole(a, 0, me, sibling, src=x_refs[a]) for a in arrays]
        for cp in started:
            cp.start()

        def start(cp):
            cp.start()
            started.append(cp)

        for a in arrays:
            whole(a, 1, (*x_chip, c), me).wait_recv()
            start(half(a, 3, (*x_chip, c), 0, (*y_chip, c)))
            start(whole(a, 5, (*x_chip, c), sibling))
        for a in arrays:
            whole(a, 2, (*y_chip, c), me).wait_recv()
            start(half(a, 4, (*y_chip, c), 1, (*x_chip, c)))
            start(whole(a, 6, (*y_chip, c), sibling))
        for a in arrays:
            half(a, 3, (*far_chip, c), 0, me).wait_recv()
            half(a, 4, (*far_chip, c), 1, me).wait_recv()
            start(whole(a, 7, (*far_chip, c), sibling))
        for a in arrays:
            whole(a, 0, sibling, me).wait_recv()
            for k, chip in ((5, x_chip), (6, y_chip), (7, far_chip)):
                whole(a, k, (*chip, 1 - c), me).wait_recv()
        for cp in started:
            cp.wait_send()
        for cp in mine:
            cp.wait()
        refs[2 * n][...] = jnp.zeros_like(refs[2 * n])

    outs = pl.pallas_call(
        body, out_shape=tuple(jax.ShapeDtypeStruct((N_DEV,) + s.shape, s.dtype) for s in shards) + (TOKEN,),
        in_specs=[ANY] * n, out_specs=(ANY,) * n + (pl.BlockSpec(memory_space=pltpu.VMEM),),
        scratch_shapes=[pltpu.SemaphoreType.DMA((per * n,)), pltpu.SemaphoreType.DMA((per * n,)),
                        pltpu.SemaphoreType.DMA((n,))],
        name=name,
    )(*shards)
    return outs[:n], outs[n]


def exchange_with_sibling(grads):
    n = len(grads)

    def body(*refs):
        g_refs, recv_refs = refs[:n], refs[n:2 * n]
        send_sems, recv_sems = refs[2 * n:]
        x, y, c = _position()
        copies = [pltpu.make_async_remote_copy(
            src_ref=g_refs[a].at[2 * k + 1 - c], dst_ref=recv_refs[a].at[k], send_sem=send_sems.at[4 * a + k],
            recv_sem=recv_sems.at[4 * a + k], device_id=(x, y, 1 - c), device_id_type=MESH)
            for k in range(4) for a in range(n)]
        for cp in copies:
            cp.start()
        for cp in copies:
            cp.wait()

    return pl.pallas_call(
        body, out_shape=tuple(jax.ShapeDtypeStruct((4,) + g.shape[1:], g.dtype) for g in grads),
        in_specs=[ANY] * n, out_specs=(ANY,) * n,
        scratch_shapes=[pltpu.SemaphoreType.DMA((4 * n,)), pltpu.SemaphoreType.DMA((4 * n,))], name="rs_sibling",
    )(*grads)


ELEMENTWISE_TILE_BYTES = 1536 * 1024


def _tile2(rows, cols):
    if rows % 256 == 0:
        return 256, cols
    if rows * cols * 4 > ELEMENTWISE_TILE_BYTES and cols % 256 == 0:
        return rows, 256
    return rows, cols


def add_sibling(grads, received, core, name):
    _, rows, width = grads.shape
    tr, tc = _tile2(rows, width)

    def body(c_ref, g_ref, r_ref, o_ref):
        o_ref[...] = (g_ref[...] + r_ref[...]).astype(BF16)

    blk = (1, tr, tc)
    return pl.pallas_call(
        body,
        grid_spec=pltpu.PrefetchScalarGridSpec(
            num_scalar_prefetch=1, grid=(4, rows // tr, width // tc),
            in_specs=[pl.BlockSpec(blk, lambda k, i, j, c_ref: (2 * k + c_ref[0], i, j)),
                      pl.BlockSpec(blk, lambda k, i, j, c_ref: (k, i, j))],
            out_specs=pl.BlockSpec(blk, lambda k, i, j, c_ref: (k, i, j))),
        out_shape=jax.ShapeDtypeStruct((4, rows, width), BF16), name=name, compiler_params=_params(3),
    )(core, grads, received)


HBM = pl.BlockSpec(memory_space=pltpu.HBM)
SEM = pl.BlockSpec(memory_space=pltpu.SEMAPHORE)
DATAFLOW_EFFECT = pltpu.SideEffectType.DATAFLOW_SIDE_EFFECTING
N_PEERS = N_DEV - 1


def _peer(mask):
    x, y, c = _position()
    px = 1 - x if mask & 4 else x
    py = 1 - y if mask & 2 else y
    pc = 1 - c if mask & 1 else c
    return (px, py, pc), 4 * px + 2 * py + pc


ALL_PEERS = tuple(range(1, N_DEV))
OTHER_CHIPS = (4, 2, 6)


SIBLING = 1
GATHER_MODES = ("gather", "near")


def _exchange_peers(mode):
    return {"chips": OTHER_CHIPS, "near": (SIBLING,) + OTHER_CHIPS}.get(mode, ALL_PEERS)


def _direct_copies(src_refs, land_refs, send_sems, recv_sems, mode):
    x, y, c = _position()
    me = 4 * x + 2 * y + c
    masks = _exchange_peers(mode)
    copies = []
    for a, (src, land) in enumerate(zip(src_refs, land_refs)):
        for slot, mask in enumerate(masks):
            peer, peer_index = _peer(mask)
            k = len(masks) * a + slot
            if mode in GATHER_MODES:
                source, dest = src, land.at[me]
            elif mode == "scatter":
                source, dest = src.at[peer_index], land.at[slot]
            else:
                source, dest = src.at[2 * peer[0] + peer[1]], land.at[slot]
            copies.append(pltpu.make_async_remote_copy(
                src_ref=source, dst_ref=dest, send_sem=send_sems.at[k], recv_sem=recv_sems.at[k], device_id=peer,
                device_id_type=MESH))
    return copies


def forward_to_sibling(slabs, name):
    n = len(slabs)

    def body(*refs):
        out_refs = refs[n:2 * n]
        send_sems, recv_sems = refs[2 * n + 1:]
        x, y, c = _position()
        copies = []
        for a in range(n):
            for slot, mask in enumerate(OTHER_CHIPS):
                _, held = _peer(mask)
                copies.append(pltpu.make_async_remote_copy(
                    src_ref=out_refs[a].at[held], dst_ref=out_refs[a].at[held], send_sem=send_sems.at[3 * a + slot],
                    recv_sem=recv_sems.at[3 * a + slot], device_id=(x, y, 1 - c), device_id_type=MESH))
        for cp in copies:
            cp.start()
        for cp in copies:
            cp.wait()
        refs[2 * n][...] = jnp.zeros_like(refs[2 * n])

    outs = pl.pallas_call(
        body, out_shape=tuple(jax.ShapeDtypeStruct(s.shape, s.dtype) for s in slabs) + (TOKEN,),
        in_specs=[ANY] * n, out_specs=(ANY,) * n + (pl.BlockSpec(memory_space=pltpu.VMEM),),
        input_output_aliases={i: i for i in range(n)},
        scratch_shapes=[pltpu.SemaphoreType.DMA((3 * n,)), pltpu.SemaphoreType.DMA((3 * n,))], name=name,
    )(*slabs)
    return outs[:n], outs[n]


def direct_exchange_start(arrays, mode, name):
    n = len(arrays)
    n_peers = len(_exchange_peers(mode))
    lands = [lax.empty((N_DEV,) + a.shape if mode in GATHER_MODES else (n_peers,) + a.shape[1:], a.dtype)
             for a in arrays]

    def body(*refs):
        src_refs, land_refs = refs[:n], refs[n:2 * n]
        send_sems, recv_sems = refs[2 * n], refs[2 * n + 1]
        token = refs[-1]
        for cp in _direct_copies(src_refs, land_refs, send_sems, recv_sems, mode):
            cp.start()
        token[...] = jnp.zeros_like(token)

    sems = pltpu.SemaphoreType.DMA((n_peers * n,))
    outs = pl.pallas_call(
        body, name=name,
        out_shape=(sems, sems) + tuple(pltpu.HBM(a.shape, a.dtype) for a in arrays)
        + tuple(pltpu.HBM(l.shape, l.dtype) for l in lands) + (TOKEN,),
        in_specs=[HBM] * (2 * n), out_specs=(SEM, SEM) + (HBM,) * (2 * n) + (pl.BlockSpec(memory_space=pltpu.VMEM),),
        input_output_aliases={i: 2 + i for i in range(2 * n)},
        compiler_params=pltpu.CompilerParams(has_side_effects=DATAFLOW_EFFECT),
    )(*[pltpu.with_memory_space_constraint(a, pltpu.HBM) for a in list(arrays) + lands])
    return outs[0], outs[1], outs[2:2 + n], outs[2 + n:2 + 2 * n], outs[-1]


def direct_exchange_wait(send_sems, recv_sems, arrays, lands, after, mode, name):
    n = len(arrays)

    def body(*refs):
        src_refs, land_refs = refs[:n], refs[n:2 * n]
        send_sems, recv_sems = refs[2 * n], refs[2 * n + 1]
        for cp in _direct_copies(src_refs, land_refs, send_sems, recv_sems, mode):
            cp.wait_send()
            cp.wait_recv()
        refs[-1][...] = jnp.zeros_like(refs[-1])

    outs = pl.pallas_call(
        body, name=name,
        out_shape=tuple(pltpu.HBM(a.shape, a.dtype) for a in arrays) + tuple(pltpu.HBM(l.shape, l.dtype) for l in lands)
        + (TOKEN,),
        in_specs=[HBM] * (2 * n) + [SEM, SEM, pl.BlockSpec(memory_space=pl.ANY)],
        out_specs=(HBM,) * (2 * n) + (pl.BlockSpec(memory_space=pltpu.VMEM),),
        input_output_aliases={i: i for i in range(2 * n)},
        compiler_params=pltpu.CompilerParams(has_side_effects=DATAFLOW_EFFECT),
    )(*arrays, *lands, send_sems, recv_sems, after)
    return outs[n:]


def adamw_direct(w, m, v, own, received, name):
    row_per_tile = w.shape[0] != 1
    rows, cols = (w.shape[0], w.shape[2]) if row_per_tile else w.shape[-2:]
    tr, tc = _tile2(rows, cols)

    def body(w_ref, m_ref, v_ref, own_ref, r_ref, g_ref, d_ref, nm_ref, nv_ref):
        gv = own_ref[0]
        for j in range(N_PEERS):
            gv = gv + r_ref[j].astype(F32)
        nm = ADAM_B1 * m_ref[...] + (1.0 - ADAM_B1) * gv
        nv = ADAM_B2 * v_ref[...] + (1.0 - ADAM_B2) * (gv * gv)
        m_hat = nm / (1.0 - ADAM_B1 ** ADAM_STEP)
        v_hat = nv / (1.0 - ADAM_B2 ** ADAM_STEP)
        g_ref[...] = gv
        d_ref[...] = -ADAM_LR * (m_hat / (jnp.sqrt(v_hat) + ADAM_EPS) + ADAM_WD * w_ref[...])
        nm_ref[...] = nm
        nv_ref[...] = nv

    if row_per_tile:
        one = pl.BlockSpec((tr, None, tc), lambda i, j: (i, 0, j))
    else:
        one = pl.BlockSpec((None, tr, tc), lambda i, j: (0, i, j))
    out = jax.ShapeDtypeStruct(w.shape, F32)
    return pl.pallas_call(
        body, grid=(rows // tr, cols // tc),
        in_specs=[one, one, one, pl.BlockSpec((1, tr, tc), lambda i, j: (0, i, j)),
                  pl.BlockSpec((N_PEERS, tr, tc), lambda i, j: (0, i, j))],
        out_specs=(one,) * 4, out_shape=(out,) * 4, name=name, compiler_params=_params(2),
    )(w, m, v, own, received)


def all_reduce_small(vals):
    rows, width = vals.shape

    def body(x_ref, out_ref, all_ref, send_sems, recv_sems):
        x, y, c = _position()
        me, sibling = (x, y, c), (x, y, 1 - c)
        chips = [(1 - x, y), (x, 1 - y), (1 - x, 1 - y)]

        def slab(px, py, pc):
            return all_ref.at[4 * px + 2 * py + pc]

        def copy(k, block, to, src=None):
            return pltpu.make_async_remote_copy(
                src_ref=slab(*block) if src is None else src, dst_ref=slab(*block),
                send_sem=send_sems.at[k], recv_sem=recv_sems.at[k], device_id=to, device_id_type=MESH)

        first = [copy(0, me, sibling, src=x_ref)]
        first += [copy(1 + j, me, (*chip, c), src=x_ref) for j, chip in enumerate(chips)]
        for cp in first:
            cp.start()
        all_ref[4 * x + 2 * y + c] = x_ref[...]
        passed = [copy(4 + j, (*chip, c), sibling) for j, chip in enumerate(chips)]
        for j, chip in enumerate(chips):
            copy(1 + j, (*chip, c), me).wait_recv()
            passed[j].start()
        copy(0, sibling, me).wait_recv()
        for j, chip in enumerate(chips):
            copy(4 + j, (*chip, 1 - c), me).wait_recv()
        for cp in first + passed:
            cp.wait_send()
        total = all_ref[0]
        for d in range(1, N_DEV):
            total = total + all_ref[d]
        out_ref[...] = total

    vmem = pl.BlockSpec(memory_space=pltpu.VMEM)
    return pl.pallas_call(
        body, out_shape=(jax.ShapeDtypeStruct(vals.shape, F32), jax.ShapeDtypeStruct((N_DEV, rows, width), F32)),
        in_specs=[vmem], out_specs=(vmem, vmem),
        scratch_shapes=[pltpu.SemaphoreType.DMA((7,)), pltpu.SemaphoreType.DMA((7,))], name="small_allreduce",
    )(vals)[0]


def adamw(w, g, m, v, name):
    shape = w.shape
    w2, g2, m2, v2 = [a.reshape((-1, shape[-1])) for a in (w, g, m, v)]
    rows, cols = w2.shape
    tr = 256 if rows % 256 == 0 else rows

    def body(w_ref, g_ref, m_ref, v_ref, d_ref, nm_ref, nv_ref):
        gv = g_ref[...]
        nm = ADAM_B1 * m_ref[...] + (1.0 - ADAM_B1) * gv
        nv = ADAM_B2 * v_ref[...] + (1.0 - ADAM_B2) * (gv * gv)
        m_hat = nm / (1.0 - ADAM_B1 ** ADAM_STEP)
        v_hat = nv / (1.0 - ADAM_B2 ** ADAM_STEP)
        d_ref[...] = -ADAM_LR * (m_hat / (jnp.sqrt(v_hat) + ADAM_EPS) + ADAM_WD * w_ref[...])
        nm_ref[...] = nm
        nv_ref[...] = nv

    blk = pl.BlockSpec((tr, cols), lambda i: (i, 0))
    out = jax.ShapeDtypeStruct((rows, cols), F32)
    outs = pl.pallas_call(
        body, grid=(rows // tr,), in_specs=[blk] * 4, out_specs=(blk,) * 3, out_shape=(out,) * 3,
        name=name, compiler_params=_params(1),
    )(w2, g2, m2, v2)
    return tuple(o.reshape(shape) for o in outs)


def adamw_summed(w, m, v, grads, from_sibling, received, me, name):
    rows, cols = w.shape[-2:]
    tr, tc = _tile2(rows, cols)

    def body(me_ref, w_ref, m_ref, v_ref, own_ref, sib_ref, r_ref, g_ref, d_ref, nm_ref, nv_ref):
        gv = own_ref[0] + sib_ref[0]
        for j in range(3):
            gv = gv + r_ref[j].astype(F32)
        nm = ADAM_B1 * m_ref[0] + (1.0 - ADAM_B1) * gv
        nv = ADAM_B2 * v_ref[0] + (1.0 - ADAM_B2) * (gv * gv)
        m_hat = nm / (1.0 - ADAM_B1 ** ADAM_STEP)
        v_hat = nv / (1.0 - ADAM_B2 ** ADAM_STEP)
        g_ref[0] = gv
        d_ref[0] = -ADAM_LR * (m_hat / (jnp.sqrt(v_hat) + ADAM_EPS) + ADAM_WD * w_ref[0])
        nm_ref[0] = nm
        nv_ref[0] = nv

    one = pl.BlockSpec((1, tr, tc), lambda i, j, me_ref: (0, i, j))
    out = jax.ShapeDtypeStruct((1, rows, cols), F32)
    return pl.pallas_call(
        body,
        grid_spec=pltpu.PrefetchScalarGridSpec(
            num_scalar_prefetch=1, grid=(rows // tr, cols // tc),
            in_specs=[one, one, one, pl.BlockSpec((1, tr, tc), lambda i, j, me_ref: (me_ref[0], i, j)),
                      pl.BlockSpec((1, tr, tc), lambda i, j, me_ref: (me_ref[1], i, j)),
                      pl.BlockSpec((3, tr, tc), lambda i, j, me_ref: (0, i, j))],
            out_specs=(one,) * 4),
        out_shape=(out,) * 4, name=name, compiler_params=_params(2),
    )(me, w, m, v, grads, from_sibling, received)


SMALL_VECTORS = ("ffn1_norm", "mix_norm", "ffn2_norm", "final_norm")


def _pack_small(gs):
    row = jnp.concatenate([gs["gdn_a_log"].reshape(-1), gs["gdn_dt_bias"].reshape(-1), gs["gdn_out_norm"].reshape(-1)])
    rows = [gs[n].reshape(1, D_MODEL) for n in SMALL_VECTORS]
    rows.append(jnp.pad(row, (0, D_MODEL - row.shape[0])).reshape(1, D_MODEL))
    rows.append(gs["gdn_conv_w"].reshape(-1, D_MODEL))
    packed = jnp.concatenate(rows, axis=0)
    return jnp.pad(packed, ((0, SMALL_ROWS - packed.shape[0]), (0, 0)))


def _unpack_small(packed):
    out = {n: packed[i].reshape(1, D_MODEL) for i, n in enumerate(SMALL_VECTORS)}
    row = packed[len(SMALL_VECTORS)]
    out["gdn_a_log"] = row[:GDN_HEADS].reshape(1, GDN_HEADS)
    out["gdn_dt_bias"] = row[GDN_HEADS:2 * GDN_HEADS].reshape(1, GDN_HEADS)
    out["gdn_out_norm"] = row[2 * GDN_HEADS:2 * GDN_HEADS + GDN_HEAD_DIM].reshape(1, GDN_HEAD_DIM)
    first = len(SMALL_VECTORS) + 1
    out["gdn_conv_w"] = packed[first:first + GDN_CONV * 3].reshape(GDN_CONV, 3 * GDN_WIDTH)
    return out


WEIGHTS = ("ffn1_norm", "ffn1_w_gate", "ffn1_w_up", "ffn1_w_down", "mix_norm", "w_in", "gdn_conv_w", "gdn_a_log",
           "gdn_dt_bias", "gdn_out_norm", "w_branch_a", "w_branch_b", "w_out", "ffn2_norm", "ffn2_w_gate",
           "ffn2_w_up", "ffn2_w_down", "final_norm")


def kernel(x, ffn1_norm, ffn1_w_gate, ffn1_w_up, ffn1_w_down, mix_norm, w_in, gdn_conv_w, gdn_a_log, gdn_dt_bias, gdn_out_norm, w_branch_a, w_branch_b, w_out, ffn2_norm, ffn2_w_gate, ffn2_w_up, ffn2_w_down, final_norm, loss_target, m_ffn1_norm, m_ffn1_w_gate, m_ffn1_w_up, m_ffn1_w_down, m_mix_norm, m_w_in, m_gdn_conv_w, m_gdn_a_log, m_gdn_dt_bias, m_gdn_out_norm, m_w_branch_a, m_w_branch_b, m_w_out, m_ffn2_norm, m_ffn2_w_gate, m_ffn2_w_up, m_ffn2_w_down, m_final_norm, v_ffn1_norm, v_ffn1_w_gate, v_ffn1_w_up, v_ffn1_w_down, v_mix_norm, v_w_in, v_gdn_conv_w, v_gdn_a_log, v_gdn_dt_bias, v_gdn_out_norm, v_w_branch_a, v_w_branch_b, v_w_out, v_ffn2_norm, v_ffn2_w_gate, v_ffn2_w_up, v_ffn2_w_down, v_final_norm):
    given = dict(locals())
    px, py, pc = _position()
    big_names = list(BIG_WEIGHTS)

    def shard_view(a, n):
        if n == "w_in":
            return a.transpose(2, 0, 1)
        return a.transpose(0, 2, 1) if n in TRANSPOSED else a

    def shard_unview(a, n):
        if n == "w_in":
            return a.transpose(1, 2, 0)
        return a.transpose(0, 2, 1) if n in TRANSPOSED else a

    me = 4 * px + 2 * py + pc
    me_index = me.astype(jnp.int32).reshape(1)
    late = [n for n in big_names if n.startswith("ffn2")]
    early = [n for n in big_names if n not in late]
    shards = {n: shard_view(given[n], n).reshape(given[n].shape[-1 if n in TRANSPOSED else -2], -1).astype(BF16)
              for n in big_names}
    first = [n for n in early if n.startswith("ffn1")]
    middle = [n for n in early if n not in first]
    first_slabs, first_done = all_gather_shards([shards[n] for n in first], "gather_ffn1")
    shards["gdn_conv_w"] = gdn_conv_w[0]
    middle_all = middle + ["gdn_conv_w"]
    middle_gather = direct_exchange_start([_after(shards[n], first_done) for n in middle_all], "near",
                                          "gather_mixer_start")
    ffn1_norm = _after(ffn1_norm, middle_gather[4])
    def in_chunks(slabs):
        return slabs.reshape(-1, FFN_CHUNK, D_MODEL)

    def in_slabs(chunks):
        return chunks.reshape(N_DEV, -1, D_MODEL)

    w = {n: in_chunks(slab) for n, slab in zip(first, first_slabs)}
    x1, ffn1_saved = ffn_forward(x[0], ffn1_norm, w, "ffn1")
    near_lands = direct_exchange_wait(*middle_gather[:4], x1, "near", "gather_mixer_wait")[:-1]
    near_lands = [lax.dynamic_update_slice(land, shards[n][None], (me, 0, 0)) for n, land in zip(middle_all, near_lands)]
    middle_slabs, middle_done = forward_to_sibling(near_lands, "gather_mixer_forward")
    gathered = dict(zip(middle_all, middle_slabs))
    late_gather = direct_exchange_start([_after(shards[n], middle_done) for n in late], "gather", "gather_ffn2_start")
    w["w_in_t"] = gathered["w_in"].reshape(-1, D_MODEL)
    w["w_branch_a"] = gathered["w_branch_a"].transpose(1, 0, 2).reshape(-1, D_MODEL)
    w["w_branch_b"] = gathered["w_branch_b"].reshape(D_MODEL, D_MODEL)
    w["w_out"] = gathered["w_out"].reshape(D_MODEL, D_MODEL)
    conv_full = gathered["gdn_conv_w"].transpose(1, 0, 2).reshape(GDN_CONV, 3 * GDN_WIDTH)
    small = dict(mix_norm=_after(mix_norm, late_gather[4]), gdn_a_log=gdn_a_log, gdn_dt_bias=gdn_dt_bias,
                 gdn_out_norm=gdn_out_norm, gdn_conv_w=conv_full)

    x2, mixer_saved = mixer_forward(x1, w, small)
    late_lands = direct_exchange_wait(*late_gather[:4], x2, "gather", "gather_ffn2_wait")
    for n, land in zip(late, late_lands):
        w[n] = in_chunks(lax.dynamic_update_slice(land, shards[n][None], (me, 0, 0)))
    x3, ffn2_saved = ffn_forward(x2, ffn2_norm, w, "ffn2")
    loss_local, dx3, g_final = loss_head(x3, loss_target[0], final_norm.reshape(1, D_MODEL))
    loss = lax.psum(loss_local, ("x", "y", "c"))
    dx2, g_ffn2_norm, (dw2, dw2_f32) = ffn_backward(dx3, ffn2_saved, ffn2_norm, w, "ffn2", with_payload=True)
    late_scatter = direct_exchange_start([in_slabs(g) for g in dw2], "scatter", "rs_ffn2_start")
    w_after = dict(w, w_out=_after(w["w_out"], late_scatter[4]))
    dx1, g_w = mixer_backward(dx2, mixer_saved, w_after, small)
    middle = ["w_in", "w_branch_a", "w_branch_b", "w_out"]
    g_big = dict(w_in=g_w["w_in_t"].reshape(N_DEV, -1, D_MODEL),
                 w_branch_a=g_w["w_branch_a"].reshape(-1, N_DEV, D_MODEL // N_DEV).transpose(1, 0, 2),
                 w_branch_b=g_w["w_branch_b"].reshape(N_DEV, -1, D_MODEL),
                 w_out=g_w["w_out"].reshape(N_DEV, -1, D_MODEL))
    own = {n: lax.dynamic_index_in_dim(in_slabs(g), me, 0, keepdims=True) for n, g in zip(late, dw2_f32)}
    own.update({n: lax.dynamic_index_in_dim(g_big[n], me, 0, keepdims=True) for n in middle[1:]})
    in_rows = g_w["w_in_t"].shape[0] // N_DEV
    own["w_in"] = lax.dynamic_slice(g_w["w_in_t"], (me * in_rows, 0), (in_rows, D_MODEL))[None]
    middle_scatter = direct_exchange_start([g_big[n].astype(BF16) for n in middle], "scatter", "rs_mixer_start")
    grad_x, g_ffn1_norm, dw1 = ffn_backward(dx1, ffn1_saved, _after(ffn1_norm, middle_scatter[4]), w, "ffn1")
    g_small = dict(ffn1_norm=g_ffn1_norm, ffn2_norm=g_ffn2_norm, final_norm=g_final,
                   **{n: g_w[n] for n in ("mix_norm", "gdn_a_log", "gdn_dt_bias", "gdn_out_norm", "gdn_conv_w")})

    first = [n for n in early if n.startswith("ffn1")]
    g_list = [in_slabs(g) for g in dw1]
    core = pc.astype(jnp.int32).reshape(1)
    me_and_chip = jnp.stack([me, 2 * px + py]).astype(jnp.int32)
    from_sibling = exchange_with_sibling(g_list)
    partials = [add_sibling(g, r, core, "rs_add_" + n) for n, g, r in zip(first, g_list, from_sibling)]
    first_chips = direct_exchange_start(partials, "chips", "rs_ffn1_start")

    def state_of(n):
        return [shard_view(given[p + n], n) for p in ("", "m_", "v_")]

    results = {}
    late_received = direct_exchange_wait(*late_scatter[:4], first_chips[4], "scatter", "rs_ffn2_wait")
    middle_received = direct_exchange_wait(*middle_scatter[:4], first_chips[4], "scatter", "rs_mixer_wait")
    for n, recv in zip(late + middle, list(late_received[:-1]) + list(middle_received[:-1])):
        outs = adamw_direct(*state_of(n), own[n], recv, "adamw_" + n)
        results[n] = tuple(shard_unview(o, n) for o in outs)

    done = results["w_out"][1]
    from_chips = direct_exchange_wait(*first_chips[:4], done, "chips", "rs_ffn1_wait")
    for n, g, sib, recv in zip(first, g_list, from_sibling, from_chips):
        outs = adamw_summed(*state_of(n), g, sib, recv, me_and_chip, "adamw_" + n)
        results[n] = tuple(shard_unview(o, n) for o in outs)

    small_sum = _unpack_small(all_reduce_small(_after(_pack_small(g_small), from_chips[-1])))
    conv_cols = CONV_SHARD[1]
    small_sum["gdn_conv_w"] = lax.dynamic_slice(small_sum["gdn_conv_w"], (0, me * conv_cols), (GDN_CONV, conv_cols))
    for n in WEIGHTS:
        if n not in results:
            g = small_sum[n].reshape(given[n].shape)
            results[n] = (g,) + adamw(given[n], g, given["m_" + n], given["v_" + n], "adamw_" + n)

    outs = [[results[n][i] for n in WEIGHTS] for i in range(4)]
    return (loss, grad_x[None], *outs[0], *outs[1], *outs[2], *outs[3])
```

```python
import jax
import jax.numpy as jnp
from jax import lax
from jax.experimental import pallas as pl
from jax.experimental.pallas import tpu as pltpu

F32 = jnp.float32
BF16 = jnp.bfloat16
HI = lax.Precision.HIGHEST
MESH = pl.DeviceIdType.MESH

N_DEV = 8
D_MODEL = 1024
EPS = 1e-6
ROPE_THETA = 10000.0
DSW_DILATIONS = (1, 4, 16)
DSW_HEADS_PER_GROUP = 4
DSW_HEAD_DIM = 64
DSW_BLOCK = 128
GDN_HEADS = 8
GDN_HEAD_DIM = 128
GDN_WIDTH = 1024
GDN_CONV = 4
GDN_CHUNK = 64

ADAM_LR = 0.001
ADAM_B1 = 0.9
ADAM_B2 = 0.999
ADAM_EPS = 1e-08
ADAM_WD = 0.01
ADAM_STEP = 10

VMEM_LIMIT_BYTES = 56 * 1024 * 1024
LANES = 128

NN = (((1,), (0,)), ((), ()))
NT = (((1,), (1,)), ((), ()))
TN = (((0,), (0,)), ((), ()))


def _params(n_grid):
    return pltpu.CompilerParams(dimension_semantics=("arbitrary",) * n_grid, vmem_limit_bytes=VMEM_LIMIT_BYTES)


def _tile(n, pref):
    best = None
    t = LANES
    while t <= min(n, pref):
        if n % t == 0:
            best = t
        t += LANES
    return n if best is None else best


def _weight_grad(a, g, name):
    n_tokens, m = a.shape
    n = g.shape[1]
    tm, tn = _tile(m, 512), _tile(n, 512)

    def body(a_ref, g_ref, o_ref):
        o_ref[...] = lax.dot_general(a_ref[...].astype(BF16), g_ref[...].astype(BF16), TN, preferred_element_type=F32)

    return pl.pallas_call(
        body, grid=(m // tm, n // tn),
        in_specs=[pl.BlockSpec((n_tokens, tm), lambda i, j: (0, i)), pl.BlockSpec((n_tokens, tn), lambda i, j: (0, j))],
        out_specs=pl.BlockSpec((tm, tn), lambda i, j: (i, j)),
        out_shape=jax.ShapeDtypeStruct((m, n), F32), name=name, compiler_params=_params(2),
    )(a, g)


def _rw_specs(arrs, tm, nblk):
    return [pl.BlockSpec((tm, a.shape[1] // nblk), lambda i, j: (i, j)) for a in arrs]


def _rowwise_fwd(fn, name, rows, consts, params, tm, nblk):
    n_rows = rows[0].shape[0]
    tm = min(tm, n_rows)
    ins = list(rows) + list(consts)
    avals = [jax.ShapeDtypeStruct((tm, a.shape[1] // nblk), a.dtype) for a in ins]
    avals += [jax.ShapeDtypeStruct(p.shape, p.dtype) for p in params]
    out_avals = jax.eval_shape(fn, *avals)
    n_in = len(ins) + len(params)

    def body(*refs):
        outs = fn(*[r[...] for r in refs[:n_in]])
        for r, o in zip(refs[n_in:], outs):
            r[...] = o.astype(r.dtype)

    return pl.pallas_call(
        body, grid=(n_rows // tm, nblk),
        in_specs=_rw_specs(ins, tm, nblk) + [pl.BlockSpec(p.shape, lambda i, j: (0, 0)) for p in params],
        out_specs=tuple(pl.BlockSpec((tm, o.shape[1]), lambda i, j: (i, j)) for o in out_avals),
        out_shape=tuple(jax.ShapeDtypeStruct((n_rows, o.shape[1] * nblk), o.dtype) for o in out_avals),
        name=name, compiler_params=_params(2),
    )(*ins, *params)


def _rowwise_bwd(fn, name, rows, consts, params, cts, tm, nblk):
    n_rows = rows[0].shape[0]
    tm = min(tm, n_rows)
    nr, nc, npar, nct = len(rows), len(consts), len(params), len(cts)

    def body(*refs):
        rv = [r[...] for r in refs[:nr]]
        cv = [r[...] for r in refs[nr:nr + nc]]
        pv = [r[...] for r in refs[nr + nc:nr + nc + npar]]
        ctv = [r[...] for r in refs[nr + nc + npar:nr + nc + npar + nct]]
        outs = refs[nr + nc + npar + nct:]
        _, vjp = jax.vjp(lambda *d: fn(*d[:nr], *cv, *d[nr:]), *rv, *pv)
        grads = vjp(tuple(ctv))
        for k in range(nr):
            outs[k][...] = grads[k]
        first = jnp.logical_and(pl.program_id(0) == 0, pl.program_id(1) == 0)
        for k in range(npar):
            ref = outs[nr + k]

            @pl.when(first)
            def _(ref=ref):
                ref[...] = jnp.zeros_like(ref)

            ref[...] += grads[nr + k]

    ins = list(rows) + list(consts)
    return pl.pallas_call(
        body, grid=(n_rows // tm, nblk),
        in_specs=(_rw_specs(ins, tm, nblk) + [pl.BlockSpec(p.shape, lambda i, j: (0, 0)) for p in params]
                  + _rw_specs(cts, tm, nblk)),
        out_specs=tuple(_rw_specs(rows, tm, nblk) + [pl.BlockSpec(p.shape, lambda i, j: (0, 0)) for p in params]),
        out_shape=tuple([jax.ShapeDtypeStruct(a.shape, F32) for a in rows]
                        + [jax.ShapeDtypeStruct(p.shape, F32) for p in params]),
        name=name, compiler_params=_params(2),
    )(*ins, *params, *cts)


def _merge_fn(ga, gb, pa, pb):
    return (jax.nn.sigmoid(ga) * pa + jax.nn.sigmoid(gb) * pb,)


def _outnorm_gate_fn(o, gate, gain):
    y = o * lax.rsqrt(jnp.mean(o * o, axis=-1, keepdims=True) + EPS) * gain
    return (y * (gate * jax.nn.sigmoid(gate)),)


def _beta_decay_fn(beta_raw, decay_raw, a_log, dt_bias):
    z = decay_raw + dt_bias
    softplus = jnp.maximum(z, 0.0) + jnp.log(1.0 + jnp.exp(-jnp.abs(z)))
    g = -jnp.exp(a_log) * softplus
    rows = g.shape[0]
    ii = lax.broadcasted_iota(jnp.int32, (rows, rows), 0)
    jj = lax.broadcasted_iota(jnp.int32, (rows, rows), 1)
    same_chunk_before = jnp.logical_and(jj <= ii, jj // GDN_CHUNK == ii // GDN_CHUNK).astype(F32)
    gcum = lax.dot_general(same_chunk_before, g, NN, precision=HI, preferred_element_type=F32)
    return jax.nn.sigmoid(beta_raw), gcum


def _combine_fn(o0, o1, o2, l0, l1, l2):
    m = lax.stop_gradient(jnp.maximum(jnp.maximum(l0, l1), l2))
    e0, e1, e2 = jnp.exp(l0 - m), jnp.exp(l1 - m), jnp.exp(l2 - m)
    return ((e0 * o0 + e1 * o1 + e2 * o2) / (e0 + e1 + e2),)


def _loss_fn(x, target, gain):
    y = x * lax.rsqrt(jnp.mean(x * x, axis=-1, keepdims=True) + EPS) * gain
    err = y - target
    return (0.5 * jnp.mean(err * err, axis=-1, keepdims=True),)


def _rotate(v, cos, sin):
    half = DSW_HEAD_DIM // 2
    lane = lax.broadcasted_iota(jnp.int32, cos.shape, 1)
    low = (lane % DSW_HEAD_DIM) < half
    slabs = []
    for s in range(v.shape[1] // LANES):
        x = v[:, s * LANES:(s + 1) * LANES]
        swapped = jnp.where(low, pltpu.roll(x, LANES - half, 1), pltpu.roll(x, half, 1))
        slabs.append(x * cos + swapped * sin)
    return jnp.concatenate(slabs, axis=1)


def _rope_tables(n_tokens):
    half = DSW_HEAD_DIM // 2
    inv_freq = ROPE_THETA ** (-jnp.arange(half, dtype=F32) / half)
    ang = jnp.arange(n_tokens, dtype=F32)[:, None] * inv_freq[None, :]
    cos, sin = jnp.cos(ang), jnp.sin(ang)
    return jnp.tile(jnp.concatenate([cos, cos], 1), (1, 2)), jnp.tile(jnp.concatenate([-sin, sin], 1), (1, 2))


def _attn_probs(q, kp, kc, group, n, n_blocks):
    blk = DSW_BLOCK
    k = _each(lambda a, b: jnp.concatenate([a, b], axis=0).astype(BF16), kp, kc)
    s = _each(lambda a, b: lax.dot_general(a.astype(BF16), b, NT, preferred_element_type=F32)
              * (DSW_HEAD_DIM ** -0.5), q, k)
    per_seq = [n_blocks // d for d in DSW_DILATIONS]
    blocks_per_seq = jnp.where(group == 0, per_seq[0], jnp.where(group == 1, per_seq[1], per_seq[2]))
    first = (n % blocks_per_seq) == 0
    qi = lax.broadcasted_iota(jnp.int32, (blk, 2 * blk), 0)
    kj = lax.broadcasted_iota(jnp.int32, (blk, 2 * blk), 1)
    dist = qi + blk - kj
    valid = (dist >= 0) & (dist <= blk) & jnp.logical_or(kj >= blk, jnp.logical_not(first))
    s = _each(lambda a: jnp.where(valid, a, -1e30), s)
    m = _each(lambda a: jnp.max(a, axis=-1, keepdims=True), s)
    p = _each(lambda a, b: jnp.exp(a - b), s, m)
    l = _each(lambda a: jnp.sum(a, axis=-1, keepdims=True), p)
    return _each(lambda a, b: a / b, p, l), _each(lambda a, b: a + jnp.log(b), m, l), k


GROUP_WIDTH = DSW_HEADS_PER_GROUP * DSW_HEAD_DIM


def _attn_specs(n_tokens):
    blk = DSW_BLOCK
    cur = pl.BlockSpec((1, blk, GROUP_WIDTH), lambda g, n: (g, n, 0))
    prev = pl.BlockSpec((1, blk, GROUP_WIDTH), lambda g, n: (g, jnp.maximum(n - 1, 0), 0))
    return cur, prev


def _heads_of(ref):
    x = ref[0]
    return [x[:, h * DSW_HEAD_DIM:(h + 1) * DSW_HEAD_DIM] for h in range(DSW_HEADS_PER_GROUP)]


def _group_of(heads):
    return jnp.concatenate(heads, axis=1)


def _attn_fwd(q, k, v):
    n_groups, n_tokens, _ = q.shape
    cur, prev = _attn_specs(n_tokens)

    def body(q_ref, kp_ref, kc_ref, vp_ref, vc_ref, o_ref, l_ref):
        p, lse, _ = _attn_probs(_heads_of(q_ref), _heads_of(kp_ref), _heads_of(kc_ref),
                                pl.program_id(0), pl.program_id(1), n_tokens // DSW_BLOCK)
        vv = _each(lambda a, b: jnp.concatenate([a, b], axis=0).astype(BF16), _heads_of(vp_ref), _heads_of(vc_ref))
        o = _each(lambda a, b: lax.dot_general(a.astype(BF16), b, NN, preferred_element_type=F32), p, vv)
        lse_wide = _each(lambda a: jnp.broadcast_to(a, (DSW_BLOCK, DSW_HEAD_DIM)), lse)
        o_ref[0] = _group_of(o)
        l_ref[0] = _group_of(lse_wide)

    return pl.pallas_call(
        body, grid=(n_groups, n_tokens // DSW_BLOCK), in_specs=[cur, prev, cur, prev, cur],
        out_specs=(cur, cur), out_shape=(jax.ShapeDtypeStruct(q.shape, F32), jax.ShapeDtypeStruct(q.shape, F32)),
        name="attn_fwd", compiler_params=_params(2),
    )(q, k, k, v, v)


def _attn_bwd(q, k, v, do, dlse):
    n_groups, n_tokens, _ = q.shape
    nblk = n_tokens // DSW_BLOCK
    cur, prev = _attn_specs(n_tokens)
    part = pl.BlockSpec((1, 1, 2 * DSW_BLOCK, GROUP_WIDTH), lambda g, n: (g, n, 0, 0))
    scale = DSW_HEAD_DIM ** -0.5

    def body(q_ref, kp_ref, kc_ref, vp_ref, vc_ref, do_ref, dl_ref, dq_ref, dk_ref, dv_ref):
        qs = _heads_of(q_ref)
        p, _, kb = _attn_probs(qs, _heads_of(kp_ref), _heads_of(kc_ref), pl.program_id(0), pl.program_id(1), nblk)
        qb = _each(lambda a: a.astype(BF16), qs)
        vv = _each(lambda a, b: jnp.concatenate([a, b], axis=0).astype(BF16), _heads_of(vp_ref), _heads_of(vc_ref))
        dob = _each(lambda a: a.astype(BF16), _heads_of(do_ref))
        dp = _each(lambda a, b: lax.dot_general(a, b, NT, preferred_element_type=F32), dob, vv)
        dv = _each(lambda a, b: lax.dot_general(a.astype(BF16), b, TN, preferred_element_type=F32), p, dob)
        dl = _each(lambda a: jnp.sum(a, axis=-1, keepdims=True), _heads_of(dl_ref))
        ds = _each(lambda a, b, c: (a * (b - jnp.sum(b * a, axis=-1, keepdims=True) + c) * scale).astype(BF16),
                   p, dp, dl)
        dq = _each(lambda a, b: lax.dot_general(a, b, NN, preferred_element_type=F32), ds, kb)
        dk = _each(lambda a, b: lax.dot_general(a, b, TN, preferred_element_type=F32), ds, qb)
        dq_ref[0] = _group_of(dq)
        dk_ref[0, 0] = _group_of(dk)
        dv_ref[0, 0] = _group_of(dv)

    partial_shape = jax.ShapeDtypeStruct((n_groups, nblk, 2 * DSW_BLOCK, GROUP_WIDTH), F32)
    dq, dkp, dvp = pl.pallas_call(
        body, grid=(n_groups, nblk), in_specs=[cur, prev, cur, prev, cur, cur, cur],
        out_specs=(cur, part, part), out_shape=(jax.ShapeDtypeStruct(q.shape, F32), partial_shape, partial_shape),
        name="attn_bwd", compiler_params=_params(2),
    )(q, k, k, v, v, do, dlse)

    def fold(partial):
        own = partial[:, :, DSW_BLOCK:]
        from_next = jnp.pad(partial[:, 1:, :DSW_BLOCK], ((0, 0), (0, 1), (0, 0), (0, 0)))
        return (own + from_next).reshape(n_groups, n_tokens, GROUP_WIDTH)

    return dq, fold(dkp), fold(dvp)


def _to_heads(a):
    n_tokens = a.shape[0]
    outs = []
    for gi, d in enumerate(DSW_DILATIONS):
        blk = a[:, gi * GROUP_WIDTH:(gi + 1) * GROUP_WIDTH].reshape(n_tokens // d, d, GROUP_WIDTH)
        outs.append(blk.transpose(1, 0, 2).reshape(1, n_tokens, GROUP_WIDTH))
    return jnp.concatenate(outs, 0)


def _from_heads(a):
    n_tokens = a.shape[1]
    return [a[gi].reshape(d, n_tokens // d, GROUP_WIDTH).transpose(1, 0, 2).reshape(n_tokens, GROUP_WIDTH)
            for gi, d in enumerate(DSW_DILATIONS)]


CONV_TILE = 512


def _shift_down(x, k, rows):
    return x if k == 0 else jnp.where(rows >= k, pltpu.roll(x, k, 0), 0.0)


def _shift_up(x, k, rows):
    n = x.shape[0]
    return x if k == 0 else jnp.where(rows < n - k, pltpu.roll(x, n - k, 0), 0.0)


def _conv_pre(x, w):
    rows = lax.broadcasted_iota(jnp.int32, x.shape, 0)
    acc = x * w[GDN_CONV - 1:GDN_CONV]
    for k in range(1, GDN_CONV):
        acc = acc + _shift_down(x, k, rows) * w[GDN_CONV - 1 - k:GDN_CONV - k]
    return acc, rows


def _conv_fwd(x, w):
    n_tokens, width = x.shape
    big = pl.BlockSpec((n_tokens, CONV_TILE), lambda j: (0, j))
    wsp = pl.BlockSpec((GDN_CONV, CONV_TILE), lambda j: (0, j))

    def body(x_ref, w_ref, o_ref):
        acc, _ = _conv_pre(x_ref[...], w_ref[...])
        o_ref[...] = acc * jax.nn.sigmoid(acc)

    return pl.pallas_call(
        body, grid=(width // CONV_TILE,), in_specs=[big, wsp], out_specs=big,
        out_shape=jax.ShapeDtypeStruct(x.shape, F32), name="conv_fwd", compiler_params=_params(1),
    )(x, w)


def _conv_bwd(x, w, dy):
    n_tokens, width = x.shape
    big = pl.BlockSpec((n_tokens, CONV_TILE), lambda j: (0, j))
    wsp = pl.BlockSpec((GDN_CONV, CONV_TILE), lambda j: (0, j))

    def body(x_ref, w_ref, dy_ref, dx_ref, dw_ref):
        xv, wv = x_ref[...], w_ref[...]
        acc, rows = _conv_pre(xv, wv)
        sg = jax.nn.sigmoid(acc)
        dacc = dy_ref[...].astype(F32) * (sg + acc * sg * (1.0 - sg))
        dx = dacc * wv[GDN_CONV - 1:GDN_CONV]
        for k in range(1, GDN_CONV):
            dx = dx + _shift_up(dacc, k, rows) * wv[GDN_CONV - 1 - k:GDN_CONV - k]
        dx_ref[...] = dx.astype(BF16)
        for k in range(GDN_CONV):
            dw_ref[GDN_CONV - 1 - k:GDN_CONV - k, :] = jnp.sum(dacc * _shift_down(xv, k, rows), axis=0, keepdims=True)

    return pl.pallas_call(
        body, grid=(width // CONV_TILE,), in_specs=[big, wsp, big], out_specs=(big, wsp),
        out_shape=(jax.ShapeDtypeStruct(x.shape, BF16), jax.ShapeDtypeStruct(w.shape, F32)),
        name="conv_bwd", compiler_params=_params(1),
    )(x, w, dy)


def _dot3(a, b, dn=NN):
    return lax.dot_general(a, b, dn, precision=lax.Precision.HIGH, preferred_element_type=F32)


def _bf16_dot(a, b, dn):
    return lax.dot_general(a.astype(BF16), b.astype(BF16), dn, preferred_element_type=F32)


_DOT_GRADS = {NN: (("g", "b", NT), ("a", "g", TN)), NT: (("g", "b", NN), ("g", "a", TN)),
              TN: (("b", "g", NT), ("a", "g", NN))}


def _make_bdot(dn):
    @jax.custom_vjp
    def op(a, b):
        return _bf16_dot(a, b, dn)

    def fwd(a, b):
        return op(a, b), (a, b)

    def bwd(saved, g):
        vals = dict(a=saved[0], b=saved[1], g=g)
        return tuple(_bf16_dot(vals[x], vals[y], form) for x, y, form in _DOT_GRADS[dn])

    op.defvjp(fwd, bwd)
    return op


_BDOTS = {dn: _make_bdot(dn) for dn in (NN, NT, TN)}


def _bdot(a, b, dn=NN):
    return _BDOTS[dn](a, b)


def _each(fn, *lists):
    return [fn(*items) for items in zip(*lists)]


@jax.custom_vjp
def _known_inverse(m, inverse):
    return inverse


def _known_inverse_fwd(m, inverse):
    return inverse, inverse


def _known_inverse_bwd(inverse, d_inverse):
    return -_dot3(_dot3(inverse, d_inverse, TN), inverse, NT), jnp.zeros_like(inverse)


_known_inverse.defvjp(_known_inverse_fwd, _known_inverse_bwd)


def _gdn_chunks(q, k, v, b, gcum, state, inverse=None):
    c = GDN_CHUNK
    ii = lax.broadcasted_iota(jnp.int32, (c, c), 0)
    jj = lax.broadcasted_iota(jnp.int32, (c, c), 1)
    qn = _each(lambda x: x * lax.rsqrt(jnp.sum(x * x, axis=-1, keepdims=True) + EPS) * (GDN_HEAD_DIM ** -0.5), q)
    kn = _each(lambda x: x * lax.rsqrt(jnp.sum(x * x, axis=-1, keepdims=True) + EPS), k)
    gcum_i = _each(lambda x: jnp.broadcast_to(x, (c, c)), gcum)
    gcum_j = _each(jnp.transpose, gcum_i)
    decay = _each(lambda x, y: jnp.exp(jnp.where(jj <= ii, x - y, -1e30)), gcum_i, gcum_j)
    g_last = _each(lambda x: x[c - 1:c, :], gcum)
    e_gcum = _each(jnp.exp, gcum)
    kbeta = _each(lambda x, y: x * y, kn, b)
    vbeta = _each(lambda x, y: x * y, v, b)
    m = _each(lambda x, y, d: jnp.where(jj < ii, _bdot(x, y, NT) * d, 0.0), kbeta, kn, decay)
    if inverse is not None:
        inv = _each(_known_inverse, m, inverse)
    else:
        eye = (ii == jj).astype(F32)
        inv = _each(lambda x: eye - x, m)
        power = _each(lambda x: _dot3(x, x), m)
        for step in range(5):
            inv = _each(lambda x, p: x + _dot3(x, p), inv, power)
            if step < 4:
                power = _each(lambda p: _dot3(p, p), power)
    u = _each(_dot3, inv, vbeta)
    w = _each(lambda x, y, e: _dot3(x, y * e), inv, kbeta, e_gcum)
    a_qk = _each(lambda x, y, d: _bdot(x, y, NT) * d, qn, kn, decay)
    v_new = _each(lambda x, y, s: x - _bdot(y, s), u, w, state)
    o = _each(lambda x, e, s, a, vn: _bdot(x * e, s) + _bdot(a, vn), qn, e_gcum, state, a_qk, v_new)
    new_state = _each(lambda s, gl, x, gc, vn: s * jnp.exp(gl) + _bdot(x * jnp.exp(gl - gc), vn, TN),
                      state, g_last, kn, gcum, v_new)
    return o, new_state, inv


GDN_HEADS_PER_STEP = 8


GDN_TIME_TILE = 256


def _gdn_specs(n_tokens, reverse):
    hb, hd, tt = GDN_HEADS_PER_STEP, GDN_HEAD_DIM, GDN_TIME_TILE
    nb, nt = GDN_HEADS // hb, n_tokens // tt

    def when(t):
        return nt - 1 - t if reverse else t

    q = pl.BlockSpec((tt, hb * hd), lambda h, t: (when(t), h))
    k = pl.BlockSpec((tt, hb * hd), lambda h, t: (when(t), nb + h))
    v = pl.BlockSpec((tt, hb * hd), lambda h, t: (when(t), 2 * nb + h))
    vec = pl.BlockSpec((tt, hb), lambda h, t: (when(t), h))
    states = pl.BlockSpec((hb, tt // GDN_CHUNK, hd, hd), lambda h, t: (h, when(t), 0, 0))
    inverses = pl.BlockSpec((hb, tt // GDN_CHUNK, GDN_CHUNK, GDN_CHUNK), lambda h, t: (h, when(t), 0, 0))
    return q, k, v, vec, states, inverses


def _gdn_fwd(qkv, beta, g):
    n_tokens = qkv.shape[0]
    hb, hd, tt = GDN_HEADS_PER_STEP, GDN_HEAD_DIM, GDN_TIME_TILE
    n_chunks = tt // GDN_CHUNK
    q_s, k_s, v_s, vec, st, inv_s = _gdn_specs(n_tokens, False)

    def body(q_ref, k_ref, v_ref, b_ref, g_ref, o_ref, st_ref, inv_ref, state):
        @pl.when(pl.program_id(1) == 0)
        def _():
            state[...] = jnp.zeros_like(state)

        def step(c, carry):
            r = pl.ds(pl.multiple_of(c * GDN_CHUNK, GDN_CHUNK), GDN_CHUNK)
            cols = [slice(h * hd, (h + 1) * hd) for h in range(hb)]
            old = [state[h] for h in range(hb)]
            o, new, inv = _gdn_chunks(
                [q_ref[r, cs] for cs in cols], [k_ref[r, cs] for cs in cols], [v_ref[r, cs] for cs in cols],
                [b_ref[r, h:h + 1] for h in range(hb)], [g_ref[r, h:h + 1] for h in range(hb)], old)
            for h in range(hb):
                st_ref[h, c] = old[h]
                inv_ref[h, c] = inv[h]
                o_ref[r, cols[h]] = o[h]
                state[h] = new[h]
            return carry

        lax.fori_loop(0, n_chunks, step, 0)

    n_all = n_tokens // GDN_CHUNK
    return pl.pallas_call(
        body, grid=(GDN_HEADS // hb, n_tokens // tt), in_specs=[q_s, k_s, v_s, vec, vec], out_specs=(q_s, st, inv_s),
        out_shape=(jax.ShapeDtypeStruct((n_tokens, GDN_WIDTH), F32),
                   jax.ShapeDtypeStruct((GDN_HEADS, n_all, hd, hd), F32),
                   jax.ShapeDtypeStruct((GDN_HEADS, n_all, GDN_CHUNK, GDN_CHUNK), F32)),
        scratch_shapes=[pltpu.VMEM((hb, hd, hd), F32)],
        name="gdn_fwd", compiler_params=_params(2),
    )(qkv, qkv, qkv, beta, g)


def _gdn_bwd(qkv, beta, g, states, inverses, do):
    n_tokens = qkv.shape[0]
    hb, hd, tt = GDN_HEADS_PER_STEP, GDN_HEAD_DIM, GDN_TIME_TILE
    n_chunks = tt // GDN_CHUNK
    q_s, k_s, v_s, vec, st, inv_s = _gdn_specs(n_tokens, True)

    assert hb == GDN_HEADS

    def body(q_ref, k_ref, v_ref, b_ref, g_ref, st_ref, inv_ref, do_ref, dqkv_ref, db_ref, dg_ref, dstate):
        @pl.when(pl.program_id(1) == 0)
        def _():
            dstate[...] = jnp.zeros_like(dstate)

        def step(i, carry):
            c = n_chunks - 1 - i
            r = pl.ds(pl.multiple_of(c * GDN_CHUNK, GDN_CHUNK), GDN_CHUNK)
            cols = [slice(h * hd, (h + 1) * hd) for h in range(hb)]
            args = ([q_ref[r, cs] for cs in cols], [k_ref[r, cs] for cs in cols], [v_ref[r, cs] for cs in cols],
                    [b_ref[r, h:h + 1] for h in range(hb)], [g_ref[r, h:h + 1] for h in range(hb)],
                    [st_ref[h, c] for h in range(hb)])
            saved = [inv_ref[h, c] for h in range(hb)]
            cts = ([do_ref[r, cs] for cs in cols], [dstate[h] for h in range(hb)])
            dq, dk, dv, db, dg, dst = jax.vjp(lambda *a: _gdn_chunks(*a, inverse=saved)[:2], *args)[1](cts)
            for h in range(hb):
                for part, grad in enumerate((dq, dk, dv)):
                    dqkv_ref[r, pl.ds(part * GDN_WIDTH + h * hd, hd)] = grad[h].astype(BF16)
                db_ref[r, h:h + 1] = db[h]
                dg_ref[r, h:h + 1] = dg[h]
                dstate[h] = dst[h]
            return carry

        lax.fori_loop(0, n_chunks, step, 0)

    n_t = n_tokens // tt
    thin = jax.ShapeDtypeStruct(beta.shape, F32)
    return pl.pallas_call(
        body, grid=(GDN_HEADS // hb, n_t), in_specs=[q_s, k_s, v_s, vec, vec, st, inv_s, q_s],
        out_specs=(pl.BlockSpec((tt, 3 * GDN_WIDTH), lambda h, t: (n_t - 1 - t, 0)), vec, vec),
        out_shape=(jax.ShapeDtypeStruct(qkv.shape, BF16), thin, thin),
        scratch_shapes=[pltpu.VMEM((hb, hd, hd), F32)],
        name="gdn_bwd", compiler_params=_params(2),
    )(qkv, qkv, qkv, beta, g, states, inverses, do)


FFN_ROW_TILE = 256
FFN_CHUNK = 256
FFN_FWD_ROW_TILE = 512


def _resident(shape):
    return pl.BlockSpec(shape, lambda i: (0,) * len(shape), pipeline_mode=pl.Buffered(1))


def _ffn_fwd(x, gain, wg, wu, wd, name):
    n_tokens, d = x.shape
    n_shards, n, _ = wg.shape
    tm = FFN_FWD_ROW_TILE

    def body(x_ref, gain_ref, wg_ref, wu_ref, wd_ref, o_ref, g_ref, u_ref):
        xv = x_ref[...]
        h = (xv * lax.rsqrt(jnp.mean(xv * xv, axis=-1, keepdims=True) + EPS) * gain_ref[...]).astype(BF16)
        acc = jnp.zeros((tm, d), F32)
        for j in range(n_shards):
            g = lax.dot_general(h, wg_ref[j], NT, preferred_element_type=F32)
            u = lax.dot_general(h, wu_ref[j], NT, preferred_element_type=F32)
            g_ref[j] = g.astype(BF16)
            u_ref[j] = u.astype(BF16)
            a = (g * jax.nn.sigmoid(g) * u).astype(BF16)
            acc = acc + lax.dot_general(a, wd_ref[j], NN, preferred_element_type=F32)
        o_ref[...] = xv + 0.5 * acc

    row = pl.BlockSpec((tm, d), lambda i: (i, 0))
    hid = pl.BlockSpec((n_shards, tm, n), lambda i: (0, i, 0))
    return pl.pallas_call(
        body, grid=(n_tokens // tm,),
        in_specs=[row, _resident(gain.shape), _resident(wg.shape), _resident(wu.shape), _resident(wd.shape)],
        out_specs=(row, hid, hid),
        out_shape=(jax.ShapeDtypeStruct(x.shape, F32), jax.ShapeDtypeStruct((n_shards, n_tokens, n), BF16),
                   jax.ShapeDtypeStruct((n_shards, n_tokens, n), BF16)),
        name=name, compiler_params=_params(1),
    )(x, gain, wg, wu, wd)


def _ffn_bwd_rows(x, gain, dy, g, u, wg, wu, wd, name):
    n_tokens, d = x.shape
    n_shards, n, _ = wg.shape
    tm = FFN_ROW_TILE

    def body(x_ref, gain_ref, dy_ref, g_ref, u_ref, wg_ref, wu_ref, wd_ref,
             dx_ref, dgain_ref, h_ref, dyh_ref, a_ref, dg_ref, du_ref):
        xv, dyv, gain_v = x_ref[...], dy_ref[...], gain_ref[...]
        r = lax.rsqrt(jnp.mean(xv * xv, axis=-1, keepdims=True) + EPS)
        xhat = xv * r
        h_ref[...] = (xhat * gain_v).astype(BF16)
        dyh = (0.5 * dyv).astype(BF16)
        dyh_ref[...] = dyh
        dh = jnp.zeros((tm, d), F32)
        for j in range(n_shards):
            da = lax.dot_general(dyh, wd_ref[j], NT, preferred_element_type=F32)
            gv, uv = g_ref[j].astype(F32), u_ref[j].astype(F32)
            sg = jax.nn.sigmoid(gv)
            silu = gv * sg
            a_ref[j] = (silu * uv).astype(BF16)
            dg = (da * uv * (sg + silu * (1.0 - sg))).astype(BF16)
            du = (da * silu).astype(BF16)
            dg_ref[j] = dg
            du_ref[j] = du
            dh = dh + lax.dot_general(dg, wg_ref[j], NN, preferred_element_type=F32)
            dh = dh + lax.dot_general(du, wu_ref[j], NN, preferred_element_type=F32)
        dxhat = dh * gain_v
        dx_ref[...] = dyv + r * (dxhat - xhat * jnp.mean(dxhat * xhat, axis=-1, keepdims=True))

        @pl.when(pl.program_id(0) == 0)
        def _():
            dgain_ref[...] = jnp.zeros_like(dgain_ref)

        dgain_ref[...] += jnp.sum(dh * xhat, axis=0, keepdims=True)

    row = pl.BlockSpec((tm, d), lambda i: (i, 0))
    hid = pl.BlockSpec((n_shards, tm, n), lambda i: (0, i, 0))
    hid_shape = (n_shards, n_tokens, n)
    return pl.pallas_call(
        body, grid=(n_tokens // tm,),
        in_specs=[row, _resident(gain.shape), row, hid, hid, _resident(wg.shape), _resident(wu.shape),
                  _resident(wd.shape)],
        out_specs=(row, pl.BlockSpec(gain.shape, lambda i: (0, 0)), row, row, hid, hid, hid),
        out_shape=(jax.ShapeDtypeStruct(x.shape, F32), jax.ShapeDtypeStruct(gain.shape, F32),
                   jax.ShapeDtypeStruct(x.shape, BF16), jax.ShapeDtypeStruct(x.shape, BF16),
                   jax.ShapeDtypeStruct(hid_shape, BF16), jax.ShapeDtypeStruct(hid_shape, BF16),
                   jax.ShapeDtypeStruct(hid_shape, BF16)),
        name=name, compiler_params=_params(1),
    )(x, gain, dy, g, u, wg, wu, wd)


def _ffn_bwd_weights(h, dyh, a, dg, du, name, with_payload=False):
    n_chunks, n_tokens, n = a.shape
    d = h.shape[1]

    def body(h_ref, dyh_ref, a_ref, dg_ref, du_ref, *out_refs):
        hv = h_ref[...]
        vals = (lax.dot_general(dg_ref[0], hv, TN, preferred_element_type=F32),
                lax.dot_general(du_ref[0], hv, TN, preferred_element_type=F32),
                lax.dot_general(a_ref[0], dyh_ref[...], TN, preferred_element_type=F32))
        for ref, val in zip(out_refs[-3:], vals):
            ref[0] = val
        if with_payload:
            for ref, val in zip(out_refs[:3], vals):
                ref[0] = val.astype(BF16)

    hid = pl.BlockSpec((1, n_tokens, n), lambda j: (j, 0, 0))
    out = pl.BlockSpec((1, n, d), lambda j: (j, 0, 0))
    shapes = (jax.ShapeDtypeStruct((n_chunks, n, d), F32),) * 3
    if with_payload:
        shapes = (jax.ShapeDtypeStruct((n_chunks, n, d), BF16),) * 3 + shapes
    outs = pl.pallas_call(
        body, grid=(n_chunks,), in_specs=[_resident(h.shape), _resident(dyh.shape), hid, hid, hid],
        out_specs=(out,) * len(shapes), out_shape=shapes, name=name, compiler_params=_params(1),
    )(h, dyh, a, dg, du)
    return (outs[:3], outs[3:]) if with_payload else outs


IN_PIECES = (("wq_a", 0, 768), ("wk_a", 768, 1536), ("wv_a", 1536, 2304), ("w_qkvb", 2304, 5376),
             ("w_small", 5376, 5392), ("w_ggate", 5392, 6416), ("w_gatea", 6416, 7440), ("w_gateb", 7440, 8464))
IN_NAMES = tuple(name for name, _, _ in IN_PIECES)


def _in_rows(lo, hi):
    return lo, max(hi, lo + LANES)


N_ROTATED = 2


def _in_proj_fwd(x, gain, wt, cos, sin):
    n_tokens, d = x.shape
    tm = FFN_ROW_TILE
    rows = [_in_rows(lo, hi) for _, lo, hi in IN_PIECES]

    def body(x_ref, gain_ref, wt_ref, cos_ref, sin_ref, *o_refs):
        xv = x_ref[...]
        h = (xv * lax.rsqrt(jnp.mean(xv * xv, axis=-1, keepdims=True) + EPS) * gain_ref[...]).astype(BF16)
        for k, ((lo, hi), o_ref) in enumerate(zip(rows, o_refs)):
            z = lax.dot_general(h, wt_ref[lo:hi, :], NT, preferred_element_type=F32)
            o_ref[...] = _rotate(z, cos_ref[...], sin_ref[...]) if k < N_ROTATED else z

    tab = pl.BlockSpec((tm, LANES), lambda i: (i, 0))
    return pl.pallas_call(
        body, grid=(n_tokens // tm,),
        in_specs=[pl.BlockSpec((tm, d), lambda i: (i, 0)), _resident(gain.shape), _resident(wt.shape), tab, tab],
        out_specs=tuple(pl.BlockSpec((tm, hi - lo), lambda i: (i, 0)) for lo, hi in rows),
        out_shape=tuple(jax.ShapeDtypeStruct((n_tokens, hi - lo), F32) for lo, hi in rows),
        name="in_proj_fwd", compiler_params=_params(1),
    )(x, gain, wt, cos, sin)


def _in_proj_bwd_rows(x, gain, dres, dzs, wt, cos, sin):
    n_tokens, d = x.shape
    tm = FFN_ROW_TILE
    n = len(dzs)
    rows = [_in_rows(lo, hi) for _, lo, hi in IN_PIECES]

    def body(x_ref, gain_ref, dres_ref, cos_ref, sin_ref, *refs):
        dz_refs, wt_ref = refs[:n], refs[n]
        dx_ref, dgain_ref, h_ref = refs[n + 1:n + 4]
        unrotated_refs = refs[n + 4:]
        xv, gain_v = x_ref[...], gain_ref[...]
        r = lax.rsqrt(jnp.mean(xv * xv, axis=-1, keepdims=True) + EPS)
        xhat = xv * r
        h_ref[...] = (xhat * gain_v).astype(BF16)
        dh = jnp.zeros((tm, d), F32)
        for k, (dz_ref, (lo, hi)) in enumerate(zip(dz_refs, rows)):
            dz = dz_ref[...]
            if k < N_ROTATED:
                dz = _rotate(dz, cos_ref[...], -sin_ref[...]).astype(BF16)
                unrotated_refs[k][...] = dz
            dh = dh + lax.dot_general(dz.astype(BF16), wt_ref[lo:hi, :], NN, preferred_element_type=F32)
        dxhat = dh * gain_v
        dx_ref[...] = dres_ref[...] + r * (dxhat - xhat * jnp.mean(dxhat * xhat, axis=-1, keepdims=True))

        @pl.when(pl.program_id(0) == 0)
        def _():
            dgain_ref[...] = jnp.zeros_like(dgain_ref)

        dgain_ref[...] += jnp.sum(dh * xhat, axis=0, keepdims=True)

    row = pl.BlockSpec((tm, d), lambda i: (i, 0))
    tab = pl.BlockSpec((tm, LANES), lambda i: (i, 0))
    dz_specs = [pl.BlockSpec((tm, dz.shape[1]), lambda i: (i, 0)) for dz in dzs]
    outs = pl.pallas_call(
        body, grid=(n_tokens // tm,),
        in_specs=[row, _resident(gain.shape), row, tab, tab] + dz_specs + [_resident(wt.shape)],
        out_specs=(row, pl.BlockSpec(gain.shape, lambda i: (0, 0)), row) + tuple(dz_specs[:N_ROTATED]),
        out_shape=(jax.ShapeDtypeStruct(x.shape, F32), jax.ShapeDtypeStruct(gain.shape, F32),
                   jax.ShapeDtypeStruct(x.shape, BF16))
        + tuple(jax.ShapeDtypeStruct(dz.shape, BF16) for dz in dzs[:N_ROTATED]),
        name="in_proj_bwd_rows", compiler_params=_params(1),
    )(x, gain, dres, cos, sin, *dzs, wt)
    return outs[0], outs[1], outs[2], outs[3:]


def _in_proj_bwd_weight(dwt, h, dz, lo, hi, name):
    n_tokens, d = h.shape
    width = hi - lo
    tn = _tile(width, 512) if width >= LANES else width
    dz_tile = max(tn, LANES)

    def body(dwt_ref, h_ref, dz_ref, o_ref):
        o_ref[...] = lax.dot_general(dz_ref[:, :tn].astype(BF16), h_ref[...], TN, preferred_element_type=F32)

    return pl.pallas_call(
        body, grid=(width // tn,),
        in_specs=[ANY, _resident(h.shape), pl.BlockSpec((n_tokens, dz_tile), lambda j: (0, j))],
        out_specs=pl.BlockSpec((pl.Element(tn), pl.Element(d)), lambda j: (pl.multiple_of(lo + j * tn, 16), 0)),
        out_shape=jax.ShapeDtypeStruct(dwt.shape, F32), input_output_aliases={0: 0}, name=name,
        compiler_params=_params(1),
    )(dwt, h, dz)


def _split_small(z):
    return z[:, :GDN_HEADS], z[:, GDN_HEADS:2 * GDN_HEADS]


def _heads3(q, k, v):
    return _to_heads(q), _to_heads(k), _to_heads(v)


def _tokens6(o, lse):
    return tuple(_from_heads(o)) + tuple(_from_heads(lse))


def _blocks_of(vals, nblk):
    return [[v[:, b * (v.shape[1] // nblk):(b + 1) * (v.shape[1] // nblk)] for v in vals] for b in range(nblk)]


def _rowwise_matmul_fwd(fn, name, rows, params, wt, nblk, res=None):
    n_rows = rows[0].shape[0]
    tm = FFN_ROW_TILE
    k, n = wt.shape
    nr, npar = len(rows), len(params)

    def body(*refs):
        row_vals = [r[...] for r in refs[:nr]]
        par_vals = [r[...] for r in refs[nr:nr + npar]]
        wt_ref = refs[nr + npar]
        o_ref, y_ref = refs[-2:]
        y = jnp.concatenate([fn(*blk, *par_vals)[0] for blk in _blocks_of(row_vals, nblk)], axis=1).astype(BF16)
        y_ref[...] = y
        acc = lax.dot_general(y, wt_ref[...], NN, preferred_element_type=F32)
        o_ref[...] = acc if res is None else refs[nr + npar + 1][...] + acc

    row_specs = [pl.BlockSpec((tm, a.shape[1]), lambda i: (i, 0)) for a in rows]
    ins = list(rows) + list(params) + [wt] + ([] if res is None else [res])
    specs = row_specs + [_resident(p.shape) for p in params] + [_resident(wt.shape)]
    if res is not None:
        specs.append(pl.BlockSpec((tm, n), lambda i: (i, 0)))
    return pl.pallas_call(
        body, grid=(n_rows // tm,), in_specs=specs,
        out_specs=(pl.BlockSpec((tm, n), lambda i: (i, 0)), pl.BlockSpec((tm, k), lambda i: (i, 0))),
        out_shape=(jax.ShapeDtypeStruct((n_rows, n), F32), jax.ShapeDtypeStruct((n_rows, k), BF16)),
        name=name, compiler_params=_params(1),
    )(*ins)


def _rowwise_matmul_bwd(fn, name, rows, params, wt, dout, nblk, row_dtypes=None):
    n_rows = rows[0].shape[0]
    row_dtypes = row_dtypes or (F32,) * len(rows)
    tm = FFN_ROW_TILE
    nr, npar = len(rows), len(params)

    def body(*refs):
        row_vals = [r[...] for r in refs[:nr]]
        par_vals = [r[...] for r in refs[nr:nr + npar]]
        wt_ref, dout_ref = refs[nr + npar], refs[nr + npar + 1]
        outs = refs[nr + npar + 2:]
        dy = lax.dot_general(dout_ref[...].astype(BF16), wt_ref[...], NT, preferred_element_type=F32)
        grads = [jax.vjp(fn, *blk, *par_vals)[1]((dy_blk,))
                 for blk, (dy_blk,) in zip(_blocks_of(row_vals, nblk), _blocks_of([dy], nblk))]
        for j in range(nr):
            outs[j][...] = jnp.concatenate([g[j] for g in grads], axis=1).astype(row_dtypes[j])
        for j in range(npar):
            ref = outs[nr + j]

            @pl.when(pl.program_id(0) == 0)
            def _(ref=ref):
                ref[...] = jnp.zeros_like(ref)

            for g in grads:
                ref[...] += g[nr + j]

    row_specs = [pl.BlockSpec((tm, a.shape[1]), lambda i: (i, 0)) for a in rows]
    par_specs = [_resident(p.shape) for p in params]
    return pl.pallas_call(
        body, grid=(n_rows // tm,),
        in_specs=row_specs + par_specs + [_resident(wt.shape), pl.BlockSpec((tm, dout.shape[1]), lambda i: (i, 0))],
        out_specs=tuple(row_specs + [pl.BlockSpec(p.shape, lambda i: (0, 0)) for p in params]),
        out_shape=tuple([jax.ShapeDtypeStruct(a.shape, dt) for a, dt in zip(rows, row_dtypes)]
                        + [jax.ShapeDtypeStruct(p.shape, F32) for p in params]),
        name=name, compiler_params=_params(1),
    )(*rows, *params, wt, dout)


def mixer_forward(x1, w, small):
    n_tokens = x1.shape[0]
    cos, sin = _rope_tables(n_tokens)
    proj = dict(zip(IN_NAMES, _in_proj_fwd(x1, small["mix_norm"], w["w_in_t"], cos, sin)))
    (qh, kh, vh), heads_vjp = jax.vjp(_heads3, proj["wq_a"], proj["wk_a"], proj["wv_a"])
    o, lse = _attn_fwd(qh, kh, vh)
    per_group, tokens_vjp = jax.vjp(_tokens6, o, lse)
    pa, ya = _rowwise_matmul_fwd(_combine_fn, "branch_a", per_group, (), w["w_branch_a"], 1)
    qkv = _conv_fwd(proj["w_qkvb"], small["gdn_conv_w"])
    raw, small_vjp = jax.vjp(_split_small, proj["w_small"])
    gdn_params = (small["gdn_a_log"], small["gdn_dt_bias"])
    beta, gcum = _rowwise_fwd(_beta_decay_fn, "beta_decay", raw, (), gdn_params, 512, 1)
    ob, *states = _gdn_fwd(qkv, beta, gcum)
    gate_in = (ob, proj["w_ggate"])
    pb, yb = _rowwise_matmul_fwd(_outnorm_gate_fn, "branch_b", gate_in, (small["gdn_out_norm"],), w["w_branch_b"],
                                 GDN_HEADS)
    merge_in = (proj["w_gatea"], proj["w_gateb"], pa, pb)
    x2, merged = _rowwise_matmul_fwd(_merge_fn, "out", merge_in, (), w["w_out"], 1, res=x1)
    saved = dict(x1=x1, proj=proj, cos=cos, sin=sin, heads_vjp=heads_vjp, heads=(qh, kh, vh), tokens_vjp=tokens_vjp,
                 per_group=per_group, ya=ya, qkv=qkv, raw=raw, small_vjp=small_vjp, beta=beta, gcum=gcum, states=states,
                 gate_in=gate_in, yb=yb, merge_in=merge_in, merged=merged)
    return x2, saved


def mixer_backward(dx2, s, w, small):
    proj = s["proj"]
    grads = dict(w_out=_weight_grad(s["merged"], dx2, "out_dw"))
    dgate_a, dgate_b, dpa, dpb = _rowwise_matmul_bwd(_merge_fn, "out_bwd", s["merge_in"], (), w["w_out"], dx2, 1,
                                                     (BF16,) * 4)
    grads["w_branch_b"] = _weight_grad(s["yb"], dpb, "branch_b_dw")
    grads["w_branch_a"] = _weight_grad(s["ya"], dpa, "branch_a_dw")
    dob, dggate, grads["gdn_out_norm"] = _rowwise_matmul_bwd(
        _outnorm_gate_fn, "branch_b_bwd", s["gate_in"], (small["gdn_out_norm"],), w["w_branch_b"], dpb, GDN_HEADS,
        (F32, BF16))
    dqkv, dbeta, dgcum = _gdn_bwd(s["qkv"], s["beta"], s["gcum"], *s["states"], dob)
    gdn_params = (small["gdn_a_log"], small["gdn_dt_bias"])
    dbeta_raw, ddecay_raw, grads["gdn_a_log"], grads["gdn_dt_bias"] = _rowwise_bwd(
        _beta_decay_fn, "beta_decay_bwd", s["raw"], (), gdn_params, (dbeta, dgcum), 512, 1)
    dsmall = s["small_vjp"]((dbeta_raw, ddecay_raw))[0]
    dqkvb, grads["gdn_conv_w"] = _conv_bwd(proj["w_qkvb"], small["gdn_conv_w"], dqkv)
    dper_group = _rowwise_matmul_bwd(_combine_fn, "branch_a_bwd", s["per_group"], (), w["w_branch_a"], dpa, 1)
    do, dlse = s["tokens_vjp"](tuple(dper_group))
    dqh, dkh, dvh = _attn_bwd(*s["heads"], do, dlse)
    dq_rot, dk_rot, dv = s["heads_vjp"]((dqh, dkh, dvh))
    dzs = (dq_rot, dk_rot, dv, dqkvb, dsmall, dggate, dgate_a, dgate_b)
    dx1, grads["mix_norm"], h, unrotated = _in_proj_bwd_rows(
        s["x1"], small["mix_norm"], dx2, dzs, w["w_in_t"], s["cos"], s["sin"])
    dzs = tuple(unrotated) + dzs[N_ROTATED:]
    dwt = lax.empty(w["w_in_t"].shape, F32)
    for (name, lo, hi), dz in zip(IN_PIECES, dzs):
        dwt = _in_proj_bwd_weight(dwt, h, dz, lo, hi, "in_proj_dw_" + name)
    grads["w_in_t"] = dwt
    return dx1, grads


def ffn_forward(x, gain, w, tag):
    out, g, u = _ffn_fwd(x, gain, w[tag + "_w_gate"], w[tag + "_w_up"], w[tag + "_w_down"], tag + "_fwd")
    return out, (x, g, u)


def ffn_backward(dy, saved, gain, w, tag, with_payload=False):
    x, g, u = saved
    weights = (w[tag + "_w_gate"], w[tag + "_w_up"], w[tag + "_w_down"])
    dx, dgain, h, dyh, a, dg, du = _ffn_bwd_rows(x, gain, dy, g, u, *weights, tag + "_bwd_rows")
    return dx, dgain, _ffn_bwd_weights(h, dyh, a, dg, du, tag + "_bwd_weights", with_payload)


def loss_head(x3, target, gain):
    n_tokens, d = x3.shape
    tm = FFN_ROW_TILE

    def body(x_ref, t_ref, gain_ref, loss_ref, dx_ref, dgain_ref):
        target_v = t_ref[...]
        (row_loss,), vjp = jax.vjp(lambda xv, gv: _loss_fn(xv, target_v, gv), x_ref[...], gain_ref[...])
        dx, dgain = vjp((jnp.ones_like(row_loss),))
        loss_ref[...] = row_loss
        dx_ref[...] = dx

        @pl.when(pl.program_id(0) == 0)
        def _():
            dgain_ref[...] = jnp.zeros_like(dgain_ref)

        dgain_ref[...] += dgain

    row = pl.BlockSpec((tm, d), lambda i: (i, 0))
    row_loss, dx3, dgain = pl.pallas_call(
        body, grid=(n_tokens // tm,), in_specs=[row, row, _resident(gain.shape)],
        out_specs=(pl.BlockSpec((tm, 1), lambda i: (i, 0)), row, pl.BlockSpec(gain.shape, lambda i: (0, 0))),
        out_shape=(jax.ShapeDtypeStruct((n_tokens, 1), F32), jax.ShapeDtypeStruct(x3.shape, F32),
                   jax.ShapeDtypeStruct(gain.shape, F32)),
        name="loss_head", compiler_params=_params(1),
    )(x3, target, gain)
    return jnp.sum(row_loss), dx3, dgain


BIG_WEIGHTS = ("ffn1_w_gate", "ffn1_w_up", "ffn1_w_down", "w_in", "w_branch_a", "w_branch_b", "w_out",
               "ffn2_w_gate", "ffn2_w_up", "ffn2_w_down")
TRANSPOSED = ("ffn1_w_gate", "ffn1_w_up", "w_in", "ffn2_w_gate", "ffn2_w_up")
CONV_SHARD = (GDN_CONV, 3 * GDN_WIDTH // N_DEV)
SMALL_ROWS = 24
ANY = pl.BlockSpec(memory_space=pl.ANY)


TOKEN = jax.ShapeDtypeStruct((8, LANES), F32)


def _after(value, token):
    return value + token[0, 0].astype(value.dtype)


def _position():
    return lax.axis_index("x"), lax.axis_index("y"), lax.axis_index("c")


def all_gather_shards(shards, name):
    n = len(shards)
    per = 8

    def body(*refs):
        x_refs, out_refs = refs[:n], refs[n:2 * n]
        send_sems, recv_sems, local_sems = refs[2 * n + 1:]
        x, y, c = _position()
        me, sibling = (x, y, c), (x, y, 1 - c)
        x_chip, y_chip, far_chip = (1 - x, y), (x, 1 - y), (1 - x, 1 - y)

        def slab(a, px, py, pc):
            return out_refs[a].at[4 * px + 2 * py + pc]

        def copy(a, k, src, dst, to):
            return pltpu.make_async_remote_copy(
                src_ref=src, dst_ref=dst, send_sem=send_sems.at[per * a + k], recv_sem=recv_sems.at[per * a + k],
                device_id=to, device_id_type=MESH)

        def whole(a, k, block, to, src=None):
            return copy(a, k, slab(a, *block) if src is None else src, slab(a, *block), to)

        def half(a, k, block, which, to):
            rows = shards[a].shape[0] // 2
            part = slab(a, *block).at[pl.ds(which * rows, rows)]
            return copy(a, k, part, part, to)

        arrays = range(n)
        mine = [pltpu.make_async_copy(x_refs[a], slab(a, *me), local_sems.at[a]) for a in arrays]
        for cp in mine:
            cp.start()
        started = [whole(a, 1, me, (*x_chip, c), src=x_refs[a]) for a in arrays]
        started += [whole(a, 2, me, (*y_chip, c), src=x_refs[a]) for a in arrays]
        started += [whole(a, 0, me, sibling, src=x_refs[a]) for a in arrays]
        for cp in started:
            cp.start()

        def start(cp):
            cp.start()
            started.append(cp)

        for a in arrays:
            whole(a, 1, (*x_chip, c), me).wait_recv()
            start(half(a, 3, (*x_chip, c), 0, (*y_chip, c)))
            start(whole(a, 5, (*x_chip, c), sibling))
        for a in arrays:
            whole(a, 2, (*y_chip, c), me).wait_recv()
            start(half(a, 4, (*y_chip, c), 1, (*x_chip, c)))
            start(whole(a, 6, (*y_chip, c), sibling))
        for a in arrays:
            half(a, 3, (*far_chip, c), 0, me).wait_recv()
            half(a, 4, (*far_chip, c), 1, me).wait_recv()
            start(whole(a, 7, (*far_chip, c), sibling))
        for a in arrays:
            whole(a, 0, sibling, me).wait_recv()
            for k, chip in ((5, x_chip), (6, y_chip), (7, far_chip)):
                whole(a, k, (*chip, 1 - c), me).wait_recv()
        for cp in started:
            cp.wait_send()
        for cp in mine:
            cp.wait()
        refs[2 * n][...] = jnp.zeros_like(refs[2 * n])

    outs = pl.pallas_call(
        body, out_shape=tuple(jax.ShapeDtypeStruct((N_DEV,) + s.shape, s.dtype) for s in shards) + (TOKEN,),
        in_specs=[ANY] * n, out_specs=(ANY,) * n + (pl.BlockSpec(memory_space=pltpu.VMEM),),
        scratch_shapes=[pltpu.SemaphoreType.DMA((per * n,)), pltpu.SemaphoreType.DMA((per * n,)),
                        pltpu.SemaphoreType.DMA((n,))],
        name=name,
    )(*shards)
    return outs[:n], outs[n]


def exchange_with_sibling(grads):
    n = len(grads)

    def body(*refs):
        g_refs, recv_refs = refs[:n], refs[n:2 * n]
        send_sems, recv_sems = refs[2 * n:]
        x, y, c = _position()
        copies = [pltpu.make_async_remote_copy(
            src_ref=g_refs[a].at[2 * k + 1 - c], dst_ref=recv_refs[a].at[k], send_sem=send_sems.at[4 * a + k],
            recv_sem=recv_sems.at[4 * a + k], device_id=(x, y, 1 - c), device_id_type=MESH)
            for k in range(4) for a in range(n)]
        for cp in copies:
            cp.start()
        for cp in copies:
            cp.wait()

    return pl.pallas_call(
        body, out_shape=tuple(jax.ShapeDtypeStruct((4,) + g.shape[1:], g.dtype) for g in grads),
        in_specs=[ANY] * n, out_specs=(ANY,) * n,
        scratch_shapes=[pltpu.SemaphoreType.DMA((4 * n,)), pltpu.SemaphoreType.DMA((4 * n,))], name="rs_sibling",
    )(*grads)


ELEMENTWISE_TILE_BYTES = 1536 * 1024


def _tile2(rows, cols):
    if rows % 256 == 0:
        return 256, cols
    if rows * cols * 4 > ELEMENTWISE_TILE_BYTES and cols % 256 == 0:
        return rows, 256
    return rows, cols


def add_sibling(grads, received, core, name):
    _, rows, width = grads.shape
    tr, tc = _tile2(rows, width)

    def body(c_ref, g_ref, r_ref, o_ref):
        o_ref[...] = (g_ref[...] + r_ref[...]).astype(BF16)

    blk = (1, tr, tc)
    return pl.pallas_call(
        body,
        grid_spec=pltpu.PrefetchScalarGridSpec(
            num_scalar_prefetch=1, grid=(4, rows // tr, width // tc),
            in_specs=[pl.BlockSpec(blk, lambda k, i, j, c_ref: (2 * k + c_ref[0], i, j)),
                      pl.BlockSpec(blk, lambda k, i, j, c_ref: (k, i, j))],
            out_specs=pl.BlockSpec(blk, lambda k, i, j, c_ref: (k, i, j))),
        out_shape=jax.ShapeDtypeStruct((4, rows, width), BF16), name=name, compiler_params=_params(3),
    )(core, grads, received)


HBM = pl.BlockSpec(memory_space=pltpu.HBM)
SEM = pl.BlockSpec(memory_space=pltpu.SEMAPHORE)
DATAFLOW_EFFECT = pltpu.SideEffectType.DATAFLOW_SIDE_EFFECTING
N_PEERS = N_DEV - 1


def _peer(mask):
    x, y, c = _position()
    px = 1 - x if mask & 4 else x
    py = 1 - y if mask & 2 else y
    pc = 1 - c if mask & 1 else c
    return (px, py, pc), 4 * px + 2 * py + pc


ALL_PEERS = tuple(range(1, N_DEV))
OTHER_CHIPS = (4, 2, 6)


SIBLING = 1
GATHER_MODES = ("gather", "near")


def _exchange_peers(mode):
    return {"chips": OTHER_CHIPS, "near": (SIBLING,) + OTHER_CHIPS}.get(mode, ALL_PEERS)


def _direct_copies(src_refs, land_refs, send_sems, recv_sems, mode):
    x, y, c = _position()
    me = 4 * x + 2 * y + c
    masks = _exchange_peers(mode)
    copies = []
    for a, (src, land) in enumerate(zip(src_refs, land_refs)):
        for slot, mask in enumerate(masks):
            peer, peer_index = _peer(mask)
            k = len(masks) * a + slot
            if mode in GATHER_MODES:
                source, dest = src, land.at[me]
            elif mode == "scatter":
                source, dest = src.at[peer_index], land.at[slot]
            else:
                source, dest = src.at[2 * peer[0] + peer[1]], land.at[slot]
            copies.append(pltpu.make_async_remote_copy(
                src_ref=source, dst_ref=dest, send_sem=send_sems.at[k], recv_sem=recv_sems.at[k], device_id=peer,
                device_id_type=MESH))
    return copies


def forward_to_sibling(slabs, name):
    n = len(slabs)

    def body(*refs):
        out_refs = refs[n:2 * n]
        send_sems, recv_sems = refs[2 * n + 1:]
        x, y, c = _position()
        copies = []
        for a in range(n):
            for slot, mask in enumerate(OTHER_CHIPS):
                _, held = _peer(mask)
                copies.append(pltpu.make_async_remote_copy(
                    src_ref=out_refs[a].at[held], dst_ref=out_refs[a].at[held], send_sem=send_sems.at[3 * a + slot],
                    recv_sem=recv_sems.at[3 * a + slot], device_id=(x, y, 1 - c), device_id_type=MESH))
        for cp in copies:
            cp.start()
        for cp in copies:
            cp.wait()
        refs[2 * n][...] = jnp.zeros_like(refs[2 * n])

    outs = pl.pallas_call(
        body, out_shape=tuple(jax.ShapeDtypeStruct(s.shape, s.dtype) for s in slabs) + (TOKEN,),
        in_specs=[ANY] * n, out_specs=(ANY,) * n + (pl.BlockSpec(memory_space=pltpu.VMEM),),
        input_output_aliases={i: i for i in range(n)},
        scratch_shapes=[pltpu.SemaphoreType.DMA((3 * n,)), pltpu.SemaphoreType.DMA((3 * n,))], name=name,
    )(*slabs)
    return outs[:n], outs[n]


def direct_exchange_start(arrays, mode, name):
    n = len(arrays)
    n_peers = len(_exchange_peers(mode))
    lands = [lax.empty((N_DEV,) + a.shape if mode in GATHER_MODES else (n_peers,) + a.shape[1:], a.dtype)
             for a in arrays]

    def body(*refs):
        src_refs, land_refs = refs[:n], refs[n:2 * n]
        send_sems, recv_sems = refs[2 * n], refs[2 * n + 1]
        token = refs[-1]
        for cp in _direct_copies(src_refs, land_refs, send_sems, recv_sems, mode):
            cp.start()
        token[...] = jnp.zeros_like(token)

    sems = pltpu.SemaphoreType.DMA((n_peers * n,))
    outs = pl.pallas_call(
        body, name=name,
        out_shape=(sems, sems) + tuple(pltpu.HBM(a.shape, a.dtype) for a in arrays)
        + tuple(pltpu.HBM(l.shape, l.dtype) for l in lands) + (TOKEN,),
        in_specs=[HBM] * (2 * n), out_specs=(SEM, SEM) + (HBM,) * (2 * n) + (pl.BlockSpec(memory_space=pltpu.VMEM),),
        input_output_aliases={i: 2 + i for i in range(2 * n)},
        compiler_params=pltpu.CompilerParams(has_side_effects=DATAFLOW_EFFECT),
    )(*[pltpu.with_memory_space_constraint(a, pltpu.HBM) for a in list(arrays) + lands])
    return outs[0], outs[1], outs[2:2 + n], outs[2 + n:2 + 2 * n], outs[-1]


def direct_exchange_wait(send_sems, recv_sems, arrays, lands, after, mode, name):
    n = len(arrays)

    def body(*refs):
        src_refs, land_refs = refs[:n], refs[n:2 * n]
        send_sems, recv_sems = refs[2 * n], refs[2 * n + 1]
        for cp in _direct_copies(src_refs, land_refs, send_sems, recv_sems, mode):
            cp.wait_send()
            cp.wait_recv()
        refs[-1][...] = jnp.zeros_like(refs[-1])

    outs = pl.pallas_call(
        body, name=name,
        out_shape=tuple(pltpu.HBM(a.shape, a.dtype) for a in arrays) + tuple(pltpu.HBM(l.shape, l.dtype) for l in lands)
        + (TOKEN,),
        in_specs=[HBM] * (2 * n) + [SEM, SEM, pl.BlockSpec(memory_space=pl.ANY)],
        out_specs=(HBM,) * (2 * n) + (pl.BlockSpec(memory_space=pltpu.VMEM),),
        input_output_aliases={i: i for i in range(2 * n)},
        compiler_params=pltpu.CompilerParams(has_side_effects=DATAFLOW_EFFECT),
    )(*arrays, *lands, send_sems, recv_sems, after)
    return outs[n:]


def adamw_direct(w, m, v, own, received, name):
    row_per_tile = w.shape[0] != 1
    rows, cols = (w.shape[0], w.shape[2]) if row_per_tile else w.shape[-2:]
    tr, tc = _tile2(rows, cols)

    def body(w_ref, m_ref, v_ref, own_ref, r_ref, g_ref, d_ref, nm_ref, nv_ref):
        gv = own_ref[0]
        for j in range(N_PEERS):
            gv = gv + r_ref[j].astype(F32)
        nm = ADAM_B1 * m_ref[...] + (1.0 - ADAM_B1) * gv
        nv = ADAM_B2 * v_ref[...] + (1.0 - ADAM_B2) * (gv * gv)
        m_hat = nm / (1.0 - ADAM_B1 ** ADAM_STEP)
        v_hat = nv / (1.0 - ADAM_B2 ** ADAM_STEP)
        g_ref[...] = gv
        d_ref[...] = -ADAM_LR * (m_hat / (jnp.sqrt(v_hat) + ADAM_EPS) + ADAM_WD * w_ref[...])
        nm_ref[...] = nm
        nv_ref[...] = nv

    if row_per_tile:
        one = pl.BlockSpec((tr, None, tc), lambda i, j: (i, 0, j))
    else:
        one = pl.BlockSpec((None, tr, tc), lambda i, j: (0, i, j))
    out = jax.ShapeDtypeStruct(w.shape, F32)
    return pl.pallas_call(
        body, grid=(rows // tr, cols // tc),
        in_specs=[one, one, one, pl.BlockSpec((1, tr, tc), lambda i, j: (0, i, j)),
                  pl.BlockSpec((N_PEERS, tr, tc), lambda i, j: (0, i, j))],
        out_specs=(one,) * 4, out_shape=(out,) * 4, name=name, compiler_params=_params(2),
    )(w, m, v, own, received)


def all_reduce_small(vals):
    rows, width = vals.shape

    def body(x_ref, out_ref, all_ref, send_sems, recv_sems):
        x, y, c = _position()
        me, sibling = (x, y, c), (x, y, 1 - c)
        chips = [(1 - x, y), (x, 1 - y), (1 - x, 1 - y)]

        def slab(px, py, pc):
            return all_ref.at[4 * px + 2 * py + pc]

        def copy(k, block, to, src=None):
            return pltpu.make_async_remote_copy(
                src_ref=slab(*block) if src is None else src, dst_ref=slab(*block),
                send_sem=send_sems.at[k], recv_sem=recv_sems.at[k], device_id=to, device_id_type=MESH)

        first = [copy(0, me, sibling, src=x_ref)]
        first += [copy(1 + j, me, (*chip, c), src=x_ref) for j, chip in enumerate(chips)]
        for cp in first:
            cp.start()
        all_ref[4 * x + 2 * y + c] = x_ref[...]
        passed = [copy(4 + j, (*chip, c), sibling) for j, chip in enumerate(chips)]
        for j, chip in enumerate(chips):
            copy(1 + j, (*chip, c), me).wait_recv()
            passed[j].start()
        copy(0, sibling, me).wait_recv()
        for j, chip in enumerate(chips):
            copy(4 + j, (*chip, 1 - c), me).wait_recv()
        for cp in first + passed:
            cp.wait_send()
        total = all_ref[0]
        for d in range(1, N_DEV):
            total = total + all_ref[d]
        out_ref[...] = total

    vmem = pl.BlockSpec(memory_space=pltpu.VMEM)
    return pl.pallas_call(
        body, out_shape=(jax.ShapeDtypeStruct(vals.shape, F32), jax.ShapeDtypeStruct((N_DEV, rows, width), F32)),
        in_specs=[vmem], out_specs=(vmem, vmem),
        scratch_shapes=[pltpu.SemaphoreType.DMA((7,)), pltpu.SemaphoreType.DMA((7,))], name="small_allreduce",
    )(vals)[0]


def adamw(w, g, m, v, name):
    shape = w.shape
    w2, g2, m2, v2 = [a.reshape((-1, shape[-1])) for a in (w, g, m, v)]
    rows, cols = w2.shape
    tr = 256 if rows % 256 == 0 else rows

    def body(w_ref, g_ref, m_ref, v_ref, d_ref, nm_ref, nv_ref):
        gv = g_ref[...]
        nm = ADAM_B1 * m_ref[...] + (1.0 - ADAM_B1) * gv
        nv = ADAM_B2 * v_ref[...] + (1.0 - ADAM_B2) * (gv * gv)
        m_hat = nm / (1.0 - ADAM_B1 ** ADAM_STEP)
        v_hat = nv / (1.0 - ADAM_B2 ** ADAM_STEP)
        d_ref[...] = -ADAM_LR * (m_hat / (jnp.sqrt(v_hat) + ADAM_EPS) + ADAM_WD * w_ref[...])
        nm_ref[...] = nm
        nv_ref[...] = nv

    blk = pl.BlockSpec((tr, cols), lambda i: (i, 0))
    out = jax.ShapeDtypeStruct((rows, cols), F32)
    outs = pl.pallas_call(
        body, grid=(rows // tr,), in_specs=[blk] * 4, out_specs=(blk,) * 3, out_shape=(out,) * 3,
        name=name, compiler_params=_params(1),
    )(w2, g2, m2, v2)
    return tuple(o.reshape(shape) for o in outs)


def adamw_summed(w, m, v, grads, from_sibling, received, me, name):
    rows, cols = w.shape[-2:]
    tr, tc = _tile2(rows, cols)

    def body(me_ref, w_ref, m_ref, v_ref, own_ref, sib_ref, r_ref, g_ref, d_ref, nm_ref, nv_ref):
        gv = own_ref[0] + sib_ref[0]
        for j in range(3):
            gv = gv + r_ref[j].astype(F32)
        nm = ADAM_B1 * m_ref[0] + (1.0 - ADAM_B1) * gv
        nv = ADAM_B2 * v_ref[0] + (1.0 - ADAM_B2) * (gv * gv)
        m_hat = nm / (1.0 - ADAM_B1 ** ADAM_STEP)
        v_hat = nv / (1.0 - ADAM_B2 ** ADAM_STEP)
        g_ref[0] = gv
        d_ref[0] = -ADAM_LR * (m_hat / (jnp.sqrt(v_hat) + ADAM_EPS) + ADAM_WD * w_ref[0])
        nm_ref[0] = nm
        nv_ref[0] = nv

    one = pl.BlockSpec((1, tr, tc), lambda i, j, me_ref: (0, i, j))
    out = jax.ShapeDtypeStruct((1, rows, cols), F32)
    return pl.pallas_call(
        body,
        grid_spec=pltpu.PrefetchScalarGridSpec(
            num_scalar_prefetch=1, grid=(rows // tr, cols // tc),
            in_specs=[one, one, one, pl.BlockSpec((1, tr, tc), lambda i, j, me_ref: (me_ref[0], i, j)),
                      pl.BlockSpec((1, tr, tc), lambda i, j, me_ref: (me_ref[1], i, j)),
                      pl.BlockSpec((3, tr, tc), lambda i, j, me_ref: (0, i, j))],
            out_specs=(one,) * 4),
        out_shape=(out,) * 4, name=name, compiler_params=_params(2),
    )(me, w, m, v, grads, from_sibling, received)


SMALL_VECTORS = ("ffn1_norm", "mix_norm", "ffn2_norm", "final_norm")


def _pack_small(gs):
    row = jnp.concatenate([gs["gdn_a_log"].reshape(-1), gs["gdn_dt_bias"].reshape(-1), gs["gdn_out_norm"].reshape(-1)])
    rows = [gs[n].reshape(1, D_MODEL) for n in SMALL_VECTORS]
    rows.append(jnp.pad(row, (0, D_MODEL - row.shape[0])).reshape(1, D_MODEL))
    rows.append(gs["gdn_conv_w"].reshape(-1, D_MODEL))
    packed = jnp.concatenate(rows, axis=0)
    return jnp.pad(packed, ((0, SMALL_ROWS - packed.shape[0]), (0, 0)))


def _unpack_small(packed):
    out = {n: packed[i].reshape(1, D_MODEL) for i, n in enumerate(SMALL_VECTORS)}
    row = packed[len(SMALL_VECTORS)]
    out["gdn_a_log"] = row[:GDN_HEADS].reshape(1, GDN_HEADS)
    out["gdn_dt_bias"] = row[GDN_HEADS:2 * GDN_HEADS].reshape(1, GDN_HEADS)
    out["gdn_out_norm"] = row[2 * GDN_HEADS:2 * GDN_HEADS + GDN_HEAD_DIM].reshape(1, GDN_HEAD_DIM)
    first = len(SMALL_VECTORS) + 1
    out["gdn_conv_w"] = packed[first:first + GDN_CONV * 3].reshape(GDN_CONV, 3 * GDN_WIDTH)
    return out


WEIGHTS = ("ffn1_norm", "ffn1_w_gate", "ffn1_w_up", "ffn1_w_down", "mix_norm", "w_in", "gdn_conv_w", "gdn_a_log",
           "gdn_dt_bias", "gdn_out_norm", "w_branch_a", "w_branch_b", "w_out", "ffn2_norm", "ffn2_w_gate",
           "ffn2_w_up", "ffn2_w_down", "final_norm")


def kernel(x, ffn1_norm, ffn1_w_gate, ffn1_w_up, ffn1_w_down, mix_norm, w_in, gdn_conv_w, gdn_a_log, gdn_dt_bias, gdn_out_norm, w_branch_a, w_branch_b, w_out, ffn2_norm, ffn2_w_gate, ffn2_w_up, ffn2_w_down, final_norm, loss_target, m_ffn1_norm, m_ffn1_w_gate, m_ffn1_w_up, m_ffn1_w_down, m_mix_norm, m_w_in, m_gdn_conv_w, m_gdn_a_log, m_gdn_dt_bias, m_gdn_out_norm, m_w_branch_a, m_w_branch_b, m_w_out, m_ffn2_norm, m_ffn2_w_gate, m_ffn2_w_up, m_ffn2_w_down, m_final_norm, v_ffn1_norm, v_ffn1_w_gate, v_ffn1_w_up, v_ffn1_w_down, v_mix_norm, v_w_in, v_gdn_conv_w, v_gdn_a_log, v_gdn_dt_bias, v_gdn_out_norm, v_w_branch_a, v_w_branch_b, v_w_out, v_ffn2_norm, v_ffn2_w_gate, v_ffn2_w_up, v_ffn2_w_down, v_final_norm):
    given = dict(locals())
    px, py, pc = _position()
    big_names = list(BIG_WEIGHTS)

    def shard_view(a, n):
        if n == "w_in":
            return a.transpose(2, 0, 1)
        return a.transpose(0, 2, 1) if n in TRANSPOSED else a

    def shard_unview(a, n):
        if n == "w_in":
            return a.transpose(1, 2, 0)
        return a.transpose(0, 2, 1) if n in TRANSPOSED else a

    me = 4 * px + 2 * py + pc
    me_index = me.astype(jnp.int32).reshape(1)
    late = [n for n in big_names if n.startswith("ffn2")]
    early = [n for n in big_names if n not in late]
    shards = {n: shard_view(given[n], n).reshape(given[n].shape[-1 if n in TRANSPOSED else -2], -1).astype(BF16)
              for n in big_names}
    first = [n for n in early if n.startswith("ffn1")]
    middle = [n for n in early if n not in first]
    first_slabs, first_done = all_gather_shards([shards[n] for n in first], "gather_ffn1")
    shards["gdn_conv_w"] = gdn_conv_w[0]
    middle_all = middle + ["gdn_conv_w"]
    middle_gather = direct_exchange_start([_after(shards[n], first_done) for n in middle_all], "near",
                                          "gather_mixer_start")
    ffn1_norm = _after(ffn1_norm, middle_gather[4])
    def in_chunks(slabs):
        return slabs.reshape(-1, FFN_CHUNK, D_MODEL)

    def in_slabs(chunks):
        return chunks.reshape(N_DEV, -1, D_MODEL)

    w = {n: in_chunks(slab) for n, slab in zip(first, first_slabs)}
    x1, ffn1_saved = ffn_forward(x[0], ffn1_norm, w, "ffn1")
    near_lands = direct_exchange_wait(*middle_gather[:4], x1, "near", "gather_mixer_wait")[:-1]
    near_lands = [lax.dynamic_update_slice(land, shards[n][None], (me, 0, 0)) for n, land in zip(middle_all, near_lands)]
    middle_slabs, middle_done = forward_to_sibling(near_lands, "gather_mixer_forward")
    gathered = dict(zip(middle_all, middle_slabs))
    late_gather = direct_exchange_start([_after(shards[n], middle_done) for n in late], "gather", "gather_ffn2_start")
    w["w_in_t"] = gathered["w_in"].reshape(-1, D_MODEL)
    w["w_branch_a"] = gathered["w_branch_a"].transpose(1, 0, 2).reshape(-1, D_MODEL)
    w["w_branch_b"] = gathered["w_branch_b"].reshape(D_MODEL, D_MODEL)
    w["w_out"] = gathered["w_out"].reshape(D_MODEL, D_MODEL)
    conv_full = gathered["gdn_conv_w"].transpose(1, 0, 2).reshape(GDN_CONV, 3 * GDN_WIDTH)
    small = dict(mix_norm=_after(mix_norm, late_gather[4]), gdn_a_log=gdn_a_log, gdn_dt_bias=gdn_dt_bias,
                 gdn_out_norm=gdn_out_norm, gdn_conv_w=conv_full)

    x2, mixer_saved = mixer_forward(x1, w, small)
    late_lands = direct_exchange_wait(*late_gather[:4], x2, "gather", "gather_ffn2_wait")
    for n, land in zip(late, late_lands):
        w[n] = in_chunks(lax.dynamic_update_slice(land, shards[n][None], (me, 0, 0)))
    x3, ffn2_saved = ffn_forward(x2, ffn2_norm, w, "ffn2")
    loss_local, dx3, g_final = loss_head(x3, loss_target[0], final_norm.reshape(1, D_MODEL))
    loss = lax.psum(loss_local, ("x", "y", "c"))
    dx2, g_ffn2_norm, (dw2, dw2_f32) = ffn_backward(dx3, ffn2_saved, ffn2_norm, w, "ffn2", with_payload=True)
    late_scatter = direct_exchange_start([in_slabs(g) for g in dw2], "scatter", "rs_ffn2_start")
    w_after = dict(w, w_out=_after(w["w_out"], late_scatter[4]))
    dx1, g_w = mixer_backward(dx2, mixer_saved, w_after, small)
    middle = ["w_in", "w_branch_a", "w_branch_b", "w_out"]
    g_big = dict(w_in=g_w["w_in_t"].reshape(N_DEV, -1, D_MODEL),
                 w_branch_a=g_w["w_branch_a"].reshape(-1, N_DEV, D_MODEL // N_DEV).transpose(1, 0, 2),
                 w_branch_b=g_w["w_branch_b"].reshape(N_DEV, -1, D_MODEL),
                 w_out=g_w["w_out"].reshape(N_DEV, -1, D_MODEL))
    own = {n: lax.dynamic_index_in_dim(in_slabs(g), me, 0, keepdims=True) for n, g in zip(late, dw2_f32)}
    own.update({n: lax.dynamic_index_in_dim(g_big[n], me, 0, keepdims=True) for n in middle[1:]})
    in_rows = g_w["w_in_t"].shape[0] // N_DEV
    own["w_in"] = lax.dynamic_slice(g_w["w_in_t"], (me * in_rows, 0), (in_rows, D_MODEL))[None]
    middle_scatter = direct_exchange_start([g_big[n].astype(BF16) for n in middle], "scatter", "rs_mixer_start")
    grad_x, g_ffn1_norm, dw1 = ffn_backward(dx1, ffn1_saved, _after(ffn1_norm, middle_scatter[4]), w, "ffn1")
    g_small = dict(ffn1_norm=g_ffn1_norm, ffn2_norm=g_ffn2_norm, final_norm=g_final,
                   **{n: g_w[n] for n in ("mix_norm", "gdn_a_log", "gdn_dt_bias", "gdn_out_norm", "gdn_conv_w")})

    first = [n for n in early if n.startswith("ffn1")]
    g_list = [in_slabs(g) for g in dw1]
    core = pc.astype(jnp.int32).reshape(1)
    me_and_chip = jnp.stack([me, 2 * px + py]).astype(jnp.int32)
    from_sibling = exchange_with_sibling(g_list)
    partials = [add_sibling(g, r, core, "rs_add_" + n) for n, g, r in zip(first, g_list, from_sibling)]
    first_chips = direct_exchange_start(partials, "chips", "rs_ffn1_start")

    def state_of(n):
        return [shard_view(given[p + n], n) for p in ("", "m_", "v_")]

    results = {}
    late_received = direct_exchange_wait(*late_scatter[:4], first_chips[4], "scatter", "rs_ffn2_wait")
    middle_received = direct_exchange_wait(*middle_scatter[:4], first_chips[4], "scatter", "rs_mixer_wait")
    for n, recv in zip(late + middle, list(late_received[:-1]) + list(middle_received[:-1])):
        outs = adamw_direct(*state_of(n), own[n], recv, "adamw_" + n)
        results[n] = tuple(shard_unview(o, n) for o in outs)

    done = results["w_out"][1]
    from_chips = direct_exchange_wait(*first_chips[:4], done, "chips", "rs_ffn1_wait")
    for n, g, sib, recv in zip(first, g_list, from_sibling, from_chips):
        outs = adamw_summed(*state_of(n), g, sib, recv, me_and_chip, "adamw_" + n)
        results[n] = tuple(shard_unview(o, n) for o in outs)

    small_sum = _unpack_small(all_reduce_small(_after(_pack_small(g_small), from_chips[-1])))
    conv_cols = CONV_SHARD[1]
    small_sum["gdn_conv_w"] = lax.dynamic_slice(small_sum["gdn_conv_w"], (0, me * conv_cols), (GDN_CONV, conv_cols))
    for n in WEIGHTS:
        if n not in results:
            g = small_sum[n].reshape(given[n].shape)
            results[n] = (g,) + adamw(given[n], g, given["m_" + n], given["v_" + n], "adamw_" + n)

    outs = [[results[n][i] for n in WEIGHTS] for i in range(4)]
    return (loss, grad_x[None], *outs[0], *outs[1], *outs[2], *outs[3])
```
